```python
import jax, jax.numpy as jnp
from jax import lax
import numpy as np

D_MODEL = 1024
BATCH = 8
SEQ = 4096
DEPTH = 2

GRID_W = 64
Q_BLOCK = 128
HEAD_DIM = 64
EPS = 1e-6
ROPE_THETA = 10000.0

A_HEADS = 8
A_KV_HEADS = 2
A_GROUPS = A_HEADS // A_KV_HEADS
A_Q = A_HEADS * HEAD_DIM
A_KV = A_KV_HEADS * HEAD_DIM
A_OUT = A_Q

B_HEADS = 8
B_NOPE = 64
B_ROPE = 32
B_V = 64
B_Q_LORA = 256
B_KV_LORA = 128
B_OUT = B_HEADS * B_V

C_HEADS = 16
C_KV_HEADS = 4
C_GROUPS = C_HEADS // C_KV_HEADS
C_Q = C_HEADS * HEAD_DIM
C_KV = C_KV_HEADS * HEAD_DIM
WINDOW = 128

EVEN_IN = A_Q + 2 * A_KV + A_OUT + B_Q_LORA + B_KV_LORA + B_ROPE + B_OUT
EVEN_MIX = A_OUT + B_OUT
ODD_IN = C_Q + 2 * C_KV + C_Q
ODD_MIX = C_Q
N_EVEN = (DEPTH + 1) // 2
N_ODD = DEPTH // 2

kernel_name = "hybrid_gqa_mla_swa_adaln_encoder"


def rms_norm(x, g):
    xf = x.astype(jnp.float32)
    y = xf * lax.rsqrt(jnp.mean(xf * xf, axis=-1, keepdims=True) + EPS)
    return (y * g.astype(jnp.float32)).astype(x.dtype)


def rope_cos_sin(pos, dim):
    inv = ROPE_THETA ** (-jnp.arange(0, dim, 2, dtype=jnp.float32) / dim)
    ang = pos.astype(jnp.float32)[:, None] * inv[None, :]
    return jnp.cos(ang), jnp.sin(ang)


def apply_rope(x, cos, sin):
    half = x.shape[-1] // 2
    x1, x2 = x[..., :half], x[..., half:]
    cos = cos.astype(x.dtype)
    sin = sin.astype(x.dtype)
    return jnp.concatenate([x1 * cos - x2 * sin, x1 * sin + x2 * cos], axis=-1)


def axial_rope(x, cos_r, sin_r, cos_c, sin_c):
    half = HEAD_DIM // 2
    xr = apply_rope(x[..., :half], cos_r[:, None, :], sin_r[:, None, :])
    xc = apply_rope(x[..., half:], cos_c[:, None, :], sin_c[:, None, :])
    return jnp.concatenate([xr, xc], axis=-1)


def to_blocks(t):
    b, s = t.shape[:2]
    return jnp.moveaxis(t.reshape((b, s // Q_BLOCK, Q_BLOCK) + t.shape[2:]), 1, 0)


def from_blocks(t):
    nb, b = t.shape[:2]
    return jnp.moveaxis(t, 0, 1).reshape((b, nb * Q_BLOCK) + t.shape[3:])


def dense_gqa_attention(q, k, v):
    b, s = q.shape[:2]
    scale = HEAD_DIM ** -0.5

    def block(qi):
        sc = jnp.einsum('bqkgd,bskd->bkgqs', qi, k).astype(jnp.float32) * scale
        p = jax.nn.softmax(sc, axis=-1).astype(v.dtype)
        return jnp.einsum('bkgqs,bskd->bqkgd', p, v)

    o = from_blocks(lax.map(block, to_blocks(q)))
    return o.reshape(b, s, A_HEADS * HEAD_DIM)


def mla_attention(q_lat, q_rope, c_kv, k_rope, w_uv):
    b, s = q_lat.shape[:2]
    scale = (B_NOPE + B_ROPE) ** -0.5

    def block(args):
        ql, qr = args
        sc = (jnp.einsum('bqhc,bsc->bhqs', ql, c_kv)
              + jnp.einsum('bqhr,bsr->bhqs', qr, k_rope)).astype(jnp.float32) * scale
        p = jax.nn.softmax(sc, axis=-1).astype(c_kv.dtype)
        return jnp.einsum('bhqs,bsc->bqhc', p, c_kv)

    o_lat = from_blocks(lax.map(block, (to_blocks(q_lat), to_blocks(q_rope))))
    o = jnp.einsum('bshc,chd->bshd', o_lat, w_uv)
    return o.reshape(b, s, B_HEADS * B_V)


def windowed_sink_attention(q, k, v, sink, slopes):
    b, s = q.shape[:2]
    nb = s // Q_BLOCK
    scale = HEAD_DIM ** -0.5
    qb = q.reshape(b, nb, Q_BLOCK, C_KV_HEADS, C_GROUPS, HEAD_DIM)

    def neighbours(t):
        tb = t.reshape(b, nb, Q_BLOCK, C_KV_HEADS, HEAD_DIM)
        tp = jnp.pad(tb, ((0, 0), (1, 1), (0, 0), (0, 0), (0, 0)))
        return jnp.concatenate([tp[:, :-2], tp[:, 1:-1], tp[:, 2:]], axis=2)

    kb, vb = neighbours(k), neighbours(v)
    rel = jnp.arange(3 * Q_BLOCK)[None, :] - Q_BLOCK - jnp.arange(Q_BLOCK)[:, None]
    key_pos = (jnp.arange(nb)[:, None] - 1) * Q_BLOCK + jnp.arange(3 * Q_BLOCK)[None, :]
    valid = (jnp.abs(rel) <= WINDOW)[None] & ((key_pos >= 0) & (key_pos < s))[:, None, :]
    bias = -slopes.reshape(C_KV_HEADS, C_GROUPS)[:, :, None, None] * jnp.abs(rel).astype(jnp.float32)
    sc = jnp.einsum('bnqkgd,bnskd->bnkgqs', qb, kb).astype(jnp.float32) * scale + bias
    sc = jnp.where(valid[None, :, None, None], sc, -jnp.inf)
    sink_l = sink.astype(jnp.float32).reshape(C_KV_HEADS, C_GROUPS)[None, None, :, :, None, None]
    m = jnp.maximum(jnp.max(sc, axis=-1, keepdims=True), sink_l)
    p = jnp.exp(sc - m)
    denom = jnp.sum(p, axis=-1, keepdims=True) + jnp.exp(sink_l - m)
    o = jnp.einsum('bnkgqs,bnskd->bnqkgd', (p / denom).astype(v.dtype), vb)
    return o.reshape(b, s, C_HEADS * HEAD_DIM)


def even_mixer(h, w_in, q_norm_a, k_norm_a, q_lora_norm, kv_lora_norm, w_uq, w_uk, w_uv, w_out,
               cos_r, sin_r, cos_c, sin_c, cos_t, sin_t):
    b, s, _ = h.shape
    proj = h @ w_in
    splits = list(np.cumsum([A_Q, A_KV, A_KV, A_OUT, B_Q_LORA, B_KV_LORA, B_ROPE]))
    qa, ka, va, ga, cq, ckv, kr, gb = jnp.split(proj, splits, axis=-1)
    qa = axial_rope(rms_norm(qa.reshape(b, s, A_HEADS, HEAD_DIM), q_norm_a), cos_r, sin_r, cos_c, sin_c)
    ka = axial_rope(rms_norm(ka.reshape(b, s, A_KV_HEADS, HEAD_DIM), k_norm_a), cos_r, sin_r, cos_c, sin_c)
    oa = dense_gqa_attention(qa.reshape(b, s, A_KV_HEADS, A_GROUPS, HEAD_DIM), ka,
                             va.reshape(b, s, A_KV_HEADS, HEAD_DIM))
    oa = oa * jax.nn.silu(ga)
    cq = rms_norm(cq, q_lora_norm)
    ckv = rms_norm(ckv, kv_lora_norm)
    qb = (cq @ w_uq).reshape(b, s, B_HEADS, B_NOPE + B_ROPE)
    q_nope, q_rope = qb[..., :B_NOPE], qb[..., B_NOPE:]
    q_rope = apply_rope(q_rope, cos_t[:, None, :], sin_t[:, None, :])
    k_rope = apply_rope(kr, cos_t, sin_t)
    q_lat = jnp.einsum('bshd,chd->bshc', q_nope, w_uk)
    ob = mla_attention(q_lat, q_rope, ckv, k_rope, w_uv) * jax.nn.silu(gb)
    return jnp.concatenate([oa, ob], axis=-1) @ w_out


def odd_mixer(h, w_in, sink, w_out, slopes):
    b, s, _ = h.shape
    proj = h @ w_in
    qc, kc, vc, gc = jnp.split(proj, [C_Q, C_Q + C_KV, C_Q + 2 * C_KV], axis=-1)
    oc = windowed_sink_attention(qc.reshape(b, s, C_KV_HEADS, C_GROUPS, HEAD_DIM),
                                 kc.reshape(b, s, C_KV_HEADS, HEAD_DIM),
                                 vc.reshape(b, s, C_KV_HEADS, HEAD_DIM), sink, slopes)
    return (oc * jax.nn.silu(gc)) @ w_out


def _fwd_setup_inputs(seed: int = 0) -> dict:
    key = jax.random.key(seed)
    ks = jax.random.split(key, 20)
    f32 = jnp.float32
    nrm = lambda k, shape, s: jax.random.normal(k, shape, f32) * s
    gain = lambda k, shape: 1.0 + 0.02 * jax.random.normal(k, shape, f32)
    return {
        "x": nrm(ks[0], (BATCH, SEQ, D_MODEL), 1.0),
        "c": nrm(ks[1], (BATCH, D_MODEL), 1.0),
        "norm_w": gain(ks[2], (DEPTH, D_MODEL)),
        "ada_w": nrm(ks[3], (DEPTH, D_MODEL, 3 * D_MODEL), 0.02),
        "ada_b": nrm(ks[4], (DEPTH, 3 * D_MODEL), 0.02),
        "even_w_in": nrm(ks[5], (N_EVEN, D_MODEL, EVEN_IN), D_MODEL ** -0.5),
        "a_q_norm": gain(ks[6], (N_EVEN, HEAD_DIM)),
        "a_k_norm": gain(ks[7], (N_EVEN, HEAD_DIM)),
        "b_q_lora_norm": gain(ks[8], (N_EVEN, B_Q_LORA)),
        "b_kv_lora_norm": gain(ks[9], (N_EVEN, B_KV_LORA)),
        "b_w_uq": nrm(ks[10], (N_EVEN, B_Q_LORA, B_HEADS * (B_NOPE + B_ROPE)), B_Q_LORA ** -0.5),
        "b_w_uk": nrm(ks[11], (N_EVEN, B_KV_LORA, B_HEADS, B_NOPE), B_KV_LORA ** -0.5),
        "b_w_uv": nrm(ks[12], (N_EVEN, B_KV_LORA, B_HEADS, B_V), B_KV_LORA ** -0.5),
        "even_w_out": nrm(ks[13], (N_EVEN, EVEN_MIX, D_MODEL), EVEN_MIX ** -0.5),
        "odd_w_in": nrm(ks[14], (N_ODD, D_MODEL, ODD_IN), D_MODEL ** -0.5),
        "c_sink": nrm(ks[15], (N_ODD, C_HEADS), 0.5),
        "odd_w_out": nrm(ks[16], (N_ODD, ODD_MIX, D_MODEL), ODD_MIX ** -0.5),
        "final_norm": gain(ks[17], (D_MODEL,)),
    }


def _fwd_reference(x, c, norm_w, ada_w, ada_b, even_w_in, a_q_norm, a_k_norm, b_q_lora_norm,
              b_kv_lora_norm, b_w_uq, b_w_uk, b_w_uv, even_w_out, odd_w_in, c_sink, odd_w_out,
              final_norm):
    s = x.shape[1]
    rows = s // GRID_W
    row = jnp.repeat(jnp.arange(rows), GRID_W)
    col = jnp.tile(jnp.arange(GRID_W), rows)
    tok = jnp.arange(s)
    cos_r, sin_r = rope_cos_sin(row, HEAD_DIM // 2)
    cos_c, sin_c = rope_cos_sin(col, HEAD_DIM // 2)
    cos_t, sin_t = rope_cos_sin(tok, B_ROPE)
    slopes = 2.0 ** (-8.0 * jnp.arange(1, C_HEADS + 1, dtype=jnp.float32) / C_HEADS)
    c_act = jax.nn.silu(c)
    for layer in range(DEPTH):
        mod = c_act @ ada_w[layer] + ada_b[layer]
        shift, scale, gate = jnp.split(mod, 3, axis=-1)
        h = rms_norm(x, norm_w[layer]) * (1.0 + scale[:, None, :]) + shift[:, None, :]
        if layer % 2 == 0:
            i = layer // 2
            y = even_mixer(h, even_w_in[i], a_q_norm[i], a_k_norm[i], b_q_lora_norm[i],
                           b_kv_lora_norm[i], b_w_uq[i], b_w_uk[i], b_w_uv[i], even_w_out[i],
                           cos_r, sin_r, cos_c, sin_c, cos_t, sin_t)
        else:
            i = layer // 2
            y = odd_mixer(h, odd_w_in[i], c_sink[i], odd_w_out[i], slopes)
        x = x + gate[:, None, :] * y
    return rms_norm(x, final_norm)


import jax as _jax
import jax.numpy as _jnp

TWIN_FORMAT = 'train_step'
FWD_PARAMS = ['x', 'c', 'norm_w', 'ada_w', 'ada_b', 'even_w_in', 'a_q_norm', 'a_k_norm', 'b_q_lora_norm', 'b_kv_lora_norm', 'b_w_uq', 'b_w_uk', 'b_w_uv', 'even_w_out', 'odd_w_in', 'c_sink', 'odd_w_out', 'final_norm']
TWIN_WEIGHTS = ['norm_w', 'ada_w', 'ada_b', 'even_w_in', 'a_q_norm', 'a_k_norm', 'b_q_lora_norm', 'b_kv_lora_norm', 'b_w_uq', 'b_w_uk', 'b_w_uv', 'even_w_out', 'odd_w_in', 'c_sink', 'odd_w_out', 'final_norm']
TWIN_DIFF_INPUT = 'x'
TWIN_INPUTS = ['x', 'c', 'norm_w', 'ada_w', 'ada_b', 'even_w_in', 'a_q_norm', 'a_k_norm', 'b_q_lora_norm', 'b_kv_lora_norm', 'b_w_uq', 'b_w_uk', 'b_w_uv', 'even_w_out', 'odd_w_in', 'c_sink', 'odd_w_out', 'final_norm', 'loss_target', 'm_norm_w', 'm_ada_w', 'm_ada_b', 'm_even_w_in', 'm_a_q_norm', 'm_a_k_norm', 'm_b_q_lora_norm', 'm_b_kv_lora_norm', 'm_b_w_uq', 'm_b_w_uk', 'm_b_w_uv', 'm_even_w_out', 'm_odd_w_in', 'm_c_sink', 'm_odd_w_out', 'm_final_norm', 'v_norm_w', 'v_ada_w', 'v_ada_b', 'v_even_w_in', 'v_a_q_norm', 'v_a_k_norm', 'v_b_q_lora_norm', 'v_b_kv_lora_norm', 'v_b_w_uq', 'v_b_w_uk', 'v_b_w_uv', 'v_even_w_out', 'v_odd_w_in', 'v_c_sink', 'v_odd_w_out', 'v_final_norm']
TWIN_OUTPUTS = ['loss', 'grad_x', 'grad_norm_w', 'grad_ada_w', 'grad_ada_b', 'grad_even_w_in', 'grad_a_q_norm', 'grad_a_k_norm', 'grad_b_q_lora_norm', 'grad_b_kv_lora_norm', 'grad_b_w_uq', 'grad_b_w_uk', 'grad_b_w_uv', 'grad_even_w_out', 'grad_odd_w_in', 'grad_c_sink', 'grad_odd_w_out', 'grad_final_norm', 'delta_norm_w', 'delta_ada_w', 'delta_ada_b', 'delta_even_w_in', 'delta_a_q_norm', 'delta_a_k_norm', 'delta_b_q_lora_norm', 'delta_b_kv_lora_norm', 'delta_b_w_uq', 'delta_b_w_uk', 'delta_b_w_uv', 'delta_even_w_out', 'delta_odd_w_in', 'delta_c_sink', 'delta_odd_w_out', 'delta_final_norm', 'new_m_norm_w', 'new_m_ada_w', 'new_m_ada_b', 'new_m_even_w_in', 'new_m_a_q_norm', 'new_m_a_k_norm', 'new_m_b_q_lora_norm', 'new_m_b_kv_lora_norm', 'new_m_b_w_uq', 'new_m_b_w_uk', 'new_m_b_w_uv', 'new_m_even_w_out', 'new_m_odd_w_in', 'new_m_c_sink', 'new_m_odd_w_out', 'new_m_final_norm', 'new_v_norm_w', 'new_v_ada_w', 'new_v_ada_b', 'new_v_even_w_in', 'new_v_a_q_norm', 'new_v_a_k_norm', 'new_v_b_q_lora_norm', 'new_v_b_kv_lora_norm', 'new_v_b_w_uq', 'new_v_b_w_uk', 'new_v_b_w_uv', 'new_v_even_w_out', 'new_v_odd_w_in', 'new_v_c_sink', 'new_v_odd_w_out', 'new_v_final_norm']
TWIN_LEAF_KINDS = {'loss': 'loss', 'grad_x': 'grad_x', 'grad_norm_w': 'grad_w', 'grad_ada_w': 'grad_w', 'grad_ada_b': 'grad_w', 'grad_even_w_in': 'grad_w', 'grad_a_q_norm': 'grad_w', 'grad_a_k_norm': 'grad_w', 'grad_b_q_lora_norm': 'grad_w', 'grad_b_kv_lora_norm': 'grad_w', 'grad_b_w_uq': 'grad_w', 'grad_b_w_uk': 'grad_w', 'grad_b_w_uv': 'grad_w', 'grad_even_w_out': 'grad_w', 'grad_odd_w_in': 'grad_w', 'grad_c_sink': 'grad_w', 'grad_odd_w_out': 'grad_w', 'grad_final_norm': 'grad_w', 'delta_norm_w': 'delta_w', 'delta_ada_w': 'delta_w', 'delta_ada_b': 'delta_w', 'delta_even_w_in': 'delta_w', 'delta_a_q_norm': 'delta_w', 'delta_a_k_norm': 'delta_w', 'delta_b_q_lora_norm': 'delta_w', 'delta_b_kv_lora_norm': 'delta_w', 'delta_b_w_uq': 'delta_w', 'delta_b_w_uk': 'delta_w', 'delta_b_w_uv': 'delta_w', 'delta_even_w_out': 'delta_w', 'delta_odd_w_in': 'delta_w', 'delta_c_sink': 'delta_w', 'delta_odd_w_out': 'delta_w', 'delta_final_norm': 'delta_w', 'new_m_norm_w': 'new_m', 'new_m_ada_w': 'new_m', 'new_m_ada_b': 'new_m', 'new_m_even_w_in': 'new_m', 'new_m_a_q_norm': 'new_m', 'new_m_a_k_norm': 'new_m', 'new_m_b_q_lora_norm': 'new_m', 'new_m_b_kv_lora_norm': 'new_m', 'new_m_b_w_uq': 'new_m', 'new_m_b_w_uk': 'new_m', 'new_m_b_w_uv': 'new_m', 'new_m_even_w_out': 'new_m', 'new_m_odd_w_in': 'new_m', 'new_m_c_sink': 'new_m', 'new_m_odd_w_out': 'new_m', 'new_m_final_norm': 'new_m', 'new_v_norm_w': 'new_v', 'new_v_ada_w': 'new_v', 'new_v_ada_b': 'new_v', 'new_v_even_w_in': 'new_v', 'new_v_a_q_norm': 'new_v', 'new_v_a_k_norm': 'new_v', 'new_v_b_q_lora_norm': 'new_v', 'new_v_b_kv_lora_norm': 'new_v', 'new_v_b_w_uq': 'new_v', 'new_v_b_w_uk': 'new_v', 'new_v_b_w_uv': 'new_v', 'new_v_even_w_out': 'new_v', 'new_v_odd_w_in': 'new_v', 'new_v_c_sink': 'new_v', 'new_v_odd_w_out': 'new_v', 'new_v_final_norm': 'new_v'}


def _forward(args):
    return _fwd_reference(*[args[k] for k in FWD_PARAMS])


def _output_shape():
    def fwd():
        inp = _fwd_setup_inputs(0)
        return _fwd_reference(*[inp[k] for k in FWD_PARAMS])
    out = _jax.eval_shape(fwd)
    return out.shape, out.dtype

N_MICROBATCH = 1
ADAM_LR = 0.001
ADAM_B1 = 0.9
ADAM_B2 = 0.999
ADAM_EPS = 1e-08
ADAM_WD = 0.01
ADAM_STEP = 10
PER_EXAMPLE_BATCH_AXIS = {'x': 0, 'c': 0, 'loss_target': 0}
SHARED_INPUTS = []
_WEIGHT_DTYPES = {'norm_w': _jnp.float32, 'ada_w': _jnp.float32, 'ada_b': _jnp.float32, 'even_w_in': _jnp.float32, 'a_q_norm': _jnp.float32, 'a_k_norm': _jnp.float32, 'b_q_lora_norm': _jnp.float32, 'b_kv_lora_norm': _jnp.float32, 'b_w_uq': _jnp.float32, 'b_w_uk': _jnp.float32, 'b_w_uv': _jnp.float32, 'even_w_out': _jnp.float32, 'odd_w_in': _jnp.float32, 'c_sink': _jnp.float32, 'odd_w_out': _jnp.float32, 'final_norm': _jnp.float32}
MOMENT_SCALE = {'norm_w': 2.682122e-02, 'ada_w': 2.716515e-02, 'ada_b': 4.522126e-02, 'even_w_in': 1.854859e-02, 'a_q_norm': 1.144208e-02, 'a_k_norm': 1.235742e-02, 'b_q_lora_norm': 8.150408e-03, 'b_kv_lora_norm': 3.736320e-02, 'b_w_uq': 4.551363e-03, 'b_w_uk': 4.721110e-03, 'b_w_uv': 1.714791e-02, 'even_w_out': 1.826841e-02, 'odd_w_in': 2.421478e-02, 'c_sink': 4.511270e-02, 'odd_w_out': 2.307845e-02, 'final_norm': 3.202355e+01}


def _to_microbatches(a, axis):
    t = _jnp.moveaxis(a, axis, 0)
    t = t.reshape((N_MICROBATCH, t.shape[0] // N_MICROBATCH) + t.shape[1:])
    return _jnp.moveaxis(t, 1, axis + 1)


def setup_inputs(seed: int = 0) -> dict:
    inp = _fwd_setup_inputs(seed)
    key = _jax.random.fold_in(_jax.random.key(seed), 7919)
    shape, _ = _output_shape()
    out = dict(inp)
    out["loss_target"] = _jax.random.normal(_jax.random.fold_in(key, 0), shape, _jnp.float32)
    for i, name in enumerate(TWIN_WEIGHTS):
        w = inp[name].astype(_jnp.float32)
        if MOMENT_SCALE is None:
            s = _jnp.sqrt(_jnp.mean(_jnp.square(w)) + 1e-30)
        else:
            s = MOMENT_SCALE[name]
        km, kv = _jax.random.split(_jax.random.fold_in(key, i + 1))
        out[name] = w
        out["m_" + name] = s * _jax.random.normal(km, w.shape, _jnp.float32)
        out["v_" + name] = (s * s) * _jax.random.uniform(kv, w.shape, _jnp.float32, 0.5, 1.5)
    if N_MICROBATCH > 1:
        for name, axis in PER_EXAMPLE_BATCH_AXIS.items():
            out[name] = _to_microbatches(out[name], axis)
    return {'x': out['x'], 'c': out['c'], 'norm_w': out['norm_w'], 'ada_w': out['ada_w'], 'ada_b': out['ada_b'], 'even_w_in': out['even_w_in'], 'a_q_norm': out['a_q_norm'], 'a_k_norm': out['a_k_norm'], 'b_q_lora_norm': out['b_q_lora_norm'], 'b_kv_lora_norm': out['b_kv_lora_norm'], 'b_w_uq': out['b_w_uq'], 'b_w_uk': out['b_w_uk'], 'b_w_uv': out['b_w_uv'], 'even_w_out': out['even_w_out'], 'odd_w_in': out['odd_w_in'], 'c_sink': out['c_sink'], 'odd_w_out': out['odd_w_out'], 'final_norm': out['final_norm'], 'loss_target': out['loss_target'], 'm_norm_w': out['m_norm_w'], 'm_ada_w': out['m_ada_w'], 'm_ada_b': out['m_ada_b'], 'm_even_w_in': out['m_even_w_in'], 'm_a_q_norm': out['m_a_q_norm'], 'm_a_k_norm': out['m_a_k_norm'], 'm_b_q_lora_norm': out['m_b_q_lora_norm'], 'm_b_kv_lora_norm': out['m_b_kv_lora_norm'], 'm_b_w_uq': out['m_b_w_uq'], 'm_b_w_uk': out['m_b_w_uk'], 'm_b_w_uv': out['m_b_w_uv'], 'm_even_w_out': out['m_even_w_out'], 'm_odd_w_in': out['m_odd_w_in'], 'm_c_sink': out['m_c_sink'], 'm_odd_w_out': out['m_odd_w_out'], 'm_final_norm': out['m_final_norm'], 'v_norm_w': out['v_norm_w'], 'v_ada_w': out['v_ada_w'], 'v_ada_b': out['v_ada_b'], 'v_even_w_in': out['v_even_w_in'], 'v_a_q_norm': out['v_a_q_norm'], 'v_a_k_norm': out['v_a_k_norm'], 'v_b_q_lora_norm': out['v_b_q_lora_norm'], 'v_b_kv_lora_norm': out['v_b_kv_lora_norm'], 'v_b_w_uq': out['v_b_w_uq'], 'v_b_w_uk': out['v_b_w_uk'], 'v_b_w_uv': out['v_b_w_uv'], 'v_even_w_out': out['v_even_w_out'], 'v_odd_w_in': out['v_odd_w_in'], 'v_c_sink': out['v_c_sink'], 'v_odd_w_out': out['v_odd_w_out'], 'v_final_norm': out['v_final_norm']}


def _loss(weights, diff, rest, loss_target):
    with _jax.named_scope("forward"):
        args = {**rest, TWIN_DIFF_INPUT: diff, **{k: w.astype(_WEIGHT_DTYPES[k]) for k, w in weights.items()}}
        y = _forward(args)
    with _jax.named_scope("loss_head"):
        err = _jnp.square(y.astype(_jnp.float32) - loss_target)
        return 0.5 * _jnp.sum(_jnp.mean(err, axis=-1)) if err.ndim else 0.5 * err


def _adamw(w, g, m, v):
    m = ADAM_B1 * m + (1.0 - ADAM_B1) * g
    v = ADAM_B2 * v + (1.0 - ADAM_B2) * _jnp.square(g)
    m_hat = m / (1.0 - ADAM_B1 ** ADAM_STEP)
    v_hat = v / (1.0 - ADAM_B2 ** ADAM_STEP)
    delta = -ADAM_LR * (m_hat / (_jnp.sqrt(v_hat) + ADAM_EPS) + ADAM_WD * w)
    return delta, m, v


def reference(x, c, norm_w, ada_w, ada_b, even_w_in, a_q_norm, a_k_norm, b_q_lora_norm, b_kv_lora_norm, b_w_uq, b_w_uk, b_w_uv, even_w_out, odd_w_in, c_sink, odd_w_out, final_norm, loss_target, m_norm_w, m_ada_w, m_ada_b, m_even_w_in, m_a_q_norm, m_a_k_norm, m_b_q_lora_norm, m_b_kv_lora_norm, m_b_w_uq, m_b_w_uk, m_b_w_uv, m_even_w_out, m_odd_w_in, m_c_sink, m_odd_w_out, m_final_norm, v_norm_w, v_ada_w, v_ada_b, v_even_w_in, v_a_q_norm, v_a_k_norm, v_b_q_lora_norm, v_b_kv_lora_norm, v_b_w_uq, v_b_w_uk, v_b_w_uv, v_even_w_out, v_odd_w_in, v_c_sink, v_odd_w_out, v_final_norm):
    given = dict(x=x, c=c, norm_w=norm_w, ada_w=ada_w, ada_b=ada_b, even_w_in=even_w_in, a_q_norm=a_q_norm, a_k_norm=a_k_norm, b_q_lora_norm=b_q_lora_norm, b_kv_lora_norm=b_kv_lora_norm, b_w_uq=b_w_uq, b_w_uk=b_w_uk, b_w_uv=b_w_uv, even_w_out=even_w_out, odd_w_in=odd_w_in, c_sink=c_sink, odd_w_out=odd_w_out, final_norm=final_norm, loss_target=loss_target, m_norm_w=m_norm_w, m_ada_w=m_ada_w, m_ada_b=m_ada_b, m_even_w_in=m_even_w_in, m_a_q_norm=m_a_q_norm, m_a_k_norm=m_a_k_norm, m_b_q_lora_norm=m_b_q_lora_norm, m_b_kv_lora_norm=m_b_kv_lora_norm, m_b_w_uq=m_b_w_uq, m_b_w_uk=m_b_w_uk, m_b_w_uv=m_b_w_uv, m_even_w_out=m_even_w_out, m_odd_w_in=m_odd_w_in, m_c_sink=m_c_sink, m_odd_w_out=m_odd_w_out, m_final_norm=m_final_norm, v_norm_w=v_norm_w, v_ada_w=v_ada_w, v_ada_b=v_ada_b, v_even_w_in=v_even_w_in, v_a_q_norm=v_a_q_norm, v_a_k_norm=v_a_k_norm, v_b_q_lora_norm=v_b_q_lora_norm, v_b_kv_lora_norm=v_b_kv_lora_norm, v_b_w_uq=v_b_w_uq, v_b_w_uk=v_b_w_uk, v_b_w_uv=v_b_w_uv, v_even_w_out=v_even_w_out, v_odd_w_in=v_odd_w_in, v_c_sink=v_c_sink, v_odd_w_out=v_odd_w_out, v_final_norm=v_final_norm)
    weights = {n: given[n] for n in TWIN_WEIGHTS}
    shared = {n: given[n] for n in SHARED_INPUTS}
    per_example = {n: given[n] for n in ['x', 'c']}
    grad_fn = _jax.value_and_grad(_loss, argnums=(0, 1))

    def one_microbatch(ex, loss_target):
        ex = dict(ex)
        diff = ex.pop(TWIN_DIFF_INPUT)
        return grad_fn(weights, diff, {**shared, **ex}, loss_target)

    if N_MICROBATCH == 1:
        loss, (grad_w, grad_x) = one_microbatch(per_example, given["loss_target"])
    else:
        def body(carry, xs):
            loss_sum, grad_sum = carry
            l_k, (gw_k, gx_k) = one_microbatch(xs[0], xs[1])
            with _jax.named_scope("update"):
                return (loss_sum + l_k, _jax.tree.map(_jnp.add, grad_sum, gw_k)), gx_k

        init = (_jnp.zeros((), _jnp.float32), _jax.tree.map(_jnp.zeros_like, weights))
        (loss, grad_w), grad_x = _jax.lax.scan(body, init, (per_example, given["loss_target"]))
    with _jax.named_scope("update"):
        delta_w, new_m, new_v = {}, {}, {}
        for n in TWIN_WEIGHTS:
            delta_w[n], new_m[n], new_v[n] = _adamw(weights[n], grad_w[n], given["m_" + n], given["v_" + n])
    return (loss, grad_x, *[grad_w[n] for n in TWIN_WEIGHTS], *[delta_w[n] for n in TWIN_WEIGHTS],
            *[new_m[n] for n in TWIN_WEIGHTS], *[new_v[n] for n in TWIN_WEIGHTS])
```

```python
import functools

import numpy as np
import jax
import jax.numpy as jnp
from jax import lax
from jax.experimental import pallas as pl
from jax.experimental.pallas import tpu as pltpu

F32 = jnp.float32
BF16 = jnp.bfloat16
HIGHEST = lax.Precision.HIGHEST
MESH_ID = pl.DeviceIdType.MESH

D_MODEL = 1024
HEAD_DIM = 64
GRID_W = 64
EPS = 1e-6
ROPE_THETA = 10000.0
A_HEADS, A_KV_HEADS = 8, 2
B_HEADS, B_NOPE, B_ROPE, B_V = 8, 64, 32, 64
B_Q_LORA, B_KV_LORA = 256, 128
C_HEADS, C_KV_HEADS = 16, 4
WINDOW = 128
EVEN_IN, ODD_IN = 2208, 2560
EVEN_P = 2304
N_CHIPS, N_DEV = 4, 8
LANES = 128
NEG = -1e30
VMEM_LIMIT = 60 * 1024 * 1024

ADAM_LR, ADAM_B1, ADAM_B2, ADAM_EPS, ADAM_WD, ADAM_STEP = 0.001, 0.9, 0.999, 1e-08, 0.01, 10

ROW_TILE = 256


def _dot(a, b):
    return lax.dot_general(a, b, (((1,), (0,)), ((), ())), preferred_element_type=F32)


def _dot_nt(a, b):
    return lax.dot_general(a, b, (((1,), (1,)), ((), ())), preferred_element_type=F32)


def _dot_tn(a, b):
    return lax.dot_general(a, b, (((0,), (0,)), ((), ())), preferred_element_type=F32)


def _dot_f32(a, b):
    return lax.dot_general(a, b, (((1,), (0,)), ((), ())), precision=HIGHEST, preferred_element_type=F32)


def _sigmoid(x):
    return 1.0 / (1.0 + jnp.exp(-x))


def _silu_and_grad(g):
    s = _sigmoid(g)
    return g * s, s * (1.0 + g * (1.0 - s))


def _lane_iota():
    return lax.broadcasted_iota(jnp.int32, (1, LANES), 1)


def _partner(x, lane):
    return jnp.where((lane % 32) < 16, pltpu.roll(x, LANES - 16, 1), pltpu.roll(x, 16, 1))


def _rot(x, cos, sin_signed, lane):
    return x * cos + _partner(x, lane) * sin_signed


def _rot_bwd(dy, cos, sin_signed, lane):
    return dy * cos + _partner(dy * sin_signed, lane)


def _rms(x):
    return lax.rsqrt(jnp.mean(x * x, axis=-1, keepdims=True) + EPS)


def _rms_bwd(x, r, g):
    return r * g - x * (r * r * r) * jnp.mean(x * g, axis=-1, keepdims=True)


def _seg_mean(v, seg_ones):
    return _dot_f32(v, seg_ones) * (1.0 / HEAD_DIM)


def _row_spec(ts, cols):
    return pl.BlockSpec((ts, cols), lambda i: (i, 0))


def _full_spec(shape, single=True):
    nd = len(shape)
    if single:
        return pl.BlockSpec(shape, lambda i: (0,) * nd, pipeline_mode=pl.Buffered(1))
    return pl.BlockSpec(shape, lambda i: (0,) * nd)


def _sds(shape, dtype):
    return jax.ShapeDtypeStruct(shape, dtype)


def _params(sem):
    return pltpu.CompilerParams(dimension_semantics=sem, vmem_limit_bytes=VMEM_LIMIT)


def _even_pre_fwd(x, mod, nw, wie, qn, kn, seg, ca, sa, ct, st, qln, kvln, wuq, wuk):
    S = x.shape[0]
    ts = min(ROW_TILE, S)

    def body(x_ref, mod_ref, nw_ref, wie_ref, qn_ref, kn_ref, seg_ref, ca_ref, sa_ref, ct_ref, st_ref, qln_ref,
             kvln_ref, wuq_ref, wuk_ref, h_ref, proj_ref, qa_ref, ka_ref, va_ref, qcat_ref, kcat_ref):
        xv = x_ref[...]
        h = (xv * _rms(xv) * nw_ref[...]) * (1.0 + mod_ref[1:2, :]) + mod_ref[0:1, :]
        hb = h.astype(BF16)
        h_ref[...] = hb
        proj = _dot(hb, wie_ref[...])
        proj_ref[...] = proj
        lane = _lane_iota()
        ca_v, sa_v, ct_v, st_v = ca_ref[...], sa_ref[...], ct_ref[...], st_ref[...]
        seg_v = seg_ref[...]
        for cb in range(4):
            xc = proj[:, LANES * cb:LANES * (cb + 1)]
            r = lax.rsqrt(_seg_mean(xc * xc, seg_v) + EPS)
            y = _rot(xc * r * qn_ref[...], ca_v, sa_v, lane)
            qa_ref[:, LANES * cb:LANES * (cb + 1)] = (y * 0.125).astype(BF16)
        kc = proj[:, 512:640]
        r = lax.rsqrt(_seg_mean(kc * kc, seg_v) + EPS)
        ka_ref[...] = _rot(kc * r * kn_ref[...], ca_v, sa_v, lane).astype(BF16)
        va_ref[...] = proj[:, 640:768].astype(BF16)
        cq = proj[:, 1280:1536]
        cqn = (cq * _rms(cq) * qln_ref[...]).astype(BF16)
        ckv = proj[:, 1536:1664]
        ckvn = ckv * _rms(ckv) * kvln_ref[...]
        qb = _dot(cqn, wuq_ref[...])
        qlat = _dot(qb[:, 0:512].astype(BF16), wuk_ref[...])
        for hh in range(B_HEADS):
            qcat_ref[hh, :, 0:LANES] = qlat[:, LANES * hh:LANES * (hh + 1)].astype(BF16)
            qr = _rot(qb[:, 512 + LANES * hh:512 + LANES * (hh + 1)], ct_v, st_v, lane)
            qcat_ref[hh, :, LANES:2 * LANES] = qr.astype(BF16)
        kcat_ref[:, 0:LANES] = ckvn.astype(BF16)
        kcat_ref[:, LANES:2 * LANES] = _rot(proj[:, 1664:1792], ct_v, st_v, lane).astype(BF16)

    return pl.pallas_call(
        body, name="even_pre_fwd", grid=(S // ts,),
        in_specs=[_row_spec(ts, D_MODEL), _full_spec((3, D_MODEL)), _full_spec((1, D_MODEL)), _full_spec((D_MODEL, EVEN_P)),
                  _full_spec((1, LANES)), _full_spec((1, LANES)), _full_spec((LANES, LANES)),
                  _row_spec(ts, LANES), _row_spec(ts, LANES), _row_spec(ts, LANES), _row_spec(ts, LANES),
                  _full_spec((1, B_Q_LORA)), _full_spec((1, B_KV_LORA)), _full_spec((B_Q_LORA, 1536)), _full_spec((512, 1024))],
        out_specs=[_row_spec(ts, D_MODEL), _row_spec(ts, EVEN_P), _row_spec(ts, 512), _row_spec(ts, LANES), _row_spec(ts, LANES),
                   pl.BlockSpec((B_HEADS, ts, 2 * LANES), lambda i: (0, i, 0)), _row_spec(ts, 2 * LANES)],
        out_shape=[_sds((S, D_MODEL), BF16), _sds((S, EVEN_P), F32), _sds((S, 512), BF16), _sds((S, LANES), BF16),
                   _sds((S, LANES), BF16), _sds((B_HEADS, S, 2 * LANES), BF16), _sds((S, 2 * LANES), BF16)],
        compiler_params=_params(("arbitrary",)),
    )(x, mod, nw, wie, qn, kn, seg, ca, sa, ct, st, qln, kvln, wuq, wuk)


def _band_bias_mask(s, slope, i, jj, nq, tq, tk):
    row = lax.broadcasted_iota(jnp.int32, (2 * tq, tk), 0)
    t_pos = i * tq + jnp.where(row >= tq, row - tq, row)
    s_pos = jj * tk + lax.broadcasted_iota(jnp.int32, (2 * tq, tk), 1)
    dist = jnp.abs(t_pos - s_pos)
    s = s - slope * dist.astype(F32)
    valid = (dist <= WINDOW) & (jj >= 0) & (jj < nq)
    return jnp.where(valid, s, NEG)


def _band_offset(j):
    return jnp.where(j == 1, -1, 0) + jnp.where(j == 2, 1, 0)


def _pp_fwd(q, k, v, *, kdiv, tq, tk, name, slope=None, sink=None):
    S = q.shape[0]
    nb = q.shape[1] // LANES
    nq = S // tq
    band = slope is not None
    nkv = 3 if band else S // tk
    if band:
        assert tq == tk and WINDOW <= tk

    def kv_map(b, i, j):
        if band:
            return (jnp.clip(i + _band_offset(j), 0, nq - 1), b // kdiv)
        return (j, b // kdiv)

    def body(*refs):
        if band:
            q_ref, k_ref, v_ref, slope_ref, sink_ref, o_ref, lse_ref, qs, m_s, l_s, acc = refs
        else:
            q_ref, k_ref, v_ref, o_ref, lse_ref, qs, m_s, l_s, acc = refs
        i, j = pl.program_id(1), pl.program_id(2)
        lo = _lane_iota() < HEAD_DIM

        @pl.when(j == 0)
        def _():
            qv = q_ref[...]
            zero = jnp.zeros_like(qv)
            qs[0:tq, :] = jnp.where(lo, qv, zero)
            qs[tq:2 * tq, :] = jnp.where(lo, zero, qv)
            m_s[...] = jnp.full((2 * tq, 1), NEG, F32)
            l_s[...] = jnp.zeros((2 * tq, 1), F32)
            acc[...] = jnp.zeros((2 * tq, LANES), F32)

        s = _dot_nt(qs[...], k_ref[...])
        if band:
            s = _band_bias_mask(s, slope_ref[0], i, i + _band_offset(j), nq, tq, tk)
        m_prev = m_s[...]
        m_new = jnp.maximum(m_prev, jnp.max(s, axis=-1, keepdims=True))
        alpha = jnp.exp(m_prev - m_new)
        p = jnp.exp(s - m_new)
        l_s[...] = alpha * l_s[...] + jnp.sum(p, axis=-1, keepdims=True)
        acc[...] = alpha * acc[...] + _dot(p.astype(BF16), v_ref[...])
        m_s[...] = m_new

        @pl.when(j == nkv - 1)
        def _():
            m_f, l_f, a_f = m_s[...], l_s[...], acc[...]
            if band:
                sk = sink_ref[0]
                m_t = jnp.maximum(m_f, sk)
                a = jnp.exp(m_f - m_t)
                l_f = l_f * a + jnp.exp(sk - m_t)
                a_f = a_f * a
                m_f = m_t
            ov = a_f / l_f
            o_ref[...] = jnp.where(lo, ov[0:tq, :], ov[tq:2 * tq, :])
            lse_ref[0, 0] = m_f + jnp.log(l_f)

    in_specs = [pl.BlockSpec((tq, LANES), lambda b, i, j: (i, b)), pl.BlockSpec((tk, LANES), kv_map),
                pl.BlockSpec((tk, LANES), kv_map)]
    args = [q, k, v]
    if band:
        in_specs += [pl.BlockSpec((1, 2 * tq, 1), lambda b, i, j: (b, 0, 0))] * 2
        args += [slope, sink]
    return pl.pallas_call(
        body, name=name, grid=(nb, nq, nkv), in_specs=in_specs,
        out_specs=[pl.BlockSpec((tq, LANES), lambda b, i, j: (i, b)),
                   pl.BlockSpec((1, 1, 2 * tq, 1), lambda b, i, j: (b, i, 0, 0))],
        out_shape=[_sds((S, nb * LANES), F32), _sds((nb, nq, 2 * tq, 1), F32)],
        scratch_shapes=[pltpu.VMEM((2 * tq, LANES), BF16), pltpu.VMEM((2 * tq, 1), F32), pltpu.VMEM((2 * tq, 1), F32),
                        pltpu.VMEM((2 * tq, LANES), F32)],
        compiler_params=_params(("arbitrary", "arbitrary", "arbitrary")),
    )(*args)


def _pp_bwd(q, k, v, o, do, lse, *, kdiv, tq, tk, name, slope=None, sink=None):
    S = q.shape[0]
    nb = q.shape[1] // LANES
    nkb = k.shape[1] // LANES
    nq = S // tq
    band = slope is not None
    nkv = 3 if band else S // tk

    def kv_map(b, i, j):
        if band:
            return (jnp.clip(i + _band_offset(j), 0, nq - 1), b // kdiv)
        return (j, b // kdiv)

    def body(*refs):
        if band:
            (q_ref, k_ref, v_ref, o_ref, do_ref, lse_ref, slope_ref, sink_ref, dq_ref, dk_ref, dv_ref, dsink_ref,
             qs, dos, delta_s, dq_acc) = refs
        else:
            q_ref, k_ref, v_ref, o_ref, do_ref, lse_ref, dq_ref, dk_ref, dv_ref, qs, dos, delta_s, dq_acc = refs
        b, i, j = pl.program_id(0), pl.program_id(1), pl.program_id(2)
        lo = _lane_iota() < HEAD_DIM

        @pl.when((b % kdiv == 0) & (i == 0) & (j == 0))
        def _():
            dk_ref[...] = jnp.zeros((S, LANES), F32)
            dv_ref[...] = jnp.zeros((S, LANES), F32)

        @pl.when(j == 0)
        def _():
            qv = q_ref[...]
            zero = jnp.zeros_like(qv)
            qs[0:tq, :] = jnp.where(lo, qv, zero)
            qs[tq:2 * tq, :] = jnp.where(lo, zero, qv)
            dov = do_ref[...]
            dob = dov.astype(BF16)
            zb = jnp.zeros_like(dob)
            dos[0:tq, :] = jnp.where(lo, dob, zb)
            dos[tq:2 * tq, :] = jnp.where(lo, zb, dob)
            prod = dov * o_ref[...]
            d_lo = jnp.sum(jnp.where(lo, prod, 0.0), axis=-1, keepdims=True)
            d_hi = jnp.sum(jnp.where(lo, 0.0, prod), axis=-1, keepdims=True)
            delta_s[0:tq, :] = d_lo
            delta_s[tq:2 * tq, :] = d_hi
            dq_acc[...] = jnp.zeros((2 * tq, LANES), F32)
            if band:
                @pl.when(i == 0)
                def _():
                    dsink_ref[...] = jnp.zeros((1, 8, LANES), F32)
                lse_v = lse_ref[0, 0]
                sk = sink_ref[0]
                c_lo = -jnp.exp(sk[0:tq] - lse_v[0:tq]) * d_lo
                c_hi = -jnp.exp(sk[tq:2 * tq] - lse_v[tq:2 * tq]) * d_hi
                dsink_ref[0, 0:1, :] += jnp.broadcast_to(jnp.sum(c_lo, axis=0, keepdims=True), (1, LANES))
                dsink_ref[0, 1:2, :] += jnp.broadcast_to(jnp.sum(c_hi, axis=0, keepdims=True), (1, LANES))

        kv = k_ref[...]
        s = _dot_nt(qs[...], kv)
        if band:
            jj = i + _band_offset(j)
            s = _band_bias_mask(s, slope_ref[0], i, jj, nq, tq, tk)
            jc = jnp.clip(jj, 0, nq - 1)
        else:
            jc = j
        p = jnp.exp(s - lse_ref[0, 0])
        dp = _dot_nt(dos[...], v_ref[...])
        ds = (p * (dp - delta_s[...])).astype(BF16)
        rows = pl.ds(pl.multiple_of(jc * tk, tk), tk)
        dv_ref[rows, :] += _dot_tn(p.astype(BF16), dos[...])
        dk_ref[rows, :] += _dot_tn(ds, qs[...])
        dq_acc[...] += _dot(ds, kv)

        @pl.when(j == nkv - 1)
        def _():
            dqv = dq_acc[...]
            dq_ref[...] = jnp.where(lo, dqv[0:tq, :], dqv[tq:2 * tq, :])

    qmap = lambda b, i, j: (i, b)
    in_specs = [pl.BlockSpec((tq, LANES), qmap), pl.BlockSpec((tk, LANES), kv_map), pl.BlockSpec((tk, LANES), kv_map),
                pl.BlockSpec((tq, LANES), qmap), pl.BlockSpec((tq, LANES), qmap),
                pl.BlockSpec((1, 1, 2 * tq, 1), lambda b, i, j: (b, i, 0, 0))]
    args = [q, k, v, o, do, lse]
    out_specs = [pl.BlockSpec((tq, LANES), qmap), pl.BlockSpec((S, LANES), lambda b, i, j: (0, b // kdiv)),
                 pl.BlockSpec((S, LANES), lambda b, i, j: (0, b // kdiv))]
    out_shape = [_sds((S, nb * LANES), F32), _sds((S, nkb * LANES), F32), _sds((S, nkb * LANES), F32)]
    if band:
        in_specs += [pl.BlockSpec((1, 2 * tq, 1), lambda b, i, j: (b, 0, 0))] * 2
        args += [slope, sink]
        out_specs.append(pl.BlockSpec((1, 8, LANES), lambda b, i, j: (b, 0, 0)))
        out_shape.append(_sds((nb, 8, LANES), F32))
    return pl.pallas_call(
        body, name=name, grid=(nb, nq, nkv), in_specs=in_specs, out_specs=out_specs, out_shape=out_shape,
        scratch_shapes=[pltpu.VMEM((2 * tq, LANES), BF16), pltpu.VMEM((2 * tq, LANES), BF16), pltpu.VMEM((2 * tq, 1), F32),
                        pltpu.VMEM((2 * tq, LANES), F32)],
        compiler_params=_params(("arbitrary", "arbitrary", "arbitrary")),
    )(*args)


MLA_SCALE = (B_NOPE + B_ROPE) ** -0.5


def _mla_fwd(q, kcat, *, tq, tk):
    S = kcat.shape[0]
    nq, nkv = S // tq, S // tk
    R = B_HEADS * tq

    def body(q_ref, k_ref, o_ref, lse_ref, m_s, l_s, acc):
        j = pl.program_id(1)

        @pl.when(j == 0)
        def _():
            m_s[...] = jnp.full((R, 1), NEG, F32)
            l_s[...] = jnp.zeros((R, 1), F32)
            acc[...] = jnp.zeros((R, LANES), F32)

        kv = k_ref[...]
        s = _dot_nt(q_ref[...].reshape(R, 2 * LANES), kv) * MLA_SCALE
        m_prev = m_s[...]
        m_new = jnp.maximum(m_prev, jnp.max(s, axis=-1, keepdims=True))
        alpha = jnp.exp(m_prev - m_new)
        p = jnp.exp(s - m_new)
        l_s[...] = alpha * l_s[...] + jnp.sum(p, axis=-1, keepdims=True)
        acc[...] = alpha * acc[...] + _dot(p.astype(BF16), kv[:, 0:LANES])
        m_s[...] = m_new

        @pl.when(j == nkv - 1)
        def _():
            l_f = l_s[...]
            o_ref[...] = (acc[...] / l_f).reshape(B_HEADS, tq, LANES)
            lse_ref[0] = m_s[...] + jnp.log(l_f)

    return pl.pallas_call(
        body, name="mla_fwd", grid=(nq, nkv),
        in_specs=[pl.BlockSpec((B_HEADS, tq, 2 * LANES), lambda i, j: (0, i, 0)), pl.BlockSpec((tk, 2 * LANES), lambda i, j: (j, 0))],
        out_specs=[pl.BlockSpec((B_HEADS, tq, LANES), lambda i, j: (0, i, 0)), pl.BlockSpec((1, R, 1), lambda i, j: (i, 0, 0))],
        out_shape=[_sds((B_HEADS, S, LANES), F32), _sds((nq, R, 1), F32)],
        scratch_shapes=[pltpu.VMEM((R, 1), F32), pltpu.VMEM((R, 1), F32), pltpu.VMEM((R, LANES), F32)],
        compiler_params=_params(("arbitrary", "arbitrary")),
    )(q, kcat)


def _mla_bwd(q, kcat, o, do, lse, *, tq, tk):
    S = kcat.shape[0]
    nq, nkv = S // tq, S // tk
    R = B_HEADS * tq

    def body(q_ref, k_ref, o_ref, do_ref, lse_ref, dq_ref, dk_ref, dos, delta_s, dq_acc):
        i, j = pl.program_id(0), pl.program_id(1)

        @pl.when((i == 0) & (j == 0))
        def _():
            dk_ref[...] = jnp.zeros((S, 2 * LANES), F32)

        @pl.when(j == 0)
        def _():
            dov = do_ref[...].reshape(R, LANES)
            dos[...] = dov.astype(BF16)
            delta_s[...] = jnp.sum(dov * o_ref[...].reshape(R, LANES), axis=-1, keepdims=True)
            dq_acc[...] = jnp.zeros((R, 2 * LANES), F32)

        kv = k_ref[...]
        qv = q_ref[...].reshape(R, 2 * LANES)
        s = _dot_nt(qv, kv) * MLA_SCALE
        p = jnp.exp(s - lse_ref[0])
        dp = _dot_nt(dos[...], kv[:, 0:LANES])
        ds = (p * (dp - delta_s[...]) * MLA_SCALE).astype(BF16)
        rows = pl.ds(pl.multiple_of(j * tk, tk), tk)
        dk_ref[rows, :] += _dot_tn(ds, qv)
        dk_ref[rows, 0:LANES] += _dot_tn(p.astype(BF16), dos[...])
        dq_acc[...] += _dot(ds, kv)

        @pl.when(j == nkv - 1)
        def _():
            dq_ref[...] = dq_acc[...].reshape(B_HEADS, tq, 2 * LANES)

    return pl.pallas_call(
        body, name="mla_bwd", grid=(nq, nkv),
        in_specs=[pl.BlockSpec((B_HEADS, tq, 2 * LANES), lambda i, j: (0, i, 0)), pl.BlockSpec((tk, 2 * LANES), lambda i, j: (j, 0)),
                  pl.BlockSpec((B_HEADS, tq, LANES), lambda i, j: (0, i, 0)), pl.BlockSpec((B_HEADS, tq, LANES), lambda i, j: (0, i, 0)),
                  pl.BlockSpec((1, R, 1), lambda i, j: (i, 0, 0))],
        out_specs=[pl.BlockSpec((B_HEADS, tq, 2 * LANES), lambda i, j: (0, i, 0)), pl.BlockSpec((S, 2 * LANES), lambda i, j: (0, 0))],
        out_shape=[_sds((B_HEADS, S, 2 * LANES), F32), _sds((S, 2 * LANES), F32)],
        scratch_shapes=[pltpu.VMEM((R, LANES), BF16), pltpu.VMEM((R, 1), F32), pltpu.VMEM((R, 2 * LANES), F32)],
        compiler_params=_params(("arbitrary", "arbitrary")),
    )(q, kcat, o, do, lse)


def _sum_rows(v):
    return jnp.sum(v, axis=0, keepdims=True)


def _norm_mod_bwd(dh, xv, mod_ref, nw_ref, stats_ref):
    r = _rms(xv)
    xn = xv * r
    nw = nw_ref[...]
    stats_ref[0:1, :] += _sum_rows(dh)
    stats_ref[1:2, :] += _sum_rows(dh * (xn * nw))
    dn = dh * (1.0 + mod_ref[1:2, :])
    stats_ref[2:3, :] += _sum_rows(dn * xn)
    return _rms_bwd(xv, r, dn * nw)


def _even_post_fwd(oa, olat, proj, x, gate, wuv, woe):
    S = x.shape[0]
    ts = min(ROW_TILE, S)

    def body(oa_ref, ol_ref, proj_ref, x_ref, gate_ref, wuv_ref, woe_ref, y_ref, x1_ref):
        sa, _ = _silu_and_grad(proj_ref[:, 768:1280])
        sb, _ = _silu_and_grad(proj_ref[:, 1792:2304])
        olc = jnp.concatenate([ol_ref[hh] for hh in range(B_HEADS)], axis=1).astype(BF16)
        ob = _dot(olc, wuv_ref[...])
        mix = jnp.concatenate([oa_ref[...] * sa, ob * sb], axis=1).astype(BF16)
        y = _dot(mix, woe_ref[...])
        y_ref[...] = y
        x1_ref[...] = x_ref[...] + gate_ref[...] * y

    return pl.pallas_call(
        body, name="even_post_fwd", grid=(S // ts,),
        in_specs=[_row_spec(ts, 512), pl.BlockSpec((B_HEADS, ts, LANES), lambda i: (0, i, 0)), _row_spec(ts, EVEN_P),
                  _row_spec(ts, D_MODEL), _full_spec((1, D_MODEL)), _full_spec((1024, 512)), _full_spec((1024, D_MODEL))],
        out_specs=[_row_spec(ts, D_MODEL), _row_spec(ts, D_MODEL)],
        out_shape=[_sds((S, D_MODEL), F32), _sds((S, D_MODEL), F32)],
        compiler_params=_params(("arbitrary",)),
    )(oa, olat, proj, x, gate, wuv, woe)


def _odd_pre_fwd(x, mod, nw, wio):
    S = x.shape[0]
    ts = min(ROW_TILE, S)

    def body(x_ref, mod_ref, nw_ref, wio_ref, h_ref, proj_ref, q_ref, k_ref, v_ref):
        xv = x_ref[...]
        h = (xv * _rms(xv) * nw_ref[...]) * (1.0 + mod_ref[1:2, :]) + mod_ref[0:1, :]
        hb = h.astype(BF16)
        h_ref[...] = hb
        proj = _dot(hb, wio_ref[...])
        proj_ref[...] = proj
        q_ref[...] = (proj[:, 0:1024] * 0.125).astype(BF16)
        k_ref[...] = proj[:, 1024:1280].astype(BF16)
        v_ref[...] = proj[:, 1280:1536].astype(BF16)

    return pl.pallas_call(
        body, name="odd_pre_fwd", grid=(S // ts,),
        in_specs=[_row_spec(ts, D_MODEL), _full_spec((3, D_MODEL)), _full_spec((1, D_MODEL)), _full_spec((D_MODEL, ODD_IN))],
        out_specs=[_row_spec(ts, D_MODEL), _row_spec(ts, ODD_IN), _row_spec(ts, 1024), _row_spec(ts, 256), _row_spec(ts, 256)],
        out_shape=[_sds((S, D_MODEL), BF16), _sds((S, ODD_IN), F32), _sds((S, 1024), BF16), _sds((S, 256), BF16),
                   _sds((S, 256), BF16)],
        compiler_params=_params(("arbitrary",)),
    )(x, mod, nw, wio)


def _odd_post(oc, proj, x1, gate, woo, fw, tgt):
    S = x1.shape[0]
    ts = min(ROW_TILE, S)

    def body(oc_ref, proj_ref, x_ref, gate_ref, woo_ref, fw_ref, tgt_ref, doc_ref, dgc_ref, dx2_ref, dwoo_ref, stats_ref):
        @pl.when(pl.program_id(0) == 0)
        def _():
            dwoo_ref[...] = jnp.zeros((D_MODEL, D_MODEL), F32)
            stats_ref[...] = jnp.zeros((8, D_MODEL), F32)

        ocv = oc_ref[...]
        sg, dsg = _silu_and_grad(proj_ref[:, 1536:2560])
        mix = (ocv * sg).astype(BF16)
        woo_v = woo_ref[...]
        y = _dot(mix, woo_v)
        gate_v = gate_ref[...]
        x2 = x_ref[...] + gate_v * y
        r = _rms(x2)
        fw_v = fw_ref[...]
        xn = x2 * r
        err = xn * fw_v - tgt_ref[...]
        dout = err * (1.0 / D_MODEL)
        dx2 = _rms_bwd(x2, r, dout * fw_v)
        dx2_ref[...] = dx2
        stats_ref[0:1, :] += _sum_rows(dout * xn)
        stats_ref[1:2, :] += _sum_rows(dx2 * y)
        loss_t = 0.5 * jnp.sum(_sum_rows(err * dout), axis=-1, keepdims=True)
        stats_ref[2:3, :] += jnp.broadcast_to(loss_t, (1, D_MODEL))
        dy = (gate_v * dx2).astype(BF16)
        dmix = _dot_nt(dy, woo_v)
        dwoo_ref[...] += _dot_tn(mix, dy)
        doc_ref[...] = dmix * sg
        dgc_ref[...] = dmix * ocv * dsg

    return pl.pallas_call(
        body, name="odd_post", grid=(S // ts,),
        in_specs=[_row_spec(ts, D_MODEL), _row_spec(ts, ODD_IN), _row_spec(ts, D_MODEL), _full_spec((1, D_MODEL)),
                  _full_spec((D_MODEL, D_MODEL)), _full_spec((1, D_MODEL)), _row_spec(ts, D_MODEL)],
        out_specs=[_row_spec(ts, D_MODEL), _row_spec(ts, D_MODEL), _row_spec(ts, D_MODEL),
                   _full_spec((D_MODEL, D_MODEL), single=False), _full_spec((8, D_MODEL), single=False)],
        out_shape=[_sds((S, D_MODEL), F32), _sds((S, D_MODEL), F32), _sds((S, D_MODEL), F32), _sds((D_MODEL, D_MODEL), F32),
                   _sds((8, D_MODEL), F32)],
        compiler_params=_params(("arbitrary",)),
    )(oc, proj, x1, gate, woo, fw, tgt)


def _odd_pre_bwd(dq, dk, dv, dgc, h, x, dx_res, mod, nw, wio):
    S = x.shape[0]
    ts = min(ROW_TILE, S)

    def body(dq_ref, dk_ref, dv_ref, dgc_ref, h_ref, x_ref, dxr_ref, mod_ref, nw_ref, wio_ref, dx_ref, dw_ref, stats_ref):
        @pl.when(pl.program_id(0) == 0)
        def _():
            dw_ref[...] = jnp.zeros((D_MODEL, ODD_IN), F32)
            stats_ref[...] = jnp.zeros((8, D_MODEL), F32)

        dproj = jnp.concatenate([dq_ref[...] * 0.125, dk_ref[...], dv_ref[...], dgc_ref[...]], axis=1).astype(BF16)
        dh = _dot_nt(dproj, wio_ref[...])
        dw_ref[...] += _dot_tn(h_ref[...], dproj)
        dx_ref[...] = dxr_ref[...] + _norm_mod_bwd(dh, x_ref[...], mod_ref, nw_ref, stats_ref)

    return pl.pallas_call(
        body, name="odd_pre_bwd", grid=(S // ts,),
        in_specs=[_row_spec(ts, 1024), _row_spec(ts, 256), _row_spec(ts, 256), _row_spec(ts, 1024), _row_spec(ts, D_MODEL),
                  _row_spec(ts, D_MODEL), _row_spec(ts, D_MODEL), _full_spec((3, D_MODEL)), _full_spec((1, D_MODEL)),
                  _full_spec((D_MODEL, ODD_IN))],
        out_specs=[_row_spec(ts, D_MODEL), _full_spec((D_MODEL, ODD_IN), single=False), _full_spec((8, D_MODEL), single=False)],
        out_shape=[_sds((S, D_MODEL), F32), _sds((D_MODEL, ODD_IN), F32), _sds((8, D_MODEL), F32)],
        compiler_params=_params(("arbitrary",)),
    )(dq, dk, dv, dgc, h, x, dx_res, mod, nw, wio)


def _even_post_bwd(dx1, y, oa, olat, proj, gate, wuv, woe):
    S = dx1.shape[0]
    ts = min(ROW_TILE, S)

    def body(dx_ref, y_ref, oa_ref, ol_ref, proj_ref, gate_ref, wuv_ref, woe_ref,
             doa_ref, dga_ref, dgb_ref, dol_ref, dwoe_ref, dwuv_ref, stats_ref):
        @pl.when(pl.program_id(0) == 0)
        def _():
            dwoe_ref[...] = jnp.zeros((D_MODEL, D_MODEL), F32)
            dwuv_ref[...] = jnp.zeros((1024, 512), F32)
            stats_ref[...] = jnp.zeros((8, D_MODEL), F32)

        dxv = dx_ref[...]
        stats_ref[0:1, :] += _sum_rows(dxv * y_ref[...])
        dy = (gate_ref[...] * dxv).astype(BF16)
        sa, dsa = _silu_and_grad(proj_ref[:, 768:1280])
        sb, dsb = _silu_and_grad(proj_ref[:, 1792:2304])
        olc = jnp.concatenate([ol_ref[hh] for hh in range(B_HEADS)], axis=1).astype(BF16)
        wuv_v = wuv_ref[...]
        ob = _dot(olc, wuv_v)
        oav = oa_ref[...]
        mix = jnp.concatenate([oav * sa, ob * sb], axis=1).astype(BF16)
        dmix = _dot_nt(dy, woe_ref[...])
        dwoe_ref[...] += _dot_tn(mix, dy)
        dma, dmb = dmix[:, 0:512], dmix[:, 512:1024]
        doa_ref[...] = dma * sa
        dga_ref[...] = dma * oav * dsa
        dgb_ref[...] = dmb * ob * dsb
        dob = (dmb * sb).astype(BF16)
        dol = _dot_nt(dob, wuv_v)
        dwuv_ref[...] += _dot_tn(olc, dob)
        for hh in range(B_HEADS):
            dol_ref[hh] = dol[:, LANES * hh:LANES * (hh + 1)]

    head_spec = pl.BlockSpec((B_HEADS, ts, LANES), lambda i: (0, i, 0))
    return pl.pallas_call(
        body, name="even_post_bwd", grid=(S // ts,),
        in_specs=[_row_spec(ts, D_MODEL), _row_spec(ts, D_MODEL), _row_spec(ts, 512), head_spec, _row_spec(ts, EVEN_P),
                  _full_spec((1, D_MODEL)), _full_spec((1024, 512)), _full_spec((1024, D_MODEL))],
        out_specs=[_row_spec(ts, 512), _row_spec(ts, 512), _row_spec(ts, 512), head_spec,
                   _full_spec((D_MODEL, D_MODEL), single=False), _full_spec((1024, 512), single=False),
                   _full_spec((8, D_MODEL), single=False)],
        out_shape=[_sds((S, 512), F32), _sds((S, 512), F32), _sds((S, 512), F32), _sds((B_HEADS, S, LANES), F32),
                   _sds((D_MODEL, D_MODEL), F32), _sds((1024, 512), F32), _sds((8, D_MODEL), F32)],
        compiler_params=_params(("arbitrary",)),
    )(dx1, y, oa, olat, proj, gate, wuv, woe)


EVEN_BWD_ROW_TILE = 128


def _even_pre_bwd(x, h, proj, dqa, dka, dva, dga, dgb, dqcat, dkcat, dx_res, mod, nw, wie, qn, kn, seg, ca, sa, ct, st,
                  qln, kvln, wuq, wuk):
    S = x.shape[0]
    ts = min(EVEN_BWD_ROW_TILE, S)

    def body(x_ref, h_ref, proj_ref, dqa_ref, dka_ref, dva_ref, dga_ref, dgb_ref, dqc_ref, dkc_ref, dxr_ref, mod_ref, nw_ref,
             wie_ref, qn_ref, kn_ref, seg_ref, ca_ref, sa_ref, ct_ref, st_ref, qln_ref, kvln_ref, wuq_ref, wuk_ref,
             dx_ref, dwie_ref, dwuq_ref, dwuk_ref, stats_ref, nstats_ref):
        @pl.when(pl.program_id(0) == 0)
        def _():
            dwie_ref[...] = jnp.zeros((D_MODEL, EVEN_P), F32)
            dwuq_ref[...] = jnp.zeros((B_Q_LORA, 1536), F32)
            dwuk_ref[...] = jnp.zeros((512, 1024), F32)
            stats_ref[...] = jnp.zeros((8, D_MODEL), F32)
            nstats_ref[...] = jnp.zeros((8, 256), F32)

        lane = _lane_iota()
        ca_v, sa_v, ct_v, st_v = ca_ref[...], sa_ref[...], ct_ref[...], st_ref[...]
        seg_v = seg_ref[...]

        def head_norm_bwd(xc, dy, w):
            r = lax.rsqrt(_seg_mean(xc * xc, seg_v) + EPS)
            g = dy * w
            dxc = r * g - xc * (r * r * r) * _seg_mean(xc * g, seg_v)
            return dxc, _sum_rows(dy * (xc * r))

        pieces = []
        dqn = jnp.zeros((1, LANES), F32)
        for cb in range(4):
            sl = slice(LANES * cb, LANES * (cb + 1))
            dy = _rot_bwd(dqa_ref[:, sl] * 0.125, ca_v, sa_v, lane)
            dxc, dw = head_norm_bwd(proj_ref[:, sl], dy, qn_ref[...])
            pieces.append(dxc)
            dqn = dqn + dw
        dxc, dkn = head_norm_bwd(proj_ref[:, 512:640], _rot_bwd(dka_ref[...], ca_v, sa_v, lane), kn_ref[...])
        pieces += [dxc, dva_ref[...], dga_ref[...]]
        nstats_ref[0:1, 0:LANES] += dqn + pltpu.roll(dqn, HEAD_DIM, 1)
        nstats_ref[1:2, 0:LANES] += dkn + pltpu.roll(dkn, HEAD_DIM, 1)

        cq = proj_ref[:, 1280:1536]
        rq = _rms(cq)
        cqn_f = cq * rq
        qln_v = qln_ref[...]
        cqn = (cqn_f * qln_v).astype(BF16)
        wuq_v, wuk_v = wuq_ref[...], wuk_ref[...]
        qnope = _dot(cqn, wuq_v[:, 0:512]).astype(BF16)
        dqlat = jnp.concatenate([dqc_ref[hh, :, 0:LANES] for hh in range(B_HEADS)], axis=1).astype(BF16)
        dqnope = _dot_nt(dqlat, wuk_v)
        dwuk_ref[...] += _dot_tn(qnope, dqlat)
        dqr = [_rot_bwd(dqc_ref[hh, :, LANES:2 * LANES], ct_v, st_v, lane) for hh in range(B_HEADS)]
        dqb = jnp.concatenate([dqnope] + dqr, axis=1).astype(BF16)
        dcqn = _dot_nt(dqb, wuq_v)
        dwuq_ref[...] += _dot_tn(cqn, dqb)
        nstats_ref[2:3, :] += _sum_rows(dcqn * cqn_f)
        dcq = _rms_bwd(cq, rq, dcqn * qln_v)
        ckv = proj_ref[:, 1536:1664]
        rk = _rms(ckv)
        dckvn = dkc_ref[:, 0:LANES]
        nstats_ref[3:4, 0:LANES] += _sum_rows(dckvn * (ckv * rk))
        dckv = _rms_bwd(ckv, rk, dckvn * kvln_ref[...])
        dkr = _rot_bwd(dkc_ref[:, LANES:2 * LANES], ct_v, st_v, lane)
        pieces += [dcq, dckv, dkr, dgb_ref[...]]
        dproj = jnp.concatenate(pieces, axis=1).astype(BF16)
        dh = _dot_nt(dproj, wie_ref[...])
        dwie_ref[...] += _dot_tn(h_ref[...], dproj)
        dx_ref[...] = dxr_ref[...] + _norm_mod_bwd(dh, x_ref[...], mod_ref, nw_ref, stats_ref)

    return pl.pallas_call(
        body, name="even_pre_bwd", grid=(S // ts,),
        in_specs=[_row_spec(ts, D_MODEL), _row_spec(ts, D_MODEL), _row_spec(ts, EVEN_P), _row_spec(ts, 512), _row_spec(ts, LANES),
                  _row_spec(ts, LANES), _row_spec(ts, 512), _row_spec(ts, 512),
                  pl.BlockSpec((B_HEADS, ts, 2 * LANES), lambda i: (0, i, 0)), _row_spec(ts, 2 * LANES), _row_spec(ts, D_MODEL),
                  _full_spec((3, D_MODEL)), _full_spec((1, D_MODEL)), _full_spec((D_MODEL, EVEN_P)),
                  _full_spec((1, LANES)), _full_spec((1, LANES)), _full_spec((LANES, LANES)),
                  _row_spec(ts, LANES), _row_spec(ts, LANES), _row_spec(ts, LANES), _row_spec(ts, LANES),
                  _full_spec((1, B_Q_LORA)), _full_spec((1, B_KV_LORA)), _full_spec((B_Q_LORA, 1536)), _full_spec((512, 1024))],
        out_specs=[_row_spec(ts, D_MODEL), _full_spec((D_MODEL, EVEN_P), single=False), _full_spec((B_Q_LORA, 1536), single=False),
                   _full_spec((512, 1024), single=False), _full_spec((8, D_MODEL), single=False), _full_spec((8, 256), single=False)],
        out_shape=[_sds((S, D_MODEL), F32), _sds((D_MODEL, EVEN_P), F32), _sds((B_Q_LORA, 1536), F32), _sds((512, 1024), F32),
                   _sds((8, D_MODEL), F32), _sds((8, 256), F32)],
        compiler_params=_params(("arbitrary",)),
    )(x, h, proj, dqa, dka, dva, dga, dgb, dqcat, dkcat, dx_res, mod, nw, wie, qn, kn, seg, ca, sa, ct, st, qln, kvln, wuq, wuk)


def _ada_fwd(c_all, w, b):
    n = w.shape[2]

    def body(c_ref, w_ref, b_ref, o_ref):
        cv = c_ref[...]
        o_ref[0] = _dot_f32(cv * _sigmoid(cv), w_ref[0]) + b_ref[0]

    return pl.pallas_call(
        body, name="ada_fwd", grid=(2,),
        in_specs=[pl.BlockSpec((N_DEV, D_MODEL), lambda l: (0, 0)), pl.BlockSpec((1, D_MODEL, n), lambda l: (l, 0, 0)),
                  pl.BlockSpec((1, 1, n), lambda l: (l, 0, 0))],
        out_specs=pl.BlockSpec((1, N_DEV, n), lambda l: (l, 0, 0)),
        out_shape=_sds((2, N_DEV, n), F32),
        compiler_params=_params(("arbitrary",)),
    )(c_all, w, b)


def _ada_bwd(c_all_t, dmod):
    n = dmod.shape[2]

    def body(c_ref, d_ref, o_ref):
        cv = c_ref[...]
        act = cv * _sigmoid(cv)
        dv = d_ref[0]
        acc = act[:, 0:1] * dv[0:1, :]
        for bb in range(1, N_DEV):
            acc = acc + act[:, bb:bb + 1] * dv[bb:bb + 1, :]
        o_ref[0] = acc

    return pl.pallas_call(
        body, name="ada_bwd", grid=(2,),
        in_specs=[pl.BlockSpec((D_MODEL, N_DEV), lambda l: (0, 0)), pl.BlockSpec((1, N_DEV, n), lambda l: (l, 0, 0))],
        out_specs=pl.BlockSpec((1, D_MODEL, n), lambda l: (l, 0, 0)),
        out_shape=_sds((2, D_MODEL, n), F32),
        compiler_params=_params(("arbitrary",)),
    )(c_all_t, dmod)


ADAM_ROW_TILE = 256


def _adam(parts, w, m, v, name):
    P, R, C = parts.shape
    tr = R if R <= ADAM_ROW_TILE else ADAM_ROW_TILE
    assert R % tr == 0

    def body(p_ref, w_ref, m_ref, v_ref, g_ref, d_ref, nm_ref, nv_ref):
        g = p_ref[0]
        for k in range(1, P):
            g = g + p_ref[k]
        g_ref[...] = g
        m_new = ADAM_B1 * m_ref[...] + (1.0 - ADAM_B1) * g
        v_new = ADAM_B2 * v_ref[...] + (1.0 - ADAM_B2) * jnp.square(g)
        m_hat = m_new / (1.0 - ADAM_B1 ** ADAM_STEP)
        v_hat = v_new / (1.0 - ADAM_B2 ** ADAM_STEP)
        d_ref[...] = -ADAM_LR * (m_hat / (jnp.sqrt(v_hat) + ADAM_EPS) + ADAM_WD * w_ref[...])
        nm_ref[...] = m_new
        nv_ref[...] = v_new

    spec = pl.BlockSpec((tr, C), lambda i: (i, 0))
    return pl.pallas_call(
        body, name=name, grid=(R // tr,),
        in_specs=[pl.BlockSpec((P, tr, C), lambda i: (0, i, 0)), spec, spec, spec],
        out_specs=[spec, spec, spec, spec], out_shape=[_sds((R, C), F32)] * 4,
        compiler_params=_params(("arbitrary",)),
    )(parts, w, m, v)


_ANY = pl.BlockSpec(memory_space=pl.ANY)
CHIP_FLIPS = ((1, 0), (0, 1), (1, 1))
DEV_FLIPS = tuple((dx, dy, dc) for dx in (0, 1) for dy in (0, 1) for dc in (0, 1) if dx + dy + dc)


def _flip(a, d):
    return a if d == 0 else 1 - a


def _my_place():
    return lax.axis_index("x"), lax.axis_index("y"), lax.axis_index("c")


def _gather_dev8(arrs, name):
    n = len(arrs)

    def body(*refs):
        ins, outs = refs[:n], refs[n:2 * n]
        send_sems, recv_sems, loc_sems = refs[2 * n:]
        x, y, c = _my_place()
        me = 4 * x + 2 * y + c
        copies = []
        for a in range(n):
            loc = pltpu.make_async_copy(ins[a], outs[a].at[me], loc_sems.at[a])
            loc.start()
            copies.append(loc)
            for k, (dx, dy, dc) in enumerate(DEV_FLIPS):
                cp = pltpu.make_async_remote_copy(
                    src_ref=ins[a], dst_ref=outs[a].at[me], send_sem=send_sems.at[a, k], recv_sem=recv_sems.at[a, k],
                    device_id=(_flip(x, dx), _flip(y, dy), _flip(c, dc)), device_id_type=MESH_ID)
                cp.start()
                copies.append(cp)
        for cp in copies:
            cp.wait()

    return pl.pallas_call(
        body, name=name, in_specs=[_ANY] * n, out_specs=[_ANY] * n,
        out_shape=[_sds((N_DEV,) + a.shape, a.dtype) for a in arrs],
        scratch_shapes=[pltpu.SemaphoreType.DMA((n, 7)), pltpu.SemaphoreType.DMA((n, 7)), pltpu.SemaphoreType.DMA((n,))],
    )(*arrs)


def _gather_chip4(arrs, name):
    n = len(arrs)

    def body(*refs):
        ins, outs = refs[:n], refs[n:2 * n]
        send_sems, recv_sems, loc_sems = refs[2 * n:]
        x, y, c = _my_place()
        me = 2 * x + y
        copies = []
        for a in range(n):
            loc = pltpu.make_async_copy(ins[a], outs[a].at[me], loc_sems.at[a])
            loc.start()
            copies.append(loc)
            for k, (dx, dy) in enumerate(CHIP_FLIPS):
                cp = pltpu.make_async_remote_copy(
                    src_ref=ins[a], dst_ref=outs[a].at[me], send_sem=send_sems.at[a, k], recv_sem=recv_sems.at[a, k],
                    device_id=(_flip(x, dx), _flip(y, dy), c), device_id_type=MESH_ID)
                cp.start()
                copies.append(cp)
        for cp in copies:
            cp.wait()

    return pl.pallas_call(
        body, name=name, in_specs=[_ANY] * n, out_specs=[_ANY] * n,
        out_shape=[_sds((N_CHIPS,) + a.shape, a.dtype) for a in arrs],
        scratch_shapes=[pltpu.SemaphoreType.DMA((n, 3)), pltpu.SemaphoreType.DMA((n, 3)), pltpu.SemaphoreType.DMA((n,))],
    )(*arrs)


def _reduce_exchange(arrs, name):
    n = len(arrs)

    def body(*refs):
        ins, outs = refs[:n], refs[n:2 * n]
        send_sems, recv_sems, loc_sems = refs[2 * n:]
        x, y, c = _my_place()
        chip = 2 * x + y
        sibling = (x, y, 1 - c)

        def remote(src, slot, a, k, to):
            return pltpu.make_async_remote_copy(src_ref=src, dst_ref=outs[a].at[slot], send_sem=send_sems.at[a, k],
                                                recv_sem=recv_sems.at[a, k], device_id=to, device_id_type=MESH_ID)

        sends, locs = [], []
        for a in range(n):
            loc = pltpu.make_async_copy(ins[a].at[chip], outs[a].at[2 * chip + c], loc_sems.at[a])
            loc.start()
            locs.append(loc)
            first = [remote(ins[a].at[chip], 2 * chip + c, a, 0, sibling)]
            for k, (dx, dy) in enumerate(CHIP_FLIPS):
                px, py = _flip(x, dx), _flip(y, dy)
                first.append(remote(ins[a].at[2 * px + py], 2 * chip + c, a, 1 + k, (px, py, c)))
            for cp in first:
                cp.start()
            sends += first
        for a in range(n):
            for k, (dx, dy) in enumerate(CHIP_FLIPS):
                slot = 2 * (2 * _flip(x, dx) + _flip(y, dy)) + c
                remote(outs[a].at[slot], slot, a, 1 + k, sibling).wait_recv()
                fwd = remote(outs[a].at[slot], slot, a, 4 + k, sibling)
                fwd.start()
                sends.append(fwd)
        for a in range(n):
            remote(outs[a].at[2 * chip + 1 - c], 2 * chip + 1 - c, a, 0, sibling).wait_recv()
            for k, (dx, dy) in enumerate(CHIP_FLIPS):
                slot = 2 * (2 * _flip(x, dx) + _flip(y, dy)) + 1 - c
                remote(outs[a].at[slot], slot, a, 4 + k, sibling).wait_recv()
        for cp in sends:
            cp.wait_send()
        for loc in locs:
            loc.wait()

    return pl.pallas_call(
        body, name=name, in_specs=[_ANY] * n, out_specs=[_ANY] * n,
        out_shape=[_sds((N_DEV,) + a.shape[1:], a.dtype) for a in arrs],
        scratch_shapes=[pltpu.SemaphoreType.DMA((n, 7)), pltpu.SemaphoreType.DMA((n, 7)), pltpu.SemaphoreType.DMA((n,))],
    )(*arrs)


def _pair_order(nheads, nkv):
    group = nheads // nkv
    order = []
    for m in range(nkv // 2):
        for i in range(group):
            order += [2 * m * group + i, (2 * m + 1) * group + i]
    return order


A_ORDER = _pair_order(A_HEADS, A_KV_HEADS)
C_ORDER = _pair_order(C_HEADS, C_KV_HEADS)
A_INV = [int(k) for k in np.argsort(A_ORDER)]
C_INV = [int(k) for k in np.argsort(C_ORDER)]


def _perm_heads(w, order, axis):
    return jnp.concatenate([lax.slice_in_dim(w, HEAD_DIM * h, HEAD_DIM * (h + 1), axis=axis) for h in order], axis=axis)


def _even_in_layout(w):
    return jnp.concatenate([_perm_heads(w[:, 0:512], A_ORDER, 1), w[:, 512:768], _perm_heads(w[:, 768:1280], A_ORDER, 1),
                            w[:, 1280:1696], jnp.zeros((w.shape[0], 96), w.dtype), w[:, 1696:2208]], axis=1)


def _even_in_unlayout(g):
    return jnp.concatenate([_perm_heads(g[:, 0:512], A_INV, 1), g[:, 512:768], _perm_heads(g[:, 768:1280], A_INV, 1),
                            g[:, 1280:1696], g[:, 1792:2304]], axis=1)


def _even_out_layout(w):
    return jnp.concatenate([_perm_heads(w[0:512], A_ORDER, 0), w[512:1024]], axis=0)


def _even_out_unlayout(g):
    return jnp.concatenate([_perm_heads(g[0:512], A_INV, 0), g[512:1024]], axis=0)


def _odd_in_layout(w):
    return jnp.concatenate([_perm_heads(w[:, 0:1024], C_ORDER, 1), w[:, 1024:1536], _perm_heads(w[:, 1536:2560], C_ORDER, 1)],
                           axis=1)


def _odd_in_unlayout(g):
    return jnp.concatenate([_perm_heads(g[:, 0:1024], C_INV, 1), g[:, 1024:1536], _perm_heads(g[:, 1536:2560], C_INV, 1)],
                           axis=1)


def _uq_layout(w):
    per = B_NOPE + B_ROPE
    pad = jnp.zeros((w.shape[0], LANES - B_ROPE), w.dtype)
    nope = [w[:, per * h:per * h + B_NOPE] for h in range(B_HEADS)]
    rope = [jnp.concatenate([w[:, per * h + B_NOPE:per * (h + 1)], pad], axis=1) for h in range(B_HEADS)]
    return jnp.concatenate(nope + rope, axis=1)


def _uq_unlayout(g):
    parts = []
    for h in range(B_HEADS):
        parts += [g[:, B_NOPE * h:B_NOPE * (h + 1)], g[:, 512 + LANES * h:512 + LANES * h + B_ROPE]]
    return jnp.concatenate(parts, axis=1)


def _block_diag(blocks):
    rows = []
    for h, blk in enumerate(blocks):
        r, cdim = blk.shape
        n = len(blocks)
        rows.append(jnp.concatenate([jnp.zeros((r, cdim * h), blk.dtype), blk, jnp.zeros((r, cdim * (n - 1 - h)), blk.dtype)],
                                    axis=1))
    return jnp.concatenate(rows, axis=0)


def _uk_layout(w):
    return _block_diag([w[:, h, :].T for h in range(B_HEADS)])


def _uk_unlayout(g):
    return jnp.stack([g[B_NOPE * h:B_NOPE * (h + 1), LANES * h:LANES * (h + 1)].T for h in range(B_HEADS)], axis=1)


def _uv_layout(w):
    return _block_diag([w[:, h, :] for h in range(B_HEADS)])


def _uv_unlayout(g):
    return jnp.stack([g[LANES * h:LANES * (h + 1), B_V * h:B_V * (h + 1)] for h in range(B_HEADS)], axis=1)


def _rope_tables(S):
    inv = ROPE_THETA ** (-jnp.arange(0, 32, 2, dtype=F32) / 32)
    tok = jnp.arange(S)

    def tab(pos):
        ang = pos.astype(F32)[:, None] * inv[None, :]
        cos, sin = jnp.cos(ang), jnp.sin(ang)
        return jnp.concatenate([cos, cos], axis=1), jnp.concatenate([-sin, sin], axis=1)

    cr, sr = tab(tok // GRID_W)
    cc, sc = tab(tok % GRID_W)
    ct, st = tab(tok)
    return (jnp.tile(jnp.concatenate([cr, cc], axis=1), (1, 2)), jnp.tile(jnp.concatenate([sr, sc], axis=1), (1, 2)),
            jnp.tile(ct, (1, 4)), jnp.tile(st, (1, 4)))


A_TQ, A_TK = 256, 512
B_TQ, B_TK = 128, 512
C_T = 256


def _local_step(x0, tgt, mod, norm_w, wie, wuq, wuk, wuv, woe, wio, woo, a_q_norm, a_k_norm, q_lora_norm, kv_lora_norm,
                c_sink, final_norm):
    S = x0.shape[0]
    mod3 = mod.reshape(2, 3, D_MODEL)
    ca, sa, ct, st = _rope_tables(S)
    lane_seg = np.arange(LANES) // HEAD_DIM
    seg = jnp.asarray((lane_seg[:, None] == lane_seg[None, :]).astype(np.float32))
    qn = jnp.tile(a_q_norm.reshape(1, HEAD_DIM), (1, 2))
    kn = jnp.tile(a_k_norm.reshape(1, HEAD_DIM), (1, 2))
    qln, kvln = q_lora_norm.reshape(1, B_Q_LORA), kv_lora_norm.reshape(1, B_KV_LORA)
    nw0, nw1 = norm_w[0:1], norm_w[1:2]
    gate0, gate1 = mod3[0, 2:3], mod3[1, 2:3]
    a_tq, a_tk, b_tq, b_tk, c_t = min(A_TQ, S), min(A_TK, S), min(B_TQ, S), min(B_TK, S), min(C_T, S)

    h0, proj_e, qa, ka, va, qcat, kcat = _even_pre_fwd(x0, mod3[0], nw0, wie, qn, kn, seg, ca, sa, ct, st, qln, kvln, wuq, wuk)
    oa, lse_a = _pp_fwd(qa, ka, va, kdiv=4, tq=a_tq, tk=a_tk, name="attn_a_fwd")
    olat, lse_b = _mla_fwd(qcat, kcat, tq=b_tq, tk=b_tk)
    y0, x1 = _even_post_fwd(oa, olat, proj_e, x0, gate0, wuv, woe)
    h1, proj_o, qc, kc, vc = _odd_pre_fwd(x1, mod3[1], nw1, wio)
    slopes = 2.0 ** (-8.0 * jnp.arange(1, C_HEADS + 1, dtype=F32) / C_HEADS)
    c_order = np.asarray(C_ORDER)
    slope_rows = jnp.repeat(slopes[c_order].reshape(C_HEADS // 2, 2), c_t, axis=1)[:, :, None]
    sink_rows = jnp.repeat(c_sink.reshape(C_HEADS)[c_order].reshape(C_HEADS // 2, 2), c_t, axis=1)[:, :, None]
    oc, lse_c = _pp_fwd(qc, kc, vc, kdiv=4, tq=c_t, tk=c_t, name="attn_c_fwd", slope=slope_rows, sink=sink_rows)
    doc, dgc, dx2, dwoo, st_f = _odd_post(oc, proj_o, x1, gate1, woo, final_norm.reshape(1, D_MODEL), tgt)
    dqc, dkc, dvc, dsink_raw = _pp_bwd(qc, kc, vc, oc, doc, lse_c, kdiv=4, tq=c_t, tk=c_t, name="attn_c_bwd",
                                       slope=slope_rows, sink=sink_rows)
    dx1, dwio, st_1 = _odd_pre_bwd(dqc, dkc, dvc, dgc, h1, x1, dx2, mod3[1], nw1, wio)
    doa, dga, dgb, dolat, dwoe, dwuv, st_e = _even_post_bwd(dx1, y0, oa, olat, proj_e, gate0, wuv, woe)
    dqa, dka, dva = _pp_bwd(qa, ka, va, oa, doa, lse_a, kdiv=4, tq=a_tq, tk=a_tk, name="attn_a_bwd")
    dqcat, dkcat = _mla_bwd(qcat, kcat, olat, dolat, lse_b, tq=b_tq, tk=b_tk)
    dx0, dwie, dwuq, dwuk, st_0, nst = _even_pre_bwd(x0, h0, proj_e, dqa, dka, dva, dga, dgb, dqcat, dkcat, dx1, mod3[0], nw0,
                                                     wie, qn, kn, seg, ca, sa, ct, st, qln, kvln, wuq, wuk)
    dsink_pairs = jnp.stack([dsink_raw[:, 0, 0], dsink_raw[:, 1, 0]], axis=1).reshape(C_HEADS)
    return dict(
        loss=st_f[2, 0], dx=dx0,
        dmod=jnp.stack([jnp.concatenate([st_0[0], st_0[1], st_e[0]]), jnp.concatenate([st_1[0], st_1[1], st_f[1]])]),
        norm_w=jnp.stack([st_0[2], st_1[2]]), final_norm=st_f[0],
        a_q_norm=nst[0:1, 0:HEAD_DIM], a_k_norm=nst[1:2, 0:HEAD_DIM], b_q_lora_norm=nst[2:3, :], b_kv_lora_norm=nst[3:4, 0:LANES],
        c_sink=dsink_pairs[np.asarray(C_INV)].reshape(1, C_HEADS),
        even_w_in=dwie, b_w_uq=dwuq, b_w_uk=dwuk, b_w_uv=dwuv, even_w_out=dwoe, odd_w_in=dwio, odd_w_out=dwoo)


WEIGHT_NAMES = ("norm_w", "ada_w", "ada_b", "even_w_in", "a_q_norm", "a_k_norm", "b_q_lora_norm", "b_kv_lora_norm", "b_w_uq",
                "b_w_uk", "b_w_uv", "even_w_out", "odd_w_in", "c_sink", "odd_w_out", "final_norm")
SMALL_NAMES = ("dmod", "norm_w", "final_norm", "a_q_norm", "a_k_norm", "b_q_lora_norm", "b_kv_lora_norm", "c_sink")


def _cols_to_chips(g):
    r, n4 = g.shape
    return jnp.transpose(g.reshape(r, N_CHIPS, n4 // N_CHIPS), (1, 0, 2))


def _chips_to_cols(g):
    p, r, n = g.shape
    return jnp.transpose(g, (1, 0, 2)).reshape(r, p * n)


def kernel(x, c, norm_w, ada_w, ada_b, even_w_in, a_q_norm, a_k_norm, b_q_lora_norm, b_kv_lora_norm, b_w_uq, b_w_uk, b_w_uv, even_w_out, odd_w_in, c_sink, odd_w_out, final_norm, loss_target, m_norm_w, m_ada_w, m_ada_b, m_even_w_in, m_a_q_norm, m_a_k_norm, m_b_q_lora_norm, m_b_kv_lora_norm, m_b_w_uq, m_b_w_uk, m_b_w_uv, m_even_w_out, m_odd_w_in, m_c_sink, m_odd_w_out, m_final_norm, v_norm_w, v_ada_w, v_ada_b, v_even_w_in, v_a_q_norm, v_a_k_norm, v_b_q_lora_norm, v_b_kv_lora_norm, v_b_w_uq, v_b_w_uk, v_b_w_uv, v_even_w_out, v_odd_w_in, v_c_sink, v_odd_w_out, v_final_norm):
    given = dict(locals())
    xi, yi, ci = _my_place()
    chip = 2 * xi + yi
    dev = 2 * chip + ci
    n_ada = ada_w.shape[2]

    (c_all,) = _gather_dev8([c], "gather_c")
    c_all = c_all.reshape(N_DEV, D_MODEL)
    bias = lax.dynamic_slice_in_dim(ada_b, chip * n_ada, n_ada, axis=1).reshape(2, 1, n_ada)
    mod_cols = _ada_fwd(c_all, ada_w, bias)
    mod_all, wie_g, wuq_g, woe_g, wio_g, woo_g = _gather_chip4(
        [mod_cols, even_w_in[0].astype(BF16), b_w_uq[0].astype(BF16), even_w_out[0].astype(BF16), odd_w_in[0].astype(BF16),
         odd_w_out[0].astype(BF16)], "gather_weights")
    mod = jnp.transpose(lax.dynamic_index_in_dim(mod_all, dev, axis=2, keepdims=False), (1, 0, 2)).reshape(2, 3 * D_MODEL)

    res = _local_step(
        x[0], loss_target[0], mod, norm_w,
        _even_in_layout(_chips_to_cols(wie_g)), _uq_layout(_chips_to_cols(wuq_g)), _uk_layout(b_w_uk[0].astype(BF16)),
        _uv_layout(b_w_uv[0].astype(BF16)), _even_out_layout(woe_g.reshape(D_MODEL, D_MODEL)), _odd_in_layout(_chips_to_cols(wio_g)),
        _perm_heads(woo_g.reshape(D_MODEL, D_MODEL), C_ORDER, 0), a_q_norm, a_k_norm, b_q_lora_norm, b_kv_lora_norm, c_sink,
        final_norm)

    shard_parts = dict(zip(
        ("even_w_in", "b_w_uq", "even_w_out", "odd_w_in", "odd_w_out"),
        _reduce_exchange(
            [_cols_to_chips(_even_in_unlayout(res["even_w_in"])), _cols_to_chips(_uq_unlayout(res["b_w_uq"])),
             _even_out_unlayout(res["even_w_out"]).reshape(N_CHIPS, D_MODEL // N_CHIPS, D_MODEL),
             _cols_to_chips(_odd_in_unlayout(res["odd_w_in"])),
             _perm_heads(res["odd_w_out"], C_INV, 0).reshape(N_CHIPS, D_MODEL // N_CHIPS, D_MODEL)], "reduce_exchange")))

    small = jnp.concatenate([res[k].reshape(-1) for k in SMALL_NAMES]).reshape(1, -1)
    latent = jnp.stack([_uk_unlayout(res["b_w_uk"]).reshape(B_KV_LORA, 512), _uv_unlayout(res["b_w_uv"]).reshape(B_KV_LORA, 512)])
    small_all, latent_all = _gather_dev8([small, latent], "gather_small")
    small_all = small_all.reshape(N_DEV, -1)
    parts, off = {}, 0
    for k in SMALL_NAMES:
        n = int(np.prod(res[k].shape))
        parts[k] = small_all[:, off:off + n]
        off += n
    dmod_all = parts.pop("dmod").reshape(N_DEV, 2, 3 * D_MODEL)
    dmod_cols = jnp.transpose(lax.dynamic_slice_in_dim(dmod_all, chip * n_ada, n_ada, axis=2), (1, 0, 2))
    parts["ada_w"] = _ada_bwd(c_all.T, dmod_cols).reshape(1, 2 * D_MODEL, n_ada)
    parts["ada_b"] = dmod_all
    parts["b_w_uk"], parts["b_w_uv"] = latent_all[:, 0], latent_all[:, 1]
    parts.update(shard_parts)

    grads, deltas, new_m, new_v = [], [], [], []
    for k in WEIGHT_NAMES:
        w = given[k]
        p = parts[k]
        shape2 = (p.shape[-2], p.shape[-1]) if p.ndim == 3 else (1, p.shape[-1])
        p = p.reshape((p.shape[0],) + shape2)
        outs = _adam(p, w.reshape(shape2), given["m_" + k].reshape(shape2), given["v_" + k].reshape(shape2), "adam_" + k)
        for lst, o in zip((grads, deltas, new_m, new_v), outs):
            lst.append(o.reshape(w.shape))
    loss = lax.psum(res["loss"], ("x", "y", "c"))
    return (loss, res["dx"][None], *grads, *deltas, *new_m, *new_v)
```

```python
import functools

import numpy as np
import jax
import jax.numpy as jnp
from jax import lax
from jax.experimental import pallas as pl
from jax.experimental.pallas import tpu as pltpu

F32 = jnp.float32
BF16 = jnp.bfloat16
HIGHEST = lax.Precision.HIGHEST
MESH_ID = pl.DeviceIdType.MESH

D_MODEL = 1024
HEAD_DIM = 64
GRID_W = 64
EPS = 1e-6
ROPE_THETA = 10000.0
A_HEADS, A_KV_HEADS = 8, 2
B_HEADS, B_NOPE, B_ROPE, B_V = 8, 64, 32, 64
B_Q_LORA, B_KV_LORA = 256, 128
C_HEADS, C_KV_HEADS = 16, 4
WINDOW = 128
EVEN_IN, ODD_IN = 2208, 2560
EVEN_P = 2304
N_CHIPS, N_DEV = 4, 8
LANES = 128
NEG = -1e30
VMEM_LIMIT = 60 * 1024 * 1024

ADAM_LR, ADAM_B1, ADAM_B2, ADAM_EPS, ADAM_WD, ADAM_STEP = 0.001, 0.9, 0.999, 1e-08, 0.01, 10

ROW_TILE = 256


def _dot(a, b):
    return lax.dot_general(a, b, (((1,), (0,)), ((), ())), preferred_element_type=F32)


def _dot_nt(a, b):
    return lax.dot_general(a, b, (((1,), (1,)), ((), ())), preferred_element_type=F32)


def _dot_tn(a, b):
    return lax.dot_general(a, b, (((0,), (0,)), ((), ())), preferred_element_type=F32)


def _dot_f32(a, b):
    return lax.dot_general(a, b, (((1,), (0,)), ((), ())), precision=HIGHEST, preferred_element_type=F32)


def _sigmoid(x):
    return 1.0 / (1.0 + jnp.exp(-x))


def _silu_and_grad(g):
    s = _sigmoid(g)
    return g * s, s * (1.0 + g * (1.0 - s))


def _lane_iota():
    return lax.broadcasted_iota(jnp.int32, (1, LANES), 1)


def _partner(x, lane):
    return jnp.where((lane % 32) < 16, pltpu.roll(x, LANES - 16, 1), pltpu.roll(x, 16, 1))


def _rot(x, cos, sin_signed, lane):
    return x * cos + _partner(x, lane) * sin_signed


def _rot_bwd(dy, cos, sin_signed, lane):
    return dy * cos + _partner(dy * sin_signed, lane)


def _rms(x):
    return lax.rsqrt(jnp.mean(x * x, axis=-1, keepdims=True) + EPS)


def _rms_bwd(x, r, g):
    return r * g - x * (r * r * r) * jnp.mean(x * g, axis=-1, keepdims=True)


def _seg_mean(v, seg_ones):
    return _dot_f32(v, seg_ones) * (1.0 / HEAD_DIM)


def _row_spec(ts, cols):
    return pl.BlockSpec((ts, cols), lambda i: (i, 0))


def _full_spec(shape, single=True):
    nd = len(shape)
    if single:
        return pl.BlockSpec(shape, lambda i: (0,) * nd, pipeline_mode=pl.Buffered(1))
    return pl.BlockSpec(shape, lambda i: (0,) * nd)


def _sds(shape, dtype):
    return jax.ShapeDtypeStruct(shape, dtype)


def _params(sem):
    return pltpu.CompilerParams(dimension_semantics=sem, vmem_limit_bytes=VMEM_LIMIT)


def _even_pre_fwd(x, mod, nw, wie, qn, kn, seg, ca, sa, ct, st, qln, kvln, wuq, wuk):
    S = x.shape[0]
    ts = min(ROW_TILE, S)

    def body(x_ref, mod_ref, nw_ref, wie_ref, qn_ref, kn_ref, seg_ref, ca_ref, sa_ref, ct_ref, st_ref, qln_ref,
             kvln_ref, wuq_ref, wuk_ref, h_ref, proj_ref, qa_ref, ka_ref, va_ref, qcat_ref, kcat_ref):
        xv = x_ref[...]
        h = (xv * _rms(xv) * nw_ref[...]) * (1.0 + mod_ref[1:2, :]) + mod_ref[0:1, :]
        hb = h.astype(BF16)
        h_ref[...] = hb
        proj = _dot(hb, wie_ref[...])
        proj_ref[...] = proj
        lane = _lane_iota()
        ca_v, sa_v, ct_v, st_v = ca_ref[...], sa_ref[...], ct_ref[...], st_ref[...]
        seg_v = seg_ref[...]
        for cb in range(4):
            xc = proj[:, LANES * cb:LANES * (cb + 1)]
            r = lax.rsqrt(_seg_mean(xc * xc, seg_v) + EPS)
            y = _rot(xc * r * qn_ref[...], ca_v, sa_v, lane)
            qa_ref[:, LANES * cb:LANES * (cb + 1)] = (y * 0.125).astype(BF16)
        kc = proj[:, 512:640]
        r = lax.rsqrt(_seg_mean(kc * kc, seg_v) + EPS)
        ka_ref[...] = _rot(kc * r * kn_ref[...], ca_v, sa_v, lane).astype(BF16)
        va_ref[...] = proj[:, 640:768].astype(BF16)
        cq = proj[:, 1280:1536]
        cqn = (cq * _rms(cq) * qln_ref[...]).astype(BF16)
        ckv = proj[:, 1536:1664]
        ckvn = ckv * _rms(ckv) * kvln_ref[...]
        qb = _dot(cqn, wuq_ref[...])
        qlat = _dot(qb[:, 0:512].astype(BF16), wuk_ref[...])
        for hh in range(B_HEADS):
            qcat_ref[hh, :, 0:LANES] = qlat[:, LANES * hh:LANES * (hh + 1)].astype(BF16)
            qr = _rot(qb[:, 512 + LANES * hh:512 + LANES * (hh + 1)], ct_v, st_v, lane)
            qcat_ref[hh, :, LANES:2 * LANES] = qr.astype(BF16)
        kcat_ref[:, 0:LANES] = ckvn.astype(BF16)
        kcat_ref[:, LANES:2 * LANES] = _rot(proj[:, 1664:1792], ct_v, st_v, lane).astype(BF16)

    return pl.pallas_call(
        body, name="even_pre_fwd", grid=(S // ts,),
        in_specs=[_row_spec(ts, D_MODEL), _full_spec((3, D_MODEL)), _full_spec((1, D_MODEL)), _full_spec((D_MODEL, EVEN_P)),
                  _full_spec((1, LANES)), _full_spec((1, LANES)), _full_spec((LANES, LANES)),
                  _row_spec(ts, LANES), _row_spec(ts, LANES), _row_spec(ts, LANES), _row_spec(ts, LANES),
                  _full_spec((1, B_Q_LORA)), _full_spec((1, B_KV_LORA)), _full_spec((B_Q_LORA, 1536)), _full_spec((512, 1024))],
        out_specs=[_row_spec(ts, D_MODEL), _row_spec(ts, EVEN_P), _row_spec(ts, 512), _row_spec(ts, LANES), _row_spec(ts, LANES),
                   pl.BlockSpec((B_HEADS, ts, 2 * LANES), lambda i: (0, i, 0)), _row_spec(ts, 2 * LANES)],
        out_shape=[_sds((S, D_MODEL), BF16), _sds((S, EVEN_P), F32), _sds((S, 512), BF16), _sds((S, LANES), BF16),
                   _sds((S, LANES), BF16), _sds((B_HEADS, S, 2 * LANES), BF16), _sds((S, 2 * LANES), BF16)],
        compiler_params=_params(("arbitrary",)),
    )(x, mod, nw, wie, qn, kn, seg, ca, sa, ct, st, qln, kvln, wuq, wuk)


def _band_bias_mask(s, slope, i, jj, nq, tq, tk):
    row = lax.broadcasted_iota(jnp.int32, (2 * tq, tk), 0)
    t_pos = i * tq + jnp.where(row >= tq, row - tq, row)
    s_pos = jj * tk + lax.broadcasted_iota(jnp.int32, (2 * tq, tk), 1)
    dist = jnp.abs(t_pos - s_pos)
    s = s - slope * dist.astype(F32)
    valid = (dist <= WINDOW) & (jj >= 0) & (jj < nq)
    return jnp.where(valid, s, NEG)


def _band_offset(j):
    return jnp.where(j == 1, -1, 0) + jnp.where(j == 2, 1, 0)


def _pp_fwd(q, k, v, *, kdiv, tq, tk, name, slope=None, sink=None):
    S = q.shape[0]
    nb = q.shape[1] // LANES
    nq = S // tq
    band = slope is not None
    nkv = 3 if band else S // tk
    if band:
        assert tq == tk and WINDOW <= tk

    def kv_map(b, i, j):
        if band:
            return (jnp.clip(i + _band_offset(j), 0, nq - 1), b // kdiv)
        return (j, b // kdiv)

    def body(*refs):
        if band:
            q_ref, k_ref, v_ref, slope_ref, sink_ref, o_ref, lse_ref, qs, m_s, l_s, acc = refs
        else:
            q_ref, k_ref, v_ref, o_ref, lse_ref, qs, m_s, l_s, acc = refs
        i, j = pl.program_id(1), pl.program_id(2)
        lo = _lane_iota() < HEAD_DIM

        @pl.when(j == 0)
        def _():
            qv = q_ref[...]
            zero = jnp.zeros_like(qv)
            qs[0:tq, :] = jnp.where(lo, qv, zero)
            qs[tq:2 * tq, :] = jnp.where(lo, zero, qv)
            m_s[...] = jnp.full((2 * tq, 1), NEG, F32)
            l_s[...] = jnp.zeros((2 * tq, 1), F32)
            acc[...] = jnp.zeros((2 * tq, LANES), F32)

        s = _dot_nt(qs[...], k_ref[...])
        if band:
            s = _band_bias_mask(s, slope_ref[0], i, i + _band_offset(j), nq, tq, tk)
        m_prev = m_s[...]
        m_new = jnp.maximum(m_prev, jnp.max(s, axis=-1, keepdims=True))
        alpha = jnp.exp(m_prev - m_new)
        p = jnp.exp(s - m_new)
        l_s[...] = alpha * l_s[...] + jnp.sum(p, axis=-1, keepdims=True)
        acc[...] = alpha * acc[...] + _dot(p.astype(BF16), v_ref[...])
        m_s[...] = m_new

        @pl.when(j == nkv - 1)
        def _():
            m_f, l_f, a_f = m_s[...], l_s[...], acc[...]
            if band:
                sk = sink_ref[0]
                m_t = jnp.maximum(m_f, sk)
                a = jnp.exp(m_f - m_t)
                l_f = l_f * a + jnp.exp(sk - m_t)
                a_f = a_f * a
                m_f = m_t
            ov = a_f / l_f
            o_ref[...] = jnp.where(lo, ov[0:tq, :], ov[tq:2 * tq, :])
            lse_ref[0, 0] = m_f + jnp.log(l_f)

    in_specs = [pl.BlockSpec((tq, LANES), lambda b, i, j: (i, b)), pl.BlockSpec((tk, LANES), kv_map),
                pl.BlockSpec((tk, LANES), kv_map)]
    args = [q, k, v]
    if band:
        in_specs += [pl.BlockSpec((1, 2 * tq, 1), lambda b, i, j: (b, 0, 0))] * 2
        args += [slope, sink]
    return pl.pallas_call(
        body, name=name, grid=(nb, nq, nkv), in_specs=in_specs,
        out_specs=[pl.BlockSpec((tq, LANES), lambda b, i, j: (i, b)),
                   pl.BlockSpec((1, 1, 2 * tq, 1), lambda b, i, j: (b, i, 0, 0))],
        out_shape=[_sds((S, nb * LANES), F32), _sds((nb, nq, 2 * tq, 1), F32)],
        scratch_shapes=[pltpu.VMEM((2 * tq, LANES), BF16), pltpu.VMEM((2 * tq, 1), F32), pltpu.VMEM((2 * tq, 1), F32),
                        pltpu.VMEM((2 * tq, LANES), F32)],
        compiler_params=_params(("arbitrary", "arbitrary", "arbitrary")),
    )(*args)


def _pp_bwd(q, k, v, o, do, lse, *, kdiv, tq, tk, name, slope=None, sink=None):
    S = q.shape[0]
    nb = q.shape[1] // LANES
    nkb = k.shape[1] // LANES
    nq = S // tq
    band = slope is not None
    nkv = 3 if band else S // tk

    def kv_map(b, i, j):
        if band:
            return (jnp.clip(i + _band_offset(j), 0, nq - 1), b // kdiv)
        return (j, b // kdiv)

    def body(*refs):
        if band:
            (q_ref, k_ref, v_ref, o_ref, do_ref, lse_ref, slope_ref, sink_ref, dq_ref, dk_ref, dv_ref, dsink_ref,
             qs, dos, delta_s, dq_acc) = refs
        else:
            q_ref, k_ref, v_ref, o_ref, do_ref, lse_ref, dq_ref, dk_ref, dv_ref, qs, dos, delta_s, dq_acc = refs
        b, i, j = pl.program_id(0), pl.program_id(1), pl.program_id(2)
        lo = _lane_iota() < HEAD_DIM

        @pl.when((b % kdiv == 0) & (i == 0) & (j == 0))
        def _():
            dk_ref[...] = jnp.zeros((S, LANES), F32)
            dv_ref[...] = jnp.zeros((S, LANES), F32)

        @pl.when(j == 0)
        def _():
            qv = q_ref[...]
            zero = jnp.zeros_like(qv)
            qs[0:tq, :] = jnp.where(lo, qv, zero)
            qs[tq:2 * tq, :] = jnp.where(lo, zero, qv)
            dov = do_ref[...]
            dob = dov.astype(BF16)
            zb = jnp.zeros_like(dob)
            dos[0:tq, :] = jnp.where(lo, dob, zb)
            dos[tq:2 * tq, :] = jnp.where(lo, zb, dob)
            prod = dov * o_ref[...]
            d_lo = jnp.sum(jnp.where(lo, prod, 0.0), axis=-1, keepdims=True)
            d_hi = jnp.sum(jnp.where(lo, 0.0, prod), axis=-1, keepdims=True)
            delta_s[0:tq, :] = d_lo
            delta_s[tq:2 * tq, :] = d_hi
            dq_acc[...] = jnp.zeros((2 * tq, LANES), F32)
            if band:
                @pl.when(i == 0)
                def _():
                    dsink_ref[...] = jnp.zeros((1, 8, LANES), F32)
                lse_v = lse_ref[0, 0]
                sk = sink_ref[0]
                c_lo = -jnp.exp(sk[0:tq] - lse_v[0:tq]) * d_lo
                c_hi = -jnp.exp(sk[tq:2 * tq] - lse_v[tq:2 * tq]) * d_hi
                dsink_ref[0, 0:1, :] += jnp.broadcast_to(jnp.sum(c_lo, axis=0, keepdims=True), (1, LANES))
                dsink_ref[0, 1:2, :] += jnp.broadcast_to(jnp.sum(c_hi, axis=0, keepdims=True), (1, LANES))

        kv = k_ref[...]
        s = _dot_nt(qs[...], kv)
        if band:
            jj = i + _band_offset(j)
            s = _band_bias_mask(s, slope_ref[0], i, jj, nq, tq, tk)
            jc = jnp.clip(jj, 0, nq - 1)
        else:
            jc = j
        p = jnp.exp(s - lse_ref[0, 0])
        dp = _dot_nt(dos[...], v_ref[...])
        ds = (p * (dp - delta_s[...])).astype(BF16)
        rows = pl.ds(pl.multiple_of(jc * tk, tk), tk)
        dv_ref[rows, :] += _dot_tn(p.astype(BF16), dos[...])
        dk_ref[rows, :] += _dot_tn(ds, qs[...])
        dq_acc[...] += _dot(ds, kv)

        @pl.when(j == nkv - 1)
        def _():
            dqv = dq_acc[...]
            dq_ref[...] = jnp.where(lo, dqv[0:tq, :], dqv[tq:2 * tq, :])

    qmap = lambda b, i, j: (i, b)
    in_specs = [pl.BlockSpec((tq, LANES), qmap), pl.BlockSpec((tk, LANES), kv_map), pl.BlockSpec((tk, LANES), kv_map),
                pl.BlockSpec((tq, LANES), qmap), pl.BlockSpec((tq, LANES), qmap),
                pl.BlockSpec((1, 1, 2 * tq, 1), lambda b, i, j: (b, i, 0, 0))]
    args = [q, k, v, o, do, lse]
    out_specs = [pl.BlockSpec((tq, LANES), qmap), pl.BlockSpec((S, LANES), lambda b, i, j: (0, b // kdiv)),
                 pl.BlockSpec((S, LANES), lambda b, i, j: (0, b // kdiv))]
    out_shape = [_sds((S, nb * LANES), F32), _sds((S, nkb * LANES), F32), _sds((S, nkb * LANES), F32)]
    if band:
        in_specs += [pl.BlockSpec((1, 2 * tq, 1), lambda b, i, j: (b, 0, 0))] * 2
        args += [slope, sink]
        out_specs.append(pl.BlockSpec((1, 8, LANES), lambda b, i, j: (b, 0, 0)))
        out_shape.append(_sds((nb, 8, LANES), F32))
    return pl.pallas_call(
        body, name=name, grid=(nb, nq, nkv), in_specs=in_specs, out_specs=out_specs, out_shape=out_shape,
        scratch_shapes=[pltpu.VMEM((2 * tq, LANES), BF16), pltpu.VMEM((2 * tq, LANES), BF16), pltpu.VMEM((2 * tq, 1), F32),
                        pltpu.VMEM((2 * tq, LANES), F32)],
        compiler_params=_params(("arbitrary", "arbitrary", "arbitrary")),
    )(*args)


def _win_rows(i, tq, nk, S):
    k0 = jnp.clip(i * tq - WINDOW, 0, S - nk)
    return k0, pl.ds(pl.multiple_of(k0, LANES), nk)


def _win_scores(qs, kk, slope, i, k0, tq, nk):
    s = _dot_nt(qs, kk)
    row = lax.broadcasted_iota(jnp.int32, (2 * tq, nk), 0)
    t_pos = i * tq + jnp.where(row >= tq, row - tq, row)
    dist = jnp.abs(t_pos - (k0 + lax.broadcasted_iota(jnp.int32, (2 * tq, nk), 1)))
    return jnp.where(dist <= WINDOW, s - slope * dist.astype(F32), NEG)


def _stack_halves(v, lo):
    zero = jnp.zeros_like(v)
    return jnp.concatenate([jnp.where(lo, v, zero), jnp.where(lo, zero, v)], axis=0)


def _win_fwd(q, k, v, slope, sink, *, kdiv, tq, name):
    S = q.shape[0]
    nb = q.shape[1] // LANES
    nq = S // tq
    nk = min(tq + 2 * WINDOW, S)

    def body(q_ref, k_ref, v_ref, slope_ref, sink_ref, o_ref, lse_ref):
        i = pl.program_id(1)
        lo = _lane_iota() < HEAD_DIM
        k0, rows = _win_rows(i, tq, nk, S)
        s = _win_scores(_stack_halves(q_ref[...], lo), k_ref[rows, :], slope_ref[0], i, k0, tq, nk)
        sk = sink_ref[0]
        m = jnp.maximum(jnp.max(s, axis=-1, keepdims=True), sk)
        p = jnp.exp(s - m)
        l = jnp.sum(p, axis=-1, keepdims=True) + jnp.exp(sk - m)
        ov = _dot(p.astype(BF16), v_ref[rows, :]) / l
        o_ref[...] = jnp.where(lo, ov[0:tq, :], ov[tq:2 * tq, :])
        lse_ref[0, 0] = m + jnp.log(l)

    kv_spec = pl.BlockSpec((S, LANES), lambda b, i: (0, b // kdiv))
    row_spec = pl.BlockSpec((1, 2 * tq, 1), lambda b, i: (b, 0, 0))
    return pl.pallas_call(
        body, name=name, grid=(nb, nq),
        in_specs=[pl.BlockSpec((tq, LANES), lambda b, i: (i, b)), kv_spec, kv_spec, row_spec, row_spec],
        out_specs=[pl.BlockSpec((tq, LANES), lambda b, i: (i, b)), pl.BlockSpec((1, 1, 2 * tq, 1), lambda b, i: (b, i, 0, 0))],
        out_shape=[_sds((S, nb * LANES), F32), _sds((nb, nq, 2 * tq, 1), F32)],
        compiler_params=_params(("arbitrary", "arbitrary")),
    )(q, k, v, slope, sink)


def _win_bwd(q, k, v, o, do, lse, slope, sink, *, kdiv, tq, name):
    S = q.shape[0]
    nb = q.shape[1] // LANES
    nkb = k.shape[1] // LANES
    nq = S // tq
    nk = min(tq + 2 * WINDOW, S)

    def body(q_ref, k_ref, v_ref, o_ref, do_ref, lse_ref, slope_ref, sink_ref, dq_ref, dk_ref, dv_ref, dsink_ref):
        b, i = pl.program_id(0), pl.program_id(1)
        lo = _lane_iota() < HEAD_DIM

        @pl.when((b % kdiv == 0) & (i == 0))
        def _():
            dk_ref[...] = jnp.zeros((S, LANES), F32)
            dv_ref[...] = jnp.zeros((S, LANES), F32)

        @pl.when(i == 0)
        def _():
            dsink_ref[...] = jnp.zeros((1, 8, LANES), F32)

        k0, rows = _win_rows(i, tq, nk, S)
        qs = _stack_halves(q_ref[...], lo)
        dov = do_ref[...]
        dos = _stack_halves(dov.astype(BF16), lo)
        prod = dov * o_ref[...]
        d_lo = jnp.sum(jnp.where(lo, prod, 0.0), axis=-1, keepdims=True)
        d_hi = jnp.sum(jnp.where(lo, 0.0, prod), axis=-1, keepdims=True)
        delta = jnp.concatenate([d_lo, d_hi], axis=0)
        lse_v = lse_ref[0, 0]
        c_sink = -jnp.exp(sink_ref[0] - lse_v) * delta
        dsink_ref[0, 0:1, :] += jnp.broadcast_to(jnp.sum(c_sink[0:tq], axis=0, keepdims=True), (1, LANES))
        dsink_ref[0, 1:2, :] += jnp.broadcast_to(jnp.sum(c_sink[tq:2 * tq], axis=0, keepdims=True), (1, LANES))
        kk = k_ref[rows, :]
        p = jnp.exp(_win_scores(qs, kk, slope_ref[0], i, k0, tq, nk) - lse_v)
        ds = (p * (_dot_nt(dos, v_ref[rows, :]) - delta)).astype(BF16)
        dv_ref[rows, :] += _dot_tn(p.astype(BF16), dos)
        dk_ref[rows, :] += _dot_tn(ds, qs)
        dqv = _dot(ds, kk)
        dq_ref[...] = jnp.where(lo, dqv[0:tq, :], dqv[tq:2 * tq, :])

    qmap = lambda b, i: (i, b)
    kv_spec = pl.BlockSpec((S, LANES), lambda b, i: (0, b // kdiv))
    row_spec = pl.BlockSpec((1, 2 * tq, 1), lambda b, i: (b, 0, 0))
    return pl.pallas_call(
        body, name=name, grid=(nb, nq),
        in_specs=[pl.BlockSpec((tq, LANES), qmap), kv_spec, kv_spec, pl.BlockSpec((tq, LANES), qmap), pl.BlockSpec((tq, LANES), qmap),
                  pl.BlockSpec((1, 1, 2 * tq, 1), lambda b, i: (b, i, 0, 0)), row_spec, row_spec],
        out_specs=[pl.BlockSpec((tq, LANES), qmap), kv_spec, kv_spec, pl.BlockSpec((1, 8, LANES), lambda b, i: (b, 0, 0))],
        out_shape=[_sds((S, nb * LANES), F32), _sds((S, nkb * LANES), F32), _sds((S, nkb * LANES), F32), _sds((nb, 8, LANES), F32)],
        compiler_params=_params(("arbitrary", "arbitrary")),
    )(q, k, v, o, do, lse, slope, sink)


MLA_SCALE = (B_NOPE + B_ROPE) ** -0.5


def _mla_fwd(q, kcat, *, tq, tk):
    S = kcat.shape[0]
    nq, nkv = S // tq, S // tk
    R = B_HEADS * tq

    def body(q_ref, k_ref, o_ref, lse_ref, m_s, l_s, acc):
        j = pl.program_id(1)

        @pl.when(j == 0)
        def _():
            m_s[...] = jnp.full((R, 1), NEG, F32)
            l_s[...] = jnp.zeros((R, 1), F32)
            acc[...] = jnp.zeros((R, LANES), F32)

        kv = k_ref[...]
        s = _dot_nt(q_ref[...].reshape(R, 2 * LANES), kv) * MLA_SCALE
        m_prev = m_s[...]
        m_new = jnp.maximum(m_prev, jnp.max(s, axis=-1, keepdims=True))
        alpha = jnp.exp(m_prev - m_new)
        p = jnp.exp(s - m_new)
        l_s[...] = alpha * l_s[...] + jnp.sum(p, axis=-1, keepdims=True)
        acc[...] = alpha * acc[...] + _dot(p.astype(BF16), kv[:, 0:LANES])
        m_s[...] = m_new

        @pl.when(j == nkv - 1)
        def _():
            l_f = l_s[...]
            o_ref[...] = (acc[...] / l_f).reshape(B_HEADS, tq, LANES)
            lse_ref[0] = m_s[...] + jnp.log(l_f)

    return pl.pallas_call(
        body, name="mla_fwd", grid=(nq, nkv),
        in_specs=[pl.BlockSpec((B_HEADS, tq, 2 * LANES), lambda i, j: (0, i, 0)), pl.BlockSpec((tk, 2 * LANES), lambda i, j: (j, 0))],
        out_specs=[pl.BlockSpec((B_HEADS, tq, LANES), lambda i, j: (0, i, 0)), pl.BlockSpec((1, R, 1), lambda i, j: (i, 0, 0))],
        out_shape=[_sds((B_HEADS, S, LANES), F32), _sds((nq, R, 1), F32)],
        scratch_shapes=[pltpu.VMEM((R, 1), F32), pltpu.VMEM((R, 1), F32), pltpu.VMEM((R, LANES), F32)],
        compiler_params=_params(("arbitrary", "arbitrary")),
    )(q, kcat)


def _mla_bwd(q, kcat, o, do, lse, *, tq, tk):
    S = kcat.shape[0]
    nq, nkv = S // tq, S // tk
    R = B_HEADS * tq

    def body(q_ref, k_ref, o_ref, do_ref, lse_ref, dq_ref, dk_ref, dos, delta_s, dq_acc):
        i, j = pl.program_id(0), pl.program_id(1)

        @pl.when((i == 0) & (j == 0))
        def _():
            dk_ref[...] = jnp.zeros((S, 2 * LANES), F32)

        @pl.when(j == 0)
        def _():
            dov = do_ref[...].reshape(R, LANES)
            dos[...] = dov.astype(BF16)
            delta_s[...] = jnp.sum(dov * o_ref[...].reshape(R, LANES), axis=-1, keepdims=True)
            dq_acc[...] = jnp.zeros((R, 2 * LANES), F32)

        kv = k_ref[...]
        qv = q_ref[...].reshape(R, 2 * LANES)
        s = _dot_nt(qv, kv) * MLA_SCALE
        p = jnp.exp(s - lse_ref[0])
        dp = _dot_nt(dos[...], kv[:, 0:LANES])
        ds = (p * (dp - delta_s[...]) * MLA_SCALE).astype(BF16)
        rows = pl.ds(pl.multiple_of(j * tk, tk), tk)
        dk_ref[rows, :] += _dot_tn(ds, qv)
        dk_ref[rows, 0:LANES] += _dot_tn(p.astype(BF16), dos[...])
        dq_acc[...] += _dot(ds, kv)

        @pl.when(j == nkv - 1)
        def _():
            dq_ref[...] = dq_acc[...].reshape(B_HEADS, tq, 2 * LANES)

    return pl.pallas_call(
        body, name="mla_bwd", grid=(nq, nkv),
        in_specs=[pl.BlockSpec((B_HEADS, tq, 2 * LANES), lambda i, j: (0, i, 0)), pl.BlockSpec((tk, 2 * LANES), lambda i, j: (j, 0)),
                  pl.BlockSpec((B_HEADS, tq, LANES), lambda i, j: (0, i, 0)), pl.BlockSpec((B_HEADS, tq, LANES), lambda i, j: (0, i, 0)),
                  pl.BlockSpec((1, R, 1), lambda i, j: (i, 0, 0))],
        out_specs=[pl.BlockSpec((B_HEADS, tq, 2 * LANES), lambda i, j: (0, i, 0)), pl.BlockSpec((S, 2 * LANES), lambda i, j: (0, 0))],
        out_shape=[_sds((B_HEADS, S, 2 * LANES), F32), _sds((S, 2 * LANES), F32)],
        scratch_shapes=[pltpu.VMEM((R, LANES), BF16), pltpu.VMEM((R, 1), F32), pltpu.VMEM((R, 2 * LANES), F32)],
        compiler_params=_params(("arbitrary", "arbitrary")),
    )(q, kcat, o, do, lse)


def _sum_rows(v):
    return jnp.sum(v, axis=0, keepdims=True)


def _norm_mod_bwd(dh, xv, mod_ref, nw_ref, stats_ref):
    r = _rms(xv)
    xn = xv * r
    nw = nw_ref[...]
    stats_ref[0:1, :] += _sum_rows(dh)
    stats_ref[1:2, :] += _sum_rows(dh * (xn * nw))
    dn = dh * (1.0 + mod_ref[1:2, :])
    stats_ref[2:3, :] += _sum_rows(dn * xn)
    return _rms_bwd(xv, r, dn * nw)


def _even_post_fwd(oa, olat, proj, x, gate, wuv, woe):
    S = x.shape[0]
    ts = min(ROW_TILE, S)

    def body(oa_ref, ol_ref, proj_ref, x_ref, gate_ref, wuv_ref, woe_ref, y_ref, x1_ref):
        sa, _ = _silu_and_grad(proj_ref[:, 768:1280])
        sb, _ = _silu_and_grad(proj_ref[:, 1792:2304])
        olc = jnp.concatenate([ol_ref[hh] for hh in range(B_HEADS)], axis=1).astype(BF16)
        ob = _dot(olc, wuv_ref[...])
        mix = jnp.concatenate([oa_ref[...] * sa, ob * sb], axis=1).astype(BF16)
        y = _dot(mix, woe_ref[...])
        y_ref[...] = y
        x1_ref[...] = x_ref[...] + gate_ref[...] * y

    return pl.pallas_call(
        body, name="even_post_fwd", grid=(S // ts,),
        in_specs=[_row_spec(ts, 512), pl.BlockSpec((B_HEADS, ts, LANES), lambda i: (0, i, 0)), _row_spec(ts, EVEN_P),
                  _row_spec(ts, D_MODEL), _full_spec((1, D_MODEL)), _full_spec((1024, 512)), _full_spec((1024, D_MODEL))],
        out_specs=[_row_spec(ts, D_MODEL), _row_spec(ts, D_MODEL)],
        out_shape=[_sds((S, D_MODEL), F32), _sds((S, D_MODEL), F32)],
        compiler_params=_params(("arbitrary",)),
    )(oa, olat, proj, x, gate, wuv, woe)


def _odd_pre_fwd(x, mod, nw, wio):
    S = x.shape[0]
    ts = min(ROW_TILE, S)

    def body(x_ref, mod_ref, nw_ref, wio_ref, h_ref, proj_ref, q_ref, k_ref, v_ref):
        xv = x_ref[...]
        h = (xv * _rms(xv) * nw_ref[...]) * (1.0 + mod_ref[1:2, :]) + mod_ref[0:1, :]
        hb = h.astype(BF16)
        h_ref[...] = hb
        proj = _dot(hb, wio_ref[...])
        proj_ref[...] = proj
        q_ref[...] = (proj[:, 0:1024] * 0.125).astype(BF16)
        k_ref[...] = proj[:, 1024:1280].astype(BF16)
        v_ref[...] = proj[:, 1280:1536].astype(BF16)

    return pl.pallas_call(
        body, name="odd_pre_fwd", grid=(S // ts,),
        in_specs=[_row_spec(ts, D_MODEL), _full_spec((3, D_MODEL)), _full_spec((1, D_MODEL)), _full_spec((D_MODEL, ODD_IN))],
        out_specs=[_row_spec(ts, D_MODEL), _row_spec(ts, ODD_IN), _row_spec(ts, 1024), _row_spec(ts, 256), _row_spec(ts, 256)],
        out_shape=[_sds((S, D_MODEL), BF16), _sds((S, ODD_IN), F32), _sds((S, 1024), BF16), _sds((S, 256), BF16),
                   _sds((S, 256), BF16)],
        compiler_params=_params(("arbitrary",)),
    )(x, mod, nw, wio)


def _odd_post(oc, proj, x1, gate, woo, fw, tgt):
    S = x1.shape[0]
    ts = min(ROW_TILE, S)

    def body(oc_ref, proj_ref, x_ref, gate_ref, woo_ref, fw_ref, tgt_ref, doc_ref, dgc_ref, dx2_ref, dwoo_ref, stats_ref):
        @pl.when(pl.program_id(0) == 0)
        def _():
            dwoo_ref[...] = jnp.zeros((D_MODEL, D_MODEL), F32)
            stats_ref[...] = jnp.zeros((8, D_MODEL), F32)

        ocv = oc_ref[...]
        sg, dsg = _silu_and_grad(proj_ref[:, 1536:2560])
        mix = (ocv * sg).astype(BF16)
        woo_v = woo_ref[...]
        y = _dot(mix, woo_v)
        gate_v = gate_ref[...]
        x2 = x_ref[...] + gate_v * y
        r = _rms(x2)
        fw_v = fw_ref[...]
        xn = x2 * r
        err = xn * fw_v - tgt_ref[...]
        dout = err * (1.0 / D_MODEL)
        dx2 = _rms_bwd(x2, r, dout * fw_v)
        dx2_ref[...] = dx2
        stats_ref[0:1, :] += _sum_rows(dout * xn)
        stats_ref[1:2, :] += _sum_rows(dx2 * y)
        loss_t = 0.5 * jnp.sum(_sum_rows(err * dout), axis=-1, keepdims=True)
        stats_ref[2:3, :] += jnp.broadcast_to(loss_t, (1, D_MODEL))
        dy = (gate_v * dx2).astype(BF16)
        dmix = _dot_nt(dy, woo_v)
        dwoo_ref[...] += _dot_tn(mix, dy)
        doc_ref[...] = dmix * sg
        dgc_ref[...] = dmix * ocv * dsg

    return pl.pallas_call(
        body, name="odd_post", grid=(S // ts,),
        in_specs=[_row_spec(ts, D_MODEL), _row_spec(ts, ODD_IN), _row_spec(ts, D_MODEL), _full_spec((1, D_MODEL)),
                  _full_spec((D_MODEL, D_MODEL)), _full_spec((1, D_MODEL)), _row_spec(ts, D_MODEL)],
        out_specs=[_row_spec(ts, D_MODEL), _row_spec(ts, D_MODEL), _row_spec(ts, D_MODEL),
                   _full_spec((D_MODEL, D_MODEL), single=False), _full_spec((8, D_MODEL), single=False)],
        out_shape=[_sds((S, D_MODEL), F32), _sds((S, D_MODEL), F32), _sds((S, D_MODEL), F32), _sds((D_MODEL, D_MODEL), F32),
                   _sds((8, D_MODEL), F32)],
        compiler_params=_params(("arbitrary",)),
    )(oc, proj, x1, gate, woo, fw, tgt)


def _odd_pre_bwd(dq, dk, dv, dgc, h, x, dx_res, mod, nw, wio):
    S = x.shape[0]
    ts = min(ROW_TILE, S)

    def body(dq_ref, dk_ref, dv_ref, dgc_ref, h_ref, x_ref, dxr_ref, mod_ref, nw_ref, wio_ref, dx_ref, dw_ref, stats_ref):
        @pl.when(pl.program_id(0) == 0)
        def _():
            dw_ref[...] = jnp.zeros((D_MODEL, ODD_IN), F32)
            stats_ref[...] = jnp.zeros((8, D_MODEL), F32)

        dproj = jnp.concatenate([dq_ref[...] * 0.125, dk_ref[...], dv_ref[...], dgc_ref[...]], axis=1).astype(BF16)
        dh = _dot_nt(dproj, wio_ref[...])
        dw_ref[...] += _dot_tn(h_ref[...], dproj)
        dx_ref[...] = dxr_ref[...] + _norm_mod_bwd(dh, x_ref[...], mod_ref, nw_ref, stats_ref)

    return pl.pallas_call(
        body, name="odd_pre_bwd", grid=(S // ts,),
        in_specs=[_row_spec(ts, 1024), _row_spec(ts, 256), _row_spec(ts, 256), _row_spec(ts, 1024), _row_spec(ts, D_MODEL),
                  _row_spec(ts, D_MODEL), _row_spec(ts, D_MODEL), _full_spec((3, D_MODEL)), _full_spec((1, D_MODEL)),
                  _full_spec((D_MODEL, ODD_IN))],
        out_specs=[_row_spec(ts, D_MODEL), _full_spec((D_MODEL, ODD_IN), single=False), _full_spec((8, D_MODEL), single=False)],
        out_shape=[_sds((S, D_MODEL), F32), _sds((D_MODEL, ODD_IN), F32), _sds((8, D_MODEL), F32)],
        compiler_params=_params(("arbitrary",)),
    )(dq, dk, dv, dgc, h, x, dx_res, mod, nw, wio)


def _even_post_bwd(dx1, y, oa, olat, proj, gate, wuv, woe):
    S = dx1.shape[0]
    ts = min(ROW_TILE, S)

    def body(dx_ref, y_ref, oa_ref, ol_ref, proj_ref, gate_ref, wuv_ref, woe_ref,
             doa_ref, dga_ref, dgb_ref, dol_ref, dwoe_ref, dwuv_ref, stats_ref):
        @pl.when(pl.program_id(0) == 0)
        def _():
            dwoe_ref[...] = jnp.zeros((D_MODEL, D_MODEL), F32)
            dwuv_ref[...] = jnp.zeros((1024, 512), F32)
            stats_ref[...] = jnp.zeros((8, D_MODEL), F32)

        dxv = dx_ref[...]
        stats_ref[0:1, :] += _sum_rows(dxv * y_ref[...])
        dy = (gate_ref[...] * dxv).astype(BF16)
        sa, dsa = _silu_and_grad(proj_ref[:, 768:1280])
        sb, dsb = _silu_and_grad(proj_ref[:, 1792:2304])
        olc = jnp.concatenate([ol_ref[hh] for hh in range(B_HEADS)], axis=1).astype(BF16)
        wuv_v = wuv_ref[...]
        ob = _dot(olc, wuv_v)
        oav = oa_ref[...]
        mix = jnp.concatenate([oav * sa, ob * sb], axis=1).astype(BF16)
        dmix = _dot_nt(dy, woe_ref[...])
        dwoe_ref[...] += _dot_tn(mix, dy)
        dma, dmb = dmix[:, 0:512], dmix[:, 512:1024]
        doa_ref[...] = dma * sa
        dga_ref[...] = dma * oav * dsa
        dgb_ref[...] = dmb * ob * dsb
        dob = (dmb * sb).astype(BF16)
        dol = _dot_nt(dob, wuv_v)
        dwuv_ref[...] += _dot_tn(olc, dob)
        for hh in range(B_HEADS):
            dol_ref[hh] = dol[:, LANES * hh:LANES * (hh + 1)]

    head_spec = pl.BlockSpec((B_HEADS, ts, LANES), lambda i: (0, i, 0))
    return pl.pallas_call(
        body, name="even_post_bwd", grid=(S // ts,),
        in_specs=[_row_spec(ts, D_MODEL), _row_spec(ts, D_MODEL), _row_spec(ts, 512), head_spec, _row_spec(ts, EVEN_P),
                  _full_spec((1, D_MODEL)), _full_spec((1024, 512)), _full_spec((1024, D_MODEL))],
        out_specs=[_row_spec(ts, 512), _row_spec(ts, 512), _row_spec(ts, 512), head_spec,
                   _full_spec((D_MODEL, D_MODEL), single=False), _full_spec((1024, 512), single=False),
                   _full_spec((8, D_MODEL), single=False)],
        out_shape=[_sds((S, 512), F32), _sds((S, 512), F32), _sds((S, 512), F32), _sds((B_HEADS, S, LANES), F32),
                   _sds((D_MODEL, D_MODEL), F32), _sds((1024, 512), F32), _sds((8, D_MODEL), F32)],
        compiler_params=_params(("arbitrary",)),
    )(dx1, y, oa, olat, proj, gate, wuv, woe)


EVEN_BWD_ROW_TILE = 128


def _even_pre_bwd(x, h, proj, dqa, dka, dva, dga, dgb, dqcat, dkcat, dx_res, mod, nw, wie, qn, kn, seg, ca, sa, ct, st,
                  qln, kvln, wuq, wuk):
    S = x.shape[0]
    ts = min(EVEN_BWD_ROW_TILE, S)

    def body(x_ref, h_ref, proj_ref, dqa_ref, dka_ref, dva_ref, dga_ref, dgb_ref, dqc_ref, dkc_ref, dxr_ref, mod_ref, nw_ref,
             wie_ref, qn_ref, kn_ref, seg_ref, ca_ref, sa_ref, ct_ref, st_ref, qln_ref, kvln_ref, wuq_ref, wuk_ref,
             dx_ref, dwie_ref, dwuq_ref, dwuk_ref, stats_ref, nstats_ref):
        @pl.when(pl.program_id(0) == 0)
        def _():
            dwie_ref[...] = jnp.zeros((D_MODEL, EVEN_P), F32)
            dwuq_ref[...] = jnp.zeros((B_Q_LORA, 1536), F32)
            dwuk_ref[...] = jnp.zeros((512, 1024), F32)
            stats_ref[...] = jnp.zeros((8, D_MODEL), F32)
            nstats_ref[...] = jnp.zeros((8, 256), F32)

        lane = _lane_iota()
        ca_v, sa_v, ct_v, st_v = ca_ref[...], sa_ref[...], ct_ref[...], st_ref[...]
        seg_v = seg_ref[...]

        def head_norm_bwd(xc, dy, w):
            r = lax.rsqrt(_seg_mean(xc * xc, seg_v) + EPS)
            g = dy * w
            dxc = r * g - xc * (r * r * r) * _seg_mean(xc * g, seg_v)
            return dxc, _sum_rows(dy * (xc * r))

        pieces = []
        dqn = jnp.zeros((1, LANES), F32)
        for cb in range(4):
            sl = slice(LANES * cb, LANES * (cb + 1))
            dy = _rot_bwd(dqa_ref[:, sl] * 0.125, ca_v, sa_v, lane)
            dxc, dw = head_norm_bwd(proj_ref[:, sl], dy, qn_ref[...])
            pieces.append(dxc)
            dqn = dqn + dw
        dxc, dkn = head_norm_bwd(proj_ref[:, 512:640], _rot_bwd(dka_ref[...], ca_v, sa_v, lane), kn_ref[...])
        pieces += [dxc, dva_ref[...], dga_ref[...]]
        nstats_ref[0:1, 0:LANES] += dqn + pltpu.roll(dqn, HEAD_DIM, 1)
        nstats_ref[1:2, 0:LANES] += dkn + pltpu.roll(dkn, HEAD_DIM, 1)

        cq = proj_ref[:, 1280:1536]
        rq = _rms(cq)
        cqn_f = cq * rq
        qln_v = qln_ref[...]
        cqn = (cqn_f * qln_v).astype(BF16)
        wuq_v, wuk_v = wuq_ref[...], wuk_ref[...]
        qnope = _dot(cqn, wuq_v[:, 0:512]).astype(BF16)
        dqlat = jnp.concatenate([dqc_ref[hh, :, 0:LANES] for hh in range(B_HEADS)], axis=1).astype(BF16)
        dqnope = _dot_nt(dqlat, wuk_v)
        dwuk_ref[...] += _dot_tn(qnope, dqlat)
        dqr = [_rot_bwd(dqc_ref[hh, :, LANES:2 * LANES], ct_v, st_v, lane) for hh in range(B_HEADS)]
        dqb = jnp.concatenate([dqnope] + dqr, axis=1).astype(BF16)
        dcqn = _dot_nt(dqb, wuq_v)
        dwuq_ref[...] += _dot_tn(cqn, dqb)
        nstats_ref[2:3, :] += _sum_rows(dcqn * cqn_f)
        dcq = _rms_bwd(cq, rq, dcqn * qln_v)
        ckv = proj_ref[:, 1536:1664]
        rk = _rms(ckv)
        dckvn = dkc_ref[:, 0:LANES]
        nstats_ref[3:4, 0:LANES] += _sum_rows(dckvn * (ckv * rk))
        dckv = _rms_bwd(ckv, rk, dckvn * kvln_ref[...])
        dkr = _rot_bwd(dkc_ref[:, LANES:2 * LANES], ct_v, st_v, lane)
        pieces += [dcq, dckv, dkr, dgb_ref[...]]
        dproj = jnp.concatenate(pieces, axis=1).astype(BF16)
        dh = _dot_nt(dproj, wie_ref[...])
        dwie_ref[...] += _dot_tn(h_ref[...], dproj)
        dx_ref[...] = dxr_ref[...] + _norm_mod_bwd(dh, x_ref[...], mod_ref, nw_ref, stats_ref)

    return pl.pallas_call(
        body, name="even_pre_bwd", grid=(S // ts,),
        in_specs=[_row_spec(ts, D_MODEL), _row_spec(ts, D_MODEL), _row_spec(ts, EVEN_P), _row_spec(ts, 512), _row_spec(ts, LANES),
                  _row_spec(ts, LANES), _row_spec(ts, 512), _row_spec(ts, 512),
                  pl.BlockSpec((B_HEADS, ts, 2 * LANES), lambda i: (0, i, 0)), _row_spec(ts, 2 * LANES), _row_spec(ts, D_MODEL),
                  _full_spec((3, D_MODEL)), _full_spec((1, D_MODEL)), _full_spec((D_MODEL, EVEN_P)),
                  _full_spec((1, LANES)), _full_spec((1, LANES)), _full_spec((LANES, LANES)),
                  _row_spec(ts, LANES), _row_spec(ts, LANES), _row_spec(ts, LANES), _row_spec(ts, LANES),
                  _full_spec((1, B_Q_LORA)), _full_spec((1, B_KV_LORA)), _full_spec((B_Q_LORA, 1536)), _full_spec((512, 1024))],
        out_specs=[_row_spec(ts, D_MODEL), _full_spec((D_MODEL, EVEN_P), single=False), _full_spec((B_Q_LORA, 1536), single=False),
                   _full_spec((512, 1024), single=False), _full_spec((8, D_MODEL), single=False), _full_spec((8, 256), single=False)],
        out_shape=[_sds((S, D_MODEL), F32), _sds((D_MODEL, EVEN_P), F32), _sds((B_Q_LORA, 1536), F32), _sds((512, 1024), F32),
                   _sds((8, D_MODEL), F32), _sds((8, 256), F32)],
        compiler_params=_params(("arbitrary",)),
    )(x, h, proj, dqa, dka, dva, dga, dgb, dqcat, dkcat, dx_res, mod, nw, wie, qn, kn, seg, ca, sa, ct, st, qln, kvln, wuq, wuk)


def _ada_fwd(c_all, w, b):
    n = w.shape[2]

    def body(c_ref, w_ref, b_ref, o_ref):
        cv = c_ref[...]
        o_ref[0] = _dot_f32(cv * _sigmoid(cv), w_ref[0]) + b_ref[0]

    return pl.pallas_call(
        body, name="ada_fwd", grid=(2,),
        in_specs=[pl.BlockSpec((N_DEV, D_MODEL), lambda l: (0, 0)), pl.BlockSpec((1, D_MODEL, n), lambda l: (l, 0, 0)),
                  pl.BlockSpec((1, 1, n), lambda l: (l, 0, 0))],
        out_specs=pl.BlockSpec((1, N_DEV, n), lambda l: (l, 0, 0)),
        out_shape=_sds((2, N_DEV, n), F32),
        compiler_params=_params(("arbitrary",)),
    )(c_all, w, b)


def _ada_bwd(c_all_t, dmod):
    n = dmod.shape[2]

    def body(c_ref, d_ref, o_ref):
        cv = c_ref[...]
        act = cv * _sigmoid(cv)
        dv = d_ref[0]
        acc = act[:, 0:1] * dv[0:1, :]
        for bb in range(1, N_DEV):
            acc = acc + act[:, bb:bb + 1] * dv[bb:bb + 1, :]
        o_ref[0] = acc

    return pl.pallas_call(
        body, name="ada_bwd", grid=(2,),
        in_specs=[pl.BlockSpec((D_MODEL, N_DEV), lambda l: (0, 0)), pl.BlockSpec((1, N_DEV, n), lambda l: (l, 0, 0))],
        out_specs=pl.BlockSpec((1, D_MODEL, n), lambda l: (l, 0, 0)),
        out_shape=_sds((2, D_MODEL, n), F32),
        compiler_params=_params(("arbitrary",)),
    )(c_all_t, dmod)


ADAM_ROW_TILE = 256


def _adam(parts, w, m, v, name):
    P, R, C = parts.shape
    tr = R if R <= ADAM_ROW_TILE else ADAM_ROW_TILE
    assert R % tr == 0

    def body(p_ref, w_ref, m_ref, v_ref, g_ref, d_ref, nm_ref, nv_ref):
        g = p_ref[0]
        for k in range(1, P):
            g = g + p_ref[k]
        g_ref[...] = g
        m_new = ADAM_B1 * m_ref[...] + (1.0 - ADAM_B1) * g
        v_new = ADAM_B2 * v_ref[...] + (1.0 - ADAM_B2) * jnp.square(g)
        m_hat = m_new / (1.0 - ADAM_B1 ** ADAM_STEP)
        v_hat = v_new / (1.0 - ADAM_B2 ** ADAM_STEP)
        d_ref[...] = -ADAM_LR * (m_hat / (jnp.sqrt(v_hat) + ADAM_EPS) + ADAM_WD * w_ref[...])
        nm_ref[...] = m_new
        nv_ref[...] = v_new

    spec = pl.BlockSpec((tr, C), lambda i: (i, 0))
    return pl.pallas_call(
        body, name=name, grid=(R // tr,),
        in_specs=[pl.BlockSpec((P, tr, C), lambda i: (0, i, 0)), spec, spec, spec],
        out_specs=[spec, spec, spec, spec], out_shape=[_sds((R, C), F32)] * 4,
        compiler_params=_params(("arbitrary",)),
    )(parts, w, m, v)


_ANY = pl.BlockSpec(memory_space=pl.ANY)
CHIP_FLIPS = ((1, 0), (0, 1), (1, 1))
DEV_FLIPS = tuple((dx, dy, dc) for dx in (0, 1) for dy in (0, 1) for dc in (0, 1) if dx + dy + dc)


def _flip(a, d):
    return a if d == 0 else 1 - a


def _my_place():
    return lax.axis_index("x"), lax.axis_index("y"), lax.axis_index("c")


def _gather_dev8(arrs, name):
    n = len(arrs)

    def body(*refs):
        ins, outs = refs[:n], refs[n:2 * n]
        send_sems, recv_sems, loc_sems = refs[2 * n:]
        x, y, c = _my_place()
        me = 4 * x + 2 * y + c
        copies = []
        for a in range(n):
            loc = pltpu.make_async_copy(ins[a], outs[a].at[me], loc_sems.at[a])
            loc.start()
            copies.append(loc)
            for k, (dx, dy, dc) in enumerate(DEV_FLIPS):
                cp = pltpu.make_async_remote_copy(
                    src_ref=ins[a], dst_ref=outs[a].at[me], send_sem=send_sems.at[a, k], recv_sem=recv_sems.at[a, k],
                    device_id=(_flip(x, dx), _flip(y, dy), _flip(c, dc)), device_id_type=MESH_ID)
                cp.start()
                copies.append(cp)
        for cp in copies:
            cp.wait()

    return pl.pallas_call(
        body, name=name, in_specs=[_ANY] * n, out_specs=[_ANY] * n,
        out_shape=[_sds((N_DEV,) + a.shape, a.dtype) for a in arrs],
        scratch_shapes=[pltpu.SemaphoreType.DMA((n, 7)), pltpu.SemaphoreType.DMA((n, 7)), pltpu.SemaphoreType.DMA((n,))],
    )(*arrs)


def _gather_chip4(arrs, name):
    n = len(arrs)

    def body(*refs):
        ins, outs = refs[:n], refs[n:2 * n]
        send_sems, recv_sems, loc_sems = refs[2 * n:]
        x, y, c = _my_place()
        me = 2 * x + y
        copies = []
        for a in range(n):
            loc = pltpu.make_async_copy(ins[a], outs[a].at[me], loc_sems.at[a])
            loc.start()
            copies.append(loc)
            for k, (dx, dy) in enumerate(CHIP_FLIPS):
                cp = pltpu.make_async_remote_copy(
                    src_ref=ins[a], dst_ref=outs[a].at[me], send_sem=send_sems.at[a, k], recv_sem=recv_sems.at[a, k],
                    device_id=(_flip(x, dx), _flip(y, dy), c), device_id_type=MESH_ID)
                cp.start()
                copies.append(cp)
        for cp in copies:
            cp.wait()

    return pl.pallas_call(
        body, name=name, in_specs=[_ANY] * n, out_specs=[_ANY] * n,
        out_shape=[_sds((N_CHIPS,) + a.shape, a.dtype) for a in arrs],
        scratch_shapes=[pltpu.SemaphoreType.DMA((n, 3)), pltpu.SemaphoreType.DMA((n, 3)), pltpu.SemaphoreType.DMA((n,))],
    )(*arrs)


def _reduce_exchange(arrs, name):
    n = len(arrs)

    def body(*refs):
        ins, outs = refs[:n], refs[n:2 * n]
        send_sems, recv_sems, loc_sems = refs[2 * n:]
        x, y, c = _my_place()
        chip = 2 * x + y
        sibling = (x, y, 1 - c)

        def remote(src, slot, a, k, to):
            return pltpu.make_async_remote_copy(src_ref=src, dst_ref=outs[a].at[slot], send_sem=send_sems.at[a, k],
                                                recv_sem=recv_sems.at[a, k], device_id=to, device_id_type=MESH_ID)

        sends, locs = [], []
        for a in range(n):
            loc = pltpu.make_async_copy(ins[a].at[chip], outs[a].at[2 * chip + c], loc_sems.at[a])
            loc.start()
            locs.append(loc)
            first = [remote(ins[a].at[chip], 2 * chip + c, a, 0, sibling)]
            for k, (dx, dy) in enumerate(CHIP_FLIPS):
                px, py = _flip(x, dx), _flip(y, dy)
                first.append(remote(ins[a].at[2 * px + py], 2 * chip + c, a, 1 + k, (px, py, c)))
            for cp in first:
                cp.start()
            sends += first
        for a in range(n):
            for k, (dx, dy) in enumerate(CHIP_FLIPS):
                slot = 2 * (2 * _flip(x, dx) + _flip(y, dy)) + c
                remote(outs[a].at[slot], slot, a, 1 + k, sibling).wait_recv()
                fwd = remote(outs[a].at[slot], slot, a, 4 + k, sibling)
                fwd.start()
                sends.append(fwd)
        for a in range(n):
            remote(outs[a].at[2 * chip + 1 - c], 2 * chip + 1 - c, a, 0, sibling).wait_recv()
            for k, (dx, dy) in enumerate(CHIP_FLIPS):
                slot = 2 * (2 * _flip(x, dx) + _flip(y, dy)) + 1 - c
                remote(outs[a].at[slot], slot, a, 4 + k, sibling).wait_recv()
        for cp in sends:
            cp.wait_send()
        for loc in locs:
            loc.wait()

    return pl.pallas_call(
        body, name=name, in_specs=[_ANY] * n, out_specs=[_ANY] * n,
        out_shape=[_sds((N_DEV,) + a.shape[1:], a.dtype) for a in arrs],
        scratch_shapes=[pltpu.SemaphoreType.DMA((n, 7)), pltpu.SemaphoreType.DMA((n, 7)), pltpu.SemaphoreType.DMA((n,))],
    )(*arrs)


def _pair_order(nheads, nkv):
    group = nheads // nkv
    order = []
    for m in range(nkv // 2):
        for i in range(group):
            order += [2 * m * group + i, (2 * m + 1) * group + i]
    return order


A_ORDER = _pair_order(A_HEADS, A_KV_HEADS)
C_ORDER = _pair_order(C_HEADS, C_KV_HEADS)
A_INV = [int(k) for k in np.argsort(A_ORDER)]
C_INV = [int(k) for k in np.argsort(C_ORDER)]


def _perm_heads(w, order, axis):
    return jnp.concatenate([lax.slice_in_dim(w, HEAD_DIM * h, HEAD_DIM * (h + 1), axis=axis) for h in order], axis=axis)


def _even_in_layout(w):
    return jnp.concatenate([_perm_heads(w[:, 0:512], A_ORDER, 1), w[:, 512:768], _perm_heads(w[:, 768:1280], A_ORDER, 1),
                            w[:, 1280:1696], jnp.zeros((w.shape[0], 96), w.dtype), w[:, 1696:2208]], axis=1)


def _even_in_unlayout(g):
    return jnp.concatenate([_perm_heads(g[:, 0:512], A_INV, 1), g[:, 512:768], _perm_heads(g[:, 768:1280], A_INV, 1),
                            g[:, 1280:1696], g[:, 1792:2304]], axis=1)


def _even_out_layout(w):
    return jnp.concatenate([_perm_heads(w[0:512], A_ORDER, 0), w[512:1024]], axis=0)


def _even_out_unlayout(g):
    return jnp.concatenate([_perm_heads(g[0:512], A_INV, 0), g[512:1024]], axis=0)


def _odd_in_layout(w):
    return jnp.concatenate([_perm_heads(w[:, 0:1024], C_ORDER, 1), w[:, 1024:1536], _perm_heads(w[:, 1536:2560], C_ORDER, 1)],
                           axis=1)


def _odd_in_unlayout(g):
    return jnp.concatenate([_perm_heads(g[:, 0:1024], C_INV, 1), g[:, 1024:1536], _perm_heads(g[:, 1536:2560], C_INV, 1)],
                           axis=1)


def _uq_layout(w):
    per = B_NOPE + B_ROPE
    pad = jnp.zeros((w.shape[0], LANES - B_ROPE), w.dtype)
    nope = [w[:, per * h:per * h + B_NOPE] for h in range(B_HEADS)]
    rope = [jnp.concatenate([w[:, per * h + B_NOPE:per * (h + 1)], pad], axis=1) for h in range(B_HEADS)]
    return jnp.concatenate(nope + rope, axis=1)


def _uq_unlayout(g):
    parts = []
    for h in range(B_HEADS):
        parts += [g[:, B_NOPE * h:B_NOPE * (h + 1)], g[:, 512 + LANES * h:512 + LANES * h + B_ROPE]]
    return jnp.concatenate(parts, axis=1)


def _block_diag(blocks):
    rows = []
    for h, blk in enumerate(blocks):
        r, cdim = blk.shape
        n = len(blocks)
        rows.append(jnp.concatenate([jnp.zeros((r, cdim * h), blk.dtype), blk, jnp.zeros((r, cdim * (n - 1 - h)), blk.dtype)],
                                    axis=1))
    return jnp.concatenate(rows, axis=0)


def _uk_layout(w):
    return _block_diag([w[:, h, :].T for h in range(B_HEADS)])


def _uk_unlayout(g):
    return jnp.stack([g[B_NOPE * h:B_NOPE * (h + 1), LANES * h:LANES * (h + 1)].T for h in range(B_HEADS)], axis=1)


def _uv_layout(w):
    return _block_diag([w[:, h, :] for h in range(B_HEADS)])


def _uv_unlayout(g):
    return jnp.stack([g[LANES * h:LANES * (h + 1), B_V * h:B_V * (h + 1)] for h in range(B_HEADS)], axis=1)


def _rope_tables(S):
    inv = ROPE_THETA ** (-jnp.arange(0, 32, 2, dtype=F32) / 32)
    tok = jnp.arange(S)

    def tab(pos):
        ang = pos.astype(F32)[:, None] * inv[None, :]
        cos, sin = jnp.cos(ang), jnp.sin(ang)
        return jnp.concatenate([cos, cos], axis=1), jnp.concatenate([-sin, sin], axis=1)

    cr, sr = tab(tok // GRID_W)
    cc, sc = tab(tok % GRID_W)
    ct, st = tab(tok)
    return (jnp.tile(jnp.concatenate([cr, cc], axis=1), (1, 2)), jnp.tile(jnp.concatenate([sr, sc], axis=1), (1, 2)),
            jnp.tile(ct, (1, 4)), jnp.tile(st, (1, 4)))


A_TQ, A_TK = 256, 2048
B_TQ, B_TK = 128, 2048
B_BWD_TK = 1024
C_T = 256


def _local_step(x0, tgt, mod, norm_w, wie, wuq, wuk, wuv, woe, wio, woo, a_q_norm, a_k_norm, q_lora_norm, kv_lora_norm,
                c_sink, final_norm):
    S = x0.shape[0]
    mod3 = mod.reshape(2, 3, D_MODEL)
    ca, sa, ct, st = _rope_tables(S)
    lane_seg = np.arange(LANES) // HEAD_DIM
    seg = jnp.asarray((lane_seg[:, None] == lane_seg[None, :]).astype(np.float32))
    qn = jnp.tile(a_q_norm.reshape(1, HEAD_DIM), (1, 2))
    kn = jnp.tile(a_k_norm.reshape(1, HEAD_DIM), (1, 2))
    qln, kvln = q_lora_norm.reshape(1, B_Q_LORA), kv_lora_norm.reshape(1, B_KV_LORA)
    nw0, nw1 = norm_w[0:1], norm_w[1:2]
    gate0, gate1 = mod3[0, 2:3], mod3[1, 2:3]
    a_tq, a_tk, b_tq, b_tk, c_t = min(A_TQ, S), min(A_TK, S), min(B_TQ, S), min(B_TK, S), min(C_T, S)

    h0, proj_e, qa, ka, va, qcat, kcat = _even_pre_fwd(x0, mod3[0], nw0, wie, qn, kn, seg, ca, sa, ct, st, qln, kvln, wuq, wuk)
    oa, lse_a = _pp_fwd(qa, ka, va, kdiv=4, tq=a_tq, tk=a_tk, name="attn_a_fwd")
    olat, lse_b = _mla_fwd(qcat, kcat, tq=b_tq, tk=b_tk)
    y0, x1 = _even_post_fwd(oa, olat, proj_e, x0, gate0, wuv, woe)
    h1, proj_o, qc, kc, vc = _odd_pre_fwd(x1, mod3[1], nw1, wio)
    slopes = 2.0 ** (-8.0 * jnp.arange(1, C_HEADS + 1, dtype=F32) / C_HEADS)
    c_order = np.asarray(C_ORDER)
    slope_rows = jnp.repeat(slopes[c_order].reshape(C_HEADS // 2, 2), c_t, axis=1)[:, :, None]
    sink_rows = jnp.repeat(c_sink.reshape(C_HEADS)[c_order].reshape(C_HEADS // 2, 2), c_t, axis=1)[:, :, None]
    oc, lse_c = _win_fwd(qc, kc, vc, slope_rows, sink_rows, kdiv=4, tq=c_t, name="attn_c_fwd")
    doc, dgc, dx2, dwoo, st_f = _odd_post(oc, proj_o, x1, gate1, woo, final_norm.reshape(1, D_MODEL), tgt)
    dqc, dkc, dvc, dsink_raw = _win_bwd(qc, kc, vc, oc, doc, lse_c, slope_rows, sink_rows, kdiv=4, tq=c_t, name="attn_c_bwd")
    dx1, dwio, st_1 = _odd_pre_bwd(dqc, dkc, dvc, dgc, h1, x1, dx2, mod3[1], nw1, wio)
    doa, dga, dgb, dolat, dwoe, dwuv, st_e = _even_post_bwd(dx1, y0, oa, olat, proj_e, gate0, wuv, woe)
    dqa, dka, dva = _pp_bwd(qa, ka, va, oa, doa, lse_a, kdiv=4, tq=a_tq, tk=a_tk, name="attn_a_bwd")
    dqcat, dkcat = _mla_bwd(qcat, kcat, olat, dolat, lse_b, tq=b_tq, tk=min(B_BWD_TK, S))
    dx0, dwie, dwuq, dwuk, st_0, nst = _even_pre_bwd(x0, h0, proj_e, dqa, dka, dva, dga, dgb, dqcat, dkcat, dx1, mod3[0], nw0,
                                                     wie, qn, kn, seg, ca, sa, ct, st, qln, kvln, wuq, wuk)
    dsink_pairs = jnp.stack([dsink_raw[:, 0, 0], dsink_raw[:, 1, 0]], axis=1).reshape(C_HEADS)
    return dict(
        loss=st_f[2, 0], dx=dx0,
        dmod=jnp.stack([jnp.concatenate([st_0[0], st_0[1], st_e[0]]), jnp.concatenate([st_1[0], st_1[1], st_f[1]])]),
        norm_w=jnp.stack([st_0[2], st_1[2]]), final_norm=st_f[0],
        a_q_norm=nst[0:1, 0:HEAD_DIM], a_k_norm=nst[1:2, 0:HEAD_DIM], b_q_lora_norm=nst[2:3, :], b_kv_lora_norm=nst[3:4, 0:LANES],
        c_sink=dsink_pairs[np.asarray(C_INV)].reshape(1, C_HEADS),
        even_w_in=dwie, b_w_uq=dwuq, b_w_uk=dwuk, b_w_uv=dwuv, even_w_out=dwoe, odd_w_in=dwio, odd_w_out=dwoo)


WEIGHT_NAMES = ("norm_w", "ada_w", "ada_b", "even_w_in", "a_q_norm", "a_k_norm", "b_q_lora_norm", "b_kv_lora_norm", "b_w_uq",
                "b_w_uk", "b_w_uv", "even_w_out", "odd_w_in", "c_sink", "odd_w_out", "final_norm")
SMALL_NAMES = ("dmod", "norm_w", "final_norm", "a_q_norm", "a_k_norm", "b_q_lora_norm", "b_kv_lora_norm", "c_sink")


def _cols_to_chips(g):
    r, n4 = g.shape
    return jnp.transpose(g.reshape(r, N_CHIPS, n4 // N_CHIPS), (1, 0, 2))


def _chips_to_cols(g):
    p, r, n = g.shape
    return jnp.transpose(g, (1, 0, 2)).reshape(r, p * n)


def kernel(x, c, norm_w, ada_w, ada_b, even_w_in, a_q_norm, a_k_norm, b_q_lora_norm, b_kv_lora_norm, b_w_uq, b_w_uk, b_w_uv, even_w_out, odd_w_in, c_sink, odd_w_out, final_norm, loss_target, m_norm_w, m_ada_w, m_ada_b, m_even_w_in, m_a_q_norm, m_a_k_norm, m_b_q_lora_norm, m_b_kv_lora_norm, m_b_w_uq, m_b_w_uk, m_b_w_uv, m_even_w_out, m_odd_w_in, m_c_sink, m_odd_w_out, m_final_norm, v_norm_w, v_ada_w, v_ada_b, v_even_w_in, v_a_q_norm, v_a_k_norm, v_b_q_lora_norm, v_b_kv_lora_norm, v_b_w_uq, v_b_w_uk, v_b_w_uv, v_even_w_out, v_odd_w_in, v_c_sink, v_odd_w_out, v_final_norm):
    given = dict(locals())
    xi, yi, ci = _my_place()
    chip = 2 * xi + yi
    dev = 2 * chip + ci
    n_ada = ada_w.shape[2]

    (c_all,) = _gather_dev8([c], "gather_c")
    c_all = c_all.reshape(N_DEV, D_MODEL)
    bias = lax.dynamic_slice_in_dim(ada_b, chip * n_ada, n_ada, axis=1).reshape(2, 1, n_ada)
    mod_cols = _ada_fwd(c_all, ada_w, bias)
    mod_all, wie_g, wuq_g, woe_g, wio_g, woo_g = _gather_chip4(
        [mod_cols, even_w_in[0].astype(BF16), b_w_uq[0].astype(BF16), even_w_out[0].astype(BF16), odd_w_in[0].astype(BF16),
         odd_w_out[0].astype(BF16)], "gather_weights")
    mod = jnp.transpose(lax.dynamic_index_in_dim(mod_all, dev, axis=2, keepdims=False), (1, 0, 2)).reshape(2, 3 * D_MODEL)

    res = _local_step(
        x[0], loss_target[0], mod, norm_w,
        _even_in_layout(_chips_to_cols(wie_g)), _uq_layout(_chips_to_cols(wuq_g)), _uk_layout(b_w_uk[0].astype(BF16)),
        _uv_layout(b_w_uv[0].astype(BF16)), _even_out_layout(woe_g.reshape(D_MODEL, D_MODEL)), _odd_in_layout(_chips_to_cols(wio_g)),
        _perm_heads(woo_g.reshape(D_MODEL, D_MODEL), C_ORDER, 0), a_q_norm, a_k_norm, b_q_lora_norm, b_kv_lora_norm, c_sink,
        final_norm)

    shard_parts = dict(zip(
        ("even_w_in", "b_w_uq", "even_w_out", "odd_w_in", "odd_w_out"),
        _reduce_exchange(
            [_cols_to_chips(_even_in_unlayout(res["even_w_in"])), _cols_to_chips(_uq_unlayout(res["b_w_uq"])),
             _even_out_unlayout(res["even_w_out"]).reshape(N_CHIPS, D_MODEL // N_CHIPS, D_MODEL),
             _cols_to_chips(_odd_in_unlayout(res["odd_w_in"])),
             _perm_heads(res["odd_w_out"], C_INV, 0).reshape(N_CHIPS, D_MODEL // N_CHIPS, D_MODEL)], "reduce_exchange")))

    small = jnp.concatenate([res[k].reshape(-1) for k in SMALL_NAMES]).reshape(1, -1)
    latent = jnp.stack([_uk_unlayout(res["b_w_uk"]).reshape(B_KV_LORA, 512), _uv_unlayout(res["b_w_uv"]).reshape(B_KV_LORA, 512)])
    small_all, latent_all = _gather_dev8([small, latent], "gather_small")
    small_all = small_all.reshape(N_DEV, -1)
    parts, off = {}, 0
    for k in SMALL_NAMES:
        n = int(np.prod(res[k].shape))
        parts[k] = small_all[:, off:off + n]
        off += n
    dmod_all = parts.pop("dmod").reshape(N_DEV, 2, 3 * D_MODEL)
    dmod_cols = jnp.transpose(lax.dynamic_slice_in_dim(dmod_all, chip * n_ada, n_ada, axis=2), (1, 0, 2))
    parts["ada_w"] = _ada_bwd(c_all.T, dmod_cols).reshape(1, 2 * D_MODEL, n_ada)
    parts["ada_b"] = dmod_all
    parts["b_w_uk"], parts["b_w_uv"] = latent_all[:, 0], latent_all[:, 1]
    parts.update(shard_parts)

    grads, deltas, new_m, new_v = [], [], [], []
    for k in WEIGHT_NAMES:
        w = given[k]
        p = parts[k]
        shape2 = (p.shape[-2], p.shape[-1]) if p.ndim == 3 else (1, p.shape[-1])
        p = p.reshape((p.shape[0],) + shape2)
        outs = _adam(p, w.reshape(shape2), given["m_" + k].reshape(shape2), given["v_" + k].reshape(shape2), "adam_" + k)
        for lst, o in zip((grads, deltas, new_m, new_v), outs):
            lst.append(o.reshape(w.shape))
    loss = lax.psum(res["loss"], ("x", "y", "c"))
    return (loss, res["dx"][None], *grads, *deltas, *new_m, *new_v)
```

```python
import functools

import numpy as np
import jax
import jax.numpy as jnp
from jax import lax
from jax.experimental import pallas as pl
from jax.experimental.pallas import tpu as pltpu

F32 = jnp.float32
BF16 = jnp.bfloat16
HIGHEST = lax.Precision.HIGHEST
MESH_ID = pl.DeviceIdType.MESH

D_MODEL = 1024
HEAD_DIM = 64
GRID_W = 64
EPS = 1e-6
ROPE_THETA = 10000.0
A_HEADS, A_KV_HEADS = 8, 2
B_HEADS, B_NOPE, B_ROPE, B_V = 8, 64, 32, 64
B_Q_LORA, B_KV_LORA = 256, 128
C_HEADS, C_KV_HEADS = 16, 4
WINDOW = 128
EVEN_IN, ODD_IN = 2208, 2560
EVEN_P = 2304
N_CHIPS, N_DEV = 4, 8
LANES = 128
NEG = -1e30
VMEM_LIMIT = 60 * 1024 * 1024

ADAM_LR, ADAM_B1, ADAM_B2, ADAM_EPS, ADAM_WD, ADAM_STEP = 0.001, 0.9, 0.999, 1e-08, 0.01, 10

ROW_TILE = 256


def _dot(a, b):
    return lax.dot_general(a, b, (((1,), (0,)), ((), ())), preferred_element_type=F32)


def _dot_nt(a, b):
    return lax.dot_general(a, b, (((1,), (1,)), ((), ())), preferred_element_type=F32)


def _dot_tn(a, b):
    return lax.dot_general(a, b, (((0,), (0,)), ((), ())), preferred_element_type=F32)


def _dot_f32(a, b):
    return lax.dot_general(a, b, (((1,), (0,)), ((), ())), precision=HIGHEST, preferred_element_type=F32)


def _sigmoid(x):
    return 1.0 / (1.0 + jnp.exp(-x))


def _silu_and_grad(g):
    s = _sigmoid(g)
    return g * s, s * (1.0 + g * (1.0 - s))


def _lane_iota():
    return lax.broadcasted_iota(jnp.int32, (1, LANES), 1)


def _partner(x, lane):
    return jnp.where((lane % 32) < 16, pltpu.roll(x, LANES - 16, 1), pltpu.roll(x, 16, 1))


def _rot(x, cos, sin_signed, lane):
    return x * cos + _partner(x, lane) * sin_signed


def _rot_bwd(dy, cos, sin_signed, lane):
    return dy * cos + _partner(dy * sin_signed, lane)


def _rms(x):
    return lax.rsqrt(jnp.mean(x * x, axis=-1, keepdims=True) + EPS)


def _rms_bwd(x, r, g):
    return r * g - x * (r * r * r) * jnp.mean(x * g, axis=-1, keepdims=True)


def _seg_mean(v, seg_ones):
    return _dot_f32(v, seg_ones) * (1.0 / HEAD_DIM)


def _row_spec(ts, cols):
    return pl.BlockSpec((ts, cols), lambda i: (i, 0))


def _full_spec(shape, single=True):
    nd = len(shape)
    if single:
        return pl.BlockSpec(shape, lambda i: (0,) * nd, pipeline_mode=pl.Buffered(1))
    return pl.BlockSpec(shape, lambda i: (0,) * nd)


def _sds(shape, dtype):
    return jax.ShapeDtypeStruct(shape, dtype)


def _params(sem):
    return pltpu.CompilerParams(dimension_semantics=sem, vmem_limit_bytes=VMEM_LIMIT)


def _even_pre_fwd(x, mod, nw, wie, qn, kn, seg, ca, sa, ct, st, qln, kvln, wuq, wuk):
    S = x.shape[0]
    ts = min(ROW_TILE, S)

    def body(x_ref, mod_ref, nw_ref, wie_ref, qn_ref, kn_ref, seg_ref, ca_ref, sa_ref, ct_ref, st_ref, qln_ref,
             kvln_ref, wuq_ref, wuk_ref, h_ref, proj_ref, qa_ref, ka_ref, va_ref, qcat_ref, kcat_ref):
        xv = x_ref[...]
        h = (xv * _rms(xv) * nw_ref[...]) * (1.0 + mod_ref[1:2, :]) + mod_ref[0:1, :]
        hb = h.astype(BF16)
        h_ref[...] = hb
        proj = _dot(hb, wie_ref[...])
        proj_ref[...] = proj
        lane = _lane_iota()
        ca_v, sa_v, ct_v, st_v = ca_ref[...], sa_ref[...], ct_ref[...], st_ref[...]
        seg_v = seg_ref[...]
        for cb in range(4):
            xc = proj[:, LANES * cb:LANES * (cb + 1)]
            r = lax.rsqrt(_seg_mean(xc * xc, seg_v) + EPS)
            y = _rot(xc * r * qn_ref[...], ca_v, sa_v, lane)
            qa_ref[:, LANES * cb:LANES * (cb + 1)] = (y * 0.125).astype(BF16)
        kc = proj[:, 512:640]
        r = lax.rsqrt(_seg_mean(kc * kc, seg_v) + EPS)
        ka_ref[...] = _rot(kc * r * kn_ref[...], ca_v, sa_v, lane).astype(BF16)
        va_ref[...] = proj[:, 640:768].astype(BF16)
        cq = proj[:, 1280:1536]
        cqn = (cq * _rms(cq) * qln_ref[...]).astype(BF16)
        ckv = proj[:, 1536:1664]
        ckvn = ckv * _rms(ckv) * kvln_ref[...]
        qb = _dot(cqn, wuq_ref[...])
        qlat = _dot(qb[:, 0:512].astype(BF16), wuk_ref[...])
        for hh in range(B_HEADS):
            qcat_ref[hh, :, 0:LANES] = qlat[:, LANES * hh:LANES * (hh + 1)].astype(BF16)
            qr = _rot(qb[:, 512 + LANES * hh:512 + LANES * (hh + 1)], ct_v, st_v, lane)
            qcat_ref[hh, :, LANES:2 * LANES] = qr.astype(BF16)
        kcat_ref[:, 0:LANES] = ckvn.astype(BF16)
        kcat_ref[:, LANES:2 * LANES] = _rot(proj[:, 1664:1792], ct_v, st_v, lane).astype(BF16)

    return pl.pallas_call(
        body, name="even_pre_fwd", grid=(S // ts,),
        in_specs=[_row_spec(ts, D_MODEL), _full_spec((3, D_MODEL)), _full_spec((1, D_MODEL)), _full_spec((D_MODEL, EVEN_P)),
                  _full_spec((1, LANES)), _full_spec((1, LANES)), _full_spec((LANES, LANES)),
                  _row_spec(ts, LANES), _row_spec(ts, LANES), _row_spec(ts, LANES), _row_spec(ts, LANES),
                  _full_spec((1, B_Q_LORA)), _full_spec((1, B_KV_LORA)), _full_spec((B_Q_LORA, 1536)), _full_spec((512, 1024))],
        out_specs=[_row_spec(ts, D_MODEL), _row_spec(ts, EVEN_P), _row_spec(ts, 512), _row_spec(ts, LANES), _row_spec(ts, LANES),
                   pl.BlockSpec((B_HEADS, ts, 2 * LANES), lambda i: (0, i, 0)), _row_spec(ts, 2 * LANES)],
        out_shape=[_sds((S, D_MODEL), BF16), _sds((S, EVEN_P), F32), _sds((S, 512), BF16), _sds((S, LANES), BF16),
                   _sds((S, LANES), BF16), _sds((B_HEADS, S, 2 * LANES), BF16), _sds((S, 2 * LANES), BF16)],
        compiler_params=_params(("arbitrary",)),
    )(x, mod, nw, wie, qn, kn, seg, ca, sa, ct, st, qln, kvln, wuq, wuk)


def _band_bias_mask(s, slope, i, jj, nq, tq, tk):
    row = lax.broadcasted_iota(jnp.int32, (2 * tq, tk), 0)
    t_pos = i * tq + jnp.where(row >= tq, row - tq, row)
    s_pos = jj * tk + lax.broadcasted_iota(jnp.int32, (2 * tq, tk), 1)
    dist = jnp.abs(t_pos - s_pos)
    s = s - slope * dist.astype(F32)
    valid = (dist <= WINDOW) & (jj >= 0) & (jj < nq)
    return jnp.where(valid, s, NEG)


def _band_offset(j):
    return jnp.where(j == 1, -1, 0) + jnp.where(j == 2, 1, 0)


def _pp_fwd(q, k, v, *, kdiv, tq, tk, name, slope=None, sink=None):
    S = q.shape[0]
    nb = q.shape[1] // LANES
    nq = S // tq
    band = slope is not None
    nkv = 3 if band else S // tk
    if band:
        assert tq == tk and WINDOW <= tk

    def kv_map(b, i, j):
        if band:
            return (jnp.clip(i + _band_offset(j), 0, nq - 1), b // kdiv)
        return (j, b // kdiv)

    def body(*refs):
        if band:
            q_ref, k_ref, v_ref, slope_ref, sink_ref, o_ref, lse_ref, qs, m_s, l_s, acc = refs
        else:
            q_ref, k_ref, v_ref, o_ref, lse_ref, qs, m_s, l_s, acc = refs
        i, j = pl.program_id(1), pl.program_id(2)
        lo = _lane_iota() < HEAD_DIM

        @pl.when(j == 0)
        def _():
            qv = q_ref[...]
            zero = jnp.zeros_like(qv)
            qs[0:tq, :] = jnp.where(lo, qv, zero)
            qs[tq:2 * tq, :] = jnp.where(lo, zero, qv)
            m_s[...] = jnp.full((2 * tq, 1), NEG, F32)
            l_s[...] = jnp.zeros((2 * tq, 1), F32)
            acc[...] = jnp.zeros((2 * tq, LANES), F32)

        s = _dot_nt(qs[...], k_ref[...])
        if band:
            s = _band_bias_mask(s, slope_ref[0], i, i + _band_offset(j), nq, tq, tk)
        m_prev = m_s[...]
        m_new = jnp.maximum(m_prev, jnp.max(s, axis=-1, keepdims=True))
        alpha = jnp.exp(m_prev - m_new)
        p = jnp.exp(s - m_new)
        l_s[...] = alpha * l_s[...] + jnp.sum(p, axis=-1, keepdims=True)
        acc[...] = alpha * acc[...] + _dot(p.astype(BF16), v_ref[...])
        m_s[...] = m_new

        @pl.when(j == nkv - 1)
        def _():
            m_f, l_f, a_f = m_s[...], l_s[...], acc[...]
            if band:
                sk = sink_ref[0]
                m_t = jnp.maximum(m_f, sk)
                a = jnp.exp(m_f - m_t)
                l_f = l_f * a + jnp.exp(sk - m_t)
                a_f = a_f * a
                m_f = m_t
            ov = a_f / l_f
            o_ref[...] = jnp.where(lo, ov[0:tq, :], ov[tq:2 * tq, :])
            lse_ref[0, 0] = m_f + jnp.log(l_f)

    in_specs = [pl.BlockSpec((tq, LANES), lambda b, i, j: (i, b)), pl.BlockSpec((tk, LANES), kv_map),
                pl.BlockSpec((tk, LANES), kv_map)]
    args = [q, k, v]
    if band:
        in_specs += [pl.BlockSpec((1, 2 * tq, 1), lambda b, i, j: (b, 0, 0))] * 2
        args += [slope, sink]
    return pl.pallas_call(
        body, name=name, grid=(nb, nq, nkv), in_specs=in_specs,
        out_specs=[pl.BlockSpec((tq, LANES), lambda b, i, j: (i, b)),
                   pl.BlockSpec((1, 1, 2 * tq, 1), lambda b, i, j: (b, i, 0, 0))],
        out_shape=[_sds((S, nb * LANES), F32), _sds((nb, nq, 2 * tq, 1), F32)],
        scratch_shapes=[pltpu.VMEM((2 * tq, LANES), BF16), pltpu.VMEM((2 * tq, 1), F32), pltpu.VMEM((2 * tq, 1), F32),
                        pltpu.VMEM((2 * tq, LANES), F32)],
        compiler_params=_params(("arbitrary", "arbitrary", "arbitrary")),
    )(*args)


def _pp_bwd(q, k, v, o, do, lse, *, kdiv, tq, tk, name, slope=None, sink=None):
    S = q.shape[0]
    nb = q.shape[1] // LANES
    nkb = k.shape[1] // LANES
    nq = S // tq
    band = slope is not None
    nkv = 3 if band else S // tk

    def kv_map(b, i, j):
        if band:
            return (jnp.clip(i + _band_offset(j), 0, nq - 1), b // kdiv)
        return (j, b // kdiv)

    def body(*refs):
        if band:
            (q_ref, k_ref, v_ref, o_ref, do_ref, lse_ref, slope_ref, sink_ref, dq_ref, dk_ref, dv_ref, dsink_ref,
             qs, dos, delta_s, dq_acc) = refs
        else:
            q_ref, k_ref, v_ref, o_ref, do_ref, lse_ref, dq_ref, dk_ref, dv_ref, qs, dos, delta_s, dq_acc = refs
        b, i, j = pl.program_id(0), pl.program_id(1), pl.program_id(2)
        lo = _lane_iota() < HEAD_DIM

        @pl.when((b % kdiv == 0) & (i == 0) & (j == 0))
        def _():
            dk_ref[...] = jnp.zeros((S, LANES), F32)
            dv_ref[...] = jnp.zeros((S, LANES), F32)

        @pl.when(j == 0)
        def _():
            qv = q_ref[...]
            zero = jnp.zeros_like(qv)
            qs[0:tq, :] = jnp.where(lo, qv, zero)
            qs[tq:2 * tq, :] = jnp.where(lo, zero, qv)
            dov = do_ref[...]
            dob = dov.astype(BF16)
            zb = jnp.zeros_like(dob)
            dos[0:tq, :] = jnp.where(lo, dob, zb)
            dos[tq:2 * tq, :] = jnp.where(lo, zb, dob)
            prod = dov * o_ref[...]
            d_lo = jnp.sum(jnp.where(lo, prod, 0.0), axis=-1, keepdims=True)
            d_hi = jnp.sum(jnp.where(lo, 0.0, prod), axis=-1, keepdims=True)
            delta_s[0:tq, :] = d_lo
            delta_s[tq:2 * tq, :] = d_hi
            dq_acc[...] = jnp.zeros((2 * tq, LANES), F32)
            if band:
                @pl.when(i == 0)
                def _():
                    dsink_ref[...] = jnp.zeros((1, 8, LANES), F32)
                lse_v = lse_ref[0, 0]
                sk = sink_ref[0]
                c_lo = -jnp.exp(sk[0:tq] - lse_v[0:tq]) * d_lo
                c_hi = -jnp.exp(sk[tq:2 * tq] - lse_v[tq:2 * tq]) * d_hi
                dsink_ref[0, 0:1, :] += jnp.broadcast_to(jnp.sum(c_lo, axis=0, keepdims=True), (1, LANES))
                dsink_ref[0, 1:2, :] += jnp.broadcast_to(jnp.sum(c_hi, axis=0, keepdims=True), (1, LANES))

        kv = k_ref[...]
        s = _dot_nt(qs[...], kv)
        if band:
            jj = i + _band_offset(j)
            s = _band_bias_mask(s, slope_ref[0], i, jj, nq, tq, tk)
            jc = jnp.clip(jj, 0, nq - 1)
        else:
            jc = j
        p = jnp.exp(s - lse_ref[0, 0])
        dp = _dot_nt(dos[...], v_ref[...])
        ds = (p * (dp - delta_s[...])).astype(BF16)
        rows = pl.ds(pl.multiple_of(jc * tk, tk), tk)
        dv_ref[rows, :] += _dot_tn(p.astype(BF16), dos[...])
        dk_ref[rows, :] += _dot_tn(ds, qs[...])
        dq_acc[...] += _dot(ds, kv)

        @pl.when(j == nkv - 1)
        def _():
            dqv = dq_acc[...]
            dq_ref[...] = jnp.where(lo, dqv[0:tq, :], dqv[tq:2 * tq, :])

    qmap = lambda b, i, j: (i, b)
    in_specs = [pl.BlockSpec((tq, LANES), qmap), pl.BlockSpec((tk, LANES), kv_map), pl.BlockSpec((tk, LANES), kv_map),
                pl.BlockSpec((tq, LANES), qmap), pl.BlockSpec((tq, LANES), qmap),
                pl.BlockSpec((1, 1, 2 * tq, 1), lambda b, i, j: (b, i, 0, 0))]
    args = [q, k, v, o, do, lse]
    out_specs = [pl.BlockSpec((tq, LANES), qmap), pl.BlockSpec((S, LANES), lambda b, i, j: (0, b // kdiv)),
                 pl.BlockSpec((S, LANES), lambda b, i, j: (0, b // kdiv))]
    out_shape = [_sds((S, nb * LANES), F32), _sds((S, nkb * LANES), F32), _sds((S, nkb * LANES), F32)]
    if band:
        in_specs += [pl.BlockSpec((1, 2 * tq, 1), lambda b, i, j: (b, 0, 0))] * 2
        args += [slope, sink]
        out_specs.append(pl.BlockSpec((1, 8, LANES), lambda b, i, j: (b, 0, 0)))
        out_shape.append(_sds((nb, 8, LANES), F32))
    return pl.pallas_call(
        body, name=name, grid=(nb, nq, nkv), in_specs=in_specs, out_specs=out_specs, out_shape=out_shape,
        scratch_shapes=[pltpu.VMEM((2 * tq, LANES), BF16), pltpu.VMEM((2 * tq, LANES), BF16), pltpu.VMEM((2 * tq, 1), F32),
                        pltpu.VMEM((2 * tq, LANES), F32)],
        compiler_params=_params(("arbitrary", "arbitrary", "arbitrary")),
    )(*args)


def _win_rows(i, tq, nk, S):
    k0 = jnp.clip(i * tq - WINDOW, 0, S - nk)
    return k0, pl.ds(pl.multiple_of(k0, LANES), nk)


def _win_scores(qs, kk, slope, i, k0, tq, nk):
    s = _dot_nt(qs, kk)
    row = lax.broadcasted_iota(jnp.int32, (2 * tq, nk), 0)
    t_pos = i * tq + jnp.where(row >= tq, row - tq, row)
    dist = jnp.abs(t_pos - (k0 + lax.broadcasted_iota(jnp.int32, (2 * tq, nk), 1)))
    return jnp.where(dist <= WINDOW, s - slope * dist.astype(F32), NEG)


def _stack_halves(v, lo):
    zero = jnp.zeros_like(v)
    return jnp.concatenate([jnp.where(lo, v, zero), jnp.where(lo, zero, v)], axis=0)


def _win_fwd(q, k, v, slope, sink, *, kdiv, tq, name):
    S = q.shape[0]
    nb = q.shape[1] // LANES
    nq = S // tq
    nk = min(tq + 2 * WINDOW, S)

    def body(q_ref, k_ref, v_ref, slope_ref, sink_ref, o_ref, lse_ref):
        i = pl.program_id(1)
        lo = _lane_iota() < HEAD_DIM
        k0, rows = _win_rows(i, tq, nk, S)
        s = _win_scores(_stack_halves(q_ref[...], lo), k_ref[rows, :], slope_ref[0], i, k0, tq, nk)
        sk = sink_ref[0]
        m = jnp.maximum(jnp.max(s, axis=-1, keepdims=True), sk)
        p = jnp.exp(s - m)
        l = jnp.sum(p, axis=-1, keepdims=True) + jnp.exp(sk - m)
        ov = _dot(p.astype(BF16), v_ref[rows, :]) / l
        o_ref[...] = jnp.where(lo, ov[0:tq, :], ov[tq:2 * tq, :])
        lse_ref[0, 0] = m + jnp.log(l)

    kv_spec = pl.BlockSpec((S, LANES), lambda b, i: (0, b // kdiv))
    row_spec = pl.BlockSpec((1, 2 * tq, 1), lambda b, i: (b, 0, 0))
    return pl.pallas_call(
        body, name=name, grid=(nb, nq),
        in_specs=[pl.BlockSpec((tq, LANES), lambda b, i: (i, b)), kv_spec, kv_spec, row_spec, row_spec],
        out_specs=[pl.BlockSpec((tq, LANES), lambda b, i: (i, b)), pl.BlockSpec((1, 1, 2 * tq, 1), lambda b, i: (b, i, 0, 0))],
        out_shape=[_sds((S, nb * LANES), F32), _sds((nb, nq, 2 * tq, 1), F32)],
        compiler_params=_params(("arbitrary", "arbitrary")),
    )(q, k, v, slope, sink)


def _win_bwd(q, k, v, o, do, lse, slope, sink, *, kdiv, tq, name):
    S = q.shape[0]
    nb = q.shape[1] // LANES
    nkb = k.shape[1] // LANES
    nq = S // tq
    nk = min(tq + 2 * WINDOW, S)

    def body(q_ref, k_ref, v_ref, o_ref, do_ref, lse_ref, slope_ref, sink_ref, dq_ref, dk_ref, dv_ref, dsink_ref):
        b, i = pl.program_id(0), pl.program_id(1)
        lo = _lane_iota() < HEAD_DIM

        @pl.when((b % kdiv == 0) & (i == 0))
        def _():
            dk_ref[...] = jnp.zeros((S, LANES), F32)
            dv_ref[...] = jnp.zeros((S, LANES), F32)

        @pl.when(i == 0)
        def _():
            dsink_ref[...] = jnp.zeros((1, 8, LANES), F32)

        k0, rows = _win_rows(i, tq, nk, S)
        qs = _stack_halves(q_ref[...], lo)
        dov = do_ref[...]
        dos = _stack_halves(dov.astype(BF16), lo)
        prod = dov * o_ref[...]
        d_lo = jnp.sum(jnp.where(lo, prod, 0.0), axis=-1, keepdims=True)
        d_hi = jnp.sum(jnp.where(lo, 0.0, prod), axis=-1, keepdims=True)
        delta = jnp.concatenate([d_lo, d_hi], axis=0)
        lse_v = lse_ref[0, 0]
        c_sink = -jnp.exp(sink_ref[0] - lse_v) * delta
        dsink_ref[0, 0:1, :] += jnp.broadcast_to(jnp.sum(c_sink[0:tq], axis=0, keepdims=True), (1, LANES))
        dsink_ref[0, 1:2, :] += jnp.broadcast_to(jnp.sum(c_sink[tq:2 * tq], axis=0, keepdims=True), (1, LANES))
        kk = k_ref[rows, :]
        p = jnp.exp(_win_scores(qs, kk, slope_ref[0], i, k0, tq, nk) - lse_v)
        ds = (p * (_dot_nt(dos, v_ref[rows, :]) - delta)).astype(BF16)
        dv_ref[rows, :] += _dot_tn(p.astype(BF16), dos)
        dk_ref[rows, :] += _dot_tn(ds, qs)
        dqv = _dot(ds, kk)
        dq_ref[...] = jnp.where(lo, dqv[0:tq, :], dqv[tq:2 * tq, :])

    qmap = lambda b, i: (i, b)
    kv_spec = pl.BlockSpec((S, LANES), lambda b, i: (0, b // kdiv))
    row_spec = pl.BlockSpec((1, 2 * tq, 1), lambda b, i: (b, 0, 0))
    return pl.pallas_call(
        body, name=name, grid=(nb, nq),
        in_specs=[pl.BlockSpec((tq, LANES), qmap), kv_spec, kv_spec, pl.BlockSpec((tq, LANES), qmap), pl.BlockSpec((tq, LANES), qmap),
                  pl.BlockSpec((1, 1, 2 * tq, 1), lambda b, i: (b, i, 0, 0)), row_spec, row_spec],
        out_specs=[pl.BlockSpec((tq, LANES), qmap), kv_spec, kv_spec, pl.BlockSpec((1, 8, LANES), lambda b, i: (b, 0, 0))],
        out_shape=[_sds((S, nb * LANES), F32), _sds((S, nkb * LANES), F32), _sds((S, nkb * LANES), F32), _sds((nb, 8, LANES), F32)],
        compiler_params=_params(("arbitrary", "arbitrary")),
    )(q, k, v, o, do, lse, slope, sink)


MLA_SCALE = (B_NOPE + B_ROPE) ** -0.5


def _mla_fwd(q, kcat, *, tq, tk):
    S = kcat.shape[0]
    nq, nkv = S // tq, S // tk
    R = B_HEADS * tq

    def body(q_ref, k_ref, o_ref, lse_ref, m_s, l_s, acc):
        j = pl.program_id(1)

        @pl.when(j == 0)
        def _():
            m_s[...] = jnp.full((R, 1), NEG, F32)
            l_s[...] = jnp.zeros((R, 1), F32)
            acc[...] = jnp.zeros((R, LANES), F32)

        kv = k_ref[...]
        s = _dot_nt(q_ref[...].reshape(R, 2 * LANES), kv) * MLA_SCALE
        m_prev = m_s[...]
        m_new = jnp.maximum(m_prev, jnp.max(s, axis=-1, keepdims=True))
        alpha = jnp.exp(m_prev - m_new)
        p = jnp.exp(s - m_new)
        l_s[...] = alpha * l_s[...] + jnp.sum(p, axis=-1, keepdims=True)
        acc[...] = alpha * acc[...] + _dot(p.astype(BF16), kv[:, 0:LANES])
        m_s[...] = m_new

        @pl.when(j == nkv - 1)
        def _():
            l_f = l_s[...]
            o_ref[...] = (acc[...] / l_f).reshape(B_HEADS, tq, LANES)
            lse_ref[0] = m_s[...] + jnp.log(l_f)

    return pl.pallas_call(
        body, name="mla_fwd", grid=(nq, nkv),
        in_specs=[pl.BlockSpec((B_HEADS, tq, 2 * LANES), lambda i, j: (0, i, 0)), pl.BlockSpec((tk, 2 * LANES), lambda i, j: (j, 0))],
        out_specs=[pl.BlockSpec((B_HEADS, tq, LANES), lambda i, j: (0, i, 0)), pl.BlockSpec((1, R, 1), lambda i, j: (i, 0, 0))],
        out_shape=[_sds((B_HEADS, S, LANES), F32), _sds((nq, R, 1), F32)],
        scratch_shapes=[pltpu.VMEM((R, 1), F32), pltpu.VMEM((R, 1), F32), pltpu.VMEM((R, LANES), F32)],
        compiler_params=_params(("arbitrary", "arbitrary")),
    )(q, kcat)


def _mla_bwd(q, kcat, o, do, lse, *, tq, tk):
    S = kcat.shape[0]
    nq, nkv = S // tq, S // tk
    R = B_HEADS * tq

    def body(q_ref, k_ref, o_ref, do_ref, lse_ref, dq_ref, dk_ref, dos, delta_s, dq_acc):
        i, j = pl.program_id(0), pl.program_id(1)

        @pl.when((i == 0) & (j == 0))
        def _():
            dk_ref[...] = jnp.zeros((S, 2 * LANES), F32)

        @pl.when(j == 0)
        def _():
            dov = do_ref[...].reshape(R, LANES)
            dos[...] = dov.astype(BF16)
            delta_s[...] = jnp.sum(dov * o_ref[...].reshape(R, LANES), axis=-1, keepdims=True)
            dq_acc[...] = jnp.zeros((R, 2 * LANES), F32)

        kv = k_ref[...]
        qv = q_ref[...].reshape(R, 2 * LANES)
        s = _dot_nt(qv, kv) * MLA_SCALE
        p = jnp.exp(s - lse_ref[0])
        dp = _dot_nt(dos[...], kv[:, 0:LANES])
        ds = (p * (dp - delta_s[...]) * MLA_SCALE).astype(BF16)
        rows = pl.ds(pl.multiple_of(j * tk, tk), tk)
        dk_ref[rows, :] += _dot_tn(ds, qv)
        dk_ref[rows, 0:LANES] += _dot_tn(p.astype(BF16), dos[...])
        dq_acc[...] += _dot(ds, kv)

        @pl.when(j == nkv - 1)
        def _():
            dq_ref[...] = dq_acc[...].reshape(B_HEADS, tq, 2 * LANES)

    return pl.pallas_call(
        body, name="mla_bwd", grid=(nq, nkv),
        in_specs=[pl.BlockSpec((B_HEADS, tq, 2 * LANES), lambda i, j: (0, i, 0)), pl.BlockSpec((tk, 2 * LANES), lambda i, j: (j, 0)),
                  pl.BlockSpec((B_HEADS, tq, LANES), lambda i, j: (0, i, 0)), pl.BlockSpec((B_HEADS, tq, LANES), lambda i, j: (0, i, 0)),
                  pl.BlockSpec((1, R, 1), lambda i, j: (i, 0, 0))],
        out_specs=[pl.BlockSpec((B_HEADS, tq, 2 * LANES), lambda i, j: (0, i, 0)), pl.BlockSpec((S, 2 * LANES), lambda i, j: (0, 0))],
        out_shape=[_sds((B_HEADS, S, 2 * LANES), F32), _sds((S, 2 * LANES), F32)],
        scratch_shapes=[pltpu.VMEM((R, LANES), BF16), pltpu.VMEM((R, 1), F32), pltpu.VMEM((R, 2 * LANES), F32)],
        compiler_params=_params(("arbitrary", "arbitrary")),
    )(q, kcat, o, do, lse)


def _sum_rows(v):
    return jnp.sum(v, axis=0, keepdims=True)


def _norm_mod_bwd(dh, xv, mod_ref, nw_ref, stats_ref):
    r = _rms(xv)
    xn = xv * r
    nw = nw_ref[...]
    stats_ref[0:1, :] += _sum_rows(dh)
    stats_ref[1:2, :] += _sum_rows(dh * (xn * nw))
    dn = dh * (1.0 + mod_ref[1:2, :])
    stats_ref[2:3, :] += _sum_rows(dn * xn)
    return _rms_bwd(xv, r, dn * nw)


def _even_post_fwd(oa, olat, proj, x, gate, wuv, woe):
    S = x.shape[0]
    ts = min(ROW_TILE, S)

    def body(oa_ref, ol_ref, proj_ref, x_ref, gate_ref, wuv_ref, woe_ref, y_ref, x1_ref):
        sa, _ = _silu_and_grad(proj_ref[:, 768:1280])
        sb, _ = _silu_and_grad(proj_ref[:, 1792:2304])
        olc = jnp.concatenate([ol_ref[hh] for hh in range(B_HEADS)], axis=1).astype(BF16)
        ob = _dot(olc, wuv_ref[...])
        mix = jnp.concatenate([oa_ref[...] * sa, ob * sb], axis=1).astype(BF16)
        y = _dot(mix, woe_ref[...])
        y_ref[...] = y
        x1_ref[...] = x_ref[...] + gate_ref[...] * y

    return pl.pallas_call(
        body, name="even_post_fwd", grid=(S // ts,),
        in_specs=[_row_spec(ts, 512), pl.BlockSpec((B_HEADS, ts, LANES), lambda i: (0, i, 0)), _row_spec(ts, EVEN_P),
                  _row_spec(ts, D_MODEL), _full_spec((1, D_MODEL)), _full_spec((1024, 512)), _full_spec((1024, D_MODEL))],
        out_specs=[_row_spec(ts, D_MODEL), _row_spec(ts, D_MODEL)],
        out_shape=[_sds((S, D_MODEL), F32), _sds((S, D_MODEL), F32)],
        compiler_params=_params(("arbitrary",)),
    )(oa, olat, proj, x, gate, wuv, woe)


def _odd_pre_fwd(x, mod, nw, wio):
    S = x.shape[0]
    ts = min(ROW_TILE, S)

    def body(x_ref, mod_ref, nw_ref, wio_ref, h_ref, proj_ref, q_ref, k_ref, v_ref):
        xv = x_ref[...]
        h = (xv * _rms(xv) * nw_ref[...]) * (1.0 + mod_ref[1:2, :]) + mod_ref[0:1, :]
        hb = h.astype(BF16)
        h_ref[...] = hb
        proj = _dot(hb, wio_ref[...])
        proj_ref[...] = proj
        q_ref[...] = (proj[:, 0:1024] * 0.125).astype(BF16)
        k_ref[...] = proj[:, 1024:1280].astype(BF16)
        v_ref[...] = proj[:, 1280:1536].astype(BF16)

    return pl.pallas_call(
        body, name="odd_pre_fwd", grid=(S // ts,),
        in_specs=[_row_spec(ts, D_MODEL), _full_spec((3, D_MODEL)), _full_spec((1, D_MODEL)), _full_spec((D_MODEL, ODD_IN))],
        out_specs=[_row_spec(ts, D_MODEL), _row_spec(ts, ODD_IN), _row_spec(ts, 1024), _row_spec(ts, 256), _row_spec(ts, 256)],
        out_shape=[_sds((S, D_MODEL), BF16), _sds((S, ODD_IN), F32), _sds((S, 1024), BF16), _sds((S, 256), BF16),
                   _sds((S, 256), BF16)],
        compiler_params=_params(("arbitrary",)),
    )(x, mod, nw, wio)


def _odd_post(oc, proj, x1, gate, woo, fw, tgt):
    S = x1.shape[0]
    ts = min(ROW_TILE, S)

    def body(oc_ref, proj_ref, x_ref, gate_ref, woo_ref, fw_ref, tgt_ref, doc_ref, dgc_ref, dx2_ref, dwoo_ref, stats_ref):
        @pl.when(pl.program_id(0) == 0)
        def _():
            dwoo_ref[...] = jnp.zeros((D_MODEL, D_MODEL), F32)
            stats_ref[...] = jnp.zeros((8, D_MODEL), F32)

        ocv = oc_ref[...]
        sg, dsg = _silu_and_grad(proj_ref[:, 1536:2560])
        mix = (ocv * sg).astype(BF16)
        woo_v = woo_ref[...]
        y = _dot(mix, woo_v)
        gate_v = gate_ref[...]
        x2 = x_ref[...] + gate_v * y
        r = _rms(x2)
        fw_v = fw_ref[...]
        xn = x2 * r
        err = xn * fw_v - tgt_ref[...]
        dout = err * (1.0 / D_MODEL)
        dx2 = _rms_bwd(x2, r, dout * fw_v)
        dx2_ref[...] = dx2
        stats_ref[0:1, :] += _sum_rows(dout * xn)
        stats_ref[1:2, :] += _sum_rows(dx2 * y)
        loss_t = 0.5 * jnp.sum(_sum_rows(err * dout), axis=-1, keepdims=True)
        stats_ref[2:3, :] += jnp.broadcast_to(loss_t, (1, D_MODEL))
        dy = (gate_v * dx2).astype(BF16)
        dmix = _dot_nt(dy, woo_v)
        dwoo_ref[...] += _dot_tn(mix, dy)
        doc_ref[...] = dmix * sg
        dgc_ref[...] = dmix * ocv * dsg

    return pl.pallas_call(
        body, name="odd_post", grid=(S // ts,),
        in_specs=[_row_spec(ts, D_MODEL), _row_spec(ts, ODD_IN), _row_spec(ts, D_MODEL), _full_spec((1, D_MODEL)),
                  _full_spec((D_MODEL, D_MODEL)), _full_spec((1, D_MODEL)), _row_spec(ts, D_MODEL)],
        out_specs=[_row_spec(ts, D_MODEL), _row_spec(ts, D_MODEL), _row_spec(ts, D_MODEL),
                   _full_spec((D_MODEL, D_MODEL), single=False), _full_spec((8, D_MODEL), single=False)],
        out_shape=[_sds((S, D_MODEL), F32), _sds((S, D_MODEL), F32), _sds((S, D_MODEL), F32), _sds((D_MODEL, D_MODEL), F32),
                   _sds((8, D_MODEL), F32)],
        compiler_params=_params(("arbitrary",)),
    )(oc, proj, x1, gate, woo, fw, tgt)


def _odd_pre_bwd(dq, dk, dv, dgc, h, x, dx_res, mod, nw, wio):
    S = x.shape[0]
    ts = min(ROW_TILE, S)

    def body(dq_ref, dk_ref, dv_ref, dgc_ref, h_ref, x_ref, dxr_ref, mod_ref, nw_ref, wio_ref, dx_ref, dw_ref, stats_ref):
        @pl.when(pl.program_id(0) == 0)
        def _():
            dw_ref[...] = jnp.zeros((D_MODEL, ODD_IN), F32)
            stats_ref[...] = jnp.zeros((8, D_MODEL), F32)

        dproj = jnp.concatenate([dq_ref[...] * 0.125, dk_ref[...], dv_ref[...], dgc_ref[...]], axis=1).astype(BF16)
        dh = _dot_nt(dproj, wio_ref[...])
        dw_ref[...] += _dot_tn(h_ref[...], dproj)
        dx_ref[...] = dxr_ref[...] + _norm_mod_bwd(dh, x_ref[...], mod_ref, nw_ref, stats_ref)

    return pl.pallas_call(
        body, name="odd_pre_bwd", grid=(S // ts,),
        in_specs=[_row_spec(ts, 1024), _row_spec(ts, 256), _row_spec(ts, 256), _row_spec(ts, 1024), _row_spec(ts, D_MODEL),
                  _row_spec(ts, D_MODEL), _row_spec(ts, D_MODEL), _full_spec((3, D_MODEL)), _full_spec((1, D_MODEL)),
                  _full_spec((D_MODEL, ODD_IN))],
        out_specs=[_row_spec(ts, D_MODEL), _full_spec((D_MODEL, ODD_IN), single=False), _full_spec((8, D_MODEL), single=False)],
        out_shape=[_sds((S, D_MODEL), F32), _sds((D_MODEL, ODD_IN), F32), _sds((8, D_MODEL), F32)],
        compiler_params=_params(("arbitrary",)),
    )(dq, dk, dv, dgc, h, x, dx_res, mod, nw, wio)


def _even_post_bwd(dx1, y, oa, olat, proj, gate, wuv, woe):
    S = dx1.shape[0]
    ts = min(ROW_TILE, S)

    def body(dx_ref, y_ref, oa_ref, ol_ref, proj_ref, gate_ref, wuv_ref, woe_ref,
             doa_ref, dga_ref, dgb_ref, dol_ref, dwoe_ref, dwuv_ref, stats_ref):
        @pl.when(pl.program_id(0) == 0)
        def _():
            dwoe_ref[...] = jnp.zeros((D_MODEL, D_MODEL), F32)
            dwuv_ref[...] = jnp.zeros((1024, 512), F32)
            stats_ref[...] = jnp.zeros((8, D_MODEL), F32)

        dxv = dx_ref[...]
        stats_ref[0:1, :] += _sum_rows(dxv * y_ref[...])
        dy = (gate_ref[...] * dxv).astype(BF16)
        sa, dsa = _silu_and_grad(proj_ref[:, 768:1280])
        sb, dsb = _silu_and_grad(proj_ref[:, 1792:2304])
        olc = jnp.concatenate([ol_ref[hh] for hh in range(B_HEADS)], axis=1).astype(BF16)
        wuv_v = wuv_ref[...]
        ob = _dot(olc, wuv_v)
        oav = oa_ref[...]
        mix = jnp.concatenate([oav * sa, ob * sb], axis=1).astype(BF16)
        dmix = _dot_nt(dy, woe_ref[...])
        dwoe_ref[...] += _dot_tn(mix, dy)
        dma, dmb = dmix[:, 0:512], dmix[:, 512:1024]
        doa_ref[...] = dma * sa
        dga_ref[...] = dma * oav * dsa
        dgb_ref[...] = dmb * ob * dsb
        dob = (dmb * sb).astype(BF16)
        dol = _dot_nt(dob, wuv_v)
        dwuv_ref[...] += _dot_tn(olc, dob)
        for hh in range(B_HEADS):
            dol_ref[hh] = dol[:, LANES * hh:LANES * (hh + 1)]

    head_spec = pl.BlockSpec((B_HEADS, ts, LANES), lambda i: (0, i, 0))
    return pl.pallas_call(
        body, name="even_post_bwd", grid=(S // ts,),
        in_specs=[_row_spec(ts, D_MODEL), _row_spec(ts, D_MODEL), _row_spec(ts, 512), head_spec, _row_spec(ts, EVEN_P),
                  _full_spec((1, D_MODEL)), _full_spec((1024, 512)), _full_spec((1024, D_MODEL))],
        out_specs=[_row_spec(ts, 512), _row_spec(ts, 512), _row_spec(ts, 512), head_spec,
                   _full_spec((D_MODEL, D_MODEL), single=False), _full_spec((1024, 512), single=False),
                   _full_spec((8, D_MODEL), single=False)],
        out_shape=[_sds((S, 512), F32), _sds((S, 512), F32), _sds((S, 512), F32), _sds((B_HEADS, S, LANES), F32),
                   _sds((D_MODEL, D_MODEL), F32), _sds((1024, 512), F32), _sds((8, D_MODEL), F32)],
        compiler_params=_params(("arbitrary",)),
    )(dx1, y, oa, olat, proj, gate, wuv, woe)


EVEN_BWD_ROW_TILE = 128


def _even_pre_bwd(x, h, proj, dqa, dka, dva, dga, dgb, dqcat, dkcat, dx_res, mod, nw, wie, qn, kn, seg, ca, sa, ct, st,
                  qln, kvln, wuq, wuk):
    S = x.shape[0]
    ts = min(EVEN_BWD_ROW_TILE, S)

    def body(x_ref, h_ref, proj_ref, dqa_ref, dka_ref, dva_ref, dga_ref, dgb_ref, dqc_ref, dkc_ref, dxr_ref, mod_ref, nw_ref,
             wie_ref, qn_ref, kn_ref, seg_ref, ca_ref, sa_ref, ct_ref, st_ref, qln_ref, kvln_ref, wuq_ref, wuk_ref,
             dx_ref, dwie_ref, dwuq_ref, dwuk_ref, stats_ref, nstats_ref):
        @pl.when(pl.program_id(0) == 0)
        def _():
            dwie_ref[...] = jnp.zeros((D_MODEL, EVEN_P), F32)
            dwuq_ref[...] = jnp.zeros((B_Q_LORA, 1536), F32)
            dwuk_ref[...] = jnp.zeros((512, 1024), F32)
            stats_ref[...] = jnp.zeros((8, D_MODEL), F32)
            nstats_ref[...] = jnp.zeros((8, 256), F32)

        lane = _lane_iota()
        ca_v, sa_v, ct_v, st_v = ca_ref[...], sa_ref[...], ct_ref[...], st_ref[...]
        seg_v = seg_ref[...]

        def head_norm_bwd(xc, dy, w):
            r = lax.rsqrt(_seg_mean(xc * xc, seg_v) + EPS)
            g = dy * w
            dxc = r * g - xc * (r * r * r) * _seg_mean(xc * g, seg_v)
            return dxc, _sum_rows(dy * (xc * r))

        pieces = []
        dqn = jnp.zeros((1, LANES), F32)
        for cb in range(4):
            sl = slice(LANES * cb, LANES * (cb + 1))
            dy = _rot_bwd(dqa_ref[:, sl] * 0.125, ca_v, sa_v, lane)
            dxc, dw = head_norm_bwd(proj_ref[:, sl], dy, qn_ref[...])
            pieces.append(dxc)
            dqn = dqn + dw
        dxc, dkn = head_norm_bwd(proj_ref[:, 512:640], _rot_bwd(dka_ref[...], ca_v, sa_v, lane), kn_ref[...])
        pieces += [dxc, dva_ref[...], dga_ref[...]]
        nstats_ref[0:1, 0:LANES] += dqn + pltpu.roll(dqn, HEAD_DIM, 1)
        nstats_ref[1:2, 0:LANES] += dkn + pltpu.roll(dkn, HEAD_DIM, 1)

        cq = proj_ref[:, 1280:1536]
        rq = _rms(cq)
        cqn_f = cq * rq
        qln_v = qln_ref[...]
        cqn = (cqn_f * qln_v).astype(BF16)
        wuq_v, wuk_v = wuq_ref[...], wuk_ref[...]
        qnope = _dot(cqn, wuq_v[:, 0:512]).astype(BF16)
        dqlat = jnp.concatenate([dqc_ref[hh, :, 0:LANES] for hh in range(B_HEADS)], axis=1).astype(BF16)
        dqnope = _dot_nt(dqlat, wuk_v)
        dwuk_ref[...] += _dot_tn(qnope, dqlat)
        dqr = [_rot_bwd(dqc_ref[hh, :, LANES:2 * LANES], ct_v, st_v, lane) for hh in range(B_HEADS)]
        dqb = jnp.concatenate([dqnope] + dqr, axis=1).astype(BF16)
        dcqn = _dot_nt(dqb, wuq_v)
        dwuq_ref[...] += _dot_tn(cqn, dqb)
        nstats_ref[2:3, :] += _sum_rows(dcqn * cqn_f)
        dcq = _rms_bwd(cq, rq, dcqn * qln_v)
        ckv = proj_ref[:, 1536:1664]
        rk = _rms(ckv)
        dckvn = dkc_ref[:, 0:LANES]
        nstats_ref[3:4, 0:LANES] += _sum_rows(dckvn * (ckv * rk))
        dckv = _rms_bwd(ckv, rk, dckvn * kvln_ref[...])
        dkr = _rot_bwd(dkc_ref[:, LANES:2 * LANES], ct_v, st_v, lane)
        pieces += [dcq, dckv, dkr, dgb_ref[...]]
        dproj = jnp.concatenate(pieces, axis=1).astype(BF16)
        dh = _dot_nt(dproj, wie_ref[...])
        dwie_ref[...] += _dot_tn(h_ref[...], dproj)
        dx_ref[...] = dxr_ref[...] + _norm_mod_bwd(dh, x_ref[...], mod_ref, nw_ref, stats_ref)

    return pl.pallas_call(
        body, name="even_pre_bwd", grid=(S // ts,),
        in_specs=[_row_spec(ts, D_MODEL), _row_spec(ts, D_MODEL), _row_spec(ts, EVEN_P), _row_spec(ts, 512), _row_spec(ts, LANES),
                  _row_spec(ts, LANES), _row_spec(ts, 512), _row_spec(ts, 512),
                  pl.BlockSpec((B_HEADS, ts, 2 * LANES), lambda i: (0, i, 0)), _row_spec(ts, 2 * LANES), _row_spec(ts, D_MODEL),
                  _full_spec((3, D_MODEL)), _full_spec((1, D_MODEL)), _full_spec((D_MODEL, EVEN_P)),
                  _full_spec((1, LANES)), _full_spec((1, LANES)), _full_spec((LANES, LANES)),
                  _row_spec(ts, LANES), _row_spec(ts, LANES), _row_spec(ts, LANES), _row_spec(ts, LANES),
                  _full_spec((1, B_Q_LORA)), _full_spec((1, B_KV_LORA)), _full_spec((B_Q_LORA, 1536)), _full_spec((512, 1024))],
        out_specs=[_row_spec(ts, D_MODEL), _full_spec((D_MODEL, EVEN_P), single=False), _full_spec((B_Q_LORA, 1536), single=False),
                   _full_spec((512, 1024), single=False), _full_spec((8, D_MODEL), single=False), _full_spec((8, 256), single=False)],
        out_shape=[_sds((S, D_MODEL), F32), _sds((D_MODEL, EVEN_P), F32), _sds((B_Q_LORA, 1536), F32), _sds((512, 1024), F32),
                   _sds((8, D_MODEL), F32), _sds((8, 256), F32)],
        compiler_params=_params(("arbitrary",)),
    )(x, h, proj, dqa, dka, dva, dga, dgb, dqcat, dkcat, dx_res, mod, nw, wie, qn, kn, seg, ca, sa, ct, st, qln, kvln, wuq, wuk)


def _ada_fwd(c_all, w, b):
    n = w.shape[2]

    def body(c_ref, w_ref, b_ref, o_ref):
        cv = c_ref[...]
        o_ref[0] = _dot_f32(cv * _sigmoid(cv), w_ref[0]) + b_ref[0]

    return pl.pallas_call(
        body, name="ada_fwd", grid=(2,),
        in_specs=[pl.BlockSpec((N_DEV, D_MODEL), lambda l: (0, 0)), pl.BlockSpec((1, D_MODEL, n), lambda l: (l, 0, 0)),
                  pl.BlockSpec((1, 1, n), lambda l: (l, 0, 0))],
        out_specs=pl.BlockSpec((1, N_DEV, n), lambda l: (l, 0, 0)),
        out_shape=_sds((2, N_DEV, n), F32),
        compiler_params=_params(("arbitrary",)),
    )(c_all, w, b)


def _ada_bwd(c_all_t, dmod):
    n = dmod.shape[2]

    def body(c_ref, d_ref, o_ref):
        cv = c_ref[...]
        act = cv * _sigmoid(cv)
        dv = d_ref[0]
        acc = act[:, 0:1] * dv[0:1, :]
        for bb in range(1, N_DEV):
            acc = acc + act[:, bb:bb + 1] * dv[bb:bb + 1, :]
        o_ref[0] = acc

    return pl.pallas_call(
        body, name="ada_bwd", grid=(2,),
        in_specs=[pl.BlockSpec((D_MODEL, N_DEV), lambda l: (0, 0)), pl.BlockSpec((1, N_DEV, n), lambda l: (l, 0, 0))],
        out_specs=pl.BlockSpec((1, D_MODEL, n), lambda l: (l, 0, 0)),
        out_shape=_sds((2, D_MODEL, n), F32),
        compiler_params=_params(("arbitrary",)),
    )(c_all_t, dmod)


ADAM_ROW_TILE = 256


def _adam(parts, w, m, v, name):
    P, R, C = parts.shape
    tr = R if R <= ADAM_ROW_TILE else ADAM_ROW_TILE
    assert R % tr == 0

    def body(p_ref, w_ref, m_ref, v_ref, g_ref, d_ref, nm_ref, nv_ref):
        g = p_ref[0].astype(F32)
        for k in range(1, P):
            g = g + p_ref[k].astype(F32)
        g_ref[...] = g
        m_new = ADAM_B1 * m_ref[...] + (1.0 - ADAM_B1) * g
        v_new = ADAM_B2 * v_ref[...] + (1.0 - ADAM_B2) * jnp.square(g)
        m_hat = m_new / (1.0 - ADAM_B1 ** ADAM_STEP)
        v_hat = v_new / (1.0 - ADAM_B2 ** ADAM_STEP)
        d_ref[...] = -ADAM_LR * (m_hat / (jnp.sqrt(v_hat) + ADAM_EPS) + ADAM_WD * w_ref[...])
        nm_ref[...] = m_new
        nv_ref[...] = v_new

    spec = pl.BlockSpec((tr, C), lambda i: (i, 0))
    return pl.pallas_call(
        body, name=name, grid=(R // tr,),
        in_specs=[pl.BlockSpec((P, tr, C), lambda i: (0, i, 0)), spec, spec, spec],
        out_specs=[spec, spec, spec, spec], out_shape=[_sds((R, C), F32)] * 4,
        compiler_params=_params(("arbitrary",)),
    )(parts, w, m, v)


_ANY = pl.BlockSpec(memory_space=pl.ANY)
CHIP_FLIPS = ((1, 0), (0, 1), (1, 1))
DEV_FLIPS = tuple((dx, dy, dc) for dx in (0, 1) for dy in (0, 1) for dc in (0, 1) if dx + dy + dc)


def _flip(a, d):
    return a if d == 0 else 1 - a


def _my_place():
    return lax.axis_index("x"), lax.axis_index("y"), lax.axis_index("c")


def _gather_dev8(arrs, name):
    n = len(arrs)

    def body(*refs):
        ins, outs = refs[:n], refs[n:2 * n]
        send_sems, recv_sems, loc_sems = refs[2 * n:]
        x, y, c = _my_place()
        me = 4 * x + 2 * y + c
        copies = []
        for a in range(n):
            loc = pltpu.make_async_copy(ins[a], outs[a].at[me], loc_sems.at[a])
            loc.start()
            copies.append(loc)
            for k, (dx, dy, dc) in enumerate(DEV_FLIPS):
                cp = pltpu.make_async_remote_copy(
                    src_ref=ins[a], dst_ref=outs[a].at[me], send_sem=send_sems.at[a, k], recv_sem=recv_sems.at[a, k],
                    device_id=(_flip(x, dx), _flip(y, dy), _flip(c, dc)), device_id_type=MESH_ID)
                cp.start()
                copies.append(cp)
        for cp in copies:
            cp.wait()

    return pl.pallas_call(
        body, name=name, in_specs=[_ANY] * n, out_specs=[_ANY] * n,
        out_shape=[_sds((N_DEV,) + a.shape, a.dtype) for a in arrs],
        scratch_shapes=[pltpu.SemaphoreType.DMA((n, 7)), pltpu.SemaphoreType.DMA((n, 7)), pltpu.SemaphoreType.DMA((n,))],
    )(*arrs)


def _gather_chip4(arrs, name):
    n = len(arrs)

    def body(*refs):
        ins, outs = refs[:n], refs[n:2 * n]
        send_sems, recv_sems, loc_sems = refs[2 * n:]
        x, y, c = _my_place()
        me = 2 * x + y
        copies = []
        for a in range(n):
            loc = pltpu.make_async_copy(ins[a], outs[a].at[me], loc_sems.at[a])
            loc.start()
            copies.append(loc)
            for k, (dx, dy) in enumerate(CHIP_FLIPS):
                cp = pltpu.make_async_remote_copy(
                    src_ref=ins[a], dst_ref=outs[a].at[me], send_sem=send_sems.at[a, k], recv_sem=recv_sems.at[a, k],
                    device_id=(_flip(x, dx), _flip(y, dy), c), device_id_type=MESH_ID)
                cp.start()
                copies.append(cp)
        for cp in copies:
            cp.wait()

    return pl.pallas_call(
        body, name=name, in_specs=[_ANY] * n, out_specs=[_ANY] * n,
        out_shape=[_sds((N_CHIPS,) + a.shape, a.dtype) for a in arrs],
        scratch_shapes=[pltpu.SemaphoreType.DMA((n, 3)), pltpu.SemaphoreType.DMA((n, 3)), pltpu.SemaphoreType.DMA((n,))],
    )(*arrs)


def _gather_chip4_halves(arrs, name):
    n = len(arrs)

    def body(*refs):
        ins, outs = refs[:n], refs[n:2 * n]
        send_sems, recv_sems, loc_sems = refs[2 * n:]
        x, y, c = _my_place()
        chip = 2 * x + y
        sibling = (x, y, 1 - c)

        def remote(src, p, half, a, k, to):
            return pltpu.make_async_remote_copy(src_ref=src, dst_ref=outs[a].at[p, half], send_sem=send_sems.at[a, k],
                                                recv_sem=recv_sems.at[a, k], device_id=to, device_id_type=MESH_ID)

        sends, locs = [], []
        for a in range(n):
            loc = pltpu.make_async_copy(ins[a], outs[a].at[chip], loc_sems.at[a])
            loc.start()
            locs.append(loc)
            for k, (dx, dy) in enumerate(CHIP_FLIPS):
                cp = remote(ins[a].at[c], chip, c, a, k, (_flip(x, dx), _flip(y, dy), c))
                cp.start()
                sends.append(cp)
        for a in range(n):
            for k, (dx, dy) in enumerate(CHIP_FLIPS):
                p = 2 * _flip(x, dx) + _flip(y, dy)
                remote(outs[a].at[p, c], p, c, a, k, sibling).wait_recv()
                fwd = remote(outs[a].at[p, c], p, c, a, 3 + k, sibling)
                fwd.start()
                sends.append(fwd)
        for a in range(n):
            for k, (dx, dy) in enumerate(CHIP_FLIPS):
                p = 2 * _flip(x, dx) + _flip(y, dy)
                remote(outs[a].at[p, 1 - c], p, 1 - c, a, 3 + k, sibling).wait_recv()
        for cp in sends:
            cp.wait_send()
        for loc in locs:
            loc.wait()

    return pl.pallas_call(
        body, name=name, in_specs=[_ANY] * n, out_specs=[_ANY] * n,
        out_shape=[_sds((N_CHIPS,) + a.shape, a.dtype) for a in arrs],
        scratch_shapes=[pltpu.SemaphoreType.DMA((n, 6)), pltpu.SemaphoreType.DMA((n, 6)), pltpu.SemaphoreType.DMA((n,))],
    )(*arrs)


def _reduce_exchange(arrs, name):
    n = len(arrs)

    def body(*refs):
        ins, outs = refs[:n], refs[n:2 * n]
        send_sems, recv_sems, loc_sems = refs[2 * n:]
        x, y, c = _my_place()
        chip = 2 * x + y
        sibling = (x, y, 1 - c)

        def remote(src, slot, a, k, to):
            return pltpu.make_async_remote_copy(src_ref=src, dst_ref=outs[a].at[slot], send_sem=send_sems.at[a, k],
                                                recv_sem=recv_sems.at[a, k], device_id=to, device_id_type=MESH_ID)

        sends, locs = [], []
        for a in range(n):
            loc = pltpu.make_async_copy(ins[a].at[chip], outs[a].at[2 * chip + c], loc_sems.at[a])
            loc.start()
            locs.append(loc)
            first = [remote(ins[a].at[chip], 2 * chip + c, a, 0, sibling)]
            for k, (dx, dy) in enumerate(CHIP_FLIPS):
                px, py = _flip(x, dx), _flip(y, dy)
                first.append(remote(ins[a].at[2 * px + py], 2 * chip + c, a, 1 + k, (px, py, c)))
            for cp in first:
                cp.start()
            sends += first
        for a in range(n):
            for k, (dx, dy) in enumerate(CHIP_FLIPS):
                slot = 2 * (2 * _flip(x, dx) + _flip(y, dy)) + c
                remote(outs[a].at[slot], slot, a, 1 + k, sibling).wait_recv()
                fwd = remote(outs[a].at[slot], slot, a, 4 + k, sibling)
                fwd.start()
                sends.append(fwd)
        for a in range(n):
            remote(outs[a].at[2 * chip + 1 - c], 2 * chip + 1 - c, a, 0, sibling).wait_recv()
            for k, (dx, dy) in enumerate(CHIP_FLIPS):
                slot = 2 * (2 * _flip(x, dx) + _flip(y, dy)) + 1 - c
                remote(outs[a].at[slot], slot, a, 4 + k, sibling).wait_recv()
        for cp in sends:
            cp.wait_send()
        for loc in locs:
            loc.wait()

    return pl.pallas_call(
        body, name=name, in_specs=[_ANY] * n, out_specs=[_ANY] * n,
        out_shape=[_sds((N_DEV,) + a.shape[1:], a.dtype) for a in arrs],
        scratch_shapes=[pltpu.SemaphoreType.DMA((n, 7)), pltpu.SemaphoreType.DMA((n, 7)), pltpu.SemaphoreType.DMA((n,))],
    )(*arrs)


def _pair_order(nheads, nkv):
    group = nheads // nkv
    order = []
    for m in range(nkv // 2):
        for i in range(group):
            order += [2 * m * group + i, (2 * m + 1) * group + i]
    return order


A_ORDER = _pair_order(A_HEADS, A_KV_HEADS)
C_ORDER = _pair_order(C_HEADS, C_KV_HEADS)
A_INV = [int(k) for k in np.argsort(A_ORDER)]
C_INV = [int(k) for k in np.argsort(C_ORDER)]


def _perm_heads(w, order, axis):
    return jnp.concatenate([lax.slice_in_dim(w, HEAD_DIM * h, HEAD_DIM * (h + 1), axis=axis) for h in order], axis=axis)


def _even_in_layout(w):
    return jnp.concatenate([_perm_heads(w[:, 0:512], A_ORDER, 1), w[:, 512:768], _perm_heads(w[:, 768:1280], A_ORDER, 1),
                            w[:, 1280:1696], jnp.zeros((w.shape[0], 96), w.dtype), w[:, 1696:2208]], axis=1)


def _even_in_unlayout(g):
    return jnp.concatenate([_perm_heads(g[:, 0:512], A_INV, 1), g[:, 512:768], _perm_heads(g[:, 768:1280], A_INV, 1),
                            g[:, 1280:1696], g[:, 1792:2304]], axis=1)


def _even_out_layout(w):
    return jnp.concatenate([_perm_heads(w[0:512], A_ORDER, 0), w[512:1024]], axis=0)


def _even_out_unlayout(g):
    return jnp.concatenate([_perm_heads(g[0:512], A_INV, 0), g[512:1024]], axis=0)


def _odd_in_layout(w):
    return jnp.concatenate([_perm_heads(w[:, 0:1024], C_ORDER, 1), w[:, 1024:1536], _perm_heads(w[:, 1536:2560], C_ORDER, 1)],
                           axis=1)


def _odd_in_unlayout(g):
    return jnp.concatenate([_perm_heads(g[:, 0:1024], C_INV, 1), g[:, 1024:1536], _perm_heads(g[:, 1536:2560], C_INV, 1)],
                           axis=1)


def _uq_layout(w):
    per = B_NOPE + B_ROPE
    pad = jnp.zeros((w.shape[0], LANES - B_ROPE), w.dtype)
    nope = [w[:, per * h:per * h + B_NOPE] for h in range(B_HEADS)]
    rope = [jnp.concatenate([w[:, per * h + B_NOPE:per * (h + 1)], pad], axis=1) for h in range(B_HEADS)]
    return jnp.concatenate(nope + rope, axis=1)


def _uq_unlayout(g):
    parts = []
    for h in range(B_HEADS):
        parts += [g[:, B_NOPE * h:B_NOPE * (h + 1)], g[:, 512 + LANES * h:512 + LANES * h + B_ROPE]]
    return jnp.concatenate(parts, axis=1)


def _block_diag(blocks):
    rows = []
    for h, blk in enumerate(blocks):
        r, cdim = blk.shape
        n = len(blocks)
        rows.append(jnp.concatenate([jnp.zeros((r, cdim * h), blk.dtype), blk, jnp.zeros((r, cdim * (n - 1 - h)), blk.dtype)],
                                    axis=1))
    return jnp.concatenate(rows, axis=0)


def _uk_layout(w):
    return _block_diag([w[:, h, :].T for h in range(B_HEADS)])


def _uk_unlayout(g):
    return jnp.stack([g[B_NOPE * h:B_NOPE * (h + 1), LANES * h:LANES * (h + 1)].T for h in range(B_HEADS)], axis=1)


def _uv_layout(w):
    return _block_diag([w[:, h, :] for h in range(B_HEADS)])


def _uv_unlayout(g):
    return jnp.stack([g[LANES * h:LANES * (h + 1), B_V * h:B_V * (h + 1)] for h in range(B_HEADS)], axis=1)


def _rope_tables(S):
    inv = ROPE_THETA ** (-jnp.arange(0, 32, 2, dtype=F32) / 32)
    tok = jnp.arange(S)

    def tab(pos):
        ang = pos.astype(F32)[:, None] * inv[None, :]
        cos, sin = jnp.cos(ang), jnp.sin(ang)
        return jnp.concatenate([cos, cos], axis=1), jnp.concatenate([-sin, sin], axis=1)

    cr, sr = tab(tok // GRID_W)
    cc, sc = tab(tok % GRID_W)
    ct, st = tab(tok)
    return (jnp.tile(jnp.concatenate([cr, cc], axis=1), (1, 2)), jnp.tile(jnp.concatenate([sr, sc], axis=1), (1, 2)),
            jnp.tile(ct, (1, 4)), jnp.tile(st, (1, 4)))


A_TQ, A_TK = 256, 2048
B_TQ, B_TK = 128, 2048
B_BWD_TK = 1024
C_T = 256


def _local_step(x0, tgt, mod, norm_w, wie, wuq, wuk, wuv, woe, wio, woo, a_q_norm, a_k_norm, q_lora_norm, kv_lora_norm,
                c_sink, final_norm):
    S = x0.shape[0]
    mod3 = mod.reshape(2, 3, D_MODEL)
    ca, sa, ct, st = _rope_tables(S)
    lane_seg = np.arange(LANES) // HEAD_DIM
    seg = jnp.asarray((lane_seg[:, None] == lane_seg[None, :]).astype(np.float32))
    qn = jnp.tile(a_q_norm.reshape(1, HEAD_DIM), (1, 2))
    kn = jnp.tile(a_k_norm.reshape(1, HEAD_DIM), (1, 2))
    qln, kvln = q_lora_norm.reshape(1, B_Q_LORA), kv_lora_norm.reshape(1, B_KV_LORA)
    nw0, nw1 = norm_w[0:1], norm_w[1:2]
    gate0, gate1 = mod3[0, 2:3], mod3[1, 2:3]
    a_tq, a_tk, b_tq, b_tk, c_t = min(A_TQ, S), min(A_TK, S), min(B_TQ, S), min(B_TK, S), min(C_T, S)

    h0, proj_e, qa, ka, va, qcat, kcat = _even_pre_fwd(x0, mod3[0], nw0, wie, qn, kn, seg, ca, sa, ct, st, qln, kvln, wuq, wuk)
    oa, lse_a = _pp_fwd(qa, ka, va, kdiv=4, tq=a_tq, tk=a_tk, name="attn_a_fwd")
    olat, lse_b = _mla_fwd(qcat, kcat, tq=b_tq, tk=b_tk)
    y0, x1 = _even_post_fwd(oa, olat, proj_e, x0, gate0, wuv, woe)
    h1, proj_o, qc, kc, vc = _odd_pre_fwd(x1, mod3[1], nw1, wio)
    slopes = 2.0 ** (-8.0 * jnp.arange(1, C_HEADS + 1, dtype=F32) / C_HEADS)
    c_order = np.asarray(C_ORDER)
    slope_rows = jnp.repeat(slopes[c_order].reshape(C_HEADS // 2, 2), c_t, axis=1)[:, :, None]
    sink_rows = jnp.repeat(c_sink.reshape(C_HEADS)[c_order].reshape(C_HEADS // 2, 2), c_t, axis=1)[:, :, None]
    oc, lse_c = _win_fwd(qc, kc, vc, slope_rows, sink_rows, kdiv=4, tq=c_t, name="attn_c_fwd")
    doc, dgc, dx2, dwoo, st_f = _odd_post(oc, proj_o, x1, gate1, woo, final_norm.reshape(1, D_MODEL), tgt)
    dqc, dkc, dvc, dsink_raw = _win_bwd(qc, kc, vc, oc, doc, lse_c, slope_rows, sink_rows, kdiv=4, tq=c_t, name="attn_c_bwd")
    dx1, dwio, st_1 = _odd_pre_bwd(dqc, dkc, dvc, dgc, h1, x1, dx2, mod3[1], nw1, wio)
    doa, dga, dgb, dolat, dwoe, dwuv, st_e = _even_post_bwd(dx1, y0, oa, olat, proj_e, gate0, wuv, woe)
    dqa, dka, dva = _pp_bwd(qa, ka, va, oa, doa, lse_a, kdiv=4, tq=a_tq, tk=a_tk, name="attn_a_bwd")
    dqcat, dkcat = _mla_bwd(qcat, kcat, olat, dolat, lse_b, tq=b_tq, tk=min(B_BWD_TK, S))
    dx0, dwie, dwuq, dwuk, st_0, nst = _even_pre_bwd(x0, h0, proj_e, dqa, dka, dva, dga, dgb, dqcat, dkcat, dx1, mod3[0], nw0,
                                                     wie, qn, kn, seg, ca, sa, ct, st, qln, kvln, wuq, wuk)
    dsink_pairs = jnp.stack([dsink_raw[:, 0, 0], dsink_raw[:, 1, 0]], axis=1).reshape(C_HEADS)
    return dict(
        loss=st_f[2, 0], dx=dx0,
        dmod=jnp.stack([jnp.concatenate([st_0[0], st_0[1], st_e[0]]), jnp.concatenate([st_1[0], st_1[1], st_f[1]])]),
        norm_w=jnp.stack([st_0[2], st_1[2]]), final_norm=st_f[0],
        a_q_norm=nst[0:1, 0:HEAD_DIM], a_k_norm=nst[1:2, 0:HEAD_DIM], b_q_lora_norm=nst[2:3, :], b_kv_lora_norm=nst[3:4, 0:LANES],
        c_sink=dsink_pairs[np.asarray(C_INV)].reshape(1, C_HEADS),
        even_w_in=dwie, b_w_uq=dwuq, b_w_uk=dwuk, b_w_uv=dwuv, even_w_out=dwoe, odd_w_in=dwio, odd_w_out=dwoo)


WEIGHT_NAMES = ("norm_w", "ada_w", "ada_b", "even_w_in", "a_q_norm", "a_k_norm", "b_q_lora_norm", "b_kv_lora_norm", "b_w_uq",
                "b_w_uk", "b_w_uv", "even_w_out", "odd_w_in", "c_sink", "odd_w_out", "final_norm")
SMALL_NAMES = ("dmod", "norm_w", "final_norm", "a_q_norm", "a_k_norm", "b_q_lora_norm", "b_kv_lora_norm", "c_sink")


def _cols_to_chips(g):
    r, n4 = g.shape
    return jnp.transpose(g.reshape(r, N_CHIPS, n4 // N_CHIPS), (1, 0, 2))


def _chips_to_cols(g):
    p, r, n = g.shape
    return jnp.transpose(g, (1, 0, 2)).reshape(r, p * n)


def kernel(x, c, norm_w, ada_w, ada_b, even_w_in, a_q_norm, a_k_norm, b_q_lora_norm, b_kv_lora_norm, b_w_uq, b_w_uk, b_w_uv, even_w_out, odd_w_in, c_sink, odd_w_out, final_norm, loss_target, m_norm_w, m_ada_w, m_ada_b, m_even_w_in, m_a_q_norm, m_a_k_norm, m_b_q_lora_norm, m_b_kv_lora_norm, m_b_w_uq, m_b_w_uk, m_b_w_uv, m_even_w_out, m_odd_w_in, m_c_sink, m_odd_w_out, m_final_norm, v_norm_w, v_ada_w, v_ada_b, v_even_w_in, v_a_q_norm, v_a_k_norm, v_b_q_lora_norm, v_b_kv_lora_norm, v_b_w_uq, v_b_w_uk, v_b_w_uv, v_even_w_out, v_odd_w_in, v_c_sink, v_odd_w_out, v_final_norm):
    given = dict(locals())
    xi, yi, ci = _my_place()
    chip = 2 * xi + yi
    dev = 2 * chip + ci
    n_ada = ada_w.shape[2]

    (c_all,) = _gather_dev8([c], "gather_c")
    c_all = c_all.reshape(N_DEV, D_MODEL)
    bias = lax.dynamic_slice_in_dim(ada_b, chip * n_ada, n_ada, axis=1).reshape(2, 1, n_ada)
    mod_cols = _ada_fwd(c_all, ada_w, bias)
    def halves(w):
        return w.astype(BF16).reshape((2, w.shape[0] // 2) + w.shape[1:])

    gathered = _gather_chip4_halves(
        [mod_cols, halves(even_w_in[0]), halves(b_w_uq[0]), halves(even_w_out[0]), halves(odd_w_in[0]), halves(odd_w_out[0])],
        "gather_weights")
    mod_all = gathered[0]
    wie_g, wuq_g, woe_g, wio_g, woo_g = [g.reshape((N_CHIPS, 2 * g.shape[2]) + g.shape[3:]) for g in gathered[1:]]
    mod = jnp.transpose(lax.dynamic_index_in_dim(mod_all, dev, axis=2, keepdims=False), (1, 0, 2)).reshape(2, 3 * D_MODEL)

    res = _local_step(
        x[0], loss_target[0], mod, norm_w,
        _even_in_layout(_chips_to_cols(wie_g)), _uq_layout(_chips_to_cols(wuq_g)), _uk_layout(b_w_uk[0].astype(BF16)),
        _uv_layout(b_w_uv[0].astype(BF16)), _even_out_layout(woe_g.reshape(D_MODEL, D_MODEL)), _odd_in_layout(_chips_to_cols(wio_g)),
        _perm_heads(woo_g.reshape(D_MODEL, D_MODEL), C_ORDER, 0), a_q_norm, a_k_norm, b_q_lora_norm, b_kv_lora_norm, c_sink,
        final_norm)

    shard_parts = dict(zip(
        ("even_w_in", "b_w_uq", "even_w_out", "odd_w_in", "odd_w_out"),
        _reduce_exchange(
            [_cols_to_chips(_even_in_unlayout(res["even_w_in"].astype(BF16))), _cols_to_chips(_uq_unlayout(res["b_w_uq"].astype(BF16))),
             _even_out_unlayout(res["even_w_out"].astype(BF16)).reshape(N_CHIPS, D_MODEL // N_CHIPS, D_MODEL),
             _cols_to_chips(_odd_in_unlayout(res["odd_w_in"].astype(BF16))),
             _perm_heads(res["odd_w_out"].astype(BF16), C_INV, 0).reshape(N_CHIPS, D_MODEL // N_CHIPS, D_MODEL)],
            "reduce_exchange")))

    small = jnp.concatenate([res[k].reshape(-1) for k in SMALL_NAMES]).reshape(1, -1)
    latent = jnp.stack([_uk_unlayout(res["b_w_uk"]).reshape(B_KV_LORA, 512),
                        _uv_unlayout(res["b_w_uv"]).reshape(B_KV_LORA, 512)]).astype(BF16)
    small_all, latent_all = _gather_dev8([small, latent], "gather_small")
    small_all = small_all.reshape(N_DEV, -1)
    parts, off = {}, 0
    for k in SMALL_NAMES:
        n = int(np.prod(res[k].shape))
        parts[k] = small_all[:, off:off + n]
        off += n
    dmod_all = parts.pop("dmod").reshape(N_DEV, 2, 3 * D_MODEL)
    dmod_cols = jnp.transpose(lax.dynamic_slice_in_dim(dmod_all, chip * n_ada, n_ada, axis=2), (1, 0, 2))
    parts["ada_w"] = _ada_bwd(c_all.T, dmod_cols).reshape(1, 2 * D_MODEL, n_ada)
    parts["ada_b"] = dmod_all
    parts["b_w_uk"], parts["b_w_uv"] = latent_all[:, 0], latent_all[:, 1]
    parts.update(shard_parts)

    grads, deltas, new_m, new_v = [], [], [], []
    for k in WEIGHT_NAMES:
        w = given[k]
        p = parts[k]
        shape2 = (p.shape[-2], p.shape[-1]) if p.ndim == 3 else (1, p.shape[-1])
        p = p.reshape((p.shape[0],) + shape2)
        outs = _adam(p, w.reshape(shape2), given["m_" + k].reshape(shape2), given["v_" + k].reshape(shape2), "adam_" + k)
        for lst, o in zip((grads, deltas, new_m, new_v), outs):
            lst.append(o.reshape(w.shape))
    loss = lax.psum(res["loss"], ("x", "y", "c"))
    return (loss, res["dx"][None], *grads, *deltas, *new_m, *new_v)
```

```python
import functools

import numpy as np
import jax
import jax.numpy as jnp
from jax import lax
from jax.experimental import pallas as pl
from jax.experimental.pallas import tpu as pltpu

F32 = jnp.float32
BF16 = jnp.bfloat16
HIGHEST = lax.Precision.HIGHEST
MESH_ID = pl.DeviceIdType.MESH

D_MODEL = 1024
HEAD_DIM = 64
GRID_W = 64
EPS = 1e-6
ROPE_THETA = 10000.0
A_HEADS, A_KV_HEADS = 8, 2
B_HEADS, B_NOPE, B_ROPE, B_V = 8, 64, 32, 64
B_Q_LORA, B_KV_LORA = 256, 128
C_HEADS, C_KV_HEADS = 16, 4
WINDOW = 128
EVEN_IN, ODD_IN = 2208, 2560
EVEN_P = 2304
N_CHIPS, N_DEV = 4, 8
LANES = 128
NEG = -1e30
VMEM_LIMIT = 60 * 1024 * 1024

ADAM_LR, ADAM_B1, ADAM_B2, ADAM_EPS, ADAM_WD, ADAM_STEP = 0.001, 0.9, 0.999, 1e-08, 0.01, 10

ROW_TILE = 256


def _dot(a, b):
    return lax.dot_general(a, b, (((1,), (0,)), ((), ())), preferred_element_type=F32)


def _dot_nt(a, b):
    return lax.dot_general(a, b, (((1,), (1,)), ((), ())), preferred_element_type=F32)


def _dot_tn(a, b):
    return lax.dot_general(a, b, (((0,), (0,)), ((), ())), preferred_element_type=F32)


def _dot_f32(a, b):
    return lax.dot_general(a, b, (((1,), (0,)), ((), ())), precision=HIGHEST, preferred_element_type=F32)


def _sigmoid(x):
    return 1.0 / (1.0 + jnp.exp(-x))


def _silu_and_grad(g):
    s = _sigmoid(g)
    return g * s, s * (1.0 + g * (1.0 - s))


def _lane_iota():
    return lax.broadcasted_iota(jnp.int32, (1, LANES), 1)


def _partner(x, lane):
    return jnp.where((lane % 32) < 16, pltpu.roll(x, LANES - 16, 1), pltpu.roll(x, 16, 1))


def _rot(x, cos, sin_signed, lane):
    return x * cos + _partner(x, lane) * sin_signed


def _rot_bwd(dy, cos, sin_signed, lane):
    return dy * cos + _partner(dy * sin_signed, lane)


def _rms(x):
    return lax.rsqrt(jnp.mean(x * x, axis=-1, keepdims=True) + EPS)


def _rms_bwd(x, r, g):
    return r * g - x * (r * r * r) * jnp.mean(x * g, axis=-1, keepdims=True)


def _seg_mean(v, seg_ones):
    return _dot_f32(v, seg_ones) * (1.0 / HEAD_DIM)


def _row_spec(ts, cols):
    return pl.BlockSpec((ts, cols), lambda i: (i, 0))


def _full_spec(shape, single=True):
    nd = len(shape)
    if single:
        return pl.BlockSpec(shape, lambda i: (0,) * nd, pipeline_mode=pl.Buffered(1))
    return pl.BlockSpec(shape, lambda i: (0,) * nd)


def _sds(shape, dtype):
    return jax.ShapeDtypeStruct(shape, dtype)


def _params(sem):
    return pltpu.CompilerParams(dimension_semantics=sem, vmem_limit_bytes=VMEM_LIMIT)


def _even_pre_fwd(x, mod, nw, wie, qn, kn, seg, ca, sa, ct, st, qln, kvln, wuq, wuk):
    S = x.shape[0]
    ts = min(ROW_TILE, S)

    def body(x_ref, mod_ref, nw_ref, wie_ref, qn_ref, kn_ref, seg_ref, ca_ref, sa_ref, ct_ref, st_ref, qln_ref,
             kvln_ref, wuq_ref, wuk_ref, h_ref, proj_ref, qa_ref, ka_ref, va_ref, qcat_ref, kcat_ref):
        xv = x_ref[...]
        h = (xv * _rms(xv) * nw_ref[...]) * (1.0 + mod_ref[1:2, :]) + mod_ref[0:1, :]
        hb = h.astype(BF16)
        h_ref[...] = hb
        proj = _dot(hb, wie_ref[...])
        proj_ref[...] = proj
        lane = _lane_iota()
        ca_v, sa_v, ct_v, st_v = ca_ref[...], sa_ref[...], ct_ref[...], st_ref[...]
        seg_v = seg_ref[...]
        for cb in range(4):
            xc = proj[:, LANES * cb:LANES * (cb + 1)]
            r = lax.rsqrt(_seg_mean(xc * xc, seg_v) + EPS)
            y = _rot(xc * r * qn_ref[...], ca_v, sa_v, lane)
            qa_ref[:, LANES * cb:LANES * (cb + 1)] = (y * 0.125).astype(BF16)
        kc = proj[:, 512:640]
        r = lax.rsqrt(_seg_mean(kc * kc, seg_v) + EPS)
        ka_ref[...] = _rot(kc * r * kn_ref[...], ca_v, sa_v, lane).astype(BF16)
        va_ref[...] = proj[:, 640:768].astype(BF16)
        cq = proj[:, 1280:1536]
        cqn = (cq * _rms(cq) * qln_ref[...]).astype(BF16)
        ckv = proj[:, 1536:1664]
        ckvn = ckv * _rms(ckv) * kvln_ref[...]
        qb = _dot(cqn, wuq_ref[...])
        qlat = _dot(qb[:, 0:512].astype(BF16), wuk_ref[...])
        for hh in range(B_HEADS):
            qcat_ref[hh, :, 0:LANES] = qlat[:, LANES * hh:LANES * (hh + 1)].astype(BF16)
            qr = _rot(qb[:, 512 + LANES * hh:512 + LANES * (hh + 1)], ct_v, st_v, lane)
            qcat_ref[hh, :, LANES:2 * LANES] = qr.astype(BF16)
        kcat_ref[:, 0:LANES] = ckvn.astype(BF16)
        kcat_ref[:, LANES:2 * LANES] = _rot(proj[:, 1664:1792], ct_v, st_v, lane).astype(BF16)

    return pl.pallas_call(
        body, name="even_pre_fwd", grid=(S // ts,),
        in_specs=[_row_spec(ts, D_MODEL), _full_spec((3, D_MODEL)), _full_spec((1, D_MODEL)), _full_spec((D_MODEL, EVEN_P)),
                  _full_spec((1, LANES)), _full_spec((1, LANES)), _full_spec((LANES, LANES)),
                  _row_spec(ts, LANES), _row_spec(ts, LANES), _row_spec(ts, LANES), _row_spec(ts, LANES),
                  _full_spec((1, B_Q_LORA)), _full_spec((1, B_KV_LORA)), _full_spec((B_Q_LORA, 1536)), _full_spec((512, 1024))],
        out_specs=[_row_spec(ts, D_MODEL), _row_spec(ts, EVEN_P), _row_spec(ts, 512), _row_spec(ts, LANES), _row_spec(ts, LANES),
                   pl.BlockSpec((B_HEADS, ts, 2 * LANES), lambda i: (0, i, 0)), _row_spec(ts, 2 * LANES)],
        out_shape=[_sds((S, D_MODEL), BF16), _sds((S, EVEN_P), F32), _sds((S, 512), BF16), _sds((S, LANES), BF16),
                   _sds((S, LANES), BF16), _sds((B_HEADS, S, 2 * LANES), BF16), _sds((S, 2 * LANES), BF16)],
        compiler_params=_params(("arbitrary",)),
    )(x, mod, nw, wie, qn, kn, seg, ca, sa, ct, st, qln, kvln, wuq, wuk)


MLA_SCALE = (B_NOPE + B_ROPE) ** -0.5
LOG2E = 1.4426950408889634

def _row_lo():
    return lax.broadcasted_iota(jnp.int32, (LANES, 1), 0) < HEAD_DIM


def _stack_cols(vT, rlo):
    zero = jnp.zeros_like(vT)
    return jnp.concatenate([jnp.where(rlo, vT, zero), jnp.where(rlo, zero, vT)], axis=1)


def _stack_rows(v, lo):
    zero = jnp.zeros_like(v)
    return jnp.concatenate([jnp.where(lo, v, zero), jnp.where(lo, zero, v)], axis=0)


def _pick_halves_T(xT, rlo, t):
    return jnp.where(rlo, xT[:, 0:t], xT[:, t:2 * t]).T


def _pp_fwd(q, k, vT, *, kdiv, tq, tk, sub, name):
    S = k.shape[0]; nb = q.shape[1] // LANES; nq = S // tq; nkv = S // tk; nsub = tk // sub

    def body(q_ref, k_ref, vT_ref, o_ref, lse_ref, qs, m_s, l_s, acc):
        j = pl.program_id(2)
        rlo = _row_lo()

        @pl.when(j == 0)
        def _():
            qs[...] = _stack_cols(q_ref[...].astype(F32).T, rlo).astype(BF16)
            m_s[...] = jnp.full((1, 2 * tq), NEG, F32)
            l_s[...] = jnp.zeros((1, 2 * tq), F32)
            acc[...] = jnp.zeros((LANES, 2 * tq), F32)

        qsv = qs[...]
        m, l, a = m_s[...], l_s[...], acc[...]
        s_cur = _dot(k_ref[0:sub, :], qsv)
        for t in range(nsub):
            if t + 1 < nsub:
                s_next = _dot(k_ref[sub * (t + 1):sub * (t + 2), :], qsv)
            m_new = jnp.maximum(m, jnp.max(s_cur, axis=0, keepdims=True))
            alpha = jnp.exp(m - m_new)
            p = jnp.exp(s_cur - m_new)
            l = alpha * l + jnp.sum(p, axis=0, keepdims=True)
            a = alpha * a + _dot(vT_ref[:, sub * t:sub * (t + 1)], p.astype(BF16))
            m = m_new
            if t + 1 < nsub:
                s_cur = s_next
        m_s[...], l_s[...], acc[...] = m, l, a

        @pl.when(j == nkv - 1)
        def _():
            l_f = l_s[...]
            o_ref[...] = _pick_halves_T(acc[...] / l_f, rlo, tq)
            lse_ref[0, 0] = m_s[...] + jnp.log(l_f)

    return pl.pallas_call(
        body, name=name, grid=(nb, nq, nkv),
        in_specs=[pl.BlockSpec((tq, LANES), lambda b, i, j: (i, b)), pl.BlockSpec((tk, LANES), lambda b, i, j: (j, b // kdiv)),
                  pl.BlockSpec((LANES, tk), lambda b, i, j: (b // kdiv, j))],
        out_specs=[pl.BlockSpec((tq, LANES), lambda b, i, j: (i, b)), pl.BlockSpec((1, 1, 1, 2 * tq), lambda b, i, j: (b, i, 0, 0))],
        out_shape=[_sds((S, nb * LANES), F32), _sds((nb, nq, 1, 2 * tq), F32)],
        scratch_shapes=[pltpu.VMEM((LANES, 2 * tq), BF16), pltpu.VMEM((1, 2 * tq), F32), pltpu.VMEM((1, 2 * tq), F32),
                        pltpu.VMEM((LANES, 2 * tq), F32)],
        compiler_params=_params(("arbitrary",) * 3))(q, k, vT)


def _pp_bwd(q, k, kT, v, o, do, lse, *, kdiv, tq, tk, sub, name):
    S = k.shape[0]; nb = q.shape[1] // LANES; nkb = k.shape[1] // LANES; nq = S // tq; nkv = S // tk; nsub = tk // sub

    def body(q_ref, k_ref, kT_ref, v_ref, o_ref, do_ref, lse_ref, dq_ref, dk_ref, dv_ref, qsT, qs, dosT, dos, delta_s, dq_acc):
        b, i, j = pl.program_id(0), pl.program_id(1), pl.program_id(2)
        rlo = _row_lo()
        lo = lax.broadcasted_iota(jnp.int32, (1, LANES), 1) < HEAD_DIM

        @pl.when((b % kdiv == 0) & (i == 0) & (j == 0))
        def _():
            dk_ref[...] = jnp.zeros((S, LANES), F32)
            dv_ref[...] = jnp.zeros((S, LANES), F32)

        @pl.when(j == 0)
        def _():
            qv = q_ref[...]
            qs[...] = _stack_rows(qv, lo)
            qsT[...] = _stack_cols(qv.astype(F32).T, rlo).astype(BF16)
            dov = do_ref[...]
            dos[...] = _stack_rows(dov.astype(BF16), lo)
            dosT[...] = _stack_cols(dov.T, rlo).astype(BF16)
            prodT = (dov * o_ref[...]).T
            delta_s[...] = jnp.concatenate([jnp.sum(jnp.where(rlo, prodT, 0.0), axis=0, keepdims=True),
                                            jnp.sum(jnp.where(rlo, 0.0, prodT), axis=0, keepdims=True)], axis=1)
            dq_acc[...] = jnp.zeros((LANES, 2 * tq), F32)

        qsTv, dosTv, qsv, dosv = qsT[...], dosT[...], qs[...], dos[...]
        lse_v, delta_v = lse_ref[0, 0], delta_s[...]
        dqa = dq_acc[...]
        s_cur = _dot(k_ref[0:sub, :], qsTv)
        dp_cur = _dot(v_ref[0:sub, :], dosTv)
        for t in range(nsub):
            if t + 1 < nsub:
                s_next = _dot(k_ref[sub * (t + 1):sub * (t + 2), :], qsTv)
                dp_next = _dot(v_ref[sub * (t + 1):sub * (t + 2), :], dosTv)
            p = jnp.exp(s_cur - lse_v)
            ds = (p * (dp_cur - delta_v)).astype(BF16)
            rows = pl.ds(pl.multiple_of(j * tk + sub * t, sub), sub)
            dv_ref[rows, :] += _dot(p.astype(BF16), dosv)
            dk_ref[rows, :] += _dot(ds, qsv)
            dqa = dqa + _dot(kT_ref[:, sub * t:sub * (t + 1)], ds)
            if t + 1 < nsub:
                s_cur, dp_cur = s_next, dp_next
        dq_acc[...] = dqa

        @pl.when(j == nkv - 1)
        def _():
            dq_ref[...] = _pick_halves_T(dq_acc[...], rlo, tq)

    qmap = lambda b, i, j: (i, b)
    kmap = lambda b, i, j: (j, b // kdiv)
    res = lambda b, i, j: (0, b // kdiv)
    return pl.pallas_call(
        body, name=name, grid=(nb, nq, nkv),
        in_specs=[pl.BlockSpec((tq, LANES), qmap), pl.BlockSpec((tk, LANES), kmap), pl.BlockSpec((LANES, tk), lambda b, i, j: (b // kdiv, j)),
                  pl.BlockSpec((tk, LANES), kmap), pl.BlockSpec((tq, LANES), qmap), pl.BlockSpec((tq, LANES), qmap),
                  pl.BlockSpec((1, 1, 1, 2 * tq), lambda b, i, j: (b, i, 0, 0))],
        out_specs=[pl.BlockSpec((tq, LANES), qmap), pl.BlockSpec((S, LANES), res), pl.BlockSpec((S, LANES), res)],
        out_shape=[_sds((S, nb * LANES), F32), _sds((S, nkb * LANES), F32), _sds((S, nkb * LANES), F32)],
        scratch_shapes=[pltpu.VMEM((LANES, 2 * tq), BF16), pltpu.VMEM((2 * tq, LANES), BF16), pltpu.VMEM((LANES, 2 * tq), BF16),
                        pltpu.VMEM((2 * tq, LANES), BF16), pltpu.VMEM((1, 2 * tq), F32), pltpu.VMEM((LANES, 2 * tq), F32)],
        compiler_params=_params(("arbitrary",) * 3))(q, k, kT, v, o, do, lse)


MLA_C = MLA_SCALE * LOG2E


def _mla_fwd(q, kcat, kcatT, *, tq, tk, sub):
    S = kcat.shape[0]; nq, nkv = S // tq, S // tk; R = B_HEADS * tq; nsub = tk // sub

    def body(q_ref, k_ref, vT_ref, o_ref, lse_ref, qT, m_s, l_s, acc):
        j = pl.program_id(1)

        @pl.when(j == 0)
        def _():
            qT[...] = q_ref[...].reshape(R, 2 * LANES).astype(F32).T.astype(BF16)
            m_s[...] = jnp.full((1, R), NEG, F32)
            l_s[...] = jnp.zeros((1, R), F32)
            acc[...] = jnp.zeros((LANES, R), F32)

        qTv = qT[...]
        m, l, a = m_s[...], l_s[...], acc[...]
        s_cur = _dot(k_ref[0:sub, :], qTv)
        for t in range(nsub):
            if t + 1 < nsub:
                s_next = _dot(k_ref[sub * (t + 1):sub * (t + 2), :], qTv)
            m_new = jnp.maximum(m, jnp.max(s_cur, axis=0, keepdims=True))
            alpha = jnp.exp2((m - m_new) * MLA_C)
            p = jnp.exp2((s_cur - m_new) * MLA_C)
            l = alpha * l + jnp.sum(p, axis=0, keepdims=True)
            a = alpha * a + _dot(vT_ref[:, sub * t:sub * (t + 1)], p.astype(BF16))
            m = m_new
            if t + 1 < nsub:
                s_cur = s_next
        m_s[...], l_s[...], acc[...] = m, l, a

        @pl.when(j == nkv - 1)
        def _():
            l_f = l_s[...]
            o_ref[...] = (acc[...] / l_f).T.reshape(B_HEADS, tq, LANES)
            lse_ref[0] = m_s[...] * MLA_SCALE + jnp.log(l_f)

    return pl.pallas_call(
        body, name="mla_fwd", grid=(nq, nkv),
        in_specs=[pl.BlockSpec((B_HEADS, tq, 2 * LANES), lambda i, j: (0, i, 0)), pl.BlockSpec((tk, 2 * LANES), lambda i, j: (j, 0)),
                  pl.BlockSpec((LANES, tk), lambda i, j: (0, j))],
        out_specs=[pl.BlockSpec((B_HEADS, tq, LANES), lambda i, j: (0, i, 0)), pl.BlockSpec((1, 1, R), lambda i, j: (i, 0, 0))],
        out_shape=[_sds((B_HEADS, S, LANES), F32), _sds((nq, 1, R), F32)],
        scratch_shapes=[pltpu.VMEM((2 * LANES, R), BF16), pltpu.VMEM((1, R), F32), pltpu.VMEM((1, R), F32), pltpu.VMEM((LANES, R), F32)],
        compiler_params=_params(("arbitrary", "arbitrary")))(q, kcat, kcatT)


def _mla_bwd(q, kcat, kcatT, o, do, lse, *, tq, tk, sub):
    S = kcat.shape[0]; nq, nkv = S // tq, S // tk; R = B_HEADS * tq; nsub = tk // sub

    def body(q_ref, k_ref, kT_ref, o_ref, do_ref, lse_ref, dq_ref, dk_ref, qT, dosT, dos, delta_s, dq_acc):
        i, j = pl.program_id(0), pl.program_id(1)

        @pl.when((i == 0) & (j == 0))
        def _():
            dk_ref[...] = jnp.zeros((S, 2 * LANES), F32)

        @pl.when(j == 0)
        def _():
            qT[...] = q_ref[...].reshape(R, 2 * LANES).astype(F32).T.astype(BF16)
            dov = do_ref[...].reshape(R, LANES)
            dos[...] = dov.astype(BF16)
            dosT[...] = dov.T.astype(BF16)
            delta_s[...] = jnp.sum((dov * o_ref[...].reshape(R, LANES)).T, axis=0, keepdims=True)
            dq_acc[...] = jnp.zeros((2 * LANES, R), F32)

        qTv, dosTv, dosv = qT[...], dosT[...], dos[...]
        qv = q_ref[...].reshape(R, 2 * LANES)
        lse_v, delta_v = lse_ref[0] * LOG2E, delta_s[...]
        dqa = dq_acc[...]
        s_cur = _dot(k_ref[0:sub, :], qTv)
        dp_cur = _dot(k_ref[0:sub, 0:LANES], dosTv)
        for t in range(nsub):
            if t + 1 < nsub:
                s_next = _dot(k_ref[sub * (t + 1):sub * (t + 2), :], qTv)
                dp_next = _dot(k_ref[sub * (t + 1):sub * (t + 2), 0:LANES], dosTv)
            p = jnp.exp2(s_cur * MLA_C - lse_v)
            ds = (p * (dp_cur - delta_v) * MLA_SCALE).astype(BF16)
            rows = pl.ds(pl.multiple_of(j * tk + sub * t, sub), sub)
            dk_ref[rows, :] += _dot(ds, qv)
            dk_ref[rows, 0:LANES] += _dot(p.astype(BF16), dosv)
            dqa = dqa + _dot(kT_ref[:, sub * t:sub * (t + 1)], ds)
            if t + 1 < nsub:
                s_cur, dp_cur = s_next, dp_next
        dq_acc[...] = dqa

        @pl.when(j == nkv - 1)
        def _():
            dq_ref[...] = dq_acc[...].T.reshape(B_HEADS, tq, 2 * LANES)

    hspec = lambda w: pl.BlockSpec((B_HEADS, tq, w), lambda i, j: (0, i, 0))
    return pl.pallas_call(
        body, name="mla_bwd", grid=(nq, nkv),
        in_specs=[hspec(2 * LANES), pl.BlockSpec((tk, 2 * LANES), lambda i, j: (j, 0)), pl.BlockSpec((2 * LANES, tk), lambda i, j: (0, j)),
                  hspec(LANES), hspec(LANES), pl.BlockSpec((1, 1, R), lambda i, j: (i, 0, 0))],
        out_specs=[hspec(2 * LANES), pl.BlockSpec((S, 2 * LANES), lambda i, j: (0, 0))],
        out_shape=[_sds((B_HEADS, S, 2 * LANES), F32), _sds((S, 2 * LANES), F32)],
        scratch_shapes=[pltpu.VMEM((2 * LANES, R), BF16), pltpu.VMEM((LANES, R), BF16), pltpu.VMEM((R, LANES), BF16),
                        pltpu.VMEM((1, R), F32), pltpu.VMEM((2 * LANES, R), F32)],
        compiler_params=_params(("arbitrary", "arbitrary")))(q, kcat, kcatT, o, do, lse)


def _win_start(i, tq, nk, S):
    return pl.multiple_of(jnp.clip(i * tq - WINDOW, 0, S - nk), LANES)


def _win_dist_table(S, tq):
    nk = min(tq + 2 * WINDOW, S)
    nq = S // tq
    r = np.arange(nk)[:, None]
    c = (np.arange(2 * tq) % tq)[None, :]
    tabs = []
    for rel in (0, WINDOW, (nq - 1) * tq - (S - nk)):
        dist = np.abs(rel + c - r).astype(np.float32)
        tabs.append(np.where(dist <= WINDOW, dist, np.float32(1e32)))
    return jnp.asarray(np.stack(tabs))


def _win_dist_spec(nk, tq, nq):
    return pl.BlockSpec((1, nk, 2 * tq), lambda b, i: (jnp.where(i == 0, 0, jnp.where(i == nq - 1, 2, 1)), 0, 0))


def _win_fwd(q, k, vT, dist, slope, sink, *, kdiv, tq, name):
    S = k.shape[0]; nb = q.shape[1] // LANES; nq = S // tq; nk = min(tq + 2 * WINDOW, S)

    def body(q_ref, k_ref, vT_ref, dist_ref, slope_ref, sink_ref, o_ref, lse_ref):
        i = pl.program_id(1)
        rlo = _row_lo()
        k0 = _win_start(i, tq, nk, S)
        qsT = _stack_cols(q_ref[...].astype(F32).T, rlo).astype(BF16)
        s = _dot(k_ref[pl.ds(k0, nk), :], qsT) - slope_ref[0] * dist_ref[0]
        sk = sink_ref[0]
        m = jnp.maximum(jnp.max(s, axis=0, keepdims=True), sk)
        p = jnp.exp(s - m)
        l = jnp.sum(p, axis=0, keepdims=True) + jnp.exp(sk - m)
        o_ref[...] = _pick_halves_T(_dot(vT_ref[:, pl.ds(k0, nk)], p.astype(BF16)) / l, rlo, tq)
        lse_ref[0, 0] = m + jnp.log(l)

    row_spec = pl.BlockSpec((1, 1, 2 * tq), lambda b, i: (b, 0, 0))
    return pl.pallas_call(
        body, name=name, grid=(nb, nq),
        in_specs=[pl.BlockSpec((tq, LANES), lambda b, i: (i, b)), pl.BlockSpec((S, LANES), lambda b, i: (0, b // kdiv)),
                  pl.BlockSpec((LANES, S), lambda b, i: (b // kdiv, 0)), _win_dist_spec(nk, tq, nq), row_spec, row_spec],
        out_specs=[pl.BlockSpec((tq, LANES), lambda b, i: (i, b)), pl.BlockSpec((1, 1, 1, 2 * tq), lambda b, i: (b, i, 0, 0))],
        out_shape=[_sds((S, nb * LANES), F32), _sds((nb, nq, 1, 2 * tq), F32)],
        compiler_params=_params(("arbitrary", "arbitrary")))(q, k, vT, dist, slope, sink)


def _win_bwd(q, k, kT, v, o, do, lse, dist, slope, sink, *, kdiv, tq, name):
    S = k.shape[0]; nb = q.shape[1] // LANES; nkb = k.shape[1] // LANES; nq = S // tq; nk = min(tq + 2 * WINDOW, S)

    def body(q_ref, k_ref, kT_ref, v_ref, o_ref, do_ref, lse_ref, dist_ref, slope_ref, sink_ref, dq_ref, dk_ref, dv_ref, dsink_ref, ds_acc):
        b, i = pl.program_id(0), pl.program_id(1)
        rlo = _row_lo()
        lo = lax.broadcasted_iota(jnp.int32, (1, LANES), 1) < HEAD_DIM

        @pl.when((b % kdiv == 0) & (i == 0))
        def _():
            dk_ref[...] = jnp.zeros((S, LANES), F32)
            dv_ref[...] = jnp.zeros((S, LANES), F32)

        @pl.when(i == 0)
        def _():
            ds_acc[...] = jnp.zeros((1, 2 * tq), F32)

        k0 = _win_start(i, tq, nk, S)
        rows = pl.ds(k0, nk)
        qv = q_ref[...]
        qs = _stack_rows(qv, lo)
        qsT = _stack_cols(qv.astype(F32).T, rlo).astype(BF16)
        dov = do_ref[...]
        dos = _stack_rows(dov.astype(BF16), lo)
        dosT = _stack_cols(dov.T, rlo).astype(BF16)
        prodT = (dov * o_ref[...]).T
        delta = jnp.concatenate([jnp.sum(jnp.where(rlo, prodT, 0.0), axis=0, keepdims=True),
                                 jnp.sum(jnp.where(rlo, 0.0, prodT), axis=0, keepdims=True)], axis=1)
        lse_v = lse_ref[0, 0]
        ds_acc[...] += -jnp.exp(sink_ref[0] - lse_v) * delta
        p = jnp.exp(_dot(k_ref[rows, :], qsT) - slope_ref[0] * dist_ref[0] - lse_v)
        ds = (p * (_dot(v_ref[rows, :], dosT) - delta)).astype(BF16)
        dv_ref[rows, :] += _dot(p.astype(BF16), dos)
        dk_ref[rows, :] += _dot(ds, qs)
        dq_ref[...] = _pick_halves_T(_dot(kT_ref[:, rows], ds), rlo, tq)

        @pl.when(i == nq - 1)
        def _():
            acc = ds_acc[...]
            dsink_ref[0] = jnp.concatenate(
                [jnp.broadcast_to(jnp.sum(acc[:, 0:tq], axis=1, keepdims=True), (1, LANES)),
                 jnp.broadcast_to(jnp.sum(acc[:, tq:2 * tq], axis=1, keepdims=True), (1, LANES)), jnp.zeros((6, LANES), F32)], axis=0)

    qmap = lambda b, i: (i, b)
    kv_spec = pl.BlockSpec((S, LANES), lambda b, i: (0, b // kdiv))
    row_spec = pl.BlockSpec((1, 1, 2 * tq), lambda b, i: (b, 0, 0))
    return pl.pallas_call(
        body, name=name, grid=(nb, nq),
        in_specs=[pl.BlockSpec((tq, LANES), qmap), kv_spec, pl.BlockSpec((LANES, S), lambda b, i: (b // kdiv, 0)), kv_spec,
                  pl.BlockSpec((tq, LANES), qmap), pl.BlockSpec((tq, LANES), qmap),
                  pl.BlockSpec((1, 1, 1, 2 * tq), lambda b, i: (b, i, 0, 0)), _win_dist_spec(nk, tq, nq), row_spec, row_spec],
        out_specs=[pl.BlockSpec((tq, LANES), qmap), kv_spec, kv_spec, pl.BlockSpec((1, 8, LANES), lambda b, i: (b, 0, 0))],
        out_shape=[_sds((S, nb * LANES), F32), _sds((S, nkb * LANES), F32), _sds((S, nkb * LANES), F32), _sds((nb, 8, LANES), F32)],
        scratch_shapes=[pltpu.VMEM((1, 2 * tq), F32)],
        compiler_params=_params(("arbitrary", "arbitrary")))(q, k, kT, v, o, do, lse, dist, slope, sink)


def _sum_rows(v):
    return jnp.sum(v, axis=0, keepdims=True)


def _norm_mod_bwd(dh, xv, mod_ref, nw_ref, stats_ref):
    r = _rms(xv)
    xn = xv * r
    nw = nw_ref[...]
    stats_ref[0:1, :] += _sum_rows(dh)
    stats_ref[1:2, :] += _sum_rows(dh * (xn * nw))
    dn = dh * (1.0 + mod_ref[1:2, :])
    stats_ref[2:3, :] += _sum_rows(dn * xn)
    return _rms_bwd(xv, r, dn * nw)


def _even_post_fwd(oa, olat, proj, x, gate, wuv, woe):
    S = x.shape[0]
    ts = min(ROW_TILE, S)

    def body(oa_ref, ol_ref, proj_ref, x_ref, gate_ref, wuv_ref, woe_ref, y_ref, x1_ref):
        sa, _ = _silu_and_grad(proj_ref[:, 768:1280])
        sb, _ = _silu_and_grad(proj_ref[:, 1792:2304])
        olc = jnp.concatenate([ol_ref[hh] for hh in range(B_HEADS)], axis=1).astype(BF16)
        ob = _dot(olc, wuv_ref[...])
        mix = jnp.concatenate([oa_ref[...] * sa, ob * sb], axis=1).astype(BF16)
        y = _dot(mix, woe_ref[...])
        y_ref[...] = y
        x1_ref[...] = x_ref[...] + gate_ref[...] * y

    return pl.pallas_call(
        body, name="even_post_fwd", grid=(S // ts,),
        in_specs=[_row_spec(ts, 512), pl.BlockSpec((B_HEADS, ts, LANES), lambda i: (0, i, 0)), _row_spec(ts, EVEN_P),
                  _row_spec(ts, D_MODEL), _full_spec((1, D_MODEL)), _full_spec((1024, 512)), _full_spec((1024, D_MODEL))],
        out_specs=[_row_spec(ts, D_MODEL), _row_spec(ts, D_MODEL)],
        out_shape=[_sds((S, D_MODEL), F32), _sds((S, D_MODEL), F32)],
        compiler_params=_params(("arbitrary",)),
    )(oa, olat, proj, x, gate, wuv, woe)


def _odd_pre_fwd(x, mod, nw, wio):
    S = x.shape[0]
    ts = min(ROW_TILE, S)

    def body(x_ref, mod_ref, nw_ref, wio_ref, h_ref, proj_ref, q_ref, k_ref, v_ref):
        xv = x_ref[...]
        h = (xv * _rms(xv) * nw_ref[...]) * (1.0 + mod_ref[1:2, :]) + mod_ref[0:1, :]
        hb = h.astype(BF16)
        h_ref[...] = hb
        proj = _dot(hb, wio_ref[...])
        proj_ref[...] = proj
        q_ref[...] = (proj[:, 0:1024] * 0.125).astype(BF16)
        k_ref[...] = proj[:, 1024:1280].astype(BF16)
        v_ref[...] = proj[:, 1280:1536].astype(BF16)

    return pl.pallas_call(
        body, name="odd_pre_fwd", grid=(S // ts,),
        in_specs=[_row_spec(ts, D_MODEL), _full_spec((3, D_MODEL)), _full_spec((1, D_MODEL)), _full_spec((D_MODEL, ODD_IN))],
        out_specs=[_row_spec(ts, D_MODEL), _row_spec(ts, ODD_IN), _row_spec(ts, 1024), _row_spec(ts, 256), _row_spec(ts, 256)],
        out_shape=[_sds((S, D_MODEL), BF16), _sds((S, ODD_IN), F32), _sds((S, 1024), BF16), _sds((S, 256), BF16),
                   _sds((S, 256), BF16)],
        compiler_params=_params(("arbitrary",)),
    )(x, mod, nw, wio)


def _odd_post(oc, proj, x1, gate, woo, fw, tgt):
    S = x1.shape[0]
    ts = min(ROW_TILE, S)

    def body(oc_ref, proj_ref, x_ref, gate_ref, woo_ref, fw_ref, tgt_ref, doc_ref, dgc_ref, dx2_ref, dwoo_ref, stats_ref):
        @pl.when(pl.program_id(0) == 0)
        def _():
            dwoo_ref[...] = jnp.zeros((D_MODEL, D_MODEL), F32)
            stats_ref[...] = jnp.zeros((8, D_MODEL), F32)

        ocv = oc_ref[...]
        sg, dsg = _silu_and_grad(proj_ref[:, 1536:2560])
        mix = (ocv * sg).astype(BF16)
        woo_v = woo_ref[...]
        y = _dot(mix, woo_v)
        gate_v = gate_ref[...]
        x2 = x_ref[...] + gate_v * y
        r = _rms(x2)
        fw_v = fw_ref[...]
        xn = x2 * r
        err = xn * fw_v - tgt_ref[...]
        dout = err * (1.0 / D_MODEL)
        dx2 = _rms_bwd(x2, r, dout * fw_v)
        dx2_ref[...] = dx2
        stats_ref[0:1, :] += _sum_rows(dout * xn)
        stats_ref[1:2, :] += _sum_rows(dx2 * y)
        loss_t = 0.5 * jnp.sum(_sum_rows(err * dout), axis=-1, keepdims=True)
        stats_ref[2:3, :] += jnp.broadcast_to(loss_t, (1, D_MODEL))
        dy = (gate_v * dx2).astype(BF16)
        dmix = _dot_nt(dy, woo_v)
        dwoo_ref[...] += _dot_tn(mix, dy)
        doc_ref[...] = dmix * sg
        dgc_ref[...] = dmix * ocv * dsg

    return pl.pallas_call(
        body, name="odd_post", grid=(S // ts,),
        in_specs=[_row_spec(ts, D_MODEL), _row_spec(ts, ODD_IN), _row_spec(ts, D_MODEL), _full_spec((1, D_MODEL)),
                  _full_spec((D_MODEL, D_MODEL)), _full_spec((1, D_MODEL)), _row_spec(ts, D_MODEL)],
        out_specs=[_row_spec(ts, D_MODEL), _row_spec(ts, D_MODEL), _row_spec(ts, D_MODEL),
                   _full_spec((D_MODEL, D_MODEL), single=False), _full_spec((8, D_MODEL), single=False)],
        out_shape=[_sds((S, D_MODEL), F32), _sds((S, D_MODEL), F32), _sds((S, D_MODEL), F32), _sds((D_MODEL, D_MODEL), F32),
                   _sds((8, D_MODEL), F32)],
        compiler_params=_params(("arbitrary",)),
    )(oc, proj, x1, gate, woo, fw, tgt)


def _odd_pre_bwd(dq, dk, dv, dgc, h, x, dx_res, mod, nw, wio):
    S = x.shape[0]
    ts = min(ROW_TILE, S)

    def body(dq_ref, dk_ref, dv_ref, dgc_ref, h_ref, x_ref, dxr_ref, mod_ref, nw_ref, wio_ref, dx_ref, dw_ref, stats_ref):
        @pl.when(pl.program_id(0) == 0)
        def _():
            dw_ref[...] = jnp.zeros((D_MODEL, ODD_IN), F32)
            stats_ref[...] = jnp.zeros((8, D_MODEL), F32)

        dproj = jnp.concatenate([dq_ref[...] * 0.125, dk_ref[...], dv_ref[...], dgc_ref[...]], axis=1).astype(BF16)
        dh = _dot_nt(dproj, wio_ref[...])
        dw_ref[...] += _dot_tn(h_ref[...], dproj)
        dx_ref[...] = dxr_ref[...] + _norm_mod_bwd(dh, x_ref[...], mod_ref, nw_ref, stats_ref)

    return pl.pallas_call(
        body, name="odd_pre_bwd", grid=(S // ts,),
        in_specs=[_row_spec(ts, 1024), _row_spec(ts, 256), _row_spec(ts, 256), _row_spec(ts, 1024), _row_spec(ts, D_MODEL),
                  _row_spec(ts, D_MODEL), _row_spec(ts, D_MODEL), _full_spec((3, D_MODEL)), _full_spec((1, D_MODEL)),
                  _full_spec((D_MODEL, ODD_IN))],
        out_specs=[_row_spec(ts, D_MODEL), _full_spec((D_MODEL, ODD_IN), single=False), _full_spec((8, D_MODEL), single=False)],
        out_shape=[_sds((S, D_MODEL), F32), _sds((D_MODEL, ODD_IN), F32), _sds((8, D_MODEL), F32)],
        compiler_params=_params(("arbitrary",)),
    )(dq, dk, dv, dgc, h, x, dx_res, mod, nw, wio)


def _even_post_bwd(dx1, y, oa, olat, proj, gate, wuv, woe):
    S = dx1.shape[0]
    ts = min(ROW_TILE, S)

    def body(dx_ref, y_ref, oa_ref, ol_ref, proj_ref, gate_ref, wuv_ref, woe_ref,
             doa_ref, dga_ref, dgb_ref, dol_ref, dwoe_ref, dwuv_ref, stats_ref):
        @pl.when(pl.program_id(0) == 0)
        def _():
            dwoe_ref[...] = jnp.zeros((D_MODEL, D_MODEL), F32)
            dwuv_ref[...] = jnp.zeros((1024, 512), F32)
            stats_ref[...] = jnp.zeros((8, D_MODEL), F32)

        dxv = dx_ref[...]
        stats_ref[0:1, :] += _sum_rows(dxv * y_ref[...])
        dy = (gate_ref[...] * dxv).astype(BF16)
        sa, dsa = _silu_and_grad(proj_ref[:, 768:1280])
        sb, dsb = _silu_and_grad(proj_ref[:, 1792:2304])
        olc = jnp.concatenate([ol_ref[hh] for hh in range(B_HEADS)], axis=1).astype(BF16)
        wuv_v = wuv_ref[...]
        ob = _dot(olc, wuv_v)
        oav = oa_ref[...]
        mix = jnp.concatenate([oav * sa, ob * sb], axis=1).astype(BF16)
        dmix = _dot_nt(dy, woe_ref[...])
        dwoe_ref[...] += _dot_tn(mix, dy)
        dma, dmb = dmix[:, 0:512], dmix[:, 512:1024]
        doa_ref[...] = dma * sa
        dga_ref[...] = dma * oav * dsa
        dgb_ref[...] = dmb * ob * dsb
        dob = (dmb * sb).astype(BF16)
        dol = _dot_nt(dob, wuv_v)
        dwuv_ref[...] += _dot_tn(olc, dob)
        for hh in range(B_HEADS):
            dol_ref[hh] = dol[:, LANES * hh:LANES * (hh + 1)]

    head_spec = pl.BlockSpec((B_HEADS, ts, LANES), lambda i: (0, i, 0))
    return pl.pallas_call(
        body, name="even_post_bwd", grid=(S // ts,),
        in_specs=[_row_spec(ts, D_MODEL), _row_spec(ts, D_MODEL), _row_spec(ts, 512), head_spec, _row_spec(ts, EVEN_P),
                  _full_spec((1, D_MODEL)), _full_spec((1024, 512)), _full_spec((1024, D_MODEL))],
        out_specs=[_row_spec(ts, 512), _row_spec(ts, 512), _row_spec(ts, 512), head_spec,
                   _full_spec((D_MODEL, D_MODEL), single=False), _full_spec((1024, 512), single=False),
                   _full_spec((8, D_MODEL), single=False)],
        out_shape=[_sds((S, 512), F32), _sds((S, 512), F32), _sds((S, 512), F32), _sds((B_HEADS, S, LANES), F32),
                   _sds((D_MODEL, D_MODEL), F32), _sds((1024, 512), F32), _sds((8, D_MODEL), F32)],
        compiler_params=_params(("arbitrary",)),
    )(dx1, y, oa, olat, proj, gate, wuv, woe)


EVEN_BWD_ROW_TILE = 128


def _even_pre_bwd(x, h, proj, dqa, dka, dva, dga, dgb, dqcat, dkcat, dx_res, mod, nw, wie, qn, kn, seg, ca, sa, ct, st,
                  qln, kvln, wuq, wuk):
    S = x.shape[0]
    ts = min(EVEN_BWD_ROW_TILE, S)

    def body(x_ref, h_ref, proj_ref, dqa_ref, dka_ref, dva_ref, dga_ref, dgb_ref, dqc_ref, dkc_ref, dxr_ref, mod_ref, nw_ref,
             wie_ref, qn_ref, kn_ref, seg_ref, ca_ref, sa_ref, ct_ref, st_ref, qln_ref, kvln_ref, wuq_ref, wuk_ref,
             dx_ref, dwie_ref, dwuq_ref, dwuk_ref, stats_ref, nstats_ref):
        @pl.when(pl.program_id(0) == 0)
        def _():
            dwie_ref[...] = jnp.zeros((D_MODEL, EVEN_P), F32)
            dwuq_ref[...] = jnp.zeros((B_Q_LORA, 1536), F32)
            dwuk_ref[...] = jnp.zeros((512, 1024), F32)
            stats_ref[...] = jnp.zeros((8, D_MODEL), F32)
            nstats_ref[...] = jnp.zeros((8, 256), F32)

        lane = _lane_iota()
        ca_v, sa_v, ct_v, st_v = ca_ref[...], sa_ref[...], ct_ref[...], st_ref[...]
        seg_v = seg_ref[...]

        def head_norm_bwd(xc, dy, w):
            r = lax.rsqrt(_seg_mean(xc * xc, seg_v) + EPS)
            g = dy * w
            dxc = r * g - xc * (r * r * r) * _seg_mean(xc * g, seg_v)
            return dxc, _sum_rows(dy * (xc * r))

        pieces = []
        dqn = jnp.zeros((1, LANES), F32)
        for cb in range(4):
            sl = slice(LANES * cb, LANES * (cb + 1))
            dy = _rot_bwd(dqa_ref[:, sl] * 0.125, ca_v, sa_v, lane)
            dxc, dw = head_norm_bwd(proj_ref[:, sl], dy, qn_ref[...])
            pieces.append(dxc)
            dqn = dqn + dw
        dxc, dkn = head_norm_bwd(proj_ref[:, 512:640], _rot_bwd(dka_ref[...], ca_v, sa_v, lane), kn_ref[...])
        pieces += [dxc, dva_ref[...], dga_ref[...]]
        nstats_ref[0:1, 0:LANES] += dqn + pltpu.roll(dqn, HEAD_DIM, 1)
        nstats_ref[1:2, 0:LANES] += dkn + pltpu.roll(dkn, HEAD_DIM, 1)

        cq = proj_ref[:, 1280:1536]
        rq = _rms(cq)
        cqn_f = cq * rq
        qln_v = qln_ref[...]
        cqn = (cqn_f * qln_v).astype(BF16)
        wuq_v, wuk_v = wuq_ref[...], wuk_ref[...]
        qnope = _dot(cqn, wuq_v[:, 0:512]).astype(BF16)
        dqlat = jnp.concatenate([dqc_ref[hh, :, 0:LANES] for hh in range(B_HEADS)], axis=1).astype(BF16)
        dqnope = _dot_nt(dqlat, wuk_v)
        dwuk_ref[...] += _dot_tn(qnope, dqlat)
        dqr = [_rot_bwd(dqc_ref[hh, :, LANES:2 * LANES], ct_v, st_v, lane) for hh in range(B_HEADS)]
        dqb = jnp.concatenate([dqnope] + dqr, axis=1).astype(BF16)
        dcqn = _dot_nt(dqb, wuq_v)
        dwuq_ref[...] += _dot_tn(cqn, dqb)
        nstats_ref[2:3, :] += _sum_rows(dcqn * cqn_f)
        dcq = _rms_bwd(cq, rq, dcqn * qln_v)
        ckv = proj_ref[:, 1536:1664]
        rk = _rms(ckv)
        dckvn = dkc_ref[:, 0:LANES]
        nstats_ref[3:4, 0:LANES] += _sum_rows(dckvn * (ckv * rk))
        dckv = _rms_bwd(ckv, rk, dckvn * kvln_ref[...])
        dkr = _rot_bwd(dkc_ref[:, LANES:2 * LANES], ct_v, st_v, lane)
        pieces += [dcq, dckv, dkr, dgb_ref[...]]
        dproj = jnp.concatenate(pieces, axis=1).astype(BF16)
        dh = _dot_nt(dproj, wie_ref[...])
        dwie_ref[...] += _dot_tn(h_ref[...], dproj)
        dx_ref[...] = dxr_ref[...] + _norm_mod_bwd(dh, x_ref[...], mod_ref, nw_ref, stats_ref)

    return pl.pallas_call(
        body, name="even_pre_bwd", grid=(S // ts,),
        in_specs=[_row_spec(ts, D_MODEL), _row_spec(ts, D_MODEL), _row_spec(ts, EVEN_P), _row_spec(ts, 512), _row_spec(ts, LANES),
                  _row_spec(ts, LANES), _row_spec(ts, 512), _row_spec(ts, 512),
                  pl.BlockSpec((B_HEADS, ts, 2 * LANES), lambda i: (0, i, 0)), _row_spec(ts, 2 * LANES), _row_spec(ts, D_MODEL),
                  _full_spec((3, D_MODEL)), _full_spec((1, D_MODEL)), _full_spec((D_MODEL, EVEN_P)),
                  _full_spec((1, LANES)), _full_spec((1, LANES)), _full_spec((LANES, LANES)),
                  _row_spec(ts, LANES), _row_spec(ts, LANES), _row_spec(ts, LANES), _row_spec(ts, LANES),
                  _full_spec((1, B_Q_LORA)), _full_spec((1, B_KV_LORA)), _full_spec((B_Q_LORA, 1536)), _full_spec((512, 1024))],
        out_specs=[_row_spec(ts, D_MODEL), _full_spec((D_MODEL, EVEN_P), single=False), _full_spec((B_Q_LORA, 1536), single=False),
                   _full_spec((512, 1024), single=False), _full_spec((8, D_MODEL), single=False), _full_spec((8, 256), single=False)],
        out_shape=[_sds((S, D_MODEL), F32), _sds((D_MODEL, EVEN_P), F32), _sds((B_Q_LORA, 1536), F32), _sds((512, 1024), F32),
                   _sds((8, D_MODEL), F32), _sds((8, 256), F32)],
        compiler_params=_params(("arbitrary",)),
    )(x, h, proj, dqa, dka, dva, dga, dgb, dqcat, dkcat, dx_res, mod, nw, wie, qn, kn, seg, ca, sa, ct, st, qln, kvln, wuq, wuk)


def _ada_fwd(c_all, w, b):
    n = w.shape[2]

    def body(c_ref, w_ref, b_ref, o_ref):
        cv = c_ref[...]
        o_ref[0] = _dot_f32(cv * _sigmoid(cv), w_ref[0]) + b_ref[0]

    return pl.pallas_call(
        body, name="ada_fwd", grid=(2,),
        in_specs=[pl.BlockSpec((N_DEV, D_MODEL), lambda l: (0, 0)), pl.BlockSpec((1, D_MODEL, n), lambda l: (l, 0, 0)),
                  pl.BlockSpec((1, 1, n), lambda l: (l, 0, 0))],
        out_specs=pl.BlockSpec((1, N_DEV, n), lambda l: (l, 0, 0)),
        out_shape=_sds((2, N_DEV, n), F32),
        compiler_params=_params(("arbitrary",)),
    )(c_all, w, b)


def _ada_bwd(c_all_t, dmod):
    n = dmod.shape[2]

    def body(c_ref, d_ref, o_ref):
        cv = c_ref[...]
        act = cv * _sigmoid(cv)
        dv = d_ref[0]
        acc = act[:, 0:1] * dv[0:1, :]
        for bb in range(1, N_DEV):
            acc = acc + act[:, bb:bb + 1] * dv[bb:bb + 1, :]
        o_ref[0] = acc

    return pl.pallas_call(
        body, name="ada_bwd", grid=(2,),
        in_specs=[pl.BlockSpec((D_MODEL, N_DEV), lambda l: (0, 0)), pl.BlockSpec((1, N_DEV, n), lambda l: (l, 0, 0))],
        out_specs=pl.BlockSpec((1, D_MODEL, n), lambda l: (l, 0, 0)),
        out_shape=_sds((2, D_MODEL, n), F32),
        compiler_params=_params(("arbitrary",)),
    )(c_all_t, dmod)


ADAM_ROW_TILE = 256


def _adam(parts, w, m, v, name):
    P, R, C = parts.shape
    tr = R if R <= ADAM_ROW_TILE else ADAM_ROW_TILE
    assert R % tr == 0

    def body(p_ref, w_ref, m_ref, v_ref, g_ref, d_ref, nm_ref, nv_ref):
        g = p_ref[0].astype(F32)
        for k in range(1, P):
            g = g + p_ref[k].astype(F32)
        g_ref[...] = g
        m_new = ADAM_B1 * m_ref[...] + (1.0 - ADAM_B1) * g
        v_new = ADAM_B2 * v_ref[...] + (1.0 - ADAM_B2) * jnp.square(g)
        m_hat = m_new / (1.0 - ADAM_B1 ** ADAM_STEP)
        v_hat = v_new / (1.0 - ADAM_B2 ** ADAM_STEP)
        d_ref[...] = -ADAM_LR * (m_hat / (jnp.sqrt(v_hat) + ADAM_EPS) + ADAM_WD * w_ref[...])
        nm_ref[...] = m_new
        nv_ref[...] = v_new

    spec = pl.BlockSpec((tr, C), lambda i: (i, 0))
    return pl.pallas_call(
        body, name=name, grid=(R // tr,),
        in_specs=[pl.BlockSpec((P, tr, C), lambda i: (0, i, 0)), spec, spec, spec],
        out_specs=[spec, spec, spec, spec], out_shape=[_sds((R, C), F32)] * 4,
        compiler_params=_params(("arbitrary",)),
    )(parts, w, m, v)


_ANY = pl.BlockSpec(memory_space=pl.ANY)
CHIP_FLIPS = ((1, 0), (0, 1), (1, 1))
DEV_FLIPS = tuple((dx, dy, dc) for dx in (0, 1) for dy in (0, 1) for dc in (0, 1) if dx + dy + dc)


def _flip(a, d):
    return a if d == 0 else 1 - a


def _my_place():
    return lax.axis_index("x"), lax.axis_index("y"), lax.axis_index("c")


def _gather_dev8(arrs, name):
    n = len(arrs)

    def body(*refs):
        ins, outs = refs[:n], refs[n:2 * n]
        send_sems, recv_sems, loc_sems = refs[2 * n:]
        x, y, c = _my_place()
        me = 4 * x + 2 * y + c
        copies = []
        for a in range(n):
            loc = pltpu.make_async_copy(ins[a], outs[a].at[me], loc_sems.at[a])
            loc.start()
            copies.append(loc)
            for k, (dx, dy, dc) in enumerate(DEV_FLIPS):
                cp = pltpu.make_async_remote_copy(
                    src_ref=ins[a], dst_ref=outs[a].at[me], send_sem=send_sems.at[a, k], recv_sem=recv_sems.at[a, k],
                    device_id=(_flip(x, dx), _flip(y, dy), _flip(c, dc)), device_id_type=MESH_ID)
                cp.start()
                copies.append(cp)
        for cp in copies:
            cp.wait()

    return pl.pallas_call(
        body, name=name, in_specs=[_ANY] * n, out_specs=[_ANY] * n,
        out_shape=[_sds((N_DEV,) + a.shape, a.dtype) for a in arrs],
        scratch_shapes=[pltpu.SemaphoreType.DMA((n, 7)), pltpu.SemaphoreType.DMA((n, 7)), pltpu.SemaphoreType.DMA((n,))],
    )(*arrs)


def _gather_chip4_halves(arrs, name):
    n = len(arrs)

    def body(*refs):
        ins, outs = refs[:n], refs[n:2 * n]
        send_sems, recv_sems, loc_sems = refs[2 * n:]
        x, y, c = _my_place()
        chip = 2 * x + y
        sibling = (x, y, 1 - c)

        def remote(src, p, half, a, k, to):
            return pltpu.make_async_remote_copy(src_ref=src, dst_ref=outs[a].at[p, half], send_sem=send_sems.at[a, k],
                                                recv_sem=recv_sems.at[a, k], device_id=to, device_id_type=MESH_ID)

        sends, locs = [], []
        for a in range(n):
            loc = pltpu.make_async_copy(ins[a], outs[a].at[chip], loc_sems.at[a])
            loc.start()
            locs.append(loc)
            for k, (dx, dy) in enumerate(CHIP_FLIPS):
                cp = remote(ins[a].at[c], chip, c, a, k, (_flip(x, dx), _flip(y, dy), c))
                cp.start()
                sends.append(cp)
        for a in range(n):
            for k, (dx, dy) in enumerate(CHIP_FLIPS):
                p = 2 * _flip(x, dx) + _flip(y, dy)
                remote(outs[a].at[p, c], p, c, a, k, sibling).wait_recv()
                fwd = remote(outs[a].at[p, c], p, c, a, 3 + k, sibling)
                fwd.start()
                sends.append(fwd)
        for a in range(n):
            for k, (dx, dy) in enumerate(CHIP_FLIPS):
                p = 2 * _flip(x, dx) + _flip(y, dy)
                remote(outs[a].at[p, 1 - c], p, 1 - c, a, 3 + k, sibling).wait_recv()
        for cp in sends:
            cp.wait_send()
        for loc in locs:
            loc.wait()

    return pl.pallas_call(
        body, name=name, in_specs=[_ANY] * n, out_specs=[_ANY] * n,
        out_shape=[_sds((N_CHIPS,) + a.shape, a.dtype) for a in arrs],
        scratch_shapes=[pltpu.SemaphoreType.DMA((n, 6)), pltpu.SemaphoreType.DMA((n, 6)), pltpu.SemaphoreType.DMA((n,))],
    )(*arrs)


def _reduce_exchange(arrs, name):
    n = len(arrs)

    def body(*refs):
        ins, outs = refs[:n], refs[n:2 * n]
        send_sems, recv_sems, loc_sems = refs[2 * n:]
        x, y, c = _my_place()
        chip = 2 * x + y
        sibling = (x, y, 1 - c)

        def remote(src, slot, a, k, to):
            return pltpu.make_async_remote_copy(src_ref=src, dst_ref=outs[a].at[slot], send_sem=send_sems.at[a, k],
                                                recv_sem=recv_sems.at[a, k], device_id=to, device_id_type=MESH_ID)

        sends, locs = [], []
        for a in range(n):
            loc = pltpu.make_async_copy(ins[a].at[chip], outs[a].at[2 * chip + c], loc_sems.at[a])
            loc.start()
            locs.append(loc)
            first = [remote(ins[a].at[chip], 2 * chip + c, a, 0, sibling)]
            for k, (dx, dy) in enumerate(CHIP_FLIPS):
                px, py = _flip(x, dx), _flip(y, dy)
                first.append(remote(ins[a].at[2 * px + py], 2 * chip + c, a, 1 + k, (px, py, c)))
            for cp in first:
                cp.start()
            sends += first
        for a in range(n):
            for k, (dx, dy) in enumerate(CHIP_FLIPS):
                slot = 2 * (2 * _flip(x, dx) + _flip(y, dy)) + c
                remote(outs[a].at[slot], slot, a, 1 + k, sibling).wait_recv()
                fwd = remote(outs[a].at[slot], slot, a, 4 + k, sibling)
                fwd.start()
                sends.append(fwd)
        for a in range(n):
            remote(outs[a].at[2 * chip + 1 - c], 2 * chip + 1 - c, a, 0, sibling).wait_recv()
            for k, (dx, dy) in enumerate(CHIP_FLIPS):
                slot = 2 * (2 * _flip(x, dx) + _flip(y, dy)) + 1 - c
                remote(outs[a].at[slot], slot, a, 4 + k, sibling).wait_recv()
        for cp in sends:
            cp.wait_send()
        for loc in locs:
            loc.wait()

    return pl.pallas_call(
        body, name=name, in_specs=[_ANY] * n, out_specs=[_ANY] * n,
        out_shape=[_sds((N_DEV,) + a.shape[1:], a.dtype) for a in arrs],
        scratch_shapes=[pltpu.SemaphoreType.DMA((n, 7)), pltpu.SemaphoreType.DMA((n, 7)), pltpu.SemaphoreType.DMA((n,))],
    )(*arrs)


def _pair_order(nheads, nkv):
    group = nheads // nkv
    order = []
    for m in range(nkv // 2):
        for i in range(group):
            order += [2 * m * group + i, (2 * m + 1) * group + i]
    return order


A_ORDER = _pair_order(A_HEADS, A_KV_HEADS)
C_ORDER = _pair_order(C_HEADS, C_KV_HEADS)
A_INV = [int(k) for k in np.argsort(A_ORDER)]
C_INV = [int(k) for k in np.argsort(C_ORDER)]


def _perm_heads(w, order, axis):
    return jnp.concatenate([lax.slice_in_dim(w, HEAD_DIM * h, HEAD_DIM * (h + 1), axis=axis) for h in order], axis=axis)


def _even_in_layout(w):
    return jnp.concatenate([_perm_heads(w[:, 0:512], A_ORDER, 1), w[:, 512:768], _perm_heads(w[:, 768:1280], A_ORDER, 1),
                            w[:, 1280:1696], jnp.zeros((w.shape[0], 96), w.dtype), w[:, 1696:2208]], axis=1)


def _even_in_unlayout(g):
    return jnp.concatenate([_perm_heads(g[:, 0:512], A_INV, 1), g[:, 512:768], _perm_heads(g[:, 768:1280], A_INV, 1),
                            g[:, 1280:1696], g[:, 1792:2304]], axis=1)


def _even_out_layout(w):
    return jnp.concatenate([_perm_heads(w[0:512], A_ORDER, 0), w[512:1024]], axis=0)


def _even_out_unlayout(g):
    return jnp.concatenate([_perm_heads(g[0:512], A_INV, 0), g[512:1024]], axis=0)


def _odd_in_layout(w):
    return jnp.concatenate([_perm_heads(w[:, 0:1024], C_ORDER, 1), w[:, 1024:1536], _perm_heads(w[:, 1536:2560], C_ORDER, 1)],
                           axis=1)


def _odd_in_unlayout(g):
    return jnp.concatenate([_perm_heads(g[:, 0:1024], C_INV, 1), g[:, 1024:1536], _perm_heads(g[:, 1536:2560], C_INV, 1)],
                           axis=1)


def _uq_layout(w):
    per = B_NOPE + B_ROPE
    pad = jnp.zeros((w.shape[0], LANES - B_ROPE), w.dtype)
    nope = [w[:, per * h:per * h + B_NOPE] for h in range(B_HEADS)]
    rope = [jnp.concatenate([w[:, per * h + B_NOPE:per * (h + 1)], pad], axis=1) for h in range(B_HEADS)]
    return jnp.concatenate(nope + rope, axis=1)


def _uq_unlayout(g):
    parts = []
    for h in range(B_HEADS):
        parts += [g[:, B_NOPE * h:B_NOPE * (h + 1)], g[:, 512 + LANES * h:512 + LANES * h + B_ROPE]]
    return jnp.concatenate(parts, axis=1)


def _block_diag(blocks):
    rows = []
    for h, blk in enumerate(blocks):
        r, cdim = blk.shape
        n = len(blocks)
        rows.append(jnp.concatenate([jnp.zeros((r, cdim * h), blk.dtype), blk, jnp.zeros((r, cdim * (n - 1 - h)), blk.dtype)],
                                    axis=1))
    return jnp.concatenate(rows, axis=0)


def _uk_layout(w):
    return _block_diag([w[:, h, :].T for h in range(B_HEADS)])


def _uk_unlayout(g):
    return jnp.stack([g[B_NOPE * h:B_NOPE * (h + 1), LANES * h:LANES * (h + 1)].T for h in range(B_HEADS)], axis=1)


def _uv_layout(w):
    return _block_diag([w[:, h, :] for h in range(B_HEADS)])


def _uv_unlayout(g):
    return jnp.stack([g[LANES * h:LANES * (h + 1), B_V * h:B_V * (h + 1)] for h in range(B_HEADS)], axis=1)


def _rope_tables(S):
    inv = ROPE_THETA ** (-jnp.arange(0, 32, 2, dtype=F32) / 32)
    tok = jnp.arange(S)

    def tab(pos):
        ang = pos.astype(F32)[:, None] * inv[None, :]
        cos, sin = jnp.cos(ang), jnp.sin(ang)
        return jnp.concatenate([cos, cos], axis=1), jnp.concatenate([-sin, sin], axis=1)

    cr, sr = tab(tok // GRID_W)
    cc, sc = tab(tok % GRID_W)
    ct, st = tab(tok)
    return (jnp.tile(jnp.concatenate([cr, cc], axis=1), (1, 2)), jnp.tile(jnp.concatenate([sr, sc], axis=1), (1, 2)),
            jnp.tile(ct, (1, 4)), jnp.tile(st, (1, 4)))


A_TQ, A_TK, A_SUB = 256, 4096, 1024
B_TQ, B_TK, B_SUB = 128, 4096, 1024
B_BWD_TK, B_BWD_SUB = 2048, 512
C_T = 256


def _local_step(x0, tgt, mod, norm_w, wie, wuq, wuk, wuv, woe, wio, woo, a_q_norm, a_k_norm, q_lora_norm, kv_lora_norm,
                c_sink, final_norm):
    S = x0.shape[0]
    mod3 = mod.reshape(2, 3, D_MODEL)
    ca, sa, ct, st = _rope_tables(S)
    lane_seg = np.arange(LANES) // HEAD_DIM
    seg = jnp.asarray((lane_seg[:, None] == lane_seg[None, :]).astype(np.float32))
    qn = jnp.tile(a_q_norm.reshape(1, HEAD_DIM), (1, 2))
    kn = jnp.tile(a_k_norm.reshape(1, HEAD_DIM), (1, 2))
    qln, kvln = q_lora_norm.reshape(1, B_Q_LORA), kv_lora_norm.reshape(1, B_KV_LORA)
    nw0, nw1 = norm_w[0:1], norm_w[1:2]
    gate0, gate1 = mod3[0, 2:3], mod3[1, 2:3]
    a_tq, a_tk, b_tq, b_tk, bb_tk, c_t = min(A_TQ, S), min(A_TK, S), min(B_TQ, S), min(B_TK, S), min(B_BWD_TK, S), min(C_T, S)
    a_sub, b_sub, bb_sub = min(A_SUB, a_tk), min(B_SUB, b_tk), min(B_BWD_SUB, bb_tk)

    h0, proj_e, qa, ka, va, qcat, kcat = _even_pre_fwd(x0, mod3[0], nw0, wie, qn, kn, seg, ca, sa, ct, st, qln, kvln, wuq, wuk)
    kcat_t = kcat.T
    oa, lse_a = _pp_fwd(qa, ka, va.T, kdiv=4, tq=a_tq, tk=a_tk, sub=a_sub, name="attn_a_fwd")
    olat, lse_b = _mla_fwd(qcat, kcat, kcat_t, tq=b_tq, tk=b_tk, sub=b_sub)
    y0, x1 = _even_post_fwd(oa, olat, proj_e, x0, gate0, wuv, woe)
    h1, proj_o, qc, kc, vc = _odd_pre_fwd(x1, mod3[1], nw1, wio)
    slopes = 2.0 ** (-8.0 * jnp.arange(1, C_HEADS + 1, dtype=F32) / C_HEADS)
    c_order = np.asarray(C_ORDER)
    slope_rows = jnp.repeat(slopes[c_order].reshape(C_HEADS // 2, 2), c_t, axis=1)[:, None, :]
    sink_rows = jnp.repeat(c_sink.reshape(C_HEADS)[c_order].reshape(C_HEADS // 2, 2), c_t, axis=1)[:, None, :]
    win_dist = _win_dist_table(S, c_t)
    oc, lse_c = _win_fwd(qc, kc, vc.T, win_dist, slope_rows, sink_rows, kdiv=4, tq=c_t, name="attn_c_fwd")
    doc, dgc, dx2, dwoo, st_f = _odd_post(oc, proj_o, x1, gate1, woo, final_norm.reshape(1, D_MODEL), tgt)
    dqc, dkc, dvc, dsink_raw = _win_bwd(qc, kc, kc.T, vc, oc, doc, lse_c, win_dist, slope_rows, sink_rows, kdiv=4, tq=c_t,
                                        name="attn_c_bwd")
    dx1, dwio, st_1 = _odd_pre_bwd(dqc, dkc, dvc, dgc, h1, x1, dx2, mod3[1], nw1, wio)
    doa, dga, dgb, dolat, dwoe, dwuv, st_e = _even_post_bwd(dx1, y0, oa, olat, proj_e, gate0, wuv, woe)
    dqa, dka, dva = _pp_bwd(qa, ka, ka.T, va, oa, doa, lse_a, kdiv=4, tq=a_tq, tk=a_tk, sub=a_sub, name="attn_a_bwd")
    dqcat, dkcat = _mla_bwd(qcat, kcat, kcat_t, olat, dolat, lse_b, tq=b_tq, tk=bb_tk, sub=bb_sub)
    dx0, dwie, dwuq, dwuk, st_0, nst = _even_pre_bwd(x0, h0, proj_e, dqa, dka, dva, dga, dgb, dqcat, dkcat, dx1, mod3[0], nw0,
                                                     wie, qn, kn, seg, ca, sa, ct, st, qln, kvln, wuq, wuk)
    dsink_pairs = jnp.stack([dsink_raw[:, 0, 0], dsink_raw[:, 1, 0]], axis=1).reshape(C_HEADS)
    return dict(
        loss=st_f[2, 0], dx=dx0,
        dmod=jnp.stack([jnp.concatenate([st_0[0], st_0[1], st_e[0]]), jnp.concatenate([st_1[0], st_1[1], st_f[1]])]),
        norm_w=jnp.stack([st_0[2], st_1[2]]), final_norm=st_f[0],
        a_q_norm=nst[0:1, 0:HEAD_DIM], a_k_norm=nst[1:2, 0:HEAD_DIM], b_q_lora_norm=nst[2:3, :], b_kv_lora_norm=nst[3:4, 0:LANES],
        c_sink=dsink_pairs[np.asarray(C_INV)].reshape(1, C_HEADS),
        even_w_in=dwie, b_w_uq=dwuq, b_w_uk=dwuk, b_w_uv=dwuv, even_w_out=dwoe, odd_w_in=dwio, odd_w_out=dwoo)


WEIGHT_NAMES = ("norm_w", "ada_w", "ada_b", "even_w_in", "a_q_norm", "a_k_norm", "b_q_lora_norm", "b_kv_lora_norm", "b_w_uq",
                "b_w_uk", "b_w_uv", "even_w_out", "odd_w_in", "c_sink", "odd_w_out", "final_norm")
SMALL_NAMES = ("dmod", "norm_w", "final_norm", "a_q_norm", "a_k_norm", "b_q_lora_norm", "b_kv_lora_norm", "c_sink")


def _cols_to_chips(g):
    r, n4 = g.shape
    return jnp.transpose(g.reshape(r, N_CHIPS, n4 // N_CHIPS), (1, 0, 2))


def _chips_to_cols(g):
    p, r, n = g.shape
    return jnp.transpose(g, (1, 0, 2)).reshape(r, p * n)


def kernel(x, c, norm_w, ada_w, ada_b, even_w_in, a_q_norm, a_k_norm, b_q_lora_norm, b_kv_lora_norm, b_w_uq, b_w_uk, b_w_uv, even_w_out, odd_w_in, c_sink, odd_w_out, final_norm, loss_target, m_norm_w, m_ada_w, m_ada_b, m_even_w_in, m_a_q_norm, m_a_k_norm, m_b_q_lora_norm, m_b_kv_lora_norm, m_b_w_uq, m_b_w_uk, m_b_w_uv, m_even_w_out, m_odd_w_in, m_c_sink, m_odd_w_out, m_final_norm, v_norm_w, v_ada_w, v_ada_b, v_even_w_in, v_a_q_norm, v_a_k_norm, v_b_q_lora_norm, v_b_kv_lora_norm, v_b_w_uq, v_b_w_uk, v_b_w_uv, v_even_w_out, v_odd_w_in, v_c_sink, v_odd_w_out, v_final_norm):
    given = dict(locals())
    xi, yi, ci = _my_place()
    chip = 2 * xi + yi
    dev = 2 * chip + ci
    n_ada = ada_w.shape[2]

    (c_all,) = _gather_dev8([c], "gather_c")
    c_all = c_all.reshape(N_DEV, D_MODEL)
    bias = lax.dynamic_slice_in_dim(ada_b, chip * n_ada, n_ada, axis=1).reshape(2, 1, n_ada)
    mod_cols = _ada_fwd(c_all, ada_w, bias)
    def halves(w):
        return w.astype(BF16).reshape((2, w.shape[0] // 2) + w.shape[1:])

    gathered = _gather_chip4_halves(
        [mod_cols, halves(even_w_in[0]), halves(b_w_uq[0]), halves(even_w_out[0]), halves(odd_w_in[0]), halves(odd_w_out[0])],
        "gather_weights")
    mod_all = gathered[0]
    wie_g, wuq_g, woe_g, wio_g, woo_g = [g.reshape((N_CHIPS, 2 * g.shape[2]) + g.shape[3:]) for g in gathered[1:]]
    mod = jnp.transpose(lax.dynamic_index_in_dim(mod_all, dev, axis=2, keepdims=False), (1, 0, 2)).reshape(2, 3 * D_MODEL)

    res = _local_step(
        x[0], loss_target[0], mod, norm_w,
        _even_in_layout(_chips_to_cols(wie_g)), _uq_layout(_chips_to_cols(wuq_g)), _uk_layout(b_w_uk[0].astype(BF16)),
        _uv_layout(b_w_uv[0].astype(BF16)), _even_out_layout(woe_g.reshape(D_MODEL, D_MODEL)), _odd_in_layout(_chips_to_cols(wio_g)),
        _perm_heads(woo_g.reshape(D_MODEL, D_MODEL), C_ORDER, 0), a_q_norm, a_k_norm, b_q_lora_norm, b_kv_lora_norm, c_sink,
        final_norm)

    shard_parts = dict(zip(
        ("even_w_in", "b_w_uq", "even_w_out", "odd_w_in", "odd_w_out"),
        _reduce_exchange(
            [_cols_to_chips(_even_in_unlayout(res["even_w_in"].astype(BF16))), _cols_to_chips(_uq_unlayout(res["b_w_uq"].astype(BF16))),
             _even_out_unlayout(res["even_w_out"].astype(BF16)).reshape(N_CHIPS, D_MODEL // N_CHIPS, D_MODEL),
             _cols_to_chips(_odd_in_unlayout(res["odd_w_in"].astype(BF16))),
             _perm_heads(res["odd_w_out"].astype(BF16), C_INV, 0).reshape(N_CHIPS, D_MODEL // N_CHIPS, D_MODEL)],
            "reduce_exchange")))

    small = jnp.concatenate([res[k].reshape(-1) for k in SMALL_NAMES]).reshape(1, -1)
    latent = jnp.stack([_uk_unlayout(res["b_w_uk"]).reshape(B_KV_LORA, 512),
                        _uv_unlayout(res["b_w_uv"]).reshape(B_KV_LORA, 512)]).astype(BF16)
    small_all, latent_all = _gather_dev8([small, latent], "gather_small")
    small_all = small_all.reshape(N_DEV, -1)
    parts, off = {}, 0
    for k in SMALL_NAMES:
        n = int(np.prod(res[k].shape))
        parts[k] = small_all[:, off:off + n]
        off += n
    dmod_all = parts.pop("dmod").reshape(N_DEV, 2, 3 * D_MODEL)
    dmod_cols = jnp.transpose(lax.dynamic_slice_in_dim(dmod_all, chip * n_ada, n_ada, axis=2), (1, 0, 2))
    parts["ada_w"] = _ada_bwd(c_all.T, dmod_cols).reshape(1, 2 * D_MODEL, n_ada)
    parts["ada_b"] = dmod_all
    parts["b_w_uk"], parts["b_w_uv"] = latent_all[:, 0], latent_all[:, 1]
    parts.update(shard_parts)

    grads, deltas, new_m, new_v = [], [], [], []
    for k in WEIGHT_NAMES:
        w = given[k]
        p = parts[k]
        shape2 = (p.shape[-2], p.shape[-1]) if p.ndim == 3 else (1, p.shape[-1])
        p = p.reshape((p.shape[0],) + shape2)
        outs = _adam(p, w.reshape(shape2), given["m_" + k].reshape(shape2), given["v_" + k].reshape(shape2), "adam_" + k)
        for lst, o in zip((grads, deltas, new_m, new_v), outs):
            lst.append(o.reshape(w.shape))
    loss = lax.psum(res["loss"], ("x", "y", "c"))
    return (loss, res["dx"][None], *grads, *deltas, *new_m, *new_v)
```

```python
import functools

import numpy as np
import jax
import jax.numpy as jnp
from jax import lax
from jax.experimental import pallas as pl
from jax.experimental.pallas import tpu as pltpu

F32 = jnp.float32
BF16 = jnp.bfloat16
HIGHEST = lax.Precision.HIGHEST
MESH_ID = pl.DeviceIdType.MESH

D_MODEL = 1024
HEAD_DIM = 64
GRID_W = 64
EPS = 1e-6
ROPE_THETA = 10000.0
A_HEADS, A_KV_HEADS = 8, 2
B_HEADS, B_NOPE, B_ROPE, B_V = 8, 64, 32, 64
B_Q_LORA, B_KV_LORA = 256, 128
C_HEADS, C_KV_HEADS = 16, 4
WINDOW = 128
EVEN_IN, ODD_IN = 2208, 2560
EVEN_P = 2304
N_CHIPS, N_DEV = 4, 8
LANES = 128
NEG = -1e30
VMEM_LIMIT = 60 * 1024 * 1024

ADAM_LR, ADAM_B1, ADAM_B2, ADAM_EPS, ADAM_WD, ADAM_STEP = 0.001, 0.9, 0.999, 1e-08, 0.01, 10

ROW_TILE = 256


def _dot(a, b):
    return lax.dot_general(a, b, (((1,), (0,)), ((), ())), preferred_element_type=F32)


def _dot_nt(a, b):
    return lax.dot_general(a, b, (((1,), (1,)), ((), ())), preferred_element_type=F32)


def _dot_tn(a, b):
    return lax.dot_general(a, b, (((0,), (0,)), ((), ())), preferred_element_type=F32)


def _dot_f32(a, b):
    return lax.dot_general(a, b, (((1,), (0,)), ((), ())), precision=HIGHEST, preferred_element_type=F32)


def _sigmoid(x):
    return 1.0 / (1.0 + jnp.exp(-x))


def _silu_and_grad(g):
    s = _sigmoid(g)
    return g * s, s * (1.0 + g * (1.0 - s))


def _lane_iota():
    return lax.broadcasted_iota(jnp.int32, (1, LANES), 1)


def _partner(x, lane):
    return jnp.where((lane % 32) < 16, pltpu.roll(x, LANES - 16, 1), pltpu.roll(x, 16, 1))


def _rot(x, cos, sin_signed, lane):
    return x * cos + _partner(x, lane) * sin_signed


def _rot_bwd(dy, cos, sin_signed, lane):
    return dy * cos + _partner(dy * sin_signed, lane)


def _rms(x):
    return lax.rsqrt(jnp.mean(x * x, axis=-1, keepdims=True) + EPS)


def _rms_bwd(x, r, g):
    return r * g - x * (r * r * r) * jnp.mean(x * g, axis=-1, keepdims=True)


def _seg_mean(v, seg_ones):
    hi = v.astype(BF16)
    lo = (v - hi.astype(F32)).astype(BF16)
    return (_dot(hi, seg_ones) + _dot(lo, seg_ones)) * (1.0 / HEAD_DIM)


def _row_spec(ts, cols):
    return pl.BlockSpec((ts, cols), lambda i: (i, 0))


def _full_spec(shape, single=True):
    nd = len(shape)
    if single:
        return pl.BlockSpec(shape, lambda i: (0,) * nd, pipeline_mode=pl.Buffered(1))
    return pl.BlockSpec(shape, lambda i: (0,) * nd)


def _sds(shape, dtype):
    return jax.ShapeDtypeStruct(shape, dtype)


def _params(sem):
    return pltpu.CompilerParams(dimension_semantics=sem, vmem_limit_bytes=VMEM_LIMIT)


def _even_pre_fwd(x, mod, nw, wie, qn, kn, seg, ca, sa, ct, st, qln, kvln, wuq, wuk):
    S = x.shape[0]
    ts = min(ROW_TILE, S)

    def body(x_ref, mod_ref, nw_ref, wie_ref, qn_ref, kn_ref, seg_ref, ca_ref, sa_ref, ct_ref, st_ref, qln_ref,
             kvln_ref, wuq_ref, wuk_ref, h_ref, proj_ref, qa_ref, ka_ref, va_ref, qcat_ref, kcat_ref):
        xv = x_ref[...]
        h = (xv * _rms(xv) * nw_ref[...]) * (1.0 + mod_ref[1:2, :]) + mod_ref[0:1, :]
        hb = h.astype(BF16)
        h_ref[...] = hb
        proj = _dot(hb, wie_ref[...])
        proj_ref[...] = proj
        lane = _lane_iota()
        ca_v, sa_v, ct_v, st_v = ca_ref[...], sa_ref[...], ct_ref[...], st_ref[...]
        seg_v = seg_ref[...]
        for cb in range(4):
            xc = proj[:, LANES * cb:LANES * (cb + 1)]
            r = lax.rsqrt(_seg_mean(xc * xc, seg_v) + EPS)
            y = _rot(xc * r * qn_ref[...], ca_v, sa_v, lane)
            qa_ref[:, LANES * cb:LANES * (cb + 1)] = (y * 0.125).astype(BF16)
        kc = proj[:, 512:640]
        r = lax.rsqrt(_seg_mean(kc * kc, seg_v) + EPS)
        ka_ref[...] = _rot(kc * r * kn_ref[...], ca_v, sa_v, lane).astype(BF16)
        va_ref[...] = proj[:, 640:768].astype(BF16)
        cq = proj[:, 1280:1536]
        cqn = (cq * _rms(cq) * qln_ref[...]).astype(BF16)
        ckv = proj[:, 1536:1664]
        ckvn = ckv * _rms(ckv) * kvln_ref[...]
        qb = _dot(cqn, wuq_ref[...])
        qlat = _dot(qb[:, 0:512].astype(BF16), wuk_ref[...])
        for hh in range(B_HEADS):
            qcat_ref[hh, :, 0:LANES] = qlat[:, LANES * hh:LANES * (hh + 1)].astype(BF16)
            qr = _rot(qb[:, 512 + LANES * hh:512 + LANES * (hh + 1)], ct_v, st_v, lane)
            qcat_ref[hh, :, LANES:2 * LANES] = qr.astype(BF16)
        kcat_ref[:, 0:LANES] = ckvn.astype(BF16)
        kcat_ref[:, LANES:2 * LANES] = _rot(proj[:, 1664:1792], ct_v, st_v, lane).astype(BF16)

    return pl.pallas_call(
        body, name="even_pre_fwd", grid=(S // ts,),
        in_specs=[_row_spec(ts, D_MODEL), _full_spec((3, D_MODEL)), _full_spec((1, D_MODEL)), _full_spec((D_MODEL, EVEN_P)),
                  _full_spec((1, LANES)), _full_spec((1, LANES)), _full_spec((LANES, LANES)),
                  _row_spec(ts, LANES), _row_spec(ts, LANES), _row_spec(ts, LANES), _row_spec(ts, LANES),
                  _full_spec((1, B_Q_LORA)), _full_spec((1, B_KV_LORA)), _full_spec((B_Q_LORA, 1536)), _full_spec((512, 1024))],
        out_specs=[_row_spec(ts, D_MODEL), _row_spec(ts, EVEN_P), _row_spec(ts, 512), _row_spec(ts, LANES), _row_spec(ts, LANES),
                   pl.BlockSpec((B_HEADS, ts, 2 * LANES), lambda i: (0, i, 0)), _row_spec(ts, 2 * LANES)],
        out_shape=[_sds((S, D_MODEL), BF16), _sds((S, EVEN_P), F32), _sds((S, 512), BF16), _sds((S, LANES), BF16),
                   _sds((S, LANES), BF16), _sds((B_HEADS, S, 2 * LANES), BF16), _sds((S, 2 * LANES), BF16)],
        compiler_params=_params(("arbitrary",)),
    )(x, mod, nw, wie, qn, kn, seg, ca, sa, ct, st, qln, kvln, wuq, wuk)


MLA_SCALE = (B_NOPE + B_ROPE) ** -0.5
LOG2E = 1.4426950408889634

def _row_lo():
    return lax.broadcasted_iota(jnp.int32, (LANES, 1), 0) < HEAD_DIM


def _stack_cols(vT, rlo):
    zero = jnp.zeros_like(vT)
    return jnp.concatenate([jnp.where(rlo, vT, zero), jnp.where(rlo, zero, vT)], axis=1)


def _stack_rows(v, lo):
    zero = jnp.zeros_like(v)
    return jnp.concatenate([jnp.where(lo, v, zero), jnp.where(lo, zero, v)], axis=0)


def _pick_halves_T(xT, rlo, t):
    return jnp.where(rlo, xT[:, 0:t], xT[:, t:2 * t]).T


def _pp_fwd(q, k, vT, *, kdiv, tq, tk, sub, name):
    S = k.shape[0]; nb = q.shape[1] // LANES; nq = S // tq; nkv = S // tk; nsub = tk // sub

    def body(q_ref, k_ref, vT_ref, o_ref, lse_ref, qs, m_s, l_s, acc):
        j = pl.program_id(2)
        rlo = _row_lo()

        @pl.when(j == 0)
        def _():
            qs[...] = _stack_cols(q_ref[...].astype(F32).T, rlo).astype(BF16)
            m_s[...] = jnp.full((1, 2 * tq), NEG, F32)
            l_s[...] = jnp.zeros((1, 2 * tq), F32)
            acc[...] = jnp.zeros((LANES, 2 * tq), F32)

        qsv = qs[...]
        m, l, a = m_s[...], l_s[...], acc[...]
        s_cur = _dot(k_ref[0:sub, :], qsv)
        for t in range(nsub):
            if t + 1 < nsub:
                s_next = _dot(k_ref[sub * (t + 1):sub * (t + 2), :], qsv)
            m_new = jnp.maximum(m, jnp.max(s_cur, axis=0, keepdims=True))
            alpha = jnp.exp(m - m_new)
            p = jnp.exp(s_cur - m_new)
            l = alpha * l + jnp.sum(p, axis=0, keepdims=True)
            a = alpha * a + _dot(vT_ref[:, sub * t:sub * (t + 1)], p.astype(BF16))
            m = m_new
            if t + 1 < nsub:
                s_cur = s_next
        m_s[...], l_s[...], acc[...] = m, l, a

        @pl.when(j == nkv - 1)
        def _():
            l_f = l_s[...]
            o_ref[...] = _pick_halves_T(acc[...] / l_f, rlo, tq)
            lse_ref[0, 0] = m_s[...] + jnp.log(l_f)

    return pl.pallas_call(
        body, name=name, grid=(nb, nq, nkv),
        in_specs=[pl.BlockSpec((tq, LANES), lambda b, i, j: (i, b)), pl.BlockSpec((tk, LANES), lambda b, i, j: (j, b // kdiv)),
                  pl.BlockSpec((LANES, tk), lambda b, i, j: (b // kdiv, j))],
        out_specs=[pl.BlockSpec((tq, LANES), lambda b, i, j: (i, b)), pl.BlockSpec((1, 1, 1, 2 * tq), lambda b, i, j: (b, i, 0, 0))],
        out_shape=[_sds((S, nb * LANES), F32), _sds((nb, nq, 1, 2 * tq), F32)],
        scratch_shapes=[pltpu.VMEM((LANES, 2 * tq), BF16), pltpu.VMEM((1, 2 * tq), F32), pltpu.VMEM((1, 2 * tq), F32),
                        pltpu.VMEM((LANES, 2 * tq), F32)],
        compiler_params=_params(("arbitrary",) * 3))(q, k, vT)


def _pp_bwd(q, k, kT, v, o, do, lse, *, kdiv, tq, tk, sub, name):
    S = k.shape[0]; nb = q.shape[1] // LANES; nkb = k.shape[1] // LANES; nq = S // tq; nkv = S // tk; nsub = tk // sub

    def body(q_ref, k_ref, kT_ref, v_ref, o_ref, do_ref, lse_ref, dq_ref, dk_ref, dv_ref, qsT, qs, dosT, dos, delta_s, dq_acc):
        b, i, j = pl.program_id(0), pl.program_id(1), pl.program_id(2)
        rlo = _row_lo()
        lo = lax.broadcasted_iota(jnp.int32, (1, LANES), 1) < HEAD_DIM

        @pl.when((b % kdiv == 0) & (i == 0) & (j == 0))
        def _():
            dk_ref[...] = jnp.zeros((S, LANES), F32)
            dv_ref[...] = jnp.zeros((S, LANES), F32)

        @pl.when(j == 0)
        def _():
            qv = q_ref[...]
            qs[...] = _stack_rows(qv, lo)
            qsT[...] = _stack_cols(qv.astype(F32).T, rlo).astype(BF16)
            dov = do_ref[...]
            dos[...] = _stack_rows(dov.astype(BF16), lo)
            dosT[...] = _stack_cols(dov.T, rlo).astype(BF16)
            prodT = (dov * o_ref[...]).T
            delta_s[...] = jnp.concatenate([jnp.sum(jnp.where(rlo, prodT, 0.0), axis=0, keepdims=True),
                                            jnp.sum(jnp.where(rlo, 0.0, prodT), axis=0, keepdims=True)], axis=1)
            dq_acc[...] = jnp.zeros((LANES, 2 * tq), F32)

        qsTv, dosTv, qsv, dosv = qsT[...], dosT[...], qs[...], dos[...]
        lse_v, delta_v = lse_ref[0, 0], delta_s[...]
        dqa = dq_acc[...]
        s_cur = _dot(k_ref[0:sub, :], qsTv)
        dp_cur = _dot(v_ref[0:sub, :], dosTv)
        for t in range(nsub):
            if t + 1 < nsub:
                s_next = _dot(k_ref[sub * (t + 1):sub * (t + 2), :], qsTv)
                dp_next = _dot(v_ref[sub * (t + 1):sub * (t + 2), :], dosTv)
            p = jnp.exp(s_cur - lse_v)
            ds = (p * (dp_cur - delta_v)).astype(BF16)
            rows = pl.ds(pl.multiple_of(j * tk + sub * t, sub), sub)
            dv_ref[rows, :] += _dot(p.astype(BF16), dosv)
            dk_ref[rows, :] += _dot(ds, qsv)
            dqa = dqa + _dot(kT_ref[:, sub * t:sub * (t + 1)], ds)
            if t + 1 < nsub:
                s_cur, dp_cur = s_next, dp_next
        dq_acc[...] = dqa

        @pl.when(j == nkv - 1)
        def _():
            dq_ref[...] = _pick_halves_T(dq_acc[...], rlo, tq)

    qmap = lambda b, i, j: (i, b)
    kmap = lambda b, i, j: (j, b // kdiv)
    res = lambda b, i, j: (0, b // kdiv)
    return pl.pallas_call(
        body, name=name, grid=(nb, nq, nkv),
        in_specs=[pl.BlockSpec((tq, LANES), qmap), pl.BlockSpec((tk, LANES), kmap), pl.BlockSpec((LANES, tk), lambda b, i, j: (b // kdiv, j)),
                  pl.BlockSpec((tk, LANES), kmap), pl.BlockSpec((tq, LANES), qmap), pl.BlockSpec((tq, LANES), qmap),
                  pl.BlockSpec((1, 1, 1, 2 * tq), lambda b, i, j: (b, i, 0, 0))],
        out_specs=[pl.BlockSpec((tq, LANES), qmap), pl.BlockSpec((S, LANES), res), pl.BlockSpec((S, LANES), res)],
        out_shape=[_sds((S, nb * LANES), F32), _sds((S, nkb * LANES), F32), _sds((S, nkb * LANES), F32)],
        scratch_shapes=[pltpu.VMEM((LANES, 2 * tq), BF16), pltpu.VMEM((2 * tq, LANES), BF16), pltpu.VMEM((LANES, 2 * tq), BF16),
                        pltpu.VMEM((2 * tq, LANES), BF16), pltpu.VMEM((1, 2 * tq), F32), pltpu.VMEM((LANES, 2 * tq), F32)],
        compiler_params=_params(("arbitrary",) * 3))(q, k, kT, v, o, do, lse)


MLA_C = MLA_SCALE * LOG2E


def _mla_fwd(q, kcat, kcatT, *, tq, tk, sub):
    S = kcat.shape[0]; nq, nkv = S // tq, S // tk; R = B_HEADS * tq; nsub = tk // sub

    def body(q_ref, k_ref, vT_ref, o_ref, lse_ref, qT, m_s, l_s, acc):
        j = pl.program_id(1)

        @pl.when(j == 0)
        def _():
            qT[...] = q_ref[...].reshape(R, 2 * LANES).astype(F32).T.astype(BF16)
            m_s[...] = jnp.full((1, R), NEG, F32)
            l_s[...] = jnp.zeros((1, R), F32)
            acc[...] = jnp.zeros((LANES, R), F32)

        qTv = qT[...]
        m, l, a = m_s[...], l_s[...], acc[...]
        s_cur = _dot(k_ref[0:sub, :], qTv)
        for t in range(nsub):
            if t + 1 < nsub:
                s_next = _dot(k_ref[sub * (t + 1):sub * (t + 2), :], qTv)
            m_new = jnp.maximum(m, jnp.max(s_cur, axis=0, keepdims=True))
            alpha = jnp.exp2((m - m_new) * MLA_C)
            p = jnp.exp2((s_cur - m_new) * MLA_C)
            l = alpha * l + jnp.sum(p, axis=0, keepdims=True)
            a = alpha * a + _dot(vT_ref[:, sub * t:sub * (t + 1)], p.astype(BF16))
            m = m_new
            if t + 1 < nsub:
                s_cur = s_next
        m_s[...], l_s[...], acc[...] = m, l, a

        @pl.when(j == nkv - 1)
        def _():
            l_f = l_s[...]
            o_ref[...] = (acc[...] / l_f).T.reshape(B_HEADS, tq, LANES)
            lse_ref[0] = m_s[...] * MLA_SCALE + jnp.log(l_f)

    return pl.pallas_call(
        body, name="mla_fwd", grid=(nq, nkv),
        in_specs=[pl.BlockSpec((B_HEADS, tq, 2 * LANES), lambda i, j: (0, i, 0)), pl.BlockSpec((tk, 2 * LANES), lambda i, j: (j, 0)),
                  pl.BlockSpec((LANES, tk), lambda i, j: (0, j))],
        out_specs=[pl.BlockSpec((B_HEADS, tq, LANES), lambda i, j: (0, i, 0)), pl.BlockSpec((1, 1, R), lambda i, j: (i, 0, 0))],
        out_shape=[_sds((B_HEADS, S, LANES), F32), _sds((nq, 1, R), F32)],
        scratch_shapes=[pltpu.VMEM((2 * LANES, R), BF16), pltpu.VMEM((1, R), F32), pltpu.VMEM((1, R), F32), pltpu.VMEM((LANES, R), F32)],
        compiler_params=_params(("arbitrary", "arbitrary")))(q, kcat, kcatT)


def _mla_bwd(q, kcat, kcatT, o, do, lse, *, tq, tk, sub):
    S = kcat.shape[0]; nq, nkv = S // tq, S // tk; R = B_HEADS * tq; nsub = tk // sub

    def body(q_ref, k_ref, kT_ref, o_ref, do_ref, lse_ref, dq_ref, dk_ref, qT, dosT, dos, delta_s, dq_acc):
        i, j = pl.program_id(0), pl.program_id(1)

        @pl.when((i == 0) & (j == 0))
        def _():
            dk_ref[...] = jnp.zeros((S, 2 * LANES), F32)

        @pl.when(j == 0)
        def _():
            qT[...] = q_ref[...].reshape(R, 2 * LANES).astype(F32).T.astype(BF16)
            dov = do_ref[...].reshape(R, LANES)
            dos[...] = dov.astype(BF16)
            dosT[...] = dov.T.astype(BF16)
            delta_s[...] = jnp.sum((dov * o_ref[...].reshape(R, LANES)).T, axis=0, keepdims=True)
            dq_acc[...] = jnp.zeros((2 * LANES, R), F32)

        qTv, dosTv, dosv = qT[...], dosT[...], dos[...]
        qv = q_ref[...].reshape(R, 2 * LANES)
        lse_v, delta_v = lse_ref[0] * LOG2E, delta_s[...]
        dqa = dq_acc[...]
        s_cur = _dot(k_ref[0:sub, :], qTv)
        dp_cur = _dot(k_ref[0:sub, 0:LANES], dosTv)
        for t in range(nsub):
            if t + 1 < nsub:
                s_next = _dot(k_ref[sub * (t + 1):sub * (t + 2), :], qTv)
                dp_next = _dot(k_ref[sub * (t + 1):sub * (t + 2), 0:LANES], dosTv)
            p = jnp.exp2(s_cur * MLA_C - lse_v)
            ds = (p * (dp_cur - delta_v) * MLA_SCALE).astype(BF16)
            rows = pl.ds(pl.multiple_of(j * tk + sub * t, sub), sub)
            dk_ref[rows, :] += _dot(ds, qv)
            dk_ref[rows, 0:LANES] += _dot(p.astype(BF16), dosv)
            dqa = dqa + _dot(kT_ref[:, sub * t:sub * (t + 1)], ds)
            if t + 1 < nsub:
                s_cur, dp_cur = s_next, dp_next
        dq_acc[...] = dqa

        @pl.when(j == nkv - 1)
        def _():
            dq_ref[...] = dq_acc[...].T.reshape(B_HEADS, tq, 2 * LANES)

    hspec = lambda w: pl.BlockSpec((B_HEADS, tq, w), lambda i, j: (0, i, 0))
    return pl.pallas_call(
        body, name="mla_bwd", grid=(nq, nkv),
        in_specs=[hspec(2 * LANES), pl.BlockSpec((tk, 2 * LANES), lambda i, j: (j, 0)), pl.BlockSpec((2 * LANES, tk), lambda i, j: (0, j)),
                  hspec(LANES), hspec(LANES), pl.BlockSpec((1, 1, R), lambda i, j: (i, 0, 0))],
        out_specs=[hspec(2 * LANES), pl.BlockSpec((S, 2 * LANES), lambda i, j: (0, 0))],
        out_shape=[_sds((B_HEADS, S, 2 * LANES), F32), _sds((S, 2 * LANES), F32)],
        scratch_shapes=[pltpu.VMEM((2 * LANES, R), BF16), pltpu.VMEM((LANES, R), BF16), pltpu.VMEM((R, LANES), BF16),
                        pltpu.VMEM((1, R), F32), pltpu.VMEM((2 * LANES, R), F32)],
        compiler_params=_params(("arbitrary", "arbitrary")))(q, kcat, kcatT, o, do, lse)


def _win_start(i, tq, nk, S):
    return pl.multiple_of(jnp.clip(i * tq - WINDOW, 0, S - nk), LANES)


def _win_dist_table(S, tq):
    nk = min(tq + 2 * WINDOW, S)
    nq = S // tq
    r = np.arange(nk)[:, None]
    c = (np.arange(2 * tq) % tq)[None, :]
    tabs = []
    for rel in (0, WINDOW, (nq - 1) * tq - (S - nk)):
        dist = np.abs(rel + c - r).astype(np.float32)
        tabs.append(np.where(dist <= WINDOW, dist, np.float32(1e32)))
    return jnp.asarray(np.stack(tabs))


def _win_dist_spec(nk, tq, nq):
    return pl.BlockSpec((1, nk, 2 * tq), lambda b, i: (jnp.where(i == 0, 0, jnp.where(i == nq - 1, 2, 1)), 0, 0))


def _win_fwd(q, k, vT, dist, slope, sink, *, kdiv, tq, nbs, name):
    S = k.shape[0]; nb = q.shape[1] // LANES; nq = S // tq; nk = min(tq + 2 * WINDOW, S)
    assert nb % nbs == 0 and kdiv % nbs == 0

    def body(q_ref, k_ref, vT_ref, dist_ref, slope_ref, sink_ref, o_ref, lse_ref):
        i = pl.program_id(1)
        rlo = _row_lo()
        k0 = _win_start(i, tq, nk, S)
        kk, vv, dd = k_ref[pl.ds(k0, nk), :], vT_ref[:, pl.ds(k0, nk)], dist_ref[0]
        for u in range(nbs):
            qsT = _stack_cols(q_ref[:, LANES * u:LANES * (u + 1)].astype(F32).T, rlo).astype(BF16)
            s = _dot(kk, qsT) - slope_ref[u] * dd
            sk = sink_ref[u]
            m = jnp.maximum(jnp.max(s, axis=0, keepdims=True), sk)
            p = jnp.exp(s - m)
            l = jnp.sum(p, axis=0, keepdims=True) + jnp.exp(sk - m)
            o_ref[:, LANES * u:LANES * (u + 1)] = _pick_halves_T(_dot(vv, p.astype(BF16)) / l, rlo, tq)
            lse_ref[u, 0] = m + jnp.log(l)

    row_spec = pl.BlockSpec((nbs, 1, 2 * tq), lambda b, i: (b, 0, 0))
    return pl.pallas_call(
        body, name=name, grid=(nb // nbs, nq),
        in_specs=[pl.BlockSpec((tq, nbs * LANES), lambda b, i: (i, b)), pl.BlockSpec((S, LANES), lambda b, i: (0, b * nbs // kdiv)),
                  pl.BlockSpec((LANES, S), lambda b, i: (b * nbs // kdiv, 0)), _win_dist_spec(nk, tq, nq), row_spec, row_spec],
        out_specs=[pl.BlockSpec((tq, nbs * LANES), lambda b, i: (i, b)), pl.BlockSpec((nbs, 1, 1, 2 * tq), lambda b, i: (b, i, 0, 0))],
        out_shape=[_sds((S, nb * LANES), F32), _sds((nb, nq, 1, 2 * tq), F32)],
        compiler_params=_params(("arbitrary", "arbitrary")))(q, k, vT, dist, slope, sink)


def _win_bwd(q, k, kT, v, o, do, lse, dist, slope, sink, *, kdiv, tq, nbs, name):
    S = k.shape[0]; nb = q.shape[1] // LANES; nkb = k.shape[1] // LANES; nq = S // tq; nk = min(tq + 2 * WINDOW, S)
    steps_per_kv = kdiv // nbs

    def body(q_ref, k_ref, kT_ref, v_ref, o_ref, do_ref, lse_ref, dist_ref, slope_ref, sink_ref, dq_ref, dk_ref, dv_ref, dsink_ref, ds_acc):
        b, i = pl.program_id(0), pl.program_id(1)
        rlo = _row_lo()
        lo = lax.broadcasted_iota(jnp.int32, (1, LANES), 1) < HEAD_DIM

        @pl.when((b % steps_per_kv == 0) & (i == 0))
        def _():
            dk_ref[...] = jnp.zeros((S, LANES), F32)
            dv_ref[...] = jnp.zeros((S, LANES), F32)

        @pl.when(i == 0)
        def _():
            ds_acc[...] = jnp.zeros((nbs, 2 * tq), F32)

        k0 = _win_start(i, tq, nk, S)
        rows = pl.ds(k0, nk)
        kk, vv, kkT, dd = k_ref[rows, :], v_ref[rows, :], kT_ref[:, rows], dist_ref[0]
        dv_sum, dk_sum = None, None
        for u in range(nbs):
            cols = slice(LANES * u, LANES * (u + 1))
            qv = q_ref[:, cols]
            qs = _stack_rows(qv, lo)
            qsT = _stack_cols(qv.astype(F32).T, rlo).astype(BF16)
            dov = do_ref[:, cols]
            dos = _stack_rows(dov.astype(BF16), lo)
            dosT = _stack_cols(dov.T, rlo).astype(BF16)
            prodT = (dov * o_ref[:, cols]).T
            delta = jnp.concatenate([jnp.sum(jnp.where(rlo, prodT, 0.0), axis=0, keepdims=True),
                                     jnp.sum(jnp.where(rlo, 0.0, prodT), axis=0, keepdims=True)], axis=1)
            lse_v = lse_ref[u, 0]
            ds_acc[u:u + 1, :] += -jnp.exp(sink_ref[u] - lse_v) * delta
            p = jnp.exp(_dot(kk, qsT) - slope_ref[u] * dd - lse_v)
            ds = (p * (_dot(vv, dosT) - delta)).astype(BF16)
            dv_u, dk_u = _dot(p.astype(BF16), dos), _dot(ds, qs)
            dv_sum = dv_u if dv_sum is None else dv_sum + dv_u
            dk_sum = dk_u if dk_sum is None else dk_sum + dk_u
            dq_ref[:, cols] = _pick_halves_T(_dot(kkT, ds), rlo, tq)
        dv_ref[rows, :] += dv_sum
        dk_ref[rows, :] += dk_sum

        @pl.when(i == nq - 1)
        def _():
            acc = ds_acc[...]
            for u in range(nbs):
                dsink_ref[u] = jnp.concatenate(
                    [jnp.broadcast_to(jnp.sum(acc[u:u + 1, 0:tq], axis=1, keepdims=True), (1, LANES)),
                     jnp.broadcast_to(jnp.sum(acc[u:u + 1, tq:2 * tq], axis=1, keepdims=True), (1, LANES)),
                     jnp.zeros((6, LANES), F32)], axis=0)

    qmap = lambda b, i: (i, b)
    kv_spec = pl.BlockSpec((S, LANES), lambda b, i: (0, b * nbs // kdiv))
    row_spec = pl.BlockSpec((nbs, 1, 2 * tq), lambda b, i: (b, 0, 0))
    wide = pl.BlockSpec((tq, nbs * LANES), qmap)
    return pl.pallas_call(
        body, name=name, grid=(nb // nbs, nq),
        in_specs=[wide, kv_spec, pl.BlockSpec((LANES, S), lambda b, i: (b * nbs // kdiv, 0)), kv_spec, wide, wide,
                  pl.BlockSpec((nbs, 1, 1, 2 * tq), lambda b, i: (b, i, 0, 0)), _win_dist_spec(nk, tq, nq), row_spec, row_spec],
        out_specs=[wide, kv_spec, kv_spec, pl.BlockSpec((nbs, 8, LANES), lambda b, i: (b, 0, 0))],
        out_shape=[_sds((S, nb * LANES), F32), _sds((S, nkb * LANES), F32), _sds((S, nkb * LANES), F32), _sds((nb, 8, LANES), F32)],
        scratch_shapes=[pltpu.VMEM((nbs, 2 * tq), F32)],
        compiler_params=_params(("arbitrary", "arbitrary")))(q, k, kT, v, o, do, lse, dist, slope, sink)


def _sum_rows(v):
    return jnp.sum(v, axis=0, keepdims=True)


def _norm_mod_bwd(dh, xv, mod_ref, nw_ref, stats_ref):
    r = _rms(xv)
    xn = xv * r
    nw = nw_ref[...]
    stats_ref[0:1, :] += _sum_rows(dh)
    stats_ref[1:2, :] += _sum_rows(dh * (xn * nw))
    dn = dh * (1.0 + mod_ref[1:2, :])
    stats_ref[2:3, :] += _sum_rows(dn * xn)
    return _rms_bwd(xv, r, dn * nw)


def _even_post_fwd(oa, olat, proj, x, gate, wuv, woe):
    S = x.shape[0]
    ts = min(ROW_TILE, S)

    def body(oa_ref, ol_ref, proj_ref, x_ref, gate_ref, wuv_ref, woe_ref, y_ref, x1_ref):
        sa, _ = _silu_and_grad(proj_ref[:, 768:1280])
        sb, _ = _silu_and_grad(proj_ref[:, 1792:2304])
        olc = jnp.concatenate([ol_ref[hh] for hh in range(B_HEADS)], axis=1).astype(BF16)
        ob = _dot(olc, wuv_ref[...])
        mix = jnp.concatenate([oa_ref[...] * sa, ob * sb], axis=1).astype(BF16)
        y = _dot(mix, woe_ref[...])
        y_ref[...] = y
        x1_ref[...] = x_ref[...] + gate_ref[...] * y

    return pl.pallas_call(
        body, name="even_post_fwd", grid=(S // ts,),
        in_specs=[_row_spec(ts, 512), pl.BlockSpec((B_HEADS, ts, LANES), lambda i: (0, i, 0)), _row_spec(ts, EVEN_P),
                  _row_spec(ts, D_MODEL), _full_spec((1, D_MODEL)), _full_spec((1024, 512)), _full_spec((1024, D_MODEL))],
        out_specs=[_row_spec(ts, D_MODEL), _row_spec(ts, D_MODEL)],
        out_shape=[_sds((S, D_MODEL), F32), _sds((S, D_MODEL), F32)],
        compiler_params=_params(("arbitrary",)),
    )(oa, olat, proj, x, gate, wuv, woe)


def _odd_pre_fwd(x, mod, nw, wio):
    S = x.shape[0]
    ts = min(ROW_TILE, S)

    def body(x_ref, mod_ref, nw_ref, wio_ref, h_ref, proj_ref, q_ref, k_ref, v_ref):
        xv = x_ref[...]
        h = (xv * _rms(xv) * nw_ref[...]) * (1.0 + mod_ref[1:2, :]) + mod_ref[0:1, :]
        hb = h.astype(BF16)
        h_ref[...] = hb
        proj = _dot(hb, wio_ref[...])
        proj_ref[...] = proj
        q_ref[...] = (proj[:, 0:1024] * 0.125).astype(BF16)
        k_ref[...] = proj[:, 1024:1280].astype(BF16)
        v_ref[...] = proj[:, 1280:1536].astype(BF16)

    return pl.pallas_call(
        body, name="odd_pre_fwd", grid=(S // ts,),
        in_specs=[_row_spec(ts, D_MODEL), _full_spec((3, D_MODEL)), _full_spec((1, D_MODEL)), _full_spec((D_MODEL, ODD_IN))],
        out_specs=[_row_spec(ts, D_MODEL), _row_spec(ts, ODD_IN), _row_spec(ts, 1024), _row_spec(ts, 256), _row_spec(ts, 256)],
        out_shape=[_sds((S, D_MODEL), BF16), _sds((S, ODD_IN), F32), _sds((S, 1024), BF16), _sds((S, 256), BF16),
                   _sds((S, 256), BF16)],
        compiler_params=_params(("arbitrary",)),
    )(x, mod, nw, wio)


def _odd_post(oc, proj, x1, gate, woo, fw, tgt):
    S = x1.shape[0]
    ts = min(ROW_TILE, S)

    def body(oc_ref, proj_ref, x_ref, gate_ref, woo_ref, fw_ref, tgt_ref, doc_ref, dgc_ref, dx2_ref, dwoo_ref, stats_ref):
        @pl.when(pl.program_id(0) == 0)
        def _():
            dwoo_ref[...] = jnp.zeros((D_MODEL, D_MODEL), F32)
            stats_ref[...] = jnp.zeros((8, D_MODEL), F32)

        ocv = oc_ref[...]
        sg, dsg = _silu_and_grad(proj_ref[:, 1536:2560])
        mix = (ocv * sg).astype(BF16)
        woo_v = woo_ref[...]
        y = _dot(mix, woo_v)
        gate_v = gate_ref[...]
        x2 = x_ref[...] + gate_v * y
        r = _rms(x2)
        fw_v = fw_ref[...]
        xn = x2 * r
        err = xn * fw_v - tgt_ref[...]
        dout = err * (1.0 / D_MODEL)
        dx2 = _rms_bwd(x2, r, dout * fw_v)
        dx2_ref[...] = dx2
        stats_ref[0:1, :] += _sum_rows(dout * xn)
        stats_ref[1:2, :] += _sum_rows(dx2 * y)
        loss_t = 0.5 * jnp.sum(_sum_rows(err * dout), axis=-1, keepdims=True)
        stats_ref[2:3, :] += jnp.broadcast_to(loss_t, (1, D_MODEL))
        dy = (gate_v * dx2).astype(BF16)
        dmix = _dot_nt(dy, woo_v)
        dwoo_ref[...] += _dot_tn(mix, dy)
        doc_ref[...] = dmix * sg
        dgc_ref[...] = dmix * ocv * dsg

    return pl.pallas_call(
        body, name="odd_post", grid=(S // ts,),
        in_specs=[_row_spec(ts, D_MODEL), _row_spec(ts, ODD_IN), _row_spec(ts, D_MODEL), _full_spec((1, D_MODEL)),
                  _full_spec((D_MODEL, D_MODEL)), _full_spec((1, D_MODEL)), _row_spec(ts, D_MODEL)],
        out_specs=[_row_spec(ts, D_MODEL), _row_spec(ts, D_MODEL), _row_spec(ts, D_MODEL),
                   _full_spec((D_MODEL, D_MODEL), single=False), _full_spec((8, D_MODEL), single=False)],
        out_shape=[_sds((S, D_MODEL), F32), _sds((S, D_MODEL), F32), _sds((S, D_MODEL), F32), _sds((D_MODEL, D_MODEL), F32),
                   _sds((8, D_MODEL), F32)],
        compiler_params=_params(("arbitrary",)),
    )(oc, proj, x1, gate, woo, fw, tgt)


def _odd_pre_bwd(dq, dk, dv, dgc, h, x, dx_res, mod, nw, wio):
    S = x.shape[0]
    ts = min(ROW_TILE, S)

    def body(dq_ref, dk_ref, dv_ref, dgc_ref, h_ref, x_ref, dxr_ref, mod_ref, nw_ref, wio_ref, dx_ref, dw_ref, stats_ref):
        @pl.when(pl.program_id(0) == 0)
        def _():
            dw_ref[...] = jnp.zeros((D_MODEL, ODD_IN), F32)
            stats_ref[...] = jnp.zeros((8, D_MODEL), F32)

        dproj = jnp.concatenate([dq_ref[...] * 0.125, dk_ref[...], dv_ref[...], dgc_ref[...]], axis=1).astype(BF16)
        dh = _dot_nt(dproj, wio_ref[...])
        dw_ref[...] += _dot_tn(h_ref[...], dproj)
        dx_ref[...] = dxr_ref[...] + _norm_mod_bwd(dh, x_ref[...], mod_ref, nw_ref, stats_ref)

    return pl.pallas_call(
        body, name="odd_pre_bwd", grid=(S // ts,),
        in_specs=[_row_spec(ts, 1024), _row_spec(ts, 256), _row_spec(ts, 256), _row_spec(ts, 1024), _row_spec(ts, D_MODEL),
                  _row_spec(ts, D_MODEL), _row_spec(ts, D_MODEL), _full_spec((3, D_MODEL)), _full_spec((1, D_MODEL)),
                  _full_spec((D_MODEL, ODD_IN))],
        out_specs=[_row_spec(ts, D_MODEL), _full_spec((D_MODEL, ODD_IN), single=False), _full_spec((8, D_MODEL), single=False)],
        out_shape=[_sds((S, D_MODEL), F32), _sds((D_MODEL, ODD_IN), F32), _sds((8, D_MODEL), F32)],
        compiler_params=_params(("arbitrary",)),
    )(dq, dk, dv, dgc, h, x, dx_res, mod, nw, wio)


def _even_post_bwd(dx1, y, oa, olat, proj, gate, wuv, woe):
    S = dx1.shape[0]
    ts = min(ROW_TILE, S)

    def body(dx_ref, y_ref, oa_ref, ol_ref, proj_ref, gate_ref, wuv_ref, woe_ref,
             doa_ref, dga_ref, dgb_ref, dol_ref, dwoe_ref, dwuv_ref, stats_ref):
        @pl.when(pl.program_id(0) == 0)
        def _():
            dwoe_ref[...] = jnp.zeros((D_MODEL, D_MODEL), F32)
            dwuv_ref[...] = jnp.zeros((1024, 512), F32)
            stats_ref[...] = jnp.zeros((8, D_MODEL), F32)

        dxv = dx_ref[...]
        stats_ref[0:1, :] += _sum_rows(dxv * y_ref[...])
        dy = (gate_ref[...] * dxv).astype(BF16)
        sa, dsa = _silu_and_grad(proj_ref[:, 768:1280])
        sb, dsb = _silu_and_grad(proj_ref[:, 1792:2304])
        olc = jnp.concatenate([ol_ref[hh] for hh in range(B_HEADS)], axis=1).astype(BF16)
        wuv_v = wuv_ref[...]
        ob = _dot(olc, wuv_v)
        oav = oa_ref[...]
        mix = jnp.concatenate([oav * sa, ob * sb], axis=1).astype(BF16)
        dmix = _dot_nt(dy, woe_ref[...])
        dwoe_ref[...] += _dot_tn(mix, dy)
        dma, dmb = dmix[:, 0:512], dmix[:, 512:1024]
        doa_ref[...] = dma * sa
        dga_ref[...] = dma * oav * dsa
        dgb_ref[...] = dmb * ob * dsb
        dob = (dmb * sb).astype(BF16)
        dol = _dot_nt(dob, wuv_v)
        dwuv_ref[...] += _dot_tn(olc, dob)
        for hh in range(B_HEADS):
            dol_ref[hh] = dol[:, LANES * hh:LANES * (hh + 1)]

    head_spec = pl.BlockSpec((B_HEADS, ts, LANES), lambda i: (0, i, 0))
    return pl.pallas_call(
        body, name="even_post_bwd", grid=(S // ts,),
        in_specs=[_row_spec(ts, D_MODEL), _row_spec(ts, D_MODEL), _row_spec(ts, 512), head_spec, _row_spec(ts, EVEN_P),
                  _full_spec((1, D_MODEL)), _full_spec((1024, 512)), _full_spec((1024, D_MODEL))],
        out_specs=[_row_spec(ts, 512), _row_spec(ts, 512), _row_spec(ts, 512), head_spec,
                   _full_spec((D_MODEL, D_MODEL), single=False), _full_spec((1024, 512), single=False),
                   _full_spec((8, D_MODEL), single=False)],
        out_shape=[_sds((S, 512), F32), _sds((S, 512), F32), _sds((S, 512), F32), _sds((B_HEADS, S, LANES), F32),
                   _sds((D_MODEL, D_MODEL), F32), _sds((1024, 512), F32), _sds((8, D_MODEL), F32)],
        compiler_params=_params(("arbitrary",)),
    )(dx1, y, oa, olat, proj, gate, wuv, woe)


def _even_pre_bwd(x, h, proj, dqa, dka, dva, dga, dgb, dqcat, dkcat, dx_res, mod, nw, wie, qn, kn, seg, ca, sa, ct, st,
                  qln, kvln, wuq, wuk):
    S = x.shape[0]
    ts = min(ROW_TILE, S)
    nsteps = S // ts

    def body(x_ref, h_ref, proj_ref, dqa_ref, dka_ref, dva_ref, dga_ref, dgb_ref, dqc_ref, dkc_ref, dxr_ref, mod_ref, nw_ref,
             wie_ref, qn_ref, kn_ref, seg_ref, ca_ref, sa_ref, ct_ref, st_ref, qln_ref, kvln_ref, wuq_ref, wuk_ref,
             dx_ref, dwie_out, dwuq_out, dwuk_out, stats_ref, nstats_ref, dwie_ref, dwuq_ref, dwuk_ref):
        @pl.when(pl.program_id(0) == 0)
        def _():
            dwie_ref[...] = jnp.zeros((D_MODEL, EVEN_P), F32)
            dwuq_ref[...] = jnp.zeros((B_Q_LORA, 1536), F32)
            dwuk_ref[...] = jnp.zeros((512, 1024), F32)
            stats_ref[...] = jnp.zeros((8, D_MODEL), F32)
            nstats_ref[...] = jnp.zeros((8, 256), F32)

        lane = _lane_iota()
        ca_v, sa_v, ct_v, st_v = ca_ref[...], sa_ref[...], ct_ref[...], st_ref[...]
        seg_v = seg_ref[...]

        def head_norm_bwd(xc, dy, w):
            r = lax.rsqrt(_seg_mean(xc * xc, seg_v) + EPS)
            g = dy * w
            dxc = r * g - xc * (r * r * r) * _seg_mean(xc * g, seg_v)
            return dxc, _sum_rows(dy * (xc * r))

        pieces = []
        dqn = jnp.zeros((1, LANES), F32)
        for cb in range(4):
            sl = slice(LANES * cb, LANES * (cb + 1))
            dy = _rot_bwd(dqa_ref[:, sl] * 0.125, ca_v, sa_v, lane)
            dxc, dw = head_norm_bwd(proj_ref[:, sl], dy, qn_ref[...])
            pieces.append(dxc)
            dqn = dqn + dw
        dxc, dkn = head_norm_bwd(proj_ref[:, 512:640], _rot_bwd(dka_ref[...], ca_v, sa_v, lane), kn_ref[...])
        pieces += [dxc, dva_ref[...], dga_ref[...]]
        nstats_ref[0:1, 0:LANES] += dqn + pltpu.roll(dqn, HEAD_DIM, 1)
        nstats_ref[1:2, 0:LANES] += dkn + pltpu.roll(dkn, HEAD_DIM, 1)

        cq = proj_ref[:, 1280:1536]
        rq = _rms(cq)
        cqn_f = cq * rq
        qln_v = qln_ref[...]
        cqn = (cqn_f * qln_v).astype(BF16)
        wuq_v, wuk_v = wuq_ref[...], wuk_ref[...]
        qnope = _dot(cqn, wuq_v[:, 0:512]).astype(BF16)
        dqlat = jnp.concatenate([dqc_ref[hh, :, 0:LANES] for hh in range(B_HEADS)], axis=1).astype(BF16)
        dqnope = _dot_nt(dqlat, wuk_v)
        dwuk_ref[...] += _dot_tn(qnope, dqlat)
        dqr = [_rot_bwd(dqc_ref[hh, :, LANES:2 * LANES], ct_v, st_v, lane) for hh in range(B_HEADS)]
        dqb = jnp.concatenate([dqnope] + dqr, axis=1).astype(BF16)
        dcqn = _dot_nt(dqb, wuq_v)
        dwuq_ref[...] += _dot_tn(cqn, dqb)
        nstats_ref[2:3, :] += _sum_rows(dcqn * cqn_f)
        dcq = _rms_bwd(cq, rq, dcqn * qln_v)
        ckv = proj_ref[:, 1536:1664]
        rk = _rms(ckv)
        dckvn = dkc_ref[:, 0:LANES]
        nstats_ref[3:4, 0:LANES] += _sum_rows(dckvn * (ckv * rk))
        dckv = _rms_bwd(ckv, rk, dckvn * kvln_ref[...])
        dkr = _rot_bwd(dkc_ref[:, LANES:2 * LANES], ct_v, st_v, lane)
        pieces += [dcq, dckv, dkr, dgb_ref[...]]
        dproj = jnp.concatenate(pieces, axis=1).astype(BF16)
        dh = _dot_nt(dproj, wie_ref[...])
        dwie_ref[...] += _dot_tn(h_ref[...], dproj)
        dx_ref[...] = dxr_ref[...] + _norm_mod_bwd(dh, x_ref[...], mod_ref, nw_ref, stats_ref)

        @pl.when(pl.program_id(0) == nsteps - 1)
        def _():
            pltpu.sync_copy(dwie_ref, dwie_out)
            pltpu.sync_copy(dwuq_ref, dwuq_out)
            pltpu.sync_copy(dwuk_ref, dwuk_out)

    return pl.pallas_call(
        body, name="even_pre_bwd", grid=(nsteps,),
        in_specs=[_row_spec(ts, D_MODEL), _row_spec(ts, D_MODEL), _row_spec(ts, EVEN_P), _row_spec(ts, 512), _row_spec(ts, LANES),
                  _row_spec(ts, LANES), _row_spec(ts, 512), _row_spec(ts, 512),
                  pl.BlockSpec((B_HEADS, ts, 2 * LANES), lambda i: (0, i, 0)), _row_spec(ts, 2 * LANES), _row_spec(ts, D_MODEL),
                  _full_spec((3, D_MODEL)), _full_spec((1, D_MODEL)), _full_spec((D_MODEL, EVEN_P)),
                  _full_spec((1, LANES)), _full_spec((1, LANES)), _full_spec((LANES, LANES)),
                  _row_spec(ts, LANES), _row_spec(ts, LANES), _row_spec(ts, LANES), _row_spec(ts, LANES),
                  _full_spec((1, B_Q_LORA)), _full_spec((1, B_KV_LORA)), _full_spec((B_Q_LORA, 1536)), _full_spec((512, 1024))],
        out_specs=[_row_spec(ts, D_MODEL), _ANY, _ANY, _ANY, _full_spec((8, D_MODEL), single=False), _full_spec((8, 256), single=False)],
        out_shape=[_sds((S, D_MODEL), F32), _sds((D_MODEL, EVEN_P), F32), _sds((B_Q_LORA, 1536), F32), _sds((512, 1024), F32),
                   _sds((8, D_MODEL), F32), _sds((8, 256), F32)],
        scratch_shapes=[pltpu.VMEM((D_MODEL, EVEN_P), F32), pltpu.VMEM((B_Q_LORA, 1536), F32), pltpu.VMEM((512, 1024), F32)],
        compiler_params=_params(("arbitrary",)),
    )(x, h, proj, dqa, dka, dva, dga, dgb, dqcat, dkcat, dx_res, mod, nw, wie, qn, kn, seg, ca, sa, ct, st, qln, kvln, wuq, wuk)


def _ada_fwd(c_all, w, b):
    n = w.shape[2]

    def body(c_ref, w_ref, b_ref, o_ref):
        cv = c_ref[...]
        o_ref[0] = _dot_f32(cv * _sigmoid(cv), w_ref[0]) + b_ref[0]

    return pl.pallas_call(
        body, name="ada_fwd", grid=(2,),
        in_specs=[pl.BlockSpec((N_DEV, D_MODEL), lambda l: (0, 0)), pl.BlockSpec((1, D_MODEL, n), lambda l: (l, 0, 0)),
                  pl.BlockSpec((1, 1, n), lambda l: (l, 0, 0))],
        out_specs=pl.BlockSpec((1, N_DEV, n), lambda l: (l, 0, 0)),
        out_shape=_sds((2, N_DEV, n), F32),
        compiler_params=_params(("arbitrary",)),
    )(c_all, w, b)


def _ada_bwd(c_all_t, dmod):
    n = dmod.shape[2]

    def body(c_ref, d_ref, o_ref):
        cv = c_ref[...]
        act = cv * _sigmoid(cv)
        dv = d_ref[0]
        acc = act[:, 0:1] * dv[0:1, :]
        for bb in range(1, N_DEV):
            acc = acc + act[:, bb:bb + 1] * dv[bb:bb + 1, :]
        o_ref[0] = acc

    return pl.pallas_call(
        body, name="ada_bwd", grid=(2,),
        in_specs=[pl.BlockSpec((D_MODEL, N_DEV), lambda l: (0, 0)), pl.BlockSpec((1, N_DEV, n), lambda l: (l, 0, 0))],
        out_specs=pl.BlockSpec((1, D_MODEL, n), lambda l: (l, 0, 0)),
        out_shape=_sds((2, D_MODEL, n), F32),
        compiler_params=_params(("arbitrary",)),
    )(c_all_t, dmod)


ADAM_ROW_TILE = 256


def _adam(parts, w, m, v, name):
    P, R, C = parts.shape
    tr = R if R <= ADAM_ROW_TILE else ADAM_ROW_TILE
    assert R % tr == 0

    def body(p_ref, w_ref, m_ref, v_ref, g_ref, d_ref, nm_ref, nv_ref):
        g = p_ref[0].astype(F32)
        for k in range(1, P):
            g = g + p_ref[k].astype(F32)
        g_ref[...] = g
        m_new = ADAM_B1 * m_ref[...] + (1.0 - ADAM_B1) * g
        v_new = ADAM_B2 * v_ref[...] + (1.0 - ADAM_B2) * jnp.square(g)
        m_hat = m_new / (1.0 - ADAM_B1 ** ADAM_STEP)
        v_hat = v_new / (1.0 - ADAM_B2 ** ADAM_STEP)
        d_ref[...] = -ADAM_LR * (m_hat / (jnp.sqrt(v_hat) + ADAM_EPS) + ADAM_WD * w_ref[...])
        nm_ref[...] = m_new
        nv_ref[...] = v_new

    spec = pl.BlockSpec((tr, C), lambda i: (i, 0))
    return pl.pallas_call(
        body, name=name, grid=(R // tr,),
        in_specs=[pl.BlockSpec((P, tr, C), lambda i: (0, i, 0)), spec, spec, spec],
        out_specs=[spec, spec, spec, spec], out_shape=[_sds((R, C), F32)] * 4,
        compiler_params=_params(("arbitrary",)),
    )(parts, w, m, v)


_ANY = pl.BlockSpec(memory_space=pl.ANY)
CHIP_FLIPS = ((1, 0), (0, 1), (1, 1))
DEV_FLIPS = tuple((dx, dy, dc) for dx in (0, 1) for dy in (0, 1) for dc in (0, 1) if dx + dy + dc)


def _flip(a, d):
    return a if d == 0 else 1 - a


def _my_place():
    return lax.axis_index("x"), lax.axis_index("y"), lax.axis_index("c")


def _gather_dev8(arrs, name):
    n = len(arrs)

    def body(*refs):
        ins, outs = refs[:n], refs[n:2 * n]
        send_sems, recv_sems, loc_sems = refs[2 * n:]
        x, y, c = _my_place()
        me = 4 * x + 2 * y + c
        copies = []
        for a in range(n):
            loc = pltpu.make_async_copy(ins[a], outs[a].at[me], loc_sems.at[a])
            loc.start()
            copies.append(loc)
            for k, (dx, dy, dc) in enumerate(DEV_FLIPS):
                cp = pltpu.make_async_remote_copy(
                    src_ref=ins[a], dst_ref=outs[a].at[me], send_sem=send_sems.at[a, k], recv_sem=recv_sems.at[a, k],
                    device_id=(_flip(x, dx), _flip(y, dy), _flip(c, dc)), device_id_type=MESH_ID)
                cp.start()
                copies.append(cp)
        for cp in copies:
            cp.wait()

    return pl.pallas_call(
        body, name=name, in_specs=[_ANY] * n, out_specs=[_ANY] * n,
        out_shape=[_sds((N_DEV,) + a.shape, a.dtype) for a in arrs],
        scratch_shapes=[pltpu.SemaphoreType.DMA((n, 7)), pltpu.SemaphoreType.DMA((n, 7)), pltpu.SemaphoreType.DMA((n,))],
    )(*arrs)


def _gather_chip4_halves(arrs, name):
    n = len(arrs)

    def body(*refs):
        ins, outs = refs[:n], refs[n:2 * n]
        send_sems, recv_sems, loc_sems = refs[2 * n:]
        x, y, c = _my_place()
        chip = 2 * x + y
        sibling = (x, y, 1 - c)

        def remote(src, p, half, a, k, to):
            return pltpu.make_async_remote_copy(src_ref=src, dst_ref=outs[a].at[p, half], send_sem=send_sems.at[a, k],
                                                recv_sem=recv_sems.at[a, k], device_id=to, device_id_type=MESH_ID)

        sends, locs = [], []
        for a in range(n):
            loc = pltpu.make_async_copy(ins[a], outs[a].at[chip], loc_sems.at[a])
            loc.start()
            locs.append(loc)
            for k, (dx, dy) in enumerate(CHIP_FLIPS):
                cp = remote(ins[a].at[c], chip, c, a, k, (_flip(x, dx), _flip(y, dy), c))
                cp.start()
                sends.append(cp)
        for a in range(n):
            for k, (dx, dy) in enumerate(CHIP_FLIPS):
                p = 2 * _flip(x, dx) + _flip(y, dy)
                remote(outs[a].at[p, c], p, c, a, k, sibling).wait_recv()
                fwd = remote(outs[a].at[p, c], p, c, a, 3 + k, sibling)
                fwd.start()
                sends.append(fwd)
        for a in range(n):
            for k, (dx, dy) in enumerate(CHIP_FLIPS):
                p = 2 * _flip(x, dx) + _flip(y, dy)
                remote(outs[a].at[p, 1 - c], p, 1 - c, a, 3 + k, sibling).wait_recv()
        for cp in sends:
            cp.wait_send()
        for loc in locs:
            loc.wait()

    return pl.pallas_call(
        body, name=name, in_specs=[_ANY] * n, out_specs=[_ANY] * n,
        out_shape=[_sds((N_CHIPS,) + a.shape, a.dtype) for a in arrs],
        scratch_shapes=[pltpu.SemaphoreType.DMA((n, 6)), pltpu.SemaphoreType.DMA((n, 6)), pltpu.SemaphoreType.DMA((n,))],
    )(*arrs)


def _reduce_exchange(arrs, name):
    n = len(arrs)

    def body(*refs):
        ins, outs = refs[:n], refs[n:2 * n]
        send_sems, recv_sems, loc_sems = refs[2 * n:]
        x, y, c = _my_place()
        chip = 2 * x + y
        sibling = (x, y, 1 - c)

        def remote(src, slot, a, k, to):
            return pltpu.make_async_remote_copy(src_ref=src, dst_ref=outs[a].at[slot], send_sem=send_sems.at[a, k],
                                                recv_sem=recv_sems.at[a, k], device_id=to, device_id_type=MESH_ID)

        sends, locs = [], []
        for a in range(n):
            loc = pltpu.make_async_copy(ins[a].at[chip], outs[a].at[2 * chip + c], loc_sems.at[a])
            loc.start()
            locs.append(loc)
            first = [remote(ins[a].at[chip], 2 * chip + c, a, 0, sibling)]
            for k, (dx, dy) in enumerate(CHIP_FLIPS):
                px, py = _flip(x, dx), _flip(y, dy)
                first.append(remote(ins[a].at[2 * px + py], 2 * chip + c, a, 1 + k, (px, py, c)))
            for cp in first:
                cp.start()
            sends += first
        for a in range(n):
            for k, (dx, dy) in enumerate(CHIP_FLIPS):
                slot = 2 * (2 * _flip(x, dx) + _flip(y, dy)) + c
                remote(outs[a].at[slot], slot, a, 1 + k, sibling).wait_recv()
                fwd = remote(outs[a].at[slot], slot, a, 4 + k, sibling)
                fwd.start()
                sends.append(fwd)
        for a in range(n):
            remote(outs[a].at[2 * chip + 1 - c], 2 * chip + 1 - c, a, 0, sibling).wait_recv()
            for k, (dx, dy) in enumerate(CHIP_FLIPS):
                slot = 2 * (2 * _flip(x, dx) + _flip(y, dy)) + 1 - c
                remote(outs[a].at[slot], slot, a, 4 + k, sibling).wait_recv()
        for cp in sends:
            cp.wait_send()
        for loc in locs:
            loc.wait()

    return pl.pallas_call(
        body, name=name, in_specs=[_ANY] * n, out_specs=[_ANY] * n,
        out_shape=[_sds((N_DEV,) + a.shape[1:], a.dtype) for a in arrs],
        scratch_shapes=[pltpu.SemaphoreType.DMA((n, 7)), pltpu.SemaphoreType.DMA((n, 7)), pltpu.SemaphoreType.DMA((n,))],
    )(*arrs)


def _pair_order(nheads, nkv):
    group = nheads // nkv
    order = []
    for m in range(nkv // 2):
        for i in range(group):
            order += [2 * m * group + i, (2 * m + 1) * group + i]
    return order


A_ORDER = _pair_order(A_HEADS, A_KV_HEADS)
C_ORDER = _pair_order(C_HEADS, C_KV_HEADS)
A_INV = [int(k) for k in np.argsort(A_ORDER)]
C_INV = [int(k) for k in np.argsort(C_ORDER)]


def _perm_heads(w, order, axis):
    return jnp.concatenate([lax.slice_in_dim(w, HEAD_DIM * h, HEAD_DIM * (h + 1), axis=axis) for h in order], axis=axis)


def _even_in_layout(w):
    return jnp.concatenate([_perm_heads(w[:, 0:512], A_ORDER, 1), w[:, 512:768], _perm_heads(w[:, 768:1280], A_ORDER, 1),
                            w[:, 1280:1696], jnp.zeros((w.shape[0], 96), w.dtype), w[:, 1696:2208]], axis=1)


def _even_in_unlayout(g):
    return jnp.concatenate([_perm_heads(g[:, 0:512], A_INV, 1), g[:, 512:768], _perm_heads(g[:, 768:1280], A_INV, 1),
                            g[:, 1280:1696], g[:, 1792:2304]], axis=1)


def _even_out_layout(w):
    return jnp.concatenate([_perm_heads(w[0:512], A_ORDER, 0), w[512:1024]], axis=0)


def _even_out_unlayout(g):
    return jnp.concatenate([_perm_heads(g[0:512], A_INV, 0), g[512:1024]], axis=0)


def _odd_in_layout(w):
    return jnp.concatenate([_perm_heads(w[:, 0:1024], C_ORDER, 1), w[:, 1024:1536], _perm_heads(w[:, 1536:2560], C_ORDER, 1)],
                           axis=1)


def _odd_in_unlayout(g):
    return jnp.concatenate([_perm_heads(g[:, 0:1024], C_INV, 1), g[:, 1024:1536], _perm_heads(g[:, 1536:2560], C_INV, 1)],
                           axis=1)


def _uq_layout(w):
    per = B_NOPE + B_ROPE
    pad = jnp.zeros((w.shape[0], LANES - B_ROPE), w.dtype)
    nope = [w[:, per * h:per * h + B_NOPE] for h in range(B_HEADS)]
    rope = [jnp.concatenate([w[:, per * h + B_NOPE:per * (h + 1)], pad], axis=1) for h in range(B_HEADS)]
    return jnp.concatenate(nope + rope, axis=1)


def _uq_unlayout(g):
    parts = []
    for h in range(B_HEADS):
        parts += [g[:, B_NOPE * h:B_NOPE * (h + 1)], g[:, 512 + LANES * h:512 + LANES * h + B_ROPE]]
    return jnp.concatenate(parts, axis=1)


def _block_diag(blocks):
    rows = []
    for h, blk in enumerate(blocks):
        r, cdim = blk.shape
        n = len(blocks)
        rows.append(jnp.concatenate([jnp.zeros((r, cdim * h), blk.dtype), blk, jnp.zeros((r, cdim * (n - 1 - h)), blk.dtype)],
                                    axis=1))
    return jnp.concatenate(rows, axis=0)


def _uk_layout(w):
    return _block_diag([w[:, h, :].T for h in range(B_HEADS)])


def _uk_unlayout(g):
    return jnp.stack([g[B_NOPE * h:B_NOPE * (h + 1), LANES * h:LANES * (h + 1)].T for h in range(B_HEADS)], axis=1)


def _uv_layout(w):
    return _block_diag([w[:, h, :] for h in range(B_HEADS)])


def _uv_unlayout(g):
    return jnp.stack([g[LANES * h:LANES * (h + 1), B_V * h:B_V * (h + 1)] for h in range(B_HEADS)], axis=1)


def _rope_tables(S):
    inv = ROPE_THETA ** (-jnp.arange(0, 32, 2, dtype=F32) / 32)
    tok = jnp.arange(S)

    def tab(pos):
        ang = pos.astype(F32)[:, None] * inv[None, :]
        cos, sin = jnp.cos(ang), jnp.sin(ang)
        return jnp.concatenate([cos, cos], axis=1), jnp.concatenate([-sin, sin], axis=1)

    cr, sr = tab(tok // GRID_W)
    cc, sc = tab(tok % GRID_W)
    ct, st = tab(tok)
    return (jnp.tile(jnp.concatenate([cr, cc], axis=1), (1, 2)), jnp.tile(jnp.concatenate([sr, sc], axis=1), (1, 2)),
            jnp.tile(ct, (1, 4)), jnp.tile(st, (1, 4)))


A_TQ, A_TK, A_SUB = 256, 4096, 1024
B_TQ, B_TK, B_SUB = 128, 4096, 1024
B_BWD_TK, B_BWD_SUB = 2048, 512
C_T = 256
C_BLOCKS_PER_STEP = 4


def _local_step(x0, tgt, mod, norm_w, wie, wuq, wuk, wuv, woe, wio, woo, a_q_norm, a_k_norm, q_lora_norm, kv_lora_norm,
                c_sink, final_norm):
    S = x0.shape[0]
    mod3 = mod.reshape(2, 3, D_MODEL)
    ca, sa, ct, st = _rope_tables(S)
    lane_seg = np.arange(LANES) // HEAD_DIM
    seg = jnp.asarray((lane_seg[:, None] == lane_seg[None, :]).astype(np.float32)).astype(BF16)
    qn = jnp.tile(a_q_norm.reshape(1, HEAD_DIM), (1, 2))
    kn = jnp.tile(a_k_norm.reshape(1, HEAD_DIM), (1, 2))
    qln, kvln = q_lora_norm.reshape(1, B_Q_LORA), kv_lora_norm.reshape(1, B_KV_LORA)
    nw0, nw1 = norm_w[0:1], norm_w[1:2]
    gate0, gate1 = mod3[0, 2:3], mod3[1, 2:3]
    a_tq, a_tk, b_tq, b_tk, bb_tk, c_t = min(A_TQ, S), min(A_TK, S), min(B_TQ, S), min(B_TK, S), min(B_BWD_TK, S), min(C_T, S)
    a_sub, b_sub, bb_sub = min(A_SUB, a_tk), min(B_SUB, b_tk), min(B_BWD_SUB, bb_tk)

    h0, proj_e, qa, ka, va, qcat, kcat = _even_pre_fwd(x0, mod3[0], nw0, wie, qn, kn, seg, ca, sa, ct, st, qln, kvln, wuq, wuk)
    kcat_t = kcat.T
    oa, lse_a = _pp_fwd(qa, ka, va.T, kdiv=4, tq=a_tq, tk=a_tk, sub=a_sub, name="attn_a_fwd")
    olat, lse_b = _mla_fwd(qcat, kcat, kcat_t, tq=b_tq, tk=b_tk, sub=b_sub)
    y0, x1 = _even_post_fwd(oa, olat, proj_e, x0, gate0, wuv, woe)
    h1, proj_o, qc, kc, vc = _odd_pre_fwd(x1, mod3[1], nw1, wio)
    slopes = 2.0 ** (-8.0 * jnp.arange(1, C_HEADS + 1, dtype=F32) / C_HEADS)
    c_order = np.asarray(C_ORDER)
    slope_rows = jnp.repeat(slopes[c_order].reshape(C_HEADS // 2, 2), c_t, axis=1)[:, None, :]
    sink_rows = jnp.repeat(c_sink.reshape(C_HEADS)[c_order].reshape(C_HEADS // 2, 2), c_t, axis=1)[:, None, :]
    win_dist = _win_dist_table(S, c_t)
    oc, lse_c = _win_fwd(qc, kc, vc.T, win_dist, slope_rows, sink_rows, kdiv=4, tq=c_t, nbs=C_BLOCKS_PER_STEP, name="attn_c_fwd")
    doc, dgc, dx2, dwoo, st_f = _odd_post(oc, proj_o, x1, gate1, woo, final_norm.reshape(1, D_MODEL), tgt)
    dqc, dkc, dvc, dsink_raw = _win_bwd(qc, kc, kc.T, vc, oc, doc, lse_c, win_dist, slope_rows, sink_rows, kdiv=4, tq=c_t,
                                        nbs=C_BLOCKS_PER_STEP, name="attn_c_bwd")
    dx1, dwio, st_1 = _odd_pre_bwd(dqc, dkc, dvc, dgc, h1, x1, dx2, mod3[1], nw1, wio)
    doa, dga, dgb, dolat, dwoe, dwuv, st_e = _even_post_bwd(dx1, y0, oa, olat, proj_e, gate0, wuv, woe)
    dqa, dka, dva = _pp_bwd(qa, ka, ka.T, va, oa, doa, lse_a, kdiv=4, tq=a_tq, tk=a_tk, sub=a_sub, name="attn_a_bwd")
    dqcat, dkcat = _mla_bwd(qcat, kcat, kcat_t, olat, dolat, lse_b, tq=b_tq, tk=bb_tk, sub=bb_sub)
    dx0, dwie, dwuq, dwuk, st_0, nst = _even_pre_bwd(x0, h0, proj_e, dqa, dka, dva, dga, dgb, dqcat, dkcat, dx1, mod3[0], nw0,
                                                     wie, qn, kn, seg, ca, sa, ct, st, qln, kvln, wuq, wuk)
    dsink_pairs = jnp.stack([dsink_raw[:, 0, 0], dsink_raw[:, 1, 0]], axis=1).reshape(C_HEADS)
    return dict(
        loss=st_f[2, 0], dx=dx0,
        dmod=jnp.stack([jnp.concatenate([st_0[0], st_0[1], st_e[0]]), jnp.concatenate([st_1[0], st_1[1], st_f[1]])]),
        norm_w=jnp.stack([st_0[2], st_1[2]]), final_norm=st_f[0],
        a_q_norm=nst[0:1, 0:HEAD_DIM], a_k_norm=nst[1:2, 0:HEAD_DIM], b_q_lora_norm=nst[2:3, :], b_kv_lora_norm=nst[3:4, 0:LANES],
        c_sink=dsink_pairs[np.asarray(C_INV)].reshape(1, C_HEADS),
        even_w_in=dwie, b_w_uq=dwuq, b_w_uk=dwuk, b_w_uv=dwuv, even_w_out=dwoe, odd_w_in=dwio, odd_w_out=dwoo)


WEIGHT_NAMES = ("norm_w", "ada_w", "ada_b", "even_w_in", "a_q_norm", "a_k_norm", "b_q_lora_norm", "b_kv_lora_norm", "b_w_uq",
                "b_w_uk", "b_w_uv", "even_w_out", "odd_w_in", "c_sink", "odd_w_out", "final_norm")
SMALL_NAMES = ("dmod", "norm_w", "final_norm", "a_q_norm", "a_k_norm", "b_q_lora_norm", "b_kv_lora_norm", "c_sink")


def _cols_to_chips(g):
    r, n4 = g.shape
    return jnp.transpose(g.reshape(r, N_CHIPS, n4 // N_CHIPS), (1, 0, 2))


def _chips_to_cols(g):
    p, r, n = g.shape
    return jnp.transpose(g, (1, 0, 2)).reshape(r, p * n)


def kernel(x, c, norm_w, ada_w, ada_b, even_w_in, a_q_norm, a_k_norm, b_q_lora_norm, b_kv_lora_norm, b_w_uq, b_w_uk, b_w_uv, even_w_out, odd_w_in, c_sink, odd_w_out, final_norm, loss_target, m_norm_w, m_ada_w, m_ada_b, m_even_w_in, m_a_q_norm, m_a_k_norm, m_b_q_lora_norm, m_b_kv_lora_norm, m_b_w_uq, m_b_w_uk, m_b_w_uv, m_even_w_out, m_odd_w_in, m_c_sink, m_odd_w_out, m_final_norm, v_norm_w, v_ada_w, v_ada_b, v_even_w_in, v_a_q_norm, v_a_k_norm, v_b_q_lora_norm, v_b_kv_lora_norm, v_b_w_uq, v_b_w_uk, v_b_w_uv, v_even_w_out, v_odd_w_in, v_c_sink, v_odd_w_out, v_final_norm):
    given = dict(locals())
    xi, yi, ci = _my_place()
    chip = 2 * xi + yi
    dev = 2 * chip + ci
    n_ada = ada_w.shape[2]

    (c_all,) = _gather_dev8([c], "gather_c")
    c_all = c_all.reshape(N_DEV, D_MODEL)
    bias = lax.dynamic_slice_in_dim(ada_b, chip * n_ada, n_ada, axis=1).reshape(2, 1, n_ada)
    mod_cols = _ada_fwd(c_all, ada_w, bias)
    def halves(w):
        return w.astype(BF16).reshape((2, w.shape[0] // 2) + w.shape[1:])

    gathered = _gather_chip4_halves(
        [mod_cols, halves(even_w_in[0]), halves(b_w_uq[0]), halves(even_w_out[0]), halves(odd_w_in[0]), halves(odd_w_out[0])],
        "gather_weights")
    mod_all = gathered[0]
    wie_g, wuq_g, woe_g, wio_g, woo_g = [g.reshape((N_CHIPS, 2 * g.shape[2]) + g.shape[3:]) for g in gathered[1:]]
    mod = jnp.transpose(lax.dynamic_index_in_dim(mod_all, dev, axis=2, keepdims=False), (1, 0, 2)).reshape(2, 3 * D_MODEL)

    res = _local_step(
        x[0], loss_target[0], mod, norm_w,
        _even_in_layout(_chips_to_cols(wie_g)), _uq_layout(_chips_to_cols(wuq_g)), _uk_layout(b_w_uk[0].astype(BF16)),
        _uv_layout(b_w_uv[0].astype(BF16)), _even_out_layout(woe_g.reshape(D_MODEL, D_MODEL)), _odd_in_layout(_chips_to_cols(wio_g)),
        _perm_heads(woo_g.reshape(D_MODEL, D_MODEL), C_ORDER, 0), a_q_norm, a_k_norm, b_q_lora_norm, b_kv_lora_norm, c_sink,
        final_norm)

    shard_parts = dict(zip(
        ("even_w_in", "b_w_uq", "even_w_out", "odd_w_in", "odd_w_out"),
        _reduce_exchange(
            [_cols_to_chips(_even_in_unlayout(res["even_w_in"].astype(BF16))), _cols_to_chips(_uq_unlayout(res["b_w_uq"].astype(BF16))),
             _even_out_unlayout(res["even_w_out"].astype(BF16)).reshape(N_CHIPS, D_MODEL // N_CHIPS, D_MODEL),
             _cols_to_chips(_odd_in_unlayout(res["odd_w_in"].astype(BF16))),
             _perm_heads(res["odd_w_out"].astype(BF16), C_INV, 0).reshape(N_CHIPS, D_MODEL // N_CHIPS, D_MODEL)],
            "reduce_exchange")))

    small = jnp.concatenate([res[k].reshape(-1) for k in SMALL_NAMES]).reshape(1, -1)
    latent = jnp.stack([_uk_unlayout(res["b_w_uk"]).reshape(B_KV_LORA, 512),
                        _uv_unlayout(res["b_w_uv"]).reshape(B_KV_LORA, 512)]).astype(BF16)
    small_all, latent_all = _gather_dev8([small, latent], "gather_small")
    small_all = small_all.reshape(N_DEV, -1)
    parts, off = {}, 0
    for k in SMALL_NAMES:
        n = int(np.prod(res[k].shape))
        parts[k] = small_all[:, off:off + n]
        off += n
    dmod_all = parts.pop("dmod").reshape(N_DEV, 2, 3 * D_MODEL)
    dmod_cols = jnp.transpose(lax.dynamic_slice_in_dim(dmod_all, chip * n_ada, n_ada, axis=2), (1, 0, 2))
    parts["ada_w"] = _ada_bwd(c_all.T, dmod_cols).reshape(1, 2 * D_MODEL, n_ada)
    parts["ada_b"] = dmod_all
    parts["b_w_uk"], parts["b_w_uv"] = latent_all[:, 0], latent_all[:, 1]
    parts.update(shard_parts)

    grads, deltas, new_m, new_v = [], [], [], []
    for k in WEIGHT_NAMES:
        w = given[k]
        p = parts[k]
        shape2 = (p.shape[-2], p.shape[-1]) if p.ndim == 3 else (1, p.shape[-1])
        p = p.reshape((p.shape[0],) + shape2)
        outs = _adam(p, w.reshape(shape2), given["m_" + k].reshape(shape2), given["v_" + k].reshape(shape2), "adam_" + k)
        for lst, o in zip((grads, deltas, new_m, new_v), outs):
            lst.append(o.reshape(w.shape))
    loss = lax.psum(res["loss"], ("x", "y", "c"))
    return (loss, res["dx"][None], *grads, *deltas, *new_m, *new_v)
```

```python
import functools

import numpy as np
import jax
import jax.numpy as jnp
from jax import lax
from jax.experimental import pallas as pl
from jax.experimental.pallas import tpu as pltpu

F32 = jnp.float32
BF16 = jnp.bfloat16
HIGHEST = lax.Precision.HIGHEST
MESH_ID = pl.DeviceIdType.MESH

D_MODEL = 1024
HEAD_DIM = 64
GRID_W = 64
EPS = 1e-6
ROPE_THETA = 10000.0
A_HEADS, A_KV_HEADS = 8, 2
B_HEADS, B_NOPE, B_ROPE, B_V = 8, 64, 32, 64
B_Q_LORA, B_KV_LORA = 256, 128
C_HEADS, C_KV_HEADS = 16, 4
WINDOW = 128
EVEN_IN, ODD_IN = 2208, 2560
EVEN_P = 2304
N_CHIPS, N_DEV = 4, 8
LANES = 128
NEG = -1e30
VMEM_LIMIT = 60 * 1024 * 1024

ADAM_LR, ADAM_B1, ADAM_B2, ADAM_EPS, ADAM_WD, ADAM_STEP = 0.001, 0.9, 0.999, 1e-08, 0.01, 10

ROW_TILE = 256


def _dot(a, b):
    return lax.dot_general(a, b, (((1,), (0,)), ((), ())), preferred_element_type=F32)


def _dot_nt(a, b):
    return lax.dot_general(a, b, (((1,), (1,)), ((), ())), preferred_element_type=F32)


def _dot_tn(a, b):
    return lax.dot_general(a, b, (((0,), (0,)), ((), ())), preferred_element_type=F32)


def _dot_f32(a, b):
    return lax.dot_general(a, b, (((1,), (0,)), ((), ())), precision=HIGHEST, preferred_element_type=F32)


def _sigmoid(x):
    return 1.0 / (1.0 + jnp.exp(-x))


def _silu_and_grad(g):
    s = _sigmoid(g)
    return g * s, s * (1.0 + g * (1.0 - s))


def _lane_iota():
    return lax.broadcasted_iota(jnp.int32, (1, LANES), 1)


def _partner(x, lane):
    return jnp.where((lane % 32) < 16, pltpu.roll(x, LANES - 16, 1), pltpu.roll(x, 16, 1))


def _rot(x, cos, sin_signed, lane):
    return x * cos + _partner(x, lane) * sin_signed


def _rot_bwd(dy, cos, sin_signed, lane):
    return dy * cos + _partner(dy * sin_signed, lane)


def _rms(x):
    return lax.rsqrt(jnp.mean(x * x, axis=-1, keepdims=True) + EPS)


def _rms_bwd(x, r, g):
    return r * g - x * (r * r * r) * jnp.mean(x * g, axis=-1, keepdims=True)


def _seg_mean(v, seg_ones):
    hi = v.astype(BF16)
    lo = (v - hi.astype(F32)).astype(BF16)
    return (_dot(hi, seg_ones) + _dot(lo, seg_ones)) * (1.0 / HEAD_DIM)


def _row_spec(ts, cols):
    return pl.BlockSpec((ts, cols), lambda i: (i, 0))


def _full_spec(shape, single=True):
    nd = len(shape)
    if single:
        return pl.BlockSpec(shape, lambda i: (0,) * nd, pipeline_mode=pl.Buffered(1))
    return pl.BlockSpec(shape, lambda i: (0,) * nd)


def _sds(shape, dtype):
    return jax.ShapeDtypeStruct(shape, dtype)


def _params(sem):
    return pltpu.CompilerParams(dimension_semantics=sem, vmem_limit_bytes=VMEM_LIMIT)


def _even_pre_fwd(x, mod, nw, wie, qn, kn, seg, ca, sa, ct, st, qln, kvln, wuq, wuk):
    S = x.shape[0]
    ts = min(ROW_TILE, S)

    def body(x_ref, mod_ref, nw_ref, wie_ref, qn_ref, kn_ref, seg_ref, ca_ref, sa_ref, ct_ref, st_ref, qln_ref,
             kvln_ref, wuq_ref, wuk_ref, h_ref, proj_ref, qa_ref, ka_ref, va_ref, qcat_ref, kcat_ref):
        xv = x_ref[...]
        h = (xv * _rms(xv) * nw_ref[...]) * (1.0 + mod_ref[1:2, :]) + mod_ref[0:1, :]
        hb = h.astype(BF16)
        h_ref[...] = hb
        proj = _dot(hb, wie_ref[...])
        proj_ref[...] = proj
        lane = _lane_iota()
        ca_v, sa_v, ct_v, st_v = ca_ref[...], sa_ref[...], ct_ref[...], st_ref[...]
        seg_v = seg_ref[...]
        for cb in range(4):
            xc = proj[:, LANES * cb:LANES * (cb + 1)]
            r = lax.rsqrt(_seg_mean(xc * xc, seg_v) + EPS)
            y = _rot(xc * r * qn_ref[...], ca_v, sa_v, lane)
            qa_ref[:, LANES * cb:LANES * (cb + 1)] = (y * 0.125).astype(BF16)
        kc = proj[:, 512:640]
        r = lax.rsqrt(_seg_mean(kc * kc, seg_v) + EPS)
        ka_ref[...] = _rot(kc * r * kn_ref[...], ca_v, sa_v, lane).astype(BF16)
        va_ref[...] = proj[:, 640:768].astype(BF16)
        cq = proj[:, 1280:1536]
        cqn = (cq * _rms(cq) * qln_ref[...]).astype(BF16)
        ckv = proj[:, 1536:1664]
        ckvn = ckv * _rms(ckv) * kvln_ref[...]
        qb = _dot(cqn, wuq_ref[...])
        qlat = _dot(qb[:, 0:512].astype(BF16), wuk_ref[...])
        for hh in range(B_HEADS):
            qcat_ref[hh, :, 0:LANES] = qlat[:, LANES * hh:LANES * (hh + 1)].astype(BF16)
            qr = _rot(qb[:, 512 + LANES * hh:512 + LANES * (hh + 1)], ct_v, st_v, lane)
            qcat_ref[hh, :, LANES:2 * LANES] = qr.astype(BF16)
        kcat_ref[:, 0:LANES] = ckvn.astype(BF16)
        kcat_ref[:, LANES:2 * LANES] = _rot(proj[:, 1664:1792], ct_v, st_v, lane).astype(BF16)

    return pl.pallas_call(
        body, name="even_pre_fwd", grid=(S // ts,),
        in_specs=[_row_spec(ts, D_MODEL), _full_spec((3, D_MODEL)), _full_spec((1, D_MODEL)), _full_spec((D_MODEL, EVEN_P)),
                  _full_spec((1, LANES)), _full_spec((1, LANES)), _full_spec((LANES, LANES)),
                  _row_spec(ts, LANES), _row_spec(ts, LANES), _row_spec(ts, LANES), _row_spec(ts, LANES),
                  _full_spec((1, B_Q_LORA)), _full_spec((1, B_KV_LORA)), _full_spec((B_Q_LORA, 1536)), _full_spec((512, 1024))],
        out_specs=[_row_spec(ts, D_MODEL), _row_spec(ts, EVEN_P), _row_spec(ts, 512), _row_spec(ts, LANES), _row_spec(ts, LANES),
                   pl.BlockSpec((B_HEADS, ts, 2 * LANES), lambda i: (0, i, 0)), _row_spec(ts, 2 * LANES)],
        out_shape=[_sds((S, D_MODEL), BF16), _sds((S, EVEN_P), F32), _sds((S, 512), BF16), _sds((S, LANES), BF16),
                   _sds((S, LANES), BF16), _sds((B_HEADS, S, 2 * LANES), BF16), _sds((S, 2 * LANES), BF16)],
        compiler_params=_params(("arbitrary",)),
    )(x, mod, nw, wie, qn, kn, seg, ca, sa, ct, st, qln, kvln, wuq, wuk)


MLA_SCALE = (B_NOPE + B_ROPE) ** -0.5
LOG2E = 1.4426950408889634

def _row_lo():
    return lax.broadcasted_iota(jnp.int32, (LANES, 1), 0) < HEAD_DIM


def _stack_cols(vT, rlo):
    zero = jnp.zeros_like(vT)
    return jnp.concatenate([jnp.where(rlo, vT, zero), jnp.where(rlo, zero, vT)], axis=1)


def _stack_rows(v, lo):
    zero = jnp.zeros_like(v)
    return jnp.concatenate([jnp.where(lo, v, zero), jnp.where(lo, zero, v)], axis=0)


def _pick_halves_T(xT, rlo, t):
    return jnp.where(rlo, xT[:, 0:t], xT[:, t:2 * t]).T


def _side_split(refs, n_in, n_out, n_scratch, side):
    ns = side.n if side is not None else 0
    cuts = np.cumsum([0, n_in, ns, n_out, ns, n_scratch])
    return [refs[a:b] for a, b in zip(cuts[:-1], cuts[1:])] + [refs[cuts[-1]:]]


def _side_hooks(side, side_ins, side_outs, side_sems, step, total):
    if side is None:
        return lambda: None
    start, mid, end = side.phases(side_ins, side_outs, side_sems)
    pl.when(step == 0)(start)
    pl.when(step == total // 2)(mid)
    return lambda: pl.when(step == total - 1)(end)


def _side_specs(side):
    if side is None:
        return [], [], [], [], []
    return list(side.arrs), [_ANY] * side.n, [_ANY] * side.n, list(side.out_shapes), side.sem_shapes()


def _pp_fwd(q, k, vT, *, kdiv, tq, tk, sub, name, side=None):
    S = k.shape[0]; nb = q.shape[1] // LANES; nq = S // tq; nkv = S // tk; nsub = tk // sub

    def body(*refs):
        (q_ref, k_ref, vT_ref), side_ins, (o_ref, lse_ref), side_outs, (qs, m_s, l_s, acc), side_sems = _side_split(refs, 3, 2, 4, side)
        j = pl.program_id(2)
        rlo = _row_lo()
        step = (pl.program_id(0) * nq + pl.program_id(1)) * nkv + j
        side_end = _side_hooks(side, side_ins, side_outs, side_sems, step, nb * nq * nkv)

        @pl.when(j == 0)
        def _():
            qs[...] = _stack_cols(q_ref[...].astype(F32).T, rlo).astype(BF16)
            m_s[...] = jnp.full((1, 2 * tq), NEG, F32)
            l_s[...] = jnp.zeros((1, 2 * tq), F32)
            acc[...] = jnp.zeros((LANES, 2 * tq), F32)

        qsv = qs[...]
        m, l, a = m_s[...], l_s[...], acc[...]
        s_cur = _dot(k_ref[0:sub, :], qsv)
        for t in range(nsub):
            if t + 1 < nsub:
                s_next = _dot(k_ref[sub * (t + 1):sub * (t + 2), :], qsv)
            m_new = jnp.maximum(m, jnp.max(s_cur, axis=0, keepdims=True))
            alpha = jnp.exp(m - m_new)
            p = jnp.exp(s_cur - m_new)
            l = alpha * l + jnp.sum(p, axis=0, keepdims=True)
            a = alpha * a + _dot(vT_ref[:, sub * t:sub * (t + 1)], p.astype(BF16))
            m = m_new
            if t + 1 < nsub:
                s_cur = s_next
        m_s[...], l_s[...], acc[...] = m, l, a

        @pl.when(j == nkv - 1)
        def _():
            l_f = l_s[...]
            o_ref[...] = _pick_halves_T(acc[...] / l_f, rlo, tq)
            lse_ref[0, 0] = m_s[...] + jnp.log(l_f)

        side_end()

    s_args, s_in, s_out, s_shapes, s_sems = _side_specs(side)
    return pl.pallas_call(
        body, name=name, grid=(nb, nq, nkv),
        in_specs=[pl.BlockSpec((tq, LANES), lambda b, i, j: (i, b)), pl.BlockSpec((tk, LANES), lambda b, i, j: (j, b // kdiv)),
                  pl.BlockSpec((LANES, tk), lambda b, i, j: (b // kdiv, j))] + s_in,
        out_specs=[pl.BlockSpec((tq, LANES), lambda b, i, j: (i, b)),
                   pl.BlockSpec((1, 1, 1, 2 * tq), lambda b, i, j: (b, i, 0, 0))] + s_out,
        out_shape=[_sds((S, nb * LANES), F32), _sds((nb, nq, 1, 2 * tq), F32)] + s_shapes,
        scratch_shapes=[pltpu.VMEM((LANES, 2 * tq), BF16), pltpu.VMEM((1, 2 * tq), F32), pltpu.VMEM((1, 2 * tq), F32),
                        pltpu.VMEM((LANES, 2 * tq), F32)] + s_sems,
        compiler_params=_params(("arbitrary",) * 3))(q, k, vT, *s_args)


def _pp_bwd(q, k, kT, v, o, do, lse, *, kdiv, tq, tk, sub, name, side=None):
    S = k.shape[0]; nb = q.shape[1] // LANES; nkb = k.shape[1] // LANES; nq = S // tq; nkv = S // tk; nsub = tk // sub

    def body(*refs):
        ((q_ref, k_ref, kT_ref, v_ref, o_ref, do_ref, lse_ref), side_ins, (dq_ref, dk_ref, dv_ref), side_outs,
         (qsT, qs, dosT, dos, delta_s, dq_acc), side_sems) = _side_split(refs, 7, 3, 6, side)
        b, i, j = pl.program_id(0), pl.program_id(1), pl.program_id(2)
        rlo = _row_lo()
        lo = lax.broadcasted_iota(jnp.int32, (1, LANES), 1) < HEAD_DIM
        side_end = _side_hooks(side, side_ins, side_outs, side_sems, (b * nq + i) * nkv + j, nb * nq * nkv)

        @pl.when((b % kdiv == 0) & (i == 0) & (j == 0))
        def _():
            dk_ref[...] = jnp.zeros((S, LANES), F32)
            dv_ref[...] = jnp.zeros((S, LANES), F32)

        @pl.when(j == 0)
        def _():
            qv = q_ref[...]
            qs[...] = _stack_rows(qv, lo)
            qsT[...] = _stack_cols(qv.astype(F32).T, rlo).astype(BF16)
            dov = do_ref[...]
            dos[...] = _stack_rows(dov.astype(BF16), lo)
            dosT[...] = _stack_cols(dov.T, rlo).astype(BF16)
            prodT = (dov * o_ref[...]).T
            delta_s[...] = jnp.concatenate([jnp.sum(jnp.where(rlo, prodT, 0.0), axis=0, keepdims=True),
                                            jnp.sum(jnp.where(rlo, 0.0, prodT), axis=0, keepdims=True)], axis=1)
            dq_acc[...] = jnp.zeros((LANES, 2 * tq), F32)

        qsTv, dosTv, qsv, dosv = qsT[...], dosT[...], qs[...], dos[...]
        lse_v, delta_v = lse_ref[0, 0], delta_s[...]
        dqa = dq_acc[...]
        s_cur = _dot(k_ref[0:sub, :], qsTv)
        dp_cur = _dot(v_ref[0:sub, :], dosTv)
        for t in range(nsub):
            if t + 1 < nsub:
                s_next = _dot(k_ref[sub * (t + 1):sub * (t + 2), :], qsTv)
                dp_next = _dot(v_ref[sub * (t + 1):sub * (t + 2), :], dosTv)
            p = jnp.exp(s_cur - lse_v)
            ds = (p * (dp_cur - delta_v)).astype(BF16)
            rows = pl.ds(pl.multiple_of(j * tk + sub * t, sub), sub)
            dv_ref[rows, :] += _dot(p.astype(BF16), dosv)
            dk_ref[rows, :] += _dot(ds, qsv)
            dqa = dqa + _dot(kT_ref[:, sub * t:sub * (t + 1)], ds)
            if t + 1 < nsub:
                s_cur, dp_cur = s_next, dp_next
        dq_acc[...] = dqa

        @pl.when(j == nkv - 1)
        def _():
            dq_ref[...] = _pick_halves_T(dq_acc[...], rlo, tq)

        side_end()

    qmap = lambda b, i, j: (i, b)
    kmap = lambda b, i, j: (j, b // kdiv)
    res = lambda b, i, j: (0, b // kdiv)
    s_args, s_in, s_out, s_shapes, s_sems = _side_specs(side)
    return pl.pallas_call(
        body, name=name, grid=(nb, nq, nkv),
        in_specs=[pl.BlockSpec((tq, LANES), qmap), pl.BlockSpec((tk, LANES), kmap), pl.BlockSpec((LANES, tk), lambda b, i, j: (b // kdiv, j)),
                  pl.BlockSpec((tk, LANES), kmap), pl.BlockSpec((tq, LANES), qmap), pl.BlockSpec((tq, LANES), qmap),
                  pl.BlockSpec((1, 1, 1, 2 * tq), lambda b, i, j: (b, i, 0, 0))] + s_in,
        out_specs=[pl.BlockSpec((tq, LANES), qmap), pl.BlockSpec((S, LANES), res), pl.BlockSpec((S, LANES), res)] + s_out,
        out_shape=[_sds((S, nb * LANES), F32), _sds((S, nkb * LANES), F32), _sds((S, nkb * LANES), F32)] + s_shapes,
        scratch_shapes=[pltpu.VMEM((LANES, 2 * tq), BF16), pltpu.VMEM((2 * tq, LANES), BF16), pltpu.VMEM((LANES, 2 * tq), BF16),
                        pltpu.VMEM((2 * tq, LANES), BF16), pltpu.VMEM((1, 2 * tq), F32), pltpu.VMEM((LANES, 2 * tq), F32)] + s_sems,
        compiler_params=_params(("arbitrary",) * 3))(q, k, kT, v, o, do, lse, *s_args)


MLA_C = MLA_SCALE * LOG2E


def _mla_fwd(q, kcat, kcatT, *, tq, tk, sub):
    S = kcat.shape[0]; nq, nkv = S // tq, S // tk; R = B_HEADS * tq; nsub = tk // sub

    def body(q_ref, k_ref, vT_ref, o_ref, lse_ref, qT, m_s, l_s, acc):
        j = pl.program_id(1)

        @pl.when(j == 0)
        def _():
            qT[...] = q_ref[...].reshape(R, 2 * LANES).astype(F32).T.astype(BF16)
            m_s[...] = jnp.full((1, R), NEG, F32)
            l_s[...] = jnp.zeros((1, R), F32)
            acc[...] = jnp.zeros((LANES, R), F32)

        qTv = qT[...]
        m, l, a = m_s[...], l_s[...], acc[...]
        s_cur = _dot(k_ref[0:sub, :], qTv)
        for t in range(nsub):
            if t + 1 < nsub:
                s_next = _dot(k_ref[sub * (t + 1):sub * (t + 2), :], qTv)
            m_new = jnp.maximum(m, jnp.max(s_cur, axis=0, keepdims=True))
            alpha = jnp.exp2((m - m_new) * MLA_C)
            p = jnp.exp2((s_cur - m_new) * MLA_C)
            l = alpha * l + jnp.sum(p, axis=0, keepdims=True)
            a = alpha * a + _dot(vT_ref[:, sub * t:sub * (t + 1)], p.astype(BF16))
            m = m_new
            if t + 1 < nsub:
                s_cur = s_next
        m_s[...], l_s[...], acc[...] = m, l, a

        @pl.when(j == nkv - 1)
        def _():
            l_f = l_s[...]
            o_ref[...] = (acc[...] / l_f).T.reshape(B_HEADS, tq, LANES)
            lse_ref[0] = m_s[...] * MLA_SCALE + jnp.log(l_f)

    return pl.pallas_call(
        body, name="mla_fwd", grid=(nq, nkv),
        in_specs=[pl.BlockSpec((B_HEADS, tq, 2 * LANES), lambda i, j: (0, i, 0)), pl.BlockSpec((tk, 2 * LANES), lambda i, j: (j, 0)),
                  pl.BlockSpec((LANES, tk), lambda i, j: (0, j))],
        out_specs=[pl.BlockSpec((B_HEADS, tq, LANES), lambda i, j: (0, i, 0)), pl.BlockSpec((1, 1, R), lambda i, j: (i, 0, 0))],
        out_shape=[_sds((B_HEADS, S, LANES), F32), _sds((nq, 1, R), F32)],
        scratch_shapes=[pltpu.VMEM((2 * LANES, R), BF16), pltpu.VMEM((1, R), F32), pltpu.VMEM((1, R), F32), pltpu.VMEM((LANES, R), F32)],
        compiler_params=_params(("arbitrary", "arbitrary")))(q, kcat, kcatT)


def _mla_bwd(q, kcat, kcatT, o, do, lse, *, tq, tk, sub):
    S = kcat.shape[0]; nq, nkv = S // tq, S // tk; R = B_HEADS * tq; nsub = tk // sub

    def body(q_ref, k_ref, kT_ref, o_ref, do_ref, lse_ref, dq_ref, dk_ref, qT, dosT, dos, delta_s, dq_acc):
        i, j = pl.program_id(0), pl.program_id(1)

        @pl.when((i == 0) & (j == 0))
        def _():
            dk_ref[...] = jnp.zeros((S, 2 * LANES), F32)

        @pl.when(j == 0)
        def _():
            qT[...] = q_ref[...].reshape(R, 2 * LANES).astype(F32).T.astype(BF16)
            dov = do_ref[...].reshape(R, LANES)
            dos[...] = dov.astype(BF16)
            dosT[...] = dov.T.astype(BF16)
            delta_s[...] = jnp.sum((dov * o_ref[...].reshape(R, LANES)).T, axis=0, keepdims=True)
            dq_acc[...] = jnp.zeros((2 * LANES, R), F32)

        qTv, dosTv, dosv = qT[...], dosT[...], dos[...]
        qv = q_ref[...].reshape(R, 2 * LANES)
        lse_v, delta_v = lse_ref[0] * LOG2E, delta_s[...]
        dqa = dq_acc[...]
        s_cur = _dot(k_ref[0:sub, :], qTv)
        dp_cur = _dot(k_ref[0:sub, 0:LANES], dosTv)
        for t in range(nsub):
            if t + 1 < nsub:
                s_next = _dot(k_ref[sub * (t + 1):sub * (t + 2), :], qTv)
                dp_next = _dot(k_ref[sub * (t + 1):sub * (t + 2), 0:LANES], dosTv)
            p = jnp.exp2(s_cur * MLA_C - lse_v)
            ds = (p * (dp_cur - delta_v) * MLA_SCALE).astype(BF16)
            rows = pl.ds(pl.multiple_of(j * tk + sub * t, sub), sub)
            dk_ref[rows, :] += _dot(ds, qv)
            dk_ref[rows, 0:LANES] += _dot(p.astype(BF16), dosv)
            dqa = dqa + _dot(kT_ref[:, sub * t:sub * (t + 1)], ds)
            if t + 1 < nsub:
                s_cur, dp_cur = s_next, dp_next
        dq_acc[...] = dqa

        @pl.when(j == nkv - 1)
        def _():
            dq_ref[...] = dq_acc[...].T.reshape(B_HEADS, tq, 2 * LANES)

    hspec = lambda w: pl.BlockSpec((B_HEADS, tq, w), lambda i, j: (0, i, 0))
    return pl.pallas_call(
        body, name="mla_bwd", grid=(nq, nkv),
        in_specs=[hspec(2 * LANES), pl.BlockSpec((tk, 2 * LANES), lambda i, j: (j, 0)), pl.BlockSpec((2 * LANES, tk), lambda i, j: (0, j)),
                  hspec(LANES), hspec(LANES), pl.BlockSpec((1, 1, R), lambda i, j: (i, 0, 0))],
        out_specs=[hspec(2 * LANES), pl.BlockSpec((S, 2 * LANES), lambda i, j: (0, 0))],
        out_shape=[_sds((B_HEADS, S, 2 * LANES), F32), _sds((S, 2 * LANES), F32)],
        scratch_shapes=[pltpu.VMEM((2 * LANES, R), BF16), pltpu.VMEM((LANES, R), BF16), pltpu.VMEM((R, LANES), BF16),
                        pltpu.VMEM((1, R), F32), pltpu.VMEM((2 * LANES, R), F32)],
        compiler_params=_params(("arbitrary", "arbitrary")))(q, kcat, kcatT, o, do, lse)


def _win_start(i, tq, nk, S):
    return pl.multiple_of(jnp.clip(i * tq - WINDOW, 0, S - nk), LANES)


def _win_dist_table(S, tq):
    nk = min(tq + 2 * WINDOW, S)
    nq = S // tq
    r = np.arange(nk)[:, None]
    c = (np.arange(2 * tq) % tq)[None, :]
    tabs = []
    for rel in (0, WINDOW, (nq - 1) * tq - (S - nk)):
        dist = np.abs(rel + c - r).astype(np.float32)
        tabs.append(np.where(dist <= WINDOW, dist, np.float32(1e32)))
    return jnp.asarray(np.stack(tabs))


def _win_dist_spec(nk, tq, nq):
    return pl.BlockSpec((1, nk, 2 * tq), lambda b, i: (jnp.where(i == 0, 0, jnp.where(i == nq - 1, 2, 1)), 0, 0))


def _win_fwd(q, k, vT, dist, slope, sink, *, kdiv, tq, nbs, name):
    S = k.shape[0]; nb = q.shape[1] // LANES; nq = S // tq; nk = min(tq + 2 * WINDOW, S)
    assert nb % nbs == 0 and kdiv % nbs == 0

    def body(q_ref, k_ref, vT_ref, dist_ref, slope_ref, sink_ref, o_ref, lse_ref):
        i = pl.program_id(1)
        rlo = _row_lo()
        k0 = _win_start(i, tq, nk, S)
        kk, vv, dd = k_ref[pl.ds(k0, nk), :], vT_ref[:, pl.ds(k0, nk)], dist_ref[0]
        for u in range(nbs):
            qsT = _stack_cols(q_ref[:, LANES * u:LANES * (u + 1)].astype(F32).T, rlo).astype(BF16)
            s = _dot(kk, qsT) - slope_ref[u] * dd
            sk = sink_ref[u]
            m = jnp.maximum(jnp.max(s, axis=0, keepdims=True), sk)
            p = jnp.exp(s - m)
            l = jnp.sum(p, axis=0, keepdims=True) + jnp.exp(sk - m)
            o_ref[:, LANES * u:LANES * (u + 1)] = _pick_halves_T(_dot(vv, p.astype(BF16)) / l, rlo, tq)
            lse_ref[u, 0] = m + jnp.log(l)

    row_spec = pl.BlockSpec((nbs, 1, 2 * tq), lambda b, i: (b, 0, 0))
    return pl.pallas_call(
        body, name=name, grid=(nb // nbs, nq),
        in_specs=[pl.BlockSpec((tq, nbs * LANES), lambda b, i: (i, b)), pl.BlockSpec((S, LANES), lambda b, i: (0, b * nbs // kdiv)),
                  pl.BlockSpec((LANES, S), lambda b, i: (b * nbs // kdiv, 0)), _win_dist_spec(nk, tq, nq), row_spec, row_spec],
        out_specs=[pl.BlockSpec((tq, nbs * LANES), lambda b, i: (i, b)), pl.BlockSpec((nbs, 1, 1, 2 * tq), lambda b, i: (b, i, 0, 0))],
        out_shape=[_sds((S, nb * LANES), F32), _sds((nb, nq, 1, 2 * tq), F32)],
        compiler_params=_params(("arbitrary", "arbitrary")))(q, k, vT, dist, slope, sink)


def _win_bwd(q, k, kT, v, o, do, lse, dist, slope, sink, *, kdiv, tq, nbs, name):
    S = k.shape[0]; nb = q.shape[1] // LANES; nkb = k.shape[1] // LANES; nq = S // tq; nk = min(tq + 2 * WINDOW, S)
    steps_per_kv = kdiv // nbs

    def body(q_ref, k_ref, kT_ref, v_ref, o_ref, do_ref, lse_ref, dist_ref, slope_ref, sink_ref, dq_ref, dk_ref, dv_ref, dsink_ref, ds_acc):
        b, i = pl.program_id(0), pl.program_id(1)
        rlo = _row_lo()
        lo = lax.broadcasted_iota(jnp.int32, (1, LANES), 1) < HEAD_DIM

        @pl.when((b % steps_per_kv == 0) & (i == 0))
        def _():
            dk_ref[...] = jnp.zeros((S, LANES), F32)
            dv_ref[...] = jnp.zeros((S, LANES), F32)

        @pl.when(i == 0)
        def _():
            ds_acc[...] = jnp.zeros((nbs, 2 * tq), F32)

        k0 = _win_start(i, tq, nk, S)
        rows = pl.ds(k0, nk)
        kk, vv, kkT, dd = k_ref[rows, :], v_ref[rows, :], kT_ref[:, rows], dist_ref[0]
        dv_sum, dk_sum = None, None
        for u in range(nbs):
            cols = slice(LANES * u, LANES * (u + 1))
            qv = q_ref[:, cols]
            qs = _stack_rows(qv, lo)
            qsT = _stack_cols(qv.astype(F32).T, rlo).astype(BF16)
            dov = do_ref[:, cols]
            dos = _stack_rows(dov.astype(BF16), lo)
            dosT = _stack_cols(dov.T, rlo).astype(BF16)
            prodT = (dov * o_ref[:, cols]).T
            delta = jnp.concatenate([jnp.sum(jnp.where(rlo, prodT, 0.0), axis=0, keepdims=True),
                                     jnp.sum(jnp.where(rlo, 0.0, prodT), axis=0, keepdims=True)], axis=1)
            lse_v = lse_ref[u, 0]
            ds_acc[u:u + 1, :] += -jnp.exp(sink_ref[u] - lse_v) * delta
            p = jnp.exp(_dot(kk, qsT) - slope_ref[u] * dd - lse_v)
            ds = (p * (_dot(vv, dosT) - delta)).astype(BF16)
            dv_u, dk_u = _dot(p.astype(BF16), dos), _dot(ds, qs)
            dv_sum = dv_u if dv_sum is None else dv_sum + dv_u
            dk_sum = dk_u if dk_sum is None else dk_sum + dk_u
            dq_ref[:, cols] = _pick_halves_T(_dot(kkT, ds), rlo, tq)
        dv_ref[rows, :] += dv_sum
        dk_ref[rows, :] += dk_sum

        @pl.when(i == nq - 1)
        def _():
            acc = ds_acc[...]
            for u in range(nbs):
                dsink_ref[u] = jnp.concatenate(
                    [jnp.broadcast_to(jnp.sum(acc[u:u + 1, 0:tq], axis=1, keepdims=True), (1, LANES)),
                     jnp.broadcast_to(jnp.sum(acc[u:u + 1, tq:2 * tq], axis=1, keepdims=True), (1, LANES)),
                     jnp.zeros((6, LANES), F32)], axis=0)

    qmap = lambda b, i: (i, b)
    kv_spec = pl.BlockSpec((S, LANES), lambda b, i: (0, b * nbs // kdiv))
    row_spec = pl.BlockSpec((nbs, 1, 2 * tq), lambda b, i: (b, 0, 0))
    wide = pl.BlockSpec((tq, nbs * LANES), qmap)
    return pl.pallas_call(
        body, name=name, grid=(nb // nbs, nq),
        in_specs=[wide, kv_spec, pl.BlockSpec((LANES, S), lambda b, i: (b * nbs // kdiv, 0)), kv_spec, wide, wide,
                  pl.BlockSpec((nbs, 1, 1, 2 * tq), lambda b, i: (b, i, 0, 0)), _win_dist_spec(nk, tq, nq), row_spec, row_spec],
        out_specs=[wide, kv_spec, kv_spec, pl.BlockSpec((nbs, 8, LANES), lambda b, i: (b, 0, 0))],
        out_shape=[_sds((S, nb * LANES), F32), _sds((S, nkb * LANES), F32), _sds((S, nkb * LANES), F32), _sds((nb, 8, LANES), F32)],
        scratch_shapes=[pltpu.VMEM((nbs, 2 * tq), F32)],
        compiler_params=_params(("arbitrary", "arbitrary")))(q, k, kT, v, o, do, lse, dist, slope, sink)


def _sum_rows(v):
    return jnp.sum(v, axis=0, keepdims=True)


def _norm_mod_bwd(dh, xv, mod_ref, nw_ref, stats_ref):
    r = _rms(xv)
    xn = xv * r
    nw = nw_ref[...]
    stats_ref[0:1, :] += _sum_rows(dh)
    stats_ref[1:2, :] += _sum_rows(dh * (xn * nw))
    dn = dh * (1.0 + mod_ref[1:2, :])
    stats_ref[2:3, :] += _sum_rows(dn * xn)
    return _rms_bwd(xv, r, dn * nw)


def _even_post_fwd(oa, olat, proj, x, gate, wuv, woe):
    S = x.shape[0]
    ts = min(ROW_TILE, S)

    def body(oa_ref, ol_ref, proj_ref, x_ref, gate_ref, wuv_ref, woe_ref, y_ref, x1_ref):
        sa, _ = _silu_and_grad(proj_ref[:, 768:1280])
        sb, _ = _silu_and_grad(proj_ref[:, 1792:2304])
        olc = jnp.concatenate([ol_ref[hh] for hh in range(B_HEADS)], axis=1).astype(BF16)
        ob = _dot(olc, wuv_ref[...])
        mix = jnp.concatenate([oa_ref[...] * sa, ob * sb], axis=1).astype(BF16)
        y = _dot(mix, woe_ref[...])
        y_ref[...] = y
        x1_ref[...] = x_ref[...] + gate_ref[...] * y

    return pl.pallas_call(
        body, name="even_post_fwd", grid=(S // ts,),
        in_specs=[_row_spec(ts, 512), pl.BlockSpec((B_HEADS, ts, LANES), lambda i: (0, i, 0)), _row_spec(ts, EVEN_P),
                  _row_spec(ts, D_MODEL), _full_spec((1, D_MODEL)), _full_spec((1024, 512)), _full_spec((1024, D_MODEL))],
        out_specs=[_row_spec(ts, D_MODEL), _row_spec(ts, D_MODEL)],
        out_shape=[_sds((S, D_MODEL), F32), _sds((S, D_MODEL), F32)],
        compiler_params=_params(("arbitrary",)),
    )(oa, olat, proj, x, gate, wuv, woe)


def _odd_pre_fwd(x, mod, nw, wio):
    S = x.shape[0]
    ts = min(ROW_TILE, S)

    def body(x_ref, mod_ref, nw_ref, wio_ref, h_ref, proj_ref, q_ref, k_ref, v_ref):
        xv = x_ref[...]
        h = (xv * _rms(xv) * nw_ref[...]) * (1.0 + mod_ref[1:2, :]) + mod_ref[0:1, :]
        hb = h.astype(BF16)
        h_ref[...] = hb
        proj = _dot(hb, wio_ref[...])
        proj_ref[...] = proj
        q_ref[...] = (proj[:, 0:1024] * 0.125).astype(BF16)
        k_ref[...] = proj[:, 1024:1280].astype(BF16)
        v_ref[...] = proj[:, 1280:1536].astype(BF16)

    return pl.pallas_call(
        body, name="odd_pre_fwd", grid=(S // ts,),
        in_specs=[_row_spec(ts, D_MODEL), _full_spec((3, D_MODEL)), _full_spec((1, D_MODEL)), _full_spec((D_MODEL, ODD_IN))],
        out_specs=[_row_spec(ts, D_MODEL), _row_spec(ts, ODD_IN), _row_spec(ts, 1024), _row_spec(ts, 256), _row_spec(ts, 256)],
        out_shape=[_sds((S, D_MODEL), BF16), _sds((S, ODD_IN), F32), _sds((S, 1024), BF16), _sds((S, 256), BF16),
                   _sds((S, 256), BF16)],
        compiler_params=_params(("arbitrary",)),
    )(x, mod, nw, wio)


def _odd_post(oc, proj, x1, gate, woo, fw, tgt):
    S = x1.shape[0]
    ts = min(ROW_TILE, S)

    def body(oc_ref, proj_ref, x_ref, gate_ref, woo_ref, fw_ref, tgt_ref, doc_ref, dgc_ref, dx2_ref, dwoo_ref, stats_ref):
        @pl.when(pl.program_id(0) == 0)
        def _():
            dwoo_ref[...] = jnp.zeros((D_MODEL, D_MODEL), F32)
            stats_ref[...] = jnp.zeros((8, D_MODEL), F32)

        ocv = oc_ref[...]
        sg, dsg = _silu_and_grad(proj_ref[:, 1536:2560])
        mix = (ocv * sg).astype(BF16)
        woo_v = woo_ref[...]
        y = _dot(mix, woo_v)
        gate_v = gate_ref[...]
        x2 = x_ref[...] + gate_v * y
        r = _rms(x2)
        fw_v = fw_ref[...]
        xn = x2 * r
        err = xn * fw_v - tgt_ref[...]
        dout = err * (1.0 / D_MODEL)
        dx2 = _rms_bwd(x2, r, dout * fw_v)
        dx2_ref[...] = dx2
        stats_ref[0:1, :] += _sum_rows(dout * xn)
        stats_ref[1:2, :] += _sum_rows(dx2 * y)
        loss_t = 0.5 * jnp.sum(_sum_rows(err * dout), axis=-1, keepdims=True)
        stats_ref[2:3, :] += jnp.broadcast_to(loss_t, (1, D_MODEL))
        dy = (gate_v * dx2).astype(BF16)
        dmix = _dot_nt(dy, woo_v)
        dwoo_ref[...] += _dot_tn(mix, dy)
        doc_ref[...] = dmix * sg
        dgc_ref[...] = dmix * ocv * dsg

    return pl.pallas_call(
        body, name="odd_post", grid=(S // ts,),
        in_specs=[_row_spec(ts, D_MODEL), _row_spec(ts, ODD_IN), _row_spec(ts, D_MODEL), _full_spec((1, D_MODEL)),
                  _full_spec((D_MODEL, D_MODEL)), _full_spec((1, D_MODEL)), _row_spec(ts, D_MODEL)],
        out_specs=[_row_spec(ts, D_MODEL), _row_spec(ts, D_MODEL), _row_spec(ts, D_MODEL),
                   _full_spec((D_MODEL, D_MODEL), single=False), _full_spec((8, D_MODEL), single=False)],
        out_shape=[_sds((S, D_MODEL), F32), _sds((S, D_MODEL), F32), _sds((S, D_MODEL), F32), _sds((D_MODEL, D_MODEL), F32),
                   _sds((8, D_MODEL), F32)],
        compiler_params=_params(("arbitrary",)),
    )(oc, proj, x1, gate, woo, fw, tgt)


def _odd_pre_bwd(dq, dk, dv, dgc, h, x, dx_res, mod, nw, wio):
    S = x.shape[0]
    ts = min(ROW_TILE, S)

    def body(dq_ref, dk_ref, dv_ref, dgc_ref, h_ref, x_ref, dxr_ref, mod_ref, nw_ref, wio_ref, dx_ref, dw_ref, stats_ref):
        @pl.when(pl.program_id(0) == 0)
        def _():
            dw_ref[...] = jnp.zeros((D_MODEL, ODD_IN), F32)
            stats_ref[...] = jnp.zeros((8, D_MODEL), F32)

        dproj = jnp.concatenate([dq_ref[...] * 0.125, dk_ref[...], dv_ref[...], dgc_ref[...]], axis=1).astype(BF16)
        dh = _dot_nt(dproj, wio_ref[...])
        dw_ref[...] += _dot_tn(h_ref[...], dproj)
        dx_ref[...] = dxr_ref[...] + _norm_mod_bwd(dh, x_ref[...], mod_ref, nw_ref, stats_ref)

    return pl.pallas_call(
        body, name="odd_pre_bwd", grid=(S // ts,),
        in_specs=[_row_spec(ts, 1024), _row_spec(ts, 256), _row_spec(ts, 256), _row_spec(ts, 1024), _row_spec(ts, D_MODEL),
                  _row_spec(ts, D_MODEL), _row_spec(ts, D_MODEL), _full_spec((3, D_MODEL)), _full_spec((1, D_MODEL)),
                  _full_spec((D_MODEL, ODD_IN))],
        out_specs=[_row_spec(ts, D_MODEL), _full_spec((D_MODEL, ODD_IN), single=False), _full_spec((8, D_MODEL), single=False)],
        out_shape=[_sds((S, D_MODEL), F32), _sds((D_MODEL, ODD_IN), F32), _sds((8, D_MODEL), F32)],
        compiler_params=_params(("arbitrary",)),
    )(dq, dk, dv, dgc, h, x, dx_res, mod, nw, wio)


def _even_post_bwd(dx1, y, oa, olat, proj, gate, wuv, woe):
    S = dx1.shape[0]
    ts = min(ROW_TILE, S)

    def body(dx_ref, y_ref, oa_ref, ol_ref, proj_ref, gate_ref, wuv_ref, woe_ref,
             doa_ref, dga_ref, dgb_ref, dol_ref, dwoe_ref, dwuv_ref, stats_ref):
        @pl.when(pl.program_id(0) == 0)
        def _():
            dwoe_ref[...] = jnp.zeros((D_MODEL, D_MODEL), F32)
            dwuv_ref[...] = jnp.zeros((1024, 512), F32)
            stats_ref[...] = jnp.zeros((8, D_MODEL), F32)

        dxv = dx_ref[...]
        stats_ref[0:1, :] += _sum_rows(dxv * y_ref[...])
        dy = (gate_ref[...] * dxv).astype(BF16)
        sa, dsa = _silu_and_grad(proj_ref[:, 768:1280])
        sb, dsb = _silu_and_grad(proj_ref[:, 1792:2304])
        olc = jnp.concatenate([ol_ref[hh] for hh in range(B_HEADS)], axis=1).astype(BF16)
        wuv_v = wuv_ref[...]
        ob = _dot(olc, wuv_v)
        oav = oa_ref[...]
        mix = jnp.concatenate([oav * sa, ob * sb], axis=1).astype(BF16)
        dmix = _dot_nt(dy, woe_ref[...])
        dwoe_ref[...] += _dot_tn(mix, dy)
        dma, dmb = dmix[:, 0:512], dmix[:, 512:1024]
        doa_ref[...] = dma * sa
        dga_ref[...] = dma * oav * dsa
        dgb_ref[...] = dmb * ob * dsb
        dob = (dmb * sb).astype(BF16)
        dol = _dot_nt(dob, wuv_v)
        dwuv_ref[...] += _dot_tn(olc, dob)
        for hh in range(B_HEADS):
            dol_ref[hh] = dol[:, LANES * hh:LANES * (hh + 1)]

    head_spec = pl.BlockSpec((B_HEADS, ts, LANES), lambda i: (0, i, 0))
    return pl.pallas_call(
        body, name="even_post_bwd", grid=(S // ts,),
        in_specs=[_row_spec(ts, D_MODEL), _row_spec(ts, D_MODEL), _row_spec(ts, 512), head_spec, _row_spec(ts, EVEN_P),
                  _full_spec((1, D_MODEL)), _full_spec((1024, 512)), _full_spec((1024, D_MODEL))],
        out_specs=[_row_spec(ts, 512), _row_spec(ts, 512), _row_spec(ts, 512), head_spec,
                   _full_spec((D_MODEL, D_MODEL), single=False), _full_spec((1024, 512), single=False),
                   _full_spec((8, D_MODEL), single=False)],
        out_shape=[_sds((S, 512), F32), _sds((S, 512), F32), _sds((S, 512), F32), _sds((B_HEADS, S, LANES), F32),
                   _sds((D_MODEL, D_MODEL), F32), _sds((1024, 512), F32), _sds((8, D_MODEL), F32)],
        compiler_params=_params(("arbitrary",)),
    )(dx1, y, oa, olat, proj, gate, wuv, woe)


def _even_pre_bwd(x, h, proj, dqa, dka, dva, dga, dgb, dqcat, dkcat, dx_res, mod, nw, wie, qn, kn, seg, ca, sa, ct, st,
                  qln, kvln, wuq, wuk):
    S = x.shape[0]
    ts = min(ROW_TILE, S)
    nsteps = S // ts

    def body(x_ref, h_ref, proj_ref, dqa_ref, dka_ref, dva_ref, dga_ref, dgb_ref, dqc_ref, dkc_ref, dxr_ref, mod_ref, nw_ref,
             wie_ref, qn_ref, kn_ref, seg_ref, ca_ref, sa_ref, ct_ref, st_ref, qln_ref, kvln_ref, wuq_ref, wuk_ref,
             dx_ref, dwie_out, dwuq_out, dwuk_out, stats_ref, nstats_ref, dwie_ref, dwuq_ref, dwuk_ref):
        @pl.when(pl.program_id(0) == 0)
        def _():
            dwie_ref[...] = jnp.zeros((D_MODEL, EVEN_P), F32)
            dwuq_ref[...] = jnp.zeros((B_Q_LORA, 1536), F32)
            dwuk_ref[...] = jnp.zeros((512, 1024), F32)
            stats_ref[...] = jnp.zeros((8, D_MODEL), F32)
            nstats_ref[...] = jnp.zeros((8, 256), F32)

        lane = _lane_iota()
        ca_v, sa_v, ct_v, st_v = ca_ref[...], sa_ref[...], ct_ref[...], st_ref[...]
        seg_v = seg_ref[...]

        def head_norm_bwd(xc, dy, w):
            r = lax.rsqrt(_seg_mean(xc * xc, seg_v) + EPS)
            g = dy * w
            dxc = r * g - xc * (r * r * r) * _seg_mean(xc * g, seg_v)
            return dxc, _sum_rows(dy * (xc * r))

        pieces = []
        dqn = jnp.zeros((1, LANES), F32)
        for cb in range(4):
            sl = slice(LANES * cb, LANES * (cb + 1))
            dy = _rot_bwd(dqa_ref[:, sl] * 0.125, ca_v, sa_v, lane)
            dxc, dw = head_norm_bwd(proj_ref[:, sl], dy, qn_ref[...])
            pieces.append(dxc)
            dqn = dqn + dw
        dxc, dkn = head_norm_bwd(proj_ref[:, 512:640], _rot_bwd(dka_ref[...], ca_v, sa_v, lane), kn_ref[...])
        pieces += [dxc, dva_ref[...], dga_ref[...]]
        nstats_ref[0:1, 0:LANES] += dqn + pltpu.roll(dqn, HEAD_DIM, 1)
        nstats_ref[1:2, 0:LANES] += dkn + pltpu.roll(dkn, HEAD_DIM, 1)

        cq = proj_ref[:, 1280:1536]
        rq = _rms(cq)
        cqn_f = cq * rq
        qln_v = qln_ref[...]
        cqn = (cqn_f * qln_v).astype(BF16)
        wuq_v, wuk_v = wuq_ref[...], wuk_ref[...]
        qnope = _dot(cqn, wuq_v[:, 0:512]).astype(BF16)
        dqlat = jnp.concatenate([dqc_ref[hh, :, 0:LANES] for hh in range(B_HEADS)], axis=1).astype(BF16)
        dqnope = _dot_nt(dqlat, wuk_v)
        dwuk_ref[...] += _dot_tn(qnope, dqlat)
        dqr = [_rot_bwd(dqc_ref[hh, :, LANES:2 * LANES], ct_v, st_v, lane) for hh in range(B_HEADS)]
        dqb = jnp.concatenate([dqnope] + dqr, axis=1).astype(BF16)
        dcqn = _dot_nt(dqb, wuq_v)
        dwuq_ref[...] += _dot_tn(cqn, dqb)
        nstats_ref[2:3, :] += _sum_rows(dcqn * cqn_f)
        dcq = _rms_bwd(cq, rq, dcqn * qln_v)
        ckv = proj_ref[:, 1536:1664]
        rk = _rms(ckv)
        dckvn = dkc_ref[:, 0:LANES]
        nstats_ref[3:4, 0:LANES] += _sum_rows(dckvn * (ckv * rk))
        dckv = _rms_bwd(ckv, rk, dckvn * kvln_ref[...])
        dkr = _rot_bwd(dkc_ref[:, LANES:2 * LANES], ct_v, st_v, lane)
        pieces += [dcq, dckv, dkr, dgb_ref[...]]
        dproj = jnp.concatenate(pieces, axis=1).astype(BF16)
        dh = _dot_nt(dproj, wie_ref[...])
        dwie_ref[...] += _dot_tn(h_ref[...], dproj)
        dx_ref[...] = dxr_ref[...] + _norm_mod_bwd(dh, x_ref[...], mod_ref, nw_ref, stats_ref)

        @pl.when(pl.program_id(0) == nsteps - 1)
        def _():
            pltpu.sync_copy(dwie_ref, dwie_out)
            pltpu.sync_copy(dwuq_ref, dwuq_out)
            pltpu.sync_copy(dwuk_ref, dwuk_out)

    return pl.pallas_call(
        body, name="even_pre_bwd", grid=(nsteps,),
        in_specs=[_row_spec(ts, D_MODEL), _row_spec(ts, D_MODEL), _row_spec(ts, EVEN_P), _row_spec(ts, 512), _row_spec(ts, LANES),
                  _row_spec(ts, LANES), _row_spec(ts, 512), _row_spec(ts, 512),
                  pl.BlockSpec((B_HEADS, ts, 2 * LANES), lambda i: (0, i, 0)), _row_spec(ts, 2 * LANES), _row_spec(ts, D_MODEL),
                  _full_spec((3, D_MODEL)), _full_spec((1, D_MODEL)), _full_spec((D_MODEL, EVEN_P)),
                  _full_spec((1, LANES)), _full_spec((1, LANES)), _full_spec((LANES, LANES)),
                  _row_spec(ts, LANES), _row_spec(ts, LANES), _row_spec(ts, LANES), _row_spec(ts, LANES),
                  _full_spec((1, B_Q_LORA)), _full_spec((1, B_KV_LORA)), _full_spec((B_Q_LORA, 1536)), _full_spec((512, 1024))],
        out_specs=[_row_spec(ts, D_MODEL), _ANY, _ANY, _ANY, _full_spec((8, D_MODEL), single=False), _full_spec((8, 256), single=False)],
        out_shape=[_sds((S, D_MODEL), F32), _sds((D_MODEL, EVEN_P), F32), _sds((B_Q_LORA, 1536), F32), _sds((512, 1024), F32),
                   _sds((8, D_MODEL), F32), _sds((8, 256), F32)],
        scratch_shapes=[pltpu.VMEM((D_MODEL, EVEN_P), F32), pltpu.VMEM((B_Q_LORA, 1536), F32), pltpu.VMEM((512, 1024), F32)],
        compiler_params=_params(("arbitrary",)),
    )(x, h, proj, dqa, dka, dva, dga, dgb, dqcat, dkcat, dx_res, mod, nw, wie, qn, kn, seg, ca, sa, ct, st, qln, kvln, wuq, wuk)


def _ada_fwd(c_all, w, b):
    n = w.shape[2]

    def body(c_ref, w_ref, b_ref, o_ref):
        cv = c_ref[...]
        o_ref[0] = _dot_f32(cv * _sigmoid(cv), w_ref[0]) + b_ref[0]

    return pl.pallas_call(
        body, name="ada_fwd", grid=(2,),
        in_specs=[pl.BlockSpec((N_DEV, D_MODEL), lambda l: (0, 0)), pl.BlockSpec((1, D_MODEL, n), lambda l: (l, 0, 0)),
                  pl.BlockSpec((1, 1, n), lambda l: (l, 0, 0))],
        out_specs=pl.BlockSpec((1, N_DEV, n), lambda l: (l, 0, 0)),
        out_shape=_sds((2, N_DEV, n), F32),
        compiler_params=_params(("arbitrary",)),
    )(c_all, w, b)


def _ada_bwd(c_all_t, dmod):
    n = dmod.shape[2]

    def body(c_ref, d_ref, o_ref):
        cv = c_ref[...]
        act = cv * _sigmoid(cv)
        dv = d_ref[0]
        acc = act[:, 0:1] * dv[0:1, :]
        for bb in range(1, N_DEV):
            acc = acc + act[:, bb:bb + 1] * dv[bb:bb + 1, :]
        o_ref[0] = acc

    return pl.pallas_call(
        body, name="ada_bwd", grid=(2,),
        in_specs=[pl.BlockSpec((D_MODEL, N_DEV), lambda l: (0, 0)), pl.BlockSpec((1, N_DEV, n), lambda l: (l, 0, 0))],
        out_specs=pl.BlockSpec((1, D_MODEL, n), lambda l: (l, 0, 0)),
        out_shape=_sds((2, D_MODEL, n), F32),
        compiler_params=_params(("arbitrary",)),
    )(c_all_t, dmod)


ADAM_ROW_TILE = 256


def _adam(parts, w, m, v, name):
    P, R, C = parts.shape
    tr = R if R <= ADAM_ROW_TILE else ADAM_ROW_TILE
    assert R % tr == 0

    def body(p_ref, w_ref, m_ref, v_ref, g_ref, d_ref, nm_ref, nv_ref):
        g = p_ref[0].astype(F32)
        for k in range(1, P):
            g = g + p_ref[k].astype(F32)
        g_ref[...] = g
        m_new = ADAM_B1 * m_ref[...] + (1.0 - ADAM_B1) * g
        v_new = ADAM_B2 * v_ref[...] + (1.0 - ADAM_B2) * jnp.square(g)
        m_hat = m_new / (1.0 - ADAM_B1 ** ADAM_STEP)
        v_hat = v_new / (1.0 - ADAM_B2 ** ADAM_STEP)
        d_ref[...] = -ADAM_LR * (m_hat / (jnp.sqrt(v_hat) + ADAM_EPS) + ADAM_WD * w_ref[...])
        nm_ref[...] = m_new
        nv_ref[...] = v_new

    spec = pl.BlockSpec((tr, C), lambda i: (i, 0))
    return pl.pallas_call(
        body, name=name, grid=(R // tr,),
        in_specs=[pl.BlockSpec((P, tr, C), lambda i: (0, i, 0)), spec, spec, spec],
        out_specs=[spec, spec, spec, spec], out_shape=[_sds((R, C), F32)] * 4,
        compiler_params=_params(("arbitrary",)),
    )(parts, w, m, v)


_ANY = pl.BlockSpec(memory_space=pl.ANY)
CHIP_FLIPS = ((1, 0), (0, 1), (1, 1))
DEV_FLIPS = tuple((dx, dy, dc) for dx in (0, 1) for dy in (0, 1) for dc in (0, 1) if dx + dy + dc)


def _flip(a, d):
    return a if d == 0 else 1 - a


def _my_place():
    return lax.axis_index("x"), lax.axis_index("y"), lax.axis_index("c")


def _gather_dev8(arrs, name):
    n = len(arrs)

    def body(*refs):
        ins, outs = refs[:n], refs[n:2 * n]
        send_sems, recv_sems, loc_sems = refs[2 * n:]
        x, y, c = _my_place()
        me = 4 * x + 2 * y + c
        copies = []
        for a in range(n):
            loc = pltpu.make_async_copy(ins[a], outs[a].at[me], loc_sems.at[a])
            loc.start()
            copies.append(loc)
            for k, (dx, dy, dc) in enumerate(DEV_FLIPS):
                cp = pltpu.make_async_remote_copy(
                    src_ref=ins[a], dst_ref=outs[a].at[me], send_sem=send_sems.at[a, k], recv_sem=recv_sems.at[a, k],
                    device_id=(_flip(x, dx), _flip(y, dy), _flip(c, dc)), device_id_type=MESH_ID)
                cp.start()
                copies.append(cp)
        for cp in copies:
            cp.wait()

    return pl.pallas_call(
        body, name=name, in_specs=[_ANY] * n, out_specs=[_ANY] * n,
        out_shape=[_sds((N_DEV,) + a.shape, a.dtype) for a in arrs],
        scratch_shapes=[pltpu.SemaphoreType.DMA((n, 7)), pltpu.SemaphoreType.DMA((n, 7)), pltpu.SemaphoreType.DMA((n,))],
    )(*arrs)


class _Exchange:
    def __init__(self, arrs, out_shapes, n_sems, phases):
        self.arrs, self.out_shapes, self.n_sems, self._phases = list(arrs), list(out_shapes), n_sems, phases

    @property
    def n(self):
        return len(self.arrs)

    def sem_shapes(self):
        return [pltpu.SemaphoreType.DMA((self.n, self.n_sems)), pltpu.SemaphoreType.DMA((self.n, self.n_sems)),
                pltpu.SemaphoreType.DMA((self.n,))]

    def phases(self, ins, outs, sems):
        return self._phases(ins, outs, *sems)

    def run(self, name):
        n = self.n

        def body(*refs):
            start, mid, end = self.phases(refs[:n], refs[n:2 * n], refs[2 * n:])
            start()
            mid()
            end()

        return pl.pallas_call(body, name=name, in_specs=[_ANY] * n, out_specs=[_ANY] * n, out_shape=self.out_shapes,
                              scratch_shapes=self.sem_shapes())(*self.arrs)


def _gather_halves_phases(ins, outs, send_sems, recv_sems, loc_sems):
    n = len(ins)
    x, y, c = _my_place()
    chip = 2 * x + y
    sibling = (x, y, 1 - c)
    peers = [(_flip(x, dx), _flip(y, dy)) for dx, dy in CHIP_FLIPS]

    def remote(src, p, half, a, k, to):
        return pltpu.make_async_remote_copy(src_ref=src, dst_ref=outs[a].at[p, half], send_sem=send_sems.at[a, k],
                                            recv_sem=recv_sems.at[a, k], device_id=to, device_id_type=MESH_ID)

    def local(a):
        return pltpu.make_async_copy(ins[a], outs[a].at[chip], loc_sems.at[a])

    def first(a, k):
        return remote(ins[a].at[c], chip, c, a, k, (*peers[k], c))

    def passed(a, k):
        p = 2 * peers[k][0] + peers[k][1]
        return remote(outs[a].at[p, c], p, c, a, 3 + k, sibling)

    def start():
        for a in range(n):
            local(a).start()
            for k in range(3):
                first(a, k).start()

    def mid():
        for a in range(n):
            for k in range(3):
                p = 2 * peers[k][0] + peers[k][1]
                remote(outs[a].at[p, c], p, c, a, k, sibling).wait_recv()
                passed(a, k).start()

    def end():
        for a in range(n):
            for k in range(3):
                p = 2 * peers[k][0] + peers[k][1]
                remote(outs[a].at[p, 1 - c], p, 1 - c, a, 3 + k, sibling).wait_recv()
        for a in range(n):
            for k in range(3):
                first(a, k).wait_send()
                passed(a, k).wait_send()
            local(a).wait()

    return start, mid, end


def _gather_chip4_halves(arrs):
    return _Exchange(arrs, [_sds((N_CHIPS,) + a.shape, a.dtype) for a in arrs], 6, _gather_halves_phases)


def _reduce_phases(ins, outs, send_sems, recv_sems, loc_sems):
    n = len(ins)
    x, y, c = _my_place()
    chip = 2 * x + y
    sibling = (x, y, 1 - c)
    peers = [(_flip(x, dx), _flip(y, dy)) for dx, dy in CHIP_FLIPS]

    def remote(src, slot, a, k, to):
        return pltpu.make_async_remote_copy(src_ref=src, dst_ref=outs[a].at[slot], send_sem=send_sems.at[a, k],
                                            recv_sem=recv_sems.at[a, k], device_id=to, device_id_type=MESH_ID)

    def local(a):
        return pltpu.make_async_copy(ins[a].at[chip], outs[a].at[2 * chip + c], loc_sems.at[a])

    def own(a):
        return remote(ins[a].at[chip], 2 * chip + c, a, 0, sibling)

    def first(a, k):
        return remote(ins[a].at[2 * peers[k][0] + peers[k][1]], 2 * chip + c, a, 1 + k, (*peers[k], c))

    def passed(a, k):
        slot = 2 * (2 * peers[k][0] + peers[k][1]) + c
        return remote(outs[a].at[slot], slot, a, 4 + k, sibling)

    def start():
        for a in range(n):
            local(a).start()
            own(a).start()
            for k in range(3):
                first(a, k).start()

    def mid():
        for a in range(n):
            for k in range(3):
                slot = 2 * (2 * peers[k][0] + peers[k][1]) + c
                remote(outs[a].at[slot], slot, a, 1 + k, sibling).wait_recv()
                passed(a, k).start()

    def end():
        for a in range(n):
            remote(outs[a].at[2 * chip + 1 - c], 2 * chip + 1 - c, a, 0, sibling).wait_recv()
            for k in range(3):
                slot = 2 * (2 * peers[k][0] + peers[k][1]) + 1 - c
                remote(outs[a].at[slot], slot, a, 4 + k, sibling).wait_recv()
        for a in range(n):
            own(a).wait_send()
            for k in range(3):
                first(a, k).wait_send()
                passed(a, k).wait_send()
            local(a).wait()

    return start, mid, end


def _reduce_exchange(arrs):
    return _Exchange(arrs, [_sds((N_DEV,) + a.shape[1:], a.dtype) for a in arrs], 7, _reduce_phases)


def _pair_order(nheads, nkv):
    group = nheads // nkv
    order = []
    for m in range(nkv // 2):
        for i in range(group):
            order += [2 * m * group + i, (2 * m + 1) * group + i]
    return order


A_ORDER = _pair_order(A_HEADS, A_KV_HEADS)
C_ORDER = _pair_order(C_HEADS, C_KV_HEADS)
A_INV = [int(k) for k in np.argsort(A_ORDER)]
C_INV = [int(k) for k in np.argsort(C_ORDER)]


def _perm_heads(w, order, axis):
    return jnp.concatenate([lax.slice_in_dim(w, HEAD_DIM * h, HEAD_DIM * (h + 1), axis=axis) for h in order], axis=axis)


def _even_in_layout(w):
    return jnp.concatenate([_perm_heads(w[:, 0:512], A_ORDER, 1), w[:, 512:768], _perm_heads(w[:, 768:1280], A_ORDER, 1),
                            w[:, 1280:1696], jnp.zeros((w.shape[0], 96), w.dtype), w[:, 1696:2208]], axis=1)


def _even_in_unlayout(g):
    return jnp.concatenate([_perm_heads(g[:, 0:512], A_INV, 1), g[:, 512:768], _perm_heads(g[:, 768:1280], A_INV, 1),
                            g[:, 1280:1696], g[:, 1792:2304]], axis=1)


def _even_out_layout(w):
    return jnp.concatenate([_perm_heads(w[0:512], A_ORDER, 0), w[512:1024]], axis=0)


def _even_out_unlayout(g):
    return jnp.concatenate([_perm_heads(g[0:512], A_INV, 0), g[512:1024]], axis=0)


def _odd_in_layout(w):
    return jnp.concatenate([_perm_heads(w[:, 0:1024], C_ORDER, 1), w[:, 1024:1536], _perm_heads(w[:, 1536:2560], C_ORDER, 1)],
                           axis=1)


def _odd_in_unlayout(g):
    return jnp.concatenate([_perm_heads(g[:, 0:1024], C_INV, 1), g[:, 1024:1536], _perm_heads(g[:, 1536:2560], C_INV, 1)],
                           axis=1)


def _uq_layout(w):
    per = B_NOPE + B_ROPE
    pad = jnp.zeros((w.shape[0], LANES - B_ROPE), w.dtype)
    nope = [w[:, per * h:per * h + B_NOPE] for h in range(B_HEADS)]
    rope = [jnp.concatenate([w[:, per * h + B_NOPE:per * (h + 1)], pad], axis=1) for h in range(B_HEADS)]
    return jnp.concatenate(nope + rope, axis=1)


def _uq_unlayout(g):
    parts = []
    for h in range(B_HEADS):
        parts += [g[:, B_NOPE * h:B_NOPE * (h + 1)], g[:, 512 + LANES * h:512 + LANES * h + B_ROPE]]
    return jnp.concatenate(parts, axis=1)


def _block_diag(blocks):
    rows = []
    for h, blk in enumerate(blocks):
        r, cdim = blk.shape
        n = len(blocks)
        rows.append(jnp.concatenate([jnp.zeros((r, cdim * h), blk.dtype), blk, jnp.zeros((r, cdim * (n - 1 - h)), blk.dtype)],
                                    axis=1))
    return jnp.concatenate(rows, axis=0)


def _uk_layout(w):
    return _block_diag([w[:, h, :].T for h in range(B_HEADS)])


def _uk_unlayout(g):
    return jnp.stack([g[B_NOPE * h:B_NOPE * (h + 1), LANES * h:LANES * (h + 1)].T for h in range(B_HEADS)], axis=1)


def _uv_layout(w):
    return _block_diag([w[:, h, :] for h in range(B_HEADS)])


def _uv_unlayout(g):
    return jnp.stack([g[LANES * h:LANES * (h + 1), B_V * h:B_V * (h + 1)] for h in range(B_HEADS)], axis=1)


def _rope_tables(S):
    inv = ROPE_THETA ** (-jnp.arange(0, 32, 2, dtype=F32) / 32)
    tok = jnp.arange(S)

    def tab(pos):
        ang = pos.astype(F32)[:, None] * inv[None, :]
        cos, sin = jnp.cos(ang), jnp.sin(ang)
        return jnp.concatenate([cos, cos], axis=1), jnp.concatenate([-sin, sin], axis=1)

    cr, sr = tab(tok // GRID_W)
    cc, sc = tab(tok % GRID_W)
    ct, st = tab(tok)
    return (jnp.tile(jnp.concatenate([cr, cc], axis=1), (1, 2)), jnp.tile(jnp.concatenate([sr, sc], axis=1), (1, 2)),
            jnp.tile(ct, (1, 4)), jnp.tile(st, (1, 4)))


A_TQ, A_TK, A_SUB = 256, 4096, 1024
B_TQ, B_TK, B_SUB = 128, 4096, 1024
B_BWD_TK, B_BWD_SUB = 2048, 512
C_T = 256
C_BLOCKS_PER_STEP = 4


def _local_step(x0, tgt, mod, norm_w, wie, wuq, wuk, wuv, late_shards, a_q_norm, a_k_norm, q_lora_norm, kv_lora_norm,
                c_sink, final_norm):
    S = x0.shape[0]
    mod3 = mod.reshape(2, 3, D_MODEL)
    ca, sa, ct, st = _rope_tables(S)
    lane_seg = np.arange(LANES) // HEAD_DIM
    seg = jnp.asarray((lane_seg[:, None] == lane_seg[None, :]).astype(np.float32)).astype(BF16)
    qn = jnp.tile(a_q_norm.reshape(1, HEAD_DIM), (1, 2))
    kn = jnp.tile(a_k_norm.reshape(1, HEAD_DIM), (1, 2))
    qln, kvln = q_lora_norm.reshape(1, B_Q_LORA), kv_lora_norm.reshape(1, B_KV_LORA)
    nw0, nw1 = norm_w[0:1], norm_w[1:2]
    gate0, gate1 = mod3[0, 2:3], mod3[1, 2:3]
    a_tq, a_tk, b_tq, b_tk, bb_tk, c_t = min(A_TQ, S), min(A_TK, S), min(B_TQ, S), min(B_TK, S), min(B_BWD_TK, S), min(C_T, S)
    a_sub, b_sub, bb_sub = min(A_SUB, a_tk), min(B_SUB, b_tk), min(B_BWD_SUB, bb_tk)

    h0, proj_e, qa, ka, va, qcat, kcat = _even_pre_fwd(x0, mod3[0], nw0, wie, qn, kn, seg, ca, sa, ct, st, qln, kvln, wuq, wuk)
    kcat_t = kcat.T
    oa, lse_a, woe_g, wio_g, woo_g = _pp_fwd(qa, ka, va.T, kdiv=4, tq=a_tq, tk=a_tk, sub=a_sub, name="attn_a_fwd",
                                             side=_gather_chip4_halves(late_shards))
    woe = _even_out_layout(woe_g.reshape(D_MODEL, D_MODEL))
    wio = _odd_in_layout(_chips_to_cols(wio_g.reshape(N_CHIPS, D_MODEL, ODD_IN // N_CHIPS)))
    woo = _perm_heads(woo_g.reshape(D_MODEL, D_MODEL), C_ORDER, 0)
    olat, lse_b = _mla_fwd(qcat, kcat, kcat_t, tq=b_tq, tk=b_tk, sub=b_sub)
    y0, x1 = _even_post_fwd(oa, olat, proj_e, x0, gate0, wuv, woe)
    h1, proj_o, qc, kc, vc = _odd_pre_fwd(x1, mod3[1], nw1, wio)
    slopes = 2.0 ** (-8.0 * jnp.arange(1, C_HEADS + 1, dtype=F32) / C_HEADS)
    c_order = np.asarray(C_ORDER)
    slope_rows = jnp.repeat(slopes[c_order].reshape(C_HEADS // 2, 2), c_t, axis=1)[:, None, :]
    sink_rows = jnp.repeat(c_sink.reshape(C_HEADS)[c_order].reshape(C_HEADS // 2, 2), c_t, axis=1)[:, None, :]
    win_dist = _win_dist_table(S, c_t)
    oc, lse_c = _win_fwd(qc, kc, vc.T, win_dist, slope_rows, sink_rows, kdiv=4, tq=c_t, nbs=C_BLOCKS_PER_STEP, name="attn_c_fwd")
    doc, dgc, dx2, dwoo, st_f = _odd_post(oc, proj_o, x1, gate1, woo, final_norm.reshape(1, D_MODEL), tgt)
    dqc, dkc, dvc, dsink_raw = _win_bwd(qc, kc, kc.T, vc, oc, doc, lse_c, win_dist, slope_rows, sink_rows, kdiv=4, tq=c_t,
                                        nbs=C_BLOCKS_PER_STEP, name="attn_c_bwd")
    dx1, dwio, st_1 = _odd_pre_bwd(dqc, dkc, dvc, dgc, h1, x1, dx2, mod3[1], nw1, wio)
    doa, dga, dgb, dolat, dwoe, dwuv, st_e = _even_post_bwd(dx1, y0, oa, olat, proj_e, gate0, wuv, woe)
    late_grads = _reduce_exchange(
        [_even_out_unlayout(dwoe.astype(BF16)).reshape(N_CHIPS, D_MODEL // N_CHIPS, D_MODEL),
         _cols_to_chips(_odd_in_unlayout(dwio.astype(BF16))),
         _perm_heads(dwoo.astype(BF16), C_INV, 0).reshape(N_CHIPS, D_MODEL // N_CHIPS, D_MODEL)])
    dqa, dka, dva, p_woe, p_wio, p_woo = _pp_bwd(qa, ka, ka.T, va, oa, doa, lse_a, kdiv=4, tq=a_tq, tk=a_tk, sub=a_sub,
                                                 name="attn_a_bwd", side=late_grads)
    dqcat, dkcat = _mla_bwd(qcat, kcat, kcat_t, olat, dolat, lse_b, tq=b_tq, tk=bb_tk, sub=bb_sub)
    dx0, dwie, dwuq, dwuk, st_0, nst = _even_pre_bwd(x0, h0, proj_e, dqa, dka, dva, dga, dgb, dqcat, dkcat, dx1, mod3[0], nw0,
                                                     wie, qn, kn, seg, ca, sa, ct, st, qln, kvln, wuq, wuk)
    dsink_pairs = jnp.stack([dsink_raw[:, 0, 0], dsink_raw[:, 1, 0]], axis=1).reshape(C_HEADS)
    return dict(
        loss=st_f[2, 0], dx=dx0,
        dmod=jnp.stack([jnp.concatenate([st_0[0], st_0[1], st_e[0]]), jnp.concatenate([st_1[0], st_1[1], st_f[1]])]),
        norm_w=jnp.stack([st_0[2], st_1[2]]), final_norm=st_f[0],
        a_q_norm=nst[0:1, 0:HEAD_DIM], a_k_norm=nst[1:2, 0:HEAD_DIM], b_q_lora_norm=nst[2:3, :], b_kv_lora_norm=nst[3:4, 0:LANES],
        c_sink=dsink_pairs[np.asarray(C_INV)].reshape(1, C_HEADS),
        even_w_in=dwie, b_w_uq=dwuq, b_w_uk=dwuk, b_w_uv=dwuv, even_w_out=p_woe, odd_w_in=p_wio, odd_w_out=p_woo)


WEIGHT_NAMES = ("norm_w", "ada_w", "ada_b", "even_w_in", "a_q_norm", "a_k_norm", "b_q_lora_norm", "b_kv_lora_norm", "b_w_uq",
                "b_w_uk", "b_w_uv", "even_w_out", "odd_w_in", "c_sink", "odd_w_out", "final_norm")
SMALL_NAMES = ("dmod", "norm_w", "final_norm", "a_q_norm", "a_k_norm", "b_q_lora_norm", "b_kv_lora_norm", "c_sink")


def _cols_to_chips(g):
    r, n4 = g.shape
    return jnp.transpose(g.reshape(r, N_CHIPS, n4 // N_CHIPS), (1, 0, 2))


def _chips_to_cols(g):
    p, r, n = g.shape
    return jnp.transpose(g, (1, 0, 2)).reshape(r, p * n)


def kernel(x, c, norm_w, ada_w, ada_b, even_w_in, a_q_norm, a_k_norm, b_q_lora_norm, b_kv_lora_norm, b_w_uq, b_w_uk, b_w_uv, even_w_out, odd_w_in, c_sink, odd_w_out, final_norm, loss_target, m_norm_w, m_ada_w, m_ada_b, m_even_w_in, m_a_q_norm, m_a_k_norm, m_b_q_lora_norm, m_b_kv_lora_norm, m_b_w_uq, m_b_w_uk, m_b_w_uv, m_even_w_out, m_odd_w_in, m_c_sink, m_odd_w_out, m_final_norm, v_norm_w, v_ada_w, v_ada_b, v_even_w_in, v_a_q_norm, v_a_k_norm, v_b_q_lora_norm, v_b_kv_lora_norm, v_b_w_uq, v_b_w_uk, v_b_w_uv, v_even_w_out, v_odd_w_in, v_c_sink, v_odd_w_out, v_final_norm):
    given = dict(locals())
    xi, yi, ci = _my_place()
    chip = 2 * xi + yi
    dev = 2 * chip + ci
    n_ada = ada_w.shape[2]

    (c_all,) = _gather_dev8([c], "gather_c")
    c_all = c_all.reshape(N_DEV, D_MODEL)
    bias = lax.dynamic_slice_in_dim(ada_b, chip * n_ada, n_ada, axis=1).reshape(2, 1, n_ada)
    mod_cols = _ada_fwd(c_all, ada_w, bias)
    def halves(w):
        return w.astype(BF16).reshape((2, w.shape[0] // 2) + w.shape[1:])

    mod_all, wie_g, wuq_g = _gather_chip4_halves([mod_cols, halves(even_w_in[0]), halves(b_w_uq[0])]).run("gather_weights")
    wie_g = wie_g.reshape(N_CHIPS, D_MODEL, EVEN_IN // N_CHIPS)
    wuq_g = wuq_g.reshape(N_CHIPS, B_Q_LORA, -1)
    mod = jnp.transpose(lax.dynamic_index_in_dim(mod_all, dev, axis=2, keepdims=False), (1, 0, 2)).reshape(2, 3 * D_MODEL)

    res = _local_step(
        x[0], loss_target[0], mod, norm_w,
        _even_in_layout(_chips_to_cols(wie_g)), _uq_layout(_chips_to_cols(wuq_g)), _uk_layout(b_w_uk[0].astype(BF16)),
        _uv_layout(b_w_uv[0].astype(BF16)), [halves(even_w_out[0]), halves(odd_w_in[0]), halves(odd_w_out[0])],
        a_q_norm, a_k_norm, b_q_lora_norm, b_kv_lora_norm, c_sink, final_norm)

    shard_parts = dict(zip(
        ("even_w_in", "b_w_uq"),
        _reduce_exchange([_cols_to_chips(_even_in_unlayout(res["even_w_in"].astype(BF16))),
                          _cols_to_chips(_uq_unlayout(res["b_w_uq"].astype(BF16)))]).run("reduce_exchange")))
    shard_parts.update({k: res[k] for k in ("even_w_out", "odd_w_in", "odd_w_out")})

    small = jnp.concatenate([res[k].reshape(-1) for k in SMALL_NAMES]).reshape(1, -1)
    latent = jnp.stack([_uk_unlayout(res["b_w_uk"]).reshape(B_KV_LORA, 512),
                        _uv_unlayout(res["b_w_uv"]).reshape(B_KV_LORA, 512)]).astype(BF16)
    small_all, latent_all = _gather_dev8([small, latent], "gather_small")
    small_all = small_all.reshape(N_DEV, -1)
    parts, off = {}, 0
    for k in SMALL_NAMES:
        n = int(np.prod(res[k].shape))
        parts[k] = small_all[:, off:off + n]
        off += n
    dmod_all = parts.pop("dmod").reshape(N_DEV, 2, 3 * D_MODEL)
    dmod_cols = jnp.transpose(lax.dynamic_slice_in_dim(dmod_all, chip * n_ada, n_ada, axis=2), (1, 0, 2))
    parts["ada_w"] = _ada_bwd(c_all.T, dmod_cols).reshape(1, 2 * D_MODEL, n_ada)
    parts["ada_b"] = dmod_all
    parts["b_w_uk"], parts["b_w_uv"] = latent_all[:, 0], latent_all[:, 1]
    parts.update(shard_parts)

    grads, deltas, new_m, new_v = [], [], [], []
    for k in WEIGHT_NAMES:
        w = given[k]
        p = parts[k]
        shape2 = (p.shape[-2], p.shape[-1]) if p.ndim == 3 else (1, p.shape[-1])
        p = p.reshape((p.shape[0],) + shape2)
        outs = _adam(p, w.reshape(shape2), given["m_" + k].reshape(shape2), given["v_" + k].reshape(shape2), "adam_" + k)
        for lst, o in zip((grads, deltas, new_m, new_v), outs):
            lst.append(o.reshape(w.shape))
    loss = lax.psum(res["loss"], ("x", "y", "c"))
    return (loss, res["dx"][None], *grads, *deltas, *new_m, *new_v)
```

```python
import functools

import numpy as np
import jax
import jax.numpy as jnp
from jax import lax
from jax.experimental import pallas as pl
from jax.experimental.pallas import tpu as pltpu

F32 = jnp.float32
BF16 = jnp.bfloat16
HIGHEST = lax.Precision.HIGHEST
MESH_ID = pl.DeviceIdType.MESH

D_MODEL = 1024
HEAD_DIM = 64
GRID_W = 64
EPS = 1e-6
ROPE_THETA = 10000.0
A_HEADS, A_KV_HEADS = 8, 2
B_HEADS, B_NOPE, B_ROPE, B_V = 8, 64, 32, 64
B_Q_LORA, B_KV_LORA = 256, 128
C_HEADS, C_KV_HEADS = 16, 4
WINDOW = 128
EVEN_IN, ODD_IN = 2208, 2560
EVEN_P = 2304
N_CHIPS, N_DEV = 4, 8
LANES = 128
NEG = -1e30
VMEM_LIMIT = 60 * 1024 * 1024

ADAM_LR, ADAM_B1, ADAM_B2, ADAM_EPS, ADAM_WD, ADAM_STEP = 0.001, 0.9, 0.999, 1e-08, 0.01, 10

ROW_TILE = 256


def _dot(a, b):
    return lax.dot_general(a, b, (((1,), (0,)), ((), ())), preferred_element_type=F32)


def _dot_nt(a, b):
    return lax.dot_general(a, b, (((1,), (1,)), ((), ())), preferred_element_type=F32)


def _dot_tn(a, b):
    return lax.dot_general(a, b, (((0,), (0,)), ((), ())), preferred_element_type=F32)


def _dot_f32(a, b):
    return lax.dot_general(a, b, (((1,), (0,)), ((), ())), precision=HIGHEST, preferred_element_type=F32)


def _sigmoid(x):
    return 1.0 / (1.0 + jnp.exp(-x))


def _silu_and_grad(g):
    s = _sigmoid(g)
    return g * s, s * (1.0 + g * (1.0 - s))


def _lane_iota():
    return lax.broadcasted_iota(jnp.int32, (1, LANES), 1)


def _partner(x, lane):
    return jnp.where((lane % 32) < 16, pltpu.roll(x, LANES - 16, 1), pltpu.roll(x, 16, 1))


def _rot(x, cos, sin_signed, lane):
    return x * cos + _partner(x, lane) * sin_signed


def _rot_bwd(dy, cos, sin_signed, lane):
    return dy * cos + _partner(dy * sin_signed, lane)


def _rms(x):
    return lax.rsqrt(jnp.mean(x * x, axis=-1, keepdims=True) + EPS)


def _rms_bwd(x, r, g):
    return r * g - x * (r * r * r) * jnp.mean(x * g, axis=-1, keepdims=True)


def _seg_mean(v, seg_ones):
    hi = v.astype(BF16)
    lo = (v - hi.astype(F32)).astype(BF16)
    return (_dot(hi, seg_ones) + _dot(lo, seg_ones)) * (1.0 / HEAD_DIM)


def _row_spec(ts, cols):
    return pl.BlockSpec((ts, cols), lambda i: (i, 0))


def _full_spec(shape, single=True):
    nd = len(shape)
    if single:
        return pl.BlockSpec(shape, lambda i: (0,) * nd, pipeline_mode=pl.Buffered(1))
    return pl.BlockSpec(shape, lambda i: (0,) * nd)


def _sds(shape, dtype):
    return jax.ShapeDtypeStruct(shape, dtype)


def _params(sem):
    return pltpu.CompilerParams(dimension_semantics=sem, vmem_limit_bytes=VMEM_LIMIT)


def _even_pre_fwd(x, mod, nw, wie, qn, kn, seg, ca, sa, ct, st, qln, kvln, wuq, wuk):
    S = x.shape[0]
    ts = min(ROW_TILE, S)

    def body(x_ref, mod_ref, nw_ref, wie_ref, qn_ref, kn_ref, seg_ref, ca_ref, sa_ref, ct_ref, st_ref, qln_ref,
             kvln_ref, wuq_ref, wuk_ref, h_ref, proj_ref, qa_ref, ka_ref, va_ref, qcat_ref, kcat_ref, kat_ref, vat_ref, kcatt_ref):
        xv = x_ref[...]
        h = (xv * _rms(xv) * nw_ref[...]) * (1.0 + mod_ref[1:2, :]) + mod_ref[0:1, :]
        hb = h.astype(BF16)
        h_ref[...] = hb
        proj = _dot(hb, wie_ref[...])
        proj_ref[...] = proj
        lane = _lane_iota()
        ca_v, sa_v, ct_v, st_v = ca_ref[...], sa_ref[...], ct_ref[...], st_ref[...]
        seg_v = seg_ref[...]
        for cb in range(4):
            xc = proj[:, LANES * cb:LANES * (cb + 1)]
            r = lax.rsqrt(_seg_mean(xc * xc, seg_v) + EPS)
            y = _rot(xc * r * qn_ref[...], ca_v, sa_v, lane)
            qa_ref[:, LANES * cb:LANES * (cb + 1)] = (y * 0.125).astype(BF16)
        kc = proj[:, 512:640]
        r = lax.rsqrt(_seg_mean(kc * kc, seg_v) + EPS)
        ka_v = _rot(kc * r * kn_ref[...], ca_v, sa_v, lane)
        ka_ref[...] = ka_v.astype(BF16)
        kat_ref[...] = ka_v.T.astype(BF16)
        va_ref[...] = proj[:, 640:768].astype(BF16)
        vat_ref[...] = proj[:, 640:768].T.astype(BF16)
        cq = proj[:, 1280:1536]
        cqn = (cq * _rms(cq) * qln_ref[...]).astype(BF16)
        ckv = proj[:, 1536:1664]
        ckvn = ckv * _rms(ckv) * kvln_ref[...]
        qb = _dot(cqn, wuq_ref[...])
        qlat = _dot(qb[:, 0:512].astype(BF16), wuk_ref[...])
        for hh in range(B_HEADS):
            qcat_ref[hh, :, 0:LANES] = qlat[:, LANES * hh:LANES * (hh + 1)].astype(BF16)
            qr = _rot(qb[:, 512 + LANES * hh:512 + LANES * (hh + 1)], ct_v, st_v, lane)
            qcat_ref[hh, :, LANES:2 * LANES] = qr.astype(BF16)
        kr = _rot(proj[:, 1664:1792], ct_v, st_v, lane)
        kcat_ref[:, 0:LANES] = ckvn.astype(BF16)
        kcat_ref[:, LANES:2 * LANES] = kr.astype(BF16)
        kcatt_ref[0:LANES, :] = ckvn.T.astype(BF16)
        kcatt_ref[LANES:2 * LANES, :] = kr.T.astype(BF16)

    col_spec = lambda rows: pl.BlockSpec((rows, ts), lambda i: (0, i))
    return pl.pallas_call(
        body, name="even_pre_fwd", grid=(S // ts,),
        in_specs=[_row_spec(ts, D_MODEL), _full_spec((3, D_MODEL)), _full_spec((1, D_MODEL)), _full_spec((D_MODEL, EVEN_P)),
                  _full_spec((1, LANES)), _full_spec((1, LANES)), _full_spec((LANES, LANES)),
                  _row_spec(ts, LANES), _row_spec(ts, LANES), _row_spec(ts, LANES), _row_spec(ts, LANES),
                  _full_spec((1, B_Q_LORA)), _full_spec((1, B_KV_LORA)), _full_spec((B_Q_LORA, 1536)), _full_spec((512, 1024))],
        out_specs=[_row_spec(ts, D_MODEL), _row_spec(ts, EVEN_P), _row_spec(ts, 512), _row_spec(ts, LANES), _row_spec(ts, LANES),
                   pl.BlockSpec((B_HEADS, ts, 2 * LANES), lambda i: (0, i, 0)), _row_spec(ts, 2 * LANES),
                   col_spec(LANES), col_spec(LANES), col_spec(2 * LANES)],
        out_shape=[_sds((S, D_MODEL), BF16), _sds((S, EVEN_P), F32), _sds((S, 512), BF16), _sds((S, LANES), BF16),
                   _sds((S, LANES), BF16), _sds((B_HEADS, S, 2 * LANES), BF16), _sds((S, 2 * LANES), BF16),
                   _sds((LANES, S), BF16), _sds((LANES, S), BF16), _sds((2 * LANES, S), BF16)],
        compiler_params=_params(("arbitrary",)),
    )(x, mod, nw, wie, qn, kn, seg, ca, sa, ct, st, qln, kvln, wuq, wuk)


MLA_SCALE = (B_NOPE + B_ROPE) ** -0.5
LOG2E = 1.4426950408889634

def _row_lo():
    return lax.broadcasted_iota(jnp.int32, (LANES, 1), 0) < HEAD_DIM


def _stack_cols(vT, rlo):
    zero = jnp.zeros_like(vT)
    return jnp.concatenate([jnp.where(rlo, vT, zero), jnp.where(rlo, zero, vT)], axis=1)


def _stack_rows(v, lo):
    zero = jnp.zeros_like(v)
    return jnp.concatenate([jnp.where(lo, v, zero), jnp.where(lo, zero, v)], axis=0)


def _pick_halves_T(xT, rlo, t):
    return jnp.where(rlo, xT[:, 0:t], xT[:, t:2 * t]).T


def _side_split(refs, n_in, n_out, n_scratch, side):
    ns = side.n if side is not None else 0
    cuts = np.cumsum([0, n_in, ns, n_out, ns, n_scratch])
    return [refs[a:b] for a, b in zip(cuts[:-1], cuts[1:])] + [refs[cuts[-1]:]]


def _side_hooks(side, side_ins, side_outs, side_sems, step, total):
    if side is None:
        return lambda: None
    start, mid, end = side.phases(side_ins, side_outs, side_sems)
    pl.when(step == 0)(start)
    pl.when(step == total // 2)(mid)
    return lambda: pl.when(step == total - 1)(end)


def _side_specs(side):
    if side is None:
        return [], [], [], [], []
    return list(side.arrs), [_ANY] * side.n, [_ANY] * side.n, list(side.out_shapes), side.sem_shapes()


def _pp_fwd(q, k, vT, *, kdiv, tq, tk, sub, name, side=None):
    S = k.shape[0]; nb = q.shape[1] // LANES; nq = S // tq; nkv = S // tk; nsub = tk // sub

    def body(*refs):
        (q_ref, k_ref, vT_ref), side_ins, (o_ref, lse_ref), side_outs, (qs, m_s, l_s, acc), side_sems = _side_split(refs, 3, 2, 4, side)
        j = pl.program_id(2)
        rlo = _row_lo()
        step = (pl.program_id(0) * nq + pl.program_id(1)) * nkv + j
        side_end = _side_hooks(side, side_ins, side_outs, side_sems, step, nb * nq * nkv)

        @pl.when(j == 0)
        def _():
            qs[...] = _stack_cols(q_ref[...].astype(F32).T, rlo).astype(BF16)
            m_s[...] = jnp.full((1, 2 * tq), NEG, F32)
            l_s[...] = jnp.zeros((1, 2 * tq), F32)
            acc[...] = jnp.zeros((LANES, 2 * tq), F32)

        qsv = qs[...]
        m, l, a = m_s[...], l_s[...], acc[...]
        s_cur = _dot(k_ref[0:sub, :], qsv)
        for t in range(nsub):
            if t + 1 < nsub:
                s_next = _dot(k_ref[sub * (t + 1):sub * (t + 2), :], qsv)
            m_new = jnp.maximum(m, jnp.max(s_cur, axis=0, keepdims=True))
            alpha = jnp.exp(m - m_new)
            p = jnp.exp(s_cur - m_new)
            l = alpha * l + jnp.sum(p, axis=0, keepdims=True)
            a = alpha * a + _dot(vT_ref[:, sub * t:sub * (t + 1)], p.astype(BF16))
            m = m_new
            if t + 1 < nsub:
                s_cur = s_next
        m_s[...], l_s[...], acc[...] = m, l, a

        @pl.when(j == nkv - 1)
        def _():
            l_f = l_s[...]
            o_ref[...] = _pick_halves_T(acc[...] / l_f, rlo, tq)
            lse_ref[0, 0] = m_s[...] + jnp.log(l_f)

        side_end()

    s_args, s_in, s_out, s_shapes, s_sems = _side_specs(side)
    return pl.pallas_call(
        body, name=name, grid=(nb, nq, nkv),
        in_specs=[pl.BlockSpec((tq, LANES), lambda b, i, j: (i, b)), pl.BlockSpec((tk, LANES), lambda b, i, j: (j, b // kdiv)),
                  pl.BlockSpec((LANES, tk), lambda b, i, j: (b // kdiv, j))] + s_in,
        out_specs=[pl.BlockSpec((tq, LANES), lambda b, i, j: (i, b)),
                   pl.BlockSpec((1, 1, 1, 2 * tq), lambda b, i, j: (b, i, 0, 0))] + s_out,
        out_shape=[_sds((S, nb * LANES), F32), _sds((nb, nq, 1, 2 * tq), F32)] + s_shapes,
        scratch_shapes=[pltpu.VMEM((LANES, 2 * tq), BF16), pltpu.VMEM((1, 2 * tq), F32), pltpu.VMEM((1, 2 * tq), F32),
                        pltpu.VMEM((LANES, 2 * tq), F32)] + s_sems,
        compiler_params=_params(("arbitrary",) * 3))(q, k, vT, *s_args)


def _pp_bwd(q, k, kT, v, o, do, lse, *, kdiv, tq, tk, sub, name, side=None):
    S = k.shape[0]; nb = q.shape[1] // LANES; nkb = k.shape[1] // LANES; nq = S // tq; nkv = S // tk; nsub = tk // sub

    def body(*refs):
        ((q_ref, k_ref, kT_ref, v_ref, o_ref, do_ref, lse_ref), side_ins, (dq_ref, dk_ref, dv_ref), side_outs,
         (qsT, qs, dosT, dos, delta_s, dq_acc), side_sems) = _side_split(refs, 7, 3, 6, side)
        b, i, j = pl.program_id(0), pl.program_id(1), pl.program_id(2)
        rlo = _row_lo()
        lo = lax.broadcasted_iota(jnp.int32, (1, LANES), 1) < HEAD_DIM
        side_end = _side_hooks(side, side_ins, side_outs, side_sems, (b * nq + i) * nkv + j, nb * nq * nkv)

        @pl.when((b % kdiv == 0) & (i == 0) & (j == 0))
        def _():
            dk_ref[...] = jnp.zeros((S, LANES), F32)
            dv_ref[...] = jnp.zeros((S, LANES), F32)

        @pl.when(j == 0)
        def _():
            qv = q_ref[...]
            qs[...] = _stack_rows(qv, lo)
            qsT[...] = _stack_cols(qv.astype(F32).T, rlo).astype(BF16)
            dov = do_ref[...]
            dos[...] = _stack_rows(dov.astype(BF16), lo)
            dosT[...] = _stack_cols(dov.T, rlo).astype(BF16)
            prodT = (dov * o_ref[...]).T
            delta_s[...] = jnp.concatenate([jnp.sum(jnp.where(rlo, prodT, 0.0), axis=0, keepdims=True),
                                            jnp.sum(jnp.where(rlo, 0.0, prodT), axis=0, keepdims=True)], axis=1)
            dq_acc[...] = jnp.zeros((LANES, 2 * tq), F32)

        qsTv, dosTv, qsv, dosv = qsT[...], dosT[...], qs[...], dos[...]
        lse_v, delta_v = lse_ref[0, 0], delta_s[...]
        dqa = dq_acc[...]
        s_cur = _dot(k_ref[0:sub, :], qsTv)
        dp_cur = _dot(v_ref[0:sub, :], dosTv)
        for t in range(nsub):
            if t + 1 < nsub:
                s_next = _dot(k_ref[sub * (t + 1):sub * (t + 2), :], qsTv)
                dp_next = _dot(v_ref[sub * (t + 1):sub * (t + 2), :], dosTv)
            p = jnp.exp(s_cur - lse_v)
            ds = (p * (dp_cur - delta_v)).astype(BF16)
            rows = pl.ds(pl.multiple_of(j * tk + sub * t, sub), sub)
            dv_ref[rows, :] += _dot(p.astype(BF16), dosv)
            dk_ref[rows, :] += _dot(ds, qsv)
            dqa = dqa + _dot(kT_ref[:, sub * t:sub * (t + 1)], ds)
            if t + 1 < nsub:
                s_cur, dp_cur = s_next, dp_next
        dq_acc[...] = dqa

        @pl.when(j == nkv - 1)
        def _():
            dq_ref[...] = _pick_halves_T(dq_acc[...], rlo, tq)

        side_end()

    qmap = lambda b, i, j: (i, b)
    kmap = lambda b, i, j: (j, b // kdiv)
    res = lambda b, i, j: (0, b // kdiv)
    s_args, s_in, s_out, s_shapes, s_sems = _side_specs(side)
    return pl.pallas_call(
        body, name=name, grid=(nb, nq, nkv),
        in_specs=[pl.BlockSpec((tq, LANES), qmap), pl.BlockSpec((tk, LANES), kmap), pl.BlockSpec((LANES, tk), lambda b, i, j: (b // kdiv, j)),
                  pl.BlockSpec((tk, LANES), kmap), pl.BlockSpec((tq, LANES), qmap), pl.BlockSpec((tq, LANES), qmap),
                  pl.BlockSpec((1, 1, 1, 2 * tq), lambda b, i, j: (b, i, 0, 0))] + s_in,
        out_specs=[pl.BlockSpec((tq, LANES), qmap), pl.BlockSpec((S, LANES), res), pl.BlockSpec((S, LANES), res)] + s_out,
        out_shape=[_sds((S, nb * LANES), F32), _sds((S, nkb * LANES), F32), _sds((S, nkb * LANES), F32)] + s_shapes,
        scratch_shapes=[pltpu.VMEM((LANES, 2 * tq), BF16), pltpu.VMEM((2 * tq, LANES), BF16), pltpu.VMEM((LANES, 2 * tq), BF16),
                        pltpu.VMEM((2 * tq, LANES), BF16), pltpu.VMEM((1, 2 * tq), F32), pltpu.VMEM((LANES, 2 * tq), F32)] + s_sems,
        compiler_params=_params(("arbitrary",) * 3))(q, k, kT, v, o, do, lse, *s_args)


MLA_C = MLA_SCALE * LOG2E


def _mla_fwd(q, kcat, kcatT, *, tq, tk, sub):
    S = kcat.shape[0]; nq, nkv = S // tq, S // tk; R = B_HEADS * tq; nsub = tk // sub

    def body(q_ref, k_ref, vT_ref, o_ref, lse_ref, qT, m_s, l_s, acc):
        j = pl.program_id(1)

        @pl.when(j == 0)
        def _():
            qT[...] = q_ref[...].reshape(R, 2 * LANES).astype(F32).T.astype(BF16)
            m_s[...] = jnp.full((1, R), NEG, F32)
            l_s[...] = jnp.zeros((1, R), F32)
            acc[...] = jnp.zeros((LANES, R), F32)

        qTv = qT[...]
        m, l, a = m_s[...], l_s[...], acc[...]
        s_cur = _dot(k_ref[0:sub, :], qTv)
        for t in range(nsub):
            if t + 1 < nsub:
                s_next = _dot(k_ref[sub * (t + 1):sub * (t + 2), :], qTv)
            m_new = jnp.maximum(m, jnp.max(s_cur, axis=0, keepdims=True))
            alpha = jnp.exp2((m - m_new) * MLA_C)
            p = jnp.exp2((s_cur - m_new) * MLA_C)
            l = alpha * l + jnp.sum(p, axis=0, keepdims=True)
            a = alpha * a + _dot(vT_ref[:, sub * t:sub * (t + 1)], p.astype(BF16))
            m = m_new
            if t + 1 < nsub:
                s_cur = s_next
        m_s[...], l_s[...], acc[...] = m, l, a

        @pl.when(j == nkv - 1)
        def _():
            l_f = l_s[...]
            o_ref[...] = (acc[...] / l_f).T.reshape(B_HEADS, tq, LANES)
            lse_ref[0] = m_s[...] * MLA_SCALE + jnp.log(l_f)

    return pl.pallas_call(
        body, name="mla_fwd", grid=(nq, nkv),
        in_specs=[pl.BlockSpec((B_HEADS, tq, 2 * LANES), lambda i, j: (0, i, 0)), pl.BlockSpec((tk, 2 * LANES), lambda i, j: (j, 0)),
                  pl.BlockSpec((LANES, tk), lambda i, j: (0, j))],
        out_specs=[pl.BlockSpec((B_HEADS, tq, LANES), lambda i, j: (0, i, 0)), pl.BlockSpec((1, 1, R), lambda i, j: (i, 0, 0))],
        out_shape=[_sds((B_HEADS, S, LANES), F32), _sds((nq, 1, R), F32)],
        scratch_shapes=[pltpu.VMEM((2 * LANES, R), BF16), pltpu.VMEM((1, R), F32), pltpu.VMEM((1, R), F32), pltpu.VMEM((LANES, R), F32)],
        compiler_params=_params(("arbitrary", "arbitrary")))(q, kcat, kcatT)


def _mla_bwd(q, kcat, kcatT, o, do, lse, *, tq, tk, sub):
    S = kcat.shape[0]; nq, nkv = S // tq, S // tk; R = B_HEADS * tq; nsub = tk // sub

    def body(q_ref, k_ref, kT_ref, o_ref, do_ref, lse_ref, dq_ref, dk_ref, qT, dosT, dos, delta_s, dq_acc):
        i, j = pl.program_id(0), pl.program_id(1)

        @pl.when((i == 0) & (j == 0))
        def _():
            dk_ref[...] = jnp.zeros((S, 2 * LANES), F32)

        @pl.when(j == 0)
        def _():
            qT[...] = q_ref[...].reshape(R, 2 * LANES).astype(F32).T.astype(BF16)
            dov = do_ref[...].reshape(R, LANES)
            dos[...] = dov.astype(BF16)
            dosT[...] = dov.T.astype(BF16)
            delta_s[...] = jnp.sum((dov * o_ref[...].reshape(R, LANES)).T, axis=0, keepdims=True)
            dq_acc[...] = jnp.zeros((2 * LANES, R), F32)

        qTv, dosTv, dosv = qT[...], dosT[...], dos[...]
        qv = q_ref[...].reshape(R, 2 * LANES)
        lse_v, delta_v = lse_ref[0] * LOG2E, delta_s[...]
        dqa = dq_acc[...]
        s_cur = _dot(k_ref[0:sub, :], qTv)
        dp_cur = _dot(k_ref[0:sub, 0:LANES], dosTv)
        for t in range(nsub):
            if t + 1 < nsub:
                s_next = _dot(k_ref[sub * (t + 1):sub * (t + 2), :], qTv)
                dp_next = _dot(k_ref[sub * (t + 1):sub * (t + 2), 0:LANES], dosTv)
            p = jnp.exp2(s_cur * MLA_C - lse_v)
            ds = (p * (dp_cur - delta_v) * MLA_SCALE).astype(BF16)
            rows = pl.ds(pl.multiple_of(j * tk + sub * t, sub), sub)
            dk_ref[rows, :] += _dot(ds, qv)
            dk_ref[rows, 0:LANES] += _dot(p.astype(BF16), dosv)
            dqa = dqa + _dot(kT_ref[:, sub * t:sub * (t + 1)], ds)
            if t + 1 < nsub:
                s_cur, dp_cur = s_next, dp_next
        dq_acc[...] = dqa

        @pl.when(j == nkv - 1)
        def _():
            dq_ref[...] = dq_acc[...].T.reshape(B_HEADS, tq, 2 * LANES)

    hspec = lambda w: pl.BlockSpec((B_HEADS, tq, w), lambda i, j: (0, i, 0))
    return pl.pallas_call(
        body, name="mla_bwd", grid=(nq, nkv),
        in_specs=[hspec(2 * LANES), pl.BlockSpec((tk, 2 * LANES), lambda i, j: (j, 0)), pl.BlockSpec((2 * LANES, tk), lambda i, j: (0, j)),
                  hspec(LANES), hspec(LANES), pl.BlockSpec((1, 1, R), lambda i, j: (i, 0, 0))],
        out_specs=[hspec(2 * LANES), pl.BlockSpec((S, 2 * LANES), lambda i, j: (0, 0))],
        out_shape=[_sds((B_HEADS, S, 2 * LANES), F32), _sds((S, 2 * LANES), F32)],
        scratch_shapes=[pltpu.VMEM((2 * LANES, R), BF16), pltpu.VMEM((LANES, R), BF16), pltpu.VMEM((R, LANES), BF16),
                        pltpu.VMEM((1, R), F32), pltpu.VMEM((2 * LANES, R), F32)],
        compiler_params=_params(("arbitrary", "arbitrary")))(q, kcat, kcatT, o, do, lse)


def _win_start(i, tq, nk, S):
    return pl.multiple_of(jnp.clip(i * tq - WINDOW, 0, S - nk), LANES)


def _win_dist_table(S, tq):
    nk = min(tq + 2 * WINDOW, S)
    nq = S // tq
    r = np.arange(nk)[:, None]
    c = (np.arange(2 * tq) % tq)[None, :]
    tabs = []
    for rel in (0, WINDOW, (nq - 1) * tq - (S - nk)):
        dist = np.abs(rel + c - r).astype(np.float32)
        tabs.append(np.where(dist <= WINDOW, dist, np.float32(1e32)))
    return jnp.asarray(np.stack(tabs))


def _win_dist_spec(nk, tq, nq):
    return pl.BlockSpec((1, nk, 2 * tq), lambda b, i: (jnp.where(i == 0, 0, jnp.where(i == nq - 1, 2, 1)), 0, 0))


def _win_fwd(q, k, vT, dist, slope, sink, *, kdiv, tq, nbs, name):
    S = k.shape[0]; nb = q.shape[1] // LANES; nq = S // tq; nk = min(tq + 2 * WINDOW, S)
    assert nb % nbs == 0 and kdiv % nbs == 0

    def body(q_ref, k_ref, vT_ref, dist_ref, slope_ref, sink_ref, o_ref, lse_ref):
        i = pl.program_id(1)
        rlo = _row_lo()
        k0 = _win_start(i, tq, nk, S)
        kk, vv, dd = k_ref[pl.ds(k0, nk), :], vT_ref[:, pl.ds(k0, nk)], dist_ref[0]
        for u in range(nbs):
            qsT = _stack_cols(q_ref[:, LANES * u:LANES * (u + 1)].astype(F32).T, rlo).astype(BF16)
            s = _dot(kk, qsT) - slope_ref[u] * dd
            sk = sink_ref[u]
            m = jnp.maximum(jnp.max(s, axis=0, keepdims=True), sk)
            p = jnp.exp(s - m)
            l = jnp.sum(p, axis=0, keepdims=True) + jnp.exp(sk - m)
            o_ref[:, LANES * u:LANES * (u + 1)] = _pick_halves_T(_dot(vv, p.astype(BF16)) / l, rlo, tq)
            lse_ref[u, 0] = m + jnp.log(l)

    row_spec = pl.BlockSpec((nbs, 1, 2 * tq), lambda b, i: (b, 0, 0))
    return pl.pallas_call(
        body, name=name, grid=(nb // nbs, nq),
        in_specs=[pl.BlockSpec((tq, nbs * LANES), lambda b, i: (i, b)), pl.BlockSpec((S, LANES), lambda b, i: (0, b * nbs // kdiv)),
                  pl.BlockSpec((LANES, S), lambda b, i: (b * nbs // kdiv, 0)), _win_dist_spec(nk, tq, nq), row_spec, row_spec],
        out_specs=[pl.BlockSpec((tq, nbs * LANES), lambda b, i: (i, b)), pl.BlockSpec((nbs, 1, 1, 2 * tq), lambda b, i: (b, i, 0, 0))],
        out_shape=[_sds((S, nb * LANES), F32), _sds((nb, nq, 1, 2 * tq), F32)],
        compiler_params=_params(("arbitrary", "arbitrary")))(q, k, vT, dist, slope, sink)


def _win_bwd(q, k, kT, v, o, do, lse, dist, slope, sink, *, kdiv, tq, nbs, name):
    S = k.shape[0]; nb = q.shape[1] // LANES; nkb = k.shape[1] // LANES; nq = S // tq; nk = min(tq + 2 * WINDOW, S)
    steps_per_kv = kdiv // nbs

    def body(q_ref, k_ref, kT_ref, v_ref, o_ref, do_ref, lse_ref, dist_ref, slope_ref, sink_ref, dq_ref, dk_ref, dv_ref, dsink_ref, ds_acc):
        b, i = pl.program_id(0), pl.program_id(1)
        rlo = _row_lo()
        lo = lax.broadcasted_iota(jnp.int32, (1, LANES), 1) < HEAD_DIM

        @pl.when((b % steps_per_kv == 0) & (i == 0))
        def _():
            dk_ref[...] = jnp.zeros((S, LANES), F32)
            dv_ref[...] = jnp.zeros((S, LANES), F32)

        @pl.when(i == 0)
        def _():
            ds_acc[...] = jnp.zeros((nbs, 2 * tq), F32)

        k0 = _win_start(i, tq, nk, S)
        rows = pl.ds(k0, nk)
        kk, vv, kkT, dd = k_ref[rows, :], v_ref[rows, :], kT_ref[:, rows], dist_ref[0]
        dv_sum, dk_sum = None, None
        for u in range(nbs):
            cols = slice(LANES * u, LANES * (u + 1))
            qv = q_ref[:, cols]
            qs = _stack_rows(qv, lo)
            qsT = _stack_cols(qv.astype(F32).T, rlo).astype(BF16)
            dov = do_ref[:, cols]
            dos = _stack_rows(dov.astype(BF16), lo)
            dosT = _stack_cols(dov.T, rlo).astype(BF16)
            prodT = (dov * o_ref[:, cols]).T
            delta = jnp.concatenate([jnp.sum(jnp.where(rlo, prodT, 0.0), axis=0, keepdims=True),
                                     jnp.sum(jnp.where(rlo, 0.0, prodT), axis=0, keepdims=True)], axis=1)
            lse_v = lse_ref[u, 0]
            ds_acc[u:u + 1, :] += -jnp.exp(sink_ref[u] - lse_v) * delta
            p = jnp.exp(_dot(kk, qsT) - slope_ref[u] * dd - lse_v)
            ds = (p * (_dot(vv, dosT) - delta)).astype(BF16)
            dv_u, dk_u = _dot(p.astype(BF16), dos), _dot(ds, qs)
            dv_sum = dv_u if dv_sum is None else dv_sum + dv_u
            dk_sum = dk_u if dk_sum is None else dk_sum + dk_u
            dq_ref[:, cols] = (_pick_halves_T(_dot(kkT, ds), rlo, tq) * 0.125).astype(BF16)
        dv_ref[rows, :] += dv_sum
        dk_ref[rows, :] += dk_sum

        @pl.when(i == nq - 1)
        def _():
            acc = ds_acc[...]
            for u in range(nbs):
                dsink_ref[u] = jnp.concatenate(
                    [jnp.broadcast_to(jnp.sum(acc[u:u + 1, 0:tq], axis=1, keepdims=True), (1, LANES)),
                     jnp.broadcast_to(jnp.sum(acc[u:u + 1, tq:2 * tq], axis=1, keepdims=True), (1, LANES)),
                     jnp.zeros((6, LANES), F32)], axis=0)

    qmap = lambda b, i: (i, b)
    kv_spec = pl.BlockSpec((S, LANES), lambda b, i: (0, b * nbs // kdiv))
    row_spec = pl.BlockSpec((nbs, 1, 2 * tq), lambda b, i: (b, 0, 0))
    wide = pl.BlockSpec((tq, nbs * LANES), qmap)
    return pl.pallas_call(
        body, name=name, grid=(nb // nbs, nq),
        in_specs=[wide, kv_spec, pl.BlockSpec((LANES, S), lambda b, i: (b * nbs // kdiv, 0)), kv_spec, wide, wide,
                  pl.BlockSpec((nbs, 1, 1, 2 * tq), lambda b, i: (b, i, 0, 0)), _win_dist_spec(nk, tq, nq), row_spec, row_spec],
        out_specs=[wide, kv_spec, kv_spec, pl.BlockSpec((nbs, 8, LANES), lambda b, i: (b, 0, 0))],
        out_shape=[_sds((S, nb * LANES), BF16), _sds((S, nkb * LANES), F32), _sds((S, nkb * LANES), F32), _sds((nb, 8, LANES), F32)],
        scratch_shapes=[pltpu.VMEM((nbs, 2 * tq), F32)],
        compiler_params=_params(("arbitrary", "arbitrary")))(q, k, kT, v, o, do, lse, dist, slope, sink)


def _sum_rows(v):
    return jnp.sum(v, axis=0, keepdims=True)


def _norm_mod_bwd(dh, xv, mod_ref, nw_ref, stats_ref):
    r = _rms(xv)
    xn = xv * r
    nw = nw_ref[...]
    stats_ref[0:1, :] += _sum_rows(dh)
    stats_ref[1:2, :] += _sum_rows(dh * (xn * nw))
    dn = dh * (1.0 + mod_ref[1:2, :])
    stats_ref[2:3, :] += _sum_rows(dn * xn)
    return _rms_bwd(xv, r, dn * nw)


def _even_gate_specs(ts):
    return [pl.BlockSpec((ts, 256), lambda i, c=c: (i, c)) for c in (3, 4, 7, 8)]


def _even_post_fwd(oa, olat, proj, x, gate, wuv, woe):
    S = x.shape[0]
    ts = min(ROW_TILE, S)

    def body(oa_ref, ol_ref, ga0_ref, ga1_ref, gb0_ref, gb1_ref, x_ref, gate_ref, wuv_ref, woe_ref, y_ref, x1_ref):
        sa, _ = _silu_and_grad(jnp.concatenate([ga0_ref[...], ga1_ref[...]], axis=1))
        sb, _ = _silu_and_grad(jnp.concatenate([gb0_ref[...], gb1_ref[...]], axis=1))
        olc = jnp.concatenate([ol_ref[hh] for hh in range(B_HEADS)], axis=1).astype(BF16)
        ob = _dot(olc, wuv_ref[...])
        mix = jnp.concatenate([oa_ref[...] * sa, ob * sb], axis=1).astype(BF16)
        y = _dot(mix, woe_ref[...])
        y_ref[...] = y
        x1_ref[...] = x_ref[...] + gate_ref[...] * y

    return pl.pallas_call(
        body, name="even_post_fwd", grid=(S // ts,),
        in_specs=[_row_spec(ts, 512), pl.BlockSpec((B_HEADS, ts, LANES), lambda i: (0, i, 0))] + _even_gate_specs(ts) +
                 [_row_spec(ts, D_MODEL), _full_spec((1, D_MODEL)), _full_spec((1024, 512)), _full_spec((1024, D_MODEL))],
        out_specs=[_row_spec(ts, D_MODEL), _row_spec(ts, D_MODEL)],
        out_shape=[_sds((S, D_MODEL), F32), _sds((S, D_MODEL), F32)],
        compiler_params=_params(("arbitrary",)),
    )(oa, olat, proj, proj, proj, proj, x, gate, wuv, woe)


def _odd_pre_fwd(x, mod, nw, wio):
    S = x.shape[0]
    ts = min(ROW_TILE, S)

    def body(x_ref, mod_ref, nw_ref, wio_ref, h_ref, proj_ref, q_ref, k_ref, v_ref, kt_ref, vt_ref):
        xv = x_ref[...]
        h = (xv * _rms(xv) * nw_ref[...]) * (1.0 + mod_ref[1:2, :]) + mod_ref[0:1, :]
        hb = h.astype(BF16)
        h_ref[...] = hb
        proj = _dot(hb, wio_ref[...])
        proj_ref[...] = proj
        q_ref[...] = (proj[:, 0:1024] * 0.125).astype(BF16)
        k_ref[...] = proj[:, 1024:1280].astype(BF16)
        v_ref[...] = proj[:, 1280:1536].astype(BF16)
        kt_ref[...] = proj[:, 1024:1280].T.astype(BF16)
        vt_ref[...] = proj[:, 1280:1536].T.astype(BF16)

    col_spec = pl.BlockSpec((256, ts), lambda i: (0, i))
    return pl.pallas_call(
        body, name="odd_pre_fwd", grid=(S // ts,),
        in_specs=[_row_spec(ts, D_MODEL), _full_spec((3, D_MODEL)), _full_spec((1, D_MODEL)), _full_spec((D_MODEL, ODD_IN))],
        out_specs=[_row_spec(ts, D_MODEL), _row_spec(ts, ODD_IN), _row_spec(ts, 1024), _row_spec(ts, 256), _row_spec(ts, 256),
                   col_spec, col_spec],
        out_shape=[_sds((S, D_MODEL), BF16), _sds((S, ODD_IN), F32), _sds((S, 1024), BF16), _sds((S, 256), BF16),
                   _sds((S, 256), BF16), _sds((256, S), BF16), _sds((256, S), BF16)],
        compiler_params=_params(("arbitrary",)),
    )(x, mod, nw, wio)


def _odd_post(oc, proj, x1, gate, woo, fw, tgt):
    S = x1.shape[0]
    ts = min(ROW_TILE, S)

    def body(oc_ref, g0_ref, g1_ref, x_ref, gate_ref, woo_ref, fw_ref, tgt_ref, doc_ref, dgc_ref, dx2_ref, dwoo_ref, stats_ref):
        @pl.when(pl.program_id(0) == 0)
        def _():
            dwoo_ref[...] = jnp.zeros((D_MODEL, D_MODEL), F32)
            stats_ref[...] = jnp.zeros((8, D_MODEL), F32)

        ocv = oc_ref[...]
        sg, dsg = _silu_and_grad(jnp.concatenate([g0_ref[...], g1_ref[...]], axis=1))
        mix = (ocv * sg).astype(BF16)
        woo_v = woo_ref[...]
        y = _dot(mix, woo_v)
        gate_v = gate_ref[...]
        x2 = x_ref[...] + gate_v * y
        r = _rms(x2)
        fw_v = fw_ref[...]
        xn = x2 * r
        err = xn * fw_v - tgt_ref[...]
        dout = err * (1.0 / D_MODEL)
        dx2 = _rms_bwd(x2, r, dout * fw_v)
        dx2_ref[...] = dx2
        stats_ref[0:1, :] += _sum_rows(dout * xn)
        stats_ref[1:2, :] += _sum_rows(dx2 * y)
        loss_t = 0.5 * jnp.sum(_sum_rows(err * dout), axis=-1, keepdims=True)
        stats_ref[2:3, :] += jnp.broadcast_to(loss_t, (1, D_MODEL))
        dy = (gate_v * dx2).astype(BF16)
        dmix = _dot_nt(dy, woo_v)
        dwoo_ref[...] += _dot_tn(mix, dy)
        doc_ref[...] = dmix * sg
        dgc_ref[...] = (dmix * ocv * dsg).astype(BF16)

    gate_cols = [pl.BlockSpec((ts, 512), lambda i, c=c: (i, c)) for c in (3, 4)]
    return pl.pallas_call(
        body, name="odd_post", grid=(S // ts,),
        in_specs=[_row_spec(ts, D_MODEL)] + gate_cols + [_row_spec(ts, D_MODEL), _full_spec((1, D_MODEL)),
                  _full_spec((D_MODEL, D_MODEL)), _full_spec((1, D_MODEL)), _row_spec(ts, D_MODEL)],
        out_specs=[_row_spec(ts, D_MODEL), _row_spec(ts, D_MODEL), _row_spec(ts, D_MODEL),
                   _full_spec((D_MODEL, D_MODEL), single=False), _full_spec((8, D_MODEL), single=False)],
        out_shape=[_sds((S, D_MODEL), F32), _sds((S, D_MODEL), BF16), _sds((S, D_MODEL), F32), _sds((D_MODEL, D_MODEL), F32),
                   _sds((8, D_MODEL), F32)],
        compiler_params=_params(("arbitrary",)),
    )(oc, proj, proj, x1, gate, woo, fw, tgt)


def _odd_pre_bwd(dq, dk, dv, dgc, h, x, dx_res, mod, nw, wio):
    S = x.shape[0]
    ts = min(ROW_TILE, S)

    def body(dq_ref, dk_ref, dv_ref, dgc_ref, h_ref, x_ref, dxr_ref, mod_ref, nw_ref, wio_ref, dx_ref, dw_ref, stats_ref):
        @pl.when(pl.program_id(0) == 0)
        def _():
            dw_ref[...] = jnp.zeros((D_MODEL, ODD_IN), F32)
            stats_ref[...] = jnp.zeros((8, D_MODEL), F32)

        dproj = jnp.concatenate([dq_ref[...], dk_ref[...].astype(BF16), dv_ref[...].astype(BF16), dgc_ref[...]], axis=1)
        dh = _dot_nt(dproj, wio_ref[...])
        dw_ref[...] += _dot_tn(h_ref[...], dproj)
        dx_ref[...] = dxr_ref[...] + _norm_mod_bwd(dh, x_ref[...], mod_ref, nw_ref, stats_ref)

    return pl.pallas_call(
        body, name="odd_pre_bwd", grid=(S // ts,),
        in_specs=[_row_spec(ts, 1024), _row_spec(ts, 256), _row_spec(ts, 256), _row_spec(ts, 1024), _row_spec(ts, D_MODEL),
                  _row_spec(ts, D_MODEL), _row_spec(ts, D_MODEL), _full_spec((3, D_MODEL)), _full_spec((1, D_MODEL)),
                  _full_spec((D_MODEL, ODD_IN))],
        out_specs=[_row_spec(ts, D_MODEL), _full_spec((D_MODEL, ODD_IN), single=False), _full_spec((8, D_MODEL), single=False)],
        out_shape=[_sds((S, D_MODEL), F32), _sds((D_MODEL, ODD_IN), F32), _sds((8, D_MODEL), F32)],
        compiler_params=_params(("arbitrary",)),
    )(dq, dk, dv, dgc, h, x, dx_res, mod, nw, wio)


def _even_post_bwd(dx1, y, oa, olat, proj, gate, wuv, woe):
    S = dx1.shape[0]
    ts = min(ROW_TILE, S)

    def body(dx_ref, y_ref, oa_ref, ol_ref, ga0_ref, ga1_ref, gb0_ref, gb1_ref, gate_ref, wuv_ref, woe_ref,
             doa_ref, dga_ref, dgb_ref, dol_ref, dwoe_ref, dwuv_ref, stats_ref):
        @pl.when(pl.program_id(0) == 0)
        def _():
            dwoe_ref[...] = jnp.zeros((D_MODEL, D_MODEL), F32)
            dwuv_ref[...] = jnp.zeros((1024, 512), F32)
            stats_ref[...] = jnp.zeros((8, D_MODEL), F32)

        dxv = dx_ref[...]
        stats_ref[0:1, :] += _sum_rows(dxv * y_ref[...])
        dy = (gate_ref[...] * dxv).astype(BF16)
        sa, dsa = _silu_and_grad(jnp.concatenate([ga0_ref[...], ga1_ref[...]], axis=1))
        sb, dsb = _silu_and_grad(jnp.concatenate([gb0_ref[...], gb1_ref[...]], axis=1))
        olc = jnp.concatenate([ol_ref[hh] for hh in range(B_HEADS)], axis=1).astype(BF16)
        wuv_v = wuv_ref[...]
        ob = _dot(olc, wuv_v)
        oav = oa_ref[...]
        mix = jnp.concatenate([oav * sa, ob * sb], axis=1).astype(BF16)
        dmix = _dot_nt(dy, woe_ref[...])
        dwoe_ref[...] += _dot_tn(mix, dy)
        dma, dmb = dmix[:, 0:512], dmix[:, 512:1024]
        doa_ref[...] = dma * sa
        dga_ref[...] = (dma * oav * dsa).astype(BF16)
        dgb_ref[...] = (dmb * ob * dsb).astype(BF16)
        dob = (dmb * sb).astype(BF16)
        dol = _dot_nt(dob, wuv_v)
        dwuv_ref[...] += _dot_tn(olc, dob)
        for hh in range(B_HEADS):
            dol_ref[hh] = dol[:, LANES * hh:LANES * (hh + 1)]

    head_spec = pl.BlockSpec((B_HEADS, ts, LANES), lambda i: (0, i, 0))
    return pl.pallas_call(
        body, name="even_post_bwd", grid=(S // ts,),
        in_specs=[_row_spec(ts, D_MODEL), _row_spec(ts, D_MODEL), _row_spec(ts, 512), head_spec] + _even_gate_specs(ts) +
                 [_full_spec((1, D_MODEL)), _full_spec((1024, 512)), _full_spec((1024, D_MODEL))],
        out_specs=[_row_spec(ts, 512), _row_spec(ts, 512), _row_spec(ts, 512), head_spec,
                   _full_spec((D_MODEL, D_MODEL), single=False), _full_spec((1024, 512), single=False),
                   _full_spec((8, D_MODEL), single=False)],
        out_shape=[_sds((S, 512), F32), _sds((S, 512), BF16), _sds((S, 512), BF16), _sds((B_HEADS, S, LANES), F32),
                   _sds((D_MODEL, D_MODEL), F32), _sds((1024, 512), F32), _sds((8, D_MODEL), F32)],
        compiler_params=_params(("arbitrary",)),
    )(dx1, y, oa, olat, proj, proj, proj, proj, gate, wuv, woe)


def _even_pre_bwd(x, h, proj, dqa, dka, dva, dga, dgb, dqcat, dkcat, dx_res, mod, nw, wie, qn, kn, seg, ca, sa, ct, st,
                  qln, kvln, wuq, wuk):
    S = x.shape[0]
    ts = min(ROW_TILE, S)
    nsteps = S // ts

    def body(x_ref, h_ref, proj_ref, dqa_ref, dka_ref, dva_ref, dga_ref, dgb_ref, dqc_ref, dkc_ref, dxr_ref, mod_ref, nw_ref,
             wie_ref, qn_ref, kn_ref, seg_ref, ca_ref, sa_ref, ct_ref, st_ref, qln_ref, kvln_ref, wuq_ref, wuk_ref,
             dx_ref, dwie_out, dwuq_out, dwuk_out, stats_ref, nstats_ref, dwie_ref, dwuq_ref, dwuk_ref):
        @pl.when(pl.program_id(0) == 0)
        def _():
            dwie_ref[...] = jnp.zeros((D_MODEL, EVEN_P), F32)
            dwuq_ref[...] = jnp.zeros((B_Q_LORA, 1536), F32)
            dwuk_ref[...] = jnp.zeros((512, 1024), F32)
            stats_ref[...] = jnp.zeros((8, D_MODEL), F32)
            nstats_ref[...] = jnp.zeros((8, 256), F32)

        lane = _lane_iota()
        ca_v, sa_v, ct_v, st_v = ca_ref[...], sa_ref[...], ct_ref[...], st_ref[...]
        seg_v = seg_ref[...]

        def head_norm_bwd(xc, dy, w):
            r = lax.rsqrt(_seg_mean(xc * xc, seg_v) + EPS)
            g = dy * w
            dxc = r * g - xc * (r * r * r) * _seg_mean(xc * g, seg_v)
            return dxc, _sum_rows(dy * (xc * r))

        pieces = []
        dqn = jnp.zeros((1, LANES), F32)
        for cb in range(4):
            sl = slice(LANES * cb, LANES * (cb + 1))
            dy = _rot_bwd(dqa_ref[:, sl] * 0.125, ca_v, sa_v, lane)
            dxc, dw = head_norm_bwd(proj_ref[:, sl], dy, qn_ref[...])
            pieces.append(dxc)
            dqn = dqn + dw
        dxc, dkn = head_norm_bwd(proj_ref[:, 512:640], _rot_bwd(dka_ref[...], ca_v, sa_v, lane), kn_ref[...])
        pieces += [dxc, dva_ref[...], dga_ref[...]]
        nstats_ref[0:1, 0:LANES] += dqn + pltpu.roll(dqn, HEAD_DIM, 1)
        nstats_ref[1:2, 0:LANES] += dkn + pltpu.roll(dkn, HEAD_DIM, 1)

        cq = proj_ref[:, 1280:1536]
        rq = _rms(cq)
        cqn_f = cq * rq
        qln_v = qln_ref[...]
        cqn = (cqn_f * qln_v).astype(BF16)
        wuq_v, wuk_v = wuq_ref[...], wuk_ref[...]
        qnope = _dot(cqn, wuq_v[:, 0:512]).astype(BF16)
        dqlat = jnp.concatenate([dqc_ref[hh, :, 0:LANES] for hh in range(B_HEADS)], axis=1).astype(BF16)
        dqnope = _dot_nt(dqlat, wuk_v)
        dwuk_ref[...] += _dot_tn(qnope, dqlat)
        dqr = [_rot_bwd(dqc_ref[hh, :, LANES:2 * LANES], ct_v, st_v, lane) for hh in range(B_HEADS)]
        dqb = jnp.concatenate([dqnope] + dqr, axis=1).astype(BF16)
        dcqn = _dot_nt(dqb, wuq_v)
        dwuq_ref[...] += _dot_tn(cqn, dqb)
        nstats_ref[2:3, :] += _sum_rows(dcqn * cqn_f)
        dcq = _rms_bwd(cq, rq, dcqn * qln_v)
        ckv = proj_ref[:, 1536:1664]
        rk = _rms(ckv)
        dckvn = dkc_ref[:, 0:LANES]
        nstats_ref[3:4, 0:LANES] += _sum_rows(dckvn * (ckv * rk))
        dckv = _rms_bwd(ckv, rk, dckvn * kvln_ref[...])
        dkr = _rot_bwd(dkc_ref[:, LANES:2 * LANES], ct_v, st_v, lane)
        pieces += [dcq, dckv, dkr, dgb_ref[...]]
        dproj = jnp.concatenate([piece.astype(BF16) for piece in pieces], axis=1)
        dh = _dot_nt(dproj, wie_ref[...])
        dwie_ref[...] += _dot_tn(h_ref[...], dproj)
        dx_ref[...] = dxr_ref[...] + _norm_mod_bwd(dh, x_ref[...], mod_ref, nw_ref, stats_ref)

        @pl.when(pl.program_id(0) == nsteps - 1)
        def _():
            pltpu.sync_copy(dwie_ref, dwie_out)
            pltpu.sync_copy(dwuq_ref, dwuq_out)
            pltpu.sync_copy(dwuk_ref, dwuk_out)

    return pl.pallas_call(
        body, name="even_pre_bwd", grid=(nsteps,),
        in_specs=[_row_spec(ts, D_MODEL), _row_spec(ts, D_MODEL), _row_spec(ts, EVEN_P), _row_spec(ts, 512), _row_spec(ts, LANES),
                  _row_spec(ts, LANES), _row_spec(ts, 512), _row_spec(ts, 512),
                  pl.BlockSpec((B_HEADS, ts, 2 * LANES), lambda i: (0, i, 0)), _row_spec(ts, 2 * LANES), _row_spec(ts, D_MODEL),
                  _full_spec((3, D_MODEL)), _full_spec((1, D_MODEL)), _full_spec((D_MODEL, EVEN_P)),
                  _full_spec((1, LANES)), _full_spec((1, LANES)), _full_spec((LANES, LANES)),
                  _row_spec(ts, LANES), _row_spec(ts, LANES), _row_spec(ts, LANES), _row_spec(ts, LANES),
                  _full_spec((1, B_Q_LORA)), _full_spec((1, B_KV_LORA)), _full_spec((B_Q_LORA, 1536)), _full_spec((512, 1024))],
        out_specs=[_row_spec(ts, D_MODEL), _ANY, _ANY, _ANY, _full_spec((8, D_MODEL), single=False), _full_spec((8, 256), single=False)],
        out_shape=[_sds((S, D_MODEL), F32), _sds((D_MODEL, EVEN_P), F32), _sds((B_Q_LORA, 1536), F32), _sds((512, 1024), F32),
                   _sds((8, D_MODEL), F32), _sds((8, 256), F32)],
        scratch_shapes=[pltpu.VMEM((D_MODEL, EVEN_P), F32), pltpu.VMEM((B_Q_LORA, 1536), F32), pltpu.VMEM((512, 1024), F32)],
        compiler_params=_params(("arbitrary",)),
    )(x, h, proj, dqa, dka, dva, dga, dgb, dqcat, dkcat, dx_res, mod, nw, wie, qn, kn, seg, ca, sa, ct, st, qln, kvln, wuq, wuk)


def _ada_fwd(c_all, w, b):
    n = w.shape[2]

    def body(c_ref, w_ref, b_ref, o_ref):
        cv = c_ref[...]
        o_ref[0] = _dot_f32(cv * _sigmoid(cv), w_ref[0]) + b_ref[0]

    return pl.pallas_call(
        body, name="ada_fwd", grid=(2,),
        in_specs=[pl.BlockSpec((N_DEV, D_MODEL), lambda l: (0, 0)), pl.BlockSpec((1, D_MODEL, n), lambda l: (l, 0, 0)),
                  pl.BlockSpec((1, 1, n), lambda l: (l, 0, 0))],
        out_specs=pl.BlockSpec((1, N_DEV, n), lambda l: (l, 0, 0)),
        out_shape=_sds((2, N_DEV, n), F32),
        compiler_params=_params(("arbitrary",)),
    )(c_all, w, b)


def _ada_bwd(c_all_t, dmod):
    n = dmod.shape[2]

    def body(c_ref, d_ref, o_ref):
        cv = c_ref[...]
        act = cv * _sigmoid(cv)
        dv = d_ref[0]
        acc = act[:, 0:1] * dv[0:1, :]
        for bb in range(1, N_DEV):
            acc = acc + act[:, bb:bb + 1] * dv[bb:bb + 1, :]
        o_ref[0] = acc

    return pl.pallas_call(
        body, name="ada_bwd", grid=(2,),
        in_specs=[pl.BlockSpec((D_MODEL, N_DEV), lambda l: (0, 0)), pl.BlockSpec((1, N_DEV, n), lambda l: (l, 0, 0))],
        out_specs=pl.BlockSpec((1, D_MODEL, n), lambda l: (l, 0, 0)),
        out_shape=_sds((2, D_MODEL, n), F32),
        compiler_params=_params(("arbitrary",)),
    )(c_all_t, dmod)


ADAM_ROW_TILE = 256


def _adam_update(g, w, m, v):
    m_new = ADAM_B1 * m + (1.0 - ADAM_B1) * g
    v_new = ADAM_B2 * v + (1.0 - ADAM_B2) * jnp.square(g)
    m_hat = m_new / (1.0 - ADAM_B1 ** ADAM_STEP)
    v_hat = v_new / (1.0 - ADAM_B2 ** ADAM_STEP)
    return -ADAM_LR * (m_hat / (jnp.sqrt(v_hat) + ADAM_EPS) + ADAM_WD * w), m_new, v_new


SMALL_ROWS = dict(dmod=(0, D_MODEL), norm_w=(6, D_MODEL), final_norm=(8, D_MODEL), a_q_norm=(9, HEAD_DIM), a_k_norm=(10, HEAD_DIM),
                  b_q_lora_norm=(11, B_Q_LORA), b_kv_lora_norm=(12, B_KV_LORA), c_sink=(13, C_HEADS))
SMALL_WEIGHTS = ("ada_b", "norm_w", "final_norm", "a_q_norm", "a_k_norm", "b_q_lora_norm", "b_kv_lora_norm", "c_sink")


def _pack_small(res):
    def padded(v):
        return jnp.concatenate([v, jnp.zeros((v.shape[0], D_MODEL - v.shape[1]), F32)], axis=1)

    rows = [res["dmod"].reshape(6, D_MODEL), res["norm_w"], res["final_norm"].reshape(1, D_MODEL)]
    rows += [padded(res[k]) for k in ("a_q_norm", "a_k_norm", "b_q_lora_norm", "b_kv_lora_norm", "c_sink")]
    return jnp.concatenate(rows + [jnp.zeros((2, D_MODEL), F32)], axis=0)


def _adam_small(parts, ws, ms, vs):
    nw = len(SMALL_WEIGHTS)

    def body(*refs):
        p_ref = refs[0]
        w_refs, m_refs, v_refs = refs[1:1 + nw], refs[1 + nw:1 + 2 * nw], refs[1 + 2 * nw:1 + 3 * nw]
        outs = refs[1 + 3 * nw:]
        g_all = p_ref[0]
        for k in range(1, N_DEV):
            g_all = g_all + p_ref[k]
        for idx, name in enumerate(SMALL_WEIGHTS):
            if name == "ada_b":
                g = jnp.concatenate([jnp.concatenate([g_all[3 * l + t:3 * l + t + 1] for t in range(3)], axis=1) for l in range(2)],
                                    axis=0)
            else:
                row, width = SMALL_ROWS[name]
                g = g_all[row:row + w_refs[idx].shape[0], 0:width]
            d, m_new, v_new = _adam_update(g, w_refs[idx][...], m_refs[idx][...], v_refs[idx][...])
            outs[4 * idx][...], outs[4 * idx + 1][...], outs[4 * idx + 2][...], outs[4 * idx + 3][...] = g, d, m_new, v_new

    out_shape = []
    for w in ws:
        out_shape += [_sds(w.shape, F32)] * 4
    return pl.pallas_call(body, name="adam_small", out_shape=out_shape,
                          compiler_params=pltpu.CompilerParams(vmem_limit_bytes=VMEM_LIMIT))(parts, *ws, *ms, *vs)


def _adam(parts, w, m, v, name):
    P, R, C = parts.shape
    tr = R if R <= ADAM_ROW_TILE else ADAM_ROW_TILE
    assert R % tr == 0

    def body(p_ref, w_ref, m_ref, v_ref, g_ref, d_ref, nm_ref, nv_ref):
        g = p_ref[0].astype(F32)
        for k in range(1, P):
            g = g + p_ref[k].astype(F32)
        g_ref[...] = g
        d_ref[...], nm_ref[...], nv_ref[...] = _adam_update(g, w_ref[...], m_ref[...], v_ref[...])

    spec = pl.BlockSpec((tr, C), lambda i: (i, 0))
    return pl.pallas_call(
        body, name=name, grid=(R // tr,),
        in_specs=[pl.BlockSpec((P, tr, C), lambda i: (0, i, 0)), spec, spec, spec],
        out_specs=[spec, spec, spec, spec], out_shape=[_sds((R, C), F32)] * 4,
        compiler_params=_params(("arbitrary",)),
    )(parts, w, m, v)


_ANY = pl.BlockSpec(memory_space=pl.ANY)
CHIP_FLIPS = ((1, 0), (0, 1), (1, 1))
DEV_FLIPS = tuple((dx, dy, dc) for dx in (0, 1) for dy in (0, 1) for dc in (0, 1) if dx + dy + dc)


def _flip(a, d):
    return a if d == 0 else 1 - a


def _my_place():
    return lax.axis_index("x"), lax.axis_index("y"), lax.axis_index("c")


def _gather_dev8(arrs, name):
    n = len(arrs)

    def body(*refs):
        ins, outs = refs[:n], refs[n:2 * n]
        send_sems, recv_sems, loc_sems = refs[2 * n:]
        x, y, c = _my_place()
        me = 4 * x + 2 * y + c
        copies = []
        for a in range(n):
            loc = pltpu.make_async_copy(ins[a], outs[a].at[me], loc_sems.at[a])
            loc.start()
            copies.append(loc)
            for k, (dx, dy, dc) in enumerate(DEV_FLIPS):
                cp = pltpu.make_async_remote_copy(
                    src_ref=ins[a], dst_ref=outs[a].at[me], send_sem=send_sems.at[a, k], recv_sem=recv_sems.at[a, k],
                    device_id=(_flip(x, dx), _flip(y, dy), _flip(c, dc)), device_id_type=MESH_ID)
                cp.start()
                copies.append(cp)
        for cp in copies:
            cp.wait()

    return pl.pallas_call(
        body, name=name, in_specs=[_ANY] * n, out_specs=[_ANY] * n,
        out_shape=[_sds((N_DEV,) + a.shape, a.dtype) for a in arrs],
        scratch_shapes=[pltpu.SemaphoreType.DMA((n, 7)), pltpu.SemaphoreType.DMA((n, 7)), pltpu.SemaphoreType.DMA((n,))],
    )(*arrs)


class _Exchange:
    def __init__(self, arrs, out_shapes, n_sems, phases):
        self.arrs, self.out_shapes, self.n_sems, self._phases = list(arrs), list(out_shapes), n_sems, phases

    @property
    def n(self):
        return len(self.arrs)

    def sem_shapes(self):
        return [pltpu.SemaphoreType.DMA((self.n, self.n_sems)), pltpu.SemaphoreType.DMA((self.n, self.n_sems)),
                pltpu.SemaphoreType.DMA((self.n,))]

    def phases(self, ins, outs, sems):
        return self._phases(ins, outs, *sems)

    def run(self, name):
        n = self.n

        def body(*refs):
            start, mid, end = self.phases(refs[:n], refs[n:2 * n], refs[2 * n:])
            start()
            mid()
            end()

        return pl.pallas_call(body, name=name, in_specs=[_ANY] * n, out_specs=[_ANY] * n, out_shape=self.out_shapes,
                              scratch_shapes=self.sem_shapes())(*self.arrs)


def _gather_halves_phases(ins, outs, send_sems, recv_sems, loc_sems):
    n = len(ins)
    x, y, c = _my_place()
    chip = 2 * x + y
    sibling = (x, y, 1 - c)
    peers = [(_flip(x, dx), _flip(y, dy)) for dx, dy in CHIP_FLIPS]

    def remote(src, p, half, a, k, to):
        return pltpu.make_async_remote_copy(src_ref=src, dst_ref=outs[a].at[p, half], send_sem=send_sems.at[a, k],
                                            recv_sem=recv_sems.at[a, k], device_id=to, device_id_type=MESH_ID)

    def local(a):
        return pltpu.make_async_copy(ins[a], outs[a].at[chip], loc_sems.at[a])

    def first(a, k):
        return remote(ins[a].at[c], chip, c, a, k, (*peers[k], c))

    def passed(a, k):
        p = 2 * peers[k][0] + peers[k][1]
        return remote(outs[a].at[p, c], p, c, a, 3 + k, sibling)

    def start():
        for a in range(n):
            local(a).start()
            for k in range(3):
                first(a, k).start()

    def mid():
        for a in range(n):
            for k in range(3):
                p = 2 * peers[k][0] + peers[k][1]
                remote(outs[a].at[p, c], p, c, a, k, sibling).wait_recv()
                passed(a, k).start()

    def end():
        for a in range(n):
            for k in range(3):
                p = 2 * peers[k][0] + peers[k][1]
                remote(outs[a].at[p, 1 - c], p, 1 - c, a, 3 + k, sibling).wait_recv()
        for a in range(n):
            for k in range(3):
                first(a, k).wait_send()
                passed(a, k).wait_send()
            local(a).wait()

    return start, mid, end


def _gather_chip4_halves(arrs):
    return _Exchange(arrs, [_sds((N_CHIPS,) + a.shape, a.dtype) for a in arrs], 6, _gather_halves_phases)


def _reduce_phases(ins, outs, send_sems, recv_sems, loc_sems):
    n = len(ins)
    x, y, c = _my_place()
    chip = 2 * x + y
    sibling = (x, y, 1 - c)
    peers = [(_flip(x, dx), _flip(y, dy)) for dx, dy in CHIP_FLIPS]

    def remote(src, slot, a, k, to):
        return pltpu.make_async_remote_copy(src_ref=src, dst_ref=outs[a].at[slot], send_sem=send_sems.at[a, k],
                                            recv_sem=recv_sems.at[a, k], device_id=to, device_id_type=MESH_ID)

    def local(a):
        return pltpu.make_async_copy(ins[a].at[chip], outs[a].at[2 * chip + c], loc_sems.at[a])

    def own(a):
        return remote(ins[a].at[chip], 2 * chip + c, a, 0, sibling)

    def first(a, k):
        return remote(ins[a].at[2 * peers[k][0] + peers[k][1]], 2 * chip + c, a, 1 + k, (*peers[k], c))

    def passed(a, k):
        slot = 2 * (2 * peers[k][0] + peers[k][1]) + c
        return remote(outs[a].at[slot], slot, a, 4 + k, sibling)

    def start():
        for a in range(n):
            local(a).start()
            own(a).start()
            for k in range(3):
                first(a, k).start()

    def mid():
        for a in range(n):
            for k in range(3):
                slot = 2 * (2 * peers[k][0] + peers[k][1]) + c
                remote(outs[a].at[slot], slot, a, 1 + k, sibling).wait_recv()
                passed(a, k).start()

    def end():
        for a in range(n):
            remote(outs[a].at[2 * chip + 1 - c], 2 * chip + 1 - c, a, 0, sibling).wait_recv()
            for k in range(3):
                slot = 2 * (2 * peers[k][0] + peers[k][1]) + 1 - c
                remote(outs[a].at[slot], slot, a, 4 + k, sibling).wait_recv()
        for a in range(n):
            own(a).wait_send()
            for k in range(3):
                first(a, k).wait_send()
                passed(a, k).wait_send()
            local(a).wait()

    return start, mid, end


def _reduce_exchange(arrs):
    return _Exchange(arrs, [_sds((N_DEV,) + a.shape[1:], a.dtype) for a in arrs], 7, _reduce_phases)


def _pair_order(nheads, nkv):
    group = nheads // nkv
    order = []
    for m in range(nkv // 2):
        for i in range(group):
            order += [2 * m * group + i, (2 * m + 1) * group + i]
    return order


A_ORDER = _pair_order(A_HEADS, A_KV_HEADS)
C_ORDER = _pair_order(C_HEADS, C_KV_HEADS)
A_INV = [int(k) for k in np.argsort(A_ORDER)]
C_INV = [int(k) for k in np.argsort(C_ORDER)]


def _perm_heads(w, order, axis):
    return jnp.concatenate([lax.slice_in_dim(w, HEAD_DIM * h, HEAD_DIM * (h + 1), axis=axis) for h in order], axis=axis)


def _even_in_layout(w):
    return jnp.concatenate([_perm_heads(w[:, 0:512], A_ORDER, 1), w[:, 512:768], _perm_heads(w[:, 768:1280], A_ORDER, 1),
                            w[:, 1280:1696], jnp.zeros((w.shape[0], 96), w.dtype), w[:, 1696:2208]], axis=1)


def _even_in_unlayout(g):
    return jnp.concatenate([_perm_heads(g[:, 0:512], A_INV, 1), g[:, 512:768], _perm_heads(g[:, 768:1280], A_INV, 1),
                            g[:, 1280:1696], g[:, 1792:2304]], axis=1)


def _even_out_layout(w):
    return jnp.concatenate([_perm_heads(w[0:512], A_ORDER, 0), w[512:1024]], axis=0)


def _even_out_unlayout(g):
    return jnp.concatenate([_perm_heads(g[0:512], A_INV, 0), g[512:1024]], axis=0)


def _odd_in_layout(w):
    return jnp.concatenate([_perm_heads(w[:, 0:1024], C_ORDER, 1), w[:, 1024:1536], _perm_heads(w[:, 1536:2560], C_ORDER, 1)],
                           axis=1)


def _odd_in_unlayout(g):
    return jnp.concatenate([_perm_heads(g[:, 0:1024], C_INV, 1), g[:, 1024:1536], _perm_heads(g[:, 1536:2560], C_INV, 1)],
                           axis=1)


def _uq_layout(w):
    per = B_NOPE + B_ROPE
    pad = jnp.zeros((w.shape[0], LANES - B_ROPE), w.dtype)
    nope = [w[:, per * h:per * h + B_NOPE] for h in range(B_HEADS)]
    rope = [jnp.concatenate([w[:, per * h + B_NOPE:per * (h + 1)], pad], axis=1) for h in range(B_HEADS)]
    return jnp.concatenate(nope + rope, axis=1)


def _uq_unlayout(g):
    parts = []
    for h in range(B_HEADS):
        parts += [g[:, B_NOPE * h:B_NOPE * (h + 1)], g[:, 512 + LANES * h:512 + LANES * h + B_ROPE]]
    return jnp.concatenate(parts, axis=1)


def _block_diag(blocks):
    rows = []
    for h, blk in enumerate(blocks):
        r, cdim = blk.shape
        n = len(blocks)
        rows.append(jnp.concatenate([jnp.zeros((r, cdim * h), blk.dtype), blk, jnp.zeros((r, cdim * (n - 1 - h)), blk.dtype)],
                                    axis=1))
    return jnp.concatenate(rows, axis=0)


def _uk_layout(w):
    return _block_diag([w[:, h, :].T for h in range(B_HEADS)])


def _uk_unlayout(g):
    return jnp.stack([g[B_NOPE * h:B_NOPE * (h + 1), LANES * h:LANES * (h + 1)].T for h in range(B_HEADS)], axis=1)


def _uv_layout(w):
    return _block_diag([w[:, h, :] for h in range(B_HEADS)])


def _uv_unlayout(g):
    return jnp.stack([g[LANES * h:LANES * (h + 1), B_V * h:B_V * (h + 1)] for h in range(B_HEADS)], axis=1)


def _rope_tables(S):
    inv = ROPE_THETA ** (-jnp.arange(0, 32, 2, dtype=F32) / 32)
    tok = jnp.arange(S)

    def tab(pos):
        ang = pos.astype(F32)[:, None] * inv[None, :]
        cos, sin = jnp.cos(ang), jnp.sin(ang)
        return jnp.concatenate([cos, cos], axis=1), jnp.concatenate([-sin, sin], axis=1)

    cr, sr = tab(tok // GRID_W)
    cc, sc = tab(tok % GRID_W)
    ct, st = tab(tok)
    return (jnp.tile(jnp.concatenate([cr, cc], axis=1), (1, 2)), jnp.tile(jnp.concatenate([sr, sc], axis=1), (1, 2)),
            jnp.tile(ct, (1, 4)), jnp.tile(st, (1, 4)))


A_TQ, A_TK, A_SUB = 256, 4096, 1024
B_TQ, B_TK, B_SUB = 128, 4096, 1024
B_BWD_TK, B_BWD_SUB = 2048, 512
C_T = 256
C_BLOCKS_PER_STEP = 4


def _local_step(x0, tgt, mod, norm_w, wie, wuq, wuk, wuv, late_shards, a_q_norm, a_k_norm, q_lora_norm, kv_lora_norm,
                c_sink, final_norm):
    S = x0.shape[0]
    mod3 = mod.reshape(2, 3, D_MODEL)
    ca, sa, ct, st = _rope_tables(S)
    lane_seg = np.arange(LANES) // HEAD_DIM
    seg = jnp.asarray((lane_seg[:, None] == lane_seg[None, :]).astype(np.float32)).astype(BF16)
    qn = jnp.tile(a_q_norm.reshape(1, HEAD_DIM), (1, 2))
    kn = jnp.tile(a_k_norm.reshape(1, HEAD_DIM), (1, 2))
    qln, kvln = q_lora_norm.reshape(1, B_Q_LORA), kv_lora_norm.reshape(1, B_KV_LORA)
    nw0, nw1 = norm_w[0:1], norm_w[1:2]
    gate0, gate1 = mod3[0, 2:3], mod3[1, 2:3]
    a_tq, a_tk, b_tq, b_tk, bb_tk, c_t = min(A_TQ, S), min(A_TK, S), min(B_TQ, S), min(B_TK, S), min(B_BWD_TK, S), min(C_T, S)
    a_sub, b_sub, bb_sub = min(A_SUB, a_tk), min(B_SUB, b_tk), min(B_BWD_SUB, bb_tk)

    h0, proj_e, qa, ka, va, qcat, kcat, ka_t, va_t, kcat_t = _even_pre_fwd(x0, mod3[0], nw0, wie, qn, kn, seg, ca, sa, ct, st,
                                                                           qln, kvln, wuq, wuk)
    oa, lse_a, woe_g, wio_g, woo_g = _pp_fwd(qa, ka, va_t, kdiv=4, tq=a_tq, tk=a_tk, sub=a_sub, name="attn_a_fwd",
                                             side=_gather_chip4_halves(late_shards))
    woe = _even_out_layout(woe_g.reshape(D_MODEL, D_MODEL))
    wio = _odd_in_layout(_chips_to_cols(wio_g.reshape(N_CHIPS, D_MODEL, ODD_IN // N_CHIPS)))
    woo = _perm_heads(woo_g.reshape(D_MODEL, D_MODEL), C_ORDER, 0)
    olat, lse_b = _mla_fwd(qcat, kcat, kcat_t, tq=b_tq, tk=b_tk, sub=b_sub)
    y0, x1 = _even_post_fwd(oa, olat, proj_e, x0, gate0, wuv, woe)
    h1, proj_o, qc, kc, vc, kc_t, vc_t = _odd_pre_fwd(x1, mod3[1], nw1, wio)
    slopes = 2.0 ** (-8.0 * jnp.arange(1, C_HEADS + 1, dtype=F32) / C_HEADS)
    c_order = np.asarray(C_ORDER)
    slope_rows = jnp.repeat(slopes[c_order].reshape(C_HEADS // 2, 2), c_t, axis=1)[:, None, :]
    sink_rows = jnp.repeat(c_sink.reshape(C_HEADS)[c_order].reshape(C_HEADS // 2, 2), c_t, axis=1)[:, None, :]
    win_dist = _win_dist_table(S, c_t)
    oc, lse_c = _win_fwd(qc, kc, vc_t, win_dist, slope_rows, sink_rows, kdiv=4, tq=c_t, nbs=C_BLOCKS_PER_STEP, name="attn_c_fwd")
    doc, dgc, dx2, dwoo, st_f = _odd_post(oc, proj_o, x1, gate1, woo, final_norm.reshape(1, D_MODEL), tgt)
    dqc, dkc, dvc, dsink_raw = _win_bwd(qc, kc, kc_t, vc, oc, doc, lse_c, win_dist, slope_rows, sink_rows, kdiv=4, tq=c_t,
                                        nbs=C_BLOCKS_PER_STEP, name="attn_c_bwd")
    dx1, dwio, st_1 = _odd_pre_bwd(dqc, dkc, dvc, dgc, h1, x1, dx2, mod3[1], nw1, wio)
    doa, dga, dgb, dolat, dwoe, dwuv, st_e = _even_post_bwd(dx1, y0, oa, olat, proj_e, gate0, wuv, woe)
    late_grads = _reduce_exchange(
        [_even_out_unlayout(dwoe.astype(BF16)).reshape(N_CHIPS, D_MODEL // N_CHIPS, D_MODEL),
         _cols_to_chips(_odd_in_unlayout(dwio.astype(BF16))),
         _perm_heads(dwoo.astype(BF16), C_INV, 0).reshape(N_CHIPS, D_MODEL // N_CHIPS, D_MODEL)])
    dqa, dka, dva, p_woe, p_wio, p_woo = _pp_bwd(qa, ka, ka_t, va, oa, doa, lse_a, kdiv=4, tq=a_tq, tk=a_tk, sub=a_sub,
                                                 name="attn_a_bwd", side=late_grads)
    dqcat, dkcat = _mla_bwd(qcat, kcat, kcat_t, olat, dolat, lse_b, tq=b_tq, tk=bb_tk, sub=bb_sub)
    dx0, dwie, dwuq, dwuk, st_0, nst = _even_pre_bwd(x0, h0, proj_e, dqa, dka, dva, dga, dgb, dqcat, dkcat, dx1, mod3[0], nw0,
                                                     wie, qn, kn, seg, ca, sa, ct, st, qln, kvln, wuq, wuk)
    dsink_pairs = jnp.stack([dsink_raw[:, 0, 0], dsink_raw[:, 1, 0]], axis=1).reshape(C_HEADS)
    return dict(
        loss=st_f[2, 0], dx=dx0,
        dmod=jnp.stack([jnp.concatenate([st_0[0], st_0[1], st_e[0]]), jnp.concatenate([st_1[0], st_1[1], st_f[1]])]),
        norm_w=jnp.stack([st_0[2], st_1[2]]), final_norm=st_f[0],
        a_q_norm=nst[0:1, 0:HEAD_DIM], a_k_norm=nst[1:2, 0:HEAD_DIM], b_q_lora_norm=nst[2:3, :], b_kv_lora_norm=nst[3:4, 0:LANES],
        c_sink=dsink_pairs[np.asarray(C_INV)].reshape(1, C_HEADS),
        even_w_in=dwie, b_w_uq=dwuq, b_w_uk=dwuk, b_w_uv=dwuv, even_w_out=p_woe, odd_w_in=p_wio, odd_w_out=p_woo)


WEIGHT_NAMES = ("norm_w", "ada_w", "ada_b", "even_w_in", "a_q_norm", "a_k_norm", "b_q_lora_norm", "b_kv_lora_norm", "b_w_uq",
                "b_w_uk", "b_w_uv", "even_w_out", "odd_w_in", "c_sink", "odd_w_out", "final_norm")


def _cols_to_chips(g):
    r, n4 = g.shape
    return jnp.transpose(g.reshape(r, N_CHIPS, n4 // N_CHIPS), (1, 0, 2))


def _chips_to_cols(g):
    p, r, n = g.shape
    return jnp.transpose(g, (1, 0, 2)).reshape(r, p * n)


def kernel(x, c, norm_w, ada_w, ada_b, even_w_in, a_q_norm, a_k_norm, b_q_lora_norm, b_kv_lora_norm, b_w_uq, b_w_uk, b_w_uv, even_w_out, odd_w_in, c_sink, odd_w_out, final_norm, loss_target, m_norm_w, m_ada_w, m_ada_b, m_even_w_in, m_a_q_norm, m_a_k_norm, m_b_q_lora_norm, m_b_kv_lora_norm, m_b_w_uq, m_b_w_uk, m_b_w_uv, m_even_w_out, m_odd_w_in, m_c_sink, m_odd_w_out, m_final_norm, v_norm_w, v_ada_w, v_ada_b, v_even_w_in, v_a_q_norm, v_a_k_norm, v_b_q_lora_norm, v_b_kv_lora_norm, v_b_w_uq, v_b_w_uk, v_b_w_uv, v_even_w_out, v_odd_w_in, v_c_sink, v_odd_w_out, v_final_norm):
    given = dict(locals())
    xi, yi, ci = _my_place()
    chip = 2 * xi + yi
    dev = 2 * chip + ci
    n_ada = ada_w.shape[2]

    (c_all,) = _gather_dev8([c], "gather_c")
    c_all = c_all.reshape(N_DEV, D_MODEL)
    bias = lax.dynamic_slice_in_dim(ada_b, chip * n_ada, n_ada, axis=1).reshape(2, 1, n_ada)
    mod_cols = _ada_fwd(c_all, ada_w, bias)
    def halves(w):
        return w.astype(BF16).reshape((2, w.shape[0] // 2) + w.shape[1:])

    mod_all, wie_g, wuq_g = _gather_chip4_halves([mod_cols, halves(even_w_in[0]), halves(b_w_uq[0])]).run("gather_weights")
    wie_g = wie_g.reshape(N_CHIPS, D_MODEL, EVEN_IN // N_CHIPS)
    wuq_g = wuq_g.reshape(N_CHIPS, B_Q_LORA, -1)
    mod = jnp.transpose(lax.dynamic_index_in_dim(mod_all, dev, axis=2, keepdims=False), (1, 0, 2)).reshape(2, 3 * D_MODEL)

    res = _local_step(
        x[0], loss_target[0], mod, norm_w,
        _even_in_layout(_chips_to_cols(wie_g)), _uq_layout(_chips_to_cols(wuq_g)), _uk_layout(b_w_uk[0].astype(BF16)),
        _uv_layout(b_w_uv[0].astype(BF16)), [halves(even_w_out[0]), halves(odd_w_in[0]), halves(odd_w_out[0])],
        a_q_norm, a_k_norm, b_q_lora_norm, b_kv_lora_norm, c_sink, final_norm)

    shard_parts = dict(zip(
        ("even_w_in", "b_w_uq"),
        _reduce_exchange([_cols_to_chips(_even_in_unlayout(res["even_w_in"].astype(BF16))),
                          _cols_to_chips(_uq_unlayout(res["b_w_uq"].astype(BF16)))]).run("reduce_exchange")))
    shard_parts.update({k: res[k] for k in ("even_w_out", "odd_w_in", "odd_w_out")})

    latent = jnp.stack([_uk_unlayout(res["b_w_uk"]).reshape(B_KV_LORA, 512),
                        _uv_unlayout(res["b_w_uv"]).reshape(B_KV_LORA, 512)]).astype(BF16)
    small_all, latent_all = _gather_dev8([_pack_small(res), latent], "gather_small")
    dmod_all = small_all[:, 0:6, :].reshape(N_DEV, 2, 3 * D_MODEL)
    dmod_cols = jnp.transpose(lax.dynamic_slice_in_dim(dmod_all, chip * n_ada, n_ada, axis=2), (1, 0, 2))
    parts = dict(shard_parts)
    parts["ada_w"] = _ada_bwd(c_all.T, dmod_cols).reshape(1, 2 * D_MODEL, n_ada)
    parts["b_w_uk"], parts["b_w_uv"] = latent_all[:, 0], latent_all[:, 1]

    def as2d(a):
        return a.reshape((-1, a.shape[-1]) if a.ndim > 1 else (1, a.shape[0]))

    results = {}
    small_outs = _adam_small(small_all, *[[as2d(given[pre + k]) for k in SMALL_WEIGHTS] for pre in ("", "m_", "v_")])
    for idx, k in enumerate(SMALL_WEIGHTS):
        results[k] = small_outs[4 * idx:4 * idx + 4]
    for k, p in parts.items():
        shape2 = (p.shape[-2], p.shape[-1])
        results[k] = _adam(p, given[k].reshape(shape2), given["m_" + k].reshape(shape2), given["v_" + k].reshape(shape2),
                           "adam_" + k)
    by_kind = [[results[k][t].reshape(given[k].shape) for k in WEIGHT_NAMES] for t in range(4)]
    loss = lax.psum(res["loss"], ("x", "y", "c"))
    return (loss, res["dx"][None], *by_kind[0], *by_kind[1], *by_kind[2], *by_kind[3])
```

```python
import functools

import numpy as np
import jax
import jax.numpy as jnp
from jax import lax
from jax.experimental import pallas as pl
from jax.experimental.pallas import tpu as pltpu

F32 = jnp.float32
BF16 = jnp.bfloat16
HIGHEST = lax.Precision.HIGHEST
MESH_ID = pl.DeviceIdType.MESH

D_MODEL = 1024
HEAD_DIM = 64
GRID_W = 64
EPS = 1e-6
ROPE_THETA = 10000.0
A_HEADS, A_KV_HEADS = 8, 2
B_HEADS, B_NOPE, B_ROPE, B_V = 8, 64, 32, 64
B_Q_LORA, B_KV_LORA = 256, 128
C_HEADS, C_KV_HEADS = 16, 4
WINDOW = 128
EVEN_IN, ODD_IN = 2208, 2560
EVEN_P = 2304
N_CHIPS, N_DEV = 4, 8
LANES = 128
NEG = -1e30
VMEM_LIMIT = 60 * 1024 * 1024

ADAM_LR, ADAM_B1, ADAM_B2, ADAM_EPS, ADAM_WD, ADAM_STEP = 0.001, 0.9, 0.999, 1e-08, 0.01, 10

ROW_TILE = 256


def _dot(a, b):
    return lax.dot_general(a, b, (((1,), (0,)), ((), ())), preferred_element_type=F32)


def _dot_nt(a, b):
    return lax.dot_general(a, b, (((1,), (1,)), ((), ())), preferred_element_type=F32)


def _dot_tn(a, b):
    return lax.dot_general(a, b, (((0,), (0,)), ((), ())), preferred_element_type=F32)


def _dot_f32(a, b):
    return lax.dot_general(a, b, (((1,), (0,)), ((), ())), precision=HIGHEST, preferred_element_type=F32)


def _sigmoid(x):
    return 1.0 / (1.0 + jnp.exp(-x))


def _silu_and_grad(g):
    s = _sigmoid(g)
    return g * s, s * (1.0 + g * (1.0 - s))


def _lane_iota():
    return lax.broadcasted_iota(jnp.int32, (1, LANES), 1)


def _partner(x, lane):
    return jnp.where((lane % 32) < 16, pltpu.roll(x, LANES - 16, 1), pltpu.roll(x, 16, 1))


def _rot(x, cos, sin_signed, lane):
    return x * cos + _partner(x, lane) * sin_signed


def _rot_bwd(dy, cos, sin_signed, lane):
    return dy * cos + _partner(dy * sin_signed, lane)


def _rms(x):
    return lax.rsqrt(jnp.mean(x * x, axis=-1, keepdims=True) + EPS)


def _rms_bwd(x, r, g):
    return r * g - x * (r * r * r) * jnp.mean(x * g, axis=-1, keepdims=True)


def _seg_mean(v, seg_ones):
    hi = v.astype(BF16)
    lo = (v - hi.astype(F32)).astype(BF16)
    return (_dot(hi, seg_ones) + _dot(lo, seg_ones)) * (1.0 / HEAD_DIM)


def _row_spec(ts, cols):
    return pl.BlockSpec((ts, cols), lambda i: (i, 0))


def _full_spec(shape, single=True):
    nd = len(shape)
    if single:
        return pl.BlockSpec(shape, lambda i: (0,) * nd, pipeline_mode=pl.Buffered(1))
    return pl.BlockSpec(shape, lambda i: (0,) * nd)


def _sds(shape, dtype):
    return jax.ShapeDtypeStruct(shape, dtype)


def _params(sem):
    return pltpu.CompilerParams(dimension_semantics=sem, vmem_limit_bytes=VMEM_LIMIT)


def _even_pre_fwd(x, mod, nw, wie, qn, kn, seg, ca, sa, ct, st, qln, kvln, wuq, wuk):
    S = x.shape[0]
    ts = min(ROW_TILE, S)

    def body(x_ref, mod_ref, nw_ref, wie_ref, qn_ref, kn_ref, seg_ref, ca_ref, sa_ref, ct_ref, st_ref, qln_ref,
             kvln_ref, wuq_ref, wuk_ref, h_ref, proj_ref, qa_ref, ka_ref, va_ref, qcat_ref, kcat_ref, kat_ref, vat_ref, kcatt_ref):
        xv = x_ref[...]
        h = (xv * _rms(xv) * nw_ref[...]) * (1.0 + mod_ref[1:2, :]) + mod_ref[0:1, :]
        hb = h.astype(BF16)
        h_ref[...] = hb
        proj = _dot(hb, wie_ref[...])
        proj_ref[...] = proj
        lane = _lane_iota()
        ca_v, sa_v, ct_v, st_v = ca_ref[...], sa_ref[...], ct_ref[...], st_ref[...]
        seg_v = seg_ref[...]
        for cb in range(4):
            xc = proj[:, LANES * cb:LANES * (cb + 1)]
            r = lax.rsqrt(_seg_mean(xc * xc, seg_v) + EPS)
            y = _rot(xc * r * qn_ref[...], ca_v, sa_v, lane)
            qa_ref[:, LANES * cb:LANES * (cb + 1)] = (y * 0.125).astype(BF16)
        kc = proj[:, 512:640]
        r = lax.rsqrt(_seg_mean(kc * kc, seg_v) + EPS)
        ka_v = _rot(kc * r * kn_ref[...], ca_v, sa_v, lane)
        ka_ref[...] = ka_v.astype(BF16)
        kat_ref[...] = ka_v.T.astype(BF16)
        va_ref[...] = proj[:, 640:768].astype(BF16)
        vat_ref[...] = proj[:, 640:768].T.astype(BF16)
        cq = proj[:, 1280:1536]
        cqn = (cq * _rms(cq) * qln_ref[...]).astype(BF16)
        ckv = proj[:, 1536:1664]
        ckvn = ckv * _rms(ckv) * kvln_ref[...]
        qb = _dot(cqn, wuq_ref[...])
        qlat = _dot(qb[:, 0:512].astype(BF16), wuk_ref[...])
        for hh in range(B_HEADS):
            qcat_ref[hh, :, 0:LANES] = qlat[:, LANES * hh:LANES * (hh + 1)].astype(BF16)
            qr = _rot(qb[:, 512 + LANES * hh:512 + LANES * (hh + 1)], ct_v, st_v, lane)
            qcat_ref[hh, :, LANES:2 * LANES] = qr.astype(BF16)
        kr = _rot(proj[:, 1664:1792], ct_v, st_v, lane)
        kcat_ref[:, 0:LANES] = ckvn.astype(BF16)
        kcat_ref[:, LANES:2 * LANES] = kr.astype(BF16)
        kcatt_ref[0:LANES, :] = ckvn.T.astype(BF16)
        kcatt_ref[LANES:2 * LANES, :] = kr.T.astype(BF16)

    col_spec = lambda rows: pl.BlockSpec((rows, ts), lambda i: (0, i))
    return pl.pallas_call(
        body, name="even_pre_fwd", grid=(S // ts,),
        in_specs=[_row_spec(ts, D_MODEL), _full_spec((3, D_MODEL)), _full_spec((1, D_MODEL)), _full_spec((D_MODEL, EVEN_P)),
                  _full_spec((1, LANES)), _full_spec((1, LANES)), _full_spec((LANES, LANES)),
                  _row_spec(ts, LANES), _row_spec(ts, LANES), _row_spec(ts, LANES), _row_spec(ts, LANES),
                  _full_spec((1, B_Q_LORA)), _full_spec((1, B_KV_LORA)), _full_spec((B_Q_LORA, 1536)), _full_spec((512, 1024))],
        out_specs=[_row_spec(ts, D_MODEL), _row_spec(ts, EVEN_P), _row_spec(ts, 512), _row_spec(ts, LANES), _row_spec(ts, LANES),
                   pl.BlockSpec((B_HEADS, ts, 2 * LANES), lambda i: (0, i, 0)), _row_spec(ts, 2 * LANES),
                   col_spec(LANES), col_spec(LANES), col_spec(2 * LANES)],
        out_shape=[_sds((S, D_MODEL), BF16), _sds((S, EVEN_P), F32), _sds((S, 512), BF16), _sds((S, LANES), BF16),
                   _sds((S, LANES), BF16), _sds((B_HEADS, S, 2 * LANES), BF16), _sds((S, 2 * LANES), BF16),
                   _sds((LANES, S), BF16), _sds((LANES, S), BF16), _sds((2 * LANES, S), BF16)],
        compiler_params=_params(("arbitrary",)),
    )(x, mod, nw, wie, qn, kn, seg, ca, sa, ct, st, qln, kvln, wuq, wuk)


MLA_SCALE = (B_NOPE + B_ROPE) ** -0.5
LOG2E = 1.4426950408889634

def _row_lo():
    return lax.broadcasted_iota(jnp.int32, (LANES, 1), 0) < HEAD_DIM


def _stack_cols(vT, rlo):
    zero = jnp.zeros_like(vT)
    return jnp.concatenate([jnp.where(rlo, vT, zero), jnp.where(rlo, zero, vT)], axis=1)


def _stack_rows(v, lo):
    zero = jnp.zeros_like(v)
    return jnp.concatenate([jnp.where(lo, v, zero), jnp.where(lo, zero, v)], axis=0)


def _pick_halves_T(xT, rlo, t):
    return jnp.where(rlo, xT[:, 0:t], xT[:, t:2 * t]).T


def _side_split(refs, n_in, n_out, n_scratch, side):
    ns = side.n if side is not None else 0
    cuts = np.cumsum([0, n_in, ns, n_out, ns, n_scratch])
    return [refs[a:b] for a, b in zip(cuts[:-1], cuts[1:])] + [refs[cuts[-1]:]]


def _side_hooks(side, side_ins, side_outs, side_sems, step, total):
    if side is None:
        return lambda: None
    start, mid, end = side.phases(side_ins, side_outs, side_sems)
    pl.when(step == 0)(start)
    pl.when(step == total // 2)(mid)
    return lambda: pl.when(step == total - 1)(end)


def _side_specs(side):
    if side is None:
        return [], [], [], [], []
    return list(side.arrs), [_ANY] * side.n, [_ANY] * side.n, list(side.out_shapes), side.sem_shapes()


def _pp_fwd(q, k, vT, *, kdiv, tq, tk, sub, name, side=None):
    S = k.shape[0]; nb = q.shape[1] // LANES; nq = S // tq; nkv = S // tk; nsub = tk // sub

    def body(*refs):
        (q_ref, k_ref, vT_ref), side_ins, (o_ref, lse_ref), side_outs, (qs, m_s, l_s, acc), side_sems = _side_split(refs, 3, 2, 4, side)
        j = pl.program_id(2)
        rlo = _row_lo()
        step = (pl.program_id(0) * nq + pl.program_id(1)) * nkv + j
        side_end = _side_hooks(side, side_ins, side_outs, side_sems, step, nb * nq * nkv)

        @pl.when(j == 0)
        def _():
            qs[...] = _stack_cols(q_ref[...].astype(F32).T, rlo).astype(BF16)
            m_s[...] = jnp.full((1, 2 * tq), NEG, F32)
            l_s[...] = jnp.zeros((1, 2 * tq), F32)
            acc[...] = jnp.zeros((LANES, 2 * tq), F32)

        qsv = qs[...]
        m, l, a = m_s[...], l_s[...], acc[...]
        s_cur = _dot(k_ref[0:sub, :], qsv)
        for t in range(nsub):
            if t + 1 < nsub:
                s_next = _dot(k_ref[sub * (t + 1):sub * (t + 2), :], qsv)
            m_new = jnp.maximum(m, jnp.max(s_cur, axis=0, keepdims=True))
            alpha = jnp.exp(m - m_new)
            p = jnp.exp(s_cur - m_new)
            l = alpha * l + jnp.sum(p, axis=0, keepdims=True)
            a = alpha * a + _dot(vT_ref[:, sub * t:sub * (t + 1)], p.astype(BF16))
            m = m_new
            if t + 1 < nsub:
                s_cur = s_next
        m_s[...], l_s[...], acc[...] = m, l, a

        @pl.when(j == nkv - 1)
        def _():
            l_f = l_s[...]
            o_ref[...] = _pick_halves_T(acc[...] / l_f, rlo, tq)
            lse_ref[0, 0] = m_s[...] + jnp.log(l_f)

        side_end()

    s_args, s_in, s_out, s_shapes, s_sems = _side_specs(side)
    return pl.pallas_call(
        body, name=name, grid=(nb, nq, nkv),
        in_specs=[pl.BlockSpec((tq, LANES), lambda b, i, j: (i, b)), pl.BlockSpec((tk, LANES), lambda b, i, j: (j, b // kdiv)),
                  pl.BlockSpec((LANES, tk), lambda b, i, j: (b // kdiv, j))] + s_in,
        out_specs=[pl.BlockSpec((tq, LANES), lambda b, i, j: (i, b)),
                   pl.BlockSpec((1, 1, 1, 2 * tq), lambda b, i, j: (b, i, 0, 0))] + s_out,
        out_shape=[_sds((S, nb * LANES), F32), _sds((nb, nq, 1, 2 * tq), F32)] + s_shapes,
        scratch_shapes=[pltpu.VMEM((LANES, 2 * tq), BF16), pltpu.VMEM((1, 2 * tq), F32), pltpu.VMEM((1, 2 * tq), F32),
                        pltpu.VMEM((LANES, 2 * tq), F32)] + s_sems,
        compiler_params=_params(("arbitrary",) * 3))(q, k, vT, *s_args)


def _pp_bwd(q, k, kT, v, o, do, lse, *, kdiv, tq, tk, sub, name, side=None):
    S = k.shape[0]; nb = q.shape[1] // LANES; nkb = k.shape[1] // LANES; nq = S // tq; nkv = S // tk; nsub = tk // sub

    def body(*refs):
        ((q_ref, k_ref, kT_ref, v_ref, o_ref, do_ref, lse_ref), side_ins, (dq_ref, dk_ref, dv_ref), side_outs,
         (qsT, qs, dosT, dos, delta_s, dq_acc), side_sems) = _side_split(refs, 7, 3, 6, side)
        b, i, j = pl.program_id(0), pl.program_id(1), pl.program_id(2)
        rlo = _row_lo()
        lo = lax.broadcasted_iota(jnp.int32, (1, LANES), 1) < HEAD_DIM
        side_end = _side_hooks(side, side_ins, side_outs, side_sems, (b * nq + i) * nkv + j, nb * nq * nkv)

        @pl.when((b % kdiv == 0) & (i == 0) & (j == 0))
        def _():
            dk_ref[...] = jnp.zeros((S, LANES), F32)
            dv_ref[...] = jnp.zeros((S, LANES), F32)

        @pl.when(j == 0)
        def _():
            qv = q_ref[...]
            qs[...] = _stack_rows(qv, lo)
            qsT[...] = _stack_cols(qv.astype(F32).T, rlo).astype(BF16)
            dov = do_ref[...]
            dos[...] = _stack_rows(dov.astype(BF16), lo)
            dosT[...] = _stack_cols(dov.T, rlo).astype(BF16)
            prodT = (dov * o_ref[...]).T
            delta_s[...] = jnp.concatenate([jnp.sum(jnp.where(rlo, prodT, 0.0), axis=0, keepdims=True),
                                            jnp.sum(jnp.where(rlo, 0.0, prodT), axis=0, keepdims=True)], axis=1)
            dq_acc[...] = jnp.zeros((LANES, 2 * tq), F32)

        qsTv, dosTv, qsv, dosv = qsT[...], dosT[...], qs[...], dos[...]
        lse_v, delta_v = lse_ref[0, 0], delta_s[...]
        dqa = dq_acc[...]
        s_cur = _dot(k_ref[0:sub, :], qsTv)
        dp_cur = _dot(v_ref[0:sub, :], dosTv)
        for t in range(nsub):
            if t + 1 < nsub:
                s_next = _dot(k_ref[sub * (t + 1):sub * (t + 2), :], qsTv)
                dp_next = _dot(v_ref[sub * (t + 1):sub * (t + 2), :], dosTv)
            p = jnp.exp(s_cur - lse_v)
            ds = (p * (dp_cur - delta_v)).astype(BF16)
            rows = pl.ds(pl.multiple_of(j * tk + sub * t, sub), sub)
            dv_ref[rows, :] += _dot(p.astype(BF16), dosv)
            dk_ref[rows, :] += _dot(ds, qsv)
            dqa = dqa + _dot(kT_ref[:, sub * t:sub * (t + 1)], ds)
            if t + 1 < nsub:
                s_cur, dp_cur = s_next, dp_next
        dq_acc[...] = dqa

        @pl.when(j == nkv - 1)
        def _():
            dq_ref[...] = _pick_halves_T(dq_acc[...], rlo, tq)

        side_end()

    qmap = lambda b, i, j: (i, b)
    kmap = lambda b, i, j: (j, b // kdiv)
    res = lambda b, i, j: (0, b // kdiv)
    s_args, s_in, s_out, s_shapes, s_sems = _side_specs(side)
    return pl.pallas_call(
        body, name=name, grid=(nb, nq, nkv),
        in_specs=[pl.BlockSpec((tq, LANES), qmap), pl.BlockSpec((tk, LANES), kmap), pl.BlockSpec((LANES, tk), lambda b, i, j: (b // kdiv, j)),
                  pl.BlockSpec((tk, LANES), kmap), pl.BlockSpec((tq, LANES), qmap), pl.BlockSpec((tq, LANES), qmap),
                  pl.BlockSpec((1, 1, 1, 2 * tq), lambda b, i, j: (b, i, 0, 0))] + s_in,
        out_specs=[pl.BlockSpec((tq, LANES), qmap), pl.BlockSpec((S, LANES), res), pl.BlockSpec((S, LANES), res)] + s_out,
        out_shape=[_sds((S, nb * LANES), F32), _sds((S, nkb * LANES), F32), _sds((S, nkb * LANES), F32)] + s_shapes,
        scratch_shapes=[pltpu.VMEM((LANES, 2 * tq), BF16), pltpu.VMEM((2 * tq, LANES), BF16), pltpu.VMEM((LANES, 2 * tq), BF16),
                        pltpu.VMEM((2 * tq, LANES), BF16), pltpu.VMEM((1, 2 * tq), F32), pltpu.VMEM((LANES, 2 * tq), F32)] + s_sems,
        compiler_params=_params(("arbitrary",) * 3))(q, k, kT, v, o, do, lse, *s_args)


MLA_C = MLA_SCALE * LOG2E


def _mla_fwd(q, kcat, kcatT, *, tq, tk, sub):
    S = kcat.shape[0]; nq, nkv = S // tq, S // tk; R = B_HEADS * tq; nsub = tk // sub

    def body(q_ref, k_ref, vT_ref, o_ref, lse_ref, qT, m_s, l_s, acc):
        j = pl.program_id(1)

        @pl.when(j == 0)
        def _():
            qT[...] = q_ref[...].reshape(R, 2 * LANES).astype(F32).T.astype(BF16)
            m_s[...] = jnp.full((1, R), NEG, F32)
            l_s[...] = jnp.zeros((1, R), F32)
            acc[...] = jnp.zeros((LANES, R), F32)

        qTv = qT[...]
        m, l, a = m_s[...], l_s[...], acc[...]
        s_cur = _dot(k_ref[0:sub, :], qTv)
        for t in range(nsub):
            if t + 1 < nsub:
                s_next = _dot(k_ref[sub * (t + 1):sub * (t + 2), :], qTv)
            m_new = jnp.maximum(m, jnp.max(s_cur, axis=0, keepdims=True))
            alpha = jnp.exp2((m - m_new) * MLA_C)
            p = jnp.exp2((s_cur - m_new) * MLA_C)
            l = alpha * l + jnp.sum(p, axis=0, keepdims=True)
            a = alpha * a + _dot(vT_ref[:, sub * t:sub * (t + 1)], p.astype(BF16))
            m = m_new
            if t + 1 < nsub:
                s_cur = s_next
        m_s[...], l_s[...], acc[...] = m, l, a

        @pl.when(j == nkv - 1)
        def _():
            l_f = l_s[...]
            o_ref[...] = (acc[...] / l_f).T.reshape(B_HEADS, tq, LANES)
            lse_ref[0] = m_s[...] * MLA_SCALE + jnp.log(l_f)

    return pl.pallas_call(
        body, name="mla_fwd", grid=(nq, nkv),
        in_specs=[pl.BlockSpec((B_HEADS, tq, 2 * LANES), lambda i, j: (0, i, 0)), pl.BlockSpec((tk, 2 * LANES), lambda i, j: (j, 0)),
                  pl.BlockSpec((LANES, tk), lambda i, j: (0, j))],
        out_specs=[pl.BlockSpec((B_HEADS, tq, LANES), lambda i, j: (0, i, 0)), pl.BlockSpec((1, 1, R), lambda i, j: (i, 0, 0))],
        out_shape=[_sds((B_HEADS, S, LANES), F32), _sds((nq, 1, R), F32)],
        scratch_shapes=[pltpu.VMEM((2 * LANES, R), BF16), pltpu.VMEM((1, R), F32), pltpu.VMEM((1, R), F32), pltpu.VMEM((LANES, R), F32)],
        compiler_params=_params(("arbitrary", "arbitrary")))(q, kcat, kcatT)


def _mla_bwd(q, kcat, kcatT, o, do, lse, *, tq, tk, sub):
    S = kcat.shape[0]; nq, nkv = S // tq, S // tk; R = B_HEADS * tq; nsub = tk // sub

    def body(q_ref, k_ref, kT_ref, o_ref, do_ref, lse_ref, dq_ref, dk_ref, qT, dosT, dos, delta_s, dq_acc):
        i, j = pl.program_id(0), pl.program_id(1)

        @pl.when((i == 0) & (j == 0))
        def _():
            dk_ref[...] = jnp.zeros((S, 2 * LANES), F32)

        @pl.when(j == 0)
        def _():
            qT[...] = q_ref[...].reshape(R, 2 * LANES).astype(F32).T.astype(BF16)
            dov = do_ref[...].reshape(R, LANES)
            dos[...] = dov.astype(BF16)
            dosT[...] = dov.T.astype(BF16)
            delta_s[...] = jnp.sum((dov * o_ref[...].reshape(R, LANES)).T, axis=0, keepdims=True)
            dq_acc[...] = jnp.zeros((2 * LANES, R), F32)

        qTv, dosTv, dosv = qT[...], dosT[...], dos[...]
        qv = q_ref[...].reshape(R, 2 * LANES)
        lse_v, delta_v = lse_ref[0] * LOG2E, delta_s[...]
        dqa = dq_acc[...]
        s_cur = _dot(k_ref[0:sub, :], qTv)
        dp_cur = _dot(k_ref[0:sub, 0:LANES], dosTv)
        for t in range(nsub):
            if t + 1 < nsub:
                s_next = _dot(k_ref[sub * (t + 1):sub * (t + 2), :], qTv)
                dp_next = _dot(k_ref[sub * (t + 1):sub * (t + 2), 0:LANES], dosTv)
            p = jnp.exp2(s_cur * MLA_C - lse_v)
            ds = (p * (dp_cur - delta_v) * MLA_SCALE).astype(BF16)
            rows = pl.ds(pl.multiple_of(j * tk + sub * t, sub), sub)
            dk_ref[rows, :] += _dot(ds, qv)
            dk_ref[rows, 0:LANES] += _dot(p.astype(BF16), dosv)
            dqa = dqa + _dot(kT_ref[:, sub * t:sub * (t + 1)], ds)
            if t + 1 < nsub:
                s_cur, dp_cur = s_next, dp_next
        dq_acc[...] = dqa

        @pl.when(j == nkv - 1)
        def _():
            dq_ref[...] = dq_acc[...].T.reshape(B_HEADS, tq, 2 * LANES)

    hspec = lambda w: pl.BlockSpec((B_HEADS, tq, w), lambda i, j: (0, i, 0))
    return pl.pallas_call(
        body, name="mla_bwd", grid=(nq, nkv),
        in_specs=[hspec(2 * LANES), pl.BlockSpec((tk, 2 * LANES), lambda i, j: (j, 0)), pl.BlockSpec((2 * LANES, tk), lambda i, j: (0, j)),
                  hspec(LANES), hspec(LANES), pl.BlockSpec((1, 1, R), lambda i, j: (i, 0, 0))],
        out_specs=[hspec(2 * LANES), pl.BlockSpec((S, 2 * LANES), lambda i, j: (0, 0))],
        out_shape=[_sds((B_HEADS, S, 2 * LANES), F32), _sds((S, 2 * LANES), F32)],
        scratch_shapes=[pltpu.VMEM((2 * LANES, R), BF16), pltpu.VMEM((LANES, R), BF16), pltpu.VMEM((R, LANES), BF16),
                        pltpu.VMEM((1, R), F32), pltpu.VMEM((2 * LANES, R), F32)],
        compiler_params=_params(("arbitrary", "arbitrary")))(q, kcat, kcatT, o, do, lse)


def _win_start(i, tq, nk, S):
    return pl.multiple_of(jnp.clip(i * tq - WINDOW, 0, S - nk), LANES)


def _win_dist_table(S, tq):
    nk = min(tq + 2 * WINDOW, S)
    nq = S // tq
    r = np.arange(nk)[:, None]
    c = (np.arange(2 * tq) % tq)[None, :]
    tabs = []
    for rel in (0, WINDOW, (nq - 1) * tq - (S - nk)):
        dist = np.abs(rel + c - r).astype(np.float32)
        tabs.append(np.where(dist <= WINDOW, dist, np.float32(1e32)))
    return jnp.asarray(np.stack(tabs))


def _win_dist_spec(nk, tq, nq):
    return pl.BlockSpec((1, nk, 2 * tq), lambda b, i: (jnp.where(i == 0, 0, jnp.where(i == nq - 1, 2, 1)), 0, 0))


def _win_fwd(q, k, vT, dist, slope, sink, *, kdiv, tq, nbs, name):
    S = k.shape[0]; nb = q.shape[1] // LANES; nq = S // tq; nk = min(tq + 2 * WINDOW, S)
    assert nb % nbs == 0 and kdiv % nbs == 0

    def body(q_ref, k_ref, vT_ref, dist_ref, slope_ref, sink_ref, o_ref, lse_ref):
        i = pl.program_id(1)
        rlo = _row_lo()
        k0 = _win_start(i, tq, nk, S)
        kk, vv, dd = k_ref[pl.ds(k0, nk), :], vT_ref[:, pl.ds(k0, nk)], dist_ref[0]
        for u in range(nbs):
            qsT = _stack_cols(q_ref[:, LANES * u:LANES * (u + 1)].astype(F32).T, rlo).astype(BF16)
            s = _dot(kk, qsT) - slope_ref[u] * dd
            sk = sink_ref[u]
            m = jnp.maximum(jnp.max(s, axis=0, keepdims=True), sk)
            p = jnp.exp(s - m)
            l = jnp.sum(p, axis=0, keepdims=True) + jnp.exp(sk - m)
            o_ref[:, LANES * u:LANES * (u + 1)] = _pick_halves_T(_dot(vv, p.astype(BF16)) / l, rlo, tq)
            lse_ref[u, 0] = m + jnp.log(l)

    row_spec = pl.BlockSpec((nbs, 1, 2 * tq), lambda b, i: (b, 0, 0))
    return pl.pallas_call(
        body, name=name, grid=(nb // nbs, nq),
        in_specs=[pl.BlockSpec((tq, nbs * LANES), lambda b, i: (i, b)), pl.BlockSpec((S, LANES), lambda b, i: (0, b * nbs // kdiv)),
                  pl.BlockSpec((LANES, S), lambda b, i: (b * nbs // kdiv, 0)), _win_dist_spec(nk, tq, nq), row_spec, row_spec],
        out_specs=[pl.BlockSpec((tq, nbs * LANES), lambda b, i: (i, b)), pl.BlockSpec((nbs, 1, 1, 2 * tq), lambda b, i: (b, i, 0, 0))],
        out_shape=[_sds((S, nb * LANES), F32), _sds((nb, nq, 1, 2 * tq), F32)],
        compiler_params=_params(("arbitrary", "arbitrary")))(q, k, vT, dist, slope, sink)


def _win_bwd(q, k, kT, v, o, do, lse, dist, slope, sink, *, kdiv, tq, nbs, name):
    S = k.shape[0]; nb = q.shape[1] // LANES; nkb = k.shape[1] // LANES; nq = S // tq; nk = min(tq + 2 * WINDOW, S)
    steps_per_kv = kdiv // nbs

    def body(q_ref, k_ref, kT_ref, v_ref, o_ref, do_ref, lse_ref, dist_ref, slope_ref, sink_ref, dq_ref, dk_ref, dv_ref, dsink_ref, ds_acc):
        b, i = pl.program_id(0), pl.program_id(1)
        rlo = _row_lo()
        lo = lax.broadcasted_iota(jnp.int32, (1, LANES), 1) < HEAD_DIM

        @pl.when((b % steps_per_kv == 0) & (i == 0))
        def _():
            dk_ref[...] = jnp.zeros((S, LANES), F32)
            dv_ref[...] = jnp.zeros((S, LANES), F32)

        @pl.when(i == 0)
        def _():
            ds_acc[...] = jnp.zeros((nbs, 2 * tq), F32)

        k0 = _win_start(i, tq, nk, S)
        rows = pl.ds(k0, nk)
        kk, vv, kkT, dd = k_ref[rows, :], v_ref[rows, :], kT_ref[:, rows], dist_ref[0]
        dv_sum, dk_sum = None, None
        for u in range(nbs):
            cols = slice(LANES * u, LANES * (u + 1))
            qv = q_ref[:, cols]
            qs = _stack_rows(qv, lo)
            qsT = _stack_cols(qv.astype(F32).T, rlo).astype(BF16)
            dov = do_ref[:, cols]
            dos = _stack_rows(dov.astype(BF16), lo)
            dosT = _stack_cols(dov.T, rlo).astype(BF16)
            prodT = (dov * o_ref[:, cols]).T
            delta = jnp.concatenate([jnp.sum(jnp.where(rlo, prodT, 0.0), axis=0, keepdims=True),
                                     jnp.sum(jnp.where(rlo, 0.0, prodT), axis=0, keepdims=True)], axis=1)
            lse_v = lse_ref[u, 0]
            ds_acc[u:u + 1, :] += -jnp.exp(sink_ref[u] - lse_v) * delta
            p = jnp.exp(_dot(kk, qsT) - slope_ref[u] * dd - lse_v)
            ds = (p * (_dot(vv, dosT) - delta)).astype(BF16)
            dv_u, dk_u = _dot(p.astype(BF16), dos), _dot(ds, qs)
            dv_sum = dv_u if dv_sum is None else dv_sum + dv_u
            dk_sum = dk_u if dk_sum is None else dk_sum + dk_u
            dq_ref[:, cols] = (_pick_halves_T(_dot(kkT, ds), rlo, tq) * 0.125).astype(BF16)
        dv_ref[rows, :] += dv_sum
        dk_ref[rows, :] += dk_sum

        @pl.when(i == nq - 1)
        def _():
            acc = ds_acc[...]
            for u in range(nbs):
                dsink_ref[u] = jnp.concatenate(
                    [jnp.broadcast_to(jnp.sum(acc[u:u + 1, 0:tq], axis=1, keepdims=True), (1, LANES)),
                     jnp.broadcast_to(jnp.sum(acc[u:u + 1, tq:2 * tq], axis=1, keepdims=True), (1, LANES)),
                     jnp.zeros((6, LANES), F32)], axis=0)

    qmap = lambda b, i: (i, b)
    kv_spec = pl.BlockSpec((S, LANES), lambda b, i: (0, b * nbs // kdiv))
    row_spec = pl.BlockSpec((nbs, 1, 2 * tq), lambda b, i: (b, 0, 0))
    wide = pl.BlockSpec((tq, nbs * LANES), qmap)
    return pl.pallas_call(
        body, name=name, grid=(nb // nbs, nq),
        in_specs=[wide, kv_spec, pl.BlockSpec((LANES, S), lambda b, i: (b * nbs // kdiv, 0)), kv_spec, wide, wide,
                  pl.BlockSpec((nbs, 1, 1, 2 * tq), lambda b, i: (b, i, 0, 0)), _win_dist_spec(nk, tq, nq), row_spec, row_spec],
        out_specs=[wide, kv_spec, kv_spec, pl.BlockSpec((nbs, 8, LANES), lambda b, i: (b, 0, 0))],
        out_shape=[_sds((S, nb * LANES), BF16), _sds((S, nkb * LANES), F32), _sds((S, nkb * LANES), F32), _sds((nb, 8, LANES), F32)],
        scratch_shapes=[pltpu.VMEM((nbs, 2 * tq), F32)],
        compiler_params=_params(("arbitrary", "arbitrary")))(q, k, kT, v, o, do, lse, dist, slope, sink)


def _sum_rows(v):
    return jnp.sum(v, axis=0, keepdims=True)


def _norm_mod_bwd(dh, xv, mod_ref, nw_ref, stats_ref):
    r = _rms(xv)
    xn = xv * r
    nw = nw_ref[...]
    stats_ref[0:1, :] += _sum_rows(dh)
    stats_ref[1:2, :] += _sum_rows(dh * (xn * nw))
    dn = dh * (1.0 + mod_ref[1:2, :])
    stats_ref[2:3, :] += _sum_rows(dn * xn)
    return _rms_bwd(xv, r, dn * nw)


def _even_gate_specs(ts):
    return [pl.BlockSpec((ts, 256), lambda i, c=c: (i, c)) for c in (3, 4, 7, 8)]


def _even_post_fwd(oa, olat, proj, x, gate, wuv, woe):
    S = x.shape[0]
    ts = min(ROW_TILE, S)

    def body(oa_ref, ol_ref, ga0_ref, ga1_ref, gb0_ref, gb1_ref, x_ref, gate_ref, wuv_ref, woe_ref, y_ref, x1_ref):
        sa, _ = _silu_and_grad(jnp.concatenate([ga0_ref[...], ga1_ref[...]], axis=1))
        sb, _ = _silu_and_grad(jnp.concatenate([gb0_ref[...], gb1_ref[...]], axis=1))
        olc = jnp.concatenate([ol_ref[hh] for hh in range(B_HEADS)], axis=1).astype(BF16)
        ob = _dot(olc, wuv_ref[...])
        mix = jnp.concatenate([oa_ref[...] * sa, ob * sb], axis=1).astype(BF16)
        y = _dot(mix, woe_ref[...])
        y_ref[...] = y
        x1_ref[...] = x_ref[...] + gate_ref[...] * y

    return pl.pallas_call(
        body, name="even_post_fwd", grid=(S // ts,),
        in_specs=[_row_spec(ts, 512), pl.BlockSpec((B_HEADS, ts, LANES), lambda i: (0, i, 0))] + _even_gate_specs(ts) +
                 [_row_spec(ts, D_MODEL), _full_spec((1, D_MODEL)), _full_spec((1024, 512)), _full_spec((1024, D_MODEL))],
        out_specs=[_row_spec(ts, D_MODEL), _row_spec(ts, D_MODEL)],
        out_shape=[_sds((S, D_MODEL), F32), _sds((S, D_MODEL), F32)],
        compiler_params=_params(("arbitrary",)),
    )(oa, olat, proj, proj, proj, proj, x, gate, wuv, woe)


def _odd_pre_fwd(x, mod, nw, wio):
    S = x.shape[0]
    ts = min(ROW_TILE, S)

    def body(x_ref, mod_ref, nw_ref, wio_ref, h_ref, proj_ref, q_ref, k_ref, v_ref, kt_ref, vt_ref):
        xv = x_ref[...]
        h = (xv * _rms(xv) * nw_ref[...]) * (1.0 + mod_ref[1:2, :]) + mod_ref[0:1, :]
        hb = h.astype(BF16)
        h_ref[...] = hb
        proj = _dot(hb, wio_ref[...])
        proj_ref[...] = proj
        q_ref[...] = (proj[:, 0:1024] * 0.125).astype(BF16)
        k_ref[...] = proj[:, 1024:1280].astype(BF16)
        v_ref[...] = proj[:, 1280:1536].astype(BF16)
        kt_ref[...] = proj[:, 1024:1280].T.astype(BF16)
        vt_ref[...] = proj[:, 1280:1536].T.astype(BF16)

    col_spec = pl.BlockSpec((256, ts), lambda i: (0, i))
    return pl.pallas_call(
        body, name="odd_pre_fwd", grid=(S // ts,),
        in_specs=[_row_spec(ts, D_MODEL), _full_spec((3, D_MODEL)), _full_spec((1, D_MODEL)), _full_spec((D_MODEL, ODD_IN))],
        out_specs=[_row_spec(ts, D_MODEL), _row_spec(ts, ODD_IN), _row_spec(ts, 1024), _row_spec(ts, 256), _row_spec(ts, 256),
                   col_spec, col_spec],
        out_shape=[_sds((S, D_MODEL), BF16), _sds((S, ODD_IN), F32), _sds((S, 1024), BF16), _sds((S, 256), BF16),
                   _sds((S, 256), BF16), _sds((256, S), BF16), _sds((256, S), BF16)],
        compiler_params=_params(("arbitrary",)),
    )(x, mod, nw, wio)


def _odd_post(oc, proj, x1, gate, woo, fw, tgt):
    S = x1.shape[0]
    ts = min(ROW_TILE, S)

    def body(oc_ref, g0_ref, g1_ref, x_ref, gate_ref, woo_ref, fw_ref, tgt_ref, doc_ref, dgc_ref, dx2_ref, dwoo_ref, stats_ref):
        @pl.when(pl.program_id(0) == 0)
        def _():
            dwoo_ref[...] = jnp.zeros((D_MODEL, D_MODEL), F32)
            stats_ref[...] = jnp.zeros((8, D_MODEL), F32)

        ocv = oc_ref[...]
        sg, dsg = _silu_and_grad(jnp.concatenate([g0_ref[...], g1_ref[...]], axis=1))
        mix = (ocv * sg).astype(BF16)
        woo_v = woo_ref[...]
        y = _dot(mix, woo_v)
        gate_v = gate_ref[...]
        x2 = x_ref[...] + gate_v * y
        r = _rms(x2)
        fw_v = fw_ref[...]
        xn = x2 * r
        err = xn * fw_v - tgt_ref[...]
        dout = err * (1.0 / D_MODEL)
        dx2 = _rms_bwd(x2, r, dout * fw_v)
        dx2_ref[...] = dx2
        stats_ref[0:1, :] += _sum_rows(dout * xn)
        stats_ref[1:2, :] += _sum_rows(dx2 * y)
        loss_t = 0.5 * jnp.sum(_sum_rows(err * dout), axis=-1, keepdims=True)
        stats_ref[2:3, :] += jnp.broadcast_to(loss_t, (1, D_MODEL))
        dy = (gate_v * dx2).astype(BF16)
        dmix = _dot_nt(dy, woo_v)
        dwoo_ref[...] += _dot_tn(mix, dy)
        doc_ref[...] = dmix * sg
        dgc_ref[...] = (dmix * ocv * dsg).astype(BF16)

    gate_cols = [pl.BlockSpec((ts, 512), lambda i, c=c: (i, c)) for c in (3, 4)]
    return pl.pallas_call(
        body, name="odd_post", grid=(S // ts,),
        in_specs=[_row_spec(ts, D_MODEL)] + gate_cols + [_row_spec(ts, D_MODEL), _full_spec((1, D_MODEL)),
                  _full_spec((D_MODEL, D_MODEL)), _full_spec((1, D_MODEL)), _row_spec(ts, D_MODEL)],
        out_specs=[_row_spec(ts, D_MODEL), _row_spec(ts, D_MODEL), _row_spec(ts, D_MODEL),
                   _full_spec((D_MODEL, D_MODEL), single=False), _full_spec((8, D_MODEL), single=False)],
        out_shape=[_sds((S, D_MODEL), F32), _sds((S, D_MODEL), BF16), _sds((S, D_MODEL), F32), _sds((D_MODEL, D_MODEL), F32),
                   _sds((8, D_MODEL), F32)],
        compiler_params=_params(("arbitrary",)),
    )(oc, proj, proj, x1, gate, woo, fw, tgt)


def _odd_pre_bwd(dq, dk, dv, dgc, h, x, dx_res, mod, nw, wio):
    S = x.shape[0]
    ts = min(ROW_TILE, S)

    def body(dq_ref, dk_ref, dv_ref, dgc_ref, h_ref, x_ref, dxr_ref, mod_ref, nw_ref, wio_ref, dx_ref, dw_ref, stats_ref):
        @pl.when(pl.program_id(0) == 0)
        def _():
            dw_ref[...] = jnp.zeros((D_MODEL, ODD_IN), F32)
            stats_ref[...] = jnp.zeros((8, D_MODEL), F32)

        dproj = jnp.concatenate([dq_ref[...], dk_ref[...].astype(BF16), dv_ref[...].astype(BF16), dgc_ref[...]], axis=1)
        dh = _dot_nt(dproj, wio_ref[...])
        dw_ref[...] += _dot_tn(h_ref[...], dproj)
        dx_ref[...] = dxr_ref[...] + _norm_mod_bwd(dh, x_ref[...], mod_ref, nw_ref, stats_ref)

    return pl.pallas_call(
        body, name="odd_pre_bwd", grid=(S // ts,),
        in_specs=[_row_spec(ts, 1024), _row_spec(ts, 256), _row_spec(ts, 256), _row_spec(ts, 1024), _row_spec(ts, D_MODEL),
                  _row_spec(ts, D_MODEL), _row_spec(ts, D_MODEL), _full_spec((3, D_MODEL)), _full_spec((1, D_MODEL)),
                  _full_spec((D_MODEL, ODD_IN))],
        out_specs=[_row_spec(ts, D_MODEL), _full_spec((D_MODEL, ODD_IN), single=False), _full_spec((8, D_MODEL), single=False)],
        out_shape=[_sds((S, D_MODEL), F32), _sds((D_MODEL, ODD_IN), F32), _sds((8, D_MODEL), F32)],
        compiler_params=_params(("arbitrary",)),
    )(dq, dk, dv, dgc, h, x, dx_res, mod, nw, wio)


def _even_post_bwd(dx1, y, oa, olat, proj, gate, wuv, woe):
    S = dx1.shape[0]
    ts = min(ROW_TILE, S)

    def body(dx_ref, y_ref, oa_ref, ol_ref, ga0_ref, ga1_ref, gb0_ref, gb1_ref, gate_ref, wuv_ref, woe_ref,
             doa_ref, dga_ref, dgb_ref, dol_ref, dwoe_ref, dwuv_ref, stats_ref):
        @pl.when(pl.program_id(0) == 0)
        def _():
            dwoe_ref[...] = jnp.zeros((D_MODEL, D_MODEL), F32)
            dwuv_ref[...] = jnp.zeros((1024, 512), F32)
            stats_ref[...] = jnp.zeros((8, D_MODEL), F32)

        dxv = dx_ref[...]
        stats_ref[0:1, :] += _sum_rows(dxv * y_ref[...])
        dy = (gate_ref[...] * dxv).astype(BF16)
        sa, dsa = _silu_and_grad(jnp.concatenate([ga0_ref[...], ga1_ref[...]], axis=1))
        sb, dsb = _silu_and_grad(jnp.concatenate([gb0_ref[...], gb1_ref[...]], axis=1))
        olc = jnp.concatenate([ol_ref[hh] for hh in range(B_HEADS)], axis=1).astype(BF16)
        wuv_v = wuv_ref[...]
        ob = _dot(olc, wuv_v)
        oav = oa_ref[...]
        mix = jnp.concatenate([oav * sa, ob * sb], axis=1).astype(BF16)
        dmix = _dot_nt(dy, woe_ref[...])
        dwoe_ref[...] += _dot_tn(mix, dy)
        dma, dmb = dmix[:, 0:512], dmix[:, 512:1024]
        doa_ref[...] = dma * sa
        dga_ref[...] = (dma * oav * dsa).astype(BF16)
        dgb_ref[...] = (dmb * ob * dsb).astype(BF16)
        dob = (dmb * sb).astype(BF16)
        dol = _dot_nt(dob, wuv_v)
        dwuv_ref[...] += _dot_tn(olc, dob)
        for hh in range(B_HEADS):
            dol_ref[hh] = dol[:, LANES * hh:LANES * (hh + 1)]

    head_spec = pl.BlockSpec((B_HEADS, ts, LANES), lambda i: (0, i, 0))
    return pl.pallas_call(
        body, name="even_post_bwd", grid=(S // ts,),
        in_specs=[_row_spec(ts, D_MODEL), _row_spec(ts, D_MODEL), _row_spec(ts, 512), head_spec] + _even_gate_specs(ts) +
                 [_full_spec((1, D_MODEL)), _full_spec((1024, 512)), _full_spec((1024, D_MODEL))],
        out_specs=[_row_spec(ts, 512), _row_spec(ts, 512), _row_spec(ts, 512), head_spec,
                   _full_spec((D_MODEL, D_MODEL), single=False), _full_spec((1024, 512), single=False),
                   _full_spec((8, D_MODEL), single=False)],
        out_shape=[_sds((S, 512), F32), _sds((S, 512), BF16), _sds((S, 512), BF16), _sds((B_HEADS, S, LANES), F32),
                   _sds((D_MODEL, D_MODEL), F32), _sds((1024, 512), F32), _sds((8, D_MODEL), F32)],
        compiler_params=_params(("arbitrary",)),
    )(dx1, y, oa, olat, proj, proj, proj, proj, gate, wuv, woe)


def _even_pre_bwd(x, h, proj, dqa, dka, dva, dga, dgb, dqcat, dkcat, dx_res, mod, nw, wie, qn, kn, seg, ca, sa, ct, st,
                  qln, kvln, wuq, wuk):
    S = x.shape[0]
    ts = min(ROW_TILE, S)
    nsteps = S // ts

    def body(x_ref, h_ref, proj_ref, dqa_ref, dka_ref, dva_ref, dga_ref, dgb_ref, dqc_ref, dkc_ref, dxr_ref, mod_ref, nw_ref,
             wie_ref, qn_ref, kn_ref, seg_ref, ca_ref, sa_ref, ct_ref, st_ref, qln_ref, kvln_ref, wuq_ref, wuk_ref,
             dx_ref, dwie_out, dwuq_out, dwuk_out, stats_ref, nstats_ref, dwie_ref, dwuq_ref, dwuk_ref):
        @pl.when(pl.program_id(0) == 0)
        def _():
            dwie_ref[...] = jnp.zeros((D_MODEL, EVEN_P), F32)
            dwuq_ref[...] = jnp.zeros((B_Q_LORA, 1536), F32)
            dwuk_ref[...] = jnp.zeros((512, 1024), F32)
            stats_ref[...] = jnp.zeros((8, D_MODEL), F32)
            nstats_ref[...] = jnp.zeros((8, 256), F32)

        lane = _lane_iota()
        ca_v, sa_v, ct_v, st_v = ca_ref[...], sa_ref[...], ct_ref[...], st_ref[...]
        seg_v = seg_ref[...]

        def head_norm_bwd(xc, dy, w):
            r = lax.rsqrt(_seg_mean(xc * xc, seg_v) + EPS)
            g = dy * w
            dxc = r * g - xc * (r * r * r) * _seg_mean(xc * g, seg_v)
            return dxc, _sum_rows(dy * (xc * r))

        pieces = []
        dqn = jnp.zeros((1, LANES), F32)
        for cb in range(4):
            sl = slice(LANES * cb, LANES * (cb + 1))
            dy = _rot_bwd(dqa_ref[:, sl] * 0.125, ca_v, sa_v, lane)
            dxc, dw = head_norm_bwd(proj_ref[:, sl], dy, qn_ref[...])
            pieces.append(dxc)
            dqn = dqn + dw
        dxc, dkn = head_norm_bwd(proj_ref[:, 512:640], _rot_bwd(dka_ref[...], ca_v, sa_v, lane), kn_ref[...])
        pieces += [dxc, dva_ref[...], dga_ref[...]]
        nstats_ref[0:1, 0:LANES] += dqn + pltpu.roll(dqn, HEAD_DIM, 1)
        nstats_ref[1:2, 0:LANES] += dkn + pltpu.roll(dkn, HEAD_DIM, 1)

        cq = proj_ref[:, 1280:1536]
        rq = _rms(cq)
        cqn_f = cq * rq
        qln_v = qln_ref[...]
        cqn = (cqn_f * qln_v).astype(BF16)
        wuq_v, wuk_v = wuq_ref[...], wuk_ref[...]
        qnope = _dot(cqn, wuq_v[:, 0:512]).astype(BF16)
        dqlat = jnp.concatenate([dqc_ref[hh, :, 0:LANES] for hh in range(B_HEADS)], axis=1).astype(BF16)
        dqnope = _dot_nt(dqlat, wuk_v)
        dwuk_ref[...] += _dot_tn(qnope, dqlat)
        dqr = [_rot_bwd(dqc_ref[hh, :, LANES:2 * LANES], ct_v, st_v, lane) for hh in range(B_HEADS)]
        dqb = jnp.concatenate([dqnope] + dqr, axis=1).astype(BF16)
        dcqn = _dot_nt(dqb, wuq_v)
        dwuq_ref[...] += _dot_tn(cqn, dqb)
        nstats_ref[2:3, :] += _sum_rows(dcqn * cqn_f)
        dcq = _rms_bwd(cq, rq, dcqn * qln_v)
        ckv = proj_ref[:, 1536:1664]
        rk = _rms(ckv)
        dckvn = dkc_ref[:, 0:LANES]
        nstats_ref[3:4, 0:LANES] += _sum_rows(dckvn * (ckv * rk))
        dckv = _rms_bwd(ckv, rk, dckvn * kvln_ref[...])
        dkr = _rot_bwd(dkc_ref[:, LANES:2 * LANES], ct_v, st_v, lane)
        pieces += [dcq, dckv, dkr, dgb_ref[...]]
        dproj = jnp.concatenate([piece.astype(BF16) for piece in pieces], axis=1)
        dh = _dot_nt(dproj, wie_ref[...])
        dwie_ref[...] += _dot_tn(h_ref[...], dproj)
        dx_ref[...] = dxr_ref[...] + _norm_mod_bwd(dh, x_ref[...], mod_ref, nw_ref, stats_ref)

        @pl.when(pl.program_id(0) == nsteps - 1)
        def _():
            pltpu.sync_copy(dwie_ref, dwie_out)
            pltpu.sync_copy(dwuq_ref, dwuq_out)
            pltpu.sync_copy(dwuk_ref, dwuk_out)

    return pl.pallas_call(
        body, name="even_pre_bwd", grid=(nsteps,),
        in_specs=[_row_spec(ts, D_MODEL), _row_spec(ts, D_MODEL), _row_spec(ts, EVEN_P), _row_spec(ts, 512), _row_spec(ts, LANES),
                  _row_spec(ts, LANES), _row_spec(ts, 512), _row_spec(ts, 512),
                  pl.BlockSpec((B_HEADS, ts, 2 * LANES), lambda i: (0, i, 0)), _row_spec(ts, 2 * LANES), _row_spec(ts, D_MODEL),
                  _full_spec((3, D_MODEL)), _full_spec((1, D_MODEL)), _full_spec((D_MODEL, EVEN_P)),
                  _full_spec((1, LANES)), _full_spec((1, LANES)), _full_spec((LANES, LANES)),
                  _row_spec(ts, LANES), _row_spec(ts, LANES), _row_spec(ts, LANES), _row_spec(ts, LANES),
                  _full_spec((1, B_Q_LORA)), _full_spec((1, B_KV_LORA)), _full_spec((B_Q_LORA, 1536)), _full_spec((512, 1024))],
        out_specs=[_row_spec(ts, D_MODEL), _ANY, _ANY, _ANY, _full_spec((8, D_MODEL), single=False), _full_spec((8, 256), single=False)],
        out_shape=[_sds((S, D_MODEL), F32), _sds((D_MODEL, EVEN_P), F32), _sds((B_Q_LORA, 1536), F32), _sds((512, 1024), F32),
                   _sds((8, D_MODEL), F32), _sds((8, 256), F32)],
        scratch_shapes=[pltpu.VMEM((D_MODEL, EVEN_P), F32), pltpu.VMEM((B_Q_LORA, 1536), F32), pltpu.VMEM((512, 1024), F32)],
        compiler_params=_params(("arbitrary",)),
    )(x, h, proj, dqa, dka, dva, dga, dgb, dqcat, dkcat, dx_res, mod, nw, wie, qn, kn, seg, ca, sa, ct, st, qln, kvln, wuq, wuk)


def _ada_fwd(c_all, w, b):
    n = w.shape[2]

    def body(c_ref, w_ref, b_ref, o_ref):
        cv = c_ref[...]
        o_ref[0] = _dot_f32(cv * _sigmoid(cv), w_ref[0]) + b_ref[0]

    return pl.pallas_call(
        body, name="ada_fwd", grid=(2,),
        in_specs=[pl.BlockSpec((N_DEV, D_MODEL), lambda l: (0, 0)), pl.BlockSpec((1, D_MODEL, n), lambda l: (l, 0, 0)),
                  pl.BlockSpec((1, 1, n), lambda l: (l, 0, 0))],
        out_specs=pl.BlockSpec((1, N_DEV, n), lambda l: (l, 0, 0)),
        out_shape=_sds((2, N_DEV, n), F32),
        compiler_params=_params(("arbitrary",)),
    )(c_all, w, b)


def _ada_bwd(c_all_t, dmod):
    n = dmod.shape[2]

    def body(c_ref, d_ref, o_ref):
        cv = c_ref[...]
        act = cv * _sigmoid(cv)
        dv = d_ref[0]
        acc = act[:, 0:1] * dv[0:1, :]
        for bb in range(1, N_DEV):
            acc = acc + act[:, bb:bb + 1] * dv[bb:bb + 1, :]
        o_ref[0] = acc

    return pl.pallas_call(
        body, name="ada_bwd", grid=(2,),
        in_specs=[pl.BlockSpec((D_MODEL, N_DEV), lambda l: (0, 0)), pl.BlockSpec((1, N_DEV, n), lambda l: (l, 0, 0))],
        out_specs=pl.BlockSpec((1, D_MODEL, n), lambda l: (l, 0, 0)),
        out_shape=_sds((2, D_MODEL, n), F32),
        compiler_params=_params(("arbitrary",)),
    )(c_all_t, dmod)


ADAM_ROW_TILE = 256


def _adam_update(g, w, m, v):
    m_new = ADAM_B1 * m + (1.0 - ADAM_B1) * g
    v_new = ADAM_B2 * v + (1.0 - ADAM_B2) * jnp.square(g)
    m_hat = m_new / (1.0 - ADAM_B1 ** ADAM_STEP)
    v_hat = v_new / (1.0 - ADAM_B2 ** ADAM_STEP)
    return -ADAM_LR * (m_hat / (jnp.sqrt(v_hat) + ADAM_EPS) + ADAM_WD * w), m_new, v_new


SMALL_ROWS = dict(dmod=(0, D_MODEL), norm_w=(6, D_MODEL), final_norm=(8, D_MODEL), a_q_norm=(9, HEAD_DIM), a_k_norm=(10, HEAD_DIM),
                  b_q_lora_norm=(11, B_Q_LORA), b_kv_lora_norm=(12, B_KV_LORA), c_sink=(13, C_HEADS))
SMALL_WEIGHTS = ("ada_b", "norm_w", "final_norm", "a_q_norm", "a_k_norm", "b_q_lora_norm", "b_kv_lora_norm", "c_sink")


def _pack_small(res):
    def padded(v):
        return jnp.concatenate([v, jnp.zeros((v.shape[0], D_MODEL - v.shape[1]), F32)], axis=1)

    rows = [res["dmod"].reshape(6, D_MODEL), res["norm_w"], res["final_norm"].reshape(1, D_MODEL)]
    rows += [padded(res[k]) for k in ("a_q_norm", "a_k_norm", "b_q_lora_norm", "b_kv_lora_norm", "c_sink")]
    return jnp.concatenate(rows + [jnp.zeros((2, D_MODEL), F32)], axis=0)


def _adam_small(parts, ws, ms, vs):
    nw = len(SMALL_WEIGHTS)

    def body(*refs):
        p_ref = refs[0]
        w_refs, m_refs, v_refs = refs[1:1 + nw], refs[1 + nw:1 + 2 * nw], refs[1 + 2 * nw:1 + 3 * nw]
        outs = refs[1 + 3 * nw:]
        g_all = p_ref[0]
        for k in range(1, N_DEV):
            g_all = g_all + p_ref[k]
        for idx, name in enumerate(SMALL_WEIGHTS):
            if name == "ada_b":
                g = jnp.concatenate([jnp.concatenate([g_all[3 * l + t:3 * l + t + 1] for t in range(3)], axis=1) for l in range(2)],
                                    axis=0)
            else:
                row, width = SMALL_ROWS[name]
                g = g_all[row:row + w_refs[idx].shape[0], 0:width]
            d, m_new, v_new = _adam_update(g, w_refs[idx][...], m_refs[idx][...], v_refs[idx][...])
            outs[4 * idx][...], outs[4 * idx + 1][...], outs[4 * idx + 2][...], outs[4 * idx + 3][...] = g, d, m_new, v_new

    out_shape = []
    for w in ws:
        out_shape += [_sds(w.shape, F32)] * 4
    return pl.pallas_call(body, name="adam_small", out_shape=out_shape,
                          compiler_params=pltpu.CompilerParams(vmem_limit_bytes=VMEM_LIMIT))(parts, *ws, *ms, *vs)


def _adam(parts, w, m, v, name):
    P, R, C = parts.shape
    tr = R if R <= ADAM_ROW_TILE else ADAM_ROW_TILE
    assert R % tr == 0

    def body(p_ref, w_ref, m_ref, v_ref, g_ref, d_ref, nm_ref, nv_ref):
        g = p_ref[0].astype(F32)
        for k in range(1, P):
            g = g + p_ref[k].astype(F32)
        g_ref[...] = g
        d_ref[...], nm_ref[...], nv_ref[...] = _adam_update(g, w_ref[...], m_ref[...], v_ref[...])

    spec = pl.BlockSpec((tr, C), lambda i: (i, 0))
    return pl.pallas_call(
        body, name=name, grid=(R // tr,),
        in_specs=[pl.BlockSpec((P, tr, C), lambda i: (0, i, 0)), spec, spec, spec],
        out_specs=[spec, spec, spec, spec], out_shape=[_sds((R, C), F32)] * 4,
        compiler_params=_params(("arbitrary",)),
    )(parts, w, m, v)


_ANY = pl.BlockSpec(memory_space=pl.ANY)
CHIP_FLIPS = ((1, 0), (0, 1), (1, 1))
DEV_FLIPS = tuple((dx, dy, dc) for dx in (0, 1) for dy in (0, 1) for dc in (0, 1) if dx + dy + dc)


def _flip(a, d):
    return a if d == 0 else 1 - a


def _my_place():
    return lax.axis_index("x"), lax.axis_index("y"), lax.axis_index("c")


def _gather_dev8(arrs, name):
    n = len(arrs)

    def body(*refs):
        ins, outs = refs[:n], refs[n:2 * n]
        send_sems, recv_sems, loc_sems = refs[2 * n:]
        x, y, c = _my_place()
        me = 4 * x + 2 * y + c
        copies = []
        for a in range(n):
            loc = pltpu.make_async_copy(ins[a], outs[a].at[me], loc_sems.at[a])
            loc.start()
            copies.append(loc)
            for k, (dx, dy, dc) in enumerate(DEV_FLIPS):
                cp = pltpu.make_async_remote_copy(
                    src_ref=ins[a], dst_ref=outs[a].at[me], send_sem=send_sems.at[a, k], recv_sem=recv_sems.at[a, k],
                    device_id=(_flip(x, dx), _flip(y, dy), _flip(c, dc)), device_id_type=MESH_ID)
                cp.start()
                copies.append(cp)
        for cp in copies:
            cp.wait()

    return pl.pallas_call(
        body, name=name, in_specs=[_ANY] * n, out_specs=[_ANY] * n,
        out_shape=[_sds((N_DEV,) + a.shape, a.dtype) for a in arrs],
        scratch_shapes=[pltpu.SemaphoreType.DMA((n, 7)), pltpu.SemaphoreType.DMA((n, 7)), pltpu.SemaphoreType.DMA((n,))],
    )(*arrs)


class _Exchange:
    def __init__(self, arrs, out_shapes, n_sems, phases):
        self.arrs, self.out_shapes, self.n_sems, self._phases = list(arrs), list(out_shapes), n_sems, phases

    @property
    def n(self):
        return len(self.arrs)

    def sem_shapes(self):
        return [pltpu.SemaphoreType.DMA((self.n, self.n_sems)), pltpu.SemaphoreType.DMA((self.n, self.n_sems)),
                pltpu.SemaphoreType.DMA((self.n,))]

    def phases(self, ins, outs, sems):
        return self._phases(ins, outs, *sems)

    def run(self, name):
        n = self.n

        def body(*refs):
            start, mid, end = self.phases(refs[:n], refs[n:2 * n], refs[2 * n:])
            start()
            mid()
            end()

        return pl.pallas_call(body, name=name, in_specs=[_ANY] * n, out_specs=[_ANY] * n, out_shape=self.out_shapes,
                              scratch_shapes=self.sem_shapes())(*self.arrs)


def _gather_halves_phases(ins, outs, send_sems, recv_sems, loc_sems):
    n = len(ins)
    x, y, c = _my_place()
    chip = 2 * x + y
    sibling = (x, y, 1 - c)
    peers = [(_flip(x, dx), _flip(y, dy)) for dx, dy in CHIP_FLIPS]

    def remote(src, p, half, a, k, to):
        return pltpu.make_async_remote_copy(src_ref=src, dst_ref=outs[a].at[p, half], send_sem=send_sems.at[a, k],
                                            recv_sem=recv_sems.at[a, k], device_id=to, device_id_type=MESH_ID)

    def local(a):
        return pltpu.make_async_copy(ins[a], outs[a].at[chip], loc_sems.at[a])

    def first(a, k):
        return remote(ins[a].at[c], chip, c, a, k, (*peers[k], c))

    def passed(a, k):
        p = 2 * peers[k][0] + peers[k][1]
        return remote(outs[a].at[p, c], p, c, a, 3 + k, sibling)

    def start():
        for a in range(n):
            local(a).start()
            for k in range(3):
                first(a, k).start()

    def mid():
        for a in range(n):
            for k in range(3):
                p = 2 * peers[k][0] + peers[k][1]
                remote(outs[a].at[p, c], p, c, a, k, sibling).wait_recv()
                passed(a, k).start()

    def end():
        for a in range(n):
            for k in range(3):
                p = 2 * peers[k][0] + peers[k][1]
                remote(outs[a].at[p, 1 - c], p, 1 - c, a, 3 + k, sibling).wait_recv()
        for a in range(n):
            for k in range(3):
                first(a, k).wait_send()
                passed(a, k).wait_send()
            local(a).wait()

    return start, mid, end


def _gather_chip4_halves(arrs):
    return _Exchange(arrs, [_sds((N_CHIPS,) + a.shape, a.dtype) for a in arrs], 6, _gather_halves_phases)


def _reduce_phases(ins, outs, send_sems, recv_sems, loc_sems):
    n = len(ins)
    x, y, c = _my_place()
    chip = 2 * x + y
    sibling = (x, y, 1 - c)
    peers = [(_flip(x, dx), _flip(y, dy)) for dx, dy in CHIP_FLIPS]

    def remote(src, slot, a, k, to):
        return pltpu.make_async_remote_copy(src_ref=src, dst_ref=outs[a].at[slot], send_sem=send_sems.at[a, k],
                                            recv_sem=recv_sems.at[a, k], device_id=to, device_id_type=MESH_ID)

    def local(a):
        return pltpu.make_async_copy(ins[a].at[chip], outs[a].at[2 * chip + c], loc_sems.at[a])

    def own(a):
        return remote(ins[a].at[chip], 2 * chip + c, a, 0, sibling)

    def first(a, k):
        return remote(ins[a].at[2 * peers[k][0] + peers[k][1]], 2 * chip + c, a, 1 + k, (*peers[k], c))

    def passed(a, k):
        slot = 2 * (2 * peers[k][0] + peers[k][1]) + c
        return remote(outs[a].at[slot], slot, a, 4 + k, sibling)

    def start():
        for a in range(n):
            local(a).start()
            own(a).start()
            for k in range(3):
                first(a, k).start()

    def mid():
        for a in range(n):
            for k in range(3):
                slot = 2 * (2 * peers[k][0] + peers[k][1]) + c
                remote(outs[a].at[slot], slot, a, 1 + k, sibling).wait_recv()
                passed(a, k).start()

    def end():
        for a in range(n):
            remote(outs[a].at[2 * chip + 1 - c], 2 * chip + 1 - c, a, 0, sibling).wait_recv()
            for k in range(3):
                slot = 2 * (2 * peers[k][0] + peers[k][1]) + 1 - c
                remote(outs[a].at[slot], slot, a, 4 + k, sibling).wait_recv()
        for a in range(n):
            own(a).wait_send()
            for k in range(3):
                first(a, k).wait_send()
                passed(a, k).wait_send()
            local(a).wait()

    return start, mid, end


def _reduce_exchange(arrs):
    return _Exchange(arrs, [_sds((N_DEV,) + a.shape[1:], a.dtype) for a in arrs], 7, _reduce_phases)


def _pair_order(nheads, nkv):
    group = nheads // nkv
    order = []
    for m in range(nkv // 2):
        for i in range(group):
            order += [2 * m * group + i, (2 * m + 1) * group + i]
    return order


A_ORDER = _pair_order(A_HEADS, A_KV_HEADS)
C_ORDER = _pair_order(C_HEADS, C_KV_HEADS)
A_INV = [int(k) for k in np.argsort(A_ORDER)]
C_INV = [int(k) for k in np.argsort(C_ORDER)]


def _perm_heads(w, order, axis):
    return jnp.concatenate([lax.slice_in_dim(w, HEAD_DIM * h, HEAD_DIM * (h + 1), axis=axis) for h in order], axis=axis)


def _even_in_layout(w):
    return jnp.concatenate([_perm_heads(w[:, 0:512], A_ORDER, 1), w[:, 512:768], _perm_heads(w[:, 768:1280], A_ORDER, 1),
                            w[:, 1280:1696], jnp.zeros((w.shape[0], 96), w.dtype), w[:, 1696:2208]], axis=1)


def _even_in_unlayout(g):
    return jnp.concatenate([_perm_heads(g[:, 0:512], A_INV, 1), g[:, 512:768], _perm_heads(g[:, 768:1280], A_INV, 1),
                            g[:, 1280:1696], g[:, 1792:2304]], axis=1)


def _even_out_layout(w):
    return jnp.concatenate([_perm_heads(w[0:512], A_ORDER, 0), w[512:1024]], axis=0)


def _even_out_unlayout(g):
    return jnp.concatenate([_perm_heads(g[0:512], A_INV, 0), g[512:1024]], axis=0)


def _odd_in_layout(w):
    return jnp.concatenate([_perm_heads(w[:, 0:1024], C_ORDER, 1), w[:, 1024:1536], _perm_heads(w[:, 1536:2560], C_ORDER, 1)],
                           axis=1)


def _odd_in_unlayout(g):
    return jnp.concatenate([_perm_heads(g[:, 0:1024], C_INV, 1), g[:, 1024:1536], _perm_heads(g[:, 1536:2560], C_INV, 1)],
                           axis=1)


def _uq_layout(w):
    per = B_NOPE + B_ROPE
    pad = jnp.zeros((w.shape[0], LANES - B_ROPE), w.dtype)
    nope = [w[:, per * h:per * h + B_NOPE] for h in range(B_HEADS)]
    rope = [jnp.concatenate([w[:, per * h + B_NOPE:per * (h + 1)], pad], axis=1) for h in range(B_HEADS)]
    return jnp.concatenate(nope + rope, axis=1)


def _uq_unlayout(g):
    parts = []
    for h in range(B_HEADS):
        parts += [g[:, B_NOPE * h:B_NOPE * (h + 1)], g[:, 512 + LANES * h:512 + LANES * h + B_ROPE]]
    return jnp.concatenate(parts, axis=1)


def _block_diag(blocks):
    rows = []
    for h, blk in enumerate(blocks):
        r, cdim = blk.shape
        n = len(blocks)
        rows.append(jnp.concatenate([jnp.zeros((r, cdim * h), blk.dtype), blk, jnp.zeros((r, cdim * (n - 1 - h)), blk.dtype)],
                                    axis=1))
    return jnp.concatenate(rows, axis=0)


def _uk_layout(w):
    return _block_diag([w[:, h, :].T for h in range(B_HEADS)])


def _uk_unlayout(g):
    return jnp.stack([g[B_NOPE * h:B_NOPE * (h + 1), LANES * h:LANES * (h + 1)].T for h in range(B_HEADS)], axis=1)


def _uv_layout(w):
    return _block_diag([w[:, h, :] for h in range(B_HEADS)])


def _uv_unlayout(g):
    return jnp.stack([g[LANES * h:LANES * (h + 1), B_V * h:B_V * (h + 1)] for h in range(B_HEADS)], axis=1)


def _rope_tables(S):
    inv = ROPE_THETA ** (-jnp.arange(0, 32, 2, dtype=F32) / 32)
    tok = jnp.arange(S)

    def tab(pos):
        ang = pos.astype(F32)[:, None] * inv[None, :]
        cos, sin = jnp.cos(ang), jnp.sin(ang)
        return jnp.concatenate([cos, cos], axis=1), jnp.concatenate([-sin, sin], axis=1)

    cr, sr = tab(tok // GRID_W)
    cc, sc = tab(tok % GRID_W)
    ct, st = tab(tok)
    return (jnp.tile(jnp.concatenate([cr, cc], axis=1), (1, 2)), jnp.tile(jnp.concatenate([sr, sc], axis=1), (1, 2)),
            jnp.tile(ct, (1, 4)), jnp.tile(st, (1, 4)))


A_TQ, A_TK, A_SUB = 512, 4096, 512
B_TQ, B_TK, B_SUB = 128, 4096, 1024
B_BWD_TK, B_BWD_SUB = 4096, 512
C_T = 256
C_BLOCKS_PER_STEP = 4


def _local_step(x0, tgt, mod, norm_w, wie, wuq, wuk, wuv, late_shards, a_q_norm, a_k_norm, q_lora_norm, kv_lora_norm,
                c_sink, final_norm):
    S = x0.shape[0]
    mod3 = mod.reshape(2, 3, D_MODEL)
    ca, sa, ct, st = _rope_tables(S)
    lane_seg = np.arange(LANES) // HEAD_DIM
    seg = jnp.asarray((lane_seg[:, None] == lane_seg[None, :]).astype(np.float32)).astype(BF16)
    qn = jnp.tile(a_q_norm.reshape(1, HEAD_DIM), (1, 2))
    kn = jnp.tile(a_k_norm.reshape(1, HEAD_DIM), (1, 2))
    qln, kvln = q_lora_norm.reshape(1, B_Q_LORA), kv_lora_norm.reshape(1, B_KV_LORA)
    nw0, nw1 = norm_w[0:1], norm_w[1:2]
    gate0, gate1 = mod3[0, 2:3], mod3[1, 2:3]
    a_tq, a_tk, b_tq, b_tk, bb_tk, c_t = min(A_TQ, S), min(A_TK, S), min(B_TQ, S), min(B_TK, S), min(B_BWD_TK, S), min(C_T, S)
    a_sub, b_sub, bb_sub = min(A_SUB, a_tk), min(B_SUB, b_tk), min(B_BWD_SUB, bb_tk)

    h0, proj_e, qa, ka, va, qcat, kcat, ka_t, va_t, kcat_t = _even_pre_fwd(x0, mod3[0], nw0, wie, qn, kn, seg, ca, sa, ct, st,
                                                                           qln, kvln, wuq, wuk)
    oa, lse_a, woe_g, wio_g, woo_g = _pp_fwd(qa, ka, va_t, kdiv=4, tq=a_tq, tk=a_tk, sub=a_sub, name="attn_a_fwd",
                                             side=_gather_chip4_halves(late_shards))
    woe = _even_out_layout(woe_g.reshape(D_MODEL, D_MODEL))
    wio = _odd_in_layout(_chips_to_cols(wio_g.reshape(N_CHIPS, D_MODEL, ODD_IN // N_CHIPS)))
    woo = _perm_heads(woo_g.reshape(D_MODEL, D_MODEL), C_ORDER, 0)
    olat, lse_b = _mla_fwd(qcat, kcat, kcat_t, tq=b_tq, tk=b_tk, sub=b_sub)
    y0, x1 = _even_post_fwd(oa, olat, proj_e, x0, gate0, wuv, woe)
    h1, proj_o, qc, kc, vc, kc_t, vc_t = _odd_pre_fwd(x1, mod3[1], nw1, wio)
    slopes = 2.0 ** (-8.0 * jnp.arange(1, C_HEADS + 1, dtype=F32) / C_HEADS)
    c_order = np.asarray(C_ORDER)
    slope_rows = jnp.repeat(slopes[c_order].reshape(C_HEADS // 2, 2), c_t, axis=1)[:, None, :]
    sink_rows = jnp.repeat(c_sink.reshape(C_HEADS)[c_order].reshape(C_HEADS // 2, 2), c_t, axis=1)[:, None, :]
    win_dist = _win_dist_table(S, c_t)
    oc, lse_c = _win_fwd(qc, kc, vc_t, win_dist, slope_rows, sink_rows, kdiv=4, tq=c_t, nbs=C_BLOCKS_PER_STEP, name="attn_c_fwd")
    doc, dgc, dx2, dwoo, st_f = _odd_post(oc, proj_o, x1, gate1, woo, final_norm.reshape(1, D_MODEL), tgt)
    dqc, dkc, dvc, dsink_raw = _win_bwd(qc, kc, kc_t, vc, oc, doc, lse_c, win_dist, slope_rows, sink_rows, kdiv=4, tq=c_t,
                                        nbs=C_BLOCKS_PER_STEP, name="attn_c_bwd")
    dx1, dwio, st_1 = _odd_pre_bwd(dqc, dkc, dvc, dgc, h1, x1, dx2, mod3[1], nw1, wio)
    doa, dga, dgb, dolat, dwoe, dwuv, st_e = _even_post_bwd(dx1, y0, oa, olat, proj_e, gate0, wuv, woe)
    late_grads = _reduce_exchange(
        [_even_out_unlayout(dwoe.astype(BF16)).reshape(N_CHIPS, D_MODEL // N_CHIPS, D_MODEL),
         _cols_to_chips(_odd_in_unlayout(dwio.astype(BF16))),
         _perm_heads(dwoo.astype(BF16), C_INV, 0).reshape(N_CHIPS, D_MODEL // N_CHIPS, D_MODEL)])
    dqa, dka, dva, p_woe, p_wio, p_woo = _pp_bwd(qa, ka, ka_t, va, oa, doa, lse_a, kdiv=4, tq=a_tq, tk=a_tk, sub=a_sub,
                                                 name="attn_a_bwd", side=late_grads)
    dqcat, dkcat = _mla_bwd(qcat, kcat, kcat_t, olat, dolat, lse_b, tq=b_tq, tk=bb_tk, sub=bb_sub)
    dx0, dwie, dwuq, dwuk, st_0, nst = _even_pre_bwd(x0, h0, proj_e, dqa, dka, dva, dga, dgb, dqcat, dkcat, dx1, mod3[0], nw0,
                                                     wie, qn, kn, seg, ca, sa, ct, st, qln, kvln, wuq, wuk)
    dsink_pairs = jnp.stack([dsink_raw[:, 0, 0], dsink_raw[:, 1, 0]], axis=1).reshape(C_HEADS)
    return dict(
        loss=st_f[2, 0], dx=dx0,
        dmod=jnp.stack([jnp.concatenate([st_0[0], st_0[1], st_e[0]]), jnp.concatenate([st_1[0], st_1[1], st_f[1]])]),
        norm_w=jnp.stack([st_0[2], st_1[2]]), final_norm=st_f[0],
        a_q_norm=nst[0:1, 0:HEAD_DIM], a_k_norm=nst[1:2, 0:HEAD_DIM], b_q_lora_norm=nst[2:3, :], b_kv_lora_norm=nst[3:4, 0:LANES],
        c_sink=dsink_pairs[np.asarray(C_INV)].reshape(1, C_HEADS),
        even_w_in=dwie, b_w_uq=dwuq, b_w_uk=dwuk, b_w_uv=dwuv, even_w_out=p_woe, odd_w_in=p_wio, odd_w_out=p_woo)


WEIGHT_NAMES = ("norm_w", "ada_w", "ada_b", "even_w_in", "a_q_norm", "a_k_norm", "b_q_lora_norm", "b_kv_lora_norm", "b_w_uq",
                "b_w_uk", "b_w_uv", "even_w_out", "odd_w_in", "c_sink", "odd_w_out", "final_norm")


def _cols_to_chips(g):
    r, n4 = g.shape
    return jnp.transpose(g.reshape(r, N_CHIPS, n4 // N_CHIPS), (1, 0, 2))


def _chips_to_cols(g):
    p, r, n = g.shape
    return jnp.transpose(g, (1, 0, 2)).reshape(r, p * n)


def kernel(x, c, norm_w, ada_w, ada_b, even_w_in, a_q_norm, a_k_norm, b_q_lora_norm, b_kv_lora_norm, b_w_uq, b_w_uk, b_w_uv, even_w_out, odd_w_in, c_sink, odd_w_out, final_norm, loss_target, m_norm_w, m_ada_w, m_ada_b, m_even_w_in, m_a_q_norm, m_a_k_norm, m_b_q_lora_norm, m_b_kv_lora_norm, m_b_w_uq, m_b_w_uk, m_b_w_uv, m_even_w_out, m_odd_w_in, m_c_sink, m_odd_w_out, m_final_norm, v_norm_w, v_ada_w, v_ada_b, v_even_w_in, v_a_q_norm, v_a_k_norm, v_b_q_lora_norm, v_b_kv_lora_norm, v_b_w_uq, v_b_w_uk, v_b_w_uv, v_even_w_out, v_odd_w_in, v_c_sink, v_odd_w_out, v_final_norm):
    given = dict(locals())
    xi, yi, ci = _my_place()
    chip = 2 * xi + yi
    dev = 2 * chip + ci
    n_ada = ada_w.shape[2]

    (c_all,) = _gather_dev8([c], "gather_c")
    c_all = c_all.reshape(N_DEV, D_MODEL)
    bias = lax.dynamic_slice_in_dim(ada_b, chip * n_ada, n_ada, axis=1).reshape(2, 1, n_ada)
    mod_cols = _ada_fwd(c_all, ada_w, bias)
    def halves(w):
        return w.astype(BF16).reshape((2, w.shape[0] // 2) + w.shape[1:])

    mod_all, wie_g, wuq_g = _gather_chip4_halves([mod_cols, halves(even_w_in[0]), halves(b_w_uq[0])]).run("gather_weights")
    wie_g = wie_g.reshape(N_CHIPS, D_MODEL, EVEN_IN // N_CHIPS)
    wuq_g = wuq_g.reshape(N_CHIPS, B_Q_LORA, -1)
    mod = jnp.transpose(lax.dynamic_index_in_dim(mod_all, dev, axis=2, keepdims=False), (1, 0, 2)).reshape(2, 3 * D_MODEL)

    res = _local_step(
        x[0], loss_target[0], mod, norm_w,
        _even_in_layout(_chips_to_cols(wie_g)), _uq_layout(_chips_to_cols(wuq_g)), _uk_layout(b_w_uk[0].astype(BF16)),
        _uv_layout(b_w_uv[0].astype(BF16)), [halves(even_w_out[0]), halves(odd_w_in[0]), halves(odd_w_out[0])],
        a_q_norm, a_k_norm, b_q_lora_norm, b_kv_lora_norm, c_sink, final_norm)

    shard_parts = dict(zip(
        ("even_w_in", "b_w_uq"),
        _reduce_exchange([_cols_to_chips(_even_in_unlayout(res["even_w_in"].astype(BF16))),
                          _cols_to_chips(_uq_unlayout(res["b_w_uq"].astype(BF16)))]).run("reduce_exchange")))
    shard_parts.update({k: res[k] for k in ("even_w_out", "odd_w_in", "odd_w_out")})

    latent = jnp.stack([_uk_unlayout(res["b_w_uk"]).reshape(B_KV_LORA, 512),
                        _uv_unlayout(res["b_w_uv"]).reshape(B_KV_LORA, 512)]).astype(BF16)
    small_all, latent_all = _gather_dev8([_pack_small(res), latent], "gather_small")
    dmod_all = small_all[:, 0:6, :].reshape(N_DEV, 2, 3 * D_MODEL)
    dmod_cols = jnp.transpose(lax.dynamic_slice_in_dim(dmod_all, chip * n_ada, n_ada, axis=2), (1, 0, 2))
    parts = dict(shard_parts)
    parts["ada_w"] = _ada_bwd(c_all.T, dmod_cols).reshape(1, 2 * D_MODEL, n_ada)
    parts["b_w_uk"], parts["b_w_uv"] = latent_all[:, 0], latent_all[:, 1]

    def as2d(a):
        return a.reshape((-1, a.shape[-1]) if a.ndim > 1 else (1, a.shape[0]))

    results = {}
    small_outs = _adam_small(small_all, *[[as2d(given[pre + k]) for k in SMALL_WEIGHTS] for pre in ("", "m_", "v_")])
    for idx, k in enumerate(SMALL_WEIGHTS):
        results[k] = small_outs[4 * idx:4 * idx + 4]
    for k, p in parts.items():
        shape2 = (p.shape[-2], p.shape[-1])
        results[k] = _adam(p, given[k].reshape(shape2), given["m_" + k].reshape(shape2), given["v_" + k].reshape(shape2),
                           "adam_" + k)
    by_kind = [[results[k][t].reshape(given[k].shape) for k in WEIGHT_NAMES] for t in range(4)]
    loss = lax.psum(res["loss"], ("x", "y", "c"))
    return (loss, res["dx"][None], *by_kind[0], *by_kind[1], *by_kind[2], *by_kind[3])
```

```python
import functools

import numpy as np
import jax
import jax.numpy as jnp
from jax import lax
from jax.experimental import pallas as pl
from jax.experimental.pallas import tpu as pltpu

F32 = jnp.float32
BF16 = jnp.bfloat16
HIGHEST = lax.Precision.HIGHEST
MESH_ID = pl.DeviceIdType.MESH

D_MODEL = 1024
HEAD_DIM = 64
GRID_W = 64
EPS = 1e-6
ROPE_THETA = 10000.0
A_HEADS, A_KV_HEADS = 8, 2
B_HEADS, B_NOPE, B_ROPE, B_V = 8, 64, 32, 64
B_Q_LORA, B_KV_LORA = 256, 128
C_HEADS, C_KV_HEADS = 16, 4
WINDOW = 128
EVEN_IN, ODD_IN = 2208, 2560
EVEN_P = 2304
N_CHIPS, N_DEV = 4, 8
LANES = 128
NEG = -1e30
VMEM_LIMIT = 60 * 1024 * 1024

ADAM_LR, ADAM_B1, ADAM_B2, ADAM_EPS, ADAM_WD, ADAM_STEP = 0.001, 0.9, 0.999, 1e-08, 0.01, 10

ROW_TILE = 256


def _dot(a, b):
    return lax.dot_general(a, b, (((1,), (0,)), ((), ())), preferred_element_type=F32)


def _dot_nt(a, b):
    return lax.dot_general(a, b, (((1,), (1,)), ((), ())), preferred_element_type=F32)


def _dot_tn(a, b):
    return lax.dot_general(a, b, (((0,), (0,)), ((), ())), preferred_element_type=F32)


def _dot_f32(a, b):
    return lax.dot_general(a, b, (((1,), (0,)), ((), ())), precision=HIGHEST, preferred_element_type=F32)


def _sigmoid(x):
    return 1.0 / (1.0 + jnp.exp(-x))


def _silu_and_grad(g):
    s = _sigmoid(g)
    return g * s, s * (1.0 + g * (1.0 - s))


def _lane_iota():
    return lax.broadcasted_iota(jnp.int32, (1, LANES), 1)


def _partner(x, lane):
    return jnp.where((lane % 32) < 16, pltpu.roll(x, LANES - 16, 1), pltpu.roll(x, 16, 1))


def _rot(x, cos, sin_signed, lane):
    return x * cos + _partner(x, lane) * sin_signed


def _rot_bwd(dy, cos, sin_signed, lane):
    return dy * cos + _partner(dy * sin_signed, lane)


def _rms(x):
    return lax.rsqrt(jnp.mean(x * x, axis=-1, keepdims=True) + EPS)


def _rms_bwd(x, r, g):
    return r * g - x * (r * r * r) * jnp.mean(x * g, axis=-1, keepdims=True)


def _seg_mean(v, seg_ones):
    hi = v.astype(BF16)
    lo = (v - hi.astype(F32)).astype(BF16)
    return (_dot(hi, seg_ones) + _dot(lo, seg_ones)) * (1.0 / HEAD_DIM)


def _dup_heads(x, lane):
    swapped = pltpu.roll(x, HEAD_DIM, 1)
    lo = lane < HEAD_DIM
    return jnp.concatenate([jnp.where(lo, x, swapped), jnp.where(lo, swapped, x)], axis=1)


def _fold_heads(x2, lane):
    a, b = x2[:, 0:LANES], x2[:, LANES:2 * LANES]
    return jnp.where(lane < HEAD_DIM, a + pltpu.roll(a, HEAD_DIM, 1), b + pltpu.roll(b, HEAD_DIM, 1))


def _row_spec(ts, cols):
    return pl.BlockSpec((ts, cols), lambda i: (i, 0))


def _full_spec(shape, single=True):
    nd = len(shape)
    if single:
        return pl.BlockSpec(shape, lambda i: (0,) * nd, pipeline_mode=pl.Buffered(1))
    return pl.BlockSpec(shape, lambda i: (0,) * nd)


def _sds(shape, dtype):
    return jax.ShapeDtypeStruct(shape, dtype)


def _params(sem):
    return pltpu.CompilerParams(dimension_semantics=sem, vmem_limit_bytes=VMEM_LIMIT)


def _even_pre_fwd(x, mod, nw, wie, qn, kn, seg, ca, sa, ct, st, qln, kvln, wuq, wuk):
    S = x.shape[0]
    ts = min(ROW_TILE, S)

    def body(x_ref, mod_ref, nw_ref, wie_ref, qn_ref, kn_ref, seg_ref, ca_ref, sa_ref, ct_ref, st_ref, qln_ref,
             kvln_ref, wuq_ref, wuk_ref, h_ref, proj_ref, qa_ref, ka_ref, va_ref, qcat_ref, kcat_ref, kat_ref, vat_ref, kcatt_ref):
        xv = x_ref[...]
        h = (xv * _rms(xv) * nw_ref[...]) * (1.0 + mod_ref[1:2, :]) + mod_ref[0:1, :]
        hb = h.astype(BF16)
        h_ref[...] = hb
        proj = _dot(hb, wie_ref[...])
        proj_ref[...] = proj
        lane = _lane_iota()
        ca_v, sa_v, ct_v, st_v = ca_ref[...], sa_ref[...], ct_ref[...], st_ref[...]
        seg_v = seg_ref[...]
        for cb in range(4):
            xc = proj[:, LANES * cb:LANES * (cb + 1)]
            r = lax.rsqrt(_seg_mean(xc * xc, seg_v) + EPS)
            y = _rot(xc * r * qn_ref[...], ca_v, sa_v, lane)
            qa_ref[:, LANES * cb:LANES * (cb + 1)] = (y * 0.125).astype(BF16)
        kc = proj[:, 512:640]
        r = lax.rsqrt(_seg_mean(kc * kc, seg_v) + EPS)
        ka_v = _dup_heads(_rot(kc * r * kn_ref[...], ca_v, sa_v, lane), lane)
        ka_ref[...] = ka_v.astype(BF16)
        kat_ref[...] = ka_v.T.astype(BF16)
        va_v = _dup_heads(proj[:, 640:768], lane)
        va_ref[...] = va_v.astype(BF16)
        vat_ref[...] = va_v.T.astype(BF16)
        cq = proj[:, 1280:1536]
        cqn = (cq * _rms(cq) * qln_ref[...]).astype(BF16)
        ckv = proj[:, 1536:1664]
        ckvn = ckv * _rms(ckv) * kvln_ref[...]
        qb = _dot(cqn, wuq_ref[...])
        qlat = _dot(qb[:, 0:512].astype(BF16), wuk_ref[...])
        for hh in range(B_HEADS):
            qcat_ref[hh, :, 0:LANES] = qlat[:, LANES * hh:LANES * (hh + 1)].astype(BF16)
            qr = _rot(qb[:, 512 + LANES * hh:512 + LANES * (hh + 1)], ct_v, st_v, lane)
            qcat_ref[hh, :, LANES:2 * LANES] = qr.astype(BF16)
        kr = _rot(proj[:, 1664:1792], ct_v, st_v, lane)
        kcat_ref[:, 0:LANES] = ckvn.astype(BF16)
        kcat_ref[:, LANES:2 * LANES] = kr.astype(BF16)
        kcatt_ref[0:LANES, :] = ckvn.T.astype(BF16)
        kcatt_ref[LANES:2 * LANES, :] = kr.T.astype(BF16)

    col_spec = lambda rows: pl.BlockSpec((rows, ts), lambda i: (0, i))
    return pl.pallas_call(
        body, name="even_pre_fwd", grid=(S // ts,),
        in_specs=[_row_spec(ts, D_MODEL), _full_spec((3, D_MODEL)), _full_spec((1, D_MODEL)), _full_spec((D_MODEL, EVEN_P)),
                  _full_spec((1, LANES)), _full_spec((1, LANES)), _full_spec((LANES, LANES)),
                  _row_spec(ts, LANES), _row_spec(ts, LANES), _row_spec(ts, LANES), _row_spec(ts, LANES),
                  _full_spec((1, B_Q_LORA)), _full_spec((1, B_KV_LORA)), _full_spec((B_Q_LORA, 1536)), _full_spec((512, 1024))],
        out_specs=[_row_spec(ts, D_MODEL), _row_spec(ts, EVEN_P), _row_spec(ts, 512), _row_spec(ts, 2 * LANES), _row_spec(ts, 2 * LANES),
                   pl.BlockSpec((B_HEADS, ts, 2 * LANES), lambda i: (0, i, 0)), _row_spec(ts, 2 * LANES),
                   col_spec(2 * LANES), col_spec(2 * LANES), col_spec(2 * LANES)],
        out_shape=[_sds((S, D_MODEL), BF16), _sds((S, EVEN_P), F32), _sds((S, 512), BF16), _sds((S, 2 * LANES), BF16),
                   _sds((S, 2 * LANES), BF16), _sds((B_HEADS, S, 2 * LANES), BF16), _sds((S, 2 * LANES), BF16),
                   _sds((2 * LANES, S), BF16), _sds((2 * LANES, S), BF16), _sds((2 * LANES, S), BF16)],
        compiler_params=_params(("arbitrary",)),
    )(x, mod, nw, wie, qn, kn, seg, ca, sa, ct, st, qln, kvln, wuq, wuk)


MLA_SCALE = (B_NOPE + B_ROPE) ** -0.5
LOG2E = 1.4426950408889634

def _row_lo():
    return lax.broadcasted_iota(jnp.int32, (LANES, 1), 0) < HEAD_DIM


def _stack_cols(vT, rlo):
    zero = jnp.zeros_like(vT)
    return jnp.concatenate([jnp.where(rlo, vT, zero), jnp.where(rlo, zero, vT)], axis=1)


def _stack_rows(v, lo):
    zero = jnp.zeros_like(v)
    return jnp.concatenate([jnp.where(lo, v, zero), jnp.where(lo, zero, v)], axis=0)


def _pick_halves_T(xT, rlo, t):
    return jnp.where(rlo, xT[:, 0:t], xT[:, t:2 * t]).T


def _side_split(refs, n_in, n_out, n_scratch, side):
    ns = side.n if side is not None else 0
    cuts = np.cumsum([0, n_in, ns, n_out, ns, n_scratch])
    return [refs[a:b] for a, b in zip(cuts[:-1], cuts[1:])] + [refs[cuts[-1]:]]


def _side_hooks(side, side_ins, side_outs, side_sems, step, total):
    if side is None:
        return lambda: None
    start, mid, end = side.phases(side_ins, side_outs, side_sems)
    pl.when(step == 0)(start)
    pl.when(step == total // 2)(mid)
    return lambda: pl.when(step == total - 1)(end)


def _side_specs(side):
    if side is None:
        return [], [], [], [], []
    return list(side.arrs), [_ANY] * side.n, [_ANY] * side.n, list(side.out_shapes), side.sem_shapes()


def _pp_fwd(q, k, vT, *, kdiv, tq, tk, sub, name, side=None):
    S = k.shape[0]; nb = q.shape[1] // LANES; nq = S // tq; nkv = S // tk; nsub = tk // sub

    def body(*refs):
        (q_ref, k_ref, vT_ref), side_ins, (o_ref, lse_ref), side_outs, (qs, m_s, l_s, acc), side_sems = _side_split(refs, 3, 2, 4, side)
        j = pl.program_id(2)
        rlo = _row_lo()
        step = (pl.program_id(0) * nq + pl.program_id(1)) * nkv + j
        side_end = _side_hooks(side, side_ins, side_outs, side_sems, step, nb * nq * nkv)

        @pl.when(j == 0)
        def _():
            qs[...] = _stack_cols(q_ref[...].astype(F32).T, rlo).astype(BF16)
            m_s[...] = jnp.full((1, 2 * tq), NEG, F32)
            l_s[...] = jnp.zeros((1, 2 * tq), F32)
            acc[...] = jnp.zeros((LANES, 2 * tq), F32)

        qsv = qs[...]
        m, l, a = m_s[...], l_s[...], acc[...]
        s_cur = _dot(k_ref[0:sub, :], qsv)
        for t in range(nsub):
            if t + 1 < nsub:
                s_next = _dot(k_ref[sub * (t + 1):sub * (t + 2), :], qsv)
            m_new = jnp.maximum(m, jnp.max(s_cur, axis=0, keepdims=True))
            alpha = jnp.exp(m - m_new)
            p = jnp.exp(s_cur - m_new)
            l = alpha * l + jnp.sum(p, axis=0, keepdims=True)
            a = alpha * a + _dot(vT_ref[:, sub * t:sub * (t + 1)], p.astype(BF16))
            m = m_new
            if t + 1 < nsub:
                s_cur = s_next
        m_s[...], l_s[...], acc[...] = m, l, a

        @pl.when(j == nkv - 1)
        def _():
            l_f = l_s[...]
            o_ref[...] = _pick_halves_T(acc[...] / l_f, rlo, tq)
            lse_ref[0, 0] = m_s[...] + jnp.log(l_f)

        side_end()

    s_args, s_in, s_out, s_shapes, s_sems = _side_specs(side)
    return pl.pallas_call(
        body, name=name, grid=(nb, nq, nkv),
        in_specs=[pl.BlockSpec((tq, LANES), lambda b, i, j: (i, b)), pl.BlockSpec((tk, LANES), lambda b, i, j: (j, b // kdiv)),
                  pl.BlockSpec((LANES, tk), lambda b, i, j: (b // kdiv, j))] + s_in,
        out_specs=[pl.BlockSpec((tq, LANES), lambda b, i, j: (i, b)),
                   pl.BlockSpec((1, 1, 1, 2 * tq), lambda b, i, j: (b, i, 0, 0))] + s_out,
        out_shape=[_sds((S, nb * LANES), F32), _sds((nb, nq, 1, 2 * tq), F32)] + s_shapes,
        scratch_shapes=[pltpu.VMEM((LANES, 2 * tq), BF16), pltpu.VMEM((1, 2 * tq), F32), pltpu.VMEM((1, 2 * tq), F32),
                        pltpu.VMEM((LANES, 2 * tq), F32)] + s_sems,
        compiler_params=_params(("arbitrary",) * 3))(q, k, vT, *s_args)


def _pp_bwd(q, k, kT, v, o, do, lse, *, kdiv, tq, tk, sub, name, side=None):
    S = k.shape[0]; nb = q.shape[1] // LANES; nkb = k.shape[1] // LANES; nq = S // tq; nkv = S // tk; nsub = tk // sub

    def body(*refs):
        ((q_ref, k_ref, kT_ref, v_ref, o_ref, do_ref, lse_ref), side_ins, (dq_ref, dk_ref, dv_ref), side_outs,
         (qsT, qs, dosT, dos, delta_s, dq_acc), side_sems) = _side_split(refs, 7, 3, 6, side)
        b, i, j = pl.program_id(0), pl.program_id(1), pl.program_id(2)
        rlo = _row_lo()
        lo = lax.broadcasted_iota(jnp.int32, (1, LANES), 1) < HEAD_DIM
        side_end = _side_hooks(side, side_ins, side_outs, side_sems, (b * nq + i) * nkv + j, nb * nq * nkv)

        @pl.when((b % kdiv == 0) & (i == 0) & (j == 0))
        def _():
            dk_ref[...] = jnp.zeros((S, LANES), F32)
            dv_ref[...] = jnp.zeros((S, LANES), F32)

        @pl.when(j == 0)
        def _():
            qv = q_ref[...]
            qs[...] = _stack_rows(qv, lo)
            qsT[...] = _stack_cols(qv.astype(F32).T, rlo).astype(BF16)
            dov = do_ref[...]
            dos[...] = _stack_rows(dov.astype(BF16), lo)
            dosT[...] = _stack_cols(dov.T, rlo).astype(BF16)
            prodT = (dov * o_ref[...]).T
            delta_s[...] = jnp.concatenate([jnp.sum(jnp.where(rlo, prodT, 0.0), axis=0, keepdims=True),
                                            jnp.sum(jnp.where(rlo, 0.0, prodT), axis=0, keepdims=True)], axis=1)
            dq_acc[...] = jnp.zeros((LANES, 2 * tq), F32)

        qsTv, dosTv, qsv, dosv = qsT[...], dosT[...], qs[...], dos[...]
        lse_v, delta_v = lse_ref[0, 0], delta_s[...]
        dqa = dq_acc[...]
        s_cur = _dot(k_ref[0:sub, :], qsTv)
        dp_cur = _dot(v_ref[0:sub, :], dosTv)
        for t in range(nsub):
            if t + 1 < nsub:
                s_next = _dot(k_ref[sub * (t + 1):sub * (t + 2), :], qsTv)
                dp_next = _dot(v_ref[sub * (t + 1):sub * (t + 2), :], dosTv)
            p = jnp.exp(s_cur - lse_v)
            ds = (p * (dp_cur - delta_v)).astype(BF16)
            rows = pl.ds(pl.multiple_of(j * tk + sub * t, sub), sub)
            dv_ref[rows, :] += _dot(p.astype(BF16), dosv)
            dk_ref[rows, :] += _dot(ds, qsv)
            dqa = dqa + _dot(kT_ref[:, sub * t:sub * (t + 1)], ds)
            if t + 1 < nsub:
                s_cur, dp_cur = s_next, dp_next
        dq_acc[...] = dqa

        @pl.when(j == nkv - 1)
        def _():
            dq_ref[...] = _pick_halves_T(dq_acc[...], rlo, tq)

        side_end()

    qmap = lambda b, i, j: (i, b)
    kmap = lambda b, i, j: (j, b // kdiv)
    res = lambda b, i, j: (0, b // kdiv)
    s_args, s_in, s_out, s_shapes, s_sems = _side_specs(side)
    return pl.pallas_call(
        body, name=name, grid=(nb, nq, nkv),
        in_specs=[pl.BlockSpec((tq, LANES), qmap), pl.BlockSpec((tk, LANES), kmap), pl.BlockSpec((LANES, tk), lambda b, i, j: (b // kdiv, j)),
                  pl.BlockSpec((tk, LANES), kmap), pl.BlockSpec((tq, LANES), qmap), pl.BlockSpec((tq, LANES), qmap),
                  pl.BlockSpec((1, 1, 1, 2 * tq), lambda b, i, j: (b, i, 0, 0))] + s_in,
        out_specs=[pl.BlockSpec((tq, LANES), qmap), pl.BlockSpec((S, LANES), res), pl.BlockSpec((S, LANES), res)] + s_out,
        out_shape=[_sds((S, nb * LANES), F32), _sds((S, nkb * LANES), F32), _sds((S, nkb * LANES), F32)] + s_shapes,
        scratch_shapes=[pltpu.VMEM((LANES, 2 * tq), BF16), pltpu.VMEM((2 * tq, LANES), BF16), pltpu.VMEM((LANES, 2 * tq), BF16),
                        pltpu.VMEM((2 * tq, LANES), BF16), pltpu.VMEM((1, 2 * tq), F32), pltpu.VMEM((LANES, 2 * tq), F32)] + s_sems,
        compiler_params=_params(("arbitrary",) * 3))(q, k, kT, v, o, do, lse, *s_args)


MLA_C = MLA_SCALE * LOG2E


def _mla_fwd(q, kcat, kcatT, *, tq, tk, sub):
    S = kcat.shape[0]; nq, nkv = S // tq, S // tk; R = B_HEADS * tq; nsub = tk // sub

    def body(q_ref, k_ref, vT_ref, o_ref, lse_ref, qT, m_s, l_s, acc):
        j = pl.program_id(1)

        @pl.when(j == 0)
        def _():
            qT[...] = q_ref[...].reshape(R, 2 * LANES).astype(F32).T.astype(BF16)
            m_s[...] = jnp.full((1, R), NEG, F32)
            l_s[...] = jnp.zeros((1, R), F32)
            acc[...] = jnp.zeros((LANES, R), F32)

        qTv = qT[...]
        m, l, a = m_s[...], l_s[...], acc[...]
        s_cur = _dot(k_ref[0:sub, :], qTv)
        for t in range(nsub):
            if t + 1 < nsub:
                s_next = _dot(k_ref[sub * (t + 1):sub * (t + 2), :], qTv)
            m_new = jnp.maximum(m, jnp.max(s_cur, axis=0, keepdims=True))
            alpha = jnp.exp2((m - m_new) * MLA_C)
            p = jnp.exp2((s_cur - m_new) * MLA_C)
            l = alpha * l + jnp.sum(p, axis=0, keepdims=True)
            a = alpha * a + _dot(vT_ref[:, sub * t:sub * (t + 1)], p.astype(BF16))
            m = m_new
            if t + 1 < nsub:
                s_cur = s_next
        m_s[...], l_s[...], acc[...] = m, l, a

        @pl.when(j == nkv - 1)
        def _():
            l_f = l_s[...]
            o_ref[...] = (acc[...] / l_f).T.reshape(B_HEADS, tq, LANES)
            lse_ref[0] = m_s[...] * MLA_SCALE + jnp.log(l_f)

    return pl.pallas_call(
        body, name="mla_fwd", grid=(nq, nkv),
        in_specs=[pl.BlockSpec((B_HEADS, tq, 2 * LANES), lambda i, j: (0, i, 0)), pl.BlockSpec((tk, 2 * LANES), lambda i, j: (j, 0)),
                  pl.BlockSpec((LANES, tk), lambda i, j: (0, j))],
        out_specs=[pl.BlockSpec((B_HEADS, tq, LANES), lambda i, j: (0, i, 0)), pl.BlockSpec((1, 1, R), lambda i, j: (i, 0, 0))],
        out_shape=[_sds((B_HEADS, S, LANES), F32), _sds((nq, 1, R), F32)],
        scratch_shapes=[pltpu.VMEM((2 * LANES, R), BF16), pltpu.VMEM((1, R), F32), pltpu.VMEM((1, R), F32), pltpu.VMEM((LANES, R), F32)],
        compiler_params=_params(("arbitrary", "arbitrary")))(q, kcat, kcatT)


def _mla_bwd(q, kcat, kcatT, o, do, lse, *, tq, tk, sub):
    S = kcat.shape[0]; nq, nkv = S // tq, S // tk; R = B_HEADS * tq; nsub = tk // sub

    def body(q_ref, k_ref, kT_ref, o_ref, do_ref, lse_ref, dq_ref, dk_ref, qT, dosT, dos, delta_s, dq_acc):
        i, j = pl.program_id(0), pl.program_id(1)

        @pl.when((i == 0) & (j == 0))
        def _():
            dk_ref[...] = jnp.zeros((S, 2 * LANES), F32)

        @pl.when(j == 0)
        def _():
            qT[...] = q_ref[...].reshape(R, 2 * LANES).astype(F32).T.astype(BF16)
            dov = do_ref[...].reshape(R, LANES)
            dos[...] = dov.astype(BF16)
            dosT[...] = dov.T.astype(BF16)
            delta_s[...] = jnp.sum((dov * o_ref[...].reshape(R, LANES)).T, axis=0, keepdims=True)
            dq_acc[...] = jnp.zeros((2 * LANES, R), F32)

        qTv, dosTv, dosv = qT[...], dosT[...], dos[...]
        qv = q_ref[...].reshape(R, 2 * LANES)
        lse_v, delta_v = lse_ref[0] * LOG2E, delta_s[...]
        dqa = dq_acc[...]
        s_cur = _dot(k_ref[0:sub, :], qTv)
        dp_cur = _dot(k_ref[0:sub, 0:LANES], dosTv)
        for t in range(nsub):
            if t + 1 < nsub:
                s_next = _dot(k_ref[sub * (t + 1):sub * (t + 2), :], qTv)
                dp_next = _dot(k_ref[sub * (t + 1):sub * (t + 2), 0:LANES], dosTv)
            p = jnp.exp2(s_cur * MLA_C - lse_v)
            ds = (p * (dp_cur - delta_v) * MLA_SCALE).astype(BF16)
            rows = pl.ds(pl.multiple_of(j * tk + sub * t, sub), sub)
            dk_ref[rows, :] += _dot(ds, qv)
            dk_ref[rows, 0:LANES] += _dot(p.astype(BF16), dosv)
            dqa = dqa + _dot(kT_ref[:, sub * t:sub * (t + 1)], ds)
            if t + 1 < nsub:
                s_cur, dp_cur = s_next, dp_next
        dq_acc[...] = dqa

        @pl.when(j == nkv - 1)
        def _():
            dq_ref[...] = dq_acc[...].T.reshape(B_HEADS, tq, 2 * LANES)

    hspec = lambda w: pl.BlockSpec((B_HEADS, tq, w), lambda i, j: (0, i, 0))
    return pl.pallas_call(
        body, name="mla_bwd", grid=(nq, nkv),
        in_specs=[hspec(2 * LANES), pl.BlockSpec((tk, 2 * LANES), lambda i, j: (j, 0)), pl.BlockSpec((2 * LANES, tk), lambda i, j: (0, j)),
                  hspec(LANES), hspec(LANES), pl.BlockSpec((1, 1, R), lambda i, j: (i, 0, 0))],
        out_specs=[hspec(2 * LANES), pl.BlockSpec((S, 2 * LANES), lambda i, j: (0, 0))],
        out_shape=[_sds((B_HEADS, S, 2 * LANES), F32), _sds((S, 2 * LANES), F32)],
        scratch_shapes=[pltpu.VMEM((2 * LANES, R), BF16), pltpu.VMEM((LANES, R), BF16), pltpu.VMEM((R, LANES), BF16),
                        pltpu.VMEM((1, R), F32), pltpu.VMEM((2 * LANES, R), F32)],
        compiler_params=_params(("arbitrary", "arbitrary")))(q, kcat, kcatT, o, do, lse)


def _win_start(i, tq, nk, S):
    return pl.multiple_of(jnp.clip(i * tq - WINDOW, 0, S - nk), LANES)


def _win_dist_table(S, tq):
    nk = min(tq + 2 * WINDOW, S)
    nq = S // tq
    r = np.arange(nk)[:, None]
    c = (np.arange(2 * tq) % tq)[None, :]
    tabs = []
    for rel in (0, WINDOW, (nq - 1) * tq - (S - nk)):
        dist = np.abs(rel + c - r).astype(np.float32)
        tabs.append(np.where(dist <= WINDOW, dist, np.float32(1e32)))
    return jnp.asarray(np.stack(tabs))


def _win_dist_spec(nk, tq, nq):
    return pl.BlockSpec((1, nk, 2 * tq), lambda b, i: (jnp.where(i == 0, 0, jnp.where(i == nq - 1, 2, 1)), 0, 0))


def _win_fwd(q, k, vT, dist, slope, sink, *, kdiv, tq, nbs, name):
    S = k.shape[0]; nb = q.shape[1] // LANES; nq = S // tq; nk = min(tq + 2 * WINDOW, S)
    assert nb % nbs == 0 and nbs % kdiv == 0
    kvw = (nbs // kdiv) * LANES

    def body(q_ref, k_ref, vT_ref, dist_ref, slope_ref, sink_ref, o_ref, lse_ref):
        i = pl.program_id(1)
        rlo = _row_lo()
        k0 = _win_start(i, tq, nk, S)
        kk, vv, dd = k_ref[pl.ds(k0, nk), :], vT_ref[:, pl.ds(k0, nk)], dist_ref[0]
        for u in range(nbs):
            kv = slice(LANES * (u // kdiv), LANES * (u // kdiv + 1))
            qsT = _stack_cols(q_ref[:, LANES * u:LANES * (u + 1)].astype(F32).T, rlo).astype(BF16)
            s = _dot(kk[:, kv], qsT) - slope_ref[u] * dd
            sk = sink_ref[u]
            m = jnp.maximum(jnp.max(s, axis=0, keepdims=True), sk)
            p = jnp.exp(s - m)
            l = jnp.sum(p, axis=0, keepdims=True) + jnp.exp(sk - m)
            o_ref[:, LANES * u:LANES * (u + 1)] = _pick_halves_T(_dot(vv[kv, :], p.astype(BF16)) / l, rlo, tq)
            lse_ref[u, 0] = m + jnp.log(l)

    row_spec = pl.BlockSpec((nbs, 1, 2 * tq), lambda b, i: (b, 0, 0))
    return pl.pallas_call(
        body, name=name, grid=(nb // nbs, nq),
        in_specs=[pl.BlockSpec((tq, nbs * LANES), lambda b, i: (i, b)), pl.BlockSpec((S, kvw), lambda b, i: (0, b)),
                  pl.BlockSpec((kvw, S), lambda b, i: (b, 0)), _win_dist_spec(nk, tq, nq), row_spec, row_spec],
        out_specs=[pl.BlockSpec((tq, nbs * LANES), lambda b, i: (i, b)), pl.BlockSpec((nbs, 1, 1, 2 * tq), lambda b, i: (b, i, 0, 0))],
        out_shape=[_sds((S, nb * LANES), F32), _sds((nb, nq, 1, 2 * tq), F32)],
        compiler_params=_params(("arbitrary", "arbitrary")))(q, k, vT, dist, slope, sink)


def _win_bwd(q, k, kT, v, o, do, lse, dist, slope, sink, *, kdiv, tq, nbs, name):
    S = k.shape[0]; nb = q.shape[1] // LANES; nkb = k.shape[1] // LANES; nq = S // tq; nk = min(tq + 2 * WINDOW, S)
    assert nb % nbs == 0 and nbs % kdiv == 0
    nkv = nbs // kdiv
    kvw = nkv * LANES

    def body(q_ref, k_ref, kT_ref, v_ref, o_ref, do_ref, lse_ref, dist_ref, slope_ref, sink_ref, dq_ref, dk_ref, dv_ref, dsink_ref, ds_acc):
        i = pl.program_id(1)
        rlo = _row_lo()
        lo = lax.broadcasted_iota(jnp.int32, (1, LANES), 1) < HEAD_DIM

        @pl.when(i == 0)
        def _():
            dk_ref[...] = jnp.zeros((S, kvw), F32)
            dv_ref[...] = jnp.zeros((S, kvw), F32)
            ds_acc[...] = jnp.zeros((nbs, 2 * tq), F32)

        k0 = _win_start(i, tq, nk, S)
        rows = pl.ds(k0, nk)
        kk_all, vv_all, kkT_all, dd = k_ref[rows, :], v_ref[rows, :], kT_ref[:, rows], dist_ref[0]
        dv_sum, dk_sum = [None] * nkv, [None] * nkv
        for u in range(nbs):
            g = u // kdiv
            kv = slice(LANES * g, LANES * (g + 1))
            kk, vv, kkT = kk_all[:, kv], vv_all[:, kv], kkT_all[kv, :]
            cols = slice(LANES * u, LANES * (u + 1))
            qv = q_ref[:, cols]
            qs = _stack_rows(qv, lo)
            qsT = _stack_cols(qv.astype(F32).T, rlo).astype(BF16)
            dov = do_ref[:, cols]
            dos = _stack_rows(dov.astype(BF16), lo)
            dosT = _stack_cols(dov.T, rlo).astype(BF16)
            prodT = (dov * o_ref[:, cols]).T
            delta = jnp.concatenate([jnp.sum(jnp.where(rlo, prodT, 0.0), axis=0, keepdims=True),
                                     jnp.sum(jnp.where(rlo, 0.0, prodT), axis=0, keepdims=True)], axis=1)
            lse_v = lse_ref[u, 0]
            ds_acc[u:u + 1, :] += -jnp.exp(sink_ref[u] - lse_v) * delta
            p = jnp.exp(_dot(kk, qsT) - slope_ref[u] * dd - lse_v)
            ds = (p * (_dot(vv, dosT) - delta)).astype(BF16)
            dv_u, dk_u = _dot(p.astype(BF16), dos), _dot(ds, qs)
            dv_sum[g] = dv_u if dv_sum[g] is None else dv_sum[g] + dv_u
            dk_sum[g] = dk_u if dk_sum[g] is None else dk_sum[g] + dk_u
            dq_ref[:, cols] = (_pick_halves_T(_dot(kkT, ds), rlo, tq) * 0.125).astype(BF16)
        dv_ref[rows, :] += jnp.concatenate(dv_sum, axis=1)
        dk_ref[rows, :] += jnp.concatenate(dk_sum, axis=1)

        @pl.when(i == nq - 1)
        def _():
            acc = ds_acc[...]
            for u in range(nbs):
                dsink_ref[u] = jnp.concatenate(
                    [jnp.broadcast_to(jnp.sum(acc[u:u + 1, 0:tq], axis=1, keepdims=True), (1, LANES)),
                     jnp.broadcast_to(jnp.sum(acc[u:u + 1, tq:2 * tq], axis=1, keepdims=True), (1, LANES)),
                     jnp.zeros((6, LANES), F32)], axis=0)

    qmap = lambda b, i: (i, b)
    kv_spec = pl.BlockSpec((S, kvw), lambda b, i: (0, b))
    row_spec = pl.BlockSpec((nbs, 1, 2 * tq), lambda b, i: (b, 0, 0))
    wide = pl.BlockSpec((tq, nbs * LANES), qmap)
    return pl.pallas_call(
        body, name=name, grid=(nb // nbs, nq),
        in_specs=[wide, kv_spec, pl.BlockSpec((kvw, S), lambda b, i: (b, 0)), kv_spec, wide, wide,
                  pl.BlockSpec((nbs, 1, 1, 2 * tq), lambda b, i: (b, i, 0, 0)), _win_dist_spec(nk, tq, nq), row_spec, row_spec],
        out_specs=[wide, kv_spec, kv_spec, pl.BlockSpec((nbs, 8, LANES), lambda b, i: (b, 0, 0))],
        out_shape=[_sds((S, nb * LANES), BF16), _sds((S, nkb * LANES), F32), _sds((S, nkb * LANES), F32), _sds((nb, 8, LANES), F32)],
        scratch_shapes=[pltpu.VMEM((nbs, 2 * tq), F32)],
        compiler_params=_params(("arbitrary", "arbitrary")))(q, k, kT, v, o, do, lse, dist, slope, sink)


def _sum_rows(v):
    return jnp.sum(v, axis=0, keepdims=True)


def _norm_mod_bwd(dh, xv, mod_ref, nw_ref, stats_ref):
    r = _rms(xv)
    xn = xv * r
    nw = nw_ref[...]
    stats_ref[0:1, :] += _sum_rows(dh)
    stats_ref[1:2, :] += _sum_rows(dh * (xn * nw))
    dn = dh * (1.0 + mod_ref[1:2, :])
    stats_ref[2:3, :] += _sum_rows(dn * xn)
    return _rms_bwd(xv, r, dn * nw)


def _even_gate_specs(ts):
    return [pl.BlockSpec((ts, 256), lambda i, c=c: (i, c)) for c in (3, 4, 7, 8)]


def _even_post_fwd(oa, olat, proj, x, gate, wuv, woe):
    S = x.shape[0]
    ts = min(ROW_TILE, S)

    def body(oa_ref, ol_ref, ga0_ref, ga1_ref, gb0_ref, gb1_ref, x_ref, gate_ref, wuv_ref, woe_ref, y_ref, x1_ref):
        sa, _ = _silu_and_grad(jnp.concatenate([ga0_ref[...], ga1_ref[...]], axis=1))
        sb, _ = _silu_and_grad(jnp.concatenate([gb0_ref[...], gb1_ref[...]], axis=1))
        olc = jnp.concatenate([ol_ref[hh] for hh in range(B_HEADS)], axis=1).astype(BF16)
        ob = _dot(olc, wuv_ref[...])
        mix = jnp.concatenate([oa_ref[...] * sa, ob * sb], axis=1).astype(BF16)
        y = _dot(mix, woe_ref[...])
        y_ref[...] = y
        x1_ref[...] = x_ref[...] + gate_ref[...] * y

    return pl.pallas_call(
        body, name="even_post_fwd", grid=(S // ts,),
        in_specs=[_row_spec(ts, 512), pl.BlockSpec((B_HEADS, ts, LANES), lambda i: (0, i, 0))] + _even_gate_specs(ts) +
                 [_row_spec(ts, D_MODEL), _full_spec((1, D_MODEL)), _full_spec((1024, 512)), _full_spec((1024, D_MODEL))],
        out_specs=[_row_spec(ts, D_MODEL), _row_spec(ts, D_MODEL)],
        out_shape=[_sds((S, D_MODEL), F32), _sds((S, D_MODEL), F32)],
        compiler_params=_params(("arbitrary",)),
    )(oa, olat, proj, proj, proj, proj, x, gate, wuv, woe)


def _odd_pre_fwd(x, mod, nw, wio):
    S = x.shape[0]
    ts = min(ROW_TILE, S)

    def body(x_ref, mod_ref, nw_ref, wio_ref, h_ref, proj_ref, q_ref, k_ref, v_ref, kt_ref, vt_ref):
        xv = x_ref[...]
        h = (xv * _rms(xv) * nw_ref[...]) * (1.0 + mod_ref[1:2, :]) + mod_ref[0:1, :]
        hb = h.astype(BF16)
        h_ref[...] = hb
        proj = jnp.concatenate([_dot(hb, wio_ref[p]) for p in range(N_CHIPS)], axis=1)
        proj_ref[...] = proj
        q_ref[...] = (proj[:, 0:1024] * 0.125).astype(BF16)
        lane = _lane_iota()
        k_v = jnp.concatenate([_dup_heads(proj[:, 1024 + LANES * j:1024 + LANES * (j + 1)], lane) for j in range(2)], axis=1)
        v_v = jnp.concatenate([_dup_heads(proj[:, 1280 + LANES * j:1280 + LANES * (j + 1)], lane) for j in range(2)], axis=1)
        k_ref[...] = k_v.astype(BF16)
        v_ref[...] = v_v.astype(BF16)
        kt_ref[...] = k_v.T.astype(BF16)
        vt_ref[...] = v_v.T.astype(BF16)

    col_spec = pl.BlockSpec((512, ts), lambda i: (0, i))
    return pl.pallas_call(
        body, name="odd_pre_fwd", grid=(S // ts,),
        in_specs=[_row_spec(ts, D_MODEL), _full_spec((3, D_MODEL)), _full_spec((1, D_MODEL)),
                  _full_spec((N_CHIPS, D_MODEL, ODD_IN // N_CHIPS))],
        out_specs=[_row_spec(ts, D_MODEL), _row_spec(ts, ODD_IN), _row_spec(ts, 1024), _row_spec(ts, 512), _row_spec(ts, 512),
                   col_spec, col_spec],
        out_shape=[_sds((S, D_MODEL), BF16), _sds((S, ODD_IN), F32), _sds((S, 1024), BF16), _sds((S, 512), BF16),
                   _sds((S, 512), BF16), _sds((512, S), BF16), _sds((512, S), BF16)],
        compiler_params=_params(("arbitrary",)),
    )(x, mod, nw, wio)


def _odd_post(oc, proj, x1, gate, woo, fw, tgt):
    S = x1.shape[0]
    ts = min(ROW_TILE, S)
    nsteps = S // ts

    def body(oc_ref, g0_ref, g1_ref, x_ref, gate_ref, woo_ref, fw_ref, tgt_ref, doc_ref, dgc_ref, dx2_ref, dwoo_out, stats_ref,
             dwoo_ref):
        @pl.when(pl.program_id(0) == 0)
        def _():
            dwoo_ref[...] = jnp.zeros((D_MODEL, D_MODEL), F32)
            stats_ref[...] = jnp.zeros((8, D_MODEL), F32)

        ocv = oc_ref[...]
        sg, dsg = _silu_and_grad(jnp.concatenate([g0_ref[...], g1_ref[...]], axis=1))
        mix = (ocv * sg).astype(BF16)
        woo_v = woo_ref[...]
        y = _dot(mix, woo_v)
        gate_v = gate_ref[...]
        x2 = x_ref[...] + gate_v * y
        r = _rms(x2)
        fw_v = fw_ref[...]
        xn = x2 * r
        err = xn * fw_v - tgt_ref[...]
        dout = err * (1.0 / D_MODEL)
        dx2 = _rms_bwd(x2, r, dout * fw_v)
        dx2_ref[...] = dx2
        stats_ref[0:1, :] += _sum_rows(dout * xn)
        stats_ref[1:2, :] += _sum_rows(dx2 * y)
        loss_t = 0.5 * jnp.sum(_sum_rows(err * dout), axis=-1, keepdims=True)
        stats_ref[2:3, :] += jnp.broadcast_to(loss_t, (1, D_MODEL))
        dy = (gate_v * dx2).astype(BF16)
        dmix = _dot_nt(dy, woo_v)
        dwoo_ref[...] += _dot_tn(mix, dy)
        doc_ref[...] = dmix * sg
        dgc_ref[...] = (dmix * ocv * dsg).astype(BF16)

        @pl.when(pl.program_id(0) == nsteps - 1)
        def _():
            dwoo_out[...] = dwoo_ref[...].astype(BF16)

    gate_cols = [pl.BlockSpec((ts, 512), lambda i, c=c: (i, c)) for c in (3, 4)]
    return pl.pallas_call(
        body, name="odd_post", grid=(nsteps,),
        in_specs=[_row_spec(ts, D_MODEL)] + gate_cols + [_row_spec(ts, D_MODEL), _full_spec((1, D_MODEL)),
                  _full_spec((D_MODEL, D_MODEL)), _full_spec((1, D_MODEL)), _row_spec(ts, D_MODEL)],
        out_specs=[_row_spec(ts, D_MODEL), _row_spec(ts, D_MODEL), _row_spec(ts, D_MODEL),
                   _full_spec((D_MODEL, D_MODEL), single=False), _full_spec((8, D_MODEL), single=False)],
        out_shape=[_sds((S, D_MODEL), F32), _sds((S, D_MODEL), BF16), _sds((S, D_MODEL), F32), _sds((D_MODEL, D_MODEL), BF16),
                   _sds((8, D_MODEL), F32)],
        scratch_shapes=[pltpu.VMEM((D_MODEL, D_MODEL), F32)],
        compiler_params=_params(("arbitrary",)),
    )(oc, proj, proj, x1, gate, woo, fw, tgt)


def _odd_pre_bwd(dq, dk, dv, dgc, h, x, dx_res, mod, nw, wio):
    S = x.shape[0]
    ts = min(ROW_TILE, S)
    nsteps = S // ts
    wsh = ODD_IN // N_CHIPS

    def body(dq_ref, dk_ref, dv_ref, dgc_ref, h_ref, x_ref, dxr_ref, mod_ref, nw_ref, wio_ref, dx_ref, dw_ref, stats_ref, dw_acc):
        @pl.when(pl.program_id(0) == 0)
        def _():
            dw_acc[...] = jnp.zeros((N_CHIPS, D_MODEL, wsh), F32)
            stats_ref[...] = jnp.zeros((8, D_MODEL), F32)

        lane = _lane_iota()
        dkv = [_fold_heads(r[:, 2 * LANES * j:2 * LANES * (j + 1)], lane).astype(BF16) for r in (dk_ref, dv_ref) for j in range(2)]
        dproj = jnp.concatenate([dq_ref[...]] + dkv + [dgc_ref[...]], axis=1)
        hv = h_ref[...]
        dh = None
        for p in range(N_CHIPS):
            dp_cols = dproj[:, wsh * p:wsh * (p + 1)]
            part = _dot_nt(dp_cols, wio_ref[p])
            dh = part if dh is None else dh + part
            dw_acc[p] += _dot_tn(hv, dp_cols)
        dx_ref[...] = dxr_ref[...] + _norm_mod_bwd(dh, x_ref[...], mod_ref, nw_ref, stats_ref)

        @pl.when(pl.program_id(0) == nsteps - 1)
        def _():
            dw_ref[...] = dw_acc[...].astype(BF16)

    return pl.pallas_call(
        body, name="odd_pre_bwd", grid=(nsteps,),
        in_specs=[_row_spec(ts, 1024), _row_spec(ts, 512), _row_spec(ts, 512), _row_spec(ts, 1024), _row_spec(ts, D_MODEL),
                  _row_spec(ts, D_MODEL), _row_spec(ts, D_MODEL), _full_spec((3, D_MODEL)), _full_spec((1, D_MODEL)),
                  _full_spec((N_CHIPS, D_MODEL, wsh))],
        out_specs=[_row_spec(ts, D_MODEL), _full_spec((N_CHIPS, D_MODEL, wsh), single=False), _full_spec((8, D_MODEL), single=False)],
        out_shape=[_sds((S, D_MODEL), F32), _sds((N_CHIPS, D_MODEL, wsh), BF16), _sds((8, D_MODEL), F32)],
        scratch_shapes=[pltpu.VMEM((N_CHIPS, D_MODEL, wsh), F32)],
        compiler_params=_params(("arbitrary",)),
    )(dq, dk, dv, dgc, h, x, dx_res, mod, nw, wio)


def _even_post_bwd(dx1, y, oa, olat, proj, gate, wuv, woe):
    S = dx1.shape[0]
    ts = min(ROW_TILE, S)
    nsteps = S // ts

    def body(dx_ref, y_ref, oa_ref, ol_ref, ga0_ref, ga1_ref, gb0_ref, gb1_ref, gate_ref, wuv_ref, woe_ref,
             doa_ref, dga_ref, dgb_ref, dol_ref, dwoe_out, dwuv_ref, stats_ref, dwoe_ref):
        @pl.when(pl.program_id(0) == 0)
        def _():
            dwoe_ref[...] = jnp.zeros((D_MODEL, D_MODEL), F32)
            dwuv_ref[...] = jnp.zeros((1024, 512), F32)
            stats_ref[...] = jnp.zeros((8, D_MODEL), F32)

        dxv = dx_ref[...]
        stats_ref[0:1, :] += _sum_rows(dxv * y_ref[...])
        dy = (gate_ref[...] * dxv).astype(BF16)
        sa, dsa = _silu_and_grad(jnp.concatenate([ga0_ref[...], ga1_ref[...]], axis=1))
        sb, dsb = _silu_and_grad(jnp.concatenate([gb0_ref[...], gb1_ref[...]], axis=1))
        olc = jnp.concatenate([ol_ref[hh] for hh in range(B_HEADS)], axis=1).astype(BF16)
        wuv_v = wuv_ref[...]
        ob = _dot(olc, wuv_v)
        oav = oa_ref[...]
        mix = jnp.concatenate([oav * sa, ob * sb], axis=1).astype(BF16)
        dmix = _dot_nt(dy, woe_ref[...])
        dwoe_ref[...] += _dot_tn(mix, dy)
        dma, dmb = dmix[:, 0:512], dmix[:, 512:1024]
        doa_ref[...] = dma * sa
        dga_ref[...] = (dma * oav * dsa).astype(BF16)
        dgb_ref[...] = (dmb * ob * dsb).astype(BF16)
        dob = (dmb * sb).astype(BF16)
        dol = _dot_nt(dob, wuv_v)
        dwuv_ref[...] += _dot_tn(olc, dob)
        for hh in range(B_HEADS):
            dol_ref[hh] = dol[:, LANES * hh:LANES * (hh + 1)]

        @pl.when(pl.program_id(0) == nsteps - 1)
        def _():
            dwoe_out[...] = dwoe_ref[...].astype(BF16)

    head_spec = pl.BlockSpec((B_HEADS, ts, LANES), lambda i: (0, i, 0))
    return pl.pallas_call(
        body, name="even_post_bwd", grid=(nsteps,),
        in_specs=[_row_spec(ts, D_MODEL), _row_spec(ts, D_MODEL), _row_spec(ts, 512), head_spec] + _even_gate_specs(ts) +
                 [_full_spec((1, D_MODEL)), _full_spec((1024, 512)), _full_spec((1024, D_MODEL))],
        out_specs=[_row_spec(ts, 512), _row_spec(ts, 512), _row_spec(ts, 512), head_spec,
                   _full_spec((D_MODEL, D_MODEL), single=False), _full_spec((1024, 512), single=False),
                   _full_spec((8, D_MODEL), single=False)],
        out_shape=[_sds((S, 512), F32), _sds((S, 512), BF16), _sds((S, 512), BF16), _sds((B_HEADS, S, LANES), F32),
                   _sds((D_MODEL, D_MODEL), BF16), _sds((1024, 512), F32), _sds((8, D_MODEL), F32)],
        scratch_shapes=[pltpu.VMEM((D_MODEL, D_MODEL), F32)],
        compiler_params=_params(("arbitrary",)),
    )(dx1, y, oa, olat, proj, proj, proj, proj, gate, wuv, woe)


def _even_pre_bwd(x, h, proj, dqa, dka, dva, dga, dgb, dqcat, dkcat, dx_res, mod, nw, wie, qn, kn, seg, ca, sa, ct, st,
                  qln, kvln, wuq, wuk):
    S = x.shape[0]
    ts = min(ROW_TILE, S)
    nsteps = S // ts

    def body(x_ref, h_ref, proj_ref, dqa_ref, dka_ref, dva_ref, dga_ref, dgb_ref, dqc_ref, dkc_ref, dxr_ref, mod_ref, nw_ref,
             wie_ref, qn_ref, kn_ref, seg_ref, ca_ref, sa_ref, ct_ref, st_ref, qln_ref, kvln_ref, wuq_ref, wuk_ref,
             dx_ref, dwie_out, dwuq_out, dwuk_out, stats_ref, nstats_ref, dwie_ref, dwuq_ref, dwuk_ref):
        @pl.when(pl.program_id(0) == 0)
        def _():
            dwie_ref[...] = jnp.zeros((D_MODEL, EVEN_P), F32)
            dwuq_ref[...] = jnp.zeros((B_Q_LORA, 1536), F32)
            dwuk_ref[...] = jnp.zeros((512, 1024), F32)
            stats_ref[...] = jnp.zeros((8, D_MODEL), F32)
            nstats_ref[...] = jnp.zeros((8, 256), F32)

        lane = _lane_iota()
        ca_v, sa_v, ct_v, st_v = ca_ref[...], sa_ref[...], ct_ref[...], st_ref[...]
        seg_v = seg_ref[...]

        def head_norm_bwd(xc, dy, w):
            r = lax.rsqrt(_seg_mean(xc * xc, seg_v) + EPS)
            g = dy * w
            dxc = r * g - xc * (r * r * r) * _seg_mean(xc * g, seg_v)
            return dxc, _sum_rows(dy * (xc * r))

        pieces = []
        dqn = jnp.zeros((1, LANES), F32)
        for cb in range(4):
            sl = slice(LANES * cb, LANES * (cb + 1))
            dy = _rot_bwd(dqa_ref[:, sl] * 0.125, ca_v, sa_v, lane)
            dxc, dw = head_norm_bwd(proj_ref[:, sl], dy, qn_ref[...])
            pieces.append(dxc)
            dqn = dqn + dw
        dxc, dkn = head_norm_bwd(proj_ref[:, 512:640], _rot_bwd(_fold_heads(dka_ref[...], lane), ca_v, sa_v, lane), kn_ref[...])
        pieces += [dxc, _fold_heads(dva_ref[...], lane), dga_ref[...]]
        nstats_ref[0:1, 0:LANES] += dqn + pltpu.roll(dqn, HEAD_DIM, 1)
        nstats_ref[1:2, 0:LANES] += dkn + pltpu.roll(dkn, HEAD_DIM, 1)

        cq = proj_ref[:, 1280:1536]
        rq = _rms(cq)
        cqn_f = cq * rq
        qln_v = qln_ref[...]
        cqn = (cqn_f * qln_v).astype(BF16)
        wuq_v, wuk_v = wuq_ref[...], wuk_ref[...]
        qnope = _dot(cqn, wuq_v[:, 0:512]).astype(BF16)
        dqlat = jnp.concatenate([dqc_ref[hh, :, 0:LANES] for hh in range(B_HEADS)], axis=1).astype(BF16)
        dqnope = _dot_nt(dqlat, wuk_v)
        dwuk_ref[...] += _dot_tn(qnope, dqlat)
        dqr = [_rot_bwd(dqc_ref[hh, :, LANES:2 * LANES], ct_v, st_v, lane) for hh in range(B_HEADS)]
        dqb = jnp.concatenate([dqnope] + dqr, axis=1).astype(BF16)
        dcqn = _dot_nt(dqb, wuq_v)
        dwuq_ref[...] += _dot_tn(cqn, dqb)
        nstats_ref[2:3, :] += _sum_rows(dcqn * cqn_f)
        dcq = _rms_bwd(cq, rq, dcqn * qln_v)
        ckv = proj_ref[:, 1536:1664]
        rk = _rms(ckv)
        dckvn = dkc_ref[:, 0:LANES]
        nstats_ref[3:4, 0:LANES] += _sum_rows(dckvn * (ckv * rk))
        dckv = _rms_bwd(ckv, rk, dckvn * kvln_ref[...])
        dkr = _rot_bwd(dkc_ref[:, LANES:2 * LANES], ct_v, st_v, lane)
        pieces += [dcq, dckv, dkr, dgb_ref[...]]
        dproj = jnp.concatenate([piece.astype(BF16) for piece in pieces], axis=1)
        dh = _dot_nt(dproj, wie_ref[...])
        dwie_ref[...] += _dot_tn(h_ref[...], dproj)
        dx_ref[...] = dxr_ref[...] + _norm_mod_bwd(dh, x_ref[...], mod_ref, nw_ref, stats_ref)

        @pl.when(pl.program_id(0) == nsteps - 1)
        def _():
            pltpu.sync_copy(dwie_ref, dwie_out)
            pltpu.sync_copy(dwuq_ref, dwuq_out)
            pltpu.sync_copy(dwuk_ref, dwuk_out)

    return pl.pallas_call(
        body, name="even_pre_bwd", grid=(nsteps,),
        in_specs=[_row_spec(ts, D_MODEL), _row_spec(ts, D_MODEL), _row_spec(ts, EVEN_P), _row_spec(ts, 512), _row_spec(ts, 2 * LANES),
                  _row_spec(ts, 2 * LANES), _row_spec(ts, 512), _row_spec(ts, 512),
                  pl.BlockSpec((B_HEADS, ts, 2 * LANES), lambda i: (0, i, 0)), _row_spec(ts, 2 * LANES), _row_spec(ts, D_MODEL),
                  _full_spec((3, D_MODEL)), _full_spec((1, D_MODEL)), _full_spec((D_MODEL, EVEN_P)),
                  _full_spec((1, LANES)), _full_spec((1, LANES)), _full_spec((LANES, LANES)),
                  _row_spec(ts, LANES), _row_spec(ts, LANES), _row_spec(ts, LANES), _row_spec(ts, LANES),
                  _full_spec((1, B_Q_LORA)), _full_spec((1, B_KV_LORA)), _full_spec((B_Q_LORA, 1536)), _full_spec((512, 1024))],
        out_specs=[_row_spec(ts, D_MODEL), _ANY, _ANY, _ANY, _full_spec((8, D_MODEL), single=False), _full_spec((8, 256), single=False)],
        out_shape=[_sds((S, D_MODEL), F32), _sds((D_MODEL, EVEN_P), F32), _sds((B_Q_LORA, 1536), F32), _sds((512, 1024), F32),
                   _sds((8, D_MODEL), F32), _sds((8, 256), F32)],
        scratch_shapes=[pltpu.VMEM((D_MODEL, EVEN_P), F32), pltpu.VMEM((B_Q_LORA, 1536), F32), pltpu.VMEM((512, 1024), F32)],
        compiler_params=_params(("arbitrary",)),
    )(x, h, proj, dqa, dka, dva, dga, dgb, dqcat, dkcat, dx_res, mod, nw, wie, qn, kn, seg, ca, sa, ct, st, qln, kvln, wuq, wuk)


def _ada_fwd(c_all, w, b):
    n = w.shape[2]

    def body(c_ref, w_ref, b_ref, o_ref):
        cv = c_ref[...]
        o_ref[0] = _dot_f32(cv * _sigmoid(cv), w_ref[0]) + b_ref[0]

    return pl.pallas_call(
        body, name="ada_fwd", grid=(2,),
        in_specs=[pl.BlockSpec((N_DEV, D_MODEL), lambda l: (0, 0)), pl.BlockSpec((1, D_MODEL, n), lambda l: (l, 0, 0)),
                  pl.BlockSpec((1, 1, n), lambda l: (l, 0, 0))],
        out_specs=pl.BlockSpec((1, N_DEV, n), lambda l: (l, 0, 0)),
        out_shape=_sds((2, N_DEV, n), F32),
        compiler_params=_params(("arbitrary",)),
    )(c_all, w, b)


def _ada_bwd(c_all_t, dmod):
    n = dmod.shape[2]

    def body(c_ref, d_ref, o_ref):
        cv = c_ref[...]
        act = cv * _sigmoid(cv)
        dv = d_ref[0]
        acc = act[:, 0:1] * dv[0:1, :]
        for bb in range(1, N_DEV):
            acc = acc + act[:, bb:bb + 1] * dv[bb:bb + 1, :]
        o_ref[0] = acc

    return pl.pallas_call(
        body, name="ada_bwd", grid=(2,),
        in_specs=[pl.BlockSpec((D_MODEL, N_DEV), lambda l: (0, 0)), pl.BlockSpec((1, N_DEV, n), lambda l: (l, 0, 0))],
        out_specs=pl.BlockSpec((1, D_MODEL, n), lambda l: (l, 0, 0)),
        out_shape=_sds((2, D_MODEL, n), F32),
        compiler_params=_params(("arbitrary",)),
    )(c_all_t, dmod)


ADAM_ROW_TILE = 256


def _adam_update(g, w, m, v):
    m_new = ADAM_B1 * m + (1.0 - ADAM_B1) * g
    v_new = ADAM_B2 * v + (1.0 - ADAM_B2) * jnp.square(g)
    m_hat = m_new / (1.0 - ADAM_B1 ** ADAM_STEP)
    v_hat = v_new / (1.0 - ADAM_B2 ** ADAM_STEP)
    return -ADAM_LR * (m_hat / (jnp.sqrt(v_hat) + ADAM_EPS) + ADAM_WD * w), m_new, v_new


SMALL_ROWS = dict(dmod=(0, D_MODEL), norm_w=(6, D_MODEL), final_norm=(8, D_MODEL), a_q_norm=(9, HEAD_DIM), a_k_norm=(10, HEAD_DIM),
                  b_q_lora_norm=(11, B_Q_LORA), b_kv_lora_norm=(12, B_KV_LORA), c_sink=(13, C_HEADS))
SMALL_WEIGHTS = ("ada_b", "norm_w", "final_norm", "a_q_norm", "a_k_norm", "b_q_lora_norm", "b_kv_lora_norm", "c_sink")


def _pack_small(res):
    def padded(v):
        return jnp.concatenate([v, jnp.zeros((v.shape[0], D_MODEL - v.shape[1]), F32)], axis=1)

    rows = [res["dmod"].reshape(6, D_MODEL), res["norm_w"], res["final_norm"].reshape(1, D_MODEL)]
    rows += [padded(res[k]) for k in ("a_q_norm", "a_k_norm", "b_q_lora_norm", "b_kv_lora_norm", "c_sink")]
    return jnp.concatenate(rows + [jnp.zeros((2, D_MODEL), F32)], axis=0)


def _adam_small(parts, ws, ms, vs):
    nw = len(SMALL_WEIGHTS)

    def body(*refs):
        p_ref = refs[0]
        w_refs, m_refs, v_refs = refs[1:1 + nw], refs[1 + nw:1 + 2 * nw], refs[1 + 2 * nw:1 + 3 * nw]
        outs = refs[1 + 3 * nw:]
        g_all = p_ref[0]
        for k in range(1, N_DEV):
            g_all = g_all + p_ref[k]
        for idx, name in enumerate(SMALL_WEIGHTS):
            if name == "ada_b":
                g = jnp.concatenate([jnp.concatenate([g_all[3 * l + t:3 * l + t + 1] for t in range(3)], axis=1) for l in range(2)],
                                    axis=0)
            else:
                row, width = SMALL_ROWS[name]
                g = g_all[row:row + w_refs[idx].shape[0], 0:width]
            d, m_new, v_new = _adam_update(g, w_refs[idx][...], m_refs[idx][...], v_refs[idx][...])
            outs[4 * idx][...], outs[4 * idx + 1][...], outs[4 * idx + 2][...], outs[4 * idx + 3][...] = g, d, m_new, v_new

    out_shape = []
    for w in ws:
        out_shape += [_sds(w.shape, F32)] * 4
    return pl.pallas_call(body, name="adam_small", out_shape=out_shape,
                          compiler_params=pltpu.CompilerParams(vmem_limit_bytes=VMEM_LIMIT))(parts, *ws, *ms, *vs)


def _adam(parts, w, m, v, name):
    P, R, C = parts.shape
    tr = R if R <= ADAM_ROW_TILE else ADAM_ROW_TILE
    assert R % tr == 0

    def body(p_ref, w_ref, m_ref, v_ref, g_ref, d_ref, nm_ref, nv_ref):
        g = p_ref[0].astype(F32)
        for k in range(1, P):
            g = g + p_ref[k].astype(F32)
        g_ref[...] = g
        d_ref[...], nm_ref[...], nv_ref[...] = _adam_update(g, w_ref[...], m_ref[...], v_ref[...])

    spec = pl.BlockSpec((tr, C), lambda i: (i, 0))
    return pl.pallas_call(
        body, name=name, grid=(R // tr,),
        in_specs=[pl.BlockSpec((P, tr, C), lambda i: (0, i, 0)), spec, spec, spec],
        out_specs=[spec, spec, spec, spec], out_shape=[_sds((R, C), F32)] * 4,
        compiler_params=_params(("arbitrary",)),
    )(parts, w, m, v)


_ANY = pl.BlockSpec(memory_space=pl.ANY)
CHIP_FLIPS = ((1, 0), (0, 1), (1, 1))
DEV_FLIPS = tuple((dx, dy, dc) for dx in (0, 1) for dy in (0, 1) for dc in (0, 1) if dx + dy + dc)


def _flip(a, d):
    return a if d == 0 else 1 - a


def _my_place():
    return lax.axis_index("x"), lax.axis_index("y"), lax.axis_index("c")


def _gather_dev8(arrs, name):
    n = len(arrs)

    def body(*refs):
        ins, outs = refs[:n], refs[n:2 * n]
        send_sems, recv_sems, loc_sems = refs[2 * n:]
        x, y, c = _my_place()
        me = 4 * x + 2 * y + c
        copies = []
        for a in range(n):
            loc = pltpu.make_async_copy(ins[a], outs[a].at[me], loc_sems.at[a])
            loc.start()
            copies.append(loc)
            for k, (dx, dy, dc) in enumerate(DEV_FLIPS):
                cp = pltpu.make_async_remote_copy(
                    src_ref=ins[a], dst_ref=outs[a].at[me], send_sem=send_sems.at[a, k], recv_sem=recv_sems.at[a, k],
                    device_id=(_flip(x, dx), _flip(y, dy), _flip(c, dc)), device_id_type=MESH_ID)
                cp.start()
                copies.append(cp)
        for cp in copies:
            cp.wait()

    return pl.pallas_call(
        body, name=name, in_specs=[_ANY] * n, out_specs=[_ANY] * n,
        out_shape=[_sds((N_DEV,) + a.shape, a.dtype) for a in arrs],
        scratch_shapes=[pltpu.SemaphoreType.DMA((n, 7)), pltpu.SemaphoreType.DMA((n, 7)), pltpu.SemaphoreType.DMA((n,))],
    )(*arrs)


class _Exchange:
    def __init__(self, arrs, out_shapes, n_sems, phases):
        self.arrs, self.out_shapes, self.n_sems, self._phases = list(arrs), list(out_shapes), n_sems, phases

    @property
    def n(self):
        return len(self.arrs)

    def sem_shapes(self):
        return [pltpu.SemaphoreType.DMA((self.n, self.n_sems)), pltpu.SemaphoreType.DMA((self.n, self.n_sems)),
                pltpu.SemaphoreType.DMA((self.n,))]

    def phases(self, ins, outs, sems):
        return self._phases(ins, outs, *sems)

    def run(self, name):
        n = self.n

        def body(*refs):
            start, mid, end = self.phases(refs[:n], refs[n:2 * n], refs[2 * n:])
            start()
            mid()
            end()

        return pl.pallas_call(body, name=name, in_specs=[_ANY] * n, out_specs=[_ANY] * n, out_shape=self.out_shapes,
                              scratch_shapes=self.sem_shapes())(*self.arrs)


def _gather_halves_phases(ins, outs, send_sems, recv_sems, loc_sems):
    n = len(ins)
    x, y, c = _my_place()
    chip = 2 * x + y
    sibling = (x, y, 1 - c)
    peers = [(_flip(x, dx), _flip(y, dy)) for dx, dy in CHIP_FLIPS]

    def remote(src, p, half, a, k, to):
        return pltpu.make_async_remote_copy(src_ref=src, dst_ref=outs[a].at[p, half], send_sem=send_sems.at[a, k],
                                            recv_sem=recv_sems.at[a, k], device_id=to, device_id_type=MESH_ID)

    def local(a):
        return pltpu.make_async_copy(ins[a], outs[a].at[chip], loc_sems.at[a])

    def first(a, k):
        return remote(ins[a].at[c], chip, c, a, k, (*peers[k], c))

    def passed(a, k):
        p = 2 * peers[k][0] + peers[k][1]
        return remote(outs[a].at[p, c], p, c, a, 3 + k, sibling)

    def start():
        for a in range(n):
            local(a).start()
            for k in range(3):
                first(a, k).start()

    def mid():
        for a in range(n):
            for k in range(3):
                p = 2 * peers[k][0] + peers[k][1]
                remote(outs[a].at[p, c], p, c, a, k, sibling).wait_recv()
                passed(a, k).start()

    def end():
        for a in range(n):
            for k in range(3):
                p = 2 * peers[k][0] + peers[k][1]
                remote(outs[a].at[p, 1 - c], p, 1 - c, a, 3 + k, sibling).wait_recv()
        for a in range(n):
            for k in range(3):
                first(a, k).wait_send()
                passed(a, k).wait_send()
            local(a).wait()

    return start, mid, end


def _gather_chip4_halves(arrs):
    return _Exchange(arrs, [_sds((N_CHIPS,) + a.shape, a.dtype) for a in arrs], 6, _gather_halves_phases)


def _reduce_phases(ins, outs, send_sems, recv_sems, loc_sems):
    n = len(ins)
    x, y, c = _my_place()
    chip = 2 * x + y
    sibling = (x, y, 1 - c)
    peers = [(_flip(x, dx), _flip(y, dy)) for dx, dy in CHIP_FLIPS]

    def remote(src, slot, a, k, to):
        return pltpu.make_async_remote_copy(src_ref=src, dst_ref=outs[a].at[slot], send_sem=send_sems.at[a, k],
                                            recv_sem=recv_sems.at[a, k], device_id=to, device_id_type=MESH_ID)

    def local(a):
        return pltpu.make_async_copy(ins[a].at[chip], outs[a].at[2 * chip + c], loc_sems.at[a])

    def own(a):
        return remote(ins[a].at[chip], 2 * chip + c, a, 0, sibling)

    def first(a, k):
        return remote(ins[a].at[2 * peers[k][0] + peers[k][1]], 2 * chip + c, a, 1 + k, (*peers[k], c))

    def passed(a, k):
        slot = 2 * (2 * peers[k][0] + peers[k][1]) + c
        return remote(outs[a].at[slot], slot, a, 4 + k, sibling)

    def start():
        for a in range(n):
            local(a).start()
            own(a).start()
            for k in range(3):
                first(a, k).start()

    def mid():
        for a in range(n):
            for k in range(3):
                slot = 2 * (2 * peers[k][0] + peers[k][1]) + c
                remote(outs[a].at[slot], slot, a, 1 + k, sibling).wait_recv()
                passed(a, k).start()

    def end():
        for a in range(n):
            remote(outs[a].at[2 * chip + 1 - c], 2 * chip + 1 - c, a, 0, sibling).wait_recv()
            for k in range(3):
                slot = 2 * (2 * peers[k][0] + peers[k][1]) + 1 - c
                remote(outs[a].at[slot], slot, a, 4 + k, sibling).wait_recv()
        for a in range(n):
            own(a).wait_send()
            for k in range(3):
                first(a, k).wait_send()
                passed(a, k).wait_send()
            local(a).wait()

    return start, mid, end


def _reduce_exchange(arrs):
    return _Exchange(arrs, [_sds((N_DEV,) + a.shape[1:], a.dtype) for a in arrs], 7, _reduce_phases)


def _even_in_layout(w):
    return jnp.concatenate([w[:, 0:1696], jnp.zeros((w.shape[0], 96), w.dtype), w[:, 1696:2208]], axis=1)


def _even_in_unlayout(g):
    return jnp.concatenate([g[:, 0:1696], g[:, 1792:2304]], axis=1)


def _uq_layout(w):
    per = B_NOPE + B_ROPE
    pad = jnp.zeros((w.shape[0], LANES - B_ROPE), w.dtype)
    nope = [w[:, per * h:per * h + B_NOPE] for h in range(B_HEADS)]
    rope = [jnp.concatenate([w[:, per * h + B_NOPE:per * (h + 1)], pad], axis=1) for h in range(B_HEADS)]
    return jnp.concatenate(nope + rope, axis=1)


def _uq_unlayout(g):
    parts = []
    for h in range(B_HEADS):
        parts += [g[:, B_NOPE * h:B_NOPE * (h + 1)], g[:, 512 + LANES * h:512 + LANES * h + B_ROPE]]
    return jnp.concatenate(parts, axis=1)


def _block_diag(blocks):
    rows = []
    for h, blk in enumerate(blocks):
        r, cdim = blk.shape
        n = len(blocks)
        rows.append(jnp.concatenate([jnp.zeros((r, cdim * h), blk.dtype), blk, jnp.zeros((r, cdim * (n - 1 - h)), blk.dtype)],
                                    axis=1))
    return jnp.concatenate(rows, axis=0)


def _uk_layout(w):
    return _block_diag([w[:, h, :].T for h in range(B_HEADS)])


def _uk_unlayout(g):
    return jnp.stack([g[B_NOPE * h:B_NOPE * (h + 1), LANES * h:LANES * (h + 1)].T for h in range(B_HEADS)], axis=1)


def _uv_layout(w):
    return _block_diag([w[:, h, :] for h in range(B_HEADS)])


def _uv_unlayout(g):
    return jnp.stack([g[LANES * h:LANES * (h + 1), B_V * h:B_V * (h + 1)] for h in range(B_HEADS)], axis=1)


def _rope_tables(S):
    inv = ROPE_THETA ** (-jnp.arange(0, 32, 2, dtype=F32) / 32)
    tok = jnp.arange(S)

    def tab(pos):
        ang = pos.astype(F32)[:, None] * inv[None, :]
        cos, sin = jnp.cos(ang), jnp.sin(ang)
        return jnp.concatenate([cos, cos], axis=1), jnp.concatenate([-sin, sin], axis=1)

    cr, sr = tab(tok // GRID_W)
    cc, sc = tab(tok % GRID_W)
    ct, st = tab(tok)
    return (jnp.tile(jnp.concatenate([cr, cc], axis=1), (1, 2)), jnp.tile(jnp.concatenate([sr, sc], axis=1), (1, 2)),
            jnp.tile(ct, (1, 4)), jnp.tile(st, (1, 4)))


A_TQ, A_TK, A_SUB = 512, 4096, 512
B_TQ, B_TK, B_SUB = 128, 4096, 1024
B_BWD_TK, B_BWD_SUB = 4096, 512
C_T = 256
C_BLOCKS_PER_STEP = 4
KV_SHARE = 2


def _local_step(x0, tgt, mod, norm_w, wie, wuq, wuk, wuv, late_shards, a_q_norm, a_k_norm, q_lora_norm, kv_lora_norm,
                c_sink, final_norm):
    S = x0.shape[0]
    mod3 = mod.reshape(2, 3, D_MODEL)
    ca, sa, ct, st = _rope_tables(S)
    lane_seg = np.arange(LANES) // HEAD_DIM
    seg = jnp.asarray((lane_seg[:, None] == lane_seg[None, :]).astype(np.float32)).astype(BF16)
    qn = jnp.tile(a_q_norm.reshape(1, HEAD_DIM), (1, 2))
    kn = jnp.tile(a_k_norm.reshape(1, HEAD_DIM), (1, 2))
    qln, kvln = q_lora_norm.reshape(1, B_Q_LORA), kv_lora_norm.reshape(1, B_KV_LORA)
    nw0, nw1 = norm_w[0:1], norm_w[1:2]
    gate0, gate1 = mod3[0, 2:3], mod3[1, 2:3]
    a_tq, a_tk, b_tq, b_tk, bb_tk, c_t = min(A_TQ, S), min(A_TK, S), min(B_TQ, S), min(B_TK, S), min(B_BWD_TK, S), min(C_T, S)
    a_sub, b_sub, bb_sub = min(A_SUB, a_tk), min(B_SUB, b_tk), min(B_BWD_SUB, bb_tk)

    h0, proj_e, qa, ka, va, qcat, kcat, ka_t, va_t, kcat_t = _even_pre_fwd(x0, mod3[0], nw0, wie, qn, kn, seg, ca, sa, ct, st,
                                                                           qln, kvln, wuq, wuk)
    oa, lse_a, woe_g, wio_g, woo_g = _pp_fwd(qa, ka, va_t, kdiv=KV_SHARE, tq=a_tq, tk=a_tk, sub=a_sub, name="attn_a_fwd",
                                             side=_gather_chip4_halves(late_shards))
    woe = woe_g.reshape(D_MODEL, D_MODEL)
    wio = wio_g.reshape(N_CHIPS, D_MODEL, ODD_IN // N_CHIPS)
    woo = woo_g.reshape(D_MODEL, D_MODEL)
    olat, lse_b = _mla_fwd(qcat, kcat, kcat_t, tq=b_tq, tk=b_tk, sub=b_sub)
    y0, x1 = _even_post_fwd(oa, olat, proj_e, x0, gate0, wuv, woe)
    h1, proj_o, qc, kc, vc, kc_t, vc_t = _odd_pre_fwd(x1, mod3[1], nw1, wio)
    slopes = 2.0 ** (-8.0 * jnp.arange(1, C_HEADS + 1, dtype=F32) / C_HEADS)
    slope_rows = jnp.repeat(slopes.reshape(C_HEADS // 2, 2), c_t, axis=1)[:, None, :]
    sink_rows = jnp.repeat(c_sink.reshape(C_HEADS // 2, 2), c_t, axis=1)[:, None, :]
    win_dist = _win_dist_table(S, c_t)
    oc, lse_c = _win_fwd(qc, kc, vc_t, win_dist, slope_rows, sink_rows, kdiv=KV_SHARE, tq=c_t, nbs=C_BLOCKS_PER_STEP,
                         name="attn_c_fwd")
    doc, dgc, dx2, dwoo, st_f = _odd_post(oc, proj_o, x1, gate1, woo, final_norm.reshape(1, D_MODEL), tgt)
    dqc, dkc, dvc, dsink_raw = _win_bwd(qc, kc, kc_t, vc, oc, doc, lse_c, win_dist, slope_rows, sink_rows, kdiv=KV_SHARE, tq=c_t,
                                        nbs=C_BLOCKS_PER_STEP, name="attn_c_bwd")
    dx1, dwio, st_1 = _odd_pre_bwd(dqc, dkc, dvc, dgc, h1, x1, dx2, mod3[1], nw1, wio)
    doa, dga, dgb, dolat, dwoe, dwuv, st_e = _even_post_bwd(dx1, y0, oa, olat, proj_e, gate0, wuv, woe)
    late_grads = _reduce_exchange([dwoe.reshape(N_CHIPS, D_MODEL // N_CHIPS, D_MODEL), dwio,
                                   dwoo.reshape(N_CHIPS, D_MODEL // N_CHIPS, D_MODEL)])
    dqa, dka, dva, p_woe, p_wio, p_woo = _pp_bwd(qa, ka, ka_t, va, oa, doa, lse_a, kdiv=KV_SHARE, tq=a_tq, tk=a_tk, sub=a_sub,
                                                 name="attn_a_bwd", side=late_grads)
    dqcat, dkcat = _mla_bwd(qcat, kcat, kcat_t, olat, dolat, lse_b, tq=b_tq, tk=bb_tk, sub=bb_sub)
    dx0, dwie, dwuq, dwuk, st_0, nst = _even_pre_bwd(x0, h0, proj_e, dqa, dka, dva, dga, dgb, dqcat, dkcat, dx1, mod3[0], nw0,
                                                     wie, qn, kn, seg, ca, sa, ct, st, qln, kvln, wuq, wuk)
    dsink_pairs = jnp.stack([dsink_raw[:, 0, 0], dsink_raw[:, 1, 0]], axis=1).reshape(C_HEADS)
    return dict(
        loss=st_f[2, 0], dx=dx0,
        dmod=jnp.stack([jnp.concatenate([st_0[0], st_0[1], st_e[0]]), jnp.concatenate([st_1[0], st_1[1], st_f[1]])]),
        norm_w=jnp.stack([st_0[2], st_1[2]]), final_norm=st_f[0],
        a_q_norm=nst[0:1, 0:HEAD_DIM], a_k_norm=nst[1:2, 0:HEAD_DIM], b_q_lora_norm=nst[2:3, :], b_kv_lora_norm=nst[3:4, 0:LANES],
        c_sink=dsink_pairs.reshape(1, C_HEADS),
        even_w_in=dwie, b_w_uq=dwuq, b_w_uk=dwuk, b_w_uv=dwuv, even_w_out=p_woe, odd_w_in=p_wio, odd_w_out=p_woo)


WEIGHT_NAMES = ("norm_w", "ada_w", "ada_b", "even_w_in", "a_q_norm", "a_k_norm", "b_q_lora_norm", "b_kv_lora_norm", "b_w_uq",
                "b_w_uk", "b_w_uv", "even_w_out", "odd_w_in", "c_sink", "odd_w_out", "final_norm")


def _cols_to_chips(g):
    r, n4 = g.shape
    return jnp.transpose(g.reshape(r, N_CHIPS, n4 // N_CHIPS), (1, 0, 2))


def _chips_to_cols(g):
    p, r, n = g.shape
    return jnp.transpose(g, (1, 0, 2)).reshape(r, p * n)


def kernel(x, c, norm_w, ada_w, ada_b, even_w_in, a_q_norm, a_k_norm, b_q_lora_norm, b_kv_lora_norm, b_w_uq, b_w_uk, b_w_uv, even_w_out, odd_w_in, c_sink, odd_w_out, final_norm, loss_target, m_norm_w, m_ada_w, m_ada_b, m_even_w_in, m_a_q_norm, m_a_k_norm, m_b_q_lora_norm, m_b_kv_lora_norm, m_b_w_uq, m_b_w_uk, m_b_w_uv, m_even_w_out, m_odd_w_in, m_c_sink, m_odd_w_out, m_final_norm, v_norm_w, v_ada_w, v_ada_b, v_even_w_in, v_a_q_norm, v_a_k_norm, v_b_q_lora_norm, v_b_kv_lora_norm, v_b_w_uq, v_b_w_uk, v_b_w_uv, v_even_w_out, v_odd_w_in, v_c_sink, v_odd_w_out, v_final_norm):
    given = dict(locals())
    xi, yi, ci = _my_place()
    chip = 2 * xi + yi
    dev = 2 * chip + ci
    n_ada = ada_w.shape[2]

    (c_all,) = _gather_dev8([c], "gather_c")
    c_all = c_all.reshape(N_DEV, D_MODEL)
    bias = lax.dynamic_slice_in_dim(ada_b, chip * n_ada, n_ada, axis=1).reshape(2, 1, n_ada)
    mod_cols = _ada_fwd(c_all, ada_w, bias)
    def halves(w):
        return w.astype(BF16).reshape((2, w.shape[0] // 2) + w.shape[1:])

    mod_all, wie_g, wuq_g = _gather_chip4_halves([mod_cols, halves(even_w_in[0]), halves(b_w_uq[0])]).run("gather_weights")
    wie_g = wie_g.reshape(N_CHIPS, D_MODEL, EVEN_IN // N_CHIPS)
    wuq_g = wuq_g.reshape(N_CHIPS, B_Q_LORA, -1)
    mod = jnp.transpose(lax.dynamic_index_in_dim(mod_all, dev, axis=2, keepdims=False), (1, 0, 2)).reshape(2, 3 * D_MODEL)

    res = _local_step(
        x[0], loss_target[0], mod, norm_w,
        _even_in_layout(_chips_to_cols(wie_g)), _uq_layout(_chips_to_cols(wuq_g)), _uk_layout(b_w_uk[0].astype(BF16)),
        _uv_layout(b_w_uv[0].astype(BF16)), [halves(even_w_out[0]), halves(odd_w_in[0]), halves(odd_w_out[0])],
        a_q_norm, a_k_norm, b_q_lora_norm, b_kv_lora_norm, c_sink, final_norm)

    shard_parts = dict(zip(
        ("even_w_in", "b_w_uq"),
        _reduce_exchange([_cols_to_chips(_even_in_unlayout(res["even_w_in"].astype(BF16))),
                          _cols_to_chips(_uq_unlayout(res["b_w_uq"].astype(BF16)))]).run("reduce_exchange")))
    shard_parts.update({k: res[k] for k in ("even_w_out", "odd_w_in", "odd_w_out")})

    latent = jnp.stack([_uk_unlayout(res["b_w_uk"]).reshape(B_KV_LORA, 512),
                        _uv_unlayout(res["b_w_uv"]).reshape(B_KV_LORA, 512)]).astype(BF16)
    small_all, latent_all = _gather_dev8([_pack_small(res), latent], "gather_small")
    dmod_all = small_all[:, 0:6, :].reshape(N_DEV, 2, 3 * D_MODEL)
    dmod_cols = jnp.transpose(lax.dynamic_slice_in_dim(dmod_all, chip * n_ada, n_ada, axis=2), (1, 0, 2))
    parts = dict(shard_parts)
    parts["ada_w"] = _ada_bwd(c_all.T, dmod_cols).reshape(1, 2 * D_MODEL, n_ada)
    parts["b_w_uk"], parts["b_w_uv"] = latent_all[:, 0], latent_all[:, 1]

    def as2d(a):
        return a.reshape((-1, a.shape[-1]) if a.ndim > 1 else (1, a.shape[0]))

    results = {}
    small_outs = _adam_small(small_all, *[[as2d(given[pre + k]) for k in SMALL_WEIGHTS] for pre in ("", "m_", "v_")])
    for idx, k in enumerate(SMALL_WEIGHTS):
        results[k] = small_outs[4 * idx:4 * idx + 4]
    for k, p in parts.items():
        shape2 = (p.shape[-2], p.shape[-1])
        results[k] = _adam(p, given[k].reshape(shape2), given["m_" + k].reshape(shape2), given["v_" + k].reshape(shape2),
                           "adam_" + k)
    by_kind = [[results[k][t].reshape(given[k].shape) for k in WEIGHT_NAMES] for t in range(4)]
    loss = lax.psum(res["loss"], ("x", "y", "c"))
    return (loss, res["dx"][None], *by_kind[0], *by_kind[1], *by_kind[2], *by_kind[3])
```

```python
import functools

import numpy as np
import jax
import jax.numpy as jnp
from jax import lax
from jax.experimental import pallas as pl
from jax.experimental.pallas import tpu as pltpu

F32 = jnp.float32
BF16 = jnp.bfloat16
HIGHEST = lax.Precision.HIGHEST
MESH_ID = pl.DeviceIdType.MESH

D_MODEL = 1024
HEAD_DIM = 64
GRID_W = 64
EPS = 1e-6
ROPE_THETA = 10000.0
A_HEADS, A_KV_HEADS = 8, 2
B_HEADS, B_NOPE, B_ROPE, B_V = 8, 64, 32, 64
B_Q_LORA, B_KV_LORA = 256, 128
C_HEADS, C_KV_HEADS = 16, 4
WINDOW = 128
EVEN_IN, ODD_IN = 2208, 2560
EVEN_P = 2304
N_CHIPS, N_DEV = 4, 8
LANES = 128
NEG = -1e30
VMEM_LIMIT = 60 * 1024 * 1024

ADAM_LR, ADAM_B1, ADAM_B2, ADAM_EPS, ADAM_WD, ADAM_STEP = 0.001, 0.9, 0.999, 1e-08, 0.01, 10

ROW_TILE = 256


def _dot(a, b):
    return lax.dot_general(a, b, (((1,), (0,)), ((), ())), preferred_element_type=F32)


def _dot_nt(a, b):
    return lax.dot_general(a, b, (((1,), (1,)), ((), ())), preferred_element_type=F32)


def _dot_tn(a, b):
    return lax.dot_general(a, b, (((0,), (0,)), ((), ())), preferred_element_type=F32)


def _dot_f32(a, b):
    return lax.dot_general(a, b, (((1,), (0,)), ((), ())), precision=HIGHEST, preferred_element_type=F32)


def _sigmoid(x):
    return 1.0 / (1.0 + jnp.exp(-x))


def _silu_and_grad(g):
    s = _sigmoid(g)
    return g * s, s * (1.0 + g * (1.0 - s))


def _lane_iota():
    return lax.broadcasted_iota(jnp.int32, (1, LANES), 1)


def _partner(x, lane):
    return jnp.where((lane % 32) < 16, pltpu.roll(x, LANES - 16, 1), pltpu.roll(x, 16, 1))


def _rot(x, cos, sin_signed, lane):
    return x * cos + _partner(x, lane) * sin_signed


def _rot_bwd(dy, cos, sin_signed, lane):
    return dy * cos + _partner(dy * sin_signed, lane)


def _rms(x):
    return lax.rsqrt(jnp.mean(x * x, axis=-1, keepdims=True) + EPS)


def _rms_bwd(x, r, g):
    return r * g - x * (r * r * r) * jnp.mean(x * g, axis=-1, keepdims=True)


def _seg_mean(v, seg_ones):
    hi = v.astype(BF16)
    lo = (v - hi.astype(F32)).astype(BF16)
    return (_dot(hi, seg_ones) + _dot(lo, seg_ones)) * (1.0 / HEAD_DIM)


def _dup_heads(x, lane):
    swapped = pltpu.roll(x, HEAD_DIM, 1)
    lo = lane < HEAD_DIM
    return jnp.concatenate([jnp.where(lo, x, swapped), jnp.where(lo, swapped, x)], axis=1)


def _fold_heads(x2, lane):
    a, b = x2[:, 0:LANES], x2[:, LANES:2 * LANES]
    return jnp.where(lane < HEAD_DIM, a + pltpu.roll(a, HEAD_DIM, 1), b + pltpu.roll(b, HEAD_DIM, 1))


def _row_spec(ts, cols):
    return pl.BlockSpec((ts, cols), lambda i: (i, 0))


def _full_spec(shape, single=True):
    nd = len(shape)
    if single:
        return pl.BlockSpec(shape, lambda i: (0,) * nd, pipeline_mode=pl.Buffered(1))
    return pl.BlockSpec(shape, lambda i: (0,) * nd)


def _sds(shape, dtype):
    return jax.ShapeDtypeStruct(shape, dtype)


def _params(sem):
    return pltpu.CompilerParams(dimension_semantics=sem, vmem_limit_bytes=VMEM_LIMIT)


def _even_pre_fwd(x, mod, nw, wie, qn, kn, seg, ca, sa, ct, st, qln, kvln, wuq, wuk):
    S = x.shape[0]
    ts = min(ROW_TILE, S)

    def body(x_ref, mod_ref, nw_ref, wie_ref, qn_ref, kn_ref, seg_ref, ca_ref, sa_ref, ct_ref, st_ref, qln_ref,
             kvln_ref, wuq_ref, wuk_ref, h_ref, proj_ref, qa_ref, ka_ref, va_ref, qcat_ref, kcat_ref, kat_ref, vat_ref, kcatt_ref):
        xv = x_ref[...]
        h = (xv * _rms(xv) * nw_ref[...]) * (1.0 + mod_ref[1:2, :]) + mod_ref[0:1, :]
        hb = h.astype(BF16)
        h_ref[...] = hb
        proj = _dot(hb, wie_ref[...])
        proj_ref[...] = proj
        lane = _lane_iota()
        ca_v, sa_v, ct_v, st_v = ca_ref[...], sa_ref[...], ct_ref[...], st_ref[...]
        seg_v = seg_ref[...]
        for cb in range(4):
            xc = proj[:, LANES * cb:LANES * (cb + 1)]
            r = lax.rsqrt(_seg_mean(xc * xc, seg_v) + EPS)
            y = _rot(xc * r * qn_ref[...], ca_v, sa_v, lane)
            qa_ref[:, LANES * cb:LANES * (cb + 1)] = (y * 0.125).astype(BF16)
        kc = proj[:, 512:640]
        r = lax.rsqrt(_seg_mean(kc * kc, seg_v) + EPS)
        ka_v = _dup_heads(_rot(kc * r * kn_ref[...], ca_v, sa_v, lane), lane)
        ka_ref[...] = ka_v.astype(BF16)
        kat_ref[...] = ka_v.T.astype(BF16)
        va_v = _dup_heads(proj[:, 640:768], lane)
        va_ref[...] = va_v.astype(BF16)
        vat_ref[...] = va_v.T.astype(BF16)
        cq = proj[:, 1280:1536]
        cqn = (cq * _rms(cq) * qln_ref[...]).astype(BF16)
        ckv = proj[:, 1536:1664]
        ckvn = ckv * _rms(ckv) * kvln_ref[...]
        qb = _dot(cqn, wuq_ref[...])
        qlat = _dot(qb[:, 0:512].astype(BF16), wuk_ref[...])
        for hh in range(B_HEADS):
            qcat_ref[hh, :, 0:LANES] = qlat[:, LANES * hh:LANES * (hh + 1)].astype(BF16)
            qr = _rot(qb[:, 512 + LANES * hh:512 + LANES * (hh + 1)], ct_v, st_v, lane)
            qcat_ref[hh, :, LANES:2 * LANES] = qr.astype(BF16)
        kr = _rot(proj[:, 1664:1792], ct_v, st_v, lane)
        kcat_ref[:, 0:LANES] = ckvn.astype(BF16)
        kcat_ref[:, LANES:2 * LANES] = kr.astype(BF16)
        kcatt_ref[0:LANES, :] = ckvn.T.astype(BF16)
        kcatt_ref[LANES:2 * LANES, :] = kr.T.astype(BF16)

    col_spec = lambda rows: pl.BlockSpec((rows, ts), lambda i: (0, i))
    return pl.pallas_call(
        body, name="even_pre_fwd", grid=(S // ts,),
        in_specs=[_row_spec(ts, D_MODEL), _full_spec((3, D_MODEL)), _full_spec((1, D_MODEL)), _full_spec((D_MODEL, EVEN_P)),
                  _full_spec((1, LANES)), _full_spec((1, LANES)), _full_spec((LANES, LANES)),
                  _row_spec(ts, LANES), _row_spec(ts, LANES), _row_spec(ts, LANES), _row_spec(ts, LANES),
                  _full_spec((1, B_Q_LORA)), _full_spec((1, B_KV_LORA)), _full_spec((B_Q_LORA, 1536)), _full_spec((512, 1024))],
        out_specs=[_row_spec(ts, D_MODEL), _row_spec(ts, EVEN_P), _row_spec(ts, 512), _row_spec(ts, 2 * LANES), _row_spec(ts, 2 * LANES),
                   pl.BlockSpec((B_HEADS, ts, 2 * LANES), lambda i: (0, i, 0)), _row_spec(ts, 2 * LANES),
                   col_spec(2 * LANES), col_spec(2 * LANES), col_spec(2 * LANES)],
        out_shape=[_sds((S, D_MODEL), BF16), _sds((S, EVEN_P), F32), _sds((S, 512), BF16), _sds((S, 2 * LANES), BF16),
                   _sds((S, 2 * LANES), BF16), _sds((B_HEADS, S, 2 * LANES), BF16), _sds((S, 2 * LANES), BF16),
                   _sds((2 * LANES, S), BF16), _sds((2 * LANES, S), BF16), _sds((2 * LANES, S), BF16)],
        compiler_params=_params(("arbitrary",)),
    )(x, mod, nw, wie, qn, kn, seg, ca, sa, ct, st, qln, kvln, wuq, wuk)


MLA_SCALE = (B_NOPE + B_ROPE) ** -0.5
LOG2E = 1.4426950408889634

def _row_lo():
    return lax.broadcasted_iota(jnp.int32, (LANES, 1), 0) < HEAD_DIM


def _stack_cols(vT, rlo):
    zero = jnp.zeros_like(vT)
    return jnp.concatenate([jnp.where(rlo, vT, zero), jnp.where(rlo, zero, vT)], axis=1)


def _stack_rows(v, lo):
    zero = jnp.zeros_like(v)
    return jnp.concatenate([jnp.where(lo, v, zero), jnp.where(lo, zero, v)], axis=0)


def _pick_halves_T(xT, rlo, t):
    return jnp.where(rlo, xT[:, 0:t], xT[:, t:2 * t]).T


def _side_split(refs, n_in, n_out, n_scratch, side):
    ns = side.n if side is not None else 0
    cuts = np.cumsum([0, n_in, ns, n_out, ns, n_scratch])
    return [refs[a:b] for a, b in zip(cuts[:-1], cuts[1:])] + [refs[cuts[-1]:]]


def _side_hooks(side, side_ins, side_outs, side_sems, step, total):
    if side is None:
        return lambda: None
    start, mid, end = side.phases(side_ins, side_outs, side_sems)
    pl.when(step == 0)(start)
    pl.when(step == total // 2)(mid)
    return lambda: pl.when(step == total - 1)(end)


def _side_specs(side):
    if side is None:
        return [], [], [], [], []
    return list(side.arrs), [_ANY] * side.n, [_ANY] * side.n, list(side.out_shapes), side.sem_shapes()


def _pp_fwd(q, k, vT, *, kdiv, tq, tk, sub, name, side=None):
    S = k.shape[0]; nb = q.shape[1] // LANES; nq = S // tq; nkv = S // tk; nsub = tk // sub

    def body(*refs):
        (q_ref, k_ref, vT_ref), side_ins, (o_ref, lse_ref), side_outs, (qs, m_s, l_s, acc), side_sems = _side_split(refs, 3, 2, 4, side)
        j = pl.program_id(2)
        rlo = _row_lo()
        step = (pl.program_id(0) * nq + pl.program_id(1)) * nkv + j
        side_end = _side_hooks(side, side_ins, side_outs, side_sems, step, nb * nq * nkv)

        @pl.when(j == 0)
        def _():
            qs[...] = _stack_cols(q_ref[...].astype(F32).T, rlo).astype(BF16)
            m_s[...] = jnp.full((1, 2 * tq), NEG, F32)
            l_s[...] = jnp.zeros((1, 2 * tq), F32)
            acc[...] = jnp.zeros((LANES, 2 * tq), F32)

        qsv = qs[...]
        m, l, a = m_s[...], l_s[...], acc[...]
        s_cur = _dot(k_ref[0:sub, :], qsv)
        for t in range(nsub):
            if t + 1 < nsub:
                s_next = _dot(k_ref[sub * (t + 1):sub * (t + 2), :], qsv)
            m_new = jnp.maximum(m, jnp.max(s_cur, axis=0, keepdims=True))
            alpha = jnp.exp(m - m_new)
            p = jnp.exp(s_cur - m_new)
            l = alpha * l + jnp.sum(p, axis=0, keepdims=True)
            a = alpha * a + _dot(vT_ref[:, sub * t:sub * (t + 1)], p.astype(BF16))
            m = m_new
            if t + 1 < nsub:
                s_cur = s_next
        m_s[...], l_s[...], acc[...] = m, l, a

        @pl.when(j == nkv - 1)
        def _():
            l_f = l_s[...]
            o_ref[...] = _pick_halves_T(acc[...] / l_f, rlo, tq)
            lse_ref[0, 0] = m_s[...] + jnp.log(l_f)

        side_end()

    s_args, s_in, s_out, s_shapes, s_sems = _side_specs(side)
    return pl.pallas_call(
        body, name=name, grid=(nb, nq, nkv),
        in_specs=[pl.BlockSpec((tq, LANES), lambda b, i, j: (i, b)), pl.BlockSpec((tk, LANES), lambda b, i, j: (j, b // kdiv)),
                  pl.BlockSpec((LANES, tk), lambda b, i, j: (b // kdiv, j))] + s_in,
        out_specs=[pl.BlockSpec((tq, LANES), lambda b, i, j: (i, b)),
                   pl.BlockSpec((1, 1, 1, 2 * tq), lambda b, i, j: (b, i, 0, 0))] + s_out,
        out_shape=[_sds((S, nb * LANES), F32), _sds((nb, nq, 1, 2 * tq), F32)] + s_shapes,
        scratch_shapes=[pltpu.VMEM((LANES, 2 * tq), BF16), pltpu.VMEM((1, 2 * tq), F32), pltpu.VMEM((1, 2 * tq), F32),
                        pltpu.VMEM((LANES, 2 * tq), F32)] + s_sems,
        compiler_params=_params(("arbitrary",) * 3))(q, k, vT, *s_args)


def _pp_bwd(q, k, kT, v, o, do, lse, *, kdiv, tq, tk, sub, name, side=None):
    S = k.shape[0]; nb = q.shape[1] // LANES; nkb = k.shape[1] // LANES; nq = S // tq; nkv = S // tk; nsub = tk // sub

    def body(*refs):
        ((q_ref, k_ref, kT_ref, v_ref, o_ref, do_ref, lse_ref), side_ins, (dq_ref, dk_ref, dv_ref), side_outs,
         (qsT, qs, dosT, dos, delta_s, dq_acc), side_sems) = _side_split(refs, 7, 3, 6, side)
        b, i, j = pl.program_id(0), pl.program_id(1), pl.program_id(2)
        rlo = _row_lo()
        lo = lax.broadcasted_iota(jnp.int32, (1, LANES), 1) < HEAD_DIM
        side_end = _side_hooks(side, side_ins, side_outs, side_sems, (b * nq + i) * nkv + j, nb * nq * nkv)

        @pl.when((b % kdiv == 0) & (i == 0) & (j == 0))
        def _():
            dk_ref[...] = jnp.zeros((S, LANES), F32)
            dv_ref[...] = jnp.zeros((S, LANES), F32)

        @pl.when(j == 0)
        def _():
            qv = q_ref[...]
            qs[...] = _stack_rows(qv, lo)
            qsT[...] = _stack_cols(qv.astype(F32).T, rlo).astype(BF16)
            dov = do_ref[...]
            dos[...] = _stack_rows(dov.astype(BF16), lo)
            dosT[...] = _stack_cols(dov.T, rlo).astype(BF16)
            prodT = (dov * o_ref[...]).T
            delta_s[...] = jnp.concatenate([jnp.sum(jnp.where(rlo, prodT, 0.0), axis=0, keepdims=True),
                                            jnp.sum(jnp.where(rlo, 0.0, prodT), axis=0, keepdims=True)], axis=1)
            dq_acc[...] = jnp.zeros((LANES, 2 * tq), F32)

        qsTv, dosTv, qsv, dosv = qsT[...], dosT[...], qs[...], dos[...]
        lse_v, delta_v = lse_ref[0, 0], delta_s[...]
        dqa = dq_acc[...]
        s_cur = _dot(k_ref[0:sub, :], qsTv)
        dp_cur = _dot(v_ref[0:sub, :], dosTv)
        for t in range(nsub):
            if t + 1 < nsub:
                s_next = _dot(k_ref[sub * (t + 1):sub * (t + 2), :], qsTv)
                dp_next = _dot(v_ref[sub * (t + 1):sub * (t + 2), :], dosTv)
            p = jnp.exp(s_cur - lse_v)
            ds = (p * (dp_cur - delta_v)).astype(BF16)
            rows = pl.ds(pl.multiple_of(j * tk + sub * t, sub), sub)
            dv_ref[rows, :] += _dot(p.astype(BF16), dosv)
            dk_ref[rows, :] += _dot(ds, qsv)
            dqa = dqa + _dot(kT_ref[:, sub * t:sub * (t + 1)], ds)
            if t + 1 < nsub:
                s_cur, dp_cur = s_next, dp_next
        dq_acc[...] = dqa

        @pl.when(j == nkv - 1)
        def _():
            dq_ref[...] = _pick_halves_T(dq_acc[...], rlo, tq)

        side_end()

    qmap = lambda b, i, j: (i, b)
    kmap = lambda b, i, j: (j, b // kdiv)
    res = lambda b, i, j: (0, b // kdiv)
    s_args, s_in, s_out, s_shapes, s_sems = _side_specs(side)
    return pl.pallas_call(
        body, name=name, grid=(nb, nq, nkv),
        in_specs=[pl.BlockSpec((tq, LANES), qmap), pl.BlockSpec((tk, LANES), kmap), pl.BlockSpec((LANES, tk), lambda b, i, j: (b // kdiv, j)),
                  pl.BlockSpec((tk, LANES), kmap), pl.BlockSpec((tq, LANES), qmap), pl.BlockSpec((tq, LANES), qmap),
                  pl.BlockSpec((1, 1, 1, 2 * tq), lambda b, i, j: (b, i, 0, 0))] + s_in,
        out_specs=[pl.BlockSpec((tq, LANES), qmap), pl.BlockSpec((S, LANES), res), pl.BlockSpec((S, LANES), res)] + s_out,
        out_shape=[_sds((S, nb * LANES), F32), _sds((S, nkb * LANES), F32), _sds((S, nkb * LANES), F32)] + s_shapes,
        scratch_shapes=[pltpu.VMEM((LANES, 2 * tq), BF16), pltpu.VMEM((2 * tq, LANES), BF16), pltpu.VMEM((LANES, 2 * tq), BF16),
                        pltpu.VMEM((2 * tq, LANES), BF16), pltpu.VMEM((1, 2 * tq), F32), pltpu.VMEM((LANES, 2 * tq), F32)] + s_sems,
        compiler_params=_params(("arbitrary",) * 3))(q, k, kT, v, o, do, lse, *s_args)


MLA_C = MLA_SCALE * LOG2E


def _mla_fwd(q, kcat, kcatT, *, tq, tk, sub):
    S = kcat.shape[0]; nq, nkv = S // tq, S // tk; R = B_HEADS * tq; nsub = tk // sub

    def body(q_ref, k_ref, vT_ref, o_ref, lse_ref, qT, m_s, l_s, acc):
        j = pl.program_id(1)

        @pl.when(j == 0)
        def _():
            qT[...] = q_ref[...].reshape(R, 2 * LANES).astype(F32).T.astype(BF16)
            m_s[...] = jnp.full((1, R), NEG, F32)
            l_s[...] = jnp.zeros((1, R), F32)
            acc[...] = jnp.zeros((LANES, R), F32)

        qTv = qT[...]
        m, l, a = m_s[...], l_s[...], acc[...]
        s_cur = _dot(k_ref[0:sub, :], qTv)
        for t in range(nsub):
            if t + 1 < nsub:
                s_next = _dot(k_ref[sub * (t + 1):sub * (t + 2), :], qTv)
            m_new = jnp.maximum(m, jnp.max(s_cur, axis=0, keepdims=True))
            alpha = jnp.exp2((m - m_new) * MLA_C)
            p = jnp.exp2((s_cur - m_new) * MLA_C)
            l = alpha * l + jnp.sum(p, axis=0, keepdims=True)
            a = alpha * a + _dot(vT_ref[:, sub * t:sub * (t + 1)], p.astype(BF16))
            m = m_new
            if t + 1 < nsub:
                s_cur = s_next
        m_s[...], l_s[...], acc[...] = m, l, a

        @pl.when(j == nkv - 1)
        def _():
            l_f = l_s[...]
            o_ref[...] = (acc[...] / l_f).T.reshape(B_HEADS, tq, LANES)
            lse_ref[0] = m_s[...] * MLA_SCALE + jnp.log(l_f)

    return pl.pallas_call(
        body, name="mla_fwd", grid=(nq, nkv),
        in_specs=[pl.BlockSpec((B_HEADS, tq, 2 * LANES), lambda i, j: (0, i, 0)), pl.BlockSpec((tk, 2 * LANES), lambda i, j: (j, 0)),
                  pl.BlockSpec((LANES, tk), lambda i, j: (0, j))],
        out_specs=[pl.BlockSpec((B_HEADS, tq, LANES), lambda i, j: (0, i, 0)), pl.BlockSpec((1, 1, R), lambda i, j: (i, 0, 0))],
        out_shape=[_sds((B_HEADS, S, LANES), F32), _sds((nq, 1, R), F32)],
        scratch_shapes=[pltpu.VMEM((2 * LANES, R), BF16), pltpu.VMEM((1, R), F32), pltpu.VMEM((1, R), F32), pltpu.VMEM((LANES, R), F32)],
        compiler_params=_params(("arbitrary", "arbitrary")))(q, kcat, kcatT)


def _mla_bwd(q, kcat, kcatT, o, do, lse, *, tq, tk, sub):
    S = kcat.shape[0]; nq, nkv = S // tq, S // tk; R = B_HEADS * tq; nsub = tk // sub

    def body(q_ref, k_ref, kT_ref, o_ref, do_ref, lse_ref, dq_ref, dk_ref, qT, dosT, dos, delta_s, dq_acc):
        i, j = pl.program_id(0), pl.program_id(1)

        @pl.when((i == 0) & (j == 0))
        def _():
            dk_ref[...] = jnp.zeros((S, 2 * LANES), F32)

        @pl.when(j == 0)
        def _():
            qT[...] = q_ref[...].reshape(R, 2 * LANES).astype(F32).T.astype(BF16)
            dov = do_ref[...].reshape(R, LANES)
            dos[...] = dov.astype(BF16)
            dosT[...] = dov.T.astype(BF16)
            delta_s[...] = jnp.sum((dov * o_ref[...].reshape(R, LANES)).T, axis=0, keepdims=True)
            dq_acc[...] = jnp.zeros((2 * LANES, R), F32)

        qTv, dosTv, dosv = qT[...], dosT[...], dos[...]
        qv = q_ref[...].reshape(R, 2 * LANES)
        lse_v, delta_v = lse_ref[0] * LOG2E, delta_s[...]
        dqa = dq_acc[...]
        s_cur = _dot(k_ref[0:sub, :], qTv)
        dp_cur = _dot(k_ref[0:sub, 0:LANES], dosTv)
        for t in range(nsub):
            if t + 1 < nsub:
                s_next = _dot(k_ref[sub * (t + 1):sub * (t + 2), :], qTv)
                dp_next = _dot(k_ref[sub * (t + 1):sub * (t + 2), 0:LANES], dosTv)
            p = jnp.exp2(s_cur * MLA_C - lse_v)
            ds = (p * (dp_cur - delta_v) * MLA_SCALE).astype(BF16)
            rows = pl.ds(pl.multiple_of(j * tk + sub * t, sub), sub)
            dk_ref[rows, :] += _dot(ds, qv)
            dk_ref[rows, 0:LANES] += _dot(p.astype(BF16), dosv)
            dqa = dqa + _dot(kT_ref[:, sub * t:sub * (t + 1)], ds)
            if t + 1 < nsub:
                s_cur, dp_cur = s_next, dp_next
        dq_acc[...] = dqa

        @pl.when(j == nkv - 1)
        def _():
            dq_ref[...] = dq_acc[...].T.reshape(B_HEADS, tq, 2 * LANES)

    hspec = lambda w: pl.BlockSpec((B_HEADS, tq, w), lambda i, j: (0, i, 0))
    return pl.pallas_call(
        body, name="mla_bwd", grid=(nq, nkv),
        in_specs=[hspec(2 * LANES), pl.BlockSpec((tk, 2 * LANES), lambda i, j: (j, 0)), pl.BlockSpec((2 * LANES, tk), lambda i, j: (0, j)),
                  hspec(LANES), hspec(LANES), pl.BlockSpec((1, 1, R), lambda i, j: (i, 0, 0))],
        out_specs=[hspec(2 * LANES), pl.BlockSpec((S, 2 * LANES), lambda i, j: (0, 0))],
        out_shape=[_sds((B_HEADS, S, 2 * LANES), F32), _sds((S, 2 * LANES), F32)],
        scratch_shapes=[pltpu.VMEM((2 * LANES, R), BF16), pltpu.VMEM((LANES, R), BF16), pltpu.VMEM((R, LANES), BF16),
                        pltpu.VMEM((1, R), F32), pltpu.VMEM((2 * LANES, R), F32)],
        compiler_params=_params(("arbitrary", "arbitrary")))(q, kcat, kcatT, o, do, lse)


def _win_start(i, tq, nk, S):
    return pl.multiple_of(jnp.clip(i * tq - WINDOW, 0, S - nk), LANES)


def _win_dist_table(S, tq):
    nk = min(tq + 2 * WINDOW, S)
    nq = S // tq
    r = np.arange(nk)[:, None]
    c = (np.arange(2 * tq) % tq)[None, :]
    tabs = []
    for rel in (0, WINDOW, (nq - 1) * tq - (S - nk)):
        dist = np.abs(rel + c - r).astype(np.float32)
        tabs.append(np.where(dist <= WINDOW, dist, np.float32(1e32)))
    return jnp.asarray(np.stack(tabs))


def _win_dist_spec(nk, tq, nq):
    return pl.BlockSpec((1, nk, 2 * tq), lambda b, i: (jnp.where(i == 0, 0, jnp.where(i == nq - 1, 2, 1)), 0, 0))


def _win_fwd(q, k, vT, dist, slope, sink, *, kdiv, tq, nbs, name):
    S = k.shape[0]; nb = q.shape[1] // LANES; nq = S // tq; nk = min(tq + 2 * WINDOW, S)
    assert nb % nbs == 0 and nbs % kdiv == 0
    kvw = (nbs // kdiv) * LANES

    def body(q_ref, k_ref, vT_ref, dist_ref, slope_ref, sink_ref, o_ref, lse_ref):
        i = pl.program_id(1)
        rlo = _row_lo()
        k0 = _win_start(i, tq, nk, S)
        kk, vv, dd = k_ref[pl.ds(k0, nk), :], vT_ref[:, pl.ds(k0, nk)], dist_ref[0]
        for u in range(nbs):
            kv = slice(LANES * (u // kdiv), LANES * (u // kdiv + 1))
            qsT = _stack_cols(q_ref[:, LANES * u:LANES * (u + 1)].astype(F32).T, rlo).astype(BF16)
            s = _dot(kk[:, kv], qsT) - slope_ref[u] * dd
            sk = sink_ref[u]
            m = jnp.maximum(jnp.max(s, axis=0, keepdims=True), sk)
            p = jnp.exp(s - m)
            l = jnp.sum(p, axis=0, keepdims=True) + jnp.exp(sk - m)
            o_ref[:, LANES * u:LANES * (u + 1)] = _pick_halves_T(_dot(vv[kv, :], p.astype(BF16)) / l, rlo, tq)
            lse_ref[u, 0] = m + jnp.log(l)

    row_spec = pl.BlockSpec((nbs, 1, 2 * tq), lambda b, i: (b, 0, 0))
    return pl.pallas_call(
        body, name=name, grid=(nb // nbs, nq),
        in_specs=[pl.BlockSpec((tq, nbs * LANES), lambda b, i: (i, b)), pl.BlockSpec((S, kvw), lambda b, i: (0, b)),
                  pl.BlockSpec((kvw, S), lambda b, i: (b, 0)), _win_dist_spec(nk, tq, nq), row_spec, row_spec],
        out_specs=[pl.BlockSpec((tq, nbs * LANES), lambda b, i: (i, b)), pl.BlockSpec((nbs, 1, 1, 2 * tq), lambda b, i: (b, i, 0, 0))],
        out_shape=[_sds((S, nb * LANES), F32), _sds((nb, nq, 1, 2 * tq), F32)],
        compiler_params=_params(("arbitrary", "arbitrary")))(q, k, vT, dist, slope, sink)


def _win_bwd(q, k, kT, v, o, do, lse, dist, slope, sink, *, kdiv, tq, nbs, name):
    S = k.shape[0]; nb = q.shape[1] // LANES; nkb = k.shape[1] // LANES; nq = S // tq; nk = min(tq + 2 * WINDOW, S)
    assert nb % nbs == 0 and nbs % kdiv == 0
    nkv = nbs // kdiv
    kvw = nkv * LANES

    def body(q_ref, k_ref, kT_ref, v_ref, o_ref, do_ref, lse_ref, dist_ref, slope_ref, sink_ref, dq_ref, dk_ref, dv_ref, dsink_ref, ds_acc):
        i = pl.program_id(1)
        rlo = _row_lo()
        lo = lax.broadcasted_iota(jnp.int32, (1, LANES), 1) < HEAD_DIM

        @pl.when(i == 0)
        def _():
            dk_ref[...] = jnp.zeros((S, kvw), F32)
            dv_ref[...] = jnp.zeros((S, kvw), F32)
            ds_acc[...] = jnp.zeros((nbs, 2 * tq), F32)

        k0 = _win_start(i, tq, nk, S)
        rows = pl.ds(k0, nk)
        kk_all, vv_all, kkT_all, dd = k_ref[rows, :], v_ref[rows, :], kT_ref[:, rows], dist_ref[0]
        dv_sum, dk_sum = [None] * nkv, [None] * nkv
        for u in range(nbs):
            g = u // kdiv
            kv = slice(LANES * g, LANES * (g + 1))
            kk, vv, kkT = kk_all[:, kv], vv_all[:, kv], kkT_all[kv, :]
            cols = slice(LANES * u, LANES * (u + 1))
            qv = q_ref[:, cols]
            qs = _stack_rows(qv, lo)
            qsT = _stack_cols(qv.astype(F32).T, rlo).astype(BF16)
            dov = do_ref[:, cols]
            dos = _stack_rows(dov.astype(BF16), lo)
            dosT = _stack_cols(dov.T, rlo).astype(BF16)
            prodT = (dov * o_ref[:, cols]).T
            delta = jnp.concatenate([jnp.sum(jnp.where(rlo, prodT, 0.0), axis=0, keepdims=True),
                                     jnp.sum(jnp.where(rlo, 0.0, prodT), axis=0, keepdims=True)], axis=1)
            lse_v = lse_ref[u, 0]
            ds_acc[u:u + 1, :] += -jnp.exp(sink_ref[u] - lse_v) * delta
            p = jnp.exp(_dot(kk, qsT) - slope_ref[u] * dd - lse_v)
            ds = (p * (_dot(vv, dosT) - delta)).astype(BF16)
            dv_u, dk_u = _dot(p.astype(BF16), dos), _dot(ds, qs)
            dv_sum[g] = dv_u if dv_sum[g] is None else dv_sum[g] + dv_u
            dk_sum[g] = dk_u if dk_sum[g] is None else dk_sum[g] + dk_u
            dq_ref[:, cols] = (_pick_halves_T(_dot(kkT, ds), rlo, tq) * 0.125).astype(BF16)
        dv_ref[rows, :] += jnp.concatenate(dv_sum, axis=1)
        dk_ref[rows, :] += jnp.concatenate(dk_sum, axis=1)

        @pl.when(i == nq - 1)
        def _():
            acc = ds_acc[...]
            for u in range(nbs):
                dsink_ref[u] = jnp.concatenate(
                    [jnp.broadcast_to(jnp.sum(acc[u:u + 1, 0:tq], axis=1, keepdims=True), (1, LANES)),
                     jnp.broadcast_to(jnp.sum(acc[u:u + 1, tq:2 * tq], axis=1, keepdims=True), (1, LANES)),
                     jnp.zeros((6, LANES), F32)], axis=0)

    qmap = lambda b, i: (i, b)
    kv_spec = pl.BlockSpec((S, kvw), lambda b, i: (0, b))
    row_spec = pl.BlockSpec((nbs, 1, 2 * tq), lambda b, i: (b, 0, 0))
    wide = pl.BlockSpec((tq, nbs * LANES), qmap)
    return pl.pallas_call(
        body, name=name, grid=(nb // nbs, nq),
        in_specs=[wide, kv_spec, pl.BlockSpec((kvw, S), lambda b, i: (b, 0)), kv_spec, wide, wide,
                  pl.BlockSpec((nbs, 1, 1, 2 * tq), lambda b, i: (b, i, 0, 0)), _win_dist_spec(nk, tq, nq), row_spec, row_spec],
        out_specs=[wide, kv_spec, kv_spec, pl.BlockSpec((nbs, 8, LANES), lambda b, i: (b, 0, 0))],
        out_shape=[_sds((S, nb * LANES), BF16), _sds((S, nkb * LANES), F32), _sds((S, nkb * LANES), F32), _sds((nb, 8, LANES), F32)],
        scratch_shapes=[pltpu.VMEM((nbs, 2 * tq), F32)],
        compiler_params=_params(("arbitrary", "arbitrary")))(q, k, kT, v, o, do, lse, dist, slope, sink)


def _sum_rows(v):
    return jnp.sum(v, axis=0, keepdims=True)


def _norm_mod_bwd(dh, xv, mod_ref, nw_ref, stats_ref):
    r = _rms(xv)
    xn = xv * r
    nw = nw_ref[...]
    stats_ref[0:1, :] += _sum_rows(dh)
    stats_ref[1:2, :] += _sum_rows(dh * (xn * nw))
    dn = dh * (1.0 + mod_ref[1:2, :])
    stats_ref[2:3, :] += _sum_rows(dn * xn)
    return _rms_bwd(xv, r, dn * nw)


def _even_gate_specs(ts):
    return [pl.BlockSpec((ts, 256), lambda i, c=c: (i, c)) for c in (3, 4, 7, 8)]


def _even_post_fwd(oa, olat, proj, x, gate, wuv, woe):
    S = x.shape[0]
    ts = min(ROW_TILE, S)

    def body(oa_ref, ol_ref, ga0_ref, ga1_ref, gb0_ref, gb1_ref, x_ref, gate_ref, wuv_ref, woe_ref, y_ref, x1_ref):
        sa, _ = _silu_and_grad(jnp.concatenate([ga0_ref[...], ga1_ref[...]], axis=1))
        sb, _ = _silu_and_grad(jnp.concatenate([gb0_ref[...], gb1_ref[...]], axis=1))
        olc = jnp.concatenate([ol_ref[hh] for hh in range(B_HEADS)], axis=1).astype(BF16)
        ob = _dot(olc, wuv_ref[...])
        mix = jnp.concatenate([oa_ref[...] * sa, ob * sb], axis=1).astype(BF16)
        y = _dot(mix, woe_ref[...])
        y_ref[...] = y
        x1_ref[...] = x_ref[...] + gate_ref[...] * y

    return pl.pallas_call(
        body, name="even_post_fwd", grid=(S // ts,),
        in_specs=[_row_spec(ts, 512), pl.BlockSpec((B_HEADS, ts, LANES), lambda i: (0, i, 0))] + _even_gate_specs(ts) +
                 [_row_spec(ts, D_MODEL), _full_spec((1, D_MODEL)), _full_spec((1024, 512)), _full_spec((1024, D_MODEL))],
        out_specs=[_row_spec(ts, D_MODEL), _row_spec(ts, D_MODEL)],
        out_shape=[_sds((S, D_MODEL), F32), _sds((S, D_MODEL), F32)],
        compiler_params=_params(("arbitrary",)),
    )(oa, olat, proj, proj, proj, proj, x, gate, wuv, woe)


def _odd_pre_fwd(x, mod, nw, wio):
    S = x.shape[0]
    ts = min(ROW_TILE, S)

    def body(x_ref, mod_ref, nw_ref, wio_ref, h_ref, proj_ref, q_ref, k_ref, v_ref, kt_ref, vt_ref):
        xv = x_ref[...]
        h = (xv * _rms(xv) * nw_ref[...]) * (1.0 + mod_ref[1:2, :]) + mod_ref[0:1, :]
        hb = h.astype(BF16)
        h_ref[...] = hb
        proj = jnp.concatenate([_dot(hb, wio_ref[p]) for p in range(N_CHIPS)], axis=1)
        proj_ref[...] = proj
        q_ref[...] = (proj[:, 0:1024] * 0.125).astype(BF16)
        lane = _lane_iota()
        k_v = jnp.concatenate([_dup_heads(proj[:, 1024 + LANES * j:1024 + LANES * (j + 1)], lane) for j in range(2)], axis=1)
        v_v = jnp.concatenate([_dup_heads(proj[:, 1280 + LANES * j:1280 + LANES * (j + 1)], lane) for j in range(2)], axis=1)
        k_ref[...] = k_v.astype(BF16)
        v_ref[...] = v_v.astype(BF16)
        kt_ref[...] = k_v.T.astype(BF16)
        vt_ref[...] = v_v.T.astype(BF16)

    col_spec = pl.BlockSpec((512, ts), lambda i: (0, i))
    return pl.pallas_call(
        body, name="odd_pre_fwd", grid=(S // ts,),
        in_specs=[_row_spec(ts, D_MODEL), _full_spec((3, D_MODEL)), _full_spec((1, D_MODEL)),
                  _full_spec((N_CHIPS, D_MODEL, ODD_IN // N_CHIPS))],
        out_specs=[_row_spec(ts, D_MODEL), _row_spec(ts, ODD_IN), _row_spec(ts, 1024), _row_spec(ts, 512), _row_spec(ts, 512),
                   col_spec, col_spec],
        out_shape=[_sds((S, D_MODEL), BF16), _sds((S, ODD_IN), F32), _sds((S, 1024), BF16), _sds((S, 512), BF16),
                   _sds((S, 512), BF16), _sds((512, S), BF16), _sds((512, S), BF16)],
        compiler_params=_params(("arbitrary",)),
    )(x, mod, nw, wio)


def _odd_post(oc, proj, x1, gate, woo, fw, tgt):
    S = x1.shape[0]
    ts = min(ROW_TILE, S)
    nsteps = S // ts

    def body(oc_ref, g0_ref, g1_ref, x_ref, gate_ref, woo_ref, fw_ref, tgt_ref, doc_ref, dgc_ref, dx2_ref, dwoo_out, stats_ref,
             dwoo_ref):
        @pl.when(pl.program_id(0) == 0)
        def _():
            dwoo_ref[...] = jnp.zeros((D_MODEL, D_MODEL), F32)
            stats_ref[...] = jnp.zeros((8, D_MODEL), F32)

        ocv = oc_ref[...]
        sg, dsg = _silu_and_grad(jnp.concatenate([g0_ref[...], g1_ref[...]], axis=1))
        mix = (ocv * sg).astype(BF16)
        woo_v = woo_ref[...]
        y = _dot(mix, woo_v)
        gate_v = gate_ref[...]
        x2 = x_ref[...] + gate_v * y
        r = _rms(x2)
        fw_v = fw_ref[...]
        xn = x2 * r
        err = xn * fw_v - tgt_ref[...]
        dout = err * (1.0 / D_MODEL)
        dx2 = _rms_bwd(x2, r, dout * fw_v)
        dx2_ref[...] = dx2
        stats_ref[0:1, :] += _sum_rows(dout * xn)
        stats_ref[1:2, :] += _sum_rows(dx2 * y)
        loss_t = 0.5 * jnp.sum(_sum_rows(err * dout), axis=-1, keepdims=True)
        stats_ref[2:3, :] += jnp.broadcast_to(loss_t, (1, D_MODEL))
        dy = (gate_v * dx2).astype(BF16)
        dmix = _dot_nt(dy, woo_v)
        dwoo_ref[...] += _dot_tn(mix, dy)
        doc_ref[...] = dmix * sg
        dgc_ref[...] = (dmix * ocv * dsg).astype(BF16)

        @pl.when(pl.program_id(0) == nsteps - 1)
        def _():
            dwoo_out[...] = dwoo_ref[...].astype(BF16)

    gate_cols = [pl.BlockSpec((ts, 512), lambda i, c=c: (i, c)) for c in (3, 4)]
    return pl.pallas_call(
        body, name="odd_post", grid=(nsteps,),
        in_specs=[_row_spec(ts, D_MODEL)] + gate_cols + [_row_spec(ts, D_MODEL), _full_spec((1, D_MODEL)),
                  _full_spec((D_MODEL, D_MODEL)), _full_spec((1, D_MODEL)), _row_spec(ts, D_MODEL)],
        out_specs=[_row_spec(ts, D_MODEL), _row_spec(ts, D_MODEL), _row_spec(ts, D_MODEL),
                   _full_spec((D_MODEL, D_MODEL), single=False), _full_spec((8, D_MODEL), single=False)],
        out_shape=[_sds((S, D_MODEL), F32), _sds((S, D_MODEL), BF16), _sds((S, D_MODEL), F32), _sds((D_MODEL, D_MODEL), BF16),
                   _sds((8, D_MODEL), F32)],
        scratch_shapes=[pltpu.VMEM((D_MODEL, D_MODEL), F32)],
        compiler_params=_params(("arbitrary",)),
    )(oc, proj, proj, x1, gate, woo, fw, tgt)


def _odd_pre_bwd(dq, dk, dv, dgc, h, x, dx_res, mod, nw, wio):
    S = x.shape[0]
    ts = min(ROW_TILE, S)
    nsteps = S // ts
    wsh = ODD_IN // N_CHIPS

    def body(dq_ref, dk_ref, dv_ref, dgc_ref, h_ref, x_ref, dxr_ref, mod_ref, nw_ref, wio_ref, dx_ref, dw_ref, stats_ref, dw_acc):
        @pl.when(pl.program_id(0) == 0)
        def _():
            dw_acc[...] = jnp.zeros((N_CHIPS, D_MODEL, wsh), F32)
            stats_ref[...] = jnp.zeros((8, D_MODEL), F32)

        lane = _lane_iota()
        dkv = [_fold_heads(r[:, 2 * LANES * j:2 * LANES * (j + 1)], lane).astype(BF16) for r in (dk_ref, dv_ref) for j in range(2)]
        dproj = jnp.concatenate([dq_ref[...]] + dkv + [dgc_ref[...]], axis=1)
        hv = h_ref[...]
        dh = None
        for p in range(N_CHIPS):
            dp_cols = dproj[:, wsh * p:wsh * (p + 1)]
            part = _dot_nt(dp_cols, wio_ref[p])
            dh = part if dh is None else dh + part
            dw_acc[p] += _dot_tn(hv, dp_cols)
        dx_ref[...] = dxr_ref[...] + _norm_mod_bwd(dh, x_ref[...], mod_ref, nw_ref, stats_ref)

        @pl.when(pl.program_id(0) == nsteps - 1)
        def _():
            dw_ref[...] = dw_acc[...].astype(BF16)

    return pl.pallas_call(
        body, name="odd_pre_bwd", grid=(nsteps,),
        in_specs=[_row_spec(ts, 1024), _row_spec(ts, 512), _row_spec(ts, 512), _row_spec(ts, 1024), _row_spec(ts, D_MODEL),
                  _row_spec(ts, D_MODEL), _row_spec(ts, D_MODEL), _full_spec((3, D_MODEL)), _full_spec((1, D_MODEL)),
                  _full_spec((N_CHIPS, D_MODEL, wsh))],
        out_specs=[_row_spec(ts, D_MODEL), _full_spec((N_CHIPS, D_MODEL, wsh), single=False), _full_spec((8, D_MODEL), single=False)],
        out_shape=[_sds((S, D_MODEL), F32), _sds((N_CHIPS, D_MODEL, wsh), BF16), _sds((8, D_MODEL), F32)],
        scratch_shapes=[pltpu.VMEM((N_CHIPS, D_MODEL, wsh), F32)],
        compiler_params=_params(("arbitrary",)),
    )(dq, dk, dv, dgc, h, x, dx_res, mod, nw, wio)


def _even_post_bwd(dx1, y, oa, olat, proj, gate, wuv, woe):
    S = dx1.shape[0]
    ts = min(ROW_TILE, S)
    nsteps = S // ts

    def body(dx_ref, y_ref, oa_ref, ol_ref, ga0_ref, ga1_ref, gb0_ref, gb1_ref, gate_ref, wuv_ref, woe_ref,
             doa_ref, dga_ref, dgb_ref, dol_ref, dwoe_out, dwuv_ref, stats_ref, dwoe_ref):
        @pl.when(pl.program_id(0) == 0)
        def _():
            dwoe_ref[...] = jnp.zeros((D_MODEL, D_MODEL), F32)
            dwuv_ref[...] = jnp.zeros((1024, 512), F32)
            stats_ref[...] = jnp.zeros((8, D_MODEL), F32)

        dxv = dx_ref[...]
        stats_ref[0:1, :] += _sum_rows(dxv * y_ref[...])
        dy = (gate_ref[...] * dxv).astype(BF16)
        sa, dsa = _silu_and_grad(jnp.concatenate([ga0_ref[...], ga1_ref[...]], axis=1))
        sb, dsb = _silu_and_grad(jnp.concatenate([gb0_ref[...], gb1_ref[...]], axis=1))
        olc = jnp.concatenate([ol_ref[hh] for hh in range(B_HEADS)], axis=1).astype(BF16)
        wuv_v = wuv_ref[...]
        ob = _dot(olc, wuv_v)
        oav = oa_ref[...]
        mix = jnp.concatenate([oav * sa, ob * sb], axis=1).astype(BF16)
        dmix = _dot_nt(dy, woe_ref[...])
        dwoe_ref[...] += _dot_tn(mix, dy)
        dma, dmb = dmix[:, 0:512], dmix[:, 512:1024]
        doa_ref[...] = dma * sa
        dga_ref[...] = (dma * oav * dsa).astype(BF16)
        dgb_ref[...] = (dmb * ob * dsb).astype(BF16)
        dob = (dmb * sb).astype(BF16)
        dol = _dot_nt(dob, wuv_v)
        dwuv_ref[...] += _dot_tn(olc, dob)
        for hh in range(B_HEADS):
            dol_ref[hh] = dol[:, LANES * hh:LANES * (hh + 1)]

        @pl.when(pl.program_id(0) == nsteps - 1)
        def _():
            dwoe_out[...] = dwoe_ref[...].astype(BF16)

    head_spec = pl.BlockSpec((B_HEADS, ts, LANES), lambda i: (0, i, 0))
    return pl.pallas_call(
        body, name="even_post_bwd", grid=(nsteps,),
        in_specs=[_row_spec(ts, D_MODEL), _row_spec(ts, D_MODEL), _row_spec(ts, 512), head_spec] + _even_gate_specs(ts) +
                 [_full_spec((1, D_MODEL)), _full_spec((1024, 512)), _full_spec((1024, D_MODEL))],
        out_specs=[_row_spec(ts, 512), _row_spec(ts, 512), _row_spec(ts, 512), head_spec,
                   _full_spec((D_MODEL, D_MODEL), single=False), _full_spec((1024, 512), single=False),
                   _full_spec((8, D_MODEL), single=False)],
        out_shape=[_sds((S, 512), F32), _sds((S, 512), BF16), _sds((S, 512), BF16), _sds((B_HEADS, S, LANES), F32),
                   _sds((D_MODEL, D_MODEL), BF16), _sds((1024, 512), F32), _sds((8, D_MODEL), F32)],
        scratch_shapes=[pltpu.VMEM((D_MODEL, D_MODEL), F32)],
        compiler_params=_params(("arbitrary",)),
    )(dx1, y, oa, olat, proj, proj, proj, proj, gate, wuv, woe)


def _even_pre_bwd(x, h, proj, dqa, dka, dva, dga, dgb, dqcat, dkcat, dx_res, mod, nw, wie, qn, kn, seg, ca, sa, ct, st,
                  qln, kvln, wuq, wuk):
    S = x.shape[0]
    ts = min(ROW_TILE, S)
    nsteps = S // ts

    def body(x_ref, h_ref, proj_ref, dqa_ref, dka_ref, dva_ref, dga_ref, dgb_ref, dqc_ref, dkc_ref, dxr_ref, mod_ref, nw_ref,
             wie_ref, qn_ref, kn_ref, seg_ref, ca_ref, sa_ref, ct_ref, st_ref, qln_ref, kvln_ref, wuq_ref, wuk_ref,
             dx_ref, dwie_out, dwuq_out, dwuk_out, stats_ref, nstats_ref, dwie_ref, dwuq_ref, dwuk_ref):
        @pl.when(pl.program_id(0) == 0)
        def _():
            dwie_ref[...] = jnp.zeros((D_MODEL, EVEN_P), F32)
            dwuq_ref[...] = jnp.zeros((B_Q_LORA, 1536), F32)
            dwuk_ref[...] = jnp.zeros((512, 1024), F32)
            stats_ref[...] = jnp.zeros((8, D_MODEL), F32)
            nstats_ref[...] = jnp.zeros((8, 256), F32)

        lane = _lane_iota()
        ca_v, sa_v, ct_v, st_v = ca_ref[...], sa_ref[...], ct_ref[...], st_ref[...]
        seg_v = seg_ref[...]

        def head_norm_bwd(xc, dy, w):
            r = lax.rsqrt(_seg_mean(xc * xc, seg_v) + EPS)
            g = dy * w
            dxc = r * g - xc * (r * r * r) * _seg_mean(xc * g, seg_v)
            return dxc, _sum_rows(dy * (xc * r))

        pieces = []
        dqn = jnp.zeros((1, LANES), F32)
        for cb in range(4):
            sl = slice(LANES * cb, LANES * (cb + 1))
            dy = _rot_bwd(dqa_ref[:, sl] * 0.125, ca_v, sa_v, lane)
            dxc, dw = head_norm_bwd(proj_ref[:, sl], dy, qn_ref[...])
            pieces.append(dxc)
            dqn = dqn + dw
        dxc, dkn = head_norm_bwd(proj_ref[:, 512:640], _rot_bwd(_fold_heads(dka_ref[...], lane), ca_v, sa_v, lane), kn_ref[...])
        pieces += [dxc, _fold_heads(dva_ref[...], lane), dga_ref[...]]
        nstats_ref[0:1, 0:LANES] += dqn + pltpu.roll(dqn, HEAD_DIM, 1)
        nstats_ref[1:2, 0:LANES] += dkn + pltpu.roll(dkn, HEAD_DIM, 1)

        cq = proj_ref[:, 1280:1536]
        rq = _rms(cq)
        cqn_f = cq * rq
        qln_v = qln_ref[...]
        cqn = (cqn_f * qln_v).astype(BF16)
        wuq_v, wuk_v = wuq_ref[...], wuk_ref[...]
        qnope = _dot(cqn, wuq_v[:, 0:512]).astype(BF16)
        dqlat = jnp.concatenate([dqc_ref[hh, :, 0:LANES] for hh in range(B_HEADS)], axis=1).astype(BF16)
        dqnope = _dot_nt(dqlat, wuk_v)
        dwuk_ref[...] += _dot_tn(qnope, dqlat)
        dqr = [_rot_bwd(dqc_ref[hh, :, LANES:2 * LANES], ct_v, st_v, lane) for hh in range(B_HEADS)]
        dqb = jnp.concatenate([dqnope] + dqr, axis=1).astype(BF16)
        dcqn = _dot_nt(dqb, wuq_v)
        dwuq_ref[...] += _dot_tn(cqn, dqb)
        nstats_ref[2:3, :] += _sum_rows(dcqn * cqn_f)
        dcq = _rms_bwd(cq, rq, dcqn * qln_v)
        ckv = proj_ref[:, 1536:1664]
        rk = _rms(ckv)
        dckvn = dkc_ref[:, 0:LANES]
        nstats_ref[3:4, 0:LANES] += _sum_rows(dckvn * (ckv * rk))
        dckv = _rms_bwd(ckv, rk, dckvn * kvln_ref[...])
        dkr = _rot_bwd(dkc_ref[:, LANES:2 * LANES], ct_v, st_v, lane)
        pieces += [dcq, dckv, dkr, dgb_ref[...]]
        dproj = jnp.concatenate([piece.astype(BF16) for piece in pieces], axis=1)
        dh = _dot_nt(dproj, wie_ref[...])
        dwie_ref[...] += _dot_tn(h_ref[...], dproj)
        dx_ref[...] = dxr_ref[...] + _norm_mod_bwd(dh, x_ref[...], mod_ref, nw_ref, stats_ref)

        @pl.when(pl.program_id(0) == nsteps - 1)
        def _():
            pltpu.sync_copy(dwie_ref, dwie_out)
            pltpu.sync_copy(dwuq_ref, dwuq_out)
            pltpu.sync_copy(dwuk_ref, dwuk_out)

    return pl.pallas_call(
        body, name="even_pre_bwd", grid=(nsteps,),
        in_specs=[_row_spec(ts, D_MODEL), _row_spec(ts, D_MODEL), _row_spec(ts, EVEN_P), _row_spec(ts, 512), _row_spec(ts, 2 * LANES),
                  _row_spec(ts, 2 * LANES), _row_spec(ts, 512), _row_spec(ts, 512),
                  pl.BlockSpec((B_HEADS, ts, 2 * LANES), lambda i: (0, i, 0)), _row_spec(ts, 2 * LANES), _row_spec(ts, D_MODEL),
                  _full_spec((3, D_MODEL)), _full_spec((1, D_MODEL)), _full_spec((D_MODEL, EVEN_P)),
                  _full_spec((1, LANES)), _full_spec((1, LANES)), _full_spec((LANES, LANES)),
                  _row_spec(ts, LANES), _row_spec(ts, LANES), _row_spec(ts, LANES), _row_spec(ts, LANES),
                  _full_spec((1, B_Q_LORA)), _full_spec((1, B_KV_LORA)), _full_spec((B_Q_LORA, 1536)), _full_spec((512, 1024))],
        out_specs=[_row_spec(ts, D_MODEL), _ANY, _ANY, _ANY, _full_spec((8, D_MODEL), single=False), _full_spec((8, 256), single=False)],
        out_shape=[_sds((S, D_MODEL), F32), _sds((D_MODEL, EVEN_P), F32), _sds((B_Q_LORA, 1536), F32), _sds((512, 1024), F32),
                   _sds((8, D_MODEL), F32), _sds((8, 256), F32)],
        scratch_shapes=[pltpu.VMEM((D_MODEL, EVEN_P), F32), pltpu.VMEM((B_Q_LORA, 1536), F32), pltpu.VMEM((512, 1024), F32)],
        compiler_params=_params(("arbitrary",)),
    )(x, h, proj, dqa, dka, dva, dga, dgb, dqcat, dkcat, dx_res, mod, nw, wie, qn, kn, seg, ca, sa, ct, st, qln, kvln, wuq, wuk)


def _ada_fwd(c_all, w, b):
    n = w.shape[2]

    def body(c_ref, w_ref, b_ref, o_ref):
        cv = c_ref[...]
        o_ref[0] = _dot_f32(cv * _sigmoid(cv), w_ref[0]) + b_ref[0]

    return pl.pallas_call(
        body, name="ada_fwd", grid=(2,),
        in_specs=[pl.BlockSpec((N_DEV, D_MODEL), lambda l: (0, 0)), pl.BlockSpec((1, D_MODEL, n), lambda l: (l, 0, 0)),
                  pl.BlockSpec((1, 1, n), lambda l: (l, 0, 0))],
        out_specs=pl.BlockSpec((1, N_DEV, n), lambda l: (l, 0, 0)),
        out_shape=_sds((2, N_DEV, n), F32),
        compiler_params=_params(("arbitrary",)),
    )(c_all, w, b)


def _ada_bwd(c_all_t, dmod):
    n = dmod.shape[2]

    def body(c_ref, d_ref, o_ref):
        cv = c_ref[...]
        act = cv * _sigmoid(cv)
        dv = d_ref[0]
        acc = act[:, 0:1] * dv[0:1, :]
        for bb in range(1, N_DEV):
            acc = acc + act[:, bb:bb + 1] * dv[bb:bb + 1, :]
        o_ref[0] = acc

    return pl.pallas_call(
        body, name="ada_bwd", grid=(2,),
        in_specs=[pl.BlockSpec((D_MODEL, N_DEV), lambda l: (0, 0)), pl.BlockSpec((1, N_DEV, n), lambda l: (l, 0, 0))],
        out_specs=pl.BlockSpec((1, D_MODEL, n), lambda l: (l, 0, 0)),
        out_shape=_sds((2, D_MODEL, n), F32),
        compiler_params=_params(("arbitrary",)),
    )(c_all_t, dmod)


ADAM_ROW_TILE = 256


def _adam_update(g, w, m, v):
    m_new = ADAM_B1 * m + (1.0 - ADAM_B1) * g
    v_new = ADAM_B2 * v + (1.0 - ADAM_B2) * jnp.square(g)
    m_hat = m_new / (1.0 - ADAM_B1 ** ADAM_STEP)
    v_hat = v_new / (1.0 - ADAM_B2 ** ADAM_STEP)
    return -ADAM_LR * (m_hat / (jnp.sqrt(v_hat) + ADAM_EPS) + ADAM_WD * w), m_new, v_new


SMALL_ROWS = dict(dmod=(0, D_MODEL), norm_w=(6, D_MODEL), final_norm=(8, D_MODEL), a_q_norm=(9, HEAD_DIM), a_k_norm=(10, HEAD_DIM),
                  b_q_lora_norm=(11, B_Q_LORA), b_kv_lora_norm=(12, B_KV_LORA), c_sink=(13, C_HEADS))
SMALL_WEIGHTS = ("ada_b", "norm_w", "final_norm", "a_q_norm", "a_k_norm", "b_q_lora_norm", "b_kv_lora_norm", "c_sink")
LOSS_ROW = 14


def _pack_small(res):
    def padded(v):
        return jnp.concatenate([v, jnp.zeros((v.shape[0], D_MODEL - v.shape[1]), F32)], axis=1)

    rows = [res["dmod"].reshape(6, D_MODEL), res["norm_w"], res["final_norm"].reshape(1, D_MODEL)]
    rows += [padded(res[k]) for k in ("a_q_norm", "a_k_norm", "b_q_lora_norm", "b_kv_lora_norm", "c_sink")]
    return jnp.concatenate(rows + [res["loss_row"], jnp.zeros((1, D_MODEL), F32)], axis=0)


def _adam_small(parts, ws, ms, vs):
    nw = len(SMALL_WEIGHTS)

    def body(*refs):
        p_ref = refs[0]
        w_refs, m_refs, v_refs = refs[1:1 + nw], refs[1 + nw:1 + 2 * nw], refs[1 + 2 * nw:1 + 3 * nw]
        outs = refs[1 + 3 * nw:]
        g_all = p_ref[0]
        for k in range(1, N_DEV):
            g_all = g_all + p_ref[k]
        for idx, name in enumerate(SMALL_WEIGHTS):
            if name == "ada_b":
                g = jnp.concatenate([jnp.concatenate([g_all[3 * l + t:3 * l + t + 1] for t in range(3)], axis=1) for l in range(2)],
                                    axis=0)
            else:
                row, width = SMALL_ROWS[name]
                g = g_all[row:row + w_refs[idx].shape[0], 0:width]
            d, m_new, v_new = _adam_update(g, w_refs[idx][...], m_refs[idx][...], v_refs[idx][...])
            outs[4 * idx][...], outs[4 * idx + 1][...], outs[4 * idx + 2][...], outs[4 * idx + 3][...] = g, d, m_new, v_new
        outs[4 * nw][...] = g_all[LOSS_ROW:LOSS_ROW + 1, 0:LANES]

    out_shape = []
    for w in ws:
        out_shape += [_sds(w.shape, F32)] * 4
    out_shape.append(_sds((1, LANES), F32))
    return pl.pallas_call(body, name="adam_small", out_shape=out_shape,
                          compiler_params=pltpu.CompilerParams(vmem_limit_bytes=VMEM_LIMIT))(parts, *ws, *ms, *vs)


def _adam(parts, w, m, v, name):
    P, R, C = parts.shape
    tr = R if R <= ADAM_ROW_TILE else ADAM_ROW_TILE
    assert R % tr == 0

    def body(p_ref, w_ref, m_ref, v_ref, g_ref, d_ref, nm_ref, nv_ref):
        g = p_ref[0].astype(F32)
        for k in range(1, P):
            g = g + p_ref[k].astype(F32)
        g_ref[...] = g
        d_ref[...], nm_ref[...], nv_ref[...] = _adam_update(g, w_ref[...], m_ref[...], v_ref[...])

    spec = pl.BlockSpec((tr, C), lambda i: (i, 0))
    return pl.pallas_call(
        body, name=name, grid=(R // tr,),
        in_specs=[pl.BlockSpec((P, tr, C), lambda i: (0, i, 0)), spec, spec, spec],
        out_specs=[spec, spec, spec, spec], out_shape=[_sds((R, C), F32)] * 4,
        compiler_params=_params(("arbitrary",)),
    )(parts, w, m, v)


_ANY = pl.BlockSpec(memory_space=pl.ANY)
CHIP_FLIPS = ((1, 0), (0, 1), (1, 1))
DEV_FLIPS = tuple((dx, dy, dc) for dx in (0, 1) for dy in (0, 1) for dc in (0, 1) if dx + dy + dc)


def _flip(a, d):
    return a if d == 0 else 1 - a


def _my_place():
    return lax.axis_index("x"), lax.axis_index("y"), lax.axis_index("c")


def _gather8_copies(ins, outs, send_sems, recv_sems, loc_sems):
    x, y, c = _my_place()
    me = 4 * x + 2 * y + c
    copies = []
    for a in range(len(ins)):
        copies.append(pltpu.make_async_copy(ins[a], outs[a].at[me], loc_sems.at[a]))
        for k, (dx, dy, dc) in enumerate(DEV_FLIPS):
            copies.append(pltpu.make_async_remote_copy(
                src_ref=ins[a], dst_ref=outs[a].at[me], send_sem=send_sems.at[a, k], recv_sem=recv_sems.at[a, k],
                device_id=(_flip(x, dx), _flip(y, dy), _flip(c, dc)), device_id_type=MESH_ID))
    return copies


def _gather8_sems(n):
    return [pltpu.SemaphoreType.DMA((n, 7)), pltpu.SemaphoreType.DMA((n, 7)), pltpu.SemaphoreType.DMA((n,))]


def _gather_dev8(arrs, name):
    n = len(arrs)

    def body(*refs):
        copies = _gather8_copies(refs[:n], refs[n:2 * n], *refs[2 * n:])
        for cp in copies:
            cp.start()
        for cp in copies:
            cp.wait()

    return pl.pallas_call(
        body, name=name, in_specs=[_ANY] * n, out_specs=[_ANY] * n,
        out_shape=[_sds((N_DEV,) + a.shape, a.dtype) for a in arrs], scratch_shapes=_gather8_sems(n),
    )(*arrs)


class _Exchange:
    def __init__(self, arrs, out_shapes, n_sems, phases):
        self.arrs, self.out_shapes, self.n_sems, self._phases = list(arrs), list(out_shapes), n_sems, phases

    @property
    def n(self):
        return len(self.arrs)

    def sem_shapes(self):
        return [pltpu.SemaphoreType.DMA((self.n, self.n_sems)), pltpu.SemaphoreType.DMA((self.n, self.n_sems)),
                pltpu.SemaphoreType.DMA((self.n,))]

    def phases(self, ins, outs, sems):
        return self._phases(ins, outs, *sems)

    def run(self, name):
        n = self.n

        def body(*refs):
            start, mid, end = self.phases(refs[:n], refs[n:2 * n], refs[2 * n:])
            start()
            mid()
            end()

        return pl.pallas_call(body, name=name, in_specs=[_ANY] * n, out_specs=[_ANY] * n, out_shape=self.out_shapes,
                              scratch_shapes=self.sem_shapes())(*self.arrs)

    def run_with_gather(self, gather_arrs, name):
        n, g = self.n, len(gather_arrs)

        def body(*refs):
            ins, g_ins, outs, g_outs = refs[:n], refs[n:n + g], refs[n + g:2 * n + g], refs[2 * n + g:2 * (n + g)]
            sems = refs[2 * (n + g):]
            start, mid, end = self.phases(ins, outs, sems[:3])
            copies = _gather8_copies(g_ins, g_outs, *sems[3:])
            start()
            for cp in copies:
                cp.start()
            mid()
            end()
            for cp in copies:
                cp.wait()

        outs = pl.pallas_call(
            body, name=name, in_specs=[_ANY] * (n + g), out_specs=[_ANY] * (n + g),
            out_shape=self.out_shapes + [_sds((N_DEV,) + a.shape, a.dtype) for a in gather_arrs],
            scratch_shapes=self.sem_shapes() + _gather8_sems(g))(*self.arrs, *gather_arrs)
        return outs[:n], outs[n:]


def _gather_halves_phases(ins, outs, send_sems, recv_sems, loc_sems):
    n = len(ins)
    x, y, c = _my_place()
    chip = 2 * x + y
    sibling = (x, y, 1 - c)
    peers = [(_flip(x, dx), _flip(y, dy)) for dx, dy in CHIP_FLIPS]

    def remote(src, p, half, a, k, to):
        return pltpu.make_async_remote_copy(src_ref=src, dst_ref=outs[a].at[p, half], send_sem=send_sems.at[a, k],
                                            recv_sem=recv_sems.at[a, k], device_id=to, device_id_type=MESH_ID)

    def local(a):
        return pltpu.make_async_copy(ins[a], outs[a].at[chip], loc_sems.at[a])

    def first(a, k):
        return remote(ins[a].at[c], chip, c, a, k, (*peers[k], c))

    def passed(a, k):
        p = 2 * peers[k][0] + peers[k][1]
        return remote(outs[a].at[p, c], p, c, a, 3 + k, sibling)

    def start():
        for a in range(n):
            local(a).start()
            for k in range(3):
                first(a, k).start()

    def mid():
        for a in range(n):
            for k in range(3):
                p = 2 * peers[k][0] + peers[k][1]
                remote(outs[a].at[p, c], p, c, a, k, sibling).wait_recv()
                passed(a, k).start()

    def end():
        for a in range(n):
            for k in range(3):
                p = 2 * peers[k][0] + peers[k][1]
                remote(outs[a].at[p, 1 - c], p, 1 - c, a, 3 + k, sibling).wait_recv()
        for a in range(n):
            for k in range(3):
                first(a, k).wait_send()
                passed(a, k).wait_send()
            local(a).wait()

    return start, mid, end


def _gather_chip4_halves(arrs):
    return _Exchange(arrs, [_sds((N_CHIPS,) + a.shape, a.dtype) for a in arrs], 6, _gather_halves_phases)


def _reduce_phases(ins, outs, send_sems, recv_sems, loc_sems):
    n = len(ins)
    x, y, c = _my_place()
    chip = 2 * x + y
    sibling = (x, y, 1 - c)
    peers = [(_flip(x, dx), _flip(y, dy)) for dx, dy in CHIP_FLIPS]

    def remote(src, slot, a, k, to):
        return pltpu.make_async_remote_copy(src_ref=src, dst_ref=outs[a].at[slot], send_sem=send_sems.at[a, k],
                                            recv_sem=recv_sems.at[a, k], device_id=to, device_id_type=MESH_ID)

    def local(a):
        return pltpu.make_async_copy(ins[a].at[chip], outs[a].at[2 * chip + c], loc_sems.at[a])

    def own(a):
        return remote(ins[a].at[chip], 2 * chip + c, a, 0, sibling)

    def first(a, k):
        return remote(ins[a].at[2 * peers[k][0] + peers[k][1]], 2 * chip + c, a, 1 + k, (*peers[k], c))

    def passed(a, k):
        slot = 2 * (2 * peers[k][0] + peers[k][1]) + c
        return remote(outs[a].at[slot], slot, a, 4 + k, sibling)

    def start():
        for a in range(n):
            local(a).start()
            own(a).start()
            for k in range(3):
                first(a, k).start()

    def mid():
        for a in range(n):
            for k in range(3):
                slot = 2 * (2 * peers[k][0] + peers[k][1]) + c
                remote(outs[a].at[slot], slot, a, 1 + k, sibling).wait_recv()
                passed(a, k).start()

    def end():
        for a in range(n):
            remote(outs[a].at[2 * chip + 1 - c], 2 * chip + 1 - c, a, 0, sibling).wait_recv()
            for k in range(3):
                slot = 2 * (2 * peers[k][0] + peers[k][1]) + 1 - c
                remote(outs[a].at[slot], slot, a, 4 + k, sibling).wait_recv()
        for a in range(n):
            own(a).wait_send()
            for k in range(3):
                first(a, k).wait_send()
                passed(a, k).wait_send()
            local(a).wait()

    return start, mid, end


def _reduce_exchange(arrs):
    return _Exchange(arrs, [_sds((N_DEV,) + a.shape[1:], a.dtype) for a in arrs], 7, _reduce_phases)


def _even_in_layout(w):
    return jnp.concatenate([w[:, 0:1696], jnp.zeros((w.shape[0], 96), w.dtype), w[:, 1696:2208]], axis=1)


def _even_in_unlayout(g):
    return jnp.concatenate([g[:, 0:1696], g[:, 1792:2304]], axis=1)


def _uq_layout(w):
    per = B_NOPE + B_ROPE
    pad = jnp.zeros((w.shape[0], LANES - B_ROPE), w.dtype)
    nope = [w[:, per * h:per * h + B_NOPE] for h in range(B_HEADS)]
    rope = [jnp.concatenate([w[:, per * h + B_NOPE:per * (h + 1)], pad], axis=1) for h in range(B_HEADS)]
    return jnp.concatenate(nope + rope, axis=1)


def _uq_unlayout(g):
    parts = []
    for h in range(B_HEADS):
        parts += [g[:, B_NOPE * h:B_NOPE * (h + 1)], g[:, 512 + LANES * h:512 + LANES * h + B_ROPE]]
    return jnp.concatenate(parts, axis=1)


def _block_diag(blocks):
    rows = []
    for h, blk in enumerate(blocks):
        r, cdim = blk.shape
        n = len(blocks)
        rows.append(jnp.concatenate([jnp.zeros((r, cdim * h), blk.dtype), blk, jnp.zeros((r, cdim * (n - 1 - h)), blk.dtype)],
                                    axis=1))
    return jnp.concatenate(rows, axis=0)


def _uk_layout(w):
    return _block_diag([w[:, h, :].T for h in range(B_HEADS)])


def _uk_unlayout(g):
    return jnp.stack([g[B_NOPE * h:B_NOPE * (h + 1), LANES * h:LANES * (h + 1)].T for h in range(B_HEADS)], axis=1)


def _uv_layout(w):
    return _block_diag([w[:, h, :] for h in range(B_HEADS)])


def _uv_unlayout(g):
    return jnp.stack([g[LANES * h:LANES * (h + 1), B_V * h:B_V * (h + 1)] for h in range(B_HEADS)], axis=1)


def _rope_tables(S):
    inv = ROPE_THETA ** (-jnp.arange(0, 32, 2, dtype=F32) / 32)
    tok = jnp.arange(S)

    def tab(pos):
        ang = pos.astype(F32)[:, None] * inv[None, :]
        cos, sin = jnp.cos(ang), jnp.sin(ang)
        return jnp.concatenate([cos, cos], axis=1), jnp.concatenate([-sin, sin], axis=1)

    cr, sr = tab(tok // GRID_W)
    cc, sc = tab(tok % GRID_W)
    ct, st = tab(tok)
    return (jnp.tile(jnp.concatenate([cr, cc], axis=1), (1, 2)), jnp.tile(jnp.concatenate([sr, sc], axis=1), (1, 2)),
            jnp.tile(ct, (1, 4)), jnp.tile(st, (1, 4)))


A_TQ, A_TK, A_SUB = 512, 4096, 512
B_TQ, B_TK, B_SUB = 128, 4096, 1024
B_BWD_TK, B_BWD_SUB = 4096, 512
C_T = 256
C_BLOCKS_PER_STEP = 8
KV_SHARE = 2


def _local_step(x0, tgt, mod, norm_w, wie, wuq, wuk, wuv, late_shards, a_q_norm, a_k_norm, q_lora_norm, kv_lora_norm,
                c_sink, final_norm):
    S = x0.shape[0]
    mod3 = mod.reshape(2, 3, D_MODEL)
    ca, sa, ct, st = _rope_tables(S)
    lane_seg = np.arange(LANES) // HEAD_DIM
    seg = jnp.asarray((lane_seg[:, None] == lane_seg[None, :]).astype(np.float32)).astype(BF16)
    qn = jnp.tile(a_q_norm.reshape(1, HEAD_DIM), (1, 2))
    kn = jnp.tile(a_k_norm.reshape(1, HEAD_DIM), (1, 2))
    qln, kvln = q_lora_norm.reshape(1, B_Q_LORA), kv_lora_norm.reshape(1, B_KV_LORA)
    nw0, nw1 = norm_w[0:1], norm_w[1:2]
    gate0, gate1 = mod3[0, 2:3], mod3[1, 2:3]
    a_tq, a_tk, b_tq, b_tk, bb_tk, c_t = min(A_TQ, S), min(A_TK, S), min(B_TQ, S), min(B_TK, S), min(B_BWD_TK, S), min(C_T, S)
    a_sub, b_sub, bb_sub = min(A_SUB, a_tk), min(B_SUB, b_tk), min(B_BWD_SUB, bb_tk)

    h0, proj_e, qa, ka, va, qcat, kcat, ka_t, va_t, kcat_t = _even_pre_fwd(x0, mod3[0], nw0, wie, qn, kn, seg, ca, sa, ct, st,
                                                                           qln, kvln, wuq, wuk)
    oa, lse_a, woe_g, wio_g, woo_g = _pp_fwd(qa, ka, va_t, kdiv=KV_SHARE, tq=a_tq, tk=a_tk, sub=a_sub, name="attn_a_fwd",
                                             side=_gather_chip4_halves(late_shards))
    woe = woe_g.reshape(D_MODEL, D_MODEL)
    wio = wio_g.reshape(N_CHIPS, D_MODEL, ODD_IN // N_CHIPS)
    woo = woo_g.reshape(D_MODEL, D_MODEL)
    olat, lse_b = _mla_fwd(qcat, kcat, kcat_t, tq=b_tq, tk=b_tk, sub=b_sub)
    y0, x1 = _even_post_fwd(oa, olat, proj_e, x0, gate0, wuv, woe)
    h1, proj_o, qc, kc, vc, kc_t, vc_t = _odd_pre_fwd(x1, mod3[1], nw1, wio)
    slopes = 2.0 ** (-8.0 * jnp.arange(1, C_HEADS + 1, dtype=F32) / C_HEADS)
    slope_rows = jnp.repeat(slopes.reshape(C_HEADS // 2, 2), c_t, axis=1)[:, None, :]
    sink_rows = jnp.repeat(c_sink.reshape(C_HEADS // 2, 2), c_t, axis=1)[:, None, :]
    win_dist = _win_dist_table(S, c_t)
    oc, lse_c = _win_fwd(qc, kc, vc_t, win_dist, slope_rows, sink_rows, kdiv=KV_SHARE, tq=c_t, nbs=C_BLOCKS_PER_STEP,
                         name="attn_c_fwd")
    doc, dgc, dx2, dwoo, st_f = _odd_post(oc, proj_o, x1, gate1, woo, final_norm.reshape(1, D_MODEL), tgt)
    dqc, dkc, dvc, dsink_raw = _win_bwd(qc, kc, kc_t, vc, oc, doc, lse_c, win_dist, slope_rows, sink_rows, kdiv=KV_SHARE, tq=c_t,
                                        nbs=C_BLOCKS_PER_STEP, name="attn_c_bwd")
    dx1, dwio, st_1 = _odd_pre_bwd(dqc, dkc, dvc, dgc, h1, x1, dx2, mod3[1], nw1, wio)
    doa, dga, dgb, dolat, dwoe, dwuv, st_e = _even_post_bwd(dx1, y0, oa, olat, proj_e, gate0, wuv, woe)
    late_grads = _reduce_exchange([dwoe.reshape(N_CHIPS, D_MODEL // N_CHIPS, D_MODEL), dwio,
                                   dwoo.reshape(N_CHIPS, D_MODEL // N_CHIPS, D_MODEL)])
    dqa, dka, dva, p_woe, p_wio, p_woo = _pp_bwd(qa, ka, ka_t, va, oa, doa, lse_a, kdiv=KV_SHARE, tq=a_tq, tk=a_tk, sub=a_sub,
                                                 name="attn_a_bwd", side=late_grads)
    dqcat, dkcat = _mla_bwd(qcat, kcat, kcat_t, olat, dolat, lse_b, tq=b_tq, tk=bb_tk, sub=bb_sub)
    dx0, dwie, dwuq, dwuk, st_0, nst = _even_pre_bwd(x0, h0, proj_e, dqa, dka, dva, dga, dgb, dqcat, dkcat, dx1, mod3[0], nw0,
                                                     wie, qn, kn, seg, ca, sa, ct, st, qln, kvln, wuq, wuk)
    dsink_pairs = jnp.stack([dsink_raw[:, 0, 0], dsink_raw[:, 1, 0]], axis=1).reshape(C_HEADS)
    return dict(
        loss_row=st_f[2:3], dx=dx0,
        dmod=jnp.stack([jnp.concatenate([st_0[0], st_0[1], st_e[0]]), jnp.concatenate([st_1[0], st_1[1], st_f[1]])]),
        norm_w=jnp.stack([st_0[2], st_1[2]]), final_norm=st_f[0],
        a_q_norm=nst[0:1, 0:HEAD_DIM], a_k_norm=nst[1:2, 0:HEAD_DIM], b_q_lora_norm=nst[2:3, :], b_kv_lora_norm=nst[3:4, 0:LANES],
        c_sink=dsink_pairs.reshape(1, C_HEADS),
        even_w_in=dwie, b_w_uq=dwuq, b_w_uk=dwuk, b_w_uv=dwuv, even_w_out=p_woe, odd_w_in=p_wio, odd_w_out=p_woo)


WEIGHT_NAMES = ("norm_w", "ada_w", "ada_b", "even_w_in", "a_q_norm", "a_k_norm", "b_q_lora_norm", "b_kv_lora_norm", "b_w_uq",
                "b_w_uk", "b_w_uv", "even_w_out", "odd_w_in", "c_sink", "odd_w_out", "final_norm")


def _cols_to_chips(g):
    r, n4 = g.shape
    return jnp.transpose(g.reshape(r, N_CHIPS, n4 // N_CHIPS), (1, 0, 2))


def _chips_to_cols(g):
    p, r, n = g.shape
    return jnp.transpose(g, (1, 0, 2)).reshape(r, p * n)


def kernel(x, c, norm_w, ada_w, ada_b, even_w_in, a_q_norm, a_k_norm, b_q_lora_norm, b_kv_lora_norm, b_w_uq, b_w_uk, b_w_uv, even_w_out, odd_w_in, c_sink, odd_w_out, final_norm, loss_target, m_norm_w, m_ada_w, m_ada_b, m_even_w_in, m_a_q_norm, m_a_k_norm, m_b_q_lora_norm, m_b_kv_lora_norm, m_b_w_uq, m_b_w_uk, m_b_w_uv, m_even_w_out, m_odd_w_in, m_c_sink, m_odd_w_out, m_final_norm, v_norm_w, v_ada_w, v_ada_b, v_even_w_in, v_a_q_norm, v_a_k_norm, v_b_q_lora_norm, v_b_kv_lora_norm, v_b_w_uq, v_b_w_uk, v_b_w_uv, v_even_w_out, v_odd_w_in, v_c_sink, v_odd_w_out, v_final_norm):
    given = dict(locals())
    xi, yi, ci = _my_place()
    chip = 2 * xi + yi
    dev = 2 * chip + ci
    n_ada = ada_w.shape[2]

    (c_all,) = _gather_dev8([c], "gather_c")
    c_all = c_all.reshape(N_DEV, D_MODEL)
    bias = lax.dynamic_slice_in_dim(ada_b, chip * n_ada, n_ada, axis=1).reshape(2, 1, n_ada)
    mod_cols = _ada_fwd(c_all, ada_w, bias)
    def halves(w):
        return w.astype(BF16).reshape((2, w.shape[0] // 2) + w.shape[1:])

    mod_all, wie_g, wuq_g = _gather_chip4_halves([mod_cols, halves(even_w_in[0]), halves(b_w_uq[0])]).run("gather_weights")
    wie_g = wie_g.reshape(N_CHIPS, D_MODEL, EVEN_IN // N_CHIPS)
    wuq_g = wuq_g.reshape(N_CHIPS, B_Q_LORA, -1)
    mod = jnp.transpose(lax.dynamic_index_in_dim(mod_all, dev, axis=2, keepdims=False), (1, 0, 2)).reshape(2, 3 * D_MODEL)

    res = _local_step(
        x[0], loss_target[0], mod, norm_w,
        _even_in_layout(_chips_to_cols(wie_g)), _uq_layout(_chips_to_cols(wuq_g)), _uk_layout(b_w_uk[0].astype(BF16)),
        _uv_layout(b_w_uv[0].astype(BF16)), [halves(even_w_out[0]), halves(odd_w_in[0]), halves(odd_w_out[0])],
        a_q_norm, a_k_norm, b_q_lora_norm, b_kv_lora_norm, c_sink, final_norm)

    latent = jnp.stack([_uk_unlayout(res["b_w_uk"]).reshape(B_KV_LORA, 512),
                        _uv_unlayout(res["b_w_uv"]).reshape(B_KV_LORA, 512)]).astype(BF16)
    (p_wie, p_wuq), (small_all, latent_all) = _reduce_exchange(
        [_cols_to_chips(_even_in_unlayout(res["even_w_in"].astype(BF16))), _cols_to_chips(_uq_unlayout(res["b_w_uq"].astype(BF16)))]
    ).run_with_gather([_pack_small(res), latent], "reduce_exchange")
    shard_parts = dict(even_w_in=p_wie, b_w_uq=p_wuq, **{k: res[k] for k in ("even_w_out", "odd_w_in", "odd_w_out")})
    dmod_all = small_all[:, 0:6, :].reshape(N_DEV, 2, 3 * D_MODEL)
    dmod_cols = jnp.transpose(lax.dynamic_slice_in_dim(dmod_all, chip * n_ada, n_ada, axis=2), (1, 0, 2))
    parts = dict(shard_parts)
    parts["ada_w"] = _ada_bwd(c_all.T, dmod_cols).reshape(1, 2 * D_MODEL, n_ada)
    parts["b_w_uk"], parts["b_w_uv"] = latent_all[:, 0], latent_all[:, 1]

    def as2d(a):
        return a.reshape((-1, a.shape[-1]) if a.ndim > 1 else (1, a.shape[0]))

    results = {}
    small_outs = _adam_small(small_all, *[[as2d(given[pre + k]) for k in SMALL_WEIGHTS] for pre in ("", "m_", "v_")])
    for idx, k in enumerate(SMALL_WEIGHTS):
        results[k] = small_outs[4 * idx:4 * idx + 4]
    for k, p in parts.items():
        shape2 = (p.shape[-2], p.shape[-1])
        results[k] = _adam(p, given[k].reshape(shape2), given["m_" + k].reshape(shape2), given["v_" + k].reshape(shape2),
                           "adam_" + k)
    by_kind = [[results[k][t].reshape(given[k].shape) for k in WEIGHT_NAMES] for t in range(4)]
    return (small_outs[-1][0, 0], res["dx"][None], *by_kind[0], *by_kind[1], *by_kind[2], *by_kind[3])
```

```python
import numpy as np
import jax
import jax.numpy as jnp
from jax import lax
from jax.experimental import pallas as pl
from jax.experimental.pallas import tpu as pltpu

F32 = jnp.float32
BF16 = jnp.bfloat16
HIGHEST = lax.Precision.HIGHEST
MESH_ID = pl.DeviceIdType.MESH

D_MODEL = 1024
HEAD_DIM = 64
GRID_W = 64
EPS = 1e-6
ROPE_THETA = 10000.0
B_HEADS, B_NOPE, B_ROPE, B_V = 8, 64, 32, 64
B_Q_LORA, B_KV_LORA = 256, 128
C_HEADS = 16
WINDOW = 128
EVEN_IN, ODD_IN = 2208, 2560
EVEN_P = 2304
N_CHIPS, N_DEV = 4, 8
LANES = 128
NEG = -1e30
VMEM_LIMIT = 60 * 1024 * 1024

ADAM_LR, ADAM_B1, ADAM_B2, ADAM_EPS, ADAM_WD, ADAM_STEP = 0.001, 0.9, 0.999, 1e-08, 0.01, 10

ROW_TILE = 512
EVEN_PRE_BWD_ROW_TILE = 256


def _dot(a, b):
    return lax.dot_general(a, b, (((1,), (0,)), ((), ())), preferred_element_type=F32)


def _dot_nt(a, b):
    return lax.dot_general(a, b, (((1,), (1,)), ((), ())), preferred_element_type=F32)


def _dot_tn(a, b):
    return lax.dot_general(a, b, (((0,), (0,)), ((), ())), preferred_element_type=F32)


def _dot_f32(a, b):
    return lax.dot_general(a, b, (((1,), (0,)), ((), ())), precision=HIGHEST, preferred_element_type=F32)


def _sigmoid(x):
    return 1.0 / (1.0 + jnp.exp(-x))


def _silu_and_grad(g):
    s = _sigmoid(g)
    return g * s, s * (1.0 + g * (1.0 - s))


def _lane_iota():
    return lax.broadcasted_iota(jnp.int32, (1, LANES), 1)


def _partner(x, lane):
    return jnp.where((lane % 32) < 16, pltpu.roll(x, LANES - 16, 1), pltpu.roll(x, 16, 1))


def _rot(x, cos, sin_signed, lane):
    return x * cos + _partner(x, lane) * sin_signed


def _rot_bwd(dy, cos, sin_signed, lane):
    return dy * cos + _partner(dy * sin_signed, lane)


def _rms(x):
    return lax.rsqrt(jnp.mean(x * x, axis=-1, keepdims=True) + EPS)


def _rms_bwd(x, r, g):
    return r * g - x * (r * r * r) * jnp.mean(x * g, axis=-1, keepdims=True)


def _seg_mean(v, seg_ones):
    hi = v.astype(BF16)
    lo = (v - hi.astype(F32)).astype(BF16)
    return (_dot(hi, seg_ones) + _dot(lo, seg_ones)) * (1.0 / HEAD_DIM)


def _dup_heads(x, lane):
    swapped = pltpu.roll(x, HEAD_DIM, 1)
    lo = lane < HEAD_DIM
    return jnp.concatenate([jnp.where(lo, x, swapped), jnp.where(lo, swapped, x)], axis=1)


def _fold_heads(x2, lane):
    a, b = x2[:, 0:LANES], x2[:, LANES:2 * LANES]
    return jnp.where(lane < HEAD_DIM, a + pltpu.roll(a, HEAD_DIM, 1), b + pltpu.roll(b, HEAD_DIM, 1))


def _row_spec(ts, cols):
    return pl.BlockSpec((ts, cols), lambda i: (i, 0))


def _full_spec(shape, single=True):
    nd = len(shape)
    if single:
        return pl.BlockSpec(shape, lambda i: (0,) * nd, pipeline_mode=pl.Buffered(1))
    return pl.BlockSpec(shape, lambda i: (0,) * nd)


def _sds(shape, dtype):
    return jax.ShapeDtypeStruct(shape, dtype)


def _params(sem):
    return pltpu.CompilerParams(dimension_semantics=sem, vmem_limit_bytes=VMEM_LIMIT)


def _even_pre_fwd(x, mod, nw, wie, qn, kn, seg, ca, sa, ct, st, qln, kvln, wuq, wuk):
    S = x.shape[0]
    ts = min(ROW_TILE, S)

    def body(x_ref, mod_ref, nw_ref, wie_ref, qn_ref, kn_ref, seg_ref, ca_ref, sa_ref, ct_ref, st_ref, qln_ref,
             kvln_ref, wuq_ref, wuk_ref, h_ref, proj_ref, qa_ref, ka_ref, va_ref, qcat_ref, kcat_ref, kat_ref, vat_ref, kcatt_ref):
        xv = x_ref[...]
        h = (xv * _rms(xv) * nw_ref[...]) * (1.0 + mod_ref[1:2, :]) + mod_ref[0:1, :]
        hb = h.astype(BF16)
        h_ref[...] = hb
        proj = _dot(hb, wie_ref[...])
        proj_ref[...] = proj
        lane = _lane_iota()
        ca_v, sa_v, ct_v, st_v = ca_ref[...], sa_ref[...], ct_ref[...], st_ref[...]
        seg_v = seg_ref[...]
        for cb in range(4):
            xc = proj[:, LANES * cb:LANES * (cb + 1)]
            r = lax.rsqrt(_seg_mean(xc * xc, seg_v) + EPS)
            y = _rot(xc * r * qn_ref[...], ca_v, sa_v, lane)
            qa_ref[:, LANES * cb:LANES * (cb + 1)] = (y * 0.125).astype(BF16)
        kc = proj[:, 512:640]
        r = lax.rsqrt(_seg_mean(kc * kc, seg_v) + EPS)
        ka_v = _dup_heads(_rot(kc * r * kn_ref[...], ca_v, sa_v, lane), lane)
        ka_ref[...] = ka_v.astype(BF16)
        kat_ref[...] = ka_v.T.astype(BF16)
        va_v = _dup_heads(proj[:, 640:768], lane)
        va_ref[...] = va_v.astype(BF16)
        vat_ref[...] = va_v.T.astype(BF16)
        cq = proj[:, 1280:1536]
        cqn = (cq * _rms(cq) * qln_ref[...]).astype(BF16)
        ckv = proj[:, 1536:1664]
        ckvn = ckv * _rms(ckv) * kvln_ref[...]
        qb = _dot(cqn, wuq_ref[...])
        qlat = _dot(qb[:, 0:512].astype(BF16), wuk_ref[...])
        for hh in range(B_HEADS):
            qcat_ref[hh, :, 0:LANES] = qlat[:, LANES * hh:LANES * (hh + 1)].astype(BF16)
            qr = _rot(qb[:, 512 + LANES * hh:512 + LANES * (hh + 1)], ct_v, st_v, lane)
            qcat_ref[hh, :, LANES:2 * LANES] = qr.astype(BF16)
        kr = _rot(proj[:, 1664:1792], ct_v, st_v, lane)
        kcat_ref[:, 0:LANES] = ckvn.astype(BF16)
        kcat_ref[:, LANES:2 * LANES] = kr.astype(BF16)
        kcatt_ref[0:LANES, :] = ckvn.T.astype(BF16)
        kcatt_ref[LANES:2 * LANES, :] = kr.T.astype(BF16)

    col_spec = lambda rows: pl.BlockSpec((rows, ts), lambda i: (0, i))
    return pl.pallas_call(
        body, name="even_pre_fwd", grid=(S // ts,),
        in_specs=[_row_spec(ts, D_MODEL), _full_spec((3, D_MODEL)), _full_spec((1, D_MODEL)), _full_spec((D_MODEL, EVEN_P)),
                  _full_spec((1, LANES)), _full_spec((1, LANES)), _full_spec((LANES, LANES)),
                  _row_spec(ts, LANES), _row_spec(ts, LANES), _row_spec(ts, LANES), _row_spec(ts, LANES),
                  _full_spec((1, B_Q_LORA)), _full_spec((1, B_KV_LORA)), _full_spec((B_Q_LORA, 1536)), _full_spec((512, 1024))],
        out_specs=[_row_spec(ts, D_MODEL), _row_spec(ts, EVEN_P), _row_spec(ts, 512), _row_spec(ts, 2 * LANES), _row_spec(ts, 2 * LANES),
                   pl.BlockSpec((B_HEADS, ts, 2 * LANES), lambda i: (0, i, 0)), _row_spec(ts, 2 * LANES),
                   col_spec(2 * LANES), col_spec(2 * LANES), col_spec(2 * LANES)],
        out_shape=[_sds((S, D_MODEL), BF16), _sds((S, EVEN_P), F32), _sds((S, 512), BF16), _sds((S, 2 * LANES), BF16),
                   _sds((S, 2 * LANES), BF16), _sds((B_HEADS, S, 2 * LANES), BF16), _sds((S, 2 * LANES), BF16),
                   _sds((2 * LANES, S), BF16), _sds((2 * LANES, S), BF16), _sds((2 * LANES, S), BF16)],
        compiler_params=_params(("arbitrary",)),
    )(x, mod, nw, wie, qn, kn, seg, ca, sa, ct, st, qln, kvln, wuq, wuk)


MLA_SCALE = (B_NOPE + B_ROPE) ** -0.5
LOG2E = 1.4426950408889634

def _row_lo():
    return lax.broadcasted_iota(jnp.int32, (LANES, 1), 0) < HEAD_DIM


def _stack_cols(vT, rlo):
    zero = jnp.zeros_like(vT)
    return jnp.concatenate([jnp.where(rlo, vT, zero), jnp.where(rlo, zero, vT)], axis=1)


def _stack_rows(v, lo):
    zero = jnp.zeros_like(v)
    return jnp.concatenate([jnp.where(lo, v, zero), jnp.where(lo, zero, v)], axis=0)


def _pick_halves_T(xT, rlo, t):
    return jnp.where(rlo, xT[:, 0:t], xT[:, t:2 * t]).T


def _side_split(refs, n_in, n_out, n_scratch, side):
    ns = side.n if side is not None else 0
    cuts = np.cumsum([0, n_in, ns, n_out, ns, n_scratch])
    return [refs[a:b] for a, b in zip(cuts[:-1], cuts[1:])] + [refs[cuts[-1]:]]


def _side_hooks(side, side_ins, side_outs, side_sems, step, total):
    if side is None:
        return lambda: None
    start, mid, end = side.phases(side_ins, side_outs, side_sems)
    pl.when(step == 0)(start)
    pl.when(step == total // 2)(mid)
    return lambda: pl.when(step == total - 1)(end)


def _side_specs(side):
    if side is None:
        return [], [], [], [], []
    return list(side.arrs), [_ANY] * side.n, [_ANY] * side.n, list(side.out_shapes), side.sem_shapes()


def _pp_fwd(q, k, vT, *, kdiv, tq, tk, sub, name, side=None):
    S = k.shape[0]; nb = q.shape[1] // LANES; nq = S // tq; nkv = S // tk; nsub = tk // sub

    def body(*refs):
        (q_ref, k_ref, vT_ref), side_ins, (o_ref, lse_ref), side_outs, (qs, m_s, l_s, acc), side_sems = _side_split(refs, 3, 2, 4, side)
        j = pl.program_id(2)
        rlo = _row_lo()
        step = (pl.program_id(0) * nq + pl.program_id(1)) * nkv + j
        side_end = _side_hooks(side, side_ins, side_outs, side_sems, step, nb * nq * nkv)

        @pl.when(j == 0)
        def _():
            qs[...] = _stack_cols(q_ref[...].astype(F32).T, rlo).astype(BF16)
            m_s[...] = jnp.full((1, 2 * tq), NEG, F32)
            l_s[...] = jnp.zeros((1, 2 * tq), F32)
            acc[...] = jnp.zeros((LANES, 2 * tq), F32)

        qsv = qs[...]
        m, l, a = m_s[...], l_s[...], acc[...]
        s_cur = _dot(k_ref[0:sub, :], qsv)
        for t in range(nsub):
            if t + 1 < nsub:
                s_next = _dot(k_ref[sub * (t + 1):sub * (t + 2), :], qsv)
            m_new = jnp.maximum(m, jnp.max(s_cur, axis=0, keepdims=True))
            alpha = jnp.exp(m - m_new)
            p = jnp.exp(s_cur - m_new)
            l = alpha * l + jnp.sum(p, axis=0, keepdims=True)
            a = alpha * a + _dot(vT_ref[:, sub * t:sub * (t + 1)], p.astype(BF16))
            m = m_new
            if t + 1 < nsub:
                s_cur = s_next
        m_s[...], l_s[...], acc[...] = m, l, a

        @pl.when(j == nkv - 1)
        def _():
            l_f = l_s[...]
            o_ref[...] = _pick_halves_T(acc[...] / l_f, rlo, tq)
            lse_ref[0, 0] = m_s[...] + jnp.log(l_f)

        side_end()

    s_args, s_in, s_out, s_shapes, s_sems = _side_specs(side)
    return pl.pallas_call(
        body, name=name, grid=(nb, nq, nkv),
        in_specs=[pl.BlockSpec((tq, LANES), lambda b, i, j: (i, b)), pl.BlockSpec((tk, LANES), lambda b, i, j: (j, b // kdiv)),
                  pl.BlockSpec((LANES, tk), lambda b, i, j: (b // kdiv, j))] + s_in,
        out_specs=[pl.BlockSpec((tq, LANES), lambda b, i, j: (i, b)),
                   pl.BlockSpec((1, 1, 1, 2 * tq), lambda b, i, j: (b, i, 0, 0))] + s_out,
        out_shape=[_sds((S, nb * LANES), F32), _sds((nb, nq, 1, 2 * tq), F32)] + s_shapes,
        scratch_shapes=[pltpu.VMEM((LANES, 2 * tq), BF16), pltpu.VMEM((1, 2 * tq), F32), pltpu.VMEM((1, 2 * tq), F32),
                        pltpu.VMEM((LANES, 2 * tq), F32)] + s_sems,
        compiler_params=_params(("arbitrary",) * 3))(q, k, vT, *s_args)


def _pp_bwd(q, k, kT, v, o, do, lse, *, kdiv, tq, tk, sub, name, side=None):
    S = k.shape[0]; nb = q.shape[1] // LANES; nkb = k.shape[1] // LANES; nq = S // tq; nkv = S // tk; nsub = tk // sub

    def body(*refs):
        ((q_ref, k_ref, kT_ref, v_ref, o_ref, do_ref, lse_ref), side_ins, (dq_ref, dk_ref, dv_ref), side_outs,
         (qsT, qs, dosT, dos, delta_s, dq_acc), side_sems) = _side_split(refs, 7, 3, 6, side)
        b, i, j = pl.program_id(0), pl.program_id(1), pl.program_id(2)
        rlo = _row_lo()
        lo = lax.broadcasted_iota(jnp.int32, (1, LANES), 1) < HEAD_DIM
        side_end = _side_hooks(side, side_ins, side_outs, side_sems, (b * nq + i) * nkv + j, nb * nq * nkv)

        @pl.when((b % kdiv == 0) & (i == 0) & (j == 0))
        def _():
            dk_ref[...] = jnp.zeros((S, LANES), F32)
            dv_ref[...] = jnp.zeros((S, LANES), F32)

        @pl.when(j == 0)
        def _():
            qv = q_ref[...]
            qs[...] = _stack_rows(qv, lo)
            qsT[...] = _stack_cols(qv.astype(F32).T, rlo).astype(BF16)
            dov = do_ref[...]
            dos[...] = _stack_rows(dov.astype(BF16), lo)
            dosT[...] = _stack_cols(dov.T, rlo).astype(BF16)
            prodT = (dov * o_ref[...]).T
            delta_s[...] = jnp.concatenate([jnp.sum(jnp.where(rlo, prodT, 0.0), axis=0, keepdims=True),
                                            jnp.sum(jnp.where(rlo, 0.0, prodT), axis=0, keepdims=True)], axis=1)
            dq_acc[...] = jnp.zeros((LANES, 2 * tq), F32)

        qsTv, dosTv, qsv, dosv = qsT[...], dosT[...], qs[...], dos[...]
        lse_v, delta_v = lse_ref[0, 0], delta_s[...]
        dqa = dq_acc[...]
        s_cur = _dot(k_ref[0:sub, :], qsTv)
        dp_cur = _dot(v_ref[0:sub, :], dosTv)
        for t in range(nsub):
            if t + 1 < nsub:
                s_next = _dot(k_ref[sub * (t + 1):sub * (t + 2), :], qsTv)
                dp_next = _dot(v_ref[sub * (t + 1):sub * (t + 2), :], dosTv)
            p = jnp.exp(s_cur - lse_v)
            ds = (p * (dp_cur - delta_v)).astype(BF16)
            rows = pl.ds(pl.multiple_of(j * tk + sub * t, sub), sub)
            dv_ref[rows, :] += _dot(p.astype(BF16), dosv)
            dk_ref[rows, :] += _dot(ds, qsv)
            dqa = dqa + _dot(kT_ref[:, sub * t:sub * (t + 1)], ds)
            if t + 1 < nsub:
                s_cur, dp_cur = s_next, dp_next
        dq_acc[...] = dqa

        @pl.when(j == nkv - 1)
        def _():
            dq_ref[...] = _pick_halves_T(dq_acc[...], rlo, tq)

        side_end()

    qmap = lambda b, i, j: (i, b)
    kmap = lambda b, i, j: (j, b // kdiv)
    res = lambda b, i, j: (0, b // kdiv)
    s_args, s_in, s_out, s_shapes, s_sems = _side_specs(side)
    return pl.pallas_call(
        body, name=name, grid=(nb, nq, nkv),
        in_specs=[pl.BlockSpec((tq, LANES), qmap), pl.BlockSpec((tk, LANES), kmap), pl.BlockSpec((LANES, tk), lambda b, i, j: (b // kdiv, j)),
                  pl.BlockSpec((tk, LANES), kmap), pl.BlockSpec((tq, LANES), qmap), pl.BlockSpec((tq, LANES), qmap),
                  pl.BlockSpec((1, 1, 1, 2 * tq), lambda b, i, j: (b, i, 0, 0))] + s_in,
        out_specs=[pl.BlockSpec((tq, LANES), qmap), pl.BlockSpec((S, LANES), res), pl.BlockSpec((S, LANES), res)] + s_out,
        out_shape=[_sds((S, nb * LANES), F32), _sds((S, nkb * LANES), F32), _sds((S, nkb * LANES), F32)] + s_shapes,
        scratch_shapes=[pltpu.VMEM((LANES, 2 * tq), BF16), pltpu.VMEM((2 * tq, LANES), BF16), pltpu.VMEM((LANES, 2 * tq), BF16),
                        pltpu.VMEM((2 * tq, LANES), BF16), pltpu.VMEM((1, 2 * tq), F32), pltpu.VMEM((LANES, 2 * tq), F32)] + s_sems,
        compiler_params=_params(("arbitrary",) * 3))(q, k, kT, v, o, do, lse, *s_args)


MLA_C = MLA_SCALE * LOG2E


def _mla_fwd(q, kcat, kcatT, *, tq, tk, sub):
    S = kcat.shape[0]; nq, nkv = S // tq, S // tk; R = B_HEADS * tq; nsub = tk // sub

    def body(q_ref, k_ref, vT_ref, o_ref, lse_ref, qT, m_s, l_s, acc):
        j = pl.program_id(1)

        @pl.when(j == 0)
        def _():
            qT[...] = q_ref[...].reshape(R, 2 * LANES).astype(F32).T.astype(BF16)
            m_s[...] = jnp.full((1, R), NEG, F32)
            l_s[...] = jnp.zeros((1, R), F32)
            acc[...] = jnp.zeros((LANES, R), F32)

        qTv = qT[...]
        m, l, a = m_s[...], l_s[...], acc[...]
        s_cur = _dot(k_ref[0:sub, :], qTv)
        for t in range(nsub):
            if t + 1 < nsub:
                s_next = _dot(k_ref[sub * (t + 1):sub * (t + 2), :], qTv)
            m_new = jnp.maximum(m, jnp.max(s_cur, axis=0, keepdims=True))
            alpha = jnp.exp2((m - m_new) * MLA_C)
            p = jnp.exp2((s_cur - m_new) * MLA_C)
            l = alpha * l + jnp.sum(p, axis=0, keepdims=True)
            a = alpha * a + _dot(vT_ref[:, sub * t:sub * (t + 1)], p.astype(BF16))
            m = m_new
            if t + 1 < nsub:
                s_cur = s_next
        m_s[...], l_s[...], acc[...] = m, l, a

        @pl.when(j == nkv - 1)
        def _():
            l_f = l_s[...]
            o_ref[...] = (acc[...] / l_f).T.reshape(B_HEADS, tq, LANES)
            lse_ref[0] = m_s[...] * MLA_SCALE + jnp.log(l_f)

    return pl.pallas_call(
        body, name="mla_fwd", grid=(nq, nkv),
        in_specs=[pl.BlockSpec((B_HEADS, tq, 2 * LANES), lambda i, j: (0, i, 0)), pl.BlockSpec((tk, 2 * LANES), lambda i, j: (j, 0)),
                  pl.BlockSpec((LANES, tk), lambda i, j: (0, j))],
        out_specs=[pl.BlockSpec((B_HEADS, tq, LANES), lambda i, j: (0, i, 0)), pl.BlockSpec((1, 1, R), lambda i, j: (i, 0, 0))],
        out_shape=[_sds((B_HEADS, S, LANES), F32), _sds((nq, 1, R), F32)],
        scratch_shapes=[pltpu.VMEM((2 * LANES, R), BF16), pltpu.VMEM((1, R), F32), pltpu.VMEM((1, R), F32), pltpu.VMEM((LANES, R), F32)],
        compiler_params=_params(("arbitrary", "arbitrary")))(q, kcat, kcatT)


def _mla_bwd(q, kcat, kcatT, o, do, lse, *, tq, tk, sub):
    S = kcat.shape[0]; nq, nkv = S // tq, S // tk; R = B_HEADS * tq; nsub = tk // sub

    def body(q_ref, k_ref, kT_ref, o_ref, do_ref, lse_ref, dq_ref, dk_ref, qT, dosT, dos, delta_s, dq_acc):
        i, j = pl.program_id(0), pl.program_id(1)

        @pl.when((i == 0) & (j == 0))
        def _():
            dk_ref[...] = jnp.zeros((S, 2 * LANES), F32)

        @pl.when(j == 0)
        def _():
            qT[...] = q_ref[...].reshape(R, 2 * LANES).astype(F32).T.astype(BF16)
            dov = do_ref[...].reshape(R, LANES)
            dos[...] = dov.astype(BF16)
            dosT[...] = dov.T.astype(BF16)
            delta_s[...] = jnp.sum((dov * o_ref[...].reshape(R, LANES)).T, axis=0, keepdims=True)
            dq_acc[...] = jnp.zeros((2 * LANES, R), F32)

        qTv, dosTv, dosv = qT[...], dosT[...], dos[...]
        qv = q_ref[...].reshape(R, 2 * LANES)
        lse_v, delta_v = lse_ref[0] * LOG2E, delta_s[...]
        dqa = dq_acc[...]
        s_cur = _dot(k_ref[0:sub, :], qTv)
        dp_cur = _dot(k_ref[0:sub, 0:LANES], dosTv)
        for t in range(nsub):
            if t + 1 < nsub:
                s_next = _dot(k_ref[sub * (t + 1):sub * (t + 2), :], qTv)
                dp_next = _dot(k_ref[sub * (t + 1):sub * (t + 2), 0:LANES], dosTv)
            p = jnp.exp2(s_cur * MLA_C - lse_v)
            ds = (p * (dp_cur - delta_v) * MLA_SCALE).astype(BF16)
            rows = pl.ds(pl.multiple_of(j * tk + sub * t, sub), sub)
            dk_ref[rows, :] += _dot(ds, qv)
            dk_ref[rows, 0:LANES] += _dot(p.astype(BF16), dosv)
            dqa = dqa + _dot(kT_ref[:, sub * t:sub * (t + 1)], ds)
            if t + 1 < nsub:
                s_cur, dp_cur = s_next, dp_next
        dq_acc[...] = dqa

        @pl.when(j == nkv - 1)
        def _():
            dq_ref[...] = dq_acc[...].T.reshape(B_HEADS, tq, 2 * LANES)

    hspec = lambda w: pl.BlockSpec((B_HEADS, tq, w), lambda i, j: (0, i, 0))
    return pl.pallas_call(
        body, name="mla_bwd", grid=(nq, nkv),
        in_specs=[hspec(2 * LANES), pl.BlockSpec((tk, 2 * LANES), lambda i, j: (j, 0)), pl.BlockSpec((2 * LANES, tk), lambda i, j: (0, j)),
                  hspec(LANES), hspec(LANES), pl.BlockSpec((1, 1, R), lambda i, j: (i, 0, 0))],
        out_specs=[hspec(2 * LANES), pl.BlockSpec((S, 2 * LANES), lambda i, j: (0, 0))],
        out_shape=[_sds((B_HEADS, S, 2 * LANES), F32), _sds((S, 2 * LANES), F32)],
        scratch_shapes=[pltpu.VMEM((2 * LANES, R), BF16), pltpu.VMEM((LANES, R), BF16), pltpu.VMEM((R, LANES), BF16),
                        pltpu.VMEM((1, R), F32), pltpu.VMEM((2 * LANES, R), F32)],
        compiler_params=_params(("arbitrary", "arbitrary")))(q, kcat, kcatT, o, do, lse)


def _win_start(i, tq, nk, S):
    return pl.multiple_of(jnp.clip(i * tq - WINDOW, 0, S - nk), LANES)


def _win_dist_table(S, tq):
    nk = min(tq + 2 * WINDOW, S)
    nq = S // tq
    r = np.arange(nk)[:, None]
    c = (np.arange(2 * tq) % tq)[None, :]
    tabs = []
    for rel in (0, WINDOW, (nq - 1) * tq - (S - nk)):
        dist = np.abs(rel + c - r).astype(np.float32)
        tabs.append(np.where(dist <= WINDOW, dist, np.float32(1e32)))
    return jnp.asarray(np.stack(tabs))


def _win_dist_spec(nk, tq, nq):
    return pl.BlockSpec((1, nk, 2 * tq), lambda b, i: (jnp.where(i == 0, 0, jnp.where(i == nq - 1, 2, 1)), 0, 0))


def _win_fwd(q, k, vT, dist, slope, sink, *, kdiv, tq, nbs, name):
    S = k.shape[0]; nb = q.shape[1] // LANES; nq = S // tq; nk = min(tq + 2 * WINDOW, S)
    assert nb % nbs == 0 and nbs % kdiv == 0
    kvw = (nbs // kdiv) * LANES

    def body(q_ref, k_ref, vT_ref, dist_ref, slope_ref, sink_ref, o_ref, lse_ref):
        i = pl.program_id(1)
        rlo = _row_lo()
        k0 = _win_start(i, tq, nk, S)
        kk, vv, dd = k_ref[pl.ds(k0, nk), :], vT_ref[:, pl.ds(k0, nk)], dist_ref[0]
        for u in range(nbs):
            kv = slice(LANES * (u // kdiv), LANES * (u // kdiv + 1))
            qsT = _stack_cols(q_ref[:, LANES * u:LANES * (u + 1)].astype(F32).T, rlo).astype(BF16)
            s = _dot(kk[:, kv], qsT) - slope_ref[u] * dd
            sk = sink_ref[u]
            m = jnp.maximum(jnp.max(s, axis=0, keepdims=True), sk)
            p = jnp.exp(s - m)
            l = jnp.sum(p, axis=0, keepdims=True) + jnp.exp(sk - m)
            o_ref[:, LANES * u:LANES * (u + 1)] = _pick_halves_T(_dot(vv[kv, :], p.astype(BF16)) / l, rlo, tq)
            lse_ref[u, 0] = m + jnp.log(l)

    row_spec = pl.BlockSpec((nbs, 1, 2 * tq), lambda b, i: (b, 0, 0))
    return pl.pallas_call(
        body, name=name, grid=(nb // nbs, nq),
        in_specs=[pl.BlockSpec((tq, nbs * LANES), lambda b, i: (i, b)), pl.BlockSpec((S, kvw), lambda b, i: (0, b)),
                  pl.BlockSpec((kvw, S), lambda b, i: (b, 0)), _win_dist_spec(nk, tq, nq), row_spec, row_spec],
        out_specs=[pl.BlockSpec((tq, nbs * LANES), lambda b, i: (i, b)), pl.BlockSpec((nbs, 1, 1, 2 * tq), lambda b, i: (b, i, 0, 0))],
        out_shape=[_sds((S, nb * LANES), F32), _sds((nb, nq, 1, 2 * tq), F32)],
        compiler_params=_params(("arbitrary", "arbitrary")))(q, k, vT, dist, slope, sink)


def _win_bwd(q, k, kT, v, o, do, lse, dist, slope, sink, *, kdiv, tq, nbs, name):
    S = k.shape[0]; nb = q.shape[1] // LANES; nkb = k.shape[1] // LANES; nq = S // tq; nk = min(tq + 2 * WINDOW, S)
    assert nb % nbs == 0 and nbs % kdiv == 0
    nkv = nbs // kdiv
    kvw = nkv * LANES

    def body(q_ref, k_ref, kT_ref, v_ref, o_ref, do_ref, lse_ref, dist_ref, slope_ref, sink_ref, dq_ref, dk_ref, dv_ref, dsink_ref, ds_acc):
        i = pl.program_id(1)
        rlo = _row_lo()
        lo = lax.broadcasted_iota(jnp.int32, (1, LANES), 1) < HEAD_DIM

        @pl.when(i == 0)
        def _():
            dk_ref[...] = jnp.zeros((S, kvw), F32)
            dv_ref[...] = jnp.zeros((S, kvw), F32)
            ds_acc[...] = jnp.zeros((nbs, 2 * tq), F32)

        k0 = _win_start(i, tq, nk, S)
        rows = pl.ds(k0, nk)
        kk_all, vv_all, kkT_all, dd = k_ref[rows, :], v_ref[rows, :], kT_ref[:, rows], dist_ref[0]
        dv_sum, dk_sum = [None] * nkv, [None] * nkv
        for u in range(nbs):
            g = u // kdiv
            kv = slice(LANES * g, LANES * (g + 1))
            kk, vv, kkT = kk_all[:, kv], vv_all[:, kv], kkT_all[kv, :]
            cols = slice(LANES * u, LANES * (u + 1))
            qv = q_ref[:, cols]
            qs = _stack_rows(qv, lo)
            qsT = _stack_cols(qv.astype(F32).T, rlo).astype(BF16)
            dov = do_ref[:, cols]
            dos = _stack_rows(dov.astype(BF16), lo)
            dosT = _stack_cols(dov.T, rlo).astype(BF16)
            prodT = (dov * o_ref[:, cols]).T
            delta = jnp.concatenate([jnp.sum(jnp.where(rlo, prodT, 0.0), axis=0, keepdims=True),
                                     jnp.sum(jnp.where(rlo, 0.0, prodT), axis=0, keepdims=True)], axis=1)
            lse_v = lse_ref[u, 0]
            ds_acc[u:u + 1, :] += -jnp.exp(sink_ref[u] - lse_v) * delta
            p = jnp.exp(_dot(kk, qsT) - slope_ref[u] * dd - lse_v)
            ds = (p * (_dot(vv, dosT) - delta)).astype(BF16)
            dv_u, dk_u = _dot(p.astype(BF16), dos), _dot(ds, qs)
            dv_sum[g] = dv_u if dv_sum[g] is None else dv_sum[g] + dv_u
            dk_sum[g] = dk_u if dk_sum[g] is None else dk_sum[g] + dk_u
            dq_ref[:, cols] = (_pick_halves_T(_dot(kkT, ds), rlo, tq) * 0.125).astype(BF16)
        dv_ref[rows, :] += jnp.concatenate(dv_sum, axis=1)
        dk_ref[rows, :] += jnp.concatenate(dk_sum, axis=1)

        @pl.when(i == nq - 1)
        def _():
            acc = ds_acc[...]
            for u in range(nbs):
                dsink_ref[u] = jnp.concatenate(
                    [jnp.broadcast_to(jnp.sum(acc[u:u + 1, 0:tq], axis=1, keepdims=True), (1, LANES)),
                     jnp.broadcast_to(jnp.sum(acc[u:u + 1, tq:2 * tq], axis=1, keepdims=True), (1, LANES)),
                     jnp.zeros((6, LANES), F32)], axis=0)

    qmap = lambda b, i: (i, b)
    kv_spec = pl.BlockSpec((S, kvw), lambda b, i: (0, b))
    row_spec = pl.BlockSpec((nbs, 1, 2 * tq), lambda b, i: (b, 0, 0))
    wide = pl.BlockSpec((tq, nbs * LANES), qmap)
    return pl.pallas_call(
        body, name=name, grid=(nb // nbs, nq),
        in_specs=[wide, kv_spec, pl.BlockSpec((kvw, S), lambda b, i: (b, 0)), kv_spec, wide, wide,
                  pl.BlockSpec((nbs, 1, 1, 2 * tq), lambda b, i: (b, i, 0, 0)), _win_dist_spec(nk, tq, nq), row_spec, row_spec],
        out_specs=[wide, kv_spec, kv_spec, pl.BlockSpec((nbs, 8, LANES), lambda b, i: (b, 0, 0))],
        out_shape=[_sds((S, nb * LANES), BF16), _sds((S, nkb * LANES), F32), _sds((S, nkb * LANES), F32), _sds((nb, 8, LANES), F32)],
        scratch_shapes=[pltpu.VMEM((nbs, 2 * tq), F32)],
        compiler_params=_params(("arbitrary", "arbitrary")))(q, k, kT, v, o, do, lse, dist, slope, sink)


def _sum_rows(v):
    return jnp.sum(v, axis=0, keepdims=True)


def _norm_mod_bwd(dh, xv, mod_ref, nw_ref, stats_ref):
    r = _rms(xv)
    xn = xv * r
    nw = nw_ref[...]
    stats_ref[0:1, :] += _sum_rows(dh)
    stats_ref[1:2, :] += _sum_rows(dh * (xn * nw))
    dn = dh * (1.0 + mod_ref[1:2, :])
    stats_ref[2:3, :] += _sum_rows(dn * xn)
    return _rms_bwd(xv, r, dn * nw)


def _even_gate_specs(ts):
    return [pl.BlockSpec((ts, 256), lambda i, c=c: (i, c)) for c in (3, 4, 7, 8)]


def _even_post_fwd(oa, olat, proj, x, gate, wuv, woe):
    S = x.shape[0]
    ts = min(ROW_TILE, S)

    def body(oa_ref, ol_ref, ga0_ref, ga1_ref, gb0_ref, gb1_ref, x_ref, gate_ref, wuv_ref, woe_ref, y_ref, x1_ref):
        sa, _ = _silu_and_grad(jnp.concatenate([ga0_ref[...], ga1_ref[...]], axis=1))
        sb, _ = _silu_and_grad(jnp.concatenate([gb0_ref[...], gb1_ref[...]], axis=1))
        olc = jnp.concatenate([ol_ref[hh] for hh in range(B_HEADS)], axis=1).astype(BF16)
        ob = _dot(olc, wuv_ref[...])
        mix = jnp.concatenate([oa_ref[...] * sa, ob * sb], axis=1).astype(BF16)
        y = _dot(mix, woe_ref[...])
        y_ref[...] = y
        x1_ref[...] = x_ref[...] + gate_ref[...] * y

    return pl.pallas_call(
        body, name="even_post_fwd", grid=(S // ts,),
        in_specs=[_row_spec(ts, 512), pl.BlockSpec((B_HEADS, ts, LANES), lambda i: (0, i, 0))] + _even_gate_specs(ts) +
                 [_row_spec(ts, D_MODEL), _full_spec((1, D_MODEL)), _full_spec((1024, 512)), _full_spec((1024, D_MODEL))],
        out_specs=[_row_spec(ts, D_MODEL), _row_spec(ts, D_MODEL)],
        out_shape=[_sds((S, D_MODEL), F32), _sds((S, D_MODEL), F32)],
        compiler_params=_params(("arbitrary",)),
    )(oa, olat, proj, proj, proj, proj, x, gate, wuv, woe)


def _odd_pre_fwd(x, mod, nw, wio):
    S = x.shape[0]
    ts = min(ROW_TILE, S)

    def body(x_ref, mod_ref, nw_ref, wio_ref, h_ref, proj_ref, q_ref, k_ref, v_ref, kt_ref, vt_ref):
        xv = x_ref[...]
        h = (xv * _rms(xv) * nw_ref[...]) * (1.0 + mod_ref[1:2, :]) + mod_ref[0:1, :]
        hb = h.astype(BF16)
        h_ref[...] = hb
        proj = jnp.concatenate([_dot(hb, wio_ref[p]) for p in range(N_CHIPS)], axis=1)
        proj_ref[...] = proj
        q_ref[...] = (proj[:, 0:1024] * 0.125).astype(BF16)
        lane = _lane_iota()
        k_v = jnp.concatenate([_dup_heads(proj[:, 1024 + LANES * j:1024 + LANES * (j + 1)], lane) for j in range(2)], axis=1)
        v_v = jnp.concatenate([_dup_heads(proj[:, 1280 + LANES * j:1280 + LANES * (j + 1)], lane) for j in range(2)], axis=1)
        k_ref[...] = k_v.astype(BF16)
        v_ref[...] = v_v.astype(BF16)
        kt_ref[...] = k_v.T.astype(BF16)
        vt_ref[...] = v_v.T.astype(BF16)

    col_spec = pl.BlockSpec((512, ts), lambda i: (0, i))
    return pl.pallas_call(
        body, name="odd_pre_fwd", grid=(S // ts,),
        in_specs=[_row_spec(ts, D_MODEL), _full_spec((3, D_MODEL)), _full_spec((1, D_MODEL)),
                  _full_spec((N_CHIPS, D_MODEL, ODD_IN // N_CHIPS))],
        out_specs=[_row_spec(ts, D_MODEL), _row_spec(ts, ODD_IN), _row_spec(ts, 1024), _row_spec(ts, 512), _row_spec(ts, 512),
                   col_spec, col_spec],
        out_shape=[_sds((S, D_MODEL), BF16), _sds((S, ODD_IN), F32), _sds((S, 1024), BF16), _sds((S, 512), BF16),
                   _sds((S, 512), BF16), _sds((512, S), BF16), _sds((512, S), BF16)],
        compiler_params=_params(("arbitrary",)),
    )(x, mod, nw, wio)


def _odd_post(oc, proj, x1, gate, woo, fw, tgt):
    S = x1.shape[0]
    ts = min(ROW_TILE, S)
    nsteps = S // ts

    def body(oc_ref, g0_ref, g1_ref, x_ref, gate_ref, woo_ref, fw_ref, tgt_ref, doc_ref, dgc_ref, dx2_ref, dwoo_out, stats_ref,
             dwoo_ref):
        @pl.when(pl.program_id(0) == 0)
        def _():
            dwoo_ref[...] = jnp.zeros((D_MODEL, D_MODEL), F32)
            stats_ref[...] = jnp.zeros((8, D_MODEL), F32)

        ocv = oc_ref[...]
        sg, dsg = _silu_and_grad(jnp.concatenate([g0_ref[...], g1_ref[...]], axis=1))
        mix = (ocv * sg).astype(BF16)
        woo_v = woo_ref[...]
        y = _dot(mix, woo_v)
        gate_v = gate_ref[...]
        x2 = x_ref[...] + gate_v * y
        r = _rms(x2)
        fw_v = fw_ref[...]
        xn = x2 * r
        err = xn * fw_v - tgt_ref[...]
        dout = err * (1.0 / D_MODEL)
        dx2 = _rms_bwd(x2, r, dout * fw_v)
        dx2_ref[...] = dx2
        stats_ref[0:1, :] += _sum_rows(dout * xn)
        stats_ref[1:2, :] += _sum_rows(dx2 * y)
        loss_t = 0.5 * jnp.sum(_sum_rows(err * dout), axis=-1, keepdims=True)
        stats_ref[2:3, :] += jnp.broadcast_to(loss_t, (1, D_MODEL))
        dy = (gate_v * dx2).astype(BF16)
        dmix = _dot_nt(dy, woo_v)
        dwoo_ref[...] += _dot_tn(mix, dy)
        doc_ref[...] = dmix * sg
        dgc_ref[...] = (dmix * ocv * dsg).astype(BF16)

        @pl.when(pl.program_id(0) == nsteps - 1)
        def _():
            dwoo_out[...] = dwoo_ref[...].astype(BF16)

    gate_cols = [pl.BlockSpec((ts, 512), lambda i, c=c: (i, c)) for c in (3, 4)]
    return pl.pallas_call(
        body, name="odd_post", grid=(nsteps,),
        in_specs=[_row_spec(ts, D_MODEL)] + gate_cols + [_row_spec(ts, D_MODEL), _full_spec((1, D_MODEL)),
                  _full_spec((D_MODEL, D_MODEL)), _full_spec((1, D_MODEL)), _row_spec(ts, D_MODEL)],
        out_specs=[_row_spec(ts, D_MODEL), _row_spec(ts, D_MODEL), _row_spec(ts, D_MODEL),
                   _full_spec((D_MODEL, D_MODEL), single=False), _full_spec((8, D_MODEL), single=False)],
        out_shape=[_sds((S, D_MODEL), F32), _sds((S, D_MODEL), BF16), _sds((S, D_MODEL), F32), _sds((D_MODEL, D_MODEL), BF16),
                   _sds((8, D_MODEL), F32)],
        scratch_shapes=[pltpu.VMEM((D_MODEL, D_MODEL), F32)],
        compiler_params=_params(("arbitrary",)),
    )(oc, proj, proj, x1, gate, woo, fw, tgt)


def _odd_pre_bwd(dq, dk, dv, dgc, h, x, dx_res, mod, nw, wio):
    S = x.shape[0]
    ts = min(ROW_TILE, S)
    nsteps = S // ts
    wsh = ODD_IN // N_CHIPS

    def body(dq_ref, dk_ref, dv_ref, dgc_ref, h_ref, x_ref, dxr_ref, mod_ref, nw_ref, wio_ref, dx_ref, dw_ref, stats_ref, dw_acc):
        @pl.when(pl.program_id(0) == 0)
        def _():
            dw_acc[...] = jnp.zeros((N_CHIPS, D_MODEL, wsh), F32)
            stats_ref[...] = jnp.zeros((8, D_MODEL), F32)

        lane = _lane_iota()
        dkv = [_fold_heads(r[:, 2 * LANES * j:2 * LANES * (j + 1)], lane).astype(BF16) for r in (dk_ref, dv_ref) for j in range(2)]
        dproj = jnp.concatenate([dq_ref[...]] + dkv + [dgc_ref[...]], axis=1)
        hv = h_ref[...]
        dh = None
        for p in range(N_CHIPS):
            dp_cols = dproj[:, wsh * p:wsh * (p + 1)]
            part = _dot_nt(dp_cols, wio_ref[p])
            dh = part if dh is None else dh + part
            dw_acc[p] += _dot_tn(hv, dp_cols)
        dx_ref[...] = dxr_ref[...] + _norm_mod_bwd(dh, x_ref[...], mod_ref, nw_ref, stats_ref)

        @pl.when(pl.program_id(0) == nsteps - 1)
        def _():
            dw_ref[...] = dw_acc[...].astype(BF16)

    return pl.pallas_call(
        body, name="odd_pre_bwd", grid=(nsteps,),
        in_specs=[_row_spec(ts, 1024), _row_spec(ts, 512), _row_spec(ts, 512), _row_spec(ts, 1024), _row_spec(ts, D_MODEL),
                  _row_spec(ts, D_MODEL), _row_spec(ts, D_MODEL), _full_spec((3, D_MODEL)), _full_spec((1, D_MODEL)),
                  _full_spec((N_CHIPS, D_MODEL, wsh))],
        out_specs=[_row_spec(ts, D_MODEL), _full_spec((N_CHIPS, D_MODEL, wsh), single=False), _full_spec((8, D_MODEL), single=False)],
        out_shape=[_sds((S, D_MODEL), F32), _sds((N_CHIPS, D_MODEL, wsh), BF16), _sds((8, D_MODEL), F32)],
        scratch_shapes=[pltpu.VMEM((N_CHIPS, D_MODEL, wsh), F32)],
        compiler_params=_params(("arbitrary",)),
    )(dq, dk, dv, dgc, h, x, dx_res, mod, nw, wio)


def _even_post_bwd(dx1, y, oa, olat, proj, gate, wuv, woe):
    S = dx1.shape[0]
    ts = min(ROW_TILE, S)
    nsteps = S // ts

    def body(dx_ref, y_ref, oa_ref, ol_ref, ga0_ref, ga1_ref, gb0_ref, gb1_ref, gate_ref, wuv_ref, woe_ref,
             doa_ref, dga_ref, dgb_ref, dol_ref, dwoe_out, dwuv_ref, stats_ref, dwoe_ref):
        @pl.when(pl.program_id(0) == 0)
        def _():
            dwoe_ref[...] = jnp.zeros((D_MODEL, D_MODEL), F32)
            dwuv_ref[...] = jnp.zeros((1024, 512), F32)
            stats_ref[...] = jnp.zeros((8, D_MODEL), F32)

        dxv = dx_ref[...]
        stats_ref[0:1, :] += _sum_rows(dxv * y_ref[...])
        dy = (gate_ref[...] * dxv).astype(BF16)
        sa, dsa = _silu_and_grad(jnp.concatenate([ga0_ref[...], ga1_ref[...]], axis=1))
        sb, dsb = _silu_and_grad(jnp.concatenate([gb0_ref[...], gb1_ref[...]], axis=1))
        olc = jnp.concatenate([ol_ref[hh] for hh in range(B_HEADS)], axis=1).astype(BF16)
        wuv_v = wuv_ref[...]
        ob = _dot(olc, wuv_v)
        oav = oa_ref[...]
        mix = jnp.concatenate([oav * sa, ob * sb], axis=1).astype(BF16)
        dmix = _dot_nt(dy, woe_ref[...])
        dwoe_ref[...] += _dot_tn(mix, dy)
        dma, dmb = dmix[:, 0:512], dmix[:, 512:1024]
        doa_ref[...] = dma * sa
        dga_ref[...] = (dma * oav * dsa).astype(BF16)
        dgb_ref[...] = (dmb * ob * dsb).astype(BF16)
        dob = (dmb * sb).astype(BF16)
        dol = _dot_nt(dob, wuv_v)
        dwuv_ref[...] += _dot_tn(olc, dob)
        for hh in range(B_HEADS):
            dol_ref[hh] = dol[:, LANES * hh:LANES * (hh + 1)]

        @pl.when(pl.program_id(0) == nsteps - 1)
        def _():
            dwoe_out[...] = dwoe_ref[...].astype(BF16)

    head_spec = pl.BlockSpec((B_HEADS, ts, LANES), lambda i: (0, i, 0))
    return pl.pallas_call(
        body, name="even_post_bwd", grid=(nsteps,),
        in_specs=[_row_spec(ts, D_MODEL), _row_spec(ts, D_MODEL), _row_spec(ts, 512), head_spec] + _even_gate_specs(ts) +
                 [_full_spec((1, D_MODEL)), _full_spec((1024, 512)), _full_spec((1024, D_MODEL))],
        out_specs=[_row_spec(ts, 512), _row_spec(ts, 512), _row_spec(ts, 512), head_spec,
                   _full_spec((D_MODEL, D_MODEL), single=False), _full_spec((1024, 512), single=False),
                   _full_spec((8, D_MODEL), single=False)],
        out_shape=[_sds((S, 512), F32), _sds((S, 512), BF16), _sds((S, 512), BF16), _sds((B_HEADS, S, LANES), F32),
                   _sds((D_MODEL, D_MODEL), BF16), _sds((1024, 512), F32), _sds((8, D_MODEL), F32)],
        scratch_shapes=[pltpu.VMEM((D_MODEL, D_MODEL), F32)],
        compiler_params=_params(("arbitrary",)),
    )(dx1, y, oa, olat, proj, proj, proj, proj, gate, wuv, woe)


def _even_pre_bwd(x, h, proj, dqa, dka, dva, dga, dgb, dqcat, dkcat, dx_res, mod, nw, wie, qn, kn, seg, ca, sa, ct, st,
                  qln, kvln, wuq, wuk):
    S = x.shape[0]
    ts = min(EVEN_PRE_BWD_ROW_TILE, S)
    nsteps = S // ts

    def body(x_ref, h_ref, proj_ref, dqa_ref, dka_ref, dva_ref, dga_ref, dgb_ref, dqc_ref, dkc_ref, dxr_ref, mod_ref, nw_ref,
             wie_ref, qn_ref, kn_ref, seg_ref, ca_ref, sa_ref, ct_ref, st_ref, qln_ref, kvln_ref, wuq_ref, wuk_ref,
             dx_ref, dwie_out, dwuq_out, dwuk_out, stats_ref, nstats_ref, dwie_ref, dwuq_ref, dwuk_ref):
        @pl.when(pl.program_id(0) == 0)
        def _():
            dwie_ref[...] = jnp.zeros((D_MODEL, EVEN_P), F32)
            dwuq_ref[...] = jnp.zeros((B_Q_LORA, 1536), F32)
            dwuk_ref[...] = jnp.zeros((512, 1024), F32)
            stats_ref[...] = jnp.zeros((8, D_MODEL), F32)
            nstats_ref[...] = jnp.zeros((8, 256), F32)

        lane = _lane_iota()
        ca_v, sa_v, ct_v, st_v = ca_ref[...], sa_ref[...], ct_ref[...], st_ref[...]
        seg_v = seg_ref[...]

        def head_norm_bwd(xc, dy, w):
            r = lax.rsqrt(_seg_mean(xc * xc, seg_v) + EPS)
            g = dy * w
            dxc = r * g - xc * (r * r * r) * _seg_mean(xc * g, seg_v)
            return dxc, _sum_rows(dy * (xc * r))

        pieces = []
        dqn = jnp.zeros((1, LANES), F32)
        for cb in range(4):
            sl = slice(LANES * cb, LANES * (cb + 1))
            dy = _rot_bwd(dqa_ref[:, sl] * 0.125, ca_v, sa_v, lane)
            dxc, dw = head_norm_bwd(proj_ref[:, sl], dy, qn_ref[...])
            pieces.append(dxc)
            dqn = dqn + dw
        dxc, dkn = head_norm_bwd(proj_ref[:, 512:640], _rot_bwd(_fold_heads(dka_ref[...], lane), ca_v, sa_v, lane), kn_ref[...])
        pieces += [dxc, _fold_heads(dva_ref[...], lane), dga_ref[...]]
        nstats_ref[0:1, 0:LANES] += dqn + pltpu.roll(dqn, HEAD_DIM, 1)
        nstats_ref[1:2, 0:LANES] += dkn + pltpu.roll(dkn, HEAD_DIM, 1)

        cq = proj_ref[:, 1280:1536]
        rq = _rms(cq)
        cqn_f = cq * rq
        qln_v = qln_ref[...]
        cqn = (cqn_f * qln_v).astype(BF16)
        wuq_v, wuk_v = wuq_ref[...], wuk_ref[...]
        qnope = _dot(cqn, wuq_v[:, 0:512]).astype(BF16)
        dqlat = jnp.concatenate([dqc_ref[hh, :, 0:LANES] for hh in range(B_HEADS)], axis=1).astype(BF16)
        dqnope = _dot_nt(dqlat, wuk_v)
        dwuk_ref[...] += _dot_tn(qnope, dqlat)
        dqr = [_rot_bwd(dqc_ref[hh, :, LANES:2 * LANES], ct_v, st_v, lane) for hh in range(B_HEADS)]
        dqb = jnp.concatenate([dqnope] + dqr, axis=1).astype(BF16)
        dcqn = _dot_nt(dqb, wuq_v)
        dwuq_ref[...] += _dot_tn(cqn, dqb)
        nstats_ref[2:3, :] += _sum_rows(dcqn * cqn_f)
        dcq = _rms_bwd(cq, rq, dcqn * qln_v)
        ckv = proj_ref[:, 1536:1664]
        rk = _rms(ckv)
        dckvn = dkc_ref[:, 0:LANES]
        nstats_ref[3:4, 0:LANES] += _sum_rows(dckvn * (ckv * rk))
        dckv = _rms_bwd(ckv, rk, dckvn * kvln_ref[...])
        dkr = _rot_bwd(dkc_ref[:, LANES:2 * LANES], ct_v, st_v, lane)
        pieces += [dcq, dckv, dkr, dgb_ref[...]]
        dproj = jnp.concatenate([piece.astype(BF16) for piece in pieces], axis=1)
        dh = _dot_nt(dproj, wie_ref[...])
        dwie_ref[...] += _dot_tn(h_ref[...], dproj)
        dx_ref[...] = dxr_ref[...] + _norm_mod_bwd(dh, x_ref[...], mod_ref, nw_ref, stats_ref)

        @pl.when(pl.program_id(0) == nsteps - 1)
        def _():
            pltpu.sync_copy(dwie_ref, dwie_out)
            pltpu.sync_copy(dwuq_ref, dwuq_out)
            pltpu.sync_copy(dwuk_ref, dwuk_out)

    return pl.pallas_call(
        body, name="even_pre_bwd", grid=(nsteps,),
        in_specs=[_row_spec(ts, D_MODEL), _row_spec(ts, D_MODEL), _row_spec(ts, EVEN_P), _row_spec(ts, 512), _row_spec(ts, 2 * LANES),
                  _row_spec(ts, 2 * LANES), _row_spec(ts, 512), _row_spec(ts, 512),
                  pl.BlockSpec((B_HEADS, ts, 2 * LANES), lambda i: (0, i, 0)), _row_spec(ts, 2 * LANES), _row_spec(ts, D_MODEL),
                  _full_spec((3, D_MODEL)), _full_spec((1, D_MODEL)), _full_spec((D_MODEL, EVEN_P)),
                  _full_spec((1, LANES)), _full_spec((1, LANES)), _full_spec((LANES, LANES)),
                  _row_spec(ts, LANES), _row_spec(ts, LANES), _row_spec(ts, LANES), _row_spec(ts, LANES),
                  _full_spec((1, B_Q_LORA)), _full_spec((1, B_KV_LORA)), _full_spec((B_Q_LORA, 1536)), _full_spec((512, 1024))],
        out_specs=[_row_spec(ts, D_MODEL), _ANY, _ANY, _ANY, _full_spec((8, D_MODEL), single=False), _full_spec((8, 256), single=False)],
        out_shape=[_sds((S, D_MODEL), F32), _sds((D_MODEL, EVEN_P), F32), _sds((B_Q_LORA, 1536), F32), _sds((512, 1024), F32),
                   _sds((8, D_MODEL), F32), _sds((8, 256), F32)],
        scratch_shapes=[pltpu.VMEM((D_MODEL, EVEN_P), F32), pltpu.VMEM((B_Q_LORA, 1536), F32), pltpu.VMEM((512, 1024), F32)],
        compiler_params=_params(("arbitrary",)),
    )(x, h, proj, dqa, dka, dva, dga, dgb, dqcat, dkcat, dx_res, mod, nw, wie, qn, kn, seg, ca, sa, ct, st, qln, kvln, wuq, wuk)


def _ada_fwd(c_all, w, b):
    n = w.shape[2]

    def body(c_ref, w_ref, b_ref, o_ref):
        cv = c_ref[...]
        o_ref[0] = _dot_f32(cv * _sigmoid(cv), w_ref[0]) + b_ref[0]

    return pl.pallas_call(
        body, name="ada_fwd", grid=(2,),
        in_specs=[pl.BlockSpec((N_DEV, D_MODEL), lambda l: (0, 0)), pl.BlockSpec((1, D_MODEL, n), lambda l: (l, 0, 0)),
                  pl.BlockSpec((1, 1, n), lambda l: (l, 0, 0))],
        out_specs=pl.BlockSpec((1, N_DEV, n), lambda l: (l, 0, 0)),
        out_shape=_sds((2, N_DEV, n), F32),
        compiler_params=_params(("arbitrary",)),
    )(c_all, w, b)


def _ada_bwd(c_all_t, dmod):
    n = dmod.shape[2]

    def body(c_ref, d_ref, o_ref):
        cv = c_ref[...]
        act = cv * _sigmoid(cv)
        dv = d_ref[0]
        acc = act[:, 0:1] * dv[0:1, :]
        for bb in range(1, N_DEV):
            acc = acc + act[:, bb:bb + 1] * dv[bb:bb + 1, :]
        o_ref[0] = acc

    return pl.pallas_call(
        body, name="ada_bwd", grid=(2,),
        in_specs=[pl.BlockSpec((D_MODEL, N_DEV), lambda l: (0, 0)), pl.BlockSpec((1, N_DEV, n), lambda l: (l, 0, 0))],
        out_specs=pl.BlockSpec((1, D_MODEL, n), lambda l: (l, 0, 0)),
        out_shape=_sds((2, D_MODEL, n), F32),
        compiler_params=_params(("arbitrary",)),
    )(c_all_t, dmod)


ADAM_ROW_TILE = 256


def _adam_update(g, w, m, v):
    m_new = ADAM_B1 * m + (1.0 - ADAM_B1) * g
    v_new = ADAM_B2 * v + (1.0 - ADAM_B2) * jnp.square(g)
    m_hat = m_new / (1.0 - ADAM_B1 ** ADAM_STEP)
    v_hat = v_new / (1.0 - ADAM_B2 ** ADAM_STEP)
    return -ADAM_LR * (m_hat / (jnp.sqrt(v_hat) + ADAM_EPS) + ADAM_WD * w), m_new, v_new


SMALL_ROWS = dict(dmod=(0, D_MODEL), norm_w=(6, D_MODEL), final_norm=(8, D_MODEL), a_q_norm=(9, HEAD_DIM), a_k_norm=(10, HEAD_DIM),
                  b_q_lora_norm=(11, B_Q_LORA), b_kv_lora_norm=(12, B_KV_LORA), c_sink=(13, C_HEADS))
SMALL_WEIGHTS = ("ada_b", "norm_w", "final_norm", "a_q_norm", "a_k_norm", "b_q_lora_norm", "b_kv_lora_norm", "c_sink")
LOSS_ROW = 14


def _pack_small(res):
    def padded(v):
        return jnp.concatenate([v, jnp.zeros((v.shape[0], D_MODEL - v.shape[1]), F32)], axis=1)

    rows = [res["dmod"].reshape(6, D_MODEL), res["norm_w"], res["final_norm"].reshape(1, D_MODEL)]
    rows += [padded(res[k]) for k in ("a_q_norm", "a_k_norm", "b_q_lora_norm", "b_kv_lora_norm", "c_sink")]
    return jnp.concatenate(rows + [res["loss_row"], jnp.zeros((1, D_MODEL), F32)], axis=0)


def _adam_small(parts, ws, ms, vs):
    nw = len(SMALL_WEIGHTS)

    def body(*refs):
        p_ref = refs[0]
        w_refs, m_refs, v_refs = refs[1:1 + nw], refs[1 + nw:1 + 2 * nw], refs[1 + 2 * nw:1 + 3 * nw]
        outs = refs[1 + 3 * nw:]
        g_all = p_ref[0]
        for k in range(1, N_DEV):
            g_all = g_all + p_ref[k]
        for idx, name in enumerate(SMALL_WEIGHTS):
            if name == "ada_b":
                g = jnp.concatenate([jnp.concatenate([g_all[3 * l + t:3 * l + t + 1] for t in range(3)], axis=1) for l in range(2)],
                                    axis=0)
            else:
                row, width = SMALL_ROWS[name]
                g = g_all[row:row + w_refs[idx].shape[0], 0:width]
            d, m_new, v_new = _adam_update(g, w_refs[idx][...], m_refs[idx][...], v_refs[idx][...])
            outs[4 * idx][...], outs[4 * idx + 1][...], outs[4 * idx + 2][...], outs[4 * idx + 3][...] = g, d, m_new, v_new
        outs[4 * nw][...] = g_all[LOSS_ROW:LOSS_ROW + 1, 0:LANES]

    out_shape = []
    for w in ws:
        out_shape += [_sds(w.shape, F32)] * 4
    out_shape.append(_sds((1, LANES), F32))
    return pl.pallas_call(body, name="adam_small", out_shape=out_shape,
                          compiler_params=pltpu.CompilerParams(vmem_limit_bytes=VMEM_LIMIT))(parts, *ws, *ms, *vs)


def _adam(parts, w, m, v, name):
    P, R, C = parts.shape
    tr = R if R <= ADAM_ROW_TILE else ADAM_ROW_TILE
    assert R % tr == 0

    def body(p_ref, w_ref, m_ref, v_ref, g_ref, d_ref, nm_ref, nv_ref):
        g = p_ref[0].astype(F32)
        for k in range(1, P):
            g = g + p_ref[k].astype(F32)
        g_ref[...] = g
        d_ref[...], nm_ref[...], nv_ref[...] = _adam_update(g, w_ref[...], m_ref[...], v_ref[...])

    spec = pl.BlockSpec((tr, C), lambda i: (i, 0))
    return pl.pallas_call(
        body, name=name, grid=(R // tr,),
        in_specs=[pl.BlockSpec((P, tr, C), lambda i: (0, i, 0)), spec, spec, spec],
        out_specs=[spec, spec, spec, spec], out_shape=[_sds((R, C), F32)] * 4,
        compiler_params=_params(("arbitrary",)),
    )(parts, w, m, v)


_ANY = pl.BlockSpec(memory_space=pl.ANY)
CHIP_FLIPS = ((1, 0), (0, 1), (1, 1))
DEV_FLIPS = tuple((dx, dy, dc) for dx in (0, 1) for dy in (0, 1) for dc in (0, 1) if dx + dy + dc)


def _flip(a, d):
    return a if d == 0 else 1 - a


def _my_place():
    return lax.axis_index("x"), lax.axis_index("y"), lax.axis_index("c")


def _gather8_copies(ins, outs, send_sems, recv_sems, loc_sems):
    x, y, c = _my_place()
    me = 4 * x + 2 * y + c
    copies = []
    for a in range(len(ins)):
        copies.append(pltpu.make_async_copy(ins[a], outs[a].at[me], loc_sems.at[a]))
        for k, (dx, dy, dc) in enumerate(DEV_FLIPS):
            copies.append(pltpu.make_async_remote_copy(
                src_ref=ins[a], dst_ref=outs[a].at[me], send_sem=send_sems.at[a, k], recv_sem=recv_sems.at[a, k],
                device_id=(_flip(x, dx), _flip(y, dy), _flip(c, dc)), device_id_type=MESH_ID))
    return copies


def _gather8_sems(n):
    return [pltpu.SemaphoreType.DMA((n, 7)), pltpu.SemaphoreType.DMA((n, 7)), pltpu.SemaphoreType.DMA((n,))]


def _gather_dev8(arrs, name):
    n = len(arrs)

    def body(*refs):
        copies = _gather8_copies(refs[:n], refs[n:2 * n], *refs[2 * n:])
        for cp in copies:
            cp.start()
        for cp in copies:
            cp.wait()

    return pl.pallas_call(
        body, name=name, in_specs=[_ANY] * n, out_specs=[_ANY] * n,
        out_shape=[_sds((N_DEV,) + a.shape, a.dtype) for a in arrs], scratch_shapes=_gather8_sems(n),
    )(*arrs)


class _Exchange:
    def __init__(self, arrs, out_shapes, n_sems, phases):
        self.arrs, self.out_shapes, self.n_sems, self._phases = list(arrs), list(out_shapes), n_sems, phases

    @property
    def n(self):
        return len(self.arrs)

    def sem_shapes(self):
        return [pltpu.SemaphoreType.DMA((self.n, self.n_sems)), pltpu.SemaphoreType.DMA((self.n, self.n_sems)),
                pltpu.SemaphoreType.DMA((self.n,))]

    def phases(self, ins, outs, sems):
        return self._phases(ins, outs, *sems)

    def run(self, name):
        n = self.n

        def body(*refs):
            start, mid, end = self.phases(refs[:n], refs[n:2 * n], refs[2 * n:])
            start()
            mid()
            end()

        return pl.pallas_call(body, name=name, in_specs=[_ANY] * n, out_specs=[_ANY] * n, out_shape=self.out_shapes,
                              scratch_shapes=self.sem_shapes())(*self.arrs)

    def run_with_gather(self, gather_arrs, name):
        n, g = self.n, len(gather_arrs)

        def body(*refs):
            ins, g_ins, outs, g_outs = refs[:n], refs[n:n + g], refs[n + g:2 * n + g], refs[2 * n + g:2 * (n + g)]
            sems = refs[2 * (n + g):]
            start, mid, end = self.phases(ins, outs, sems[:3])
            copies = _gather8_copies(g_ins, g_outs, *sems[3:])
            start()
            for cp in copies:
                cp.start()
            mid()
            end()
            for cp in copies:
                cp.wait()

        outs = pl.pallas_call(
            body, name=name, in_specs=[_ANY] * (n + g), out_specs=[_ANY] * (n + g),
            out_shape=self.out_shapes + [_sds((N_DEV,) + a.shape, a.dtype) for a in gather_arrs],
            scratch_shapes=self.sem_shapes() + _gather8_sems(g))(*self.arrs, *gather_arrs)
        return outs[:n], outs[n:]


def _gather_halves_phases(ins, outs, send_sems, recv_sems, loc_sems):
    n = len(ins)
    x, y, c = _my_place()
    chip = 2 * x + y
    sibling = (x, y, 1 - c)
    peers = [(_flip(x, dx), _flip(y, dy)) for dx, dy in CHIP_FLIPS]

    def remote(src, p, half, a, k, to):
        return pltpu.make_async_remote_copy(src_ref=src, dst_ref=outs[a].at[p, half], send_sem=send_sems.at[a, k],
                                            recv_sem=recv_sems.at[a, k], device_id=to, device_id_type=MESH_ID)

    def local(a):
        return pltpu.make_async_copy(ins[a], outs[a].at[chip], loc_sems.at[a])

    def first(a, k):
        return remote(ins[a].at[c], chip, c, a, k, (*peers[k], c))

    def passed(a, k):
        p = 2 * peers[k][0] + peers[k][1]
        return remote(outs[a].at[p, c], p, c, a, 3 + k, sibling)

    def start():
        for a in range(n):
            local(a).start()
            for k in range(3):
                first(a, k).start()

    def mid():
        for a in range(n):
            for k in range(3):
                p = 2 * peers[k][0] + peers[k][1]
                remote(outs[a].at[p, c], p, c, a, k, sibling).wait_recv()
                passed(a, k).start()

    def end():
        for a in range(n):
            for k in range(3):
                p = 2 * peers[k][0] + peers[k][1]
                remote(outs[a].at[p, 1 - c], p, 1 - c, a, 3 + k, sibling).wait_recv()
        for a in range(n):
            for k in range(3):
                first(a, k).wait_send()
                passed(a, k).wait_send()
            local(a).wait()

    return start, mid, end


def _gather_chip4_halves(arrs):
    return _Exchange(arrs, [_sds((N_CHIPS,) + a.shape, a.dtype) for a in arrs], 6, _gather_halves_phases)


def _reduce_phases(ins, outs, send_sems, recv_sems, loc_sems):
    n = len(ins)
    x, y, c = _my_place()
    chip = 2 * x + y
    sibling = (x, y, 1 - c)
    peers = [(_flip(x, dx), _flip(y, dy)) for dx, dy in CHIP_FLIPS]

    def remote(src, slot, a, k, to):
        return pltpu.make_async_remote_copy(src_ref=src, dst_ref=outs[a].at[slot], send_sem=send_sems.at[a, k],
                                            recv_sem=recv_sems.at[a, k], device_id=to, device_id_type=MESH_ID)

    def local(a):
        return pltpu.make_async_copy(ins[a].at[chip], outs[a].at[2 * chip + c], loc_sems.at[a])

    def own(a):
        return remote(ins[a].at[chip], 2 * chip + c, a, 0, sibling)

    def first(a, k):
        return remote(ins[a].at[2 * peers[k][0] + peers[k][1]], 2 * chip + c, a, 1 + k, (*peers[k], c))

    def passed(a, k):
        slot = 2 * (2 * peers[k][0] + peers[k][1]) + c
        return remote(outs[a].at[slot], slot, a, 4 + k, sibling)

    def start():
        for a in range(n):
            local(a).start()
            own(a).start()
            for k in range(3):
                first(a, k).start()

    def mid():
        for a in range(n):
            for k in range(3):
                slot = 2 * (2 * peers[k][0] + peers[k][1]) + c
                remote(outs[a].at[slot], slot, a, 1 + k, sibling).wait_recv()
                passed(a, k).start()

    def end():
        for a in range(n):
            remote(outs[a].at[2 * chip + 1 - c], 2 * chip + 1 - c, a, 0, sibling).wait_recv()
            for k in range(3):
                slot = 2 * (2 * peers[k][0] + peers[k][1]) + 1 - c
                remote(outs[a].at[slot], slot, a, 4 + k, sibling).wait_recv()
        for a in range(n):
            own(a).wait_send()
            for k in range(3):
                first(a, k).wait_send()
                passed(a, k).wait_send()
            local(a).wait()

    return start, mid, end


def _reduce_exchange(arrs):
    return _Exchange(arrs, [_sds((N_DEV,) + a.shape[1:], a.dtype) for a in arrs], 7, _reduce_phases)


def _even_in_layout(w):
    return jnp.concatenate([w[:, 0:1696], jnp.zeros((w.shape[0], 96), w.dtype), w[:, 1696:2208]], axis=1)


def _even_in_unlayout(g):
    return jnp.concatenate([g[:, 0:1696], g[:, 1792:2304]], axis=1)


def _uq_layout(w):
    per = B_NOPE + B_ROPE
    pad = jnp.zeros((w.shape[0], LANES - B_ROPE), w.dtype)
    nope = [w[:, per * h:per * h + B_NOPE] for h in range(B_HEADS)]
    rope = [jnp.concatenate([w[:, per * h + B_NOPE:per * (h + 1)], pad], axis=1) for h in range(B_HEADS)]
    return jnp.concatenate(nope + rope, axis=1)


def _uq_unlayout(g):
    parts = []
    for h in range(B_HEADS):
        parts += [g[:, B_NOPE * h:B_NOPE * (h + 1)], g[:, 512 + LANES * h:512 + LANES * h + B_ROPE]]
    return jnp.concatenate(parts, axis=1)


def _block_diag(blocks):
    rows = []
    for h, blk in enumerate(blocks):
        r, cdim = blk.shape
        n = len(blocks)
        rows.append(jnp.concatenate([jnp.zeros((r, cdim * h), blk.dtype), blk, jnp.zeros((r, cdim * (n - 1 - h)), blk.dtype)],
                                    axis=1))
    return jnp.concatenate(rows, axis=0)


def _uk_layout(w):
    return _block_diag([w[:, h, :].T for h in range(B_HEADS)])


def _uk_unlayout(g):
    return jnp.stack([g[B_NOPE * h:B_NOPE * (h + 1), LANES * h:LANES * (h + 1)].T for h in range(B_HEADS)], axis=1)


def _uv_layout(w):
    return _block_diag([w[:, h, :] for h in range(B_HEADS)])


def _uv_unlayout(g):
    return jnp.stack([g[LANES * h:LANES * (h + 1), B_V * h:B_V * (h + 1)] for h in range(B_HEADS)], axis=1)


def _rope_tables(S):
    inv = ROPE_THETA ** (-jnp.arange(0, 32, 2, dtype=F32) / 32)
    tok = jnp.arange(S)

    def tab(pos):
        ang = pos.astype(F32)[:, None] * inv[None, :]
        cos, sin = jnp.cos(ang), jnp.sin(ang)
        return jnp.concatenate([cos, cos], axis=1), jnp.concatenate([-sin, sin], axis=1)

    cr, sr = tab(tok // GRID_W)
    cc, sc = tab(tok % GRID_W)
    ct, st = tab(tok)
    return (jnp.tile(jnp.concatenate([cr, cc], axis=1), (1, 2)), jnp.tile(jnp.concatenate([sr, sc], axis=1), (1, 2)),
            jnp.tile(ct, (1, 4)), jnp.tile(st, (1, 4)))


A_TQ, A_TK, A_SUB = 512, 4096, 512
B_TQ, B_TK, B_SUB = 128, 4096, 1024
B_BWD_TK, B_BWD_SUB = 4096, 512
C_T = 256
C_BLOCKS_PER_STEP = 8
KV_SHARE = 2


def _local_step(x0, tgt, mod, norm_w, wie, wuq, wuk, wuv, late_shards, a_q_norm, a_k_norm, q_lora_norm, kv_lora_norm,
                c_sink, final_norm):
    S = x0.shape[0]
    mod3 = mod.reshape(2, 3, D_MODEL)
    ca, sa, ct, st = _rope_tables(S)
    lane_seg = np.arange(LANES) // HEAD_DIM
    seg = jnp.asarray((lane_seg[:, None] == lane_seg[None, :]).astype(np.float32)).astype(BF16)
    qn = jnp.tile(a_q_norm.reshape(1, HEAD_DIM), (1, 2))
    kn = jnp.tile(a_k_norm.reshape(1, HEAD_DIM), (1, 2))
    qln, kvln = q_lora_norm.reshape(1, B_Q_LORA), kv_lora_norm.reshape(1, B_KV_LORA)
    nw0, nw1 = norm_w[0:1], norm_w[1:2]
    gate0, gate1 = mod3[0, 2:3], mod3[1, 2:3]
    a_tq, a_tk, b_tq, b_tk, bb_tk, c_t = min(A_TQ, S), min(A_TK, S), min(B_TQ, S), min(B_TK, S), min(B_BWD_TK, S), min(C_T, S)
    a_sub, b_sub, bb_sub = min(A_SUB, a_tk), min(B_SUB, b_tk), min(B_BWD_SUB, bb_tk)

    h0, proj_e, qa, ka, va, qcat, kcat, ka_t, va_t, kcat_t = _even_pre_fwd(x0, mod3[0], nw0, wie, qn, kn, seg, ca, sa, ct, st,
                                                                           qln, kvln, wuq, wuk)
    oa, lse_a, woe_g, wio_g, woo_g = _pp_fwd(qa, ka, va_t, kdiv=KV_SHARE, tq=a_tq, tk=a_tk, sub=a_sub, name="attn_a_fwd",
                                             side=_gather_chip4_halves(late_shards))
    woe = woe_g.reshape(D_MODEL, D_MODEL)
    wio = wio_g.reshape(N_CHIPS, D_MODEL, ODD_IN // N_CHIPS)
    woo = woo_g.reshape(D_MODEL, D_MODEL)
    olat, lse_b = _mla_fwd(qcat, kcat, kcat_t, tq=b_tq, tk=b_tk, sub=b_sub)
    y0, x1 = _even_post_fwd(oa, olat, proj_e, x0, gate0, wuv, woe)
    h1, proj_o, qc, kc, vc, kc_t, vc_t = _odd_pre_fwd(x1, mod3[1], nw1, wio)
    slopes = 2.0 ** (-8.0 * jnp.arange(1, C_HEADS + 1, dtype=F32) / C_HEADS)
    slope_rows = jnp.repeat(slopes.reshape(C_HEADS // 2, 2), c_t, axis=1)[:, None, :]
    sink_rows = jnp.repeat(c_sink.reshape(C_HEADS // 2, 2), c_t, axis=1)[:, None, :]
    win_dist = _win_dist_table(S, c_t)
    oc, lse_c = _win_fwd(qc, kc, vc_t, win_dist, slope_rows, sink_rows, kdiv=KV_SHARE, tq=c_t, nbs=C_BLOCKS_PER_STEP,
                         name="attn_c_fwd")
    doc, dgc, dx2, dwoo, st_f = _odd_post(oc, proj_o, x1, gate1, woo, final_norm.reshape(1, D_MODEL), tgt)
    dqc, dkc, dvc, dsink_raw = _win_bwd(qc, kc, kc_t, vc, oc, doc, lse_c, win_dist, slope_rows, sink_rows, kdiv=KV_SHARE, tq=c_t,
                                        nbs=C_BLOCKS_PER_STEP, name="attn_c_bwd")
    dx1, dwio, st_1 = _odd_pre_bwd(dqc, dkc, dvc, dgc, h1, x1, dx2, mod3[1], nw1, wio)
    doa, dga, dgb, dolat, dwoe, dwuv, st_e = _even_post_bwd(dx1, y0, oa, olat, proj_e, gate0, wuv, woe)
    late_grads = _reduce_exchange([dwoe.reshape(N_CHIPS, D_MODEL // N_CHIPS, D_MODEL), dwio,
                                   dwoo.reshape(N_CHIPS, D_MODEL // N_CHIPS, D_MODEL)])
    dqa, dka, dva, p_woe, p_wio, p_woo = _pp_bwd(qa, ka, ka_t, va, oa, doa, lse_a, kdiv=KV_SHARE, tq=a_tq, tk=a_tk, sub=a_sub,
                                                 name="attn_a_bwd", side=late_grads)
    dqcat, dkcat = _mla_bwd(qcat, kcat, kcat_t, olat, dolat, lse_b, tq=b_tq, tk=bb_tk, sub=bb_sub)
    dx0, dwie, dwuq, dwuk, st_0, nst = _even_pre_bwd(x0, h0, proj_e, dqa, dka, dva, dga, dgb, dqcat, dkcat, dx1, mod3[0], nw0,
                                                     wie, qn, kn, seg, ca, sa, ct, st, qln, kvln, wuq, wuk)
    dsink_pairs = jnp.stack([dsink_raw[:, 0, 0], dsink_raw[:, 1, 0]], axis=1).reshape(C_HEADS)
    return dict(
        loss_row=st_f[2:3], dx=dx0,
        dmod=jnp.stack([jnp.concatenate([st_0[0], st_0[1], st_e[0]]), jnp.concatenate([st_1[0], st_1[1], st_f[1]])]),
        norm_w=jnp.stack([st_0[2], st_1[2]]), final_norm=st_f[0],
        a_q_norm=nst[0:1, 0:HEAD_DIM], a_k_norm=nst[1:2, 0:HEAD_DIM], b_q_lora_norm=nst[2:3, :], b_kv_lora_norm=nst[3:4, 0:LANES],
        c_sink=dsink_pairs.reshape(1, C_HEADS),
        even_w_in=dwie, b_w_uq=dwuq, b_w_uk=dwuk, b_w_uv=dwuv, even_w_out=p_woe, odd_w_in=p_wio, odd_w_out=p_woo)


WEIGHT_NAMES = ("norm_w", "ada_w", "ada_b", "even_w_in", "a_q_norm", "a_k_norm", "b_q_lora_norm", "b_kv_lora_norm", "b_w_uq",
                "b_w_uk", "b_w_uv", "even_w_out", "odd_w_in", "c_sink", "odd_w_out", "final_norm")


def _cols_to_chips(g):
    r, n4 = g.shape
    return jnp.transpose(g.reshape(r, N_CHIPS, n4 // N_CHIPS), (1, 0, 2))


def _chips_to_cols(g):
    p, r, n = g.shape
    return jnp.transpose(g, (1, 0, 2)).reshape(r, p * n)


def kernel(x, c, norm_w, ada_w, ada_b, even_w_in, a_q_norm, a_k_norm, b_q_lora_norm, b_kv_lora_norm, b_w_uq, b_w_uk, b_w_uv, even_w_out, odd_w_in, c_sink, odd_w_out, final_norm, loss_target, m_norm_w, m_ada_w, m_ada_b, m_even_w_in, m_a_q_norm, m_a_k_norm, m_b_q_lora_norm, m_b_kv_lora_norm, m_b_w_uq, m_b_w_uk, m_b_w_uv, m_even_w_out, m_odd_w_in, m_c_sink, m_odd_w_out, m_final_norm, v_norm_w, v_ada_w, v_ada_b, v_even_w_in, v_a_q_norm, v_a_k_norm, v_b_q_lora_norm, v_b_kv_lora_norm, v_b_w_uq, v_b_w_uk, v_b_w_uv, v_even_w_out, v_odd_w_in, v_c_sink, v_odd_w_out, v_final_norm):
    given = dict(locals())
    xi, yi, ci = _my_place()
    chip = 2 * xi + yi
    dev = 2 * chip + ci
    n_ada = ada_w.shape[2]

    (c_all,) = _gather_dev8([c], "gather_c")
    c_all = c_all.reshape(N_DEV, D_MODEL)
    bias = lax.dynamic_slice_in_dim(ada_b, chip * n_ada, n_ada, axis=1).reshape(2, 1, n_ada)
    mod_cols = _ada_fwd(c_all, ada_w, bias)
    def halves(w):
        return w.astype(BF16).reshape((2, w.shape[0] // 2) + w.shape[1:])

    mod_all, wie_g, wuq_g = _gather_chip4_halves([mod_cols, halves(even_w_in[0]), halves(b_w_uq[0])]).run("gather_weights")
    wie_g = wie_g.reshape(N_CHIPS, D_MODEL, EVEN_IN // N_CHIPS)
    wuq_g = wuq_g.reshape(N_CHIPS, B_Q_LORA, -1)
    mod = jnp.transpose(lax.dynamic_index_in_dim(mod_all, dev, axis=2, keepdims=False), (1, 0, 2)).reshape(2, 3 * D_MODEL)

    res = _local_step(
        x[0], loss_target[0], mod, norm_w,
        _even_in_layout(_chips_to_cols(wie_g)), _uq_layout(_chips_to_cols(wuq_g)), _uk_layout(b_w_uk[0].astype(BF16)),
        _uv_layout(b_w_uv[0].astype(BF16)), [halves(even_w_out[0]), halves(odd_w_in[0]), halves(odd_w_out[0])],
        a_q_norm, a_k_norm, b_q_lora_norm, b_kv_lora_norm, c_sink, final_norm)

    latent = jnp.stack([_uk_unlayout(res["b_w_uk"]).reshape(B_KV_LORA, 512),
                        _uv_unlayout(res["b_w_uv"]).reshape(B_KV_LORA, 512)]).astype(BF16)
    (p_wie, p_wuq), (small_all, latent_all) = _reduce_exchange(
        [_cols_to_chips(_even_in_unlayout(res["even_w_in"].astype(BF16))), _cols_to_chips(_uq_unlayout(res["b_w_uq"].astype(BF16)))]
    ).run_with_gather([_pack_small(res), latent], "reduce_exchange")
    shard_parts = dict(even_w_in=p_wie, b_w_uq=p_wuq, **{k: res[k] for k in ("even_w_out", "odd_w_in", "odd_w_out")})
    dmod_all = small_all[:, 0:6, :].reshape(N_DEV, 2, 3 * D_MODEL)
    dmod_cols = jnp.transpose(lax.dynamic_slice_in_dim(dmod_all, chip * n_ada, n_ada, axis=2), (1, 0, 2))
    parts = dict(shard_parts)
    parts["ada_w"] = _ada_bwd(c_all.T, dmod_cols).reshape(1, 2 * D_MODEL, n_ada)
    parts["b_w_uk"], parts["b_w_uv"] = latent_all[:, 0], latent_all[:, 1]

    def as2d(a):
        return a.reshape((-1, a.shape[-1]) if a.ndim > 1 else (1, a.shape[0]))

    results = {}
    small_outs = _adam_small(small_all, *[[as2d(given[pre + k]) for k in SMALL_WEIGHTS] for pre in ("", "m_", "v_")])
    for idx, k in enumerate(SMALL_WEIGHTS):
        results[k] = small_outs[4 * idx:4 * idx + 4]
    for k, p in parts.items():
        shape2 = (p.shape[-2], p.shape[-1])
        results[k] = _adam(p, given[k].reshape(shape2), given["m_" + k].reshape(shape2), given["v_" + k].reshape(shape2),
                           "adam_" + k)
    by_kind = [[results[k][t].reshape(given[k].shape) for k in WEIGHT_NAMES] for t in range(4)]
    return (small_outs[-1][0, 0], res["dx"][None], *by_kind[0], *by_kind[1], *by_kind[2], *by_kind[3])
```

```python
import numpy as np
import jax
import jax.numpy as jnp
from jax import lax
from jax.experimental import pallas as pl
from jax.experimental.pallas import tpu as pltpu

F32 = jnp.float32
BF16 = jnp.bfloat16
HIGHEST = lax.Precision.HIGHEST
MESH_ID = pl.DeviceIdType.MESH

D_MODEL = 1024
HEAD_DIM = 64
GRID_W = 64
EPS = 1e-6
ROPE_THETA = 10000.0
B_HEADS, B_NOPE, B_ROPE, B_V = 8, 64, 32, 64
B_Q_LORA, B_KV_LORA = 256, 128
C_HEADS = 16
WINDOW = 128
EVEN_IN, ODD_IN = 2208, 2560
EVEN_P = 2304
N_CHIPS, N_DEV = 4, 8
LANES = 128
NEG = -1e30
VMEM_LIMIT = 60 * 1024 * 1024

ADAM_LR, ADAM_B1, ADAM_B2, ADAM_EPS, ADAM_WD, ADAM_STEP = 0.001, 0.9, 0.999, 1e-08, 0.01, 10

ROW_TILE = 512
IN_PROJ_ROW_TILE = 256


def _dot(a, b):
    return lax.dot_general(a, b, (((1,), (0,)), ((), ())), preferred_element_type=F32)


def _dot_nt(a, b):
    return lax.dot_general(a, b, (((1,), (1,)), ((), ())), preferred_element_type=F32)


def _dot_tn(a, b):
    return lax.dot_general(a, b, (((0,), (0,)), ((), ())), preferred_element_type=F32)


def _dot_f32(a, b):
    return lax.dot_general(a, b, (((1,), (0,)), ((), ())), precision=HIGHEST, preferred_element_type=F32)


def _sigmoid(x):
    return 1.0 / (1.0 + jnp.exp(-x))


def _silu_and_grad(g):
    s = _sigmoid(g)
    return g * s, s * (1.0 + g * (1.0 - s))


def _lane_iota():
    return lax.broadcasted_iota(jnp.int32, (1, LANES), 1)


def _partner(x, lane):
    return jnp.where((lane % 32) < 16, pltpu.roll(x, LANES - 16, 1), pltpu.roll(x, 16, 1))


def _rot(x, cos, sin_signed, lane):
    return x * cos + _partner(x, lane) * sin_signed


def _rot_bwd(dy, cos, sin_signed, lane):
    return dy * cos + _partner(dy * sin_signed, lane)


def _rms(x):
    return lax.rsqrt(jnp.mean(x * x, axis=-1, keepdims=True) + EPS)


def _rms_bwd(x, r, g):
    return r * g - x * (r * r * r) * jnp.mean(x * g, axis=-1, keepdims=True)


def _seg_mean(v, seg_ones):
    hi = v.astype(BF16)
    lo = (v - hi.astype(F32)).astype(BF16)
    return (_dot(hi, seg_ones) + _dot(lo, seg_ones)) * (1.0 / HEAD_DIM)


def _dup_heads(x, lane):
    swapped = pltpu.roll(x, HEAD_DIM, 1)
    lo = lane < HEAD_DIM
    return jnp.concatenate([jnp.where(lo, x, swapped), jnp.where(lo, swapped, x)], axis=1)


def _fold_heads(x2, lane):
    a, b = x2[:, 0:LANES], x2[:, LANES:2 * LANES]
    return jnp.where(lane < HEAD_DIM, a + pltpu.roll(a, HEAD_DIM, 1), b + pltpu.roll(b, HEAD_DIM, 1))


def _row_spec(ts, cols):
    return pl.BlockSpec((ts, cols), lambda i: (i, 0))


def _full_spec(shape, single=True):
    nd = len(shape)
    if single:
        return pl.BlockSpec(shape, lambda i: (0,) * nd, pipeline_mode=pl.Buffered(1))
    return pl.BlockSpec(shape, lambda i: (0,) * nd)


def _sds(shape, dtype):
    return jax.ShapeDtypeStruct(shape, dtype)


def _params(sem):
    return pltpu.CompilerParams(dimension_semantics=sem, vmem_limit_bytes=VMEM_LIMIT)


def _even_pre_fwd(x, mod, nw, wie, qn, kn, seg, ca, sa, ct, st, qln, kvln, wuq, wuk):
    S = x.shape[0]
    ts = min(IN_PROJ_ROW_TILE, S)

    def body(x_ref, mod_ref, nw_ref, wie_ref, qn_ref, kn_ref, seg_ref, ca_ref, sa_ref, ct_ref, st_ref, qln_ref,
             kvln_ref, wuq_ref, wuk_ref, h_ref, proj_ref, qa_ref, ka_ref, va_ref, qcat_ref, kcat_ref, kat_ref, vat_ref, kcatt_ref):
        xv = x_ref[...]
        h = (xv * _rms(xv) * nw_ref[...]) * (1.0 + mod_ref[1:2, :]) + mod_ref[0:1, :]
        hb = h.astype(BF16)
        h_ref[...] = hb
        proj = _dot(hb, wie_ref[...])
        proj_ref[...] = proj
        lane = _lane_iota()
        ca_v, sa_v, ct_v, st_v = ca_ref[...], sa_ref[...], ct_ref[...], st_ref[...]
        seg_v = seg_ref[...]
        for cb in range(4):
            xc = proj[:, LANES * cb:LANES * (cb + 1)]
            r = lax.rsqrt(_seg_mean(xc * xc, seg_v) + EPS)
            y = _rot(xc * r * qn_ref[...], ca_v, sa_v, lane)
            qa_ref[:, LANES * cb:LANES * (cb + 1)] = (y * 0.125).astype(BF16)
        kc = proj[:, 512:640]
        r = lax.rsqrt(_seg_mean(kc * kc, seg_v) + EPS)
        ka_v = _dup_heads(_rot(kc * r * kn_ref[...], ca_v, sa_v, lane), lane)
        ka_ref[...] = ka_v.astype(BF16)
        kat_ref[...] = ka_v.T.astype(BF16)
        va_v = _dup_heads(proj[:, 640:768], lane)
        va_ref[...] = va_v.astype(BF16)
        vat_ref[...] = va_v.T.astype(BF16)
        cq = proj[:, 1280:1536]
        cqn = (cq * _rms(cq) * qln_ref[...]).astype(BF16)
        ckv = proj[:, 1536:1664]
        ckvn = ckv * _rms(ckv) * kvln_ref[...]
        qb = _dot(cqn, wuq_ref[...])
        qlat = _dot(qb[:, 0:512].astype(BF16), wuk_ref[...])
        for hh in range(B_HEADS):
            qcat_ref[hh, :, 0:LANES] = qlat[:, LANES * hh:LANES * (hh + 1)].astype(BF16)
            qr = _rot(qb[:, 512 + LANES * hh:512 + LANES * (hh + 1)], ct_v, st_v, lane)
            qcat_ref[hh, :, LANES:2 * LANES] = qr.astype(BF16)
        kr = _rot(proj[:, 1664:1792], ct_v, st_v, lane)
        kcat_ref[:, 0:LANES] = ckvn.astype(BF16)
        kcat_ref[:, LANES:2 * LANES] = kr.astype(BF16)
        kcatt_ref[0:LANES, :] = ckvn.T.astype(BF16)
        kcatt_ref[LANES:2 * LANES, :] = kr.T.astype(BF16)

    col_spec = lambda rows: pl.BlockSpec((rows, ts), lambda i: (0, i))
    return pl.pallas_call(
        body, name="even_pre_fwd", grid=(S // ts,),
        in_specs=[_row_spec(ts, D_MODEL), _full_spec((3, D_MODEL)), _full_spec((1, D_MODEL)), _full_spec((D_MODEL, EVEN_P)),
                  _full_spec((1, LANES)), _full_spec((1, LANES)), _full_spec((LANES, LANES)),
                  _row_spec(ts, LANES), _row_spec(ts, LANES), _row_spec(ts, LANES), _row_spec(ts, LANES),
                  _full_spec((1, B_Q_LORA)), _full_spec((1, B_KV_LORA)), _full_spec((B_Q_LORA, 1536)), _full_spec((512, 1024))],
        out_specs=[_row_spec(ts, D_MODEL), _row_spec(ts, EVEN_P), _row_spec(ts, 512), _row_spec(ts, 2 * LANES), _row_spec(ts, 2 * LANES),
                   pl.BlockSpec((B_HEADS, ts, 2 * LANES), lambda i: (0, i, 0)), _row_spec(ts, 2 * LANES),
                   col_spec(2 * LANES), col_spec(2 * LANES), col_spec(2 * LANES)],
        out_shape=[_sds((S, D_MODEL), BF16), _sds((S, EVEN_P), F32), _sds((S, 512), BF16), _sds((S, 2 * LANES), BF16),
                   _sds((S, 2 * LANES), BF16), _sds((B_HEADS, S, 2 * LANES), BF16), _sds((S, 2 * LANES), BF16),
                   _sds((2 * LANES, S), BF16), _sds((2 * LANES, S), BF16), _sds((2 * LANES, S), BF16)],
        compiler_params=_params(("arbitrary",)),
    )(x, mod, nw, wie, qn, kn, seg, ca, sa, ct, st, qln, kvln, wuq, wuk)


MLA_SCALE = (B_NOPE + B_ROPE) ** -0.5
LOG2E = 1.4426950408889634

def _row_lo():
    return lax.broadcasted_iota(jnp.int32, (LANES, 1), 0) < HEAD_DIM


def _stack_cols(vT, rlo):
    zero = jnp.zeros_like(vT)
    return jnp.concatenate([jnp.where(rlo, vT, zero), jnp.where(rlo, zero, vT)], axis=1)


def _stack_rows(v, lo):
    zero = jnp.zeros_like(v)
    return jnp.concatenate([jnp.where(lo, v, zero), jnp.where(lo, zero, v)], axis=0)


def _pick_halves_T(xT, rlo, t):
    return jnp.where(rlo, xT[:, 0:t], xT[:, t:2 * t]).T


def _side_split(refs, n_in, n_out, n_scratch, side):
    ns = side.n if side is not None else 0
    cuts = np.cumsum([0, n_in, ns, n_out, ns, n_scratch])
    return [refs[a:b] for a, b in zip(cuts[:-1], cuts[1:])] + [refs[cuts[-1]:]]


def _side_hooks(side, side_ins, side_outs, side_sems, step, total):
    if side is None:
        return lambda: None
    start, mid, end = side.phases(side_ins, side_outs, side_sems)
    pl.when(step == 0)(start)
    pl.when(step == total // 2)(mid)
    return lambda: pl.when(step == total - 1)(end)


def _side_specs(side):
    if side is None:
        return [], [], [], [], []
    return list(side.arrs), [_ANY] * side.n, [_ANY] * side.n, list(side.out_shapes), side.sem_shapes()


def _pp_fwd(q, k, vT, *, kdiv, tq, tk, sub, name, side=None):
    S = k.shape[0]; nb = q.shape[1] // LANES; nq = S // tq; nkv = S // tk; nsub = tk // sub

    def body(*refs):
        (q_ref, k_ref, vT_ref), side_ins, (o_ref, lse_ref), side_outs, (qs, m_s, l_s, acc), side_sems = _side_split(refs, 3, 2, 4, side)
        j = pl.program_id(2)
        rlo = _row_lo()
        step = (pl.program_id(0) * nq + pl.program_id(1)) * nkv + j
        side_end = _side_hooks(side, side_ins, side_outs, side_sems, step, nb * nq * nkv)

        @pl.when(j == 0)
        def _():
            qs[...] = _stack_cols(q_ref[...].astype(F32).T, rlo).astype(BF16)
            m_s[...] = jnp.full((1, 2 * tq), NEG, F32)
            l_s[...] = jnp.zeros((1, 2 * tq), F32)
            acc[...] = jnp.zeros((LANES, 2 * tq), F32)

        qsv = qs[...]
        m, l, a = m_s[...], l_s[...], acc[...]
        s_cur = _dot(k_ref[0:sub, :], qsv)
        for t in range(nsub):
            if t + 1 < nsub:
                s_next = _dot(k_ref[sub * (t + 1):sub * (t + 2), :], qsv)
            m_new = jnp.maximum(m, jnp.max(s_cur, axis=0, keepdims=True))
            alpha = jnp.exp(m - m_new)
            p = jnp.exp(s_cur - m_new)
            l = alpha * l + jnp.sum(p, axis=0, keepdims=True)
            a = alpha * a + _dot(vT_ref[:, sub * t:sub * (t + 1)], p.astype(BF16))
            m = m_new
            if t + 1 < nsub:
                s_cur = s_next
        m_s[...], l_s[...], acc[...] = m, l, a

        @pl.when(j == nkv - 1)
        def _():
            l_f = l_s[...]
            o_ref[...] = _pick_halves_T(acc[...] / l_f, rlo, tq)
            lse_ref[0, 0] = m_s[...] + jnp.log(l_f)

        side_end()

    s_args, s_in, s_out, s_shapes, s_sems = _side_specs(side)
    return pl.pallas_call(
        body, name=name, grid=(nb, nq, nkv),
        in_specs=[pl.BlockSpec((tq, LANES), lambda b, i, j: (i, b)), pl.BlockSpec((tk, LANES), lambda b, i, j: (j, b // kdiv)),
                  pl.BlockSpec((LANES, tk), lambda b, i, j: (b // kdiv, j))] + s_in,
        out_specs=[pl.BlockSpec((tq, LANES), lambda b, i, j: (i, b)),
                   pl.BlockSpec((1, 1, 1, 2 * tq), lambda b, i, j: (b, i, 0, 0))] + s_out,
        out_shape=[_sds((S, nb * LANES), F32), _sds((nb, nq, 1, 2 * tq), F32)] + s_shapes,
        scratch_shapes=[pltpu.VMEM((LANES, 2 * tq), BF16), pltpu.VMEM((1, 2 * tq), F32), pltpu.VMEM((1, 2 * tq), F32),
                        pltpu.VMEM((LANES, 2 * tq), F32)] + s_sems,
        compiler_params=_params(("arbitrary",) * 3))(q, k, vT, *s_args)


def _pp_bwd(q, k, kT, v, o, do, lse, *, kdiv, tq, tk, sub, name, side=None):
    S = k.shape[0]; nb = q.shape[1] // LANES; nkb = k.shape[1] // LANES; nq = S // tq; nkv = S // tk; nsub = tk // sub

    def body(*refs):
        ((q_ref, k_ref, kT_ref, v_ref, o_ref, do_ref, lse_ref), side_ins, (dq_ref, dk_ref, dv_ref), side_outs,
         (qsT, qs, dosT, dos, delta_s, dq_acc), side_sems) = _side_split(refs, 7, 3, 6, side)
        b, i, j = pl.program_id(0), pl.program_id(1), pl.program_id(2)
        rlo = _row_lo()
        lo = lax.broadcasted_iota(jnp.int32, (1, LANES), 1) < HEAD_DIM
        side_end = _side_hooks(side, side_ins, side_outs, side_sems, (b * nq + i) * nkv + j, nb * nq * nkv)

        @pl.when((b % kdiv == 0) & (i == 0) & (j == 0))
        def _():
            dk_ref[...] = jnp.zeros((S, LANES), F32)
            dv_ref[...] = jnp.zeros((S, LANES), F32)

        @pl.when(j == 0)
        def _():
            qv = q_ref[...]
            qs[...] = _stack_rows(qv, lo)
            qsT[...] = _stack_cols(qv.astype(F32).T, rlo).astype(BF16)
            dov = do_ref[...]
            dos[...] = _stack_rows(dov.astype(BF16), lo)
            dosT[...] = _stack_cols(dov.T, rlo).astype(BF16)
            prodT = (dov * o_ref[...]).T
            delta_s[...] = jnp.concatenate([jnp.sum(jnp.where(rlo, prodT, 0.0), axis=0, keepdims=True),
                                            jnp.sum(jnp.where(rlo, 0.0, prodT), axis=0, keepdims=True)], axis=1)
            dq_acc[...] = jnp.zeros((LANES, 2 * tq), F32)

        qsTv, dosTv, qsv, dosv = qsT[...], dosT[...], qs[...], dos[...]
        lse_v, delta_v = lse_ref[0, 0], delta_s[...]
        dqa = dq_acc[...]
        s_cur = _dot(k_ref[0:sub, :], qsTv)
        dp_cur = _dot(v_ref[0:sub, :], dosTv)
        for t in range(nsub):
            if t + 1 < nsub:
                s_next = _dot(k_ref[sub * (t + 1):sub * (t + 2), :], qsTv)
                dp_next = _dot(v_ref[sub * (t + 1):sub * (t + 2), :], dosTv)
            p = jnp.exp(s_cur - lse_v)
            ds = (p * (dp_cur - delta_v)).astype(BF16)
            rows = pl.ds(pl.multiple_of(j * tk + sub * t, sub), sub)
            dv_ref[rows, :] += _dot(p.astype(BF16), dosv)
            dk_ref[rows, :] += _dot(ds, qsv)
            dqa = dqa + _dot(kT_ref[:, sub * t:sub * (t + 1)], ds)
            if t + 1 < nsub:
                s_cur, dp_cur = s_next, dp_next
        dq_acc[...] = dqa

        @pl.when(j == nkv - 1)
        def _():
            dq_ref[...] = _pick_halves_T(dq_acc[...], rlo, tq)

        side_end()

    qmap = lambda b, i, j: (i, b)
    kmap = lambda b, i, j: (j, b // kdiv)
    res = lambda b, i, j: (0, b // kdiv)
    s_args, s_in, s_out, s_shapes, s_sems = _side_specs(side)
    return pl.pallas_call(
        body, name=name, grid=(nb, nq, nkv),
        in_specs=[pl.BlockSpec((tq, LANES), qmap), pl.BlockSpec((tk, LANES), kmap), pl.BlockSpec((LANES, tk), lambda b, i, j: (b // kdiv, j)),
                  pl.BlockSpec((tk, LANES), kmap), pl.BlockSpec((tq, LANES), qmap), pl.BlockSpec((tq, LANES), qmap),
                  pl.BlockSpec((1, 1, 1, 2 * tq), lambda b, i, j: (b, i, 0, 0))] + s_in,
        out_specs=[pl.BlockSpec((tq, LANES), qmap), pl.BlockSpec((S, LANES), res), pl.BlockSpec((S, LANES), res)] + s_out,
        out_shape=[_sds((S, nb * LANES), F32), _sds((S, nkb * LANES), F32), _sds((S, nkb * LANES), F32)] + s_shapes,
        scratch_shapes=[pltpu.VMEM((LANES, 2 * tq), BF16), pltpu.VMEM((2 * tq, LANES), BF16), pltpu.VMEM((LANES, 2 * tq), BF16),
                        pltpu.VMEM((2 * tq, LANES), BF16), pltpu.VMEM((1, 2 * tq), F32), pltpu.VMEM((LANES, 2 * tq), F32)] + s_sems,
        compiler_params=_params(("arbitrary",) * 3))(q, k, kT, v, o, do, lse, *s_args)


MLA_C = MLA_SCALE * LOG2E


def _mla_fwd(q, kcat, kcatT, *, tq, tk, sub):
    S = kcat.shape[0]; nq, nkv = S // tq, S // tk; R = B_HEADS * tq; nsub = tk // sub

    def body(q_ref, k_ref, vT_ref, o_ref, lse_ref, qT, m_s, l_s, acc):
        j = pl.program_id(1)

        @pl.when(j == 0)
        def _():
            qT[...] = q_ref[...].reshape(R, 2 * LANES).astype(F32).T.astype(BF16)
            m_s[...] = jnp.full((1, R), NEG, F32)
            l_s[...] = jnp.zeros((1, R), F32)
            acc[...] = jnp.zeros((LANES, R), F32)

        qTv = qT[...]
        m, l, a = m_s[...], l_s[...], acc[...]
        s_cur = _dot(k_ref[0:sub, :], qTv)
        for t in range(nsub):
            if t + 1 < nsub:
                s_next = _dot(k_ref[sub * (t + 1):sub * (t + 2), :], qTv)
            m_new = jnp.maximum(m, jnp.max(s_cur, axis=0, keepdims=True))
            alpha = jnp.exp2((m - m_new) * MLA_C)
            p = jnp.exp2((s_cur - m_new) * MLA_C)
            l = alpha * l + jnp.sum(p, axis=0, keepdims=True)
            a = alpha * a + _dot(vT_ref[:, sub * t:sub * (t + 1)], p.astype(BF16))
            m = m_new
            if t + 1 < nsub:
                s_cur = s_next
        m_s[...], l_s[...], acc[...] = m, l, a

        @pl.when(j == nkv - 1)
        def _():
            l_f = l_s[...]
            o_ref[...] = (acc[...] / l_f).T.reshape(B_HEADS, tq, LANES)
            lse_ref[0] = m_s[...] * MLA_SCALE + jnp.log(l_f)

    return pl.pallas_call(
        body, name="mla_fwd", grid=(nq, nkv),
        in_specs=[pl.BlockSpec((B_HEADS, tq, 2 * LANES), lambda i, j: (0, i, 0)), pl.BlockSpec((tk, 2 * LANES), lambda i, j: (j, 0)),
                  pl.BlockSpec((LANES, tk), lambda i, j: (0, j))],
        out_specs=[pl.BlockSpec((B_HEADS, tq, LANES), lambda i, j: (0, i, 0)), pl.BlockSpec((1, 1, R), lambda i, j: (i, 0, 0))],
        out_shape=[_sds((B_HEADS, S, LANES), F32), _sds((nq, 1, R), F32)],
        scratch_shapes=[pltpu.VMEM((2 * LANES, R), BF16), pltpu.VMEM((1, R), F32), pltpu.VMEM((1, R), F32), pltpu.VMEM((LANES, R), F32)],
        compiler_params=_params(("arbitrary", "arbitrary")))(q, kcat, kcatT)


def _mla_bwd(q, kcat, kcatT, o, do, lse, *, tq, tk, sub):
    S = kcat.shape[0]; nq, nkv = S // tq, S // tk; R = B_HEADS * tq; nsub = tk // sub

    def body(q_ref, k_ref, kT_ref, o_ref, do_ref, lse_ref, dq_ref, dk_ref, qT, dosT, dos, delta_s, dq_acc):
        i, j = pl.program_id(0), pl.program_id(1)

        @pl.when((i == 0) & (j == 0))
        def _():
            dk_ref[...] = jnp.zeros((S, 2 * LANES), F32)

        @pl.when(j == 0)
        def _():
            qT[...] = q_ref[...].reshape(R, 2 * LANES).astype(F32).T.astype(BF16)
            dov = do_ref[...].reshape(R, LANES)
            dos[...] = dov.astype(BF16)
            dosT[...] = dov.T.astype(BF16)
            delta_s[...] = jnp.sum((dov * o_ref[...].reshape(R, LANES)).T, axis=0, keepdims=True)
            dq_acc[...] = jnp.zeros((2 * LANES, R), F32)

        qTv, dosTv, dosv = qT[...], dosT[...], dos[...]
        qv = q_ref[...].reshape(R, 2 * LANES)
        lse_v, delta_v = lse_ref[0] * LOG2E, delta_s[...]
        dqa = dq_acc[...]
        s_cur = _dot(k_ref[0:sub, :], qTv)
        dp_cur = _dot(k_ref[0:sub, 0:LANES], dosTv)
        for t in range(nsub):
            if t + 1 < nsub:
                s_next = _dot(k_ref[sub * (t + 1):sub * (t + 2), :], qTv)
                dp_next = _dot(k_ref[sub * (t + 1):sub * (t + 2), 0:LANES], dosTv)
            p = jnp.exp2(s_cur * MLA_C - lse_v)
            ds = (p * (dp_cur - delta_v) * MLA_SCALE).astype(BF16)
            rows = pl.ds(pl.multiple_of(j * tk + sub * t, sub), sub)
            dk_ref[rows, :] += _dot(ds, qv)
            dk_ref[rows, 0:LANES] += _dot(p.astype(BF16), dosv)
            dqa = dqa + _dot(kT_ref[:, sub * t:sub * (t + 1)], ds)
            if t + 1 < nsub:
                s_cur, dp_cur = s_next, dp_next
        dq_acc[...] = dqa

        @pl.when(j == nkv - 1)
        def _():
            dq_ref[...] = dq_acc[...].T.reshape(B_HEADS, tq, 2 * LANES)

    hspec = lambda w: pl.BlockSpec((B_HEADS, tq, w), lambda i, j: (0, i, 0))
    return pl.pallas_call(
        body, name="mla_bwd", grid=(nq, nkv),
        in_specs=[hspec(2 * LANES), pl.BlockSpec((tk, 2 * LANES), lambda i, j: (j, 0)), pl.BlockSpec((2 * LANES, tk), lambda i, j: (0, j)),
                  hspec(LANES), hspec(LANES), pl.BlockSpec((1, 1, R), lambda i, j: (i, 0, 0))],
        out_specs=[hspec(2 * LANES), pl.BlockSpec((S, 2 * LANES), lambda i, j: (0, 0))],
        out_shape=[_sds((B_HEADS, S, 2 * LANES), F32), _sds((S, 2 * LANES), F32)],
        scratch_shapes=[pltpu.VMEM((2 * LANES, R), BF16), pltpu.VMEM((LANES, R), BF16), pltpu.VMEM((R, LANES), BF16),
                        pltpu.VMEM((1, R), F32), pltpu.VMEM((2 * LANES, R), F32)],
        compiler_params=_params(("arbitrary", "arbitrary")))(q, kcat, kcatT, o, do, lse)


def _win_start(i, tq, nk, S):
    return pl.multiple_of(jnp.clip(i * tq - WINDOW, 0, S - nk), LANES)


def _win_dist_table(S, tq):
    nk = min(tq + 2 * WINDOW, S)
    nq = S // tq
    r = np.arange(nk)[:, None]
    c = (np.arange(2 * tq) % tq)[None, :]
    tabs = []
    for rel in (0, WINDOW, (nq - 1) * tq - (S - nk)):
        dist = np.abs(rel + c - r).astype(np.float32)
        tabs.append(np.where(dist <= WINDOW, dist, np.float32(1e32)))
    return jnp.asarray(np.stack(tabs))


def _win_dist_spec(nk, tq, nq):
    return pl.BlockSpec((1, nk, 2 * tq), lambda b, i: (jnp.where(i == 0, 0, jnp.where(i == nq - 1, 2, 1)), 0, 0))


def _win_fwd(q, k, vT, dist, slope, sink, *, kdiv, tq, nbs, name):
    S = k.shape[0]; nb = q.shape[1] // LANES; nq = S // tq; nk = min(tq + 2 * WINDOW, S)
    assert nb % nbs == 0 and nbs % kdiv == 0
    kvw = (nbs // kdiv) * LANES

    def body(q_ref, k_ref, vT_ref, dist_ref, slope_ref, sink_ref, o_ref, lse_ref):
        i = pl.program_id(1)
        rlo = _row_lo()
        k0 = _win_start(i, tq, nk, S)
        kk, vv, dd = k_ref[pl.ds(k0, nk), :], vT_ref[:, pl.ds(k0, nk)], dist_ref[0]
        for u in range(nbs):
            kv = slice(LANES * (u // kdiv), LANES * (u // kdiv + 1))
            qsT = _stack_cols(q_ref[:, LANES * u:LANES * (u + 1)].astype(F32).T, rlo).astype(BF16)
            s = _dot(kk[:, kv], qsT) - slope_ref[u] * dd
            sk = sink_ref[u]
            m = jnp.maximum(jnp.max(s, axis=0, keepdims=True), sk)
            p = jnp.exp(s - m)
            l = jnp.sum(p, axis=0, keepdims=True) + jnp.exp(sk - m)
            o_ref[:, LANES * u:LANES * (u + 1)] = _pick_halves_T(_dot(vv[kv, :], p.astype(BF16)) / l, rlo, tq)
            lse_ref[u, 0] = m + jnp.log(l)

    row_spec = pl.BlockSpec((nbs, 1, 2 * tq), lambda b, i: (b, 0, 0))
    return pl.pallas_call(
        body, name=name, grid=(nb // nbs, nq),
        in_specs=[pl.BlockSpec((tq, nbs * LANES), lambda b, i: (i, b)), pl.BlockSpec((S, kvw), lambda b, i: (0, b)),
                  pl.BlockSpec((kvw, S), lambda b, i: (b, 0)), _win_dist_spec(nk, tq, nq), row_spec, row_spec],
        out_specs=[pl.BlockSpec((tq, nbs * LANES), lambda b, i: (i, b)), pl.BlockSpec((nbs, 1, 1, 2 * tq), lambda b, i: (b, i, 0, 0))],
        out_shape=[_sds((S, nb * LANES), F32), _sds((nb, nq, 1, 2 * tq), F32)],
        compiler_params=_params(("arbitrary", "arbitrary")))(q, k, vT, dist, slope, sink)


def _win_bwd(q, k, kT, v, o, do, lse, dist, slope, sink, *, kdiv, tq, nbs, name):
    S = k.shape[0]; nb = q.shape[1] // LANES; nkb = k.shape[1] // LANES; nq = S // tq; nk = min(tq + 2 * WINDOW, S)
    assert nb % nbs == 0 and nbs % kdiv == 0
    nkv = nbs // kdiv
    kvw = nkv * LANES

    def body(q_ref, k_ref, kT_ref, v_ref, o_ref, do_ref, lse_ref, dist_ref, slope_ref, sink_ref, dq_ref, dk_ref, dv_ref, dsink_ref, ds_acc):
        i = pl.program_id(1)
        rlo = _row_lo()
        lo = lax.broadcasted_iota(jnp.int32, (1, LANES), 1) < HEAD_DIM

        @pl.when(i == 0)
        def _():
            dk_ref[...] = jnp.zeros((S, kvw), F32)
            dv_ref[...] = jnp.zeros((S, kvw), F32)
            ds_acc[...] = jnp.zeros((nbs, 2 * tq), F32)

        k0 = _win_start(i, tq, nk, S)
        rows = pl.ds(k0, nk)
        kk_all, vv_all, kkT_all, dd = k_ref[rows, :], v_ref[rows, :], kT_ref[:, rows], dist_ref[0]
        dv_sum, dk_sum = [None] * nkv, [None] * nkv
        for u in range(nbs):
            g = u // kdiv
            kv = slice(LANES * g, LANES * (g + 1))
            kk, vv, kkT = kk_all[:, kv], vv_all[:, kv], kkT_all[kv, :]
            cols = slice(LANES * u, LANES * (u + 1))
            qv = q_ref[:, cols]
            qs = _stack_rows(qv, lo)
            qsT = _stack_cols(qv.astype(F32).T, rlo).astype(BF16)
            dov = do_ref[:, cols]
            dos = _stack_rows(dov.astype(BF16), lo)
            dosT = _stack_cols(dov.T, rlo).astype(BF16)
            prodT = (dov * o_ref[:, cols]).T
            delta = jnp.concatenate([jnp.sum(jnp.where(rlo, prodT, 0.0), axis=0, keepdims=True),
                                     jnp.sum(jnp.where(rlo, 0.0, prodT), axis=0, keepdims=True)], axis=1)
            lse_v = lse_ref[u, 0]
            ds_acc[u:u + 1, :] += -jnp.exp(sink_ref[u] - lse_v) * delta
            p = jnp.exp(_dot(kk, qsT) - slope_ref[u] * dd - lse_v)
            ds = (p * (_dot(vv, dosT) - delta)).astype(BF16)
            dv_u, dk_u = _dot(p.astype(BF16), dos), _dot(ds, qs)
            dv_sum[g] = dv_u if dv_sum[g] is None else dv_sum[g] + dv_u
            dk_sum[g] = dk_u if dk_sum[g] is None else dk_sum[g] + dk_u
            dq_ref[:, cols] = (_pick_halves_T(_dot(kkT, ds), rlo, tq) * 0.125).astype(BF16)
        dv_ref[rows, :] += jnp.concatenate(dv_sum, axis=1)
        dk_ref[rows, :] += jnp.concatenate(dk_sum, axis=1)

        @pl.when(i == nq - 1)
        def _():
            acc = ds_acc[...]
            for u in range(nbs):
                dsink_ref[u] = jnp.concatenate(
                    [jnp.broadcast_to(jnp.sum(acc[u:u + 1, 0:tq], axis=1, keepdims=True), (1, LANES)),
                     jnp.broadcast_to(jnp.sum(acc[u:u + 1, tq:2 * tq], axis=1, keepdims=True), (1, LANES)),
                     jnp.zeros((6, LANES), F32)], axis=0)

    qmap = lambda b, i: (i, b)
    kv_spec = pl.BlockSpec((S, kvw), lambda b, i: (0, b))
    row_spec = pl.BlockSpec((nbs, 1, 2 * tq), lambda b, i: (b, 0, 0))
    wide = pl.BlockSpec((tq, nbs * LANES), qmap)
    return pl.pallas_call(
        body, name=name, grid=(nb // nbs, nq),
        in_specs=[wide, kv_spec, pl.BlockSpec((kvw, S), lambda b, i: (b, 0)), kv_spec, wide, wide,
                  pl.BlockSpec((nbs, 1, 1, 2 * tq), lambda b, i: (b, i, 0, 0)), _win_dist_spec(nk, tq, nq), row_spec, row_spec],
        out_specs=[wide, kv_spec, kv_spec, pl.BlockSpec((nbs, 8, LANES), lambda b, i: (b, 0, 0))],
        out_shape=[_sds((S, nb * LANES), BF16), _sds((S, nkb * LANES), F32), _sds((S, nkb * LANES), F32), _sds((nb, 8, LANES), F32)],
        scratch_shapes=[pltpu.VMEM((nbs, 2 * tq), F32)],
        compiler_params=_params(("arbitrary", "arbitrary")))(q, k, kT, v, o, do, lse, dist, slope, sink)


def _sum_rows(v):
    return jnp.sum(v, axis=0, keepdims=True)


def _norm_mod_bwd(dh, xv, mod_ref, nw_ref, stats_ref):
    r = _rms(xv)
    xn = xv * r
    nw = nw_ref[...]
    stats_ref[0:1, :] += _sum_rows(dh)
    stats_ref[1:2, :] += _sum_rows(dh * (xn * nw))
    dn = dh * (1.0 + mod_ref[1:2, :])
    stats_ref[2:3, :] += _sum_rows(dn * xn)
    return _rms_bwd(xv, r, dn * nw)


def _even_gate_specs(ts):
    return [pl.BlockSpec((ts, 256), lambda i, c=c: (i, c)) for c in (3, 4, 7, 8)]


def _even_post_fwd(oa, olat, proj, x, gate, wuv, woe):
    S = x.shape[0]
    ts = min(ROW_TILE, S)

    def body(oa_ref, ol_ref, ga0_ref, ga1_ref, gb0_ref, gb1_ref, x_ref, gate_ref, wuv_ref, woe_ref, y_ref, x1_ref):
        sa, _ = _silu_and_grad(jnp.concatenate([ga0_ref[...], ga1_ref[...]], axis=1))
        sb, _ = _silu_and_grad(jnp.concatenate([gb0_ref[...], gb1_ref[...]], axis=1))
        olc = jnp.concatenate([ol_ref[hh] for hh in range(B_HEADS)], axis=1).astype(BF16)
        ob = _dot(olc, wuv_ref[...])
        mix = jnp.concatenate([oa_ref[...] * sa, ob * sb], axis=1).astype(BF16)
        y = _dot(mix, woe_ref[...])
        y_ref[...] = y
        x1_ref[...] = x_ref[...] + gate_ref[...] * y

    return pl.pallas_call(
        body, name="even_post_fwd", grid=(S // ts,),
        in_specs=[_row_spec(ts, 512), pl.BlockSpec((B_HEADS, ts, LANES), lambda i: (0, i, 0))] + _even_gate_specs(ts) +
                 [_row_spec(ts, D_MODEL), _full_spec((1, D_MODEL)), _full_spec((1024, 512)), _full_spec((1024, D_MODEL))],
        out_specs=[_row_spec(ts, D_MODEL), _row_spec(ts, D_MODEL)],
        out_shape=[_sds((S, D_MODEL), F32), _sds((S, D_MODEL), F32)],
        compiler_params=_params(("arbitrary",)),
    )(oa, olat, proj, proj, proj, proj, x, gate, wuv, woe)


def _odd_pre_fwd(x, mod, nw, wio):
    S = x.shape[0]
    ts = min(ROW_TILE, S)

    def body(x_ref, mod_ref, nw_ref, wio_ref, h_ref, proj_ref, q_ref, k_ref, v_ref, kt_ref, vt_ref):
        xv = x_ref[...]
        h = (xv * _rms(xv) * nw_ref[...]) * (1.0 + mod_ref[1:2, :]) + mod_ref[0:1, :]
        hb = h.astype(BF16)
        h_ref[...] = hb
        proj = jnp.concatenate([_dot(hb, wio_ref[p]) for p in range(N_CHIPS)], axis=1)
        proj_ref[...] = proj
        q_ref[...] = (proj[:, 0:1024] * 0.125).astype(BF16)
        lane = _lane_iota()
        k_v = jnp.concatenate([_dup_heads(proj[:, 1024 + LANES * j:1024 + LANES * (j + 1)], lane) for j in range(2)], axis=1)
        v_v = jnp.concatenate([_dup_heads(proj[:, 1280 + LANES * j:1280 + LANES * (j + 1)], lane) for j in range(2)], axis=1)
        k_ref[...] = k_v.astype(BF16)
        v_ref[...] = v_v.astype(BF16)
        kt_ref[...] = k_v.T.astype(BF16)
        vt_ref[...] = v_v.T.astype(BF16)

    col_spec = pl.BlockSpec((512, ts), lambda i: (0, i))
    return pl.pallas_call(
        body, name="odd_pre_fwd", grid=(S // ts,),
        in_specs=[_row_spec(ts, D_MODEL), _full_spec((3, D_MODEL)), _full_spec((1, D_MODEL)),
                  _full_spec((N_CHIPS, D_MODEL, ODD_IN // N_CHIPS))],
        out_specs=[_row_spec(ts, D_MODEL), _row_spec(ts, ODD_IN), _row_spec(ts, 1024), _row_spec(ts, 512), _row_spec(ts, 512),
                   col_spec, col_spec],
        out_shape=[_sds((S, D_MODEL), BF16), _sds((S, ODD_IN), F32), _sds((S, 1024), BF16), _sds((S, 512), BF16),
                   _sds((S, 512), BF16), _sds((512, S), BF16), _sds((512, S), BF16)],
        compiler_params=_params(("arbitrary",)),
    )(x, mod, nw, wio)


def _odd_post(oc, proj, x1, gate, woo, fw, tgt):
    S = x1.shape[0]
    ts = min(ROW_TILE, S)
    nsteps = S // ts

    def body(oc_ref, g0_ref, g1_ref, x_ref, gate_ref, woo_ref, fw_ref, tgt_ref, doc_ref, dgc_ref, dx2_ref, dwoo_out, stats_ref,
             dwoo_ref):
        @pl.when(pl.program_id(0) == 0)
        def _():
            dwoo_ref[...] = jnp.zeros((D_MODEL, D_MODEL), F32)
            stats_ref[...] = jnp.zeros((8, D_MODEL), F32)

        ocv = oc_ref[...]
        sg, dsg = _silu_and_grad(jnp.concatenate([g0_ref[...], g1_ref[...]], axis=1))
        mix = (ocv * sg).astype(BF16)
        woo_v = woo_ref[...]
        y = _dot(mix, woo_v)
        gate_v = gate_ref[...]
        x2 = x_ref[...] + gate_v * y
        r = _rms(x2)
        fw_v = fw_ref[...]
        xn = x2 * r
        err = xn * fw_v - tgt_ref[...]
        dout = err * (1.0 / D_MODEL)
        dx2 = _rms_bwd(x2, r, dout * fw_v)
        dx2_ref[...] = dx2
        stats_ref[0:1, :] += _sum_rows(dout * xn)
        stats_ref[1:2, :] += _sum_rows(dx2 * y)
        loss_t = 0.5 * jnp.sum(_sum_rows(err * dout), axis=-1, keepdims=True)
        stats_ref[2:3, :] += jnp.broadcast_to(loss_t, (1, D_MODEL))
        dy = (gate_v * dx2).astype(BF16)
        dmix = _dot_nt(dy, woo_v)
        dwoo_ref[...] += _dot_tn(mix, dy)
        doc_ref[...] = dmix * sg
        dgc_ref[...] = (dmix * ocv * dsg).astype(BF16)

        @pl.when(pl.program_id(0) == nsteps - 1)
        def _():
            dwoo_out[...] = dwoo_ref[...].astype(BF16)

    gate_cols = [pl.BlockSpec((ts, 512), lambda i, c=c: (i, c)) for c in (3, 4)]
    return pl.pallas_call(
        body, name="odd_post", grid=(nsteps,),
        in_specs=[_row_spec(ts, D_MODEL)] + gate_cols + [_row_spec(ts, D_MODEL), _full_spec((1, D_MODEL)),
                  _full_spec((D_MODEL, D_MODEL)), _full_spec((1, D_MODEL)), _row_spec(ts, D_MODEL)],
        out_specs=[_row_spec(ts, D_MODEL), _row_spec(ts, D_MODEL), _row_spec(ts, D_MODEL),
                   _full_spec((D_MODEL, D_MODEL), single=False), _full_spec((8, D_MODEL), single=False)],
        out_shape=[_sds((S, D_MODEL), F32), _sds((S, D_MODEL), BF16), _sds((S, D_MODEL), F32), _sds((D_MODEL, D_MODEL), BF16),
                   _sds((8, D_MODEL), F32)],
        scratch_shapes=[pltpu.VMEM((D_MODEL, D_MODEL), F32)],
        compiler_params=_params(("arbitrary",)),
    )(oc, proj, proj, x1, gate, woo, fw, tgt)


def _odd_pre_bwd(dq, dk, dv, dgc, h, x, dx_res, mod, nw, wio):
    S = x.shape[0]
    ts = min(IN_PROJ_ROW_TILE, S)
    nsteps = S // ts
    wsh = ODD_IN // N_CHIPS

    def body(dq_ref, dk_ref, dv_ref, dgc_ref, h_ref, x_ref, dxr_ref, mod_ref, nw_ref, wio_ref, dx_ref, dw_ref, stats_ref, dw_acc):
        @pl.when(pl.program_id(0) == 0)
        def _():
            dw_acc[...] = jnp.zeros((N_CHIPS, D_MODEL, wsh), F32)
            stats_ref[...] = jnp.zeros((8, D_MODEL), F32)

        lane = _lane_iota()
        dkv = [_fold_heads(r[:, 2 * LANES * j:2 * LANES * (j + 1)], lane).astype(BF16) for r in (dk_ref, dv_ref) for j in range(2)]
        dproj = jnp.concatenate([dq_ref[...]] + dkv + [dgc_ref[...]], axis=1)
        hv = h_ref[...]
        dh = None
        for p in range(N_CHIPS):
            dp_cols = dproj[:, wsh * p:wsh * (p + 1)]
            part = _dot_nt(dp_cols, wio_ref[p])
            dh = part if dh is None else dh + part
            dw_acc[p] += _dot_tn(hv, dp_cols)
        dx_ref[...] = dxr_ref[...] + _norm_mod_bwd(dh, x_ref[...], mod_ref, nw_ref, stats_ref)

        @pl.when(pl.program_id(0) == nsteps - 1)
        def _():
            dw_ref[...] = dw_acc[...].astype(BF16)

    return pl.pallas_call(
        body, name="odd_pre_bwd", grid=(nsteps,),
        in_specs=[_row_spec(ts, 1024), _row_spec(ts, 512), _row_spec(ts, 512), _row_spec(ts, 1024), _row_spec(ts, D_MODEL),
                  _row_spec(ts, D_MODEL), _row_spec(ts, D_MODEL), _full_spec((3, D_MODEL)), _full_spec((1, D_MODEL)),
                  _full_spec((N_CHIPS, D_MODEL, wsh))],
        out_specs=[_row_spec(ts, D_MODEL), _full_spec((N_CHIPS, D_MODEL, wsh), single=False), _full_spec((8, D_MODEL), single=False)],
        out_shape=[_sds((S, D_MODEL), F32), _sds((N_CHIPS, D_MODEL, wsh), BF16), _sds((8, D_MODEL), F32)],
        scratch_shapes=[pltpu.VMEM((N_CHIPS, D_MODEL, wsh), F32)],
        compiler_params=_params(("arbitrary",)),
    )(dq, dk, dv, dgc, h, x, dx_res, mod, nw, wio)


def _even_post_bwd(dx1, y, oa, olat, proj, gate, wuv, woe):
    S = dx1.shape[0]
    ts = min(ROW_TILE, S)
    nsteps = S // ts

    def body(dx_ref, y_ref, oa_ref, ol_ref, ga0_ref, ga1_ref, gb0_ref, gb1_ref, gate_ref, wuv_ref, woe_ref,
             doa_ref, dga_ref, dgb_ref, dol_ref, dwoe_out, dwuv_ref, stats_ref, dwoe_ref):
        @pl.when(pl.program_id(0) == 0)
        def _():
            dwoe_ref[...] = jnp.zeros((D_MODEL, D_MODEL), F32)
            dwuv_ref[...] = jnp.zeros((1024, 512), F32)
            stats_ref[...] = jnp.zeros((8, D_MODEL), F32)

        dxv = dx_ref[...]
        stats_ref[0:1, :] += _sum_rows(dxv * y_ref[...])
        dy = (gate_ref[...] * dxv).astype(BF16)
        sa, dsa = _silu_and_grad(jnp.concatenate([ga0_ref[...], ga1_ref[...]], axis=1))
        sb, dsb = _silu_and_grad(jnp.concatenate([gb0_ref[...], gb1_ref[...]], axis=1))
        olc = jnp.concatenate([ol_ref[hh] for hh in range(B_HEADS)], axis=1).astype(BF16)
        wuv_v = wuv_ref[...]
        ob = _dot(olc, wuv_v)
        oav = oa_ref[...]
        mix = jnp.concatenate([oav * sa, ob * sb], axis=1).astype(BF16)
        dmix = _dot_nt(dy, woe_ref[...])
        dwoe_ref[...] += _dot_tn(mix, dy)
        dma, dmb = dmix[:, 0:512], dmix[:, 512:1024]
        doa_ref[...] = dma * sa
        dga_ref[...] = (dma * oav * dsa).astype(BF16)
        dgb_ref[...] = (dmb * ob * dsb).astype(BF16)
        dob = (dmb * sb).astype(BF16)
        dol = _dot_nt(dob, wuv_v)
        dwuv_ref[...] += _dot_tn(olc, dob)
        for hh in range(B_HEADS):
            dol_ref[hh] = dol[:, LANES * hh:LANES * (hh + 1)]

        @pl.when(pl.program_id(0) == nsteps - 1)
        def _():
            dwoe_out[...] = dwoe_ref[...].astype(BF16)

    head_spec = pl.BlockSpec((B_HEADS, ts, LANES), lambda i: (0, i, 0))
    return pl.pallas_call(
        body, name="even_post_bwd", grid=(nsteps,),
        in_specs=[_row_spec(ts, D_MODEL), _row_spec(ts, D_MODEL), _row_spec(ts, 512), head_spec] + _even_gate_specs(ts) +
                 [_full_spec((1, D_MODEL)), _full_spec((1024, 512)), _full_spec((1024, D_MODEL))],
        out_specs=[_row_spec(ts, 512), _row_spec(ts, 512), _row_spec(ts, 512), head_spec,
                   _full_spec((D_MODEL, D_MODEL), single=False), _full_spec((1024, 512), single=False),
                   _full_spec((8, D_MODEL), single=False)],
        out_shape=[_sds((S, 512), F32), _sds((S, 512), BF16), _sds((S, 512), BF16), _sds((B_HEADS, S, LANES), F32),
                   _sds((D_MODEL, D_MODEL), BF16), _sds((1024, 512), F32), _sds((8, D_MODEL), F32)],
        scratch_shapes=[pltpu.VMEM((D_MODEL, D_MODEL), F32)],
        compiler_params=_params(("arbitrary",)),
    )(dx1, y, oa, olat, proj, proj, proj, proj, gate, wuv, woe)


def _even_pre_bwd(x, h, proj, dqa, dka, dva, dga, dgb, dqcat, dkcat, dx_res, mod, nw, wie, qn, kn, seg, ca, sa, ct, st,
                  qln, kvln, wuq, wuk):
    S = x.shape[0]
    ts = min(IN_PROJ_ROW_TILE, S)
    nsteps = S // ts

    def body(x_ref, h_ref, proj_ref, dqa_ref, dka_ref, dva_ref, dga_ref, dgb_ref, dqc_ref, dkc_ref, dxr_ref, mod_ref, nw_ref,
             wie_ref, qn_ref, kn_ref, seg_ref, ca_ref, sa_ref, ct_ref, st_ref, qln_ref, kvln_ref, wuq_ref, wuk_ref,
             dx_ref, dwie_out, dwuq_out, dwuk_out, stats_ref, nstats_ref, dwie_ref, dwuq_ref, dwuk_ref):
        @pl.when(pl.program_id(0) == 0)
        def _():
            dwie_ref[...] = jnp.zeros((D_MODEL, EVEN_P), F32)
            dwuq_ref[...] = jnp.zeros((B_Q_LORA, 1536), F32)
            dwuk_ref[...] = jnp.zeros((512, 1024), F32)
            stats_ref[...] = jnp.zeros((8, D_MODEL), F32)
            nstats_ref[...] = jnp.zeros((8, 256), F32)

        lane = _lane_iota()
        ca_v, sa_v, ct_v, st_v = ca_ref[...], sa_ref[...], ct_ref[...], st_ref[...]
        seg_v = seg_ref[...]

        def head_norm_bwd(xc, dy, w):
            r = lax.rsqrt(_seg_mean(xc * xc, seg_v) + EPS)
            g = dy * w
            dxc = r * g - xc * (r * r * r) * _seg_mean(xc * g, seg_v)
            return dxc, _sum_rows(dy * (xc * r))

        pieces = []
        dqn = jnp.zeros((1, LANES), F32)
        for cb in range(4):
            sl = slice(LANES * cb, LANES * (cb + 1))
            dy = _rot_bwd(dqa_ref[:, sl] * 0.125, ca_v, sa_v, lane)
            dxc, dw = head_norm_bwd(proj_ref[:, sl], dy, qn_ref[...])
            pieces.append(dxc)
            dqn = dqn + dw
        dxc, dkn = head_norm_bwd(proj_ref[:, 512:640], _rot_bwd(_fold_heads(dka_ref[...], lane), ca_v, sa_v, lane), kn_ref[...])
        pieces += [dxc, _fold_heads(dva_ref[...], lane), dga_ref[...]]
        nstats_ref[0:1, 0:LANES] += dqn + pltpu.roll(dqn, HEAD_DIM, 1)
        nstats_ref[1:2, 0:LANES] += dkn + pltpu.roll(dkn, HEAD_DIM, 1)

        cq = proj_ref[:, 1280:1536]
        rq = _rms(cq)
        cqn_f = cq * rq
        qln_v = qln_ref[...]
        cqn = (cqn_f * qln_v).astype(BF16)
        wuq_v, wuk_v = wuq_ref[...], wuk_ref[...]
        qnope = _dot(cqn, wuq_v[:, 0:512]).astype(BF16)
        dqlat = jnp.concatenate([dqc_ref[hh, :, 0:LANES] for hh in range(B_HEADS)], axis=1).astype(BF16)
        dqnope = _dot_nt(dqlat, wuk_v)
        dwuk_ref[...] += _dot_tn(qnope, dqlat)
        dqr = [_rot_bwd(dqc_ref[hh, :, LANES:2 * LANES], ct_v, st_v, lane) for hh in range(B_HEADS)]
        dqb = jnp.concatenate([dqnope] + dqr, axis=1).astype(BF16)
        dcqn = _dot_nt(dqb, wuq_v)
        dwuq_ref[...] += _dot_tn(cqn, dqb)
        nstats_ref[2:3, :] += _sum_rows(dcqn * cqn_f)
        dcq = _rms_bwd(cq, rq, dcqn * qln_v)
        ckv = proj_ref[:, 1536:1664]
        rk = _rms(ckv)
        dckvn = dkc_ref[:, 0:LANES]
        nstats_ref[3:4, 0:LANES] += _sum_rows(dckvn * (ckv * rk))
        dckv = _rms_bwd(ckv, rk, dckvn * kvln_ref[...])
        dkr = _rot_bwd(dkc_ref[:, LANES:2 * LANES], ct_v, st_v, lane)
        pieces += [dcq, dckv, dkr, dgb_ref[...]]
        dproj = jnp.concatenate([piece.astype(BF16) for piece in pieces], axis=1)
        dh = _dot_nt(dproj, wie_ref[...])
        dwie_ref[...] += _dot_tn(h_ref[...], dproj)
        dx_ref[...] = dxr_ref[...] + _norm_mod_bwd(dh, x_ref[...], mod_ref, nw_ref, stats_ref)

        @pl.when(pl.program_id(0) == nsteps - 1)
        def _():
            pltpu.sync_copy(dwie_ref, dwie_out)
            pltpu.sync_copy(dwuq_ref, dwuq_out)
            pltpu.sync_copy(dwuk_ref, dwuk_out)

    return pl.pallas_call(
        body, name="even_pre_bwd", grid=(nsteps,),
        in_specs=[_row_spec(ts, D_MODEL), _row_spec(ts, D_MODEL), _row_spec(ts, EVEN_P), _row_spec(ts, 512), _row_spec(ts, 2 * LANES),
                  _row_spec(ts, 2 * LANES), _row_spec(ts, 512), _row_spec(ts, 512),
                  pl.BlockSpec((B_HEADS, ts, 2 * LANES), lambda i: (0, i, 0)), _row_spec(ts, 2 * LANES), _row_spec(ts, D_MODEL),
                  _full_spec((3, D_MODEL)), _full_spec((1, D_MODEL)), _full_spec((D_MODEL, EVEN_P)),
                  _full_spec((1, LANES)), _full_spec((1, LANES)), _full_spec((LANES, LANES)),
                  _row_spec(ts, LANES), _row_spec(ts, LANES), _row_spec(ts, LANES), _row_spec(ts, LANES),
                  _full_spec((1, B_Q_LORA)), _full_spec((1, B_KV_LORA)), _full_spec((B_Q_LORA, 1536)), _full_spec((512, 1024))],
        out_specs=[_row_spec(ts, D_MODEL), _ANY, _ANY, _ANY, _full_spec((8, D_MODEL), single=False), _full_spec((8, 256), single=False)],
        out_shape=[_sds((S, D_MODEL), F32), _sds((D_MODEL, EVEN_P), F32), _sds((B_Q_LORA, 1536), F32), _sds((512, 1024), F32),
                   _sds((8, D_MODEL), F32), _sds((8, 256), F32)],
        scratch_shapes=[pltpu.VMEM((D_MODEL, EVEN_P), F32), pltpu.VMEM((B_Q_LORA, 1536), F32), pltpu.VMEM((512, 1024), F32)],
        compiler_params=_params(("arbitrary",)),
    )(x, h, proj, dqa, dka, dva, dga, dgb, dqcat, dkcat, dx_res, mod, nw, wie, qn, kn, seg, ca, sa, ct, st, qln, kvln, wuq, wuk)


def _ada_fwd(c_all, w, b):
    n = w.shape[2]

    def body(c_ref, w_ref, b_ref, o_ref):
        cv = c_ref[...]
        o_ref[0] = _dot_f32(cv * _sigmoid(cv), w_ref[0]) + b_ref[0]

    return pl.pallas_call(
        body, name="ada_fwd", grid=(2,),
        in_specs=[pl.BlockSpec((N_DEV, D_MODEL), lambda l: (0, 0)), pl.BlockSpec((1, D_MODEL, n), lambda l: (l, 0, 0)),
                  pl.BlockSpec((1, 1, n), lambda l: (l, 0, 0))],
        out_specs=pl.BlockSpec((1, N_DEV, n), lambda l: (l, 0, 0)),
        out_shape=_sds((2, N_DEV, n), F32),
        compiler_params=_params(("arbitrary",)),
    )(c_all, w, b)


def _ada_bwd(c_all_t, dmod):
    n = dmod.shape[2]

    def body(c_ref, d_ref, o_ref):
        cv = c_ref[...]
        act = cv * _sigmoid(cv)
        dv = d_ref[0]
        acc = act[:, 0:1] * dv[0:1, :]
        for bb in range(1, N_DEV):
            acc = acc + act[:, bb:bb + 1] * dv[bb:bb + 1, :]
        o_ref[0] = acc

    return pl.pallas_call(
        body, name="ada_bwd", grid=(2,),
        in_specs=[pl.BlockSpec((D_MODEL, N_DEV), lambda l: (0, 0)), pl.BlockSpec((1, N_DEV, n), lambda l: (l, 0, 0))],
        out_specs=pl.BlockSpec((1, D_MODEL, n), lambda l: (l, 0, 0)),
        out_shape=_sds((2, D_MODEL, n), F32),
        compiler_params=_params(("arbitrary",)),
    )(c_all_t, dmod)


ADAM_ROW_TILE = 256


def _adam_update(g, w, m, v):
    m_new = ADAM_B1 * m + (1.0 - ADAM_B1) * g
    v_new = ADAM_B2 * v + (1.0 - ADAM_B2) * jnp.square(g)
    m_hat = m_new / (1.0 - ADAM_B1 ** ADAM_STEP)
    v_hat = v_new / (1.0 - ADAM_B2 ** ADAM_STEP)
    return -ADAM_LR * (m_hat / (jnp.sqrt(v_hat) + ADAM_EPS) + ADAM_WD * w), m_new, v_new


SMALL_ROWS = dict(dmod=(0, D_MODEL), norm_w=(6, D_MODEL), final_norm=(8, D_MODEL), a_q_norm=(9, HEAD_DIM), a_k_norm=(10, HEAD_DIM),
                  b_q_lora_norm=(11, B_Q_LORA), b_kv_lora_norm=(12, B_KV_LORA), c_sink=(13, C_HEADS))
SMALL_WEIGHTS = ("ada_b", "norm_w", "final_norm", "a_q_norm", "a_k_norm", "b_q_lora_norm", "b_kv_lora_norm", "c_sink")
LOSS_ROW = 14


def _pack_small(res):
    def padded(v):
        return jnp.concatenate([v, jnp.zeros((v.shape[0], D_MODEL - v.shape[1]), F32)], axis=1)

    rows = [res["dmod"].reshape(6, D_MODEL), res["norm_w"], res["final_norm"].reshape(1, D_MODEL)]
    rows += [padded(res[k]) for k in ("a_q_norm", "a_k_norm", "b_q_lora_norm", "b_kv_lora_norm", "c_sink")]
    return jnp.concatenate(rows + [res["loss_row"], jnp.zeros((1, D_MODEL), F32)], axis=0)


def _adam_small(parts, ws, ms, vs):
    nw = len(SMALL_WEIGHTS)

    def body(*refs):
        p_ref = refs[0]
        w_refs, m_refs, v_refs = refs[1:1 + nw], refs[1 + nw:1 + 2 * nw], refs[1 + 2 * nw:1 + 3 * nw]
        outs = refs[1 + 3 * nw:]
        g_all = p_ref[0]
        for k in range(1, N_DEV):
            g_all = g_all + p_ref[k]
        for idx, name in enumerate(SMALL_WEIGHTS):
            if name == "ada_b":
                g = jnp.concatenate([jnp.concatenate([g_all[3 * l + t:3 * l + t + 1] for t in range(3)], axis=1) for l in range(2)],
                                    axis=0)
            else:
                row, width = SMALL_ROWS[name]
                g = g_all[row:row + w_refs[idx].shape[0], 0:width]
            d, m_new, v_new = _adam_update(g, w_refs[idx][...], m_refs[idx][...], v_refs[idx][...])
            outs[4 * idx][...], outs[4 * idx + 1][...], outs[4 * idx + 2][...], outs[4 * idx + 3][...] = g, d, m_new, v_new
        outs[4 * nw][...] = g_all[LOSS_ROW:LOSS_ROW + 1, 0:LANES]

    out_shape = []
    for w in ws:
        out_shape += [_sds(w.shape, F32)] * 4
    out_shape.append(_sds((1, LANES), F32))
    return pl.pallas_call(body, name="adam_small", out_shape=out_shape,
                          compiler_params=pltpu.CompilerParams(vmem_limit_bytes=VMEM_LIMIT))(parts, *ws, *ms, *vs)


def _adam(parts, w, m, v, name):
    P, R, C = parts.shape
    tr = R if R <= ADAM_ROW_TILE else ADAM_ROW_TILE
    assert R % tr == 0

    def body(p_ref, w_ref, m_ref, v_ref, g_ref, d_ref, nm_ref, nv_ref):
        g = p_ref[0].astype(F32)
        for k in range(1, P):
            g = g + p_ref[k].astype(F32)
        g_ref[...] = g
        d_ref[...], nm_ref[...], nv_ref[...] = _adam_update(g, w_ref[...], m_ref[...], v_ref[...])

    spec = pl.BlockSpec((tr, C), lambda i: (i, 0))
    return pl.pallas_call(
        body, name=name, grid=(R // tr,),
        in_specs=[pl.BlockSpec((P, tr, C), lambda i: (0, i, 0)), spec, spec, spec],
        out_specs=[spec, spec, spec, spec], out_shape=[_sds((R, C), F32)] * 4,
        compiler_params=_params(("arbitrary",)),
    )(parts, w, m, v)


_ANY = pl.BlockSpec(memory_space=pl.ANY)
CHIP_FLIPS = ((1, 0), (0, 1), (1, 1))
DEV_FLIPS = tuple((dx, dy, dc) for dx in (0, 1) for dy in (0, 1) for dc in (0, 1) if dx + dy + dc)


def _flip(a, d):
    return a if d == 0 else 1 - a


def _my_place():
    return lax.axis_index("x"), lax.axis_index("y"), lax.axis_index("c")


def _gather8_copies(ins, outs, send_sems, recv_sems, loc_sems):
    x, y, c = _my_place()
    me = 4 * x + 2 * y + c
    copies = []
    for a in range(len(ins)):
        copies.append(pltpu.make_async_copy(ins[a], outs[a].at[me], loc_sems.at[a]))
        for k, (dx, dy, dc) in enumerate(DEV_FLIPS):
            copies.append(pltpu.make_async_remote_copy(
                src_ref=ins[a], dst_ref=outs[a].at[me], send_sem=send_sems.at[a, k], recv_sem=recv_sems.at[a, k],
                device_id=(_flip(x, dx), _flip(y, dy), _flip(c, dc)), device_id_type=MESH_ID))
    return copies


def _gather8_sems(n):
    return [pltpu.SemaphoreType.DMA((n, 7)), pltpu.SemaphoreType.DMA((n, 7)), pltpu.SemaphoreType.DMA((n,))]


def _gather_dev8(arrs, name):
    n = len(arrs)

    def body(*refs):
        copies = _gather8_copies(refs[:n], refs[n:2 * n], *refs[2 * n:])
        for cp in copies:
            cp.start()
        for cp in copies:
            cp.wait()

    return pl.pallas_call(
        body, name=name, in_specs=[_ANY] * n, out_specs=[_ANY] * n,
        out_shape=[_sds((N_DEV,) + a.shape, a.dtype) for a in arrs], scratch_shapes=_gather8_sems(n),
    )(*arrs)


class _Exchange:
    def __init__(self, arrs, out_shapes, n_sems, phases):
        self.arrs, self.out_shapes, self.n_sems, self._phases = list(arrs), list(out_shapes), n_sems, phases

    @property
    def n(self):
        return len(self.arrs)

    def sem_shapes(self):
        return [pltpu.SemaphoreType.DMA((self.n, self.n_sems)), pltpu.SemaphoreType.DMA((self.n, self.n_sems)),
                pltpu.SemaphoreType.DMA((self.n,))]

    def phases(self, ins, outs, sems):
        return self._phases(ins, outs, *sems)

    def run(self, name):
        n = self.n

        def body(*refs):
            start, mid, end = self.phases(refs[:n], refs[n:2 * n], refs[2 * n:])
            start()
            mid()
            end()

        return pl.pallas_call(body, name=name, in_specs=[_ANY] * n, out_specs=[_ANY] * n, out_shape=self.out_shapes,
                              scratch_shapes=self.sem_shapes())(*self.arrs)

    def run_with_gather(self, gather_arrs, name):
        n, g = self.n, len(gather_arrs)

        def body(*refs):
            ins, g_ins, outs, g_outs = refs[:n], refs[n:n + g], refs[n + g:2 * n + g], refs[2 * n + g:2 * (n + g)]
            sems = refs[2 * (n + g):]
            start, mid, end = self.phases(ins, outs, sems[:3])
            copies = _gather8_copies(g_ins, g_outs, *sems[3:])
            start()
            for cp in copies:
                cp.start()
            mid()
            end()
            for cp in copies:
                cp.wait()

        outs = pl.pallas_call(
            body, name=name, in_specs=[_ANY] * (n + g), out_specs=[_ANY] * (n + g),
            out_shape=self.out_shapes + [_sds((N_DEV,) + a.shape, a.dtype) for a in gather_arrs],
            scratch_shapes=self.sem_shapes() + _gather8_sems(g))(*self.arrs, *gather_arrs)
        return outs[:n], outs[n:]


def _gather_halves_phases(ins, outs, send_sems, recv_sems, loc_sems):
    n = len(ins)
    x, y, c = _my_place()
    chip = 2 * x + y
    sibling = (x, y, 1 - c)
    peers = [(_flip(x, dx), _flip(y, dy)) for dx, dy in CHIP_FLIPS]

    def remote(src, p, half, a, k, to):
        return pltpu.make_async_remote_copy(src_ref=src, dst_ref=outs[a].at[p, half], send_sem=send_sems.at[a, k],
                                            recv_sem=recv_sems.at[a, k], device_id=to, device_id_type=MESH_ID)

    def local(a):
        return pltpu.make_async_copy(ins[a], outs[a].at[chip], loc_sems.at[a])

    def first(a, k):
        return remote(ins[a].at[c], chip, c, a, k, (*peers[k], c))

    def passed(a, k):
        p = 2 * peers[k][0] + peers[k][1]
        return remote(outs[a].at[p, c], p, c, a, 3 + k, sibling)

    def start():
        for a in range(n):
            local(a).start()
            for k in range(3):
                first(a, k).start()

    def mid():
        for a in range(n):
            for k in range(3):
                p = 2 * peers[k][0] + peers[k][1]
                remote(outs[a].at[p, c], p, c, a, k, sibling).wait_recv()
                passed(a, k).start()

    def end():
        for a in range(n):
            for k in range(3):
                p = 2 * peers[k][0] + peers[k][1]
                remote(outs[a].at[p, 1 - c], p, 1 - c, a, 3 + k, sibling).wait_recv()
        for a in range(n):
            for k in range(3):
                first(a, k).wait_send()
                passed(a, k).wait_send()
            local(a).wait()

    return start, mid, end


def _gather_chip4_halves(arrs):
    return _Exchange(arrs, [_sds((N_CHIPS,) + a.shape, a.dtype) for a in arrs], 6, _gather_halves_phases)


def _reduce_phases(ins, outs, send_sems, recv_sems, loc_sems):
    n = len(ins)
    x, y, c = _my_place()
    chip = 2 * x + y
    sibling = (x, y, 1 - c)
    peers = [(_flip(x, dx), _flip(y, dy)) for dx, dy in CHIP_FLIPS]

    def remote(src, slot, a, k, to):
        return pltpu.make_async_remote_copy(src_ref=src, dst_ref=outs[a].at[slot], send_sem=send_sems.at[a, k],
                                            recv_sem=recv_sems.at[a, k], device_id=to, device_id_type=MESH_ID)

    def local(a):
        return pltpu.make_async_copy(ins[a].at[chip], outs[a].at[2 * chip + c], loc_sems.at[a])

    def own(a):
        return remote(ins[a].at[chip], 2 * chip + c, a, 0, sibling)

    def first(a, k):
        return remote(ins[a].at[2 * peers[k][0] + peers[k][1]], 2 * chip + c, a, 1 + k, (*peers[k], c))

    def passed(a, k):
        slot = 2 * (2 * peers[k][0] + peers[k][1]) + c
        return remote(outs[a].at[slot], slot, a, 4 + k, sibling)

    def start():
        for a in range(n):
            local(a).start()
            own(a).start()
            for k in range(3):
                first(a, k).start()

    def mid():
        for a in range(n):
            for k in range(3):
                slot = 2 * (2 * peers[k][0] + peers[k][1]) + c
                remote(outs[a].at[slot], slot, a, 1 + k, sibling).wait_recv()
                passed(a, k).start()

    def end():
        for a in range(n):
            remote(outs[a].at[2 * chip + 1 - c], 2 * chip + 1 - c, a, 0, sibling).wait_recv()
            for k in range(3):
                slot = 2 * (2 * peers[k][0] + peers[k][1]) + 1 - c
                remote(outs[a].at[slot], slot, a, 4 + k, sibling).wait_recv()
        for a in range(n):
            own(a).wait_send()
            for k in range(3):
                first(a, k).wait_send()
                passed(a, k).wait_send()
            local(a).wait()

    return start, mid, end


def _reduce_exchange(arrs):
    return _Exchange(arrs, [_sds((N_DEV,) + a.shape[1:], a.dtype) for a in arrs], 7, _reduce_phases)


def _even_in_layout(w):
    return jnp.concatenate([w[:, 0:1696], jnp.zeros((w.shape[0], 96), w.dtype), w[:, 1696:2208]], axis=1)


def _even_in_unlayout(g):
    return jnp.concatenate([g[:, 0:1696], g[:, 1792:2304]], axis=1)


def _uq_layout(w):
    per = B_NOPE + B_ROPE
    pad = jnp.zeros((w.shape[0], LANES - B_ROPE), w.dtype)
    nope = [w[:, per * h:per * h + B_NOPE] for h in range(B_HEADS)]
    rope = [jnp.concatenate([w[:, per * h + B_NOPE:per * (h + 1)], pad], axis=1) for h in range(B_HEADS)]
    return jnp.concatenate(nope + rope, axis=1)


def _uq_unlayout(g):
    parts = []
    for h in range(B_HEADS):
        parts += [g[:, B_NOPE * h:B_NOPE * (h + 1)], g[:, 512 + LANES * h:512 + LANES * h + B_ROPE]]
    return jnp.concatenate(parts, axis=1)


def _block_diag(blocks):
    rows = []
    for h, blk in enumerate(blocks):
        r, cdim = blk.shape
        n = len(blocks)
        rows.append(jnp.concatenate([jnp.zeros((r, cdim * h), blk.dtype), blk, jnp.zeros((r, cdim * (n - 1 - h)), blk.dtype)],
                                    axis=1))
    return jnp.concatenate(rows, axis=0)


def _uk_layout(w):
    return _block_diag([w[:, h, :].T for h in range(B_HEADS)])


def _uk_unlayout(g):
    return jnp.stack([g[B_NOPE * h:B_NOPE * (h + 1), LANES * h:LANES * (h + 1)].T for h in range(B_HEADS)], axis=1)


def _uv_layout(w):
    return _block_diag([w[:, h, :] for h in range(B_HEADS)])


def _uv_unlayout(g):
    return jnp.stack([g[LANES * h:LANES * (h + 1), B_V * h:B_V * (h + 1)] for h in range(B_HEADS)], axis=1)


def _rope_tables(S):
    inv = ROPE_THETA ** (-jnp.arange(0, 32, 2, dtype=F32) / 32)
    tok = jnp.arange(S)

    def tab(pos):
        ang = pos.astype(F32)[:, None] * inv[None, :]
        cos, sin = jnp.cos(ang), jnp.sin(ang)
        return jnp.concatenate([cos, cos], axis=1), jnp.concatenate([-sin, sin], axis=1)

    cr, sr = tab(tok // GRID_W)
    cc, sc = tab(tok % GRID_W)
    ct, st = tab(tok)
    return (jnp.tile(jnp.concatenate([cr, cc], axis=1), (1, 2)), jnp.tile(jnp.concatenate([sr, sc], axis=1), (1, 2)),
            jnp.tile(ct, (1, 4)), jnp.tile(st, (1, 4)))


A_TQ, A_TK, A_SUB = 512, 4096, 512
B_TQ, B_TK, B_SUB = 128, 4096, 1024
B_BWD_TK, B_BWD_SUB = 4096, 512
C_T = 256
C_BLOCKS_PER_STEP = 8
KV_SHARE = 2


def _local_step(x0, tgt, mod, norm_w, wie, wuq, wuk, wuv, late_shards, a_q_norm, a_k_norm, q_lora_norm, kv_lora_norm,
                c_sink, final_norm):
    S = x0.shape[0]
    mod3 = mod.reshape(2, 3, D_MODEL)
    ca, sa, ct, st = _rope_tables(S)
    lane_seg = np.arange(LANES) // HEAD_DIM
    seg = jnp.asarray((lane_seg[:, None] == lane_seg[None, :]).astype(np.float32)).astype(BF16)
    qn = jnp.tile(a_q_norm.reshape(1, HEAD_DIM), (1, 2))
    kn = jnp.tile(a_k_norm.reshape(1, HEAD_DIM), (1, 2))
    qln, kvln = q_lora_norm.reshape(1, B_Q_LORA), kv_lora_norm.reshape(1, B_KV_LORA)
    nw0, nw1 = norm_w[0:1], norm_w[1:2]
    gate0, gate1 = mod3[0, 2:3], mod3[1, 2:3]
    a_tq, a_tk, b_tq, b_tk, bb_tk, c_t = min(A_TQ, S), min(A_TK, S), min(B_TQ, S), min(B_TK, S), min(B_BWD_TK, S), min(C_T, S)
    a_sub, b_sub, bb_sub = min(A_SUB, a_tk), min(B_SUB, b_tk), min(B_BWD_SUB, bb_tk)

    h0, proj_e, qa, ka, va, qcat, kcat, ka_t, va_t, kcat_t = _even_pre_fwd(x0, mod3[0], nw0, wie, qn, kn, seg, ca, sa, ct, st,
                                                                           qln, kvln, wuq, wuk)
    oa, lse_a, woe_g, wio_g, woo_g = _pp_fwd(qa, ka, va_t, kdiv=KV_SHARE, tq=a_tq, tk=a_tk, sub=a_sub, name="attn_a_fwd",
                                             side=_gather_chip4_halves(late_shards))
    woe = woe_g.reshape(D_MODEL, D_MODEL)
    wio = wio_g.reshape(N_CHIPS, D_MODEL, ODD_IN // N_CHIPS)
    woo = woo_g.reshape(D_MODEL, D_MODEL)
    olat, lse_b = _mla_fwd(qcat, kcat, kcat_t, tq=b_tq, tk=b_tk, sub=b_sub)
    y0, x1 = _even_post_fwd(oa, olat, proj_e, x0, gate0, wuv, woe)
    h1, proj_o, qc, kc, vc, kc_t, vc_t = _odd_pre_fwd(x1, mod3[1], nw1, wio)
    slopes = 2.0 ** (-8.0 * jnp.arange(1, C_HEADS + 1, dtype=F32) / C_HEADS)
    slope_rows = jnp.repeat(slopes.reshape(C_HEADS // 2, 2), c_t, axis=1)[:, None, :]
    sink_rows = jnp.repeat(c_sink.reshape(C_HEADS // 2, 2), c_t, axis=1)[:, None, :]
    win_dist = _win_dist_table(S, c_t)
    oc, lse_c = _win_fwd(qc, kc, vc_t, win_dist, slope_rows, sink_rows, kdiv=KV_SHARE, tq=c_t, nbs=C_BLOCKS_PER_STEP,
                         name="attn_c_fwd")
    doc, dgc, dx2, dwoo, st_f = _odd_post(oc, proj_o, x1, gate1, woo, final_norm.reshape(1, D_MODEL), tgt)
    dqc, dkc, dvc, dsink_raw = _win_bwd(qc, kc, kc_t, vc, oc, doc, lse_c, win_dist, slope_rows, sink_rows, kdiv=KV_SHARE, tq=c_t,
                                        nbs=C_BLOCKS_PER_STEP, name="attn_c_bwd")
    dx1, dwio, st_1 = _odd_pre_bwd(dqc, dkc, dvc, dgc, h1, x1, dx2, mod3[1], nw1, wio)
    doa, dga, dgb, dolat, dwoe, dwuv, st_e = _even_post_bwd(dx1, y0, oa, olat, proj_e, gate0, wuv, woe)
    late_grads = _reduce_exchange([dwoe.reshape(N_CHIPS, D_MODEL // N_CHIPS, D_MODEL), dwio,
                                   dwoo.reshape(N_CHIPS, D_MODEL // N_CHIPS, D_MODEL)])
    dqa, dka, dva, p_woe, p_wio, p_woo = _pp_bwd(qa, ka, ka_t, va, oa, doa, lse_a, kdiv=KV_SHARE, tq=a_tq, tk=a_tk, sub=a_sub,
                                                 name="attn_a_bwd", side=late_grads)
    dqcat, dkcat = _mla_bwd(qcat, kcat, kcat_t, olat, dolat, lse_b, tq=b_tq, tk=bb_tk, sub=bb_sub)
    dx0, dwie, dwuq, dwuk, st_0, nst = _even_pre_bwd(x0, h0, proj_e, dqa, dka, dva, dga, dgb, dqcat, dkcat, dx1, mod3[0], nw0,
                                                     wie, qn, kn, seg, ca, sa, ct, st, qln, kvln, wuq, wuk)
    dsink_pairs = jnp.stack([dsink_raw[:, 0, 0], dsink_raw[:, 1, 0]], axis=1).reshape(C_HEADS)
    return dict(
        loss_row=st_f[2:3], dx=dx0,
        dmod=jnp.stack([jnp.concatenate([st_0[0], st_0[1], st_e[0]]), jnp.concatenate([st_1[0], st_1[1], st_f[1]])]),
        norm_w=jnp.stack([st_0[2], st_1[2]]), final_norm=st_f[0],
        a_q_norm=nst[0:1, 0:HEAD_DIM], a_k_norm=nst[1:2, 0:HEAD_DIM], b_q_lora_norm=nst[2:3, :], b_kv_lora_norm=nst[3:4, 0:LANES],
        c_sink=dsink_pairs.reshape(1, C_HEADS),
        even_w_in=dwie, b_w_uq=dwuq, b_w_uk=dwuk, b_w_uv=dwuv, even_w_out=p_woe, odd_w_in=p_wio, odd_w_out=p_woo)


WEIGHT_NAMES = ("norm_w", "ada_w", "ada_b", "even_w_in", "a_q_norm", "a_k_norm", "b_q_lora_norm", "b_kv_lora_norm", "b_w_uq",
                "b_w_uk", "b_w_uv", "even_w_out", "odd_w_in", "c_sink", "odd_w_out", "final_norm")


def _cols_to_chips(g):
    r, n4 = g.shape
    return jnp.transpose(g.reshape(r, N_CHIPS, n4 // N_CHIPS), (1, 0, 2))


def _chips_to_cols(g):
    p, r, n = g.shape
    return jnp.transpose(g, (1, 0, 2)).reshape(r, p * n)


def kernel(x, c, norm_w, ada_w, ada_b, even_w_in, a_q_norm, a_k_norm, b_q_lora_norm, b_kv_lora_norm, b_w_uq, b_w_uk, b_w_uv, even_w_out, odd_w_in, c_sink, odd_w_out, final_norm, loss_target, m_norm_w, m_ada_w, m_ada_b, m_even_w_in, m_a_q_norm, m_a_k_norm, m_b_q_lora_norm, m_b_kv_lora_norm, m_b_w_uq, m_b_w_uk, m_b_w_uv, m_even_w_out, m_odd_w_in, m_c_sink, m_odd_w_out, m_final_norm, v_norm_w, v_ada_w, v_ada_b, v_even_w_in, v_a_q_norm, v_a_k_norm, v_b_q_lora_norm, v_b_kv_lora_norm, v_b_w_uq, v_b_w_uk, v_b_w_uv, v_even_w_out, v_odd_w_in, v_c_sink, v_odd_w_out, v_final_norm):
    given = dict(locals())
    xi, yi, ci = _my_place()
    chip = 2 * xi + yi
    dev = 2 * chip + ci
    n_ada = ada_w.shape[2]

    (c_all,) = _gather_dev8([c], "gather_c")
    c_all = c_all.reshape(N_DEV, D_MODEL)
    bias = lax.dynamic_slice_in_dim(ada_b, chip * n_ada, n_ada, axis=1).reshape(2, 1, n_ada)
    mod_cols = _ada_fwd(c_all, ada_w, bias)
    def halves(w):
        return w.astype(BF16).reshape((2, w.shape[0] // 2) + w.shape[1:])

    mod_all, wie_g, wuq_g = _gather_chip4_halves([mod_cols, halves(even_w_in[0]), halves(b_w_uq[0])]).run("gather_weights")
    wie_g = wie_g.reshape(N_CHIPS, D_MODEL, EVEN_IN // N_CHIPS)
    wuq_g = wuq_g.reshape(N_CHIPS, B_Q_LORA, -1)
    mod = jnp.transpose(lax.dynamic_index_in_dim(mod_all, dev, axis=2, keepdims=False), (1, 0, 2)).reshape(2, 3 * D_MODEL)

    res = _local_step(
        x[0], loss_target[0], mod, norm_w,
        _even_in_layout(_chips_to_cols(wie_g)), _uq_layout(_chips_to_cols(wuq_g)), _uk_layout(b_w_uk[0].astype(BF16)),
        _uv_layout(b_w_uv[0].astype(BF16)), [halves(even_w_out[0]), halves(odd_w_in[0]), halves(odd_w_out[0])],
        a_q_norm, a_k_norm, b_q_lora_norm, b_kv_lora_norm, c_sink, final_norm)

    latent = jnp.stack([_uk_unlayout(res["b_w_uk"]).reshape(B_KV_LORA, 512),
                        _uv_unlayout(res["b_w_uv"]).reshape(B_KV_LORA, 512)]).astype(BF16)
    (p_wie, p_wuq), (small_all, latent_all) = _reduce_exchange(
        [_cols_to_chips(_even_in_unlayout(res["even_w_in"].astype(BF16))), _cols_to_chips(_uq_unlayout(res["b_w_uq"].astype(BF16)))]
    ).run_with_gather([_pack_small(res), latent], "reduce_exchange")
    shard_parts = dict(even_w_in=p_wie, b_w_uq=p_wuq, **{k: res[k] for k in ("even_w_out", "odd_w_in", "odd_w_out")})
    dmod_all = small_all[:, 0:6, :].reshape(N_DEV, 2, 3 * D_MODEL)
    dmod_cols = jnp.transpose(lax.dynamic_slice_in_dim(dmod_all, chip * n_ada, n_ada, axis=2), (1, 0, 2))
    parts = dict(shard_parts)
    parts["ada_w"] = _ada_bwd(c_all.T, dmod_cols).reshape(1, 2 * D_MODEL, n_ada)
    parts["b_w_uk"], parts["b_w_uv"] = latent_all[:, 0], latent_all[:, 1]

    def as2d(a):
        return a.reshape((-1, a.shape[-1]) if a.ndim > 1 else (1, a.shape[0]))

    results = {}
    small_outs = _adam_small(small_all, *[[as2d(given[pre + k]) for k in SMALL_WEIGHTS] for pre in ("", "m_", "v_")])
    for idx, k in enumerate(SMALL_WEIGHTS):
        results[k] = small_outs[4 * idx:4 * idx + 4]
    for k, p in parts.items():
        shape2 = (p.shape[-2], p.shape[-1])
        results[k] = _adam(p, given[k].reshape(shape2), given["m_" + k].reshape(shape2), given["v_" + k].reshape(shape2),
                           "adam_" + k)
    by_kind = [[results[k][t].reshape(given[k].shape) for k in WEIGHT_NAMES] for t in range(4)]
    return (small_outs[-1][0, 0], res["dx"][None], *by_kind[0], *by_kind[1], *by_kind[2], *by_kind[3])
```

```python
import numpy as np
import jax
import jax.numpy as jnp
from jax import lax
from jax.experimental import pallas as pl
from jax.experimental.pallas import tpu as pltpu

F32 = jnp.float32
BF16 = jnp.bfloat16
HIGHEST = lax.Precision.HIGHEST
MESH_ID = pl.DeviceIdType.MESH

D_MODEL = 1024
HEAD_DIM = 64
GRID_W = 64
EPS = 1e-6
ROPE_THETA = 10000.0
B_HEADS, B_NOPE, B_ROPE, B_V = 8, 64, 32, 64
B_Q_LORA, B_KV_LORA = 256, 128
C_HEADS = 16
WINDOW = 128
EVEN_IN, ODD_IN = 2208, 2560
EVEN_P = 2304
N_CHIPS, N_DEV = 4, 8
LANES = 128
NEG = -1e30
VMEM_LIMIT = 60 * 1024 * 1024

ADAM_LR, ADAM_B1, ADAM_B2, ADAM_EPS, ADAM_WD, ADAM_STEP = 0.001, 0.9, 0.999, 1e-08, 0.01, 10

ROW_TILE = 512
IN_PROJ_ROW_TILE = 256


def _dot(a, b):
    return lax.dot_general(a, b, (((1,), (0,)), ((), ())), preferred_element_type=F32)


def _dot_nt(a, b):
    return lax.dot_general(a, b, (((1,), (1,)), ((), ())), preferred_element_type=F32)


def _dot_tn(a, b):
    return lax.dot_general(a, b, (((0,), (0,)), ((), ())), preferred_element_type=F32)


def _dot_f32(a, b):
    return lax.dot_general(a, b, (((1,), (0,)), ((), ())), precision=HIGHEST, preferred_element_type=F32)


def _sigmoid(x):
    return 1.0 / (1.0 + jnp.exp(-x))


def _silu_and_grad(g):
    s = _sigmoid(g)
    return g * s, s * (1.0 + g * (1.0 - s))


def _lane_iota():
    return lax.broadcasted_iota(jnp.int32, (1, LANES), 1)


def _partner(x, lane):
    return jnp.where((lane % 32) < 16, pltpu.roll(x, LANES - 16, 1), pltpu.roll(x, 16, 1))


def _rot(x, cos, sin_signed, lane):
    return x * cos + _partner(x, lane) * sin_signed


def _rot_bwd(dy, cos, sin_signed, lane):
    return dy * cos + _partner(dy * sin_signed, lane)


def _rms(x):
    return lax.rsqrt(jnp.mean(x * x, axis=-1, keepdims=True) + EPS)


def _rms_bwd(x, r, g):
    return r * g - x * (r * r * r) * jnp.mean(x * g, axis=-1, keepdims=True)


def _seg_mean(v, seg_ones):
    hi = v.astype(BF16)
    lo = (v - hi.astype(F32)).astype(BF16)
    return (_dot(hi, seg_ones) + _dot(lo, seg_ones)) * (1.0 / HEAD_DIM)


def _dup_heads(x, lane):
    swapped = pltpu.roll(x, HEAD_DIM, 1)
    lo = lane < HEAD_DIM
    return jnp.concatenate([jnp.where(lo, x, swapped), jnp.where(lo, swapped, x)], axis=1)


def _fold_heads(x2, lane):
    a, b = x2[:, 0:LANES], x2[:, LANES:2 * LANES]
    return jnp.where(lane < HEAD_DIM, a + pltpu.roll(a, HEAD_DIM, 1), b + pltpu.roll(b, HEAD_DIM, 1))


def _row_spec(ts, cols):
    return pl.BlockSpec((ts, cols), lambda i: (i, 0))


def _full_spec(shape, single=True):
    nd = len(shape)
    if single:
        return pl.BlockSpec(shape, lambda i: (0,) * nd, pipeline_mode=pl.Buffered(1))
    return pl.BlockSpec(shape, lambda i: (0,) * nd)


def _sds(shape, dtype):
    return jax.ShapeDtypeStruct(shape, dtype)


def _params(sem):
    return pltpu.CompilerParams(dimension_semantics=sem, vmem_limit_bytes=VMEM_LIMIT)


def _even_pre_fwd(x, mod, nw, wie, qn, kn, seg, ca, sa, ct, st, qln, kvln, wuq, wuk):
    S = x.shape[0]
    ts = min(IN_PROJ_ROW_TILE, S)

    def body(x_ref, mod_ref, nw_ref, wie_ref, qn_ref, kn_ref, seg_ref, ca_ref, sa_ref, ct_ref, st_ref, qln_ref,
             kvln_ref, wuq_ref, wuk_ref, h_ref, proj_ref, qa_ref, ka_ref, va_ref, qcat_ref, kcat_ref, kat_ref, vat_ref, kcatt_ref):
        xv = x_ref[...]
        h = (xv * _rms(xv) * nw_ref[...]) * (1.0 + mod_ref[1:2, :]) + mod_ref[0:1, :]
        hb = h.astype(BF16)
        h_ref[...] = hb
        proj = _dot(hb, wie_ref[...])
        proj_ref[...] = proj
        lane = _lane_iota()
        ca_v, sa_v, ct_v, st_v = ca_ref[...], sa_ref[...], ct_ref[...], st_ref[...]
        seg_v = seg_ref[...]
        for cb in range(4):
            xc = proj[:, LANES * cb:LANES * (cb + 1)]
            r = lax.rsqrt(_seg_mean(xc * xc, seg_v) + EPS)
            y = _rot(xc * r * qn_ref[...], ca_v, sa_v, lane)
            qa_ref[:, LANES * cb:LANES * (cb + 1)] = (y * 0.125).astype(BF16)
        kc = proj[:, 512:640]
        r = lax.rsqrt(_seg_mean(kc * kc, seg_v) + EPS)
        ka_v = _dup_heads(_rot(kc * r * kn_ref[...], ca_v, sa_v, lane), lane)
        ka_ref[...] = ka_v.astype(BF16)
        kat_ref[...] = ka_v.T.astype(BF16)
        va_v = _dup_heads(proj[:, 640:768], lane)
        va_ref[...] = va_v.astype(BF16)
        vat_ref[...] = va_v.T.astype(BF16)
        cq = proj[:, 1280:1536]
        cqn = (cq * _rms(cq) * qln_ref[...]).astype(BF16)
        ckv = proj[:, 1536:1664]
        ckvn = ckv * _rms(ckv) * kvln_ref[...]
        qb = _dot(cqn, wuq_ref[...])
        qlat = _dot(qb[:, 0:512].astype(BF16), wuk_ref[...])
        for hh in range(B_HEADS):
            qcat_ref[hh, :, 0:LANES] = qlat[:, LANES * hh:LANES * (hh + 1)].astype(BF16)
            qr = _rot(qb[:, 512 + LANES * hh:512 + LANES * (hh + 1)], ct_v, st_v, lane)
            qcat_ref[hh, :, LANES:2 * LANES] = qr.astype(BF16)
        kr = _rot(proj[:, 1664:1792], ct_v, st_v, lane)
        kcat_ref[:, 0:LANES] = ckvn.astype(BF16)
        kcat_ref[:, LANES:2 * LANES] = kr.astype(BF16)
        kcatt_ref[0:LANES, :] = ckvn.T.astype(BF16)
        kcatt_ref[LANES:2 * LANES, :] = kr.T.astype(BF16)

    col_spec = lambda rows: pl.BlockSpec((rows, ts), lambda i: (0, i))
    return pl.pallas_call(
        body, name="even_pre_fwd", grid=(S // ts,),
        in_specs=[_row_spec(ts, D_MODEL), _full_spec((3, D_MODEL)), _full_spec((1, D_MODEL)), _full_spec((D_MODEL, EVEN_P)),
                  _full_spec((1, LANES)), _full_spec((1, LANES)), _full_spec((LANES, LANES)),
                  _row_spec(ts, LANES), _row_spec(ts, LANES), _row_spec(ts, LANES), _row_spec(ts, LANES),
                  _full_spec((1, B_Q_LORA)), _full_spec((1, B_KV_LORA)), _full_spec((B_Q_LORA, 1536)), _full_spec((512, 1024))],
        out_specs=[_row_spec(ts, D_MODEL), _row_spec(ts, EVEN_P), _row_spec(ts, 512), _row_spec(ts, 2 * LANES), _row_spec(ts, 2 * LANES),
                   pl.BlockSpec((B_HEADS, ts, 2 * LANES), lambda i: (0, i, 0)), _row_spec(ts, 2 * LANES),
                   col_spec(2 * LANES), col_spec(2 * LANES), col_spec(2 * LANES)],
        out_shape=[_sds((S, D_MODEL), BF16), _sds((S, EVEN_P), F32), _sds((S, 512), BF16), _sds((S, 2 * LANES), BF16),
                   _sds((S, 2 * LANES), BF16), _sds((B_HEADS, S, 2 * LANES), BF16), _sds((S, 2 * LANES), BF16),
                   _sds((2 * LANES, S), BF16), _sds((2 * LANES, S), BF16), _sds((2 * LANES, S), BF16)],
        compiler_params=_params(("arbitrary",)),
    )(x, mod, nw, wie, qn, kn, seg, ca, sa, ct, st, qln, kvln, wuq, wuk)


MLA_SCALE = (B_NOPE + B_ROPE) ** -0.5
LOG2E = 1.4426950408889634

def _row_lo():
    return lax.broadcasted_iota(jnp.int32, (LANES, 1), 0) < HEAD_DIM


def _stack_cols(vT, rlo):
    zero = jnp.zeros_like(vT)
    return jnp.concatenate([jnp.where(rlo, vT, zero), jnp.where(rlo, zero, vT)], axis=1)


def _stack_rows(v, lo):
    zero = jnp.zeros_like(v)
    return jnp.concatenate([jnp.where(lo, v, zero), jnp.where(lo, zero, v)], axis=0)


def _pick_halves_T(xT, rlo, t):
    return jnp.where(rlo, xT[:, 0:t], xT[:, t:2 * t]).T


def _side_split(refs, n_in, n_out, n_scratch, side):
    ns = side.n if side is not None else 0
    cuts = np.cumsum([0, n_in, ns, n_out, ns, n_scratch])
    return [refs[a:b] for a, b in zip(cuts[:-1], cuts[1:])] + [refs[cuts[-1]:]]


def _side_hooks(side, side_ins, side_outs, side_sems, step, total):
    if side is None:
        return lambda: None
    start, mid, end = side.phases(side_ins, side_outs, side_sems)
    pl.when(step == 0)(start)
    pl.when(step == total // 2)(mid)
    return lambda: pl.when(step == total - 1)(end)


def _side_specs(side):
    if side is None:
        return [], [], [], [], []
    return list(side.arrs), [_ANY] * side.n, [_ANY] * side.n, list(side.out_shapes), side.sem_shapes()


def _pp_fwd(q, k, vT, *, kdiv, tq, tk, sub, name, side=None):
    S = k.shape[0]; nb = q.shape[1] // LANES; nq = S // tq; nkv = S // tk; nsub = tk // sub

    def body(*refs):
        (q_ref, k_ref, vT_ref), side_ins, (o_ref, lse_ref), side_outs, (qs, m_s, l_s, acc), side_sems = _side_split(refs, 3, 2, 4, side)
        j = pl.program_id(2)
        rlo = _row_lo()
        step = (pl.program_id(0) * nq + pl.program_id(1)) * nkv + j
        side_end = _side_hooks(side, side_ins, side_outs, side_sems, step, nb * nq * nkv)

        @pl.when(j == 0)
        def _():
            qs[...] = _stack_cols(q_ref[...].astype(F32).T, rlo).astype(BF16)
            m_s[...] = jnp.full((1, 2 * tq), NEG, F32)
            l_s[...] = jnp.zeros((1, 2 * tq), F32)
            acc[...] = jnp.zeros((LANES, 2 * tq), F32)

        qsv = qs[...]
        m, l, a = m_s[...], l_s[...], acc[...]
        s_cur = _dot(k_ref[0:sub, :], qsv)
        for t in range(nsub):
            if t + 1 < nsub:
                s_next = _dot(k_ref[sub * (t + 1):sub * (t + 2), :], qsv)
            m_new = jnp.maximum(m, jnp.max(s_cur, axis=0, keepdims=True))
            alpha = jnp.exp(m - m_new)
            p = jnp.exp(s_cur - m_new)
            l = alpha * l + jnp.sum(p, axis=0, keepdims=True)
            a = alpha * a + _dot(vT_ref[:, sub * t:sub * (t + 1)], p.astype(BF16))
            m = m_new
            if t + 1 < nsub:
                s_cur = s_next
        m_s[...], l_s[...], acc[...] = m, l, a

        @pl.when(j == nkv - 1)
        def _():
            l_f = l_s[...]
            o_ref[...] = _pick_halves_T(acc[...] / l_f, rlo, tq).astype(BF16)
            lse_ref[0, 0] = m_s[...] + jnp.log(l_f)

        side_end()

    s_args, s_in, s_out, s_shapes, s_sems = _side_specs(side)
    return pl.pallas_call(
        body, name=name, grid=(nb, nq, nkv),
        in_specs=[pl.BlockSpec((tq, LANES), lambda b, i, j: (i, b)), pl.BlockSpec((tk, LANES), lambda b, i, j: (j, b // kdiv)),
                  pl.BlockSpec((LANES, tk), lambda b, i, j: (b // kdiv, j))] + s_in,
        out_specs=[pl.BlockSpec((tq, LANES), lambda b, i, j: (i, b)),
                   pl.BlockSpec((1, 1, 1, 2 * tq), lambda b, i, j: (b, i, 0, 0))] + s_out,
        out_shape=[_sds((S, nb * LANES), BF16), _sds((nb, nq, 1, 2 * tq), F32)] + s_shapes,
        scratch_shapes=[pltpu.VMEM((LANES, 2 * tq), BF16), pltpu.VMEM((1, 2 * tq), F32), pltpu.VMEM((1, 2 * tq), F32),
                        pltpu.VMEM((LANES, 2 * tq), F32)] + s_sems,
        compiler_params=_params(("arbitrary",) * 3))(q, k, vT, *s_args)


def _pp_bwd(q, k, kT, v, o, do, lse, *, kdiv, tq, tk, sub, name, side=None):
    S = k.shape[0]; nb = q.shape[1] // LANES; nkb = k.shape[1] // LANES; nq = S // tq; nkv = S // tk; nsub = tk // sub

    def body(*refs):
        ((q_ref, k_ref, kT_ref, v_ref, o_ref, do_ref, lse_ref), side_ins, (dq_ref, dk_ref, dv_ref), side_outs,
         (qsT, qs, dosT, dos, delta_s, dq_acc), side_sems) = _side_split(refs, 7, 3, 6, side)
        b, i, j = pl.program_id(0), pl.program_id(1), pl.program_id(2)
        rlo = _row_lo()
        lo = lax.broadcasted_iota(jnp.int32, (1, LANES), 1) < HEAD_DIM
        side_end = _side_hooks(side, side_ins, side_outs, side_sems, (b * nq + i) * nkv + j, nb * nq * nkv)

        @pl.when((b % kdiv == 0) & (i == 0) & (j == 0))
        def _():
            dk_ref[...] = jnp.zeros((S, LANES), F32)
            dv_ref[...] = jnp.zeros((S, LANES), F32)

        @pl.when(j == 0)
        def _():
            qv = q_ref[...]
            qs[...] = _stack_rows(qv, lo)
            qsT[...] = _stack_cols(qv.astype(F32).T, rlo).astype(BF16)
            dov = do_ref[...].astype(F32)
            dos[...] = _stack_rows(dov.astype(BF16), lo)
            dosT[...] = _stack_cols(dov.T, rlo).astype(BF16)
            prodT = (dov * o_ref[...].astype(F32)).T
            delta_s[...] = jnp.concatenate([jnp.sum(jnp.where(rlo, prodT, 0.0), axis=0, keepdims=True),
                                            jnp.sum(jnp.where(rlo, 0.0, prodT), axis=0, keepdims=True)], axis=1)
            dq_acc[...] = jnp.zeros((LANES, 2 * tq), F32)

        qsTv, dosTv, qsv, dosv = qsT[...], dosT[...], qs[...], dos[...]
        lse_v, delta_v = lse_ref[0, 0], delta_s[...]
        dqa = dq_acc[...]
        s_cur = _dot(k_ref[0:sub, :], qsTv)
        dp_cur = _dot(v_ref[0:sub, :], dosTv)
        for t in range(nsub):
            if t + 1 < nsub:
                s_next = _dot(k_ref[sub * (t + 1):sub * (t + 2), :], qsTv)
                dp_next = _dot(v_ref[sub * (t + 1):sub * (t + 2), :], dosTv)
            p = jnp.exp(s_cur - lse_v)
            ds = (p * (dp_cur - delta_v)).astype(BF16)
            rows = pl.ds(pl.multiple_of(j * tk + sub * t, sub), sub)
            dv_ref[rows, :] += _dot(p.astype(BF16), dosv)
            dk_ref[rows, :] += _dot(ds, qsv)
            dqa = dqa + _dot(kT_ref[:, sub * t:sub * (t + 1)], ds)
            if t + 1 < nsub:
                s_cur, dp_cur = s_next, dp_next
        dq_acc[...] = dqa

        @pl.when(j == nkv - 1)
        def _():
            dq_ref[...] = _pick_halves_T(dq_acc[...], rlo, tq)

        side_end()

    qmap = lambda b, i, j: (i, b)
    kmap = lambda b, i, j: (j, b // kdiv)
    res = lambda b, i, j: (0, b // kdiv)
    s_args, s_in, s_out, s_shapes, s_sems = _side_specs(side)
    return pl.pallas_call(
        body, name=name, grid=(nb, nq, nkv),
        in_specs=[pl.BlockSpec((tq, LANES), qmap), pl.BlockSpec((tk, LANES), kmap), pl.BlockSpec((LANES, tk), lambda b, i, j: (b // kdiv, j)),
                  pl.BlockSpec((tk, LANES), kmap), pl.BlockSpec((tq, LANES), qmap), pl.BlockSpec((tq, LANES), qmap),
                  pl.BlockSpec((1, 1, 1, 2 * tq), lambda b, i, j: (b, i, 0, 0))] + s_in,
        out_specs=[pl.BlockSpec((tq, LANES), qmap), pl.BlockSpec((S, LANES), res), pl.BlockSpec((S, LANES), res)] + s_out,
        out_shape=[_sds((S, nb * LANES), F32), _sds((S, nkb * LANES), F32), _sds((S, nkb * LANES), F32)] + s_shapes,
        scratch_shapes=[pltpu.VMEM((LANES, 2 * tq), BF16), pltpu.VMEM((2 * tq, LANES), BF16), pltpu.VMEM((LANES, 2 * tq), BF16),
                        pltpu.VMEM((2 * tq, LANES), BF16), pltpu.VMEM((1, 2 * tq), F32), pltpu.VMEM((LANES, 2 * tq), F32)] + s_sems,
        compiler_params=_params(("arbitrary",) * 3))(q, k, kT, v, o, do, lse, *s_args)


MLA_C = MLA_SCALE * LOG2E


def _mla_fwd(q, kcat, kcatT, *, tq, tk, sub):
    S = kcat.shape[0]; nq, nkv = S // tq, S // tk; R = B_HEADS * tq; nsub = tk // sub

    def body(q_ref, k_ref, vT_ref, o_ref, lse_ref, qT, m_s, l_s, acc):
        j = pl.program_id(1)

        @pl.when(j == 0)
        def _():
            qT[...] = q_ref[...].reshape(R, 2 * LANES).astype(F32).T.astype(BF16)
            m_s[...] = jnp.full((1, R), NEG, F32)
            l_s[...] = jnp.zeros((1, R), F32)
            acc[...] = jnp.zeros((LANES, R), F32)

        qTv = qT[...]
        m, l, a = m_s[...], l_s[...], acc[...]
        s_cur = _dot(k_ref[0:sub, :], qTv)
        for t in range(nsub):
            if t + 1 < nsub:
                s_next = _dot(k_ref[sub * (t + 1):sub * (t + 2), :], qTv)
            m_new = jnp.maximum(m, jnp.max(s_cur, axis=0, keepdims=True))
            alpha = jnp.exp2((m - m_new) * MLA_C)
            p = jnp.exp2((s_cur - m_new) * MLA_C)
            l = alpha * l + jnp.sum(p, axis=0, keepdims=True)
            a = alpha * a + _dot(vT_ref[:, sub * t:sub * (t + 1)], p.astype(BF16))
            m = m_new
            if t + 1 < nsub:
                s_cur = s_next
        m_s[...], l_s[...], acc[...] = m, l, a

        @pl.when(j == nkv - 1)
        def _():
            l_f = l_s[...]
            o_ref[...] = (acc[...] / l_f).T.reshape(B_HEADS, tq, LANES).astype(BF16)
            lse_ref[0] = m_s[...] * MLA_SCALE + jnp.log(l_f)

    return pl.pallas_call(
        body, name="mla_fwd", grid=(nq, nkv),
        in_specs=[pl.BlockSpec((B_HEADS, tq, 2 * LANES), lambda i, j: (0, i, 0)), pl.BlockSpec((tk, 2 * LANES), lambda i, j: (j, 0)),
                  pl.BlockSpec((LANES, tk), lambda i, j: (0, j))],
        out_specs=[pl.BlockSpec((B_HEADS, tq, LANES), lambda i, j: (0, i, 0)), pl.BlockSpec((1, 1, R), lambda i, j: (i, 0, 0))],
        out_shape=[_sds((B_HEADS, S, LANES), BF16), _sds((nq, 1, R), F32)],
        scratch_shapes=[pltpu.VMEM((2 * LANES, R), BF16), pltpu.VMEM((1, R), F32), pltpu.VMEM((1, R), F32), pltpu.VMEM((LANES, R), F32)],
        compiler_params=_params(("arbitrary", "arbitrary")))(q, kcat, kcatT)


def _mla_bwd(q, kcat, kcatT, o, do, lse, *, tq, tk, sub):
    S = kcat.shape[0]; nq, nkv = S // tq, S // tk; R = B_HEADS * tq; nsub = tk // sub

    def body(q_ref, k_ref, kT_ref, o_ref, do_ref, lse_ref, dq_ref, dk_ref, qT, dosT, dos, delta_s, dq_acc):
        i, j = pl.program_id(0), pl.program_id(1)

        @pl.when((i == 0) & (j == 0))
        def _():
            dk_ref[...] = jnp.zeros((S, 2 * LANES), F32)

        @pl.when(j == 0)
        def _():
            qT[...] = q_ref[...].reshape(R, 2 * LANES).astype(F32).T.astype(BF16)
            dov = do_ref[...].reshape(R, LANES).astype(F32)
            dos[...] = dov.astype(BF16)
            dosT[...] = dov.T.astype(BF16)
            delta_s[...] = jnp.sum((dov * o_ref[...].reshape(R, LANES).astype(F32)).T, axis=0, keepdims=True)
            dq_acc[...] = jnp.zeros((2 * LANES, R), F32)

        qTv, dosTv, dosv = qT[...], dosT[...], dos[...]
        qv = q_ref[...].reshape(R, 2 * LANES)
        lse_v, delta_v = lse_ref[0] * LOG2E, delta_s[...]
        dqa = dq_acc[...]
        s_cur = _dot(k_ref[0:sub, :], qTv)
        dp_cur = _dot(k_ref[0:sub, 0:LANES], dosTv)
        for t in range(nsub):
            if t + 1 < nsub:
                s_next = _dot(k_ref[sub * (t + 1):sub * (t + 2), :], qTv)
                dp_next = _dot(k_ref[sub * (t + 1):sub * (t + 2), 0:LANES], dosTv)
            p = jnp.exp2(s_cur * MLA_C - lse_v)
            ds = (p * (dp_cur - delta_v) * MLA_SCALE).astype(BF16)
            rows = pl.ds(pl.multiple_of(j * tk + sub * t, sub), sub)
            dk_ref[rows, :] += _dot(ds, qv)
            dk_ref[rows, 0:LANES] += _dot(p.astype(BF16), dosv)
            dqa = dqa + _dot(kT_ref[:, sub * t:sub * (t + 1)], ds)
            if t + 1 < nsub:
                s_cur, dp_cur = s_next, dp_next
        dq_acc[...] = dqa

        @pl.when(j == nkv - 1)
        def _():
            dq_ref[...] = dq_acc[...].T.reshape(B_HEADS, tq, 2 * LANES)

    hspec = lambda w: pl.BlockSpec((B_HEADS, tq, w), lambda i, j: (0, i, 0))
    return pl.pallas_call(
        body, name="mla_bwd", grid=(nq, nkv),
        in_specs=[hspec(2 * LANES), pl.BlockSpec((tk, 2 * LANES), lambda i, j: (j, 0)), pl.BlockSpec((2 * LANES, tk), lambda i, j: (0, j)),
                  hspec(LANES), hspec(LANES), pl.BlockSpec((1, 1, R), lambda i, j: (i, 0, 0))],
        out_specs=[hspec(2 * LANES), pl.BlockSpec((S, 2 * LANES), lambda i, j: (0, 0))],
        out_shape=[_sds((B_HEADS, S, 2 * LANES), F32), _sds((S, 2 * LANES), F32)],
        scratch_shapes=[pltpu.VMEM((2 * LANES, R), BF16), pltpu.VMEM((LANES, R), BF16), pltpu.VMEM((R, LANES), BF16),
                        pltpu.VMEM((1, R), F32), pltpu.VMEM((2 * LANES, R), F32)],
        compiler_params=_params(("arbitrary", "arbitrary")))(q, kcat, kcatT, o, do, lse)


def _win_start(i, tq, nk, S):
    return pl.multiple_of(jnp.clip(i * tq - WINDOW, 0, S - nk), LANES)


def _win_dist_table(S, tq):
    nk = min(tq + 2 * WINDOW, S)
    nq = S // tq
    r = np.arange(nk)[:, None]
    c = (np.arange(2 * tq) % tq)[None, :]
    tabs = []
    for rel in (0, WINDOW, (nq - 1) * tq - (S - nk)):
        dist = np.abs(rel + c - r).astype(np.float32)
        tabs.append(np.where(dist <= WINDOW, dist, np.float32(1e32)))
    return jnp.asarray(np.stack(tabs))


def _win_dist_spec(nk, tq, nq):
    return pl.BlockSpec((1, nk, 2 * tq), lambda b, i: (jnp.where(i == 0, 0, jnp.where(i == nq - 1, 2, 1)), 0, 0))


def _win_fwd(q, k, vT, dist, slope, sink, *, kdiv, tq, nbs, name):
    S = k.shape[0]; nb = q.shape[1] // LANES; nq = S // tq; nk = min(tq + 2 * WINDOW, S)
    assert nb % nbs == 0 and nbs % kdiv == 0
    kvw = (nbs // kdiv) * LANES

    def body(q_ref, k_ref, vT_ref, dist_ref, slope_ref, sink_ref, o_ref, lse_ref):
        i = pl.program_id(1)
        rlo = _row_lo()
        k0 = _win_start(i, tq, nk, S)
        kk, vv, dd = k_ref[pl.ds(k0, nk), :], vT_ref[:, pl.ds(k0, nk)], dist_ref[0]
        for u in range(nbs):
            kv = slice(LANES * (u // kdiv), LANES * (u // kdiv + 1))
            qsT = _stack_cols(q_ref[:, LANES * u:LANES * (u + 1)].astype(F32).T, rlo).astype(BF16)
            s = _dot(kk[:, kv], qsT) - slope_ref[u] * dd
            sk = sink_ref[u]
            m = jnp.maximum(jnp.max(s, axis=0, keepdims=True), sk)
            p = jnp.exp(s - m)
            l = jnp.sum(p, axis=0, keepdims=True) + jnp.exp(sk - m)
            o_ref[:, LANES * u:LANES * (u + 1)] = _pick_halves_T(_dot(vv[kv, :], p.astype(BF16)) / l, rlo, tq).astype(BF16)
            lse_ref[u, 0] = m + jnp.log(l)

    row_spec = pl.BlockSpec((nbs, 1, 2 * tq), lambda b, i: (b, 0, 0))
    return pl.pallas_call(
        body, name=name, grid=(nb // nbs, nq),
        in_specs=[pl.BlockSpec((tq, nbs * LANES), lambda b, i: (i, b)), pl.BlockSpec((S, kvw), lambda b, i: (0, b)),
                  pl.BlockSpec((kvw, S), lambda b, i: (b, 0)), _win_dist_spec(nk, tq, nq), row_spec, row_spec],
        out_specs=[pl.BlockSpec((tq, nbs * LANES), lambda b, i: (i, b)), pl.BlockSpec((nbs, 1, 1, 2 * tq), lambda b, i: (b, i, 0, 0))],
        out_shape=[_sds((S, nb * LANES), BF16), _sds((nb, nq, 1, 2 * tq), F32)],
        compiler_params=_params(("arbitrary", "arbitrary")))(q, k, vT, dist, slope, sink)


def _win_bwd(q, k, kT, v, o, do, lse, dist, slope, sink, *, kdiv, tq, nbs, name):
    S = k.shape[0]; nb = q.shape[1] // LANES; nkb = k.shape[1] // LANES; nq = S // tq; nk = min(tq + 2 * WINDOW, S)
    assert nb % nbs == 0 and nbs % kdiv == 0
    nkv = nbs // kdiv
    kvw = nkv * LANES

    def body(q_ref, k_ref, kT_ref, v_ref, o_ref, do_ref, lse_ref, dist_ref, slope_ref, sink_ref, dq_ref, dk_ref, dv_ref, dsink_ref, ds_acc):
        i = pl.program_id(1)
        rlo = _row_lo()
        lo = lax.broadcasted_iota(jnp.int32, (1, LANES), 1) < HEAD_DIM

        @pl.when(i == 0)
        def _():
            dk_ref[...] = jnp.zeros((S, kvw), F32)
            dv_ref[...] = jnp.zeros((S, kvw), F32)
            ds_acc[...] = jnp.zeros((nbs, 2 * tq), F32)

        k0 = _win_start(i, tq, nk, S)
        rows = pl.ds(k0, nk)
        kk_all, vv_all, kkT_all, dd = k_ref[rows, :], v_ref[rows, :], kT_ref[:, rows], dist_ref[0]
        dv_sum, dk_sum = [None] * nkv, [None] * nkv
        for u in range(nbs):
            g = u // kdiv
            kv = slice(LANES * g, LANES * (g + 1))
            kk, vv, kkT = kk_all[:, kv], vv_all[:, kv], kkT_all[kv, :]
            cols = slice(LANES * u, LANES * (u + 1))
            qv = q_ref[:, cols]
            qs = _stack_rows(qv, lo)
            qsT = _stack_cols(qv.astype(F32).T, rlo).astype(BF16)
            dov = do_ref[:, cols].astype(F32)
            dos = _stack_rows(dov.astype(BF16), lo)
            dosT = _stack_cols(dov.T, rlo).astype(BF16)
            prodT = (dov * o_ref[:, cols].astype(F32)).T
            delta = jnp.concatenate([jnp.sum(jnp.where(rlo, prodT, 0.0), axis=0, keepdims=True),
                                     jnp.sum(jnp.where(rlo, 0.0, prodT), axis=0, keepdims=True)], axis=1)
            lse_v = lse_ref[u, 0]
            ds_acc[u:u + 1, :] += -jnp.exp(sink_ref[u] - lse_v) * delta
            p = jnp.exp(_dot(kk, qsT) - slope_ref[u] * dd - lse_v)
            ds = (p * (_dot(vv, dosT) - delta)).astype(BF16)
            dv_u, dk_u = _dot(p.astype(BF16), dos), _dot(ds, qs)
            dv_sum[g] = dv_u if dv_sum[g] is None else dv_sum[g] + dv_u
            dk_sum[g] = dk_u if dk_sum[g] is None else dk_sum[g] + dk_u
            dq_ref[:, cols] = (_pick_halves_T(_dot(kkT, ds), rlo, tq) * 0.125).astype(BF16)
        dv_ref[rows, :] += jnp.concatenate(dv_sum, axis=1)
        dk_ref[rows, :] += jnp.concatenate(dk_sum, axis=1)

        @pl.when(i == nq - 1)
        def _():
            acc = ds_acc[...]
            for u in range(nbs):
                dsink_ref[u] = jnp.concatenate(
                    [jnp.broadcast_to(jnp.sum(acc[u:u + 1, 0:tq], axis=1, keepdims=True), (1, LANES)),
                     jnp.broadcast_to(jnp.sum(acc[u:u + 1, tq:2 * tq], axis=1, keepdims=True), (1, LANES)),
                     jnp.zeros((6, LANES), F32)], axis=0)

    qmap = lambda b, i: (i, b)
    kv_spec = pl.BlockSpec((S, kvw), lambda b, i: (0, b))
    row_spec = pl.BlockSpec((nbs, 1, 2 * tq), lambda b, i: (b, 0, 0))
    wide = pl.BlockSpec((tq, nbs * LANES), qmap)
    return pl.pallas_call(
        body, name=name, grid=(nb // nbs, nq),
        in_specs=[wide, kv_spec, pl.BlockSpec((kvw, S), lambda b, i: (b, 0)), kv_spec, wide, wide,
                  pl.BlockSpec((nbs, 1, 1, 2 * tq), lambda b, i: (b, i, 0, 0)), _win_dist_spec(nk, tq, nq), row_spec, row_spec],
        out_specs=[wide, kv_spec, kv_spec, pl.BlockSpec((nbs, 8, LANES), lambda b, i: (b, 0, 0))],
        out_shape=[_sds((S, nb * LANES), BF16), _sds((S, nkb * LANES), F32), _sds((S, nkb * LANES), F32), _sds((nb, 8, LANES), F32)],
        scratch_shapes=[pltpu.VMEM((nbs, 2 * tq), F32)],
        compiler_params=_params(("arbitrary", "arbitrary")))(q, k, kT, v, o, do, lse, dist, slope, sink)


def _sum_rows(v):
    return jnp.sum(v, axis=0, keepdims=True)


def _norm_mod_bwd(dh, xv, mod_ref, nw_ref, stats_ref):
    r = _rms(xv)
    xn = xv * r
    nw = nw_ref[...]
    stats_ref[0:1, :] += _sum_rows(dh)
    stats_ref[1:2, :] += _sum_rows(dh * (xn * nw))
    dn = dh * (1.0 + mod_ref[1:2, :])
    stats_ref[2:3, :] += _sum_rows(dn * xn)
    return _rms_bwd(xv, r, dn * nw)


def _even_gate_specs(ts):
    return [pl.BlockSpec((ts, 256), lambda i, c=c: (i, c)) for c in (3, 4, 7, 8)]


def _even_post_fwd(oa, olat, proj, x, gate, wuv, woe):
    S = x.shape[0]
    ts = min(ROW_TILE, S)

    def body(oa_ref, ol_ref, ga0_ref, ga1_ref, gb0_ref, gb1_ref, x_ref, gate_ref, wuv_ref, woe_ref, y_ref, x1_ref):
        sa, _ = _silu_and_grad(jnp.concatenate([ga0_ref[...], ga1_ref[...]], axis=1))
        sb, _ = _silu_and_grad(jnp.concatenate([gb0_ref[...], gb1_ref[...]], axis=1))
        olc = jnp.concatenate([ol_ref[hh] for hh in range(B_HEADS)], axis=1).astype(BF16)
        ob = _dot(olc, wuv_ref[...])
        mix = jnp.concatenate([oa_ref[...] * sa, ob * sb], axis=1).astype(BF16)
        y = _dot(mix, woe_ref[...])
        y_ref[...] = y.astype(BF16)
        x1_ref[...] = x_ref[...] + gate_ref[...] * y

    return pl.pallas_call(
        body, name="even_post_fwd", grid=(S // ts,),
        in_specs=[_row_spec(ts, 512), pl.BlockSpec((B_HEADS, ts, LANES), lambda i: (0, i, 0))] + _even_gate_specs(ts) +
                 [_row_spec(ts, D_MODEL), _full_spec((1, D_MODEL)), _full_spec((1024, 512)), _full_spec((1024, D_MODEL))],
        out_specs=[_row_spec(ts, D_MODEL), _row_spec(ts, D_MODEL)],
        out_shape=[_sds((S, D_MODEL), BF16), _sds((S, D_MODEL), F32)],
        compiler_params=_params(("arbitrary",)),
    )(oa, olat, proj, proj, proj, proj, x, gate, wuv, woe)


def _odd_pre_fwd(x, mod, nw, wio):
    S = x.shape[0]
    ts = min(ROW_TILE, S)

    def body(x_ref, mod_ref, nw_ref, wio_ref, h_ref, proj_ref, q_ref, k_ref, v_ref, kt_ref, vt_ref):
        xv = x_ref[...]
        h = (xv * _rms(xv) * nw_ref[...]) * (1.0 + mod_ref[1:2, :]) + mod_ref[0:1, :]
        hb = h.astype(BF16)
        h_ref[...] = hb
        proj = jnp.concatenate([_dot(hb, wio_ref[p]) for p in range(N_CHIPS)], axis=1)
        proj_ref[...] = proj
        q_ref[...] = (proj[:, 0:1024] * 0.125).astype(BF16)
        lane = _lane_iota()
        k_v = jnp.concatenate([_dup_heads(proj[:, 1024 + LANES * j:1024 + LANES * (j + 1)], lane) for j in range(2)], axis=1)
        v_v = jnp.concatenate([_dup_heads(proj[:, 1280 + LANES * j:1280 + LANES * (j + 1)], lane) for j in range(2)], axis=1)
        k_ref[...] = k_v.astype(BF16)
        v_ref[...] = v_v.astype(BF16)
        kt_ref[...] = k_v.T.astype(BF16)
        vt_ref[...] = v_v.T.astype(BF16)

    col_spec = pl.BlockSpec((512, ts), lambda i: (0, i))
    return pl.pallas_call(
        body, name="odd_pre_fwd", grid=(S // ts,),
        in_specs=[_row_spec(ts, D_MODEL), _full_spec((3, D_MODEL)), _full_spec((1, D_MODEL)),
                  _full_spec((N_CHIPS, D_MODEL, ODD_IN // N_CHIPS))],
        out_specs=[_row_spec(ts, D_MODEL), _row_spec(ts, ODD_IN), _row_spec(ts, 1024), _row_spec(ts, 512), _row_spec(ts, 512),
                   col_spec, col_spec],
        out_shape=[_sds((S, D_MODEL), BF16), _sds((S, ODD_IN), F32), _sds((S, 1024), BF16), _sds((S, 512), BF16),
                   _sds((S, 512), BF16), _sds((512, S), BF16), _sds((512, S), BF16)],
        compiler_params=_params(("arbitrary",)),
    )(x, mod, nw, wio)


def _odd_post(oc, proj, x1, gate, woo, fw, tgt):
    S = x1.shape[0]
    ts = min(ROW_TILE, S)
    nsteps = S // ts

    def body(oc_ref, g0_ref, g1_ref, x_ref, gate_ref, woo_ref, fw_ref, tgt_ref, doc_ref, dgc_ref, dx2_ref, dwoo_out, stats_ref,
             dwoo_ref):
        @pl.when(pl.program_id(0) == 0)
        def _():
            dwoo_ref[...] = jnp.zeros((D_MODEL, D_MODEL), F32)
            stats_ref[...] = jnp.zeros((8, D_MODEL), F32)

        ocv = oc_ref[...]
        sg, dsg = _silu_and_grad(jnp.concatenate([g0_ref[...], g1_ref[...]], axis=1))
        mix = (ocv * sg).astype(BF16)
        woo_v = woo_ref[...]
        y = _dot(mix, woo_v)
        gate_v = gate_ref[...]
        x2 = x_ref[...] + gate_v * y
        r = _rms(x2)
        fw_v = fw_ref[...]
        xn = x2 * r
        err = xn * fw_v - tgt_ref[...]
        dout = err * (1.0 / D_MODEL)
        dx2 = _rms_bwd(x2, r, dout * fw_v)
        dx2_ref[...] = dx2
        stats_ref[0:1, :] += _sum_rows(dout * xn)
        stats_ref[1:2, :] += _sum_rows(dx2 * y)
        loss_t = 0.5 * jnp.sum(_sum_rows(err * dout), axis=-1, keepdims=True)
        stats_ref[2:3, :] += jnp.broadcast_to(loss_t, (1, D_MODEL))
        dy = (gate_v * dx2).astype(BF16)
        dmix = _dot_nt(dy, woo_v)
        dwoo_ref[...] += _dot_tn(mix, dy)
        doc_ref[...] = (dmix * sg).astype(BF16)
        dgc_ref[...] = (dmix * ocv * dsg).astype(BF16)

        @pl.when(pl.program_id(0) == nsteps - 1)
        def _():
            dwoo_out[...] = dwoo_ref[...].astype(BF16)

    gate_cols = [pl.BlockSpec((ts, 512), lambda i, c=c: (i, c)) for c in (3, 4)]
    return pl.pallas_call(
        body, name="odd_post", grid=(nsteps,),
        in_specs=[_row_spec(ts, D_MODEL)] + gate_cols + [_row_spec(ts, D_MODEL), _full_spec((1, D_MODEL)),
                  _full_spec((D_MODEL, D_MODEL)), _full_spec((1, D_MODEL)), _row_spec(ts, D_MODEL)],
        out_specs=[_row_spec(ts, D_MODEL), _row_spec(ts, D_MODEL), _row_spec(ts, D_MODEL),
                   _full_spec((D_MODEL, D_MODEL), single=False), _full_spec((8, D_MODEL), single=False)],
        out_shape=[_sds((S, D_MODEL), BF16), _sds((S, D_MODEL), BF16), _sds((S, D_MODEL), F32), _sds((D_MODEL, D_MODEL), BF16),
                   _sds((8, D_MODEL), F32)],
        scratch_shapes=[pltpu.VMEM((D_MODEL, D_MODEL), F32)],
        compiler_params=_params(("arbitrary",)),
    )(oc, proj, proj, x1, gate, woo, fw, tgt)


def _odd_pre_bwd(dq, dk, dv, dgc, h, x, dx_res, mod, nw, wio):
    S = x.shape[0]
    ts = min(IN_PROJ_ROW_TILE, S)
    nsteps = S // ts
    wsh = ODD_IN // N_CHIPS

    def body(dq_ref, dk_ref, dv_ref, dgc_ref, h_ref, x_ref, dxr_ref, mod_ref, nw_ref, wio_ref, dx_ref, dw_ref, stats_ref, dw_acc):
        @pl.when(pl.program_id(0) == 0)
        def _():
            dw_acc[...] = jnp.zeros((N_CHIPS, D_MODEL, wsh), F32)
            stats_ref[...] = jnp.zeros((8, D_MODEL), F32)

        lane = _lane_iota()
        dkv = [_fold_heads(r[:, 2 * LANES * j:2 * LANES * (j + 1)], lane).astype(BF16) for r in (dk_ref, dv_ref) for j in range(2)]
        dproj = jnp.concatenate([dq_ref[...]] + dkv + [dgc_ref[...]], axis=1)
        hv = h_ref[...]
        dh = None
        for p in range(N_CHIPS):
            dp_cols = dproj[:, wsh * p:wsh * (p + 1)]
            part = _dot_nt(dp_cols, wio_ref[p])
            dh = part if dh is None else dh + part
            dw_acc[p] += _dot_tn(hv, dp_cols)
        dx_ref[...] = dxr_ref[...] + _norm_mod_bwd(dh, x_ref[...], mod_ref, nw_ref, stats_ref)

        @pl.when(pl.program_id(0) == nsteps - 1)
        def _():
            dw_ref[...] = dw_acc[...].astype(BF16)

    return pl.pallas_call(
        body, name="odd_pre_bwd", grid=(nsteps,),
        in_specs=[_row_spec(ts, 1024), _row_spec(ts, 512), _row_spec(ts, 512), _row_spec(ts, 1024), _row_spec(ts, D_MODEL),
                  _row_spec(ts, D_MODEL), _row_spec(ts, D_MODEL), _full_spec((3, D_MODEL)), _full_spec((1, D_MODEL)),
                  _full_spec((N_CHIPS, D_MODEL, wsh))],
        out_specs=[_row_spec(ts, D_MODEL), _full_spec((N_CHIPS, D_MODEL, wsh), single=False), _full_spec((8, D_MODEL), single=False)],
        out_shape=[_sds((S, D_MODEL), F32), _sds((N_CHIPS, D_MODEL, wsh), BF16), _sds((8, D_MODEL), F32)],
        scratch_shapes=[pltpu.VMEM((N_CHIPS, D_MODEL, wsh), F32)],
        compiler_params=_params(("arbitrary",)),
    )(dq, dk, dv, dgc, h, x, dx_res, mod, nw, wio)


def _even_post_bwd(dx1, y, oa, olat, proj, gate, wuv, woe):
    S = dx1.shape[0]
    ts = min(ROW_TILE, S)
    nsteps = S // ts

    def body(dx_ref, y_ref, oa_ref, ol_ref, ga0_ref, ga1_ref, gb0_ref, gb1_ref, gate_ref, wuv_ref, woe_ref,
             doa_ref, dga_ref, dgb_ref, dol_ref, dwoe_out, dwuv_ref, stats_ref, dwoe_ref):
        @pl.when(pl.program_id(0) == 0)
        def _():
            dwoe_ref[...] = jnp.zeros((D_MODEL, D_MODEL), F32)
            dwuv_ref[...] = jnp.zeros((1024, 512), F32)
            stats_ref[...] = jnp.zeros((8, D_MODEL), F32)

        dxv = dx_ref[...]
        stats_ref[0:1, :] += _sum_rows(dxv * y_ref[...])
        dy = (gate_ref[...] * dxv).astype(BF16)
        sa, dsa = _silu_and_grad(jnp.concatenate([ga0_ref[...], ga1_ref[...]], axis=1))
        sb, dsb = _silu_and_grad(jnp.concatenate([gb0_ref[...], gb1_ref[...]], axis=1))
        olc = jnp.concatenate([ol_ref[hh] for hh in range(B_HEADS)], axis=1).astype(BF16)
        wuv_v = wuv_ref[...]
        ob = _dot(olc, wuv_v)
        oav = oa_ref[...]
        mix = jnp.concatenate([oav * sa, ob * sb], axis=1).astype(BF16)
        dmix = _dot_nt(dy, woe_ref[...])
        dwoe_ref[...] += _dot_tn(mix, dy)
        dma, dmb = dmix[:, 0:512], dmix[:, 512:1024]
        doa_ref[...] = (dma * sa).astype(BF16)
        dga_ref[...] = (dma * oav * dsa).astype(BF16)
        dgb_ref[...] = (dmb * ob * dsb).astype(BF16)
        dob = (dmb * sb).astype(BF16)
        dol = _dot_nt(dob, wuv_v)
        dwuv_ref[...] += _dot_tn(olc, dob)
        for hh in range(B_HEADS):
            dol_ref[hh] = dol[:, LANES * hh:LANES * (hh + 1)].astype(BF16)

        @pl.when(pl.program_id(0) == nsteps - 1)
        def _():
            dwoe_out[...] = dwoe_ref[...].astype(BF16)

    head_spec = pl.BlockSpec((B_HEADS, ts, LANES), lambda i: (0, i, 0))
    return pl.pallas_call(
        body, name="even_post_bwd", grid=(nsteps,),
        in_specs=[_row_spec(ts, D_MODEL), _row_spec(ts, D_MODEL), _row_spec(ts, 512), head_spec] + _even_gate_specs(ts) +
                 [_full_spec((1, D_MODEL)), _full_spec((1024, 512)), _full_spec((1024, D_MODEL))],
        out_specs=[_row_spec(ts, 512), _row_spec(ts, 512), _row_spec(ts, 512), head_spec,
                   _full_spec((D_MODEL, D_MODEL), single=False), _full_spec((1024, 512), single=False),
                   _full_spec((8, D_MODEL), single=False)],
        out_shape=[_sds((S, 512), BF16), _sds((S, 512), BF16), _sds((S, 512), BF16), _sds((B_HEADS, S, LANES), BF16),
                   _sds((D_MODEL, D_MODEL), BF16), _sds((1024, 512), F32), _sds((8, D_MODEL), F32)],
        scratch_shapes=[pltpu.VMEM((D_MODEL, D_MODEL), F32)],
        compiler_params=_params(("arbitrary",)),
    )(dx1, y, oa, olat, proj, proj, proj, proj, gate, wuv, woe)


def _even_pre_bwd(x, h, proj, dqa, dka, dva, dga, dgb, dqcat, dkcat, dx_res, mod, nw, wie, qn, kn, seg, ca, sa, ct, st,
                  qln, kvln, wuq, wuk):
    S = x.shape[0]
    ts = min(IN_PROJ_ROW_TILE, S)
    nsteps = S // ts

    def body(x_ref, h_ref, proj_ref, dqa_ref, dka_ref, dva_ref, dga_ref, dgb_ref, dqc_ref, dkc_ref, dxr_ref, mod_ref, nw_ref,
             wie_ref, qn_ref, kn_ref, seg_ref, ca_ref, sa_ref, ct_ref, st_ref, qln_ref, kvln_ref, wuq_ref, wuk_ref,
             dx_ref, dwie_out, dwuq_out, dwuk_out, stats_ref, nstats_ref, dwie_ref, dwuq_ref, dwuk_ref):
        @pl.when(pl.program_id(0) == 0)
        def _():
            dwie_ref[...] = jnp.zeros((D_MODEL, EVEN_P), F32)
            dwuq_ref[...] = jnp.zeros((B_Q_LORA, 1536), F32)
            dwuk_ref[...] = jnp.zeros((512, 1024), F32)
            stats_ref[...] = jnp.zeros((8, D_MODEL), F32)
            nstats_ref[...] = jnp.zeros((8, 256), F32)

        lane = _lane_iota()
        ca_v, sa_v, ct_v, st_v = ca_ref[...], sa_ref[...], ct_ref[...], st_ref[...]
        seg_v = seg_ref[...]

        def head_norm_bwd(xc, dy, w):
            r = lax.rsqrt(_seg_mean(xc * xc, seg_v) + EPS)
            g = dy * w
            dxc = r * g - xc * (r * r * r) * _seg_mean(xc * g, seg_v)
            return dxc, _sum_rows(dy * (xc * r))

        pieces = []
        dqn = jnp.zeros((1, LANES), F32)
        for cb in range(4):
            sl = slice(LANES * cb, LANES * (cb + 1))
            dy = _rot_bwd(dqa_ref[:, sl] * 0.125, ca_v, sa_v, lane)
            dxc, dw = head_norm_bwd(proj_ref[:, sl], dy, qn_ref[...])
            pieces.append(dxc)
            dqn = dqn + dw
        dxc, dkn = head_norm_bwd(proj_ref[:, 512:640], _rot_bwd(_fold_heads(dka_ref[...], lane), ca_v, sa_v, lane), kn_ref[...])
        pieces += [dxc, _fold_heads(dva_ref[...], lane), dga_ref[...]]
        nstats_ref[0:1, 0:LANES] += dqn + pltpu.roll(dqn, HEAD_DIM, 1)
        nstats_ref[1:2, 0:LANES] += dkn + pltpu.roll(dkn, HEAD_DIM, 1)

        cq = proj_ref[:, 1280:1536]
        rq = _rms(cq)
        cqn_f = cq * rq
        qln_v = qln_ref[...]
        cqn = (cqn_f * qln_v).astype(BF16)
        wuq_v, wuk_v = wuq_ref[...], wuk_ref[...]
        qnope = _dot(cqn, wuq_v[:, 0:512]).astype(BF16)
        dqlat = jnp.concatenate([dqc_ref[hh, :, 0:LANES] for hh in range(B_HEADS)], axis=1).astype(BF16)
        dqnope = _dot_nt(dqlat, wuk_v)
        dwuk_ref[...] += _dot_tn(qnope, dqlat)
        dqr = [_rot_bwd(dqc_ref[hh, :, LANES:2 * LANES], ct_v, st_v, lane) for hh in range(B_HEADS)]
        dqb = jnp.concatenate([dqnope] + dqr, axis=1).astype(BF16)
        dcqn = _dot_nt(dqb, wuq_v)
        dwuq_ref[...] += _dot_tn(cqn, dqb)
        nstats_ref[2:3, :] += _sum_rows(dcqn * cqn_f)
        dcq = _rms_bwd(cq, rq, dcqn * qln_v)
        ckv = proj_ref[:, 1536:1664]
        rk = _rms(ckv)
        dckvn = dkc_ref[:, 0:LANES]
        nstats_ref[3:4, 0:LANES] += _sum_rows(dckvn * (ckv * rk))
        dckv = _rms_bwd(ckv, rk, dckvn * kvln_ref[...])
        dkr = _rot_bwd(dkc_ref[:, LANES:2 * LANES], ct_v, st_v, lane)
        pieces += [dcq, dckv, dkr, dgb_ref[...]]
        dproj = jnp.concatenate([piece.astype(BF16) for piece in pieces], axis=1)
        dh = _dot_nt(dproj, wie_ref[...])
        dwie_ref[...] += _dot_tn(h_ref[...], dproj)
        dx_ref[...] = dxr_ref[...] + _norm_mod_bwd(dh, x_ref[...], mod_ref, nw_ref, stats_ref)

        @pl.when(pl.program_id(0) == nsteps - 1)
        def _():
            pltpu.sync_copy(dwie_ref, dwie_out)
            pltpu.sync_copy(dwuq_ref, dwuq_out)
            pltpu.sync_copy(dwuk_ref, dwuk_out)

    return pl.pallas_call(
        body, name="even_pre_bwd", grid=(nsteps,),
        in_specs=[_row_spec(ts, D_MODEL), _row_spec(ts, D_MODEL), _row_spec(ts, EVEN_P), _row_spec(ts, 512), _row_spec(ts, 2 * LANES),
                  _row_spec(ts, 2 * LANES), _row_spec(ts, 512), _row_spec(ts, 512),
                  pl.BlockSpec((B_HEADS, ts, 2 * LANES), lambda i: (0, i, 0)), _row_spec(ts, 2 * LANES), _row_spec(ts, D_MODEL),
                  _full_spec((3, D_MODEL)), _full_spec((1, D_MODEL)), _full_spec((D_MODEL, EVEN_P)),
                  _full_spec((1, LANES)), _full_spec((1, LANES)), _full_spec((LANES, LANES)),
                  _row_spec(ts, LANES), _row_spec(ts, LANES), _row_spec(ts, LANES), _row_spec(ts, LANES),
                  _full_spec((1, B_Q_LORA)), _full_spec((1, B_KV_LORA)), _full_spec((B_Q_LORA, 1536)), _full_spec((512, 1024))],
        out_specs=[_row_spec(ts, D_MODEL), _ANY, _ANY, _ANY, _full_spec((8, D_MODEL), single=False), _full_spec((8, 256), single=False)],
        out_shape=[_sds((S, D_MODEL), F32), _sds((D_MODEL, EVEN_P), F32), _sds((B_Q_LORA, 1536), F32), _sds((512, 1024), F32),
                   _sds((8, D_MODEL), F32), _sds((8, 256), F32)],
        scratch_shapes=[pltpu.VMEM((D_MODEL, EVEN_P), F32), pltpu.VMEM((B_Q_LORA, 1536), F32), pltpu.VMEM((512, 1024), F32)],
        compiler_params=_params(("arbitrary",)),
    )(x, h, proj, dqa, dka, dva, dga, dgb, dqcat, dkcat, dx_res, mod, nw, wie, qn, kn, seg, ca, sa, ct, st, qln, kvln, wuq, wuk)


def _ada_fwd(c_all, w, b):
    n = w.shape[2]

    def body(c_ref, w_ref, b_ref, o_ref):
        cv = c_ref[...]
        o_ref[0] = _dot_f32(cv * _sigmoid(cv), w_ref[0]) + b_ref[0]

    return pl.pallas_call(
        body, name="ada_fwd", grid=(2,),
        in_specs=[pl.BlockSpec((N_DEV, D_MODEL), lambda l: (0, 0)), pl.BlockSpec((1, D_MODEL, n), lambda l: (l, 0, 0)),
                  pl.BlockSpec((1, 1, n), lambda l: (l, 0, 0))],
        out_specs=pl.BlockSpec((1, N_DEV, n), lambda l: (l, 0, 0)),
        out_shape=_sds((2, N_DEV, n), F32),
        compiler_params=_params(("arbitrary",)),
    )(c_all, w, b)


def _ada_bwd(c_all_t, dmod):
    n = dmod.shape[2]

    def body(c_ref, d_ref, o_ref):
        cv = c_ref[...]
        act = cv * _sigmoid(cv)
        dv = d_ref[0]
        acc = act[:, 0:1] * dv[0:1, :]
        for bb in range(1, N_DEV):
            acc = acc + act[:, bb:bb + 1] * dv[bb:bb + 1, :]
        o_ref[0] = acc

    return pl.pallas_call(
        body, name="ada_bwd", grid=(2,),
        in_specs=[pl.BlockSpec((D_MODEL, N_DEV), lambda l: (0, 0)), pl.BlockSpec((1, N_DEV, n), lambda l: (l, 0, 0))],
        out_specs=pl.BlockSpec((1, D_MODEL, n), lambda l: (l, 0, 0)),
        out_shape=_sds((2, D_MODEL, n), F32),
        compiler_params=_params(("arbitrary",)),
    )(c_all_t, dmod)


ADAM_ROW_TILE = 256


def _adam_update(g, w, m, v):
    m_new = ADAM_B1 * m + (1.0 - ADAM_B1) * g
    v_new = ADAM_B2 * v + (1.0 - ADAM_B2) * jnp.square(g)
    m_hat = m_new / (1.0 - ADAM_B1 ** ADAM_STEP)
    v_hat = v_new / (1.0 - ADAM_B2 ** ADAM_STEP)
    return -ADAM_LR * (m_hat / (jnp.sqrt(v_hat) + ADAM_EPS) + ADAM_WD * w), m_new, v_new


SMALL_ROWS = dict(dmod=(0, D_MODEL), norm_w=(6, D_MODEL), final_norm=(8, D_MODEL), a_q_norm=(9, HEAD_DIM), a_k_norm=(10, HEAD_DIM),
                  b_q_lora_norm=(11, B_Q_LORA), b_kv_lora_norm=(12, B_KV_LORA), c_sink=(13, C_HEADS))
SMALL_WEIGHTS = ("ada_b", "norm_w", "final_norm", "a_q_norm", "a_k_norm", "b_q_lora_norm", "b_kv_lora_norm", "c_sink")
LOSS_ROW = 14


def _pack_small(res):
    def padded(v):
        return jnp.concatenate([v, jnp.zeros((v.shape[0], D_MODEL - v.shape[1]), F32)], axis=1)

    rows = [res["dmod"].reshape(6, D_MODEL), res["norm_w"], res["final_norm"].reshape(1, D_MODEL)]
    rows += [padded(res[k]) for k in ("a_q_norm", "a_k_norm", "b_q_lora_norm", "b_kv_lora_norm", "c_sink")]
    return jnp.concatenate(rows + [res["loss_row"], jnp.zeros((1, D_MODEL), F32)], axis=0)


def _adam_small(parts, ws, ms, vs):
    nw = len(SMALL_WEIGHTS)

    def body(*refs):
        p_ref = refs[0]
        w_refs, m_refs, v_refs = refs[1:1 + nw], refs[1 + nw:1 + 2 * nw], refs[1 + 2 * nw:1 + 3 * nw]
        outs = refs[1 + 3 * nw:]
        g_all = p_ref[0]
        for k in range(1, N_DEV):
            g_all = g_all + p_ref[k]
        for idx, name in enumerate(SMALL_WEIGHTS):
            if name == "ada_b":
                g = jnp.concatenate([jnp.concatenate([g_all[3 * l + t:3 * l + t + 1] for t in range(3)], axis=1) for l in range(2)],
                                    axis=0)
            else:
                row, width = SMALL_ROWS[name]
                g = g_all[row:row + w_refs[idx].shape[0], 0:width]
            d, m_new, v_new = _adam_update(g, w_refs[idx][...], m_refs[idx][...], v_refs[idx][...])
            outs[4 * idx][...], outs[4 * idx + 1][...], outs[4 * idx + 2][...], outs[4 * idx + 3][...] = g, d, m_new, v_new
        outs[4 * nw][...] = g_all[LOSS_ROW:LOSS_ROW + 1, 0:LANES]

    out_shape = []
    for w in ws:
        out_shape += [_sds(w.shape, F32)] * 4
    out_shape.append(_sds((1, LANES), F32))
    return pl.pallas_call(body, name="adam_small", out_shape=out_shape,
                          compiler_params=pltpu.CompilerParams(vmem_limit_bytes=VMEM_LIMIT))(parts, *ws, *ms, *vs)


def _adam(parts, w, m, v, name):
    P, R, C = parts.shape
    tr = R if R <= ADAM_ROW_TILE else ADAM_ROW_TILE
    assert R % tr == 0

    def body(p_ref, w_ref, m_ref, v_ref, g_ref, d_ref, nm_ref, nv_ref):
        g = p_ref[0].astype(F32)
        for k in range(1, P):
            g = g + p_ref[k].astype(F32)
        g_ref[...] = g
        d_ref[...], nm_ref[...], nv_ref[...] = _adam_update(g, w_ref[...], m_ref[...], v_ref[...])

    spec = pl.BlockSpec((tr, C), lambda i: (i, 0))
    return pl.pallas_call(
        body, name=name, grid=(R // tr,),
        in_specs=[pl.BlockSpec((P, tr, C), lambda i: (0, i, 0)), spec, spec, spec],
        out_specs=[spec, spec, spec, spec], out_shape=[_sds((R, C), F32)] * 4,
        compiler_params=_params(("arbitrary",)),
    )(parts, w, m, v)


_ANY = pl.BlockSpec(memory_space=pl.ANY)
CHIP_FLIPS = ((1, 0), (0, 1), (1, 1))
DEV_FLIPS = tuple((dx, dy, dc) for dx in (0, 1) for dy in (0, 1) for dc in (0, 1) if dx + dy + dc)


def _flip(a, d):
    return a if d == 0 else 1 - a


def _my_place():
    return lax.axis_index("x"), lax.axis_index("y"), lax.axis_index("c")


def _gather8_copies(ins, outs, send_sems, recv_sems, loc_sems):
    x, y, c = _my_place()
    me = 4 * x + 2 * y + c
    copies = []
    for a in range(len(ins)):
        copies.append(pltpu.make_async_copy(ins[a], outs[a].at[me], loc_sems.at[a]))
        for k, (dx, dy, dc) in enumerate(DEV_FLIPS):
            copies.append(pltpu.make_async_remote_copy(
                src_ref=ins[a], dst_ref=outs[a].at[me], send_sem=send_sems.at[a, k], recv_sem=recv_sems.at[a, k],
                device_id=(_flip(x, dx), _flip(y, dy), _flip(c, dc)), device_id_type=MESH_ID))
    return copies


def _gather8_sems(n):
    return [pltpu.SemaphoreType.DMA((n, 7)), pltpu.SemaphoreType.DMA((n, 7)), pltpu.SemaphoreType.DMA((n,))]


def _gather_dev8(arrs, name):
    n = len(arrs)

    def body(*refs):
        copies = _gather8_copies(refs[:n], refs[n:2 * n], *refs[2 * n:])
        for cp in copies:
            cp.start()
        for cp in copies:
            cp.wait()

    return pl.pallas_call(
        body, name=name, in_specs=[_ANY] * n, out_specs=[_ANY] * n,
        out_shape=[_sds((N_DEV,) + a.shape, a.dtype) for a in arrs], scratch_shapes=_gather8_sems(n),
    )(*arrs)


class _Exchange:
    def __init__(self, arrs, out_shapes, n_sems, phases):
        self.arrs, self.out_shapes, self.n_sems, self._phases = list(arrs), list(out_shapes), n_sems, phases

    @property
    def n(self):
        return len(self.arrs)

    def sem_shapes(self):
        return [pltpu.SemaphoreType.DMA((self.n, self.n_sems)), pltpu.SemaphoreType.DMA((self.n, self.n_sems)),
                pltpu.SemaphoreType.DMA((self.n,))]

    def phases(self, ins, outs, sems):
        return self._phases(ins, outs, *sems)

    def run(self, name):
        n = self.n

        def body(*refs):
            start, mid, end = self.phases(refs[:n], refs[n:2 * n], refs[2 * n:])
            start()
            mid()
            end()

        return pl.pallas_call(body, name=name, in_specs=[_ANY] * n, out_specs=[_ANY] * n, out_shape=self.out_shapes,
                              scratch_shapes=self.sem_shapes())(*self.arrs)

    def run_with_gather(self, gather_arrs, name):
        n, g = self.n, len(gather_arrs)

        def body(*refs):
            ins, g_ins, outs, g_outs = refs[:n], refs[n:n + g], refs[n + g:2 * n + g], refs[2 * n + g:2 * (n + g)]
            sems = refs[2 * (n + g):]
            start, mid, end = self.phases(ins, outs, sems[:3])
            copies = _gather8_copies(g_ins, g_outs, *sems[3:])
            start()
            for cp in copies:
                cp.start()
            mid()
            end()
            for cp in copies:
                cp.wait()

        outs = pl.pallas_call(
            body, name=name, in_specs=[_ANY] * (n + g), out_specs=[_ANY] * (n + g),
            out_shape=self.out_shapes + [_sds((N_DEV,) + a.shape, a.dtype) for a in gather_arrs],
            scratch_shapes=self.sem_shapes() + _gather8_sems(g))(*self.arrs, *gather_arrs)
        return outs[:n], outs[n:]


def _gather_halves_phases(ins, outs, send_sems, recv_sems, loc_sems):
    n = len(ins)
    x, y, c = _my_place()
    chip = 2 * x + y
    sibling = (x, y, 1 - c)
    peers = [(_flip(x, dx), _flip(y, dy)) for dx, dy in CHIP_FLIPS]

    def remote(src, p, half, a, k, to):
        return pltpu.make_async_remote_copy(src_ref=src, dst_ref=outs[a].at[p, half], send_sem=send_sems.at[a, k],
                                            recv_sem=recv_sems.at[a, k], device_id=to, device_id_type=MESH_ID)

    def local(a):
        return pltpu.make_async_copy(ins[a], outs[a].at[chip], loc_sems.at[a])

    def first(a, k):
        return remote(ins[a].at[c], chip, c, a, k, (*peers[k], c))

    def passed(a, k):
        p = 2 * peers[k][0] + peers[k][1]
        return remote(outs[a].at[p, c], p, c, a, 3 + k, sibling)

    def start():
        for a in range(n):
            local(a).start()
            for k in range(3):
                first(a, k).start()

    def mid():
        for a in range(n):
            for k in range(3):
                p = 2 * peers[k][0] + peers[k][1]
                remote(outs[a].at[p, c], p, c, a, k, sibling).wait_recv()
                passed(a, k).start()

    def end():
        for a in range(n):
            for k in range(3):
                p = 2 * peers[k][0] + peers[k][1]
                remote(outs[a].at[p, 1 - c], p, 1 - c, a, 3 + k, sibling).wait_recv()
        for a in range(n):
            for k in range(3):
                first(a, k).wait_send()
                passed(a, k).wait_send()
            local(a).wait()

    return start, mid, end


def _gather_chip4_halves(arrs):
    return _Exchange(arrs, [_sds((N_CHIPS,) + a.shape, a.dtype) for a in arrs], 6, _gather_halves_phases)


def _reduce_phases(ins, outs, send_sems, recv_sems, loc_sems):
    n = len(ins)
    x, y, c = _my_place()
    chip = 2 * x + y
    sibling = (x, y, 1 - c)
    peers = [(_flip(x, dx), _flip(y, dy)) for dx, dy in CHIP_FLIPS]

    def remote(src, slot, a, k, to):
        return pltpu.make_async_remote_copy(src_ref=src, dst_ref=outs[a].at[slot], send_sem=send_sems.at[a, k],
                                            recv_sem=recv_sems.at[a, k], device_id=to, device_id_type=MESH_ID)

    def local(a):
        return pltpu.make_async_copy(ins[a].at[chip], outs[a].at[2 * chip + c], loc_sems.at[a])

    def own(a):
        return remote(ins[a].at[chip], 2 * chip + c, a, 0, sibling)

    def first(a, k):
        return remote(ins[a].at[2 * peers[k][0] + peers[k][1]], 2 * chip + c, a, 1 + k, (*peers[k], c))

    def passed(a, k):
        slot = 2 * (2 * peers[k][0] + peers[k][1]) + c
        return remote(outs[a].at[slot], slot, a, 4 + k, sibling)

    def start():
        for a in range(n):
            local(a).start()
            own(a).start()
            for k in range(3):
                first(a, k).start()

    def mid():
        for a in range(n):
            for k in range(3):
                slot = 2 * (2 * peers[k][0] + peers[k][1]) + c
                remote(outs[a].at[slot], slot, a, 1 + k, sibling).wait_recv()
                passed(a, k).start()

    def end():
        for a in range(n):
            remote(outs[a].at[2 * chip + 1 - c], 2 * chip + 1 - c, a, 0, sibling).wait_recv()
            for k in range(3):
                slot = 2 * (2 * peers[k][0] + peers[k][1]) + 1 - c
                remote(outs[a].at[slot], slot, a, 4 + k, sibling).wait_recv()
        for a in range(n):
            own(a).wait_send()
            for k in range(3):
                first(a, k).wait_send()
                passed(a, k).wait_send()
            local(a).wait()

    return start, mid, end


def _reduce_exchange(arrs):
    return _Exchange(arrs, [_sds((N_DEV,) + a.shape[1:], a.dtype) for a in arrs], 7, _reduce_phases)


def _even_in_layout(w):
    return jnp.concatenate([w[:, 0:1696], jnp.zeros((w.shape[0], 96), w.dtype), w[:, 1696:2208]], axis=1)


def _even_in_unlayout(g):
    return jnp.concatenate([g[:, 0:1696], g[:, 1792:2304]], axis=1)


def _uq_layout(w):
    per = B_NOPE + B_ROPE
    pad = jnp.zeros((w.shape[0], LANES - B_ROPE), w.dtype)
    nope = [w[:, per * h:per * h + B_NOPE] for h in range(B_HEADS)]
    rope = [jnp.concatenate([w[:, per * h + B_NOPE:per * (h + 1)], pad], axis=1) for h in range(B_HEADS)]
    return jnp.concatenate(nope + rope, axis=1)


def _uq_unlayout(g):
    parts = []
    for h in range(B_HEADS):
        parts += [g[:, B_NOPE * h:B_NOPE * (h + 1)], g[:, 512 + LANES * h:512 + LANES * h + B_ROPE]]
    return jnp.concatenate(parts, axis=1)


def _block_diag(blocks):
    rows = []
    for h, blk in enumerate(blocks):
        r, cdim = blk.shape
        n = len(blocks)
        rows.append(jnp.concatenate([jnp.zeros((r, cdim * h), blk.dtype), blk, jnp.zeros((r, cdim * (n - 1 - h)), blk.dtype)],
                                    axis=1))
    return jnp.concatenate(rows, axis=0)


def _uk_layout(w):
    return _block_diag([w[:, h, :].T for h in range(B_HEADS)])


def _uk_unlayout(g):
    return jnp.stack([g[B_NOPE * h:B_NOPE * (h + 1), LANES * h:LANES * (h + 1)].T for h in range(B_HEADS)], axis=1)


def _uv_layout(w):
    return _block_diag([w[:, h, :] for h in range(B_HEADS)])


def _uv_unlayout(g):
    return jnp.stack([g[LANES * h:LANES * (h + 1), B_V * h:B_V * (h + 1)] for h in range(B_HEADS)], axis=1)


def _rope_tables(S):
    inv = ROPE_THETA ** (-jnp.arange(0, 32, 2, dtype=F32) / 32)
    tok = jnp.arange(S)

    def tab(pos):
        ang = pos.astype(F32)[:, None] * inv[None, :]
        cos, sin = jnp.cos(ang), jnp.sin(ang)
        return jnp.concatenate([cos, cos], axis=1), jnp.concatenate([-sin, sin], axis=1)

    cr, sr = tab(tok // GRID_W)
    cc, sc = tab(tok % GRID_W)
    ct, st = tab(tok)
    return (jnp.tile(jnp.concatenate([cr, cc], axis=1), (1, 2)), jnp.tile(jnp.concatenate([sr, sc], axis=1), (1, 2)),
            jnp.tile(ct, (1, 4)), jnp.tile(st, (1, 4)))


A_TQ, A_TK, A_SUB = 512, 4096, 512
B_TQ, B_TK, B_SUB = 128, 4096, 1024
B_BWD_TK, B_BWD_SUB = 4096, 512
C_T = 256
C_BLOCKS_PER_STEP = 8
KV_SHARE = 2


def _local_step(x0, tgt, mod, norm_w, wie, wuq, wuk, wuv, late_shards, a_q_norm, a_k_norm, q_lora_norm, kv_lora_norm,
                c_sink, final_norm):
    S = x0.shape[0]
    mod3 = mod.reshape(2, 3, D_MODEL)
    ca, sa, ct, st = _rope_tables(S)
    lane_seg = np.arange(LANES) // HEAD_DIM
    seg = jnp.asarray((lane_seg[:, None] == lane_seg[None, :]).astype(np.float32)).astype(BF16)
    qn = jnp.tile(a_q_norm.reshape(1, HEAD_DIM), (1, 2))
    kn = jnp.tile(a_k_norm.reshape(1, HEAD_DIM), (1, 2))
    qln, kvln = q_lora_norm.reshape(1, B_Q_LORA), kv_lora_norm.reshape(1, B_KV_LORA)
    nw0, nw1 = norm_w[0:1], norm_w[1:2]
    gate0, gate1 = mod3[0, 2:3], mod3[1, 2:3]
    a_tq, a_tk, b_tq, b_tk, bb_tk, c_t = min(A_TQ, S), min(A_TK, S), min(B_TQ, S), min(B_TK, S), min(B_BWD_TK, S), min(C_T, S)
    a_sub, b_sub, bb_sub = min(A_SUB, a_tk), min(B_SUB, b_tk), min(B_BWD_SUB, bb_tk)

    h0, proj_e, qa, ka, va, qcat, kcat, ka_t, va_t, kcat_t = _even_pre_fwd(x0, mod3[0], nw0, wie, qn, kn, seg, ca, sa, ct, st,
                                                                           qln, kvln, wuq, wuk)
    oa, lse_a, woe_g, wio_g, woo_g = _pp_fwd(qa, ka, va_t, kdiv=KV_SHARE, tq=a_tq, tk=a_tk, sub=a_sub, name="attn_a_fwd",
                                             side=_gather_chip4_halves(late_shards))
    woe = woe_g.reshape(D_MODEL, D_MODEL)
    wio = wio_g.reshape(N_CHIPS, D_MODEL, ODD_IN // N_CHIPS)
    woo = woo_g.reshape(D_MODEL, D_MODEL)
    olat, lse_b = _mla_fwd(qcat, kcat, kcat_t, tq=b_tq, tk=b_tk, sub=b_sub)
    y0, x1 = _even_post_fwd(oa, olat, proj_e, x0, gate0, wuv, woe)
    h1, proj_o, qc, kc, vc, kc_t, vc_t = _odd_pre_fwd(x1, mod3[1], nw1, wio)
    slopes = 2.0 ** (-8.0 * jnp.arange(1, C_HEADS + 1, dtype=F32) / C_HEADS)
    slope_rows = jnp.repeat(slopes.reshape(C_HEADS // 2, 2), c_t, axis=1)[:, None, :]
    sink_rows = jnp.repeat(c_sink.reshape(C_HEADS // 2, 2), c_t, axis=1)[:, None, :]
    win_dist = _win_dist_table(S, c_t)
    oc, lse_c = _win_fwd(qc, kc, vc_t, win_dist, slope_rows, sink_rows, kdiv=KV_SHARE, tq=c_t, nbs=C_BLOCKS_PER_STEP,
                         name="attn_c_fwd")
    doc, dgc, dx2, dwoo, st_f = _odd_post(oc, proj_o, x1, gate1, woo, final_norm.reshape(1, D_MODEL), tgt)
    dqc, dkc, dvc, dsink_raw = _win_bwd(qc, kc, kc_t, vc, oc, doc, lse_c, win_dist, slope_rows, sink_rows, kdiv=KV_SHARE, tq=c_t,
                                        nbs=C_BLOCKS_PER_STEP, name="attn_c_bwd")
    dx1, dwio, st_1 = _odd_pre_bwd(dqc, dkc, dvc, dgc, h1, x1, dx2, mod3[1], nw1, wio)
    doa, dga, dgb, dolat, dwoe, dwuv, st_e = _even_post_bwd(dx1, y0, oa, olat, proj_e, gate0, wuv, woe)
    late_grads = _reduce_exchange([dwoe.reshape(N_CHIPS, D_MODEL // N_CHIPS, D_MODEL), dwio,
                                   dwoo.reshape(N_CHIPS, D_MODEL // N_CHIPS, D_MODEL)])
    dqa, dka, dva, p_woe, p_wio, p_woo = _pp_bwd(qa, ka, ka_t, va, oa, doa, lse_a, kdiv=KV_SHARE, tq=a_tq, tk=a_tk, sub=a_sub,
                                                 name="attn_a_bwd", side=late_grads)
    dqcat, dkcat = _mla_bwd(qcat, kcat, kcat_t, olat, dolat, lse_b, tq=b_tq, tk=bb_tk, sub=bb_sub)
    dx0, dwie, dwuq, dwuk, st_0, nst = _even_pre_bwd(x0, h0, proj_e, dqa, dka, dva, dga, dgb, dqcat, dkcat, dx1, mod3[0], nw0,
                                                     wie, qn, kn, seg, ca, sa, ct, st, qln, kvln, wuq, wuk)
    dsink_pairs = jnp.stack([dsink_raw[:, 0, 0], dsink_raw[:, 1, 0]], axis=1).reshape(C_HEADS)
    return dict(
        loss_row=st_f[2:3], dx=dx0,
        dmod=jnp.stack([jnp.concatenate([st_0[0], st_0[1], st_e[0]]), jnp.concatenate([st_1[0], st_1[1], st_f[1]])]),
        norm_w=jnp.stack([st_0[2], st_1[2]]), final_norm=st_f[0],
        a_q_norm=nst[0:1, 0:HEAD_DIM], a_k_norm=nst[1:2, 0:HEAD_DIM], b_q_lora_norm=nst[2:3, :], b_kv_lora_norm=nst[3:4, 0:LANES],
        c_sink=dsink_pairs.reshape(1, C_HEADS),
        even_w_in=dwie, b_w_uq=dwuq, b_w_uk=dwuk, b_w_uv=dwuv, even_w_out=p_woe, odd_w_in=p_wio, odd_w_out=p_woo)


WEIGHT_NAMES = ("norm_w", "ada_w", "ada_b", "even_w_in", "a_q_norm", "a_k_norm", "b_q_lora_norm", "b_kv_lora_norm", "b_w_uq",
                "b_w_uk", "b_w_uv", "even_w_out", "odd_w_in", "c_sink", "odd_w_out", "final_norm")


def _cols_to_chips(g):
    r, n4 = g.shape
    return jnp.transpose(g.reshape(r, N_CHIPS, n4 // N_CHIPS), (1, 0, 2))


def _chips_to_cols(g):
    p, r, n = g.shape
    return jnp.transpose(g, (1, 0, 2)).reshape(r, p * n)


def kernel(x, c, norm_w, ada_w, ada_b, even_w_in, a_q_norm, a_k_norm, b_q_lora_norm, b_kv_lora_norm, b_w_uq, b_w_uk, b_w_uv, even_w_out, odd_w_in, c_sink, odd_w_out, final_norm, loss_target, m_norm_w, m_ada_w, m_ada_b, m_even_w_in, m_a_q_norm, m_a_k_norm, m_b_q_lora_norm, m_b_kv_lora_norm, m_b_w_uq, m_b_w_uk, m_b_w_uv, m_even_w_out, m_odd_w_in, m_c_sink, m_odd_w_out, m_final_norm, v_norm_w, v_ada_w, v_ada_b, v_even_w_in, v_a_q_norm, v_a_k_norm, v_b_q_lora_norm, v_b_kv_lora_norm, v_b_w_uq, v_b_w_uk, v_b_w_uv, v_even_w_out, v_odd_w_in, v_c_sink, v_odd_w_out, v_final_norm):
    given = dict(locals())
    xi, yi, ci = _my_place()
    chip = 2 * xi + yi
    dev = 2 * chip + ci
    n_ada = ada_w.shape[2]

    (c_all,) = _gather_dev8([c], "gather_c")
    c_all = c_all.reshape(N_DEV, D_MODEL)
    bias = lax.dynamic_slice_in_dim(ada_b, chip * n_ada, n_ada, axis=1).reshape(2, 1, n_ada)
    mod_cols = _ada_fwd(c_all, ada_w, bias)
    def halves(w):
        return w.astype(BF16).reshape((2, w.shape[0] // 2) + w.shape[1:])

    mod_all, wie_g, wuq_g = _gather_chip4_halves([mod_cols, halves(even_w_in[0]), halves(b_w_uq[0])]).run("gather_weights")
    wie_g = wie_g.reshape(N_CHIPS, D_MODEL, EVEN_IN // N_CHIPS)
    wuq_g = wuq_g.reshape(N_CHIPS, B_Q_LORA, -1)
    mod = jnp.transpose(lax.dynamic_index_in_dim(mod_all, dev, axis=2, keepdims=False), (1, 0, 2)).reshape(2, 3 * D_MODEL)

    res = _local_step(
        x[0], loss_target[0], mod, norm_w,
        _even_in_layout(_chips_to_cols(wie_g)), _uq_layout(_chips_to_cols(wuq_g)), _uk_layout(b_w_uk[0].astype(BF16)),
        _uv_layout(b_w_uv[0].astype(BF16)), [halves(even_w_out[0]), halves(odd_w_in[0]), halves(odd_w_out[0])],
        a_q_norm, a_k_norm, b_q_lora_norm, b_kv_lora_norm, c_sink, final_norm)

    latent = jnp.stack([_uk_unlayout(res["b_w_uk"]).reshape(B_KV_LORA, 512),
                        _uv_unlayout(res["b_w_uv"]).reshape(B_KV_LORA, 512)]).astype(BF16)
    (p_wie, p_wuq), (small_all, latent_all) = _reduce_exchange(
        [_cols_to_chips(_even_in_unlayout(res["even_w_in"].astype(BF16))), _cols_to_chips(_uq_unlayout(res["b_w_uq"].astype(BF16)))]
    ).run_with_gather([_pack_small(res), latent], "reduce_exchange")
    shard_parts = dict(even_w_in=p_wie, b_w_uq=p_wuq, **{k: res[k] for k in ("even_w_out", "odd_w_in", "odd_w_out")})
    dmod_all = small_all[:, 0:6, :].reshape(N_DEV, 2, 3 * D_MODEL)
    dmod_cols = jnp.transpose(lax.dynamic_slice_in_dim(dmod_all, chip * n_ada, n_ada, axis=2), (1, 0, 2))
    parts = dict(shard_parts)
    parts["ada_w"] = _ada_bwd(c_all.T, dmod_cols).reshape(1, 2 * D_MODEL, n_ada)
    parts["b_w_uk"], parts["b_w_uv"] = latent_all[:, 0], latent_all[:, 1]

    def as2d(a):
        return a.reshape((-1, a.shape[-1]) if a.ndim > 1 else (1, a.shape[0]))

    results = {}
    small_outs = _adam_small(small_all, *[[as2d(given[pre + k]) for k in SMALL_WEIGHTS] for pre in ("", "m_", "v_")])
    for idx, k in enumerate(SMALL_WEIGHTS):
        results[k] = small_outs[4 * idx:4 * idx + 4]
    for k, p in parts.items():
        shape2 = (p.shape[-2], p.shape[-1])
        results[k] = _adam(p, given[k].reshape(shape2), given["m_" + k].reshape(shape2), given["v_" + k].reshape(shape2),
                           "adam_" + k)
    by_kind = [[results[k][t].reshape(given[k].shape) for k in WEIGHT_NAMES] for t in range(4)]
    return (small_outs[-1][0, 0], res["dx"][None], *by_kind[0], *by_kind[1], *by_kind[2], *by_kind[3])
```

```python
import numpy as np
import jax
import jax.numpy as jnp
from jax import lax
from jax.experimental import pallas as pl
from jax.experimental.pallas import tpu as pltpu

F32 = jnp.float32
BF16 = jnp.bfloat16
HIGHEST = lax.Precision.HIGHEST
MESH_ID = pl.DeviceIdType.MESH

D_MODEL = 1024
HEAD_DIM = 64
GRID_W = 64
EPS = 1e-6
ROPE_THETA = 10000.0
B_HEADS, B_NOPE, B_ROPE, B_V = 8, 64, 32, 64
B_Q_LORA, B_KV_LORA = 256, 128
C_HEADS = 16
WINDOW = 128
EVEN_IN, ODD_IN = 2208, 2560
EVEN_P = 2304
N_CHIPS, N_DEV = 4, 8
LANES = 128
NEG = -1e30
VMEM_LIMIT = 60 * 1024 * 1024

ADAM_LR, ADAM_B1, ADAM_B2, ADAM_EPS, ADAM_WD, ADAM_STEP = 0.001, 0.9, 0.999, 1e-08, 0.01, 10

ROW_TILE = 512
IN_PROJ_ROW_TILE = 256


def _dot(a, b):
    return lax.dot_general(a, b, (((1,), (0,)), ((), ())), preferred_element_type=F32)


def _dot_nt(a, b):
    return lax.dot_general(a, b, (((1,), (1,)), ((), ())), preferred_element_type=F32)


def _dot_tn(a, b):
    return lax.dot_general(a, b, (((0,), (0,)), ((), ())), preferred_element_type=F32)


def _dot_f32(a, b):
    return lax.dot_general(a, b, (((1,), (0,)), ((), ())), precision=HIGHEST, preferred_element_type=F32)


def _sigmoid(x):
    return 1.0 / (1.0 + jnp.exp(-x))


def _silu_and_grad(g):
    s = _sigmoid(g)
    return g * s, s * (1.0 + g * (1.0 - s))


def _lane_iota():
    return lax.broadcasted_iota(jnp.int32, (1, LANES), 1)


def _partner(x, lane):
    return jnp.where((lane % 32) < 16, pltpu.roll(x, LANES - 16, 1), pltpu.roll(x, 16, 1))


def _rot(x, cos, sin_signed, lane):
    return x * cos + _partner(x, lane) * sin_signed


def _rot_bwd(dy, cos, sin_signed, lane):
    return dy * cos + _partner(dy * sin_signed, lane)


def _rms(x):
    return lax.rsqrt(jnp.mean(x * x, axis=-1, keepdims=True) + EPS)


def _rms_bwd(x, r, g):
    return r * g - x * (r * r * r) * jnp.mean(x * g, axis=-1, keepdims=True)


def _seg_mean(v, lane):
    lo = lane < HEAD_DIM
    s_lo = jnp.sum(jnp.where(lo, v, 0.0), axis=-1, keepdims=True)
    s_hi = jnp.sum(jnp.where(lo, 0.0, v), axis=-1, keepdims=True)
    return jnp.where(lo, s_lo, s_hi) * (1.0 / HEAD_DIM)


def _dup_heads(x, lane):
    swapped = pltpu.roll(x, HEAD_DIM, 1)
    lo = lane < HEAD_DIM
    return jnp.concatenate([jnp.where(lo, x, swapped), jnp.where(lo, swapped, x)], axis=1)


def _fold_heads(x2, lane):
    a, b = x2[:, 0:LANES], x2[:, LANES:2 * LANES]
    return jnp.where(lane < HEAD_DIM, a + pltpu.roll(a, HEAD_DIM, 1), b + pltpu.roll(b, HEAD_DIM, 1))


def _row_spec(ts, cols):
    return pl.BlockSpec((ts, cols), lambda i: (i, 0))


def _full_spec(shape, single=True):
    nd = len(shape)
    if single:
        return pl.BlockSpec(shape, lambda i: (0,) * nd, pipeline_mode=pl.Buffered(1))
    return pl.BlockSpec(shape, lambda i: (0,) * nd)


def _sds(shape, dtype):
    return jax.ShapeDtypeStruct(shape, dtype)


def _params(sem):
    return pltpu.CompilerParams(dimension_semantics=sem, vmem_limit_bytes=VMEM_LIMIT)


def _even_pre_fwd(x, mod, nw, wie, qn, kn, ca, sa, ct, st, qln, kvln, wuq, wuk):
    S = x.shape[0]
    ts = min(IN_PROJ_ROW_TILE, S)

    def body(x_ref, mod_ref, nw_ref, wie_ref, qn_ref, kn_ref, ca_ref, sa_ref, ct_ref, st_ref, qln_ref,
             kvln_ref, wuq_ref, wuk_ref, h_ref, proj_ref, qa_ref, ka_ref, va_ref, qcat_ref, kcat_ref, kat_ref, vat_ref, kcatt_ref):
        xv = x_ref[...]
        h = (xv * _rms(xv) * nw_ref[...]) * (1.0 + mod_ref[1:2, :]) + mod_ref[0:1, :]
        hb = h.astype(BF16)
        h_ref[...] = hb
        proj = _dot(hb, wie_ref[...])
        proj_ref[...] = proj
        lane = _lane_iota()
        ca_v, sa_v, ct_v, st_v = ca_ref[...], sa_ref[...], ct_ref[...], st_ref[...]
        for cb in range(4):
            xc = proj[:, LANES * cb:LANES * (cb + 1)]
            r = lax.rsqrt(_seg_mean(xc * xc, lane) + EPS)
            y = _rot(xc * r * qn_ref[...], ca_v, sa_v, lane)
            qa_ref[:, LANES * cb:LANES * (cb + 1)] = (y * 0.125).astype(BF16)
        kc = proj[:, 512:640]
        r = lax.rsqrt(_seg_mean(kc * kc, lane) + EPS)
        ka_v = _dup_heads(_rot(kc * r * kn_ref[...], ca_v, sa_v, lane), lane)
        ka_ref[...] = ka_v.astype(BF16)
        kat_ref[...] = ka_v.T.astype(BF16)
        va_v = _dup_heads(proj[:, 640:768], lane)
        va_ref[...] = va_v.astype(BF16)
        vat_ref[...] = va_v.T.astype(BF16)
        cq = proj[:, 1280:1536]
        cqn = (cq * _rms(cq) * qln_ref[...]).astype(BF16)
        ckv = proj[:, 1536:1664]
        ckvn = ckv * _rms(ckv) * kvln_ref[...]
        qb = _dot(cqn, wuq_ref[...])
        qlat = _dot(qb[:, 0:512].astype(BF16), wuk_ref[...])
        for hh in range(B_HEADS):
            qcat_ref[hh, :, 0:LANES] = qlat[:, LANES * hh:LANES * (hh + 1)].astype(BF16)
            qr = _rot(qb[:, 512 + LANES * hh:512 + LANES * (hh + 1)], ct_v, st_v, lane)
            qcat_ref[hh, :, LANES:2 * LANES] = qr.astype(BF16)
        kr = _rot(proj[:, 1664:1792], ct_v, st_v, lane)
        kcat_ref[:, 0:LANES] = ckvn.astype(BF16)
        kcat_ref[:, LANES:2 * LANES] = kr.astype(BF16)
        kcatt_ref[0:LANES, :] = ckvn.T.astype(BF16)
        kcatt_ref[LANES:2 * LANES, :] = kr.T.astype(BF16)

    col_spec = lambda rows: pl.BlockSpec((rows, ts), lambda i: (0, i))
    return pl.pallas_call(
        body, name="even_pre_fwd", grid=(S // ts,),
        in_specs=[_row_spec(ts, D_MODEL), _full_spec((3, D_MODEL)), _full_spec((1, D_MODEL)), _full_spec((D_MODEL, EVEN_P)),
                  _full_spec((1, LANES)), _full_spec((1, LANES)),
                  _row_spec(ts, LANES), _row_spec(ts, LANES), _row_spec(ts, LANES), _row_spec(ts, LANES),
                  _full_spec((1, B_Q_LORA)), _full_spec((1, B_KV_LORA)), _full_spec((B_Q_LORA, 1536)), _full_spec((512, 1024))],
        out_specs=[_row_spec(ts, D_MODEL), _row_spec(ts, EVEN_P), _row_spec(ts, 512), _row_spec(ts, 2 * LANES), _row_spec(ts, 2 * LANES),
                   pl.BlockSpec((B_HEADS, ts, 2 * LANES), lambda i: (0, i, 0)), _row_spec(ts, 2 * LANES),
                   col_spec(2 * LANES), col_spec(2 * LANES), col_spec(2 * LANES)],
        out_shape=[_sds((S, D_MODEL), BF16), _sds((S, EVEN_P), F32), _sds((S, 512), BF16), _sds((S, 2 * LANES), BF16),
                   _sds((S, 2 * LANES), BF16), _sds((B_HEADS, S, 2 * LANES), BF16), _sds((S, 2 * LANES), BF16),
                   _sds((2 * LANES, S), BF16), _sds((2 * LANES, S), BF16), _sds((2 * LANES, S), BF16)],
        compiler_params=_params(("arbitrary",)),
    )(x, mod, nw, wie, qn, kn, ca, sa, ct, st, qln, kvln, wuq, wuk)


MLA_SCALE = (B_NOPE + B_ROPE) ** -0.5
LOG2E = 1.4426950408889634

def _row_lo():
    return lax.broadcasted_iota(jnp.int32, (LANES, 1), 0) < HEAD_DIM


def _stack_cols(vT, rlo):
    zero = jnp.zeros_like(vT)
    return jnp.concatenate([jnp.where(rlo, vT, zero), jnp.where(rlo, zero, vT)], axis=1)


def _stack_rows(v, lo):
    zero = jnp.zeros_like(v)
    return jnp.concatenate([jnp.where(lo, v, zero), jnp.where(lo, zero, v)], axis=0)


def _pick_halves_T(xT, rlo, t):
    return jnp.where(rlo, xT[:, 0:t], xT[:, t:2 * t]).T


def _side_split(refs, n_in, n_out, n_scratch, side):
    ns = side.n if side is not None else 0
    cuts = np.cumsum([0, n_in, ns, n_out, ns, n_scratch])
    return [refs[a:b] for a, b in zip(cuts[:-1], cuts[1:])] + [refs[cuts[-1]:]]


def _side_hooks(side, side_ins, side_outs, side_sems, step, total):
    if side is None:
        return lambda: None
    start, mid, end = side.phases(side_ins, side_outs, side_sems)
    pl.when(step == 0)(start)
    pl.when(step == total // 2)(mid)
    return lambda: pl.when(step == total - 1)(end)


def _side_specs(side):
    if side is None:
        return [], [], [], [], []
    return list(side.arrs), [_ANY] * side.n, [_ANY] * side.n, list(side.out_shapes), side.sem_shapes()


def _pp_fwd(q, k, vT, *, kdiv, tq, tk, sub, name, side=None):
    S = k.shape[0]; nb = q.shape[1] // LANES; nq = S // tq; nkv = S // tk; nsub = tk // sub

    def body(*refs):
        (q_ref, k_ref, vT_ref), side_ins, (o_ref, lse_ref), side_outs, (qs, m_s, l_s, acc), side_sems = _side_split(refs, 3, 2, 4, side)
        j = pl.program_id(2)
        rlo = _row_lo()
        step = (pl.program_id(0) * nq + pl.program_id(1)) * nkv + j
        side_end = _side_hooks(side, side_ins, side_outs, side_sems, step, nb * nq * nkv)

        @pl.when(j == 0)
        def _():
            qs[...] = _stack_cols(q_ref[...].astype(F32).T, rlo).astype(BF16)
            m_s[...] = jnp.full((1, 2 * tq), NEG, F32)
            l_s[...] = jnp.zeros((1, 2 * tq), F32)
            acc[...] = jnp.zeros((LANES, 2 * tq), F32)

        qsv = qs[...]
        m, l, a = m_s[...], l_s[...], acc[...]
        s_cur = _dot(k_ref[0:sub, :], qsv)
        for t in range(nsub):
            if t + 1 < nsub:
                s_next = _dot(k_ref[sub * (t + 1):sub * (t + 2), :], qsv)
            m_new = jnp.maximum(m, jnp.max(s_cur, axis=0, keepdims=True))
            alpha = jnp.exp(m - m_new)
            p = jnp.exp(s_cur - m_new)
            l = alpha * l + jnp.sum(p, axis=0, keepdims=True)
            a = alpha * a + _dot(vT_ref[:, sub * t:sub * (t + 1)], p.astype(BF16))
            m = m_new
            if t + 1 < nsub:
                s_cur = s_next
        m_s[...], l_s[...], acc[...] = m, l, a

        @pl.when(j == nkv - 1)
        def _():
            l_f = l_s[...]
            o_ref[...] = _pick_halves_T(acc[...] / l_f, rlo, tq).astype(BF16)
            lse_ref[0, 0] = m_s[...] + jnp.log(l_f)

        side_end()

    s_args, s_in, s_out, s_shapes, s_sems = _side_specs(side)
    return pl.pallas_call(
        body, name=name, grid=(nb, nq, nkv),
        in_specs=[pl.BlockSpec((tq, LANES), lambda b, i, j: (i, b)), pl.BlockSpec((tk, LANES), lambda b, i, j: (j, b // kdiv)),
                  pl.BlockSpec((LANES, tk), lambda b, i, j: (b // kdiv, j))] + s_in,
        out_specs=[pl.BlockSpec((tq, LANES), lambda b, i, j: (i, b)),
                   pl.BlockSpec((1, 1, 1, 2 * tq), lambda b, i, j: (b, i, 0, 0))] + s_out,
        out_shape=[_sds((S, nb * LANES), BF16), _sds((nb, nq, 1, 2 * tq), F32)] + s_shapes,
        scratch_shapes=[pltpu.VMEM((LANES, 2 * tq), BF16), pltpu.VMEM((1, 2 * tq), F32), pltpu.VMEM((1, 2 * tq), F32),
                        pltpu.VMEM((LANES, 2 * tq), F32)] + s_sems,
        compiler_params=_params(("arbitrary",) * 3))(q, k, vT, *s_args)


def _pp_bwd(q, k, kT, v, o, do, lse, *, kdiv, tq, tk, sub, name, side=None):
    S = k.shape[0]; nb = q.shape[1] // LANES; nkb = k.shape[1] // LANES; nq = S // tq; nkv = S // tk; nsub = tk // sub

    def body(*refs):
        ((q_ref, k_ref, kT_ref, v_ref, o_ref, do_ref, lse_ref), side_ins, (dq_ref, dk_ref, dv_ref), side_outs,
         (qsT, qs, dosT, dos, delta_s, dq_acc), side_sems) = _side_split(refs, 7, 3, 6, side)
        b, i, j = pl.program_id(0), pl.program_id(1), pl.program_id(2)
        rlo = _row_lo()
        lo = lax.broadcasted_iota(jnp.int32, (1, LANES), 1) < HEAD_DIM
        side_end = _side_hooks(side, side_ins, side_outs, side_sems, (b * nq + i) * nkv + j, nb * nq * nkv)

        @pl.when((b % kdiv == 0) & (i == 0) & (j == 0))
        def _():
            dk_ref[...] = jnp.zeros((S, LANES), F32)
            dv_ref[...] = jnp.zeros((S, LANES), F32)

        @pl.when(j == 0)
        def _():
            qv = q_ref[...]
            qs[...] = _stack_rows(qv, lo)
            qsT[...] = _stack_cols(qv.astype(F32).T, rlo).astype(BF16)
            dov = do_ref[...].astype(F32)
            dos[...] = _stack_rows(dov.astype(BF16), lo)
            dosT[...] = _stack_cols(dov.T, rlo).astype(BF16)
            prodT = (dov * o_ref[...].astype(F32)).T
            delta_s[...] = jnp.concatenate([jnp.sum(jnp.where(rlo, prodT, 0.0), axis=0, keepdims=True),
                                            jnp.sum(jnp.where(rlo, 0.0, prodT), axis=0, keepdims=True)], axis=1)
            dq_acc[...] = jnp.zeros((LANES, 2 * tq), F32)

        qsTv, dosTv, qsv, dosv = qsT[...], dosT[...], qs[...], dos[...]
        lse_v, delta_v = lse_ref[0, 0], delta_s[...]
        dqa = dq_acc[...]
        s_cur = _dot(k_ref[0:sub, :], qsTv)
        dp_cur = _dot(v_ref[0:sub, :], dosTv)
        for t in range(nsub):
            if t + 1 < nsub:
                s_next = _dot(k_ref[sub * (t + 1):sub * (t + 2), :], qsTv)
                dp_next = _dot(v_ref[sub * (t + 1):sub * (t + 2), :], dosTv)
            p = jnp.exp(s_cur - lse_v)
            ds = (p * (dp_cur - delta_v)).astype(BF16)
            rows = pl.ds(pl.multiple_of(j * tk + sub * t, sub), sub)
            dv_ref[rows, :] += _dot(p.astype(BF16), dosv)
            dk_ref[rows, :] += _dot(ds, qsv)
            dqa = dqa + _dot(kT_ref[:, sub * t:sub * (t + 1)], ds)
            if t + 1 < nsub:
                s_cur, dp_cur = s_next, dp_next
        dq_acc[...] = dqa

        @pl.when(j == nkv - 1)
        def _():
            dq_ref[...] = _pick_halves_T(dq_acc[...], rlo, tq)

        side_end()

    qmap = lambda b, i, j: (i, b)
    kmap = lambda b, i, j: (j, b // kdiv)
    res = lambda b, i, j: (0, b // kdiv)
    s_args, s_in, s_out, s_shapes, s_sems = _side_specs(side)
    return pl.pallas_call(
        body, name=name, grid=(nb, nq, nkv),
        in_specs=[pl.BlockSpec((tq, LANES), qmap), pl.BlockSpec((tk, LANES), kmap), pl.BlockSpec((LANES, tk), lambda b, i, j: (b // kdiv, j)),
                  pl.BlockSpec((tk, LANES), kmap), pl.BlockSpec((tq, LANES), qmap), pl.BlockSpec((tq, LANES), qmap),
                  pl.BlockSpec((1, 1, 1, 2 * tq), lambda b, i, j: (b, i, 0, 0))] + s_in,
        out_specs=[pl.BlockSpec((tq, LANES), qmap), pl.BlockSpec((S, LANES), res), pl.BlockSpec((S, LANES), res)] + s_out,
        out_shape=[_sds((S, nb * LANES), F32), _sds((S, nkb * LANES), F32), _sds((S, nkb * LANES), F32)] + s_shapes,
        scratch_shapes=[pltpu.VMEM((LANES, 2 * tq), BF16), pltpu.VMEM((2 * tq, LANES), BF16), pltpu.VMEM((LANES, 2 * tq), BF16),
                        pltpu.VMEM((2 * tq, LANES), BF16), pltpu.VMEM((1, 2 * tq), F32), pltpu.VMEM((LANES, 2 * tq), F32)] + s_sems,
        compiler_params=_params(("arbitrary",) * 3))(q, k, kT, v, o, do, lse, *s_args)


MLA_C = MLA_SCALE * LOG2E


def _mla_fwd(q, kcat, kcatT, *, tq, tk, sub):
    S = kcat.shape[0]; nq, nkv = S // tq, S // tk; R = B_HEADS * tq; nsub = tk // sub

    def body(q_ref, k_ref, vT_ref, o_ref, lse_ref, qT, m_s, l_s, acc):
        j = pl.program_id(1)

        @pl.when(j == 0)
        def _():
            qT[...] = q_ref[...].reshape(R, 2 * LANES).astype(F32).T.astype(BF16)
            m_s[...] = jnp.full((1, R), NEG, F32)
            l_s[...] = jnp.zeros((1, R), F32)
            acc[...] = jnp.zeros((LANES, R), F32)

        qTv = qT[...]
        m, l, a = m_s[...], l_s[...], acc[...]
        s_cur = _dot(k_ref[0:sub, :], qTv)
        for t in range(nsub):
            if t + 1 < nsub:
                s_next = _dot(k_ref[sub * (t + 1):sub * (t + 2), :], qTv)
            m_new = jnp.maximum(m, jnp.max(s_cur, axis=0, keepdims=True))
            alpha = jnp.exp2((m - m_new) * MLA_C)
            p = jnp.exp2((s_cur - m_new) * MLA_C)
            l = alpha * l + jnp.sum(p, axis=0, keepdims=True)
            a = alpha * a + _dot(vT_ref[:, sub * t:sub * (t + 1)], p.astype(BF16))
            m = m_new
            if t + 1 < nsub:
                s_cur = s_next
        m_s[...], l_s[...], acc[...] = m, l, a

        @pl.when(j == nkv - 1)
        def _():
            l_f = l_s[...]
            o_ref[...] = (acc[...] / l_f).T.reshape(B_HEADS, tq, LANES).astype(BF16)
            lse_ref[0] = m_s[...] * MLA_SCALE + jnp.log(l_f)

    return pl.pallas_call(
        body, name="mla_fwd", grid=(nq, nkv),
        in_specs=[pl.BlockSpec((B_HEADS, tq, 2 * LANES), lambda i, j: (0, i, 0)), pl.BlockSpec((tk, 2 * LANES), lambda i, j: (j, 0)),
                  pl.BlockSpec((LANES, tk), lambda i, j: (0, j))],
        out_specs=[pl.BlockSpec((B_HEADS, tq, LANES), lambda i, j: (0, i, 0)), pl.BlockSpec((1, 1, R), lambda i, j: (i, 0, 0))],
        out_shape=[_sds((B_HEADS, S, LANES), BF16), _sds((nq, 1, R), F32)],
        scratch_shapes=[pltpu.VMEM((2 * LANES, R), BF16), pltpu.VMEM((1, R), F32), pltpu.VMEM((1, R), F32), pltpu.VMEM((LANES, R), F32)],
        compiler_params=_params(("arbitrary", "arbitrary")))(q, kcat, kcatT)


def _mla_bwd(q, kcat, kcatT, o, do, lse, *, tq, tk, sub):
    S = kcat.shape[0]; nq, nkv = S // tq, S // tk; R = B_HEADS * tq; nsub = tk // sub

    def body(q_ref, k_ref, kT_ref, o_ref, do_ref, lse_ref, dq_ref, dk_ref, qT, dosT, dos, delta_s, dq_acc):
        i, j = pl.program_id(0), pl.program_id(1)

        @pl.when((i == 0) & (j == 0))
        def _():
            dk_ref[...] = jnp.zeros((S, 2 * LANES), F32)

        @pl.when(j == 0)
        def _():
            qT[...] = q_ref[...].reshape(R, 2 * LANES).astype(F32).T.astype(BF16)
            dov = do_ref[...].reshape(R, LANES).astype(F32)
            dos[...] = dov.astype(BF16)
            dosT[...] = dov.T.astype(BF16)
            delta_s[...] = jnp.sum((dov * o_ref[...].reshape(R, LANES).astype(F32)).T, axis=0, keepdims=True)
            dq_acc[...] = jnp.zeros((2 * LANES, R), F32)

        qTv, dosTv, dosv = qT[...], dosT[...], dos[...]
        qv = q_ref[...].reshape(R, 2 * LANES)
        lse_v, delta_v = lse_ref[0] * LOG2E, delta_s[...]
        dqa = dq_acc[...]
        s_cur = _dot(k_ref[0:sub, :], qTv)
        dp_cur = _dot(k_ref[0:sub, 0:LANES], dosTv)
        for t in range(nsub):
            if t + 1 < nsub:
                s_next = _dot(k_ref[sub * (t + 1):sub * (t + 2), :], qTv)
                dp_next = _dot(k_ref[sub * (t + 1):sub * (t + 2), 0:LANES], dosTv)
            p = jnp.exp2(s_cur * MLA_C - lse_v)
            ds = (p * (dp_cur - delta_v) * MLA_SCALE).astype(BF16)
            rows = pl.ds(pl.multiple_of(j * tk + sub * t, sub), sub)
            dk_ref[rows, :] += _dot(ds, qv)
            dk_ref[rows, 0:LANES] += _dot(p.astype(BF16), dosv)
            dqa = dqa + _dot(kT_ref[:, sub * t:sub * (t + 1)], ds)
            if t + 1 < nsub:
                s_cur, dp_cur = s_next, dp_next
        dq_acc[...] = dqa

        @pl.when(j == nkv - 1)
        def _():
            dq_ref[...] = dq_acc[...].T.reshape(B_HEADS, tq, 2 * LANES)

    hspec = lambda w: pl.BlockSpec((B_HEADS, tq, w), lambda i, j: (0, i, 0))
    return pl.pallas_call(
        body, name="mla_bwd", grid=(nq, nkv),
        in_specs=[hspec(2 * LANES), pl.BlockSpec((tk, 2 * LANES), lambda i, j: (j, 0)), pl.BlockSpec((2 * LANES, tk), lambda i, j: (0, j)),
                  hspec(LANES), hspec(LANES), pl.BlockSpec((1, 1, R), lambda i, j: (i, 0, 0))],
        out_specs=[hspec(2 * LANES), pl.BlockSpec((S, 2 * LANES), lambda i, j: (0, 0))],
        out_shape=[_sds((B_HEADS, S, 2 * LANES), F32), _sds((S, 2 * LANES), F32)],
        scratch_shapes=[pltpu.VMEM((2 * LANES, R), BF16), pltpu.VMEM((LANES, R), BF16), pltpu.VMEM((R, LANES), BF16),
                        pltpu.VMEM((1, R), F32), pltpu.VMEM((2 * LANES, R), F32)],
        compiler_params=_params(("arbitrary", "arbitrary")))(q, kcat, kcatT, o, do, lse)


def _win_start(i, tq, nk, S):
    return pl.multiple_of(jnp.clip(i * tq - WINDOW, 0, S - nk), LANES)


def _win_dist_table(S, tq):
    nk = min(tq + 2 * WINDOW, S)
    nq = S // tq
    r = np.arange(nk)[:, None]
    c = (np.arange(2 * tq) % tq)[None, :]
    tabs = []
    for rel in (0, WINDOW, (nq - 1) * tq - (S - nk)):
        dist = np.abs(rel + c - r).astype(np.float32)
        tabs.append(np.where(dist <= WINDOW, dist, np.float32(1e32)))
    return jnp.asarray(np.stack(tabs))


def _win_dist_spec(nk, tq, nq):
    return pl.BlockSpec((1, nk, 2 * tq), lambda b, i: (jnp.where(i == 0, 0, jnp.where(i == nq - 1, 2, 1)), 0, 0))


def _win_fwd(q, k, vT, dist, slope, sink, *, kdiv, tq, nbs, name):
    S = k.shape[0]; nb = q.shape[1] // LANES; nq = S // tq; nk = min(tq + 2 * WINDOW, S)
    assert nb % nbs == 0 and nbs % kdiv == 0
    kvw = (nbs // kdiv) * LANES

    def body(q_ref, k_ref, vT_ref, dist_ref, slope_ref, sink_ref, o_ref, lse_ref):
        i = pl.program_id(1)
        rlo = _row_lo()
        k0 = _win_start(i, tq, nk, S)
        kk, vv, dd = k_ref[pl.ds(k0, nk), :], vT_ref[:, pl.ds(k0, nk)], dist_ref[0]
        for u in range(nbs):
            kv = slice(LANES * (u // kdiv), LANES * (u // kdiv + 1))
            qsT = _stack_cols(q_ref[:, LANES * u:LANES * (u + 1)].astype(F32).T, rlo).astype(BF16)
            s = _dot(kk[:, kv], qsT) - slope_ref[u] * dd
            sk = sink_ref[u]
            m = jnp.maximum(jnp.max(s, axis=0, keepdims=True), sk)
            p = jnp.exp(s - m)
            l = jnp.sum(p, axis=0, keepdims=True) + jnp.exp(sk - m)
            o_ref[:, LANES * u:LANES * (u + 1)] = _pick_halves_T(_dot(vv[kv, :], p.astype(BF16)) / l, rlo, tq).astype(BF16)
            lse_ref[u, 0] = m + jnp.log(l)

    row_spec = pl.BlockSpec((nbs, 1, 2 * tq), lambda b, i: (b, 0, 0))
    return pl.pallas_call(
        body, name=name, grid=(nb // nbs, nq),
        in_specs=[pl.BlockSpec((tq, nbs * LANES), lambda b, i: (i, b)), pl.BlockSpec((S, kvw), lambda b, i: (0, b)),
                  pl.BlockSpec((kvw, S), lambda b, i: (b, 0)), _win_dist_spec(nk, tq, nq), row_spec, row_spec],
        out_specs=[pl.BlockSpec((tq, nbs * LANES), lambda b, i: (i, b)), pl.BlockSpec((nbs, 1, 1, 2 * tq), lambda b, i: (b, i, 0, 0))],
        out_shape=[_sds((S, nb * LANES), BF16), _sds((nb, nq, 1, 2 * tq), F32)],
        compiler_params=_params(("arbitrary", "arbitrary")))(q, k, vT, dist, slope, sink)


def _win_bwd(q, k, kT, v, o, do, lse, dist, slope, sink, *, kdiv, tq, nbs, name):
    S = k.shape[0]; nb = q.shape[1] // LANES; nkb = k.shape[1] // LANES; nq = S // tq; nk = min(tq + 2 * WINDOW, S)
    assert nb % nbs == 0 and nbs % kdiv == 0
    nkv = nbs // kdiv
    kvw = nkv * LANES

    def body(q_ref, k_ref, kT_ref, v_ref, o_ref, do_ref, lse_ref, dist_ref, slope_ref, sink_ref, dq_ref, dk_ref, dv_ref, dsink_ref, ds_acc):
        i = pl.program_id(1)
        rlo = _row_lo()
        lo = lax.broadcasted_iota(jnp.int32, (1, LANES), 1) < HEAD_DIM

        @pl.when(i == 0)
        def _():
            dk_ref[...] = jnp.zeros((S, kvw), F32)
            dv_ref[...] = jnp.zeros((S, kvw), F32)
            ds_acc[...] = jnp.zeros((nbs, 2 * tq), F32)

        k0 = _win_start(i, tq, nk, S)
        rows = pl.ds(k0, nk)
        kk_all, vv_all, kkT_all, dd = k_ref[rows, :], v_ref[rows, :], kT_ref[:, rows], dist_ref[0]
        dv_sum, dk_sum = [None] * nkv, [None] * nkv
        for u in range(nbs):
            g = u // kdiv
            kv = slice(LANES * g, LANES * (g + 1))
            kk, vv, kkT = kk_all[:, kv], vv_all[:, kv], kkT_all[kv, :]
            cols = slice(LANES * u, LANES * (u + 1))
            qv = q_ref[:, cols]
            qs = _stack_rows(qv, lo)
            qsT = _stack_cols(qv.astype(F32).T, rlo).astype(BF16)
            dov = do_ref[:, cols].astype(F32)
            dos = _stack_rows(dov.astype(BF16), lo)
            dosT = _stack_cols(dov.T, rlo).astype(BF16)
            prodT = (dov * o_ref[:, cols].astype(F32)).T
            delta = jnp.concatenate([jnp.sum(jnp.where(rlo, prodT, 0.0), axis=0, keepdims=True),
                                     jnp.sum(jnp.where(rlo, 0.0, prodT), axis=0, keepdims=True)], axis=1)
            lse_v = lse_ref[u, 0]
            ds_acc[u:u + 1, :] += -jnp.exp(sink_ref[u] - lse_v) * delta
            p = jnp.exp(_dot(kk, qsT) - slope_ref[u] * dd - lse_v)
            ds = (p * (_dot(vv, dosT) - delta)).astype(BF16)
            dv_u, dk_u = _dot(p.astype(BF16), dos), _dot(ds, qs)
            dv_sum[g] = dv_u if dv_sum[g] is None else dv_sum[g] + dv_u
            dk_sum[g] = dk_u if dk_sum[g] is None else dk_sum[g] + dk_u
            dq_ref[:, cols] = (_pick_halves_T(_dot(kkT, ds), rlo, tq) * 0.125).astype(BF16)
        dv_ref[rows, :] += jnp.concatenate(dv_sum, axis=1)
        dk_ref[rows, :] += jnp.concatenate(dk_sum, axis=1)

        @pl.when(i == nq - 1)
        def _():
            acc = ds_acc[...]
            for u in range(nbs):
                dsink_ref[u] = jnp.concatenate(
                    [jnp.broadcast_to(jnp.sum(acc[u:u + 1, 0:tq], axis=1, keepdims=True), (1, LANES)),
                     jnp.broadcast_to(jnp.sum(acc[u:u + 1, tq:2 * tq], axis=1, keepdims=True), (1, LANES)),
                     jnp.zeros((6, LANES), F32)], axis=0)

    qmap = lambda b, i: (i, b)
    kv_spec = pl.BlockSpec((S, kvw), lambda b, i: (0, b))
    row_spec = pl.BlockSpec((nbs, 1, 2 * tq), lambda b, i: (b, 0, 0))
    wide = pl.BlockSpec((tq, nbs * LANES), qmap)
    return pl.pallas_call(
        body, name=name, grid=(nb // nbs, nq),
        in_specs=[wide, kv_spec, pl.BlockSpec((kvw, S), lambda b, i: (b, 0)), kv_spec, wide, wide,
                  pl.BlockSpec((nbs, 1, 1, 2 * tq), lambda b, i: (b, i, 0, 0)), _win_dist_spec(nk, tq, nq), row_spec, row_spec],
        out_specs=[wide, kv_spec, kv_spec, pl.BlockSpec((nbs, 8, LANES), lambda b, i: (b, 0, 0))],
        out_shape=[_sds((S, nb * LANES), BF16), _sds((S, nkb * LANES), F32), _sds((S, nkb * LANES), F32), _sds((nb, 8, LANES), F32)],
        scratch_shapes=[pltpu.VMEM((nbs, 2 * tq), F32)],
        compiler_params=_params(("arbitrary", "arbitrary")))(q, k, kT, v, o, do, lse, dist, slope, sink)


def _sum_rows(v):
    return jnp.sum(v, axis=0, keepdims=True)


def _norm_mod_bwd(dh, xv, mod_ref, nw_ref, stats_ref):
    r = _rms(xv)
    xn = xv * r
    nw = nw_ref[...]
    stats_ref[0:1, :] += _sum_rows(dh)
    stats_ref[1:2, :] += _sum_rows(dh * (xn * nw))
    dn = dh * (1.0 + mod_ref[1:2, :])
    stats_ref[2:3, :] += _sum_rows(dn * xn)
    return _rms_bwd(xv, r, dn * nw)


def _even_gate_specs(ts):
    return [pl.BlockSpec((ts, 256), lambda i, c=c: (i, c)) for c in (3, 4, 7, 8)]


def _even_post_fwd(oa, olat, proj, x, gate, wuv, woe):
    S = x.shape[0]
    ts = min(ROW_TILE, S)

    def body(oa_ref, ol_ref, ga0_ref, ga1_ref, gb0_ref, gb1_ref, x_ref, gate_ref, wuv_ref, woe_ref, y_ref, x1_ref):
        sa, _ = _silu_and_grad(jnp.concatenate([ga0_ref[...], ga1_ref[...]], axis=1))
        sb, _ = _silu_and_grad(jnp.concatenate([gb0_ref[...], gb1_ref[...]], axis=1))
        olc = jnp.concatenate([ol_ref[hh] for hh in range(B_HEADS)], axis=1).astype(BF16)
        ob = _dot(olc, wuv_ref[...])
        mix = jnp.concatenate([oa_ref[...] * sa, ob * sb], axis=1).astype(BF16)
        y = _dot(mix, woe_ref[...])
        y_ref[...] = y.astype(BF16)
        x1_ref[...] = x_ref[...] + gate_ref[...] * y

    return pl.pallas_call(
        body, name="even_post_fwd", grid=(S // ts,),
        in_specs=[_row_spec(ts, 512), pl.BlockSpec((B_HEADS, ts, LANES), lambda i: (0, i, 0))] + _even_gate_specs(ts) +
                 [_row_spec(ts, D_MODEL), _full_spec((1, D_MODEL)), _full_spec((1024, 512)), _full_spec((1024, D_MODEL))],
        out_specs=[_row_spec(ts, D_MODEL), _row_spec(ts, D_MODEL)],
        out_shape=[_sds((S, D_MODEL), BF16), _sds((S, D_MODEL), F32)],
        compiler_params=_params(("arbitrary",)),
    )(oa, olat, proj, proj, proj, proj, x, gate, wuv, woe)


def _odd_pre_fwd(x, mod, nw, wio):
    S = x.shape[0]
    ts = min(ROW_TILE, S)

    def body(x_ref, mod_ref, nw_ref, wio_ref, h_ref, proj_ref, q_ref, k_ref, v_ref, kt_ref, vt_ref):
        xv = x_ref[...]
        h = (xv * _rms(xv) * nw_ref[...]) * (1.0 + mod_ref[1:2, :]) + mod_ref[0:1, :]
        hb = h.astype(BF16)
        h_ref[...] = hb
        proj = jnp.concatenate([_dot(hb, wio_ref[p]) for p in range(N_CHIPS)], axis=1)
        proj_ref[...] = proj
        q_ref[...] = (proj[:, 0:1024] * 0.125).astype(BF16)
        lane = _lane_iota()
        k_v = jnp.concatenate([_dup_heads(proj[:, 1024 + LANES * j:1024 + LANES * (j + 1)], lane) for j in range(2)], axis=1)
        v_v = jnp.concatenate([_dup_heads(proj[:, 1280 + LANES * j:1280 + LANES * (j + 1)], lane) for j in range(2)], axis=1)
        k_ref[...] = k_v.astype(BF16)
        v_ref[...] = v_v.astype(BF16)
        kt_ref[...] = k_v.T.astype(BF16)
        vt_ref[...] = v_v.T.astype(BF16)

    col_spec = pl.BlockSpec((512, ts), lambda i: (0, i))
    return pl.pallas_call(
        body, name="odd_pre_fwd", grid=(S // ts,),
        in_specs=[_row_spec(ts, D_MODEL), _full_spec((3, D_MODEL)), _full_spec((1, D_MODEL)),
                  _full_spec((N_CHIPS, D_MODEL, ODD_IN // N_CHIPS))],
        out_specs=[_row_spec(ts, D_MODEL), _row_spec(ts, ODD_IN), _row_spec(ts, 1024), _row_spec(ts, 512), _row_spec(ts, 512),
                   col_spec, col_spec],
        out_shape=[_sds((S, D_MODEL), BF16), _sds((S, ODD_IN), F32), _sds((S, 1024), BF16), _sds((S, 512), BF16),
                   _sds((S, 512), BF16), _sds((512, S), BF16), _sds((512, S), BF16)],
        compiler_params=_params(("arbitrary",)),
    )(x, mod, nw, wio)


def _odd_post(oc, proj, x1, gate, woo, fw, tgt):
    S = x1.shape[0]
    ts = min(ROW_TILE, S)
    nsteps = S // ts

    def body(oc_ref, g0_ref, g1_ref, x_ref, gate_ref, woo_ref, fw_ref, tgt_ref, doc_ref, dgc_ref, dx2_ref, dwoo_out, stats_ref,
             dwoo_ref):
        @pl.when(pl.program_id(0) == 0)
        def _():
            dwoo_ref[...] = jnp.zeros((D_MODEL, D_MODEL), F32)
            stats_ref[...] = jnp.zeros((8, D_MODEL), F32)

        ocv = oc_ref[...]
        sg, dsg = _silu_and_grad(jnp.concatenate([g0_ref[...], g1_ref[...]], axis=1))
        mix = (ocv * sg).astype(BF16)
        woo_v = woo_ref[...]
        y = _dot(mix, woo_v)
        gate_v = gate_ref[...]
        x2 = x_ref[...] + gate_v * y
        r = _rms(x2)
        fw_v = fw_ref[...]
        xn = x2 * r
        err = xn * fw_v - tgt_ref[...]
        dout = err * (1.0 / D_MODEL)
        dx2 = _rms_bwd(x2, r, dout * fw_v)
        dx2_ref[...] = dx2
        stats_ref[0:1, :] += _sum_rows(dout * xn)
        stats_ref[1:2, :] += _sum_rows(dx2 * y)
        loss_t = 0.5 * jnp.sum(_sum_rows(err * dout), axis=-1, keepdims=True)
        stats_ref[2:3, :] += jnp.broadcast_to(loss_t, (1, D_MODEL))
        dy = (gate_v * dx2).astype(BF16)
        dmix = _dot_nt(dy, woo_v)
        dwoo_ref[...] += _dot_tn(mix, dy)
        doc_ref[...] = (dmix * sg).astype(BF16)
        dgc_ref[...] = (dmix * ocv * dsg).astype(BF16)

        @pl.when(pl.program_id(0) == nsteps - 1)
        def _():
            dwoo_out[...] = dwoo_ref[...].astype(BF16)

    gate_cols = [pl.BlockSpec((ts, 512), lambda i, c=c: (i, c)) for c in (3, 4)]
    return pl.pallas_call(
        body, name="odd_post", grid=(nsteps,),
        in_specs=[_row_spec(ts, D_MODEL)] + gate_cols + [_row_spec(ts, D_MODEL), _full_spec((1, D_MODEL)),
                  _full_spec((D_MODEL, D_MODEL)), _full_spec((1, D_MODEL)), _row_spec(ts, D_MODEL)],
        out_specs=[_row_spec(ts, D_MODEL), _row_spec(ts, D_MODEL), _row_spec(ts, D_MODEL),
                   _full_spec((D_MODEL, D_MODEL), single=False), _full_spec((8, D_MODEL), single=False)],
        out_shape=[_sds((S, D_MODEL), BF16), _sds((S, D_MODEL), BF16), _sds((S, D_MODEL), F32), _sds((D_MODEL, D_MODEL), BF16),
                   _sds((8, D_MODEL), F32)],
        scratch_shapes=[pltpu.VMEM((D_MODEL, D_MODEL), F32)],
        compiler_params=_params(("arbitrary",)),
    )(oc, proj, proj, x1, gate, woo, fw, tgt)


def _odd_pre_bwd(dq, dk, dv, dgc, h, x, dx_res, mod, nw, wio):
    S = x.shape[0]
    ts = min(IN_PROJ_ROW_TILE, S)
    nsteps = S // ts
    wsh = ODD_IN // N_CHIPS

    def body(dq_ref, dk_ref, dv_ref, dgc_ref, h_ref, x_ref, dxr_ref, mod_ref, nw_ref, wio_ref, dx_ref, dw_ref, stats_ref, dw_acc):
        @pl.when(pl.program_id(0) == 0)
        def _():
            dw_acc[...] = jnp.zeros((N_CHIPS, D_MODEL, wsh), F32)
            stats_ref[...] = jnp.zeros((8, D_MODEL), F32)

        lane = _lane_iota()
        dkv = [_fold_heads(r[:, 2 * LANES * j:2 * LANES * (j + 1)], lane).astype(BF16) for r in (dk_ref, dv_ref) for j in range(2)]
        dproj = jnp.concatenate([dq_ref[...]] + dkv + [dgc_ref[...]], axis=1)
        hv = h_ref[...]
        dh = None
        for p in range(N_CHIPS):
            dp_cols = dproj[:, wsh * p:wsh * (p + 1)]
            part = _dot_nt(dp_cols, wio_ref[p])
            dh = part if dh is None else dh + part
            dw_acc[p] += _dot_tn(hv, dp_cols)
        dx_ref[...] = dxr_ref[...] + _norm_mod_bwd(dh, x_ref[...], mod_ref, nw_ref, stats_ref)

        @pl.when(pl.program_id(0) == nsteps - 1)
        def _():
            dw_ref[...] = dw_acc[...].astype(BF16)

    return pl.pallas_call(
        body, name="odd_pre_bwd", grid=(nsteps,),
        in_specs=[_row_spec(ts, 1024), _row_spec(ts, 512), _row_spec(ts, 512), _row_spec(ts, 1024), _row_spec(ts, D_MODEL),
                  _row_spec(ts, D_MODEL), _row_spec(ts, D_MODEL), _full_spec((3, D_MODEL)), _full_spec((1, D_MODEL)),
                  _full_spec((N_CHIPS, D_MODEL, wsh))],
        out_specs=[_row_spec(ts, D_MODEL), _full_spec((N_CHIPS, D_MODEL, wsh), single=False), _full_spec((8, D_MODEL), single=False)],
        out_shape=[_sds((S, D_MODEL), F32), _sds((N_CHIPS, D_MODEL, wsh), BF16), _sds((8, D_MODEL), F32)],
        scratch_shapes=[pltpu.VMEM((N_CHIPS, D_MODEL, wsh), F32)],
        compiler_params=_params(("arbitrary",)),
    )(dq, dk, dv, dgc, h, x, dx_res, mod, nw, wio)


def _even_post_bwd(dx1, y, oa, olat, proj, gate, wuv, woe):
    S = dx1.shape[0]
    ts = min(ROW_TILE, S)
    nsteps = S // ts

    def body(dx_ref, y_ref, oa_ref, ol_ref, ga0_ref, ga1_ref, gb0_ref, gb1_ref, gate_ref, wuv_ref, woe_ref,
             doa_ref, dga_ref, dgb_ref, dol_ref, dwoe_out, dwuv_ref, stats_ref, dwoe_ref):
        @pl.when(pl.program_id(0) == 0)
        def _():
            dwoe_ref[...] = jnp.zeros((D_MODEL, D_MODEL), F32)
            dwuv_ref[...] = jnp.zeros((1024, 512), F32)
            stats_ref[...] = jnp.zeros((8, D_MODEL), F32)

        dxv = dx_ref[...]
        stats_ref[0:1, :] += _sum_rows(dxv * y_ref[...])
        dy = (gate_ref[...] * dxv).astype(BF16)
        sa, dsa = _silu_and_grad(jnp.concatenate([ga0_ref[...], ga1_ref[...]], axis=1))
        sb, dsb = _silu_and_grad(jnp.concatenate([gb0_ref[...], gb1_ref[...]], axis=1))
        olc = jnp.concatenate([ol_ref[hh] for hh in range(B_HEADS)], axis=1).astype(BF16)
        wuv_v = wuv_ref[...]
        ob = _dot(olc, wuv_v)
        oav = oa_ref[...]
        mix = jnp.concatenate([oav * sa, ob * sb], axis=1).astype(BF16)
        dmix = _dot_nt(dy, woe_ref[...])
        dwoe_ref[...] += _dot_tn(mix, dy)
        dma, dmb = dmix[:, 0:512], dmix[:, 512:1024]
        doa_ref[...] = (dma * sa).astype(BF16)
        dga_ref[...] = (dma * oav * dsa).astype(BF16)
        dgb_ref[...] = (dmb * ob * dsb).astype(BF16)
        dob = (dmb * sb).astype(BF16)
        dol = _dot_nt(dob, wuv_v)
        dwuv_ref[...] += _dot_tn(olc, dob)
        for hh in range(B_HEADS):
            dol_ref[hh] = dol[:, LANES * hh:LANES * (hh + 1)].astype(BF16)

        @pl.when(pl.program_id(0) == nsteps - 1)
        def _():
            dwoe_out[...] = dwoe_ref[...].astype(BF16)

    head_spec = pl.BlockSpec((B_HEADS, ts, LANES), lambda i: (0, i, 0))
    return pl.pallas_call(
        body, name="even_post_bwd", grid=(nsteps,),
        in_specs=[_row_spec(ts, D_MODEL), _row_spec(ts, D_MODEL), _row_spec(ts, 512), head_spec] + _even_gate_specs(ts) +
                 [_full_spec((1, D_MODEL)), _full_spec((1024, 512)), _full_spec((1024, D_MODEL))],
        out_specs=[_row_spec(ts, 512), _row_spec(ts, 512), _row_spec(ts, 512), head_spec,
                   _full_spec((D_MODEL, D_MODEL), single=False), _full_spec((1024, 512), single=False),
                   _full_spec((8, D_MODEL), single=False)],
        out_shape=[_sds((S, 512), BF16), _sds((S, 512), BF16), _sds((S, 512), BF16), _sds((B_HEADS, S, LANES), BF16),
                   _sds((D_MODEL, D_MODEL), BF16), _sds((1024, 512), F32), _sds((8, D_MODEL), F32)],
        scratch_shapes=[pltpu.VMEM((D_MODEL, D_MODEL), F32)],
        compiler_params=_params(("arbitrary",)),
    )(dx1, y, oa, olat, proj, proj, proj, proj, gate, wuv, woe)


def _even_pre_bwd(x, h, proj, dqa, dka, dva, dga, dgb, dqcat, dkcat, dx_res, mod, nw, wie, qn, kn, ca, sa, ct, st,
                  qln, kvln, wuq, wuk):
    S = x.shape[0]
    ts = min(IN_PROJ_ROW_TILE, S)
    nsteps = S // ts

    def body(x_ref, h_ref, proj_ref, dqa_ref, dka_ref, dva_ref, dga_ref, dgb_ref, dqc_ref, dkc_ref, dxr_ref, mod_ref, nw_ref,
             wie_ref, qn_ref, kn_ref, ca_ref, sa_ref, ct_ref, st_ref, qln_ref, kvln_ref, wuq_ref, wuk_ref,
             dx_ref, dwie_out, dwuq_out, dwuk_out, stats_ref, nstats_ref, dwie_ref, dwuq_ref, dwuk_ref):
        @pl.when(pl.program_id(0) == 0)
        def _():
            dwie_ref[...] = jnp.zeros((D_MODEL, EVEN_P), F32)
            dwuq_ref[...] = jnp.zeros((B_Q_LORA, 1536), F32)
            dwuk_ref[...] = jnp.zeros((512, 1024), F32)
            stats_ref[...] = jnp.zeros((8, D_MODEL), F32)
            nstats_ref[...] = jnp.zeros((8, 256), F32)

        lane = _lane_iota()
        ca_v, sa_v, ct_v, st_v = ca_ref[...], sa_ref[...], ct_ref[...], st_ref[...]

        def head_norm_bwd(xc, dy, w):
            r = lax.rsqrt(_seg_mean(xc * xc, lane) + EPS)
            g = dy * w
            dxc = r * g - xc * (r * r * r) * _seg_mean(xc * g, lane)
            return dxc, _sum_rows(dy * (xc * r))

        pieces = []
        dqn = jnp.zeros((1, LANES), F32)
        for cb in range(4):
            sl = slice(LANES * cb, LANES * (cb + 1))
            dy = _rot_bwd(dqa_ref[:, sl] * 0.125, ca_v, sa_v, lane)
            dxc, dw = head_norm_bwd(proj_ref[:, sl], dy, qn_ref[...])
            pieces.append(dxc)
            dqn = dqn + dw
        dxc, dkn = head_norm_bwd(proj_ref[:, 512:640], _rot_bwd(_fold_heads(dka_ref[...], lane), ca_v, sa_v, lane), kn_ref[...])
        pieces += [dxc, _fold_heads(dva_ref[...], lane), dga_ref[...]]
        nstats_ref[0:1, 0:LANES] += dqn + pltpu.roll(dqn, HEAD_DIM, 1)
        nstats_ref[1:2, 0:LANES] += dkn + pltpu.roll(dkn, HEAD_DIM, 1)

        cq = proj_ref[:, 1280:1536]
        rq = _rms(cq)
        cqn_f = cq * rq
        qln_v = qln_ref[...]
        cqn = (cqn_f * qln_v).astype(BF16)
        wuq_v, wuk_v = wuq_ref[...], wuk_ref[...]
        qnope = _dot(cqn, wuq_v[:, 0:512]).astype(BF16)
        dqlat = jnp.concatenate([dqc_ref[hh, :, 0:LANES] for hh in range(B_HEADS)], axis=1).astype(BF16)
        dqnope = _dot_nt(dqlat, wuk_v)
        dwuk_ref[...] += _dot_tn(qnope, dqlat)
        dqr = [_rot_bwd(dqc_ref[hh, :, LANES:2 * LANES], ct_v, st_v, lane) for hh in range(B_HEADS)]
        dqb = jnp.concatenate([dqnope] + dqr, axis=1).astype(BF16)
        dcqn = _dot_nt(dqb, wuq_v)
        dwuq_ref[...] += _dot_tn(cqn, dqb)
        nstats_ref[2:3, :] += _sum_rows(dcqn * cqn_f)
        dcq = _rms_bwd(cq, rq, dcqn * qln_v)
        ckv = proj_ref[:, 1536:1664]
        rk = _rms(ckv)
        dckvn = dkc_ref[:, 0:LANES]
        nstats_ref[3:4, 0:LANES] += _sum_rows(dckvn * (ckv * rk))
        dckv = _rms_bwd(ckv, rk, dckvn * kvln_ref[...])
        dkr = _rot_bwd(dkc_ref[:, LANES:2 * LANES], ct_v, st_v, lane)
        pieces += [dcq, dckv, dkr, dgb_ref[...]]
        dproj = jnp.concatenate([piece.astype(BF16) for piece in pieces], axis=1)
        dh = _dot_nt(dproj, wie_ref[...])
        dwie_ref[...] += _dot_tn(h_ref[...], dproj)
        dx_ref[...] = dxr_ref[...] + _norm_mod_bwd(dh, x_ref[...], mod_ref, nw_ref, stats_ref)

        @pl.when(pl.program_id(0) == nsteps - 1)
        def _():
            pltpu.sync_copy(dwie_ref, dwie_out)
            pltpu.sync_copy(dwuq_ref, dwuq_out)
            pltpu.sync_copy(dwuk_ref, dwuk_out)

    return pl.pallas_call(
        body, name="even_pre_bwd", grid=(nsteps,),
        in_specs=[_row_spec(ts, D_MODEL), _row_spec(ts, D_MODEL), _row_spec(ts, EVEN_P), _row_spec(ts, 512), _row_spec(ts, 2 * LANES),
                  _row_spec(ts, 2 * LANES), _row_spec(ts, 512), _row_spec(ts, 512),
                  pl.BlockSpec((B_HEADS, ts, 2 * LANES), lambda i: (0, i, 0)), _row_spec(ts, 2 * LANES), _row_spec(ts, D_MODEL),
                  _full_spec((3, D_MODEL)), _full_spec((1, D_MODEL)), _full_spec((D_MODEL, EVEN_P)),
                  _full_spec((1, LANES)), _full_spec((1, LANES)),
                  _row_spec(ts, LANES), _row_spec(ts, LANES), _row_spec(ts, LANES), _row_spec(ts, LANES),
                  _full_spec((1, B_Q_LORA)), _full_spec((1, B_KV_LORA)), _full_spec((B_Q_LORA, 1536)), _full_spec((512, 1024))],
        out_specs=[_row_spec(ts, D_MODEL), _ANY, _ANY, _ANY, _full_spec((8, D_MODEL), single=False), _full_spec((8, 256), single=False)],
        out_shape=[_sds((S, D_MODEL), F32), _sds((D_MODEL, EVEN_P), F32), _sds((B_Q_LORA, 1536), F32), _sds((512, 1024), F32),
                   _sds((8, D_MODEL), F32), _sds((8, 256), F32)],
        scratch_shapes=[pltpu.VMEM((D_MODEL, EVEN_P), F32), pltpu.VMEM((B_Q_LORA, 1536), F32), pltpu.VMEM((512, 1024), F32)],
        compiler_params=_params(("arbitrary",)),
    )(x, h, proj, dqa, dka, dva, dga, dgb, dqcat, dkcat, dx_res, mod, nw, wie, qn, kn, ca, sa, ct, st, qln, kvln, wuq, wuk)


def _ada_fwd(c_all, w, b):
    n = w.shape[2]

    def body(c_ref, w_ref, b_ref, o_ref):
        cv = c_ref[...]
        o_ref[0] = _dot_f32(cv * _sigmoid(cv), w_ref[0]) + b_ref[0]

    return pl.pallas_call(
        body, name="ada_fwd", grid=(2,),
        in_specs=[pl.BlockSpec((N_DEV, D_MODEL), lambda l: (0, 0)), pl.BlockSpec((1, D_MODEL, n), lambda l: (l, 0, 0)),
                  pl.BlockSpec((1, 1, n), lambda l: (l, 0, 0))],
        out_specs=pl.BlockSpec((1, N_DEV, n), lambda l: (l, 0, 0)),
        out_shape=_sds((2, N_DEV, n), F32),
        compiler_params=_params(("arbitrary",)),
    )(c_all, w, b)


def _ada_bwd(c_all_t, dmod):
    n = dmod.shape[2]

    def body(c_ref, d_ref, o_ref):
        cv = c_ref[...]
        act = cv * _sigmoid(cv)
        dv = d_ref[0]
        acc = act[:, 0:1] * dv[0:1, :]
        for bb in range(1, N_DEV):
            acc = acc + act[:, bb:bb + 1] * dv[bb:bb + 1, :]
        o_ref[0] = acc

    return pl.pallas_call(
        body, name="ada_bwd", grid=(2,),
        in_specs=[pl.BlockSpec((D_MODEL, N_DEV), lambda l: (0, 0)), pl.BlockSpec((1, N_DEV, n), lambda l: (l, 0, 0))],
        out_specs=pl.BlockSpec((1, D_MODEL, n), lambda l: (l, 0, 0)),
        out_shape=_sds((2, D_MODEL, n), F32),
        compiler_params=_params(("arbitrary",)),
    )(c_all_t, dmod)


ADAM_ROW_TILE = 256


def _adam_update(g, w, m, v):
    m_new = ADAM_B1 * m + (1.0 - ADAM_B1) * g
    v_new = ADAM_B2 * v + (1.0 - ADAM_B2) * jnp.square(g)
    m_hat = m_new / (1.0 - ADAM_B1 ** ADAM_STEP)
    v_hat = v_new / (1.0 - ADAM_B2 ** ADAM_STEP)
    return -ADAM_LR * (m_hat / (jnp.sqrt(v_hat) + ADAM_EPS) + ADAM_WD * w), m_new, v_new


SMALL_ROWS = dict(dmod=(0, D_MODEL), norm_w=(6, D_MODEL), final_norm=(8, D_MODEL), a_q_norm=(9, HEAD_DIM), a_k_norm=(10, HEAD_DIM),
                  b_q_lora_norm=(11, B_Q_LORA), b_kv_lora_norm=(12, B_KV_LORA), c_sink=(13, C_HEADS))
SMALL_WEIGHTS = ("ada_b", "norm_w", "final_norm", "a_q_norm", "a_k_norm", "b_q_lora_norm", "b_kv_lora_norm", "c_sink")
LOSS_ROW = 14


def _pack_small(res):
    def padded(v):
        return jnp.concatenate([v, jnp.zeros((v.shape[0], D_MODEL - v.shape[1]), F32)], axis=1)

    rows = [res["dmod"].reshape(6, D_MODEL), res["norm_w"], res["final_norm"].reshape(1, D_MODEL)]
    rows += [padded(res[k]) for k in ("a_q_norm", "a_k_norm", "b_q_lora_norm", "b_kv_lora_norm", "c_sink")]
    return jnp.concatenate(rows + [res["loss_row"], jnp.zeros((1, D_MODEL), F32)], axis=0)


def _adam_small(parts, ws, ms, vs):
    nw = len(SMALL_WEIGHTS)

    def body(*refs):
        p_ref = refs[0]
        w_refs, m_refs, v_refs = refs[1:1 + nw], refs[1 + nw:1 + 2 * nw], refs[1 + 2 * nw:1 + 3 * nw]
        outs = refs[1 + 3 * nw:]
        g_all = p_ref[0]
        for k in range(1, N_DEV):
            g_all = g_all + p_ref[k]
        for idx, name in enumerate(SMALL_WEIGHTS):
            if name == "ada_b":
                g = jnp.concatenate([jnp.concatenate([g_all[3 * l + t:3 * l + t + 1] for t in range(3)], axis=1) for l in range(2)],
                                    axis=0)
            else:
                row, width = SMALL_ROWS[name]
                g = g_all[row:row + w_refs[idx].shape[0], 0:width]
            d, m_new, v_new = _adam_update(g, w_refs[idx][...], m_refs[idx][...], v_refs[idx][...])
            outs[4 * idx][...], outs[4 * idx + 1][...], outs[4 * idx + 2][...], outs[4 * idx + 3][...] = g, d, m_new, v_new
        outs[4 * nw][...] = g_all[LOSS_ROW:LOSS_ROW + 1, 0:LANES]

    out_shape = []
    for w in ws:
        out_shape += [_sds(w.shape, F32)] * 4
    out_shape.append(_sds((1, LANES), F32))
    return pl.pallas_call(body, name="adam_small", out_shape=out_shape,
                          compiler_params=pltpu.CompilerParams(vmem_limit_bytes=VMEM_LIMIT))(parts, *ws, *ms, *vs)


def _adam(parts, w, m, v, name):
    P, R, C = parts.shape
    tr = R if R <= ADAM_ROW_TILE else ADAM_ROW_TILE
    assert R % tr == 0

    def body(p_ref, w_ref, m_ref, v_ref, g_ref, d_ref, nm_ref, nv_ref):
        g = p_ref[0].astype(F32)
        for k in range(1, P):
            g = g + p_ref[k].astype(F32)
        g_ref[...] = g
        d_ref[...], nm_ref[...], nv_ref[...] = _adam_update(g, w_ref[...], m_ref[...], v_ref[...])

    spec = pl.BlockSpec((tr, C), lambda i: (i, 0))
    return pl.pallas_call(
        body, name=name, grid=(R // tr,),
        in_specs=[pl.BlockSpec((P, tr, C), lambda i: (0, i, 0)), spec, spec, spec],
        out_specs=[spec, spec, spec, spec], out_shape=[_sds((R, C), F32)] * 4,
        compiler_params=_params(("arbitrary",)),
    )(parts, w, m, v)


_ANY = pl.BlockSpec(memory_space=pl.ANY)
CHIP_FLIPS = ((1, 0), (0, 1), (1, 1))
DEV_FLIPS = tuple((dx, dy, dc) for dx in (0, 1) for dy in (0, 1) for dc in (0, 1) if dx + dy + dc)


def _flip(a, d):
    return a if d == 0 else 1 - a


def _my_place():
    return lax.axis_index("x"), lax.axis_index("y"), lax.axis_index("c")


def _gather8_copies(ins, outs, send_sems, recv_sems, loc_sems):
    x, y, c = _my_place()
    me = 4 * x + 2 * y + c
    copies = []
    for a in range(len(ins)):
        copies.append(pltpu.make_async_copy(ins[a], outs[a].at[me], loc_sems.at[a]))
        for k, (dx, dy, dc) in enumerate(DEV_FLIPS):
            copies.append(pltpu.make_async_remote_copy(
                src_ref=ins[a], dst_ref=outs[a].at[me], send_sem=send_sems.at[a, k], recv_sem=recv_sems.at[a, k],
                device_id=(_flip(x, dx), _flip(y, dy), _flip(c, dc)), device_id_type=MESH_ID))
    return copies


def _gather8_sems(n):
    return [pltpu.SemaphoreType.DMA((n, 7)), pltpu.SemaphoreType.DMA((n, 7)), pltpu.SemaphoreType.DMA((n,))]


def _gather_dev8(arrs, name):
    n = len(arrs)

    def body(*refs):
        copies = _gather8_copies(refs[:n], refs[n:2 * n], *refs[2 * n:])
        for cp in copies:
            cp.start()
        for cp in copies:
            cp.wait()

    return pl.pallas_call(
        body, name=name, in_specs=[_ANY] * n, out_specs=[_ANY] * n,
        out_shape=[_sds((N_DEV,) + a.shape, a.dtype) for a in arrs], scratch_shapes=_gather8_sems(n),
    )(*arrs)


class _Exchange:
    def __init__(self, arrs, out_shapes, n_sems, phases):
        self.arrs, self.out_shapes, self.n_sems, self._phases = list(arrs), list(out_shapes), n_sems, phases

    @property
    def n(self):
        return len(self.arrs)

    def sem_shapes(self):
        return [pltpu.SemaphoreType.DMA((self.n, self.n_sems)), pltpu.SemaphoreType.DMA((self.n, self.n_sems)),
                pltpu.SemaphoreType.DMA((self.n,))]

    def phases(self, ins, outs, sems):
        return self._phases(ins, outs, *sems)

    def run(self, name):
        n = self.n

        def body(*refs):
            start, mid, end = self.phases(refs[:n], refs[n:2 * n], refs[2 * n:])
            start()
            mid()
            end()

        return pl.pallas_call(body, name=name, in_specs=[_ANY] * n, out_specs=[_ANY] * n, out_shape=self.out_shapes,
                              scratch_shapes=self.sem_shapes())(*self.arrs)

    def run_with_gather(self, gather_arrs, name):
        n, g = self.n, len(gather_arrs)

        def body(*refs):
            ins, g_ins, outs, g_outs = refs[:n], refs[n:n + g], refs[n + g:2 * n + g], refs[2 * n + g:2 * (n + g)]
            sems = refs[2 * (n + g):]
            start, mid, end = self.phases(ins, outs, sems[:3])
            copies = _gather8_copies(g_ins, g_outs, *sems[3:])
            start()
            for cp in copies:
                cp.start()
            mid()
            end()
            for cp in copies:
                cp.wait()

        outs = pl.pallas_call(
            body, name=name, in_specs=[_ANY] * (n + g), out_specs=[_ANY] * (n + g),
            out_shape=self.out_shapes + [_sds((N_DEV,) + a.shape, a.dtype) for a in gather_arrs],
            scratch_shapes=self.sem_shapes() + _gather8_sems(g))(*self.arrs, *gather_arrs)
        return outs[:n], outs[n:]


def _gather_halves_phases(ins, outs, send_sems, recv_sems, loc_sems):
    n = len(ins)
    x, y, c = _my_place()
    chip = 2 * x + y
    sibling = (x, y, 1 - c)
    peers = [(_flip(x, dx), _flip(y, dy)) for dx, dy in CHIP_FLIPS]

    def remote(src, p, half, a, k, to):
        return pltpu.make_async_remote_copy(src_ref=src, dst_ref=outs[a].at[p, half], send_sem=send_sems.at[a, k],
                                            recv_sem=recv_sems.at[a, k], device_id=to, device_id_type=MESH_ID)

    def local(a):
        return pltpu.make_async_copy(ins[a], outs[a].at[chip], loc_sems.at[a])

    def first(a, k):
        return remote(ins[a].at[c], chip, c, a, k, (*peers[k], c))

    def passed(a, k):
        p = 2 * peers[k][0] + peers[k][1]
        return remote(outs[a].at[p, c], p, c, a, 3 + k, sibling)

    def start():
        for a in range(n):
            local(a).start()
            for k in range(3):
                first(a, k).start()

    def mid():
        for a in range(n):
            for k in range(3):
                p = 2 * peers[k][0] + peers[k][1]
                remote(outs[a].at[p, c], p, c, a, k, sibling).wait_recv()
                passed(a, k).start()

    def end():
        for a in range(n):
            for k in range(3):
                p = 2 * peers[k][0] + peers[k][1]
                remote(outs[a].at[p, 1 - c], p, 1 - c, a, 3 + k, sibling).wait_recv()
        for a in range(n):
            for k in range(3):
                first(a, k).wait_send()
                passed(a, k).wait_send()
            local(a).wait()

    return start, mid, end


def _gather_chip4_halves(arrs):
    return _Exchange(arrs, [_sds((N_CHIPS,) + a.shape, a.dtype) for a in arrs], 6, _gather_halves_phases)


def _reduce_phases(ins, outs, send_sems, recv_sems, loc_sems):
    n = len(ins)
    x, y, c = _my_place()
    chip = 2 * x + y
    sibling = (x, y, 1 - c)
    peers = [(_flip(x, dx), _flip(y, dy)) for dx, dy in CHIP_FLIPS]

    def remote(src, slot, a, k, to):
        return pltpu.make_async_remote_copy(src_ref=src, dst_ref=outs[a].at[slot], send_sem=send_sems.at[a, k],
                                            recv_sem=recv_sems.at[a, k], device_id=to, device_id_type=MESH_ID)

    def local(a):
        return pltpu.make_async_copy(ins[a].at[chip], outs[a].at[2 * chip + c], loc_sems.at[a])

    def own(a):
        return remote(ins[a].at[chip], 2 * chip + c, a, 0, sibling)

    def first(a, k):
        return remote(ins[a].at[2 * peers[k][0] + peers[k][1]], 2 * chip + c, a, 1 + k, (*peers[k], c))

    def passed(a, k):
        slot = 2 * (2 * peers[k][0] + peers[k][1]) + c
        return remote(outs[a].at[slot], slot, a, 4 + k, sibling)

    def start():
        for a in range(n):
            local(a).start()
            own(a).start()
            for k in range(3):
                first(a, k).start()

    def mid():
        for a in range(n):
            for k in range(3):
                slot = 2 * (2 * peers[k][0] + peers[k][1]) + c
                remote(outs[a].at[slot], slot, a, 1 + k, sibling).wait_recv()
                passed(a, k).start()

    def end():
        for a in range(n):
            remote(outs[a].at[2 * chip + 1 - c], 2 * chip + 1 - c, a, 0, sibling).wait_recv()
            for k in range(3):
                slot = 2 * (2 * peers[k][0] + peers[k][1]) + 1 - c
                remote(outs[a].at[slot], slot, a, 4 + k, sibling).wait_recv()
        for a in range(n):
            own(a).wait_send()
            for k in range(3):
                first(a, k).wait_send()
                passed(a, k).wait_send()
            local(a).wait()

    return start, mid, end


def _reduce_exchange(arrs):
    return _Exchange(arrs, [_sds((N_DEV,) + a.shape[1:], a.dtype) for a in arrs], 7, _reduce_phases)


def _even_in_layout(w):
    return jnp.concatenate([w[:, 0:1696], jnp.zeros((w.shape[0], 96), w.dtype), w[:, 1696:2208]], axis=1)


def _even_in_unlayout(g):
    return jnp.concatenate([g[:, 0:1696], g[:, 1792:2304]], axis=1)


def _uq_layout(w):
    per = B_NOPE + B_ROPE
    pad = jnp.zeros((w.shape[0], LANES - B_ROPE), w.dtype)
    nope = [w[:, per * h:per * h + B_NOPE] for h in range(B_HEADS)]
    rope = [jnp.concatenate([w[:, per * h + B_NOPE:per * (h + 1)], pad], axis=1) for h in range(B_HEADS)]
    return jnp.concatenate(nope + rope, axis=1)


def _uq_unlayout(g):
    parts = []
    for h in range(B_HEADS):
        parts += [g[:, B_NOPE * h:B_NOPE * (h + 1)], g[:, 512 + LANES * h:512 + LANES * h + B_ROPE]]
    return jnp.concatenate(parts, axis=1)


def _block_diag(blocks):
    rows = []
    for h, blk in enumerate(blocks):
        r, cdim = blk.shape
        n = len(blocks)
        rows.append(jnp.concatenate([jnp.zeros((r, cdim * h), blk.dtype), blk, jnp.zeros((r, cdim * (n - 1 - h)), blk.dtype)],
                                    axis=1))
    return jnp.concatenate(rows, axis=0)


def _uk_layout(w):
    return _block_diag([w[:, h, :].T for h in range(B_HEADS)])


def _uk_unlayout(g):
    return jnp.stack([g[B_NOPE * h:B_NOPE * (h + 1), LANES * h:LANES * (h + 1)].T for h in range(B_HEADS)], axis=1)


def _uv_layout(w):
    return _block_diag([w[:, h, :] for h in range(B_HEADS)])


def _uv_unlayout(g):
    return jnp.stack([g[LANES * h:LANES * (h + 1), B_V * h:B_V * (h + 1)] for h in range(B_HEADS)], axis=1)


def _rope_tables(S):
    inv = ROPE_THETA ** (-jnp.arange(0, 32, 2, dtype=F32) / 32)
    tok = jnp.arange(S)

    def tab(pos):
        ang = pos.astype(F32)[:, None] * inv[None, :]
        cos, sin = jnp.cos(ang), jnp.sin(ang)
        return jnp.concatenate([cos, cos], axis=1), jnp.concatenate([-sin, sin], axis=1)

    cr, sr = tab(tok // GRID_W)
    cc, sc = tab(tok % GRID_W)
    ct, st = tab(tok)
    return (jnp.tile(jnp.concatenate([cr, cc], axis=1), (1, 2)), jnp.tile(jnp.concatenate([sr, sc], axis=1), (1, 2)),
            jnp.tile(ct, (1, 4)), jnp.tile(st, (1, 4)))


A_TQ, A_TK, A_SUB = 512, 4096, 512
B_TQ, B_TK, B_SUB = 128, 4096, 1024
B_BWD_TK, B_BWD_SUB = 4096, 512
C_T = 256
C_BLOCKS_PER_STEP = 8
KV_SHARE = 2


def _local_step(x0, tgt, mod, norm_w, wie, wuq, wuk, wuv, late_shards, a_q_norm, a_k_norm, q_lora_norm, kv_lora_norm,
                c_sink, final_norm):
    S = x0.shape[0]
    mod3 = mod.reshape(2, 3, D_MODEL)
    ca, sa, ct, st = _rope_tables(S)
    qn = jnp.tile(a_q_norm.reshape(1, HEAD_DIM), (1, 2))
    kn = jnp.tile(a_k_norm.reshape(1, HEAD_DIM), (1, 2))
    qln, kvln = q_lora_norm.reshape(1, B_Q_LORA), kv_lora_norm.reshape(1, B_KV_LORA)
    nw0, nw1 = norm_w[0:1], norm_w[1:2]
    gate0, gate1 = mod3[0, 2:3], mod3[1, 2:3]
    a_tq, a_tk, b_tq, b_tk, bb_tk, c_t = min(A_TQ, S), min(A_TK, S), min(B_TQ, S), min(B_TK, S), min(B_BWD_TK, S), min(C_T, S)
    a_sub, b_sub, bb_sub = min(A_SUB, a_tk), min(B_SUB, b_tk), min(B_BWD_SUB, bb_tk)

    h0, proj_e, qa, ka, va, qcat, kcat, ka_t, va_t, kcat_t = _even_pre_fwd(x0, mod3[0], nw0, wie, qn, kn, ca, sa, ct, st,
                                                                           qln, kvln, wuq, wuk)
    oa, lse_a, woe_g, wio_g, woo_g = _pp_fwd(qa, ka, va_t, kdiv=KV_SHARE, tq=a_tq, tk=a_tk, sub=a_sub, name="attn_a_fwd",
                                             side=_gather_chip4_halves(late_shards))
    woe = woe_g.reshape(D_MODEL, D_MODEL)
    wio = wio_g.reshape(N_CHIPS, D_MODEL, ODD_IN // N_CHIPS)
    woo = woo_g.reshape(D_MODEL, D_MODEL)
    olat, lse_b = _mla_fwd(qcat, kcat, kcat_t, tq=b_tq, tk=b_tk, sub=b_sub)
    y0, x1 = _even_post_fwd(oa, olat, proj_e, x0, gate0, wuv, woe)
    h1, proj_o, qc, kc, vc, kc_t, vc_t = _odd_pre_fwd(x1, mod3[1], nw1, wio)
    slopes = 2.0 ** (-8.0 * jnp.arange(1, C_HEADS + 1, dtype=F32) / C_HEADS)
    slope_rows = jnp.repeat(slopes.reshape(C_HEADS // 2, 2), c_t, axis=1)[:, None, :]
    sink_rows = jnp.repeat(c_sink.reshape(C_HEADS // 2, 2), c_t, axis=1)[:, None, :]
    win_dist = _win_dist_table(S, c_t)
    oc, lse_c = _win_fwd(qc, kc, vc_t, win_dist, slope_rows, sink_rows, kdiv=KV_SHARE, tq=c_t, nbs=C_BLOCKS_PER_STEP,
                         name="attn_c_fwd")
    doc, dgc, dx2, dwoo, st_f = _odd_post(oc, proj_o, x1, gate1, woo, final_norm.reshape(1, D_MODEL), tgt)
    dqc, dkc, dvc, dsink_raw = _win_bwd(qc, kc, kc_t, vc, oc, doc, lse_c, win_dist, slope_rows, sink_rows, kdiv=KV_SHARE, tq=c_t,
                                        nbs=C_BLOCKS_PER_STEP, name="attn_c_bwd")
    dx1, dwio, st_1 = _odd_pre_bwd(dqc, dkc, dvc, dgc, h1, x1, dx2, mod3[1], nw1, wio)
    doa, dga, dgb, dolat, dwoe, dwuv, st_e = _even_post_bwd(dx1, y0, oa, olat, proj_e, gate0, wuv, woe)
    late_grads = _reduce_exchange([dwoe.reshape(N_CHIPS, D_MODEL // N_CHIPS, D_MODEL), dwio,
                                   dwoo.reshape(N_CHIPS, D_MODEL // N_CHIPS, D_MODEL)])
    dqa, dka, dva, p_woe, p_wio, p_woo = _pp_bwd(qa, ka, ka_t, va, oa, doa, lse_a, kdiv=KV_SHARE, tq=a_tq, tk=a_tk, sub=a_sub,
                                                 name="attn_a_bwd", side=late_grads)
    dqcat, dkcat = _mla_bwd(qcat, kcat, kcat_t, olat, dolat, lse_b, tq=b_tq, tk=bb_tk, sub=bb_sub)
    dx0, dwie, dwuq, dwuk, st_0, nst = _even_pre_bwd(x0, h0, proj_e, dqa, dka, dva, dga, dgb, dqcat, dkcat, dx1, mod3[0], nw0,
                                                     wie, qn, kn, ca, sa, ct, st, qln, kvln, wuq, wuk)
    dsink_pairs = jnp.stack([dsink_raw[:, 0, 0], dsink_raw[:, 1, 0]], axis=1).reshape(C_HEADS)
    return dict(
        loss_row=st_f[2:3], dx=dx0,
        dmod=jnp.stack([jnp.concatenate([st_0[0], st_0[1], st_e[0]]), jnp.concatenate([st_1[0], st_1[1], st_f[1]])]),
        norm_w=jnp.stack([st_0[2], st_1[2]]), final_norm=st_f[0],
        a_q_norm=nst[0:1, 0:HEAD_DIM], a_k_norm=nst[1:2, 0:HEAD_DIM], b_q_lora_norm=nst[2:3, :], b_kv_lora_norm=nst[3:4, 0:LANES],
        c_sink=dsink_pairs.reshape(1, C_HEADS),
        even_w_in=dwie, b_w_uq=dwuq, b_w_uk=dwuk, b_w_uv=dwuv, even_w_out=p_woe, odd_w_in=p_wio, odd_w_out=p_woo)


WEIGHT_NAMES = ("norm_w", "ada_w", "ada_b", "even_w_in", "a_q_norm", "a_k_norm", "b_q_lora_norm", "b_kv_lora_norm", "b_w_uq",
                "b_w_uk", "b_w_uv", "even_w_out", "odd_w_in", "c_sink", "odd_w_out", "final_norm")


def _cols_to_chips(g):
    r, n4 = g.shape
    return jnp.transpose(g.reshape(r, N_CHIPS, n4 // N_CHIPS), (1, 0, 2))


def _chips_to_cols(g):
    p, r, n = g.shape
    return jnp.transpose(g, (1, 0, 2)).reshape(r, p * n)


def kernel(x, c, norm_w, ada_w, ada_b, even_w_in, a_q_norm, a_k_norm, b_q_lora_norm, b_kv_lora_norm, b_w_uq, b_w_uk, b_w_uv, even_w_out, odd_w_in, c_sink, odd_w_out, final_norm, loss_target, m_norm_w, m_ada_w, m_ada_b, m_even_w_in, m_a_q_norm, m_a_k_norm, m_b_q_lora_norm, m_b_kv_lora_norm, m_b_w_uq, m_b_w_uk, m_b_w_uv, m_even_w_out, m_odd_w_in, m_c_sink, m_odd_w_out, m_final_norm, v_norm_w, v_ada_w, v_ada_b, v_even_w_in, v_a_q_norm, v_a_k_norm, v_b_q_lora_norm, v_b_kv_lora_norm, v_b_w_uq, v_b_w_uk, v_b_w_uv, v_even_w_out, v_odd_w_in, v_c_sink, v_odd_w_out, v_final_norm):
    given = dict(locals())
    xi, yi, ci = _my_place()
    chip = 2 * xi + yi
    dev = 2 * chip + ci
    n_ada = ada_w.shape[2]

    (c_all,) = _gather_dev8([c], "gather_c")
    c_all = c_all.reshape(N_DEV, D_MODEL)
    bias = lax.dynamic_slice_in_dim(ada_b, chip * n_ada, n_ada, axis=1).reshape(2, 1, n_ada)
    mod_cols = _ada_fwd(c_all, ada_w, bias)
    def halves(w):
        return w.astype(BF16).reshape((2, w.shape[0] // 2) + w.shape[1:])

    mod_all, wie_g, wuq_g = _gather_chip4_halves([mod_cols, halves(even_w_in[0]), halves(b_w_uq[0])]).run("gather_weights")
    wie_g = wie_g.reshape(N_CHIPS, D_MODEL, EVEN_IN // N_CHIPS)
    wuq_g = wuq_g.reshape(N_CHIPS, B_Q_LORA, -1)
    mod = jnp.transpose(lax.dynamic_index_in_dim(mod_all, dev, axis=2, keepdims=False), (1, 0, 2)).reshape(2, 3 * D_MODEL)

    res = _local_step(
        x[0], loss_target[0], mod, norm_w,
        _even_in_layout(_chips_to_cols(wie_g)), _uq_layout(_chips_to_cols(wuq_g)), _uk_layout(b_w_uk[0].astype(BF16)),
        _uv_layout(b_w_uv[0].astype(BF16)), [halves(even_w_out[0]), halves(odd_w_in[0]), halves(odd_w_out[0])],
        a_q_norm, a_k_norm, b_q_lora_norm, b_kv_lora_norm, c_sink, final_norm)

    latent = jnp.stack([_uk_unlayout(res["b_w_uk"]).reshape(B_KV_LORA, 512),
                        _uv_unlayout(res["b_w_uv"]).reshape(B_KV_LORA, 512)]).astype(BF16)
    (p_wie, p_wuq), (small_all, latent_all) = _reduce_exchange(
        [_cols_to_chips(_even_in_unlayout(res["even_w_in"].astype(BF16))), _cols_to_chips(_uq_unlayout(res["b_w_uq"].astype(BF16)))]
    ).run_with_gather([_pack_small(res), latent], "reduce_exchange")
    shard_parts = dict(even_w_in=p_wie, b_w_uq=p_wuq, **{k: res[k] for k in ("even_w_out", "odd_w_in", "odd_w_out")})
    dmod_all = small_all[:, 0:6, :].reshape(N_DEV, 2, 3 * D_MODEL)
    dmod_cols = jnp.transpose(lax.dynamic_slice_in_dim(dmod_all, chip * n_ada, n_ada, axis=2), (1, 0, 2))
    parts = dict(shard_parts)
    parts["ada_w"] = _ada_bwd(c_all.T, dmod_cols).reshape(1, 2 * D_MODEL, n_ada)
    parts["b_w_uk"], parts["b_w_uv"] = latent_all[:, 0], latent_all[:, 1]

    def as2d(a):
        return a.reshape((-1, a.shape[-1]) if a.ndim > 1 else (1, a.shape[0]))

    results = {}
    small_outs = _adam_small(small_all, *[[as2d(given[pre + k]) for k in SMALL_WEIGHTS] for pre in ("", "m_", "v_")])
    for idx, k in enumerate(SMALL_WEIGHTS):
        results[k] = small_outs[4 * idx:4 * idx + 4]
    for k, p in parts.items():
        shape2 = (p.shape[-2], p.shape[-1])
        results[k] = _adam(p, given[k].reshape(shape2), given["m_" + k].reshape(shape2), given["v_" + k].reshape(shape2),
                           "adam_" + k)
    by_kind = [[results[k][t].reshape(given[k].shape) for k in WEIGHT_NAMES] for t in range(4)]
    return (small_outs[-1][0, 0], res["dx"][None], *by_kind[0], *by_kind[1], *by_kind[2], *by_kind[3])
```

```python
import functools

import numpy as np
import jax
import jax.numpy as jnp
from jax import lax
from jax.experimental import pallas as pl
from jax.experimental.pallas import tpu as pltpu

F32 = jnp.float32
BF16 = jnp.bfloat16
HIGHEST = lax.Precision.HIGHEST
MESH_ID = pl.DeviceIdType.MESH

D_MODEL = 1024
HEAD_DIM = 64
GRID_W = 64
EPS = 1e-6
ROPE_THETA = 10000.0
B_HEADS, B_NOPE, B_ROPE, B_V = 8, 64, 32, 64
B_Q_LORA, B_KV_LORA = 256, 128
C_HEADS = 16
WINDOW = 128
EVEN_IN, ODD_IN = 2208, 2560
EVEN_P = 2304
N_CHIPS, N_DEV = 4, 8
LANES = 128
NEG = -1e30
VMEM_LIMIT = 60 * 1024 * 1024

ADAM_LR, ADAM_B1, ADAM_B2, ADAM_EPS, ADAM_WD, ADAM_STEP = 0.001, 0.9, 0.999, 1e-08, 0.01, 10

ROW_TILE = 512
IN_PROJ_ROW_TILE = 256


def _dot(a, b):
    return lax.dot_general(a, b, (((1,), (0,)), ((), ())), preferred_element_type=F32)


def _dot_nt(a, b):
    return lax.dot_general(a, b, (((1,), (1,)), ((), ())), preferred_element_type=F32)


def _dot_tn(a, b):
    return lax.dot_general(a, b, (((0,), (0,)), ((), ())), preferred_element_type=F32)


def _dot_f32(a, b):
    return lax.dot_general(a, b, (((1,), (0,)), ((), ())), precision=HIGHEST, preferred_element_type=F32)


def _sigmoid(x):
    return 1.0 / (1.0 + jnp.exp(-x))


def _silu_and_grad(g):
    s = _sigmoid(g)
    return g * s, s * (1.0 + g * (1.0 - s))


def _lane_iota():
    return lax.broadcasted_iota(jnp.int32, (1, LANES), 1)


def _partner(x, lane):
    return jnp.where((lane % 32) < 16, pltpu.roll(x, LANES - 16, 1), pltpu.roll(x, 16, 1))


def _rot(x, cos, sin_signed, lane):
    return x * cos + _partner(x, lane) * sin_signed


def _rot_bwd(dy, cos, sin_signed, lane):
    return dy * cos + _partner(dy * sin_signed, lane)


def _rms(x):
    return lax.rsqrt(jnp.mean(x * x, axis=-1, keepdims=True) + EPS)


def _rms_bwd(x, r, g):
    return r * g - x * (r * r * r) * jnp.mean(x * g, axis=-1, keepdims=True)


def _seg_mean(v, seg_ones):
    hi = v.astype(BF16)
    lo = (v - hi.astype(F32)).astype(BF16)
    return (_dot(hi, seg_ones) + _dot(lo, seg_ones)) * (1.0 / HEAD_DIM)


def _dup_heads(x, lane):
    swapped = pltpu.roll(x, HEAD_DIM, 1)
    lo = lane < HEAD_DIM
    return jnp.concatenate([jnp.where(lo, x, swapped), jnp.where(lo, swapped, x)], axis=1)


def _fold_heads(x2, lane):
    a, b = x2[:, 0:LANES], x2[:, LANES:2 * LANES]
    return jnp.where(lane < HEAD_DIM, a + pltpu.roll(a, HEAD_DIM, 1), b + pltpu.roll(b, HEAD_DIM, 1))


def _row_spec(ts, cols):
    return pl.BlockSpec((ts, cols), lambda i: (i, 0))


def _full_spec(shape, single=True):
    nd = len(shape)
    if single:
        return pl.BlockSpec(shape, lambda i: (0,) * nd, pipeline_mode=pl.Buffered(1))
    return pl.BlockSpec(shape, lambda i: (0,) * nd)


def _sds(shape, dtype):
    return jax.ShapeDtypeStruct(shape, dtype)


def _params(sem):
    return pltpu.CompilerParams(dimension_semantics=sem, vmem_limit_bytes=VMEM_LIMIT)


def _even_pre_fwd(x, mod, nw, wie, qn, kn, seg, ca, sa, ct, st, qln, kvln, wuq, wuk):
    S = x.shape[0]
    ts = min(IN_PROJ_ROW_TILE, S)

    def body(x_ref, mod_ref, nw_ref, wie_ref, qn_ref, kn_ref, seg_ref, ca_ref, sa_ref, ct_ref, st_ref, qln_ref,
             kvln_ref, wuq_ref, wuk_ref, h_ref, proj_ref, qa_ref, ka_ref, va_ref, qcat_ref, kcat_ref, kat_ref, vat_ref, kcatt_ref):
        xv = x_ref[...]
        h = (xv * _rms(xv) * nw_ref[...]) * (1.0 + mod_ref[1:2, :]) + mod_ref[0:1, :]
        hb = h.astype(BF16)
        h_ref[...] = hb
        proj = _dot(hb, wie_ref[...])
        proj_ref[...] = proj
        lane = _lane_iota()
        ca_v, sa_v, ct_v, st_v = ca_ref[...], sa_ref[...], ct_ref[...], st_ref[...]
        seg_v = seg_ref[...]
        for cb in range(4):
            xc = proj[:, LANES * cb:LANES * (cb + 1)]
            r = lax.rsqrt(_seg_mean(xc * xc, seg_v) + EPS)
            y = _rot(xc * r * qn_ref[...], ca_v, sa_v, lane)
            qa_ref[:, LANES * cb:LANES * (cb + 1)] = (y * 0.125).astype(BF16)
        kc = proj[:, 512:640]
        r = lax.rsqrt(_seg_mean(kc * kc, seg_v) + EPS)
        ka_v = _dup_heads(_rot(kc * r * kn_ref[...], ca_v, sa_v, lane), lane)
        ka_ref[...] = ka_v.astype(BF16)
        kat_ref[...] = ka_v.T.astype(BF16)
        va_v = _dup_heads(proj[:, 640:768], lane)
        va_ref[...] = va_v.astype(BF16)
        vat_ref[...] = va_v.T.astype(BF16)
        cq = proj[:, 1280:1536]
        cqn = (cq * _rms(cq) * qln_ref[...]).astype(BF16)
        ckv = proj[:, 1536:1664]
        ckvn = ckv * _rms(ckv) * kvln_ref[...]
        qb = _dot(cqn, wuq_ref[...])
        qlat = _dot(qb[:, 0:512].astype(BF16), wuk_ref[...])
        for hh in range(B_HEADS):
            qcat_ref[hh, :, 0:LANES] = qlat[:, LANES * hh:LANES * (hh + 1)].astype(BF16)
            qr = _rot(qb[:, 512 + LANES * hh:512 + LANES * (hh + 1)], ct_v, st_v, lane)
            qcat_ref[hh, :, LANES:2 * LANES] = qr.astype(BF16)
        kr = _rot(proj[:, 1664:1792], ct_v, st_v, lane)
        kcat_ref[:, 0:LANES] = ckvn.astype(BF16)
        kcat_ref[:, LANES:2 * LANES] = kr.astype(BF16)
        kcatt_ref[0:LANES, :] = ckvn.T.astype(BF16)
        kcatt_ref[LANES:2 * LANES, :] = kr.T.astype(BF16)

    col_spec = lambda rows: pl.BlockSpec((rows, ts), lambda i: (0, i))
    return pl.pallas_call(
        body, name="even_pre_fwd", grid=(S // ts,),
        in_specs=[_row_spec(ts, D_MODEL), _full_spec((3, D_MODEL)), _full_spec((1, D_MODEL)), _full_spec((D_MODEL, EVEN_P)),
                  _full_spec((1, LANES)), _full_spec((1, LANES)), _full_spec((LANES, LANES)),
                  _row_spec(ts, LANES), _row_spec(ts, LANES), _row_spec(ts, LANES), _row_spec(ts, LANES),
                  _full_spec((1, B_Q_LORA)), _full_spec((1, B_KV_LORA)), _full_spec((B_Q_LORA, 1536)), _full_spec((512, 1024))],
        out_specs=[_row_spec(ts, D_MODEL), _row_spec(ts, EVEN_P), _row_spec(ts, 512), _row_spec(ts, 2 * LANES), _row_spec(ts, 2 * LANES),
                   pl.BlockSpec((B_HEADS, ts, 2 * LANES), lambda i: (0, i, 0)), _row_spec(ts, 2 * LANES),
                   col_spec(2 * LANES), col_spec(2 * LANES), col_spec(2 * LANES)],
        out_shape=[_sds((S, D_MODEL), BF16), _sds((S, EVEN_P), F32), _sds((S, 512), BF16), _sds((S, 2 * LANES), BF16),
                   _sds((S, 2 * LANES), BF16), _sds((B_HEADS, S, 2 * LANES), BF16), _sds((S, 2 * LANES), BF16),
                   _sds((2 * LANES, S), BF16), _sds((2 * LANES, S), BF16), _sds((2 * LANES, S), BF16)],
        compiler_params=_params(("arbitrary",)),
    )(x, mod, nw, wie, qn, kn, seg, ca, sa, ct, st, qln, kvln, wuq, wuk)


MLA_SCALE = (B_NOPE + B_ROPE) ** -0.5
LOG2E = 1.4426950408889634

def _row_lo():
    return lax.broadcasted_iota(jnp.int32, (LANES, 1), 0) < HEAD_DIM


def _stack_cols(vT, rlo):
    zero = jnp.zeros_like(vT)
    return jnp.concatenate([jnp.where(rlo, vT, zero), jnp.where(rlo, zero, vT)], axis=1)


def _stack_rows(v, lo):
    zero = jnp.zeros_like(v)
    return jnp.concatenate([jnp.where(lo, v, zero), jnp.where(lo, zero, v)], axis=0)


def _pick_halves_T(xT, rlo, t):
    return jnp.where(rlo, xT[:, 0:t], xT[:, t:2 * t]).T


def _side_split(refs, n_in, n_out, n_scratch, side):
    ns = side.n if side is not None else 0
    cuts = np.cumsum([0, n_in, ns, n_out, ns, n_scratch])
    return [refs[a:b] for a, b in zip(cuts[:-1], cuts[1:])] + [refs[cuts[-1]:]]


def _side_hooks(side, side_ins, side_outs, side_sems, step, total):
    if side is None:
        return lambda: None
    start, mid, end = side.phases(side_ins, side_outs, side_sems)
    pl.when(step == 0)(start)
    pl.when(step == total // 2)(mid)
    return lambda: pl.when(step == total - 1)(end)


def _side_specs(side):
    if side is None:
        return [], [], [], [], []
    return list(side.arrs), [_ANY] * side.n, [_ANY] * side.n, list(side.out_shapes), side.sem_shapes()


def _pp_fwd(q, k, vT, *, kdiv, tq, tk, sub, name, side=None):
    S = k.shape[0]; nb = q.shape[1] // LANES; nq = S // tq; nkv = S // tk; nsub = tk // sub

    def body(*refs):
        (q_ref, k_ref, vT_ref), side_ins, (o_ref, lse_ref), side_outs, (qs, m_s, l_s, acc), side_sems = _side_split(refs, 3, 2, 4, side)
        j = pl.program_id(2)
        rlo = _row_lo()
        step = (pl.program_id(0) * nq + pl.program_id(1)) * nkv + j
        side_end = _side_hooks(side, side_ins, side_outs, side_sems, step, nb * nq * nkv)

        @pl.when(j == 0)
        def _():
            qs[...] = _stack_cols(q_ref[...].astype(F32).T, rlo).astype(BF16)
            m_s[...] = jnp.full((1, 2 * tq), NEG, F32)
            l_s[...] = jnp.zeros((1, 2 * tq), F32)
            acc[...] = jnp.zeros((LANES, 2 * tq), F32)

        qsv = qs[...]
        m, l, a = m_s[...], l_s[...], acc[...]
        s_cur = _dot(k_ref[0:sub, :], qsv)
        for t in range(nsub):
            if t + 1 < nsub:
                s_next = _dot(k_ref[sub * (t + 1):sub * (t + 2), :], qsv)
            m_new = jnp.maximum(m, jnp.max(s_cur, axis=0, keepdims=True))
            alpha = jnp.exp(m - m_new)
            p = jnp.exp(s_cur - m_new)
            l = alpha * l + jnp.sum(p, axis=0, keepdims=True)
            a = alpha * a + _dot(vT_ref[:, sub * t:sub * (t + 1)], p.astype(BF16))
            m = m_new
            if t + 1 < nsub:
                s_cur = s_next
        m_s[...], l_s[...], acc[...] = m, l, a

        @pl.when(j == nkv - 1)
        def _():
            l_f = l_s[...]
            o_ref[...] = _pick_halves_T(acc[...] / l_f, rlo, tq).astype(BF16)
            lse_ref[0, 0] = m_s[...] + jnp.log(l_f)

        side_end()

    s_args, s_in, s_out, s_shapes, s_sems = _side_specs(side)
    return pl.pallas_call(
        body, name=name, grid=(nb, nq, nkv),
        in_specs=[pl.BlockSpec((tq, LANES), lambda b, i, j: (i, b)), pl.BlockSpec((tk, LANES), lambda b, i, j: (j, b // kdiv)),
                  pl.BlockSpec((LANES, tk), lambda b, i, j: (b // kdiv, j))] + s_in,
        out_specs=[pl.BlockSpec((tq, LANES), lambda b, i, j: (i, b)),
                   pl.BlockSpec((1, 1, 1, 2 * tq), lambda b, i, j: (b, i, 0, 0))] + s_out,
        out_shape=[_sds((S, nb * LANES), BF16), _sds((nb, nq, 1, 2 * tq), F32)] + s_shapes,
        scratch_shapes=[pltpu.VMEM((LANES, 2 * tq), BF16), pltpu.VMEM((1, 2 * tq), F32), pltpu.VMEM((1, 2 * tq), F32),
                        pltpu.VMEM((LANES, 2 * tq), F32)] + s_sems,
        compiler_params=_params(("arbitrary",) * 3))(q, k, vT, *s_args)


def _pp_bwd(q, k, kT, v, o, do, lse, *, kdiv, tq, tk, sub, name, side=None):
    S = k.shape[0]; nb = q.shape[1] // LANES; nkb = k.shape[1] // LANES; nq = S // tq; nkv = S // tk; nsub = tk // sub

    def body(*refs):
        ((q_ref, k_ref, kT_ref, v_ref, o_ref, do_ref, lse_ref), side_ins, (dq_ref, dk_ref, dv_ref), side_outs,
         (qsT, qs, dosT, dos, delta_s, dq_acc), side_sems) = _side_split(refs, 7, 3, 6, side)
        b, i, j = pl.program_id(0), pl.program_id(1), pl.program_id(2)
        rlo = _row_lo()
        lo = lax.broadcasted_iota(jnp.int32, (1, LANES), 1) < HEAD_DIM
        side_end = _side_hooks(side, side_ins, side_outs, side_sems, (b * nq + i) * nkv + j, nb * nq * nkv)

        @pl.when((b % kdiv == 0) & (i == 0) & (j == 0))
        def _():
            dk_ref[...] = jnp.zeros((S, LANES), F32)
            dv_ref[...] = jnp.zeros((S, LANES), F32)

        @pl.when(j == 0)
        def _():
            qv = q_ref[...]
            qs[...] = _stack_rows(qv, lo)
            qsT[...] = _stack_cols(qv.astype(F32).T, rlo).astype(BF16)
            dov = do_ref[...].astype(F32)
            dos[...] = _stack_rows(dov.astype(BF16), lo)
            dosT[...] = _stack_cols(dov.T, rlo).astype(BF16)
            prodT = (dov * o_ref[...].astype(F32)).T
            delta_s[...] = jnp.concatenate([jnp.sum(jnp.where(rlo, prodT, 0.0), axis=0, keepdims=True),
                                            jnp.sum(jnp.where(rlo, 0.0, prodT), axis=0, keepdims=True)], axis=1)
            dq_acc[...] = jnp.zeros((LANES, 2 * tq), F32)

        qsTv, dosTv, qsv, dosv = qsT[...], dosT[...], qs[...], dos[...]
        lse_v, delta_v = lse_ref[0, 0], delta_s[...]
        dqa = dq_acc[...]
        s_cur = _dot(k_ref[0:sub, :], qsTv)
        dp_cur = _dot(v_ref[0:sub, :], dosTv)
        for t in range(nsub):
            if t + 1 < nsub:
                s_next = _dot(k_ref[sub * (t + 1):sub * (t + 2), :], qsTv)
                dp_next = _dot(v_ref[sub * (t + 1):sub * (t + 2), :], dosTv)
            p = jnp.exp(s_cur - lse_v)
            ds = (p * (dp_cur - delta_v)).astype(BF16)
            rows = pl.ds(pl.multiple_of(j * tk + sub * t, sub), sub)
            dv_ref[rows, :] += _dot(p.astype(BF16), dosv)
            dk_ref[rows, :] += _dot(ds, qsv)
            dqa = dqa + _dot(kT_ref[:, sub * t:sub * (t + 1)], ds)
            if t + 1 < nsub:
                s_cur, dp_cur = s_next, dp_next
        dq_acc[...] = dqa

        @pl.when(j == nkv - 1)
        def _():
            dq_ref[...] = _pick_halves_T(dq_acc[...], rlo, tq)

        side_end()

    qmap = lambda b, i, j: (i, b)
    kmap = lambda b, i, j: (j, b // kdiv)
    res = lambda b, i, j: (0, b // kdiv)
    s_args, s_in, s_out, s_shapes, s_sems = _side_specs(side)
    return pl.pallas_call(
        body, name=name, grid=(nb, nq, nkv),
        in_specs=[pl.BlockSpec((tq, LANES), qmap), pl.BlockSpec((tk, LANES), kmap), pl.BlockSpec((LANES, tk), lambda b, i, j: (b // kdiv, j)),
                  pl.BlockSpec((tk, LANES), kmap), pl.BlockSpec((tq, LANES), qmap), pl.BlockSpec((tq, LANES), qmap),
                  pl.BlockSpec((1, 1, 1, 2 * tq), lambda b, i, j: (b, i, 0, 0))] + s_in,
        out_specs=[pl.BlockSpec((tq, LANES), qmap), pl.BlockSpec((S, LANES), res), pl.BlockSpec((S, LANES), res)] + s_out,
        out_shape=[_sds((S, nb * LANES), F32), _sds((S, nkb * LANES), F32), _sds((S, nkb * LANES), F32)] + s_shapes,
        scratch_shapes=[pltpu.VMEM((LANES, 2 * tq), BF16), pltpu.VMEM((2 * tq, LANES), BF16), pltpu.VMEM((LANES, 2 * tq), BF16),
                        pltpu.VMEM((2 * tq, LANES), BF16), pltpu.VMEM((1, 2 * tq), F32), pltpu.VMEM((LANES, 2 * tq), F32)] + s_sems,
        compiler_params=_params(("arbitrary",) * 3))(q, k, kT, v, o, do, lse, *s_args)


MLA_C = MLA_SCALE * LOG2E


def _mla_fwd(q, kcat, kcatT, *, tq, tk, sub):
    S = kcat.shape[0]; nq, nkv = S // tq, S // tk; R = B_HEADS * tq; nsub = tk // sub

    def body(q_ref, k_ref, vT_ref, o_ref, lse_ref, qT, m_s, l_s, acc):
        j = pl.program_id(1)

        @pl.when(j == 0)
        def _():
            qT[...] = q_ref[...].reshape(R, 2 * LANES).astype(F32).T.astype(BF16)
            m_s[...] = jnp.full((1, R), NEG, F32)
            l_s[...] = jnp.zeros((1, R), F32)
            acc[...] = jnp.zeros((LANES, R), F32)

        qTv = qT[...]
        m, l, a = m_s[...], l_s[...], acc[...]
        s_cur = _dot(k_ref[0:sub, :], qTv)
        for t in range(nsub):
            if t + 1 < nsub:
                s_next = _dot(k_ref[sub * (t + 1):sub * (t + 2), :], qTv)
            m_new = jnp.maximum(m, jnp.max(s_cur, axis=0, keepdims=True))
            alpha = jnp.exp2((m - m_new) * MLA_C)
            p = jnp.exp2((s_cur - m_new) * MLA_C)
            l = alpha * l + jnp.sum(p, axis=0, keepdims=True)
            a = alpha * a + _dot(vT_ref[:, sub * t:sub * (t + 1)], p.astype(BF16))
            m = m_new
            if t + 1 < nsub:
                s_cur = s_next
        m_s[...], l_s[...], acc[...] = m, l, a

        @pl.when(j == nkv - 1)
        def _():
            l_f = l_s[...]
            o_ref[...] = (acc[...] / l_f).T.reshape(B_HEADS, tq, LANES).astype(BF16)
            lse_ref[0] = m_s[...] * MLA_SCALE + jnp.log(l_f)

    return pl.pallas_call(
        body, name="mla_fwd", grid=(nq, nkv),
        in_specs=[pl.BlockSpec((B_HEADS, tq, 2 * LANES), lambda i, j: (0, i, 0)), pl.BlockSpec((tk, 2 * LANES), lambda i, j: (j, 0)),
                  pl.BlockSpec((LANES, tk), lambda i, j: (0, j))],
        out_specs=[pl.BlockSpec((B_HEADS, tq, LANES), lambda i, j: (0, i, 0)), pl.BlockSpec((1, 1, R), lambda i, j: (i, 0, 0))],
        out_shape=[_sds((B_HEADS, S, LANES), BF16), _sds((nq, 1, R), F32)],
        scratch_shapes=[pltpu.VMEM((2 * LANES, R), BF16), pltpu.VMEM((1, R), F32), pltpu.VMEM((1, R), F32), pltpu.VMEM((LANES, R), F32)],
        compiler_params=_params(("arbitrary", "arbitrary")))(q, kcat, kcatT)


def _mla_bwd(q, kcat, kcatT, o, do, lse, *, tq, tk, sub):
    S = kcat.shape[0]; nq, nkv = S // tq, S // tk; R = B_HEADS * tq; nsub = tk // sub

    def body(q_ref, k_ref, kT_ref, o_ref, do_ref, lse_ref, dq_ref, dk_ref, qT, dosT, dos, delta_s, dq_acc):
        i, j = pl.program_id(0), pl.program_id(1)

        @pl.when((i == 0) & (j == 0))
        def _():
            dk_ref[...] = jnp.zeros((S, 2 * LANES), F32)

        @pl.when(j == 0)
        def _():
            qT[...] = q_ref[...].reshape(R, 2 * LANES).astype(F32).T.astype(BF16)
            dov = do_ref[...].reshape(R, LANES).astype(F32)
            dos[...] = dov.astype(BF16)
            dosT[...] = dov.T.astype(BF16)
            delta_s[...] = jnp.sum((dov * o_ref[...].reshape(R, LANES).astype(F32)).T, axis=0, keepdims=True)
            dq_acc[...] = jnp.zeros((2 * LANES, R), F32)

        qTv, dosTv, dosv = qT[...], dosT[...], dos[...]
        qv = q_ref[...].reshape(R, 2 * LANES)
        lse_v, delta_v = lse_ref[0] * LOG2E, delta_s[...]
        dqa = dq_acc[...]
        s_cur = _dot(k_ref[0:sub, :], qTv)
        dp_cur = _dot(k_ref[0:sub, 0:LANES], dosTv)
        for t in range(nsub):
            if t + 1 < nsub:
                s_next = _dot(k_ref[sub * (t + 1):sub * (t + 2), :], qTv)
                dp_next = _dot(k_ref[sub * (t + 1):sub * (t + 2), 0:LANES], dosTv)
            p = jnp.exp2(s_cur * MLA_C - lse_v)
            ds = (p * (dp_cur - delta_v) * MLA_SCALE).astype(BF16)
            rows = pl.ds(pl.multiple_of(j * tk + sub * t, sub), sub)
            dk_ref[rows, :] += _dot(ds, qv)
            dk_ref[rows, 0:LANES] += _dot(p.astype(BF16), dosv)
            dqa = dqa + _dot(kT_ref[:, sub * t:sub * (t + 1)], ds)
            if t + 1 < nsub:
                s_cur, dp_cur = s_next, dp_next
        dq_acc[...] = dqa

        @pl.when(j == nkv - 1)
        def _():
            dq_ref[...] = dq_acc[...].T.reshape(B_HEADS, tq, 2 * LANES)

    hspec = lambda w: pl.BlockSpec((B_HEADS, tq, w), lambda i, j: (0, i, 0))
    return pl.pallas_call(
        body, name="mla_bwd", grid=(nq, nkv),
        in_specs=[hspec(2 * LANES), pl.BlockSpec((tk, 2 * LANES), lambda i, j: (j, 0)), pl.BlockSpec((2 * LANES, tk), lambda i, j: (0, j)),
                  hspec(LANES), hspec(LANES), pl.BlockSpec((1, 1, R), lambda i, j: (i, 0, 0))],
        out_specs=[hspec(2 * LANES), pl.BlockSpec((S, 2 * LANES), lambda i, j: (0, 0))],
        out_shape=[_sds((B_HEADS, S, 2 * LANES), F32), _sds((S, 2 * LANES), F32)],
        scratch_shapes=[pltpu.VMEM((2 * LANES, R), BF16), pltpu.VMEM((LANES, R), BF16), pltpu.VMEM((R, LANES), BF16),
                        pltpu.VMEM((1, R), F32), pltpu.VMEM((2 * LANES, R), F32)],
        compiler_params=_params(("arbitrary", "arbitrary")))(q, kcat, kcatT, o, do, lse)


def _win_start(i, tq, nk, S):
    return pl.multiple_of(jnp.clip(i * tq - WINDOW, 0, S - nk), LANES)


def _win_dist_table(S, tq):
    nk = min(tq + 2 * WINDOW, S)
    nq = S // tq
    r = np.arange(nk)[:, None]
    c = (np.arange(2 * tq) % tq)[None, :]
    tabs = []
    for rel in (0, WINDOW, (nq - 1) * tq - (S - nk)):
        dist = np.abs(rel + c - r).astype(np.float32)
        tabs.append(np.where(dist <= WINDOW, dist, np.float32(1e32)))
    return jnp.asarray(np.stack(tabs))


def _win_dist_spec(nk, tq, nq):
    return pl.BlockSpec((1, nk, 2 * tq), lambda b, i: (jnp.where(i == 0, 0, jnp.where(i == nq - 1, 2, 1)), 0, 0))


def _win_fwd(q, k, vT, dist, slope, sink, *, kdiv, tq, nbs, name):
    S = k.shape[0]; nb = q.shape[1] // LANES; nq = S // tq; nk = min(tq + 2 * WINDOW, S)
    assert nb % nbs == 0 and nbs % kdiv == 0
    kvw = (nbs // kdiv) * LANES

    def body(q_ref, k_ref, vT_ref, dist_ref, slope_ref, sink_ref, o_ref, lse_ref):
        i = pl.program_id(1)
        rlo = _row_lo()
        k0 = _win_start(i, tq, nk, S)
        kk, vv, dd = k_ref[pl.ds(k0, nk), :], vT_ref[:, pl.ds(k0, nk)], dist_ref[0]
        for u in range(nbs):
            kv = slice(LANES * (u // kdiv), LANES * (u // kdiv + 1))
            qsT = _stack_cols(q_ref[:, LANES * u:LANES * (u + 1)].astype(F32).T, rlo).astype(BF16)
            s = _dot(kk[:, kv], qsT) - slope_ref[u] * dd
            sk = sink_ref[u]
            m = jnp.maximum(jnp.max(s, axis=0, keepdims=True), sk)
            p = jnp.exp(s - m)
            l = jnp.sum(p, axis=0, keepdims=True) + jnp.exp(sk - m)
            o_ref[:, LANES * u:LANES * (u + 1)] = _pick_halves_T(_dot(vv[kv, :], p.astype(BF16)) / l, rlo, tq).astype(BF16)
            lse_ref[u, 0] = m + jnp.log(l)

    row_spec = pl.BlockSpec((nbs, 1, 2 * tq), lambda b, i: (b, 0, 0))
    return pl.pallas_call(
        body, name=name, grid=(nb // nbs, nq),
        in_specs=[pl.BlockSpec((tq, nbs * LANES), lambda b, i: (i, b)), pl.BlockSpec((S, kvw), lambda b, i: (0, b)),
                  pl.BlockSpec((kvw, S), lambda b, i: (b, 0)), _win_dist_spec(nk, tq, nq), row_spec, row_spec],
        out_specs=[pl.BlockSpec((tq, nbs * LANES), lambda b, i: (i, b)), pl.BlockSpec((nbs, 1, 1, 2 * tq), lambda b, i: (b, i, 0, 0))],
        out_shape=[_sds((S, nb * LANES), BF16), _sds((nb, nq, 1, 2 * tq), F32)],
        compiler_params=_params(("arbitrary", "arbitrary")))(q, k, vT, dist, slope, sink)


def _win_bwd(q, k, kT, v, o, do, lse, dist, slope, sink, *, kdiv, tq, nbs, name):
    S = k.shape[0]; nb = q.shape[1] // LANES; nkb = k.shape[1] // LANES; nq = S // tq; nk = min(tq + 2 * WINDOW, S)
    assert nb % nbs == 0 and nbs % kdiv == 0
    nkv = nbs // kdiv
    kvw = nkv * LANES

    def body(q_ref, k_ref, kT_ref, v_ref, o_ref, do_ref, lse_ref, dist_ref, slope_ref, sink_ref, dq_ref, dk_ref, dv_ref, dsink_ref, ds_acc):
        i = pl.program_id(1)
        rlo = _row_lo()
        lo = lax.broadcasted_iota(jnp.int32, (1, LANES), 1) < HEAD_DIM

        @pl.when(i == 0)
        def _():
            dk_ref[...] = jnp.zeros((S, kvw), F32)
            dv_ref[...] = jnp.zeros((S, kvw), F32)
            ds_acc[...] = jnp.zeros((nbs, 2 * tq), F32)

        k0 = _win_start(i, tq, nk, S)
        rows = pl.ds(k0, nk)
        kk_all, vv_all, kkT_all, dd = k_ref[rows, :], v_ref[rows, :], kT_ref[:, rows], dist_ref[0]
        dv_sum, dk_sum = [None] * nkv, [None] * nkv
        for u in range(nbs):
            g = u // kdiv
            kv = slice(LANES * g, LANES * (g + 1))
            kk, vv, kkT = kk_all[:, kv], vv_all[:, kv], kkT_all[kv, :]
            cols = slice(LANES * u, LANES * (u + 1))
            qv = q_ref[:, cols]
            qs = _stack_rows(qv, lo)
            qsT = _stack_cols(qv.astype(F32).T, rlo).astype(BF16)
            dov = do_ref[:, cols].astype(F32)
            dos = _stack_rows(dov.astype(BF16), lo)
            dosT = _stack_cols(dov.T, rlo).astype(BF16)
            prodT = (dov * o_ref[:, cols].astype(F32)).T
            delta = jnp.concatenate([jnp.sum(jnp.where(rlo, prodT, 0.0), axis=0, keepdims=True),
                                     jnp.sum(jnp.where(rlo, 0.0, prodT), axis=0, keepdims=True)], axis=1)
            lse_v = lse_ref[u, 0]
            ds_acc[u:u + 1, :] += -jnp.exp(sink_ref[u] - lse_v) * delta
            p = jnp.exp(_dot(kk, qsT) - slope_ref[u] * dd - lse_v)
            ds = (p * (_dot(vv, dosT) - delta)).astype(BF16)
            dv_u, dk_u = _dot(p.astype(BF16), dos), _dot(ds, qs)
            dv_sum[g] = dv_u if dv_sum[g] is None else dv_sum[g] + dv_u
            dk_sum[g] = dk_u if dk_sum[g] is None else dk_sum[g] + dk_u
            dq_ref[:, cols] = (_pick_halves_T(_dot(kkT, ds), rlo, tq) * 0.125).astype(BF16)
        dv_ref[rows, :] += jnp.concatenate(dv_sum, axis=1)
        dk_ref[rows, :] += jnp.concatenate(dk_sum, axis=1)

        @pl.when(i == nq - 1)
        def _():
            acc = ds_acc[...]
            for u in range(nbs):
                dsink_ref[u] = jnp.concatenate(
                    [jnp.broadcast_to(jnp.sum(acc[u:u + 1, 0:tq], axis=1, keepdims=True), (1, LANES)),
                     jnp.broadcast_to(jnp.sum(acc[u:u + 1, tq:2 * tq], axis=1, keepdims=True), (1, LANES)),
                     jnp.zeros((6, LANES), F32)], axis=0)

    qmap = lambda b, i: (i, b)
    kv_spec = pl.BlockSpec((S, kvw), lambda b, i: (0, b))
    row_spec = pl.BlockSpec((nbs, 1, 2 * tq), lambda b, i: (b, 0, 0))
    wide = pl.BlockSpec((tq, nbs * LANES), qmap)
    return pl.pallas_call(
        body, name=name, grid=(nb // nbs, nq),
        in_specs=[wide, kv_spec, pl.BlockSpec((kvw, S), lambda b, i: (b, 0)), kv_spec, wide, wide,
                  pl.BlockSpec((nbs, 1, 1, 2 * tq), lambda b, i: (b, i, 0, 0)), _win_dist_spec(nk, tq, nq), row_spec, row_spec],
        out_specs=[wide, kv_spec, kv_spec, pl.BlockSpec((nbs, 8, LANES), lambda b, i: (b, 0, 0))],
        out_shape=[_sds((S, nb * LANES), BF16), _sds((S, nkb * LANES), F32), _sds((S, nkb * LANES), F32), _sds((nb, 8, LANES), F32)],
        scratch_shapes=[pltpu.VMEM((nbs, 2 * tq), F32)],
        compiler_params=_params(("arbitrary", "arbitrary")))(q, k, kT, v, o, do, lse, dist, slope, sink)


def _sum_rows(v):
    return jnp.sum(v, axis=0, keepdims=True)


def _norm_mod_bwd(dh, xv, mod_ref, nw_ref, stats_ref):
    r = _rms(xv)
    xn = xv * r
    nw = nw_ref[...]
    stats_ref[0:1, :] += _sum_rows(dh)
    stats_ref[1:2, :] += _sum_rows(dh * (xn * nw))
    dn = dh * (1.0 + mod_ref[1:2, :])
    stats_ref[2:3, :] += _sum_rows(dn * xn)
    return _rms_bwd(xv, r, dn * nw)


def _even_gate_specs(ts):
    return [pl.BlockSpec((ts, 256), lambda i, c=c: (i, c)) for c in (3, 4, 7, 8)]


def _even_post_fwd(oa, olat, proj, x, gate, wuv, woe):
    S = x.shape[0]
    ts = min(ROW_TILE, S)

    def body(oa_ref, ol_ref, ga0_ref, ga1_ref, gb0_ref, gb1_ref, x_ref, gate_ref, wuv_ref, woe_ref, y_ref, x1_ref):
        sa, _ = _silu_and_grad(jnp.concatenate([ga0_ref[...], ga1_ref[...]], axis=1))
        sb, _ = _silu_and_grad(jnp.concatenate([gb0_ref[...], gb1_ref[...]], axis=1))
        olc = jnp.concatenate([ol_ref[hh] for hh in range(B_HEADS)], axis=1).astype(BF16)
        ob = _dot(olc, wuv_ref[...])
        mix = jnp.concatenate([oa_ref[...] * sa, ob * sb], axis=1).astype(BF16)
        y = _dot(mix, woe_ref[...])
        y_ref[...] = y.astype(BF16)
        x1_ref[...] = x_ref[...] + gate_ref[...] * y

    return pl.pallas_call(
        body, name="even_post_fwd", grid=(S // ts,),
        in_specs=[_row_spec(ts, 512), pl.BlockSpec((B_HEADS, ts, LANES), lambda i: (0, i, 0))] + _even_gate_specs(ts) +
                 [_row_spec(ts, D_MODEL), _full_spec((1, D_MODEL)), _full_spec((1024, 512)), _full_spec((1024, D_MODEL))],
        out_specs=[_row_spec(ts, D_MODEL), _row_spec(ts, D_MODEL)],
        out_shape=[_sds((S, D_MODEL), BF16), _sds((S, D_MODEL), F32)],
        compiler_params=_params(("arbitrary",)),
    )(oa, olat, proj, proj, proj, proj, x, gate, wuv, woe)


def _odd_pre_fwd(x, mod, nw, wio):
    S = x.shape[0]
    ts = min(ROW_TILE, S)

    def body(x_ref, mod_ref, nw_ref, wio_ref, h_ref, proj_ref, q_ref, k_ref, v_ref, kt_ref, vt_ref):
        xv = x_ref[...]
        h = (xv * _rms(xv) * nw_ref[...]) * (1.0 + mod_ref[1:2, :]) + mod_ref[0:1, :]
        hb = h.astype(BF16)
        h_ref[...] = hb
        proj = jnp.concatenate([_dot(hb, wio_ref[p]) for p in range(N_CHIPS)], axis=1)
        proj_ref[...] = proj
        q_ref[...] = (proj[:, 0:1024] * 0.125).astype(BF16)
        lane = _lane_iota()
        k_v = jnp.concatenate([_dup_heads(proj[:, 1024 + LANES * j:1024 + LANES * (j + 1)], lane) for j in range(2)], axis=1)
        v_v = jnp.concatenate([_dup_heads(proj[:, 1280 + LANES * j:1280 + LANES * (j + 1)], lane) for j in range(2)], axis=1)
        k_ref[...] = k_v.astype(BF16)
        v_ref[...] = v_v.astype(BF16)
        kt_ref[...] = k_v.T.astype(BF16)
        vt_ref[...] = v_v.T.astype(BF16)

    col_spec = pl.BlockSpec((512, ts), lambda i: (0, i))
    return pl.pallas_call(
        body, name="odd_pre_fwd", grid=(S // ts,),
        in_specs=[_row_spec(ts, D_MODEL), _full_spec((3, D_MODEL)), _full_spec((1, D_MODEL)),
                  _full_spec((N_CHIPS, D_MODEL, ODD_IN // N_CHIPS))],
        out_specs=[_row_spec(ts, D_MODEL), _row_spec(ts, ODD_IN), _row_spec(ts, 1024), _row_spec(ts, 512), _row_spec(ts, 512),
                   col_spec, col_spec],
        out_shape=[_sds((S, D_MODEL), BF16), _sds((S, ODD_IN), F32), _sds((S, 1024), BF16), _sds((S, 512), BF16),
                   _sds((S, 512), BF16), _sds((512, S), BF16), _sds((512, S), BF16)],
        compiler_params=_params(("arbitrary",)),
    )(x, mod, nw, wio)


def _odd_post(oc, proj, x1, gate, woo, fw, tgt):
    S = x1.shape[0]
    ts = min(ROW_TILE, S)
    nsteps = S // ts

    def body(oc_ref, g0_ref, g1_ref, x_ref, gate_ref, woo_ref, fw_ref, tgt_ref, doc_ref, dgc_ref, dx2_ref, dwoo_out, stats_ref,
             dwoo_ref):
        @pl.when(pl.program_id(0) == 0)
        def _():
            dwoo_ref[...] = jnp.zeros((D_MODEL, D_MODEL), F32)
            stats_ref[...] = jnp.zeros((8, D_MODEL), F32)

        ocv = oc_ref[...]
        sg, dsg = _silu_and_grad(jnp.concatenate([g0_ref[...], g1_ref[...]], axis=1))
        mix = (ocv * sg).astype(BF16)
        woo_v = woo_ref[...]
        y = _dot(mix, woo_v)
        gate_v = gate_ref[...]
        x2 = x_ref[...] + gate_v * y
        r = _rms(x2)
        fw_v = fw_ref[...]
        xn = x2 * r
        err = xn * fw_v - tgt_ref[...]
        dout = err * (1.0 / D_MODEL)
        dx2 = _rms_bwd(x2, r, dout * fw_v)
        dx2_ref[...] = dx2
        stats_ref[0:1, :] += _sum_rows(dout * xn)
        stats_ref[1:2, :] += _sum_rows(dx2 * y)
        loss_t = 0.5 * jnp.sum(_sum_rows(err * dout), axis=-1, keepdims=True)
        stats_ref[2:3, :] += jnp.broadcast_to(loss_t, (1, D_MODEL))
        dy = (gate_v * dx2).astype(BF16)
        dmix = _dot_nt(dy, woo_v)
        dwoo_ref[...] += _dot_tn(mix, dy)
        doc_ref[...] = (dmix * sg).astype(BF16)
        dgc_ref[...] = (dmix * ocv * dsg).astype(BF16)

        @pl.when(pl.program_id(0) == nsteps - 1)
        def _():
            dwoo_out[...] = dwoo_ref[...].astype(BF16)

    gate_cols = [pl.BlockSpec((ts, 512), lambda i, c=c: (i, c)) for c in (3, 4)]
    return pl.pallas_call(
        body, name="odd_post", grid=(nsteps,),
        in_specs=[_row_spec(ts, D_MODEL)] + gate_cols + [_row_spec(ts, D_MODEL), _full_spec((1, D_MODEL)),
                  _full_spec((D_MODEL, D_MODEL)), _full_spec((1, D_MODEL)), _row_spec(ts, D_MODEL)],
        out_specs=[_row_spec(ts, D_MODEL), _row_spec(ts, D_MODEL), _row_spec(ts, D_MODEL),
                   _full_spec((D_MODEL, D_MODEL), single=False), _full_spec((8, D_MODEL), single=False)],
        out_shape=[_sds((S, D_MODEL), BF16), _sds((S, D_MODEL), BF16), _sds((S, D_MODEL), F32), _sds((D_MODEL, D_MODEL), BF16),
                   _sds((8, D_MODEL), F32)],
        scratch_shapes=[pltpu.VMEM((D_MODEL, D_MODEL), F32)],
        compiler_params=_params(("arbitrary",)),
    )(oc, proj, proj, x1, gate, woo, fw, tgt)


def _odd_pre_bwd(dq, dk, dv, dgc, h, x, dx_res, mod, nw, wio):
    S = x.shape[0]
    ts = min(IN_PROJ_ROW_TILE, S)
    nsteps = S // ts
    wsh = ODD_IN // N_CHIPS

    def body(dq_ref, dk_ref, dv_ref, dgc_ref, h_ref, x_ref, dxr_ref, mod_ref, nw_ref, wio_ref, dx_ref, dw_ref, stats_ref, dw_acc):
        @pl.when(pl.program_id(0) == 0)
        def _():
            dw_acc[...] = jnp.zeros((N_CHIPS, D_MODEL, wsh), F32)
            stats_ref[...] = jnp.zeros((8, D_MODEL), F32)

        lane = _lane_iota()
        dkv = [_fold_heads(r[:, 2 * LANES * j:2 * LANES * (j + 1)], lane).astype(BF16) for r in (dk_ref, dv_ref) for j in range(2)]
        dproj = jnp.concatenate([dq_ref[...]] + dkv + [dgc_ref[...]], axis=1)
        hv = h_ref[...]
        dh = None
        for p in range(N_CHIPS):
            dp_cols = dproj[:, wsh * p:wsh * (p + 1)]
            part = _dot_nt(dp_cols, wio_ref[p])
            dh = part if dh is None else dh + part
            dw_acc[p] += _dot_tn(hv, dp_cols)
        dx_ref[...] = dxr_ref[...] + _norm_mod_bwd(dh, x_ref[...], mod_ref, nw_ref, stats_ref)

        @pl.when(pl.program_id(0) == nsteps - 1)
        def _():
            dw_ref[...] = dw_acc[...].astype(BF16)

    return pl.pallas_call(
        body, name="odd_pre_bwd", grid=(nsteps,),
        in_specs=[_row_spec(ts, 1024), _row_spec(ts, 512), _row_spec(ts, 512), _row_spec(ts, 1024), _row_spec(ts, D_MODEL),
                  _row_spec(ts, D_MODEL), _row_spec(ts, D_MODEL), _full_spec((3, D_MODEL)), _full_spec((1, D_MODEL)),
                  _full_spec((N_CHIPS, D_MODEL, wsh))],
        out_specs=[_row_spec(ts, D_MODEL), _full_spec((N_CHIPS, D_MODEL, wsh), single=False), _full_spec((8, D_MODEL), single=False)],
        out_shape=[_sds((S, D_MODEL), F32), _sds((N_CHIPS, D_MODEL, wsh), BF16), _sds((8, D_MODEL), F32)],
        scratch_shapes=[pltpu.VMEM((N_CHIPS, D_MODEL, wsh), F32)],
        compiler_params=_params(("arbitrary",)),
    )(dq, dk, dv, dgc, h, x, dx_res, mod, nw, wio)


def _even_post_bwd(dx1, y, oa, olat, proj, gate, wuv, woe):
    S = dx1.shape[0]
    ts = min(ROW_TILE, S)
    nsteps = S // ts

    def body(dx_ref, y_ref, oa_ref, ol_ref, ga0_ref, ga1_ref, gb0_ref, gb1_ref, gate_ref, wuv_ref, woe_ref,
             doa_ref, dga_ref, dgb_ref, dol_ref, dwoe_out, dwuv_ref, stats_ref, dwoe_ref):
        @pl.when(pl.program_id(0) == 0)
        def _():
            dwoe_ref[...] = jnp.zeros((D_MODEL, D_MODEL), F32)
            dwuv_ref[...] = jnp.zeros((1024, 512), F32)
            stats_ref[...] = jnp.zeros((8, D_MODEL), F32)

        dxv = dx_ref[...]
        stats_ref[0:1, :] += _sum_rows(dxv * y_ref[...])
        dy = (gate_ref[...] * dxv).astype(BF16)
        sa, dsa = _silu_and_grad(jnp.concatenate([ga0_ref[...], ga1_ref[...]], axis=1))
        sb, dsb = _silu_and_grad(jnp.concatenate([gb0_ref[...], gb1_ref[...]], axis=1))
        olc = jnp.concatenate([ol_ref[hh] for hh in range(B_HEADS)], axis=1).astype(BF16)
        wuv_v = wuv_ref[...]
        ob = _dot(olc, wuv_v)
        oav = oa_ref[...]
        mix = jnp.concatenate([oav * sa, ob * sb], axis=1).astype(BF16)
        dmix = _dot_nt(dy, woe_ref[...])
        dwoe_ref[...] += _dot_tn(mix, dy)
        dma, dmb = dmix[:, 0:512], dmix[:, 512:1024]
        doa_ref[...] = (dma * sa).astype(BF16)
        dga_ref[...] = (dma * oav * dsa).astype(BF16)
        dgb_ref[...] = (dmb * ob * dsb).astype(BF16)
        dob = (dmb * sb).astype(BF16)
        dol = _dot_nt(dob, wuv_v)
        dwuv_ref[...] += _dot_tn(olc, dob)
        for hh in range(B_HEADS):
            dol_ref[hh] = dol[:, LANES * hh:LANES * (hh + 1)].astype(BF16)

        @pl.when(pl.program_id(0) == nsteps - 1)
        def _():
            dwoe_out[...] = dwoe_ref[...].astype(BF16)

    head_spec = pl.BlockSpec((B_HEADS, ts, LANES), lambda i: (0, i, 0))
    return pl.pallas_call(
        body, name="even_post_bwd", grid=(nsteps,),
        in_specs=[_row_spec(ts, D_MODEL), _row_spec(ts, D_MODEL), _row_spec(ts, 512), head_spec] + _even_gate_specs(ts) +
                 [_full_spec((1, D_MODEL)), _full_spec((1024, 512)), _full_spec((1024, D_MODEL))],
        out_specs=[_row_spec(ts, 512), _row_spec(ts, 512), _row_spec(ts, 512), head_spec,
                   _full_spec((D_MODEL, D_MODEL), single=False), _full_spec((1024, 512), single=False),
                   _full_spec((8, D_MODEL), single=False)],
        out_shape=[_sds((S, 512), BF16), _sds((S, 512), BF16), _sds((S, 512), BF16), _sds((B_HEADS, S, LANES), BF16),
                   _sds((D_MODEL, D_MODEL), BF16), _sds((1024, 512), F32), _sds((8, D_MODEL), F32)],
        scratch_shapes=[pltpu.VMEM((D_MODEL, D_MODEL), F32)],
        compiler_params=_params(("arbitrary",)),
    )(dx1, y, oa, olat, proj, proj, proj, proj, gate, wuv, woe)


def _even_pre_bwd(x, h, proj, dqa, dka, dva, dga, dgb, dqcat, dkcat, dx_res, mod, nw, wie, qn, kn, seg, ca, sa, ct, st,
                  qln, kvln, wuq, wuk):
    S = x.shape[0]
    ts = min(IN_PROJ_ROW_TILE, S)
    nsteps = S // ts

    def body(x_ref, h_ref, proj_ref, dqa_ref, dka_ref, dva_ref, dga_ref, dgb_ref, dqc_ref, dkc_ref, dxr_ref, mod_ref, nw_ref,
             wie_ref, qn_ref, kn_ref, seg_ref, ca_ref, sa_ref, ct_ref, st_ref, qln_ref, kvln_ref, wuq_ref, wuk_ref,
             dx_ref, dwie_out, dwuq_out, dwuk_out, stats_ref, nstats_ref, dwie_ref, dwuq_ref, dwuk_ref):
        @pl.when(pl.program_id(0) == 0)
        def _():
            dwie_ref[...] = jnp.zeros((D_MODEL, EVEN_P), F32)
            dwuq_ref[...] = jnp.zeros((B_Q_LORA, 1536), F32)
            dwuk_ref[...] = jnp.zeros((512, 1024), F32)
            stats_ref[...] = jnp.zeros((8, D_MODEL), F32)
            nstats_ref[...] = jnp.zeros((8, 256), F32)

        lane = _lane_iota()
        ca_v, sa_v, ct_v, st_v = ca_ref[...], sa_ref[...], ct_ref[...], st_ref[...]
        seg_v = seg_ref[...]

        def head_norm_bwd(xc, dy, w):
            r = lax.rsqrt(_seg_mean(xc * xc, seg_v) + EPS)
            g = dy * w
            dxc = r * g - xc * (r * r * r) * _seg_mean(xc * g, seg_v)
            return dxc, _sum_rows(dy * (xc * r))

        pieces = []
        dqn = jnp.zeros((1, LANES), F32)
        for cb in range(4):
            sl = slice(LANES * cb, LANES * (cb + 1))
            dy = _rot_bwd(dqa_ref[:, sl] * 0.125, ca_v, sa_v, lane)
            dxc, dw = head_norm_bwd(proj_ref[:, sl], dy, qn_ref[...])
            pieces.append(dxc)
            dqn = dqn + dw
        dxc, dkn = head_norm_bwd(proj_ref[:, 512:640], _rot_bwd(_fold_heads(dka_ref[...], lane), ca_v, sa_v, lane), kn_ref[...])
        pieces += [dxc, _fold_heads(dva_ref[...], lane), dga_ref[...]]
        nstats_ref[0:1, 0:LANES] += dqn + pltpu.roll(dqn, HEAD_DIM, 1)
        nstats_ref[1:2, 0:LANES] += dkn + pltpu.roll(dkn, HEAD_DIM, 1)

        cq = proj_ref[:, 1280:1536]
        rq = _rms(cq)
        cqn_f = cq * rq
        qln_v = qln_ref[...]
        cqn = (cqn_f * qln_v).astype(BF16)
        wuq_v, wuk_v = wuq_ref[...], wuk_ref[...]
        qnope = _dot(cqn, wuq_v[:, 0:512]).astype(BF16)
        dqlat = jnp.concatenate([dqc_ref[hh, :, 0:LANES] for hh in range(B_HEADS)], axis=1).astype(BF16)
        dqnope = _dot_nt(dqlat, wuk_v)
        dwuk_ref[...] += _dot_tn(qnope, dqlat)
        dqr = [_rot_bwd(dqc_ref[hh, :, LANES:2 * LANES], ct_v, st_v, lane) for hh in range(B_HEADS)]
        dqb = jnp.concatenate([dqnope] + dqr, axis=1).astype(BF16)
        dcqn = _dot_nt(dqb, wuq_v)
        dwuq_ref[...] += _dot_tn(cqn, dqb)
        nstats_ref[2:3, :] += _sum_rows(dcqn * cqn_f)
        dcq = _rms_bwd(cq, rq, dcqn * qln_v)
        ckv = proj_ref[:, 1536:1664]
        rk = _rms(ckv)
        dckvn = dkc_ref[:, 0:LANES]
        nstats_ref[3:4, 0:LANES] += _sum_rows(dckvn * (ckv * rk))
        dckv = _rms_bwd(ckv, rk, dckvn * kvln_ref[...])
        dkr = _rot_bwd(dkc_ref[:, LANES:2 * LANES], ct_v, st_v, lane)
        pieces += [dcq, dckv, dkr, dgb_ref[...]]
        dproj = jnp.concatenate([piece.astype(BF16) for piece in pieces], axis=1)
        dh = _dot_nt(dproj, wie_ref[...])
        dwie_ref[...] += _dot_tn(h_ref[...], dproj)
        dx_ref[...] = dxr_ref[...] + _norm_mod_bwd(dh, x_ref[...], mod_ref, nw_ref, stats_ref)

        @pl.when(pl.program_id(0) == nsteps - 1)
        def _():
            pltpu.sync_copy(dwie_ref, dwie_out)
            pltpu.sync_copy(dwuq_ref, dwuq_out)
            pltpu.sync_copy(dwuk_ref, dwuk_out)

    return pl.pallas_call(
        body, name="even_pre_bwd", grid=(nsteps,),
        in_specs=[_row_spec(ts, D_MODEL), _row_spec(ts, D_MODEL), _row_spec(ts, EVEN_P), _row_spec(ts, 512), _row_spec(ts, 2 * LANES),
                  _row_spec(ts, 2 * LANES), _row_spec(ts, 512), _row_spec(ts, 512),
                  pl.BlockSpec((B_HEADS, ts, 2 * LANES), lambda i: (0, i, 0)), _row_spec(ts, 2 * LANES), _row_spec(ts, D_MODEL),
                  _full_spec((3, D_MODEL)), _full_spec((1, D_MODEL)), _full_spec((D_MODEL, EVEN_P)),
                  _full_spec((1, LANES)), _full_spec((1, LANES)), _full_spec((LANES, LANES)),
                  _row_spec(ts, LANES), _row_spec(ts, LANES), _row_spec(ts, LANES), _row_spec(ts, LANES),
                  _full_spec((1, B_Q_LORA)), _full_spec((1, B_KV_LORA)), _full_spec((B_Q_LORA, 1536)), _full_spec((512, 1024))],
        out_specs=[_row_spec(ts, D_MODEL), _ANY, _ANY, _ANY, _full_spec((8, D_MODEL), single=False), _full_spec((8, 256), single=False)],
        out_shape=[_sds((S, D_MODEL), F32), _sds((D_MODEL, EVEN_P), F32), _sds((B_Q_LORA, 1536), F32), _sds((512, 1024), F32),
                   _sds((8, D_MODEL), F32), _sds((8, 256), F32)],
        scratch_shapes=[pltpu.VMEM((D_MODEL, EVEN_P), F32), pltpu.VMEM((B_Q_LORA, 1536), F32), pltpu.VMEM((512, 1024), F32)],
        compiler_params=_params(("arbitrary",)),
    )(x, h, proj, dqa, dka, dva, dga, dgb, dqcat, dkcat, dx_res, mod, nw, wie, qn, kn, seg, ca, sa, ct, st, qln, kvln, wuq, wuk)


def _ada_fwd(c_all, w, b):
    n = w.shape[2]

    def body(c_ref, w_ref, b_ref, o_ref):
        cv = c_ref[...]
        o_ref[0] = _dot_f32(cv * _sigmoid(cv), w_ref[0]) + b_ref[0]

    return pl.pallas_call(
        body, name="ada_fwd", grid=(2,),
        in_specs=[pl.BlockSpec((N_DEV, D_MODEL), lambda l: (0, 0)), pl.BlockSpec((1, D_MODEL, n), lambda l: (l, 0, 0)),
                  pl.BlockSpec((1, 1, n), lambda l: (l, 0, 0))],
        out_specs=pl.BlockSpec((1, N_DEV, n), lambda l: (l, 0, 0)),
        out_shape=_sds((2, N_DEV, n), F32),
        compiler_params=_params(("arbitrary",)),
    )(c_all, w, b)


def _ada_bwd(c_all_t, dmod):
    n = dmod.shape[2]

    def body(c_ref, d_ref, o_ref):
        cv = c_ref[...]
        act = cv * _sigmoid(cv)
        dv = d_ref[0]
        acc = act[:, 0:1] * dv[0:1, :]
        for bb in range(1, N_DEV):
            acc = acc + act[:, bb:bb + 1] * dv[bb:bb + 1, :]
        o_ref[0] = acc

    return pl.pallas_call(
        body, name="ada_bwd", grid=(2,),
        in_specs=[pl.BlockSpec((D_MODEL, N_DEV), lambda l: (0, 0)), pl.BlockSpec((1, N_DEV, n), lambda l: (l, 0, 0))],
        out_specs=pl.BlockSpec((1, D_MODEL, n), lambda l: (l, 0, 0)),
        out_shape=_sds((2, D_MODEL, n), F32),
        compiler_params=_params(("arbitrary",)),
    )(c_all_t, dmod)


ADAM_ROW_TILE = 256


def _adam_update(g, w, m, v):
    m_new = ADAM_B1 * m + (1.0 - ADAM_B1) * g
    v_new = ADAM_B2 * v + (1.0 - ADAM_B2) * jnp.square(g)
    m_hat = m_new / (1.0 - ADAM_B1 ** ADAM_STEP)
    v_hat = v_new / (1.0 - ADAM_B2 ** ADAM_STEP)
    return -ADAM_LR * (m_hat / (jnp.sqrt(v_hat) + ADAM_EPS) + ADAM_WD * w), m_new, v_new


SMALL_ROWS = dict(dmod=(0, D_MODEL), norm_w=(6, D_MODEL), final_norm=(8, D_MODEL), a_q_norm=(9, HEAD_DIM), a_k_norm=(10, HEAD_DIM),
                  b_q_lora_norm=(11, B_Q_LORA), b_kv_lora_norm=(12, B_KV_LORA), c_sink=(13, C_HEADS))
SMALL_WEIGHTS = ("ada_b", "norm_w", "final_norm", "a_q_norm", "a_k_norm", "b_q_lora_norm", "b_kv_lora_norm", "c_sink")
LOSS_ROW = 14


def _pack_small(res):
    def padded(v):
        return jnp.concatenate([v, jnp.zeros((v.shape[0], D_MODEL - v.shape[1]), F32)], axis=1)

    rows = [res["dmod"].reshape(6, D_MODEL), res["norm_w"], res["final_norm"].reshape(1, D_MODEL)]
    rows += [padded(res[k]) for k in ("a_q_norm", "a_k_norm", "b_q_lora_norm", "b_kv_lora_norm", "c_sink")]
    return jnp.concatenate(rows + [res["loss_row"], jnp.zeros((1, D_MODEL), F32)], axis=0)


def _adam_small(parts, ws, ms, vs):
    nw = len(SMALL_WEIGHTS)

    def body(*refs):
        p_ref = refs[0]
        w_refs, m_refs, v_refs = refs[1:1 + nw], refs[1 + nw:1 + 2 * nw], refs[1 + 2 * nw:1 + 3 * nw]
        outs = refs[1 + 3 * nw:]
        g_all = p_ref[0]
        for k in range(1, N_DEV):
            g_all = g_all + p_ref[k]
        for idx, name in enumerate(SMALL_WEIGHTS):
            if name == "ada_b":
                g = jnp.concatenate([jnp.concatenate([g_all[3 * l + t:3 * l + t + 1] for t in range(3)], axis=1) for l in range(2)],
                                    axis=0)
            else:
                row, width = SMALL_ROWS[name]
                g = g_all[row:row + w_refs[idx].shape[0], 0:width]
            d, m_new, v_new = _adam_update(g, w_refs[idx][...], m_refs[idx][...], v_refs[idx][...])
            outs[4 * idx][...], outs[4 * idx + 1][...], outs[4 * idx + 2][...], outs[4 * idx + 3][...] = g, d, m_new, v_new
        outs[4 * nw][...] = g_all[LOSS_ROW:LOSS_ROW + 1, 0:LANES]

    out_shape = []
    for w in ws:
        out_shape += [_sds(w.shape, F32)] * 4
    out_shape.append(_sds((1, LANES), F32))
    return pl.pallas_call(body, name="adam_small", out_shape=out_shape,
                          compiler_params=pltpu.CompilerParams(vmem_limit_bytes=VMEM_LIMIT))(parts, *ws, *ms, *vs)


def _adam(parts, w, m, v, name):
    P, R, C = parts.shape
    tr = R if R <= ADAM_ROW_TILE else ADAM_ROW_TILE
    assert R % tr == 0

    def body(p_ref, w_ref, m_ref, v_ref, g_ref, d_ref, nm_ref, nv_ref):
        g = p_ref[0].astype(F32)
        for k in range(1, P):
            g = g + p_ref[k].astype(F32)
        g_ref[...] = g
        d_ref[...], nm_ref[...], nv_ref[...] = _adam_update(g, w_ref[...], m_ref[...], v_ref[...])

    spec = pl.BlockSpec((tr, C), lambda i: (i, 0))
    return pl.pallas_call(
        body, name=name, grid=(R // tr,),
        in_specs=[pl.BlockSpec((P, tr, C), lambda i: (0, i, 0)), spec, spec, spec],
        out_specs=[spec, spec, spec, spec], out_shape=[_sds((R, C), F32)] * 4,
        compiler_params=_params(("arbitrary",)),
    )(parts, w, m, v)


_ANY = pl.BlockSpec(memory_space=pl.ANY)
CHIP_FLIPS = ((1, 0), (0, 1), (1, 1))
DEV_FLIPS = tuple((dx, dy, dc) for dx in (0, 1) for dy in (0, 1) for dc in (0, 1) if dx + dy + dc)


def _flip(a, d):
    return a if d == 0 else 1 - a


def _my_place():
    return lax.axis_index("x"), lax.axis_index("y"), lax.axis_index("c")


def _gather8_copies(ins, outs, send_sems, recv_sems, loc_sems):
    x, y, c = _my_place()
    me = 4 * x + 2 * y + c
    copies = []
    for a in range(len(ins)):
        copies.append(pltpu.make_async_copy(ins[a], outs[a].at[me], loc_sems.at[a]))
        for k, (dx, dy, dc) in enumerate(DEV_FLIPS):
            copies.append(pltpu.make_async_remote_copy(
                src_ref=ins[a], dst_ref=outs[a].at[me], send_sem=send_sems.at[a, k], recv_sem=recv_sems.at[a, k],
                device_id=(_flip(x, dx), _flip(y, dy), _flip(c, dc)), device_id_type=MESH_ID))
    return copies


def _gather8_sems(n):
    return [pltpu.SemaphoreType.DMA((n, 7)), pltpu.SemaphoreType.DMA((n, 7)), pltpu.SemaphoreType.DMA((n,))]


def _gather_dev8(arrs, name):
    n = len(arrs)

    def body(*refs):
        copies = _gather8_copies(refs[:n], refs[n:2 * n], *refs[2 * n:])
        for cp in copies:
            cp.start()
        for cp in copies:
            cp.wait()

    return pl.pallas_call(
        body, name=name, in_specs=[_ANY] * n, out_specs=[_ANY] * n,
        out_shape=[_sds((N_DEV,) + a.shape, a.dtype) for a in arrs], scratch_shapes=_gather8_sems(n),
    )(*arrs)


class _Exchange:
    def __init__(self, arrs, out_shapes, n_sems, phases):
        self.arrs, self.out_shapes, self.n_sems, self._phases = list(arrs), list(out_shapes), n_sems, phases

    @property
    def n(self):
        return len(self.arrs)

    def sem_shapes(self):
        return [pltpu.SemaphoreType.DMA((self.n, self.n_sems)), pltpu.SemaphoreType.DMA((self.n, self.n_sems)),
                pltpu.SemaphoreType.DMA((self.n,))]

    def phases(self, ins, outs, sems):
        return self._phases(ins, outs, *sems)

    def run(self, name):
        n = self.n

        def body(*refs):
            start, mid, end = self.phases(refs[:n], refs[n:2 * n], refs[2 * n:])
            start()
            mid()
            end()

        return pl.pallas_call(body, name=name, in_specs=[_ANY] * n, out_specs=[_ANY] * n, out_shape=self.out_shapes,
                              scratch_shapes=self.sem_shapes())(*self.arrs)

def _gather_halves_phases(ins, outs, send_sems, recv_sems, loc_sems):
    n = len(ins)
    x, y, c = _my_place()
    chip = 2 * x + y
    sibling = (x, y, 1 - c)
    peers = [(_flip(x, dx), _flip(y, dy)) for dx, dy in CHIP_FLIPS]

    def remote(src, p, half, a, k, to):
        return pltpu.make_async_remote_copy(src_ref=src, dst_ref=outs[a].at[p, half], send_sem=send_sems.at[a, k],
                                            recv_sem=recv_sems.at[a, k], device_id=to, device_id_type=MESH_ID)

    def local(a):
        return pltpu.make_async_copy(ins[a], outs[a].at[chip], loc_sems.at[a])

    def first(a, k):
        return remote(ins[a].at[c], chip, c, a, k, (*peers[k], c))

    def passed(a, k):
        p = 2 * peers[k][0] + peers[k][1]
        return remote(outs[a].at[p, c], p, c, a, 3 + k, sibling)

    def start():
        for a in range(n):
            local(a).start()
            for k in range(3):
                first(a, k).start()

    def mid():
        for a in range(n):
            for k in range(3):
                p = 2 * peers[k][0] + peers[k][1]
                remote(outs[a].at[p, c], p, c, a, k, sibling).wait_recv()
                passed(a, k).start()

    def end():
        for a in range(n):
            for k in range(3):
                p = 2 * peers[k][0] + peers[k][1]
                remote(outs[a].at[p, 1 - c], p, 1 - c, a, 3 + k, sibling).wait_recv()
        for a in range(n):
            for k in range(3):
                first(a, k).wait_send()
                passed(a, k).wait_send()
            local(a).wait()

    return start, mid, end


def _gather_chip4_halves(arrs):
    return _Exchange(arrs, [_sds((N_CHIPS,) + a.shape, a.dtype) for a in arrs], 6, _gather_halves_phases)


def _reduce_phases(n_whole, ins, outs, send_sems, recv_sems, loc_sems):
    n = len(ins)
    x, y, c = _my_place()
    chip = 2 * x + y
    sibling = (x, y, 1 - c)
    peers = [(_flip(x, dx), _flip(y, dy)) for dx, dy in CHIP_FLIPS]

    def remote(src, slot, a, k, to):
        return pltpu.make_async_remote_copy(src_ref=src, dst_ref=outs[a].at[slot], send_sem=send_sems.at[a, k],
                                            recv_sem=recv_sems.at[a, k], device_id=to, device_id_type=MESH_ID)

    def block(a, p):
        return ins[a] if a >= n - n_whole else ins[a].at[p]

    def local(a):
        return pltpu.make_async_copy(block(a, chip), outs[a].at[2 * chip + c], loc_sems.at[a])

    def own(a):
        return remote(block(a, chip), 2 * chip + c, a, 0, sibling)

    def first(a, k):
        return remote(block(a, 2 * peers[k][0] + peers[k][1]), 2 * chip + c, a, 1 + k, (*peers[k], c))

    def passed(a, k):
        slot = 2 * (2 * peers[k][0] + peers[k][1]) + c
        return remote(outs[a].at[slot], slot, a, 4 + k, sibling)

    def start():
        for a in range(n):
            local(a).start()
            own(a).start()
            for k in range(3):
                first(a, k).start()

    def mid():
        for a in range(n):
            for k in range(3):
                slot = 2 * (2 * peers[k][0] + peers[k][1]) + c
                remote(outs[a].at[slot], slot, a, 1 + k, sibling).wait_recv()
                passed(a, k).start()

    def end():
        for a in range(n):
            remote(outs[a].at[2 * chip + 1 - c], 2 * chip + 1 - c, a, 0, sibling).wait_recv()
            for k in range(3):
                slot = 2 * (2 * peers[k][0] + peers[k][1]) + 1 - c
                remote(outs[a].at[slot], slot, a, 4 + k, sibling).wait_recv()
        for a in range(n):
            own(a).wait_send()
            for k in range(3):
                first(a, k).wait_send()
                passed(a, k).wait_send()
            local(a).wait()

    return start, mid, end


def _reduce_exchange(arrs, whole=()):
    shapes = [_sds((N_DEV,) + a.shape[1:], a.dtype) for a in arrs] + [_sds((N_DEV,) + a.shape, a.dtype) for a in whole]
    return _Exchange(list(arrs) + list(whole), shapes, 7, functools.partial(_reduce_phases, len(whole)))


def _even_in_layout(w):
    return jnp.concatenate([w[:, 0:1696], jnp.zeros((w.shape[0], 96), w.dtype), w[:, 1696:2208]], axis=1)


def _even_in_unlayout(g):
    return jnp.concatenate([g[:, 0:1696], g[:, 1792:2304]], axis=1)


def _uq_layout(w):
    per = B_NOPE + B_ROPE
    pad = jnp.zeros((w.shape[0], LANES - B_ROPE), w.dtype)
    nope = [w[:, per * h:per * h + B_NOPE] for h in range(B_HEADS)]
    rope = [jnp.concatenate([w[:, per * h + B_NOPE:per * (h + 1)], pad], axis=1) for h in range(B_HEADS)]
    return jnp.concatenate(nope + rope, axis=1)


def _uq_unlayout(g):
    parts = []
    for h in range(B_HEADS):
        parts += [g[:, B_NOPE * h:B_NOPE * (h + 1)], g[:, 512 + LANES * h:512 + LANES * h + B_ROPE]]
    return jnp.concatenate(parts, axis=1)


def _block_diag(blocks):
    rows = []
    for h, blk in enumerate(blocks):
        r, cdim = blk.shape
        n = len(blocks)
        rows.append(jnp.concatenate([jnp.zeros((r, cdim * h), blk.dtype), blk, jnp.zeros((r, cdim * (n - 1 - h)), blk.dtype)],
                                    axis=1))
    return jnp.concatenate(rows, axis=0)


def _uk_layout(w):
    return _block_diag([w[:, h, :].T for h in range(B_HEADS)])


def _uk_unlayout(g):
    return jnp.stack([g[B_NOPE * h:B_NOPE * (h + 1), LANES * h:LANES * (h + 1)].T for h in range(B_HEADS)], axis=1)


def _uv_layout(w):
    return _block_diag([w[:, h, :] for h in range(B_HEADS)])


def _uv_unlayout(g):
    return jnp.stack([g[LANES * h:LANES * (h + 1), B_V * h:B_V * (h + 1)] for h in range(B_HEADS)], axis=1)


def _rope_tables(S):
    inv = ROPE_THETA ** (-jnp.arange(0, 32, 2, dtype=F32) / 32)
    tok = jnp.arange(S)

    def tab(pos):
        ang = pos.astype(F32)[:, None] * inv[None, :]
        cos, sin = jnp.cos(ang), jnp.sin(ang)
        return jnp.concatenate([cos, cos], axis=1), jnp.concatenate([-sin, sin], axis=1)

    cr, sr = tab(tok // GRID_W)
    cc, sc = tab(tok % GRID_W)
    ct, st = tab(tok)
    return (jnp.tile(jnp.concatenate([cr, cc], axis=1), (1, 2)), jnp.tile(jnp.concatenate([sr, sc], axis=1), (1, 2)),
            jnp.tile(ct, (1, 4)), jnp.tile(st, (1, 4)))


A_TQ, A_TK, A_SUB = 512, 4096, 512
A_FWD_SUB = 1024
B_TQ, B_TK, B_SUB = 128, 4096, 1024
B_BWD_TK, B_BWD_SUB = 4096, 512
C_T = 256
C_BLOCKS_PER_STEP = 8
KV_SHARE = 2


def _local_step(x0, tgt, mod, norm_w, wie, wuq, wuk, wuv, late_shards, a_q_norm, a_k_norm, q_lora_norm, kv_lora_norm,
                c_sink, final_norm):
    S = x0.shape[0]
    mod3 = mod.reshape(2, 3, D_MODEL)
    ca, sa, ct, st = _rope_tables(S)
    lane_seg = np.arange(LANES) // HEAD_DIM
    seg = jnp.asarray((lane_seg[:, None] == lane_seg[None, :]).astype(np.float32)).astype(BF16)
    qn = jnp.tile(a_q_norm.reshape(1, HEAD_DIM), (1, 2))
    kn = jnp.tile(a_k_norm.reshape(1, HEAD_DIM), (1, 2))
    qln, kvln = q_lora_norm.reshape(1, B_Q_LORA), kv_lora_norm.reshape(1, B_KV_LORA)
    nw0, nw1 = norm_w[0:1], norm_w[1:2]
    gate0, gate1 = mod3[0, 2:3], mod3[1, 2:3]
    a_tq, a_tk, b_tq, b_tk, bb_tk, c_t = min(A_TQ, S), min(A_TK, S), min(B_TQ, S), min(B_TK, S), min(B_BWD_TK, S), min(C_T, S)
    a_sub, b_sub, bb_sub = min(A_SUB, a_tk), min(B_SUB, b_tk), min(B_BWD_SUB, bb_tk)

    h0, proj_e, qa, ka, va, qcat, kcat, ka_t, va_t, kcat_t = _even_pre_fwd(x0, mod3[0], nw0, wie, qn, kn, seg, ca, sa, ct, st,
                                                                           qln, kvln, wuq, wuk)
    oa, lse_a, woe_g, wio_g, woo_g = _pp_fwd(qa, ka, va_t, kdiv=KV_SHARE, tq=a_tq, tk=a_tk, sub=min(A_FWD_SUB, a_tk), name="attn_a_fwd",
                                             side=_gather_chip4_halves(late_shards))
    woe = woe_g.reshape(D_MODEL, D_MODEL)
    wio = wio_g.reshape(N_CHIPS, D_MODEL, ODD_IN // N_CHIPS)
    woo = woo_g.reshape(D_MODEL, D_MODEL)
    olat, lse_b = _mla_fwd(qcat, kcat, kcat_t, tq=b_tq, tk=b_tk, sub=b_sub)
    y0, x1 = _even_post_fwd(oa, olat, proj_e, x0, gate0, wuv, woe)
    h1, proj_o, qc, kc, vc, kc_t, vc_t = _odd_pre_fwd(x1, mod3[1], nw1, wio)
    slopes = 2.0 ** (-8.0 * jnp.arange(1, C_HEADS + 1, dtype=F32) / C_HEADS)
    slope_rows = jnp.repeat(slopes.reshape(C_HEADS // 2, 2), c_t, axis=1)[:, None, :]
    sink_rows = jnp.repeat(c_sink.reshape(C_HEADS // 2, 2), c_t, axis=1)[:, None, :]
    win_dist = _win_dist_table(S, c_t)
    oc, lse_c = _win_fwd(qc, kc, vc_t, win_dist, slope_rows, sink_rows, kdiv=KV_SHARE, tq=c_t, nbs=C_BLOCKS_PER_STEP,
                         name="attn_c_fwd")
    doc, dgc, dx2, dwoo, st_f = _odd_post(oc, proj_o, x1, gate1, woo, final_norm.reshape(1, D_MODEL), tgt)
    dqc, dkc, dvc, dsink_raw = _win_bwd(qc, kc, kc_t, vc, oc, doc, lse_c, win_dist, slope_rows, sink_rows, kdiv=KV_SHARE, tq=c_t,
                                        nbs=C_BLOCKS_PER_STEP, name="attn_c_bwd")
    dx1, dwio, st_1 = _odd_pre_bwd(dqc, dkc, dvc, dgc, h1, x1, dx2, mod3[1], nw1, wio)
    doa, dga, dgb, dolat, dwoe, dwuv, st_e = _even_post_bwd(dx1, y0, oa, olat, proj_e, gate0, wuv, woe)
    late_grads = _reduce_exchange([dwoe.reshape(N_CHIPS, D_MODEL // N_CHIPS, D_MODEL), dwio,
                                   dwoo.reshape(N_CHIPS, D_MODEL // N_CHIPS, D_MODEL)])
    dqa, dka, dva, p_woe, p_wio, p_woo = _pp_bwd(qa, ka, ka_t, va, oa, doa, lse_a, kdiv=KV_SHARE, tq=a_tq, tk=a_tk, sub=a_sub,
                                                 name="attn_a_bwd", side=late_grads)
    dqcat, dkcat = _mla_bwd(qcat, kcat, kcat_t, olat, dolat, lse_b, tq=b_tq, tk=bb_tk, sub=bb_sub)
    dx0, dwie, dwuq, dwuk, st_0, nst = _even_pre_bwd(x0, h0, proj_e, dqa, dka, dva, dga, dgb, dqcat, dkcat, dx1, mod3[0], nw0,
                                                     wie, qn, kn, seg, ca, sa, ct, st, qln, kvln, wuq, wuk)
    dsink_pairs = jnp.stack([dsink_raw[:, 0, 0], dsink_raw[:, 1, 0]], axis=1).reshape(C_HEADS)
    return dict(
        loss_row=st_f[2:3], dx=dx0,
        dmod=jnp.stack([jnp.concatenate([st_0[0], st_0[1], st_e[0]]), jnp.concatenate([st_1[0], st_1[1], st_f[1]])]),
        norm_w=jnp.stack([st_0[2], st_1[2]]), final_norm=st_f[0],
        a_q_norm=nst[0:1, 0:HEAD_DIM], a_k_norm=nst[1:2, 0:HEAD_DIM], b_q_lora_norm=nst[2:3, :], b_kv_lora_norm=nst[3:4, 0:LANES],
        c_sink=dsink_pairs.reshape(1, C_HEADS),
        even_w_in=dwie, b_w_uq=dwuq, b_w_uk=dwuk, b_w_uv=dwuv, even_w_out=p_woe, odd_w_in=p_wio, odd_w_out=p_woo)


WEIGHT_NAMES = ("norm_w", "ada_w", "ada_b", "even_w_in", "a_q_norm", "a_k_norm", "b_q_lora_norm", "b_kv_lora_norm", "b_w_uq",
                "b_w_uk", "b_w_uv", "even_w_out", "odd_w_in", "c_sink", "odd_w_out", "final_norm")


def _cols_to_chips(g):
    r, n4 = g.shape
    return jnp.transpose(g.reshape(r, N_CHIPS, n4 // N_CHIPS), (1, 0, 2))


def _chips_to_cols(g):
    p, r, n = g.shape
    return jnp.transpose(g, (1, 0, 2)).reshape(r, p * n)


def kernel(x, c, norm_w, ada_w, ada_b, even_w_in, a_q_norm, a_k_norm, b_q_lora_norm, b_kv_lora_norm, b_w_uq, b_w_uk, b_w_uv, even_w_out, odd_w_in, c_sink, odd_w_out, final_norm, loss_target, m_norm_w, m_ada_w, m_ada_b, m_even_w_in, m_a_q_norm, m_a_k_norm, m_b_q_lora_norm, m_b_kv_lora_norm, m_b_w_uq, m_b_w_uk, m_b_w_uv, m_even_w_out, m_odd_w_in, m_c_sink, m_odd_w_out, m_final_norm, v_norm_w, v_ada_w, v_ada_b, v_even_w_in, v_a_q_norm, v_a_k_norm, v_b_q_lora_norm, v_b_kv_lora_norm, v_b_w_uq, v_b_w_uk, v_b_w_uv, v_even_w_out, v_odd_w_in, v_c_sink, v_odd_w_out, v_final_norm):
    given = dict(locals())
    xi, yi, ci = _my_place()
    chip = 2 * xi + yi
    dev = 2 * chip + ci
    n_ada = ada_w.shape[2]

    (c_all,) = _gather_dev8([c], "gather_c")
    c_all = c_all.reshape(N_DEV, D_MODEL)
    bias = lax.dynamic_slice_in_dim(ada_b, chip * n_ada, n_ada, axis=1).reshape(2, 1, n_ada)
    mod_cols = _ada_fwd(c_all, ada_w, bias)
    def halves(w):
        return w.astype(BF16).reshape((2, w.shape[0] // 2) + w.shape[1:])

    mod_all, wie_g, wuq_g = _gather_chip4_halves([mod_cols, halves(even_w_in[0]), halves(b_w_uq[0])]).run("gather_weights")
    wie_g = wie_g.reshape(N_CHIPS, D_MODEL, EVEN_IN // N_CHIPS)
    wuq_g = wuq_g.reshape(N_CHIPS, B_Q_LORA, -1)
    mod = jnp.transpose(lax.dynamic_index_in_dim(mod_all, dev, axis=2, keepdims=False), (1, 0, 2)).reshape(2, 3 * D_MODEL)

    res = _local_step(
        x[0], loss_target[0], mod, norm_w,
        _even_in_layout(_chips_to_cols(wie_g)), _uq_layout(_chips_to_cols(wuq_g)), _uk_layout(b_w_uk[0].astype(BF16)),
        _uv_layout(b_w_uv[0].astype(BF16)), [halves(even_w_out[0]), halves(odd_w_in[0]), halves(odd_w_out[0])],
        a_q_norm, a_k_norm, b_q_lora_norm, b_kv_lora_norm, c_sink, final_norm)

    latent = jnp.stack([_uk_unlayout(res["b_w_uk"]).reshape(B_KV_LORA, 512),
                        _uv_unlayout(res["b_w_uv"]).reshape(B_KV_LORA, 512)]).astype(BF16)
    p_wie, p_wuq, small_all, latent_all = _reduce_exchange(
        [_cols_to_chips(_even_in_unlayout(res["even_w_in"].astype(BF16))), _cols_to_chips(_uq_unlayout(res["b_w_uq"].astype(BF16)))],
        whole=[_pack_small(res), latent]).run("reduce_exchange")
    shard_parts = dict(even_w_in=p_wie, b_w_uq=p_wuq, **{k: res[k] for k in ("even_w_out", "odd_w_in", "odd_w_out")})
    dmod_all = small_all[:, 0:6, :].reshape(N_DEV, 2, 3 * D_MODEL)
    dmod_cols = jnp.transpose(lax.dynamic_slice_in_dim(dmod_all, chip * n_ada, n_ada, axis=2), (1, 0, 2))
    parts = dict(shard_parts)
    parts["ada_w"] = _ada_bwd(c_all.T, dmod_cols).reshape(1, 2 * D_MODEL, n_ada)
    parts["b_w_uk"], parts["b_w_uv"] = latent_all[:, 0], latent_all[:, 1]

    def as2d(a):
        return a.reshape((-1, a.shape[-1]) if a.ndim > 1 else (1, a.shape[0]))

    results = {}
    small_outs = _adam_small(small_all, *[[as2d(given[pre + k]) for k in SMALL_WEIGHTS] for pre in ("", "m_", "v_")])
    for idx, k in enumerate(SMALL_WEIGHTS):
        results[k] = small_outs[4 * idx:4 * idx + 4]
    for k, p in parts.items():
        shape2 = (p.shape[-2], p.shape[-1])
        results[k] = _adam(p, given[k].reshape(shape2), given["m_" + k].reshape(shape2), given["v_" + k].reshape(shape2),
                           "adam_" + k)
    by_kind = [[results[k][t].reshape(given[k].shape) for k in WEIGHT_NAMES] for t in range(4)]
    return (small_outs[-1][0, 0], res["dx"][None], *by_kind[0], *by_kind[1], *by_kind[2], *by_kind[3])
```

```python
import functools

import numpy as np
import jax
import jax.numpy as jnp
from jax import lax
from jax.experimental import pallas as pl
from jax.experimental.pallas import tpu as pltpu

F32 = jnp.float32
BF16 = jnp.bfloat16
HIGHEST = lax.Precision.HIGHEST
MESH_ID = pl.DeviceIdType.MESH

D_MODEL = 1024
HEAD_DIM = 64
GRID_W = 64
EPS = 1e-6
ROPE_THETA = 10000.0
B_HEADS, B_NOPE, B_ROPE, B_V = 8, 64, 32, 64
B_Q_LORA, B_KV_LORA = 256, 128
C_HEADS = 16
WINDOW = 128
EVEN_IN, ODD_IN = 2208, 2560
EVEN_P = 2304
N_CHIPS, N_DEV = 4, 8
LANES = 128
NEG = -1e30
VMEM_LIMIT = 60 * 1024 * 1024

ADAM_LR, ADAM_B1, ADAM_B2, ADAM_EPS, ADAM_WD, ADAM_STEP = 0.001, 0.9, 0.999, 1e-08, 0.01, 10

ROW_TILE = 512
IN_PROJ_ROW_TILE = 256


def _dot(a, b):
    return lax.dot_general(a, b, (((1,), (0,)), ((), ())), preferred_element_type=F32)


def _dot_nt(a, b):
    return lax.dot_general(a, b, (((1,), (1,)), ((), ())), preferred_element_type=F32)


def _dot_tn(a, b):
    return lax.dot_general(a, b, (((0,), (0,)), ((), ())), preferred_element_type=F32)


def _dot_f32(a, b):
    return lax.dot_general(a, b, (((1,), (0,)), ((), ())), precision=HIGHEST, preferred_element_type=F32)


def _sigmoid(x):
    return 1.0 / (1.0 + jnp.exp(-x))


def _silu_and_grad(g):
    s = _sigmoid(g)
    return g * s, s * (1.0 + g * (1.0 - s))


def _lane_iota():
    return lax.broadcasted_iota(jnp.int32, (1, LANES), 1)


def _partner(x, lane):
    return jnp.where((lane % 32) < 16, pltpu.roll(x, LANES - 16, 1), pltpu.roll(x, 16, 1))


def _rot(x, cos, sin_signed, lane):
    return x * cos + _partner(x, lane) * sin_signed


def _rot_bwd(dy, cos, sin_signed, lane):
    return dy * cos + _partner(dy * sin_signed, lane)


def _rms(x):
    return lax.rsqrt(jnp.mean(x * x, axis=-1, keepdims=True) + EPS)


def _rms_bwd(x, r, g):
    return r * g - x * (r * r * r) * jnp.mean(x * g, axis=-1, keepdims=True)


def _seg_mean(v, seg_ones):
    hi = v.astype(BF16)
    lo = (v - hi.astype(F32)).astype(BF16)
    return (_dot(hi, seg_ones) + _dot(lo, seg_ones)) * (1.0 / HEAD_DIM)


def _dup_heads(x, lane):
    swapped = pltpu.roll(x, HEAD_DIM, 1)
    lo = lane < HEAD_DIM
    return jnp.concatenate([jnp.where(lo, x, swapped), jnp.where(lo, swapped, x)], axis=1)


def _fold_heads(x2, lane):
    a, b = x2[:, 0:LANES], x2[:, LANES:2 * LANES]
    return jnp.where(lane < HEAD_DIM, a + pltpu.roll(a, HEAD_DIM, 1), b + pltpu.roll(b, HEAD_DIM, 1))


def _row_spec(ts, cols):
    return pl.BlockSpec((ts, cols), lambda i: (i, 0))


def _full_spec(shape, single=True):
    nd = len(shape)
    if single:
        return pl.BlockSpec(shape, lambda i: (0,) * nd, pipeline_mode=pl.Buffered(1))
    return pl.BlockSpec(shape, lambda i: (0,) * nd)


def _sds(shape, dtype):
    return jax.ShapeDtypeStruct(shape, dtype)


def _params(sem):
    return pltpu.CompilerParams(dimension_semantics=sem, vmem_limit_bytes=VMEM_LIMIT)


def _even_pre_fwd(x, mod, nw, wie, qn, kn, seg, ca, sa, ct, st, qln, kvln, wuq, wuk):
    S = x.shape[0]
    ts = min(IN_PROJ_ROW_TILE, S)

    def body(x_ref, mod_ref, nw_ref, wie_ref, qn_ref, kn_ref, seg_ref, ca_ref, sa_ref, ct_ref, st_ref, qln_ref,
             kvln_ref, wuq_ref, wuk_ref, h_ref, proj_ref, qa_ref, ka_ref, va_ref, qcat_ref, kcat_ref, kat_ref, vat_ref, kcatt_ref):
        xv = x_ref[...]
        h = (xv * _rms(xv) * nw_ref[...]) * (1.0 + mod_ref[1:2, :]) + mod_ref[0:1, :]
        hb = h.astype(BF16)
        h_ref[...] = hb
        proj = _dot(hb, wie_ref[...])
        proj_ref[...] = proj
        lane = _lane_iota()
        ca_v, sa_v, ct_v, st_v = ca_ref[...], sa_ref[...], ct_ref[...], st_ref[...]
        seg_v = seg_ref[...]
        for cb in range(4):
            xc = proj[:, LANES * cb:LANES * (cb + 1)]
            r = lax.rsqrt(_seg_mean(xc * xc, seg_v) + EPS)
            y = _rot(xc * r * qn_ref[...], ca_v, sa_v, lane)
            qa_ref[:, LANES * cb:LANES * (cb + 1)] = (y * 0.125).astype(BF16)
        kc = proj[:, 512:640]
        r = lax.rsqrt(_seg_mean(kc * kc, seg_v) + EPS)
        ka_v = _dup_heads(_rot(kc * r * kn_ref[...], ca_v, sa_v, lane), lane)
        ka_ref[...] = ka_v.astype(BF16)
        kat_ref[...] = ka_v.T.astype(BF16)
        va_v = _dup_heads(proj[:, 640:768], lane)
        va_ref[...] = va_v.astype(BF16)
        vat_ref[...] = va_v.T.astype(BF16)
        cq = proj[:, 1280:1536]
        cqn = (cq * _rms(cq) * qln_ref[...]).astype(BF16)
        ckv = proj[:, 1536:1664]
        ckvn = ckv * _rms(ckv) * kvln_ref[...]
        qb = _dot(cqn, wuq_ref[...])
        qlat = _dot(qb[:, 0:512].astype(BF16), wuk_ref[...])
        for hh in range(B_HEADS):
            qcat_ref[hh, :, 0:LANES] = qlat[:, LANES * hh:LANES * (hh + 1)].astype(BF16)
            qr = _rot(qb[:, 512 + LANES * hh:512 + LANES * (hh + 1)], ct_v, st_v, lane)
            qcat_ref[hh, :, LANES:2 * LANES] = qr.astype(BF16)
        kr = _rot(proj[:, 1664:1792], ct_v, st_v, lane)
        kcat_ref[:, 0:LANES] = ckvn.astype(BF16)
        kcat_ref[:, LANES:2 * LANES] = kr.astype(BF16)
        kcatt_ref[0:LANES, :] = ckvn.T.astype(BF16)
        kcatt_ref[LANES:2 * LANES, :] = kr.T.astype(BF16)

    col_spec = lambda rows: pl.BlockSpec((rows, ts), lambda i: (0, i))
    return pl.pallas_call(
        body, name="even_pre_fwd", grid=(S // ts,),
        in_specs=[_row_spec(ts, D_MODEL), _full_spec((3, D_MODEL)), _full_spec((1, D_MODEL)), _full_spec((D_MODEL, EVEN_P)),
                  _full_spec((1, LANES)), _full_spec((1, LANES)), _full_spec((LANES, LANES)),
                  _row_spec(ts, LANES), _row_spec(ts, LANES), _row_spec(ts, LANES), _row_spec(ts, LANES),
                  _full_spec((1, B_Q_LORA)), _full_spec((1, B_KV_LORA)), _full_spec((B_Q_LORA, 1536)), _full_spec((512, 1024))],
        out_specs=[_row_spec(ts, D_MODEL), _row_spec(ts, EVEN_P), _row_spec(ts, 512), _row_spec(ts, 2 * LANES), _row_spec(ts, 2 * LANES),
                   pl.BlockSpec((B_HEADS, ts, 2 * LANES), lambda i: (0, i, 0)), _row_spec(ts, 2 * LANES),
                   col_spec(2 * LANES), col_spec(2 * LANES), col_spec(2 * LANES)],
        out_shape=[_sds((S, D_MODEL), BF16), _sds((S, EVEN_P), F32), _sds((S, 512), BF16), _sds((S, 2 * LANES), BF16),
                   _sds((S, 2 * LANES), BF16), _sds((B_HEADS, S, 2 * LANES), BF16), _sds((S, 2 * LANES), BF16),
                   _sds((2 * LANES, S), BF16), _sds((2 * LANES, S), BF16), _sds((2 * LANES, S), BF16)],
        compiler_params=_params(("arbitrary",)),
    )(x, mod, nw, wie, qn, kn, seg, ca, sa, ct, st, qln, kvln, wuq, wuk)


MLA_SCALE = (B_NOPE + B_ROPE) ** -0.5
LOG2E = 1.4426950408889634

def _row_lo():
    return lax.broadcasted_iota(jnp.int32, (LANES, 1), 0) < HEAD_DIM


def _stack_cols(vT, rlo):
    zero = jnp.zeros_like(vT)
    return jnp.concatenate([jnp.where(rlo, vT, zero), jnp.where(rlo, zero, vT)], axis=1)


def _stack_rows(v, lo):
    zero = jnp.zeros_like(v)
    return jnp.concatenate([jnp.where(lo, v, zero), jnp.where(lo, zero, v)], axis=0)


def _pick_halves_T(xT, rlo, t):
    return jnp.where(rlo, xT[:, 0:t], xT[:, t:2 * t]).T


def _side_split(refs, n_in, n_out, n_scratch, side):
    ns = side.n if side is not None else 0
    cuts = np.cumsum([0, n_in, ns, n_out, ns, n_scratch])
    return [refs[a:b] for a, b in zip(cuts[:-1], cuts[1:])] + [refs[cuts[-1]:]]


def _side_hooks(side, side_ins, side_outs, side_sems, step, total):
    if side is None:
        return lambda: None
    start, mid, end = side.phases(side_ins, side_outs, side_sems)
    pl.when(step == 0)(start)
    pl.when(step == total // 2)(mid)
    return lambda: pl.when(step == total - 1)(end)


def _side_specs(side):
    if side is None:
        return [], [], [], [], []
    return list(side.arrs), [_ANY] * side.n, [_ANY] * side.n, list(side.out_shapes), side.sem_shapes()


def _pp_fwd(q, k, vT, *, kdiv, tq, tk, sub, name, side=None):
    S = k.shape[0]; nb = q.shape[1] // LANES; nq = S // tq; nkv = S // tk; nsub = tk // sub

    def body(*refs):
        (q_ref, k_ref, vT_ref), side_ins, (o_ref, lse_ref), side_outs, (qs, m_s, l_s, acc), side_sems = _side_split(refs, 3, 2, 4, side)
        j = pl.program_id(2)
        rlo = _row_lo()
        step = (pl.program_id(0) * nq + pl.program_id(1)) * nkv + j
        side_end = _side_hooks(side, side_ins, side_outs, side_sems, step, nb * nq * nkv)

        @pl.when(j == 0)
        def _():
            qs[...] = _stack_cols(q_ref[...].astype(F32).T, rlo).astype(BF16)
            m_s[...] = jnp.full((1, 2 * tq), NEG, F32)
            l_s[...] = jnp.zeros((1, 2 * tq), F32)
            acc[...] = jnp.zeros((LANES, 2 * tq), F32)

        qsv = qs[...]
        m, l, a = m_s[...], l_s[...], acc[...]
        s_cur = _dot(k_ref[0:sub, :], qsv)
        for t in range(nsub):
            if t + 1 < nsub:
                s_next = _dot(k_ref[sub * (t + 1):sub * (t + 2), :], qsv)
            m_new = jnp.maximum(m, jnp.max(s_cur, axis=0, keepdims=True))
            alpha = jnp.exp(m - m_new)
            p = jnp.exp(s_cur - m_new)
            l = alpha * l + jnp.sum(p, axis=0, keepdims=True)
            a = alpha * a + _dot(vT_ref[:, sub * t:sub * (t + 1)], p.astype(BF16))
            m = m_new
            if t + 1 < nsub:
                s_cur = s_next
        m_s[...], l_s[...], acc[...] = m, l, a

        @pl.when(j == nkv - 1)
        def _():
            l_f = l_s[...]
            o_ref[...] = _pick_halves_T(acc[...] / l_f, rlo, tq).astype(BF16)
            lse_ref[0, 0] = m_s[...] + jnp.log(l_f)

        side_end()

    s_args, s_in, s_out, s_shapes, s_sems = _side_specs(side)
    return pl.pallas_call(
        body, name=name, grid=(nb, nq, nkv),
        in_specs=[pl.BlockSpec((tq, LANES), lambda b, i, j: (i, b)), pl.BlockSpec((tk, LANES), lambda b, i, j: (j, b // kdiv)),
                  pl.BlockSpec((LANES, tk), lambda b, i, j: (b // kdiv, j))] + s_in,
        out_specs=[pl.BlockSpec((tq, LANES), lambda b, i, j: (i, b)),
                   pl.BlockSpec((1, 1, 1, 2 * tq), lambda b, i, j: (b, i, 0, 0))] + s_out,
        out_shape=[_sds((S, nb * LANES), BF16), _sds((nb, nq, 1, 2 * tq), F32)] + s_shapes,
        scratch_shapes=[pltpu.VMEM((LANES, 2 * tq), BF16), pltpu.VMEM((1, 2 * tq), F32), pltpu.VMEM((1, 2 * tq), F32),
                        pltpu.VMEM((LANES, 2 * tq), F32)] + s_sems,
        compiler_params=_params(("arbitrary",) * 3))(q, k, vT, *s_args)


def _pp_bwd(q, k, kT, v, o, do, lse, *, kdiv, tq, tk, sub, name, side=None):
    S = k.shape[0]; nb = q.shape[1] // LANES; nkb = k.shape[1] // LANES; nq = S // tq; nkv = S // tk; nsub = tk // sub

    def body(*refs):
        ((q_ref, k_ref, kT_ref, v_ref, o_ref, do_ref, lse_ref), side_ins, (dq_ref, dk_ref, dv_ref), side_outs,
         (qsT, qs, dosT, dos, delta_s, dq_acc), side_sems) = _side_split(refs, 7, 3, 6, side)
        b, i, j = pl.program_id(0), pl.program_id(1), pl.program_id(2)
        rlo = _row_lo()
        lo = lax.broadcasted_iota(jnp.int32, (1, LANES), 1) < HEAD_DIM
        side_end = _side_hooks(side, side_ins, side_outs, side_sems, (b * nq + i) * nkv + j, nb * nq * nkv)

        @pl.when((b % kdiv == 0) & (i == 0) & (j == 0))
        def _():
            dk_ref[...] = jnp.zeros((S, LANES), F32)
            dv_ref[...] = jnp.zeros((S, LANES), F32)

        @pl.when(j == 0)
        def _():
            qv = q_ref[...]
            qs[...] = _stack_rows(qv, lo)
            qsT[...] = _stack_cols(qv.astype(F32).T, rlo).astype(BF16)
            dov = do_ref[...].astype(F32)
            dos[...] = _stack_rows(dov.astype(BF16), lo)
            dosT[...] = _stack_cols(dov.T, rlo).astype(BF16)
            prodT = (dov * o_ref[...].astype(F32)).T
            delta_s[...] = jnp.concatenate([jnp.sum(jnp.where(rlo, prodT, 0.0), axis=0, keepdims=True),
                                            jnp.sum(jnp.where(rlo, 0.0, prodT), axis=0, keepdims=True)], axis=1)
            dq_acc[...] = jnp.zeros((LANES, 2 * tq), F32)

        qsTv, dosTv, qsv, dosv = qsT[...], dosT[...], qs[...], dos[...]
        lse_v, delta_v = lse_ref[0, 0], delta_s[...]
        dqa = dq_acc[...]
        s_cur = _dot(k_ref[0:sub, :], qsTv)
        dp_cur = _dot(v_ref[0:sub, :], dosTv)
        for t in range(nsub):
            if t + 1 < nsub:
                s_next = _dot(k_ref[sub * (t + 1):sub * (t + 2), :], qsTv)
                dp_next = _dot(v_ref[sub * (t + 1):sub * (t + 2), :], dosTv)
            p = jnp.exp(s_cur - lse_v)
            ds = (p * (dp_cur - delta_v)).astype(BF16)
            rows = pl.ds(pl.multiple_of(j * tk + sub * t, sub), sub)
            dv_ref[rows, :] += _dot(p.astype(BF16), dosv)
            dk_ref[rows, :] += _dot(ds, qsv)
            dqa = dqa + _dot(kT_ref[:, sub * t:sub * (t + 1)], ds)
            if t + 1 < nsub:
                s_cur, dp_cur = s_next, dp_next
        dq_acc[...] = dqa

        @pl.when(j == nkv - 1)
        def _():
            dq_ref[...] = _pick_halves_T(dq_acc[...], rlo, tq)

        side_end()

    qmap = lambda b, i, j: (i, b)
    kmap = lambda b, i, j: (j, b // kdiv)
    res = lambda b, i, j: (0, b // kdiv)
    s_args, s_in, s_out, s_shapes, s_sems = _side_specs(side)
    return pl.pallas_call(
        body, name=name, grid=(nb, nq, nkv),
        in_specs=[pl.BlockSpec((tq, LANES), qmap), pl.BlockSpec((tk, LANES), kmap), pl.BlockSpec((LANES, tk), lambda b, i, j: (b // kdiv, j)),
                  pl.BlockSpec((tk, LANES), kmap), pl.BlockSpec((tq, LANES), qmap), pl.BlockSpec((tq, LANES), qmap),
                  pl.BlockSpec((1, 1, 1, 2 * tq), lambda b, i, j: (b, i, 0, 0))] + s_in,
        out_specs=[pl.BlockSpec((tq, LANES), qmap), pl.BlockSpec((S, LANES), res), pl.BlockSpec((S, LANES), res)] + s_out,
        out_shape=[_sds((S, nb * LANES), F32), _sds((S, nkb * LANES), F32), _sds((S, nkb * LANES), F32)] + s_shapes,
        scratch_shapes=[pltpu.VMEM((LANES, 2 * tq), BF16), pltpu.VMEM((2 * tq, LANES), BF16), pltpu.VMEM((LANES, 2 * tq), BF16),
                        pltpu.VMEM((2 * tq, LANES), BF16), pltpu.VMEM((1, 2 * tq), F32), pltpu.VMEM((LANES, 2 * tq), F32)] + s_sems,
        compiler_params=_params(("arbitrary",) * 3))(q, k, kT, v, o, do, lse, *s_args)


MLA_C = MLA_SCALE * LOG2E


def _mla_fwd(q, kcat, kcatT, *, tq, tk, sub):
    S = kcat.shape[0]; nq, nkv = S // tq, S // tk; R = B_HEADS * tq; nsub = tk // sub

    def body(q_ref, k_ref, vT_ref, o_ref, lse_ref, qT, m_s, l_s, acc):
        j = pl.program_id(1)

        @pl.when(j == 0)
        def _():
            qT[...] = q_ref[...].reshape(R, 2 * LANES).astype(F32).T.astype(BF16)
            m_s[...] = jnp.full((1, R), NEG, F32)
            l_s[...] = jnp.zeros((1, R), F32)
            acc[...] = jnp.zeros((LANES, R), F32)

        qTv = qT[...]
        m, l, a = m_s[...], l_s[...], acc[...]
        s_cur = _dot(k_ref[0:sub, :], qTv)
        for t in range(nsub):
            if t + 1 < nsub:
                s_next = _dot(k_ref[sub * (t + 1):sub * (t + 2), :], qTv)
            m_new = jnp.maximum(m, jnp.max(s_cur, axis=0, keepdims=True))
            alpha = jnp.exp2((m - m_new) * MLA_C)
            p = jnp.exp2((s_cur - m_new) * MLA_C)
            l = alpha * l + jnp.sum(p, axis=0, keepdims=True)
            a = alpha * a + _dot(vT_ref[:, sub * t:sub * (t + 1)], p.astype(BF16))
            m = m_new
            if t + 1 < nsub:
                s_cur = s_next
        m_s[...], l_s[...], acc[...] = m, l, a

        @pl.when(j == nkv - 1)
        def _():
            l_f = l_s[...]
            o_ref[...] = (acc[...] / l_f).T.reshape(B_HEADS, tq, LANES).astype(BF16)
            lse_ref[0] = m_s[...] * MLA_SCALE + jnp.log(l_f)

    return pl.pallas_call(
        body, name="mla_fwd", grid=(nq, nkv),
        in_specs=[pl.BlockSpec((B_HEADS, tq, 2 * LANES), lambda i, j: (0, i, 0)), pl.BlockSpec((tk, 2 * LANES), lambda i, j: (j, 0)),
                  pl.BlockSpec((LANES, tk), lambda i, j: (0, j))],
        out_specs=[pl.BlockSpec((B_HEADS, tq, LANES), lambda i, j: (0, i, 0)), pl.BlockSpec((1, 1, R), lambda i, j: (i, 0, 0))],
        out_shape=[_sds((B_HEADS, S, LANES), BF16), _sds((nq, 1, R), F32)],
        scratch_shapes=[pltpu.VMEM((2 * LANES, R), BF16), pltpu.VMEM((1, R), F32), pltpu.VMEM((1, R), F32), pltpu.VMEM((LANES, R), F32)],
        compiler_params=_params(("arbitrary", "arbitrary")))(q, kcat, kcatT)


def _mla_bwd(q, kcat, kcatT, o, do, lse, *, tq, tk, sub):
    S = kcat.shape[0]; nq, nkv = S // tq, S // tk; R = B_HEADS * tq; nsub = tk // sub

    def body(q_ref, k_ref, kT_ref, o_ref, do_ref, lse_ref, dq_ref, dk_ref, qT, dosT, dos, delta_s, dq_acc):
        i, j = pl.program_id(0), pl.program_id(1)

        @pl.when((i == 0) & (j == 0))
        def _():
            dk_ref[...] = jnp.zeros((S, 2 * LANES), F32)

        @pl.when(j == 0)
        def _():
            qT[...] = q_ref[...].reshape(R, 2 * LANES).astype(F32).T.astype(BF16)
            dov = do_ref[...].reshape(R, LANES).astype(F32)
            dos[...] = dov.astype(BF16)
            dosT[...] = dov.T.astype(BF16)
            delta_s[...] = jnp.sum((dov * o_ref[...].reshape(R, LANES).astype(F32)).T, axis=0, keepdims=True)
            dq_acc[...] = jnp.zeros((2 * LANES, R), F32)

        qTv, dosTv, dosv = qT[...], dosT[...], dos[...]
        qv = q_ref[...].reshape(R, 2 * LANES)
        lse_v, delta_v = lse_ref[0] * LOG2E, delta_s[...]
        dqa = dq_acc[...]
        s_cur = _dot(k_ref[0:sub, :], qTv)
        dp_cur = _dot(k_ref[0:sub, 0:LANES], dosTv)
        for t in range(nsub):
            if t + 1 < nsub:
                s_next = _dot(k_ref[sub * (t + 1):sub * (t + 2), :], qTv)
                dp_next = _dot(k_ref[sub * (t + 1):sub * (t + 2), 0:LANES], dosTv)
            p = jnp.exp2(s_cur * MLA_C - lse_v)
            ds = (p * (dp_cur - delta_v) * MLA_SCALE).astype(BF16)
            rows = pl.ds(pl.multiple_of(j * tk + sub * t, sub), sub)
            dk_ref[rows, :] += _dot(ds, qv)
            dk_ref[rows, 0:LANES] += _dot(p.astype(BF16), dosv)
            dqa = dqa + _dot(kT_ref[:, sub * t:sub * (t + 1)], ds)
            if t + 1 < nsub:
                s_cur, dp_cur = s_next, dp_next
        dq_acc[...] = dqa

        @pl.when(j == nkv - 1)
        def _():
            dq_ref[...] = dq_acc[...].T.reshape(B_HEADS, tq, 2 * LANES)

    hspec = lambda w: pl.BlockSpec((B_HEADS, tq, w), lambda i, j: (0, i, 0))
    return pl.pallas_call(
        body, name="mla_bwd", grid=(nq, nkv),
        in_specs=[hspec(2 * LANES), pl.BlockSpec((tk, 2 * LANES), lambda i, j: (j, 0)), pl.BlockSpec((2 * LANES, tk), lambda i, j: (0, j)),
                  hspec(LANES), hspec(LANES), pl.BlockSpec((1, 1, R), lambda i, j: (i, 0, 0))],
        out_specs=[hspec(2 * LANES), pl.BlockSpec((S, 2 * LANES), lambda i, j: (0, 0))],
        out_shape=[_sds((B_HEADS, S, 2 * LANES), F32), _sds((S, 2 * LANES), F32)],
        scratch_shapes=[pltpu.VMEM((2 * LANES, R), BF16), pltpu.VMEM((LANES, R), BF16), pltpu.VMEM((R, LANES), BF16),
                        pltpu.VMEM((1, R), F32), pltpu.VMEM((2 * LANES, R), F32)],
        compiler_params=_params(("arbitrary", "arbitrary")))(q, kcat, kcatT, o, do, lse)


def _win_start(i, tq, nk, S):
    return pl.multiple_of(jnp.clip(i * tq - WINDOW, 0, S - nk), LANES)


def _win_dist_table(S, tq):
    nk = min(tq + 2 * WINDOW, S)
    nq = S // tq
    r = np.arange(nk)[:, None]
    c = (np.arange(2 * tq) % tq)[None, :]
    tabs = []
    for rel in (0, WINDOW, (nq - 1) * tq - (S - nk)):
        dist = np.abs(rel + c - r).astype(np.float32)
        tabs.append(np.where(dist <= WINDOW, dist, np.float32(1e32)))
    return jnp.asarray(np.stack(tabs))


def _win_dist_spec(nk, tq, nq):
    return pl.BlockSpec((1, nk, 2 * tq), lambda b, i: (jnp.where(i == 0, 0, jnp.where(i == nq - 1, 2, 1)), 0, 0))


def _win_fwd(q, k, vT, dist, slope, sink, *, kdiv, tq, nbs, name):
    S = k.shape[0]; nb = q.shape[1] // LANES; nq = S // tq; nk = min(tq + 2 * WINDOW, S)
    assert nb % nbs == 0 and nbs % kdiv == 0
    kvw = (nbs // kdiv) * LANES

    def body(q_ref, k_ref, vT_ref, dist_ref, slope_ref, sink_ref, o_ref, lse_ref):
        i = pl.program_id(1)
        rlo = _row_lo()
        k0 = _win_start(i, tq, nk, S)
        kk, vv, dd = k_ref[pl.ds(k0, nk), :], vT_ref[:, pl.ds(k0, nk)], dist_ref[0]
        for u in range(nbs):
            kv = slice(LANES * (u // kdiv), LANES * (u // kdiv + 1))
            qsT = _stack_cols(q_ref[:, LANES * u:LANES * (u + 1)].astype(F32).T, rlo).astype(BF16)
            s = _dot(kk[:, kv], qsT) - slope_ref[u] * dd
            sk = sink_ref[u]
            m = jnp.maximum(jnp.max(s, axis=0, keepdims=True), sk)
            p = jnp.exp(s - m)
            l = jnp.sum(p, axis=0, keepdims=True) + jnp.exp(sk - m)
            o_ref[:, LANES * u:LANES * (u + 1)] = _pick_halves_T(_dot(vv[kv, :], p.astype(BF16)) / l, rlo, tq).astype(BF16)
            lse_ref[u, 0] = m + jnp.log(l)

    row_spec = pl.BlockSpec((nbs, 1, 2 * tq), lambda b, i: (b, 0, 0))
    return pl.pallas_call(
        body, name=name, grid=(nb // nbs, nq),
        in_specs=[pl.BlockSpec((tq, nbs * LANES), lambda b, i: (i, b)), pl.BlockSpec((S, kvw), lambda b, i: (0, b)),
                  pl.BlockSpec((kvw, S), lambda b, i: (b, 0)), _win_dist_spec(nk, tq, nq), row_spec, row_spec],
        out_specs=[pl.BlockSpec((tq, nbs * LANES), lambda b, i: (i, b)), pl.BlockSpec((nbs, 1, 1, 2 * tq), lambda b, i: (b, i, 0, 0))],
        out_shape=[_sds((S, nb * LANES), BF16), _sds((nb, nq, 1, 2 * tq), F32)],
        compiler_params=_params(("arbitrary", "arbitrary")))(q, k, vT, dist, slope, sink)


def _win_bwd(q, k, kT, v, o, do, lse, dist, slope, sink, *, kdiv, tq, nbs, name):
    S = k.shape[0]; nb = q.shape[1] // LANES; nkb = k.shape[1] // LANES; nq = S // tq; nk = min(tq + 2 * WINDOW, S)
    assert nb % nbs == 0 and nbs % kdiv == 0
    nkv = nbs // kdiv
    kvw = nkv * LANES

    def body(q_ref, k_ref, kT_ref, v_ref, o_ref, do_ref, lse_ref, dist_ref, slope_ref, sink_ref, dq_ref, dk_ref, dv_ref, dsink_ref, ds_acc):
        i = pl.program_id(1)
        rlo = _row_lo()
        lo = lax.broadcasted_iota(jnp.int32, (1, LANES), 1) < HEAD_DIM

        @pl.when(i == 0)
        def _():
            dk_ref[...] = jnp.zeros((S, kvw), F32)
            dv_ref[...] = jnp.zeros((S, kvw), F32)
            ds_acc[...] = jnp.zeros((nbs, 2 * tq), F32)

        k0 = _win_start(i, tq, nk, S)
        rows = pl.ds(k0, nk)
        kk_all, vv_all, kkT_all, dd = k_ref[rows, :], v_ref[rows, :], kT_ref[:, rows], dist_ref[0]
        dv_sum, dk_sum = [None] * nkv, [None] * nkv
        for u in range(nbs):
            g = u // kdiv
            kv = slice(LANES * g, LANES * (g + 1))
            kk, vv, kkT = kk_all[:, kv], vv_all[:, kv], kkT_all[kv, :]
            cols = slice(LANES * u, LANES * (u + 1))
            qv = q_ref[:, cols]
            qs = _stack_rows(qv, lo)
            qsT = _stack_cols(qv.astype(F32).T, rlo).astype(BF16)
            dov = do_ref[:, cols].astype(F32)
            dos = _stack_rows(dov.astype(BF16), lo)
            dosT = _stack_cols(dov.T, rlo).astype(BF16)
            prodT = (dov * o_ref[:, cols].astype(F32)).T
            delta = jnp.concatenate([jnp.sum(jnp.where(rlo, prodT, 0.0), axis=0, keepdims=True),
                                     jnp.sum(jnp.where(rlo, 0.0, prodT), axis=0, keepdims=True)], axis=1)
            lse_v = lse_ref[u, 0]
            ds_acc[u:u + 1, :] += -jnp.exp(sink_ref[u] - lse_v) * delta
            p = jnp.exp(_dot(kk, qsT) - slope_ref[u] * dd - lse_v)
            ds = (p * (_dot(vv, dosT) - delta)).astype(BF16)
            dv_u, dk_u = _dot(p.astype(BF16), dos), _dot(ds, qs)
            dv_sum[g] = dv_u if dv_sum[g] is None else dv_sum[g] + dv_u
            dk_sum[g] = dk_u if dk_sum[g] is None else dk_sum[g] + dk_u
            dq_ref[:, cols] = (_pick_halves_T(_dot(kkT, ds), rlo, tq) * 0.125).astype(BF16)
        dv_ref[rows, :] += jnp.concatenate(dv_sum, axis=1)
        dk_ref[rows, :] += jnp.concatenate(dk_sum, axis=1)

        @pl.when(i == nq - 1)
        def _():
            acc = ds_acc[...]
            for u in range(nbs):
                dsink_ref[u] = jnp.concatenate(
                    [jnp.broadcast_to(jnp.sum(acc[u:u + 1, 0:tq], axis=1, keepdims=True), (1, LANES)),
                     jnp.broadcast_to(jnp.sum(acc[u:u + 1, tq:2 * tq], axis=1, keepdims=True), (1, LANES)),
                     jnp.zeros((6, LANES), F32)], axis=0)

    qmap = lambda b, i: (i, b)
    kv_spec = pl.BlockSpec((S, kvw), lambda b, i: (0, b))
    row_spec = pl.BlockSpec((nbs, 1, 2 * tq), lambda b, i: (b, 0, 0))
    wide = pl.BlockSpec((tq, nbs * LANES), qmap)
    return pl.pallas_call(
        body, name=name, grid=(nb // nbs, nq),
        in_specs=[wide, kv_spec, pl.BlockSpec((kvw, S), lambda b, i: (b, 0)), kv_spec, wide, wide,
                  pl.BlockSpec((nbs, 1, 1, 2 * tq), lambda b, i: (b, i, 0, 0)), _win_dist_spec(nk, tq, nq), row_spec, row_spec],
        out_specs=[wide, kv_spec, kv_spec, pl.BlockSpec((nbs, 8, LANES), lambda b, i: (b, 0, 0))],
        out_shape=[_sds((S, nb * LANES), BF16), _sds((S, nkb * LANES), F32), _sds((S, nkb * LANES), F32), _sds((nb, 8, LANES), F32)],
        scratch_shapes=[pltpu.VMEM((nbs, 2 * tq), F32)],
        compiler_params=_params(("arbitrary", "arbitrary")))(q, k, kT, v, o, do, lse, dist, slope, sink)


def _sum_rows(v):
    return jnp.sum(v, axis=0, keepdims=True)


def _norm_mod_bwd(dh, xv, mod_ref, nw_ref, stats_ref):
    r = _rms(xv)
    xn = xv * r
    nw = nw_ref[...]
    stats_ref[0:1, :] += _sum_rows(dh)
    stats_ref[1:2, :] += _sum_rows(dh * (xn * nw))
    dn = dh * (1.0 + mod_ref[1:2, :])
    stats_ref[2:3, :] += _sum_rows(dn * xn)
    return _rms_bwd(xv, r, dn * nw)


def _even_gate_specs(ts):
    return [pl.BlockSpec((ts, 256), lambda i, c=c: (i, c)) for c in (3, 4, 7, 8)]


def _even_post_fwd(oa, olat, proj, x, gate, wuv, woe):
    S = x.shape[0]
    ts = min(ROW_TILE, S)

    def body(oa_ref, ol_ref, ga0_ref, ga1_ref, gb0_ref, gb1_ref, x_ref, gate_ref, wuv_ref, woe_ref, y_ref, x1_ref):
        sa, _ = _silu_and_grad(jnp.concatenate([ga0_ref[...], ga1_ref[...]], axis=1))
        sb, _ = _silu_and_grad(jnp.concatenate([gb0_ref[...], gb1_ref[...]], axis=1))
        olc = jnp.concatenate([ol_ref[hh] for hh in range(B_HEADS)], axis=1).astype(BF16)
        ob = _dot(olc, wuv_ref[...])
        mix = jnp.concatenate([oa_ref[...] * sa, ob * sb], axis=1).astype(BF16)
        y = _dot(mix, woe_ref[...])
        y_ref[...] = y.astype(BF16)
        x1_ref[...] = x_ref[...] + gate_ref[...] * y

    return pl.pallas_call(
        body, name="even_post_fwd", grid=(S // ts,),
        in_specs=[_row_spec(ts, 512), pl.BlockSpec((B_HEADS, ts, LANES), lambda i: (0, i, 0))] + _even_gate_specs(ts) +
                 [_row_spec(ts, D_MODEL), _full_spec((1, D_MODEL)), _full_spec((1024, 512)), _full_spec((1024, D_MODEL))],
        out_specs=[_row_spec(ts, D_MODEL), _row_spec(ts, D_MODEL)],
        out_shape=[_sds((S, D_MODEL), BF16), _sds((S, D_MODEL), F32)],
        compiler_params=_params(("arbitrary",)),
    )(oa, olat, proj, proj, proj, proj, x, gate, wuv, woe)


def _odd_pre_fwd(x, mod, nw, wio):
    S = x.shape[0]
    ts = min(ROW_TILE, S)

    def body(x_ref, mod_ref, nw_ref, wio_ref, h_ref, proj_ref, q_ref, k_ref, v_ref, kt_ref, vt_ref):
        xv = x_ref[...]
        h = (xv * _rms(xv) * nw_ref[...]) * (1.0 + mod_ref[1:2, :]) + mod_ref[0:1, :]
        hb = h.astype(BF16)
        h_ref[...] = hb
        proj = jnp.concatenate([_dot(hb, wio_ref[p]) for p in range(N_CHIPS)], axis=1)
        proj_ref[...] = proj
        q_ref[...] = (proj[:, 0:1024] * 0.125).astype(BF16)
        lane = _lane_iota()
        k_v = jnp.concatenate([_dup_heads(proj[:, 1024 + LANES * j:1024 + LANES * (j + 1)], lane) for j in range(2)], axis=1)
        v_v = jnp.concatenate([_dup_heads(proj[:, 1280 + LANES * j:1280 + LANES * (j + 1)], lane) for j in range(2)], axis=1)
        k_ref[...] = k_v.astype(BF16)
        v_ref[...] = v_v.astype(BF16)
        kt_ref[...] = k_v.T.astype(BF16)
        vt_ref[...] = v_v.T.astype(BF16)

    col_spec = pl.BlockSpec((512, ts), lambda i: (0, i))
    return pl.pallas_call(
        body, name="odd_pre_fwd", grid=(S // ts,),
        in_specs=[_row_spec(ts, D_MODEL), _full_spec((3, D_MODEL)), _full_spec((1, D_MODEL)),
                  _full_spec((N_CHIPS, D_MODEL, ODD_IN // N_CHIPS))],
        out_specs=[_row_spec(ts, D_MODEL), _row_spec(ts, ODD_IN), _row_spec(ts, 1024), _row_spec(ts, 512), _row_spec(ts, 512),
                   col_spec, col_spec],
        out_shape=[_sds((S, D_MODEL), BF16), _sds((S, ODD_IN), F32), _sds((S, 1024), BF16), _sds((S, 512), BF16),
                   _sds((S, 512), BF16), _sds((512, S), BF16), _sds((512, S), BF16)],
        compiler_params=_params(("arbitrary",)),
    )(x, mod, nw, wio)


def _odd_post(oc, proj, x1, gate, woo, fw, tgt):
    S = x1.shape[0]
    ts = min(ROW_TILE, S)
    nsteps = S // ts

    def body(oc_ref, g0_ref, g1_ref, x_ref, gate_ref, woo_ref, fw_ref, tgt_ref, doc_ref, dgc_ref, dx2_ref, dwoo_out, stats_ref,
             dwoo_ref):
        @pl.when(pl.program_id(0) == 0)
        def _():
            dwoo_ref[...] = jnp.zeros((D_MODEL, D_MODEL), F32)
            stats_ref[...] = jnp.zeros((8, D_MODEL), F32)

        ocv = oc_ref[...]
        sg, dsg = _silu_and_grad(jnp.concatenate([g0_ref[...], g1_ref[...]], axis=1))
        mix = (ocv * sg).astype(BF16)
        woo_v = woo_ref[...]
        y = _dot(mix, woo_v)
        gate_v = gate_ref[...]
        x2 = x_ref[...] + gate_v * y
        r = _rms(x2)
        fw_v = fw_ref[...]
        xn = x2 * r
        err = xn * fw_v - tgt_ref[...]
        dout = err * (1.0 / D_MODEL)
        dx2 = _rms_bwd(x2, r, dout * fw_v)
        dx2_ref[...] = dx2
        stats_ref[0:1, :] += _sum_rows(dout * xn)
        stats_ref[1:2, :] += _sum_rows(dx2 * y)
        loss_t = 0.5 * jnp.sum(_sum_rows(err * dout), axis=-1, keepdims=True)
        stats_ref[2:3, :] += jnp.broadcast_to(loss_t, (1, D_MODEL))
        dy = (gate_v * dx2).astype(BF16)
        dmix = _dot_nt(dy, woo_v)
        dwoo_ref[...] += _dot_tn(mix, dy)
        doc_ref[...] = (dmix * sg).astype(BF16)
        dgc_ref[...] = (dmix * ocv * dsg).astype(BF16)

        @pl.when(pl.program_id(0) == nsteps - 1)
        def _():
            dwoo_out[...] = dwoo_ref[...].astype(BF16)

    gate_cols = [pl.BlockSpec((ts, 512), lambda i, c=c: (i, c)) for c in (3, 4)]
    return pl.pallas_call(
        body, name="odd_post", grid=(nsteps,),
        in_specs=[_row_spec(ts, D_MODEL)] + gate_cols + [_row_spec(ts, D_MODEL), _full_spec((1, D_MODEL)),
                  _full_spec((D_MODEL, D_MODEL)), _full_spec((1, D_MODEL)), _row_spec(ts, D_MODEL)],
        out_specs=[_row_spec(ts, D_MODEL), _row_spec(ts, D_MODEL), _row_spec(ts, D_MODEL),
                   _full_spec((D_MODEL, D_MODEL), single=False), _full_spec((8, D_MODEL), single=False)],
        out_shape=[_sds((S, D_MODEL), BF16), _sds((S, D_MODEL), BF16), _sds((S, D_MODEL), F32), _sds((D_MODEL, D_MODEL), BF16),
                   _sds((8, D_MODEL), F32)],
        scratch_shapes=[pltpu.VMEM((D_MODEL, D_MODEL), F32)],
        compiler_params=_params(("arbitrary",)),
    )(oc, proj, proj, x1, gate, woo, fw, tgt)


def _odd_pre_bwd(dq, dk, dv, dgc, h, x, dx_res, mod, nw, wio):
    S = x.shape[0]
    ts = min(IN_PROJ_ROW_TILE, S)
    nsteps = S // ts
    wsh = ODD_IN // N_CHIPS

    def body(dq_ref, dk_ref, dv_ref, dgc_ref, h_ref, x_ref, dxr_ref, mod_ref, nw_ref, wio_ref, dx_ref, dw_ref, stats_ref, dw_acc):
        @pl.when(pl.program_id(0) == 0)
        def _():
            dw_acc[...] = jnp.zeros((N_CHIPS, D_MODEL, wsh), F32)
            stats_ref[...] = jnp.zeros((8, D_MODEL), F32)

        lane = _lane_iota()
        dkv = [_fold_heads(r[:, 2 * LANES * j:2 * LANES * (j + 1)], lane).astype(BF16) for r in (dk_ref, dv_ref) for j in range(2)]
        dproj = jnp.concatenate([dq_ref[...]] + dkv + [dgc_ref[...]], axis=1)
        hv = h_ref[...]
        dh = None
        for p in range(N_CHIPS):
            dp_cols = dproj[:, wsh * p:wsh * (p + 1)]
            part = _dot_nt(dp_cols, wio_ref[p])
            dh = part if dh is None else dh + part
            dw_acc[p] += _dot_tn(hv, dp_cols)
        dx_ref[...] = dxr_ref[...] + _norm_mod_bwd(dh, x_ref[...], mod_ref, nw_ref, stats_ref)

        @pl.when(pl.program_id(0) == nsteps - 1)
        def _():
            dw_ref[...] = dw_acc[...].astype(BF16)

    return pl.pallas_call(
        body, name="odd_pre_bwd", grid=(nsteps,),
        in_specs=[_row_spec(ts, 1024), _row_spec(ts, 512), _row_spec(ts, 512), _row_spec(ts, 1024), _row_spec(ts, D_MODEL),
                  _row_spec(ts, D_MODEL), _row_spec(ts, D_MODEL), _full_spec((3, D_MODEL)), _full_spec((1, D_MODEL)),
                  _full_spec((N_CHIPS, D_MODEL, wsh))],
        out_specs=[_row_spec(ts, D_MODEL), _full_spec((N_CHIPS, D_MODEL, wsh), single=False), _full_spec((8, D_MODEL), single=False)],
        out_shape=[_sds((S, D_MODEL), F32), _sds((N_CHIPS, D_MODEL, wsh), BF16), _sds((8, D_MODEL), F32)],
        scratch_shapes=[pltpu.VMEM((N_CHIPS, D_MODEL, wsh), F32)],
        compiler_params=_params(("arbitrary",)),
    )(dq, dk, dv, dgc, h, x, dx_res, mod, nw, wio)


def _even_post_bwd(dx1, y, oa, olat, proj, gate, wuv, woe):
    S = dx1.shape[0]
    ts = min(ROW_TILE, S)
    nsteps = S // ts

    def body(dx_ref, y_ref, oa_ref, ol_ref, ga0_ref, ga1_ref, gb0_ref, gb1_ref, gate_ref, wuv_ref, woe_ref,
             doa_ref, dga_ref, dgb_ref, dol_ref, dwoe_out, dwuv_ref, stats_ref, dwoe_ref):
        @pl.when(pl.program_id(0) == 0)
        def _():
            dwoe_ref[...] = jnp.zeros((D_MODEL, D_MODEL), F32)
            dwuv_ref[...] = jnp.zeros((1024, 512), F32)
            stats_ref[...] = jnp.zeros((8, D_MODEL), F32)

        dxv = dx_ref[...]
        stats_ref[0:1, :] += _sum_rows(dxv * y_ref[...])
        dy = (gate_ref[...] * dxv).astype(BF16)
        sa, dsa = _silu_and_grad(jnp.concatenate([ga0_ref[...], ga1_ref[...]], axis=1))
        sb, dsb = _silu_and_grad(jnp.concatenate([gb0_ref[...], gb1_ref[...]], axis=1))
        olc = jnp.concatenate([ol_ref[hh] for hh in range(B_HEADS)], axis=1).astype(BF16)
        wuv_v = wuv_ref[...]
        ob = _dot(olc, wuv_v)
        oav = oa_ref[...]
        mix = jnp.concatenate([oav * sa, ob * sb], axis=1).astype(BF16)
        dmix = _dot_nt(dy, woe_ref[...])
        dwoe_ref[...] += _dot_tn(mix, dy)
        dma, dmb = dmix[:, 0:512], dmix[:, 512:1024]
        doa_ref[...] = (dma * sa).astype(BF16)
        dga_ref[...] = (dma * oav * dsa).astype(BF16)
        dgb_ref[...] = (dmb * ob * dsb).astype(BF16)
        dob = (dmb * sb).astype(BF16)
        dol = _dot_nt(dob, wuv_v)
        dwuv_ref[...] += _dot_tn(olc, dob)
        for hh in range(B_HEADS):
            dol_ref[hh] = dol[:, LANES * hh:LANES * (hh + 1)].astype(BF16)

        @pl.when(pl.program_id(0) == nsteps - 1)
        def _():
            dwoe_out[...] = dwoe_ref[...].astype(BF16)

    head_spec = pl.BlockSpec((B_HEADS, ts, LANES), lambda i: (0, i, 0))
    return pl.pallas_call(
        body, name="even_post_bwd", grid=(nsteps,),
        in_specs=[_row_spec(ts, D_MODEL), _row_spec(ts, D_MODEL), _row_spec(ts, 512), head_spec] + _even_gate_specs(ts) +
                 [_full_spec((1, D_MODEL)), _full_spec((1024, 512)), _full_spec((1024, D_MODEL))],
        out_specs=[_row_spec(ts, 512), _row_spec(ts, 512), _row_spec(ts, 512), head_spec,
                   _full_spec((D_MODEL, D_MODEL), single=False), _full_spec((1024, 512), single=False),
                   _full_spec((8, D_MODEL), single=False)],
        out_shape=[_sds((S, 512), BF16), _sds((S, 512), BF16), _sds((S, 512), BF16), _sds((B_HEADS, S, LANES), BF16),
                   _sds((D_MODEL, D_MODEL), BF16), _sds((1024, 512), F32), _sds((8, D_MODEL), F32)],
        scratch_shapes=[pltpu.VMEM((D_MODEL, D_MODEL), F32)],
        compiler_params=_params(("arbitrary",)),
    )(dx1, y, oa, olat, proj, proj, proj, proj, gate, wuv, woe)


def _even_pre_bwd(x, h, proj, dqa, dka, dva, dga, dgb, dqcat, dkcat, dx_res, mod, nw, wie, qn, kn, seg, ca, sa, ct, st,
                  qln, kvln, wuq, wuk):
    S = x.shape[0]
    ts = min(IN_PROJ_ROW_TILE, S)
    nsteps = S // ts

    def body(x_ref, h_ref, proj_ref, dqa_ref, dka_ref, dva_ref, dga_ref, dgb_ref, dqc_ref, dkc_ref, dxr_ref, mod_ref, nw_ref,
             wie_ref, qn_ref, kn_ref, seg_ref, ca_ref, sa_ref, ct_ref, st_ref, qln_ref, kvln_ref, wuq_ref, wuk_ref,
             dx_ref, dwie_out, dwuq_out, dwuk_out, stats_ref, nstats_ref, dwie_ref, dwuq_ref, dwuk_ref, stage):
        @pl.when(pl.program_id(0) == 0)
        def _():
            dwie_ref[...] = jnp.zeros((D_MODEL, EVEN_P), F32)
            dwuq_ref[...] = jnp.zeros((B_Q_LORA, 1536), F32)
            dwuk_ref[...] = jnp.zeros((512, 1024), F32)
            stats_ref[...] = jnp.zeros((8, D_MODEL), F32)
            nstats_ref[...] = jnp.zeros((8, 256), F32)

        lane = _lane_iota()
        ca_v, sa_v, ct_v, st_v = ca_ref[...], sa_ref[...], ct_ref[...], st_ref[...]
        seg_v = seg_ref[...]

        def head_norm_bwd(xc, dy, w):
            r = lax.rsqrt(_seg_mean(xc * xc, seg_v) + EPS)
            g = dy * w
            dxc = r * g - xc * (r * r * r) * _seg_mean(xc * g, seg_v)
            return dxc, _sum_rows(dy * (xc * r))

        pieces = []
        dqn = jnp.zeros((1, LANES), F32)
        for cb in range(4):
            sl = slice(LANES * cb, LANES * (cb + 1))
            dy = _rot_bwd(dqa_ref[:, sl] * 0.125, ca_v, sa_v, lane)
            dxc, dw = head_norm_bwd(proj_ref[:, sl], dy, qn_ref[...])
            pieces.append(dxc)
            dqn = dqn + dw
        dxc, dkn = head_norm_bwd(proj_ref[:, 512:640], _rot_bwd(_fold_heads(dka_ref[...], lane), ca_v, sa_v, lane), kn_ref[...])
        pieces += [dxc, _fold_heads(dva_ref[...], lane), dga_ref[...]]
        nstats_ref[0:1, 0:LANES] += dqn + pltpu.roll(dqn, HEAD_DIM, 1)
        nstats_ref[1:2, 0:LANES] += dkn + pltpu.roll(dkn, HEAD_DIM, 1)

        cq = proj_ref[:, 1280:1536]
        rq = _rms(cq)
        cqn_f = cq * rq
        qln_v = qln_ref[...]
        cqn = (cqn_f * qln_v).astype(BF16)
        wuq_v, wuk_v = wuq_ref[...], wuk_ref[...]
        qnope = _dot(cqn, wuq_v[:, 0:512]).astype(BF16)
        dqlat = jnp.concatenate([dqc_ref[hh, :, 0:LANES] for hh in range(B_HEADS)], axis=1).astype(BF16)
        dqnope = _dot_nt(dqlat, wuk_v)
        dwuk_ref[...] += _dot_tn(qnope, dqlat)
        dqr = [_rot_bwd(dqc_ref[hh, :, LANES:2 * LANES], ct_v, st_v, lane) for hh in range(B_HEADS)]
        dqb = jnp.concatenate([dqnope] + dqr, axis=1).astype(BF16)
        dcqn = _dot_nt(dqb, wuq_v)
        dwuq_ref[...] += _dot_tn(cqn, dqb)
        nstats_ref[2:3, :] += _sum_rows(dcqn * cqn_f)
        dcq = _rms_bwd(cq, rq, dcqn * qln_v)
        ckv = proj_ref[:, 1536:1664]
        rk = _rms(ckv)
        dckvn = dkc_ref[:, 0:LANES]
        nstats_ref[3:4, 0:LANES] += _sum_rows(dckvn * (ckv * rk))
        dckv = _rms_bwd(ckv, rk, dckvn * kvln_ref[...])
        dkr = _rot_bwd(dkc_ref[:, LANES:2 * LANES], ct_v, st_v, lane)
        pieces += [dcq, dckv, dkr, dgb_ref[...]]
        dproj = jnp.concatenate([piece.astype(BF16) for piece in pieces], axis=1)
        dh = _dot_nt(dproj, wie_ref[...])
        dwie_ref[...] += _dot_tn(h_ref[...], dproj)
        dx_ref[...] = dxr_ref[...] + _norm_mod_bwd(dh, x_ref[...], mod_ref, nw_ref, stats_ref)

        @pl.when(pl.program_id(0) == nsteps - 1)
        def _():
            for r0 in range(0, D_MODEL, 256):
                stage[...] = dwie_ref[r0:r0 + 256, :].astype(BF16)
                pltpu.sync_copy(stage, dwie_out.at[pl.ds(r0, 256), :])
            pltpu.sync_copy(dwuq_ref, dwuq_out)
            pltpu.sync_copy(dwuk_ref, dwuk_out)

    return pl.pallas_call(
        body, name="even_pre_bwd", grid=(nsteps,),
        in_specs=[_row_spec(ts, D_MODEL), _row_spec(ts, D_MODEL), _row_spec(ts, EVEN_P), _row_spec(ts, 512), _row_spec(ts, 2 * LANES),
                  _row_spec(ts, 2 * LANES), _row_spec(ts, 512), _row_spec(ts, 512),
                  pl.BlockSpec((B_HEADS, ts, 2 * LANES), lambda i: (0, i, 0)), _row_spec(ts, 2 * LANES), _row_spec(ts, D_MODEL),
                  _full_spec((3, D_MODEL)), _full_spec((1, D_MODEL)), _full_spec((D_MODEL, EVEN_P)),
                  _full_spec((1, LANES)), _full_spec((1, LANES)), _full_spec((LANES, LANES)),
                  _row_spec(ts, LANES), _row_spec(ts, LANES), _row_spec(ts, LANES), _row_spec(ts, LANES),
                  _full_spec((1, B_Q_LORA)), _full_spec((1, B_KV_LORA)), _full_spec((B_Q_LORA, 1536)), _full_spec((512, 1024))],
        out_specs=[_row_spec(ts, D_MODEL), _ANY, _ANY, _ANY, _full_spec((8, D_MODEL), single=False), _full_spec((8, 256), single=False)],
        out_shape=[_sds((S, D_MODEL), F32), _sds((D_MODEL, EVEN_P), BF16), _sds((B_Q_LORA, 1536), F32), _sds((512, 1024), F32),
                   _sds((8, D_MODEL), F32), _sds((8, 256), F32)],
        scratch_shapes=[pltpu.VMEM((D_MODEL, EVEN_P), F32), pltpu.VMEM((B_Q_LORA, 1536), F32), pltpu.VMEM((512, 1024), F32),
                        pltpu.VMEM((256, EVEN_P), BF16)],
        compiler_params=_params(("arbitrary",)),
    )(x, h, proj, dqa, dka, dva, dga, dgb, dqcat, dkcat, dx_res, mod, nw, wie, qn, kn, seg, ca, sa, ct, st, qln, kvln, wuq, wuk)


def _ada_fwd(c_all, w, b):
    n = w.shape[2]

    def body(c_ref, w_ref, b_ref, o_ref):
        cv = c_ref[...]
        o_ref[0] = _dot_f32(cv * _sigmoid(cv), w_ref[0]) + b_ref[0]

    return pl.pallas_call(
        body, name="ada_fwd", grid=(2,),
        in_specs=[pl.BlockSpec((N_DEV, D_MODEL), lambda l: (0, 0)), pl.BlockSpec((1, D_MODEL, n), lambda l: (l, 0, 0)),
                  pl.BlockSpec((1, 1, n), lambda l: (l, 0, 0))],
        out_specs=pl.BlockSpec((1, N_DEV, n), lambda l: (l, 0, 0)),
        out_shape=_sds((2, N_DEV, n), F32),
        compiler_params=_params(("arbitrary",)),
    )(c_all, w, b)


def _ada_bwd(c_all_t, dmod):
    n = dmod.shape[2]

    def body(c_ref, d_ref, o_ref):
        cv = c_ref[...]
        act = cv * _sigmoid(cv)
        dv = d_ref[0]
        acc = act[:, 0:1] * dv[0:1, :]
        for bb in range(1, N_DEV):
            acc = acc + act[:, bb:bb + 1] * dv[bb:bb + 1, :]
        o_ref[0] = acc

    return pl.pallas_call(
        body, name="ada_bwd", grid=(2,),
        in_specs=[pl.BlockSpec((D_MODEL, N_DEV), lambda l: (0, 0)), pl.BlockSpec((1, N_DEV, n), lambda l: (l, 0, 0))],
        out_specs=pl.BlockSpec((1, D_MODEL, n), lambda l: (l, 0, 0)),
        out_shape=_sds((2, D_MODEL, n), F32),
        compiler_params=_params(("arbitrary",)),
    )(c_all_t, dmod)


ADAM_ROW_TILE = 512


def _adam_update(g, w, m, v):
    m_new = ADAM_B1 * m + (1.0 - ADAM_B1) * g
    v_new = ADAM_B2 * v + (1.0 - ADAM_B2) * jnp.square(g)
    m_hat = m_new / (1.0 - ADAM_B1 ** ADAM_STEP)
    v_hat = v_new / (1.0 - ADAM_B2 ** ADAM_STEP)
    return -ADAM_LR * (m_hat / (jnp.sqrt(v_hat) + ADAM_EPS) + ADAM_WD * w), m_new, v_new


SMALL_ROWS = dict(dmod=(0, D_MODEL), norm_w=(6, D_MODEL), final_norm=(8, D_MODEL), a_q_norm=(9, HEAD_DIM), a_k_norm=(10, HEAD_DIM),
                  b_q_lora_norm=(11, B_Q_LORA), b_kv_lora_norm=(12, B_KV_LORA), c_sink=(13, C_HEADS))
SMALL_WEIGHTS = ("ada_b", "norm_w", "final_norm", "a_q_norm", "a_k_norm", "b_q_lora_norm", "b_kv_lora_norm", "c_sink")
LOSS_ROW = 14


def _pack_small(res):
    def padded(v):
        return jnp.concatenate([v, jnp.zeros((v.shape[0], D_MODEL - v.shape[1]), F32)], axis=1)

    rows = [res["dmod"].reshape(6, D_MODEL), res["norm_w"], res["final_norm"].reshape(1, D_MODEL)]
    rows += [padded(res[k]) for k in ("a_q_norm", "a_k_norm", "b_q_lora_norm", "b_kv_lora_norm", "c_sink")]
    return jnp.concatenate(rows + [res["loss_row"], jnp.zeros((1, D_MODEL), F32)], axis=0)


def _adam_small(parts, ws, ms, vs):
    nw = len(SMALL_WEIGHTS)

    def body(*refs):
        p_ref = refs[0]
        w_refs, m_refs, v_refs = refs[1:1 + nw], refs[1 + nw:1 + 2 * nw], refs[1 + 2 * nw:1 + 3 * nw]
        outs = refs[1 + 3 * nw:]
        g_all = p_ref[0]
        for k in range(1, N_DEV):
            g_all = g_all + p_ref[k]
        for idx, name in enumerate(SMALL_WEIGHTS):
            if name == "ada_b":
                g = jnp.concatenate([jnp.concatenate([g_all[3 * l + t:3 * l + t + 1] for t in range(3)], axis=1) for l in range(2)],
                                    axis=0)
            else:
                row, width = SMALL_ROWS[name]
                g = g_all[row:row + w_refs[idx].shape[0], 0:width]
            d, m_new, v_new = _adam_update(g, w_refs[idx][...], m_refs[idx][...], v_refs[idx][...])
            outs[4 * idx][...], outs[4 * idx + 1][...], outs[4 * idx + 2][...], outs[4 * idx + 3][...] = g, d, m_new, v_new
        outs[4 * nw][...] = g_all[LOSS_ROW:LOSS_ROW + 1, 0:LANES]

    out_shape = []
    for w in ws:
        out_shape += [_sds(w.shape, F32)] * 4
    out_shape.append(_sds((1, LANES), F32))
    return pl.pallas_call(body, name="adam_small", out_shape=out_shape,
                          compiler_params=pltpu.CompilerParams(vmem_limit_bytes=VMEM_LIMIT))(parts, *ws, *ms, *vs)


def _adam(parts, w, m, v, name):
    P, R, C = parts.shape
    tr = R if R <= ADAM_ROW_TILE else ADAM_ROW_TILE
    assert R % tr == 0

    def body(p_ref, w_ref, m_ref, v_ref, g_ref, d_ref, nm_ref, nv_ref):
        g = p_ref[0].astype(F32)
        for k in range(1, P):
            g = g + p_ref[k].astype(F32)
        g_ref[...] = g
        d_ref[...], nm_ref[...], nv_ref[...] = _adam_update(g, w_ref[...], m_ref[...], v_ref[...])

    spec = pl.BlockSpec((tr, C), lambda i: (i, 0))
    return pl.pallas_call(
        body, name=name, grid=(R // tr,),
        in_specs=[pl.BlockSpec((P, tr, C), lambda i: (0, i, 0)), spec, spec, spec],
        out_specs=[spec, spec, spec, spec], out_shape=[_sds((R, C), F32)] * 4,
        compiler_params=_params(("arbitrary",)),
    )(parts, w, m, v)


_ANY = pl.BlockSpec(memory_space=pl.ANY)
CHIP_FLIPS = ((1, 0), (0, 1), (1, 1))
DEV_FLIPS = tuple((dx, dy, dc) for dx in (0, 1) for dy in (0, 1) for dc in (0, 1) if dx + dy + dc)


def _flip(a, d):
    return a if d == 0 else 1 - a


def _my_place():
    return lax.axis_index("x"), lax.axis_index("y"), lax.axis_index("c")


def _gather8_copies(ins, outs, send_sems, recv_sems, loc_sems):
    x, y, c = _my_place()
    me = 4 * x + 2 * y + c
    copies = []
    for a in range(len(ins)):
        copies.append(pltpu.make_async_copy(ins[a], outs[a].at[me], loc_sems.at[a]))
        for k, (dx, dy, dc) in enumerate(DEV_FLIPS):
            copies.append(pltpu.make_async_remote_copy(
                src_ref=ins[a], dst_ref=outs[a].at[me], send_sem=send_sems.at[a, k], recv_sem=recv_sems.at[a, k],
                device_id=(_flip(x, dx), _flip(y, dy), _flip(c, dc)), device_id_type=MESH_ID))
    return copies


def _gather8_sems(n):
    return [pltpu.SemaphoreType.DMA((n, 7)), pltpu.SemaphoreType.DMA((n, 7)), pltpu.SemaphoreType.DMA((n,))]


def _gather_dev8(arrs, name):
    n = len(arrs)

    def body(*refs):
        copies = _gather8_copies(refs[:n], refs[n:2 * n], *refs[2 * n:])
        for cp in copies:
            cp.start()
        for cp in copies:
            cp.wait()

    return pl.pallas_call(
        body, name=name, in_specs=[_ANY] * n, out_specs=[_ANY] * n,
        out_shape=[_sds((N_DEV,) + a.shape, a.dtype) for a in arrs], scratch_shapes=_gather8_sems(n),
    )(*arrs)


class _Exchange:
    def __init__(self, arrs, out_shapes, n_sems, phases):
        self.arrs, self.out_shapes, self.n_sems, self._phases = list(arrs), list(out_shapes), n_sems, phases

    @property
    def n(self):
        return len(self.arrs)

    def sem_shapes(self):
        return [pltpu.SemaphoreType.DMA((self.n, self.n_sems)), pltpu.SemaphoreType.DMA((self.n, self.n_sems)),
                pltpu.SemaphoreType.DMA((self.n,))]

    def phases(self, ins, outs, sems):
        return self._phases(ins, outs, *sems)

    def run(self, name):
        n = self.n

        def body(*refs):
            start, mid, end = self.phases(refs[:n], refs[n:2 * n], refs[2 * n:])
            start()
            mid()
            end()

        return pl.pallas_call(body, name=name, in_specs=[_ANY] * n, out_specs=[_ANY] * n, out_shape=self.out_shapes,
                              scratch_shapes=self.sem_shapes())(*self.arrs)

def _gather_halves_phases(ins, outs, send_sems, recv_sems, loc_sems):
    n = len(ins)
    x, y, c = _my_place()
    chip = 2 * x + y
    sibling = (x, y, 1 - c)
    peers = [(_flip(x, dx), _flip(y, dy)) for dx, dy in CHIP_FLIPS]

    def remote(src, p, half, a, k, to):
        return pltpu.make_async_remote_copy(src_ref=src, dst_ref=outs[a].at[p, half], send_sem=send_sems.at[a, k],
                                            recv_sem=recv_sems.at[a, k], device_id=to, device_id_type=MESH_ID)

    def local(a):
        return pltpu.make_async_copy(ins[a], outs[a].at[chip], loc_sems.at[a])

    def first(a, k):
        return remote(ins[a].at[c], chip, c, a, k, (*peers[k], c))

    def passed(a, k):
        p = 2 * peers[k][0] + peers[k][1]
        return remote(outs[a].at[p, c], p, c, a, 3 + k, sibling)

    def start():
        for a in range(n):
            local(a).start()
            for k in range(3):
                first(a, k).start()

    def mid():
        for a in range(n):
            for k in range(3):
                p = 2 * peers[k][0] + peers[k][1]
                remote(outs[a].at[p, c], p, c, a, k, sibling).wait_recv()
                passed(a, k).start()

    def end():
        for a in range(n):
            for k in range(3):
                p = 2 * peers[k][0] + peers[k][1]
                remote(outs[a].at[p, 1 - c], p, 1 - c, a, 3 + k, sibling).wait_recv()
        for a in range(n):
            for k in range(3):
                first(a, k).wait_send()
                passed(a, k).wait_send()
            local(a).wait()

    return start, mid, end


def _gather_chip4_halves(arrs):
    return _Exchange(arrs, [_sds((N_CHIPS,) + a.shape, a.dtype) for a in arrs], 6, _gather_halves_phases)


def _reduce_phases(n_whole, ins, outs, send_sems, recv_sems, loc_sems):
    n = len(ins)
    x, y, c = _my_place()
    chip = 2 * x + y
    sibling = (x, y, 1 - c)
    peers = [(_flip(x, dx), _flip(y, dy)) for dx, dy in CHIP_FLIPS]

    def remote(src, slot, a, k, to):
        return pltpu.make_async_remote_copy(src_ref=src, dst_ref=outs[a].at[slot], send_sem=send_sems.at[a, k],
                                            recv_sem=recv_sems.at[a, k], device_id=to, device_id_type=MESH_ID)

    def block(a, p):
        return ins[a] if a >= n - n_whole else ins[a].at[p]

    def local(a):
        return pltpu.make_async_copy(block(a, chip), outs[a].at[2 * chip + c], loc_sems.at[a])

    def own(a):
        return remote(block(a, chip), 2 * chip + c, a, 0, sibling)

    def first(a, k):
        return remote(block(a, 2 * peers[k][0] + peers[k][1]), 2 * chip + c, a, 1 + k, (*peers[k], c))

    def passed(a, k):
        slot = 2 * (2 * peers[k][0] + peers[k][1]) + c
        return remote(outs[a].at[slot], slot, a, 4 + k, sibling)

    def start():
        for a in range(n):
            local(a).start()
            own(a).start()
            for k in range(3):
                first(a, k).start()

    def mid():
        for a in range(n):
            for k in range(3):
                slot = 2 * (2 * peers[k][0] + peers[k][1]) + c
                remote(outs[a].at[slot], slot, a, 1 + k, sibling).wait_recv()
                passed(a, k).start()

    def end():
        for a in range(n):
            remote(outs[a].at[2 * chip + 1 - c], 2 * chip + 1 - c, a, 0, sibling).wait_recv()
            for k in range(3):
                slot = 2 * (2 * peers[k][0] + peers[k][1]) + 1 - c
                remote(outs[a].at[slot], slot, a, 4 + k, sibling).wait_recv()
        for a in range(n):
            own(a).wait_send()
            for k in range(3):
                first(a, k).wait_send()
                passed(a, k).wait_send()
            local(a).wait()

    return start, mid, end


def _reduce_exchange(arrs, whole=()):
    shapes = [_sds((N_DEV,) + a.shape[1:], a.dtype) for a in arrs] + [_sds((N_DEV,) + a.shape, a.dtype) for a in whole]
    return _Exchange(list(arrs) + list(whole), shapes, 7, functools.partial(_reduce_phases, len(whole)))


def _even_in_layout(w):
    return jnp.concatenate([w[:, 0:1696], jnp.zeros((w.shape[0], 96), w.dtype), w[:, 1696:2208]], axis=1)


def _even_in_unlayout(g):
    return jnp.concatenate([g[:, 0:1696], g[:, 1792:2304]], axis=1)


def _uq_layout(w):
    per = B_NOPE + B_ROPE
    pad = jnp.zeros((w.shape[0], LANES - B_ROPE), w.dtype)
    nope = [w[:, per * h:per * h + B_NOPE] for h in range(B_HEADS)]
    rope = [jnp.concatenate([w[:, per * h + B_NOPE:per * (h + 1)], pad], axis=1) for h in range(B_HEADS)]
    return jnp.concatenate(nope + rope, axis=1)


def _uq_unlayout(g):
    parts = []
    for h in range(B_HEADS):
        parts += [g[:, B_NOPE * h:B_NOPE * (h + 1)], g[:, 512 + LANES * h:512 + LANES * h + B_ROPE]]
    return jnp.concatenate(parts, axis=1)


def _block_diag(blocks):
    rows = []
    for h, blk in enumerate(blocks):
        r, cdim = blk.shape
        n = len(blocks)
        rows.append(jnp.concatenate([jnp.zeros((r, cdim * h), blk.dtype), blk, jnp.zeros((r, cdim * (n - 1 - h)), blk.dtype)],
                                    axis=1))
    return jnp.concatenate(rows, axis=0)


def _uk_layout(w):
    return _block_diag([w[:, h, :].T for h in range(B_HEADS)])


def _uk_unlayout(g):
    return jnp.stack([g[B_NOPE * h:B_NOPE * (h + 1), LANES * h:LANES * (h + 1)].T for h in range(B_HEADS)], axis=1)


def _uv_layout(w):
    return _block_diag([w[:, h, :] for h in range(B_HEADS)])


def _uv_unlayout(g):
    return jnp.stack([g[LANES * h:LANES * (h + 1), B_V * h:B_V * (h + 1)] for h in range(B_HEADS)], axis=1)


def _rope_tables(S):
    inv = ROPE_THETA ** (-jnp.arange(0, 32, 2, dtype=F32) / 32)
    tok = jnp.arange(S)

    def tab(pos):
        ang = pos.astype(F32)[:, None] * inv[None, :]
        cos, sin = jnp.cos(ang), jnp.sin(ang)
        return jnp.concatenate([cos, cos], axis=1), jnp.concatenate([-sin, sin], axis=1)

    cr, sr = tab(tok // GRID_W)
    cc, sc = tab(tok % GRID_W)
    ct, st = tab(tok)
    return (jnp.tile(jnp.concatenate([cr, cc], axis=1), (1, 2)), jnp.tile(jnp.concatenate([sr, sc], axis=1), (1, 2)),
            jnp.tile(ct, (1, 4)), jnp.tile(st, (1, 4)))


A_TQ, A_TK, A_SUB = 512, 4096, 512
A_FWD_SUB = 1024
B_TQ, B_TK, B_SUB = 128, 4096, 1024
B_BWD_TK, B_BWD_SUB = 4096, 512
C_T = 256
C_BLOCKS_PER_STEP = 8
KV_SHARE = 2


def _local_step(x0, tgt, mod, norm_w, wie, wuq, wuk, wuv, late_shards, a_q_norm, a_k_norm, q_lora_norm, kv_lora_norm,
                c_sink, final_norm):
    S = x0.shape[0]
    mod3 = mod.reshape(2, 3, D_MODEL)
    ca, sa, ct, st = _rope_tables(S)
    lane_seg = np.arange(LANES) // HEAD_DIM
    seg = jnp.asarray((lane_seg[:, None] == lane_seg[None, :]).astype(np.float32)).astype(BF16)
    qn = jnp.tile(a_q_norm.reshape(1, HEAD_DIM), (1, 2))
    kn = jnp.tile(a_k_norm.reshape(1, HEAD_DIM), (1, 2))
    qln, kvln = q_lora_norm.reshape(1, B_Q_LORA), kv_lora_norm.reshape(1, B_KV_LORA)
    nw0, nw1 = norm_w[0:1], norm_w[1:2]
    gate0, gate1 = mod3[0, 2:3], mod3[1, 2:3]
    a_tq, a_tk, b_tq, b_tk, bb_tk, c_t = min(A_TQ, S), min(A_TK, S), min(B_TQ, S), min(B_TK, S), min(B_BWD_TK, S), min(C_T, S)
    a_sub, b_sub, bb_sub = min(A_SUB, a_tk), min(B_SUB, b_tk), min(B_BWD_SUB, bb_tk)

    h0, proj_e, qa, ka, va, qcat, kcat, ka_t, va_t, kcat_t = _even_pre_fwd(x0, mod3[0], nw0, wie, qn, kn, seg, ca, sa, ct, st,
                                                                           qln, kvln, wuq, wuk)
    oa, lse_a, woe_g, wio_g, woo_g = _pp_fwd(qa, ka, va_t, kdiv=KV_SHARE, tq=a_tq, tk=a_tk, sub=min(A_FWD_SUB, a_tk), name="attn_a_fwd",
                                             side=_gather_chip4_halves(late_shards))
    woe = woe_g.reshape(D_MODEL, D_MODEL)
    wio = wio_g.reshape(N_CHIPS, D_MODEL, ODD_IN // N_CHIPS)
    woo = woo_g.reshape(D_MODEL, D_MODEL)
    olat, lse_b = _mla_fwd(qcat, kcat, kcat_t, tq=b_tq, tk=b_tk, sub=b_sub)
    y0, x1 = _even_post_fwd(oa, olat, proj_e, x0, gate0, wuv, woe)
    h1, proj_o, qc, kc, vc, kc_t, vc_t = _odd_pre_fwd(x1, mod3[1], nw1, wio)
    slopes = 2.0 ** (-8.0 * jnp.arange(1, C_HEADS + 1, dtype=F32) / C_HEADS)
    slope_rows = jnp.repeat(slopes.reshape(C_HEADS // 2, 2), c_t, axis=1)[:, None, :]
    sink_rows = jnp.repeat(c_sink.reshape(C_HEADS // 2, 2), c_t, axis=1)[:, None, :]
    win_dist = _win_dist_table(S, c_t)
    oc, lse_c = _win_fwd(qc, kc, vc_t, win_dist, slope_rows, sink_rows, kdiv=KV_SHARE, tq=c_t, nbs=C_BLOCKS_PER_STEP,
                         name="attn_c_fwd")
    doc, dgc, dx2, dwoo, st_f = _odd_post(oc, proj_o, x1, gate1, woo, final_norm.reshape(1, D_MODEL), tgt)
    dqc, dkc, dvc, dsink_raw = _win_bwd(qc, kc, kc_t, vc, oc, doc, lse_c, win_dist, slope_rows, sink_rows, kdiv=KV_SHARE, tq=c_t,
                                        nbs=C_BLOCKS_PER_STEP, name="attn_c_bwd")
    dx1, dwio, st_1 = _odd_pre_bwd(dqc, dkc, dvc, dgc, h1, x1, dx2, mod3[1], nw1, wio)
    doa, dga, dgb, dolat, dwoe, dwuv, st_e = _even_post_bwd(dx1, y0, oa, olat, proj_e, gate0, wuv, woe)
    late_grads = _reduce_exchange([dwoe.reshape(N_CHIPS, D_MODEL // N_CHIPS, D_MODEL), dwio,
                                   dwoo.reshape(N_CHIPS, D_MODEL // N_CHIPS, D_MODEL)])
    dqa, dka, dva, p_woe, p_wio, p_woo = _pp_bwd(qa, ka, ka_t, va, oa, doa, lse_a, kdiv=KV_SHARE, tq=a_tq, tk=a_tk, sub=a_sub,
                                                 name="attn_a_bwd", side=late_grads)
    dqcat, dkcat = _mla_bwd(qcat, kcat, kcat_t, olat, dolat, lse_b, tq=b_tq, tk=bb_tk, sub=bb_sub)
    dx0, dwie, dwuq, dwuk, st_0, nst = _even_pre_bwd(x0, h0, proj_e, dqa, dka, dva, dga, dgb, dqcat, dkcat, dx1, mod3[0], nw0,
                                                     wie, qn, kn, seg, ca, sa, ct, st, qln, kvln, wuq, wuk)
    dsink_pairs = jnp.stack([dsink_raw[:, 0, 0], dsink_raw[:, 1, 0]], axis=1).reshape(C_HEADS)
    return dict(
        loss_row=st_f[2:3], dx=dx0,
        dmod=jnp.stack([jnp.concatenate([st_0[0], st_0[1], st_e[0]]), jnp.concatenate([st_1[0], st_1[1], st_f[1]])]),
        norm_w=jnp.stack([st_0[2], st_1[2]]), final_norm=st_f[0],
        a_q_norm=nst[0:1, 0:HEAD_DIM], a_k_norm=nst[1:2, 0:HEAD_DIM], b_q_lora_norm=nst[2:3, :], b_kv_lora_norm=nst[3:4, 0:LANES],
        c_sink=dsink_pairs.reshape(1, C_HEADS),
        even_w_in=dwie, b_w_uq=dwuq, b_w_uk=dwuk, b_w_uv=dwuv, even_w_out=p_woe, odd_w_in=p_wio, odd_w_out=p_woo)


WEIGHT_NAMES = ("norm_w", "ada_w", "ada_b", "even_w_in", "a_q_norm", "a_k_norm", "b_q_lora_norm", "b_kv_lora_norm", "b_w_uq",
                "b_w_uk", "b_w_uv", "even_w_out", "odd_w_in", "c_sink", "odd_w_out", "final_norm")


def _cols_to_chips(g):
    r, n4 = g.shape
    return jnp.transpose(g.reshape(r, N_CHIPS, n4 // N_CHIPS), (1, 0, 2))


def _chips_to_cols(g):
    p, r, n = g.shape
    return jnp.transpose(g, (1, 0, 2)).reshape(r, p * n)


def kernel(x, c, norm_w, ada_w, ada_b, even_w_in, a_q_norm, a_k_norm, b_q_lora_norm, b_kv_lora_norm, b_w_uq, b_w_uk, b_w_uv, even_w_out, odd_w_in, c_sink, odd_w_out, final_norm, loss_target, m_norm_w, m_ada_w, m_ada_b, m_even_w_in, m_a_q_norm, m_a_k_norm, m_b_q_lora_norm, m_b_kv_lora_norm, m_b_w_uq, m_b_w_uk, m_b_w_uv, m_even_w_out, m_odd_w_in, m_c_sink, m_odd_w_out, m_final_norm, v_norm_w, v_ada_w, v_ada_b, v_even_w_in, v_a_q_norm, v_a_k_norm, v_b_q_lora_norm, v_b_kv_lora_norm, v_b_w_uq, v_b_w_uk, v_b_w_uv, v_even_w_out, v_odd_w_in, v_c_sink, v_odd_w_out, v_final_norm):
    given = dict(locals())
    xi, yi, ci = _my_place()
    chip = 2 * xi + yi
    dev = 2 * chip + ci
    n_ada = ada_w.shape[2]

    (c_all,) = _gather_dev8([c], "gather_c")
    c_all = c_all.reshape(N_DEV, D_MODEL)
    bias = lax.dynamic_slice_in_dim(ada_b, chip * n_ada, n_ada, axis=1).reshape(2, 1, n_ada)
    mod_cols = _ada_fwd(c_all, ada_w, bias)
    def halves(w):
        return w.astype(BF16).reshape((2, w.shape[0] // 2) + w.shape[1:])

    mod_all, wie_g, wuq_g = _gather_chip4_halves([mod_cols, halves(even_w_in[0]), halves(b_w_uq[0])]).run("gather_weights")
    wie_g = wie_g.reshape(N_CHIPS, D_MODEL, EVEN_IN // N_CHIPS)
    wuq_g = wuq_g.reshape(N_CHIPS, B_Q_LORA, -1)
    mod = jnp.transpose(lax.dynamic_index_in_dim(mod_all, dev, axis=2, keepdims=False), (1, 0, 2)).reshape(2, 3 * D_MODEL)

    res = _local_step(
        x[0], loss_target[0], mod, norm_w,
        _even_in_layout(_chips_to_cols(wie_g)), _uq_layout(_chips_to_cols(wuq_g)), _uk_layout(b_w_uk[0].astype(BF16)),
        _uv_layout(b_w_uv[0].astype(BF16)), [halves(even_w_out[0]), halves(odd_w_in[0]), halves(odd_w_out[0])],
        a_q_norm, a_k_norm, b_q_lora_norm, b_kv_lora_norm, c_sink, final_norm)

    latent = jnp.stack([_uk_unlayout(res["b_w_uk"]).reshape(B_KV_LORA, 512),
                        _uv_unlayout(res["b_w_uv"]).reshape(B_KV_LORA, 512)]).astype(BF16)
    p_wie, p_wuq, small_all, latent_all = _reduce_exchange(
        [_cols_to_chips(_even_in_unlayout(res["even_w_in"])), _cols_to_chips(_uq_unlayout(res["b_w_uq"].astype(BF16)))],
        whole=[_pack_small(res), latent]).run("reduce_exchange")
    shard_parts = dict(even_w_in=p_wie, b_w_uq=p_wuq, **{k: res[k] for k in ("even_w_out", "odd_w_in", "odd_w_out")})
    dmod_all = small_all[:, 0:6, :].reshape(N_DEV, 2, 3 * D_MODEL)
    dmod_cols = jnp.transpose(lax.dynamic_slice_in_dim(dmod_all, chip * n_ada, n_ada, axis=2), (1, 0, 2))
    parts = dict(shard_parts)
    parts["ada_w"] = _ada_bwd(c_all.T, dmod_cols).reshape(1, 2 * D_MODEL, n_ada)
    parts["b_w_uk"], parts["b_w_uv"] = latent_all[:, 0], latent_all[:, 1]

    def as2d(a):
        return a.reshape((-1, a.shape[-1]) if a.ndim > 1 else (1, a.shape[0]))

    results = {}
    small_outs = _adam_small(small_all, *[[as2d(given[pre + k]) for k in SMALL_WEIGHTS] for pre in ("", "m_", "v_")])
    for idx, k in enumerate(SMALL_WEIGHTS):
        results[k] = small_outs[4 * idx:4 * idx + 4]
    for k, p in parts.items():
        shape2 = (p.shape[-2], p.shape[-1])
        results[k] = _adam(p, given[k].reshape(shape2), given["m_" + k].reshape(shape2), given["v_" + k].reshape(shape2),
                           "adam_" + k)
    by_kind = [[results[k][t].reshape(given[k].shape) for k in WEIGHT_NAMES] for t in range(4)]
    return (small_outs[-1][0, 0], res["dx"][None], *by_kind[0], *by_kind[1], *by_kind[2], *by_kind[3])
```

```python
import functools

import numpy as np
import jax
import jax.numpy as jnp
from jax import lax
from jax.experimental import pallas as pl
from jax.experimental.pallas import tpu as pltpu

F32 = jnp.float32
BF16 = jnp.bfloat16
HIGHEST = lax.Precision.HIGHEST
MESH_ID = pl.DeviceIdType.MESH

D_MODEL = 1024
HEAD_DIM = 64
GRID_W = 64
EPS = 1e-6
ROPE_THETA = 10000.0
B_HEADS, B_NOPE, B_ROPE, B_V = 8, 64, 32, 64
B_Q_LORA, B_KV_LORA = 256, 128
C_HEADS = 16
WINDOW = 128
EVEN_IN, ODD_IN = 2208, 2560
EVEN_P = 2304
N_CHIPS, N_DEV = 4, 8
LANES = 128
NEG = -1e30
VMEM_LIMIT = 60 * 1024 * 1024

ADAM_LR, ADAM_B1, ADAM_B2, ADAM_EPS, ADAM_WD, ADAM_STEP = 0.001, 0.9, 0.999, 1e-08, 0.01, 10

ROW_TILE = 512
IN_PROJ_ROW_TILE = 256


def _dot(a, b):
    return lax.dot_general(a, b, (((1,), (0,)), ((), ())), preferred_element_type=F32)


def _dot_nt(a, b):
    return lax.dot_general(a, b, (((1,), (1,)), ((), ())), preferred_element_type=F32)


def _dot_tn(a, b):
    return lax.dot_general(a, b, (((0,), (0,)), ((), ())), preferred_element_type=F32)


def _dot_f32(a, b):
    return lax.dot_general(a, b, (((1,), (0,)), ((), ())), precision=HIGHEST, preferred_element_type=F32)


def _sigmoid(x):
    return 1.0 / (1.0 + jnp.exp(-x))


def _silu_and_grad(g):
    s = _sigmoid(g)
    return g * s, s * (1.0 + g * (1.0 - s))


def _lane_iota():
    return lax.broadcasted_iota(jnp.int32, (1, LANES), 1)


def _partner(x, lane):
    return jnp.where((lane % 32) < 16, pltpu.roll(x, LANES - 16, 1), pltpu.roll(x, 16, 1))


def _rot(x, cos, sin_signed, lane):
    return x * cos + _partner(x, lane) * sin_signed


def _rot_bwd(dy, cos, sin_signed, lane):
    return dy * cos + _partner(dy * sin_signed, lane)


def _rms(x):
    return lax.rsqrt(jnp.mean(x * x, axis=-1, keepdims=True) + EPS)


def _rms_bwd(x, r, g):
    return r * g - x * (r * r * r) * jnp.mean(x * g, axis=-1, keepdims=True)


def _seg_mean(v, seg_ones):
    hi = v.astype(BF16)
    lo = (v - hi.astype(F32)).astype(BF16)
    return (_dot(hi, seg_ones) + _dot(lo, seg_ones)) * (1.0 / HEAD_DIM)


def _dup_heads(x, lane):
    swapped = pltpu.roll(x, HEAD_DIM, 1)
    lo = lane < HEAD_DIM
    return jnp.concatenate([jnp.where(lo, x, swapped), jnp.where(lo, swapped, x)], axis=1)


def _fold_heads(x2, lane):
    a, b = x2[:, 0:LANES], x2[:, LANES:2 * LANES]
    return jnp.where(lane < HEAD_DIM, a + pltpu.roll(a, HEAD_DIM, 1), b + pltpu.roll(b, HEAD_DIM, 1))


def _row_spec(ts, cols):
    return pl.BlockSpec((ts, cols), lambda i: (i, 0))


def _full_spec(shape, single=True):
    nd = len(shape)
    if single:
        return pl.BlockSpec(shape, lambda i: (0,) * nd, pipeline_mode=pl.Buffered(1))
    return pl.BlockSpec(shape, lambda i: (0,) * nd)


def _sds(shape, dtype):
    return jax.ShapeDtypeStruct(shape, dtype)


def _params(sem):
    return pltpu.CompilerParams(dimension_semantics=sem, vmem_limit_bytes=VMEM_LIMIT)


def _even_pre_fwd(x, mod, nw, wie, qn, kn, seg, ca, sa, ct, st, qln, kvln, wuq, wuk):
    S = x.shape[0]
    ts = min(IN_PROJ_ROW_TILE, S)

    def body(x_ref, mod_ref, nw_ref, wie_ref, qn_ref, kn_ref, seg_ref, ca_ref, sa_ref, ct_ref, st_ref, qln_ref,
             kvln_ref, wuq_ref, wuk_ref, h_ref, proj_ref, qa_ref, ka_ref, va_ref, qcat_ref, kcat_ref, kat_ref, vat_ref, kcatt_ref):
        xv = x_ref[...]
        h = (xv * _rms(xv) * nw_ref[...]) * (1.0 + mod_ref[1:2, :]) + mod_ref[0:1, :]
        hb = h.astype(BF16)
        h_ref[...] = hb
        proj = _dot(hb, wie_ref[...])
        proj_ref[...] = proj
        lane = _lane_iota()
        ca_v, sa_v, ct_v, st_v = ca_ref[...], sa_ref[...], ct_ref[...], st_ref[...]
        seg_v = seg_ref[...]
        for cb in range(4):
            xc = proj[:, LANES * cb:LANES * (cb + 1)]
            r = lax.rsqrt(_seg_mean(xc * xc, seg_v) + EPS)
            y = _rot(xc * r * qn_ref[...], ca_v, sa_v, lane)
            qa_ref[:, LANES * cb:LANES * (cb + 1)] = (y * 0.125).astype(BF16)
        kc = proj[:, 512:640]
        r = lax.rsqrt(_seg_mean(kc * kc, seg_v) + EPS)
        ka_v = _dup_heads(_rot(kc * r * kn_ref[...], ca_v, sa_v, lane), lane)
        ka_ref[...] = ka_v.astype(BF16)
        kat_ref[...] = ka_v.T.astype(BF16)
        va_v = _dup_heads(proj[:, 640:768], lane)
        va_ref[...] = va_v.astype(BF16)
        vat_ref[...] = va_v.T.astype(BF16)
        cq = proj[:, 1280:1536]
        cqn = (cq * _rms(cq) * qln_ref[...]).astype(BF16)
        ckv = proj[:, 1536:1664]
        ckvn = ckv * _rms(ckv) * kvln_ref[...]
        qb = _dot(cqn, wuq_ref[...])
        qlat = _dot(qb[:, 0:512].astype(BF16), wuk_ref[...])
        for hh in range(B_HEADS):
            qcat_ref[hh, :, 0:LANES] = qlat[:, LANES * hh:LANES * (hh + 1)].astype(BF16)
            qr = _rot(qb[:, 512 + LANES * hh:512 + LANES * (hh + 1)], ct_v, st_v, lane)
            qcat_ref[hh, :, LANES:2 * LANES] = qr.astype(BF16)
        kr = _rot(proj[:, 1664:1792], ct_v, st_v, lane)
        kcat_ref[:, 0:LANES] = ckvn.astype(BF16)
        kcat_ref[:, LANES:2 * LANES] = kr.astype(BF16)
        kcatt_ref[0:LANES, :] = ckvn.T.astype(BF16)
        kcatt_ref[LANES:2 * LANES, :] = kr.T.astype(BF16)

    col_spec = lambda rows: pl.BlockSpec((rows, ts), lambda i: (0, i))
    return pl.pallas_call(
        body, name="even_pre_fwd", grid=(S // ts,),
        in_specs=[_row_spec(ts, D_MODEL), _full_spec((3, D_MODEL)), _full_spec((1, D_MODEL)), _full_spec((D_MODEL, EVEN_P)),
                  _full_spec((1, LANES)), _full_spec((1, LANES)), _full_spec((LANES, LANES)),
                  _row_spec(ts, LANES), _row_spec(ts, LANES), _row_spec(ts, LANES), _row_spec(ts, LANES),
                  _full_spec((1, B_Q_LORA)), _full_spec((1, B_KV_LORA)), _full_spec((B_Q_LORA, 1536)), _full_spec((512, 1024))],
        out_specs=[_row_spec(ts, D_MODEL), _row_spec(ts, EVEN_P), _row_spec(ts, 512), _row_spec(ts, 2 * LANES), _row_spec(ts, 2 * LANES),
                   pl.BlockSpec((B_HEADS, ts, 2 * LANES), lambda i: (0, i, 0)), _row_spec(ts, 2 * LANES),
                   col_spec(2 * LANES), col_spec(2 * LANES), col_spec(2 * LANES)],
        out_shape=[_sds((S, D_MODEL), BF16), _sds((S, EVEN_P), F32), _sds((S, 512), BF16), _sds((S, 2 * LANES), BF16),
                   _sds((S, 2 * LANES), BF16), _sds((B_HEADS, S, 2 * LANES), BF16), _sds((S, 2 * LANES), BF16),
                   _sds((2 * LANES, S), BF16), _sds((2 * LANES, S), BF16), _sds((2 * LANES, S), BF16)],
        compiler_params=_params(("arbitrary",)),
    )(x, mod, nw, wie, qn, kn, seg, ca, sa, ct, st, qln, kvln, wuq, wuk)


MLA_SCALE = (B_NOPE + B_ROPE) ** -0.5
LOG2E = 1.4426950408889634

def _row_lo():
    return lax.broadcasted_iota(jnp.int32, (LANES, 1), 0) < HEAD_DIM


def _stack_cols(vT, rlo):
    zero = jnp.zeros_like(vT)
    return jnp.concatenate([jnp.where(rlo, vT, zero), jnp.where(rlo, zero, vT)], axis=1)


def _stack_rows(v, lo):
    zero = jnp.zeros_like(v)
    return jnp.concatenate([jnp.where(lo, v, zero), jnp.where(lo, zero, v)], axis=0)


def _pick_halves_T(xT, rlo, t):
    return jnp.where(rlo, xT[:, 0:t], xT[:, t:2 * t]).T


def _side_split(refs, n_in, n_out, n_scratch, side):
    ns = side.n if side is not None else 0
    cuts = np.cumsum([0, n_in, ns, n_out, ns, n_scratch])
    return [refs[a:b] for a, b in zip(cuts[:-1], cuts[1:])] + [refs[cuts[-1]:]]


def _side_hooks(side, side_ins, side_outs, side_sems, step, total):
    if side is None:
        return lambda: None
    start, mid, end = side.phases(side_ins, side_outs, side_sems)
    pl.when(step == 0)(start)
    pl.when(step == total // 2)(mid)
    return lambda: pl.when(step == total - 1)(end)


def _side_specs(side):
    if side is None:
        return [], [], [], [], []
    return list(side.arrs), [_ANY] * side.n, [_ANY] * side.n, list(side.out_shapes), side.sem_shapes()


def _pp_fwd(q, k, vT, *, kdiv, tq, tk, sub, name, side=None):
    S = k.shape[0]; nb = q.shape[1] // LANES; nq = S // tq; nkv = S // tk; nsub = tk // sub

    def body(*refs):
        (q_ref, k_ref, vT_ref), side_ins, (o_ref, lse_ref), side_outs, (qs, m_s, l_s, acc), side_sems = _side_split(refs, 3, 2, 4, side)
        j = pl.program_id(2)
        rlo = _row_lo()
        step = (pl.program_id(0) * nq + pl.program_id(1)) * nkv + j
        side_end = _side_hooks(side, side_ins, side_outs, side_sems, step, nb * nq * nkv)

        @pl.when(j == 0)
        def _():
            qs[...] = _stack_cols(q_ref[...].astype(F32).T, rlo).astype(BF16)
            m_s[...] = jnp.full((1, 2 * tq), NEG, F32)
            l_s[...] = jnp.zeros((1, 2 * tq), F32)
            acc[...] = jnp.zeros((LANES, 2 * tq), F32)

        qsv = qs[...]
        m, l, a = m_s[...], l_s[...], acc[...]
        s_cur = _dot(k_ref[0:sub, :], qsv)
        for t in range(nsub):
            if t + 1 < nsub:
                s_next = _dot(k_ref[sub * (t + 1):sub * (t + 2), :], qsv)
            m_new = jnp.maximum(m, jnp.max(s_cur, axis=0, keepdims=True))
            alpha = jnp.exp(m - m_new)
            p = jnp.exp(s_cur - m_new)
            l = alpha * l + jnp.sum(p, axis=0, keepdims=True)
            a = alpha * a + _dot(vT_ref[:, sub * t:sub * (t + 1)], p.astype(BF16))
            m = m_new
            if t + 1 < nsub:
                s_cur = s_next
        m_s[...], l_s[...], acc[...] = m, l, a

        @pl.when(j == nkv - 1)
        def _():
            l_f = l_s[...]
            o_ref[...] = _pick_halves_T(acc[...] / l_f, rlo, tq).astype(BF16)
            lse_ref[0, 0] = m_s[...] + jnp.log(l_f)

        side_end()

    s_args, s_in, s_out, s_shapes, s_sems = _side_specs(side)
    return pl.pallas_call(
        body, name=name, grid=(nb, nq, nkv),
        in_specs=[pl.BlockSpec((tq, LANES), lambda b, i, j: (i, b)), pl.BlockSpec((tk, LANES), lambda b, i, j: (j, b // kdiv)),
                  pl.BlockSpec((LANES, tk), lambda b, i, j: (b // kdiv, j))] + s_in,
        out_specs=[pl.BlockSpec((tq, LANES), lambda b, i, j: (i, b)),
                   pl.BlockSpec((1, 1, 1, 2 * tq), lambda b, i, j: (b, i, 0, 0))] + s_out,
        out_shape=[_sds((S, nb * LANES), BF16), _sds((nb, nq, 1, 2 * tq), F32)] + s_shapes,
        scratch_shapes=[pltpu.VMEM((LANES, 2 * tq), BF16), pltpu.VMEM((1, 2 * tq), F32), pltpu.VMEM((1, 2 * tq), F32),
                        pltpu.VMEM((LANES, 2 * tq), F32)] + s_sems,
        compiler_params=_params(("arbitrary",) * 3))(q, k, vT, *s_args)


def _pp_bwd(q, k, kT, v, o, do, lse, *, kdiv, tq, tk, sub, name, side=None):
    S = k.shape[0]; nb = q.shape[1] // LANES; nkb = k.shape[1] // LANES; nq = S // tq; nkv = S // tk; nsub = tk // sub

    def body(*refs):
        ((q_ref, k_ref, kT_ref, v_ref, o_ref, do_ref, lse_ref), side_ins, (dq_ref, dk_ref, dv_ref), side_outs,
         (qsT, qs, dosT, dos, delta_s, dq_acc), side_sems) = _side_split(refs, 7, 3, 6, side)
        b, i, j = pl.program_id(0), pl.program_id(1), pl.program_id(2)
        rlo = _row_lo()
        lo = lax.broadcasted_iota(jnp.int32, (1, LANES), 1) < HEAD_DIM
        side_end = _side_hooks(side, side_ins, side_outs, side_sems, (b * nq + i) * nkv + j, nb * nq * nkv)

        @pl.when((b % kdiv == 0) & (i == 0) & (j == 0))
        def _():
            dk_ref[...] = jnp.zeros((S, LANES), F32)
            dv_ref[...] = jnp.zeros((S, LANES), F32)

        @pl.when(j == 0)
        def _():
            qv = q_ref[...]
            qs[...] = _stack_rows(qv, lo)
            qsT[...] = _stack_cols(qv.astype(F32).T, rlo).astype(BF16)
            dov = do_ref[...].astype(F32)
            dos[...] = _stack_rows(dov.astype(BF16), lo)
            dosT[...] = _stack_cols(dov.T, rlo).astype(BF16)
            prodT = (dov * o_ref[...].astype(F32)).T
            delta_s[...] = jnp.concatenate([jnp.sum(jnp.where(rlo, prodT, 0.0), axis=0, keepdims=True),
                                            jnp.sum(jnp.where(rlo, 0.0, prodT), axis=0, keepdims=True)], axis=1)
            dq_acc[...] = jnp.zeros((LANES, 2 * tq), F32)

        qsTv, dosTv, qsv, dosv = qsT[...], dosT[...], qs[...], dos[...]
        lse_v, delta_v = lse_ref[0, 0], delta_s[...]
        dqa = dq_acc[...]
        s_cur = _dot(k_ref[0:sub, :], qsTv)
        dp_cur = _dot(v_ref[0:sub, :], dosTv)
        for t in range(nsub):
            if t + 1 < nsub:
                s_next = _dot(k_ref[sub * (t + 1):sub * (t + 2), :], qsTv)
                dp_next = _dot(v_ref[sub * (t + 1):sub * (t + 2), :], dosTv)
            p = jnp.exp(s_cur - lse_v)
            ds = (p * (dp_cur - delta_v)).astype(BF16)
            rows = pl.ds(pl.multiple_of(j * tk + sub * t, sub), sub)
            dv_ref[rows, :] += _dot(p.astype(BF16), dosv)
            dk_ref[rows, :] += _dot(ds, qsv)
            dqa = dqa + _dot(kT_ref[:, sub * t:sub * (t + 1)], ds)
            if t + 1 < nsub:
                s_cur, dp_cur = s_next, dp_next
        dq_acc[...] = dqa

        @pl.when(j == nkv - 1)
        def _():
            dq_ref[...] = _pick_halves_T(dq_acc[...], rlo, tq)

        side_end()

    qmap = lambda b, i, j: (i, b)
    kmap = lambda b, i, j: (j, b // kdiv)
    res = lambda b, i, j: (0, b // kdiv)
    s_args, s_in, s_out, s_shapes, s_sems = _side_specs(side)
    return pl.pallas_call(
        body, name=name, grid=(nb, nq, nkv),
        in_specs=[pl.BlockSpec((tq, LANES), qmap), pl.BlockSpec((tk, LANES), kmap), pl.BlockSpec((LANES, tk), lambda b, i, j: (b // kdiv, j)),
                  pl.BlockSpec((tk, LANES), kmap), pl.BlockSpec((tq, LANES), qmap), pl.BlockSpec((tq, LANES), qmap),
                  pl.BlockSpec((1, 1, 1, 2 * tq), lambda b, i, j: (b, i, 0, 0))] + s_in,
        out_specs=[pl.BlockSpec((tq, LANES), qmap), pl.BlockSpec((S, LANES), res), pl.BlockSpec((S, LANES), res)] + s_out,
        out_shape=[_sds((S, nb * LANES), F32), _sds((S, nkb * LANES), F32), _sds((S, nkb * LANES), F32)] + s_shapes,
        scratch_shapes=[pltpu.VMEM((LANES, 2 * tq), BF16), pltpu.VMEM((2 * tq, LANES), BF16), pltpu.VMEM((LANES, 2 * tq), BF16),
                        pltpu.VMEM((2 * tq, LANES), BF16), pltpu.VMEM((1, 2 * tq), F32), pltpu.VMEM((LANES, 2 * tq), F32)] + s_sems,
        compiler_params=_params(("arbitrary",) * 3))(q, k, kT, v, o, do, lse, *s_args)


MLA_C = MLA_SCALE * LOG2E


def _mla_fwd(q, kcat, kcatT, *, tq, tk, sub):
    S = kcat.shape[0]; nq, nkv = S // tq, S // tk; R = B_HEADS * tq; nsub = tk // sub

    def body(q_ref, k_ref, vT_ref, o_ref, lse_ref, qT, m_s, l_s, acc):
        j = pl.program_id(1)

        @pl.when(j == 0)
        def _():
            qT[...] = q_ref[...].reshape(R, 2 * LANES).astype(F32).T.astype(BF16)
            m_s[...] = jnp.full((1, R), NEG, F32)
            l_s[...] = jnp.zeros((1, R), F32)
            acc[...] = jnp.zeros((LANES, R), F32)

        qTv = qT[...]
        m, l, a = m_s[...], l_s[...], acc[...]
        s_cur = _dot(k_ref[0:sub, :], qTv)
        for t in range(nsub):
            if t + 1 < nsub:
                s_next = _dot(k_ref[sub * (t + 1):sub * (t + 2), :], qTv)
            m_new = jnp.maximum(m, jnp.max(s_cur, axis=0, keepdims=True))
            alpha = jnp.exp2((m - m_new) * MLA_C)
            p = jnp.exp2((s_cur - m_new) * MLA_C)
            l = alpha * l + jnp.sum(p, axis=0, keepdims=True)
            a = alpha * a + _dot(vT_ref[:, sub * t:sub * (t + 1)], p.astype(BF16))
            m = m_new
            if t + 1 < nsub:
                s_cur = s_next
        m_s[...], l_s[...], acc[...] = m, l, a

        @pl.when(j == nkv - 1)
        def _():
            l_f = l_s[...]
            o_ref[...] = (acc[...] / l_f).T.reshape(B_HEADS, tq, LANES).astype(BF16)
            lse_ref[0] = m_s[...] * MLA_SCALE + jnp.log(l_f)

    return pl.pallas_call(
        body, name="mla_fwd", grid=(nq, nkv),
        in_specs=[pl.BlockSpec((B_HEADS, tq, 2 * LANES), lambda i, j: (0, i, 0)), pl.BlockSpec((tk, 2 * LANES), lambda i, j: (j, 0)),
                  pl.BlockSpec((LANES, tk), lambda i, j: (0, j))],
        out_specs=[pl.BlockSpec((B_HEADS, tq, LANES), lambda i, j: (0, i, 0)), pl.BlockSpec((1, 1, R), lambda i, j: (i, 0, 0))],
        out_shape=[_sds((B_HEADS, S, LANES), BF16), _sds((nq, 1, R), F32)],
        scratch_shapes=[pltpu.VMEM((2 * LANES, R), BF16), pltpu.VMEM((1, R), F32), pltpu.VMEM((1, R), F32), pltpu.VMEM((LANES, R), F32)],
        compiler_params=_params(("arbitrary", "arbitrary")))(q, kcat, kcatT)


def _mla_bwd(q, kcat, kcatT, o, do, lse, *, tq, tk, sub):
    S = kcat.shape[0]; nq, nkv = S // tq, S // tk; R = B_HEADS * tq; nsub = tk // sub

    def body(q_ref, k_ref, kT_ref, o_ref, do_ref, lse_ref, dq_ref, dk_ref, qT, dosT, dos, delta_s, dq_acc):
        i, j = pl.program_id(0), pl.program_id(1)

        @pl.when((i == 0) & (j == 0))
        def _():
            dk_ref[...] = jnp.zeros((S, 2 * LANES), F32)

        @pl.when(j == 0)
        def _():
            qT[...] = q_ref[...].reshape(R, 2 * LANES).astype(F32).T.astype(BF16)
            dov = do_ref[...].reshape(R, LANES).astype(F32)
            dos[...] = dov.astype(BF16)
            dosT[...] = dov.T.astype(BF16)
            delta_s[...] = jnp.sum((dov * o_ref[...].reshape(R, LANES).astype(F32)).T, axis=0, keepdims=True)
            dq_acc[...] = jnp.zeros((2 * LANES, R), F32)

        qTv, dosTv, dosv = qT[...], dosT[...], dos[...]
        qv = q_ref[...].reshape(R, 2 * LANES)
        lse_v, delta_v = lse_ref[0] * LOG2E, delta_s[...]
        dqa = dq_acc[...]
        s_cur = _dot(k_ref[0:sub, :], qTv)
        dp_cur = _dot(k_ref[0:sub, 0:LANES], dosTv)
        for t in range(nsub):
            if t + 1 < nsub:
                s_next = _dot(k_ref[sub * (t + 1):sub * (t + 2), :], qTv)
                dp_next = _dot(k_ref[sub * (t + 1):sub * (t + 2), 0:LANES], dosTv)
            p = jnp.exp2(s_cur * MLA_C - lse_v)
            ds = (p * (dp_cur - delta_v) * MLA_SCALE).astype(BF16)
            rows = pl.ds(pl.multiple_of(j * tk + sub * t, sub), sub)
            dk_ref[rows, :] += _dot(ds, qv)
            dk_ref[rows, 0:LANES] += _dot(p.astype(BF16), dosv)
            dqa = dqa + _dot(kT_ref[:, sub * t:sub * (t + 1)], ds)
            if t + 1 < nsub:
                s_cur, dp_cur = s_next, dp_next
        dq_acc[...] = dqa

        @pl.when(j == nkv - 1)
        def _():
            dq_ref[...] = dq_acc[...].T.reshape(B_HEADS, tq, 2 * LANES)

    hspec = lambda w: pl.BlockSpec((B_HEADS, tq, w), lambda i, j: (0, i, 0))
    return pl.pallas_call(
        body, name="mla_bwd", grid=(nq, nkv),
        in_specs=[hspec(2 * LANES), pl.BlockSpec((tk, 2 * LANES), lambda i, j: (j, 0)), pl.BlockSpec((2 * LANES, tk), lambda i, j: (0, j)),
                  hspec(LANES), hspec(LANES), pl.BlockSpec((1, 1, R), lambda i, j: (i, 0, 0))],
        out_specs=[hspec(2 * LANES), pl.BlockSpec((S, 2 * LANES), lambda i, j: (0, 0))],
        out_shape=[_sds((B_HEADS, S, 2 * LANES), F32), _sds((S, 2 * LANES), F32)],
        scratch_shapes=[pltpu.VMEM((2 * LANES, R), BF16), pltpu.VMEM((LANES, R), BF16), pltpu.VMEM((R, LANES), BF16),
                        pltpu.VMEM((1, R), F32), pltpu.VMEM((2 * LANES, R), F32)],
        compiler_params=_params(("arbitrary", "arbitrary")))(q, kcat, kcatT, o, do, lse)


def _win_start(i, tq, nk, S):
    return pl.multiple_of(jnp.clip(i * tq - WINDOW, 0, S - nk), LANES)


def _win_dist_table(S, tq):
    nk = min(tq + 2 * WINDOW, S)
    nq = S // tq
    r = np.arange(nk)[:, None]
    c = (np.arange(2 * tq) % tq)[None, :]
    tabs = []
    for rel in (0, WINDOW, (nq - 1) * tq - (S - nk)):
        dist = np.abs(rel + c - r).astype(np.float32)
        tabs.append(np.where(dist <= WINDOW, dist, np.float32(1e32)))
    return jnp.asarray(np.stack(tabs))


def _win_dist_spec(nk, tq, nq):
    return pl.BlockSpec((1, nk, 2 * tq), lambda b, i: (jnp.where(i == 0, 0, jnp.where(i == nq - 1, 2, 1)), 0, 0))


def _win_fwd(q, k, vT, dist, slope, sink, *, kdiv, tq, nbs, name):
    S = k.shape[0]; nb = q.shape[1] // LANES; nq = S // tq; nk = min(tq + 2 * WINDOW, S)
    assert nb % nbs == 0 and nbs % kdiv == 0
    kvw = (nbs // kdiv) * LANES

    def body(q_ref, k_ref, vT_ref, dist_ref, slope_ref, sink_ref, o_ref, lse_ref):
        i = pl.program_id(1)
        rlo = _row_lo()
        k0 = _win_start(i, tq, nk, S)
        kk, vv, dd = k_ref[pl.ds(k0, nk), :], vT_ref[:, pl.ds(k0, nk)], dist_ref[0]
        for u in range(nbs):
            kv = slice(LANES * (u // kdiv), LANES * (u // kdiv + 1))
            qsT = _stack_cols(q_ref[:, LANES * u:LANES * (u + 1)].astype(F32).T, rlo).astype(BF16)
            s = _dot(kk[:, kv], qsT) - slope_ref[u] * dd
            sk = sink_ref[u]
            m = jnp.maximum(jnp.max(s, axis=0, keepdims=True), sk)
            p = jnp.exp(s - m)
            l = jnp.sum(p, axis=0, keepdims=True) + jnp.exp(sk - m)
            o_ref[:, LANES * u:LANES * (u + 1)] = _pick_halves_T(_dot(vv[kv, :], p.astype(BF16)) / l, rlo, tq).astype(BF16)
            lse_ref[u, 0] = m + jnp.log(l)

    row_spec = pl.BlockSpec((nbs, 1, 2 * tq), lambda b, i: (b, 0, 0))
    return pl.pallas_call(
        body, name=name, grid=(nb // nbs, nq),
        in_specs=[pl.BlockSpec((tq, nbs * LANES), lambda b, i: (i, b)), pl.BlockSpec((S, kvw), lambda b, i: (0, b)),
                  pl.BlockSpec((kvw, S), lambda b, i: (b, 0)), _win_dist_spec(nk, tq, nq), row_spec, row_spec],
        out_specs=[pl.BlockSpec((tq, nbs * LANES), lambda b, i: (i, b)), pl.BlockSpec((nbs, 1, 1, 2 * tq), lambda b, i: (b, i, 0, 0))],
        out_shape=[_sds((S, nb * LANES), BF16), _sds((nb, nq, 1, 2 * tq), F32)],
        compiler_params=_params(("arbitrary", "arbitrary")))(q, k, vT, dist, slope, sink)


def _win_bwd(q, k, kT, v, o, do, lse, dist, slope, sink, *, kdiv, tq, nbs, name):
    S = k.shape[0]; nb = q.shape[1] // LANES; nkb = k.shape[1] // LANES; nq = S // tq; nk = min(tq + 2 * WINDOW, S)
    assert nb % nbs == 0 and nbs % kdiv == 0
    nkv = nbs // kdiv
    kvw = nkv * LANES

    def body(q_ref, k_ref, kT_ref, v_ref, o_ref, do_ref, lse_ref, dist_ref, slope_ref, sink_ref, dq_ref, dk_ref, dv_ref, dsink_ref, ds_acc):
        i = pl.program_id(1)
        rlo = _row_lo()
        lo = lax.broadcasted_iota(jnp.int32, (1, LANES), 1) < HEAD_DIM

        @pl.when(i == 0)
        def _():
            dk_ref[...] = jnp.zeros((S, kvw), F32)
            dv_ref[...] = jnp.zeros((S, kvw), F32)
            ds_acc[...] = jnp.zeros((nbs, 2 * tq), F32)

        k0 = _win_start(i, tq, nk, S)
        rows = pl.ds(k0, nk)
        kk_all, vv_all, kkT_all, dd = k_ref[rows, :], v_ref[rows, :], kT_ref[:, rows], dist_ref[0]
        dv_sum, dk_sum = [None] * nkv, [None] * nkv
        for u in range(nbs):
            g = u // kdiv
            kv = slice(LANES * g, LANES * (g + 1))
            kk, vv, kkT = kk_all[:, kv], vv_all[:, kv], kkT_all[kv, :]
            cols = slice(LANES * u, LANES * (u + 1))
            qv = q_ref[:, cols]
            qs = _stack_rows(qv, lo)
            qsT = _stack_cols(qv.astype(F32).T, rlo).astype(BF16)
            dov = do_ref[:, cols].astype(F32)
            dos = _stack_rows(dov.astype(BF16), lo)
            dosT = _stack_cols(dov.T, rlo).astype(BF16)
            prodT = (dov * o_ref[:, cols].astype(F32)).T
            delta = jnp.concatenate([jnp.sum(jnp.where(rlo, prodT, 0.0), axis=0, keepdims=True),
                                     jnp.sum(jnp.where(rlo, 0.0, prodT), axis=0, keepdims=True)], axis=1)
            lse_v = lse_ref[u, 0]
            ds_acc[u:u + 1, :] += -jnp.exp(sink_ref[u] - lse_v) * delta
            p = jnp.exp(_dot(kk, qsT) - slope_ref[u] * dd - lse_v)
            ds = (p * (_dot(vv, dosT) - delta)).astype(BF16)
            dv_u, dk_u = _dot(p.astype(BF16), dos), _dot(ds, qs)
            dv_sum[g] = dv_u if dv_sum[g] is None else dv_sum[g] + dv_u
            dk_sum[g] = dk_u if dk_sum[g] is None else dk_sum[g] + dk_u
            dq_ref[:, cols] = (_pick_halves_T(_dot(kkT, ds), rlo, tq) * 0.125).astype(BF16)
        dv_ref[rows, :] += jnp.concatenate(dv_sum, axis=1)
        dk_ref[rows, :] += jnp.concatenate(dk_sum, axis=1)

        @pl.when(i == nq - 1)
        def _():
            acc = ds_acc[...]
            for u in range(nbs):
                dsink_ref[u] = jnp.concatenate(
                    [jnp.broadcast_to(jnp.sum(acc[u:u + 1, 0:tq], axis=1, keepdims=True), (1, LANES)),
                     jnp.broadcast_to(jnp.sum(acc[u:u + 1, tq:2 * tq], axis=1, keepdims=True), (1, LANES)),
                     jnp.zeros((6, LANES), F32)], axis=0)

    qmap = lambda b, i: (i, b)
    kv_spec = pl.BlockSpec((S, kvw), lambda b, i: (0, b))
    row_spec = pl.BlockSpec((nbs, 1, 2 * tq), lambda b, i: (b, 0, 0))
    wide = pl.BlockSpec((tq, nbs * LANES), qmap)
    return pl.pallas_call(
        body, name=name, grid=(nb // nbs, nq),
        in_specs=[wide, kv_spec, pl.BlockSpec((kvw, S), lambda b, i: (b, 0)), kv_spec, wide, wide,
                  pl.BlockSpec((nbs, 1, 1, 2 * tq), lambda b, i: (b, i, 0, 0)), _win_dist_spec(nk, tq, nq), row_spec, row_spec],
        out_specs=[wide, kv_spec, kv_spec, pl.BlockSpec((nbs, 8, LANES), lambda b, i: (b, 0, 0))],
        out_shape=[_sds((S, nb * LANES), BF16), _sds((S, nkb * LANES), F32), _sds((S, nkb * LANES), F32), _sds((nb, 8, LANES), F32)],
        scratch_shapes=[pltpu.VMEM((nbs, 2 * tq), F32)],
        compiler_params=_params(("arbitrary", "arbitrary")))(q, k, kT, v, o, do, lse, dist, slope, sink)


def _sum_rows(v):
    return jnp.sum(v, axis=0, keepdims=True)


def _norm_mod_bwd(dh, xv, mod_ref, nw_ref, stats_ref):
    r = _rms(xv)
    xn = xv * r
    nw = nw_ref[...]
    stats_ref[0:1, :] += _sum_rows(dh)
    stats_ref[1:2, :] += _sum_rows(dh * (xn * nw))
    dn = dh * (1.0 + mod_ref[1:2, :])
    stats_ref[2:3, :] += _sum_rows(dn * xn)
    return _rms_bwd(xv, r, dn * nw)


def _even_gate_specs(ts):
    return [pl.BlockSpec((ts, 256), lambda i, c=c: (i, c)) for c in (3, 4, 7, 8)]


def _even_post_fwd(oa, olat, proj, x, gate, wuv, woe):
    S = x.shape[0]
    ts = min(ROW_TILE, S)

    def body(oa_ref, ol_ref, ga0_ref, ga1_ref, gb0_ref, gb1_ref, x_ref, gate_ref, wuv_ref, woe_ref, y_ref, x1_ref):
        sa, _ = _silu_and_grad(jnp.concatenate([ga0_ref[...], ga1_ref[...]], axis=1))
        sb, _ = _silu_and_grad(jnp.concatenate([gb0_ref[...], gb1_ref[...]], axis=1))
        olc = jnp.concatenate([ol_ref[hh] for hh in range(B_HEADS)], axis=1).astype(BF16)
        ob = _dot(olc, wuv_ref[...])
        mix = jnp.concatenate([oa_ref[...] * sa, ob * sb], axis=1).astype(BF16)
        y = _dot(mix, woe_ref[...])
        y_ref[...] = y.astype(BF16)
        x1_ref[...] = x_ref[...] + gate_ref[...] * y

    return pl.pallas_call(
        body, name="even_post_fwd", grid=(S // ts,),
        in_specs=[_row_spec(ts, 512), pl.BlockSpec((B_HEADS, ts, LANES), lambda i: (0, i, 0))] + _even_gate_specs(ts) +
                 [_row_spec(ts, D_MODEL), _full_spec((1, D_MODEL)), _full_spec((1024, 512)), _full_spec((1024, D_MODEL))],
        out_specs=[_row_spec(ts, D_MODEL), _row_spec(ts, D_MODEL)],
        out_shape=[_sds((S, D_MODEL), BF16), _sds((S, D_MODEL), F32)],
        compiler_params=_params(("arbitrary",)),
    )(oa, olat, proj, proj, proj, proj, x, gate, wuv, woe)


def _odd_pre_fwd(x, mod, nw, wio):
    S = x.shape[0]
    ts = min(ROW_TILE, S)

    def body(x_ref, mod_ref, nw_ref, wio_ref, h_ref, g_ref, q_ref, k_ref, v_ref, kt_ref, vt_ref):
        xv = x_ref[...]
        h = (xv * _rms(xv) * nw_ref[...]) * (1.0 + mod_ref[1:2, :]) + mod_ref[0:1, :]
        hb = h.astype(BF16)
        h_ref[...] = hb
        proj = jnp.concatenate([_dot(hb, wio_ref[p]) for p in range(N_CHIPS)], axis=1)
        g_ref[...] = proj[:, 1536:2560]
        q_ref[...] = (proj[:, 0:1024] * 0.125).astype(BF16)
        lane = _lane_iota()
        k_v = jnp.concatenate([_dup_heads(proj[:, 1024 + LANES * j:1024 + LANES * (j + 1)], lane) for j in range(2)], axis=1)
        v_v = jnp.concatenate([_dup_heads(proj[:, 1280 + LANES * j:1280 + LANES * (j + 1)], lane) for j in range(2)], axis=1)
        k_ref[...] = k_v.astype(BF16)
        v_ref[...] = v_v.astype(BF16)
        kt_ref[...] = k_v.T.astype(BF16)
        vt_ref[...] = v_v.T.astype(BF16)

    col_spec = pl.BlockSpec((512, ts), lambda i: (0, i))
    return pl.pallas_call(
        body, name="odd_pre_fwd", grid=(S // ts,),
        in_specs=[_row_spec(ts, D_MODEL), _full_spec((3, D_MODEL)), _full_spec((1, D_MODEL)),
                  _full_spec((N_CHIPS, D_MODEL, ODD_IN // N_CHIPS))],
        out_specs=[_row_spec(ts, D_MODEL), _row_spec(ts, 1024), _row_spec(ts, 1024), _row_spec(ts, 512), _row_spec(ts, 512),
                   col_spec, col_spec],
        out_shape=[_sds((S, D_MODEL), BF16), _sds((S, 1024), F32), _sds((S, 1024), BF16), _sds((S, 512), BF16),
                   _sds((S, 512), BF16), _sds((512, S), BF16), _sds((512, S), BF16)],
        compiler_params=_params(("arbitrary",)),
    )(x, mod, nw, wio)


def _odd_post(oc, g, x1, gate, woo, fw, tgt):
    S = x1.shape[0]
    ts = min(ROW_TILE, S)
    nsteps = S // ts

    def body(oc_ref, g_ref, x_ref, gate_ref, woo_ref, fw_ref, tgt_ref, doc_ref, dgc_ref, dx2_ref, dwoo_out, stats_ref, dwoo_ref):
        @pl.when(pl.program_id(0) == 0)
        def _():
            dwoo_ref[...] = jnp.zeros((D_MODEL, D_MODEL), F32)
            stats_ref[...] = jnp.zeros((8, D_MODEL), F32)

        ocv = oc_ref[...]
        sg, dsg = _silu_and_grad(g_ref[...])
        mix = (ocv * sg).astype(BF16)
        woo_v = woo_ref[...]
        y = _dot(mix, woo_v)
        gate_v = gate_ref[...]
        x2 = x_ref[...] + gate_v * y
        r = _rms(x2)
        fw_v = fw_ref[...]
        xn = x2 * r
        err = xn * fw_v - tgt_ref[...]
        dout = err * (1.0 / D_MODEL)
        dx2 = _rms_bwd(x2, r, dout * fw_v)
        dx2_ref[...] = dx2
        stats_ref[0:1, :] += _sum_rows(dout * xn)
        stats_ref[1:2, :] += _sum_rows(dx2 * y)
        loss_t = 0.5 * jnp.sum(_sum_rows(err * dout), axis=-1, keepdims=True)
        stats_ref[2:3, :] += jnp.broadcast_to(loss_t, (1, D_MODEL))
        dy = (gate_v * dx2).astype(BF16)
        dmix = _dot_nt(dy, woo_v)
        dwoo_ref[...] += _dot_tn(mix, dy)
        doc_ref[...] = (dmix * sg).astype(BF16)
        dgc_ref[...] = (dmix * ocv * dsg).astype(BF16)

        @pl.when(pl.program_id(0) == nsteps - 1)
        def _():
            dwoo_out[...] = dwoo_ref[...].astype(BF16)

    return pl.pallas_call(
        body, name="odd_post", grid=(nsteps,),
        in_specs=[_row_spec(ts, D_MODEL), _row_spec(ts, D_MODEL), _row_spec(ts, D_MODEL), _full_spec((1, D_MODEL)),
                  _full_spec((D_MODEL, D_MODEL)), _full_spec((1, D_MODEL)), _row_spec(ts, D_MODEL)],
        out_specs=[_row_spec(ts, D_MODEL), _row_spec(ts, D_MODEL), _row_spec(ts, D_MODEL),
                   _full_spec((D_MODEL, D_MODEL), single=False), _full_spec((8, D_MODEL), single=False)],
        out_shape=[_sds((S, D_MODEL), BF16), _sds((S, D_MODEL), BF16), _sds((S, D_MODEL), F32), _sds((D_MODEL, D_MODEL), BF16),
                   _sds((8, D_MODEL), F32)],
        scratch_shapes=[pltpu.VMEM((D_MODEL, D_MODEL), F32)],
        compiler_params=_params(("arbitrary",)),
    )(oc, g, x1, gate, woo, fw, tgt)


def _odd_pre_bwd(dq, dk, dv, dgc, h, x, dx_res, mod, nw, wio):
    S = x.shape[0]
    ts = min(IN_PROJ_ROW_TILE, S)
    nsteps = S // ts
    wsh = ODD_IN // N_CHIPS

    def body(dq_ref, dk_ref, dv_ref, dgc_ref, h_ref, x_ref, dxr_ref, mod_ref, nw_ref, wio_ref, dx_ref, dw_ref, stats_ref, dw_acc):
        @pl.when(pl.program_id(0) == 0)
        def _():
            dw_acc[...] = jnp.zeros((N_CHIPS, D_MODEL, wsh), F32)
            stats_ref[...] = jnp.zeros((8, D_MODEL), F32)

        lane = _lane_iota()
        dkv = [_fold_heads(r[:, 2 * LANES * j:2 * LANES * (j + 1)], lane).astype(BF16) for r in (dk_ref, dv_ref) for j in range(2)]
        dproj = jnp.concatenate([dq_ref[...]] + dkv + [dgc_ref[...]], axis=1)
        hv = h_ref[...]
        dh = None
        for p in range(N_CHIPS):
            dp_cols = dproj[:, wsh * p:wsh * (p + 1)]
            part = _dot_nt(dp_cols, wio_ref[p])
            dh = part if dh is None else dh + part
            dw_acc[p] += _dot_tn(hv, dp_cols)
        dx_ref[...] = dxr_ref[...] + _norm_mod_bwd(dh, x_ref[...], mod_ref, nw_ref, stats_ref)

        @pl.when(pl.program_id(0) == nsteps - 1)
        def _():
            dw_ref[...] = dw_acc[...].astype(BF16)

    return pl.pallas_call(
        body, name="odd_pre_bwd", grid=(nsteps,),
        in_specs=[_row_spec(ts, 1024), _row_spec(ts, 512), _row_spec(ts, 512), _row_spec(ts, 1024), _row_spec(ts, D_MODEL),
                  _row_spec(ts, D_MODEL), _row_spec(ts, D_MODEL), _full_spec((3, D_MODEL)), _full_spec((1, D_MODEL)),
                  _full_spec((N_CHIPS, D_MODEL, wsh))],
        out_specs=[_row_spec(ts, D_MODEL), _full_spec((N_CHIPS, D_MODEL, wsh), single=False), _full_spec((8, D_MODEL), single=False)],
        out_shape=[_sds((S, D_MODEL), F32), _sds((N_CHIPS, D_MODEL, wsh), BF16), _sds((8, D_MODEL), F32)],
        scratch_shapes=[pltpu.VMEM((N_CHIPS, D_MODEL, wsh), F32)],
        compiler_params=_params(("arbitrary",)),
    )(dq, dk, dv, dgc, h, x, dx_res, mod, nw, wio)


def _even_post_bwd(dx1, y, oa, olat, proj, gate, wuv, woe):
    S = dx1.shape[0]
    ts = min(ROW_TILE, S)
    nsteps = S // ts

    def body(dx_ref, y_ref, oa_ref, ol_ref, ga0_ref, ga1_ref, gb0_ref, gb1_ref, gate_ref, wuv_ref, woe_ref,
             doa_ref, dga_ref, dgb_ref, dol_ref, dwoe_out, dwuv_ref, stats_ref, dwoe_ref):
        @pl.when(pl.program_id(0) == 0)
        def _():
            dwoe_ref[...] = jnp.zeros((D_MODEL, D_MODEL), F32)
            dwuv_ref[...] = jnp.zeros((1024, 512), F32)
            stats_ref[...] = jnp.zeros((8, D_MODEL), F32)

        dxv = dx_ref[...]
        stats_ref[0:1, :] += _sum_rows(dxv * y_ref[...])
        dy = (gate_ref[...] * dxv).astype(BF16)
        sa, dsa = _silu_and_grad(jnp.concatenate([ga0_ref[...], ga1_ref[...]], axis=1))
        sb, dsb = _silu_and_grad(jnp.concatenate([gb0_ref[...], gb1_ref[...]], axis=1))
        olc = jnp.concatenate([ol_ref[hh] for hh in range(B_HEADS)], axis=1).astype(BF16)
        wuv_v = wuv_ref[...]
        ob = _dot(olc, wuv_v)
        oav = oa_ref[...]
        mix = jnp.concatenate([oav * sa, ob * sb], axis=1).astype(BF16)
        dmix = _dot_nt(dy, woe_ref[...])
        dwoe_ref[...] += _dot_tn(mix, dy)
        dma, dmb = dmix[:, 0:512], dmix[:, 512:1024]
        doa_ref[...] = (dma * sa).astype(BF16)
        dga_ref[...] = (dma * oav * dsa).astype(BF16)
        dgb_ref[...] = (dmb * ob * dsb).astype(BF16)
        dob = (dmb * sb).astype(BF16)
        dol = _dot_nt(dob, wuv_v)
        dwuv_ref[...] += _dot_tn(olc, dob)
        for hh in range(B_HEADS):
            dol_ref[hh] = dol[:, LANES * hh:LANES * (hh + 1)].astype(BF16)

        @pl.when(pl.program_id(0) == nsteps - 1)
        def _():
            dwoe_out[...] = dwoe_ref[...].astype(BF16)

    head_spec = pl.BlockSpec((B_HEADS, ts, LANES), lambda i: (0, i, 0))
    return pl.pallas_call(
        body, name="even_post_bwd", grid=(nsteps,),
        in_specs=[_row_spec(ts, D_MODEL), _row_spec(ts, D_MODEL), _row_spec(ts, 512), head_spec] + _even_gate_specs(ts) +
                 [_full_spec((1, D_MODEL)), _full_spec((1024, 512)), _full_spec((1024, D_MODEL))],
        out_specs=[_row_spec(ts, 512), _row_spec(ts, 512), _row_spec(ts, 512), head_spec,
                   _full_spec((D_MODEL, D_MODEL), single=False), _full_spec((1024, 512), single=False),
                   _full_spec((8, D_MODEL), single=False)],
        out_shape=[_sds((S, 512), BF16), _sds((S, 512), BF16), _sds((S, 512), BF16), _sds((B_HEADS, S, LANES), BF16),
                   _sds((D_MODEL, D_MODEL), BF16), _sds((1024, 512), F32), _sds((8, D_MODEL), F32)],
        scratch_shapes=[pltpu.VMEM((D_MODEL, D_MODEL), F32)],
        compiler_params=_params(("arbitrary",)),
    )(dx1, y, oa, olat, proj, proj, proj, proj, gate, wuv, woe)


def _even_pre_bwd(x, h, proj, dqa, dka, dva, dga, dgb, dqcat, dkcat, dx_res, mod, nw, wie, qn, kn, seg, ca, sa, ct, st,
                  qln, kvln, wuq, wuk):
    S = x.shape[0]
    ts = min(IN_PROJ_ROW_TILE, S)
    nsteps = S // ts

    def body(x_ref, h_ref, proj_ref, dqa_ref, dka_ref, dva_ref, dga_ref, dgb_ref, dqc_ref, dkc_ref, dxr_ref, mod_ref, nw_ref,
             wie_ref, qn_ref, kn_ref, seg_ref, ca_ref, sa_ref, ct_ref, st_ref, qln_ref, kvln_ref, wuq_ref, wuk_ref,
             dx_ref, dwie_out, dwuq_out, dwuk_out, stats_ref, nstats_ref, dwie_ref, dwuq_ref, dwuk_ref, stage):
        @pl.when(pl.program_id(0) == 0)
        def _():
            dwie_ref[...] = jnp.zeros((D_MODEL, EVEN_P), F32)
            dwuq_ref[...] = jnp.zeros((B_Q_LORA, 1536), F32)
            dwuk_ref[...] = jnp.zeros((512, 1024), F32)
            stats_ref[...] = jnp.zeros((8, D_MODEL), F32)
            nstats_ref[...] = jnp.zeros((8, 256), F32)

        lane = _lane_iota()
        ca_v, sa_v, ct_v, st_v = ca_ref[...], sa_ref[...], ct_ref[...], st_ref[...]
        seg_v = seg_ref[...]

        def head_norm_bwd(xc, dy, w):
            r = lax.rsqrt(_seg_mean(xc * xc, seg_v) + EPS)
            g = dy * w
            dxc = r * g - xc * (r * r * r) * _seg_mean(xc * g, seg_v)
            return dxc, _sum_rows(dy * (xc * r))

        pieces = []
        dqn = jnp.zeros((1, LANES), F32)
        for cb in range(4):
            sl = slice(LANES * cb, LANES * (cb + 1))
            dy = _rot_bwd(dqa_ref[:, sl] * 0.125, ca_v, sa_v, lane)
            dxc, dw = head_norm_bwd(proj_ref[:, sl], dy, qn_ref[...])
            pieces.append(dxc)
            dqn = dqn + dw
        dxc, dkn = head_norm_bwd(proj_ref[:, 512:640], _rot_bwd(_fold_heads(dka_ref[...], lane), ca_v, sa_v, lane), kn_ref[...])
        pieces += [dxc, _fold_heads(dva_ref[...], lane), dga_ref[...]]
        nstats_ref[0:1, 0:LANES] += dqn + pltpu.roll(dqn, HEAD_DIM, 1)
        nstats_ref[1:2, 0:LANES] += dkn + pltpu.roll(dkn, HEAD_DIM, 1)

        cq = proj_ref[:, 1280:1536]
        rq = _rms(cq)
        cqn_f = cq * rq
        qln_v = qln_ref[...]
        cqn = (cqn_f * qln_v).astype(BF16)
        wuq_v, wuk_v = wuq_ref[...], wuk_ref[...]
        qnope = _dot(cqn, wuq_v[:, 0:512]).astype(BF16)
        dqlat = jnp.concatenate([dqc_ref[hh, :, 0:LANES] for hh in range(B_HEADS)], axis=1).astype(BF16)
        dqnope = _dot_nt(dqlat, wuk_v)
        dwuk_ref[...] += _dot_tn(qnope, dqlat)
        dqr = [_rot_bwd(dqc_ref[hh, :, LANES:2 * LANES], ct_v, st_v, lane) for hh in range(B_HEADS)]
        dqb = jnp.concatenate([dqnope] + dqr, axis=1).astype(BF16)
        dcqn = _dot_nt(dqb, wuq_v)
        dwuq_ref[...] += _dot_tn(cqn, dqb)
        nstats_ref[2:3, :] += _sum_rows(dcqn * cqn_f)
        dcq = _rms_bwd(cq, rq, dcqn * qln_v)
        ckv = proj_ref[:, 1536:1664]
        rk = _rms(ckv)
        dckvn = dkc_ref[:, 0:LANES]
        nstats_ref[3:4, 0:LANES] += _sum_rows(dckvn * (ckv * rk))
        dckv = _rms_bwd(ckv, rk, dckvn * kvln_ref[...])
        dkr = _rot_bwd(dkc_ref[:, LANES:2 * LANES], ct_v, st_v, lane)
        pieces += [dcq, dckv, dkr, dgb_ref[...]]
        dproj = jnp.concatenate([piece.astype(BF16) for piece in pieces], axis=1)
        dh = _dot_nt(dproj, wie_ref[...])
        dwie_ref[...] += _dot_tn(h_ref[...], dproj)
        dx_ref[...] = dxr_ref[...] + _norm_mod_bwd(dh, x_ref[...], mod_ref, nw_ref, stats_ref)

        @pl.when(pl.program_id(0) == nsteps - 1)
        def _():
            for r0 in range(0, D_MODEL, 256):
                stage[...] = dwie_ref[r0:r0 + 256, :].astype(BF16)
                pltpu.sync_copy(stage, dwie_out.at[pl.ds(r0, 256), :])
            pltpu.sync_copy(dwuq_ref, dwuq_out)
            pltpu.sync_copy(dwuk_ref, dwuk_out)

    return pl.pallas_call(
        body, name="even_pre_bwd", grid=(nsteps,),
        in_specs=[_row_spec(ts, D_MODEL), _row_spec(ts, D_MODEL), _row_spec(ts, EVEN_P), _row_spec(ts, 512), _row_spec(ts, 2 * LANES),
                  _row_spec(ts, 2 * LANES), _row_spec(ts, 512), _row_spec(ts, 512),
                  pl.BlockSpec((B_HEADS, ts, 2 * LANES), lambda i: (0, i, 0)), _row_spec(ts, 2 * LANES), _row_spec(ts, D_MODEL),
                  _full_spec((3, D_MODEL)), _full_spec((1, D_MODEL)), _full_spec((D_MODEL, EVEN_P)),
                  _full_spec((1, LANES)), _full_spec((1, LANES)), _full_spec((LANES, LANES)),
                  _row_spec(ts, LANES), _row_spec(ts, LANES), _row_spec(ts, LANES), _row_spec(ts, LANES),
                  _full_spec((1, B_Q_LORA)), _full_spec((1, B_KV_LORA)), _full_spec((B_Q_LORA, 1536)), _full_spec((512, 1024))],
        out_specs=[_row_spec(ts, D_MODEL), _ANY, _ANY, _ANY, _full_spec((8, D_MODEL), single=False), _full_spec((8, 256), single=False)],
        out_shape=[_sds((S, D_MODEL), F32), _sds((D_MODEL, EVEN_P), BF16), _sds((B_Q_LORA, 1536), F32), _sds((512, 1024), F32),
                   _sds((8, D_MODEL), F32), _sds((8, 256), F32)],
        scratch_shapes=[pltpu.VMEM((D_MODEL, EVEN_P), F32), pltpu.VMEM((B_Q_LORA, 1536), F32), pltpu.VMEM((512, 1024), F32),
                        pltpu.VMEM((256, EVEN_P), BF16)],
        compiler_params=_params(("arbitrary",)),
    )(x, h, proj, dqa, dka, dva, dga, dgb, dqcat, dkcat, dx_res, mod, nw, wie, qn, kn, seg, ca, sa, ct, st, qln, kvln, wuq, wuk)


def _ada_fwd(c_all, w, b):
    n = w.shape[2]

    def body(c_ref, w_ref, b_ref, o_ref):
        cv = c_ref[...]
        o_ref[0] = _dot_f32(cv * _sigmoid(cv), w_ref[0]) + b_ref[0]

    return pl.pallas_call(
        body, name="ada_fwd", grid=(2,),
        in_specs=[pl.BlockSpec((N_DEV, D_MODEL), lambda l: (0, 0)), pl.BlockSpec((1, D_MODEL, n), lambda l: (l, 0, 0)),
                  pl.BlockSpec((1, 1, n), lambda l: (l, 0, 0))],
        out_specs=pl.BlockSpec((1, N_DEV, n), lambda l: (l, 0, 0)),
        out_shape=_sds((2, N_DEV, n), F32),
        compiler_params=_params(("arbitrary",)),
    )(c_all, w, b)


def _ada_bwd(c_all_t, dmod):
    n = dmod.shape[2]

    def body(c_ref, d_ref, o_ref):
        cv = c_ref[...]
        act = cv * _sigmoid(cv)
        dv = d_ref[0]
        acc = act[:, 0:1] * dv[0:1, :]
        for bb in range(1, N_DEV):
            acc = acc + act[:, bb:bb + 1] * dv[bb:bb + 1, :]
        o_ref[0] = acc

    return pl.pallas_call(
        body, name="ada_bwd", grid=(2,),
        in_specs=[pl.BlockSpec((D_MODEL, N_DEV), lambda l: (0, 0)), pl.BlockSpec((1, N_DEV, n), lambda l: (l, 0, 0))],
        out_specs=pl.BlockSpec((1, D_MODEL, n), lambda l: (l, 0, 0)),
        out_shape=_sds((2, D_MODEL, n), F32),
        compiler_params=_params(("arbitrary",)),
    )(c_all_t, dmod)


ADAM_ROW_TILE = 512


def _adam_update(g, w, m, v):
    m_new = ADAM_B1 * m + (1.0 - ADAM_B1) * g
    v_new = ADAM_B2 * v + (1.0 - ADAM_B2) * jnp.square(g)
    m_hat = m_new / (1.0 - ADAM_B1 ** ADAM_STEP)
    v_hat = v_new / (1.0 - ADAM_B2 ** ADAM_STEP)
    return -ADAM_LR * (m_hat / (jnp.sqrt(v_hat) + ADAM_EPS) + ADAM_WD * w), m_new, v_new


SMALL_ROWS = dict(dmod=(0, D_MODEL), norm_w=(6, D_MODEL), final_norm=(8, D_MODEL), a_q_norm=(9, HEAD_DIM), a_k_norm=(10, HEAD_DIM),
                  b_q_lora_norm=(11, B_Q_LORA), b_kv_lora_norm=(12, B_KV_LORA), c_sink=(13, C_HEADS))
SMALL_WEIGHTS = ("ada_b", "norm_w", "final_norm", "a_q_norm", "a_k_norm", "b_q_lora_norm", "b_kv_lora_norm", "c_sink")
LOSS_ROW = 14


def _pack_small(res):
    def padded(v):
        return jnp.concatenate([v, jnp.zeros((v.shape[0], D_MODEL - v.shape[1]), F32)], axis=1)

    rows = [res["dmod"].reshape(6, D_MODEL), res["norm_w"], res["final_norm"].reshape(1, D_MODEL)]
    rows += [padded(res[k]) for k in ("a_q_norm", "a_k_norm", "b_q_lora_norm", "b_kv_lora_norm", "c_sink")]
    return jnp.concatenate(rows + [res["loss_row"], jnp.zeros((1, D_MODEL), F32)], axis=0)


def _adam_small(parts, ws, ms, vs):
    nw = len(SMALL_WEIGHTS)

    def body(*refs):
        p_ref = refs[0]
        w_refs, m_refs, v_refs = refs[1:1 + nw], refs[1 + nw:1 + 2 * nw], refs[1 + 2 * nw:1 + 3 * nw]
        outs = refs[1 + 3 * nw:]
        g_all = p_ref[0]
        for k in range(1, N_DEV):
            g_all = g_all + p_ref[k]
        for idx, name in enumerate(SMALL_WEIGHTS):
            if name == "ada_b":
                g = jnp.concatenate([jnp.concatenate([g_all[3 * l + t:3 * l + t + 1] for t in range(3)], axis=1) for l in range(2)],
                                    axis=0)
            else:
                row, width = SMALL_ROWS[name]
                g = g_all[row:row + w_refs[idx].shape[0], 0:width]
            d, m_new, v_new = _adam_update(g, w_refs[idx][...], m_refs[idx][...], v_refs[idx][...])
            outs[4 * idx][...], outs[4 * idx + 1][...], outs[4 * idx + 2][...], outs[4 * idx + 3][...] = g, d, m_new, v_new
        outs[4 * nw][...] = g_all[LOSS_ROW:LOSS_ROW + 1, 0:LANES]

    out_shape = []
    for w in ws:
        out_shape += [_sds(w.shape, F32)] * 4
    out_shape.append(_sds((1, LANES), F32))
    return pl.pallas_call(body, name="adam_small", out_shape=out_shape,
                          compiler_params=pltpu.CompilerParams(vmem_limit_bytes=VMEM_LIMIT))(parts, *ws, *ms, *vs)


def _adam(parts, w, m, v, name):
    P, R, C = parts.shape
    tr = R if R <= ADAM_ROW_TILE else ADAM_ROW_TILE
    assert R % tr == 0

    def body(p_ref, w_ref, m_ref, v_ref, g_ref, d_ref, nm_ref, nv_ref):
        g = p_ref[0].astype(F32)
        for k in range(1, P):
            g = g + p_ref[k].astype(F32)
        g_ref[...] = g
        d_ref[...], nm_ref[...], nv_ref[...] = _adam_update(g, w_ref[...], m_ref[...], v_ref[...])

    spec = pl.BlockSpec((tr, C), lambda i: (i, 0))
    return pl.pallas_call(
        body, name=name, grid=(R // tr,),
        in_specs=[pl.BlockSpec((P, tr, C), lambda i: (0, i, 0)), spec, spec, spec],
        out_specs=[spec, spec, spec, spec], out_shape=[_sds((R, C), F32)] * 4,
        compiler_params=_params(("arbitrary",)),
    )(parts, w, m, v)


_ANY = pl.BlockSpec(memory_space=pl.ANY)
CHIP_FLIPS = ((1, 0), (0, 1), (1, 1))
DEV_FLIPS = tuple((dx, dy, dc) for dx in (0, 1) for dy in (0, 1) for dc in (0, 1) if dx + dy + dc)


def _flip(a, d):
    return a if d == 0 else 1 - a


def _my_place():
    return lax.axis_index("x"), lax.axis_index("y"), lax.axis_index("c")


def _gather8_copies(ins, outs, send_sems, recv_sems, loc_sems):
    x, y, c = _my_place()
    me = 4 * x + 2 * y + c
    copies = []
    for a in range(len(ins)):
        copies.append(pltpu.make_async_copy(ins[a], outs[a].at[me], loc_sems.at[a]))
        for k, (dx, dy, dc) in enumerate(DEV_FLIPS):
            copies.append(pltpu.make_async_remote_copy(
                src_ref=ins[a], dst_ref=outs[a].at[me], send_sem=send_sems.at[a, k], recv_sem=recv_sems.at[a, k],
                device_id=(_flip(x, dx), _flip(y, dy), _flip(c, dc)), device_id_type=MESH_ID))
    return copies


def _gather8_sems(n):
    return [pltpu.SemaphoreType.DMA((n, 7)), pltpu.SemaphoreType.DMA((n, 7)), pltpu.SemaphoreType.DMA((n,))]


def _gather_dev8(arrs, name):
    n = len(arrs)

    def body(*refs):
        copies = _gather8_copies(refs[:n], refs[n:2 * n], *refs[2 * n:])
        for cp in copies:
            cp.start()
        for cp in copies:
            cp.wait()

    return pl.pallas_call(
        body, name=name, in_specs=[_ANY] * n, out_specs=[_ANY] * n,
        out_shape=[_sds((N_DEV,) + a.shape, a.dtype) for a in arrs], scratch_shapes=_gather8_sems(n),
    )(*arrs)


class _Exchange:
    def __init__(self, arrs, out_shapes, n_sems, phases):
        self.arrs, self.out_shapes, self.n_sems, self._phases = list(arrs), list(out_shapes), n_sems, phases

    @property
    def n(self):
        return len(self.arrs)

    def sem_shapes(self):
        return [pltpu.SemaphoreType.DMA((self.n, self.n_sems)), pltpu.SemaphoreType.DMA((self.n, self.n_sems)),
                pltpu.SemaphoreType.DMA((self.n,))]

    def phases(self, ins, outs, sems):
        return self._phases(ins, outs, *sems)

    def run(self, name):
        n = self.n

        def body(*refs):
            start, mid, end = self.phases(refs[:n], refs[n:2 * n], refs[2 * n:])
            start()
            mid()
            end()

        return pl.pallas_call(body, name=name, in_specs=[_ANY] * n, out_specs=[_ANY] * n, out_shape=self.out_shapes,
                              scratch_shapes=self.sem_shapes())(*self.arrs)

def _gather_halves_phases(ins, outs, send_sems, recv_sems, loc_sems):
    n = len(ins)
    x, y, c = _my_place()
    chip = 2 * x + y
    sibling = (x, y, 1 - c)
    peers = [(_flip(x, dx), _flip(y, dy)) for dx, dy in CHIP_FLIPS]

    def remote(src, p, half, a, k, to):
        return pltpu.make_async_remote_copy(src_ref=src, dst_ref=outs[a].at[p, half], send_sem=send_sems.at[a, k],
                                            recv_sem=recv_sems.at[a, k], device_id=to, device_id_type=MESH_ID)

    def local(a):
        return pltpu.make_async_copy(ins[a], outs[a].at[chip], loc_sems.at[a])

    def first(a, k):
        return remote(ins[a].at[c], chip, c, a, k, (*peers[k], c))

    def passed(a, k):
        p = 2 * peers[k][0] + peers[k][1]
        return remote(outs[a].at[p, c], p, c, a, 3 + k, sibling)

    def start():
        for a in range(n):
            local(a).start()
            for k in range(3):
                first(a, k).start()

    def mid():
        for a in range(n):
            for k in range(3):
                p = 2 * peers[k][0] + peers[k][1]
                remote(outs[a].at[p, c], p, c, a, k, sibling).wait_recv()
                passed(a, k).start()

    def end():
        for a in range(n):
            for k in range(3):
                p = 2 * peers[k][0] + peers[k][1]
                remote(outs[a].at[p, 1 - c], p, 1 - c, a, 3 + k, sibling).wait_recv()
        for a in range(n):
            for k in range(3):
                first(a, k).wait_send()
                passed(a, k).wait_send()
            local(a).wait()

    return start, mid, end


def _gather_chip4_halves(arrs):
    return _Exchange(arrs, [_sds((N_CHIPS,) + a.shape, a.dtype) for a in arrs], 6, _gather_halves_phases)


def _reduce_phases(n_whole, ins, outs, send_sems, recv_sems, loc_sems):
    n = len(ins)
    x, y, c = _my_place()
    chip = 2 * x + y
    sibling = (x, y, 1 - c)
    peers = [(_flip(x, dx), _flip(y, dy)) for dx, dy in CHIP_FLIPS]

    def remote(src, slot, a, k, to):
        return pltpu.make_async_remote_copy(src_ref=src, dst_ref=outs[a].at[slot], send_sem=send_sems.at[a, k],
                                            recv_sem=recv_sems.at[a, k], device_id=to, device_id_type=MESH_ID)

    def block(a, p):
        return ins[a] if a >= n - n_whole else ins[a].at[p]

    def local(a):
        return pltpu.make_async_copy(block(a, chip), outs[a].at[2 * chip + c], loc_sems.at[a])

    def own(a):
        return remote(block(a, chip), 2 * chip + c, a, 0, sibling)

    def first(a, k):
        return remote(block(a, 2 * peers[k][0] + peers[k][1]), 2 * chip + c, a, 1 + k, (*peers[k], c))

    def passed(a, k):
        slot = 2 * (2 * peers[k][0] + peers[k][1]) + c
        return remote(outs[a].at[slot], slot, a, 4 + k, sibling)

    def start():
        for a in range(n):
            local(a).start()
            own(a).start()
            for k in range(3):
                first(a, k).start()

    def mid():
        for a in range(n):
            for k in range(3):
                slot = 2 * (2 * peers[k][0] + peers[k][1]) + c
                remote(outs[a].at[slot], slot, a, 1 + k, sibling).wait_recv()
                passed(a, k).start()

    def end():
        for a in range(n):
            remote(outs[a].at[2 * chip + 1 - c], 2 * chip + 1 - c, a, 0, sibling).wait_recv()
            for k in range(3):
                slot = 2 * (2 * peers[k][0] + peers[k][1]) + 1 - c
                remote(outs[a].at[slot], slot, a, 4 + k, sibling).wait_recv()
        for a in range(n):
            own(a).wait_send()
            for k in range(3):
                first(a, k).wait_send()
                passed(a, k).wait_send()
            local(a).wait()

    return start, mid, end


def _reduce_exchange(arrs, whole=()):
    shapes = [_sds((N_DEV,) + a.shape[1:], a.dtype) for a in arrs] + [_sds((N_DEV,) + a.shape, a.dtype) for a in whole]
    return _Exchange(list(arrs) + list(whole), shapes, 7, functools.partial(_reduce_phases, len(whole)))


def _even_in_layout(w):
    return jnp.concatenate([w[:, 0:1696], jnp.zeros((w.shape[0], 96), w.dtype), w[:, 1696:2208]], axis=1)


def _even_in_unlayout(g):
    return jnp.concatenate([g[:, 0:1696], g[:, 1792:2304]], axis=1)


def _uq_layout(w):
    per = B_NOPE + B_ROPE
    pad = jnp.zeros((w.shape[0], LANES - B_ROPE), w.dtype)
    nope = [w[:, per * h:per * h + B_NOPE] for h in range(B_HEADS)]
    rope = [jnp.concatenate([w[:, per * h + B_NOPE:per * (h + 1)], pad], axis=1) for h in range(B_HEADS)]
    return jnp.concatenate(nope + rope, axis=1)


def _uq_unlayout(g):
    parts = []
    for h in range(B_HEADS):
        parts += [g[:, B_NOPE * h:B_NOPE * (h + 1)], g[:, 512 + LANES * h:512 + LANES * h + B_ROPE]]
    return jnp.concatenate(parts, axis=1)


def _block_diag(blocks):
    rows = []
    for h, blk in enumerate(blocks):
        r, cdim = blk.shape
        n = len(blocks)
        rows.append(jnp.concatenate([jnp.zeros((r, cdim * h), blk.dtype), blk, jnp.zeros((r, cdim * (n - 1 - h)), blk.dtype)],
                                    axis=1))
    return jnp.concatenate(rows, axis=0)


def _uk_layout(w):
    return _block_diag([w[:, h, :].T for h in range(B_HEADS)])


def _uk_unlayout(g):
    return jnp.stack([g[B_NOPE * h:B_NOPE * (h + 1), LANES * h:LANES * (h + 1)].T for h in range(B_HEADS)], axis=1)


def _uv_layout(w):
    return _block_diag([w[:, h, :] for h in range(B_HEADS)])


def _uv_unlayout(g):
    return jnp.stack([g[LANES * h:LANES * (h + 1), B_V * h:B_V * (h + 1)] for h in range(B_HEADS)], axis=1)


def _rope_tables(S):
    inv = ROPE_THETA ** (-jnp.arange(0, 32, 2, dtype=F32) / 32)
    tok = jnp.arange(S)

    def tab(pos):
        ang = pos.astype(F32)[:, None] * inv[None, :]
        cos, sin = jnp.cos(ang), jnp.sin(ang)
        return jnp.concatenate([cos, cos], axis=1), jnp.concatenate([-sin, sin], axis=1)

    cr, sr = tab(tok // GRID_W)
    cc, sc = tab(tok % GRID_W)
    ct, st = tab(tok)
    return (jnp.tile(jnp.concatenate([cr, cc], axis=1), (1, 2)), jnp.tile(jnp.concatenate([sr, sc], axis=1), (1, 2)),
            jnp.tile(ct, (1, 4)), jnp.tile(st, (1, 4)))


A_TQ, A_TK, A_SUB = 512, 4096, 512
A_FWD_SUB = 1024
B_TQ, B_TK, B_SUB = 128, 4096, 1024
B_BWD_TK, B_BWD_SUB = 4096, 512
C_T = 256
C_BLOCKS_PER_STEP = 8
KV_SHARE = 2


def _local_step(x0, tgt, mod, norm_w, wie, wuq, wuk, wuv, late_shards, a_q_norm, a_k_norm, q_lora_norm, kv_lora_norm,
                c_sink, final_norm):
    S = x0.shape[0]
    mod3 = mod.reshape(2, 3, D_MODEL)
    ca, sa, ct, st = _rope_tables(S)
    lane_seg = np.arange(LANES) // HEAD_DIM
    seg = jnp.asarray((lane_seg[:, None] == lane_seg[None, :]).astype(np.float32)).astype(BF16)
    qn = jnp.tile(a_q_norm.reshape(1, HEAD_DIM), (1, 2))
    kn = jnp.tile(a_k_norm.reshape(1, HEAD_DIM), (1, 2))
    qln, kvln = q_lora_norm.reshape(1, B_Q_LORA), kv_lora_norm.reshape(1, B_KV_LORA)
    nw0, nw1 = norm_w[0:1], norm_w[1:2]
    gate0, gate1 = mod3[0, 2:3], mod3[1, 2:3]
    a_tq, a_tk, b_tq, b_tk, bb_tk, c_t = min(A_TQ, S), min(A_TK, S), min(B_TQ, S), min(B_TK, S), min(B_BWD_TK, S), min(C_T, S)
    a_sub, b_sub, bb_sub = min(A_SUB, a_tk), min(B_SUB, b_tk), min(B_BWD_SUB, bb_tk)

    h0, proj_e, qa, ka, va, qcat, kcat, ka_t, va_t, kcat_t = _even_pre_fwd(x0, mod3[0], nw0, wie, qn, kn, seg, ca, sa, ct, st,
                                                                           qln, kvln, wuq, wuk)
    oa, lse_a, woe_g, wio_g, woo_g = _pp_fwd(qa, ka, va_t, kdiv=KV_SHARE, tq=a_tq, tk=a_tk, sub=min(A_FWD_SUB, a_tk), name="attn_a_fwd",
                                             side=_gather_chip4_halves(late_shards))
    woe = woe_g.reshape(D_MODEL, D_MODEL)
    wio = wio_g.reshape(N_CHIPS, D_MODEL, ODD_IN // N_CHIPS)
    woo = woo_g.reshape(D_MODEL, D_MODEL)
    olat, lse_b = _mla_fwd(qcat, kcat, kcat_t, tq=b_tq, tk=b_tk, sub=b_sub)
    y0, x1 = _even_post_fwd(oa, olat, proj_e, x0, gate0, wuv, woe)
    h1, gc, qc, kc, vc, kc_t, vc_t = _odd_pre_fwd(x1, mod3[1], nw1, wio)
    slopes = 2.0 ** (-8.0 * jnp.arange(1, C_HEADS + 1, dtype=F32) / C_HEADS)
    slope_rows = jnp.repeat(slopes.reshape(C_HEADS // 2, 2), c_t, axis=1)[:, None, :]
    sink_rows = jnp.repeat(c_sink.reshape(C_HEADS // 2, 2), c_t, axis=1)[:, None, :]
    win_dist = _win_dist_table(S, c_t)
    oc, lse_c = _win_fwd(qc, kc, vc_t, win_dist, slope_rows, sink_rows, kdiv=KV_SHARE, tq=c_t, nbs=C_BLOCKS_PER_STEP,
                         name="attn_c_fwd")
    doc, dgc, dx2, dwoo, st_f = _odd_post(oc, gc, x1, gate1, woo, final_norm.reshape(1, D_MODEL), tgt)
    dqc, dkc, dvc, dsink_raw = _win_bwd(qc, kc, kc_t, vc, oc, doc, lse_c, win_dist, slope_rows, sink_rows, kdiv=KV_SHARE, tq=c_t,
                                        nbs=C_BLOCKS_PER_STEP, name="attn_c_bwd")
    dx1, dwio, st_1 = _odd_pre_bwd(dqc, dkc, dvc, dgc, h1, x1, dx2, mod3[1], nw1, wio)
    doa, dga, dgb, dolat, dwoe, dwuv, st_e = _even_post_bwd(dx1, y0, oa, olat, proj_e, gate0, wuv, woe)
    late_grads = _reduce_exchange([dwoe.reshape(N_CHIPS, D_MODEL // N_CHIPS, D_MODEL), dwio,
                                   dwoo.reshape(N_CHIPS, D_MODEL // N_CHIPS, D_MODEL)])
    dqa, dka, dva, p_woe, p_wio, p_woo = _pp_bwd(qa, ka, ka_t, va, oa, doa, lse_a, kdiv=KV_SHARE, tq=a_tq, tk=a_tk, sub=a_sub,
                                                 name="attn_a_bwd", side=late_grads)
    dqcat, dkcat = _mla_bwd(qcat, kcat, kcat_t, olat, dolat, lse_b, tq=b_tq, tk=bb_tk, sub=bb_sub)
    dx0, dwie, dwuq, dwuk, st_0, nst = _even_pre_bwd(x0, h0, proj_e, dqa, dka, dva, dga, dgb, dqcat, dkcat, dx1, mod3[0], nw0,
                                                     wie, qn, kn, seg, ca, sa, ct, st, qln, kvln, wuq, wuk)
    dsink_pairs = jnp.stack([dsink_raw[:, 0, 0], dsink_raw[:, 1, 0]], axis=1).reshape(C_HEADS)
    return dict(
        loss_row=st_f[2:3], dx=dx0,
        dmod=jnp.stack([jnp.concatenate([st_0[0], st_0[1], st_e[0]]), jnp.concatenate([st_1[0], st_1[1], st_f[1]])]),
        norm_w=jnp.stack([st_0[2], st_1[2]]), final_norm=st_f[0],
        a_q_norm=nst[0:1, 0:HEAD_DIM], a_k_norm=nst[1:2, 0:HEAD_DIM], b_q_lora_norm=nst[2:3, :], b_kv_lora_norm=nst[3:4, 0:LANES],
        c_sink=dsink_pairs.reshape(1, C_HEADS),
        even_w_in=dwie, b_w_uq=dwuq, b_w_uk=dwuk, b_w_uv=dwuv, even_w_out=p_woe, odd_w_in=p_wio, odd_w_out=p_woo)


WEIGHT_NAMES = ("norm_w", "ada_w", "ada_b", "even_w_in", "a_q_norm", "a_k_norm", "b_q_lora_norm", "b_kv_lora_norm", "b_w_uq",
                "b_w_uk", "b_w_uv", "even_w_out", "odd_w_in", "c_sink", "odd_w_out", "final_norm")


def _cols_to_chips(g):
    r, n4 = g.shape
    return jnp.transpose(g.reshape(r, N_CHIPS, n4 // N_CHIPS), (1, 0, 2))


def _chips_to_cols(g):
    p, r, n = g.shape
    return jnp.transpose(g, (1, 0, 2)).reshape(r, p * n)


def kernel(x, c, norm_w, ada_w, ada_b, even_w_in, a_q_norm, a_k_norm, b_q_lora_norm, b_kv_lora_norm, b_w_uq, b_w_uk, b_w_uv, even_w_out, odd_w_in, c_sink, odd_w_out, final_norm, loss_target, m_norm_w, m_ada_w, m_ada_b, m_even_w_in, m_a_q_norm, m_a_k_norm, m_b_q_lora_norm, m_b_kv_lora_norm, m_b_w_uq, m_b_w_uk, m_b_w_uv, m_even_w_out, m_odd_w_in, m_c_sink, m_odd_w_out, m_final_norm, v_norm_w, v_ada_w, v_ada_b, v_even_w_in, v_a_q_norm, v_a_k_norm, v_b_q_lora_norm, v_b_kv_lora_norm, v_b_w_uq, v_b_w_uk, v_b_w_uv, v_even_w_out, v_odd_w_in, v_c_sink, v_odd_w_out, v_final_norm):
    given = dict(locals())
    xi, yi, ci = _my_place()
    chip = 2 * xi + yi
    dev = 2 * chip + ci
    n_ada = ada_w.shape[2]

    (c_all,) = _gather_dev8([c], "gather_c")
    c_all = c_all.reshape(N_DEV, D_MODEL)
    bias = lax.dynamic_slice_in_dim(ada_b, chip * n_ada, n_ada, axis=1).reshape(2, 1, n_ada)
    mod_cols = _ada_fwd(c_all, ada_w, bias)
    def halves(w):
        return w.astype(BF16).reshape((2, w.shape[0] // 2) + w.shape[1:])

    mod_all, wie_g, wuq_g = _gather_chip4_halves([mod_cols, halves(even_w_in[0]), halves(b_w_uq[0])]).run("gather_weights")
    wie_g = wie_g.reshape(N_CHIPS, D_MODEL, EVEN_IN // N_CHIPS)
    wuq_g = wuq_g.reshape(N_CHIPS, B_Q_LORA, -1)
    mod = jnp.transpose(lax.dynamic_index_in_dim(mod_all, dev, axis=2, keepdims=False), (1, 0, 2)).reshape(2, 3 * D_MODEL)

    res = _local_step(
        x[0], loss_target[0], mod, norm_w,
        _even_in_layout(_chips_to_cols(wie_g)), _uq_layout(_chips_to_cols(wuq_g)), _uk_layout(b_w_uk[0].astype(BF16)),
        _uv_layout(b_w_uv[0].astype(BF16)), [halves(even_w_out[0]), halves(odd_w_in[0]), halves(odd_w_out[0])],
        a_q_norm, a_k_norm, b_q_lora_norm, b_kv_lora_norm, c_sink, final_norm)

    latent = jnp.stack([_uk_unlayout(res["b_w_uk"]).reshape(B_KV_LORA, 512),
                        _uv_unlayout(res["b_w_uv"]).reshape(B_KV_LORA, 512)]).astype(BF16)
    p_wie, p_wuq, small_all, latent_all = _reduce_exchange(
        [_cols_to_chips(_even_in_unlayout(res["even_w_in"])), _cols_to_chips(_uq_unlayout(res["b_w_uq"].astype(BF16)))],
        whole=[_pack_small(res), latent]).run("reduce_exchange")
    shard_parts = dict(even_w_in=p_wie, b_w_uq=p_wuq, **{k: res[k] for k in ("even_w_out", "odd_w_in", "odd_w_out")})
    dmod_all = small_all[:, 0:6, :].reshape(N_DEV, 2, 3 * D_MODEL)
    dmod_cols = jnp.transpose(lax.dynamic_slice_in_dim(dmod_all, chip * n_ada, n_ada, axis=2), (1, 0, 2))
    parts = dict(shard_parts)
    parts["ada_w"] = _ada_bwd(c_all.T, dmod_cols).reshape(1, 2 * D_MODEL, n_ada)
    parts["b_w_uk"], parts["b_w_uv"] = latent_all[:, 0], latent_all[:, 1]

    def as2d(a):
        return a.reshape((-1, a.shape[-1]) if a.ndim > 1 else (1, a.shape[0]))

    results = {}
    small_outs = _adam_small(small_all, *[[as2d(given[pre + k]) for k in SMALL_WEIGHTS] for pre in ("", "m_", "v_")])
    for idx, k in enumerate(SMALL_WEIGHTS):
        results[k] = small_outs[4 * idx:4 * idx + 4]
    for k, p in parts.items():
        shape2 = (p.shape[-2], p.shape[-1])
        results[k] = _adam(p, given[k].reshape(shape2), given["m_" + k].reshape(shape2), given["v_" + k].reshape(shape2),
                           "adam_" + k)
    by_kind = [[results[k][t].reshape(given[k].shape) for k in WEIGHT_NAMES] for t in range(4)]
    return (small_outs[-1][0, 0], res["dx"][None], *by_kind[0], *by_kind[1], *by_kind[2], *by_kind[3])
```

```python
import functools

import numpy as np
import jax
import jax.numpy as jnp
from jax import lax
from jax.experimental import pallas as pl
from jax.experimental.pallas import tpu as pltpu

F32 = jnp.float32
BF16 = jnp.bfloat16
HIGHEST = lax.Precision.HIGHEST
MESH_ID = pl.DeviceIdType.MESH

D_MODEL = 1024
HEAD_DIM = 64
GRID_W = 64
EPS = 1e-6
ROPE_THETA = 10000.0
B_HEADS, B_NOPE, B_ROPE, B_V = 8, 64, 32, 64
B_Q_LORA, B_KV_LORA = 256, 128
C_HEADS = 16
WINDOW = 128
EVEN_IN, ODD_IN = 2208, 2560
EVEN_P = 2304
N_CHIPS, N_DEV = 4, 8
LANES = 128
NEG = -1e30
VMEM_LIMIT = 60 * 1024 * 1024

ADAM_LR, ADAM_B1, ADAM_B2, ADAM_EPS, ADAM_WD, ADAM_STEP = 0.001, 0.9, 0.999, 1e-08, 0.01, 10

ROW_TILE = 512
IN_PROJ_ROW_TILE = 256


def _dot(a, b):
    return lax.dot_general(a, b, (((1,), (0,)), ((), ())), preferred_element_type=F32)


def _dot_nt(a, b):
    return lax.dot_general(a, b, (((1,), (1,)), ((), ())), preferred_element_type=F32)


def _dot_tn(a, b):
    return lax.dot_general(a, b, (((0,), (0,)), ((), ())), preferred_element_type=F32)


def _dot_f32(a, b):
    return lax.dot_general(a, b, (((1,), (0,)), ((), ())), precision=HIGHEST, preferred_element_type=F32)


def _sigmoid(x):
    return 1.0 / (1.0 + jnp.exp(-x))


def _silu_and_grad(g):
    s = _sigmoid(g)
    return g * s, s * (1.0 + g * (1.0 - s))


def _lane_iota():
    return lax.broadcasted_iota(jnp.int32, (1, LANES), 1)


def _partner(x, lane):
    return jnp.where((lane % 32) < 16, pltpu.roll(x, LANES - 16, 1), pltpu.roll(x, 16, 1))


def _rot(x, cos, sin_signed, lane):
    return x * cos + _partner(x, lane) * sin_signed


def _rot_bwd(dy, cos, sin_signed, lane):
    return dy * cos + _partner(dy * sin_signed, lane)


def _rms(x):
    return lax.rsqrt(jnp.mean(x * x, axis=-1, keepdims=True) + EPS)


def _rms_bwd(x, r, g):
    return r * g - x * (r * r * r) * jnp.mean(x * g, axis=-1, keepdims=True)


def _seg_mean(v, seg_ones):
    hi = v.astype(BF16)
    lo = (v - hi.astype(F32)).astype(BF16)
    return (_dot(hi, seg_ones) + _dot(lo, seg_ones)) * (1.0 / HEAD_DIM)


def _dup_heads(x, lane):
    swapped = pltpu.roll(x, HEAD_DIM, 1)
    lo = lane < HEAD_DIM
    return jnp.concatenate([jnp.where(lo, x, swapped), jnp.where(lo, swapped, x)], axis=1)


def _fold_heads(x2, lane):
    a, b = x2[:, 0:LANES], x2[:, LANES:2 * LANES]
    return jnp.where(lane < HEAD_DIM, a + pltpu.roll(a, HEAD_DIM, 1), b + pltpu.roll(b, HEAD_DIM, 1))


def _row_spec(ts, cols):
    return pl.BlockSpec((ts, cols), lambda i: (i, 0))


def _full_spec(shape, single=True):
    nd = len(shape)
    if single:
        return pl.BlockSpec(shape, lambda i: (0,) * nd, pipeline_mode=pl.Buffered(1))
    return pl.BlockSpec(shape, lambda i: (0,) * nd)


def _sds(shape, dtype):
    return jax.ShapeDtypeStruct(shape, dtype)


def _params(sem):
    return pltpu.CompilerParams(dimension_semantics=sem, vmem_limit_bytes=VMEM_LIMIT)


def _even_pre_fwd(x, mod, nw, wie, qn, kn, seg, ca, sa, ct, st, qln, kvln, wuq, wuk):
    S = x.shape[0]
    ts = min(IN_PROJ_ROW_TILE, S)

    def body(x_ref, mod_ref, nw_ref, wie_ref, qn_ref, kn_ref, seg_ref, ca_ref, sa_ref, ct_ref, st_ref, qln_ref,
             kvln_ref, wuq_ref, wuk_ref, h_ref, proj_ref, qa_ref, ka_ref, va_ref, qcat_ref, kcat_ref, kat_ref, vat_ref, kcatt_ref):
        xv = x_ref[...]
        h = (xv * _rms(xv) * nw_ref[...]) * (1.0 + mod_ref[1:2, :]) + mod_ref[0:1, :]
        hb = h.astype(BF16)
        h_ref[...] = hb
        proj = _dot(hb, wie_ref[...])
        proj_ref[...] = proj
        lane = _lane_iota()
        ca_v, sa_v, ct_v, st_v = ca_ref[...], sa_ref[...], ct_ref[...], st_ref[...]
        seg_v = seg_ref[...]
        for cb in range(4):
            xc = proj[:, LANES * cb:LANES * (cb + 1)]
            r = lax.rsqrt(_seg_mean(xc * xc, seg_v) + EPS)
            y = _rot(xc * r * qn_ref[...], ca_v, sa_v, lane)
            qa_ref[:, LANES * cb:LANES * (cb + 1)] = (y * 0.125).astype(BF16)
        kc = proj[:, 512:640]
        r = lax.rsqrt(_seg_mean(kc * kc, seg_v) + EPS)
        ka_v = _dup_heads(_rot(kc * r * kn_ref[...], ca_v, sa_v, lane), lane)
        ka_ref[...] = ka_v.astype(BF16)
        kat_ref[...] = ka_v.T.astype(BF16)
        va_v = _dup_heads(proj[:, 640:768], lane)
        va_ref[...] = va_v.astype(BF16)
        vat_ref[...] = va_v.T.astype(BF16)
        cq = proj[:, 1280:1536]
        cqn = (cq * _rms(cq) * qln_ref[...]).astype(BF16)
        ckv = proj[:, 1536:1664]
        ckvn = ckv * _rms(ckv) * kvln_ref[...]
        qb = _dot(cqn, wuq_ref[...])
        qlat = _dot(qb[:, 0:512].astype(BF16), wuk_ref[...])
        for hh in range(B_HEADS):
            qcat_ref[hh, :, 0:LANES] = qlat[:, LANES * hh:LANES * (hh + 1)].astype(BF16)
            qr = _rot(qb[:, 512 + LANES * hh:512 + LANES * (hh + 1)], ct_v, st_v, lane)
            qcat_ref[hh, :, LANES:2 * LANES] = qr.astype(BF16)
        kr = _rot(proj[:, 1664:1792], ct_v, st_v, lane)
        kcat_ref[:, 0:LANES] = ckvn.astype(BF16)
        kcat_ref[:, LANES:2 * LANES] = kr.astype(BF16)
        kcatt_ref[0:LANES, :] = ckvn.T.astype(BF16)
        kcatt_ref[LANES:2 * LANES, :] = kr.T.astype(BF16)

    col_spec = lambda rows: pl.BlockSpec((rows, ts), lambda i: (0, i))
    return pl.pallas_call(
        body, name="even_pre_fwd", grid=(S // ts,),
        in_specs=[_row_spec(ts, D_MODEL), _full_spec((3, D_MODEL)), _full_spec((1, D_MODEL)), _full_spec((D_MODEL, EVEN_P)),
                  _full_spec((1, LANES)), _full_spec((1, LANES)), _full_spec((LANES, LANES)),
                  _row_spec(ts, LANES), _row_spec(ts, LANES), _row_spec(ts, LANES), _row_spec(ts, LANES),
                  _full_spec((1, B_Q_LORA)), _full_spec((1, B_KV_LORA)), _full_spec((B_Q_LORA, 1536)), _full_spec((512, 1024))],
        out_specs=[_row_spec(ts, D_MODEL), _row_spec(ts, EVEN_P), _row_spec(ts, 512), _row_spec(ts, 2 * LANES), _row_spec(ts, 2 * LANES),
                   pl.BlockSpec((B_HEADS, ts, 2 * LANES), lambda i: (0, i, 0)), _row_spec(ts, 2 * LANES),
                   col_spec(2 * LANES), col_spec(2 * LANES), col_spec(2 * LANES)],
        out_shape=[_sds((S, D_MODEL), BF16), _sds((S, EVEN_P), F32), _sds((S, 512), BF16), _sds((S, 2 * LANES), BF16),
                   _sds((S, 2 * LANES), BF16), _sds((B_HEADS, S, 2 * LANES), BF16), _sds((S, 2 * LANES), BF16),
                   _sds((2 * LANES, S), BF16), _sds((2 * LANES, S), BF16), _sds((2 * LANES, S), BF16)],
        compiler_params=_params(("arbitrary",)),
    )(x, mod, nw, wie, qn, kn, seg, ca, sa, ct, st, qln, kvln, wuq, wuk)


MLA_SCALE = (B_NOPE + B_ROPE) ** -0.5
LOG2E = 1.4426950408889634

def _row_lo():
    return lax.broadcasted_iota(jnp.int32, (LANES, 1), 0) < HEAD_DIM


def _stack_cols(vT, rlo):
    zero = jnp.zeros_like(vT)
    return jnp.concatenate([jnp.where(rlo, vT, zero), jnp.where(rlo, zero, vT)], axis=1)


def _stack_rows(v, lo):
    zero = jnp.zeros_like(v)
    return jnp.concatenate([jnp.where(lo, v, zero), jnp.where(lo, zero, v)], axis=0)


def _pick_halves_T(xT, rlo, t):
    return jnp.where(rlo, xT[:, 0:t], xT[:, t:2 * t]).T


def _side_split(refs, n_in, n_out, n_scratch, side):
    ns = side.n if side is not None else 0
    cuts = np.cumsum([0, n_in, ns, n_out, ns, n_scratch])
    return [refs[a:b] for a, b in zip(cuts[:-1], cuts[1:])] + [refs[cuts[-1]:]]


def _side_hooks(side, side_ins, side_outs, side_sems, step, total):
    if side is None:
        return lambda: None
    start, mid, end = side.phases(side_ins, side_outs, side_sems)
    pl.when(step == 0)(start)
    pl.when(step == total // 2)(mid)
    return lambda: pl.when(step == total - 1)(end)


def _side_specs(side):
    if side is None:
        return [], [], [], [], []
    return list(side.arrs), [_ANY] * side.n, [_ANY] * side.n, list(side.out_shapes), side.sem_shapes()


def _pp_fwd(q, k, vT, *, kdiv, tq, tk, sub, name, side=None):
    S = k.shape[0]; nb = q.shape[1] // LANES; nq = S // tq; nkv = S // tk; nsub = tk // sub

    def body(*refs):
        (q_ref, k_ref, vT_ref), side_ins, (o_ref, lse_ref), side_outs, (qs, m_s, l_s, acc), side_sems = _side_split(refs, 3, 2, 4, side)
        j = pl.program_id(2)
        rlo = _row_lo()
        step = (pl.program_id(0) * nq + pl.program_id(1)) * nkv + j
        side_end = _side_hooks(side, side_ins, side_outs, side_sems, step, nb * nq * nkv)

        @pl.when(j == 0)
        def _():
            qs[...] = _stack_cols(q_ref[...].astype(F32).T, rlo).astype(BF16)
            m_s[...] = jnp.full((1, 2 * tq), NEG, F32)
            l_s[...] = jnp.zeros((1, 2 * tq), F32)
            acc[...] = jnp.zeros((LANES, 2 * tq), F32)

        qsv = qs[...]
        m, l, a = m_s[...], l_s[...], acc[...]
        s_cur = _dot(k_ref[0:sub, :], qsv)
        for t in range(nsub):
            if t + 1 < nsub:
                s_next = _dot(k_ref[sub * (t + 1):sub * (t + 2), :], qsv)
            m_new = jnp.maximum(m, jnp.max(s_cur, axis=0, keepdims=True))
            alpha = jnp.exp(m - m_new)
            p = jnp.exp(s_cur - m_new)
            l = alpha * l + jnp.sum(p, axis=0, keepdims=True)
            a = alpha * a + _dot(vT_ref[:, sub * t:sub * (t + 1)], p.astype(BF16))
            m = m_new
            if t + 1 < nsub:
                s_cur = s_next
        m_s[...], l_s[...], acc[...] = m, l, a

        @pl.when(j == nkv - 1)
        def _():
            l_f = l_s[...]
            o_ref[...] = _pick_halves_T(acc[...] / l_f, rlo, tq).astype(BF16)
            lse_ref[0, 0] = m_s[...] + jnp.log(l_f)

        side_end()

    s_args, s_in, s_out, s_shapes, s_sems = _side_specs(side)
    return pl.pallas_call(
        body, name=name, grid=(nb, nq, nkv),
        in_specs=[pl.BlockSpec((tq, LANES), lambda b, i, j: (i, b)), pl.BlockSpec((tk, LANES), lambda b, i, j: (j, b // kdiv)),
                  pl.BlockSpec((LANES, tk), lambda b, i, j: (b // kdiv, j))] + s_in,
        out_specs=[pl.BlockSpec((tq, LANES), lambda b, i, j: (i, b)),
                   pl.BlockSpec((1, 1, 1, 2 * tq), lambda b, i, j: (b, i, 0, 0))] + s_out,
        out_shape=[_sds((S, nb * LANES), BF16), _sds((nb, nq, 1, 2 * tq), F32)] + s_shapes,
        scratch_shapes=[pltpu.VMEM((LANES, 2 * tq), BF16), pltpu.VMEM((1, 2 * tq), F32), pltpu.VMEM((1, 2 * tq), F32),
                        pltpu.VMEM((LANES, 2 * tq), F32)] + s_sems,
        compiler_params=_params(("arbitrary",) * 3))(q, k, vT, *s_args)


def _pp_bwd(q, k, kT, v, o, do, lse, *, kdiv, tq, tk, sub, name, side=None):
    S = k.shape[0]; nb = q.shape[1] // LANES; nkb = k.shape[1] // LANES; nq = S // tq; nkv = S // tk; nsub = tk // sub

    def body(*refs):
        ((q_ref, k_ref, kT_ref, v_ref, o_ref, do_ref, lse_ref), side_ins, (dq_ref, dk_ref, dv_ref), side_outs,
         (qsT, qs, dosT, dos, delta_s, dq_acc), side_sems) = _side_split(refs, 7, 3, 6, side)
        b, i, j = pl.program_id(0), pl.program_id(1), pl.program_id(2)
        rlo = _row_lo()
        lo = lax.broadcasted_iota(jnp.int32, (1, LANES), 1) < HEAD_DIM
        side_end = _side_hooks(side, side_ins, side_outs, side_sems, (b * nq + i) * nkv + j, nb * nq * nkv)

        @pl.when((b % kdiv == 0) & (i == 0) & (j == 0))
        def _():
            dk_ref[...] = jnp.zeros((S, LANES), F32)
            dv_ref[...] = jnp.zeros((S, LANES), F32)

        @pl.when(j == 0)
        def _():
            qv = q_ref[...]
            qs[...] = _stack_rows(qv, lo)
            qsT[...] = _stack_cols(qv.astype(F32).T, rlo).astype(BF16)
            dov = do_ref[...].astype(F32)
            dos[...] = _stack_rows(dov.astype(BF16), lo)
            dosT[...] = _stack_cols(dov.T, rlo).astype(BF16)
            prodT = (dov * o_ref[...].astype(F32)).T
            delta_s[...] = jnp.concatenate([jnp.sum(jnp.where(rlo, prodT, 0.0), axis=0, keepdims=True),
                                            jnp.sum(jnp.where(rlo, 0.0, prodT), axis=0, keepdims=True)], axis=1)
            dq_acc[...] = jnp.zeros((LANES, 2 * tq), F32)

        qsTv, dosTv, qsv, dosv = qsT[...], dosT[...], qs[...], dos[...]
        lse_v, delta_v = lse_ref[0, 0], delta_s[...]
        dqa = dq_acc[...]
        s_cur = _dot(k_ref[0:sub, :], qsTv)
        dp_cur = _dot(v_ref[0:sub, :], dosTv)
        for t in range(nsub):
            if t + 1 < nsub:
                s_next = _dot(k_ref[sub * (t + 1):sub * (t + 2), :], qsTv)
                dp_next = _dot(v_ref[sub * (t + 1):sub * (t + 2), :], dosTv)
            p = jnp.exp(s_cur - lse_v)
            ds = (p * (dp_cur - delta_v)).astype(BF16)
            rows = pl.ds(pl.multiple_of(j * tk + sub * t, sub), sub)
            dv_ref[rows, :] += _dot(p.astype(BF16), dosv)
            dk_ref[rows, :] += _dot(ds, qsv)
            dqa = dqa + _dot(kT_ref[:, sub * t:sub * (t + 1)], ds)
            if t + 1 < nsub:
                s_cur, dp_cur = s_next, dp_next
        dq_acc[...] = dqa

        @pl.when(j == nkv - 1)
        def _():
            dq_ref[...] = _pick_halves_T(dq_acc[...], rlo, tq)

        side_end()

    qmap = lambda b, i, j: (i, b)
    kmap = lambda b, i, j: (j, b // kdiv)
    res = lambda b, i, j: (0, b // kdiv)
    s_args, s_in, s_out, s_shapes, s_sems = _side_specs(side)
    return pl.pallas_call(
        body, name=name, grid=(nb, nq, nkv),
        in_specs=[pl.BlockSpec((tq, LANES), qmap), pl.BlockSpec((tk, LANES), kmap), pl.BlockSpec((LANES, tk), lambda b, i, j: (b // kdiv, j)),
                  pl.BlockSpec((tk, LANES), kmap), pl.BlockSpec((tq, LANES), qmap), pl.BlockSpec((tq, LANES), qmap),
                  pl.BlockSpec((1, 1, 1, 2 * tq), lambda b, i, j: (b, i, 0, 0))] + s_in,
        out_specs=[pl.BlockSpec((tq, LANES), qmap), pl.BlockSpec((S, LANES), res), pl.BlockSpec((S, LANES), res)] + s_out,
        out_shape=[_sds((S, nb * LANES), F32), _sds((S, nkb * LANES), F32), _sds((S, nkb * LANES), F32)] + s_shapes,
        scratch_shapes=[pltpu.VMEM((LANES, 2 * tq), BF16), pltpu.VMEM((2 * tq, LANES), BF16), pltpu.VMEM((LANES, 2 * tq), BF16),
                        pltpu.VMEM((2 * tq, LANES), BF16), pltpu.VMEM((1, 2 * tq), F32), pltpu.VMEM((LANES, 2 * tq), F32)] + s_sems,
        compiler_params=_params(("arbitrary",) * 3))(q, k, kT, v, o, do, lse, *s_args)


MLA_C = MLA_SCALE * LOG2E


def _mla_fwd(q, kcat, kcatT, *, tq, tk, sub):
    S = kcat.shape[0]; nq, nkv = S // tq, S // tk; R = B_HEADS * tq; nsub = tk // sub

    def body(q_ref, k_ref, vT_ref, o_ref, lse_ref, qT, m_s, l_s, acc):
        j = pl.program_id(1)

        @pl.when(j == 0)
        def _():
            qT[...] = q_ref[...].reshape(R, 2 * LANES).astype(F32).T.astype(BF16)
            m_s[...] = jnp.full((1, R), NEG, F32)
            l_s[...] = jnp.zeros((1, R), F32)
            acc[...] = jnp.zeros((LANES, R), F32)

        qTv = qT[...]
        m, l, a = m_s[...], l_s[...], acc[...]
        s_cur = _dot(k_ref[0:sub, :], qTv)
        for t in range(nsub):
            if t + 1 < nsub:
                s_next = _dot(k_ref[sub * (t + 1):sub * (t + 2), :], qTv)
            m_new = jnp.maximum(m, jnp.max(s_cur, axis=0, keepdims=True))
            alpha = jnp.exp2((m - m_new) * MLA_C)
            p = jnp.exp2((s_cur - m_new) * MLA_C)
            l = alpha * l + jnp.sum(p, axis=0, keepdims=True)
            a = alpha * a + _dot(vT_ref[:, sub * t:sub * (t + 1)], p.astype(BF16))
            m = m_new
            if t + 1 < nsub:
                s_cur = s_next
        m_s[...], l_s[...], acc[...] = m, l, a

        @pl.when(j == nkv - 1)
        def _():
            l_f = l_s[...]
            o_ref[...] = (acc[...] / l_f).T.reshape(B_HEADS, tq, LANES).astype(BF16)
            lse_ref[0] = m_s[...] * MLA_SCALE + jnp.log(l_f)

    return pl.pallas_call(
        body, name="mla_fwd", grid=(nq, nkv),
        in_specs=[pl.BlockSpec((B_HEADS, tq, 2 * LANES), lambda i, j: (0, i, 0)), pl.BlockSpec((tk, 2 * LANES), lambda i, j: (j, 0)),
                  pl.BlockSpec((LANES, tk), lambda i, j: (0, j))],
        out_specs=[pl.BlockSpec((B_HEADS, tq, LANES), lambda i, j: (0, i, 0)), pl.BlockSpec((1, 1, R), lambda i, j: (i, 0, 0))],
        out_shape=[_sds((B_HEADS, S, LANES), BF16), _sds((nq, 1, R), F32)],
        scratch_shapes=[pltpu.VMEM((2 * LANES, R), BF16), pltpu.VMEM((1, R), F32), pltpu.VMEM((1, R), F32), pltpu.VMEM((LANES, R), F32)],
        compiler_params=_params(("arbitrary", "arbitrary")))(q, kcat, kcatT)


def _mla_bwd(q, kcat, kcatT, o, do, lse, *, tq, tk, sub):
    S = kcat.shape[0]; nq, nkv = S // tq, S // tk; R = B_HEADS * tq; nsub = tk // sub

    def body(q_ref, k_ref, kT_ref, o_ref, do_ref, lse_ref, dq_ref, dk_ref, qT, dosT, dos, delta_s, dq_acc):
        i, j = pl.program_id(0), pl.program_id(1)

        @pl.when((i == 0) & (j == 0))
        def _():
            dk_ref[...] = jnp.zeros((S, 2 * LANES), F32)

        @pl.when(j == 0)
        def _():
            qT[...] = q_ref[...].reshape(R, 2 * LANES).astype(F32).T.astype(BF16)
            dov = do_ref[...].reshape(R, LANES).astype(F32)
            dos[...] = dov.astype(BF16)
            dosT[...] = dov.T.astype(BF16)
            delta_s[...] = jnp.sum((dov * o_ref[...].reshape(R, LANES).astype(F32)).T, axis=0, keepdims=True)
            dq_acc[...] = jnp.zeros((2 * LANES, R), F32)

        qTv, dosTv, dosv = qT[...], dosT[...], dos[...]
        qv = q_ref[...].reshape(R, 2 * LANES)
        lse_v, delta_v = lse_ref[0] * LOG2E, delta_s[...]
        dqa = dq_acc[...]
        s_cur = _dot(k_ref[0:sub, :], qTv)
        dp_cur = _dot(k_ref[0:sub, 0:LANES], dosTv)
        for t in range(nsub):
            if t + 1 < nsub:
                s_next = _dot(k_ref[sub * (t + 1):sub * (t + 2), :], qTv)
                dp_next = _dot(k_ref[sub * (t + 1):sub * (t + 2), 0:LANES], dosTv)
            p = jnp.exp2(s_cur * MLA_C - lse_v)
            ds = (p * (dp_cur - delta_v) * MLA_SCALE).astype(BF16)
            rows = pl.ds(pl.multiple_of(j * tk + sub * t, sub), sub)
            dk_ref[rows, :] += _dot(ds, qv)
            dk_ref[rows, 0:LANES] += _dot(p.astype(BF16), dosv)
            dqa = dqa + _dot(kT_ref[:, sub * t:sub * (t + 1)], ds)
            if t + 1 < nsub:
                s_cur, dp_cur = s_next, dp_next
        dq_acc[...] = dqa

        @pl.when(j == nkv - 1)
        def _():
            dq_ref[...] = dq_acc[...].T.reshape(B_HEADS, tq, 2 * LANES)

    hspec = lambda w: pl.BlockSpec((B_HEADS, tq, w), lambda i, j: (0, i, 0))
    return pl.pallas_call(
        body, name="mla_bwd", grid=(nq, nkv),
        in_specs=[hspec(2 * LANES), pl.BlockSpec((tk, 2 * LANES), lambda i, j: (j, 0)), pl.BlockSpec((2 * LANES, tk), lambda i, j: (0, j)),
                  hspec(LANES), hspec(LANES), pl.BlockSpec((1, 1, R), lambda i, j: (i, 0, 0))],
        out_specs=[hspec(2 * LANES), pl.BlockSpec((S, 2 * LANES), lambda i, j: (0, 0))],
        out_shape=[_sds((B_HEADS, S, 2 * LANES), F32), _sds((S, 2 * LANES), F32)],
        scratch_shapes=[pltpu.VMEM((2 * LANES, R), BF16), pltpu.VMEM((LANES, R), BF16), pltpu.VMEM((R, LANES), BF16),
                        pltpu.VMEM((1, R), F32), pltpu.VMEM((2 * LANES, R), F32)],
        compiler_params=_params(("arbitrary", "arbitrary")))(q, kcat, kcatT, o, do, lse)


def _win_start(i, tq, nk, S):
    return pl.multiple_of(jnp.clip(i * tq - WINDOW, 0, S - nk), LANES)


def _win_dist_table(S, tq):
    nk = min(tq + 2 * WINDOW, S)
    nq = S // tq
    r = np.arange(nk)[:, None]
    c = (np.arange(2 * tq) % tq)[None, :]
    tabs = []
    for rel in (0, WINDOW, (nq - 1) * tq - (S - nk)):
        dist = np.abs(rel + c - r).astype(np.float32)
        tabs.append(np.where(dist <= WINDOW, dist, np.float32(1e32)))
    return jnp.asarray(np.stack(tabs))


def _win_dist_spec(nk, tq, nq):
    return pl.BlockSpec((1, nk, 2 * tq), lambda b, i: (jnp.where(i == 0, 0, jnp.where(i == nq - 1, 2, 1)), 0, 0))


def _win_fwd(q, k, vT, dist, slope, sink, *, kdiv, tq, nbs, name):
    S = k.shape[0]; nb = q.shape[1] // LANES; nq = S // tq; nk = min(tq + 2 * WINDOW, S)
    assert nb % nbs == 0 and nbs % kdiv == 0
    kvw = (nbs // kdiv) * LANES

    def body(q_ref, k_ref, vT_ref, dist_ref, slope_ref, sink_ref, o_ref, lse_ref):
        i = pl.program_id(1)
        rlo = _row_lo()
        k0 = _win_start(i, tq, nk, S)
        kk, vv, dd = k_ref[pl.ds(k0, nk), :], vT_ref[:, pl.ds(k0, nk)], dist_ref[0]
        for u in range(nbs):
            kv = slice(LANES * (u // kdiv), LANES * (u // kdiv + 1))
            qsT = _stack_cols(q_ref[:, LANES * u:LANES * (u + 1)].astype(F32).T, rlo).astype(BF16)
            s = _dot(kk[:, kv], qsT) - slope_ref[u] * dd
            sk = sink_ref[u]
            m = jnp.maximum(jnp.max(s, axis=0, keepdims=True), sk)
            p = jnp.exp(s - m)
            l = jnp.sum(p, axis=0, keepdims=True) + jnp.exp(sk - m)
            o_ref[:, LANES * u:LANES * (u + 1)] = _pick_halves_T(_dot(vv[kv, :], p.astype(BF16)) / l, rlo, tq).astype(BF16)
            lse_ref[u, 0] = m + jnp.log(l)

    row_spec = pl.BlockSpec((nbs, 1, 2 * tq), lambda b, i: (b, 0, 0))
    return pl.pallas_call(
        body, name=name, grid=(nb // nbs, nq),
        in_specs=[pl.BlockSpec((tq, nbs * LANES), lambda b, i: (i, b)), pl.BlockSpec((S, kvw), lambda b, i: (0, b)),
                  pl.BlockSpec((kvw, S), lambda b, i: (b, 0)), _win_dist_spec(nk, tq, nq), row_spec, row_spec],
        out_specs=[pl.BlockSpec((tq, nbs * LANES), lambda b, i: (i, b)), pl.BlockSpec((nbs, 1, 1, 2 * tq), lambda b, i: (b, i, 0, 0))],
        out_shape=[_sds((S, nb * LANES), BF16), _sds((nb, nq, 1, 2 * tq), F32)],
        compiler_params=_params(("arbitrary", "arbitrary")))(q, k, vT, dist, slope, sink)


def _win_bwd(q, k, kT, v, o, do, lse, dist, slope, sink, *, kdiv, tq, nbs, name):
    S = k.shape[0]; nb = q.shape[1] // LANES; nkb = k.shape[1] // LANES; nq = S // tq; nk = min(tq + 2 * WINDOW, S)
    assert nb % nbs == 0 and nbs % kdiv == 0
    nkv = nbs // kdiv
    kvw = nkv * LANES

    def body(q_ref, k_ref, kT_ref, v_ref, o_ref, do_ref, lse_ref, dist_ref, slope_ref, sink_ref, dq_ref, dk_ref, dv_ref, dsink_ref, ds_acc):
        i = pl.program_id(1)
        rlo = _row_lo()
        lo = lax.broadcasted_iota(jnp.int32, (1, LANES), 1) < HEAD_DIM

        @pl.when(i == 0)
        def _():
            dk_ref[...] = jnp.zeros((S, kvw), F32)
            dv_ref[...] = jnp.zeros((S, kvw), F32)
            ds_acc[...] = jnp.zeros((nbs, 2 * tq), F32)

        k0 = _win_start(i, tq, nk, S)
        rows = pl.ds(k0, nk)
        kk_all, vv_all, kkT_all, dd = k_ref[rows, :], v_ref[rows, :], kT_ref[:, rows], dist_ref[0]
        dv_sum, dk_sum = [None] * nkv, [None] * nkv
        for u in range(nbs):
            g = u // kdiv
            kv = slice(LANES * g, LANES * (g + 1))
            kk, vv, kkT = kk_all[:, kv], vv_all[:, kv], kkT_all[kv, :]
            cols = slice(LANES * u, LANES * (u + 1))
            qv = q_ref[:, cols]
            qs = _stack_rows(qv, lo)
            qsT = _stack_cols(qv.astype(F32).T, rlo).astype(BF16)
            dov = do_ref[:, cols].astype(F32)
            dos = _stack_rows(dov.astype(BF16), lo)
            dosT = _stack_cols(dov.T, rlo).astype(BF16)
            prodT = (dov * o_ref[:, cols].astype(F32)).T
            delta = jnp.concatenate([jnp.sum(jnp.where(rlo, prodT, 0.0), axis=0, keepdims=True),
                                     jnp.sum(jnp.where(rlo, 0.0, prodT), axis=0, keepdims=True)], axis=1)
            lse_v = lse_ref[u, 0]
            ds_acc[u:u + 1, :] += -jnp.exp(sink_ref[u] - lse_v) * delta
            p = jnp.exp(_dot(kk, qsT) - slope_ref[u] * dd - lse_v)
            ds = (p * (_dot(vv, dosT) - delta)).astype(BF16)
            dv_u, dk_u = _dot(p.astype(BF16), dos), _dot(ds, qs)
            dv_sum[g] = dv_u if dv_sum[g] is None else dv_sum[g] + dv_u
            dk_sum[g] = dk_u if dk_sum[g] is None else dk_sum[g] + dk_u
            dq_ref[:, cols] = (_pick_halves_T(_dot(kkT, ds), rlo, tq) * 0.125).astype(BF16)
        dv_ref[rows, :] += jnp.concatenate(dv_sum, axis=1)
        dk_ref[rows, :] += jnp.concatenate(dk_sum, axis=1)

        @pl.when(i == nq - 1)
        def _():
            acc = ds_acc[...]
            for u in range(nbs):
                dsink_ref[u] = jnp.concatenate(
                    [jnp.broadcast_to(jnp.sum(acc[u:u + 1, 0:tq], axis=1, keepdims=True), (1, LANES)),
                     jnp.broadcast_to(jnp.sum(acc[u:u + 1, tq:2 * tq], axis=1, keepdims=True), (1, LANES)),
                     jnp.zeros((6, LANES), F32)], axis=0)

    qmap = lambda b, i: (i, b)
    kv_spec = pl.BlockSpec((S, kvw), lambda b, i: (0, b))
    row_spec = pl.BlockSpec((nbs, 1, 2 * tq), lambda b, i: (b, 0, 0))
    wide = pl.BlockSpec((tq, nbs * LANES), qmap)
    return pl.pallas_call(
        body, name=name, grid=(nb // nbs, nq),
        in_specs=[wide, kv_spec, pl.BlockSpec((kvw, S), lambda b, i: (b, 0)), kv_spec, wide, wide,
                  pl.BlockSpec((nbs, 1, 1, 2 * tq), lambda b, i: (b, i, 0, 0)), _win_dist_spec(nk, tq, nq), row_spec, row_spec],
        out_specs=[wide, kv_spec, kv_spec, pl.BlockSpec((nbs, 8, LANES), lambda b, i: (b, 0, 0))],
        out_shape=[_sds((S, nb * LANES), BF16), _sds((S, nkb * LANES), F32), _sds((S, nkb * LANES), F32), _sds((nb, 8, LANES), F32)],
        scratch_shapes=[pltpu.VMEM((nbs, 2 * tq), F32)],
        compiler_params=_params(("arbitrary", "arbitrary")))(q, k, kT, v, o, do, lse, dist, slope, sink)


def _sum_rows(v):
    return jnp.sum(v, axis=0, keepdims=True)


def _norm_mod_bwd(dh, xv, mod_ref, nw_ref, stats_ref):
    r = _rms(xv)
    xn = xv * r
    nw = nw_ref[...]
    stats_ref[0:1, :] += _sum_rows(dh)
    stats_ref[1:2, :] += _sum_rows(dh * (xn * nw))
    dn = dh * (1.0 + mod_ref[1:2, :])
    stats_ref[2:3, :] += _sum_rows(dn * xn)
    return _rms_bwd(xv, r, dn * nw)


def _even_gate_specs(ts):
    return [pl.BlockSpec((ts, 256), lambda i, c=c: (i, c)) for c in (3, 4, 7, 8)]


def _even_post_fwd(oa, olat, proj, x, gate, wuv, woe):
    S = x.shape[0]
    ts = min(ROW_TILE, S)

    def body(oa_ref, ol_ref, ga0_ref, ga1_ref, gb0_ref, gb1_ref, x_ref, gate_ref, wuv_ref, woe_ref, y_ref, x1_ref):
        sa, _ = _silu_and_grad(jnp.concatenate([ga0_ref[...], ga1_ref[...]], axis=1))
        sb, _ = _silu_and_grad(jnp.concatenate([gb0_ref[...], gb1_ref[...]], axis=1))
        olc = jnp.concatenate([ol_ref[hh] for hh in range(B_HEADS)], axis=1).astype(BF16)
        ob = _dot(olc, wuv_ref[...])
        mix = jnp.concatenate([oa_ref[...] * sa, ob * sb], axis=1).astype(BF16)
        y = _dot(mix, woe_ref[...])
        y_ref[...] = y.astype(BF16)
        x1_ref[...] = x_ref[...] + gate_ref[...] * y

    return pl.pallas_call(
        body, name="even_post_fwd", grid=(S // ts,),
        in_specs=[_row_spec(ts, 512), pl.BlockSpec((B_HEADS, ts, LANES), lambda i: (0, i, 0))] + _even_gate_specs(ts) +
                 [_row_spec(ts, D_MODEL), _full_spec((1, D_MODEL)), _full_spec((1024, 512)), _full_spec((1024, D_MODEL))],
        out_specs=[_row_spec(ts, D_MODEL), _row_spec(ts, D_MODEL)],
        out_shape=[_sds((S, D_MODEL), BF16), _sds((S, D_MODEL), F32)],
        compiler_params=_params(("arbitrary",)),
    )(oa, olat, proj, proj, proj, proj, x, gate, wuv, woe)


def _odd_pre_fwd(x, mod, nw, wio):
    S = x.shape[0]
    ts = min(ROW_TILE, S)

    def body(x_ref, mod_ref, nw_ref, wio_ref, h_ref, g_ref, q_ref, k_ref, v_ref, kt_ref, vt_ref):
        xv = x_ref[...]
        h = (xv * _rms(xv) * nw_ref[...]) * (1.0 + mod_ref[1:2, :]) + mod_ref[0:1, :]
        hb = h.astype(BF16)
        h_ref[...] = hb
        proj = jnp.concatenate([_dot(hb, wio_ref[p]) for p in range(N_CHIPS)], axis=1)
        g_ref[...] = proj[:, 1536:2560]
        q_ref[...] = (proj[:, 0:1024] * 0.125).astype(BF16)
        lane = _lane_iota()
        k_v = jnp.concatenate([_dup_heads(proj[:, 1024 + LANES * j:1024 + LANES * (j + 1)], lane) for j in range(2)], axis=1)
        v_v = jnp.concatenate([_dup_heads(proj[:, 1280 + LANES * j:1280 + LANES * (j + 1)], lane) for j in range(2)], axis=1)
        k_ref[...] = k_v.astype(BF16)
        v_ref[...] = v_v.astype(BF16)
        kt_ref[...] = k_v.T.astype(BF16)
        vt_ref[...] = v_v.T.astype(BF16)

    col_spec = pl.BlockSpec((512, ts), lambda i: (0, i))
    return pl.pallas_call(
        body, name="odd_pre_fwd", grid=(S // ts,),
        in_specs=[_row_spec(ts, D_MODEL), _full_spec((3, D_MODEL)), _full_spec((1, D_MODEL)),
                  _full_spec((N_CHIPS, D_MODEL, ODD_IN // N_CHIPS))],
        out_specs=[_row_spec(ts, D_MODEL), _row_spec(ts, 1024), _row_spec(ts, 1024), _row_spec(ts, 512), _row_spec(ts, 512),
                   col_spec, col_spec],
        out_shape=[_sds((S, D_MODEL), BF16), _sds((S, 1024), F32), _sds((S, 1024), BF16), _sds((S, 512), BF16),
                   _sds((S, 512), BF16), _sds((512, S), BF16), _sds((512, S), BF16)],
        compiler_params=_params(("arbitrary",)),
    )(x, mod, nw, wio)


def _odd_post(oc, g, x1, gate, woo, fw, tgt):
    S = x1.shape[0]
    ts = min(ROW_TILE, S)
    nsteps = S // ts

    def body(oc_ref, g_ref, x_ref, gate_ref, woo_ref, fw_ref, tgt_ref, doc_ref, dgc_ref, dx2_ref, dwoo_out, stats_ref, dwoo_ref):
        @pl.when(pl.program_id(0) == 0)
        def _():
            dwoo_ref[...] = jnp.zeros((D_MODEL, D_MODEL), F32)
            stats_ref[...] = jnp.zeros((8, D_MODEL), F32)

        ocv = oc_ref[...]
        sg, dsg = _silu_and_grad(g_ref[...])
        mix = (ocv * sg).astype(BF16)
        woo_v = woo_ref[...]
        y = _dot(mix, woo_v)
        gate_v = gate_ref[...]
        x2 = x_ref[...] + gate_v * y
        r = _rms(x2)
        fw_v = fw_ref[...]
        xn = x2 * r
        err = xn * fw_v - tgt_ref[...]
        dout = err * (1.0 / D_MODEL)
        dx2 = _rms_bwd(x2, r, dout * fw_v)
        dx2_ref[...] = dx2
        stats_ref[0:1, :] += _sum_rows(dout * xn)
        stats_ref[1:2, :] += _sum_rows(dx2 * y)
        loss_t = 0.5 * jnp.sum(_sum_rows(err * dout), axis=-1, keepdims=True)
        stats_ref[2:3, :] += jnp.broadcast_to(loss_t, (1, D_MODEL))
        dy = (gate_v * dx2).astype(BF16)
        dmix = _dot_nt(dy, woo_v)
        dwoo_ref[...] += _dot_tn(mix, dy)
        doc_ref[...] = (dmix * sg).astype(BF16)
        dgc_ref[...] = (dmix * ocv * dsg).astype(BF16)

        @pl.when(pl.program_id(0) == nsteps - 1)
        def _():
            dwoo_out[...] = dwoo_ref[...].astype(BF16)

    return pl.pallas_call(
        body, name="odd_post", grid=(nsteps,),
        in_specs=[_row_spec(ts, D_MODEL), _row_spec(ts, D_MODEL), _row_spec(ts, D_MODEL), _full_spec((1, D_MODEL)),
                  _full_spec((D_MODEL, D_MODEL)), _full_spec((1, D_MODEL)), _row_spec(ts, D_MODEL)],
        out_specs=[_row_spec(ts, D_MODEL), _row_spec(ts, D_MODEL), _row_spec(ts, D_MODEL),
                   _full_spec((D_MODEL, D_MODEL), single=False), _full_spec((8, D_MODEL), single=False)],
        out_shape=[_sds((S, D_MODEL), BF16), _sds((S, D_MODEL), BF16), _sds((S, D_MODEL), F32), _sds((D_MODEL, D_MODEL), BF16),
                   _sds((8, D_MODEL), F32)],
        scratch_shapes=[pltpu.VMEM((D_MODEL, D_MODEL), F32)],
        compiler_params=_params(("arbitrary",)),
    )(oc, g, x1, gate, woo, fw, tgt)


def _odd_pre_bwd(dq, dk, dv, dgc, h, x, dx_res, mod, nw, wio):
    S = x.shape[0]
    ts = min(IN_PROJ_ROW_TILE, S)
    nsteps = S // ts
    wsh = ODD_IN // N_CHIPS

    def body(dq_ref, dk_ref, dv_ref, dgc_ref, h_ref, x_ref, dxr_ref, mod_ref, nw_ref, wio_ref, dx_ref, dw_ref, stats_ref, dw_acc):
        @pl.when(pl.program_id(0) == 0)
        def _():
            dw_acc[...] = jnp.zeros((N_CHIPS, D_MODEL, wsh), F32)
            stats_ref[...] = jnp.zeros((8, D_MODEL), F32)

        lane = _lane_iota()
        dkv = [_fold_heads(r[:, 2 * LANES * j:2 * LANES * (j + 1)], lane).astype(BF16) for r in (dk_ref, dv_ref) for j in range(2)]
        dproj = jnp.concatenate([dq_ref[...]] + dkv + [dgc_ref[...]], axis=1)
        hv = h_ref[...]
        dh = None
        for p in range(N_CHIPS):
            dp_cols = dproj[:, wsh * p:wsh * (p + 1)]
            part = _dot_nt(dp_cols, wio_ref[p])
            dh = part if dh is None else dh + part
            dw_acc[p] += _dot_tn(hv, dp_cols)
        dx_ref[...] = dxr_ref[...] + _norm_mod_bwd(dh, x_ref[...], mod_ref, nw_ref, stats_ref)

        @pl.when(pl.program_id(0) == nsteps - 1)
        def _():
            dw_ref[...] = dw_acc[...].astype(BF16)

    return pl.pallas_call(
        body, name="odd_pre_bwd", grid=(nsteps,),
        in_specs=[_row_spec(ts, 1024), _row_spec(ts, 512), _row_spec(ts, 512), _row_spec(ts, 1024), _row_spec(ts, D_MODEL),
                  _row_spec(ts, D_MODEL), _row_spec(ts, D_MODEL), _full_spec((3, D_MODEL)), _full_spec((1, D_MODEL)),
                  _full_spec((N_CHIPS, D_MODEL, wsh))],
        out_specs=[_row_spec(ts, D_MODEL), _full_spec((N_CHIPS, D_MODEL, wsh), single=False), _full_spec((8, D_MODEL), single=False)],
        out_shape=[_sds((S, D_MODEL), F32), _sds((N_CHIPS, D_MODEL, wsh), BF16), _sds((8, D_MODEL), F32)],
        scratch_shapes=[pltpu.VMEM((N_CHIPS, D_MODEL, wsh), F32)],
        compiler_params=_params(("arbitrary",)),
    )(dq, dk, dv, dgc, h, x, dx_res, mod, nw, wio)


def _even_post_bwd(dx1, y, oa, olat, proj, gate, wuv, woe):
    S = dx1.shape[0]
    ts = min(ROW_TILE, S)
    nsteps = S // ts

    def body(dx_ref, y_ref, oa_ref, ol_ref, ga0_ref, ga1_ref, gb0_ref, gb1_ref, gate_ref, wuv_ref, woe_ref,
             doa_ref, dga_ref, dgb_ref, dol_ref, dwoe_out, dwuv_ref, stats_ref, dwoe_ref):
        @pl.when(pl.program_id(0) == 0)
        def _():
            dwoe_ref[...] = jnp.zeros((D_MODEL, D_MODEL), F32)
            dwuv_ref[...] = jnp.zeros((1024, 512), F32)
            stats_ref[...] = jnp.zeros((8, D_MODEL), F32)

        dxv = dx_ref[...]
        stats_ref[0:1, :] += _sum_rows(dxv * y_ref[...])
        dy = (gate_ref[...] * dxv).astype(BF16)
        sa, dsa = _silu_and_grad(jnp.concatenate([ga0_ref[...], ga1_ref[...]], axis=1))
        sb, dsb = _silu_and_grad(jnp.concatenate([gb0_ref[...], gb1_ref[...]], axis=1))
        olc = jnp.concatenate([ol_ref[hh] for hh in range(B_HEADS)], axis=1).astype(BF16)
        wuv_v = wuv_ref[...]
        ob = _dot(olc, wuv_v)
        oav = oa_ref[...]
        mix = jnp.concatenate([oav * sa, ob * sb], axis=1).astype(BF16)
        dmix = _dot_nt(dy, woe_ref[...])
        dwoe_ref[...] += _dot_tn(mix, dy)
        dma, dmb = dmix[:, 0:512], dmix[:, 512:1024]
        doa_ref[...] = (dma * sa).astype(BF16)
        dga_ref[...] = (dma * oav * dsa).astype(BF16)
        dgb_ref[...] = (dmb * ob * dsb).astype(BF16)
        dob = (dmb * sb).astype(BF16)
        dol = _dot_nt(dob, wuv_v)
        dwuv_ref[...] += _dot_tn(olc, dob)
        for hh in range(B_HEADS):
            dol_ref[hh] = dol[:, LANES * hh:LANES * (hh + 1)].astype(BF16)

        @pl.when(pl.program_id(0) == nsteps - 1)
        def _():
            dwoe_out[...] = dwoe_ref[...].astype(BF16)

    head_spec = pl.BlockSpec((B_HEADS, ts, LANES), lambda i: (0, i, 0))
    return pl.pallas_call(
        body, name="even_post_bwd", grid=(nsteps,),
        in_specs=[_row_spec(ts, D_MODEL), _row_spec(ts, D_MODEL), _row_spec(ts, 512), head_spec] + _even_gate_specs(ts) +
                 [_full_spec((1, D_MODEL)), _full_spec((1024, 512)), _full_spec((1024, D_MODEL))],
        out_specs=[_row_spec(ts, 512), _row_spec(ts, 512), _row_spec(ts, 512), head_spec,
                   _full_spec((D_MODEL, D_MODEL), single=False), _full_spec((1024, 512), single=False),
                   _full_spec((8, D_MODEL), single=False)],
        out_shape=[_sds((S, 512), BF16), _sds((S, 512), BF16), _sds((S, 512), BF16), _sds((B_HEADS, S, LANES), BF16),
                   _sds((D_MODEL, D_MODEL), BF16), _sds((1024, 512), F32), _sds((8, D_MODEL), F32)],
        scratch_shapes=[pltpu.VMEM((D_MODEL, D_MODEL), F32)],
        compiler_params=_params(("arbitrary",)),
    )(dx1, y, oa, olat, proj, proj, proj, proj, gate, wuv, woe)


def _even_pre_bwd(h, proj, dqa, dka, dva, dga, dgb, dqcat, dkcat, qn, kn, seg, ca, sa, ct, st, qln, kvln, wuq, wuk):
    S = h.shape[0]
    ts = min(IN_PROJ_ROW_TILE, S)
    nsteps = S // ts

    def body(h_ref, proj_ref, dqa_ref, dka_ref, dva_ref, dga_ref, dgb_ref, dqc_ref, dkc_ref,
             qn_ref, kn_ref, seg_ref, ca_ref, sa_ref, ct_ref, st_ref, qln_ref, kvln_ref, wuq_ref, wuk_ref,
             dproj_ref, dwie_out, dwuq_out, dwuk_out, nstats_ref, dwie_ref, dwuq_ref, dwuk_ref, stage):
        @pl.when(pl.program_id(0) == 0)
        def _():
            dwie_ref[...] = jnp.zeros((D_MODEL, EVEN_P), F32)
            dwuq_ref[...] = jnp.zeros((B_Q_LORA, 1536), F32)
            dwuk_ref[...] = jnp.zeros((512, 1024), F32)
            nstats_ref[...] = jnp.zeros((8, 256), F32)

        lane = _lane_iota()
        ca_v, sa_v, ct_v, st_v = ca_ref[...], sa_ref[...], ct_ref[...], st_ref[...]
        seg_v = seg_ref[...]

        def head_norm_bwd(xc, dy, w):
            r = lax.rsqrt(_seg_mean(xc * xc, seg_v) + EPS)
            g = dy * w
            dxc = r * g - xc * (r * r * r) * _seg_mean(xc * g, seg_v)
            return dxc, _sum_rows(dy * (xc * r))

        pieces = []
        dqn = jnp.zeros((1, LANES), F32)
        for cb in range(4):
            sl = slice(LANES * cb, LANES * (cb + 1))
            dy = _rot_bwd(dqa_ref[:, sl] * 0.125, ca_v, sa_v, lane)
            dxc, dw = head_norm_bwd(proj_ref[:, sl], dy, qn_ref[...])
            pieces.append(dxc)
            dqn = dqn + dw
        dxc, dkn = head_norm_bwd(proj_ref[:, 512:640], _rot_bwd(_fold_heads(dka_ref[...], lane), ca_v, sa_v, lane), kn_ref[...])
        pieces += [dxc, _fold_heads(dva_ref[...], lane), dga_ref[...]]
        nstats_ref[0:1, 0:LANES] += dqn + pltpu.roll(dqn, HEAD_DIM, 1)
        nstats_ref[1:2, 0:LANES] += dkn + pltpu.roll(dkn, HEAD_DIM, 1)

        cq = proj_ref[:, 1280:1536]
        rq = _rms(cq)
        cqn_f = cq * rq
        qln_v = qln_ref[...]
        cqn = (cqn_f * qln_v).astype(BF16)
        wuq_v, wuk_v = wuq_ref[...], wuk_ref[...]
        qnope = _dot(cqn, wuq_v[:, 0:512]).astype(BF16)
        dqlat = jnp.concatenate([dqc_ref[hh, :, 0:LANES] for hh in range(B_HEADS)], axis=1).astype(BF16)
        dqnope = _dot_nt(dqlat, wuk_v)
        dwuk_ref[...] += _dot_tn(qnope, dqlat)
        dqr = [_rot_bwd(dqc_ref[hh, :, LANES:2 * LANES], ct_v, st_v, lane) for hh in range(B_HEADS)]
        dqb = jnp.concatenate([dqnope] + dqr, axis=1).astype(BF16)
        dcqn = _dot_nt(dqb, wuq_v)
        dwuq_ref[...] += _dot_tn(cqn, dqb)
        nstats_ref[2:3, :] += _sum_rows(dcqn * cqn_f)
        dcq = _rms_bwd(cq, rq, dcqn * qln_v)
        ckv = proj_ref[:, 1536:1664]
        rk = _rms(ckv)
        dckvn = dkc_ref[:, 0:LANES]
        nstats_ref[3:4, 0:LANES] += _sum_rows(dckvn * (ckv * rk))
        dckv = _rms_bwd(ckv, rk, dckvn * kvln_ref[...])
        dkr = _rot_bwd(dkc_ref[:, LANES:2 * LANES], ct_v, st_v, lane)
        pieces += [dcq, dckv, dkr, dgb_ref[...]]
        dproj = jnp.concatenate([piece.astype(BF16) for piece in pieces], axis=1)
        dproj_ref[...] = dproj
        dwie_ref[...] += _dot_tn(h_ref[...], dproj)

        @pl.when(pl.program_id(0) == nsteps - 1)
        def _():
            for r0 in range(0, D_MODEL, 256):
                stage[...] = dwie_ref[r0:r0 + 256, :].astype(BF16)
                pltpu.sync_copy(stage, dwie_out.at[pl.ds(r0, 256), :])
            pltpu.sync_copy(dwuq_ref, dwuq_out)
            pltpu.sync_copy(dwuk_ref, dwuk_out)

    return pl.pallas_call(
        body, name="even_pre_bwd", grid=(nsteps,),
        in_specs=[_row_spec(ts, D_MODEL), _row_spec(ts, EVEN_P), _row_spec(ts, 512), _row_spec(ts, 2 * LANES),
                  _row_spec(ts, 2 * LANES), _row_spec(ts, 512), _row_spec(ts, 512),
                  pl.BlockSpec((B_HEADS, ts, 2 * LANES), lambda i: (0, i, 0)), _row_spec(ts, 2 * LANES),
                  _full_spec((1, LANES)), _full_spec((1, LANES)), _full_spec((LANES, LANES)),
                  _row_spec(ts, LANES), _row_spec(ts, LANES), _row_spec(ts, LANES), _row_spec(ts, LANES),
                  _full_spec((1, B_Q_LORA)), _full_spec((1, B_KV_LORA)), _full_spec((B_Q_LORA, 1536)), _full_spec((512, 1024))],
        out_specs=[_row_spec(ts, EVEN_P), _ANY, _ANY, _ANY, _full_spec((8, 256), single=False)],
        out_shape=[_sds((S, EVEN_P), BF16), _sds((D_MODEL, EVEN_P), BF16), _sds((B_Q_LORA, 1536), F32), _sds((512, 1024), F32),
                   _sds((8, 256), F32)],
        scratch_shapes=[pltpu.VMEM((D_MODEL, EVEN_P), F32), pltpu.VMEM((B_Q_LORA, 1536), F32), pltpu.VMEM((512, 1024), F32),
                        pltpu.VMEM((256, EVEN_P), BF16)],
        compiler_params=_params(("arbitrary",)),
    )(h, proj, dqa, dka, dva, dga, dgb, dqcat, dkcat, qn, kn, seg, ca, sa, ct, st, qln, kvln, wuq, wuk)


def _even_in_bwd(dproj, x, dx_res, mod, nw, wie, side):
    S = x.shape[0]
    ts = min(ROW_TILE, S)
    nsteps = S // ts

    def body(*refs):
        (dproj_ref, x_ref, dxr_ref, mod_ref, nw_ref, wie_ref), side_ins, (dx_ref, stats_ref), side_outs, _, side_sems = _side_split(
            refs, 6, 2, 0, side)
        step = pl.program_id(0)
        side_end = _side_hooks(side, side_ins, side_outs, side_sems, step, nsteps)

        @pl.when(step == 0)
        def _():
            stats_ref[...] = jnp.zeros((8, D_MODEL), F32)

        dh = _dot_nt(dproj_ref[...], wie_ref[...])
        dx_ref[...] = dxr_ref[...] + _norm_mod_bwd(dh, x_ref[...], mod_ref, nw_ref, stats_ref)
        side_end()

    s_args, s_in, s_out, s_shapes, s_sems = _side_specs(side)
    return pl.pallas_call(
        body, name="even_in_bwd", grid=(nsteps,),
        in_specs=[_row_spec(ts, EVEN_P), _row_spec(ts, D_MODEL), _row_spec(ts, D_MODEL), _full_spec((3, D_MODEL)),
                  _full_spec((1, D_MODEL)), _full_spec((D_MODEL, EVEN_P))] + s_in,
        out_specs=[_row_spec(ts, D_MODEL), _full_spec((8, D_MODEL), single=False)] + s_out,
        out_shape=[_sds((S, D_MODEL), F32), _sds((8, D_MODEL), F32)] + s_shapes,
        scratch_shapes=s_sems,
        compiler_params=_params(("arbitrary",)),
    )(dproj, x, dx_res, mod, nw, wie, *s_args)


def _ada_fwd(c_all, w, b):
    n = w.shape[2]

    def body(c_ref, w_ref, b_ref, o_ref):
        cv = c_ref[...]
        o_ref[0] = _dot_f32(cv * _sigmoid(cv), w_ref[0]) + b_ref[0]

    return pl.pallas_call(
        body, name="ada_fwd", grid=(2,),
        in_specs=[pl.BlockSpec((N_DEV, D_MODEL), lambda l: (0, 0)), pl.BlockSpec((1, D_MODEL, n), lambda l: (l, 0, 0)),
                  pl.BlockSpec((1, 1, n), lambda l: (l, 0, 0))],
        out_specs=pl.BlockSpec((1, N_DEV, n), lambda l: (l, 0, 0)),
        out_shape=_sds((2, N_DEV, n), F32),
        compiler_params=_params(("arbitrary",)),
    )(c_all, w, b)


def _ada_bwd(c_all_t, dmod):
    n = dmod.shape[2]

    def body(c_ref, d_ref, o_ref):
        cv = c_ref[...]
        act = cv * _sigmoid(cv)
        dv = d_ref[0]
        acc = act[:, 0:1] * dv[0:1, :]
        for bb in range(1, N_DEV):
            acc = acc + act[:, bb:bb + 1] * dv[bb:bb + 1, :]
        o_ref[0] = acc

    return pl.pallas_call(
        body, name="ada_bwd", grid=(2,),
        in_specs=[pl.BlockSpec((D_MODEL, N_DEV), lambda l: (0, 0)), pl.BlockSpec((1, N_DEV, n), lambda l: (l, 0, 0))],
        out_specs=pl.BlockSpec((1, D_MODEL, n), lambda l: (l, 0, 0)),
        out_shape=_sds((2, D_MODEL, n), F32),
        compiler_params=_params(("arbitrary",)),
    )(c_all_t, dmod)


ADAM_ROW_TILE = 512


def _adam_update(g, w, m, v):
    m_new = ADAM_B1 * m + (1.0 - ADAM_B1) * g
    v_new = ADAM_B2 * v + (1.0 - ADAM_B2) * jnp.square(g)
    m_hat = m_new / (1.0 - ADAM_B1 ** ADAM_STEP)
    v_hat = v_new / (1.0 - ADAM_B2 ** ADAM_STEP)
    return -ADAM_LR * (m_hat / (jnp.sqrt(v_hat) + ADAM_EPS) + ADAM_WD * w), m_new, v_new


SMALL_ROWS = dict(dmod=(0, D_MODEL), norm_w=(6, D_MODEL), final_norm=(8, D_MODEL), a_q_norm=(9, HEAD_DIM), a_k_norm=(10, HEAD_DIM),
                  b_q_lora_norm=(11, B_Q_LORA), b_kv_lora_norm=(12, B_KV_LORA), c_sink=(13, C_HEADS))
SMALL_WEIGHTS = ("ada_b", "norm_w", "final_norm", "a_q_norm", "a_k_norm", "b_q_lora_norm", "b_kv_lora_norm", "c_sink")
LOSS_ROW = 14


def _pack_small(res):
    def padded(v):
        return jnp.concatenate([v, jnp.zeros((v.shape[0], D_MODEL - v.shape[1]), F32)], axis=1)

    rows = [res["dmod"].reshape(6, D_MODEL), res["norm_w"], res["final_norm"].reshape(1, D_MODEL)]
    rows += [padded(res[k]) for k in ("a_q_norm", "a_k_norm", "b_q_lora_norm", "b_kv_lora_norm", "c_sink")]
    return jnp.concatenate(rows + [res["loss_row"], jnp.zeros((1, D_MODEL), F32)], axis=0)


def _adam_small(parts, ws, ms, vs):
    nw = len(SMALL_WEIGHTS)

    def body(*refs):
        p_ref = refs[0]
        w_refs, m_refs, v_refs = refs[1:1 + nw], refs[1 + nw:1 + 2 * nw], refs[1 + 2 * nw:1 + 3 * nw]
        outs = refs[1 + 3 * nw:]
        g_all = p_ref[0]
        for k in range(1, N_DEV):
            g_all = g_all + p_ref[k]
        for idx, name in enumerate(SMALL_WEIGHTS):
            if name == "ada_b":
                g = jnp.concatenate([jnp.concatenate([g_all[3 * l + t:3 * l + t + 1] for t in range(3)], axis=1) for l in range(2)],
                                    axis=0)
            else:
                row, width = SMALL_ROWS[name]
                g = g_all[row:row + w_refs[idx].shape[0], 0:width]
            d, m_new, v_new = _adam_update(g, w_refs[idx][...], m_refs[idx][...], v_refs[idx][...])
            outs[4 * idx][...], outs[4 * idx + 1][...], outs[4 * idx + 2][...], outs[4 * idx + 3][...] = g, d, m_new, v_new
        outs[4 * nw][...] = g_all[LOSS_ROW:LOSS_ROW + 1, 0:LANES]

    out_shape = []
    for w in ws:
        out_shape += [_sds(w.shape, F32)] * 4
    out_shape.append(_sds((1, LANES), F32))
    return pl.pallas_call(body, name="adam_small", out_shape=out_shape,
                          compiler_params=pltpu.CompilerParams(vmem_limit_bytes=VMEM_LIMIT))(parts, *ws, *ms, *vs)


def _adam(parts, w, m, v, name):
    P, R, C = parts.shape
    tr = R if R <= ADAM_ROW_TILE else ADAM_ROW_TILE
    assert R % tr == 0

    def body(p_ref, w_ref, m_ref, v_ref, g_ref, d_ref, nm_ref, nv_ref):
        g = p_ref[0].astype(F32)
        for k in range(1, P):
            g = g + p_ref[k].astype(F32)
        g_ref[...] = g
        d_ref[...], nm_ref[...], nv_ref[...] = _adam_update(g, w_ref[...], m_ref[...], v_ref[...])

    spec = pl.BlockSpec((tr, C), lambda i: (i, 0))
    return pl.pallas_call(
        body, name=name, grid=(R // tr,),
        in_specs=[pl.BlockSpec((P, tr, C), lambda i: (0, i, 0)), spec, spec, spec],
        out_specs=[spec, spec, spec, spec], out_shape=[_sds((R, C), F32)] * 4,
        compiler_params=_params(("arbitrary",)),
    )(parts, w, m, v)


_ANY = pl.BlockSpec(memory_space=pl.ANY)
CHIP_FLIPS = ((1, 0), (0, 1), (1, 1))
DEV_FLIPS = tuple((dx, dy, dc) for dx in (0, 1) for dy in (0, 1) for dc in (0, 1) if dx + dy + dc)


def _flip(a, d):
    return a if d == 0 else 1 - a


def _my_place():
    return lax.axis_index("x"), lax.axis_index("y"), lax.axis_index("c")


def _gather8_copies(ins, outs, send_sems, recv_sems, loc_sems):
    x, y, c = _my_place()
    me = 4 * x + 2 * y + c
    copies = []
    for a in range(len(ins)):
        copies.append(pltpu.make_async_copy(ins[a], outs[a].at[me], loc_sems.at[a]))
        for k, (dx, dy, dc) in enumerate(DEV_FLIPS):
            copies.append(pltpu.make_async_remote_copy(
                src_ref=ins[a], dst_ref=outs[a].at[me], send_sem=send_sems.at[a, k], recv_sem=recv_sems.at[a, k],
                device_id=(_flip(x, dx), _flip(y, dy), _flip(c, dc)), device_id_type=MESH_ID))
    return copies


def _gather8_sems(n):
    return [pltpu.SemaphoreType.DMA((n, 7)), pltpu.SemaphoreType.DMA((n, 7)), pltpu.SemaphoreType.DMA((n,))]


def _gather_dev8(arrs, name):
    n = len(arrs)

    def body(*refs):
        copies = _gather8_copies(refs[:n], refs[n:2 * n], *refs[2 * n:])
        for cp in copies:
            cp.start()
        for cp in copies:
            cp.wait()

    return pl.pallas_call(
        body, name=name, in_specs=[_ANY] * n, out_specs=[_ANY] * n,
        out_shape=[_sds((N_DEV,) + a.shape, a.dtype) for a in arrs], scratch_shapes=_gather8_sems(n),
    )(*arrs)


class _Exchange:
    def __init__(self, arrs, out_shapes, n_sems, phases):
        self.arrs, self.out_shapes, self.n_sems, self._phases = list(arrs), list(out_shapes), n_sems, phases

    @property
    def n(self):
        return len(self.arrs)

    def sem_shapes(self):
        return [pltpu.SemaphoreType.DMA((self.n, self.n_sems)), pltpu.SemaphoreType.DMA((self.n, self.n_sems)),
                pltpu.SemaphoreType.DMA((self.n,))]

    def phases(self, ins, outs, sems):
        return self._phases(ins, outs, *sems)

    def run(self, name):
        n = self.n

        def body(*refs):
            start, mid, end = self.phases(refs[:n], refs[n:2 * n], refs[2 * n:])
            start()
            mid()
            end()

        return pl.pallas_call(body, name=name, in_specs=[_ANY] * n, out_specs=[_ANY] * n, out_shape=self.out_shapes,
                              scratch_shapes=self.sem_shapes())(*self.arrs)

def _gather_halves_phases(ins, outs, send_sems, recv_sems, loc_sems):
    n = len(ins)
    x, y, c = _my_place()
    chip = 2 * x + y
    sibling = (x, y, 1 - c)
    peers = [(_flip(x, dx), _flip(y, dy)) for dx, dy in CHIP_FLIPS]

    def remote(src, p, half, a, k, to):
        return pltpu.make_async_remote_copy(src_ref=src, dst_ref=outs[a].at[p, half], send_sem=send_sems.at[a, k],
                                            recv_sem=recv_sems.at[a, k], device_id=to, device_id_type=MESH_ID)

    def local(a):
        return pltpu.make_async_copy(ins[a], outs[a].at[chip], loc_sems.at[a])

    def first(a, k):
        return remote(ins[a].at[c], chip, c, a, k, (*peers[k], c))

    def passed(a, k):
        p = 2 * peers[k][0] + peers[k][1]
        return remote(outs[a].at[p, c], p, c, a, 3 + k, sibling)

    def start():
        for a in range(n):
            local(a).start()
            for k in range(3):
                first(a, k).start()

    def mid():
        for a in range(n):
            for k in range(3):
                p = 2 * peers[k][0] + peers[k][1]
                remote(outs[a].at[p, c], p, c, a, k, sibling).wait_recv()
                passed(a, k).start()

    def end():
        for a in range(n):
            for k in range(3):
                p = 2 * peers[k][0] + peers[k][1]
                remote(outs[a].at[p, 1 - c], p, 1 - c, a, 3 + k, sibling).wait_recv()
        for a in range(n):
            for k in range(3):
                first(a, k).wait_send()
                passed(a, k).wait_send()
            local(a).wait()

    return start, mid, end


def _gather_chip4_halves(arrs):
    return _Exchange(arrs, [_sds((N_CHIPS,) + a.shape, a.dtype) for a in arrs], 6, _gather_halves_phases)


def _reduce_phases(n_whole, ins, outs, send_sems, recv_sems, loc_sems):
    n = len(ins)
    x, y, c = _my_place()
    chip = 2 * x + y
    sibling = (x, y, 1 - c)
    peers = [(_flip(x, dx), _flip(y, dy)) for dx, dy in CHIP_FLIPS]

    def remote(src, slot, a, k, to):
        return pltpu.make_async_remote_copy(src_ref=src, dst_ref=outs[a].at[slot], send_sem=send_sems.at[a, k],
                                            recv_sem=recv_sems.at[a, k], device_id=to, device_id_type=MESH_ID)

    def block(a, p):
        return ins[a] if a >= n - n_whole else ins[a].at[p]

    def local(a):
        return pltpu.make_async_copy(block(a, chip), outs[a].at[2 * chip + c], loc_sems.at[a])

    def own(a):
        return remote(block(a, chip), 2 * chip + c, a, 0, sibling)

    def first(a, k):
        return remote(block(a, 2 * peers[k][0] + peers[k][1]), 2 * chip + c, a, 1 + k, (*peers[k], c))

    def passed(a, k):
        slot = 2 * (2 * peers[k][0] + peers[k][1]) + c
        return remote(outs[a].at[slot], slot, a, 4 + k, sibling)

    def start():
        for a in range(n):
            local(a).start()
            own(a).start()
            for k in range(3):
                first(a, k).start()

    def mid():
        for a in range(n):
            for k in range(3):
                slot = 2 * (2 * peers[k][0] + peers[k][1]) + c
                remote(outs[a].at[slot], slot, a, 1 + k, sibling).wait_recv()
                passed(a, k).start()

    def end():
        for a in range(n):
            remote(outs[a].at[2 * chip + 1 - c], 2 * chip + 1 - c, a, 0, sibling).wait_recv()
            for k in range(3):
                slot = 2 * (2 * peers[k][0] + peers[k][1]) + 1 - c
                remote(outs[a].at[slot], slot, a, 4 + k, sibling).wait_recv()
        for a in range(n):
            own(a).wait_send()
            for k in range(3):
                first(a, k).wait_send()
                passed(a, k).wait_send()
            local(a).wait()

    return start, mid, end


def _reduce_exchange(arrs, whole=()):
    shapes = [_sds((N_DEV,) + a.shape[1:], a.dtype) for a in arrs] + [_sds((N_DEV,) + a.shape, a.dtype) for a in whole]
    return _Exchange(list(arrs) + list(whole), shapes, 7, functools.partial(_reduce_phases, len(whole)))


def _even_in_layout(w):
    return jnp.concatenate([w[:, 0:1696], jnp.zeros((w.shape[0], 96), w.dtype), w[:, 1696:2208]], axis=1)


def _even_in_unlayout(g):
    return jnp.concatenate([g[:, 0:1696], g[:, 1792:2304]], axis=1)


def _uq_layout(w):
    per = B_NOPE + B_ROPE
    pad = jnp.zeros((w.shape[0], LANES - B_ROPE), w.dtype)
    nope = [w[:, per * h:per * h + B_NOPE] for h in range(B_HEADS)]
    rope = [jnp.concatenate([w[:, per * h + B_NOPE:per * (h + 1)], pad], axis=1) for h in range(B_HEADS)]
    return jnp.concatenate(nope + rope, axis=1)


def _uq_unlayout(g):
    parts = []
    for h in range(B_HEADS):
        parts += [g[:, B_NOPE * h:B_NOPE * (h + 1)], g[:, 512 + LANES * h:512 + LANES * h + B_ROPE]]
    return jnp.concatenate(parts, axis=1)


def _block_diag(blocks):
    rows = []
    for h, blk in enumerate(blocks):
        r, cdim = blk.shape
        n = len(blocks)
        rows.append(jnp.concatenate([jnp.zeros((r, cdim * h), blk.dtype), blk, jnp.zeros((r, cdim * (n - 1 - h)), blk.dtype)],
                                    axis=1))
    return jnp.concatenate(rows, axis=0)


def _uk_layout(w):
    return _block_diag([w[:, h, :].T for h in range(B_HEADS)])


def _uk_unlayout(g):
    return jnp.stack([g[B_NOPE * h:B_NOPE * (h + 1), LANES * h:LANES * (h + 1)].T for h in range(B_HEADS)], axis=1)


def _uv_layout(w):
    return _block_diag([w[:, h, :] for h in range(B_HEADS)])


def _uv_unlayout(g):
    return jnp.stack([g[LANES * h:LANES * (h + 1), B_V * h:B_V * (h + 1)] for h in range(B_HEADS)], axis=1)


def _rope_tables(S):
    inv = ROPE_THETA ** (-jnp.arange(0, 32, 2, dtype=F32) / 32)
    tok = jnp.arange(S)

    def tab(pos):
        ang = pos.astype(F32)[:, None] * inv[None, :]
        cos, sin = jnp.cos(ang), jnp.sin(ang)
        return jnp.concatenate([cos, cos], axis=1), jnp.concatenate([-sin, sin], axis=1)

    cr, sr = tab(tok // GRID_W)
    cc, sc = tab(tok % GRID_W)
    ct, st = tab(tok)
    return (jnp.tile(jnp.concatenate([cr, cc], axis=1), (1, 2)), jnp.tile(jnp.concatenate([sr, sc], axis=1), (1, 2)),
            jnp.tile(ct, (1, 4)), jnp.tile(st, (1, 4)))


A_TQ, A_TK, A_SUB = 512, 4096, 512
A_FWD_SUB = 1024
B_TQ, B_TK, B_SUB = 128, 4096, 1024
B_BWD_TK, B_BWD_SUB = 4096, 512
C_T = 256
C_BLOCKS_PER_STEP = 8
KV_SHARE = 2


def _local_step(x0, tgt, mod, norm_w, wie, wuq, wuk, wuv, late_shards, a_q_norm, a_k_norm, q_lora_norm, kv_lora_norm,
                c_sink, final_norm):
    S = x0.shape[0]
    mod3 = mod.reshape(2, 3, D_MODEL)
    ca, sa, ct, st = _rope_tables(S)
    lane_seg = np.arange(LANES) // HEAD_DIM
    seg = jnp.asarray((lane_seg[:, None] == lane_seg[None, :]).astype(np.float32)).astype(BF16)
    qn = jnp.tile(a_q_norm.reshape(1, HEAD_DIM), (1, 2))
    kn = jnp.tile(a_k_norm.reshape(1, HEAD_DIM), (1, 2))
    qln, kvln = q_lora_norm.reshape(1, B_Q_LORA), kv_lora_norm.reshape(1, B_KV_LORA)
    nw0, nw1 = norm_w[0:1], norm_w[1:2]
    gate0, gate1 = mod3[0, 2:3], mod3[1, 2:3]
    a_tq, a_tk, b_tq, b_tk, bb_tk, c_t = min(A_TQ, S), min(A_TK, S), min(B_TQ, S), min(B_TK, S), min(B_BWD_TK, S), min(C_T, S)
    a_sub, b_sub, bb_sub = min(A_SUB, a_tk), min(B_SUB, b_tk), min(B_BWD_SUB, bb_tk)

    h0, proj_e, qa, ka, va, qcat, kcat, ka_t, va_t, kcat_t = _even_pre_fwd(x0, mod3[0], nw0, wie, qn, kn, seg, ca, sa, ct, st,
                                                                           qln, kvln, wuq, wuk)
    oa, lse_a, woe_g, wio_g, woo_g = _pp_fwd(qa, ka, va_t, kdiv=KV_SHARE, tq=a_tq, tk=a_tk, sub=min(A_FWD_SUB, a_tk), name="attn_a_fwd",
                                             side=_gather_chip4_halves(late_shards))
    woe = woe_g.reshape(D_MODEL, D_MODEL)
    wio = wio_g.reshape(N_CHIPS, D_MODEL, ODD_IN // N_CHIPS)
    woo = woo_g.reshape(D_MODEL, D_MODEL)
    olat, lse_b = _mla_fwd(qcat, kcat, kcat_t, tq=b_tq, tk=b_tk, sub=b_sub)
    y0, x1 = _even_post_fwd(oa, olat, proj_e, x0, gate0, wuv, woe)
    h1, gc, qc, kc, vc, kc_t, vc_t = _odd_pre_fwd(x1, mod3[1], nw1, wio)
    slopes = 2.0 ** (-8.0 * jnp.arange(1, C_HEADS + 1, dtype=F32) / C_HEADS)
    slope_rows = jnp.repeat(slopes.reshape(C_HEADS // 2, 2), c_t, axis=1)[:, None, :]
    sink_rows = jnp.repeat(c_sink.reshape(C_HEADS // 2, 2), c_t, axis=1)[:, None, :]
    win_dist = _win_dist_table(S, c_t)
    oc, lse_c = _win_fwd(qc, kc, vc_t, win_dist, slope_rows, sink_rows, kdiv=KV_SHARE, tq=c_t, nbs=C_BLOCKS_PER_STEP,
                         name="attn_c_fwd")
    doc, dgc, dx2, dwoo, st_f = _odd_post(oc, gc, x1, gate1, woo, final_norm.reshape(1, D_MODEL), tgt)
    dqc, dkc, dvc, dsink_raw = _win_bwd(qc, kc, kc_t, vc, oc, doc, lse_c, win_dist, slope_rows, sink_rows, kdiv=KV_SHARE, tq=c_t,
                                        nbs=C_BLOCKS_PER_STEP, name="attn_c_bwd")
    dx1, dwio, st_1 = _odd_pre_bwd(dqc, dkc, dvc, dgc, h1, x1, dx2, mod3[1], nw1, wio)
    doa, dga, dgb, dolat, dwoe, dwuv, st_e = _even_post_bwd(dx1, y0, oa, olat, proj_e, gate0, wuv, woe)
    late_grads = _reduce_exchange([dwoe.reshape(N_CHIPS, D_MODEL // N_CHIPS, D_MODEL), dwio,
                                   dwoo.reshape(N_CHIPS, D_MODEL // N_CHIPS, D_MODEL)])
    dqa, dka, dva, p_woe, p_wio, p_woo = _pp_bwd(qa, ka, ka_t, va, oa, doa, lse_a, kdiv=KV_SHARE, tq=a_tq, tk=a_tk, sub=a_sub,
                                                 name="attn_a_bwd", side=late_grads)
    dqcat, dkcat = _mla_bwd(qcat, kcat, kcat_t, olat, dolat, lse_b, tq=b_tq, tk=bb_tk, sub=bb_sub)
    dproj, dwie, dwuq, dwuk, nst = _even_pre_bwd(h0, proj_e, dqa, dka, dva, dga, dgb, dqcat, dkcat, qn, kn, seg, ca, sa, ct, st,
                                                 qln, kvln, wuq, wuk)
    early_grads = _reduce_exchange([_cols_to_chips(_even_in_unlayout(dwie)), _cols_to_chips(_uq_unlayout(dwuq.astype(BF16)))])
    dx0, st_0, p_wie, p_wuq = _even_in_bwd(dproj, x0, dx1, mod3[0], nw0, wie, early_grads)
    dsink_pairs = jnp.stack([dsink_raw[:, 0, 0], dsink_raw[:, 1, 0]], axis=1).reshape(C_HEADS)
    return dict(
        loss_row=st_f[2:3], dx=dx0,
        dmod=jnp.stack([jnp.concatenate([st_0[0], st_0[1], st_e[0]]), jnp.concatenate([st_1[0], st_1[1], st_f[1]])]),
        norm_w=jnp.stack([st_0[2], st_1[2]]), final_norm=st_f[0],
        a_q_norm=nst[0:1, 0:HEAD_DIM], a_k_norm=nst[1:2, 0:HEAD_DIM], b_q_lora_norm=nst[2:3, :], b_kv_lora_norm=nst[3:4, 0:LANES],
        c_sink=dsink_pairs.reshape(1, C_HEADS),
        even_w_in=p_wie, b_w_uq=p_wuq, b_w_uk=dwuk, b_w_uv=dwuv, even_w_out=p_woe, odd_w_in=p_wio, odd_w_out=p_woo)


WEIGHT_NAMES = ("norm_w", "ada_w", "ada_b", "even_w_in", "a_q_norm", "a_k_norm", "b_q_lora_norm", "b_kv_lora_norm", "b_w_uq",
                "b_w_uk", "b_w_uv", "even_w_out", "odd_w_in", "c_sink", "odd_w_out", "final_norm")


def _cols_to_chips(g):
    r, n4 = g.shape
    return jnp.transpose(g.reshape(r, N_CHIPS, n4 // N_CHIPS), (1, 0, 2))


def _chips_to_cols(g):
    p, r, n = g.shape
    return jnp.transpose(g, (1, 0, 2)).reshape(r, p * n)


def kernel(x, c, norm_w, ada_w, ada_b, even_w_in, a_q_norm, a_k_norm, b_q_lora_norm, b_kv_lora_norm, b_w_uq, b_w_uk, b_w_uv, even_w_out, odd_w_in, c_sink, odd_w_out, final_norm, loss_target, m_norm_w, m_ada_w, m_ada_b, m_even_w_in, m_a_q_norm, m_a_k_norm, m_b_q_lora_norm, m_b_kv_lora_norm, m_b_w_uq, m_b_w_uk, m_b_w_uv, m_even_w_out, m_odd_w_in, m_c_sink, m_odd_w_out, m_final_norm, v_norm_w, v_ada_w, v_ada_b, v_even_w_in, v_a_q_norm, v_a_k_norm, v_b_q_lora_norm, v_b_kv_lora_norm, v_b_w_uq, v_b_w_uk, v_b_w_uv, v_even_w_out, v_odd_w_in, v_c_sink, v_odd_w_out, v_final_norm):
    given = dict(locals())
    xi, yi, ci = _my_place()
    chip = 2 * xi + yi
    dev = 2 * chip + ci
    n_ada = ada_w.shape[2]

    (c_all,) = _gather_dev8([c], "gather_c")
    c_all = c_all.reshape(N_DEV, D_MODEL)
    bias = lax.dynamic_slice_in_dim(ada_b, chip * n_ada, n_ada, axis=1).reshape(2, 1, n_ada)
    mod_cols = _ada_fwd(c_all, ada_w, bias)
    def halves(w):
        return w.astype(BF16).reshape((2, w.shape[0] // 2) + w.shape[1:])

    mod_all, wie_g, wuq_g = _gather_chip4_halves([mod_cols, halves(even_w_in[0]), halves(b_w_uq[0])]).run("gather_weights")
    wie_g = wie_g.reshape(N_CHIPS, D_MODEL, EVEN_IN // N_CHIPS)
    wuq_g = wuq_g.reshape(N_CHIPS, B_Q_LORA, -1)
    mod = jnp.transpose(lax.dynamic_index_in_dim(mod_all, dev, axis=2, keepdims=False), (1, 0, 2)).reshape(2, 3 * D_MODEL)

    res = _local_step(
        x[0], loss_target[0], mod, norm_w,
        _even_in_layout(_chips_to_cols(wie_g)), _uq_layout(_chips_to_cols(wuq_g)), _uk_layout(b_w_uk[0].astype(BF16)),
        _uv_layout(b_w_uv[0].astype(BF16)), [halves(even_w_out[0]), halves(odd_w_in[0]), halves(odd_w_out[0])],
        a_q_norm, a_k_norm, b_q_lora_norm, b_kv_lora_norm, c_sink, final_norm)

    latent = jnp.stack([_uk_unlayout(res["b_w_uk"]).reshape(B_KV_LORA, 512),
                        _uv_unlayout(res["b_w_uv"]).reshape(B_KV_LORA, 512)]).astype(BF16)
    small_all, latent_all = _reduce_exchange([], whole=[_pack_small(res), latent]).run("gather_small")
    shard_parts = {k: res[k] for k in ("even_w_in", "b_w_uq", "even_w_out", "odd_w_in", "odd_w_out")}
    dmod_all = small_all[:, 0:6, :].reshape(N_DEV, 2, 3 * D_MODEL)
    dmod_cols = jnp.transpose(lax.dynamic_slice_in_dim(dmod_all, chip * n_ada, n_ada, axis=2), (1, 0, 2))
    parts = dict(shard_parts)
    parts["ada_w"] = _ada_bwd(c_all.T, dmod_cols).reshape(1, 2 * D_MODEL, n_ada)
    parts["b_w_uk"], parts["b_w_uv"] = latent_all[:, 0], latent_all[:, 1]

    def as2d(a):
        return a.reshape((-1, a.shape[-1]) if a.ndim > 1 else (1, a.shape[0]))

    results = {}
    small_outs = _adam_small(small_all, *[[as2d(given[pre + k]) for k in SMALL_WEIGHTS] for pre in ("", "m_", "v_")])
    for idx, k in enumerate(SMALL_WEIGHTS):
        results[k] = small_outs[4 * idx:4 * idx + 4]
    for k, p in parts.items():
        shape2 = (p.shape[-2], p.shape[-1])
        results[k] = _adam(p, given[k].reshape(shape2), given["m_" + k].reshape(shape2), given["v_" + k].reshape(shape2),
                           "adam_" + k)
    by_kind = [[results[k][t].reshape(given[k].shape) for k in WEIGHT_NAMES] for t in range(4)]
    return (small_outs[-1][0, 0], res["dx"][None], *by_kind[0], *by_kind[1], *by_kind[2], *by_kind[3])
```

```python
import functools

import numpy as np
import jax
import jax.numpy as jnp
from jax import lax
from jax.experimental import pallas as pl
from jax.experimental.pallas import tpu as pltpu

F32 = jnp.float32
BF16 = jnp.bfloat16
HIGHEST = lax.Precision.HIGHEST
MESH_ID = pl.DeviceIdType.MESH

D_MODEL = 1024
HEAD_DIM = 64
GRID_W = 64
EPS = 1e-6
ROPE_THETA = 10000.0
B_HEADS, B_NOPE, B_ROPE, B_V = 8, 64, 32, 64
B_Q_LORA, B_KV_LORA = 256, 128
C_HEADS = 16
WINDOW = 128
EVEN_IN, ODD_IN = 2208, 2560
EVEN_P = 2304
N_CHIPS, N_DEV = 4, 8
LANES = 128
NEG = -1e30
VMEM_LIMIT = 60 * 1024 * 1024

ADAM_LR, ADAM_B1, ADAM_B2, ADAM_EPS, ADAM_WD, ADAM_STEP = 0.001, 0.9, 0.999, 1e-08, 0.01, 10

ROW_TILE = 512
IN_PROJ_ROW_TILE = 256


def _dot(a, b):
    return lax.dot_general(a, b, (((1,), (0,)), ((), ())), preferred_element_type=F32)


def _dot_nt(a, b):
    return lax.dot_general(a, b, (((1,), (1,)), ((), ())), preferred_element_type=F32)


def _dot_tn(a, b):
    return lax.dot_general(a, b, (((0,), (0,)), ((), ())), preferred_element_type=F32)


def _dot_f32(a, b):
    return lax.dot_general(a, b, (((1,), (0,)), ((), ())), precision=HIGHEST, preferred_element_type=F32)


def _sigmoid(x):
    return 1.0 / (1.0 + jnp.exp(-x))


def _silu_and_grad(g):
    s = _sigmoid(g)
    return g * s, s * (1.0 + g * (1.0 - s))


def _lane_iota():
    return lax.broadcasted_iota(jnp.int32, (1, LANES), 1)


def _partner(x, lane):
    return jnp.where((lane % 32) < 16, pltpu.roll(x, LANES - 16, 1), pltpu.roll(x, 16, 1))


def _rot(x, cos, sin_signed, lane):
    return x * cos + _partner(x, lane) * sin_signed


def _rot_bwd(dy, cos, sin_signed, lane):
    return dy * cos + _partner(dy * sin_signed, lane)


def _rms(x):
    return lax.rsqrt(jnp.mean(x * x, axis=-1, keepdims=True) + EPS)


def _rms_bwd(x, r, g):
    return r * g - x * (r * r * r) * jnp.mean(x * g, axis=-1, keepdims=True)


def _seg_mean(v, seg_ones):
    hi = v.astype(BF16)
    lo = (v - hi.astype(F32)).astype(BF16)
    return (_dot(hi, seg_ones) + _dot(lo, seg_ones)) * (1.0 / HEAD_DIM)


def _dup_heads(x, lane):
    swapped = pltpu.roll(x, HEAD_DIM, 1)
    lo = lane < HEAD_DIM
    return jnp.concatenate([jnp.where(lo, x, swapped), jnp.where(lo, swapped, x)], axis=1)


def _fold_heads(x2, lane):
    a, b = x2[:, 0:LANES], x2[:, LANES:2 * LANES]
    return jnp.where(lane < HEAD_DIM, a + pltpu.roll(a, HEAD_DIM, 1), b + pltpu.roll(b, HEAD_DIM, 1))


def _row_spec(ts, cols):
    return pl.BlockSpec((ts, cols), lambda i: (i, 0))


def _full_spec(shape, single=True):
    nd = len(shape)
    if single:
        return pl.BlockSpec(shape, lambda i: (0,) * nd, pipeline_mode=pl.Buffered(1))
    return pl.BlockSpec(shape, lambda i: (0,) * nd)


def _sds(shape, dtype):
    return jax.ShapeDtypeStruct(shape, dtype)


def _params(sem):
    return pltpu.CompilerParams(dimension_semantics=sem, vmem_limit_bytes=VMEM_LIMIT)


def _even_pre_fwd(x, mod, nw, wie, qn, kn, seg, ca, sa, ct, st, qln, kvln, wuq, wuk):
    S = x.shape[0]
    ts = min(IN_PROJ_ROW_TILE, S)

    def body(x_ref, mod_ref, nw_ref, wie_ref, qn_ref, kn_ref, seg_ref, ca_ref, sa_ref, ct_ref, st_ref, qln_ref,
             kvln_ref, wuq_ref, wuk_ref, h_ref, proj_ref, qa_ref, ka_ref, va_ref, qcat_ref, kcat_ref, kat_ref, vat_ref, kcatt_ref):
        xv = x_ref[...]
        h = (xv * _rms(xv) * nw_ref[...]) * (1.0 + mod_ref[1:2, :]) + mod_ref[0:1, :]
        hb = h.astype(BF16)
        h_ref[...] = hb
        proj = _dot(hb, wie_ref[...])
        proj_ref[...] = proj
        lane = _lane_iota()
        ca_v, sa_v, ct_v, st_v = ca_ref[...], sa_ref[...], ct_ref[...], st_ref[...]
        seg_v = seg_ref[...]
        for cb in range(4):
            xc = proj[:, LANES * cb:LANES * (cb + 1)]
            r = lax.rsqrt(_seg_mean(xc * xc, seg_v) + EPS)
            y = _rot(xc * r * qn_ref[...], ca_v, sa_v, lane)
            qa_ref[:, LANES * cb:LANES * (cb + 1)] = (y * 0.125).astype(BF16)
        kc = proj[:, 512:640]
        r = lax.rsqrt(_seg_mean(kc * kc, seg_v) + EPS)
        ka_v = _dup_heads(_rot(kc * r * kn_ref[...], ca_v, sa_v, lane), lane)
        ka_ref[...] = ka_v.astype(BF16)
        kat_ref[...] = ka_v.T.astype(BF16)
        va_v = _dup_heads(proj[:, 640:768], lane)
        va_ref[...] = va_v.astype(BF16)
        vat_ref[...] = va_v.T.astype(BF16)
        cq = proj[:, 1280:1536]
        cqn = (cq * _rms(cq) * qln_ref[...]).astype(BF16)
        ckv = proj[:, 1536:1664]
        ckvn = ckv * _rms(ckv) * kvln_ref[...]
        qb = _dot(cqn, wuq_ref[...])
        qlat = _dot(qb[:, 0:512].astype(BF16), wuk_ref[...])
        for hh in range(B_HEADS):
            qcat_ref[hh, :, 0:LANES] = qlat[:, LANES * hh:LANES * (hh + 1)].astype(BF16)
            qr = _rot(qb[:, 512 + LANES * hh:512 + LANES * (hh + 1)], ct_v, st_v, lane)
            qcat_ref[hh, :, LANES:2 * LANES] = qr.astype(BF16)
        kr = _rot(proj[:, 1664:1792], ct_v, st_v, lane)
        kcat_ref[:, 0:LANES] = ckvn.astype(BF16)
        kcat_ref[:, LANES:2 * LANES] = kr.astype(BF16)
        kcatt_ref[0:LANES, :] = ckvn.T.astype(BF16)
        kcatt_ref[LANES:2 * LANES, :] = kr.T.astype(BF16)

    col_spec = lambda rows: pl.BlockSpec((rows, ts), lambda i: (0, i))
    return pl.pallas_call(
        body, name="even_pre_fwd", grid=(S // ts,),
        in_specs=[_row_spec(ts, D_MODEL), _full_spec((3, D_MODEL)), _full_spec((1, D_MODEL)), _full_spec((D_MODEL, EVEN_P)),
                  _full_spec((1, LANES)), _full_spec((1, LANES)), _full_spec((LANES, LANES)),
                  _row_spec(ts, LANES), _row_spec(ts, LANES), _row_spec(ts, LANES), _row_spec(ts, LANES),
                  _full_spec((1, B_Q_LORA)), _full_spec((1, B_KV_LORA)), _full_spec((B_Q_LORA, 1536)), _full_spec((512, 1024))],
        out_specs=[_row_spec(ts, D_MODEL), _row_spec(ts, EVEN_P), _row_spec(ts, 512), _row_spec(ts, 2 * LANES), _row_spec(ts, 2 * LANES),
                   pl.BlockSpec((B_HEADS, ts, 2 * LANES), lambda i: (0, i, 0)), _row_spec(ts, 2 * LANES),
                   col_spec(2 * LANES), col_spec(2 * LANES), col_spec(2 * LANES)],
        out_shape=[_sds((S, D_MODEL), BF16), _sds((S, EVEN_P), F32), _sds((S, 512), BF16), _sds((S, 2 * LANES), BF16),
                   _sds((S, 2 * LANES), BF16), _sds((B_HEADS, S, 2 * LANES), BF16), _sds((S, 2 * LANES), BF16),
                   _sds((2 * LANES, S), BF16), _sds((2 * LANES, S), BF16), _sds((2 * LANES, S), BF16)],
        compiler_params=_params(("arbitrary",)),
    )(x, mod, nw, wie, qn, kn, seg, ca, sa, ct, st, qln, kvln, wuq, wuk)


MLA_SCALE = (B_NOPE + B_ROPE) ** -0.5
LOG2E = 1.4426950408889634

def _row_lo():
    return lax.broadcasted_iota(jnp.int32, (LANES, 1), 0) < HEAD_DIM


def _stack_cols(vT, rlo):
    zero = jnp.zeros_like(vT)
    return jnp.concatenate([jnp.where(rlo, vT, zero), jnp.where(rlo, zero, vT)], axis=1)


def _stack_rows(v, lo):
    zero = jnp.zeros_like(v)
    return jnp.concatenate([jnp.where(lo, v, zero), jnp.where(lo, zero, v)], axis=0)


def _pick_halves_T(xT, rlo, t):
    return jnp.where(rlo, xT[:, 0:t], xT[:, t:2 * t]).T


def _side_split(refs, n_in, n_out, n_scratch, side):
    ns = side.n if side is not None else 0
    cuts = np.cumsum([0, n_in, ns, n_out, ns, n_scratch])
    return [refs[a:b] for a, b in zip(cuts[:-1], cuts[1:])] + [refs[cuts[-1]:]]


def _side_hooks(side, side_ins, side_outs, side_sems, step, total):
    if side is None:
        return lambda: None
    start, mid, end = side.phases(side_ins, side_outs, side_sems)
    pl.when(step == 0)(start)
    pl.when(step == total // 2)(mid)
    return lambda: pl.when(step == total - 1)(end)


def _side_specs(side):
    if side is None:
        return [], [], [], [], []
    return list(side.arrs), [_ANY] * side.n, [_ANY] * side.n, list(side.out_shapes), side.sem_shapes()


def _pp_fwd(q, k, vT, *, kdiv, tq, tk, sub, name, side=None):
    S = k.shape[0]; nb = q.shape[1] // LANES; nq = S // tq; nkv = S // tk; nsub = tk // sub

    def body(*refs):
        (q_ref, k_ref, vT_ref), side_ins, (o_ref, lse_ref), side_outs, (qs, m_s, l_s, acc), side_sems = _side_split(refs, 3, 2, 4, side)
        j = pl.program_id(2)
        rlo = _row_lo()
        step = (pl.program_id(0) * nq + pl.program_id(1)) * nkv + j
        side_end = _side_hooks(side, side_ins, side_outs, side_sems, step, nb * nq * nkv)

        @pl.when(j == 0)
        def _():
            qs[...] = _stack_cols(q_ref[...].astype(F32).T, rlo).astype(BF16)
            m_s[...] = jnp.full((1, 2 * tq), NEG, F32)
            l_s[...] = jnp.zeros((1, 2 * tq), F32)
            acc[...] = jnp.zeros((LANES, 2 * tq), F32)

        qsv = qs[...]
        m, l, a = m_s[...], l_s[...], acc[...]
        s_cur = _dot(k_ref[0:sub, :], qsv)
        for t in range(nsub):
            if t + 1 < nsub:
                s_next = _dot(k_ref[sub * (t + 1):sub * (t + 2), :], qsv)
            m_new = jnp.maximum(m, jnp.max(s_cur, axis=0, keepdims=True))
            alpha = jnp.exp(m - m_new)
            p = jnp.exp(s_cur - m_new)
            l = alpha * l + jnp.sum(p, axis=0, keepdims=True)
            a = alpha * a + _dot(vT_ref[:, sub * t:sub * (t + 1)], p.astype(BF16))
            m = m_new
            if t + 1 < nsub:
                s_cur = s_next
        m_s[...], l_s[...], acc[...] = m, l, a

        @pl.when(j == nkv - 1)
        def _():
            l_f = l_s[...]
            o_ref[...] = _pick_halves_T(acc[...] / l_f, rlo, tq).astype(BF16)
            lse_ref[0, 0] = m_s[...] + jnp.log(l_f)

        side_end()

    s_args, s_in, s_out, s_shapes, s_sems = _side_specs(side)
    return pl.pallas_call(
        body, name=name, grid=(nb, nq, nkv),
        in_specs=[pl.BlockSpec((tq, LANES), lambda b, i, j: (i, b)), pl.BlockSpec((tk, LANES), lambda b, i, j: (j, b // kdiv)),
                  pl.BlockSpec((LANES, tk), lambda b, i, j: (b // kdiv, j))] + s_in,
        out_specs=[pl.BlockSpec((tq, LANES), lambda b, i, j: (i, b)),
                   pl.BlockSpec((1, 1, 1, 2 * tq), lambda b, i, j: (b, i, 0, 0))] + s_out,
        out_shape=[_sds((S, nb * LANES), BF16), _sds((nb, nq, 1, 2 * tq), F32)] + s_shapes,
        scratch_shapes=[pltpu.VMEM((LANES, 2 * tq), BF16), pltpu.VMEM((1, 2 * tq), F32), pltpu.VMEM((1, 2 * tq), F32),
                        pltpu.VMEM((LANES, 2 * tq), F32)] + s_sems,
        compiler_params=_params(("arbitrary",) * 3))(q, k, vT, *s_args)


def _pp_bwd(q, k, kT, v, o, do, lse, *, kdiv, tq, tk, sub, name, side=None):
    S = k.shape[0]; nb = q.shape[1] // LANES; nkb = k.shape[1] // LANES; nq = S // tq; nkv = S // tk; nsub = tk // sub

    def body(*refs):
        ((q_ref, k_ref, kT_ref, v_ref, o_ref, do_ref, lse_ref), side_ins, (dq_ref, dk_ref, dv_ref), side_outs,
         (qsT, qs, dosT, dos, delta_s, dq_acc), side_sems) = _side_split(refs, 7, 3, 6, side)
        b, i, j = pl.program_id(0), pl.program_id(1), pl.program_id(2)
        rlo = _row_lo()
        lo = lax.broadcasted_iota(jnp.int32, (1, LANES), 1) < HEAD_DIM
        side_end = _side_hooks(side, side_ins, side_outs, side_sems, (b * nq + i) * nkv + j, nb * nq * nkv)

        @pl.when((b % kdiv == 0) & (i == 0) & (j == 0))
        def _():
            dk_ref[...] = jnp.zeros((S, LANES), F32)
            dv_ref[...] = jnp.zeros((S, LANES), F32)

        @pl.when(j == 0)
        def _():
            qv = q_ref[...]
            qs[...] = _stack_rows(qv, lo)
            qsT[...] = _stack_cols(qv.astype(F32).T, rlo).astype(BF16)
            dov = do_ref[...].astype(F32)
            dos[...] = _stack_rows(dov.astype(BF16), lo)
            dosT[...] = _stack_cols(dov.T, rlo).astype(BF16)
            prodT = (dov * o_ref[...].astype(F32)).T
            delta_s[...] = jnp.concatenate([jnp.sum(jnp.where(rlo, prodT, 0.0), axis=0, keepdims=True),
                                            jnp.sum(jnp.where(rlo, 0.0, prodT), axis=0, keepdims=True)], axis=1)
            dq_acc[...] = jnp.zeros((LANES, 2 * tq), F32)

        qsTv, dosTv, qsv, dosv = qsT[...], dosT[...], qs[...], dos[...]
        lse_v, delta_v = lse_ref[0, 0], delta_s[...]
        dqa = dq_acc[...]
        s_cur = _dot(k_ref[0:sub, :], qsTv)
        dp_cur = _dot(v_ref[0:sub, :], dosTv)
        for t in range(nsub):
            if t + 1 < nsub:
                s_next = _dot(k_ref[sub * (t + 1):sub * (t + 2), :], qsTv)
                dp_next = _dot(v_ref[sub * (t + 1):sub * (t + 2), :], dosTv)
            p = jnp.exp(s_cur - lse_v)
            ds = (p * (dp_cur - delta_v)).astype(BF16)
            rows = pl.ds(pl.multiple_of(j * tk + sub * t, sub), sub)
            dv_ref[rows, :] += _dot(p.astype(BF16), dosv)
            dk_ref[rows, :] += _dot(ds, qsv)
            dqa = dqa + _dot(kT_ref[:, sub * t:sub * (t + 1)], ds)
            if t + 1 < nsub:
                s_cur, dp_cur = s_next, dp_next
        dq_acc[...] = dqa

        @pl.when(j == nkv - 1)
        def _():
            dq_ref[...] = _pick_halves_T(dq_acc[...], rlo, tq)

        side_end()

    qmap = lambda b, i, j: (i, b)
    kmap = lambda b, i, j: (j, b // kdiv)
    res = lambda b, i, j: (0, b // kdiv)
    s_args, s_in, s_out, s_shapes, s_sems = _side_specs(side)
    return pl.pallas_call(
        body, name=name, grid=(nb, nq, nkv),
        in_specs=[pl.BlockSpec((tq, LANES), qmap), pl.BlockSpec((tk, LANES), kmap), pl.BlockSpec((LANES, tk), lambda b, i, j: (b // kdiv, j)),
                  pl.BlockSpec((tk, LANES), kmap), pl.BlockSpec((tq, LANES), qmap), pl.BlockSpec((tq, LANES), qmap),
                  pl.BlockSpec((1, 1, 1, 2 * tq), lambda b, i, j: (b, i, 0, 0))] + s_in,
        out_specs=[pl.BlockSpec((tq, LANES), qmap), pl.BlockSpec((S, LANES), res), pl.BlockSpec((S, LANES), res)] + s_out,
        out_shape=[_sds((S, nb * LANES), F32), _sds((S, nkb * LANES), F32), _sds((S, nkb * LANES), F32)] + s_shapes,
        scratch_shapes=[pltpu.VMEM((LANES, 2 * tq), BF16), pltpu.VMEM((2 * tq, LANES), BF16), pltpu.VMEM((LANES, 2 * tq), BF16),
                        pltpu.VMEM((2 * tq, LANES), BF16), pltpu.VMEM((1, 2 * tq), F32), pltpu.VMEM((LANES, 2 * tq), F32)] + s_sems,
        compiler_params=_params(("arbitrary",) * 3))(q, k, kT, v, o, do, lse, *s_args)


MLA_C = MLA_SCALE * LOG2E


def _mla_fwd(q, kcat, kcatT, *, tq, tk, sub):
    S = kcat.shape[0]; nq, nkv = S // tq, S // tk; R = B_HEADS * tq; nsub = tk // sub

    def body(q_ref, k_ref, vT_ref, o_ref, lse_ref, qT, m_s, l_s, acc):
        j = pl.program_id(1)

        @pl.when(j == 0)
        def _():
            qT[...] = q_ref[...].reshape(R, 2 * LANES).astype(F32).T.astype(BF16)
            m_s[...] = jnp.full((1, R), NEG, F32)
            l_s[...] = jnp.zeros((1, R), F32)
            acc[...] = jnp.zeros((LANES, R), F32)

        qTv = qT[...]
        m, l, a = m_s[...], l_s[...], acc[...]
        s_cur = _dot(k_ref[0:sub, :], qTv)
        for t in range(nsub):
            if t + 1 < nsub:
                s_next = _dot(k_ref[sub * (t + 1):sub * (t + 2), :], qTv)
            m_new = jnp.maximum(m, jnp.max(s_cur, axis=0, keepdims=True))
            alpha = jnp.exp2((m - m_new) * MLA_C)
            p = jnp.exp2((s_cur - m_new) * MLA_C)
            l = alpha * l + jnp.sum(p, axis=0, keepdims=True)
            a = alpha * a + _dot(vT_ref[:, sub * t:sub * (t + 1)], p.astype(BF16))
            m = m_new
            if t + 1 < nsub:
                s_cur = s_next
        m_s[...], l_s[...], acc[...] = m, l, a

        @pl.when(j == nkv - 1)
        def _():
            l_f = l_s[...]
            o_ref[...] = (acc[...] / l_f).T.reshape(B_HEADS, tq, LANES).astype(BF16)
            lse_ref[0] = m_s[...] * MLA_SCALE + jnp.log(l_f)

    return pl.pallas_call(
        body, name="mla_fwd", grid=(nq, nkv),
        in_specs=[pl.BlockSpec((B_HEADS, tq, 2 * LANES), lambda i, j: (0, i, 0)), pl.BlockSpec((tk, 2 * LANES), lambda i, j: (j, 0)),
                  pl.BlockSpec((LANES, tk), lambda i, j: (0, j))],
        out_specs=[pl.BlockSpec((B_HEADS, tq, LANES), lambda i, j: (0, i, 0)), pl.BlockSpec((1, 1, R), lambda i, j: (i, 0, 0))],
        out_shape=[_sds((B_HEADS, S, LANES), BF16), _sds((nq, 1, R), F32)],
        scratch_shapes=[pltpu.VMEM((2 * LANES, R), BF16), pltpu.VMEM((1, R), F32), pltpu.VMEM((1, R), F32), pltpu.VMEM((LANES, R), F32)],
        compiler_params=_params(("arbitrary", "arbitrary")))(q, kcat, kcatT)


def _mla_bwd(q, kcat, kcatT, o, do, lse, *, tq, tk, sub):
    S = kcat.shape[0]; nq, nkv = S // tq, S // tk; R = B_HEADS * tq; nsub = tk // sub

    def body(q_ref, k_ref, kT_ref, o_ref, do_ref, lse_ref, dq_ref, dk_ref, qT, dosT, dos, delta_s, dq_acc):
        i, j = pl.program_id(0), pl.program_id(1)

        @pl.when((i == 0) & (j == 0))
        def _():
            dk_ref[...] = jnp.zeros((S, 2 * LANES), F32)

        @pl.when(j == 0)
        def _():
            qT[...] = q_ref[...].reshape(R, 2 * LANES).astype(F32).T.astype(BF16)
            dov = do_ref[...].reshape(R, LANES).astype(F32)
            dos[...] = dov.astype(BF16)
            dosT[...] = dov.T.astype(BF16)
            delta_s[...] = jnp.sum((dov * o_ref[...].reshape(R, LANES).astype(F32)).T, axis=0, keepdims=True)
            dq_acc[...] = jnp.zeros((2 * LANES, R), F32)

        qTv, dosTv, dosv = qT[...], dosT[...], dos[...]
        qv = q_ref[...].reshape(R, 2 * LANES)
        lse_v, delta_v = lse_ref[0] * LOG2E, delta_s[...]
        dqa = dq_acc[...]
        s_cur = _dot(k_ref[0:sub, :], qTv)
        dp_cur = _dot(k_ref[0:sub, 0:LANES], dosTv)
        for t in range(nsub):
            if t + 1 < nsub:
                s_next = _dot(k_ref[sub * (t + 1):sub * (t + 2), :], qTv)
                dp_next = _dot(k_ref[sub * (t + 1):sub * (t + 2), 0:LANES], dosTv)
            p = jnp.exp2(s_cur * MLA_C - lse_v)
            ds = (p * (dp_cur - delta_v) * MLA_SCALE).astype(BF16)
            rows = pl.ds(pl.multiple_of(j * tk + sub * t, sub), sub)
            dk_ref[rows, :] += _dot(ds, qv)
            dk_ref[rows, 0:LANES] += _dot(p.astype(BF16), dosv)
            dqa = dqa + _dot(kT_ref[:, sub * t:sub * (t + 1)], ds)
            if t + 1 < nsub:
                s_cur, dp_cur = s_next, dp_next
        dq_acc[...] = dqa

        @pl.when(j == nkv - 1)
        def _():
            dq_ref[...] = dq_acc[...].T.reshape(B_HEADS, tq, 2 * LANES)

    hspec = lambda w: pl.BlockSpec((B_HEADS, tq, w), lambda i, j: (0, i, 0))
    return pl.pallas_call(
        body, name="mla_bwd", grid=(nq, nkv),
        in_specs=[hspec(2 * LANES), pl.BlockSpec((tk, 2 * LANES), lambda i, j: (j, 0)), pl.BlockSpec((2 * LANES, tk), lambda i, j: (0, j)),
                  hspec(LANES), hspec(LANES), pl.BlockSpec((1, 1, R), lambda i, j: (i, 0, 0))],
        out_specs=[hspec(2 * LANES), pl.BlockSpec((S, 2 * LANES), lambda i, j: (0, 0))],
        out_shape=[_sds((B_HEADS, S, 2 * LANES), F32), _sds((S, 2 * LANES), F32)],
        scratch_shapes=[pltpu.VMEM((2 * LANES, R), BF16), pltpu.VMEM((LANES, R), BF16), pltpu.VMEM((R, LANES), BF16),
                        pltpu.VMEM((1, R), F32), pltpu.VMEM((2 * LANES, R), F32)],
        compiler_params=_params(("arbitrary", "arbitrary")))(q, kcat, kcatT, o, do, lse)


def _win_start(i, tq, nk, S):
    return pl.multiple_of(jnp.clip(i * tq - WINDOW, 0, S - nk), LANES)


def _win_dist_table(S, tq):
    nk = min(tq + 2 * WINDOW, S)
    nq = S // tq
    r = np.arange(nk)[:, None]
    c = (np.arange(2 * tq) % tq)[None, :]
    tabs = []
    for rel in (0, WINDOW, (nq - 1) * tq - (S - nk)):
        dist = np.abs(rel + c - r).astype(np.float32)
        tabs.append(np.where(dist <= WINDOW, dist, np.float32(1e32)))
    return jnp.asarray(np.stack(tabs))


def _win_dist_spec(nk, tq, nq):
    return pl.BlockSpec((1, nk, 2 * tq), lambda b, i: (jnp.where(i == 0, 0, jnp.where(i == nq - 1, 2, 1)), 0, 0))


def _win_fwd(q, k, vT, dist, slope, sink, *, kdiv, tq, nbs, name):
    S = k.shape[0]; nb = q.shape[1] // LANES; nq = S // tq; nk = min(tq + 2 * WINDOW, S)
    assert nb % nbs == 0 and nbs % kdiv == 0
    kvw = (nbs // kdiv) * LANES

    def body(q_ref, k_ref, vT_ref, dist_ref, slope_ref, sink_ref, o_ref, lse_ref):
        i = pl.program_id(1)
        rlo = _row_lo()
        k0 = _win_start(i, tq, nk, S)
        kk, vv, dd = k_ref[pl.ds(k0, nk), :], vT_ref[:, pl.ds(k0, nk)], dist_ref[0]
        for u in range(nbs):
            kv = slice(LANES * (u // kdiv), LANES * (u // kdiv + 1))
            qsT = _stack_cols(q_ref[:, LANES * u:LANES * (u + 1)].astype(F32).T, rlo).astype(BF16)
            s = _dot(kk[:, kv], qsT) - slope_ref[u] * dd
            sk = sink_ref[u]
            m = jnp.maximum(jnp.max(s, axis=0, keepdims=True), sk)
            p = jnp.exp(s - m)
            l = jnp.sum(p, axis=0, keepdims=True) + jnp.exp(sk - m)
            o_ref[:, LANES * u:LANES * (u + 1)] = _pick_halves_T(_dot(vv[kv, :], p.astype(BF16)) / l, rlo, tq).astype(BF16)
            lse_ref[u, 0] = m + jnp.log(l)

    row_spec = pl.BlockSpec((nbs, 1, 2 * tq), lambda b, i: (b, 0, 0))
    return pl.pallas_call(
        body, name=name, grid=(nb // nbs, nq),
        in_specs=[pl.BlockSpec((tq, nbs * LANES), lambda b, i: (i, b)), pl.BlockSpec((S, kvw), lambda b, i: (0, b)),
                  pl.BlockSpec((kvw, S), lambda b, i: (b, 0)), _win_dist_spec(nk, tq, nq), row_spec, row_spec],
        out_specs=[pl.BlockSpec((tq, nbs * LANES), lambda b, i: (i, b)), pl.BlockSpec((nbs, 1, 1, 2 * tq), lambda b, i: (b, i, 0, 0))],
        out_shape=[_sds((S, nb * LANES), BF16), _sds((nb, nq, 1, 2 * tq), F32)],
        compiler_params=_params(("arbitrary", "arbitrary")))(q, k, vT, dist, slope, sink)


def _win_bwd(q, k, kT, v, o, do, lse, dist, slope, sink, *, kdiv, tq, nbs, name):
    S = k.shape[0]; nb = q.shape[1] // LANES; nkb = k.shape[1] // LANES; nq = S // tq; nk = min(tq + 2 * WINDOW, S)
    assert nb % nbs == 0 and nbs % kdiv == 0
    nkv = nbs // kdiv
    kvw = nkv * LANES

    def body(q_ref, k_ref, kT_ref, v_ref, o_ref, do_ref, lse_ref, dist_ref, slope_ref, sink_ref, dq_ref, dk_ref, dv_ref, dsink_ref, ds_acc):
        i = pl.program_id(1)
        rlo = _row_lo()
        lo = lax.broadcasted_iota(jnp.int32, (1, LANES), 1) < HEAD_DIM

        @pl.when(i == 0)
        def _():
            dk_ref[...] = jnp.zeros((S, kvw), F32)
            dv_ref[...] = jnp.zeros((S, kvw), F32)
            ds_acc[...] = jnp.zeros((nbs, 2 * tq), F32)

        k0 = _win_start(i, tq, nk, S)
        rows = pl.ds(k0, nk)
        kk_all, vv_all, kkT_all, dd = k_ref[rows, :], v_ref[rows, :], kT_ref[:, rows], dist_ref[0]
        dv_sum, dk_sum = [None] * nkv, [None] * nkv
        for u in range(nbs):
            g = u // kdiv
            kv = slice(LANES * g, LANES * (g + 1))
            kk, vv, kkT = kk_all[:, kv], vv_all[:, kv], kkT_all[kv, :]
            cols = slice(LANES * u, LANES * (u + 1))
            qv = q_ref[:, cols]
            qs = _stack_rows(qv, lo)
            qsT = _stack_cols(qv.astype(F32).T, rlo).astype(BF16)
            dov = do_ref[:, cols].astype(F32)
            dos = _stack_rows(dov.astype(BF16), lo)
            dosT = _stack_cols(dov.T, rlo).astype(BF16)
            prodT = (dov * o_ref[:, cols].astype(F32)).T
            delta = jnp.concatenate([jnp.sum(jnp.where(rlo, prodT, 0.0), axis=0, keepdims=True),
                                     jnp.sum(jnp.where(rlo, 0.0, prodT), axis=0, keepdims=True)], axis=1)
            lse_v = lse_ref[u, 0]
            ds_acc[u:u + 1, :] += -jnp.exp(sink_ref[u] - lse_v) * delta
            p = jnp.exp(_dot(kk, qsT) - slope_ref[u] * dd - lse_v)
            ds = (p * (_dot(vv, dosT) - delta)).astype(BF16)
            dv_u, dk_u = _dot(p.astype(BF16), dos), _dot(ds, qs)
            dv_sum[g] = dv_u if dv_sum[g] is None else dv_sum[g] + dv_u
            dk_sum[g] = dk_u if dk_sum[g] is None else dk_sum[g] + dk_u
            dq_ref[:, cols] = (_pick_halves_T(_dot(kkT, ds), rlo, tq) * 0.125).astype(BF16)
        dv_ref[rows, :] += jnp.concatenate(dv_sum, axis=1)
        dk_ref[rows, :] += jnp.concatenate(dk_sum, axis=1)

        @pl.when(i == nq - 1)
        def _():
            acc = ds_acc[...]
            for u in range(nbs):
                dsink_ref[u] = jnp.concatenate(
                    [jnp.broadcast_to(jnp.sum(acc[u:u + 1, 0:tq], axis=1, keepdims=True), (1, LANES)),
                     jnp.broadcast_to(jnp.sum(acc[u:u + 1, tq:2 * tq], axis=1, keepdims=True), (1, LANES)),
                     jnp.zeros((6, LANES), F32)], axis=0)

    qmap = lambda b, i: (i, b)
    kv_spec = pl.BlockSpec((S, kvw), lambda b, i: (0, b))
    row_spec = pl.BlockSpec((nbs, 1, 2 * tq), lambda b, i: (b, 0, 0))
    wide = pl.BlockSpec((tq, nbs * LANES), qmap)
    return pl.pallas_call(
        body, name=name, grid=(nb // nbs, nq),
        in_specs=[wide, kv_spec, pl.BlockSpec((kvw, S), lambda b, i: (b, 0)), kv_spec, wide, wide,
                  pl.BlockSpec((nbs, 1, 1, 2 * tq), lambda b, i: (b, i, 0, 0)), _win_dist_spec(nk, tq, nq), row_spec, row_spec],
        out_specs=[wide, kv_spec, kv_spec, pl.BlockSpec((nbs, 8, LANES), lambda b, i: (b, 0, 0))],
        out_shape=[_sds((S, nb * LANES), BF16), _sds((S, nkb * LANES), F32), _sds((S, nkb * LANES), F32), _sds((nb, 8, LANES), F32)],
        scratch_shapes=[pltpu.VMEM((nbs, 2 * tq), F32)],
        compiler_params=_params(("arbitrary", "arbitrary")))(q, k, kT, v, o, do, lse, dist, slope, sink)


def _sum_rows(v):
    return jnp.sum(v, axis=0, keepdims=True)


def _norm_mod_bwd(dh, xv, mod_ref, nw_ref, stats_ref):
    r = _rms(xv)
    xn = xv * r
    nw = nw_ref[...]
    stats_ref[0:1, :] += _sum_rows(dh)
    stats_ref[1:2, :] += _sum_rows(dh * (xn * nw))
    dn = dh * (1.0 + mod_ref[1:2, :])
    stats_ref[2:3, :] += _sum_rows(dn * xn)
    return _rms_bwd(xv, r, dn * nw)


def _even_gate_specs(ts):
    return [pl.BlockSpec((ts, 256), lambda i, c=c: (i, c)) for c in (3, 4, 7, 8)]


def _even_post_fwd(oa, olat, proj, x, gate, wuv, woe):
    S = x.shape[0]
    ts = min(ROW_TILE, S)

    def body(oa_ref, ol_ref, ga0_ref, ga1_ref, gb0_ref, gb1_ref, x_ref, gate_ref, wuv_ref, woe_ref, y_ref, x1_ref):
        sa, _ = _silu_and_grad(jnp.concatenate([ga0_ref[...], ga1_ref[...]], axis=1))
        sb, _ = _silu_and_grad(jnp.concatenate([gb0_ref[...], gb1_ref[...]], axis=1))
        olc = jnp.concatenate([ol_ref[hh] for hh in range(B_HEADS)], axis=1).astype(BF16)
        ob = _dot(olc, wuv_ref[...])
        mix = jnp.concatenate([oa_ref[...] * sa, ob * sb], axis=1).astype(BF16)
        y = _dot(mix, woe_ref[...])
        y_ref[...] = y.astype(BF16)
        x1_ref[...] = x_ref[...] + gate_ref[...] * y

    return pl.pallas_call(
        body, name="even_post_fwd", grid=(S // ts,),
        in_specs=[_row_spec(ts, 512), pl.BlockSpec((B_HEADS, ts, LANES), lambda i: (0, i, 0))] + _even_gate_specs(ts) +
                 [_row_spec(ts, D_MODEL), _full_spec((1, D_MODEL)), _full_spec((1024, 512)), _full_spec((1024, D_MODEL))],
        out_specs=[_row_spec(ts, D_MODEL), _row_spec(ts, D_MODEL)],
        out_shape=[_sds((S, D_MODEL), BF16), _sds((S, D_MODEL), F32)],
        compiler_params=_params(("arbitrary",)),
    )(oa, olat, proj, proj, proj, proj, x, gate, wuv, woe)


def _odd_pre_fwd(x, mod, nw, wio):
    S = x.shape[0]
    ts = min(ROW_TILE, S)

    def body(x_ref, mod_ref, nw_ref, wio_ref, h_ref, g_ref, q_ref, k_ref, v_ref, kt_ref, vt_ref):
        xv = x_ref[...]
        h = (xv * _rms(xv) * nw_ref[...]) * (1.0 + mod_ref[1:2, :]) + mod_ref[0:1, :]
        hb = h.astype(BF16)
        h_ref[...] = hb
        proj = jnp.concatenate([_dot(hb, wio_ref[p]) for p in range(N_CHIPS)], axis=1)
        g_ref[...] = proj[:, 1536:2560]
        q_ref[...] = (proj[:, 0:1024] * 0.125).astype(BF16)
        lane = _lane_iota()
        k_v = jnp.concatenate([_dup_heads(proj[:, 1024 + LANES * j:1024 + LANES * (j + 1)], lane) for j in range(2)], axis=1)
        v_v = jnp.concatenate([_dup_heads(proj[:, 1280 + LANES * j:1280 + LANES * (j + 1)], lane) for j in range(2)], axis=1)
        k_ref[...] = k_v.astype(BF16)
        v_ref[...] = v_v.astype(BF16)
        kt_ref[...] = k_v.T.astype(BF16)
        vt_ref[...] = v_v.T.astype(BF16)

    col_spec = pl.BlockSpec((512, ts), lambda i: (0, i))
    return pl.pallas_call(
        body, name="odd_pre_fwd", grid=(S // ts,),
        in_specs=[_row_spec(ts, D_MODEL), _full_spec((3, D_MODEL)), _full_spec((1, D_MODEL)),
                  _full_spec((N_CHIPS, D_MODEL, ODD_IN // N_CHIPS))],
        out_specs=[_row_spec(ts, D_MODEL), _row_spec(ts, 1024), _row_spec(ts, 1024), _row_spec(ts, 512), _row_spec(ts, 512),
                   col_spec, col_spec],
        out_shape=[_sds((S, D_MODEL), BF16), _sds((S, 1024), F32), _sds((S, 1024), BF16), _sds((S, 512), BF16),
                   _sds((S, 512), BF16), _sds((512, S), BF16), _sds((512, S), BF16)],
        compiler_params=_params(("arbitrary",)),
    )(x, mod, nw, wio)


def _odd_post(oc, g, x1, gate, woo, fw, tgt):
    S = x1.shape[0]
    ts = min(ROW_TILE, S)
    nsteps = S // ts

    def body(oc_ref, g_ref, x_ref, gate_ref, woo_ref, fw_ref, tgt_ref, doc_ref, dgc_ref, dx2_ref, dwoo_out, stats_ref, dwoo_ref):
        @pl.when(pl.program_id(0) == 0)
        def _():
            dwoo_ref[...] = jnp.zeros((D_MODEL, D_MODEL), F32)
            stats_ref[...] = jnp.zeros((8, D_MODEL), F32)

        ocv = oc_ref[...]
        sg, dsg = _silu_and_grad(g_ref[...])
        mix = (ocv * sg).astype(BF16)
        woo_v = woo_ref[...]
        y = _dot(mix, woo_v)
        gate_v = gate_ref[...]
        x2 = x_ref[...] + gate_v * y
        r = _rms(x2)
        fw_v = fw_ref[...]
        xn = x2 * r
        err = xn * fw_v - tgt_ref[...]
        dout = err * (1.0 / D_MODEL)
        dx2 = _rms_bwd(x2, r, dout * fw_v)
        dx2_ref[...] = dx2
        stats_ref[0:1, :] += _sum_rows(dout * xn)
        stats_ref[1:2, :] += _sum_rows(dx2 * y)
        loss_t = 0.5 * jnp.sum(_sum_rows(err * dout), axis=-1, keepdims=True)
        stats_ref[2:3, :] += jnp.broadcast_to(loss_t, (1, D_MODEL))
        dy = (gate_v * dx2).astype(BF16)
        dmix = _dot_nt(dy, woo_v)
        dwoo_ref[...] += _dot_tn(mix, dy)
        doc_ref[...] = (dmix * sg).astype(BF16)
        dgc_ref[...] = (dmix * ocv * dsg).astype(BF16)

        @pl.when(pl.program_id(0) == nsteps - 1)
        def _():
            dwoo_out[...] = dwoo_ref[...].astype(BF16)

    return pl.pallas_call(
        body, name="odd_post", grid=(nsteps,),
        in_specs=[_row_spec(ts, D_MODEL), _row_spec(ts, D_MODEL), _row_spec(ts, D_MODEL), _full_spec((1, D_MODEL)),
                  _full_spec((D_MODEL, D_MODEL)), _full_spec((1, D_MODEL)), _row_spec(ts, D_MODEL)],
        out_specs=[_row_spec(ts, D_MODEL), _row_spec(ts, D_MODEL), _row_spec(ts, D_MODEL),
                   _full_spec((D_MODEL, D_MODEL), single=False), _full_spec((8, D_MODEL), single=False)],
        out_shape=[_sds((S, D_MODEL), BF16), _sds((S, D_MODEL), BF16), _sds((S, D_MODEL), F32), _sds((D_MODEL, D_MODEL), BF16),
                   _sds((8, D_MODEL), F32)],
        scratch_shapes=[pltpu.VMEM((D_MODEL, D_MODEL), F32)],
        compiler_params=_params(("arbitrary",)),
    )(oc, g, x1, gate, woo, fw, tgt)


def _odd_pre_bwd(dq, dk, dv, dgc, h, x, dx_res, mod, nw, wio):
    S = x.shape[0]
    ts = min(IN_PROJ_ROW_TILE, S)
    nsteps = S // ts
    wsh = ODD_IN // N_CHIPS

    def body(dq_ref, dk_ref, dv_ref, dgc_ref, h_ref, x_ref, dxr_ref, mod_ref, nw_ref, wio_ref, dx_ref, dw_ref, stats_ref, dw_acc):
        @pl.when(pl.program_id(0) == 0)
        def _():
            dw_acc[...] = jnp.zeros((N_CHIPS, D_MODEL, wsh), F32)
            stats_ref[...] = jnp.zeros((8, D_MODEL), F32)

        lane = _lane_iota()
        dkv = [_fold_heads(r[:, 2 * LANES * j:2 * LANES * (j + 1)], lane).astype(BF16) for r in (dk_ref, dv_ref) for j in range(2)]
        dproj = jnp.concatenate([dq_ref[...]] + dkv + [dgc_ref[...]], axis=1)
        hv = h_ref[...]
        dh = None
        for p in range(N_CHIPS):
            dp_cols = dproj[:, wsh * p:wsh * (p + 1)]
            part = _dot_nt(dp_cols, wio_ref[p])
            dh = part if dh is None else dh + part
            dw_acc[p] += _dot_tn(hv, dp_cols)
        dx_ref[...] = dxr_ref[...] + _norm_mod_bwd(dh, x_ref[...], mod_ref, nw_ref, stats_ref)

        @pl.when(pl.program_id(0) == nsteps - 1)
        def _():
            dw_ref[...] = dw_acc[...].astype(BF16)

    return pl.pallas_call(
        body, name="odd_pre_bwd", grid=(nsteps,),
        in_specs=[_row_spec(ts, 1024), _row_spec(ts, 512), _row_spec(ts, 512), _row_spec(ts, 1024), _row_spec(ts, D_MODEL),
                  _row_spec(ts, D_MODEL), _row_spec(ts, D_MODEL), _full_spec((3, D_MODEL)), _full_spec((1, D_MODEL)),
                  _full_spec((N_CHIPS, D_MODEL, wsh))],
        out_specs=[_row_spec(ts, D_MODEL), _full_spec((N_CHIPS, D_MODEL, wsh), single=False), _full_spec((8, D_MODEL), single=False)],
        out_shape=[_sds((S, D_MODEL), F32), _sds((N_CHIPS, D_MODEL, wsh), BF16), _sds((8, D_MODEL), F32)],
        scratch_shapes=[pltpu.VMEM((N_CHIPS, D_MODEL, wsh), F32)],
        compiler_params=_params(("arbitrary",)),
    )(dq, dk, dv, dgc, h, x, dx_res, mod, nw, wio)


def _even_post_bwd(dx1, y, oa, olat, proj, gate, wuv, woe):
    S = dx1.shape[0]
    ts = min(ROW_TILE, S)
    nsteps = S // ts

    def body(dx_ref, y_ref, oa_ref, ol_ref, ga0_ref, ga1_ref, gb0_ref, gb1_ref, gate_ref, wuv_ref, woe_ref,
             doa_ref, dga_ref, dgb_ref, dol_ref, dwoe_out, dwuv_ref, stats_ref, dwoe_ref):
        @pl.when(pl.program_id(0) == 0)
        def _():
            dwoe_ref[...] = jnp.zeros((D_MODEL, D_MODEL), F32)
            dwuv_ref[...] = jnp.zeros((1024, 512), F32)
            stats_ref[...] = jnp.zeros((8, D_MODEL), F32)

        dxv = dx_ref[...]
        stats_ref[0:1, :] += _sum_rows(dxv * y_ref[...])
        dy = (gate_ref[...] * dxv).astype(BF16)
        sa, dsa = _silu_and_grad(jnp.concatenate([ga0_ref[...], ga1_ref[...]], axis=1))
        sb, dsb = _silu_and_grad(jnp.concatenate([gb0_ref[...], gb1_ref[...]], axis=1))
        olc = jnp.concatenate([ol_ref[hh] for hh in range(B_HEADS)], axis=1).astype(BF16)
        wuv_v = wuv_ref[...]
        ob = _dot(olc, wuv_v)
        oav = oa_ref[...]
        mix = jnp.concatenate([oav * sa, ob * sb], axis=1).astype(BF16)
        dmix = _dot_nt(dy, woe_ref[...])
        dwoe_ref[...] += _dot_tn(mix, dy)
        dma, dmb = dmix[:, 0:512], dmix[:, 512:1024]
        doa_ref[...] = (dma * sa).astype(BF16)
        dga_ref[...] = (dma * oav * dsa).astype(BF16)
        dgb_ref[...] = (dmb * ob * dsb).astype(BF16)
        dob = (dmb * sb).astype(BF16)
        dol = _dot_nt(dob, wuv_v)
        dwuv_ref[...] += _dot_tn(olc, dob)
        for hh in range(B_HEADS):
            dol_ref[hh] = dol[:, LANES * hh:LANES * (hh + 1)].astype(BF16)

        @pl.when(pl.program_id(0) == nsteps - 1)
        def _():
            dwoe_out[...] = dwoe_ref[...].astype(BF16)

    head_spec = pl.BlockSpec((B_HEADS, ts, LANES), lambda i: (0, i, 0))
    return pl.pallas_call(
        body, name="even_post_bwd", grid=(nsteps,),
        in_specs=[_row_spec(ts, D_MODEL), _row_spec(ts, D_MODEL), _row_spec(ts, 512), head_spec] + _even_gate_specs(ts) +
                 [_full_spec((1, D_MODEL)), _full_spec((1024, 512)), _full_spec((1024, D_MODEL))],
        out_specs=[_row_spec(ts, 512), _row_spec(ts, 512), _row_spec(ts, 512), head_spec,
                   _full_spec((D_MODEL, D_MODEL), single=False), _full_spec((1024, 512), single=False),
                   _full_spec((8, D_MODEL), single=False)],
        out_shape=[_sds((S, 512), BF16), _sds((S, 512), BF16), _sds((S, 512), BF16), _sds((B_HEADS, S, LANES), BF16),
                   _sds((D_MODEL, D_MODEL), BF16), _sds((1024, 512), F32), _sds((8, D_MODEL), F32)],
        scratch_shapes=[pltpu.VMEM((D_MODEL, D_MODEL), F32)],
        compiler_params=_params(("arbitrary",)),
    )(dx1, y, oa, olat, proj, proj, proj, proj, gate, wuv, woe)


def _even_pre_bwd(x, h, proj, dqa, dka, dva, dga, dgb, dqcat, dkcat, dx_res, mod, nw, wie, qn, kn, seg, ca, sa, ct, st,
                  qln, kvln, wuq, wuk):
    S = x.shape[0]
    ts = min(IN_PROJ_ROW_TILE, S)
    nsteps = S // ts

    def body(x_ref, h_ref, proj_ref, dqa_ref, dka_ref, dva_ref, dga_ref, dgb_ref, dqc_ref, dkc_ref, dxr_ref, mod_ref, nw_ref,
             wie_ref, qn_ref, kn_ref, seg_ref, ca_ref, sa_ref, ct_ref, st_ref, qln_ref, kvln_ref, wuq_ref, wuk_ref,
             dx_ref, dwie_out, dwuq_out, dwuk_out, stats_ref, nstats_ref, dwie_ref, dwuq_ref, dwuk_ref, stage):
        @pl.when(pl.program_id(0) == 0)
        def _():
            dwie_ref[...] = jnp.zeros((D_MODEL, EVEN_P), F32)
            dwuq_ref[...] = jnp.zeros((B_Q_LORA, 1536), F32)
            dwuk_ref[...] = jnp.zeros((512, 1024), F32)
            stats_ref[...] = jnp.zeros((8, D_MODEL), F32)
            nstats_ref[...] = jnp.zeros((8, 256), F32)

        lane = _lane_iota()
        ca_v, sa_v, ct_v, st_v = ca_ref[...], sa_ref[...], ct_ref[...], st_ref[...]
        seg_v = seg_ref[...]

        def head_norm_bwd(xc, dy, w):
            r = lax.rsqrt(_seg_mean(xc * xc, seg_v) + EPS)
            g = dy * w
            dxc = r * g - xc * (r * r * r) * _seg_mean(xc * g, seg_v)
            return dxc, _sum_rows(dy * (xc * r))

        pieces = []
        dqn = jnp.zeros((1, LANES), F32)
        for cb in range(4):
            sl = slice(LANES * cb, LANES * (cb + 1))
            dy = _rot_bwd(dqa_ref[:, sl] * 0.125, ca_v, sa_v, lane)
            dxc, dw = head_norm_bwd(proj_ref[:, sl], dy, qn_ref[...])
            pieces.append(dxc)
            dqn = dqn + dw
        dxc, dkn = head_norm_bwd(proj_ref[:, 512:640], _rot_bwd(_fold_heads(dka_ref[...], lane), ca_v, sa_v, lane), kn_ref[...])
        pieces += [dxc, _fold_heads(dva_ref[...], lane), dga_ref[...]]
        nstats_ref[0:1, 0:LANES] += dqn + pltpu.roll(dqn, HEAD_DIM, 1)
        nstats_ref[1:2, 0:LANES] += dkn + pltpu.roll(dkn, HEAD_DIM, 1)

        cq = proj_ref[:, 1280:1536]
        rq = _rms(cq)
        cqn_f = cq * rq
        qln_v = qln_ref[...]
        cqn = (cqn_f * qln_v).astype(BF16)
        wuq_v, wuk_v = wuq_ref[...], wuk_ref[...]
        qnope = _dot(cqn, wuq_v[:, 0:512]).astype(BF16)
        dqlat = jnp.concatenate([dqc_ref[hh, :, 0:LANES] for hh in range(B_HEADS)], axis=1).astype(BF16)
        dqnope = _dot_nt(dqlat, wuk_v)
        dwuk_ref[...] += _dot_tn(qnope, dqlat)
        dqr = [_rot_bwd(dqc_ref[hh, :, LANES:2 * LANES], ct_v, st_v, lane) for hh in range(B_HEADS)]
        dqb = jnp.concatenate([dqnope] + dqr, axis=1).astype(BF16)
        dcqn = _dot_nt(dqb, wuq_v)
        dwuq_ref[...] += _dot_tn(cqn, dqb)
        nstats_ref[2:3, :] += _sum_rows(dcqn * cqn_f)
        dcq = _rms_bwd(cq, rq, dcqn * qln_v)
        ckv = proj_ref[:, 1536:1664]
        rk = _rms(ckv)
        dckvn = dkc_ref[:, 0:LANES]
        nstats_ref[3:4, 0:LANES] += _sum_rows(dckvn * (ckv * rk))
        dckv = _rms_bwd(ckv, rk, dckvn * kvln_ref[...])
        dkr = _rot_bwd(dkc_ref[:, LANES:2 * LANES], ct_v, st_v, lane)
        pieces += [dcq, dckv, dkr, dgb_ref[...]]
        dproj = jnp.concatenate([piece.astype(BF16) for piece in pieces], axis=1)
        dh = _dot_nt(dproj, wie_ref[...])
        dwie_ref[...] += _dot_tn(h_ref[...], dproj)
        dx_ref[...] = dxr_ref[...] + _norm_mod_bwd(dh, x_ref[...], mod_ref, nw_ref, stats_ref)

        @pl.when(pl.program_id(0) == nsteps - 1)
        def _():
            for r0 in range(0, D_MODEL, 256):
                stage[...] = dwie_ref[r0:r0 + 256, :].astype(BF16)
                pltpu.sync_copy(stage, dwie_out.at[pl.ds(r0, 256), :])
            pltpu.sync_copy(dwuq_ref, dwuq_out)
            pltpu.sync_copy(dwuk_ref, dwuk_out)

    return pl.pallas_call(
        body, name="even_pre_bwd", grid=(nsteps,),
        in_specs=[_row_spec(ts, D_MODEL), _row_spec(ts, D_MODEL), _row_spec(ts, EVEN_P), _row_spec(ts, 512), _row_spec(ts, 2 * LANES),
                  _row_spec(ts, 2 * LANES), _row_spec(ts, 512), _row_spec(ts, 512),
                  pl.BlockSpec((B_HEADS, ts, 2 * LANES), lambda i: (0, i, 0)), _row_spec(ts, 2 * LANES), _row_spec(ts, D_MODEL),
                  _full_spec((3, D_MODEL)), _full_spec((1, D_MODEL)), _full_spec((D_MODEL, EVEN_P)),
                  _full_spec((1, LANES)), _full_spec((1, LANES)), _full_spec((LANES, LANES)),
                  _row_spec(ts, LANES), _row_spec(ts, LANES), _row_spec(ts, LANES), _row_spec(ts, LANES),
                  _full_spec((1, B_Q_LORA)), _full_spec((1, B_KV_LORA)), _full_spec((B_Q_LORA, 1536)), _full_spec((512, 1024))],
        out_specs=[_row_spec(ts, D_MODEL), _ANY, _ANY, _ANY, _full_spec((8, D_MODEL), single=False), _full_spec((8, 256), single=False)],
        out_shape=[_sds((S, D_MODEL), F32), _sds((D_MODEL, EVEN_P), BF16), _sds((B_Q_LORA, 1536), F32), _sds((512, 1024), F32),
                   _sds((8, D_MODEL), F32), _sds((8, 256), F32)],
        scratch_shapes=[pltpu.VMEM((D_MODEL, EVEN_P), F32), pltpu.VMEM((B_Q_LORA, 1536), F32), pltpu.VMEM((512, 1024), F32),
                        pltpu.VMEM((256, EVEN_P), BF16)],
        compiler_params=_params(("arbitrary",)),
    )(x, h, proj, dqa, dka, dva, dga, dgb, dqcat, dkcat, dx_res, mod, nw, wie, qn, kn, seg, ca, sa, ct, st, qln, kvln, wuq, wuk)


def _ada_fwd(c_all, w, b):
    n = w.shape[2]

    def body(c_ref, w_ref, b_ref, o_ref):
        cv = c_ref[...]
        o_ref[0] = _dot_f32(cv * _sigmoid(cv), w_ref[0]) + b_ref[0]

    return pl.pallas_call(
        body, name="ada_fwd", grid=(2,),
        in_specs=[pl.BlockSpec((N_DEV, D_MODEL), lambda l: (0, 0)), pl.BlockSpec((1, D_MODEL, n), lambda l: (l, 0, 0)),
                  pl.BlockSpec((1, 1, n), lambda l: (l, 0, 0))],
        out_specs=pl.BlockSpec((1, N_DEV, n), lambda l: (l, 0, 0)),
        out_shape=_sds((2, N_DEV, n), F32),
        compiler_params=_params(("arbitrary",)),
    )(c_all, w, b)


def _ada_bwd(c_all_t, dmod):
    n = dmod.shape[2]

    def body(c_ref, d_ref, o_ref):
        cv = c_ref[...]
        act = cv * _sigmoid(cv)
        dv = d_ref[0]
        acc = act[:, 0:1] * dv[0:1, :]
        for bb in range(1, N_DEV):
            acc = acc + act[:, bb:bb + 1] * dv[bb:bb + 1, :]
        o_ref[0] = acc

    return pl.pallas_call(
        body, name="ada_bwd", grid=(2,),
        in_specs=[pl.BlockSpec((D_MODEL, N_DEV), lambda l: (0, 0)), pl.BlockSpec((1, N_DEV, n), lambda l: (l, 0, 0))],
        out_specs=pl.BlockSpec((1, D_MODEL, n), lambda l: (l, 0, 0)),
        out_shape=_sds((2, D_MODEL, n), F32),
        compiler_params=_params(("arbitrary",)),
    )(c_all_t, dmod)


ADAM_ROW_TILE = 512


def _adam_update(g, w, m, v):
    m_new = ADAM_B1 * m + (1.0 - ADAM_B1) * g
    v_new = ADAM_B2 * v + (1.0 - ADAM_B2) * jnp.square(g)
    m_hat = m_new / (1.0 - ADAM_B1 ** ADAM_STEP)
    v_hat = v_new / (1.0 - ADAM_B2 ** ADAM_STEP)
    return -ADAM_LR * (m_hat / (jnp.sqrt(v_hat) + ADAM_EPS) + ADAM_WD * w), m_new, v_new


SMALL_ROWS = dict(dmod=(0, D_MODEL), norm_w=(6, D_MODEL), final_norm=(8, D_MODEL), a_q_norm=(9, HEAD_DIM), a_k_norm=(10, HEAD_DIM),
                  b_q_lora_norm=(11, B_Q_LORA), b_kv_lora_norm=(12, B_KV_LORA), c_sink=(13, C_HEADS))
SMALL_WEIGHTS = ("ada_b", "norm_w", "final_norm", "a_q_norm", "a_k_norm", "b_q_lora_norm", "b_kv_lora_norm", "c_sink")
LOSS_ROW = 14


def _pack_small(res):
    def padded(v):
        return jnp.concatenate([v, jnp.zeros((v.shape[0], D_MODEL - v.shape[1]), F32)], axis=1)

    rows = [res["dmod"].reshape(6, D_MODEL), res["norm_w"], res["final_norm"].reshape(1, D_MODEL)]
    rows += [padded(res[k]) for k in ("a_q_norm", "a_k_norm", "b_q_lora_norm", "b_kv_lora_norm", "c_sink")]
    return jnp.concatenate(rows + [res["loss_row"], jnp.zeros((1, D_MODEL), F32)], axis=0)


def _adam_small(parts, ws, ms, vs):
    nw = len(SMALL_WEIGHTS)

    def body(*refs):
        p_ref = refs[0]
        w_refs, m_refs, v_refs = refs[1:1 + nw], refs[1 + nw:1 + 2 * nw], refs[1 + 2 * nw:1 + 3 * nw]
        outs = refs[1 + 3 * nw:]
        g_all = p_ref[0]
        for k in range(1, N_DEV):
            g_all = g_all + p_ref[k]
        for idx, name in enumerate(SMALL_WEIGHTS):
            if name == "ada_b":
                g = jnp.concatenate([jnp.concatenate([g_all[3 * l + t:3 * l + t + 1] for t in range(3)], axis=1) for l in range(2)],
                                    axis=0)
            else:
                row, width = SMALL_ROWS[name]
                g = g_all[row:row + w_refs[idx].shape[0], 0:width]
            d, m_new, v_new = _adam_update(g, w_refs[idx][...], m_refs[idx][...], v_refs[idx][...])
            outs[4 * idx][...], outs[4 * idx + 1][...], outs[4 * idx + 2][...], outs[4 * idx + 3][...] = g, d, m_new, v_new
        outs[4 * nw][...] = g_all[LOSS_ROW:LOSS_ROW + 1, 0:LANES]

    out_shape = []
    for w in ws:
        out_shape += [_sds(w.shape, F32)] * 4
    out_shape.append(_sds((1, LANES), F32))
    return pl.pallas_call(body, name="adam_small", out_shape=out_shape,
                          compiler_params=pltpu.CompilerParams(vmem_limit_bytes=VMEM_LIMIT))(parts, *ws, *ms, *vs)


def _adam(parts, w, m, v, name, by_columns=False):
    P, R, C = parts.shape
    if by_columns:
        tr, tc = R, 256
    else:
        tr, tc = (R if R <= ADAM_ROW_TILE else ADAM_ROW_TILE), C
    assert R % tr == 0 and C % tc == 0

    def body(p_ref, w_ref, m_ref, v_ref, g_ref, d_ref, nm_ref, nv_ref):
        g = p_ref[0].astype(F32)
        for k in range(1, P):
            g = g + p_ref[k].astype(F32)
        g_ref[...] = g
        d_ref[...], nm_ref[...], nv_ref[...] = _adam_update(g, w_ref[...], m_ref[...], v_ref[...])

    tile = (lambda i: (0, i)) if by_columns else (lambda i: (i, 0))
    spec = pl.BlockSpec((tr, tc), tile)
    return pl.pallas_call(
        body, name=name, grid=(C // tc if by_columns else R // tr,),
        in_specs=[pl.BlockSpec((P, tr, tc), lambda i: (0,) + tile(i)), spec, spec, spec],
        out_specs=[spec, spec, spec, spec], out_shape=[_sds((R, C), F32)] * 4,
        compiler_params=_params(("arbitrary",)),
    )(parts, w, m, v)


_ANY = pl.BlockSpec(memory_space=pl.ANY)
CHIP_FLIPS = ((1, 0), (0, 1), (1, 1))
DEV_FLIPS = tuple((dx, dy, dc) for dx in (0, 1) for dy in (0, 1) for dc in (0, 1) if dx + dy + dc)


def _flip(a, d):
    return a if d == 0 else 1 - a


def _my_place():
    return lax.axis_index("x"), lax.axis_index("y"), lax.axis_index("c")


def _gather8_copies(ins, outs, send_sems, recv_sems, loc_sems):
    x, y, c = _my_place()
    me = 4 * x + 2 * y + c
    copies = []
    for a in range(len(ins)):
        copies.append(pltpu.make_async_copy(ins[a], outs[a].at[me], loc_sems.at[a]))
        for k, (dx, dy, dc) in enumerate(DEV_FLIPS):
            copies.append(pltpu.make_async_remote_copy(
                src_ref=ins[a], dst_ref=outs[a].at[me], send_sem=send_sems.at[a, k], recv_sem=recv_sems.at[a, k],
                device_id=(_flip(x, dx), _flip(y, dy), _flip(c, dc)), device_id_type=MESH_ID))
    return copies


def _gather8_sems(n):
    return [pltpu.SemaphoreType.DMA((n, 7)), pltpu.SemaphoreType.DMA((n, 7)), pltpu.SemaphoreType.DMA((n,))]


def _gather_dev8(arrs, name):
    n = len(arrs)

    def body(*refs):
        copies = _gather8_copies(refs[:n], refs[n:2 * n], *refs[2 * n:])
        for cp in copies:
            cp.start()
        for cp in copies:
            cp.wait()

    return pl.pallas_call(
        body, name=name, in_specs=[_ANY] * n, out_specs=[_ANY] * n,
        out_shape=[_sds((N_DEV,) + a.shape, a.dtype) for a in arrs], scratch_shapes=_gather8_sems(n),
    )(*arrs)


class _Exchange:
    def __init__(self, arrs, out_shapes, n_sems, phases):
        self.arrs, self.out_shapes, self.n_sems, self._phases = list(arrs), list(out_shapes), n_sems, phases

    @property
    def n(self):
        return len(self.arrs)

    def sem_shapes(self):
        return [pltpu.SemaphoreType.DMA((self.n, self.n_sems)), pltpu.SemaphoreType.DMA((self.n, self.n_sems)),
                pltpu.SemaphoreType.DMA((self.n,))]

    def phases(self, ins, outs, sems):
        return self._phases(ins, outs, *sems)

    def run(self, name):
        n = self.n

        def body(*refs):
            start, mid, end = self.phases(refs[:n], refs[n:2 * n], refs[2 * n:])
            start()
            mid()
            end()

        return pl.pallas_call(body, name=name, in_specs=[_ANY] * n, out_specs=[_ANY] * n, out_shape=self.out_shapes,
                              scratch_shapes=self.sem_shapes())(*self.arrs)

def _gather_halves_phases(ins, outs, send_sems, recv_sems, loc_sems):
    n = len(ins)
    x, y, c = _my_place()
    chip = 2 * x + y
    sibling = (x, y, 1 - c)
    peers = [(_flip(x, dx), _flip(y, dy)) for dx, dy in CHIP_FLIPS]

    def remote(src, p, half, a, k, to):
        return pltpu.make_async_remote_copy(src_ref=src, dst_ref=outs[a].at[p, half], send_sem=send_sems.at[a, k],
                                            recv_sem=recv_sems.at[a, k], device_id=to, device_id_type=MESH_ID)

    def local(a):
        return pltpu.make_async_copy(ins[a], outs[a].at[chip], loc_sems.at[a])

    def first(a, k):
        return remote(ins[a].at[c], chip, c, a, k, (*peers[k], c))

    def passed(a, k):
        p = 2 * peers[k][0] + peers[k][1]
        return remote(outs[a].at[p, c], p, c, a, 3 + k, sibling)

    def start():
        for a in range(n):
            local(a).start()
            for k in range(3):
                first(a, k).start()

    def mid():
        for a in range(n):
            for k in range(3):
                p = 2 * peers[k][0] + peers[k][1]
                remote(outs[a].at[p, c], p, c, a, k, sibling).wait_recv()
                passed(a, k).start()

    def end():
        for a in range(n):
            for k in range(3):
                p = 2 * peers[k][0] + peers[k][1]
                remote(outs[a].at[p, 1 - c], p, 1 - c, a, 3 + k, sibling).wait_recv()
        for a in range(n):
            for k in range(3):
                first(a, k).wait_send()
                passed(a, k).wait_send()
            local(a).wait()

    return start, mid, end


def _gather_chip4_halves(arrs):
    return _Exchange(arrs, [_sds((N_CHIPS,) + a.shape, a.dtype) for a in arrs], 6, _gather_halves_phases)


def _reduce_phases(n_whole, ins, outs, send_sems, recv_sems, loc_sems):
    n = len(ins)
    x, y, c = _my_place()
    chip = 2 * x + y
    sibling = (x, y, 1 - c)
    peers = [(_flip(x, dx), _flip(y, dy)) for dx, dy in CHIP_FLIPS]

    def remote(src, slot, a, k, to):
        return pltpu.make_async_remote_copy(src_ref=src, dst_ref=outs[a].at[slot], send_sem=send_sems.at[a, k],
                                            recv_sem=recv_sems.at[a, k], device_id=to, device_id_type=MESH_ID)

    def block(a, p):
        return ins[a] if a >= n - n_whole else ins[a].at[p]

    def local(a):
        return pltpu.make_async_copy(block(a, chip), outs[a].at[2 * chip + c], loc_sems.at[a])

    def own(a):
        return remote(block(a, chip), 2 * chip + c, a, 0, sibling)

    def first(a, k):
        return remote(block(a, 2 * peers[k][0] + peers[k][1]), 2 * chip + c, a, 1 + k, (*peers[k], c))

    def passed(a, k):
        slot = 2 * (2 * peers[k][0] + peers[k][1]) + c
        return remote(outs[a].at[slot], slot, a, 4 + k, sibling)

    def start():
        for a in range(n):
            local(a).start()
            own(a).start()
            for k in range(3):
                first(a, k).start()

    def mid():
        for a in range(n):
            for k in range(3):
                slot = 2 * (2 * peers[k][0] + peers[k][1]) + c
                remote(outs[a].at[slot], slot, a, 1 + k, sibling).wait_recv()
                passed(a, k).start()

    def end():
        for a in range(n):
            remote(outs[a].at[2 * chip + 1 - c], 2 * chip + 1 - c, a, 0, sibling).wait_recv()
            for k in range(3):
                slot = 2 * (2 * peers[k][0] + peers[k][1]) + 1 - c
                remote(outs[a].at[slot], slot, a, 4 + k, sibling).wait_recv()
        for a in range(n):
            own(a).wait_send()
            for k in range(3):
                first(a, k).wait_send()
                passed(a, k).wait_send()
            local(a).wait()

    return start, mid, end


def _reduce_exchange(arrs, whole=()):
    shapes = [_sds((N_DEV,) + a.shape[1:], a.dtype) for a in arrs] + [_sds((N_DEV,) + a.shape, a.dtype) for a in whole]
    return _Exchange(list(arrs) + list(whole), shapes, 7, functools.partial(_reduce_phases, len(whole)))


def _even_in_layout(w):
    return jnp.concatenate([w[:, 0:1696], jnp.zeros((w.shape[0], 96), w.dtype), w[:, 1696:2208]], axis=1)


def _even_in_unlayout(g):
    return jnp.concatenate([g[:, 0:1696], g[:, 1792:2304]], axis=1)


def _uq_layout(w):
    per = B_NOPE + B_ROPE
    pad = jnp.zeros((w.shape[0], LANES - B_ROPE), w.dtype)
    nope = [w[:, per * h:per * h + B_NOPE] for h in range(B_HEADS)]
    rope = [jnp.concatenate([w[:, per * h + B_NOPE:per * (h + 1)], pad], axis=1) for h in range(B_HEADS)]
    return jnp.concatenate(nope + rope, axis=1)


def _uq_unlayout(g):
    parts = []
    for h in range(B_HEADS):
        parts += [g[:, B_NOPE * h:B_NOPE * (h + 1)], g[:, 512 + LANES * h:512 + LANES * h + B_ROPE]]
    return jnp.concatenate(parts, axis=1)


def _block_diag(blocks):
    rows = []
    for h, blk in enumerate(blocks):
        r, cdim = blk.shape
        n = len(blocks)
        rows.append(jnp.concatenate([jnp.zeros((r, cdim * h), blk.dtype), blk, jnp.zeros((r, cdim * (n - 1 - h)), blk.dtype)],
                                    axis=1))
    return jnp.concatenate(rows, axis=0)


def _uk_layout(w):
    return _block_diag([w[:, h, :].T for h in range(B_HEADS)])


def _uk_unlayout(g):
    return jnp.stack([g[B_NOPE * h:B_NOPE * (h + 1), LANES * h:LANES * (h + 1)].T for h in range(B_HEADS)], axis=1)


def _uv_layout(w):
    return _block_diag([w[:, h, :] for h in range(B_HEADS)])


def _uv_unlayout(g):
    return jnp.stack([g[LANES * h:LANES * (h + 1), B_V * h:B_V * (h + 1)] for h in range(B_HEADS)], axis=1)


def _rope_tables(S):
    inv = ROPE_THETA ** (-jnp.arange(0, 32, 2, dtype=F32) / 32)
    tok = jnp.arange(S)

    def tab(pos):
        ang = pos.astype(F32)[:, None] * inv[None, :]
        cos, sin = jnp.cos(ang), jnp.sin(ang)
        return jnp.concatenate([cos, cos], axis=1), jnp.concatenate([-sin, sin], axis=1)

    cr, sr = tab(tok // GRID_W)
    cc, sc = tab(tok % GRID_W)
    ct, st = tab(tok)
    return (jnp.tile(jnp.concatenate([cr, cc], axis=1), (1, 2)), jnp.tile(jnp.concatenate([sr, sc], axis=1), (1, 2)),
            jnp.tile(ct, (1, 4)), jnp.tile(st, (1, 4)))


A_TQ, A_TK, A_SUB = 512, 4096, 512
A_FWD_SUB = 1024
B_TQ, B_TK, B_SUB = 128, 4096, 1024
B_BWD_TK, B_BWD_SUB = 4096, 512
C_T = 256
C_BLOCKS_PER_STEP = 8
KV_SHARE = 2


def _local_step(x0, tgt, mod, norm_w, wie, wuq, wuk, wuv, late_shards, a_q_norm, a_k_norm, q_lora_norm, kv_lora_norm,
                c_sink, final_norm):
    S = x0.shape[0]
    mod3 = mod.reshape(2, 3, D_MODEL)
    ca, sa, ct, st = _rope_tables(S)
    lane_seg = np.arange(LANES) // HEAD_DIM
    seg = jnp.asarray((lane_seg[:, None] == lane_seg[None, :]).astype(np.float32)).astype(BF16)
    qn = jnp.tile(a_q_norm.reshape(1, HEAD_DIM), (1, 2))
    kn = jnp.tile(a_k_norm.reshape(1, HEAD_DIM), (1, 2))
    qln, kvln = q_lora_norm.reshape(1, B_Q_LORA), kv_lora_norm.reshape(1, B_KV_LORA)
    nw0, nw1 = norm_w[0:1], norm_w[1:2]
    gate0, gate1 = mod3[0, 2:3], mod3[1, 2:3]
    a_tq, a_tk, b_tq, b_tk, bb_tk, c_t = min(A_TQ, S), min(A_TK, S), min(B_TQ, S), min(B_TK, S), min(B_BWD_TK, S), min(C_T, S)
    a_sub, b_sub, bb_sub = min(A_SUB, a_tk), min(B_SUB, b_tk), min(B_BWD_SUB, bb_tk)

    h0, proj_e, qa, ka, va, qcat, kcat, ka_t, va_t, kcat_t = _even_pre_fwd(x0, mod3[0], nw0, wie, qn, kn, seg, ca, sa, ct, st,
                                                                           qln, kvln, wuq, wuk)
    oa, lse_a, woe_g, wio_g, woo_g = _pp_fwd(qa, ka, va_t, kdiv=KV_SHARE, tq=a_tq, tk=a_tk, sub=min(A_FWD_SUB, a_tk), name="attn_a_fwd",
                                             side=_gather_chip4_halves(late_shards))
    woe = woe_g.reshape(D_MODEL, D_MODEL)
    wio = wio_g.reshape(N_CHIPS, D_MODEL, ODD_IN // N_CHIPS)
    woo = woo_g.reshape(D_MODEL, D_MODEL)
    olat, lse_b = _mla_fwd(qcat, kcat, kcat_t, tq=b_tq, tk=b_tk, sub=b_sub)
    y0, x1 = _even_post_fwd(oa, olat, proj_e, x0, gate0, wuv, woe)
    h1, gc, qc, kc, vc, kc_t, vc_t = _odd_pre_fwd(x1, mod3[1], nw1, wio)
    slopes = 2.0 ** (-8.0 * jnp.arange(1, C_HEADS + 1, dtype=F32) / C_HEADS)
    slope_rows = jnp.repeat(slopes.reshape(C_HEADS // 2, 2), c_t, axis=1)[:, None, :]
    sink_rows = jnp.repeat(c_sink.reshape(C_HEADS // 2, 2), c_t, axis=1)[:, None, :]
    win_dist = _win_dist_table(S, c_t)
    oc, lse_c = _win_fwd(qc, kc, vc_t, win_dist, slope_rows, sink_rows, kdiv=KV_SHARE, tq=c_t, nbs=C_BLOCKS_PER_STEP,
                         name="attn_c_fwd")
    doc, dgc, dx2, dwoo, st_f = _odd_post(oc, gc, x1, gate1, woo, final_norm.reshape(1, D_MODEL), tgt)
    dqc, dkc, dvc, dsink_raw = _win_bwd(qc, kc, kc_t, vc, oc, doc, lse_c, win_dist, slope_rows, sink_rows, kdiv=KV_SHARE, tq=c_t,
                                        nbs=C_BLOCKS_PER_STEP, name="attn_c_bwd")
    dx1, dwio, st_1 = _odd_pre_bwd(dqc, dkc, dvc, dgc, h1, x1, dx2, mod3[1], nw1, wio)
    doa, dga, dgb, dolat, dwoe, dwuv, st_e = _even_post_bwd(dx1, y0, oa, olat, proj_e, gate0, wuv, woe)
    late_grads = _reduce_exchange([dwoe.reshape(N_CHIPS, D_MODEL // N_CHIPS, D_MODEL), dwio,
                                   dwoo.reshape(N_CHIPS, D_MODEL // N_CHIPS, D_MODEL)])
    dqa, dka, dva, p_woe, p_wio, p_woo = _pp_bwd(qa, ka, ka_t, va, oa, doa, lse_a, kdiv=KV_SHARE, tq=a_tq, tk=a_tk, sub=a_sub,
                                                 name="attn_a_bwd", side=late_grads)
    dqcat, dkcat = _mla_bwd(qcat, kcat, kcat_t, olat, dolat, lse_b, tq=b_tq, tk=bb_tk, sub=bb_sub)
    dx0, dwie, dwuq, dwuk, st_0, nst = _even_pre_bwd(x0, h0, proj_e, dqa, dka, dva, dga, dgb, dqcat, dkcat, dx1, mod3[0], nw0,
                                                     wie, qn, kn, seg, ca, sa, ct, st, qln, kvln, wuq, wuk)
    dsink_pairs = jnp.stack([dsink_raw[:, 0, 0], dsink_raw[:, 1, 0]], axis=1).reshape(C_HEADS)
    return dict(
        loss_row=st_f[2:3], dx=dx0,
        dmod=jnp.stack([jnp.concatenate([st_0[0], st_0[1], st_e[0]]), jnp.concatenate([st_1[0], st_1[1], st_f[1]])]),
        norm_w=jnp.stack([st_0[2], st_1[2]]), final_norm=st_f[0],
        a_q_norm=nst[0:1, 0:HEAD_DIM], a_k_norm=nst[1:2, 0:HEAD_DIM], b_q_lora_norm=nst[2:3, :], b_kv_lora_norm=nst[3:4, 0:LANES],
        c_sink=dsink_pairs.reshape(1, C_HEADS),
        even_w_in=dwie, b_w_uq=dwuq, b_w_uk=dwuk, b_w_uv=dwuv, even_w_out=p_woe, odd_w_in=p_wio, odd_w_out=p_woo)


WEIGHT_NAMES = ("norm_w", "ada_w", "ada_b", "even_w_in", "a_q_norm", "a_k_norm", "b_q_lora_norm", "b_kv_lora_norm", "b_w_uq",
                "b_w_uk", "b_w_uv", "even_w_out", "odd_w_in", "c_sink", "odd_w_out", "final_norm")


def _chips_to_cols(g):
    p, r, n = g.shape
    return jnp.transpose(g, (1, 0, 2)).reshape(r, p * n)


def kernel(x, c, norm_w, ada_w, ada_b, even_w_in, a_q_norm, a_k_norm, b_q_lora_norm, b_kv_lora_norm, b_w_uq, b_w_uk, b_w_uv, even_w_out, odd_w_in, c_sink, odd_w_out, final_norm, loss_target, m_norm_w, m_ada_w, m_ada_b, m_even_w_in, m_a_q_norm, m_a_k_norm, m_b_q_lora_norm, m_b_kv_lora_norm, m_b_w_uq, m_b_w_uk, m_b_w_uv, m_even_w_out, m_odd_w_in, m_c_sink, m_odd_w_out, m_final_norm, v_norm_w, v_ada_w, v_ada_b, v_even_w_in, v_a_q_norm, v_a_k_norm, v_b_q_lora_norm, v_b_kv_lora_norm, v_b_w_uq, v_b_w_uk, v_b_w_uv, v_even_w_out, v_odd_w_in, v_c_sink, v_odd_w_out, v_final_norm):
    given = dict(locals())
    xi, yi, ci = _my_place()
    chip = 2 * xi + yi
    dev = 2 * chip + ci
    n_ada = ada_w.shape[2]

    (c_all,) = _gather_dev8([c], "gather_c")
    c_all = c_all.reshape(N_DEV, D_MODEL)
    bias = lax.dynamic_slice_in_dim(ada_b, chip * n_ada, n_ada, axis=1).reshape(2, 1, n_ada)
    mod_cols = _ada_fwd(c_all, ada_w, bias)
    def halves(w):
        return w.astype(BF16).reshape((2, w.shape[0] // 2) + w.shape[1:])

    mod_all, wie_g, wuq_g = _gather_chip4_halves([mod_cols, halves(even_w_in[0]), halves(b_w_uq[0])]).run("gather_weights")
    wie_g = wie_g.reshape(N_CHIPS, D_MODEL, EVEN_IN // N_CHIPS)
    wuq_g = wuq_g.reshape(N_CHIPS, B_Q_LORA, -1)
    mod = jnp.transpose(lax.dynamic_index_in_dim(mod_all, dev, axis=2, keepdims=False), (1, 0, 2)).reshape(2, 3 * D_MODEL)

    res = _local_step(
        x[0], loss_target[0], mod, norm_w,
        _even_in_layout(_chips_to_cols(wie_g)), _uq_layout(_chips_to_cols(wuq_g)), _uk_layout(b_w_uk[0].astype(BF16)),
        _uv_layout(b_w_uv[0].astype(BF16)), [halves(even_w_out[0]), halves(odd_w_in[0]), halves(odd_w_out[0])],
        a_q_norm, a_k_norm, b_q_lora_norm, b_kv_lora_norm, c_sink, final_norm)

    latent = jnp.stack([_uk_unlayout(res["b_w_uk"]).reshape(B_KV_LORA, 512),
                        _uv_unlayout(res["b_w_uv"]).reshape(B_KV_LORA, 512)]).astype(BF16)
    p_wie, p_wuq, small_all, latent_all = _reduce_exchange(
        [_even_in_unlayout(res["even_w_in"]).T.reshape(N_CHIPS, EVEN_IN // N_CHIPS, D_MODEL),
         _uq_unlayout(res["b_w_uq"]).T.astype(BF16).reshape(N_CHIPS, -1, B_Q_LORA)],
        whole=[_pack_small(res), latent]).run("reduce_exchange")
    shard_parts = dict(even_w_in=p_wie, b_w_uq=p_wuq, **{k: res[k] for k in ("even_w_out", "odd_w_in", "odd_w_out")})
    dmod_all = small_all[:, 0:6, :].reshape(N_DEV, 2, 3 * D_MODEL)
    dmod_cols = jnp.transpose(lax.dynamic_slice_in_dim(dmod_all, chip * n_ada, n_ada, axis=2), (1, 0, 2))
    parts = dict(shard_parts)
    parts["ada_w"] = _ada_bwd(c_all.T, dmod_cols).reshape(1, 2 * D_MODEL, n_ada)
    parts["b_w_uk"], parts["b_w_uv"] = [latent_all[:, t].reshape(N_DEV, B_KV_LORA * B_HEADS, -1) for t in range(2)]

    def as2d(a):
        return a.reshape((-1, a.shape[-1]) if a.ndim > 1 else (1, a.shape[0]))

    results = {}
    small_outs = _adam_small(small_all, *[[as2d(given[pre + k]) for k in SMALL_WEIGHTS] for pre in ("", "m_", "v_")])
    for idx, k in enumerate(SMALL_WEIGHTS):
        results[k] = small_outs[4 * idx:4 * idx + 4]
    for k, p in parts.items():
        if k in ("even_w_in", "b_w_uq"):
            outs = _adam(p, given[k][0].T, given["m_" + k][0].T, given["v_" + k][0].T, "adam_" + k, by_columns=k == "even_w_in")
            results[k] = [o.T for o in outs]
            continue
        shape2 = (p.shape[-2], p.shape[-1])
        results[k] = _adam(p, given[k].reshape(shape2), given["m_" + k].reshape(shape2), given["v_" + k].reshape(shape2),
                           "adam_" + k)
    by_kind = [[results[k][t].reshape(given[k].shape) for k in WEIGHT_NAMES] for t in range(4)]
    return (small_outs[-1][0, 0], res["dx"][None], *by_kind[0], *by_kind[1], *by_kind[2], *by_kind[3])
```

```python
import functools

import numpy as np
import jax
import jax.numpy as jnp
from jax import lax
from jax.experimental import pallas as pl
from jax.experimental.pallas import tpu as pltpu

F32 = jnp.float32
BF16 = jnp.bfloat16
HIGHEST = lax.Precision.HIGHEST
MESH_ID = pl.DeviceIdType.MESH

D_MODEL = 1024
HEAD_DIM = 64
GRID_W = 64
EPS = 1e-6
ROPE_THETA = 10000.0
B_HEADS, B_NOPE, B_ROPE, B_V = 8, 64, 32, 64
B_Q_LORA, B_KV_LORA = 256, 128
C_HEADS = 16
WINDOW = 128
EVEN_IN, ODD_IN = 2208, 2560
EVEN_P = 2304
N_CHIPS, N_DEV = 4, 8
LANES = 128
NEG = -1e30
VMEM_LIMIT = 60 * 1024 * 1024

ADAM_LR, ADAM_B1, ADAM_B2, ADAM_EPS, ADAM_WD, ADAM_STEP = 0.001, 0.9, 0.999, 1e-08, 0.01, 10

ROW_TILE = 512
IN_PROJ_ROW_TILE = 256


def _dot(a, b):
    return lax.dot_general(a, b, (((1,), (0,)), ((), ())), preferred_element_type=F32)


def _dot_nt(a, b):
    return lax.dot_general(a, b, (((1,), (1,)), ((), ())), preferred_element_type=F32)


def _dot_tn(a, b):
    return lax.dot_general(a, b, (((0,), (0,)), ((), ())), preferred_element_type=F32)


def _dot_f32(a, b):
    return lax.dot_general(a, b, (((1,), (0,)), ((), ())), precision=HIGHEST, preferred_element_type=F32)


def _sigmoid(x):
    return 1.0 / (1.0 + jnp.exp(-x))


def _silu_and_grad(g):
    s = _sigmoid(g)
    return g * s, s * (1.0 + g * (1.0 - s))


def _lane_iota():
    return lax.broadcasted_iota(jnp.int32, (1, LANES), 1)


def _partner(x, lane):
    return jnp.where((lane % 32) < 16, pltpu.roll(x, LANES - 16, 1), pltpu.roll(x, 16, 1))


def _rot(x, cos, sin_signed, lane):
    return x * cos + _partner(x, lane) * sin_signed


def _rot_bwd(dy, cos, sin_signed, lane):
    return dy * cos + _partner(dy * sin_signed, lane)


def _rms(x):
    return lax.rsqrt(jnp.mean(x * x, axis=-1, keepdims=True) + EPS)


def _rms_bwd(x, r, g):
    return r * g - x * (r * r * r) * jnp.mean(x * g, axis=-1, keepdims=True)


def _seg_mean(v, seg_ones):
    hi = v.astype(BF16)
    lo = (v - hi.astype(F32)).astype(BF16)
    return (_dot(hi, seg_ones) + _dot(lo, seg_ones)) * (1.0 / HEAD_DIM)


def _dup_heads(x, lane):
    swapped = pltpu.roll(x, HEAD_DIM, 1)
    lo = lane < HEAD_DIM
    return jnp.concatenate([jnp.where(lo, x, swapped), jnp.where(lo, swapped, x)], axis=1)


def _fold_heads(x2, lane):
    a, b = x2[:, 0:LANES], x2[:, LANES:2 * LANES]
    return jnp.where(lane < HEAD_DIM, a + pltpu.roll(a, HEAD_DIM, 1), b + pltpu.roll(b, HEAD_DIM, 1))


def _row_spec(ts, cols):
    return pl.BlockSpec((ts, cols), lambda i: (i, 0))


def _full_spec(shape, single=True):
    nd = len(shape)
    if single:
        return pl.BlockSpec(shape, lambda i: (0,) * nd, pipeline_mode=pl.Buffered(1))
    return pl.BlockSpec(shape, lambda i: (0,) * nd)


def _sds(shape, dtype):
    return jax.ShapeDtypeStruct(shape, dtype)


def _params(sem):
    return pltpu.CompilerParams(dimension_semantics=sem, vmem_limit_bytes=VMEM_LIMIT)


def _even_pre_fwd(x, mod, nw, wie, qn, kn, seg, ca, sa, ct, st, qln, kvln, wuq, wuk):
    S = x.shape[0]
    ts = min(IN_PROJ_ROW_TILE, S)

    def body(x_ref, mod_ref, nw_ref, wie_ref, qn_ref, kn_ref, seg_ref, ca_ref, sa_ref, ct_ref, st_ref, qln_ref,
             kvln_ref, wuq_ref, wuk_ref, h_ref, proj_ref, qa_ref, ka_ref, va_ref, qcat_ref, kcat_ref, kat_ref, vat_ref, kcatt_ref):
        xv = x_ref[...]
        h = (xv * _rms(xv) * nw_ref[...]) * (1.0 + mod_ref[1:2, :]) + mod_ref[0:1, :]
        hb = h.astype(BF16)
        h_ref[...] = hb
        proj = _dot_nt(hb, wie_ref[...])
        proj_ref[...] = proj
        lane = _lane_iota()
        ca_v, sa_v, ct_v, st_v = ca_ref[...], sa_ref[...], ct_ref[...], st_ref[...]
        seg_v = seg_ref[...]
        for cb in range(4):
            xc = proj[:, LANES * cb:LANES * (cb + 1)]
            r = lax.rsqrt(_seg_mean(xc * xc, seg_v) + EPS)
            y = _rot(xc * r * qn_ref[...], ca_v, sa_v, lane)
            qa_ref[:, LANES * cb:LANES * (cb + 1)] = (y * 0.125).astype(BF16)
        kc = proj[:, 512:640]
        r = lax.rsqrt(_seg_mean(kc * kc, seg_v) + EPS)
        ka_v = _dup_heads(_rot(kc * r * kn_ref[...], ca_v, sa_v, lane), lane)
        ka_ref[...] = ka_v.astype(BF16)
        kat_ref[...] = ka_v.T.astype(BF16)
        va_v = _dup_heads(proj[:, 640:768], lane)
        va_ref[...] = va_v.astype(BF16)
        vat_ref[...] = va_v.T.astype(BF16)
        cq = proj[:, 1280:1536]
        cqn = (cq * _rms(cq) * qln_ref[...]).astype(BF16)
        ckv = proj[:, 1536:1664]
        ckvn = ckv * _rms(ckv) * kvln_ref[...]
        qb = _dot_nt(cqn, wuq_ref[...])
        qlat = _dot(qb[:, 0:512].astype(BF16), wuk_ref[...])
        for hh in range(B_HEADS):
            qcat_ref[hh, :, 0:LANES] = qlat[:, LANES * hh:LANES * (hh + 1)].astype(BF16)
            qr = _rot(qb[:, 512 + LANES * hh:512 + LANES * (hh + 1)], ct_v, st_v, lane)
            qcat_ref[hh, :, LANES:2 * LANES] = qr.astype(BF16)
        kr = _rot(proj[:, 1664:1792], ct_v, st_v, lane)
        kcat_ref[:, 0:LANES] = ckvn.astype(BF16)
        kcat_ref[:, LANES:2 * LANES] = kr.astype(BF16)
        kcatt_ref[0:LANES, :] = ckvn.T.astype(BF16)
        kcatt_ref[LANES:2 * LANES, :] = kr.T.astype(BF16)

    col_spec = lambda rows: pl.BlockSpec((rows, ts), lambda i: (0, i))
    return pl.pallas_call(
        body, name="even_pre_fwd", grid=(S // ts,),
        in_specs=[_row_spec(ts, D_MODEL), _full_spec((3, D_MODEL)), _full_spec((1, D_MODEL)), _full_spec((EVEN_P, D_MODEL)),
                  _full_spec((1, LANES)), _full_spec((1, LANES)), _full_spec((LANES, LANES)),
                  _row_spec(ts, LANES), _row_spec(ts, LANES), _row_spec(ts, LANES), _row_spec(ts, LANES),
                  _full_spec((1, B_Q_LORA)), _full_spec((1, B_KV_LORA)), _full_spec((1536, B_Q_LORA)), _full_spec((512, 1024))],
        out_specs=[_row_spec(ts, D_MODEL), _row_spec(ts, EVEN_P), _row_spec(ts, 512), _row_spec(ts, 2 * LANES), _row_spec(ts, 2 * LANES),
                   pl.BlockSpec((B_HEADS, ts, 2 * LANES), lambda i: (0, i, 0)), _row_spec(ts, 2 * LANES),
                   col_spec(2 * LANES), col_spec(2 * LANES), col_spec(2 * LANES)],
        out_shape=[_sds((S, D_MODEL), BF16), _sds((S, EVEN_P), F32), _sds((S, 512), BF16), _sds((S, 2 * LANES), BF16),
                   _sds((S, 2 * LANES), BF16), _sds((B_HEADS, S, 2 * LANES), BF16), _sds((S, 2 * LANES), BF16),
                   _sds((2 * LANES, S), BF16), _sds((2 * LANES, S), BF16), _sds((2 * LANES, S), BF16)],
        compiler_params=_params(("arbitrary",)),
    )(x, mod, nw, wie, qn, kn, seg, ca, sa, ct, st, qln, kvln, wuq, wuk)


MLA_SCALE = (B_NOPE + B_ROPE) ** -0.5
LOG2E = 1.4426950408889634

def _row_lo():
    return lax.broadcasted_iota(jnp.int32, (LANES, 1), 0) < HEAD_DIM


def _stack_cols(vT, rlo):
    zero = jnp.zeros_like(vT)
    return jnp.concatenate([jnp.where(rlo, vT, zero), jnp.where(rlo, zero, vT)], axis=1)


def _stack_rows(v, lo):
    zero = jnp.zeros_like(v)
    return jnp.concatenate([jnp.where(lo, v, zero), jnp.where(lo, zero, v)], axis=0)


def _pick_halves_T(xT, rlo, t):
    return jnp.where(rlo, xT[:, 0:t], xT[:, t:2 * t]).T


def _side_split(refs, n_in, n_out, n_scratch, side):
    ns = side.n if side is not None else 0
    cuts = np.cumsum([0, n_in, ns, n_out, ns, n_scratch])
    return [refs[a:b] for a, b in zip(cuts[:-1], cuts[1:])] + [refs[cuts[-1]:]]


def _side_hooks(side, side_ins, side_outs, side_sems, step, total):
    if side is None:
        return lambda: None
    start, mid, end = side.phases(side_ins, side_outs, side_sems)
    pl.when(step == 0)(start)
    pl.when(step == total // 2)(mid)
    return lambda: pl.when(step == total - 1)(end)


def _side_specs(side):
    if side is None:
        return [], [], [], [], []
    return list(side.arrs), [_ANY] * side.n, [_ANY] * side.n, list(side.out_shapes), side.sem_shapes()


def _pp_fwd(q, k, vT, *, kdiv, tq, tk, sub, name, side=None):
    S = k.shape[0]; nb = q.shape[1] // LANES; nq = S // tq; nkv = S // tk; nsub = tk // sub

    def body(*refs):
        (q_ref, k_ref, vT_ref), side_ins, (o_ref, lse_ref), side_outs, (qs, m_s, l_s, acc), side_sems = _side_split(refs, 3, 2, 4, side)
        j = pl.program_id(2)
        rlo = _row_lo()
        step = (pl.program_id(0) * nq + pl.program_id(1)) * nkv + j
        side_end = _side_hooks(side, side_ins, side_outs, side_sems, step, nb * nq * nkv)

        @pl.when(j == 0)
        def _():
            qs[...] = _stack_cols(q_ref[...].astype(F32).T, rlo).astype(BF16)
            m_s[...] = jnp.full((1, 2 * tq), NEG, F32)
            l_s[...] = jnp.zeros((1, 2 * tq), F32)
            acc[...] = jnp.zeros((LANES, 2 * tq), F32)

        qsv = qs[...]
        m, l, a = m_s[...], l_s[...], acc[...]
        s_cur = _dot(k_ref[0:sub, :], qsv)
        for t in range(nsub):
            if t + 1 < nsub:
                s_next = _dot(k_ref[sub * (t + 1):sub * (t + 2), :], qsv)
            m_new = jnp.maximum(m, jnp.max(s_cur, axis=0, keepdims=True))
            alpha = jnp.exp(m - m_new)
            p = jnp.exp(s_cur - m_new)
            l = alpha * l + jnp.sum(p, axis=0, keepdims=True)
            a = alpha * a + _dot(vT_ref[:, sub * t:sub * (t + 1)], p.astype(BF16))
            m = m_new
            if t + 1 < nsub:
                s_cur = s_next
        m_s[...], l_s[...], acc[...] = m, l, a

        @pl.when(j == nkv - 1)
        def _():
            l_f = l_s[...]
            o_ref[...] = _pick_halves_T(acc[...] / l_f, rlo, tq).astype(BF16)
            lse_ref[0, 0] = m_s[...] + jnp.log(l_f)

        side_end()

    s_args, s_in, s_out, s_shapes, s_sems = _side_specs(side)
    return pl.pallas_call(
        body, name=name, grid=(nb, nq, nkv),
        in_specs=[pl.BlockSpec((tq, LANES), lambda b, i, j: (i, b)), pl.BlockSpec((tk, LANES), lambda b, i, j: (j, b // kdiv)),
                  pl.BlockSpec((LANES, tk), lambda b, i, j: (b // kdiv, j))] + s_in,
        out_specs=[pl.BlockSpec((tq, LANES), lambda b, i, j: (i, b)),
                   pl.BlockSpec((1, 1, 1, 2 * tq), lambda b, i, j: (b, i, 0, 0))] + s_out,
        out_shape=[_sds((S, nb * LANES), BF16), _sds((nb, nq, 1, 2 * tq), F32)] + s_shapes,
        scratch_shapes=[pltpu.VMEM((LANES, 2 * tq), BF16), pltpu.VMEM((1, 2 * tq), F32), pltpu.VMEM((1, 2 * tq), F32),
                        pltpu.VMEM((LANES, 2 * tq), F32)] + s_sems,
        compiler_params=_params(("arbitrary",) * 3))(q, k, vT, *s_args)


def _pp_bwd(q, k, kT, v, o, do, lse, *, kdiv, tq, tk, sub, name, side=None):
    S = k.shape[0]; nb = q.shape[1] // LANES; nkb = k.shape[1] // LANES; nq = S // tq; nkv = S // tk; nsub = tk // sub

    def body(*refs):
        ((q_ref, k_ref, kT_ref, v_ref, o_ref, do_ref, lse_ref), side_ins, (dq_ref, dk_ref, dv_ref), side_outs,
         (qsT, qs, dosT, dos, delta_s, dq_acc), side_sems) = _side_split(refs, 7, 3, 6, side)
        b, i, j = pl.program_id(0), pl.program_id(1), pl.program_id(2)
        rlo = _row_lo()
        lo = lax.broadcasted_iota(jnp.int32, (1, LANES), 1) < HEAD_DIM
        side_end = _side_hooks(side, side_ins, side_outs, side_sems, (b * nq + i) * nkv + j, nb * nq * nkv)

        @pl.when((b % kdiv == 0) & (i == 0) & (j == 0))
        def _():
            dk_ref[...] = jnp.zeros((S, LANES), F32)
            dv_ref[...] = jnp.zeros((S, LANES), F32)

        @pl.when(j == 0)
        def _():
            qv = q_ref[...]
            qs[...] = _stack_rows(qv, lo)
            qsT[...] = _stack_cols(qv.astype(F32).T, rlo).astype(BF16)
            dov = do_ref[...].astype(F32)
            dos[...] = _stack_rows(dov.astype(BF16), lo)
            dosT[...] = _stack_cols(dov.T, rlo).astype(BF16)
            prodT = (dov * o_ref[...].astype(F32)).T
            delta_s[...] = jnp.concatenate([jnp.sum(jnp.where(rlo, prodT, 0.0), axis=0, keepdims=True),
                                            jnp.sum(jnp.where(rlo, 0.0, prodT), axis=0, keepdims=True)], axis=1)
            dq_acc[...] = jnp.zeros((LANES, 2 * tq), F32)

        qsTv, dosTv, qsv, dosv = qsT[...], dosT[...], qs[...], dos[...]
        lse_v, delta_v = lse_ref[0, 0], delta_s[...]
        dqa = dq_acc[...]
        s_cur = _dot(k_ref[0:sub, :], qsTv)
        dp_cur = _dot(v_ref[0:sub, :], dosTv)
        for t in range(nsub):
            if t + 1 < nsub:
                s_next = _dot(k_ref[sub * (t + 1):sub * (t + 2), :], qsTv)
                dp_next = _dot(v_ref[sub * (t + 1):sub * (t + 2), :], dosTv)
            p = jnp.exp(s_cur - lse_v)
            ds = (p * (dp_cur - delta_v)).astype(BF16)
            rows = pl.ds(pl.multiple_of(j * tk + sub * t, sub), sub)
            dv_ref[rows, :] += _dot(p.astype(BF16), dosv)
            dk_ref[rows, :] += _dot(ds, qsv)
            dqa = dqa + _dot(kT_ref[:, sub * t:sub * (t + 1)], ds)
            if t + 1 < nsub:
                s_cur, dp_cur = s_next, dp_next
        dq_acc[...] = dqa

        @pl.when(j == nkv - 1)
        def _():
            dq_ref[...] = _pick_halves_T(dq_acc[...], rlo, tq)

        side_end()

    qmap = lambda b, i, j: (i, b)
    kmap = lambda b, i, j: (j, b // kdiv)
    res = lambda b, i, j: (0, b // kdiv)
    s_args, s_in, s_out, s_shapes, s_sems = _side_specs(side)
    return pl.pallas_call(
        body, name=name, grid=(nb, nq, nkv),
        in_specs=[pl.BlockSpec((tq, LANES), qmap), pl.BlockSpec((tk, LANES), kmap), pl.BlockSpec((LANES, tk), lambda b, i, j: (b // kdiv, j)),
                  pl.BlockSpec((tk, LANES), kmap), pl.BlockSpec((tq, LANES), qmap), pl.BlockSpec((tq, LANES), qmap),
                  pl.BlockSpec((1, 1, 1, 2 * tq), lambda b, i, j: (b, i, 0, 0))] + s_in,
        out_specs=[pl.BlockSpec((tq, LANES), qmap), pl.BlockSpec((S, LANES), res), pl.BlockSpec((S, LANES), res)] + s_out,
        out_shape=[_sds((S, nb * LANES), F32), _sds((S, nkb * LANES), F32), _sds((S, nkb * LANES), F32)] + s_shapes,
        scratch_shapes=[pltpu.VMEM((LANES, 2 * tq), BF16), pltpu.VMEM((2 * tq, LANES), BF16), pltpu.VMEM((LANES, 2 * tq), BF16),
                        pltpu.VMEM((2 * tq, LANES), BF16), pltpu.VMEM((1, 2 * tq), F32), pltpu.VMEM((LANES, 2 * tq), F32)] + s_sems,
        compiler_params=_params(("arbitrary",) * 3))(q, k, kT, v, o, do, lse, *s_args)


MLA_C = MLA_SCALE * LOG2E


def _mla_fwd(q, kcat, kcatT, *, tq, tk, sub):
    S = kcat.shape[0]; nq, nkv = S // tq, S // tk; R = B_HEADS * tq; nsub = tk // sub

    def body(q_ref, k_ref, vT_ref, o_ref, lse_ref, qT, m_s, l_s, acc):
        j = pl.program_id(1)

        @pl.when(j == 0)
        def _():
            qT[...] = q_ref[...].reshape(R, 2 * LANES).astype(F32).T.astype(BF16)
            m_s[...] = jnp.full((1, R), NEG, F32)
            l_s[...] = jnp.zeros((1, R), F32)
            acc[...] = jnp.zeros((LANES, R), F32)

        qTv = qT[...]
        m, l, a = m_s[...], l_s[...], acc[...]
        s_cur = _dot(k_ref[0:sub, :], qTv)
        for t in range(nsub):
            if t + 1 < nsub:
                s_next = _dot(k_ref[sub * (t + 1):sub * (t + 2), :], qTv)
            m_new = jnp.maximum(m, jnp.max(s_cur, axis=0, keepdims=True))
            alpha = jnp.exp2((m - m_new) * MLA_C)
            p = jnp.exp2((s_cur - m_new) * MLA_C)
            l = alpha * l + jnp.sum(p, axis=0, keepdims=True)
            a = alpha * a + _dot(vT_ref[:, sub * t:sub * (t + 1)], p.astype(BF16))
            m = m_new
            if t + 1 < nsub:
                s_cur = s_next
        m_s[...], l_s[...], acc[...] = m, l, a

        @pl.when(j == nkv - 1)
        def _():
            l_f = l_s[...]
            o_ref[...] = (acc[...] / l_f).T.reshape(B_HEADS, tq, LANES).astype(BF16)
            lse_ref[0] = m_s[...] * MLA_SCALE + jnp.log(l_f)

    return pl.pallas_call(
        body, name="mla_fwd", grid=(nq, nkv),
        in_specs=[pl.BlockSpec((B_HEADS, tq, 2 * LANES), lambda i, j: (0, i, 0)), pl.BlockSpec((tk, 2 * LANES), lambda i, j: (j, 0)),
                  pl.BlockSpec((LANES, tk), lambda i, j: (0, j))],
        out_specs=[pl.BlockSpec((B_HEADS, tq, LANES), lambda i, j: (0, i, 0)), pl.BlockSpec((1, 1, R), lambda i, j: (i, 0, 0))],
        out_shape=[_sds((B_HEADS, S, LANES), BF16), _sds((nq, 1, R), F32)],
        scratch_shapes=[pltpu.VMEM((2 * LANES, R), BF16), pltpu.VMEM((1, R), F32), pltpu.VMEM((1, R), F32), pltpu.VMEM((LANES, R), F32)],
        compiler_params=_params(("arbitrary", "arbitrary")))(q, kcat, kcatT)


def _mla_bwd(q, kcat, kcatT, o, do, lse, *, tq, tk, sub):
    S = kcat.shape[0]; nq, nkv = S // tq, S // tk; R = B_HEADS * tq; nsub = tk // sub

    def body(q_ref, k_ref, kT_ref, o_ref, do_ref, lse_ref, dq_ref, dk_ref, qT, dosT, dos, delta_s, dq_acc):
        i, j = pl.program_id(0), pl.program_id(1)

        @pl.when((i == 0) & (j == 0))
        def _():
            dk_ref[...] = jnp.zeros((S, 2 * LANES), F32)

        @pl.when(j == 0)
        def _():
            qT[...] = q_ref[...].reshape(R, 2 * LANES).astype(F32).T.astype(BF16)
            dov = do_ref[...].reshape(R, LANES).astype(F32)
            dos[...] = dov.astype(BF16)
            dosT[...] = dov.T.astype(BF16)
            delta_s[...] = jnp.sum((dov * o_ref[...].reshape(R, LANES).astype(F32)).T, axis=0, keepdims=True)
            dq_acc[...] = jnp.zeros((2 * LANES, R), F32)

        qTv, dosTv, dosv = qT[...], dosT[...], dos[...]
        qv = q_ref[...].reshape(R, 2 * LANES)
        lse_v, delta_v = lse_ref[0] * LOG2E, delta_s[...]
        dqa = dq_acc[...]
        s_cur = _dot(k_ref[0:sub, :], qTv)
        dp_cur = _dot(k_ref[0:sub, 0:LANES], dosTv)
        for t in range(nsub):
            if t + 1 < nsub:
                s_next = _dot(k_ref[sub * (t + 1):sub * (t + 2), :], qTv)
                dp_next = _dot(k_ref[sub * (t + 1):sub * (t + 2), 0:LANES], dosTv)
            p = jnp.exp2(s_cur * MLA_C - lse_v)
            ds = (p * (dp_cur - delta_v) * MLA_SCALE).astype(BF16)
            rows = pl.ds(pl.multiple_of(j * tk + sub * t, sub), sub)
            dk_ref[rows, :] += _dot(ds, qv)
            dk_ref[rows, 0:LANES] += _dot(p.astype(BF16), dosv)
            dqa = dqa + _dot(kT_ref[:, sub * t:sub * (t + 1)], ds)
            if t + 1 < nsub:
                s_cur, dp_cur = s_next, dp_next
        dq_acc[...] = dqa

        @pl.when(j == nkv - 1)
        def _():
            dq_ref[...] = dq_acc[...].T.reshape(B_HEADS, tq, 2 * LANES)

    hspec = lambda w: pl.BlockSpec((B_HEADS, tq, w), lambda i, j: (0, i, 0))
    return pl.pallas_call(
        body, name="mla_bwd", grid=(nq, nkv),
        in_specs=[hspec(2 * LANES), pl.BlockSpec((tk, 2 * LANES), lambda i, j: (j, 0)), pl.BlockSpec((2 * LANES, tk), lambda i, j: (0, j)),
                  hspec(LANES), hspec(LANES), pl.BlockSpec((1, 1, R), lambda i, j: (i, 0, 0))],
        out_specs=[hspec(2 * LANES), pl.BlockSpec((S, 2 * LANES), lambda i, j: (0, 0))],
        out_shape=[_sds((B_HEADS, S, 2 * LANES), F32), _sds((S, 2 * LANES), F32)],
        scratch_shapes=[pltpu.VMEM((2 * LANES, R), BF16), pltpu.VMEM((LANES, R), BF16), pltpu.VMEM((R, LANES), BF16),
                        pltpu.VMEM((1, R), F32), pltpu.VMEM((2 * LANES, R), F32)],
        compiler_params=_params(("arbitrary", "arbitrary")))(q, kcat, kcatT, o, do, lse)


def _win_start(i, tq, nk, S):
    return pl.multiple_of(jnp.clip(i * tq - WINDOW, 0, S - nk), LANES)


def _win_dist_table(S, tq):
    nk = min(tq + 2 * WINDOW, S)
    nq = S // tq
    r = np.arange(nk)[:, None]
    c = (np.arange(2 * tq) % tq)[None, :]
    tabs = []
    for rel in (0, WINDOW, (nq - 1) * tq - (S - nk)):
        dist = np.abs(rel + c - r).astype(np.float32)
        tabs.append(np.where(dist <= WINDOW, dist, np.float32(1e32)))
    return jnp.asarray(np.stack(tabs))


def _win_dist_spec(nk, tq, nq):
    return pl.BlockSpec((1, nk, 2 * tq), lambda b, i: (jnp.where(i == 0, 0, jnp.where(i == nq - 1, 2, 1)), 0, 0))


def _win_fwd(q, k, vT, dist, slope, sink, *, kdiv, tq, nbs, name):
    S = k.shape[0]; nb = q.shape[1] // LANES; nq = S // tq; nk = min(tq + 2 * WINDOW, S)
    assert nb % nbs == 0 and nbs % kdiv == 0
    kvw = (nbs // kdiv) * LANES

    def body(q_ref, k_ref, vT_ref, dist_ref, slope_ref, sink_ref, o_ref, lse_ref):
        i = pl.program_id(1)
        rlo = _row_lo()
        k0 = _win_start(i, tq, nk, S)
        kk, vv, dd = k_ref[pl.ds(k0, nk), :], vT_ref[:, pl.ds(k0, nk)], dist_ref[0]
        for u in range(nbs):
            kv = slice(LANES * (u // kdiv), LANES * (u // kdiv + 1))
            qsT = _stack_cols(q_ref[:, LANES * u:LANES * (u + 1)].astype(F32).T, rlo).astype(BF16)
            s = _dot(kk[:, kv], qsT) - slope_ref[u] * dd
            sk = sink_ref[u]
            m = jnp.maximum(jnp.max(s, axis=0, keepdims=True), sk)
            p = jnp.exp(s - m)
            l = jnp.sum(p, axis=0, keepdims=True) + jnp.exp(sk - m)
            o_ref[:, LANES * u:LANES * (u + 1)] = _pick_halves_T(_dot(vv[kv, :], p.astype(BF16)) / l, rlo, tq).astype(BF16)
            lse_ref[u, 0] = m + jnp.log(l)

    row_spec = pl.BlockSpec((nbs, 1, 2 * tq), lambda b, i: (b, 0, 0))
    return pl.pallas_call(
        body, name=name, grid=(nb // nbs, nq),
        in_specs=[pl.BlockSpec((tq, nbs * LANES), lambda b, i: (i, b)), pl.BlockSpec((S, kvw), lambda b, i: (0, b)),
                  pl.BlockSpec((kvw, S), lambda b, i: (b, 0)), _win_dist_spec(nk, tq, nq), row_spec, row_spec],
        out_specs=[pl.BlockSpec((tq, nbs * LANES), lambda b, i: (i, b)), pl.BlockSpec((nbs, 1, 1, 2 * tq), lambda b, i: (b, i, 0, 0))],
        out_shape=[_sds((S, nb * LANES), BF16), _sds((nb, nq, 1, 2 * tq), F32)],
        compiler_params=_params(("arbitrary", "arbitrary")))(q, k, vT, dist, slope, sink)


def _win_bwd(q, k, kT, v, o, do, lse, dist, slope, sink, *, kdiv, tq, nbs, name):
    S = k.shape[0]; nb = q.shape[1] // LANES; nkb = k.shape[1] // LANES; nq = S // tq; nk = min(tq + 2 * WINDOW, S)
    assert nb % nbs == 0 and nbs % kdiv == 0
    nkv = nbs // kdiv
    kvw = nkv * LANES

    def body(q_ref, k_ref, kT_ref, v_ref, o_ref, do_ref, lse_ref, dist_ref, slope_ref, sink_ref, dq_ref, dk_ref, dv_ref, dsink_ref, ds_acc):
        i = pl.program_id(1)
        rlo = _row_lo()
        lo = lax.broadcasted_iota(jnp.int32, (1, LANES), 1) < HEAD_DIM

        @pl.when(i == 0)
        def _():
            dk_ref[...] = jnp.zeros((S, kvw), F32)
            dv_ref[...] = jnp.zeros((S, kvw), F32)
            ds_acc[...] = jnp.zeros((nbs, 2 * tq), F32)

        k0 = _win_start(i, tq, nk, S)
        rows = pl.ds(k0, nk)
        kk_all, vv_all, kkT_all, dd = k_ref[rows, :], v_ref[rows, :], kT_ref[:, rows], dist_ref[0]
        dv_sum, dk_sum = [None] * nkv, [None] * nkv
        for u in range(nbs):
            g = u // kdiv
            kv = slice(LANES * g, LANES * (g + 1))
            kk, vv, kkT = kk_all[:, kv], vv_all[:, kv], kkT_all[kv, :]
            cols = slice(LANES * u, LANES * (u + 1))
            qv = q_ref[:, cols]
            qs = _stack_rows(qv, lo)
            qsT = _stack_cols(qv.astype(F32).T, rlo).astype(BF16)
            dov = do_ref[:, cols].astype(F32)
            dos = _stack_rows(dov.astype(BF16), lo)
            dosT = _stack_cols(dov.T, rlo).astype(BF16)
            prodT = (dov * o_ref[:, cols].astype(F32)).T
            delta = jnp.concatenate([jnp.sum(jnp.where(rlo, prodT, 0.0), axis=0, keepdims=True),
                                     jnp.sum(jnp.where(rlo, 0.0, prodT), axis=0, keepdims=True)], axis=1)
            lse_v = lse_ref[u, 0]
            ds_acc[u:u + 1, :] += -jnp.exp(sink_ref[u] - lse_v) * delta
            p = jnp.exp(_dot(kk, qsT) - slope_ref[u] * dd - lse_v)
            ds = (p * (_dot(vv, dosT) - delta)).astype(BF16)
            dv_u, dk_u = _dot(p.astype(BF16), dos), _dot(ds, qs)
            dv_sum[g] = dv_u if dv_sum[g] is None else dv_sum[g] + dv_u
            dk_sum[g] = dk_u if dk_sum[g] is None else dk_sum[g] + dk_u
            dq_ref[:, cols] = (_pick_halves_T(_dot(kkT, ds), rlo, tq) * 0.125).astype(BF16)
        dv_ref[rows, :] += jnp.concatenate(dv_sum, axis=1)
        dk_ref[rows, :] += jnp.concatenate(dk_sum, axis=1)

        @pl.when(i == nq - 1)
        def _():
            acc = ds_acc[...]
            for u in range(nbs):
                dsink_ref[u] = jnp.concatenate(
                    [jnp.broadcast_to(jnp.sum(acc[u:u + 1, 0:tq], axis=1, keepdims=True), (1, LANES)),
                     jnp.broadcast_to(jnp.sum(acc[u:u + 1, tq:2 * tq], axis=1, keepdims=True), (1, LANES)),
                     jnp.zeros((6, LANES), F32)], axis=0)

    qmap = lambda b, i: (i, b)
    kv_spec = pl.BlockSpec((S, kvw), lambda b, i: (0, b))
    row_spec = pl.BlockSpec((nbs, 1, 2 * tq), lambda b, i: (b, 0, 0))
    wide = pl.BlockSpec((tq, nbs * LANES), qmap)
    return pl.pallas_call(
        body, name=name, grid=(nb // nbs, nq),
        in_specs=[wide, kv_spec, pl.BlockSpec((kvw, S), lambda b, i: (b, 0)), kv_spec, wide, wide,
                  pl.BlockSpec((nbs, 1, 1, 2 * tq), lambda b, i: (b, i, 0, 0)), _win_dist_spec(nk, tq, nq), row_spec, row_spec],
        out_specs=[wide, kv_spec, kv_spec, pl.BlockSpec((nbs, 8, LANES), lambda b, i: (b, 0, 0))],
        out_shape=[_sds((S, nb * LANES), BF16), _sds((S, nkb * LANES), F32), _sds((S, nkb * LANES), F32), _sds((nb, 8, LANES), F32)],
        scratch_shapes=[pltpu.VMEM((nbs, 2 * tq), F32)],
        compiler_params=_params(("arbitrary", "arbitrary")))(q, k, kT, v, o, do, lse, dist, slope, sink)


def _sum_rows(v):
    return jnp.sum(v, axis=0, keepdims=True)


def _norm_mod_bwd(dh, xv, mod_ref, nw_ref, stats_ref):
    r = _rms(xv)
    xn = xv * r
    nw = nw_ref[...]
    stats_ref[0:1, :] += _sum_rows(dh)
    stats_ref[1:2, :] += _sum_rows(dh * (xn * nw))
    dn = dh * (1.0 + mod_ref[1:2, :])
    stats_ref[2:3, :] += _sum_rows(dn * xn)
    return _rms_bwd(xv, r, dn * nw)


def _even_gate_specs(ts):
    return [pl.BlockSpec((ts, 256), lambda i, c=c: (i, c)) for c in (3, 4, 7, 8)]


def _even_post_fwd(oa, olat, proj, x, gate, wuv, woe):
    S = x.shape[0]
    ts = min(ROW_TILE, S)

    def body(oa_ref, ol_ref, ga0_ref, ga1_ref, gb0_ref, gb1_ref, x_ref, gate_ref, wuv_ref, woe_ref, y_ref, x1_ref):
        sa, _ = _silu_and_grad(jnp.concatenate([ga0_ref[...], ga1_ref[...]], axis=1))
        sb, _ = _silu_and_grad(jnp.concatenate([gb0_ref[...], gb1_ref[...]], axis=1))
        olc = jnp.concatenate([ol_ref[hh] for hh in range(B_HEADS)], axis=1).astype(BF16)
        ob = _dot(olc, wuv_ref[...])
        mix = jnp.concatenate([oa_ref[...] * sa, ob * sb], axis=1).astype(BF16)
        y = _dot(mix, woe_ref[...])
        y_ref[...] = y.astype(BF16)
        x1_ref[...] = x_ref[...] + gate_ref[...] * y

    return pl.pallas_call(
        body, name="even_post_fwd", grid=(S // ts,),
        in_specs=[_row_spec(ts, 512), pl.BlockSpec((B_HEADS, ts, LANES), lambda i: (0, i, 0))] + _even_gate_specs(ts) +
                 [_row_spec(ts, D_MODEL), _full_spec((1, D_MODEL)), _full_spec((1024, 512)), _full_spec((1024, D_MODEL))],
        out_specs=[_row_spec(ts, D_MODEL), _row_spec(ts, D_MODEL)],
        out_shape=[_sds((S, D_MODEL), BF16), _sds((S, D_MODEL), F32)],
        compiler_params=_params(("arbitrary",)),
    )(oa, olat, proj, proj, proj, proj, x, gate, wuv, woe)


def _odd_pre_fwd(x, mod, nw, wio):
    S = x.shape[0]
    ts = min(ROW_TILE, S)

    def body(x_ref, mod_ref, nw_ref, wio_ref, h_ref, g_ref, q_ref, k_ref, v_ref, kt_ref, vt_ref):
        xv = x_ref[...]
        h = (xv * _rms(xv) * nw_ref[...]) * (1.0 + mod_ref[1:2, :]) + mod_ref[0:1, :]
        hb = h.astype(BF16)
        h_ref[...] = hb
        proj = jnp.concatenate([_dot(hb, wio_ref[p]) for p in range(N_CHIPS)], axis=1)
        g_ref[...] = proj[:, 1536:2560]
        q_ref[...] = (proj[:, 0:1024] * 0.125).astype(BF16)
        lane = _lane_iota()
        k_v = jnp.concatenate([_dup_heads(proj[:, 1024 + LANES * j:1024 + LANES * (j + 1)], lane) for j in range(2)], axis=1)
        v_v = jnp.concatenate([_dup_heads(proj[:, 1280 + LANES * j:1280 + LANES * (j + 1)], lane) for j in range(2)], axis=1)
        k_ref[...] = k_v.astype(BF16)
        v_ref[...] = v_v.astype(BF16)
        kt_ref[...] = k_v.T.astype(BF16)
        vt_ref[...] = v_v.T.astype(BF16)

    col_spec = pl.BlockSpec((512, ts), lambda i: (0, i))
    return pl.pallas_call(
        body, name="odd_pre_fwd", grid=(S // ts,),
        in_specs=[_row_spec(ts, D_MODEL), _full_spec((3, D_MODEL)), _full_spec((1, D_MODEL)),
                  _full_spec((N_CHIPS, D_MODEL, ODD_IN // N_CHIPS))],
        out_specs=[_row_spec(ts, D_MODEL), _row_spec(ts, 1024), _row_spec(ts, 1024), _row_spec(ts, 512), _row_spec(ts, 512),
                   col_spec, col_spec],
        out_shape=[_sds((S, D_MODEL), BF16), _sds((S, 1024), F32), _sds((S, 1024), BF16), _sds((S, 512), BF16),
                   _sds((S, 512), BF16), _sds((512, S), BF16), _sds((512, S), BF16)],
        compiler_params=_params(("arbitrary",)),
    )(x, mod, nw, wio)


def _odd_post(oc, g, x1, gate, woo, fw, tgt):
    S = x1.shape[0]
    ts = min(ROW_TILE, S)
    nsteps = S // ts

    def body(oc_ref, g_ref, x_ref, gate_ref, woo_ref, fw_ref, tgt_ref, doc_ref, dgc_ref, dx2_ref, dwoo_out, stats_ref, dwoo_ref):
        @pl.when(pl.program_id(0) == 0)
        def _():
            dwoo_ref[...] = jnp.zeros((D_MODEL, D_MODEL), F32)
            stats_ref[...] = jnp.zeros((8, D_MODEL), F32)

        ocv = oc_ref[...]
        sg, dsg = _silu_and_grad(g_ref[...])
        mix = (ocv * sg).astype(BF16)
        woo_v = woo_ref[...]
        y = _dot(mix, woo_v)
        gate_v = gate_ref[...]
        x2 = x_ref[...] + gate_v * y
        r = _rms(x2)
        fw_v = fw_ref[...]
        xn = x2 * r
        err = xn * fw_v - tgt_ref[...]
        dout = err * (1.0 / D_MODEL)
        dx2 = _rms_bwd(x2, r, dout * fw_v)
        dx2_ref[...] = dx2
        stats_ref[0:1, :] += _sum_rows(dout * xn)
        stats_ref[1:2, :] += _sum_rows(dx2 * y)
        loss_t = 0.5 * jnp.sum(_sum_rows(err * dout), axis=-1, keepdims=True)
        stats_ref[2:3, :] += jnp.broadcast_to(loss_t, (1, D_MODEL))
        dy = (gate_v * dx2).astype(BF16)
        dmix = _dot_nt(dy, woo_v)
        dwoo_ref[...] += _dot_tn(mix, dy)
        doc_ref[...] = (dmix * sg).astype(BF16)
        dgc_ref[...] = (dmix * ocv * dsg).astype(BF16)

        @pl.when(pl.program_id(0) == nsteps - 1)
        def _():
            dwoo_out[...] = dwoo_ref[...].astype(BF16)

    return pl.pallas_call(
        body, name="odd_post", grid=(nsteps,),
        in_specs=[_row_spec(ts, D_MODEL), _row_spec(ts, D_MODEL), _row_spec(ts, D_MODEL), _full_spec((1, D_MODEL)),
                  _full_spec((D_MODEL, D_MODEL)), _full_spec((1, D_MODEL)), _row_spec(ts, D_MODEL)],
        out_specs=[_row_spec(ts, D_MODEL), _row_spec(ts, D_MODEL), _row_spec(ts, D_MODEL),
                   _full_spec((D_MODEL, D_MODEL), single=False), _full_spec((8, D_MODEL), single=False)],
        out_shape=[_sds((S, D_MODEL), BF16), _sds((S, D_MODEL), BF16), _sds((S, D_MODEL), F32), _sds((D_MODEL, D_MODEL), BF16),
                   _sds((8, D_MODEL), F32)],
        scratch_shapes=[pltpu.VMEM((D_MODEL, D_MODEL), F32)],
        compiler_params=_params(("arbitrary",)),
    )(oc, g, x1, gate, woo, fw, tgt)


def _odd_pre_bwd(dq, dk, dv, dgc, h, x, dx_res, mod, nw, wio):
    S = x.shape[0]
    ts = min(IN_PROJ_ROW_TILE, S)
    nsteps = S // ts
    wsh = ODD_IN // N_CHIPS

    def body(dq_ref, dk_ref, dv_ref, dgc_ref, h_ref, x_ref, dxr_ref, mod_ref, nw_ref, wio_ref, dx_ref, dw_ref, stats_ref, dw_acc):
        @pl.when(pl.program_id(0) == 0)
        def _():
            dw_acc[...] = jnp.zeros((N_CHIPS, D_MODEL, wsh), F32)
            stats_ref[...] = jnp.zeros((8, D_MODEL), F32)

        lane = _lane_iota()
        dkv = [_fold_heads(r[:, 2 * LANES * j:2 * LANES * (j + 1)], lane).astype(BF16) for r in (dk_ref, dv_ref) for j in range(2)]
        dproj = jnp.concatenate([dq_ref[...]] + dkv + [dgc_ref[...]], axis=1)
        hv = h_ref[...]
        dh = None
        for p in range(N_CHIPS):
            dp_cols = dproj[:, wsh * p:wsh * (p + 1)]
            part = _dot_nt(dp_cols, wio_ref[p])
            dh = part if dh is None else dh + part
            dw_acc[p] += _dot_tn(hv, dp_cols)
        dx_ref[...] = dxr_ref[...] + _norm_mod_bwd(dh, x_ref[...], mod_ref, nw_ref, stats_ref)

        @pl.when(pl.program_id(0) == nsteps - 1)
        def _():
            dw_ref[...] = dw_acc[...].astype(BF16)

    return pl.pallas_call(
        body, name="odd_pre_bwd", grid=(nsteps,),
        in_specs=[_row_spec(ts, 1024), _row_spec(ts, 512), _row_spec(ts, 512), _row_spec(ts, 1024), _row_spec(ts, D_MODEL),
                  _row_spec(ts, D_MODEL), _row_spec(ts, D_MODEL), _full_spec((3, D_MODEL)), _full_spec((1, D_MODEL)),
                  _full_spec((N_CHIPS, D_MODEL, wsh))],
        out_specs=[_row_spec(ts, D_MODEL), _full_spec((N_CHIPS, D_MODEL, wsh), single=False), _full_spec((8, D_MODEL), single=False)],
        out_shape=[_sds((S, D_MODEL), F32), _sds((N_CHIPS, D_MODEL, wsh), BF16), _sds((8, D_MODEL), F32)],
        scratch_shapes=[pltpu.VMEM((N_CHIPS, D_MODEL, wsh), F32)],
        compiler_params=_params(("arbitrary",)),
    )(dq, dk, dv, dgc, h, x, dx_res, mod, nw, wio)


def _even_post_bwd(dx1, y, oa, olat, proj, gate, wuv, woe):
    S = dx1.shape[0]
    ts = min(ROW_TILE, S)
    nsteps = S // ts

    def body(dx_ref, y_ref, oa_ref, ol_ref, ga0_ref, ga1_ref, gb0_ref, gb1_ref, gate_ref, wuv_ref, woe_ref,
             doa_ref, dga_ref, dgb_ref, dol_ref, dwoe_out, dwuv_ref, stats_ref, dwoe_ref):
        @pl.when(pl.program_id(0) == 0)
        def _():
            dwoe_ref[...] = jnp.zeros((D_MODEL, D_MODEL), F32)
            dwuv_ref[...] = jnp.zeros((1024, 512), F32)
            stats_ref[...] = jnp.zeros((8, D_MODEL), F32)

        dxv = dx_ref[...]
        stats_ref[0:1, :] += _sum_rows(dxv * y_ref[...])
        dy = (gate_ref[...] * dxv).astype(BF16)
        sa, dsa = _silu_and_grad(jnp.concatenate([ga0_ref[...], ga1_ref[...]], axis=1))
        sb, dsb = _silu_and_grad(jnp.concatenate([gb0_ref[...], gb1_ref[...]], axis=1))
        olc = jnp.concatenate([ol_ref[hh] for hh in range(B_HEADS)], axis=1).astype(BF16)
        wuv_v = wuv_ref[...]
        ob = _dot(olc, wuv_v)
        oav = oa_ref[...]
        mix = jnp.concatenate([oav * sa, ob * sb], axis=1).astype(BF16)
        dmix = _dot_nt(dy, woe_ref[...])
        dwoe_ref[...] += _dot_tn(mix, dy)
        dma, dmb = dmix[:, 0:512], dmix[:, 512:1024]
        doa_ref[...] = (dma * sa).astype(BF16)
        dga_ref[...] = (dma * oav * dsa).astype(BF16)
        dgb_ref[...] = (dmb * ob * dsb).astype(BF16)
        dob = (dmb * sb).astype(BF16)
        dol = _dot_nt(dob, wuv_v)
        dwuv_ref[...] += _dot_tn(olc, dob)
        for hh in range(B_HEADS):
            dol_ref[hh] = dol[:, LANES * hh:LANES * (hh + 1)].astype(BF16)

        @pl.when(pl.program_id(0) == nsteps - 1)
        def _():
            dwoe_out[...] = dwoe_ref[...].astype(BF16)

    head_spec = pl.BlockSpec((B_HEADS, ts, LANES), lambda i: (0, i, 0))
    return pl.pallas_call(
        body, name="even_post_bwd", grid=(nsteps,),
        in_specs=[_row_spec(ts, D_MODEL), _row_spec(ts, D_MODEL), _row_spec(ts, 512), head_spec] + _even_gate_specs(ts) +
                 [_full_spec((1, D_MODEL)), _full_spec((1024, 512)), _full_spec((1024, D_MODEL))],
        out_specs=[_row_spec(ts, 512), _row_spec(ts, 512), _row_spec(ts, 512), head_spec,
                   _full_spec((D_MODEL, D_MODEL), single=False), _full_spec((1024, 512), single=False),
                   _full_spec((8, D_MODEL), single=False)],
        out_shape=[_sds((S, 512), BF16), _sds((S, 512), BF16), _sds((S, 512), BF16), _sds((B_HEADS, S, LANES), BF16),
                   _sds((D_MODEL, D_MODEL), BF16), _sds((1024, 512), F32), _sds((8, D_MODEL), F32)],
        scratch_shapes=[pltpu.VMEM((D_MODEL, D_MODEL), F32)],
        compiler_params=_params(("arbitrary",)),
    )(dx1, y, oa, olat, proj, proj, proj, proj, gate, wuv, woe)


def _even_pre_bwd(x, h, proj, dqa, dka, dva, dga, dgb, dqcat, dkcat, dx_res, mod, nw, wie, qn, kn, seg, ca, sa, ct, st,
                  qln, kvln, wuq, wuk):
    S = x.shape[0]
    ts = min(IN_PROJ_ROW_TILE, S)
    nsteps = S // ts

    def body(x_ref, h_ref, proj_ref, dqa_ref, dka_ref, dva_ref, dga_ref, dgb_ref, dqc_ref, dkc_ref, dxr_ref, mod_ref, nw_ref,
             wie_ref, qn_ref, kn_ref, seg_ref, ca_ref, sa_ref, ct_ref, st_ref, qln_ref, kvln_ref, wuq_ref, wuk_ref,
             dx_ref, dwie_out, dwuq_out, dwuk_out, stats_ref, nstats_ref, dwie_ref, dwuq_ref, dwuk_ref, stage):
        @pl.when(pl.program_id(0) == 0)
        def _():
            dwie_ref[...] = jnp.zeros((EVEN_P, D_MODEL), F32)
            dwuq_ref[...] = jnp.zeros((1536, B_Q_LORA), F32)
            dwuk_ref[...] = jnp.zeros((512, 1024), F32)
            stats_ref[...] = jnp.zeros((8, D_MODEL), F32)
            nstats_ref[...] = jnp.zeros((8, 256), F32)

        lane = _lane_iota()
        ca_v, sa_v, ct_v, st_v = ca_ref[...], sa_ref[...], ct_ref[...], st_ref[...]
        seg_v = seg_ref[...]

        def head_norm_bwd(xc, dy, w):
            r = lax.rsqrt(_seg_mean(xc * xc, seg_v) + EPS)
            g = dy * w
            dxc = r * g - xc * (r * r * r) * _seg_mean(xc * g, seg_v)
            return dxc, _sum_rows(dy * (xc * r))

        pieces = []
        dqn = jnp.zeros((1, LANES), F32)
        for cb in range(4):
            sl = slice(LANES * cb, LANES * (cb + 1))
            dy = _rot_bwd(dqa_ref[:, sl] * 0.125, ca_v, sa_v, lane)
            dxc, dw = head_norm_bwd(proj_ref[:, sl], dy, qn_ref[...])
            pieces.append(dxc)
            dqn = dqn + dw
        dxc, dkn = head_norm_bwd(proj_ref[:, 512:640], _rot_bwd(_fold_heads(dka_ref[...], lane), ca_v, sa_v, lane), kn_ref[...])
        pieces += [dxc, _fold_heads(dva_ref[...], lane), dga_ref[...]]
        nstats_ref[0:1, 0:LANES] += dqn + pltpu.roll(dqn, HEAD_DIM, 1)
        nstats_ref[1:2, 0:LANES] += dkn + pltpu.roll(dkn, HEAD_DIM, 1)

        cq = proj_ref[:, 1280:1536]
        rq = _rms(cq)
        cqn_f = cq * rq
        qln_v = qln_ref[...]
        cqn = (cqn_f * qln_v).astype(BF16)
        wuq_v, wuk_v = wuq_ref[...], wuk_ref[...]
        qnope = _dot_nt(cqn, wuq_v[0:512, :]).astype(BF16)
        dqlat = jnp.concatenate([dqc_ref[hh, :, 0:LANES] for hh in range(B_HEADS)], axis=1).astype(BF16)
        dqnope = _dot_nt(dqlat, wuk_v)
        dwuk_ref[...] += _dot_tn(qnope, dqlat)
        dqr = [_rot_bwd(dqc_ref[hh, :, LANES:2 * LANES], ct_v, st_v, lane) for hh in range(B_HEADS)]
        dqb = jnp.concatenate([dqnope] + dqr, axis=1).astype(BF16)
        dcqn = _dot(dqb, wuq_v)
        dwuq_ref[...] += _dot_tn(dqb, cqn)
        nstats_ref[2:3, :] += _sum_rows(dcqn * cqn_f)
        dcq = _rms_bwd(cq, rq, dcqn * qln_v)
        ckv = proj_ref[:, 1536:1664]
        rk = _rms(ckv)
        dckvn = dkc_ref[:, 0:LANES]
        nstats_ref[3:4, 0:LANES] += _sum_rows(dckvn * (ckv * rk))
        dckv = _rms_bwd(ckv, rk, dckvn * kvln_ref[...])
        dkr = _rot_bwd(dkc_ref[:, LANES:2 * LANES], ct_v, st_v, lane)
        pieces += [dcq, dckv, dkr, dgb_ref[...]]
        dproj = jnp.concatenate([piece.astype(BF16) for piece in pieces], axis=1)
        dh = _dot(dproj, wie_ref[...])
        dwie_ref[...] += _dot_tn(dproj, h_ref[...])
        dx_ref[...] = dxr_ref[...] + _norm_mod_bwd(dh, x_ref[...], mod_ref, nw_ref, stats_ref)

        @pl.when(pl.program_id(0) == nsteps - 1)
        def _():
            for r0 in range(0, EVEN_P, 256):
                stage[...] = dwie_ref[r0:r0 + 256, :].astype(BF16)
                pltpu.sync_copy(stage, dwie_out.at[pl.ds(r0, 256), :])
            pltpu.sync_copy(dwuq_ref, dwuq_out)
            pltpu.sync_copy(dwuk_ref, dwuk_out)

    return pl.pallas_call(
        body, name="even_pre_bwd", grid=(nsteps,),
        in_specs=[_row_spec(ts, D_MODEL), _row_spec(ts, D_MODEL), _row_spec(ts, EVEN_P), _row_spec(ts, 512), _row_spec(ts, 2 * LANES),
                  _row_spec(ts, 2 * LANES), _row_spec(ts, 512), _row_spec(ts, 512),
                  pl.BlockSpec((B_HEADS, ts, 2 * LANES), lambda i: (0, i, 0)), _row_spec(ts, 2 * LANES), _row_spec(ts, D_MODEL),
                  _full_spec((3, D_MODEL)), _full_spec((1, D_MODEL)), _full_spec((EVEN_P, D_MODEL)),
                  _full_spec((1, LANES)), _full_spec((1, LANES)), _full_spec((LANES, LANES)),
                  _row_spec(ts, LANES), _row_spec(ts, LANES), _row_spec(ts, LANES), _row_spec(ts, LANES),
                  _full_spec((1, B_Q_LORA)), _full_spec((1, B_KV_LORA)), _full_spec((1536, B_Q_LORA)), _full_spec((512, 1024))],
        out_specs=[_row_spec(ts, D_MODEL), _ANY, _ANY, _ANY, _full_spec((8, D_MODEL), single=False), _full_spec((8, 256), single=False)],
        out_shape=[_sds((S, D_MODEL), F32), _sds((EVEN_P, D_MODEL), BF16), _sds((1536, B_Q_LORA), F32), _sds((512, 1024), F32),
                   _sds((8, D_MODEL), F32), _sds((8, 256), F32)],
        scratch_shapes=[pltpu.VMEM((EVEN_P, D_MODEL), F32), pltpu.VMEM((1536, B_Q_LORA), F32), pltpu.VMEM((512, 1024), F32),
                        pltpu.VMEM((256, D_MODEL), BF16)],
        compiler_params=_params(("arbitrary",)),
    )(x, h, proj, dqa, dka, dva, dga, dgb, dqcat, dkcat, dx_res, mod, nw, wie, qn, kn, seg, ca, sa, ct, st, qln, kvln, wuq, wuk)


def _ada_fwd(c_all, w, b):
    n = w.shape[2]

    def body(c_ref, w_ref, b_ref, o_ref):
        cv = c_ref[...]
        o_ref[0] = _dot_f32(cv * _sigmoid(cv), w_ref[0]) + b_ref[0]

    return pl.pallas_call(
        body, name="ada_fwd", grid=(2,),
        in_specs=[pl.BlockSpec((N_DEV, D_MODEL), lambda l: (0, 0)), pl.BlockSpec((1, D_MODEL, n), lambda l: (l, 0, 0)),
                  pl.BlockSpec((1, 1, n), lambda l: (l, 0, 0))],
        out_specs=pl.BlockSpec((1, N_DEV, n), lambda l: (l, 0, 0)),
        out_shape=_sds((2, N_DEV, n), F32),
        compiler_params=_params(("arbitrary",)),
    )(c_all, w, b)


def _ada_bwd(c_all_t, dmod):
    n = dmod.shape[2]

    def body(c_ref, d_ref, o_ref):
        cv = c_ref[...]
        act = cv * _sigmoid(cv)
        dv = d_ref[0]
        acc = act[:, 0:1] * dv[0:1, :]
        for bb in range(1, N_DEV):
            acc = acc + act[:, bb:bb + 1] * dv[bb:bb + 1, :]
        o_ref[0] = acc

    return pl.pallas_call(
        body, name="ada_bwd", grid=(2,),
        in_specs=[pl.BlockSpec((D_MODEL, N_DEV), lambda l: (0, 0)), pl.BlockSpec((1, N_DEV, n), lambda l: (l, 0, 0))],
        out_specs=pl.BlockSpec((1, D_MODEL, n), lambda l: (l, 0, 0)),
        out_shape=_sds((2, D_MODEL, n), F32),
        compiler_params=_params(("arbitrary",)),
    )(c_all_t, dmod)


ADAM_ROW_TILE = 512


def _adam_update(g, w, m, v):
    m_new = ADAM_B1 * m + (1.0 - ADAM_B1) * g
    v_new = ADAM_B2 * v + (1.0 - ADAM_B2) * jnp.square(g)
    m_hat = m_new / (1.0 - ADAM_B1 ** ADAM_STEP)
    v_hat = v_new / (1.0 - ADAM_B2 ** ADAM_STEP)
    return -ADAM_LR * (m_hat / (jnp.sqrt(v_hat) + ADAM_EPS) + ADAM_WD * w), m_new, v_new


SMALL_ROWS = dict(dmod=(0, D_MODEL), norm_w=(6, D_MODEL), final_norm=(8, D_MODEL), a_q_norm=(9, HEAD_DIM), a_k_norm=(10, HEAD_DIM),
                  b_q_lora_norm=(11, B_Q_LORA), b_kv_lora_norm=(12, B_KV_LORA), c_sink=(13, C_HEADS))
SMALL_WEIGHTS = ("ada_b", "norm_w", "final_norm", "a_q_norm", "a_k_norm", "b_q_lora_norm", "b_kv_lora_norm", "c_sink")
LOSS_ROW = 14


def _pack_small(res):
    def padded(v):
        return jnp.concatenate([v, jnp.zeros((v.shape[0], D_MODEL - v.shape[1]), F32)], axis=1)

    rows = [res["dmod"].reshape(6, D_MODEL), res["norm_w"], res["final_norm"].reshape(1, D_MODEL)]
    rows += [padded(res[k]) for k in ("a_q_norm", "a_k_norm", "b_q_lora_norm", "b_kv_lora_norm", "c_sink")]
    return jnp.concatenate(rows + [res["loss_row"], jnp.zeros((1, D_MODEL), F32)], axis=0)


def _adam_small(parts, ws, ms, vs):
    nw = len(SMALL_WEIGHTS)

    def body(*refs):
        p_ref = refs[0]
        w_refs, m_refs, v_refs = refs[1:1 + nw], refs[1 + nw:1 + 2 * nw], refs[1 + 2 * nw:1 + 3 * nw]
        outs = refs[1 + 3 * nw:]
        g_all = p_ref[0]
        for k in range(1, N_DEV):
            g_all = g_all + p_ref[k]
        for idx, name in enumerate(SMALL_WEIGHTS):
            if name == "ada_b":
                g = jnp.concatenate([jnp.concatenate([g_all[3 * l + t:3 * l + t + 1] for t in range(3)], axis=1) for l in range(2)],
                                    axis=0)
            else:
                row, width = SMALL_ROWS[name]
                g = g_all[row:row + w_refs[idx].shape[0], 0:width]
            d, m_new, v_new = _adam_update(g, w_refs[idx][...], m_refs[idx][...], v_refs[idx][...])
            outs[4 * idx][...], outs[4 * idx + 1][...], outs[4 * idx + 2][...], outs[4 * idx + 3][...] = g, d, m_new, v_new
        outs[4 * nw][...] = g_all[LOSS_ROW:LOSS_ROW + 1, 0:LANES]

    out_shape = []
    for w in ws:
        out_shape += [_sds(w.shape, F32)] * 4
    out_shape.append(_sds((1, LANES), F32))
    return pl.pallas_call(body, name="adam_small", out_shape=out_shape,
                          compiler_params=pltpu.CompilerParams(vmem_limit_bytes=VMEM_LIMIT))(parts, *ws, *ms, *vs)


def _adam(parts, w, m, v, name, by_columns=False):
    P, R, C = parts.shape
    if by_columns:
        tr, tc = R, 256
    else:
        tr, tc = (R if R <= ADAM_ROW_TILE else ADAM_ROW_TILE), C
    assert R % tr == 0 and C % tc == 0

    def body(p_ref, w_ref, m_ref, v_ref, g_ref, d_ref, nm_ref, nv_ref):
        g = p_ref[0].astype(F32)
        for k in range(1, P):
            g = g + p_ref[k].astype(F32)
        g_ref[...] = g
        d_ref[...], nm_ref[...], nv_ref[...] = _adam_update(g, w_ref[...], m_ref[...], v_ref[...])

    tile = (lambda i: (0, i)) if by_columns else (lambda i: (i, 0))
    spec = pl.BlockSpec((tr, tc), tile)
    return pl.pallas_call(
        body, name=name, grid=(C // tc if by_columns else R // tr,),
        in_specs=[pl.BlockSpec((P, tr, tc), lambda i: (0,) + tile(i)), spec, spec, spec],
        out_specs=[spec, spec, spec, spec], out_shape=[_sds((R, C), F32)] * 4,
        compiler_params=_params(("arbitrary",)),
    )(parts, w, m, v)


_ANY = pl.BlockSpec(memory_space=pl.ANY)
CHIP_FLIPS = ((1, 0), (0, 1), (1, 1))
DEV_FLIPS = tuple((dx, dy, dc) for dx in (0, 1) for dy in (0, 1) for dc in (0, 1) if dx + dy + dc)


def _flip(a, d):
    return a if d == 0 else 1 - a


def _my_place():
    return lax.axis_index("x"), lax.axis_index("y"), lax.axis_index("c")


def _gather8_copies(ins, outs, send_sems, recv_sems, loc_sems):
    x, y, c = _my_place()
    me = 4 * x + 2 * y + c
    copies = []
    for a in range(len(ins)):
        copies.append(pltpu.make_async_copy(ins[a], outs[a].at[me], loc_sems.at[a]))
        for k, (dx, dy, dc) in enumerate(DEV_FLIPS):
            copies.append(pltpu.make_async_remote_copy(
                src_ref=ins[a], dst_ref=outs[a].at[me], send_sem=send_sems.at[a, k], recv_sem=recv_sems.at[a, k],
                device_id=(_flip(x, dx), _flip(y, dy), _flip(c, dc)), device_id_type=MESH_ID))
    return copies


def _gather8_sems(n):
    return [pltpu.SemaphoreType.DMA((n, 7)), pltpu.SemaphoreType.DMA((n, 7)), pltpu.SemaphoreType.DMA((n,))]


def _gather_dev8(arrs, name):
    n = len(arrs)

    def body(*refs):
        copies = _gather8_copies(refs[:n], refs[n:2 * n], *refs[2 * n:])
        for cp in copies:
            cp.start()
        for cp in copies:
            cp.wait()

    return pl.pallas_call(
        body, name=name, in_specs=[_ANY] * n, out_specs=[_ANY] * n,
        out_shape=[_sds((N_DEV,) + a.shape, a.dtype) for a in arrs], scratch_shapes=_gather8_sems(n),
    )(*arrs)


class _Exchange:
    def __init__(self, arrs, out_shapes, n_sems, phases):
        self.arrs, self.out_shapes, self.n_sems, self._phases = list(arrs), list(out_shapes), n_sems, phases

    @property
    def n(self):
        return len(self.arrs)

    def sem_shapes(self):
        return [pltpu.SemaphoreType.DMA((self.n, self.n_sems)), pltpu.SemaphoreType.DMA((self.n, self.n_sems)),
                pltpu.SemaphoreType.DMA((self.n,))]

    def phases(self, ins, outs, sems):
        return self._phases(ins, outs, *sems)

    def run(self, name):
        n = self.n

        def body(*refs):
            start, mid, end = self.phases(refs[:n], refs[n:2 * n], refs[2 * n:])
            start()
            mid()
            end()

        return pl.pallas_call(body, name=name, in_specs=[_ANY] * n, out_specs=[_ANY] * n, out_shape=self.out_shapes,
                              scratch_shapes=self.sem_shapes())(*self.arrs)

def _gather_halves_phases(ins, outs, send_sems, recv_sems, loc_sems):
    n = len(ins)
    x, y, c = _my_place()
    chip = 2 * x + y
    sibling = (x, y, 1 - c)
    peers = [(_flip(x, dx), _flip(y, dy)) for dx, dy in CHIP_FLIPS]

    def remote(src, p, half, a, k, to):
        return pltpu.make_async_remote_copy(src_ref=src, dst_ref=outs[a].at[p, half], send_sem=send_sems.at[a, k],
                                            recv_sem=recv_sems.at[a, k], device_id=to, device_id_type=MESH_ID)

    def local(a):
        return pltpu.make_async_copy(ins[a], outs[a].at[chip], loc_sems.at[a])

    def first(a, k):
        return remote(ins[a].at[c], chip, c, a, k, (*peers[k], c))

    def passed(a, k):
        p = 2 * peers[k][0] + peers[k][1]
        return remote(outs[a].at[p, c], p, c, a, 3 + k, sibling)

    def start():
        for a in range(n):
            local(a).start()
            for k in range(3):
                first(a, k).start()

    def mid():
        for a in range(n):
            for k in range(3):
                p = 2 * peers[k][0] + peers[k][1]
                remote(outs[a].at[p, c], p, c, a, k, sibling).wait_recv()
                passed(a, k).start()

    def end():
        for a in range(n):
            for k in range(3):
                p = 2 * peers[k][0] + peers[k][1]
                remote(outs[a].at[p, 1 - c], p, 1 - c, a, 3 + k, sibling).wait_recv()
        for a in range(n):
            for k in range(3):
                first(a, k).wait_send()
                passed(a, k).wait_send()
            local(a).wait()

    return start, mid, end


def _gather_chip4_halves(arrs):
    return _Exchange(arrs, [_sds((N_CHIPS,) + a.shape, a.dtype) for a in arrs], 6, _gather_halves_phases)


def _reduce_phases(n_whole, ins, outs, send_sems, recv_sems, loc_sems):
    n = len(ins)
    x, y, c = _my_place()
    chip = 2 * x + y
    sibling = (x, y, 1 - c)
    peers = [(_flip(x, dx), _flip(y, dy)) for dx, dy in CHIP_FLIPS]

    def remote(src, slot, a, k, to):
        return pltpu.make_async_remote_copy(src_ref=src, dst_ref=outs[a].at[slot], send_sem=send_sems.at[a, k],
                                            recv_sem=recv_sems.at[a, k], device_id=to, device_id_type=MESH_ID)

    def block(a, p):
        return ins[a] if a >= n - n_whole else ins[a].at[p]

    def local(a):
        return pltpu.make_async_copy(block(a, chip), outs[a].at[2 * chip + c], loc_sems.at[a])

    def own(a):
        return remote(block(a, chip), 2 * chip + c, a, 0, sibling)

    def first(a, k):
        return remote(block(a, 2 * peers[k][0] + peers[k][1]), 2 * chip + c, a, 1 + k, (*peers[k], c))

    def passed(a, k):
        slot = 2 * (2 * peers[k][0] + peers[k][1]) + c
        return remote(outs[a].at[slot], slot, a, 4 + k, sibling)

    def start():
        for a in range(n):
            local(a).start()
            own(a).start()
            for k in range(3):
                first(a, k).start()

    def mid():
        for a in range(n):
            for k in range(3):
                slot = 2 * (2 * peers[k][0] + peers[k][1]) + c
                remote(outs[a].at[slot], slot, a, 1 + k, sibling).wait_recv()
                passed(a, k).start()

    def end():
        for a in range(n):
            remote(outs[a].at[2 * chip + 1 - c], 2 * chip + 1 - c, a, 0, sibling).wait_recv()
            for k in range(3):
                slot = 2 * (2 * peers[k][0] + peers[k][1]) + 1 - c
                remote(outs[a].at[slot], slot, a, 4 + k, sibling).wait_recv()
        for a in range(n):
            own(a).wait_send()
            for k in range(3):
                first(a, k).wait_send()
                passed(a, k).wait_send()
            local(a).wait()

    return start, mid, end


def _reduce_exchange(arrs, whole=()):
    shapes = [_sds((N_DEV,) + a.shape[1:], a.dtype) for a in arrs] + [_sds((N_DEV,) + a.shape, a.dtype) for a in whole]
    return _Exchange(list(arrs) + list(whole), shapes, 7, functools.partial(_reduce_phases, len(whole)))


def _shard_halves_t(w):
    wt = w.T.astype(BF16)
    n2 = wt.shape[0] // 2
    pad = jnp.zeros((-n2 % 16, wt.shape[1]), BF16)
    return jnp.stack([jnp.concatenate([wt[0:n2], pad], axis=0), jnp.concatenate([wt[n2:], pad], axis=0)])


def _gathered_rows(g, n):
    return [g[p, half, 0:n // 2] for p in range(N_CHIPS) for half in range(2)]


def _even_in_layout_t(g):
    pieces = _gathered_rows(g, EVEN_IN // N_CHIPS)
    cut = 1696 - 3 * (EVEN_IN // N_CHIPS)
    last = pieces[6]
    return jnp.concatenate(pieces[0:6] + [last[0:cut], jnp.zeros((96, last.shape[1]), last.dtype), last[cut:], pieces[7]], axis=0)


def _even_in_unlayout_t(g):
    return jnp.concatenate([g[0:1696], g[1792:2304]], axis=0)


def _uq_layout_t(g):
    wt = jnp.concatenate(_gathered_rows(g, B_HEADS * (B_NOPE + B_ROPE) // N_CHIPS), axis=0)
    per = B_NOPE + B_ROPE
    pad = jnp.zeros((LANES - B_ROPE, wt.shape[1]), wt.dtype)
    nope = [wt[per * h:per * h + B_NOPE] for h in range(B_HEADS)]
    rope = [jnp.concatenate([wt[per * h + B_NOPE:per * (h + 1)], pad], axis=0) for h in range(B_HEADS)]
    return jnp.concatenate(nope + rope, axis=0)


def _uq_unlayout_t(g):
    parts = []
    for h in range(B_HEADS):
        parts += [g[B_NOPE * h:B_NOPE * (h + 1)], g[512 + LANES * h:512 + LANES * h + B_ROPE]]
    return jnp.concatenate(parts, axis=0)


def _block_diag(blocks):
    rows = []
    for h, blk in enumerate(blocks):
        r, cdim = blk.shape
        n = len(blocks)
        rows.append(jnp.concatenate([jnp.zeros((r, cdim * h), blk.dtype), blk, jnp.zeros((r, cdim * (n - 1 - h)), blk.dtype)],
                                    axis=1))
    return jnp.concatenate(rows, axis=0)


def _uk_layout(w):
    return _block_diag([w[:, h, :].T for h in range(B_HEADS)])


def _uk_unlayout(g):
    return jnp.stack([g[B_NOPE * h:B_NOPE * (h + 1), LANES * h:LANES * (h + 1)].T for h in range(B_HEADS)], axis=1)


def _uv_layout(w):
    return _block_diag([w[:, h, :] for h in range(B_HEADS)])


def _uv_unlayout(g):
    return jnp.stack([g[LANES * h:LANES * (h + 1), B_V * h:B_V * (h + 1)] for h in range(B_HEADS)], axis=1)


def _rope_tables(S):
    inv = ROPE_THETA ** (-jnp.arange(0, 32, 2, dtype=F32) / 32)
    tok = jnp.arange(S)

    def tab(pos):
        ang = pos.astype(F32)[:, None] * inv[None, :]
        cos, sin = jnp.cos(ang), jnp.sin(ang)
        return jnp.concatenate([cos, cos], axis=1), jnp.concatenate([-sin, sin], axis=1)

    cr, sr = tab(tok // GRID_W)
    cc, sc = tab(tok % GRID_W)
    ct, st = tab(tok)
    return (jnp.tile(jnp.concatenate([cr, cc], axis=1), (1, 2)), jnp.tile(jnp.concatenate([sr, sc], axis=1), (1, 2)),
            jnp.tile(ct, (1, 4)), jnp.tile(st, (1, 4)))


A_TQ, A_TK, A_SUB = 512, 4096, 512
A_FWD_SUB = 1024
B_TQ, B_TK, B_SUB = 128, 4096, 1024
B_BWD_TK, B_BWD_SUB = 4096, 512
C_T = 256
C_BLOCKS_PER_STEP = 8
KV_SHARE = 2


def _local_step(x0, tgt, mod, norm_w, wie, wuq, wuk, wuv, late_shards, a_q_norm, a_k_norm, q_lora_norm, kv_lora_norm,
                c_sink, final_norm):
    S = x0.shape[0]
    mod3 = mod.reshape(2, 3, D_MODEL)
    ca, sa, ct, st = _rope_tables(S)
    lane_seg = np.arange(LANES) // HEAD_DIM
    seg = jnp.asarray((lane_seg[:, None] == lane_seg[None, :]).astype(np.float32)).astype(BF16)
    qn = jnp.tile(a_q_norm.reshape(1, HEAD_DIM), (1, 2))
    kn = jnp.tile(a_k_norm.reshape(1, HEAD_DIM), (1, 2))
    qln, kvln = q_lora_norm.reshape(1, B_Q_LORA), kv_lora_norm.reshape(1, B_KV_LORA)
    nw0, nw1 = norm_w[0:1], norm_w[1:2]
    gate0, gate1 = mod3[0, 2:3], mod3[1, 2:3]
    a_tq, a_tk, b_tq, b_tk, bb_tk, c_t = min(A_TQ, S), min(A_TK, S), min(B_TQ, S), min(B_TK, S), min(B_BWD_TK, S), min(C_T, S)
    a_sub, b_sub, bb_sub = min(A_SUB, a_tk), min(B_SUB, b_tk), min(B_BWD_SUB, bb_tk)

    h0, proj_e, qa, ka, va, qcat, kcat, ka_t, va_t, kcat_t = _even_pre_fwd(x0, mod3[0], nw0, wie, qn, kn, seg, ca, sa, ct, st,
                                                                           qln, kvln, wuq, wuk)
    oa, lse_a, woe_g, wio_g, woo_g = _pp_fwd(qa, ka, va_t, kdiv=KV_SHARE, tq=a_tq, tk=a_tk, sub=min(A_FWD_SUB, a_tk), name="attn_a_fwd",
                                             side=_gather_chip4_halves(late_shards))
    woe = woe_g.reshape(D_MODEL, D_MODEL)
    wio = wio_g.reshape(N_CHIPS, D_MODEL, ODD_IN // N_CHIPS)
    woo = woo_g.reshape(D_MODEL, D_MODEL)
    olat, lse_b = _mla_fwd(qcat, kcat, kcat_t, tq=b_tq, tk=b_tk, sub=b_sub)
    y0, x1 = _even_post_fwd(oa, olat, proj_e, x0, gate0, wuv, woe)
    h1, gc, qc, kc, vc, kc_t, vc_t = _odd_pre_fwd(x1, mod3[1], nw1, wio)
    slopes = 2.0 ** (-8.0 * jnp.arange(1, C_HEADS + 1, dtype=F32) / C_HEADS)
    slope_rows = jnp.repeat(slopes.reshape(C_HEADS // 2, 2), c_t, axis=1)[:, None, :]
    sink_rows = jnp.repeat(c_sink.reshape(C_HEADS // 2, 2), c_t, axis=1)[:, None, :]
    win_dist = _win_dist_table(S, c_t)
    oc, lse_c = _win_fwd(qc, kc, vc_t, win_dist, slope_rows, sink_rows, kdiv=KV_SHARE, tq=c_t, nbs=C_BLOCKS_PER_STEP,
                         name="attn_c_fwd")
    doc, dgc, dx2, dwoo, st_f = _odd_post(oc, gc, x1, gate1, woo, final_norm.reshape(1, D_MODEL), tgt)
    dqc, dkc, dvc, dsink_raw = _win_bwd(qc, kc, kc_t, vc, oc, doc, lse_c, win_dist, slope_rows, sink_rows, kdiv=KV_SHARE, tq=c_t,
                                        nbs=C_BLOCKS_PER_STEP, name="attn_c_bwd")
    dx1, dwio, st_1 = _odd_pre_bwd(dqc, dkc, dvc, dgc, h1, x1, dx2, mod3[1], nw1, wio)
    doa, dga, dgb, dolat, dwoe, dwuv, st_e = _even_post_bwd(dx1, y0, oa, olat, proj_e, gate0, wuv, woe)
    late_grads = _reduce_exchange([dwoe.reshape(N_CHIPS, D_MODEL // N_CHIPS, D_MODEL), dwio,
                                   dwoo.reshape(N_CHIPS, D_MODEL // N_CHIPS, D_MODEL)])
    dqa, dka, dva, p_woe, p_wio, p_woo = _pp_bwd(qa, ka, ka_t, va, oa, doa, lse_a, kdiv=KV_SHARE, tq=a_tq, tk=a_tk, sub=a_sub,
                                                 name="attn_a_bwd", side=late_grads)
    dqcat, dkcat = _mla_bwd(qcat, kcat, kcat_t, olat, dolat, lse_b, tq=b_tq, tk=bb_tk, sub=bb_sub)
    dx0, dwie, dwuq, dwuk, st_0, nst = _even_pre_bwd(x0, h0, proj_e, dqa, dka, dva, dga, dgb, dqcat, dkcat, dx1, mod3[0], nw0,
                                                     wie, qn, kn, seg, ca, sa, ct, st, qln, kvln, wuq, wuk)
    dsink_pairs = jnp.stack([dsink_raw[:, 0, 0], dsink_raw[:, 1, 0]], axis=1).reshape(C_HEADS)
    return dict(
        loss_row=st_f[2:3], dx=dx0,
        dmod=jnp.stack([jnp.concatenate([st_0[0], st_0[1], st_e[0]]), jnp.concatenate([st_1[0], st_1[1], st_f[1]])]),
        norm_w=jnp.stack([st_0[2], st_1[2]]), final_norm=st_f[0],
        a_q_norm=nst[0:1, 0:HEAD_DIM], a_k_norm=nst[1:2, 0:HEAD_DIM], b_q_lora_norm=nst[2:3, :], b_kv_lora_norm=nst[3:4, 0:LANES],
        c_sink=dsink_pairs.reshape(1, C_HEADS),
        even_w_in=dwie, b_w_uq=dwuq, b_w_uk=dwuk, b_w_uv=dwuv, even_w_out=p_woe, odd_w_in=p_wio, odd_w_out=p_woo)


WEIGHT_NAMES = ("norm_w", "ada_w", "ada_b", "even_w_in", "a_q_norm", "a_k_norm", "b_q_lora_norm", "b_kv_lora_norm", "b_w_uq",
                "b_w_uk", "b_w_uv", "even_w_out", "odd_w_in", "c_sink", "odd_w_out", "final_norm")


def kernel(x, c, norm_w, ada_w, ada_b, even_w_in, a_q_norm, a_k_norm, b_q_lora_norm, b_kv_lora_norm, b_w_uq, b_w_uk, b_w_uv, even_w_out, odd_w_in, c_sink, odd_w_out, final_norm, loss_target, m_norm_w, m_ada_w, m_ada_b, m_even_w_in, m_a_q_norm, m_a_k_norm, m_b_q_lora_norm, m_b_kv_lora_norm, m_b_w_uq, m_b_w_uk, m_b_w_uv, m_even_w_out, m_odd_w_in, m_c_sink, m_odd_w_out, m_final_norm, v_norm_w, v_ada_w, v_ada_b, v_even_w_in, v_a_q_norm, v_a_k_norm, v_b_q_lora_norm, v_b_kv_lora_norm, v_b_w_uq, v_b_w_uk, v_b_w_uv, v_even_w_out, v_odd_w_in, v_c_sink, v_odd_w_out, v_final_norm):
    given = dict(locals())
    xi, yi, ci = _my_place()
    chip = 2 * xi + yi
    dev = 2 * chip + ci
    n_ada = ada_w.shape[2]

    (c_all,) = _gather_dev8([c], "gather_c")
    c_all = c_all.reshape(N_DEV, D_MODEL)
    bias = lax.dynamic_slice_in_dim(ada_b, chip * n_ada, n_ada, axis=1).reshape(2, 1, n_ada)
    mod_cols = _ada_fwd(c_all, ada_w, bias)
    def halves(w):
        return w.astype(BF16).reshape((2, w.shape[0] // 2) + w.shape[1:])

    mod_all, wie_g, wuq_g = _gather_chip4_halves(
        [mod_cols, _shard_halves_t(even_w_in[0]), _shard_halves_t(b_w_uq[0])]).run("gather_weights")
    mod = jnp.transpose(lax.dynamic_index_in_dim(mod_all, dev, axis=2, keepdims=False), (1, 0, 2)).reshape(2, 3 * D_MODEL)

    res = _local_step(
        x[0], loss_target[0], mod, norm_w,
        _even_in_layout_t(wie_g), _uq_layout_t(wuq_g), _uk_layout(b_w_uk[0].astype(BF16)),
        _uv_layout(b_w_uv[0].astype(BF16)), [halves(even_w_out[0]), halves(odd_w_in[0]), halves(odd_w_out[0])],
        a_q_norm, a_k_norm, b_q_lora_norm, b_kv_lora_norm, c_sink, final_norm)

    latent = jnp.stack([_uk_unlayout(res["b_w_uk"]).reshape(B_KV_LORA, 512),
                        _uv_unlayout(res["b_w_uv"]).reshape(B_KV_LORA, 512)]).astype(BF16)
    p_wie, p_wuq, small_all, latent_all = _reduce_exchange(
        [_even_in_unlayout_t(res["even_w_in"]).reshape(N_CHIPS, EVEN_IN // N_CHIPS, D_MODEL),
         _uq_unlayout_t(res["b_w_uq"]).astype(BF16).reshape(N_CHIPS, -1, B_Q_LORA)],
        whole=[_pack_small(res), latent]).run("reduce_exchange")
    shard_parts = dict(even_w_in=p_wie, b_w_uq=p_wuq, **{k: res[k] for k in ("even_w_out", "odd_w_in", "odd_w_out")})
    dmod_all = small_all[:, 0:6, :].reshape(N_DEV, 2, 3 * D_MODEL)
    dmod_cols = jnp.transpose(lax.dynamic_slice_in_dim(dmod_all, chip * n_ada, n_ada, axis=2), (1, 0, 2))
    parts = dict(shard_parts)
    parts["ada_w"] = _ada_bwd(c_all.T, dmod_cols).reshape(1, 2 * D_MODEL, n_ada)
    parts["b_w_uk"], parts["b_w_uv"] = [latent_all[:, t].reshape(N_DEV, B_KV_LORA * B_HEADS, -1) for t in range(2)]

    def as2d(a):
        return a.reshape((-1, a.shape[-1]) if a.ndim > 1 else (1, a.shape[0]))

    results = {}
    small_outs = _adam_small(small_all, *[[as2d(given[pre + k]) for k in SMALL_WEIGHTS] for pre in ("", "m_", "v_")])
    for idx, k in enumerate(SMALL_WEIGHTS):
        results[k] = small_outs[4 * idx:4 * idx + 4]
    for k, p in parts.items():
        if k in ("even_w_in", "b_w_uq"):
            outs = _adam(p, given[k][0].T, given["m_" + k][0].T, given["v_" + k][0].T, "adam_" + k, by_columns=k == "even_w_in")
            results[k] = [o.T for o in outs]
            continue
        shape2 = (p.shape[-2], p.shape[-1])
        results[k] = _adam(p, given[k].reshape(shape2), given["m_" + k].reshape(shape2), given["v_" + k].reshape(shape2),
                           "adam_" + k)
    by_kind = [[results[k][t].reshape(given[k].shape) for k in WEIGHT_NAMES] for t in range(4)]
    return (small_outs[-1][0, 0], res["dx"][None], *by_kind[0], *by_kind[1], *by_kind[2], *by_kind[3])
```

```python
import functools

import numpy as np
import jax
import jax.numpy as jnp
from jax import lax
from jax.experimental import pallas as pl
from jax.experimental.pallas import tpu as pltpu

F32 = jnp.float32
BF16 = jnp.bfloat16
HIGHEST = lax.Precision.HIGHEST
MESH_ID = pl.DeviceIdType.MESH

D_MODEL = 1024
HEAD_DIM = 64
GRID_W = 64
EPS = 1e-6
ROPE_THETA = 10000.0
B_HEADS, B_NOPE, B_ROPE, B_V = 8, 64, 32, 64
B_Q_LORA, B_KV_LORA = 256, 128
C_HEADS = 16
WINDOW = 128
EVEN_IN, ODD_IN = 2208, 2560
EVEN_P = 2304
N_CHIPS, N_DEV = 4, 8
LANES = 128
DMA_ROWS = 8
NEG = -1e30
VMEM_LIMIT = 60 * 1024 * 1024

ADAM_LR, ADAM_B1, ADAM_B2, ADAM_EPS, ADAM_WD, ADAM_STEP = 0.001, 0.9, 0.999, 1e-08, 0.01, 10

ROW_TILE = 512
IN_PROJ_ROW_TILE = 256


def _dot(a, b):
    return lax.dot_general(a, b, (((1,), (0,)), ((), ())), preferred_element_type=F32)


def _dot_nt(a, b):
    return lax.dot_general(a, b, (((1,), (1,)), ((), ())), preferred_element_type=F32)


def _dot_tn(a, b):
    return lax.dot_general(a, b, (((0,), (0,)), ((), ())), preferred_element_type=F32)


def _dot_f32(a, b):
    return lax.dot_general(a, b, (((1,), (0,)), ((), ())), precision=HIGHEST, preferred_element_type=F32)


def _sigmoid(x):
    return 1.0 / (1.0 + jnp.exp(-x))


def _silu_and_grad(g):
    s = _sigmoid(g)
    return g * s, s * (1.0 + g * (1.0 - s))


def _lane_iota():
    return lax.broadcasted_iota(jnp.int32, (1, LANES), 1)


def _partner(x, lane):
    return jnp.where((lane % 32) < 16, pltpu.roll(x, LANES - 16, 1), pltpu.roll(x, 16, 1))


def _rot(x, cos, sin_signed, lane):
    return x * cos + _partner(x, lane) * sin_signed


def _rot_bwd(dy, cos, sin_signed, lane):
    return dy * cos + _partner(dy * sin_signed, lane)


def _rms(x):
    return lax.rsqrt(jnp.mean(x * x, axis=-1, keepdims=True) + EPS)


def _rms_bwd(x, r, g):
    return r * g - x * (r * r * r) * jnp.mean(x * g, axis=-1, keepdims=True)


def _seg_mean(v, seg_ones):
    hi = v.astype(BF16)
    lo = (v - hi.astype(F32)).astype(BF16)
    return (_dot(hi, seg_ones) + _dot(lo, seg_ones)) * (1.0 / HEAD_DIM)


def _dup_heads(x, lane):
    swapped = pltpu.roll(x, HEAD_DIM, 1)
    lo = lane < HEAD_DIM
    return jnp.concatenate([jnp.where(lo, x, swapped), jnp.where(lo, swapped, x)], axis=1)


def _fold_heads(x2, lane):
    a, b = x2[:, 0:LANES], x2[:, LANES:2 * LANES]
    return jnp.where(lane < HEAD_DIM, a + pltpu.roll(a, HEAD_DIM, 1), b + pltpu.roll(b, HEAD_DIM, 1))


def _row_spec(ts, cols):
    return pl.BlockSpec((ts, cols), lambda i: (i, 0))


def _full_spec(shape, single=True):
    nd = len(shape)
    if single:
        return pl.BlockSpec(shape, lambda i: (0,) * nd, pipeline_mode=pl.Buffered(1))
    return pl.BlockSpec(shape, lambda i: (0,) * nd)


def _sds(shape, dtype):
    return jax.ShapeDtypeStruct(shape, dtype)


def _params(sem):
    return pltpu.CompilerParams(dimension_semantics=sem, vmem_limit_bytes=VMEM_LIMIT)


def _even_pre_fwd(x, mod, nw, wie, qn, kn, seg, ca, sa, ct, st, qln, kvln, wuq, wuk):
    S = x.shape[0]
    ts = min(IN_PROJ_ROW_TILE, S)

    def body(x_ref, mod_ref, nw_ref, wie_ref, qn_ref, kn_ref, seg_ref, ca_ref, sa_ref, ct_ref, st_ref, qln_ref,
             kvln_ref, wuq_ref, wuk_ref, h_ref, proj_ref, qa_ref, ka_ref, va_ref, qcat_ref, kcat_ref, kat_ref, vat_ref, kcatt_ref):
        xv = x_ref[...]
        h = (xv * _rms(xv) * nw_ref[...]) * (1.0 + mod_ref[1:2, :]) + mod_ref[0:1, :]
        hb = h.astype(BF16)
        h_ref[...] = hb
        proj = _dot_nt(hb, wie_ref[...])
        proj_ref[...] = proj
        lane = _lane_iota()
        ca_v, sa_v, ct_v, st_v = ca_ref[...], sa_ref[...], ct_ref[...], st_ref[...]
        seg_v = seg_ref[...]
        for cb in range(4):
            xc = proj[:, LANES * cb:LANES * (cb + 1)]
            r = lax.rsqrt(_seg_mean(xc * xc, seg_v) + EPS)
            y = _rot(xc * r * qn_ref[...], ca_v, sa_v, lane)
            qa_ref[:, LANES * cb:LANES * (cb + 1)] = (y * 0.125).astype(BF16)
        kc = proj[:, 512:640]
        r = lax.rsqrt(_seg_mean(kc * kc, seg_v) + EPS)
        ka_v = _dup_heads(_rot(kc * r * kn_ref[...], ca_v, sa_v, lane), lane)
        ka_ref[...] = ka_v.astype(BF16)
        kat_ref[...] = ka_v.T.astype(BF16)
        va_v = _dup_heads(proj[:, 640:768], lane)
        va_ref[...] = va_v.astype(BF16)
        vat_ref[...] = va_v.T.astype(BF16)
        cq = proj[:, 1280:1536]
        cqn = (cq * _rms(cq) * qln_ref[...]).astype(BF16)
        ckv = proj[:, 1536:1664]
        ckvn = ckv * _rms(ckv) * kvln_ref[...]
        qb = _dot_nt(cqn, wuq_ref[...])
        qlat = _dot(qb[:, 0:512].astype(BF16), wuk_ref[...])
        for hh in range(B_HEADS):
            qcat_ref[hh, :, 0:LANES] = qlat[:, LANES * hh:LANES * (hh + 1)].astype(BF16)
            qr = _rot(qb[:, 512 + LANES * hh:512 + LANES * (hh + 1)], ct_v, st_v, lane)
            qcat_ref[hh, :, LANES:2 * LANES] = qr.astype(BF16)
        kr = _rot(proj[:, 1664:1792], ct_v, st_v, lane)
        kcat_ref[:, 0:LANES] = ckvn.astype(BF16)
        kcat_ref[:, LANES:2 * LANES] = kr.astype(BF16)
        kcatt_ref[0:LANES, :] = ckvn.T.astype(BF16)
        kcatt_ref[LANES:2 * LANES, :] = kr.T.astype(BF16)

    col_spec = lambda rows: pl.BlockSpec((rows, ts), lambda i: (0, i))
    return pl.pallas_call(
        body, name="even_pre_fwd", grid=(S // ts,),
        in_specs=[_row_spec(ts, D_MODEL), _full_spec((3, D_MODEL)), _full_spec((1, D_MODEL)), _full_spec((EVEN_P, D_MODEL)),
                  _full_spec((1, LANES)), _full_spec((1, LANES)), _full_spec((LANES, LANES)),
                  _row_spec(ts, LANES), _row_spec(ts, LANES), _row_spec(ts, LANES), _row_spec(ts, LANES),
                  _full_spec((1, B_Q_LORA)), _full_spec((1, B_KV_LORA)), _full_spec((1536, B_Q_LORA)), _full_spec((512, 1024))],
        out_specs=[_row_spec(ts, D_MODEL), _row_spec(ts, EVEN_P), _row_spec(ts, 512), _row_spec(ts, 2 * LANES), _row_spec(ts, 2 * LANES),
                   pl.BlockSpec((B_HEADS, ts, 2 * LANES), lambda i: (0, i, 0)), _row_spec(ts, 2 * LANES),
                   col_spec(2 * LANES), col_spec(2 * LANES), col_spec(2 * LANES)],
        out_shape=[_sds((S, D_MODEL), BF16), _sds((S, EVEN_P), F32), _sds((S, 512), BF16), _sds((S, 2 * LANES), BF16),
                   _sds((S, 2 * LANES), BF16), _sds((B_HEADS, S, 2 * LANES), BF16), _sds((S, 2 * LANES), BF16),
                   _sds((2 * LANES, S), BF16), _sds((2 * LANES, S), BF16), _sds((2 * LANES, S), BF16)],
        compiler_params=_params(("arbitrary",)),
    )(x, mod, nw, wie, qn, kn, seg, ca, sa, ct, st, qln, kvln, wuq, wuk)


MLA_SCALE = (B_NOPE + B_ROPE) ** -0.5
LOG2E = 1.4426950408889634

def _row_lo():
    return lax.broadcasted_iota(jnp.int32, (LANES, 1), 0) < HEAD_DIM


def _stack_cols(vT, rlo):
    zero = jnp.zeros_like(vT)
    return jnp.concatenate([jnp.where(rlo, vT, zero), jnp.where(rlo, zero, vT)], axis=1)


def _stack_rows(v, lo):
    zero = jnp.zeros_like(v)
    return jnp.concatenate([jnp.where(lo, v, zero), jnp.where(lo, zero, v)], axis=0)


def _pick_halves_T(xT, rlo, t):
    return jnp.where(rlo, xT[:, 0:t], xT[:, t:2 * t]).T


def _side_split(refs, n_in, n_out, n_scratch, side):
    ns = side.n if side is not None else 0
    cuts = np.cumsum([0, n_in, ns, n_out, ns, n_scratch])
    return [refs[a:b] for a, b in zip(cuts[:-1], cuts[1:])] + [refs[cuts[-1]:]]


def _side_hooks(side, side_ins, side_outs, side_sems, step, total):
    if side is None:
        return lambda: None
    start, mid, end = side.phases(side_ins, side_outs, side_sems)
    pl.when(step == 0)(start)
    pl.when(step == total // 2)(mid)
    return lambda: pl.when(step == total - 1)(end)


def _side_specs(side):
    if side is None:
        return [], [], [], [], []
    return list(side.arrs), [_ANY] * side.n, [_ANY] * side.n, list(side.out_shapes), side.sem_shapes()


def _pp_fwd(q, k, vT, *, kdiv, tq, tk, sub, name, side=None):
    S = k.shape[0]; nb = q.shape[1] // LANES; nq = S // tq; nkv = S // tk; nsub = tk // sub

    def body(*refs):
        (q_ref, k_ref, vT_ref), side_ins, (o_ref, lse_ref), side_outs, (qs, m_s, l_s, acc), side_sems = _side_split(refs, 3, 2, 4, side)
        j = pl.program_id(2)
        rlo = _row_lo()
        step = (pl.program_id(0) * nq + pl.program_id(1)) * nkv + j
        side_end = _side_hooks(side, side_ins, side_outs, side_sems, step, nb * nq * nkv)

        @pl.when(j == 0)
        def _():
            qs[...] = _stack_cols(q_ref[...].astype(F32).T, rlo).astype(BF16)
            m_s[...] = jnp.full((1, 2 * tq), NEG, F32)
            l_s[...] = jnp.zeros((1, 2 * tq), F32)
            acc[...] = jnp.zeros((LANES, 2 * tq), F32)

        qsv = qs[...]
        m, l, a = m_s[...], l_s[...], acc[...]
        s_cur = _dot(k_ref[0:sub, :], qsv)
        for t in range(nsub):
            if t + 1 < nsub:
                s_next = _dot(k_ref[sub * (t + 1):sub * (t + 2), :], qsv)
            m_new = jnp.maximum(m, jnp.max(s_cur, axis=0, keepdims=True))
            alpha = jnp.exp(m - m_new)
            p = jnp.exp(s_cur - m_new)
            l = alpha * l + jnp.sum(p, axis=0, keepdims=True)
            a = alpha * a + _dot(vT_ref[:, sub * t:sub * (t + 1)], p.astype(BF16))
            m = m_new
            if t + 1 < nsub:
                s_cur = s_next
        m_s[...], l_s[...], acc[...] = m, l, a

        @pl.when(j == nkv - 1)
        def _():
            l_f = l_s[...]
            o_ref[...] = _pick_halves_T(acc[...] / l_f, rlo, tq).astype(BF16)
            lse_ref[0, 0] = m_s[...] + jnp.log(l_f)

        side_end()

    s_args, s_in, s_out, s_shapes, s_sems = _side_specs(side)
    return pl.pallas_call(
        body, name=name, grid=(nb, nq, nkv),
        in_specs=[pl.BlockSpec((tq, LANES), lambda b, i, j: (i, b)), pl.BlockSpec((tk, LANES), lambda b, i, j: (j, b // kdiv)),
                  pl.BlockSpec((LANES, tk), lambda b, i, j: (b // kdiv, j))] + s_in,
        out_specs=[pl.BlockSpec((tq, LANES), lambda b, i, j: (i, b)),
                   pl.BlockSpec((1, 1, 1, 2 * tq), lambda b, i, j: (b, i, 0, 0))] + s_out,
        out_shape=[_sds((S, nb * LANES), BF16), _sds((nb, nq, 1, 2 * tq), F32)] + s_shapes,
        scratch_shapes=[pltpu.VMEM((LANES, 2 * tq), BF16), pltpu.VMEM((1, 2 * tq), F32), pltpu.VMEM((1, 2 * tq), F32),
                        pltpu.VMEM((LANES, 2 * tq), F32)] + s_sems,
        compiler_params=_params(("arbitrary",) * 3))(q, k, vT, *s_args)


def _pp_bwd(q, k, kT, v, o, do, lse, *, kdiv, tq, tk, sub, name, side=None):
    S = k.shape[0]; nb = q.shape[1] // LANES; nkb = k.shape[1] // LANES; nq = S // tq; nkv = S // tk; nsub = tk // sub

    def body(*refs):
        ((q_ref, k_ref, kT_ref, v_ref, o_ref, do_ref, lse_ref), side_ins, (dq_ref, dk_ref, dv_ref), side_outs,
         (qsT, qs, dosT, dos, delta_s, dq_acc), side_sems) = _side_split(refs, 7, 3, 6, side)
        b, i, j = pl.program_id(0), pl.program_id(1), pl.program_id(2)
        rlo = _row_lo()
        lo = lax.broadcasted_iota(jnp.int32, (1, LANES), 1) < HEAD_DIM
        side_end = _side_hooks(side, side_ins, side_outs, side_sems, (b * nq + i) * nkv + j, nb * nq * nkv)

        @pl.when((b % kdiv == 0) & (i == 0) & (j == 0))
        def _():
            dk_ref[...] = jnp.zeros((S, LANES), F32)
            dv_ref[...] = jnp.zeros((S, LANES), F32)

        @pl.when(j == 0)
        def _():
            qv = q_ref[...]
            qs[...] = _stack_rows(qv, lo)
            qsT[...] = _stack_cols(qv.astype(F32).T, rlo).astype(BF16)
            dov = do_ref[...].astype(F32)
            dos[...] = _stack_rows(dov.astype(BF16), lo)
            dosT[...] = _stack_cols(dov.T, rlo).astype(BF16)
            prodT = (dov * o_ref[...].astype(F32)).T
            delta_s[...] = jnp.concatenate([jnp.sum(jnp.where(rlo, prodT, 0.0), axis=0, keepdims=True),
                                            jnp.sum(jnp.where(rlo, 0.0, prodT), axis=0, keepdims=True)], axis=1)
            dq_acc[...] = jnp.zeros((LANES, 2 * tq), F32)

        qsTv, dosTv, qsv, dosv = qsT[...], dosT[...], qs[...], dos[...]
        lse_v, delta_v = lse_ref[0, 0], delta_s[...]
        dqa = dq_acc[...]
        s_cur = _dot(k_ref[0:sub, :], qsTv)
        dp_cur = _dot(v_ref[0:sub, :], dosTv)
        for t in range(nsub):
            if t + 1 < nsub:
                s_next = _dot(k_ref[sub * (t + 1):sub * (t + 2), :], qsTv)
                dp_next = _dot(v_ref[sub * (t + 1):sub * (t + 2), :], dosTv)
            p = jnp.exp(s_cur - lse_v)
            ds = (p * (dp_cur - delta_v)).astype(BF16)
            rows = pl.ds(pl.multiple_of(j * tk + sub * t, sub), sub)
            dv_ref[rows, :] += _dot(p.astype(BF16), dosv)
            dk_ref[rows, :] += _dot(ds, qsv)
            dqa = dqa + _dot(kT_ref[:, sub * t:sub * (t + 1)], ds)
            if t + 1 < nsub:
                s_cur, dp_cur = s_next, dp_next
        dq_acc[...] = dqa

        @pl.when(j == nkv - 1)
        def _():
            dq_ref[...] = _pick_halves_T(dq_acc[...], rlo, tq)

        side_end()

    qmap = lambda b, i, j: (i, b)
    kmap = lambda b, i, j: (j, b // kdiv)
    res = lambda b, i, j: (0, b // kdiv)
    s_args, s_in, s_out, s_shapes, s_sems = _side_specs(side)
    return pl.pallas_call(
        body, name=name, grid=(nb, nq, nkv),
        in_specs=[pl.BlockSpec((tq, LANES), qmap), pl.BlockSpec((tk, LANES), kmap), pl.BlockSpec((LANES, tk), lambda b, i, j: (b // kdiv, j)),
                  pl.BlockSpec((tk, LANES), kmap), pl.BlockSpec((tq, LANES), qmap), pl.BlockSpec((tq, LANES), qmap),
                  pl.BlockSpec((1, 1, 1, 2 * tq), lambda b, i, j: (b, i, 0, 0))] + s_in,
        out_specs=[pl.BlockSpec((tq, LANES), qmap), pl.BlockSpec((S, LANES), res), pl.BlockSpec((S, LANES), res)] + s_out,
        out_shape=[_sds((S, nb * LANES), F32), _sds((S, nkb * LANES), F32), _sds((S, nkb * LANES), F32)] + s_shapes,
        scratch_shapes=[pltpu.VMEM((LANES, 2 * tq), BF16), pltpu.VMEM((2 * tq, LANES), BF16), pltpu.VMEM((LANES, 2 * tq), BF16),
                        pltpu.VMEM((2 * tq, LANES), BF16), pltpu.VMEM((1, 2 * tq), F32), pltpu.VMEM((LANES, 2 * tq), F32)] + s_sems,
        compiler_params=_params(("arbitrary",) * 3))(q, k, kT, v, o, do, lse, *s_args)


MLA_C = MLA_SCALE * LOG2E


def _mla_fwd(q, kcat, kcatT, *, tq, tk, sub):
    S = kcat.shape[0]; nq, nkv = S // tq, S // tk; R = B_HEADS * tq; nsub = tk // sub

    def body(q_ref, k_ref, vT_ref, o_ref, lse_ref, qT, m_s, l_s, acc):
        j = pl.program_id(1)

        @pl.when(j == 0)
        def _():
            qT[...] = q_ref[...].reshape(R, 2 * LANES).astype(F32).T.astype(BF16)
            m_s[...] = jnp.full((1, R), NEG, F32)
            l_s[...] = jnp.zeros((1, R), F32)
            acc[...] = jnp.zeros((LANES, R), F32)

        qTv = qT[...]
        m, l, a = m_s[...], l_s[...], acc[...]
        s_cur = _dot(k_ref[0:sub, :], qTv)
        for t in range(nsub):
            if t + 1 < nsub:
                s_next = _dot(k_ref[sub * (t + 1):sub * (t + 2), :], qTv)
            m_new = jnp.maximum(m, jnp.max(s_cur, axis=0, keepdims=True))
            alpha = jnp.exp2((m - m_new) * MLA_C)
            p = jnp.exp2((s_cur - m_new) * MLA_C)
            l = alpha * l + jnp.sum(p, axis=0, keepdims=True)
            a = alpha * a + _dot(vT_ref[:, sub * t:sub * (t + 1)], p.astype(BF16))
            m = m_new
            if t + 1 < nsub:
                s_cur = s_next
        m_s[...], l_s[...], acc[...] = m, l, a

        @pl.when(j == nkv - 1)
        def _():
            l_f = l_s[...]
            o_ref[...] = (acc[...] / l_f).T.reshape(B_HEADS, tq, LANES).astype(BF16)
            lse_ref[0] = m_s[...] * MLA_SCALE + jnp.log(l_f)

    return pl.pallas_call(
        body, name="mla_fwd", grid=(nq, nkv),
        in_specs=[pl.BlockSpec((B_HEADS, tq, 2 * LANES), lambda i, j: (0, i, 0)), pl.BlockSpec((tk, 2 * LANES), lambda i, j: (j, 0)),
                  pl.BlockSpec((LANES, tk), lambda i, j: (0, j))],
        out_specs=[pl.BlockSpec((B_HEADS, tq, LANES), lambda i, j: (0, i, 0)), pl.BlockSpec((1, 1, R), lambda i, j: (i, 0, 0))],
        out_shape=[_sds((B_HEADS, S, LANES), BF16), _sds((nq, 1, R), F32)],
        scratch_shapes=[pltpu.VMEM((2 * LANES, R), BF16), pltpu.VMEM((1, R), F32), pltpu.VMEM((1, R), F32), pltpu.VMEM((LANES, R), F32)],
        compiler_params=_params(("arbitrary", "arbitrary")))(q, kcat, kcatT)


def _mla_bwd(q, kcat, kcatT, o, do, lse, *, tq, tk, sub):
    S = kcat.shape[0]; nq, nkv = S // tq, S // tk; R = B_HEADS * tq; nsub = tk // sub

    def body(q_ref, k_ref, kT_ref, o_ref, do_ref, lse_ref, dq_ref, dk_ref, qT, dosT, dos, delta_s, dq_acc):
        i, j = pl.program_id(0), pl.program_id(1)

        @pl.when((i == 0) & (j == 0))
        def _():
            dk_ref[...] = jnp.zeros((S, 2 * LANES), F32)

        @pl.when(j == 0)
        def _():
            qT[...] = q_ref[...].reshape(R, 2 * LANES).astype(F32).T.astype(BF16)
            dov = do_ref[...].reshape(R, LANES).astype(F32)
            dos[...] = dov.astype(BF16)
            dosT[...] = dov.T.astype(BF16)
            delta_s[...] = jnp.sum((dov * o_ref[...].reshape(R, LANES).astype(F32)).T, axis=0, keepdims=True)
            dq_acc[...] = jnp.zeros((2 * LANES, R), F32)

        qTv, dosTv, dosv = qT[...], dosT[...], dos[...]
        qv = q_ref[...].reshape(R, 2 * LANES)
        lse_v, delta_v = lse_ref[0] * LOG2E, delta_s[...]
        dqa = dq_acc[...]
        s_cur = _dot(k_ref[0:sub, :], qTv)
        dp_cur = _dot(k_ref[0:sub, 0:LANES], dosTv)
        for t in range(nsub):
            if t + 1 < nsub:
                s_next = _dot(k_ref[sub * (t + 1):sub * (t + 2), :], qTv)
                dp_next = _dot(k_ref[sub * (t + 1):sub * (t + 2), 0:LANES], dosTv)
            p = jnp.exp2(s_cur * MLA_C - lse_v)
            ds = (p * (dp_cur - delta_v) * MLA_SCALE).astype(BF16)
            rows = pl.ds(pl.multiple_of(j * tk + sub * t, sub), sub)
            dk_ref[rows, :] += _dot(ds, qv)
            dk_ref[rows, 0:LANES] += _dot(p.astype(BF16), dosv)
            dqa = dqa + _dot(kT_ref[:, sub * t:sub * (t + 1)], ds)
            if t + 1 < nsub:
                s_cur, dp_cur = s_next, dp_next
        dq_acc[...] = dqa

        @pl.when(j == nkv - 1)
        def _():
            dq_ref[...] = dq_acc[...].T.reshape(B_HEADS, tq, 2 * LANES)

    hspec = lambda w: pl.BlockSpec((B_HEADS, tq, w), lambda i, j: (0, i, 0))
    return pl.pallas_call(
        body, name="mla_bwd", grid=(nq, nkv),
        in_specs=[hspec(2 * LANES), pl.BlockSpec((tk, 2 * LANES), lambda i, j: (j, 0)), pl.BlockSpec((2 * LANES, tk), lambda i, j: (0, j)),
                  hspec(LANES), hspec(LANES), pl.BlockSpec((1, 1, R), lambda i, j: (i, 0, 0))],
        out_specs=[hspec(2 * LANES), pl.BlockSpec((S, 2 * LANES), lambda i, j: (0, 0))],
        out_shape=[_sds((B_HEADS, S, 2 * LANES), F32), _sds((S, 2 * LANES), F32)],
        scratch_shapes=[pltpu.VMEM((2 * LANES, R), BF16), pltpu.VMEM((LANES, R), BF16), pltpu.VMEM((R, LANES), BF16),
                        pltpu.VMEM((1, R), F32), pltpu.VMEM((2 * LANES, R), F32)],
        compiler_params=_params(("arbitrary", "arbitrary")))(q, kcat, kcatT, o, do, lse)


def _win_start(i, tq, nk, S):
    return pl.multiple_of(jnp.clip(i * tq - WINDOW, 0, S - nk), LANES)


def _win_dist_table(S, tq):
    nk = min(tq + 2 * WINDOW, S)
    nq = S // tq
    r = np.arange(nk)[:, None]
    c = (np.arange(2 * tq) % tq)[None, :]
    tabs = []
    for rel in (0, WINDOW, (nq - 1) * tq - (S - nk)):
        dist = np.abs(rel + c - r).astype(np.float32)
        tabs.append(np.where(dist <= WINDOW, dist, np.float32(1e32)))
    return jnp.asarray(np.stack(tabs))


def _win_dist_spec(nk, tq, nq):
    return pl.BlockSpec((1, nk, 2 * tq), lambda b, i: (jnp.where(i == 0, 0, jnp.where(i == nq - 1, 2, 1)), 0, 0))


def _win_fwd(q, k, vT, dist, slope, sink, *, kdiv, tq, nbs, name):
    S = k.shape[0]; nb = q.shape[1] // LANES; nq = S // tq; nk = min(tq + 2 * WINDOW, S)
    assert nb % nbs == 0 and nbs % kdiv == 0
    kvw = (nbs // kdiv) * LANES

    def body(q_ref, k_ref, vT_ref, dist_ref, slope_ref, sink_ref, o_ref, lse_ref):
        i = pl.program_id(1)
        rlo = _row_lo()
        k0 = _win_start(i, tq, nk, S)
        kk, vv, dd = k_ref[pl.ds(k0, nk), :], vT_ref[:, pl.ds(k0, nk)], dist_ref[0]
        for u in range(nbs):
            kv = slice(LANES * (u // kdiv), LANES * (u // kdiv + 1))
            qsT = _stack_cols(q_ref[:, LANES * u:LANES * (u + 1)].astype(F32).T, rlo).astype(BF16)
            s = _dot(kk[:, kv], qsT) - slope_ref[u] * dd
            sk = sink_ref[u]
            m = jnp.maximum(jnp.max(s, axis=0, keepdims=True), sk)
            p = jnp.exp(s - m)
            l = jnp.sum(p, axis=0, keepdims=True) + jnp.exp(sk - m)
            o_ref[:, LANES * u:LANES * (u + 1)] = _pick_halves_T(_dot(vv[kv, :], p.astype(BF16)) / l, rlo, tq).astype(BF16)
            lse_ref[u, 0] = m + jnp.log(l)

    row_spec = pl.BlockSpec((nbs, 1, 2 * tq), lambda b, i: (b, 0, 0))
    return pl.pallas_call(
        body, name=name, grid=(nb // nbs, nq),
        in_specs=[pl.BlockSpec((tq, nbs * LANES), lambda b, i: (i, b)), pl.BlockSpec((S, kvw), lambda b, i: (0, b)),
                  pl.BlockSpec((kvw, S), lambda b, i: (b, 0)), _win_dist_spec(nk, tq, nq), row_spec, row_spec],
        out_specs=[pl.BlockSpec((tq, nbs * LANES), lambda b, i: (i, b)), pl.BlockSpec((nbs, 1, 1, 2 * tq), lambda b, i: (b, i, 0, 0))],
        out_shape=[_sds((S, nb * LANES), BF16), _sds((nb, nq, 1, 2 * tq), F32)],
        compiler_params=_params(("arbitrary", "arbitrary")))(q, k, vT, dist, slope, sink)


def _win_bwd(q, k, kT, v, o, do, lse, dist, slope, sink, *, kdiv, tq, nbs, name):
    S = k.shape[0]; nb = q.shape[1] // LANES; nkb = k.shape[1] // LANES; nq = S // tq; nk = min(tq + 2 * WINDOW, S)
    assert nb % nbs == 0 and nbs % kdiv == 0
    nkv = nbs // kdiv
    kvw = nkv * LANES

    def body(q_ref, k_ref, kT_ref, v_ref, o_ref, do_ref, lse_ref, dist_ref, slope_ref, sink_ref, dq_ref, dk_ref, dv_ref, dsink_ref, ds_acc):
        i = pl.program_id(1)
        rlo = _row_lo()
        lo = lax.broadcasted_iota(jnp.int32, (1, LANES), 1) < HEAD_DIM

        @pl.when(i == 0)
        def _():
            dk_ref[...] = jnp.zeros((S, kvw), F32)
            dv_ref[...] = jnp.zeros((S, kvw), F32)
            ds_acc[...] = jnp.zeros((nbs, 2 * tq), F32)

        k0 = _win_start(i, tq, nk, S)
        rows = pl.ds(k0, nk)
        kk_all, vv_all, kkT_all, dd = k_ref[rows, :], v_ref[rows, :], kT_ref[:, rows], dist_ref[0]
        dv_sum, dk_sum = [None] * nkv, [None] * nkv
        for u in range(nbs):
            g = u // kdiv
            kv = slice(LANES * g, LANES * (g + 1))
            kk, vv, kkT = kk_all[:, kv], vv_all[:, kv], kkT_all[kv, :]
            cols = slice(LANES * u, LANES * (u + 1))
            qv = q_ref[:, cols]
            qs = _stack_rows(qv, lo)
            qsT = _stack_cols(qv.astype(F32).T, rlo).astype(BF16)
            dov = do_ref[:, cols].astype(F32)
            dos = _stack_rows(dov.astype(BF16), lo)
            dosT = _stack_cols(dov.T, rlo).astype(BF16)
            prodT = (dov * o_ref[:, cols].astype(F32)).T
            delta = jnp.concatenate([jnp.sum(jnp.where(rlo, prodT, 0.0), axis=0, keepdims=True),
                                     jnp.sum(jnp.where(rlo, 0.0, prodT), axis=0, keepdims=True)], axis=1)
            lse_v = lse_ref[u, 0]
            ds_acc[u:u + 1, :] += -jnp.exp(sink_ref[u] - lse_v) * delta
            p = jnp.exp(_dot(kk, qsT) - slope_ref[u] * dd - lse_v)
            ds = (p * (_dot(vv, dosT) - delta)).astype(BF16)
            dv_u, dk_u = _dot(p.astype(BF16), dos), _dot(ds, qs)
            dv_sum[g] = dv_u if dv_sum[g] is None else dv_sum[g] + dv_u
            dk_sum[g] = dk_u if dk_sum[g] is None else dk_sum[g] + dk_u
            dq_ref[:, cols] = (_pick_halves_T(_dot(kkT, ds), rlo, tq) * 0.125).astype(BF16)
        dv_ref[rows, :] += jnp.concatenate(dv_sum, axis=1)
        dk_ref[rows, :] += jnp.concatenate(dk_sum, axis=1)

        @pl.when(i == nq - 1)
        def _():
            acc = ds_acc[...]
            for u in range(nbs):
                dsink_ref[u] = jnp.concatenate(
                    [jnp.broadcast_to(jnp.sum(acc[u:u + 1, 0:tq], axis=1, keepdims=True), (1, LANES)),
                     jnp.broadcast_to(jnp.sum(acc[u:u + 1, tq:2 * tq], axis=1, keepdims=True), (1, LANES)),
                     jnp.zeros((6, LANES), F32)], axis=0)

    qmap = lambda b, i: (i, b)
    kv_spec = pl.BlockSpec((S, kvw), lambda b, i: (0, b))
    row_spec = pl.BlockSpec((nbs, 1, 2 * tq), lambda b, i: (b, 0, 0))
    wide = pl.BlockSpec((tq, nbs * LANES), qmap)
    return pl.pallas_call(
        body, name=name, grid=(nb // nbs, nq),
        in_specs=[wide, kv_spec, pl.BlockSpec((kvw, S), lambda b, i: (b, 0)), kv_spec, wide, wide,
                  pl.BlockSpec((nbs, 1, 1, 2 * tq), lambda b, i: (b, i, 0, 0)), _win_dist_spec(nk, tq, nq), row_spec, row_spec],
        out_specs=[wide, kv_spec, kv_spec, pl.BlockSpec((nbs, 8, LANES), lambda b, i: (b, 0, 0))],
        out_shape=[_sds((S, nb * LANES), BF16), _sds((S, nkb * LANES), F32), _sds((S, nkb * LANES), F32), _sds((nb, 8, LANES), F32)],
        scratch_shapes=[pltpu.VMEM((nbs, 2 * tq), F32)],
        compiler_params=_params(("arbitrary", "arbitrary")))(q, k, kT, v, o, do, lse, dist, slope, sink)


def _sum_rows(v):
    return jnp.sum(v, axis=0, keepdims=True)


def _norm_mod_bwd(dh, xv, mod_ref, nw_ref, stats_ref):
    r = _rms(xv)
    xn = xv * r
    nw = nw_ref[...]
    stats_ref[0:1, :] += _sum_rows(dh)
    stats_ref[1:2, :] += _sum_rows(dh * (xn * nw))
    dn = dh * (1.0 + mod_ref[1:2, :])
    stats_ref[2:3, :] += _sum_rows(dn * xn)
    return _rms_bwd(xv, r, dn * nw)


def _even_gate_specs(ts):
    return [pl.BlockSpec((ts, 256), lambda i, c=c: (i, c)) for c in (3, 4, 7, 8)]


def _even_post_fwd(oa, olat, proj, x, gate, wuv, woe):
    S = x.shape[0]
    ts = min(ROW_TILE, S)

    def body(oa_ref, ol_ref, ga0_ref, ga1_ref, gb0_ref, gb1_ref, x_ref, gate_ref, wuv_ref, woe_ref, y_ref, x1_ref):
        sa, _ = _silu_and_grad(jnp.concatenate([ga0_ref[...], ga1_ref[...]], axis=1))
        sb, _ = _silu_and_grad(jnp.concatenate([gb0_ref[...], gb1_ref[...]], axis=1))
        olc = jnp.concatenate([ol_ref[hh] for hh in range(B_HEADS)], axis=1).astype(BF16)
        ob = _dot(olc, wuv_ref[...])
        mix = jnp.concatenate([oa_ref[...] * sa, ob * sb], axis=1).astype(BF16)
        y = _dot(mix, woe_ref[...])
        y_ref[...] = y.astype(BF16)
        x1_ref[...] = x_ref[...] + gate_ref[...] * y

    return pl.pallas_call(
        body, name="even_post_fwd", grid=(S // ts,),
        in_specs=[_row_spec(ts, 512), pl.BlockSpec((B_HEADS, ts, LANES), lambda i: (0, i, 0))] + _even_gate_specs(ts) +
                 [_row_spec(ts, D_MODEL), _full_spec((1, D_MODEL)), _full_spec((1024, 512)), _full_spec((1024, D_MODEL))],
        out_specs=[_row_spec(ts, D_MODEL), _row_spec(ts, D_MODEL)],
        out_shape=[_sds((S, D_MODEL), BF16), _sds((S, D_MODEL), F32)],
        compiler_params=_params(("arbitrary",)),
    )(oa, olat, proj, proj, proj, proj, x, gate, wuv, woe)


def _odd_pre_fwd(x, mod, nw, wio):
    S = x.shape[0]
    ts = min(ROW_TILE, S)

    def body(x_ref, mod_ref, nw_ref, wio_ref, h_ref, g_ref, q_ref, k_ref, v_ref, kt_ref, vt_ref):
        xv = x_ref[...]
        h = (xv * _rms(xv) * nw_ref[...]) * (1.0 + mod_ref[1:2, :]) + mod_ref[0:1, :]
        hb = h.astype(BF16)
        h_ref[...] = hb
        proj = jnp.concatenate([_dot(hb, wio_ref[p]) for p in range(N_CHIPS)], axis=1)
        g_ref[...] = proj[:, 1536:2560]
        q_ref[...] = (proj[:, 0:1024] * 0.125).astype(BF16)
        lane = _lane_iota()
        k_v = jnp.concatenate([_dup_heads(proj[:, 1024 + LANES * j:1024 + LANES * (j + 1)], lane) for j in range(2)], axis=1)
        v_v = jnp.concatenate([_dup_heads(proj[:, 1280 + LANES * j:1280 + LANES * (j + 1)], lane) for j in range(2)], axis=1)
        k_ref[...] = k_v.astype(BF16)
        v_ref[...] = v_v.astype(BF16)
        kt_ref[...] = k_v.T.astype(BF16)
        vt_ref[...] = v_v.T.astype(BF16)

    col_spec = pl.BlockSpec((512, ts), lambda i: (0, i))
    return pl.pallas_call(
        body, name="odd_pre_fwd", grid=(S // ts,),
        in_specs=[_row_spec(ts, D_MODEL), _full_spec((3, D_MODEL)), _full_spec((1, D_MODEL)),
                  _full_spec((N_CHIPS, D_MODEL, ODD_IN // N_CHIPS))],
        out_specs=[_row_spec(ts, D_MODEL), _row_spec(ts, 1024), _row_spec(ts, 1024), _row_spec(ts, 512), _row_spec(ts, 512),
                   col_spec, col_spec],
        out_shape=[_sds((S, D_MODEL), BF16), _sds((S, 1024), F32), _sds((S, 1024), BF16), _sds((S, 512), BF16),
                   _sds((S, 512), BF16), _sds((512, S), BF16), _sds((512, S), BF16)],
        compiler_params=_params(("arbitrary",)),
    )(x, mod, nw, wio)


def _odd_post(oc, g, x1, gate, woo, fw, tgt):
    S = x1.shape[0]
    ts = min(ROW_TILE, S)
    nsteps = S // ts

    def body(oc_ref, g_ref, x_ref, gate_ref, woo_ref, fw_ref, tgt_ref, doc_ref, dgc_ref, dx2_ref, dwoo_out, stats_ref, dwoo_ref):
        @pl.when(pl.program_id(0) == 0)
        def _():
            dwoo_ref[...] = jnp.zeros((D_MODEL, D_MODEL), F32)
            stats_ref[...] = jnp.zeros((8, D_MODEL), F32)

        ocv = oc_ref[...]
        sg, dsg = _silu_and_grad(g_ref[...])
        mix = (ocv * sg).astype(BF16)
        woo_v = woo_ref[...]
        y = _dot(mix, woo_v)
        gate_v = gate_ref[...]
        x2 = x_ref[...] + gate_v * y
        r = _rms(x2)
        fw_v = fw_ref[...]
        xn = x2 * r
        err = xn * fw_v - tgt_ref[...]
        dout = err * (1.0 / D_MODEL)
        dx2 = _rms_bwd(x2, r, dout * fw_v)
        dx2_ref[...] = dx2
        stats_ref[0:1, :] += _sum_rows(dout * xn)
        stats_ref[1:2, :] += _sum_rows(dx2 * y)
        loss_t = 0.5 * jnp.sum(_sum_rows(err * dout), axis=-1, keepdims=True)
        stats_ref[2:3, :] += jnp.broadcast_to(loss_t, (1, D_MODEL))
        dy = (gate_v * dx2).astype(BF16)
        dmix = _dot_nt(dy, woo_v)
        dwoo_ref[...] += _dot_tn(mix, dy)
        doc_ref[...] = (dmix * sg).astype(BF16)
        dgc_ref[...] = (dmix * ocv * dsg).astype(BF16)

        @pl.when(pl.program_id(0) == nsteps - 1)
        def _():
            dwoo_out[...] = dwoo_ref[...].astype(BF16)

    return pl.pallas_call(
        body, name="odd_post", grid=(nsteps,),
        in_specs=[_row_spec(ts, D_MODEL), _row_spec(ts, D_MODEL), _row_spec(ts, D_MODEL), _full_spec((1, D_MODEL)),
                  _full_spec((D_MODEL, D_MODEL)), _full_spec((1, D_MODEL)), _row_spec(ts, D_MODEL)],
        out_specs=[_row_spec(ts, D_MODEL), _row_spec(ts, D_MODEL), _row_spec(ts, D_MODEL),
                   _full_spec((D_MODEL, D_MODEL), single=False), _full_spec((8, D_MODEL), single=False)],
        out_shape=[_sds((S, D_MODEL), BF16), _sds((S, D_MODEL), BF16), _sds((S, D_MODEL), F32), _sds((D_MODEL, D_MODEL), BF16),
                   _sds((8, D_MODEL), F32)],
        scratch_shapes=[pltpu.VMEM((D_MODEL, D_MODEL), F32)],
        compiler_params=_params(("arbitrary",)),
    )(oc, g, x1, gate, woo, fw, tgt)


def _odd_pre_bwd(dq, dk, dv, dgc, h, x, dx_res, mod, nw, wio):
    S = x.shape[0]
    ts = min(IN_PROJ_ROW_TILE, S)
    nsteps = S // ts
    wsh = ODD_IN // N_CHIPS

    def body(dq_ref, dk_ref, dv_ref, dgc_ref, h_ref, x_ref, dxr_ref, mod_ref, nw_ref, wio_ref, dx_ref, dw_ref, stats_ref, dw_acc):
        @pl.when(pl.program_id(0) == 0)
        def _():
            dw_acc[...] = jnp.zeros((N_CHIPS, D_MODEL, wsh), F32)
            stats_ref[...] = jnp.zeros((8, D_MODEL), F32)

        lane = _lane_iota()
        dkv = [_fold_heads(r[:, 2 * LANES * j:2 * LANES * (j + 1)], lane).astype(BF16) for r in (dk_ref, dv_ref) for j in range(2)]
        dproj = jnp.concatenate([dq_ref[...]] + dkv + [dgc_ref[...]], axis=1)
        hv = h_ref[...]
        dh = None
        for p in range(N_CHIPS):
            dp_cols = dproj[:, wsh * p:wsh * (p + 1)]
            part = _dot_nt(dp_cols, wio_ref[p])
            dh = part if dh is None else dh + part
            dw_acc[p] += _dot_tn(hv, dp_cols)
        dx_ref[...] = dxr_ref[...] + _norm_mod_bwd(dh, x_ref[...], mod_ref, nw_ref, stats_ref)

        @pl.when(pl.program_id(0) == nsteps - 1)
        def _():
            dw_ref[...] = dw_acc[...].astype(BF16)

    return pl.pallas_call(
        body, name="odd_pre_bwd", grid=(nsteps,),
        in_specs=[_row_spec(ts, 1024), _row_spec(ts, 512), _row_spec(ts, 512), _row_spec(ts, 1024), _row_spec(ts, D_MODEL),
                  _row_spec(ts, D_MODEL), _row_spec(ts, D_MODEL), _full_spec((3, D_MODEL)), _full_spec((1, D_MODEL)),
                  _full_spec((N_CHIPS, D_MODEL, wsh))],
        out_specs=[_row_spec(ts, D_MODEL), _full_spec((N_CHIPS, D_MODEL, wsh), single=False), _full_spec((8, D_MODEL), single=False)],
        out_shape=[_sds((S, D_MODEL), F32), _sds((N_CHIPS, D_MODEL, wsh), BF16), _sds((8, D_MODEL), F32)],
        scratch_shapes=[pltpu.VMEM((N_CHIPS, D_MODEL, wsh), F32)],
        compiler_params=_params(("arbitrary",)),
    )(dq, dk, dv, dgc, h, x, dx_res, mod, nw, wio)


def _even_post_bwd(dx1, y, oa, olat, proj, gate, wuv, woe):
    S = dx1.shape[0]
    ts = min(ROW_TILE, S)
    nsteps = S // ts

    def body(dx_ref, y_ref, oa_ref, ol_ref, ga0_ref, ga1_ref, gb0_ref, gb1_ref, gate_ref, wuv_ref, woe_ref,
             doa_ref, dga_ref, dgb_ref, dol_ref, dwoe_out, dwuv_ref, stats_ref, dwoe_ref):
        @pl.when(pl.program_id(0) == 0)
        def _():
            dwoe_ref[...] = jnp.zeros((D_MODEL, D_MODEL), F32)
            dwuv_ref[...] = jnp.zeros((1024, 512), F32)
            stats_ref[...] = jnp.zeros((8, D_MODEL), F32)

        dxv = dx_ref[...]
        stats_ref[0:1, :] += _sum_rows(dxv * y_ref[...])
        dy = (gate_ref[...] * dxv).astype(BF16)
        sa, dsa = _silu_and_grad(jnp.concatenate([ga0_ref[...], ga1_ref[...]], axis=1))
        sb, dsb = _silu_and_grad(jnp.concatenate([gb0_ref[...], gb1_ref[...]], axis=1))
        olc = jnp.concatenate([ol_ref[hh] for hh in range(B_HEADS)], axis=1).astype(BF16)
        wuv_v = wuv_ref[...]
        ob = _dot(olc, wuv_v)
        oav = oa_ref[...]
        mix = jnp.concatenate([oav * sa, ob * sb], axis=1).astype(BF16)
        dmix = _dot_nt(dy, woe_ref[...])
        dwoe_ref[...] += _dot_tn(mix, dy)
        dma, dmb = dmix[:, 0:512], dmix[:, 512:1024]
        doa_ref[...] = (dma * sa).astype(BF16)
        dga_ref[...] = (dma * oav * dsa).astype(BF16)
        dgb_ref[...] = (dmb * ob * dsb).astype(BF16)
        dob = (dmb * sb).astype(BF16)
        dol = _dot_nt(dob, wuv_v)
        dwuv_ref[...] += _dot_tn(olc, dob)
        for hh in range(B_HEADS):
            dol_ref[hh] = dol[:, LANES * hh:LANES * (hh + 1)].astype(BF16)

        @pl.when(pl.program_id(0) == nsteps - 1)
        def _():
            dwoe_out[...] = dwoe_ref[...].astype(BF16)

    head_spec = pl.BlockSpec((B_HEADS, ts, LANES), lambda i: (0, i, 0))
    return pl.pallas_call(
        body, name="even_post_bwd", grid=(nsteps,),
        in_specs=[_row_spec(ts, D_MODEL), _row_spec(ts, D_MODEL), _row_spec(ts, 512), head_spec] + _even_gate_specs(ts) +
                 [_full_spec((1, D_MODEL)), _full_spec((1024, 512)), _full_spec((1024, D_MODEL))],
        out_specs=[_row_spec(ts, 512), _row_spec(ts, 512), _row_spec(ts, 512), head_spec,
                   _full_spec((D_MODEL, D_MODEL), single=False), _full_spec((1024, 512), single=False),
                   _full_spec((8, D_MODEL), single=False)],
        out_shape=[_sds((S, 512), BF16), _sds((S, 512), BF16), _sds((S, 512), BF16), _sds((B_HEADS, S, LANES), BF16),
                   _sds((D_MODEL, D_MODEL), BF16), _sds((1024, 512), F32), _sds((8, D_MODEL), F32)],
        scratch_shapes=[pltpu.VMEM((D_MODEL, D_MODEL), F32)],
        compiler_params=_params(("arbitrary",)),
    )(dx1, y, oa, olat, proj, proj, proj, proj, gate, wuv, woe)


def _even_pre_bwd(x, h, proj, dqa, dka, dva, dga, dgb, dqcat, dkcat, dx_res, mod, nw, wie, qn, kn, seg, ca, sa, ct, st,
                  qln, kvln, wuq, wuk):
    S = x.shape[0]
    ts = min(IN_PROJ_ROW_TILE, S)
    nsteps = S // ts

    def body(x_ref, h_ref, proj_ref, dqa_ref, dka_ref, dva_ref, dga_ref, dgb_ref, dqc_ref, dkc_ref, dxr_ref, mod_ref, nw_ref,
             wie_ref, qn_ref, kn_ref, seg_ref, ca_ref, sa_ref, ct_ref, st_ref, qln_ref, kvln_ref, wuq_ref, wuk_ref,
             dx_ref, dwie_out, dwuq_out, dwuk_out, stats_ref, nstats_ref, dwie_ref, dwuq_ref, dwuk_ref, stage):
        @pl.when(pl.program_id(0) == 0)
        def _():
            dwie_ref[...] = jnp.zeros((EVEN_P, D_MODEL), F32)
            dwuq_ref[...] = jnp.zeros((1536, B_Q_LORA), F32)
            dwuk_ref[...] = jnp.zeros((512, 1024), F32)
            stats_ref[...] = jnp.zeros((8, D_MODEL), F32)
            nstats_ref[...] = jnp.zeros((8, 256), F32)

        lane = _lane_iota()
        ca_v, sa_v, ct_v, st_v = ca_ref[...], sa_ref[...], ct_ref[...], st_ref[...]
        seg_v = seg_ref[...]

        def head_norm_bwd(xc, dy, w):
            r = lax.rsqrt(_seg_mean(xc * xc, seg_v) + EPS)
            g = dy * w
            dxc = r * g - xc * (r * r * r) * _seg_mean(xc * g, seg_v)
            return dxc, _sum_rows(dy * (xc * r))

        pieces = []
        dqn = jnp.zeros((1, LANES), F32)
        for cb in range(4):
            sl = slice(LANES * cb, LANES * (cb + 1))
            dy = _rot_bwd(dqa_ref[:, sl] * 0.125, ca_v, sa_v, lane)
            dxc, dw = head_norm_bwd(proj_ref[:, sl], dy, qn_ref[...])
            pieces.append(dxc)
            dqn = dqn + dw
        dxc, dkn = head_norm_bwd(proj_ref[:, 512:640], _rot_bwd(_fold_heads(dka_ref[...], lane), ca_v, sa_v, lane), kn_ref[...])
        pieces += [dxc, _fold_heads(dva_ref[...], lane), dga_ref[...]]
        nstats_ref[0:1, 0:LANES] += dqn + pltpu.roll(dqn, HEAD_DIM, 1)
        nstats_ref[1:2, 0:LANES] += dkn + pltpu.roll(dkn, HEAD_DIM, 1)

        cq = proj_ref[:, 1280:1536]
        rq = _rms(cq)
        cqn_f = cq * rq
        qln_v = qln_ref[...]
        cqn = (cqn_f * qln_v).astype(BF16)
        wuq_v, wuk_v = wuq_ref[...], wuk_ref[...]
        qnope = _dot_nt(cqn, wuq_v[0:512, :]).astype(BF16)
        dqlat = jnp.concatenate([dqc_ref[hh, :, 0:LANES] for hh in range(B_HEADS)], axis=1).astype(BF16)
        dqnope = _dot_nt(dqlat, wuk_v)
        dwuk_ref[...] += _dot_tn(qnope, dqlat)
        dqr = [_rot_bwd(dqc_ref[hh, :, LANES:2 * LANES], ct_v, st_v, lane) for hh in range(B_HEADS)]
        dqb = jnp.concatenate([dqnope] + dqr, axis=1).astype(BF16)
        dcqn = _dot(dqb, wuq_v)
        dwuq_ref[...] += _dot_tn(dqb, cqn)
        nstats_ref[2:3, :] += _sum_rows(dcqn * cqn_f)
        dcq = _rms_bwd(cq, rq, dcqn * qln_v)
        ckv = proj_ref[:, 1536:1664]
        rk = _rms(ckv)
        dckvn = dkc_ref[:, 0:LANES]
        nstats_ref[3:4, 0:LANES] += _sum_rows(dckvn * (ckv * rk))
        dckv = _rms_bwd(ckv, rk, dckvn * kvln_ref[...])
        dkr = _rot_bwd(dkc_ref[:, LANES:2 * LANES], ct_v, st_v, lane)
        pieces += [dcq, dckv, dkr, dgb_ref[...]]
        dproj = jnp.concatenate([piece.astype(BF16) for piece in pieces], axis=1)
        dh = _dot(dproj, wie_ref[...])
        dwie_ref[...] += _dot_tn(dproj, h_ref[...])
        dx_ref[...] = dxr_ref[...] + _norm_mod_bwd(dh, x_ref[...], mod_ref, nw_ref, stats_ref)

        @pl.when(pl.program_id(0) == nsteps - 1)
        def _():
            for r0 in range(0, EVEN_P, 256):
                stage[...] = dwie_ref[r0:r0 + 256, :].astype(BF16)
                pltpu.sync_copy(stage, dwie_out.at[pl.ds(r0, 256), :])
            pltpu.sync_copy(dwuq_ref, dwuq_out)
            pltpu.sync_copy(dwuk_ref, dwuk_out)

    return pl.pallas_call(
        body, name="even_pre_bwd", grid=(nsteps,),
        in_specs=[_row_spec(ts, D_MODEL), _row_spec(ts, D_MODEL), _row_spec(ts, EVEN_P), _row_spec(ts, 512), _row_spec(ts, 2 * LANES),
                  _row_spec(ts, 2 * LANES), _row_spec(ts, 512), _row_spec(ts, 512),
                  pl.BlockSpec((B_HEADS, ts, 2 * LANES), lambda i: (0, i, 0)), _row_spec(ts, 2 * LANES), _row_spec(ts, D_MODEL),
                  _full_spec((3, D_MODEL)), _full_spec((1, D_MODEL)), _full_spec((EVEN_P, D_MODEL)),
                  _full_spec((1, LANES)), _full_spec((1, LANES)), _full_spec((LANES, LANES)),
                  _row_spec(ts, LANES), _row_spec(ts, LANES), _row_spec(ts, LANES), _row_spec(ts, LANES),
                  _full_spec((1, B_Q_LORA)), _full_spec((1, B_KV_LORA)), _full_spec((1536, B_Q_LORA)), _full_spec((512, 1024))],
        out_specs=[_row_spec(ts, D_MODEL), _ANY, _ANY, _ANY, _full_spec((8, D_MODEL), single=False), _full_spec((8, 256), single=False)],
        out_shape=[_sds((S, D_MODEL), F32), _sds((EVEN_P, D_MODEL), BF16), _sds((1536, B_Q_LORA), F32), _sds((512, 1024), F32),
                   _sds((8, D_MODEL), F32), _sds((8, 256), F32)],
        scratch_shapes=[pltpu.VMEM((EVEN_P, D_MODEL), F32), pltpu.VMEM((1536, B_Q_LORA), F32), pltpu.VMEM((512, 1024), F32),
                        pltpu.VMEM((256, D_MODEL), BF16)],
        compiler_params=_params(("arbitrary",)),
    )(x, h, proj, dqa, dka, dva, dga, dgb, dqcat, dkcat, dx_res, mod, nw, wie, qn, kn, seg, ca, sa, ct, st, qln, kvln, wuq, wuk)


def _ada_fwd(c_all, w, b):
    n = w.shape[2]

    def body(c_ref, w_ref, b_ref, o_ref):
        cv = c_ref[...]
        o_ref[0] = _dot_f32(cv * _sigmoid(cv), w_ref[0]) + b_ref[0]

    return pl.pallas_call(
        body, name="ada_fwd", grid=(2,),
        in_specs=[pl.BlockSpec((N_DEV, D_MODEL), lambda l: (0, 0)), pl.BlockSpec((1, D_MODEL, n), lambda l: (l, 0, 0)),
                  pl.BlockSpec((1, 1, n), lambda l: (l, 0, 0))],
        out_specs=pl.BlockSpec((1, N_DEV, n), lambda l: (l, 0, 0)),
        out_shape=_sds((2, N_DEV, n), F32),
        compiler_params=_params(("arbitrary",)),
    )(c_all, w, b)


def _ada_bwd(c_all_t, dmod):
    n = dmod.shape[2]

    def body(c_ref, d_ref, o_ref):
        cv = c_ref[...]
        act = cv * _sigmoid(cv)
        dv = d_ref[0]
        acc = act[:, 0:1] * dv[0:1, :]
        for bb in range(1, N_DEV):
            acc = acc + act[:, bb:bb + 1] * dv[bb:bb + 1, :]
        o_ref[0] = acc

    return pl.pallas_call(
        body, name="ada_bwd", grid=(2,),
        in_specs=[pl.BlockSpec((D_MODEL, N_DEV), lambda l: (0, 0)), pl.BlockSpec((1, N_DEV, n), lambda l: (l, 0, 0))],
        out_specs=pl.BlockSpec((1, D_MODEL, n), lambda l: (l, 0, 0)),
        out_shape=_sds((2, D_MODEL, n), F32),
        compiler_params=_params(("arbitrary",)),
    )(c_all_t, dmod)


ADAM_ROW_TILE = 512


def _adam_update(g, w, m, v):
    m_new = ADAM_B1 * m + (1.0 - ADAM_B1) * g
    v_new = ADAM_B2 * v + (1.0 - ADAM_B2) * jnp.square(g)
    m_hat = m_new / (1.0 - ADAM_B1 ** ADAM_STEP)
    v_hat = v_new / (1.0 - ADAM_B2 ** ADAM_STEP)
    return -ADAM_LR * (m_hat / (jnp.sqrt(v_hat) + ADAM_EPS) + ADAM_WD * w), m_new, v_new


SMALL_ROWS = dict(dmod=(0, D_MODEL), norm_w=(6, D_MODEL), final_norm=(8, D_MODEL), a_q_norm=(9, HEAD_DIM), a_k_norm=(10, HEAD_DIM),
                  b_q_lora_norm=(11, B_Q_LORA), b_kv_lora_norm=(12, B_KV_LORA), c_sink=(13, C_HEADS))
SMALL_WEIGHTS = ("ada_b", "norm_w", "final_norm", "a_q_norm", "a_k_norm", "b_q_lora_norm", "b_kv_lora_norm", "c_sink")
LOSS_ROW = 14


def _pack_small(res):
    def padded(v):
        return jnp.concatenate([v, jnp.zeros((v.shape[0], D_MODEL - v.shape[1]), F32)], axis=1)

    rows = [res["dmod"].reshape(6, D_MODEL), res["norm_w"], res["final_norm"].reshape(1, D_MODEL)]
    rows += [padded(res[k]) for k in ("a_q_norm", "a_k_norm", "b_q_lora_norm", "b_kv_lora_norm", "c_sink")]
    return jnp.concatenate(rows + [res["loss_row"], jnp.zeros((1, D_MODEL), F32)], axis=0)


def _adam_small(parts, ws, ms, vs):
    nw = len(SMALL_WEIGHTS)

    def body(*refs):
        p_ref = refs[0]
        w_refs, m_refs, v_refs = refs[1:1 + nw], refs[1 + nw:1 + 2 * nw], refs[1 + 2 * nw:1 + 3 * nw]
        outs = refs[1 + 3 * nw:]
        g_all = p_ref[0]
        for k in range(1, N_DEV):
            g_all = g_all + p_ref[k]
        for idx, name in enumerate(SMALL_WEIGHTS):
            if name == "ada_b":
                g = jnp.concatenate([jnp.concatenate([g_all[3 * l + t:3 * l + t + 1] for t in range(3)], axis=1) for l in range(2)],
                                    axis=0)
            else:
                row, width = SMALL_ROWS[name]
                g = g_all[row:row + w_refs[idx].shape[0], 0:width]
            d, m_new, v_new = _adam_update(g, w_refs[idx][...], m_refs[idx][...], v_refs[idx][...])
            outs[4 * idx][...], outs[4 * idx + 1][...], outs[4 * idx + 2][...], outs[4 * idx + 3][...] = g, d, m_new, v_new
        outs[4 * nw][...] = g_all[LOSS_ROW:LOSS_ROW + 1, 0:LANES]

    out_shape = []
    for w in ws:
        out_shape += [_sds(w.shape, F32)] * 4
    out_shape.append(_sds((1, LANES), F32))
    return pl.pallas_call(body, name="adam_small", out_shape=out_shape,
                          compiler_params=pltpu.CompilerParams(vmem_limit_bytes=VMEM_LIMIT))(parts, *ws, *ms, *vs)


def _adam(parts, w, m, v, name, by_columns=False):
    P, R, C = parts.shape
    if by_columns:
        tr, tc = R, 256
    else:
        tr, tc = (R if R <= ADAM_ROW_TILE else ADAM_ROW_TILE), C
    assert R % tr == 0 and C % tc == 0

    def body(p_ref, w_ref, m_ref, v_ref, g_ref, d_ref, nm_ref, nv_ref):
        g = p_ref[0].astype(F32)
        for k in range(1, P):
            g = g + p_ref[k].astype(F32)
        g_ref[...] = g
        d_ref[...], nm_ref[...], nv_ref[...] = _adam_update(g, w_ref[...], m_ref[...], v_ref[...])

    tile = (lambda i: (0, i)) if by_columns else (lambda i: (i, 0))
    spec = pl.BlockSpec((tr, tc), tile)
    return pl.pallas_call(
        body, name=name, grid=(C // tc if by_columns else R // tr,),
        in_specs=[pl.BlockSpec((P, tr, tc), lambda i: (0,) + tile(i)), spec, spec, spec],
        out_specs=[spec, spec, spec, spec], out_shape=[_sds((R, C), F32)] * 4,
        compiler_params=_params(("arbitrary",)),
    )(parts, w, m, v)


_ANY = pl.BlockSpec(memory_space=pl.ANY)
CHIP_FLIPS = ((1, 0), (0, 1), (1, 1))
DEV_FLIPS = tuple((dx, dy, dc) for dx in (0, 1) for dy in (0, 1) for dc in (0, 1) if dx + dy + dc)


def _flip(a, d):
    return a if d == 0 else 1 - a


def _my_place():
    return lax.axis_index("x"), lax.axis_index("y"), lax.axis_index("c")


def _gather8_copies(ins, outs, send_sems, recv_sems, loc_sems):
    x, y, c = _my_place()
    me = 4 * x + 2 * y + c
    copies = []
    for a in range(len(ins)):
        copies.append(pltpu.make_async_copy(ins[a], outs[a].at[me], loc_sems.at[a]))
        for k, (dx, dy, dc) in enumerate(DEV_FLIPS):
            copies.append(pltpu.make_async_remote_copy(
                src_ref=ins[a], dst_ref=outs[a].at[me], send_sem=send_sems.at[a, k], recv_sem=recv_sems.at[a, k],
                device_id=(_flip(x, dx), _flip(y, dy), _flip(c, dc)), device_id_type=MESH_ID))
    return copies


def _gather8_sems(n):
    return [pltpu.SemaphoreType.DMA((n, 7)), pltpu.SemaphoreType.DMA((n, 7)), pltpu.SemaphoreType.DMA((n,))]


def _gather_dev8(arrs, name):
    n = len(arrs)

    def body(*refs):
        copies = _gather8_copies(refs[:n], refs[n:2 * n], *refs[2 * n:])
        for cp in copies:
            cp.start()
        for cp in copies:
            cp.wait()

    return pl.pallas_call(
        body, name=name, in_specs=[_ANY] * n, out_specs=[_ANY] * n,
        out_shape=[_sds((N_DEV,) + a.shape, a.dtype) for a in arrs], scratch_shapes=_gather8_sems(n),
    )(*arrs)


class _Exchange:
    def __init__(self, arrs, out_shapes, n_sems, phases):
        self.arrs, self.out_shapes, self.n_sems, self._phases = list(arrs), list(out_shapes), n_sems, phases

    @property
    def n(self):
        return len(self.arrs)

    def sem_shapes(self):
        return [pltpu.SemaphoreType.DMA((self.n, self.n_sems)), pltpu.SemaphoreType.DMA((self.n, self.n_sems)),
                pltpu.SemaphoreType.DMA((self.n,))]

    def phases(self, ins, outs, sems):
        return self._phases(ins, outs, *sems)

    def run(self, name):
        n = self.n

        def body(*refs):
            start, mid, end = self.phases(refs[:n], refs[n:2 * n], refs[2 * n:])
            start()
            mid()
            end()

        return pl.pallas_call(body, name=name, in_specs=[_ANY] * n, out_specs=[_ANY] * n, out_shape=self.out_shapes,
                              scratch_shapes=self.sem_shapes())(*self.arrs)

def _gather_halves_phases(ins, outs, send_sems, recv_sems, loc_sems):
    n = len(ins)
    x, y, c = _my_place()
    chip = 2 * x + y
    sibling = (x, y, 1 - c)
    peers = [(_flip(x, dx), _flip(y, dy)) for dx, dy in CHIP_FLIPS]

    def remote(src, p, half, a, k, to):
        return pltpu.make_async_remote_copy(src_ref=src, dst_ref=outs[a].at[p, half], send_sem=send_sems.at[a, k],
                                            recv_sem=recv_sems.at[a, k], device_id=to, device_id_type=MESH_ID)

    def local(a):
        return pltpu.make_async_copy(ins[a], outs[a].at[chip], loc_sems.at[a])

    def first(a, k):
        return remote(ins[a].at[c], chip, c, a, k, (*peers[k], c))

    def passed(a, k):
        p = 2 * peers[k][0] + peers[k][1]
        return remote(outs[a].at[p, c], p, c, a, 3 + k, sibling)

    def start():
        for a in range(n):
            local(a).start()
            for k in range(3):
                first(a, k).start()

    def mid():
        for a in range(n):
            for k in range(3):
                p = 2 * peers[k][0] + peers[k][1]
                remote(outs[a].at[p, c], p, c, a, k, sibling).wait_recv()
                passed(a, k).start()

    def end():
        for a in range(n):
            for k in range(3):
                p = 2 * peers[k][0] + peers[k][1]
                remote(outs[a].at[p, 1 - c], p, 1 - c, a, 3 + k, sibling).wait_recv()
        for a in range(n):
            for k in range(3):
                first(a, k).wait_send()
                passed(a, k).wait_send()
            local(a).wait()

    return start, mid, end


def _gather_chip4_halves(arrs):
    return _Exchange(arrs, [_sds((N_CHIPS,) + a.shape, a.dtype) for a in arrs], 6, _gather_halves_phases)


def _reduce_phases(n_whole, ins, outs, send_sems, recv_sems, loc_sems):
    n = len(ins)
    x, y, c = _my_place()
    chip = 2 * x + y
    sibling = (x, y, 1 - c)
    peers = [(_flip(x, dx), _flip(y, dy)) for dx, dy in CHIP_FLIPS]

    def remote(src, slot, a, k, to):
        return pltpu.make_async_remote_copy(src_ref=src, dst_ref=outs[a].at[slot], send_sem=send_sems.at[a, k],
                                            recv_sem=recv_sems.at[a, k], device_id=to, device_id_type=MESH_ID)

    def block(a, p):
        return ins[a] if a >= n - n_whole else ins[a].at[p]

    def local(a):
        return pltpu.make_async_copy(block(a, chip), outs[a].at[2 * chip + c], loc_sems.at[a])

    def own(a):
        return remote(block(a, chip), 2 * chip + c, a, 0, sibling)

    def first(a, k):
        return remote(block(a, 2 * peers[k][0] + peers[k][1]), 2 * chip + c, a, 1 + k, (*peers[k], c))

    def passed(a, k):
        slot = 2 * (2 * peers[k][0] + peers[k][1]) + c
        return remote(outs[a].at[slot], slot, a, 4 + k, sibling)

    def start():
        for a in range(n):
            local(a).start()
            own(a).start()
            for k in range(3):
                first(a, k).start()

    def mid():
        for a in range(n):
            for k in range(3):
                slot = 2 * (2 * peers[k][0] + peers[k][1]) + c
                remote(outs[a].at[slot], slot, a, 1 + k, sibling).wait_recv()
                passed(a, k).start()

    def end():
        for a in range(n):
            remote(outs[a].at[2 * chip + 1 - c], 2 * chip + 1 - c, a, 0, sibling).wait_recv()
            for k in range(3):
                slot = 2 * (2 * peers[k][0] + peers[k][1]) + 1 - c
                remote(outs[a].at[slot], slot, a, 4 + k, sibling).wait_recv()
        for a in range(n):
            own(a).wait_send()
            for k in range(3):
                first(a, k).wait_send()
                passed(a, k).wait_send()
            local(a).wait()

    return start, mid, end


def _reduce_exchange(arrs, whole=()):
    shapes = [_sds((N_DEV,) + a.shape[1:], a.dtype) for a in arrs] + [_sds((N_DEV,) + a.shape, a.dtype) for a in whole]
    return _Exchange(list(arrs) + list(whole), shapes, 7, functools.partial(_reduce_phases, len(whole)))


def _first_half(n):
    return -(-n // (2 * DMA_ROWS)) * DMA_ROWS


def _shard_halves_t(w):
    wt = w.T.astype(BF16)
    n1 = _first_half(wt.shape[0])
    pad = jnp.zeros((2 * n1 - wt.shape[0], wt.shape[1]), BF16)
    return jnp.stack([wt[0:n1], jnp.concatenate([wt[n1:], pad], axis=0)])


def _gathered_rows(g, n):
    n1 = _first_half(n)
    return [g[p, half, 0:(n - n1 if half else n1)] for p in range(N_CHIPS) for half in range(2)]


def _even_in_layout_t(g):
    n = EVEN_IN // N_CHIPS
    n1 = _first_half(n)
    gap, gap_rows = 1696, EVEN_P - EVEN_IN
    spans = []
    for p in range(N_CHIPS):
        for half, (r0, rows) in enumerate([(0, n1), (n1, n - n1)]):
            lo, hi = p * n + r0, p * n + r0 + rows
            if lo < gap < hi:
                spans += [(p, half, 0, gap - lo, lo), (p, half, gap - lo, hi - gap, gap + gap_rows)]
            else:
                spans.append((p, half, 0, rows, lo + (gap_rows if lo >= gap else 0)))

    def body(g_ref, o_ref, zero_ref, sems):
        zero_ref[...] = jnp.zeros(zero_ref.shape, zero_ref.dtype)
        copies = [pltpu.make_async_copy(zero_ref, o_ref.at[pl.ds(gap, gap_rows)], sems.at[len(spans)])]
        for i, (p, half, src, rows, dst) in enumerate(spans):
            copies.append(pltpu.make_async_copy(g_ref.at[p, half, pl.ds(src, rows)], o_ref.at[pl.ds(dst, rows)], sems.at[i]))
        for cp in copies:
            cp.start()
        for cp in copies:
            cp.wait()

    return pl.pallas_call(body, name="even_w_in_rows", in_specs=[_ANY], out_specs=_ANY, out_shape=_sds((EVEN_P, g.shape[-1]), g.dtype),
                          scratch_shapes=[pltpu.VMEM((gap_rows, g.shape[-1]), g.dtype),
                                          pltpu.SemaphoreType.DMA((len(spans) + 1,))])(g)


def _even_in_unlayout_t(g):
    return jnp.concatenate([g[0:1696], g[1792:2304]], axis=0)


def _uq_layout_t(g):
    wt = jnp.concatenate(_gathered_rows(g, B_HEADS * (B_NOPE + B_ROPE) // N_CHIPS), axis=0)
    per = B_NOPE + B_ROPE
    pad = jnp.zeros((LANES - B_ROPE, wt.shape[1]), wt.dtype)
    nope = [wt[per * h:per * h + B_NOPE] for h in range(B_HEADS)]
    rope = [jnp.concatenate([wt[per * h + B_NOPE:per * (h + 1)], pad], axis=0) for h in range(B_HEADS)]
    return jnp.concatenate(nope + rope, axis=0)


def _uq_unlayout_t(g):
    parts = []
    for h in range(B_HEADS):
        parts += [g[B_NOPE * h:B_NOPE * (h + 1)], g[512 + LANES * h:512 + LANES * h + B_ROPE]]
    return jnp.concatenate(parts, axis=0)


def _block_diag(blocks):
    rows = []
    for h, blk in enumerate(blocks):
        r, cdim = blk.shape
        n = len(blocks)
        rows.append(jnp.concatenate([jnp.zeros((r, cdim * h), blk.dtype), blk, jnp.zeros((r, cdim * (n - 1 - h)), blk.dtype)],
                                    axis=1))
    return jnp.concatenate(rows, axis=0)


def _uk_layout(w):
    return _block_diag([w[:, h, :].T for h in range(B_HEADS)])


def _uk_unlayout(g):
    return jnp.stack([g[B_NOPE * h:B_NOPE * (h + 1), LANES * h:LANES * (h + 1)].T for h in range(B_HEADS)], axis=1)


def _uv_layout(w):
    return _block_diag([w[:, h, :] for h in range(B_HEADS)])


def _uv_unlayout(g):
    return jnp.stack([g[LANES * h:LANES * (h + 1), B_V * h:B_V * (h + 1)] for h in range(B_HEADS)], axis=1)


def _rope_tables(S):
    inv = ROPE_THETA ** (-jnp.arange(0, 32, 2, dtype=F32) / 32)
    tok = jnp.arange(S)

    def tab(pos):
        ang = pos.astype(F32)[:, None] * inv[None, :]
        cos, sin = jnp.cos(ang), jnp.sin(ang)
        return jnp.concatenate([cos, cos], axis=1), jnp.concatenate([-sin, sin], axis=1)

    cr, sr = tab(tok // GRID_W)
    cc, sc = tab(tok % GRID_W)
    ct, st = tab(tok)
    return (jnp.tile(jnp.concatenate([cr, cc], axis=1), (1, 2)), jnp.tile(jnp.concatenate([sr, sc], axis=1), (1, 2)),
            jnp.tile(ct, (1, 4)), jnp.tile(st, (1, 4)))


A_TQ, A_TK, A_SUB = 512, 4096, 512
A_FWD_SUB = 1024
B_TQ, B_TK, B_SUB = 128, 4096, 1024
B_BWD_TK, B_BWD_SUB = 4096, 512
C_T = 256
C_BLOCKS_PER_STEP = 8
KV_SHARE = 2


def _local_step(x0, tgt, mod, norm_w, wie, wuq, wuk, wuv, late_shards, a_q_norm, a_k_norm, q_lora_norm, kv_lora_norm,
                c_sink, final_norm):
    S = x0.shape[0]
    mod3 = mod.reshape(2, 3, D_MODEL)
    ca, sa, ct, st = _rope_tables(S)
    lane_seg = np.arange(LANES) // HEAD_DIM
    seg = jnp.asarray((lane_seg[:, None] == lane_seg[None, :]).astype(np.float32)).astype(BF16)
    qn = jnp.tile(a_q_norm.reshape(1, HEAD_DIM), (1, 2))
    kn = jnp.tile(a_k_norm.reshape(1, HEAD_DIM), (1, 2))
    qln, kvln = q_lora_norm.reshape(1, B_Q_LORA), kv_lora_norm.reshape(1, B_KV_LORA)
    nw0, nw1 = norm_w[0:1], norm_w[1:2]
    gate0, gate1 = mod3[0, 2:3], mod3[1, 2:3]
    a_tq, a_tk, b_tq, b_tk, bb_tk, c_t = min(A_TQ, S), min(A_TK, S), min(B_TQ, S), min(B_TK, S), min(B_BWD_TK, S), min(C_T, S)
    a_sub, b_sub, bb_sub = min(A_SUB, a_tk), min(B_SUB, b_tk), min(B_BWD_SUB, bb_tk)

    h0, proj_e, qa, ka, va, qcat, kcat, ka_t, va_t, kcat_t = _even_pre_fwd(x0, mod3[0], nw0, wie, qn, kn, seg, ca, sa, ct, st,
                                                                           qln, kvln, wuq, wuk)
    oa, lse_a, woe_g, wio_g, woo_g = _pp_fwd(qa, ka, va_t, kdiv=KV_SHARE, tq=a_tq, tk=a_tk, sub=min(A_FWD_SUB, a_tk), name="attn_a_fwd",
                                             side=_gather_chip4_halves(late_shards))
    woe = woe_g.reshape(D_MODEL, D_MODEL)
    wio = wio_g.reshape(N_CHIPS, D_MODEL, ODD_IN // N_CHIPS)
    woo = woo_g.reshape(D_MODEL, D_MODEL)
    olat, lse_b = _mla_fwd(qcat, kcat, kcat_t, tq=b_tq, tk=b_tk, sub=b_sub)
    y0, x1 = _even_post_fwd(oa, olat, proj_e, x0, gate0, wuv, woe)
    h1, gc, qc, kc, vc, kc_t, vc_t = _odd_pre_fwd(x1, mod3[1], nw1, wio)
    slopes = 2.0 ** (-8.0 * jnp.arange(1, C_HEADS + 1, dtype=F32) / C_HEADS)
    slope_rows = jnp.repeat(slopes.reshape(C_HEADS // 2, 2), c_t, axis=1)[:, None, :]
    sink_rows = jnp.repeat(c_sink.reshape(C_HEADS // 2, 2), c_t, axis=1)[:, None, :]
    win_dist = _win_dist_table(S, c_t)
    oc, lse_c = _win_fwd(qc, kc, vc_t, win_dist, slope_rows, sink_rows, kdiv=KV_SHARE, tq=c_t, nbs=C_BLOCKS_PER_STEP,
                         name="attn_c_fwd")
    doc, dgc, dx2, dwoo, st_f = _odd_post(oc, gc, x1, gate1, woo, final_norm.reshape(1, D_MODEL), tgt)
    dqc, dkc, dvc, dsink_raw = _win_bwd(qc, kc, kc_t, vc, oc, doc, lse_c, win_dist, slope_rows, sink_rows, kdiv=KV_SHARE, tq=c_t,
                                        nbs=C_BLOCKS_PER_STEP, name="attn_c_bwd")
    dx1, dwio, st_1 = _odd_pre_bwd(dqc, dkc, dvc, dgc, h1, x1, dx2, mod3[1], nw1, wio)
    doa, dga, dgb, dolat, dwoe, dwuv, st_e = _even_post_bwd(dx1, y0, oa, olat, proj_e, gate0, wuv, woe)
    late_grads = _reduce_exchange([dwoe.reshape(N_CHIPS, D_MODEL // N_CHIPS, D_MODEL), dwio,
                                   dwoo.reshape(N_CHIPS, D_MODEL // N_CHIPS, D_MODEL)])
    dqa, dka, dva, p_woe, p_wio, p_woo = _pp_bwd(qa, ka, ka_t, va, oa, doa, lse_a, kdiv=KV_SHARE, tq=a_tq, tk=a_tk, sub=a_sub,
                                                 name="attn_a_bwd", side=late_grads)
    dqcat, dkcat = _mla_bwd(qcat, kcat, kcat_t, olat, dolat, lse_b, tq=b_tq, tk=bb_tk, sub=bb_sub)
    dx0, dwie, dwuq, dwuk, st_0, nst = _even_pre_bwd(x0, h0, proj_e, dqa, dka, dva, dga, dgb, dqcat, dkcat, dx1, mod3[0], nw0,
                                                     wie, qn, kn, seg, ca, sa, ct, st, qln, kvln, wuq, wuk)
    dsink_pairs = jnp.stack([dsink_raw[:, 0, 0], dsink_raw[:, 1, 0]], axis=1).reshape(C_HEADS)
    return dict(
        loss_row=st_f[2:3], dx=dx0,
        dmod=jnp.stack([jnp.concatenate([st_0[0], st_0[1], st_e[0]]), jnp.concatenate([st_1[0], st_1[1], st_f[1]])]),
        norm_w=jnp.stack([st_0[2], st_1[2]]), final_norm=st_f[0],
        a_q_norm=nst[0:1, 0:HEAD_DIM], a_k_norm=nst[1:2, 0:HEAD_DIM], b_q_lora_norm=nst[2:3, :], b_kv_lora_norm=nst[3:4, 0:LANES],
        c_sink=dsink_pairs.reshape(1, C_HEADS),
        even_w_in=dwie, b_w_uq=dwuq, b_w_uk=dwuk, b_w_uv=dwuv, even_w_out=p_woe, odd_w_in=p_wio, odd_w_out=p_woo)


WEIGHT_NAMES = ("norm_w", "ada_w", "ada_b", "even_w_in", "a_q_norm", "a_k_norm", "b_q_lora_norm", "b_kv_lora_norm", "b_w_uq",
                "b_w_uk", "b_w_uv", "even_w_out", "odd_w_in", "c_sink", "odd_w_out", "final_norm")


def kernel(x, c, norm_w, ada_w, ada_b, even_w_in, a_q_norm, a_k_norm, b_q_lora_norm, b_kv_lora_norm, b_w_uq, b_w_uk, b_w_uv, even_w_out, odd_w_in, c_sink, odd_w_out, final_norm, loss_target, m_norm_w, m_ada_w, m_ada_b, m_even_w_in, m_a_q_norm, m_a_k_norm, m_b_q_lora_norm, m_b_kv_lora_norm, m_b_w_uq, m_b_w_uk, m_b_w_uv, m_even_w_out, m_odd_w_in, m_c_sink, m_odd_w_out, m_final_norm, v_norm_w, v_ada_w, v_ada_b, v_even_w_in, v_a_q_norm, v_a_k_norm, v_b_q_lora_norm, v_b_kv_lora_norm, v_b_w_uq, v_b_w_uk, v_b_w_uv, v_even_w_out, v_odd_w_in, v_c_sink, v_odd_w_out, v_final_norm):
    given = dict(locals())
    xi, yi, ci = _my_place()
    chip = 2 * xi + yi
    dev = 2 * chip + ci
    n_ada = ada_w.shape[2]

    (c_all,) = _gather_dev8([c], "gather_c")
    c_all = c_all.reshape(N_DEV, D_MODEL)
    bias = lax.dynamic_slice_in_dim(ada_b, chip * n_ada, n_ada, axis=1).reshape(2, 1, n_ada)
    mod_cols = _ada_fwd(c_all, ada_w, bias)
    def halves(w):
        return w.astype(BF16).reshape((2, w.shape[0] // 2) + w.shape[1:])

    mod_all, wie_g, wuq_g = _gather_chip4_halves(
        [mod_cols, _shard_halves_t(even_w_in[0]), _shard_halves_t(b_w_uq[0])]).run("gather_weights")
    mod = jnp.transpose(lax.dynamic_index_in_dim(mod_all, dev, axis=2, keepdims=False), (1, 0, 2)).reshape(2, 3 * D_MODEL)

    res = _local_step(
        x[0], loss_target[0], mod, norm_w,
        _even_in_layout_t(wie_g), _uq_layout_t(wuq_g), _uk_layout(b_w_uk[0].astype(BF16)),
        _uv_layout(b_w_uv[0].astype(BF16)), [halves(even_w_out[0]), halves(odd_w_in[0]), halves(odd_w_out[0])],
        a_q_norm, a_k_norm, b_q_lora_norm, b_kv_lora_norm, c_sink, final_norm)

    latent = jnp.stack([_uk_unlayout(res["b_w_uk"]).reshape(B_KV_LORA, 512),
                        _uv_unlayout(res["b_w_uv"]).reshape(B_KV_LORA, 512)]).astype(BF16)
    p_wie, p_wuq, small_all, latent_all = _reduce_exchange(
        [_even_in_unlayout_t(res["even_w_in"]).reshape(N_CHIPS, EVEN_IN // N_CHIPS, D_MODEL),
         _uq_unlayout_t(res["b_w_uq"]).astype(BF16).reshape(N_CHIPS, -1, B_Q_LORA)],
        whole=[_pack_small(res), latent]).run("reduce_exchange")
    shard_parts = dict(even_w_in=p_wie, b_w_uq=p_wuq, **{k: res[k] for k in ("even_w_out", "odd_w_in", "odd_w_out")})
    dmod_all = small_all[:, 0:6, :].reshape(N_DEV, 2, 3 * D_MODEL)
    dmod_cols = jnp.transpose(lax.dynamic_slice_in_dim(dmod_all, chip * n_ada, n_ada, axis=2), (1, 0, 2))
    parts = dict(shard_parts)
    parts["ada_w"] = _ada_bwd(c_all.T, dmod_cols).reshape(1, 2 * D_MODEL, n_ada)
    parts["b_w_uk"], parts["b_w_uv"] = [latent_all[:, t].reshape(N_DEV, B_KV_LORA * B_HEADS, -1) for t in range(2)]

    def as2d(a):
        return a.reshape((-1, a.shape[-1]) if a.ndim > 1 else (1, a.shape[0]))

    results = {}
    small_outs = _adam_small(small_all, *[[as2d(given[pre + k]) for k in SMALL_WEIGHTS] for pre in ("", "m_", "v_")])
    for idx, k in enumerate(SMALL_WEIGHTS):
        results[k] = small_outs[4 * idx:4 * idx + 4]
    for k, p in parts.items():
        if k in ("even_w_in", "b_w_uq"):
            outs = _adam(p, given[k][0].T, given["m_" + k][0].T, given["v_" + k][0].T, "adam_" + k, by_columns=k == "even_w_in")
            results[k] = [o.T for o in outs]
            continue
        shape2 = (p.shape[-2], p.shape[-1])
        results[k] = _adam(p, given[k].reshape(shape2), given["m_" + k].reshape(shape2), given["v_" + k].reshape(shape2),
                           "adam_" + k)
    by_kind = [[results[k][t].reshape(given[k].shape) for k in WEIGHT_NAMES] for t in range(4)]
    return (small_outs[-1][0, 0], res["dx"][None], *by_kind[0], *by_kind[1], *by_kind[2], *by_kind[3])
```

```python
import functools

import numpy as np
import jax
import jax.numpy as jnp
from jax import lax
from jax.experimental import pallas as pl
from jax.experimental.pallas import tpu as pltpu

F32 = jnp.float32
BF16 = jnp.bfloat16
HIGHEST = lax.Precision.HIGHEST
MESH_ID = pl.DeviceIdType.MESH

D_MODEL = 1024
HEAD_DIM = 64
GRID_W = 64
EPS = 1e-6
ROPE_THETA = 10000.0
B_HEADS, B_NOPE, B_ROPE, B_V = 8, 64, 32, 64
B_Q_LORA, B_KV_LORA = 256, 128
C_HEADS = 16
WINDOW = 128
EVEN_IN, ODD_IN = 2208, 2560
EVEN_P = 2304
N_CHIPS, N_DEV = 4, 8
LANES = 128
NEG = -1e30
VMEM_LIMIT = 60 * 1024 * 1024

ADAM_LR, ADAM_B1, ADAM_B2, ADAM_EPS, ADAM_WD, ADAM_STEP = 0.001, 0.9, 0.999, 1e-08, 0.01, 10

ROW_TILE = 512
IN_PROJ_ROW_TILE = 256


def _dot(a, b):
    return lax.dot_general(a, b, (((1,), (0,)), ((), ())), preferred_element_type=F32)


def _dot_nt(a, b):
    return lax.dot_general(a, b, (((1,), (1,)), ((), ())), preferred_element_type=F32)


def _dot_tn(a, b):
    return lax.dot_general(a, b, (((0,), (0,)), ((), ())), preferred_element_type=F32)


def _dot_f32(a, b):
    return lax.dot_general(a, b, (((1,), (0,)), ((), ())), precision=HIGHEST, preferred_element_type=F32)


def _sigmoid(x):
    return 1.0 / (1.0 + jnp.exp(-x))


def _silu_and_grad(g):
    s = _sigmoid(g)
    return g * s, s * (1.0 + g * (1.0 - s))


def _lane_iota():
    return lax.broadcasted_iota(jnp.int32, (1, LANES), 1)


def _partner(x, lane):
    return jnp.where((lane % 32) < 16, pltpu.roll(x, LANES - 16, 1), pltpu.roll(x, 16, 1))


def _rot(x, cos, sin_signed, lane):
    return x * cos + _partner(x, lane) * sin_signed


def _rot_bwd(dy, cos, sin_signed, lane):
    return dy * cos + _partner(dy * sin_signed, lane)


def _rms(x):
    return lax.rsqrt(jnp.mean(x * x, axis=-1, keepdims=True) + EPS)


def _rms_bwd(x, r, g):
    return r * g - x * (r * r * r) * jnp.mean(x * g, axis=-1, keepdims=True)


def _seg_mean(v, seg_ones):
    hi = v.astype(BF16)
    lo = (v - hi.astype(F32)).astype(BF16)
    return (_dot(hi, seg_ones) + _dot(lo, seg_ones)) * (1.0 / HEAD_DIM)


def _dup_heads(x, lane):
    swapped = pltpu.roll(x, HEAD_DIM, 1)
    lo = lane < HEAD_DIM
    return jnp.concatenate([jnp.where(lo, x, swapped), jnp.where(lo, swapped, x)], axis=1)


def _fold_heads(x2, lane):
    a, b = x2[:, 0:LANES], x2[:, LANES:2 * LANES]
    return jnp.where(lane < HEAD_DIM, a + pltpu.roll(a, HEAD_DIM, 1), b + pltpu.roll(b, HEAD_DIM, 1))


def _row_spec(ts, cols):
    return pl.BlockSpec((ts, cols), lambda i: (i, 0))


def _full_spec(shape, single=True):
    nd = len(shape)
    if single:
        return pl.BlockSpec(shape, lambda i: (0,) * nd, pipeline_mode=pl.Buffered(1))
    return pl.BlockSpec(shape, lambda i: (0,) * nd)


def _sds(shape, dtype):
    return jax.ShapeDtypeStruct(shape, dtype)


def _params(sem):
    return pltpu.CompilerParams(dimension_semantics=sem, vmem_limit_bytes=VMEM_LIMIT)


def _even_pre_fwd(x, mod, nw, wie, qn, kn, seg, ca, sa, ct, st, qln, kvln, wuq, wuk):
    S = x.shape[0]
    ts = min(IN_PROJ_ROW_TILE, S)

    def body(x_ref, mod_ref, nw_ref, wie_ref, qn_ref, kn_ref, seg_ref, ca_ref, sa_ref, ct_ref, st_ref, qln_ref,
             kvln_ref, wuq_ref, wuk_ref, h_ref, proj_ref, qa_ref, ka_ref, va_ref, qcat_ref, kcat_ref, kat_ref, vat_ref, kcatt_ref):
        xv = x_ref[...]
        h = (xv * _rms(xv) * nw_ref[...]) * (1.0 + mod_ref[1:2, :]) + mod_ref[0:1, :]
        hb = h.astype(BF16)
        h_ref[...] = hb
        proj = _dot_nt(hb, wie_ref[...])
        proj_ref[...] = proj
        lane = _lane_iota()
        ca_v, sa_v, ct_v, st_v = ca_ref[...], sa_ref[...], ct_ref[...], st_ref[...]
        seg_v = seg_ref[...]
        for cb in range(4):
            xc = proj[:, LANES * cb:LANES * (cb + 1)]
            r = lax.rsqrt(_seg_mean(xc * xc, seg_v) + EPS)
            y = _rot(xc * r * qn_ref[...], ca_v, sa_v, lane)
            qa_ref[:, LANES * cb:LANES * (cb + 1)] = (y * 0.125).astype(BF16)
        kc = proj[:, 512:640]
        r = lax.rsqrt(_seg_mean(kc * kc, seg_v) + EPS)
        ka_v = _dup_heads(_rot(kc * r * kn_ref[...], ca_v, sa_v, lane), lane)
        ka_ref[...] = ka_v.astype(BF16)
        kat_ref[...] = ka_v.T.astype(BF16)
        va_v = _dup_heads(proj[:, 640:768], lane)
        va_ref[...] = va_v.astype(BF16)
        vat_ref[...] = va_v.T.astype(BF16)
        cq = proj[:, 1280:1536]
        cqn = (cq * _rms(cq) * qln_ref[...]).astype(BF16)
        ckv = proj[:, 1536:1664]
        ckvn = ckv * _rms(ckv) * kvln_ref[...]
        qb = _dot_nt(cqn, wuq_ref[...])
        qlat = _dot(qb[:, 0:512].astype(BF16), wuk_ref[...])
        for hh in range(B_HEADS):
            qcat_ref[hh, :, 0:LANES] = qlat[:, LANES * hh:LANES * (hh + 1)].astype(BF16)
            qr = _rot(qb[:, 512 + LANES * hh:512 + LANES * (hh + 1)], ct_v, st_v, lane)
            qcat_ref[hh, :, LANES:2 * LANES] = qr.astype(BF16)
        kr = _rot(proj[:, 1664:1792], ct_v, st_v, lane)
        kcat_ref[:, 0:LANES] = ckvn.astype(BF16)
        kcat_ref[:, LANES:2 * LANES] = kr.astype(BF16)
        kcatt_ref[0:LANES, :] = ckvn.T.astype(BF16)
        kcatt_ref[LANES:2 * LANES, :] = kr.T.astype(BF16)

    col_spec = lambda rows: pl.BlockSpec((rows, ts), lambda i: (0, i))
    return pl.pallas_call(
        body, name="even_pre_fwd", grid=(S // ts,),
        in_specs=[_row_spec(ts, D_MODEL), _full_spec((3, D_MODEL)), _full_spec((1, D_MODEL)), _full_spec((EVEN_P, D_MODEL)),
                  _full_spec((1, LANES)), _full_spec((1, LANES)), _full_spec((LANES, LANES)),
                  _row_spec(ts, LANES), _row_spec(ts, LANES), _row_spec(ts, LANES), _row_spec(ts, LANES),
                  _full_spec((1, B_Q_LORA)), _full_spec((1, B_KV_LORA)), _full_spec((1536, B_Q_LORA)), _full_spec((512, 1024))],
        out_specs=[_row_spec(ts, D_MODEL), _row_spec(ts, EVEN_P), _row_spec(ts, 512), _row_spec(ts, 2 * LANES), _row_spec(ts, 2 * LANES),
                   pl.BlockSpec((B_HEADS, ts, 2 * LANES), lambda i: (0, i, 0)), _row_spec(ts, 2 * LANES),
                   col_spec(2 * LANES), col_spec(2 * LANES), col_spec(2 * LANES)],
        out_shape=[_sds((S, D_MODEL), BF16), _sds((S, EVEN_P), F32), _sds((S, 512), BF16), _sds((S, 2 * LANES), BF16),
                   _sds((S, 2 * LANES), BF16), _sds((B_HEADS, S, 2 * LANES), BF16), _sds((S, 2 * LANES), BF16),
                   _sds((2 * LANES, S), BF16), _sds((2 * LANES, S), BF16), _sds((2 * LANES, S), BF16)],
        compiler_params=_params(("arbitrary",)),
    )(x, mod, nw, wie, qn, kn, seg, ca, sa, ct, st, qln, kvln, wuq, wuk)


MLA_SCALE = (B_NOPE + B_ROPE) ** -0.5
LOG2E = 1.4426950408889634

def _row_lo():
    return lax.broadcasted_iota(jnp.int32, (LANES, 1), 0) < HEAD_DIM


def _stack_cols(vT, rlo):
    zero = jnp.zeros_like(vT)
    return jnp.concatenate([jnp.where(rlo, vT, zero), jnp.where(rlo, zero, vT)], axis=1)


def _stack_rows(v, lo):
    zero = jnp.zeros_like(v)
    return jnp.concatenate([jnp.where(lo, v, zero), jnp.where(lo, zero, v)], axis=0)


def _pick_halves_T(xT, rlo, t):
    return jnp.where(rlo, xT[:, 0:t], xT[:, t:2 * t]).T


def _side_split(refs, n_in, n_out, n_scratch, side):
    ns = side.n if side is not None else 0
    cuts = np.cumsum([0, n_in, ns, n_out, ns, n_scratch])
    return [refs[a:b] for a, b in zip(cuts[:-1], cuts[1:])] + [refs[cuts[-1]:]]


def _side_hooks(side, side_ins, side_outs, side_sems, step, total):
    if side is None:
        return lambda: None
    start, mid, end = side.phases(side_ins, side_outs, side_sems)
    pl.when(step == 0)(start)
    pl.when(step == total // 2)(mid)
    return lambda: pl.when(step == total - 1)(end)


def _side_specs(side):
    if side is None:
        return [], [], [], [], []
    return list(side.arrs), [_ANY] * side.n, [_ANY] * side.n, list(side.out_shapes), side.sem_shapes()


def _pp_fwd(q, k, vT, *, kdiv, tq, tk, sub, name, side=None):
    S = k.shape[0]; nb = q.shape[1] // LANES; nq = S // tq; nkv = S // tk; nsub = tk // sub

    def body(*refs):
        (q_ref, k_ref, vT_ref), side_ins, (o_ref, lse_ref), side_outs, (qs, m_s, l_s, acc), side_sems = _side_split(refs, 3, 2, 4, side)
        j = pl.program_id(2)
        rlo = _row_lo()
        step = (pl.program_id(0) * nq + pl.program_id(1)) * nkv + j
        side_end = _side_hooks(side, side_ins, side_outs, side_sems, step, nb * nq * nkv)

        @pl.when(j == 0)
        def _():
            qs[...] = _stack_cols(q_ref[...].astype(F32).T, rlo).astype(BF16)
            m_s[...] = jnp.full((1, 2 * tq), NEG, F32)
            l_s[...] = jnp.zeros((1, 2 * tq), F32)
            acc[...] = jnp.zeros((LANES, 2 * tq), F32)

        qsv = qs[...]
        m, l, a = m_s[...], l_s[...], acc[...]
        s_cur = _dot(k_ref[0:sub, :], qsv)
        for t in range(nsub):
            if t + 1 < nsub:
                s_next = _dot(k_ref[sub * (t + 1):sub * (t + 2), :], qsv)
            m_new = jnp.maximum(m, jnp.max(s_cur, axis=0, keepdims=True))
            alpha = jnp.exp(m - m_new)
            p = jnp.exp(s_cur - m_new)
            l = alpha * l + jnp.sum(p, axis=0, keepdims=True)
            a = alpha * a + _dot(vT_ref[:, sub * t:sub * (t + 1)], p.astype(BF16))
            m = m_new
            if t + 1 < nsub:
                s_cur = s_next
        m_s[...], l_s[...], acc[...] = m, l, a

        @pl.when(j == nkv - 1)
        def _():
            l_f = l_s[...]
            o_ref[...] = _pick_halves_T(acc[...] / l_f, rlo, tq).astype(BF16)
            lse_ref[0, 0] = m_s[...] + jnp.log(l_f)

        side_end()

    s_args, s_in, s_out, s_shapes, s_sems = _side_specs(side)
    return pl.pallas_call(
        body, name=name, grid=(nb, nq, nkv),
        in_specs=[pl.BlockSpec((tq, LANES), lambda b, i, j: (i, b)), pl.BlockSpec((tk, LANES), lambda b, i, j: (j, b // kdiv)),
                  pl.BlockSpec((LANES, tk), lambda b, i, j: (b // kdiv, j))] + s_in,
        out_specs=[pl.BlockSpec((tq, LANES), lambda b, i, j: (i, b)),
                   pl.BlockSpec((1, 1, 1, 2 * tq), lambda b, i, j: (b, i, 0, 0))] + s_out,
        out_shape=[_sds((S, nb * LANES), BF16), _sds((nb, nq, 1, 2 * tq), F32)] + s_shapes,
        scratch_shapes=[pltpu.VMEM((LANES, 2 * tq), BF16), pltpu.VMEM((1, 2 * tq), F32), pltpu.VMEM((1, 2 * tq), F32),
                        pltpu.VMEM((LANES, 2 * tq), F32)] + s_sems,
        compiler_params=_params(("arbitrary",) * 3))(q, k, vT, *s_args)


def _pp_bwd(q, k, kT, v, o, do, lse, *, kdiv, tq, tk, sub, name, side=None):
    S = k.shape[0]; nb = q.shape[1] // LANES; nkb = k.shape[1] // LANES; nq = S // tq; nkv = S // tk; nsub = tk // sub

    def body(*refs):
        ((q_ref, k_ref, kT_ref, v_ref, o_ref, do_ref, lse_ref), side_ins, (dq_ref, dk_ref, dv_ref), side_outs,
         (qsT, qs, dosT, dos, delta_s, dq_acc), side_sems) = _side_split(refs, 7, 3, 6, side)
        b, i, j = pl.program_id(0), pl.program_id(1), pl.program_id(2)
        rlo = _row_lo()
        lo = lax.broadcasted_iota(jnp.int32, (1, LANES), 1) < HEAD_DIM
        side_end = _side_hooks(side, side_ins, side_outs, side_sems, (b * nq + i) * nkv + j, nb * nq * nkv)

        @pl.when((b % kdiv == 0) & (i == 0) & (j == 0))
        def _():
            dk_ref[...] = jnp.zeros((S, LANES), F32)
            dv_ref[...] = jnp.zeros((S, LANES), F32)

        @pl.when(j == 0)
        def _():
            qv = q_ref[...]
            qs[...] = _stack_rows(qv, lo)
            qsT[...] = _stack_cols(qv.astype(F32).T, rlo).astype(BF16)
            dov = do_ref[...].astype(F32)
            dos[...] = _stack_rows(dov.astype(BF16), lo)
            dosT[...] = _stack_cols(dov.T, rlo).astype(BF16)
            prodT = (dov * o_ref[...].astype(F32)).T
            delta_s[...] = jnp.concatenate([jnp.sum(jnp.where(rlo, prodT, 0.0), axis=0, keepdims=True),
                                            jnp.sum(jnp.where(rlo, 0.0, prodT), axis=0, keepdims=True)], axis=1)
            dq_acc[...] = jnp.zeros((LANES, 2 * tq), F32)

        qsTv, dosTv, qsv, dosv = qsT[...], dosT[...], qs[...], dos[...]
        lse_v, delta_v = lse_ref[0, 0], delta_s[...]
        dqa = dq_acc[...]
        s_cur = _dot(k_ref[0:sub, :], qsTv)
        dp_cur = _dot(v_ref[0:sub, :], dosTv)
        for t in range(nsub):
            if t + 1 < nsub:
                s_next = _dot(k_ref[sub * (t + 1):sub * (t + 2), :], qsTv)
                dp_next = _dot(v_ref[sub * (t + 1):sub * (t + 2), :], dosTv)
            p = jnp.exp(s_cur - lse_v)
            ds = (p * (dp_cur - delta_v)).astype(BF16)
            rows = pl.ds(pl.multiple_of(j * tk + sub * t, sub), sub)
            dv_ref[rows, :] += _dot(p.astype(BF16), dosv)
            dk_ref[rows, :] += _dot(ds, qsv)
            dqa = dqa + _dot(kT_ref[:, sub * t:sub * (t + 1)], ds)
            if t + 1 < nsub:
                s_cur, dp_cur = s_next, dp_next
        dq_acc[...] = dqa

        @pl.when(j == nkv - 1)
        def _():
            dq_ref[...] = _pick_halves_T(dq_acc[...], rlo, tq)

        side_end()

    qmap = lambda b, i, j: (i, b)
    kmap = lambda b, i, j: (j, b // kdiv)
    res = lambda b, i, j: (0, b // kdiv)
    s_args, s_in, s_out, s_shapes, s_sems = _side_specs(side)
    return pl.pallas_call(
        body, name=name, grid=(nb, nq, nkv),
        in_specs=[pl.BlockSpec((tq, LANES), qmap), pl.BlockSpec((tk, LANES), kmap), pl.BlockSpec((LANES, tk), lambda b, i, j: (b // kdiv, j)),
                  pl.BlockSpec((tk, LANES), kmap), pl.BlockSpec((tq, LANES), qmap), pl.BlockSpec((tq, LANES), qmap),
                  pl.BlockSpec((1, 1, 1, 2 * tq), lambda b, i, j: (b, i, 0, 0))] + s_in,
        out_specs=[pl.BlockSpec((tq, LANES), qmap), pl.BlockSpec((S, LANES), res), pl.BlockSpec((S, LANES), res)] + s_out,
        out_shape=[_sds((S, nb * LANES), F32), _sds((S, nkb * LANES), F32), _sds((S, nkb * LANES), F32)] + s_shapes,
        scratch_shapes=[pltpu.VMEM((LANES, 2 * tq), BF16), pltpu.VMEM((2 * tq, LANES), BF16), pltpu.VMEM((LANES, 2 * tq), BF16),
                        pltpu.VMEM((2 * tq, LANES), BF16), pltpu.VMEM((1, 2 * tq), F32), pltpu.VMEM((LANES, 2 * tq), F32)] + s_sems,
        compiler_params=_params(("arbitrary",) * 3))(q, k, kT, v, o, do, lse, *s_args)


MLA_C = MLA_SCALE * LOG2E


def _mla_fwd(q, kcat, kcatT, *, tq, tk, sub):
    S = kcat.shape[0]; nq, nkv = S // tq, S // tk; R = B_HEADS * tq; nsub = tk // sub

    def body(q_ref, k_ref, vT_ref, o_ref, lse_ref, qT, m_s, l_s, acc):
        j = pl.program_id(1)

        @pl.when(j == 0)
        def _():
            qT[...] = q_ref[...].reshape(R, 2 * LANES).astype(F32).T.astype(BF16)
            m_s[...] = jnp.full((1, R), NEG, F32)
            l_s[...] = jnp.zeros((1, R), F32)
            acc[...] = jnp.zeros((LANES, R), F32)

        qTv = qT[...]
        m, l, a = m_s[...], l_s[...], acc[...]
        s_cur = _dot(k_ref[0:sub, :], qTv)
        for t in range(nsub):
            if t + 1 < nsub:
                s_next = _dot(k_ref[sub * (t + 1):sub * (t + 2), :], qTv)
            m_new = jnp.maximum(m, jnp.max(s_cur, axis=0, keepdims=True))
            alpha = jnp.exp2((m - m_new) * MLA_C)
            p = jnp.exp2((s_cur - m_new) * MLA_C)
            l = alpha * l + jnp.sum(p, axis=0, keepdims=True)
            a = alpha * a + _dot(vT_ref[:, sub * t:sub * (t + 1)], p.astype(BF16))
            m = m_new
            if t + 1 < nsub:
                s_cur = s_next
        m_s[...], l_s[...], acc[...] = m, l, a

        @pl.when(j == nkv - 1)
        def _():
            l_f = l_s[...]
            o_ref[...] = (acc[...] / l_f).T.reshape(B_HEADS, tq, LANES).astype(BF16)
            lse_ref[0] = m_s[...] * MLA_SCALE + jnp.log(l_f)

    return pl.pallas_call(
        body, name="mla_fwd", grid=(nq, nkv),
        in_specs=[pl.BlockSpec((B_HEADS, tq, 2 * LANES), lambda i, j: (0, i, 0)), pl.BlockSpec((tk, 2 * LANES), lambda i, j: (j, 0)),
                  pl.BlockSpec((LANES, tk), lambda i, j: (0, j))],
        out_specs=[pl.BlockSpec((B_HEADS, tq, LANES), lambda i, j: (0, i, 0)), pl.BlockSpec((1, 1, R), lambda i, j: (i, 0, 0))],
        out_shape=[_sds((B_HEADS, S, LANES), BF16), _sds((nq, 1, R), F32)],
        scratch_shapes=[pltpu.VMEM((2 * LANES, R), BF16), pltpu.VMEM((1, R), F32), pltpu.VMEM((1, R), F32), pltpu.VMEM((LANES, R), F32)],
        compiler_params=_params(("arbitrary", "arbitrary")))(q, kcat, kcatT)


def _mla_bwd(q, kcat, kcatT, o, do, lse, *, tq, tk, sub):
    S = kcat.shape[0]; nq, nkv = S // tq, S // tk; R = B_HEADS * tq; nsub = tk // sub

    def body(q_ref, k_ref, kT_ref, o_ref, do_ref, lse_ref, dq_ref, dk_ref, qT, dosT, dos, delta_s, dq_acc):
        i, j = pl.program_id(0), pl.program_id(1)

        @pl.when((i == 0) & (j == 0))
        def _():
            dk_ref[...] = jnp.zeros((S, 2 * LANES), F32)

        @pl.when(j == 0)
        def _():
            qT[...] = q_ref[...].reshape(R, 2 * LANES).astype(F32).T.astype(BF16)
            dov = do_ref[...].reshape(R, LANES).astype(F32)
            dos[...] = dov.astype(BF16)
            dosT[...] = dov.T.astype(BF16)
            delta_s[...] = jnp.sum((dov * o_ref[...].reshape(R, LANES).astype(F32)).T, axis=0, keepdims=True)
            dq_acc[...] = jnp.zeros((2 * LANES, R), F32)

        qTv, dosTv, dosv = qT[...], dosT[...], dos[...]
        qv = q_ref[...].reshape(R, 2 * LANES)
        lse_v, delta_v = lse_ref[0] * LOG2E, delta_s[...]
        dqa = dq_acc[...]
        s_cur = _dot(k_ref[0:sub, :], qTv)
        dp_cur = _dot(k_ref[0:sub, 0:LANES], dosTv)
        for t in range(nsub):
            if t + 1 < nsub:
                s_next = _dot(k_ref[sub * (t + 1):sub * (t + 2), :], qTv)
                dp_next = _dot(k_ref[sub * (t + 1):sub * (t + 2), 0:LANES], dosTv)
            p = jnp.exp2(s_cur * MLA_C - lse_v)
            ds = (p * (dp_cur - delta_v) * MLA_SCALE).astype(BF16)
            rows = pl.ds(pl.multiple_of(j * tk + sub * t, sub), sub)
            dk_ref[rows, :] += _dot(ds, qv)
            dk_ref[rows, 0:LANES] += _dot(p.astype(BF16), dosv)
            dqa = dqa + _dot(kT_ref[:, sub * t:sub * (t + 1)], ds)
            if t + 1 < nsub:
                s_cur, dp_cur = s_next, dp_next
        dq_acc[...] = dqa

        @pl.when(j == nkv - 1)
        def _():
            dq_ref[...] = dq_acc[...].T.reshape(B_HEADS, tq, 2 * LANES)

    hspec = lambda w: pl.BlockSpec((B_HEADS, tq, w), lambda i, j: (0, i, 0))
    return pl.pallas_call(
        body, name="mla_bwd", grid=(nq, nkv),
        in_specs=[hspec(2 * LANES), pl.BlockSpec((tk, 2 * LANES), lambda i, j: (j, 0)), pl.BlockSpec((2 * LANES, tk), lambda i, j: (0, j)),
                  hspec(LANES), hspec(LANES), pl.BlockSpec((1, 1, R), lambda i, j: (i, 0, 0))],
        out_specs=[hspec(2 * LANES), pl.BlockSpec((S, 2 * LANES), lambda i, j: (0, 0))],
        out_shape=[_sds((B_HEADS, S, 2 * LANES), F32), _sds((S, 2 * LANES), F32)],
        scratch_shapes=[pltpu.VMEM((2 * LANES, R), BF16), pltpu.VMEM((LANES, R), BF16), pltpu.VMEM((R, LANES), BF16),
                        pltpu.VMEM((1, R), F32), pltpu.VMEM((2 * LANES, R), F32)],
        compiler_params=_params(("arbitrary", "arbitrary")))(q, kcat, kcatT, o, do, lse)


def _win_start(i, tq, nk, S):
    return pl.multiple_of(jnp.clip(i * tq - WINDOW, 0, S - nk), LANES)


def _win_dist_table(S, tq):
    nk = min(tq + 2 * WINDOW, S)
    nq = S // tq
    r = np.arange(nk)[:, None]
    c = (np.arange(2 * tq) % tq)[None, :]
    tabs = []
    for rel in (0, WINDOW, (nq - 1) * tq - (S - nk)):
        dist = np.abs(rel + c - r).astype(np.float32)
        tabs.append(np.where(dist <= WINDOW, dist, np.float32(1e32)))
    return jnp.asarray(np.stack(tabs))


def _win_dist_spec(nk, tq, nq):
    return pl.BlockSpec((1, nk, 2 * tq), lambda b, i: (jnp.where(i == 0, 0, jnp.where(i == nq - 1, 2, 1)), 0, 0))


def _win_fwd(q, k, vT, dist, slope, sink, *, kdiv, tq, nbs, name):
    S = k.shape[0]; nb = q.shape[1] // LANES; nq = S // tq; nk = min(tq + 2 * WINDOW, S)
    assert nb % nbs == 0 and nbs % kdiv == 0
    kvw = (nbs // kdiv) * LANES

    def body(q_ref, k_ref, vT_ref, dist_ref, slope_ref, sink_ref, o_ref, lse_ref):
        i = pl.program_id(1)
        rlo = _row_lo()
        k0 = _win_start(i, tq, nk, S)
        kk, vv, dd = k_ref[pl.ds(k0, nk), :], vT_ref[:, pl.ds(k0, nk)], dist_ref[0]
        for u in range(nbs):
            kv = slice(LANES * (u // kdiv), LANES * (u // kdiv + 1))
            qsT = _stack_cols(q_ref[:, LANES * u:LANES * (u + 1)].astype(F32).T, rlo).astype(BF16)
            s = _dot(kk[:, kv], qsT) - slope_ref[u] * dd
            sk = sink_ref[u]
            m = jnp.maximum(jnp.max(s, axis=0, keepdims=True), sk)
            p = jnp.exp(s - m)
            l = jnp.sum(p, axis=0, keepdims=True) + jnp.exp(sk - m)
            o_ref[:, LANES * u:LANES * (u + 1)] = _pick_halves_T(_dot(vv[kv, :], p.astype(BF16)) / l, rlo, tq).astype(BF16)
            lse_ref[u, 0] = m + jnp.log(l)

    row_spec = pl.BlockSpec((nbs, 1, 2 * tq), lambda b, i: (b, 0, 0))
    return pl.pallas_call(
        body, name=name, grid=(nb // nbs, nq),
        in_specs=[pl.BlockSpec((tq, nbs * LANES), lambda b, i: (i, b)), pl.BlockSpec((S, kvw), lambda b, i: (0, b)),
                  pl.BlockSpec((kvw, S), lambda b, i: (b, 0)), _win_dist_spec(nk, tq, nq), row_spec, row_spec],
        out_specs=[pl.BlockSpec((tq, nbs * LANES), lambda b, i: (i, b)), pl.BlockSpec((nbs, 1, 1, 2 * tq), lambda b, i: (b, i, 0, 0))],
        out_shape=[_sds((S, nb * LANES), BF16), _sds((nb, nq, 1, 2 * tq), F32)],
        compiler_params=_params(("arbitrary", "arbitrary")))(q, k, vT, dist, slope, sink)


def _win_bwd(q, k, kT, v, o, do, lse, dist, slope, sink, *, kdiv, tq, nbs, name):
    S = k.shape[0]; nb = q.shape[1] // LANES; nkb = k.shape[1] // LANES; nq = S // tq; nk = min(tq + 2 * WINDOW, S)
    assert nb % nbs == 0 and nbs % kdiv == 0
    nkv = nbs // kdiv
    kvw = nkv * LANES

    def body(q_ref, k_ref, kT_ref, v_ref, o_ref, do_ref, lse_ref, dist_ref, slope_ref, sink_ref, dq_ref, dk_ref, dv_ref, dsink_ref, ds_acc):
        i = pl.program_id(1)
        rlo = _row_lo()
        lo = lax.broadcasted_iota(jnp.int32, (1, LANES), 1) < HEAD_DIM

        @pl.when(i == 0)
        def _():
            dk_ref[...] = jnp.zeros((S, kvw), F32)
            dv_ref[...] = jnp.zeros((S, kvw), F32)
            ds_acc[...] = jnp.zeros((nbs, 2 * tq), F32)

        k0 = _win_start(i, tq, nk, S)
        rows = pl.ds(k0, nk)
        kk_all, vv_all, kkT_all, dd = k_ref[rows, :], v_ref[rows, :], kT_ref[:, rows], dist_ref[0]
        dv_sum, dk_sum = [None] * nkv, [None] * nkv
        for u in range(nbs):
            g = u // kdiv
            kv = slice(LANES * g, LANES * (g + 1))
            kk, vv, kkT = kk_all[:, kv], vv_all[:, kv], kkT_all[kv, :]
            cols = slice(LANES * u, LANES * (u + 1))
            qv = q_ref[:, cols]
            qs = _stack_rows(qv, lo)
            qsT = _stack_cols(qv.astype(F32).T, rlo).astype(BF16)
            dov = do_ref[:, cols].astype(F32)
            dos = _stack_rows(dov.astype(BF16), lo)
            dosT = _stack_cols(dov.T, rlo).astype(BF16)
            prodT = (dov * o_ref[:, cols].astype(F32)).T
            delta = jnp.concatenate([jnp.sum(jnp.where(rlo, prodT, 0.0), axis=0, keepdims=True),
                                     jnp.sum(jnp.where(rlo, 0.0, prodT), axis=0, keepdims=True)], axis=1)
            lse_v = lse_ref[u, 0]
            ds_acc[u:u + 1, :] += -jnp.exp(sink_ref[u] - lse_v) * delta
            p = jnp.exp(_dot(kk, qsT) - slope_ref[u] * dd - lse_v)
            ds = (p * (_dot(vv, dosT) - delta)).astype(BF16)
            dv_u, dk_u = _dot(p.astype(BF16), dos), _dot(ds, qs)
            dv_sum[g] = dv_u if dv_sum[g] is None else dv_sum[g] + dv_u
            dk_sum[g] = dk_u if dk_sum[g] is None else dk_sum[g] + dk_u
            dq_ref[:, cols] = (_pick_halves_T(_dot(kkT, ds), rlo, tq) * 0.125).astype(BF16)
        dv_ref[rows, :] += jnp.concatenate(dv_sum, axis=1)
        dk_ref[rows, :] += jnp.concatenate(dk_sum, axis=1)

        @pl.when(i == nq - 1)
        def _():
            acc = ds_acc[...]
            for u in range(nbs):
                dsink_ref[u] = jnp.concatenate(
                    [jnp.broadcast_to(jnp.sum(acc[u:u + 1, 0:tq], axis=1, keepdims=True), (1, LANES)),
                     jnp.broadcast_to(jnp.sum(acc[u:u + 1, tq:2 * tq], axis=1, keepdims=True), (1, LANES)),
                     jnp.zeros((6, LANES), F32)], axis=0)

    qmap = lambda b, i: (i, b)
    kv_spec = pl.BlockSpec((S, kvw), lambda b, i: (0, b))
    row_spec = pl.BlockSpec((nbs, 1, 2 * tq), lambda b, i: (b, 0, 0))
    wide = pl.BlockSpec((tq, nbs * LANES), qmap)
    return pl.pallas_call(
        body, name=name, grid=(nb // nbs, nq),
        in_specs=[wide, kv_spec, pl.BlockSpec((kvw, S), lambda b, i: (b, 0)), kv_spec, wide, wide,
                  pl.BlockSpec((nbs, 1, 1, 2 * tq), lambda b, i: (b, i, 0, 0)), _win_dist_spec(nk, tq, nq), row_spec, row_spec],
        out_specs=[wide, kv_spec, kv_spec, pl.BlockSpec((nbs, 8, LANES), lambda b, i: (b, 0, 0))],
        out_shape=[_sds((S, nb * LANES), BF16), _sds((S, nkb * LANES), F32), _sds((S, nkb * LANES), F32), _sds((nb, 8, LANES), F32)],
        scratch_shapes=[pltpu.VMEM((nbs, 2 * tq), F32)],
        compiler_params=_params(("arbitrary", "arbitrary")))(q, k, kT, v, o, do, lse, dist, slope, sink)


def _sum_rows(v):
    return jnp.sum(v, axis=0, keepdims=True)


def _norm_mod_bwd(dh, xv, mod_ref, nw_ref, stats_ref):
    r = _rms(xv)
    xn = xv * r
    nw = nw_ref[...]
    stats_ref[0:1, :] += _sum_rows(dh)
    stats_ref[1:2, :] += _sum_rows(dh * (xn * nw))
    dn = dh * (1.0 + mod_ref[1:2, :])
    stats_ref[2:3, :] += _sum_rows(dn * xn)
    return _rms_bwd(xv, r, dn * nw)


def _even_gate_specs(ts):
    return [pl.BlockSpec((ts, 256), lambda i, c=c: (i, c)) for c in (3, 4, 7, 8)]


def _even_post_fwd(oa, olat, proj, x, gate, wuv, woe):
    S = x.shape[0]
    ts = min(ROW_TILE, S)

    def body(oa_ref, ol_ref, ga0_ref, ga1_ref, gb0_ref, gb1_ref, x_ref, gate_ref, wuv_ref, woe_ref, y_ref, x1_ref):
        sa, _ = _silu_and_grad(jnp.concatenate([ga0_ref[...], ga1_ref[...]], axis=1))
        sb, _ = _silu_and_grad(jnp.concatenate([gb0_ref[...], gb1_ref[...]], axis=1))
        olc = jnp.concatenate([ol_ref[hh] for hh in range(B_HEADS)], axis=1).astype(BF16)
        ob = _dot(olc, wuv_ref[...])
        mix = jnp.concatenate([oa_ref[...] * sa, ob * sb], axis=1).astype(BF16)
        y = _dot(mix, woe_ref[...])
        y_ref[...] = y.astype(BF16)
        x1_ref[...] = x_ref[...] + gate_ref[...] * y

    return pl.pallas_call(
        body, name="even_post_fwd", grid=(S // ts,),
        in_specs=[_row_spec(ts, 512), pl.BlockSpec((B_HEADS, ts, LANES), lambda i: (0, i, 0))] + _even_gate_specs(ts) +
                 [_row_spec(ts, D_MODEL), _full_spec((1, D_MODEL)), _full_spec((1024, 512)), _full_spec((1024, D_MODEL))],
        out_specs=[_row_spec(ts, D_MODEL), _row_spec(ts, D_MODEL)],
        out_shape=[_sds((S, D_MODEL), BF16), _sds((S, D_MODEL), F32)],
        compiler_params=_params(("arbitrary",)),
    )(oa, olat, proj, proj, proj, proj, x, gate, wuv, woe)


def _odd_pre_fwd(x, mod, nw, wio):
    S = x.shape[0]
    ts = min(ROW_TILE, S)

    def body(x_ref, mod_ref, nw_ref, wio_ref, h_ref, g_ref, q_ref, k_ref, v_ref, kt_ref, vt_ref):
        xv = x_ref[...]
        h = (xv * _rms(xv) * nw_ref[...]) * (1.0 + mod_ref[1:2, :]) + mod_ref[0:1, :]
        hb = h.astype(BF16)
        h_ref[...] = hb
        proj = jnp.concatenate([_dot(hb, wio_ref[p]) for p in range(N_CHIPS)], axis=1)
        g_ref[...] = proj[:, 1536:2560]
        q_ref[...] = (proj[:, 0:1024] * 0.125).astype(BF16)
        lane = _lane_iota()
        k_v = jnp.concatenate([_dup_heads(proj[:, 1024 + LANES * j:1024 + LANES * (j + 1)], lane) for j in range(2)], axis=1)
        v_v = jnp.concatenate([_dup_heads(proj[:, 1280 + LANES * j:1280 + LANES * (j + 1)], lane) for j in range(2)], axis=1)
        k_ref[...] = k_v.astype(BF16)
        v_ref[...] = v_v.astype(BF16)
        kt_ref[...] = k_v.T.astype(BF16)
        vt_ref[...] = v_v.T.astype(BF16)

    col_spec = pl.BlockSpec((512, ts), lambda i: (0, i))
    return pl.pallas_call(
        body, name="odd_pre_fwd", grid=(S // ts,),
        in_specs=[_row_spec(ts, D_MODEL), _full_spec((3, D_MODEL)), _full_spec((1, D_MODEL)),
                  _full_spec((N_CHIPS, D_MODEL, ODD_IN // N_CHIPS))],
        out_specs=[_row_spec(ts, D_MODEL), _row_spec(ts, 1024), _row_spec(ts, 1024), _row_spec(ts, 512), _row_spec(ts, 512),
                   col_spec, col_spec],
        out_shape=[_sds((S, D_MODEL), BF16), _sds((S, 1024), F32), _sds((S, 1024), BF16), _sds((S, 512), BF16),
                   _sds((S, 512), BF16), _sds((512, S), BF16), _sds((512, S), BF16)],
        compiler_params=_params(("arbitrary",)),
    )(x, mod, nw, wio)


def _odd_post(oc, g, x1, gate, woo, fw, tgt):
    S = x1.shape[0]
    ts = min(ROW_TILE, S)
    nsteps = S // ts

    def body(oc_ref, g_ref, x_ref, gate_ref, woo_ref, fw_ref, tgt_ref, doc_ref, dgc_ref, dx2_ref, dwoo_out, stats_ref, dwoo_ref):
        @pl.when(pl.program_id(0) == 0)
        def _():
            dwoo_ref[...] = jnp.zeros((D_MODEL, D_MODEL), F32)
            stats_ref[...] = jnp.zeros((8, D_MODEL), F32)

        ocv = oc_ref[...]
        sg, dsg = _silu_and_grad(g_ref[...])
        mix = (ocv * sg).astype(BF16)
        woo_v = woo_ref[...]
        y = _dot(mix, woo_v)
        gate_v = gate_ref[...]
        x2 = x_ref[...] + gate_v * y
        r = _rms(x2)
        fw_v = fw_ref[...]
        xn = x2 * r
        err = xn * fw_v - tgt_ref[...]
        dout = err * (1.0 / D_MODEL)
        dx2 = _rms_bwd(x2, r, dout * fw_v)
        dx2_ref[...] = dx2
        stats_ref[0:1, :] += _sum_rows(dout * xn)
        stats_ref[1:2, :] += _sum_rows(dx2 * y)
        loss_t = 0.5 * jnp.sum(_sum_rows(err * dout), axis=-1, keepdims=True)
        stats_ref[2:3, :] += jnp.broadcast_to(loss_t, (1, D_MODEL))
        dy = (gate_v * dx2).astype(BF16)
        dmix = _dot_nt(dy, woo_v)
        dwoo_ref[...] += _dot_tn(mix, dy)
        doc_ref[...] = (dmix * sg).astype(BF16)
        dgc_ref[...] = (dmix * ocv * dsg).astype(BF16)

        @pl.when(pl.program_id(0) == nsteps - 1)
        def _():
            dwoo_out[...] = dwoo_ref[...].astype(BF16)

    return pl.pallas_call(
        body, name="odd_post", grid=(nsteps,),
        in_specs=[_row_spec(ts, D_MODEL), _row_spec(ts, D_MODEL), _row_spec(ts, D_MODEL), _full_spec((1, D_MODEL)),
                  _full_spec((D_MODEL, D_MODEL)), _full_spec((1, D_MODEL)), _row_spec(ts, D_MODEL)],
        out_specs=[_row_spec(ts, D_MODEL), _row_spec(ts, D_MODEL), _row_spec(ts, D_MODEL),
                   _full_spec((D_MODEL, D_MODEL), single=False), _full_spec((8, D_MODEL), single=False)],
        out_shape=[_sds((S, D_MODEL), BF16), _sds((S, D_MODEL), BF16), _sds((S, D_MODEL), F32), _sds((D_MODEL, D_MODEL), BF16),
                   _sds((8, D_MODEL), F32)],
        scratch_shapes=[pltpu.VMEM((D_MODEL, D_MODEL), F32)],
        compiler_params=_params(("arbitrary",)),
    )(oc, g, x1, gate, woo, fw, tgt)


def _odd_pre_bwd(dq, dk, dv, dgc, h, x, dx_res, mod, nw, wio):
    S = x.shape[0]
    ts = min(IN_PROJ_ROW_TILE, S)
    nsteps = S // ts
    wsh = ODD_IN // N_CHIPS

    def body(dq_ref, dk_ref, dv_ref, dgc_ref, h_ref, x_ref, dxr_ref, mod_ref, nw_ref, wio_ref, dx_ref, dw_ref, stats_ref, dw_acc):
        @pl.when(pl.program_id(0) == 0)
        def _():
            dw_acc[...] = jnp.zeros((N_CHIPS, D_MODEL, wsh), F32)
            stats_ref[...] = jnp.zeros((8, D_MODEL), F32)

        lane = _lane_iota()
        dkv = [_fold_heads(r[:, 2 * LANES * j:2 * LANES * (j + 1)], lane).astype(BF16) for r in (dk_ref, dv_ref) for j in range(2)]
        dproj = jnp.concatenate([dq_ref[...]] + dkv + [dgc_ref[...]], axis=1)
        hv = h_ref[...]
        dh = None
        for p in range(N_CHIPS):
            dp_cols = dproj[:, wsh * p:wsh * (p + 1)]
            part = _dot_nt(dp_cols, wio_ref[p])
            dh = part if dh is None else dh + part
            dw_acc[p] += _dot_tn(hv, dp_cols)
        dx_ref[...] = dxr_ref[...] + _norm_mod_bwd(dh, x_ref[...], mod_ref, nw_ref, stats_ref)

        @pl.when(pl.program_id(0) == nsteps - 1)
        def _():
            dw_ref[...] = dw_acc[...].astype(BF16)

    return pl.pallas_call(
        body, name="odd_pre_bwd", grid=(nsteps,),
        in_specs=[_row_spec(ts, 1024), _row_spec(ts, 512), _row_spec(ts, 512), _row_spec(ts, 1024), _row_spec(ts, D_MODEL),
                  _row_spec(ts, D_MODEL), _row_spec(ts, D_MODEL), _full_spec((3, D_MODEL)), _full_spec((1, D_MODEL)),
                  _full_spec((N_CHIPS, D_MODEL, wsh))],
        out_specs=[_row_spec(ts, D_MODEL), _full_spec((N_CHIPS, D_MODEL, wsh), single=False), _full_spec((8, D_MODEL), single=False)],
        out_shape=[_sds((S, D_MODEL), F32), _sds((N_CHIPS, D_MODEL, wsh), BF16), _sds((8, D_MODEL), F32)],
        scratch_shapes=[pltpu.VMEM((N_CHIPS, D_MODEL, wsh), F32)],
        compiler_params=_params(("arbitrary",)),
    )(dq, dk, dv, dgc, h, x, dx_res, mod, nw, wio)


def _even_post_bwd(dx1, y, oa, olat, proj, gate, wuv, woe):
    S = dx1.shape[0]
    ts = min(ROW_TILE, S)
    nsteps = S // ts

    def body(dx_ref, y_ref, oa_ref, ol_ref, ga0_ref, ga1_ref, gb0_ref, gb1_ref, gate_ref, wuv_ref, woe_ref,
             doa_ref, dga_ref, dgb_ref, dol_ref, dwoe_out, dwuv_ref, stats_ref, dwoe_ref):
        @pl.when(pl.program_id(0) == 0)
        def _():
            dwoe_ref[...] = jnp.zeros((D_MODEL, D_MODEL), F32)
            dwuv_ref[...] = jnp.zeros((1024, 512), F32)
            stats_ref[...] = jnp.zeros((8, D_MODEL), F32)

        dxv = dx_ref[...]
        stats_ref[0:1, :] += _sum_rows(dxv * y_ref[...])
        dy = (gate_ref[...] * dxv).astype(BF16)
        sa, dsa = _silu_and_grad(jnp.concatenate([ga0_ref[...], ga1_ref[...]], axis=1))
        sb, dsb = _silu_and_grad(jnp.concatenate([gb0_ref[...], gb1_ref[...]], axis=1))
        olc = jnp.concatenate([ol_ref[hh] for hh in range(B_HEADS)], axis=1).astype(BF16)
        wuv_v = wuv_ref[...]
        ob = _dot(olc, wuv_v)
        oav = oa_ref[...]
        mix = jnp.concatenate([oav * sa, ob * sb], axis=1).astype(BF16)
        dmix = _dot_nt(dy, woe_ref[...])
        dwoe_ref[...] += _dot_tn(mix, dy)
        dma, dmb = dmix[:, 0:512], dmix[:, 512:1024]
        doa_ref[...] = (dma * sa).astype(BF16)
        dga_ref[...] = (dma * oav * dsa).astype(BF16)
        dgb_ref[...] = (dmb * ob * dsb).astype(BF16)
        dob = (dmb * sb).astype(BF16)
        dol = _dot_nt(dob, wuv_v)
        dwuv_ref[...] += _dot_tn(olc, dob)
        for hh in range(B_HEADS):
            dol_ref[hh] = dol[:, LANES * hh:LANES * (hh + 1)].astype(BF16)

        @pl.when(pl.program_id(0) == nsteps - 1)
        def _():
            dwoe_out[...] = dwoe_ref[...].astype(BF16)

    head_spec = pl.BlockSpec((B_HEADS, ts, LANES), lambda i: (0, i, 0))
    return pl.pallas_call(
        body, name="even_post_bwd", grid=(nsteps,),
        in_specs=[_row_spec(ts, D_MODEL), _row_spec(ts, D_MODEL), _row_spec(ts, 512), head_spec] + _even_gate_specs(ts) +
                 [_full_spec((1, D_MODEL)), _full_spec((1024, 512)), _full_spec((1024, D_MODEL))],
        out_specs=[_row_spec(ts, 512), _row_spec(ts, 512), _row_spec(ts, 512), head_spec,
                   _full_spec((D_MODEL, D_MODEL), single=False), _full_spec((1024, 512), single=False),
                   _full_spec((8, D_MODEL), single=False)],
        out_shape=[_sds((S, 512), BF16), _sds((S, 512), BF16), _sds((S, 512), BF16), _sds((B_HEADS, S, LANES), BF16),
                   _sds((D_MODEL, D_MODEL), BF16), _sds((1024, 512), F32), _sds((8, D_MODEL), F32)],
        scratch_shapes=[pltpu.VMEM((D_MODEL, D_MODEL), F32)],
        compiler_params=_params(("arbitrary",)),
    )(dx1, y, oa, olat, proj, proj, proj, proj, gate, wuv, woe)


def _even_pre_bwd(x, h, proj, dqa, dka, dva, dga, dgb, dqcat, dkcat, dx_res, mod, nw, wie, qn, kn, seg, ca, sa, ct, st,
                  qln, kvln, wuq, wuk):
    S = x.shape[0]
    ts = min(IN_PROJ_ROW_TILE, S)
    nsteps = S // ts

    def body(x_ref, h_ref, proj_ref, dqa_ref, dka_ref, dva_ref, dga_ref, dgb_ref, dqc_ref, dkc_ref, dxr_ref, mod_ref, nw_ref,
             wie_ref, qn_ref, kn_ref, seg_ref, ca_ref, sa_ref, ct_ref, st_ref, qln_ref, kvln_ref, wuq_ref, wuk_ref,
             dx_ref, dwie_out, dwuq_out, dwuk_out, stats_ref, nstats_ref, dwie_ref, dwuq_ref, dwuk_ref, stage):
        @pl.when(pl.program_id(0) == 0)
        def _():
            dwie_ref[...] = jnp.zeros((EVEN_P, D_MODEL), F32)
            dwuq_ref[...] = jnp.zeros((1536, B_Q_LORA), F32)
            dwuk_ref[...] = jnp.zeros((512, 1024), F32)
            stats_ref[...] = jnp.zeros((8, D_MODEL), F32)
            nstats_ref[...] = jnp.zeros((8, 256), F32)

        lane = _lane_iota()
        ca_v, sa_v, ct_v, st_v = ca_ref[...], sa_ref[...], ct_ref[...], st_ref[...]
        seg_v = seg_ref[...]

        def head_norm_bwd(xc, dy, w):
            r = lax.rsqrt(_seg_mean(xc * xc, seg_v) + EPS)
            g = dy * w
            dxc = r * g - xc * (r * r * r) * _seg_mean(xc * g, seg_v)
            return dxc, _sum_rows(dy * (xc * r))

        pieces = []
        dqn = jnp.zeros((1, LANES), F32)
        for cb in range(4):
            sl = slice(LANES * cb, LANES * (cb + 1))
            dy = _rot_bwd(dqa_ref[:, sl] * 0.125, ca_v, sa_v, lane)
            dxc, dw = head_norm_bwd(proj_ref[:, sl], dy, qn_ref[...])
            pieces.append(dxc)
            dqn = dqn + dw
        dxc, dkn = head_norm_bwd(proj_ref[:, 512:640], _rot_bwd(_fold_heads(dka_ref[...], lane), ca_v, sa_v, lane), kn_ref[...])
        pieces += [dxc, _fold_heads(dva_ref[...], lane), dga_ref[...]]
        nstats_ref[0:1, 0:LANES] += dqn + pltpu.roll(dqn, HEAD_DIM, 1)
        nstats_ref[1:2, 0:LANES] += dkn + pltpu.roll(dkn, HEAD_DIM, 1)

        cq = proj_ref[:, 1280:1536]
        rq = _rms(cq)
        cqn_f = cq * rq
        qln_v = qln_ref[...]
        cqn = (cqn_f * qln_v).astype(BF16)
        wuq_v, wuk_v = wuq_ref[...], wuk_ref[...]
        qnope = _dot_nt(cqn, wuq_v[0:512, :]).astype(BF16)
        dqlat = jnp.concatenate([dqc_ref[hh, :, 0:LANES] for hh in range(B_HEADS)], axis=1).astype(BF16)
        dqnope = _dot_nt(dqlat, wuk_v)
        dwuk_ref[...] += _dot_tn(qnope, dqlat)
        dqr = [_rot_bwd(dqc_ref[hh, :, LANES:2 * LANES], ct_v, st_v, lane) for hh in range(B_HEADS)]
        dqb = jnp.concatenate([dqnope] + dqr, axis=1).astype(BF16)
        dcqn = _dot(dqb, wuq_v)
        dwuq_ref[...] += _dot_tn(dqb, cqn)
        nstats_ref[2:3, :] += _sum_rows(dcqn * cqn_f)
        dcq = _rms_bwd(cq, rq, dcqn * qln_v)
        ckv = proj_ref[:, 1536:1664]
        rk = _rms(ckv)
        dckvn = dkc_ref[:, 0:LANES]
        nstats_ref[3:4, 0:LANES] += _sum_rows(dckvn * (ckv * rk))
        dckv = _rms_bwd(ckv, rk, dckvn * kvln_ref[...])
        dkr = _rot_bwd(dkc_ref[:, LANES:2 * LANES], ct_v, st_v, lane)
        pieces += [dcq, dckv, dkr, dgb_ref[...]]
        dproj = jnp.concatenate([piece.astype(BF16) for piece in pieces], axis=1)
        dh = _dot(dproj, wie_ref[...])
        dwie_ref[...] += _dot_tn(dproj, h_ref[...])
        dx_ref[...] = dxr_ref[...] + _norm_mod_bwd(dh, x_ref[...], mod_ref, nw_ref, stats_ref)

        @pl.when(pl.program_id(0) == nsteps - 1)
        def _():
            for r0 in range(0, EVEN_P, 256):
                stage[...] = dwie_ref[r0:r0 + 256, :].astype(BF16)
                pltpu.sync_copy(stage, dwie_out.at[pl.ds(r0, 256), :])
            pltpu.sync_copy(dwuq_ref, dwuq_out)
            pltpu.sync_copy(dwuk_ref, dwuk_out)

    return pl.pallas_call(
        body, name="even_pre_bwd", grid=(nsteps,),
        in_specs=[_row_spec(ts, D_MODEL), _row_spec(ts, D_MODEL), _row_spec(ts, EVEN_P), _row_spec(ts, 512), _row_spec(ts, 2 * LANES),
                  _row_spec(ts, 2 * LANES), _row_spec(ts, 512), _row_spec(ts, 512),
                  pl.BlockSpec((B_HEADS, ts, 2 * LANES), lambda i: (0, i, 0)), _row_spec(ts, 2 * LANES), _row_spec(ts, D_MODEL),
                  _full_spec((3, D_MODEL)), _full_spec((1, D_MODEL)), _full_spec((EVEN_P, D_MODEL)),
                  _full_spec((1, LANES)), _full_spec((1, LANES)), _full_spec((LANES, LANES)),
                  _row_spec(ts, LANES), _row_spec(ts, LANES), _row_spec(ts, LANES), _row_spec(ts, LANES),
                  _full_spec((1, B_Q_LORA)), _full_spec((1, B_KV_LORA)), _full_spec((1536, B_Q_LORA)), _full_spec((512, 1024))],
        out_specs=[_row_spec(ts, D_MODEL), _ANY, _ANY, _ANY, _full_spec((8, D_MODEL), single=False), _full_spec((8, 256), single=False)],
        out_shape=[_sds((S, D_MODEL), F32), _sds((EVEN_P, D_MODEL), BF16), _sds((1536, B_Q_LORA), F32), _sds((512, 1024), F32),
                   _sds((8, D_MODEL), F32), _sds((8, 256), F32)],
        scratch_shapes=[pltpu.VMEM((EVEN_P, D_MODEL), F32), pltpu.VMEM((1536, B_Q_LORA), F32), pltpu.VMEM((512, 1024), F32),
                        pltpu.VMEM((256, D_MODEL), BF16)],
        compiler_params=_params(("arbitrary",)),
    )(x, h, proj, dqa, dka, dva, dga, dgb, dqcat, dkcat, dx_res, mod, nw, wie, qn, kn, seg, ca, sa, ct, st, qln, kvln, wuq, wuk)


def _ada_fwd(c_all, w, b):
    n = w.shape[2]

    def body(c_ref, w_ref, b_ref, o_ref):
        cv = c_ref[...]
        o_ref[0] = _dot_f32(cv * _sigmoid(cv), w_ref[0]) + b_ref[0]

    return pl.pallas_call(
        body, name="ada_fwd", grid=(2,),
        in_specs=[pl.BlockSpec((N_DEV, D_MODEL), lambda l: (0, 0)), pl.BlockSpec((1, D_MODEL, n), lambda l: (l, 0, 0)),
                  pl.BlockSpec((1, 1, n), lambda l: (l, 0, 0))],
        out_specs=pl.BlockSpec((1, N_DEV, n), lambda l: (l, 0, 0)),
        out_shape=_sds((2, N_DEV, n), F32),
        compiler_params=_params(("arbitrary",)),
    )(c_all, w, b)


def _ada_bwd(c_all_t, dmod):
    n = dmod.shape[2]

    def body(c_ref, d_ref, o_ref):
        cv = c_ref[...]
        act = cv * _sigmoid(cv)
        dv = d_ref[0]
        acc = act[:, 0:1] * dv[0:1, :]
        for bb in range(1, N_DEV):
            acc = acc + act[:, bb:bb + 1] * dv[bb:bb + 1, :]
        o_ref[0] = acc

    return pl.pallas_call(
        body, name="ada_bwd", grid=(2,),
        in_specs=[pl.BlockSpec((D_MODEL, N_DEV), lambda l: (0, 0)), pl.BlockSpec((1, N_DEV, n), lambda l: (l, 0, 0))],
        out_specs=pl.BlockSpec((1, D_MODEL, n), lambda l: (l, 0, 0)),
        out_shape=_sds((2, D_MODEL, n), F32),
        compiler_params=_params(("arbitrary",)),
    )(c_all_t, dmod)


ADAM_ROW_TILE = 512


def _adam_update(g, w, m, v):
    m_new = ADAM_B1 * m + (1.0 - ADAM_B1) * g
    v_new = ADAM_B2 * v + (1.0 - ADAM_B2) * jnp.square(g)
    m_hat = m_new / (1.0 - ADAM_B1 ** ADAM_STEP)
    v_hat = v_new / (1.0 - ADAM_B2 ** ADAM_STEP)
    return -ADAM_LR * (m_hat / (jnp.sqrt(v_hat) + ADAM_EPS) + ADAM_WD * w), m_new, v_new


SMALL_ROWS = dict(dmod=(0, D_MODEL), norm_w=(6, D_MODEL), final_norm=(8, D_MODEL), a_q_norm=(9, HEAD_DIM), a_k_norm=(10, HEAD_DIM),
                  b_q_lora_norm=(11, B_Q_LORA), b_kv_lora_norm=(12, B_KV_LORA), c_sink=(13, C_HEADS))
SMALL_WEIGHTS = ("ada_b", "norm_w", "final_norm", "a_q_norm", "a_k_norm", "b_q_lora_norm", "b_kv_lora_norm", "c_sink")
LOSS_ROW = 14


def _pack_small(res):
    def padded(v):
        return jnp.concatenate([v, jnp.zeros((v.shape[0], D_MODEL - v.shape[1]), F32)], axis=1)

    rows = [res["dmod"].reshape(6, D_MODEL), res["norm_w"], res["final_norm"].reshape(1, D_MODEL)]
    rows += [padded(res[k]) for k in ("a_q_norm", "a_k_norm", "b_q_lora_norm", "b_kv_lora_norm", "c_sink")]
    return jnp.concatenate(rows + [res["loss_row"], jnp.zeros((1, D_MODEL), F32)], axis=0)


def _adam_small(parts, ws, ms, vs):
    nw = len(SMALL_WEIGHTS)

    def body(*refs):
        p_ref = refs[0]
        w_refs, m_refs, v_refs = refs[1:1 + nw], refs[1 + nw:1 + 2 * nw], refs[1 + 2 * nw:1 + 3 * nw]
        outs = refs[1 + 3 * nw:]
        g_all = p_ref[0]
        for k in range(1, N_DEV):
            g_all = g_all + p_ref[k]
        for idx, name in enumerate(SMALL_WEIGHTS):
            if name == "ada_b":
                g = jnp.concatenate([jnp.concatenate([g_all[3 * l + t:3 * l + t + 1] for t in range(3)], axis=1) for l in range(2)],
                                    axis=0)
            else:
                row, width = SMALL_ROWS[name]
                g = g_all[row:row + w_refs[idx].shape[0], 0:width]
            d, m_new, v_new = _adam_update(g, w_refs[idx][...], m_refs[idx][...], v_refs[idx][...])
            outs[4 * idx][...], outs[4 * idx + 1][...], outs[4 * idx + 2][...], outs[4 * idx + 3][...] = g, d, m_new, v_new
        outs[4 * nw][...] = g_all[LOSS_ROW:LOSS_ROW + 1, 0:LANES]

    out_shape = []
    for w in ws:
        out_shape += [_sds(w.shape, F32)] * 4
    out_shape.append(_sds((1, LANES), F32))
    return pl.pallas_call(body, name="adam_small", out_shape=out_shape,
                          compiler_params=pltpu.CompilerParams(vmem_limit_bytes=VMEM_LIMIT))(parts, *ws, *ms, *vs)


def _adam(parts, w, m, v, name, by_columns=False):
    P, R, C = parts.shape
    if by_columns:
        tr, tc = R, 256
    else:
        tr, tc = (R if R <= ADAM_ROW_TILE else ADAM_ROW_TILE), C
    assert R % tr == 0 and C % tc == 0

    def body(p_ref, w_ref, m_ref, v_ref, g_ref, d_ref, nm_ref, nv_ref):
        g = p_ref[0].astype(F32)
        for k in range(1, P):
            g = g + p_ref[k].astype(F32)
        g_ref[...] = g
        d_ref[...], nm_ref[...], nv_ref[...] = _adam_update(g, w_ref[...], m_ref[...], v_ref[...])

    tile = (lambda i: (0, i)) if by_columns else (lambda i: (i, 0))
    spec = pl.BlockSpec((tr, tc), tile)
    return pl.pallas_call(
        body, name=name, grid=(C // tc if by_columns else R // tr,),
        in_specs=[pl.BlockSpec((P, tr, tc), lambda i: (0,) + tile(i)), spec, spec, spec],
        out_specs=[spec, spec, spec, spec], out_shape=[_sds((R, C), F32)] * 4,
        compiler_params=_params(("arbitrary",)),
    )(parts, w, m, v)


_ANY = pl.BlockSpec(memory_space=pl.ANY)
CHIP_FLIPS = ((1, 0), (0, 1), (1, 1))
DEV_FLIPS = tuple((dx, dy, dc) for dx in (0, 1) for dy in (0, 1) for dc in (0, 1) if dx + dy + dc)


def _flip(a, d):
    return a if d == 0 else 1 - a


def _my_place():
    return lax.axis_index("x"), lax.axis_index("y"), lax.axis_index("c")


def _gather8_copies(ins, outs, send_sems, recv_sems, loc_sems):
    x, y, c = _my_place()
    me = 4 * x + 2 * y + c
    copies = []
    for a in range(len(ins)):
        copies.append(pltpu.make_async_copy(ins[a], outs[a].at[me], loc_sems.at[a]))
        for k, (dx, dy, dc) in enumerate(DEV_FLIPS):
            copies.append(pltpu.make_async_remote_copy(
                src_ref=ins[a], dst_ref=outs[a].at[me], send_sem=send_sems.at[a, k], recv_sem=recv_sems.at[a, k],
                device_id=(_flip(x, dx), _flip(y, dy), _flip(c, dc)), device_id_type=MESH_ID))
    return copies


def _gather8_sems(n):
    return [pltpu.SemaphoreType.DMA((n, 7)), pltpu.SemaphoreType.DMA((n, 7)), pltpu.SemaphoreType.DMA((n,))]


def _gather_dev8(arrs, name):
    n = len(arrs)

    def body(*refs):
        copies = _gather8_copies(refs[:n], refs[n:2 * n], *refs[2 * n:])
        for cp in copies:
            cp.start()
        for cp in copies:
            cp.wait()

    return pl.pallas_call(
        body, name=name, in_specs=[_ANY] * n, out_specs=[_ANY] * n,
        out_shape=[_sds((N_DEV,) + a.shape, a.dtype) for a in arrs], scratch_shapes=_gather8_sems(n),
    )(*arrs)


class _Exchange:
    def __init__(self, arrs, out_shapes, n_sems, phases):
        self.arrs, self.out_shapes, self.n_sems, self._phases = list(arrs), list(out_shapes), n_sems, phases

    @property
    def n(self):
        return len(self.arrs)

    def sem_shapes(self):
        return [pltpu.SemaphoreType.DMA((self.n, self.n_sems)), pltpu.SemaphoreType.DMA((self.n, self.n_sems)),
                pltpu.SemaphoreType.DMA((self.n,))]

    def phases(self, ins, outs, sems):
        return self._phases(ins, outs, *sems)

    def run(self, name):
        n = self.n

        def body(*refs):
            start, mid, end = self.phases(refs[:n], refs[n:2 * n], refs[2 * n:])
            start()
            mid()
            end()

        return pl.pallas_call(body, name=name, in_specs=[_ANY] * n, out_specs=[_ANY] * n, out_shape=self.out_shapes,
                              scratch_shapes=self.sem_shapes())(*self.arrs)

def _gather_halves_phases(ins, outs, send_sems, recv_sems, loc_sems):
    n = len(ins)
    x, y, c = _my_place()
    chip = 2 * x + y
    sibling = (x, y, 1 - c)
    peers = [(_flip(x, dx), _flip(y, dy)) for dx, dy in CHIP_FLIPS]

    def remote(src, p, half, a, k, to):
        return pltpu.make_async_remote_copy(src_ref=src, dst_ref=outs[a].at[p, half], send_sem=send_sems.at[a, k],
                                            recv_sem=recv_sems.at[a, k], device_id=to, device_id_type=MESH_ID)

    def local(a):
        return pltpu.make_async_copy(ins[a], outs[a].at[chip], loc_sems.at[a])

    def first(a, k):
        return remote(ins[a].at[c], chip, c, a, k, (*peers[k], c))

    def passed(a, k):
        p = 2 * peers[k][0] + peers[k][1]
        return remote(outs[a].at[p, c], p, c, a, 3 + k, sibling)

    def start():
        for a in range(n):
            local(a).start()
            for k in range(3):
                first(a, k).start()

    def mid():
        for a in range(n):
            for k in range(3):
                p = 2 * peers[k][0] + peers[k][1]
                remote(outs[a].at[p, c], p, c, a, k, sibling).wait_recv()
                passed(a, k).start()

    def end():
        for a in range(n):
            for k in range(3):
                p = 2 * peers[k][0] + peers[k][1]
                remote(outs[a].at[p, 1 - c], p, 1 - c, a, 3 + k, sibling).wait_recv()
        for a in range(n):
            for k in range(3):
                first(a, k).wait_send()
                passed(a, k).wait_send()
            local(a).wait()

    return start, mid, end


def _gather_chip4_halves(arrs):
    return _Exchange(arrs, [_sds((N_CHIPS,) + a.shape, a.dtype) for a in arrs], 6, _gather_halves_phases)


def _reduce_phases(n_whole, ins, outs, send_sems, recv_sems, loc_sems):
    n = len(ins)
    x, y, c = _my_place()
    chip = 2 * x + y
    sibling = (x, y, 1 - c)
    peers = [(_flip(x, dx), _flip(y, dy)) for dx, dy in CHIP_FLIPS]

    def remote(src, slot, a, k, to):
        return pltpu.make_async_remote_copy(src_ref=src, dst_ref=outs[a].at[slot], send_sem=send_sems.at[a, k],
                                            recv_sem=recv_sems.at[a, k], device_id=to, device_id_type=MESH_ID)

    def block(a, p):
        return ins[a] if a >= n - n_whole else ins[a].at[p]

    def local(a):
        return pltpu.make_async_copy(block(a, chip), outs[a].at[2 * chip + c], loc_sems.at[a])

    def own(a):
        return remote(block(a, chip), 2 * chip + c, a, 0, sibling)

    def first(a, k):
        return remote(block(a, 2 * peers[k][0] + peers[k][1]), 2 * chip + c, a, 1 + k, (*peers[k], c))

    def passed(a, k):
        slot = 2 * (2 * peers[k][0] + peers[k][1]) + c
        return remote(outs[a].at[slot], slot, a, 4 + k, sibling)

    def start():
        for a in range(n):
            local(a).start()
            own(a).start()
            for k in range(3):
                first(a, k).start()

    def mid():
        for a in range(n):
            for k in range(3):
                slot = 2 * (2 * peers[k][0] + peers[k][1]) + c
                remote(outs[a].at[slot], slot, a, 1 + k, sibling).wait_recv()
                passed(a, k).start()

    def end():
        for a in range(n):
            remote(outs[a].at[2 * chip + 1 - c], 2 * chip + 1 - c, a, 0, sibling).wait_recv()
            for k in range(3):
                slot = 2 * (2 * peers[k][0] + peers[k][1]) + 1 - c
                remote(outs[a].at[slot], slot, a, 4 + k, sibling).wait_recv()
        for a in range(n):
            own(a).wait_send()
            for k in range(3):
                first(a, k).wait_send()
                passed(a, k).wait_send()
            local(a).wait()

    return start, mid, end


def _reduce_exchange(arrs, whole=()):
    shapes = [_sds((N_DEV,) + a.shape[1:], a.dtype) for a in arrs] + [_sds((N_DEV,) + a.shape, a.dtype) for a in whole]
    return _Exchange(list(arrs) + list(whole), shapes, 7, functools.partial(_reduce_phases, len(whole)))


def _shard_halves_t(w):
    wt = w.T.astype(BF16)
    n2 = wt.shape[0] // 2
    pad = jnp.zeros((-n2 % 16, wt.shape[1]), BF16)
    return jnp.stack([jnp.concatenate([wt[0:n2], pad], axis=0), jnp.concatenate([wt[n2:], pad], axis=0)])


def _gathered_rows(g, n):
    return [g[p, half, 0:n // 2] for p in range(N_CHIPS) for half in range(2)]


def _even_in_layout_t(g):
    n2 = EVEN_IN // N_CHIPS // 2
    gap, gap_rows = 1696, EVEN_P - EVEN_IN
    assert n2 % 2 == 0 and gap % 2 == 0 and gap_rows % 2 == 0
    spans = []
    for p in range(N_CHIPS):
        for half in range(2):
            lo = (2 * p + half) * n2
            if lo < gap < lo + n2:
                spans += [(p, half, 0, (gap - lo) // 2, lo // 2), (p, half, (gap - lo) // 2, (lo + n2 - gap) // 2, (gap + gap_rows) // 2)]
            else:
                spans.append((p, half, 0, n2 // 2, (lo + (gap_rows if lo >= gap else 0)) // 2))

    def body(g_ref, o_ref, pairs):
        pairs[pl.ds(gap // 2, gap_rows // 2), :] = jnp.zeros((gap_rows // 2, pairs.shape[1]), jnp.uint32)
        for p, half, src, rows, dst in spans:
            pairs[pl.ds(dst, rows), :] = pltpu.bitcast(g_ref[p, half], jnp.uint32)[src:src + rows]
        step = 128
        for r0 in range(0, EVEN_P // 2, step):
            o_ref[pl.ds(2 * r0, 2 * step), :] = pltpu.bitcast(pairs[pl.ds(r0, step), :], o_ref.dtype)

    return pl.pallas_call(body, name="even_w_in_rows", out_shape=_sds((EVEN_P, g.shape[-1]), g.dtype),
                          scratch_shapes=[pltpu.VMEM((EVEN_P // 2, g.shape[-1]), jnp.uint32)],
                          compiler_params=pltpu.CompilerParams(vmem_limit_bytes=VMEM_LIMIT))(g)


def _even_in_unlayout_t(g):
    return jnp.concatenate([g[0:1696], g[1792:2304]], axis=0)


def _uq_layout_t(g):
    wt = jnp.concatenate(_gathered_rows(g, B_HEADS * (B_NOPE + B_ROPE) // N_CHIPS), axis=0)
    per = B_NOPE + B_ROPE
    pad = jnp.zeros((LANES - B_ROPE, wt.shape[1]), wt.dtype)
    nope = [wt[per * h:per * h + B_NOPE] for h in range(B_HEADS)]
    rope = [jnp.concatenate([wt[per * h + B_NOPE:per * (h + 1)], pad], axis=0) for h in range(B_HEADS)]
    return jnp.concatenate(nope + rope, axis=0)


def _uq_unlayout_t(g):
    parts = []
    for h in range(B_HEADS):
        parts += [g[B_NOPE * h:B_NOPE * (h + 1)], g[512 + LANES * h:512 + LANES * h + B_ROPE]]
    return jnp.concatenate(parts, axis=0)


def _block_diag(blocks):
    rows = []
    for h, blk in enumerate(blocks):
        r, cdim = blk.shape
        n = len(blocks)
        rows.append(jnp.concatenate([jnp.zeros((r, cdim * h), blk.dtype), blk, jnp.zeros((r, cdim * (n - 1 - h)), blk.dtype)],
                                    axis=1))
    return jnp.concatenate(rows, axis=0)


def _uk_layout(w):
    return _block_diag([w[:, h, :].T for h in range(B_HEADS)])


def _uk_unlayout(g):
    return jnp.stack([g[B_NOPE * h:B_NOPE * (h + 1), LANES * h:LANES * (h + 1)].T for h in range(B_HEADS)], axis=1)


def _uv_layout(w):
    return _block_diag([w[:, h, :] for h in range(B_HEADS)])


def _uv_unlayout(g):
    return jnp.stack([g[LANES * h:LANES * (h + 1), B_V * h:B_V * (h + 1)] for h in range(B_HEADS)], axis=1)


def _rope_tables(S):
    inv = ROPE_THETA ** (-jnp.arange(0, 32, 2, dtype=F32) / 32)
    tok = jnp.arange(S)

    def tab(pos):
        ang = pos.astype(F32)[:, None] * inv[None, :]
        cos, sin = jnp.cos(ang), jnp.sin(ang)
        return jnp.concatenate([cos, cos], axis=1), jnp.concatenate([-sin, sin], axis=1)

    cr, sr = tab(tok // GRID_W)
    cc, sc = tab(tok % GRID_W)
    ct, st = tab(tok)
    return (jnp.tile(jnp.concatenate([cr, cc], axis=1), (1, 2)), jnp.tile(jnp.concatenate([sr, sc], axis=1), (1, 2)),
            jnp.tile(ct, (1, 4)), jnp.tile(st, (1, 4)))


A_TQ, A_TK, A_SUB = 512, 4096, 512
A_FWD_SUB = 1024
B_TQ, B_TK, B_SUB = 128, 4096, 1024
B_BWD_TK, B_BWD_SUB = 4096, 512
C_T = 256
C_BLOCKS_PER_STEP = 8
KV_SHARE = 2


def _local_step(x0, tgt, mod, norm_w, wie, wuq, wuk, wuv, late_shards, a_q_norm, a_k_norm, q_lora_norm, kv_lora_norm,
                c_sink, final_norm):
    S = x0.shape[0]
    mod3 = mod.reshape(2, 3, D_MODEL)
    ca, sa, ct, st = _rope_tables(S)
    lane_seg = np.arange(LANES) // HEAD_DIM
    seg = jnp.asarray((lane_seg[:, None] == lane_seg[None, :]).astype(np.float32)).astype(BF16)
    qn = jnp.tile(a_q_norm.reshape(1, HEAD_DIM), (1, 2))
    kn = jnp.tile(a_k_norm.reshape(1, HEAD_DIM), (1, 2))
    qln, kvln = q_lora_norm.reshape(1, B_Q_LORA), kv_lora_norm.reshape(1, B_KV_LORA)
    nw0, nw1 = norm_w[0:1], norm_w[1:2]
    gate0, gate1 = mod3[0, 2:3], mod3[1, 2:3]
    a_tq, a_tk, b_tq, b_tk, bb_tk, c_t = min(A_TQ, S), min(A_TK, S), min(B_TQ, S), min(B_TK, S), min(B_BWD_TK, S), min(C_T, S)
    a_sub, b_sub, bb_sub = min(A_SUB, a_tk), min(B_SUB, b_tk), min(B_BWD_SUB, bb_tk)

    h0, proj_e, qa, ka, va, qcat, kcat, ka_t, va_t, kcat_t = _even_pre_fwd(x0, mod3[0], nw0, wie, qn, kn, seg, ca, sa, ct, st,
                                                                           qln, kvln, wuq, wuk)
    oa, lse_a, woe_g, wio_g, woo_g = _pp_fwd(qa, ka, va_t, kdiv=KV_SHARE, tq=a_tq, tk=a_tk, sub=min(A_FWD_SUB, a_tk), name="attn_a_fwd",
                                             side=_gather_chip4_halves(late_shards))
    woe = woe_g.reshape(D_MODEL, D_MODEL)
    wio = wio_g.reshape(N_CHIPS, D_MODEL, ODD_IN // N_CHIPS)
    woo = woo_g.reshape(D_MODEL, D_MODEL)
    olat, lse_b = _mla_fwd(qcat, kcat, kcat_t, tq=b_tq, tk=b_tk, sub=b_sub)
    y0, x1 = _even_post_fwd(oa, olat, proj_e, x0, gate0, wuv, woe)
    h1, gc, qc, kc, vc, kc_t, vc_t = _odd_pre_fwd(x1, mod3[1], nw1, wio)
    slopes = 2.0 ** (-8.0 * jnp.arange(1, C_HEADS + 1, dtype=F32) / C_HEADS)
    slope_rows = jnp.repeat(slopes.reshape(C_HEADS // 2, 2), c_t, axis=1)[:, None, :]
    sink_rows = jnp.repeat(c_sink.reshape(C_HEADS // 2, 2), c_t, axis=1)[:, None, :]
    win_dist = _win_dist_table(S, c_t)
    oc, lse_c = _win_fwd(qc, kc, vc_t, win_dist, slope_rows, sink_rows, kdiv=KV_SHARE, tq=c_t, nbs=C_BLOCKS_PER_STEP,
                         name="attn_c_fwd")
    doc, dgc, dx2, dwoo, st_f = _odd_post(oc, gc, x1, gate1, woo, final_norm.reshape(1, D_MODEL), tgt)
    dqc, dkc, dvc, dsink_raw = _win_bwd(qc, kc, kc_t, vc, oc, doc, lse_c, win_dist, slope_rows, sink_rows, kdiv=KV_SHARE, tq=c_t,
                                        nbs=C_BLOCKS_PER_STEP, name="attn_c_bwd")
    dx1, dwio, st_1 = _odd_pre_bwd(dqc, dkc, dvc, dgc, h1, x1, dx2, mod3[1], nw1, wio)
    doa, dga, dgb, dolat, dwoe, dwuv, st_e = _even_post_bwd(dx1, y0, oa, olat, proj_e, gate0, wuv, woe)
    late_grads = _reduce_exchange([dwoe.reshape(N_CHIPS, D_MODEL // N_CHIPS, D_MODEL), dwio,
                                   dwoo.reshape(N_CHIPS, D_MODEL // N_CHIPS, D_MODEL)])
    dqa, dka, dva, p_woe, p_wio, p_woo = _pp_bwd(qa, ka, ka_t, va, oa, doa, lse_a, kdiv=KV_SHARE, tq=a_tq, tk=a_tk, sub=a_sub,
                                                 name="attn_a_bwd", side=late_grads)
    dqcat, dkcat = _mla_bwd(qcat, kcat, kcat_t, olat, dolat, lse_b, tq=b_tq, tk=bb_tk, sub=bb_sub)
    dx0, dwie, dwuq, dwuk, st_0, nst = _even_pre_bwd(x0, h0, proj_e, dqa, dka, dva, dga, dgb, dqcat, dkcat, dx1, mod3[0], nw0,
                                                     wie, qn, kn, seg, ca, sa, ct, st, qln, kvln, wuq, wuk)
    dsink_pairs = jnp.stack([dsink_raw[:, 0, 0], dsink_raw[:, 1, 0]], axis=1).reshape(C_HEADS)
    return dict(
        loss_row=st_f[2:3], dx=dx0,
        dmod=jnp.stack([jnp.concatenate([st_0[0], st_0[1], st_e[0]]), jnp.concatenate([st_1[0], st_1[1], st_f[1]])]),
        norm_w=jnp.stack([st_0[2], st_1[2]]), final_norm=st_f[0],
        a_q_norm=nst[0:1, 0:HEAD_DIM], a_k_norm=nst[1:2, 0:HEAD_DIM], b_q_lora_norm=nst[2:3, :], b_kv_lora_norm=nst[3:4, 0:LANES],
        c_sink=dsink_pairs.reshape(1, C_HEADS),
        even_w_in=dwie, b_w_uq=dwuq, b_w_uk=dwuk, b_w_uv=dwuv, even_w_out=p_woe, odd_w_in=p_wio, odd_w_out=p_woo)


WEIGHT_NAMES = ("norm_w", "ada_w", "ada_b", "even_w_in", "a_q_norm", "a_k_norm", "b_q_lora_norm", "b_kv_lora_norm", "b_w_uq",
                "b_w_uk", "b_w_uv", "even_w_out", "odd_w_in", "c_sink", "odd_w_out", "final_norm")


def kernel(x, c, norm_w, ada_w, ada_b, even_w_in, a_q_norm, a_k_norm, b_q_lora_norm, b_kv_lora_norm, b_w_uq, b_w_uk, b_w_uv, even_w_out, odd_w_in, c_sink, odd_w_out, final_norm, loss_target, m_norm_w, m_ada_w, m_ada_b, m_even_w_in, m_a_q_norm, m_a_k_norm, m_b_q_lora_norm, m_b_kv_lora_norm, m_b_w_uq, m_b_w_uk, m_b_w_uv, m_even_w_out, m_odd_w_in, m_c_sink, m_odd_w_out, m_final_norm, v_norm_w, v_ada_w, v_ada_b, v_even_w_in, v_a_q_norm, v_a_k_norm, v_b_q_lora_norm, v_b_kv_lora_norm, v_b_w_uq, v_b_w_uk, v_b_w_uv, v_even_w_out, v_odd_w_in, v_c_sink, v_odd_w_out, v_final_norm):
    given = dict(locals())
    xi, yi, ci = _my_place()
    chip = 2 * xi + yi
    dev = 2 * chip + ci
    n_ada = ada_w.shape[2]

    (c_all,) = _gather_dev8([c], "gather_c")
    c_all = c_all.reshape(N_DEV, D_MODEL)
    bias = lax.dynamic_slice_in_dim(ada_b, chip * n_ada, n_ada, axis=1).reshape(2, 1, n_ada)
    mod_cols = _ada_fwd(c_all, ada_w, bias)
    def halves(w):
        return w.astype(BF16).reshape((2, w.shape[0] // 2) + w.shape[1:])

    mod_all, wie_g, wuq_g = _gather_chip4_halves(
        [mod_cols, _shard_halves_t(even_w_in[0]), _shard_halves_t(b_w_uq[0])]).run("gather_weights")
    mod = jnp.transpose(lax.dynamic_index_in_dim(mod_all, dev, axis=2, keepdims=False), (1, 0, 2)).reshape(2, 3 * D_MODEL)

    res = _local_step(
        x[0], loss_target[0], mod, norm_w,
        _even_in_layout_t(wie_g), _uq_layout_t(wuq_g), _uk_layout(b_w_uk[0].astype(BF16)),
        _uv_layout(b_w_uv[0].astype(BF16)), [halves(even_w_out[0]), halves(odd_w_in[0]), halves(odd_w_out[0])],
        a_q_norm, a_k_norm, b_q_lora_norm, b_kv_lora_norm, c_sink, final_norm)

    latent = jnp.stack([_uk_unlayout(res["b_w_uk"]).reshape(B_KV_LORA, 512),
                        _uv_unlayout(res["b_w_uv"]).reshape(B_KV_LORA, 512)]).astype(BF16)
    p_wie, p_wuq, small_all, latent_all = _reduce_exchange(
        [_even_in_unlayout_t(res["even_w_in"]).reshape(N_CHIPS, EVEN_IN // N_CHIPS, D_MODEL),
         _uq_unlayout_t(res["b_w_uq"]).astype(BF16).reshape(N_CHIPS, -1, B_Q_LORA)],
        whole=[_pack_small(res), latent]).run("reduce_exchange")
    shard_parts = dict(even_w_in=p_wie, b_w_uq=p_wuq, **{k: res[k] for k in ("even_w_out", "odd_w_in", "odd_w_out")})
    dmod_all = small_all[:, 0:6, :].reshape(N_DEV, 2, 3 * D_MODEL)
    dmod_cols = jnp.transpose(lax.dynamic_slice_in_dim(dmod_all, chip * n_ada, n_ada, axis=2), (1, 0, 2))
    parts = dict(shard_parts)
    parts["ada_w"] = _ada_bwd(c_all.T, dmod_cols).reshape(1, 2 * D_MODEL, n_ada)
    parts["b_w_uk"], parts["b_w_uv"] = [latent_all[:, t].reshape(N_DEV, B_KV_LORA * B_HEADS, -1) for t in range(2)]

    def as2d(a):
        return a.reshape((-1, a.shape[-1]) if a.ndim > 1 else (1, a.shape[0]))

    results = {}
    small_outs = _adam_small(small_all, *[[as2d(given[pre + k]) for k in SMALL_WEIGHTS] for pre in ("", "m_", "v_")])
    for idx, k in enumerate(SMALL_WEIGHTS):
        results[k] = small_outs[4 * idx:4 * idx + 4]
    for k, p in parts.items():
        if k in ("even_w_in", "b_w_uq"):
            outs = _adam(p, given[k][0].T, given["m_" + k][0].T, given["v_" + k][0].T, "adam_" + k, by_columns=k == "even_w_in")
            results[k] = [o.T for o in outs]
            continue
        shape2 = (p.shape[-2], p.shape[-1])
        results[k] = _adam(p, given[k].reshape(shape2), given["m_" + k].reshape(shape2), given["v_" + k].reshape(shape2),
                           "adam_" + k)
    by_kind = [[results[k][t].reshape(given[k].shape) for k in WEIGHT_NAMES] for t in range(4)]
    return (small_outs[-1][0, 0], res["dx"][None], *by_kind[0], *by_kind[1], *by_kind[2], *by_kind[3])
```

```python
import functools

import numpy as np
import jax
import jax.numpy as jnp
from jax import lax
from jax.experimental import pallas as pl
from jax.experimental.pallas import tpu as pltpu

F32 = jnp.float32
BF16 = jnp.bfloat16
HIGHEST = lax.Precision.HIGHEST
MESH_ID = pl.DeviceIdType.MESH

D_MODEL = 1024
HEAD_DIM = 64
GRID_W = 64
EPS = 1e-6
ROPE_THETA = 10000.0
B_HEADS, B_NOPE, B_ROPE, B_V = 8, 64, 32, 64
B_Q_LORA, B_KV_LORA = 256, 128
C_HEADS = 16
WINDOW = 128
EVEN_IN, ODD_IN = 2208, 2560
EVEN_P = 2304
N_CHIPS, N_DEV = 4, 8
LANES = 128
NEG = -1e30
VMEM_LIMIT = 60 * 1024 * 1024

ADAM_LR, ADAM_B1, ADAM_B2, ADAM_EPS, ADAM_WD, ADAM_STEP = 0.001, 0.9, 0.999, 1e-08, 0.01, 10

ROW_TILE = 512
IN_PROJ_ROW_TILE = 256


def _dot(a, b):
    return lax.dot_general(a, b, (((1,), (0,)), ((), ())), preferred_element_type=F32)


def _dot_nt(a, b):
    return lax.dot_general(a, b, (((1,), (1,)), ((), ())), preferred_element_type=F32)


def _dot_tn(a, b):
    return lax.dot_general(a, b, (((0,), (0,)), ((), ())), preferred_element_type=F32)


def _dot_f32(a, b):
    return lax.dot_general(a, b, (((1,), (0,)), ((), ())), precision=HIGHEST, preferred_element_type=F32)


def _sigmoid(x):
    return 1.0 / (1.0 + jnp.exp(-x))


def _silu_and_grad(g):
    s = _sigmoid(g)
    return g * s, s * (1.0 + g * (1.0 - s))


def _lane_iota():
    return lax.broadcasted_iota(jnp.int32, (1, LANES), 1)


def _partner(x, lane):
    return jnp.where((lane % 32) < 16, pltpu.roll(x, LANES - 16, 1), pltpu.roll(x, 16, 1))


def _rot(x, cos, sin_signed, lane):
    return x * cos + _partner(x, lane) * sin_signed


def _rot_bwd(dy, cos, sin_signed, lane):
    return dy * cos + _partner(dy * sin_signed, lane)


def _rms(x):
    return lax.rsqrt(jnp.mean(x * x, axis=-1, keepdims=True) + EPS)


def _rms_bwd(x, r, g):
    return r * g - x * (r * r * r) * jnp.mean(x * g, axis=-1, keepdims=True)


def _seg_mean(v, seg_ones):
    hi = v.astype(BF16)
    lo = (v - hi.astype(F32)).astype(BF16)
    return (_dot(hi, seg_ones) + _dot(lo, seg_ones)) * (1.0 / HEAD_DIM)


def _dup_heads(x, lane):
    swapped = pltpu.roll(x, HEAD_DIM, 1)
    lo = lane < HEAD_DIM
    return jnp.concatenate([jnp.where(lo, x, swapped), jnp.where(lo, swapped, x)], axis=1)


def _fold_heads(x2, lane):
    a, b = x2[:, 0:LANES], x2[:, LANES:2 * LANES]
    return jnp.where(lane < HEAD_DIM, a + pltpu.roll(a, HEAD_DIM, 1), b + pltpu.roll(b, HEAD_DIM, 1))


def _row_spec(ts, cols):
    return pl.BlockSpec((ts, cols), lambda i: (i, 0))


def _full_spec(shape, single=True):
    nd = len(shape)
    if single:
        return pl.BlockSpec(shape, lambda i: (0,) * nd, pipeline_mode=pl.Buffered(1))
    return pl.BlockSpec(shape, lambda i: (0,) * nd)


def _sds(shape, dtype):
    return jax.ShapeDtypeStruct(shape, dtype)


def _params(sem):
    return pltpu.CompilerParams(dimension_semantics=sem, vmem_limit_bytes=VMEM_LIMIT)


def _even_pre_fwd(x, mod, nw, wie, qn, kn, seg, ca, sa, ct, st, qln, kvln, wuq, wuk):
    S = x.shape[0]
    ts = min(IN_PROJ_ROW_TILE, S)

    def body(x_ref, mod_ref, nw_ref, wie_ref, qn_ref, kn_ref, seg_ref, ca_ref, sa_ref, ct_ref, st_ref, qln_ref,
             kvln_ref, wuq_ref, wuk_ref, h_ref, proj_ref, qa_ref, ka_ref, va_ref, qcat_ref, kcat_ref, kat_ref, vat_ref, kcatt_ref):
        xv = x_ref[...]
        h = (xv * _rms(xv) * nw_ref[...]) * (1.0 + mod_ref[1:2, :]) + mod_ref[0:1, :]
        hb = h.astype(BF16)
        h_ref[...] = hb
        proj = _dot_nt(hb, wie_ref[...])
        proj_ref[...] = proj
        lane = _lane_iota()
        ca_v, sa_v, ct_v, st_v = ca_ref[...], sa_ref[...], ct_ref[...], st_ref[...]
        seg_v = seg_ref[...]
        for cb in range(4):
            xc = proj[:, LANES * cb:LANES * (cb + 1)]
            r = lax.rsqrt(_seg_mean(xc * xc, seg_v) + EPS)
            y = _rot(xc * r * qn_ref[...], ca_v, sa_v, lane)
            qa_ref[:, LANES * cb:LANES * (cb + 1)] = (y * 0.125).astype(BF16)
        kc = proj[:, 512:640]
        r = lax.rsqrt(_seg_mean(kc * kc, seg_v) + EPS)
        ka_v = _dup_heads(_rot(kc * r * kn_ref[...], ca_v, sa_v, lane), lane)
        ka_ref[...] = ka_v.astype(BF16)
        kat_ref[...] = ka_v.T.astype(BF16)
        va_v = _dup_heads(proj[:, 640:768], lane)
        va_ref[...] = va_v.astype(BF16)
        vat_ref[...] = va_v.T.astype(BF16)
        cq = proj[:, 1280:1536]
        cqn = (cq * _rms(cq) * qln_ref[...]).astype(BF16)
        ckv = proj[:, 1536:1664]
        ckvn = ckv * _rms(ckv) * kvln_ref[...]
        qb = _dot_nt(cqn, wuq_ref[...])
        qlat = _dot(qb[:, 0:512].astype(BF16), wuk_ref[...])
        for hh in range(B_HEADS):
            qcat_ref[hh, :, 0:LANES] = qlat[:, LANES * hh:LANES * (hh + 1)].astype(BF16)
            qr = _rot(qb[:, 512 + LANES * hh:512 + LANES * (hh + 1)], ct_v, st_v, lane)
            qcat_ref[hh, :, LANES:2 * LANES] = qr.astype(BF16)
        kr = _rot(proj[:, 1664:1792], ct_v, st_v, lane)
        kcat_ref[:, 0:LANES] = ckvn.astype(BF16)
        kcat_ref[:, LANES:2 * LANES] = kr.astype(BF16)
        kcatt_ref[0:LANES, :] = ckvn.T.astype(BF16)
        kcatt_ref[LANES:2 * LANES, :] = kr.T.astype(BF16)

    col_spec = lambda rows: pl.BlockSpec((rows, ts), lambda i: (0, i))
    return pl.pallas_call(
        body, name="even_pre_fwd", grid=(S // ts,),
        in_specs=[_row_spec(ts, D_MODEL), _full_spec((3, D_MODEL)), _full_spec((1, D_MODEL)), _full_spec((EVEN_P, D_MODEL)),
                  _full_spec((1, LANES)), _full_spec((1, LANES)), _full_spec((LANES, LANES)),
                  _row_spec(ts, LANES), _row_spec(ts, LANES), _row_spec(ts, LANES), _row_spec(ts, LANES),
                  _full_spec((1, B_Q_LORA)), _full_spec((1, B_KV_LORA)), _full_spec((1536, B_Q_LORA)), _full_spec((512, 1024))],
        out_specs=[_row_spec(ts, D_MODEL), _row_spec(ts, EVEN_P), _row_spec(ts, 512), _row_spec(ts, 2 * LANES), _row_spec(ts, 2 * LANES),
                   pl.BlockSpec((B_HEADS, ts, 2 * LANES), lambda i: (0, i, 0)), _row_spec(ts, 2 * LANES),
                   col_spec(2 * LANES), col_spec(2 * LANES), col_spec(2 * LANES)],
        out_shape=[_sds((S, D_MODEL), BF16), _sds((S, EVEN_P), F32), _sds((S, 512), BF16), _sds((S, 2 * LANES), BF16),
                   _sds((S, 2 * LANES), BF16), _sds((B_HEADS, S, 2 * LANES), BF16), _sds((S, 2 * LANES), BF16),
                   _sds((2 * LANES, S), BF16), _sds((2 * LANES, S), BF16), _sds((2 * LANES, S), BF16)],
        compiler_params=_params(("arbitrary",)),
    )(x, mod, nw, wie, qn, kn, seg, ca, sa, ct, st, qln, kvln, wuq, wuk)


MLA_SCALE = (B_NOPE + B_ROPE) ** -0.5
LOG2E = 1.4426950408889634

def _row_lo():
    return lax.broadcasted_iota(jnp.int32, (LANES, 1), 0) < HEAD_DIM


def _stack_cols(vT, rlo):
    zero = jnp.zeros_like(vT)
    return jnp.concatenate([jnp.where(rlo, vT, zero), jnp.where(rlo, zero, vT)], axis=1)


def _stack_rows(v, lo):
    zero = jnp.zeros_like(v)
    return jnp.concatenate([jnp.where(lo, v, zero), jnp.where(lo, zero, v)], axis=0)


def _pick_halves_T(xT, rlo, t):
    return jnp.where(rlo, xT[:, 0:t], xT[:, t:2 * t]).T


def _side_split(refs, n_in, n_out, n_scratch, side):
    ns = side.n if side is not None else 0
    cuts = np.cumsum([0, n_in, ns, n_out, ns, n_scratch])
    return [refs[a:b] for a, b in zip(cuts[:-1], cuts[1:])] + [refs[cuts[-1]:]]


def _side_hooks(side, side_ins, side_outs, side_sems, step, total):
    if side is None:
        return lambda: None
    start, mid, end = side.phases(side_ins, side_outs, side_sems)
    pl.when(step == 0)(start)
    pl.when(step == total // 2)(mid)
    return lambda: pl.when(step == total - 1)(end)


def _side_specs(side):
    if side is None:
        return [], [], [], [], []
    return list(side.arrs), [_ANY] * side.n, [_ANY] * side.n, list(side.out_shapes), side.sem_shapes()


def _pp_fwd(q, k, vT, *, kdiv, tq, tk, sub, name, side=None):
    S = k.shape[0]; nb = q.shape[1] // LANES; nq = S // tq; nkv = S // tk; nsub = tk // sub

    def body(*refs):
        (q_ref, k_ref, vT_ref), side_ins, (o_ref, lse_ref), side_outs, (qs, m_s, l_s, acc), side_sems = _side_split(refs, 3, 2, 4, side)
        j = pl.program_id(2)
        rlo = _row_lo()
        step = (pl.program_id(0) * nq + pl.program_id(1)) * nkv + j
        side_end = _side_hooks(side, side_ins, side_outs, side_sems, step, nb * nq * nkv)

        @pl.when(j == 0)
        def _():
            qs[...] = _stack_cols(q_ref[...].astype(F32).T, rlo).astype(BF16)
            m_s[...] = jnp.full((1, 2 * tq), NEG, F32)
            l_s[...] = jnp.zeros((1, 2 * tq), F32)
            acc[...] = jnp.zeros((LANES, 2 * tq), F32)

        qsv = qs[...]
        m, l, a = m_s[...], l_s[...], acc[...]
        s_cur = _dot(k_ref[0:sub, :], qsv)
        for t in range(nsub):
            if t + 1 < nsub:
                s_next = _dot(k_ref[sub * (t + 1):sub * (t + 2), :], qsv)
            m_new = jnp.maximum(m, jnp.max(s_cur, axis=0, keepdims=True))
            alpha = jnp.exp(m - m_new)
            p = jnp.exp(s_cur - m_new)
            l = alpha * l + jnp.sum(p, axis=0, keepdims=True)
            a = alpha * a + _dot(vT_ref[:, sub * t:sub * (t + 1)], p.astype(BF16))
            m = m_new
            if t + 1 < nsub:
                s_cur = s_next
        m_s[...], l_s[...], acc[...] = m, l, a

        @pl.when(j == nkv - 1)
        def _():
            l_f = l_s[...]
            o_ref[...] = _pick_halves_T(acc[...] / l_f, rlo, tq).astype(BF16)
            lse_ref[0, 0] = m_s[...] + jnp.log(l_f)

        side_end()

    s_args, s_in, s_out, s_shapes, s_sems = _side_specs(side)
    return pl.pallas_call(
        body, name=name, grid=(nb, nq, nkv),
        in_specs=[pl.BlockSpec((tq, LANES), lambda b, i, j: (i, b)), pl.BlockSpec((tk, LANES), lambda b, i, j: (j, b // kdiv)),
                  pl.BlockSpec((LANES, tk), lambda b, i, j: (b // kdiv, j))] + s_in,
        out_specs=[pl.BlockSpec((tq, LANES), lambda b, i, j: (i, b)),
                   pl.BlockSpec((1, 1, 1, 2 * tq), lambda b, i, j: (b, i, 0, 0))] + s_out,
        out_shape=[_sds((S, nb * LANES), BF16), _sds((nb, nq, 1, 2 * tq), F32)] + s_shapes,
        scratch_shapes=[pltpu.VMEM((LANES, 2 * tq), BF16), pltpu.VMEM((1, 2 * tq), F32), pltpu.VMEM((1, 2 * tq), F32),
                        pltpu.VMEM((LANES, 2 * tq), F32)] + s_sems,
        compiler_params=_params(("arbitrary",) * 3))(q, k, vT, *s_args)


def _pp_bwd(q, k, kT, v, o, do, lse, *, kdiv, tq, tk, sub, name, side=None):
    S = k.shape[0]; nb = q.shape[1] // LANES; nkb = k.shape[1] // LANES; nq = S // tq; nkv = S // tk; nsub = tk // sub

    def body(*refs):
        ((q_ref, k_ref, kT_ref, v_ref, o_ref, do_ref, lse_ref), side_ins, (dq_ref, dk_ref, dv_ref), side_outs,
         (qsT, qs, dosT, dos, delta_s, dq_acc), side_sems) = _side_split(refs, 7, 3, 6, side)
        b, i, j = pl.program_id(0), pl.program_id(1), pl.program_id(2)
        rlo = _row_lo()
        lo = lax.broadcasted_iota(jnp.int32, (1, LANES), 1) < HEAD_DIM
        side_end = _side_hooks(side, side_ins, side_outs, side_sems, (b * nq + i) * nkv + j, nb * nq * nkv)

        @pl.when((b % kdiv == 0) & (i == 0) & (j == 0))
        def _():
            dk_ref[...] = jnp.zeros((S, LANES), F32)
            dv_ref[...] = jnp.zeros((S, LANES), F32)

        @pl.when(j == 0)
        def _():
            qv = q_ref[...]
            qs[...] = _stack_rows(qv, lo)
            qsT[...] = _stack_cols(qv.astype(F32).T, rlo).astype(BF16)
            dov = do_ref[...].astype(F32)
            dos[...] = _stack_rows(dov.astype(BF16), lo)
            dosT[...] = _stack_cols(dov.T, rlo).astype(BF16)
            prodT = (dov * o_ref[...].astype(F32)).T
            delta_s[...] = jnp.concatenate([jnp.sum(jnp.where(rlo, prodT, 0.0), axis=0, keepdims=True),
                                            jnp.sum(jnp.where(rlo, 0.0, prodT), axis=0, keepdims=True)], axis=1)
            dq_acc[...] = jnp.zeros((LANES, 2 * tq), F32)

        qsTv, dosTv, qsv, dosv = qsT[...], dosT[...], qs[...], dos[...]
        lse_v, delta_v = lse_ref[0, 0], delta_s[...]
        dqa = dq_acc[...]
        s_cur = _dot(k_ref[0:sub, :], qsTv)
        dp_cur = _dot(v_ref[0:sub, :], dosTv)
        for t in range(nsub):
            if t + 1 < nsub:
                s_next = _dot(k_ref[sub * (t + 1):sub * (t + 2), :], qsTv)
                dp_next = _dot(v_ref[sub * (t + 1):sub * (t + 2), :], dosTv)
            p = jnp.exp(s_cur - lse_v)
            ds = (p * (dp_cur - delta_v)).astype(BF16)
            rows = pl.ds(pl.multiple_of(j * tk + sub * t, sub), sub)
            dv_ref[rows, :] += _dot(p.astype(BF16), dosv)
            dk_ref[rows, :] += _dot(ds, qsv)
            dqa = dqa + _dot(kT_ref[:, sub * t:sub * (t + 1)], ds)
            if t + 1 < nsub:
                s_cur, dp_cur = s_next, dp_next
        dq_acc[...] = dqa

        @pl.when(j == nkv - 1)
        def _():
            dq_ref[...] = _pick_halves_T(dq_acc[...], rlo, tq)

        side_end()

    qmap = lambda b, i, j: (i, b)
    kmap = lambda b, i, j: (j, b // kdiv)
    res = lambda b, i, j: (0, b // kdiv)
    s_args, s_in, s_out, s_shapes, s_sems = _side_specs(side)
    return pl.pallas_call(
        body, name=name, grid=(nb, nq, nkv),
        in_specs=[pl.BlockSpec((tq, LANES), qmap), pl.BlockSpec((tk, LANES), kmap), pl.BlockSpec((LANES, tk), lambda b, i, j: (b // kdiv, j)),
                  pl.BlockSpec((tk, LANES), kmap), pl.BlockSpec((tq, LANES), qmap), pl.BlockSpec((tq, LANES), qmap),
                  pl.BlockSpec((1, 1, 1, 2 * tq), lambda b, i, j: (b, i, 0, 0))] + s_in,
        out_specs=[pl.BlockSpec((tq, LANES), qmap), pl.BlockSpec((S, LANES), res), pl.BlockSpec((S, LANES), res)] + s_out,
        out_shape=[_sds((S, nb * LANES), F32), _sds((S, nkb * LANES), F32), _sds((S, nkb * LANES), F32)] + s_shapes,
        scratch_shapes=[pltpu.VMEM((LANES, 2 * tq), BF16), pltpu.VMEM((2 * tq, LANES), BF16), pltpu.VMEM((LANES, 2 * tq), BF16),
                        pltpu.VMEM((2 * tq, LANES), BF16), pltpu.VMEM((1, 2 * tq), F32), pltpu.VMEM((LANES, 2 * tq), F32)] + s_sems,
        compiler_params=_params(("arbitrary",) * 3))(q, k, kT, v, o, do, lse, *s_args)


MLA_C = MLA_SCALE * LOG2E


def _mla_fwd(q, kcat, kcatT, *, tq, tk, sub):
    S = kcat.shape[0]; nq, nkv = S // tq, S // tk; R = B_HEADS * tq; nsub = tk // sub

    def body(q_ref, k_ref, vT_ref, o_ref, lse_ref, qT, m_s, l_s, acc):
        j = pl.program_id(1)

        @pl.when(j == 0)
        def _():
            qT[...] = q_ref[...].reshape(R, 2 * LANES).astype(F32).T.astype(BF16)
            m_s[...] = jnp.full((1, R), NEG, F32)
            l_s[...] = jnp.zeros((1, R), F32)
            acc[...] = jnp.zeros((LANES, R), F32)

        qTv = qT[...]
        m, l, a = m_s[...], l_s[...], acc[...]
        s_cur = _dot(k_ref[0:sub, :], qTv)
        for t in range(nsub):
            if t + 1 < nsub:
                s_next = _dot(k_ref[sub * (t + 1):sub * (t + 2), :], qTv)
            m_new = jnp.maximum(m, jnp.max(s_cur, axis=0, keepdims=True))
            alpha = jnp.exp2((m - m_new) * MLA_C)
            p = jnp.exp2((s_cur - m_new) * MLA_C)
            l = alpha * l + jnp.sum(p, axis=0, keepdims=True)
            a = alpha * a + _dot(vT_ref[:, sub * t:sub * (t + 1)], p.astype(BF16))
            m = m_new
            if t + 1 < nsub:
                s_cur = s_next
        m_s[...], l_s[...], acc[...] = m, l, a

        @pl.when(j == nkv - 1)
        def _():
            l_f = l_s[...]
            o_ref[...] = (acc[...] / l_f).T.reshape(B_HEADS, tq, LANES).astype(BF16)
            lse_ref[0] = m_s[...] * MLA_SCALE + jnp.log(l_f)

    return pl.pallas_call(
        body, name="mla_fwd", grid=(nq, nkv),
        in_specs=[pl.BlockSpec((B_HEADS, tq, 2 * LANES), lambda i, j: (0, i, 0)), pl.BlockSpec((tk, 2 * LANES), lambda i, j: (j, 0)),
                  pl.BlockSpec((LANES, tk), lambda i, j: (0, j))],
        out_specs=[pl.BlockSpec((B_HEADS, tq, LANES), lambda i, j: (0, i, 0)), pl.BlockSpec((1, 1, R), lambda i, j: (i, 0, 0))],
        out_shape=[_sds((B_HEADS, S, LANES), BF16), _sds((nq, 1, R), F32)],
        scratch_shapes=[pltpu.VMEM((2 * LANES, R), BF16), pltpu.VMEM((1, R), F32), pltpu.VMEM((1, R), F32), pltpu.VMEM((LANES, R), F32)],
        compiler_params=_params(("arbitrary", "arbitrary")))(q, kcat, kcatT)


def _mla_bwd(q, kcat, kcatT, o, do, lse, *, tq, tk, sub):
    S = kcat.shape[0]; nq, nkv = S // tq, S // tk; R = B_HEADS * tq; nsub = tk // sub

    def body(q_ref, k_ref, kT_ref, o_ref, do_ref, lse_ref, dq_ref, dk_ref, qT, dosT, dos, delta_s, dq_acc):
        i, j = pl.program_id(0), pl.program_id(1)

        @pl.when((i == 0) & (j == 0))
        def _():
            dk_ref[...] = jnp.zeros((S, 2 * LANES), F32)

        @pl.when(j == 0)
        def _():
            qT[...] = q_ref[...].reshape(R, 2 * LANES).astype(F32).T.astype(BF16)
            dov = do_ref[...].reshape(R, LANES).astype(F32)
            dos[...] = dov.astype(BF16)
            dosT[...] = dov.T.astype(BF16)
            delta_s[...] = jnp.sum((dov * o_ref[...].reshape(R, LANES).astype(F32)).T, axis=0, keepdims=True)
            dq_acc[...] = jnp.zeros((2 * LANES, R), F32)

        qTv, dosTv, dosv = qT[...], dosT[...], dos[...]
        qv = q_ref[...].reshape(R, 2 * LANES)
        lse_v, delta_v = lse_ref[0] * LOG2E, delta_s[...]
        dqa = dq_acc[...]
        s_cur = _dot(k_ref[0:sub, :], qTv)
        dp_cur = _dot(k_ref[0:sub, 0:LANES], dosTv)
        for t in range(nsub):
            if t + 1 < nsub:
                s_next = _dot(k_ref[sub * (t + 1):sub * (t + 2), :], qTv)
                dp_next = _dot(k_ref[sub * (t + 1):sub * (t + 2), 0:LANES], dosTv)
            p = jnp.exp2(s_cur * MLA_C - lse_v)
            ds = (p * (dp_cur - delta_v) * MLA_SCALE).astype(BF16)
            rows = pl.ds(pl.multiple_of(j * tk + sub * t, sub), sub)
            dk_ref[rows, :] += _dot(ds, qv)
            dk_ref[rows, 0:LANES] += _dot(p.astype(BF16), dosv)
            dqa = dqa + _dot(kT_ref[:, sub * t:sub * (t + 1)], ds)
            if t + 1 < nsub:
                s_cur, dp_cur = s_next, dp_next
        dq_acc[...] = dqa

        @pl.when(j == nkv - 1)
        def _():
            dq_ref[...] = dq_acc[...].T.reshape(B_HEADS, tq, 2 * LANES)

    hspec = lambda w: pl.BlockSpec((B_HEADS, tq, w), lambda i, j: (0, i, 0))
    return pl.pallas_call(
        body, name="mla_bwd", grid=(nq, nkv),
        in_specs=[hspec(2 * LANES), pl.BlockSpec((tk, 2 * LANES), lambda i, j: (j, 0)), pl.BlockSpec((2 * LANES, tk), lambda i, j: (0, j)),
                  hspec(LANES), hspec(LANES), pl.BlockSpec((1, 1, R), lambda i, j: (i, 0, 0))],
        out_specs=[hspec(2 * LANES), pl.BlockSpec((S, 2 * LANES), lambda i, j: (0, 0))],
        out_shape=[_sds((B_HEADS, S, 2 * LANES), F32), _sds((S, 2 * LANES), F32)],
        scratch_shapes=[pltpu.VMEM((2 * LANES, R), BF16), pltpu.VMEM((LANES, R), BF16), pltpu.VMEM((R, LANES), BF16),
                        pltpu.VMEM((1, R), F32), pltpu.VMEM((2 * LANES, R), F32)],
        compiler_params=_params(("arbitrary", "arbitrary")))(q, kcat, kcatT, o, do, lse)


def _win_start(i, tq, nk, S):
    return pl.multiple_of(jnp.clip(i * tq - WINDOW, 0, S - nk), LANES)


def _win_dist_table(S, tq):
    nk = min(tq + 2 * WINDOW, S)
    nq = S // tq
    r = np.arange(nk)[:, None]
    c = (np.arange(2 * tq) % tq)[None, :]
    tabs = []
    for rel in (0, WINDOW, (nq - 1) * tq - (S - nk)):
        dist = np.abs(rel + c - r).astype(np.float32)
        tabs.append(np.where(dist <= WINDOW, dist, np.float32(1e32)))
    return jnp.asarray(np.stack(tabs))


def _win_dist_spec(nk, tq, nq):
    return pl.BlockSpec((1, nk, 2 * tq), lambda b, i: (jnp.where(i == 0, 0, jnp.where(i == nq - 1, 2, 1)), 0, 0))


def _win_fwd(q, k, vT, dist, slope, sink, *, kdiv, tq, nbs, name):
    S = k.shape[0]; nb = q.shape[1] // LANES; nq = S // tq; nk = min(tq + 2 * WINDOW, S)
    assert nb % nbs == 0 and nbs % kdiv == 0
    kvw = (nbs // kdiv) * LANES

    def body(q_ref, k_ref, vT_ref, dist_ref, slope_ref, sink_ref, o_ref, lse_ref):
        i = pl.program_id(1)
        rlo = _row_lo()
        k0 = _win_start(i, tq, nk, S)
        kk, vv, dd = k_ref[pl.ds(k0, nk), :], vT_ref[:, pl.ds(k0, nk)], dist_ref[0]
        for u in range(nbs):
            kv = slice(LANES * (u // kdiv), LANES * (u // kdiv + 1))
            qsT = _stack_cols(q_ref[:, LANES * u:LANES * (u + 1)].astype(F32).T, rlo).astype(BF16)
            s = _dot(kk[:, kv], qsT) - slope_ref[u] * dd
            sk = sink_ref[u]
            m = jnp.maximum(jnp.max(s, axis=0, keepdims=True), sk)
            p = jnp.exp(s - m)
            l = jnp.sum(p, axis=0, keepdims=True) + jnp.exp(sk - m)
            o_ref[:, LANES * u:LANES * (u + 1)] = _pick_halves_T(_dot(vv[kv, :], p.astype(BF16)) / l, rlo, tq).astype(BF16)
            lse_ref[u, 0] = m + jnp.log(l)

    row_spec = pl.BlockSpec((nbs, 1, 2 * tq), lambda b, i: (b, 0, 0))
    return pl.pallas_call(
        body, name=name, grid=(nb // nbs, nq),
        in_specs=[pl.BlockSpec((tq, nbs * LANES), lambda b, i: (i, b)), pl.BlockSpec((S, kvw), lambda b, i: (0, b)),
                  pl.BlockSpec((kvw, S), lambda b, i: (b, 0)), _win_dist_spec(nk, tq, nq), row_spec, row_spec],
        out_specs=[pl.BlockSpec((tq, nbs * LANES), lambda b, i: (i, b)), pl.BlockSpec((nbs, 1, 1, 2 * tq), lambda b, i: (b, i, 0, 0))],
        out_shape=[_sds((S, nb * LANES), BF16), _sds((nb, nq, 1, 2 * tq), F32)],
        compiler_params=_params(("arbitrary", "arbitrary")))(q, k, vT, dist, slope, sink)


def _win_bwd(q, k, kT, v, o, do, lse, dist, slope, sink, *, kdiv, tq, nbs, name):
    S = k.shape[0]; nb = q.shape[1] // LANES; nkb = k.shape[1] // LANES; nq = S // tq; nk = min(tq + 2 * WINDOW, S)
    assert nb % nbs == 0 and nbs % kdiv == 0
    nkv = nbs // kdiv
    kvw = nkv * LANES

    def body(q_ref, k_ref, kT_ref, v_ref, o_ref, do_ref, lse_ref, dist_ref, slope_ref, sink_ref, dq_ref, dk_ref, dv_ref, dsink_ref, ds_acc):
        i = pl.program_id(1)
        rlo = _row_lo()
        lo = lax.broadcasted_iota(jnp.int32, (1, LANES), 1) < HEAD_DIM

        @pl.when(i == 0)
        def _():
            dk_ref[...] = jnp.zeros((S, kvw), F32)
            dv_ref[...] = jnp.zeros((S, kvw), F32)
            ds_acc[...] = jnp.zeros((nbs, 2 * tq), F32)

        k0 = _win_start(i, tq, nk, S)
        rows = pl.ds(k0, nk)
        kk_all, vv_all, kkT_all, dd = k_ref[rows, :], v_ref[rows, :], kT_ref[:, rows], dist_ref[0]
        dv_sum, dk_sum = [None] * nkv, [None] * nkv
        for u in range(nbs):
            g = u // kdiv
            kv = slice(LANES * g, LANES * (g + 1))
            kk, vv, kkT = kk_all[:, kv], vv_all[:, kv], kkT_all[kv, :]
            cols = slice(LANES * u, LANES * (u + 1))
            qv = q_ref[:, cols]
            qs = _stack_rows(qv, lo)
            qsT = _stack_cols(qv.astype(F32).T, rlo).astype(BF16)
            dov = do_ref[:, cols].astype(F32)
            dos = _stack_rows(dov.astype(BF16), lo)
            dosT = _stack_cols(dov.T, rlo).astype(BF16)
            prodT = (dov * o_ref[:, cols].astype(F32)).T
            delta = jnp.concatenate([jnp.sum(jnp.where(rlo, prodT, 0.0), axis=0, keepdims=True),
                                     jnp.sum(jnp.where(rlo, 0.0, prodT), axis=0, keepdims=True)], axis=1)
            lse_v = lse_ref[u, 0]
            ds_acc[u:u + 1, :] += -jnp.exp(sink_ref[u] - lse_v) * delta
            p = jnp.exp(_dot(kk, qsT) - slope_ref[u] * dd - lse_v)
            ds = (p * (_dot(vv, dosT) - delta)).astype(BF16)
            dv_u, dk_u = _dot(p.astype(BF16), dos), _dot(ds, qs)
            dv_sum[g] = dv_u if dv_sum[g] is None else dv_sum[g] + dv_u
            dk_sum[g] = dk_u if dk_sum[g] is None else dk_sum[g] + dk_u
            dq_ref[:, cols] = (_pick_halves_T(_dot(kkT, ds), rlo, tq) * 0.125).astype(BF16)
        dv_ref[rows, :] += jnp.concatenate(dv_sum, axis=1)
        dk_ref[rows, :] += jnp.concatenate(dk_sum, axis=1)

        @pl.when(i == nq - 1)
        def _():
            acc = ds_acc[...]
            for u in range(nbs):
                dsink_ref[u] = jnp.concatenate(
                    [jnp.broadcast_to(jnp.sum(acc[u:u + 1, 0:tq], axis=1, keepdims=True), (1, LANES)),
                     jnp.broadcast_to(jnp.sum(acc[u:u + 1, tq:2 * tq], axis=1, keepdims=True), (1, LANES)),
                     jnp.zeros((6, LANES), F32)], axis=0)

    qmap = lambda b, i: (i, b)
    kv_spec = pl.BlockSpec((S, kvw), lambda b, i: (0, b))
    row_spec = pl.BlockSpec((nbs, 1, 2 * tq), lambda b, i: (b, 0, 0))
    wide = pl.BlockSpec((tq, nbs * LANES), qmap)
    return pl.pallas_call(
        body, name=name, grid=(nb // nbs, nq),
        in_specs=[wide, kv_spec, pl.BlockSpec((kvw, S), lambda b, i: (b, 0)), kv_spec, wide, wide,
                  pl.BlockSpec((nbs, 1, 1, 2 * tq), lambda b, i: (b, i, 0, 0)), _win_dist_spec(nk, tq, nq), row_spec, row_spec],
        out_specs=[wide, kv_spec, kv_spec, pl.BlockSpec((nbs, 8, LANES), lambda b, i: (b, 0, 0))],
        out_shape=[_sds((S, nb * LANES), BF16), _sds((S, nkb * LANES), F32), _sds((S, nkb * LANES), F32), _sds((nb, 8, LANES), F32)],
        scratch_shapes=[pltpu.VMEM((nbs, 2 * tq), F32)],
        compiler_params=_params(("arbitrary", "arbitrary")))(q, k, kT, v, o, do, lse, dist, slope, sink)


def _sum_rows(v):
    return jnp.sum(v, axis=0, keepdims=True)


def _norm_mod_bwd(dh, xv, mod_ref, nw_ref, stats_ref):
    r = _rms(xv)
    xn = xv * r
    nw = nw_ref[...]
    stats_ref[0:1, :] += _sum_rows(dh)
    stats_ref[1:2, :] += _sum_rows(dh * (xn * nw))
    dn = dh * (1.0 + mod_ref[1:2, :])
    stats_ref[2:3, :] += _sum_rows(dn * xn)
    return _rms_bwd(xv, r, dn * nw)


def _even_gate_specs(ts):
    return [pl.BlockSpec((ts, 256), lambda i, c=c: (i, c)) for c in (3, 4, 7, 8)]


def _even_post_fwd(oa, olat, proj, x, gate, wuv, woe):
    S = x.shape[0]
    ts = min(ROW_TILE, S)

    def body(oa_ref, ol_ref, ga0_ref, ga1_ref, gb0_ref, gb1_ref, x_ref, gate_ref, wuv_ref, woe_ref, y_ref, x1_ref):
        sa, _ = _silu_and_grad(jnp.concatenate([ga0_ref[...], ga1_ref[...]], axis=1))
        sb, _ = _silu_and_grad(jnp.concatenate([gb0_ref[...], gb1_ref[...]], axis=1))
        olc = jnp.concatenate([ol_ref[hh] for hh in range(B_HEADS)], axis=1).astype(BF16)
        ob = _dot(olc, wuv_ref[...])
        mix = jnp.concatenate([oa_ref[...] * sa, ob * sb], axis=1).astype(BF16)
        y = _dot(mix, woe_ref[...])
        y_ref[...] = y.astype(BF16)
        x1_ref[...] = x_ref[...] + gate_ref[...] * y

    return pl.pallas_call(
        body, name="even_post_fwd", grid=(S // ts,),
        in_specs=[_row_spec(ts, 512), pl.BlockSpec((B_HEADS, ts, LANES), lambda i: (0, i, 0))] + _even_gate_specs(ts) +
                 [_row_spec(ts, D_MODEL), _full_spec((1, D_MODEL)), _full_spec((1024, 512)), _full_spec((1024, D_MODEL))],
        out_specs=[_row_spec(ts, D_MODEL), _row_spec(ts, D_MODEL)],
        out_shape=[_sds((S, D_MODEL), BF16), _sds((S, D_MODEL), F32)],
        compiler_params=_params(("arbitrary",)),
    )(oa, olat, proj, proj, proj, proj, x, gate, wuv, woe)


def _odd_pre_fwd(x, mod, nw, wio):
    S = x.shape[0]
    ts = min(ROW_TILE, S)

    def body(x_ref, mod_ref, nw_ref, wio_ref, h_ref, g_ref, q_ref, k_ref, v_ref, kt_ref, vt_ref):
        xv = x_ref[...]
        h = (xv * _rms(xv) * nw_ref[...]) * (1.0 + mod_ref[1:2, :]) + mod_ref[0:1, :]
        hb = h.astype(BF16)
        h_ref[...] = hb
        proj = jnp.concatenate([_dot(hb, wio_ref[p]) for p in range(N_CHIPS)], axis=1)
        g_ref[...] = proj[:, 1536:2560]
        q_ref[...] = (proj[:, 0:1024] * 0.125).astype(BF16)
        lane = _lane_iota()
        k_v = jnp.concatenate([_dup_heads(proj[:, 1024 + LANES * j:1024 + LANES * (j + 1)], lane) for j in range(2)], axis=1)
        v_v = jnp.concatenate([_dup_heads(proj[:, 1280 + LANES * j:1280 + LANES * (j + 1)], lane) for j in range(2)], axis=1)
        k_ref[...] = k_v.astype(BF16)
        v_ref[...] = v_v.astype(BF16)
        kt_ref[...] = k_v.T.astype(BF16)
        vt_ref[...] = v_v.T.astype(BF16)

    col_spec = pl.BlockSpec((512, ts), lambda i: (0, i))
    return pl.pallas_call(
        body, name="odd_pre_fwd", grid=(S // ts,),
        in_specs=[_row_spec(ts, D_MODEL), _full_spec((3, D_MODEL)), _full_spec((1, D_MODEL)),
                  _full_spec((N_CHIPS, D_MODEL, ODD_IN // N_CHIPS))],
        out_specs=[_row_spec(ts, D_MODEL), _row_spec(ts, 1024), _row_spec(ts, 1024), _row_spec(ts, 512), _row_spec(ts, 512),
                   col_spec, col_spec],
        out_shape=[_sds((S, D_MODEL), BF16), _sds((S, 1024), F32), _sds((S, 1024), BF16), _sds((S, 512), BF16),
                   _sds((S, 512), BF16), _sds((512, S), BF16), _sds((512, S), BF16)],
        compiler_params=_params(("arbitrary",)),
    )(x, mod, nw, wio)


def _odd_post(oc, g, x1, gate, woo, fw, tgt):
    S = x1.shape[0]
    ts = min(ROW_TILE, S)
    nsteps = S // ts

    def body(oc_ref, g_ref, x_ref, gate_ref, woo_ref, fw_ref, tgt_ref, doc_ref, dgc_ref, dx2_ref, dwoo_out, stats_ref, dwoo_ref):
        @pl.when(pl.program_id(0) == 0)
        def _():
            dwoo_ref[...] = jnp.zeros((D_MODEL, D_MODEL), F32)
            stats_ref[...] = jnp.zeros((8, D_MODEL), F32)

        ocv = oc_ref[...]
        sg, dsg = _silu_and_grad(g_ref[...])
        mix = (ocv * sg).astype(BF16)
        woo_v = woo_ref[...]
        y = _dot(mix, woo_v)
        gate_v = gate_ref[...]
        x2 = x_ref[...] + gate_v * y
        r = _rms(x2)
        fw_v = fw_ref[...]
        xn = x2 * r
        err = xn * fw_v - tgt_ref[...]
        dout = err * (1.0 / D_MODEL)
        dx2 = _rms_bwd(x2, r, dout * fw_v)
        dx2_ref[...] = dx2
        stats_ref[0:1, :] += _sum_rows(dout * xn)
        stats_ref[1:2, :] += _sum_rows(dx2 * y)
        loss_t = 0.5 * jnp.sum(_sum_rows(err * dout), axis=-1, keepdims=True)
        stats_ref[2:3, :] += jnp.broadcast_to(loss_t, (1, D_MODEL))
        dy = (gate_v * dx2).astype(BF16)
        dmix = _dot_nt(dy, woo_v)
        dwoo_ref[...] += _dot_tn(mix, dy)
        doc_ref[...] = (dmix * sg).astype(BF16)
        dgc_ref[...] = (dmix * ocv * dsg).astype(BF16)

        @pl.when(pl.program_id(0) == nsteps - 1)
        def _():
            dwoo_out[...] = dwoo_ref[...].astype(BF16)

    return pl.pallas_call(
        body, name="odd_post", grid=(nsteps,),
        in_specs=[_row_spec(ts, D_MODEL), _row_spec(ts, D_MODEL), _row_spec(ts, D_MODEL), _full_spec((1, D_MODEL)),
                  _full_spec((D_MODEL, D_MODEL)), _full_spec((1, D_MODEL)), _row_spec(ts, D_MODEL)],
        out_specs=[_row_spec(ts, D_MODEL), _row_spec(ts, D_MODEL), _row_spec(ts, D_MODEL),
                   _full_spec((D_MODEL, D_MODEL), single=False), _full_spec((8, D_MODEL), single=False)],
        out_shape=[_sds((S, D_MODEL), BF16), _sds((S, D_MODEL), BF16), _sds((S, D_MODEL), F32), _sds((D_MODEL, D_MODEL), BF16),
                   _sds((8, D_MODEL), F32)],
        scratch_shapes=[pltpu.VMEM((D_MODEL, D_MODEL), F32)],
        compiler_params=_params(("arbitrary",)),
    )(oc, g, x1, gate, woo, fw, tgt)


def _odd_pre_bwd(dq, dk, dv, dgc, h, x, dx_res, mod, nw, wio):
    S = x.shape[0]
    ts = min(IN_PROJ_ROW_TILE, S)
    nsteps = S // ts
    wsh = ODD_IN // N_CHIPS

    def body(dq_ref, dk_ref, dv_ref, dgc_ref, h_ref, x_ref, dxr_ref, mod_ref, nw_ref, wio_ref, dx_ref, dw_ref, stats_ref, dw_acc):
        @pl.when(pl.program_id(0) == 0)
        def _():
            dw_acc[...] = jnp.zeros((N_CHIPS, D_MODEL, wsh), F32)
            stats_ref[...] = jnp.zeros((8, D_MODEL), F32)

        lane = _lane_iota()
        dkv = [_fold_heads(r[:, 2 * LANES * j:2 * LANES * (j + 1)], lane).astype(BF16) for r in (dk_ref, dv_ref) for j in range(2)]
        dproj = jnp.concatenate([dq_ref[...]] + dkv + [dgc_ref[...]], axis=1)
        hv = h_ref[...]
        dh = None
        for p in range(N_CHIPS):
            dp_cols = dproj[:, wsh * p:wsh * (p + 1)]
            part = _dot_nt(dp_cols, wio_ref[p])
            dh = part if dh is None else dh + part
            dw_acc[p] += _dot_tn(hv, dp_cols)
        dx_ref[...] = dxr_ref[...] + _norm_mod_bwd(dh, x_ref[...], mod_ref, nw_ref, stats_ref)

        @pl.when(pl.program_id(0) == nsteps - 1)
        def _():
            dw_ref[...] = dw_acc[...].astype(BF16)

    return pl.pallas_call(
        body, name="odd_pre_bwd", grid=(nsteps,),
        in_specs=[_row_spec(ts, 1024), _row_spec(ts, 512), _row_spec(ts, 512), _row_spec(ts, 1024), _row_spec(ts, D_MODEL),
                  _row_spec(ts, D_MODEL), _row_spec(ts, D_MODEL), _full_spec((3, D_MODEL)), _full_spec((1, D_MODEL)),
                  _full_spec((N_CHIPS, D_MODEL, wsh))],
        out_specs=[_row_spec(ts, D_MODEL), _full_spec((N_CHIPS, D_MODEL, wsh), single=False), _full_spec((8, D_MODEL), single=False)],
        out_shape=[_sds((S, D_MODEL), F32), _sds((N_CHIPS, D_MODEL, wsh), BF16), _sds((8, D_MODEL), F32)],
        scratch_shapes=[pltpu.VMEM((N_CHIPS, D_MODEL, wsh), F32)],
        compiler_params=_params(("arbitrary",)),
    )(dq, dk, dv, dgc, h, x, dx_res, mod, nw, wio)


def _even_post_bwd(dx1, y, oa, olat, proj, gate, wuv, woe):
    S = dx1.shape[0]
    ts = min(ROW_TILE, S)
    nsteps = S // ts

    def body(dx_ref, y_ref, oa_ref, ol_ref, ga0_ref, ga1_ref, gb0_ref, gb1_ref, gate_ref, wuv_ref, woe_ref,
             doa_ref, dga_ref, dgb_ref, dol_ref, dwoe_out, dwuv_ref, stats_ref, dwoe_ref):
        @pl.when(pl.program_id(0) == 0)
        def _():
            dwoe_ref[...] = jnp.zeros((D_MODEL, D_MODEL), F32)
            dwuv_ref[...] = jnp.zeros((1024, 512), F32)
            stats_ref[...] = jnp.zeros((8, D_MODEL), F32)

        dxv = dx_ref[...]
        stats_ref[0:1, :] += _sum_rows(dxv * y_ref[...])
        dy = (gate_ref[...] * dxv).astype(BF16)
        sa, dsa = _silu_and_grad(jnp.concatenate([ga0_ref[...], ga1_ref[...]], axis=1))
        sb, dsb = _silu_and_grad(jnp.concatenate([gb0_ref[...], gb1_ref[...]], axis=1))
        olc = jnp.concatenate([ol_ref[hh] for hh in range(B_HEADS)], axis=1).astype(BF16)
        wuv_v = wuv_ref[...]
        ob = _dot(olc, wuv_v)
        oav = oa_ref[...]
        mix = jnp.concatenate([oav * sa, ob * sb], axis=1).astype(BF16)
        dmix = _dot_nt(dy, woe_ref[...])
        dwoe_ref[...] += _dot_tn(mix, dy)
        dma, dmb = dmix[:, 0:512], dmix[:, 512:1024]
        doa_ref[...] = (dma * sa).astype(BF16)
        dga_ref[...] = (dma * oav * dsa).astype(BF16)
        dgb_ref[...] = (dmb * ob * dsb).astype(BF16)
        dob = (dmb * sb).astype(BF16)
        dol = _dot_nt(dob, wuv_v)
        dwuv_ref[...] += _dot_tn(olc, dob)
        for hh in range(B_HEADS):
            dol_ref[hh] = dol[:, LANES * hh:LANES * (hh + 1)].astype(BF16)

        @pl.when(pl.program_id(0) == nsteps - 1)
        def _():
            dwoe_out[...] = dwoe_ref[...].astype(BF16)

    head_spec = pl.BlockSpec((B_HEADS, ts, LANES), lambda i: (0, i, 0))
    return pl.pallas_call(
        body, name="even_post_bwd", grid=(nsteps,),
        in_specs=[_row_spec(ts, D_MODEL), _row_spec(ts, D_MODEL), _row_spec(ts, 512), head_spec] + _even_gate_specs(ts) +
                 [_full_spec((1, D_MODEL)), _full_spec((1024, 512)), _full_spec((1024, D_MODEL))],
        out_specs=[_row_spec(ts, 512), _row_spec(ts, 512), _row_spec(ts, 512), head_spec,
                   _full_spec((D_MODEL, D_MODEL), single=False), _full_spec((1024, 512), single=False),
                   _full_spec((8, D_MODEL), single=False)],
        out_shape=[_sds((S, 512), BF16), _sds((S, 512), BF16), _sds((S, 512), BF16), _sds((B_HEADS, S, LANES), BF16),
                   _sds((D_MODEL, D_MODEL), BF16), _sds((1024, 512), F32), _sds((8, D_MODEL), F32)],
        scratch_shapes=[pltpu.VMEM((D_MODEL, D_MODEL), F32)],
        compiler_params=_params(("arbitrary",)),
    )(dx1, y, oa, olat, proj, proj, proj, proj, gate, wuv, woe)


def _even_pre_bwd(x, h, proj, dqa, dka, dva, dga, dgb, dqcat, dkcat, dx_res, mod, nw, wie, qn, kn, seg, ca, sa, ct, st,
                  qln, kvln, wuq, wuk):
    S = x.shape[0]
    ts = min(IN_PROJ_ROW_TILE, S)
    nsteps = S // ts

    def body(x_ref, h_ref, proj_ref, dqa_ref, dka_ref, dva_ref, dga_ref, dgb_ref, dqc_ref, dkc_ref, dxr_ref, mod_ref, nw_ref,
             wie_ref, qn_ref, kn_ref, seg_ref, ca_ref, sa_ref, ct_ref, st_ref, qln_ref, kvln_ref, wuq_ref, wuk_ref,
             dx_ref, dwie_out, dwuq_out, dwuk_out, stats_ref, nstats_ref, dwie_ref, dwuq_ref, dwuk_ref, stage):
        @pl.when(pl.program_id(0) == 0)
        def _():
            dwie_ref[...] = jnp.zeros((EVEN_P, D_MODEL), F32)
            dwuq_ref[...] = jnp.zeros((1536, B_Q_LORA), F32)
            dwuk_ref[...] = jnp.zeros((512, 1024), F32)
            stats_ref[...] = jnp.zeros((8, D_MODEL), F32)
            nstats_ref[...] = jnp.zeros((8, 256), F32)

        lane = _lane_iota()
        ca_v, sa_v, ct_v, st_v = ca_ref[...], sa_ref[...], ct_ref[...], st_ref[...]
        seg_v = seg_ref[...]

        def head_norm_bwd(xc, dy, w):
            r = lax.rsqrt(_seg_mean(xc * xc, seg_v) + EPS)
            g = dy * w
            dxc = r * g - xc * (r * r * r) * _seg_mean(xc * g, seg_v)
            return dxc, _sum_rows(dy * (xc * r))

        pieces = []
        dqn = jnp.zeros((1, LANES), F32)
        for cb in range(4):
            sl = slice(LANES * cb, LANES * (cb + 1))
            dy = _rot_bwd(dqa_ref[:, sl] * 0.125, ca_v, sa_v, lane)
            dxc, dw = head_norm_bwd(proj_ref[:, sl], dy, qn_ref[...])
            pieces.append(dxc)
            dqn = dqn + dw
        dxc, dkn = head_norm_bwd(proj_ref[:, 512:640], _rot_bwd(_fold_heads(dka_ref[...], lane), ca_v, sa_v, lane), kn_ref[...])
        pieces += [dxc, _fold_heads(dva_ref[...], lane), dga_ref[...]]
        nstats_ref[0:1, 0:LANES] += dqn + pltpu.roll(dqn, HEAD_DIM, 1)
        nstats_ref[1:2, 0:LANES] += dkn + pltpu.roll(dkn, HEAD_DIM, 1)

        cq = proj_ref[:, 1280:1536]
        rq = _rms(cq)
        cqn_f = cq * rq
        qln_v = qln_ref[...]
        cqn = (cqn_f * qln_v).astype(BF16)
        wuq_v, wuk_v = wuq_ref[...], wuk_ref[...]
        qnope = _dot_nt(cqn, wuq_v[0:512, :]).astype(BF16)
        dqlat = jnp.concatenate([dqc_ref[hh, :, 0:LANES] for hh in range(B_HEADS)], axis=1).astype(BF16)
        dqnope = _dot_nt(dqlat, wuk_v)
        dwuk_ref[...] += _dot_tn(qnope, dqlat)
        dqr = [_rot_bwd(dqc_ref[hh, :, LANES:2 * LANES], ct_v, st_v, lane) for hh in range(B_HEADS)]
        dqb = jnp.concatenate([dqnope] + dqr, axis=1).astype(BF16)
        dcqn = _dot(dqb, wuq_v)
        dwuq_ref[...] += _dot_tn(dqb, cqn)
        nstats_ref[2:3, :] += _sum_rows(dcqn * cqn_f)
        dcq = _rms_bwd(cq, rq, dcqn * qln_v)
        ckv = proj_ref[:, 1536:1664]
        rk = _rms(ckv)
        dckvn = dkc_ref[:, 0:LANES]
        nstats_ref[3:4, 0:LANES] += _sum_rows(dckvn * (ckv * rk))
        dckv = _rms_bwd(ckv, rk, dckvn * kvln_ref[...])
        dkr = _rot_bwd(dkc_ref[:, LANES:2 * LANES], ct_v, st_v, lane)
        pieces += [dcq, dckv, dkr, dgb_ref[...]]
        dproj = jnp.concatenate([piece.astype(BF16) for piece in pieces], axis=1)
        dh = _dot(dproj, wie_ref[...])
        dwie_ref[...] += _dot_tn(dproj, h_ref[...])
        dx_ref[...] = dxr_ref[...] + _norm_mod_bwd(dh, x_ref[...], mod_ref, nw_ref, stats_ref)

        @pl.when(pl.program_id(0) == nsteps - 1)
        def _():
            for r0 in range(0, EVEN_P, 256):
                stage[...] = dwie_ref[r0:r0 + 256, :].astype(BF16)
                pltpu.sync_copy(stage, dwie_out.at[pl.ds(r0, 256), :])
            pltpu.sync_copy(dwuq_ref, dwuq_out)
            pltpu.sync_copy(dwuk_ref, dwuk_out)

    return pl.pallas_call(
        body, name="even_pre_bwd", grid=(nsteps,),
        in_specs=[_row_spec(ts, D_MODEL), _row_spec(ts, D_MODEL), _row_spec(ts, EVEN_P), _row_spec(ts, 512), _row_spec(ts, 2 * LANES),
                  _row_spec(ts, 2 * LANES), _row_spec(ts, 512), _row_spec(ts, 512),
                  pl.BlockSpec((B_HEADS, ts, 2 * LANES), lambda i: (0, i, 0)), _row_spec(ts, 2 * LANES), _row_spec(ts, D_MODEL),
                  _full_spec((3, D_MODEL)), _full_spec((1, D_MODEL)), _full_spec((EVEN_P, D_MODEL)),
                  _full_spec((1, LANES)), _full_spec((1, LANES)), _full_spec((LANES, LANES)),
                  _row_spec(ts, LANES), _row_spec(ts, LANES), _row_spec(ts, LANES), _row_spec(ts, LANES),
                  _full_spec((1, B_Q_LORA)), _full_spec((1, B_KV_LORA)), _full_spec((1536, B_Q_LORA)), _full_spec((512, 1024))],
        out_specs=[_row_spec(ts, D_MODEL), _ANY, _ANY, _ANY, _full_spec((8, D_MODEL), single=False), _full_spec((8, 256), single=False)],
        out_shape=[_sds((S, D_MODEL), F32), _sds((EVEN_P, D_MODEL), BF16), _sds((1536, B_Q_LORA), F32), _sds((512, 1024), F32),
                   _sds((8, D_MODEL), F32), _sds((8, 256), F32)],
        scratch_shapes=[pltpu.VMEM((EVEN_P, D_MODEL), F32), pltpu.VMEM((1536, B_Q_LORA), F32), pltpu.VMEM((512, 1024), F32),
                        pltpu.VMEM((256, D_MODEL), BF16)],
        compiler_params=_params(("arbitrary",)),
    )(x, h, proj, dqa, dka, dva, dga, dgb, dqcat, dkcat, dx_res, mod, nw, wie, qn, kn, seg, ca, sa, ct, st, qln, kvln, wuq, wuk)


def _ada_fwd(c_all, w, b):
    n = w.shape[2]

    def body(c_ref, w_ref, b_ref, o_ref):
        cv = c_ref[...]
        o_ref[0] = _dot_f32(cv * _sigmoid(cv), w_ref[0]) + b_ref[0]

    return pl.pallas_call(
        body, name="ada_fwd", grid=(2,),
        in_specs=[pl.BlockSpec((N_DEV, D_MODEL), lambda l: (0, 0)), pl.BlockSpec((1, D_MODEL, n), lambda l: (l, 0, 0)),
                  pl.BlockSpec((1, 1, n), lambda l: (l, 0, 0))],
        out_specs=pl.BlockSpec((1, N_DEV, n), lambda l: (l, 0, 0)),
        out_shape=_sds((2, N_DEV, n), F32),
        compiler_params=_params(("arbitrary",)),
    )(c_all, w, b)


def _ada_bwd(c_all_t, dmod):
    n = dmod.shape[2]

    def body(c_ref, d_ref, o_ref):
        cv = c_ref[...]
        act = cv * _sigmoid(cv)
        dv = d_ref[0]
        acc = act[:, 0:1] * dv[0:1, :]
        for bb in range(1, N_DEV):
            acc = acc + act[:, bb:bb + 1] * dv[bb:bb + 1, :]
        o_ref[0] = acc

    return pl.pallas_call(
        body, name="ada_bwd", grid=(2,),
        in_specs=[pl.BlockSpec((D_MODEL, N_DEV), lambda l: (0, 0)), pl.BlockSpec((1, N_DEV, n), lambda l: (l, 0, 0))],
        out_specs=pl.BlockSpec((1, D_MODEL, n), lambda l: (l, 0, 0)),
        out_shape=_sds((2, D_MODEL, n), F32),
        compiler_params=_params(("arbitrary",)),
    )(c_all_t, dmod)


ADAM_ROW_TILE = 512


def _adam_update(g, w, m, v):
    m_new = ADAM_B1 * m + (1.0 - ADAM_B1) * g
    v_new = ADAM_B2 * v + (1.0 - ADAM_B2) * jnp.square(g)
    m_hat = m_new / (1.0 - ADAM_B1 ** ADAM_STEP)
    v_hat = v_new / (1.0 - ADAM_B2 ** ADAM_STEP)
    return -ADAM_LR * (m_hat / (jnp.sqrt(v_hat) + ADAM_EPS) + ADAM_WD * w), m_new, v_new


SMALL_ROWS = dict(dmod=(0, D_MODEL), norm_w=(6, D_MODEL), final_norm=(8, D_MODEL), a_q_norm=(9, HEAD_DIM), a_k_norm=(10, HEAD_DIM),
                  b_q_lora_norm=(11, B_Q_LORA), b_kv_lora_norm=(12, B_KV_LORA), c_sink=(13, C_HEADS))
SMALL_WEIGHTS = ("ada_b", "norm_w", "final_norm", "a_q_norm", "a_k_norm", "b_q_lora_norm", "b_kv_lora_norm", "c_sink")
LOSS_ROW = 14


def _pack_small(res):
    def padded(v):
        return jnp.concatenate([v, jnp.zeros((v.shape[0], D_MODEL - v.shape[1]), F32)], axis=1)

    rows = [res["dmod"].reshape(6, D_MODEL), res["norm_w"], res["final_norm"].reshape(1, D_MODEL)]
    rows += [padded(res[k]) for k in ("a_q_norm", "a_k_norm", "b_q_lora_norm", "b_kv_lora_norm", "c_sink")]
    return jnp.concatenate(rows + [res["loss_row"], jnp.zeros((1, D_MODEL), F32)], axis=0)


def _adam_small(parts, ws, ms, vs):
    nw = len(SMALL_WEIGHTS)

    def body(*refs):
        p_ref = refs[0]
        w_refs, m_refs, v_refs = refs[1:1 + nw], refs[1 + nw:1 + 2 * nw], refs[1 + 2 * nw:1 + 3 * nw]
        outs = refs[1 + 3 * nw:]
        g_all = p_ref[0]
        for k in range(1, N_DEV):
            g_all = g_all + p_ref[k]
        for idx, name in enumerate(SMALL_WEIGHTS):
            if name == "ada_b":
                g = jnp.concatenate([jnp.concatenate([g_all[3 * l + t:3 * l + t + 1] for t in range(3)], axis=1) for l in range(2)],
                                    axis=0)
            else:
                row, width = SMALL_ROWS[name]
                g = g_all[row:row + w_refs[idx].shape[0], 0:width]
            d, m_new, v_new = _adam_update(g, w_refs[idx][...], m_refs[idx][...], v_refs[idx][...])
            outs[4 * idx][...], outs[4 * idx + 1][...], outs[4 * idx + 2][...], outs[4 * idx + 3][...] = g, d, m_new, v_new
        outs[4 * nw][...] = g_all[LOSS_ROW:LOSS_ROW + 1, 0:LANES]

    out_shape = []
    for w in ws:
        out_shape += [_sds(w.shape, F32)] * 4
    out_shape.append(_sds((1, LANES), F32))
    return pl.pallas_call(body, name="adam_small", out_shape=out_shape,
                          compiler_params=pltpu.CompilerParams(vmem_limit_bytes=VMEM_LIMIT))(parts, *ws, *ms, *vs)


def _adam(parts, w, m, v, name, by_columns=False):
    P, R, C = parts.shape
    if by_columns:
        tr, tc = R, 256
    else:
        tr, tc = (R if R <= ADAM_ROW_TILE else ADAM_ROW_TILE), C
    assert R % tr == 0 and C % tc == 0

    def body(p_ref, w_ref, m_ref, v_ref, g_ref, d_ref, nm_ref, nv_ref):
        g = p_ref[0].astype(F32)
        for k in range(1, P):
            g = g + p_ref[k].astype(F32)
        g_ref[...] = g
        d_ref[...], nm_ref[...], nv_ref[...] = _adam_update(g, w_ref[...], m_ref[...], v_ref[...])

    tile = (lambda i: (0, i)) if by_columns else (lambda i: (i, 0))
    spec = pl.BlockSpec((tr, tc), tile)
    return pl.pallas_call(
        body, name=name, grid=(C // tc if by_columns else R // tr,),
        in_specs=[pl.BlockSpec((P, tr, tc), lambda i: (0,) + tile(i)), spec, spec, spec],
        out_specs=[spec, spec, spec, spec], out_shape=[_sds((R, C), F32)] * 4,
        compiler_params=_params(("arbitrary",)),
    )(parts, w, m, v)


_ANY = pl.BlockSpec(memory_space=pl.ANY)
CHIP_FLIPS = ((1, 0), (0, 1), (1, 1))
DEV_FLIPS = tuple((dx, dy, dc) for dx in (0, 1) for dy in (0, 1) for dc in (0, 1) if dx + dy + dc)


def _flip(a, d):
    return a if d == 0 else 1 - a


def _my_place():
    return lax.axis_index("x"), lax.axis_index("y"), lax.axis_index("c")


def _gather8_copies(ins, outs, send_sems, recv_sems, loc_sems):
    x, y, c = _my_place()
    me = 4 * x + 2 * y + c
    copies = []
    for a in range(len(ins)):
        copies.append(pltpu.make_async_copy(ins[a], outs[a].at[me], loc_sems.at[a]))
        for k, (dx, dy, dc) in enumerate(DEV_FLIPS):
            copies.append(pltpu.make_async_remote_copy(
                src_ref=ins[a], dst_ref=outs[a].at[me], send_sem=send_sems.at[a, k], recv_sem=recv_sems.at[a, k],
                device_id=(_flip(x, dx), _flip(y, dy), _flip(c, dc)), device_id_type=MESH_ID))
    return copies


def _gather8_sems(n):
    return [pltpu.SemaphoreType.DMA((n, 7)), pltpu.SemaphoreType.DMA((n, 7)), pltpu.SemaphoreType.DMA((n,))]


def _gather_dev8(arrs, name):
    n = len(arrs)

    def body(*refs):
        copies = _gather8_copies(refs[:n], refs[n:2 * n], *refs[2 * n:])
        for cp in copies:
            cp.start()
        for cp in copies:
            cp.wait()

    return pl.pallas_call(
        body, name=name, in_specs=[_ANY] * n, out_specs=[_ANY] * n,
        out_shape=[_sds((N_DEV,) + a.shape, a.dtype) for a in arrs], scratch_shapes=_gather8_sems(n),
    )(*arrs)


class _Exchange:
    def __init__(self, arrs, out_shapes, n_sems, phases):
        self.arrs, self.out_shapes, self.n_sems, self._phases = list(arrs), list(out_shapes), n_sems, phases

    @property
    def n(self):
        return len(self.arrs)

    def sem_shapes(self):
        return [pltpu.SemaphoreType.DMA((self.n, self.n_sems)), pltpu.SemaphoreType.DMA((self.n, self.n_sems)),
                pltpu.SemaphoreType.DMA((self.n,))]

    def phases(self, ins, outs, sems):
        return self._phases(ins, outs, *sems)

    def run(self, name):
        n = self.n

        def body(*refs):
            start, mid, end = self.phases(refs[:n], refs[n:2 * n], refs[2 * n:])
            start()
            mid()
            end()

        return pl.pallas_call(body, name=name, in_specs=[_ANY] * n, out_specs=[_ANY] * n, out_shape=self.out_shapes,
                              scratch_shapes=self.sem_shapes())(*self.arrs)

def _gather_halves_phases(ins, outs, send_sems, recv_sems, loc_sems):
    n = len(ins)
    x, y, c = _my_place()
    chip = 2 * x + y
    sibling = (x, y, 1 - c)
    peers = [(_flip(x, dx), _flip(y, dy)) for dx, dy in CHIP_FLIPS]

    def remote(src, p, half, a, k, to):
        return pltpu.make_async_remote_copy(src_ref=src, dst_ref=outs[a].at[p, half], send_sem=send_sems.at[a, k],
                                            recv_sem=recv_sems.at[a, k], device_id=to, device_id_type=MESH_ID)

    def local(a):
        return pltpu.make_async_copy(ins[a], outs[a].at[chip], loc_sems.at[a])

    def first(a, k):
        return remote(ins[a].at[c], chip, c, a, k, (*peers[k], c))

    def passed(a, k):
        p = 2 * peers[k][0] + peers[k][1]
        return remote(outs[a].at[p, c], p, c, a, 3 + k, sibling)

    def start():
        for a in range(n):
            local(a).start()
            for k in range(3):
                first(a, k).start()

    def mid():
        for a in range(n):
            for k in range(3):
                p = 2 * peers[k][0] + peers[k][1]
                remote(outs[a].at[p, c], p, c, a, k, sibling).wait_recv()
                passed(a, k).start()

    def end():
        for a in range(n):
            for k in range(3):
                p = 2 * peers[k][0] + peers[k][1]
                remote(outs[a].at[p, 1 - c], p, 1 - c, a, 3 + k, sibling).wait_recv()
        for a in range(n):
            for k in range(3):
                first(a, k).wait_send()
                passed(a, k).wait_send()
            local(a).wait()

    return start, mid, end


def _gather_chip4_halves(arrs):
    return _Exchange(arrs, [_sds((N_CHIPS,) + a.shape, a.dtype) for a in arrs], 6, _gather_halves_phases)


def _reduce_phases(n_whole, ins, outs, send_sems, recv_sems, loc_sems):
    n = len(ins)
    x, y, c = _my_place()
    chip = 2 * x + y
    sibling = (x, y, 1 - c)
    peers = [(_flip(x, dx), _flip(y, dy)) for dx, dy in CHIP_FLIPS]

    def remote(src, slot, a, k, to):
        return pltpu.make_async_remote_copy(src_ref=src, dst_ref=outs[a].at[slot], send_sem=send_sems.at[a, k],
                                            recv_sem=recv_sems.at[a, k], device_id=to, device_id_type=MESH_ID)

    def block(a, p):
        return ins[a] if a >= n - n_whole else ins[a].at[p]

    def local(a):
        return pltpu.make_async_copy(block(a, chip), outs[a].at[2 * chip + c], loc_sems.at[a])

    def own(a):
        return remote(block(a, chip), 2 * chip + c, a, 0, sibling)

    def first(a, k):
        return remote(block(a, 2 * peers[k][0] + peers[k][1]), 2 * chip + c, a, 1 + k, (*peers[k], c))

    def passed(a, k):
        slot = 2 * (2 * peers[k][0] + peers[k][1]) + c
        return remote(outs[a].at[slot], slot, a, 4 + k, sibling)

    def start():
        for a in range(n):
            local(a).start()
            own(a).start()
            for k in range(3):
                first(a, k).start()

    def mid():
        for a in range(n):
            for k in range(3):
                slot = 2 * (2 * peers[k][0] + peers[k][1]) + c
                remote(outs[a].at[slot], slot, a, 1 + k, sibling).wait_recv()
                passed(a, k).start()

    def end():
        for a in range(n):
            remote(outs[a].at[2 * chip + 1 - c], 2 * chip + 1 - c, a, 0, sibling).wait_recv()
            for k in range(3):
                slot = 2 * (2 * peers[k][0] + peers[k][1]) + 1 - c
                remote(outs[a].at[slot], slot, a, 4 + k, sibling).wait_recv()
        for a in range(n):
            own(a).wait_send()
            for k in range(3):
                first(a, k).wait_send()
                passed(a, k).wait_send()
            local(a).wait()

    return start, mid, end


def _reduce_exchange(arrs, whole=()):
    shapes = [_sds((N_DEV,) + a.shape[1:], a.dtype) for a in arrs] + [_sds((N_DEV,) + a.shape, a.dtype) for a in whole]
    return _Exchange(list(arrs) + list(whole), shapes, 7, functools.partial(_reduce_phases, len(whole)))


def _shard_halves_t(w):
    wt = w.T.astype(BF16)
    n2 = wt.shape[0] // 2
    pad = jnp.zeros((-n2 % 16, wt.shape[1]), BF16)
    return jnp.stack([jnp.concatenate([wt[0:n2], pad], axis=0), jnp.concatenate([wt[n2:], pad], axis=0)])


def _gathered_rows(g, n):
    return [g[p, half, 0:n // 2] for p in range(N_CHIPS) for half in range(2)]


def _even_in_layout_t(g):
    n2 = EVEN_IN // N_CHIPS // 2
    gap, gap_rows = 1696, EVEN_P - EVEN_IN
    assert n2 % 2 == 0 and gap % 2 == 0 and gap_rows % 2 == 0
    spans = []
    for p in range(N_CHIPS):
        for half in range(2):
            lo = (2 * p + half) * n2
            if lo < gap < lo + n2:
                spans += [(p, half, 0, (gap - lo) // 2, lo // 2), (p, half, (gap - lo) // 2, (lo + n2 - gap) // 2, (gap + gap_rows) // 2)]
            else:
                spans.append((p, half, 0, n2 // 2, (lo + (gap_rows if lo >= gap else 0)) // 2))

    def body(g_ref, o_ref, pairs):
        pairs[pl.ds(gap // 2, gap_rows // 2), :] = jnp.zeros((gap_rows // 2, pairs.shape[1]), jnp.uint32)
        for p, half, src, rows, dst in spans:
            pairs[pl.ds(dst, rows), :] = pltpu.bitcast(g_ref[p, half], jnp.uint32)[src:src + rows]
        step = 128
        for r0 in range(0, EVEN_P // 2, step):
            o_ref[pl.ds(2 * r0, 2 * step), :] = pltpu.bitcast(pairs[pl.ds(r0, step), :], o_ref.dtype)

    return pl.pallas_call(body, name="even_w_in_rows", out_shape=_sds((EVEN_P, g.shape[-1]), g.dtype),
                          scratch_shapes=[pltpu.VMEM((EVEN_P // 2, g.shape[-1]), jnp.uint32)],
                          compiler_params=pltpu.CompilerParams(vmem_limit_bytes=VMEM_LIMIT))(g)


def _even_in_unlayout_t(g):
    return jnp.concatenate([g[0:1696], g[1792:2304]], axis=0)


def _uq_layout_t(g):
    wt = jnp.concatenate(_gathered_rows(g, B_HEADS * (B_NOPE + B_ROPE) // N_CHIPS), axis=0)
    per = B_NOPE + B_ROPE
    pad = jnp.zeros((LANES - B_ROPE, wt.shape[1]), wt.dtype)
    nope = [wt[per * h:per * h + B_NOPE] for h in range(B_HEADS)]
    rope = [jnp.concatenate([wt[per * h + B_NOPE:per * (h + 1)], pad], axis=0) for h in range(B_HEADS)]
    return jnp.concatenate(nope + rope, axis=0)


def _uq_unlayout_t(g):
    parts = []
    for h in range(B_HEADS):
        parts += [g[B_NOPE * h:B_NOPE * (h + 1)], g[512 + LANES * h:512 + LANES * h + B_ROPE]]
    return jnp.concatenate(parts, axis=0)


def _block_diag(blocks):
    rows = []
    for h, blk in enumerate(blocks):
        r, cdim = blk.shape
        n = len(blocks)
        rows.append(jnp.concatenate([jnp.zeros((r, cdim * h), blk.dtype), blk, jnp.zeros((r, cdim * (n - 1 - h)), blk.dtype)],
                                    axis=1))
    return jnp.concatenate(rows, axis=0)


def _uk_layout(w):
    return _block_diag([w[:, h, :].T for h in range(B_HEADS)])


def _latent_rows(w):
    return jnp.transpose(w[0], (1, 2, 0)).reshape(-1, w.shape[1])


def _latent_unrows(w2, shape):
    return jnp.transpose(w2.reshape(shape[2], shape[3], shape[1]), (2, 0, 1)).reshape(shape)


def _uk_unlayout(g):
    return jnp.concatenate([g[B_NOPE * h:B_NOPE * (h + 1), LANES * h:LANES * (h + 1)] for h in range(B_HEADS)], axis=0)


def _uv_layout(w):
    return _block_diag([w[:, h, :] for h in range(B_HEADS)])


def _uv_unlayout(g):
    return jnp.concatenate([g[LANES * h:LANES * (h + 1), B_V * h:B_V * (h + 1)].T for h in range(B_HEADS)], axis=0)


def _rope_tables(S):
    inv = ROPE_THETA ** (-jnp.arange(0, 32, 2, dtype=F32) / 32)
    tok = jnp.arange(S)

    def tab(pos):
        ang = pos.astype(F32)[:, None] * inv[None, :]
        cos, sin = jnp.cos(ang), jnp.sin(ang)
        return jnp.concatenate([cos, cos], axis=1), jnp.concatenate([-sin, sin], axis=1)

    cr, sr = tab(tok // GRID_W)
    cc, sc = tab(tok % GRID_W)
    ct, st = tab(tok)
    return (jnp.tile(jnp.concatenate([cr, cc], axis=1), (1, 2)), jnp.tile(jnp.concatenate([sr, sc], axis=1), (1, 2)),
            jnp.tile(ct, (1, 4)), jnp.tile(st, (1, 4)))


A_TQ, A_TK, A_SUB = 512, 4096, 512
A_FWD_SUB = 1024
B_TQ, B_TK, B_SUB = 128, 4096, 1024
B_BWD_TK, B_BWD_SUB = 4096, 512
C_T = 256
C_BLOCKS_PER_STEP = 8
KV_SHARE = 2


def _local_step(x0, tgt, mod, norm_w, wie, wuq, wuk, wuv, late_shards, a_q_norm, a_k_norm, q_lora_norm, kv_lora_norm,
                c_sink, final_norm):
    S = x0.shape[0]
    mod3 = mod.reshape(2, 3, D_MODEL)
    ca, sa, ct, st = _rope_tables(S)
    lane_seg = np.arange(LANES) // HEAD_DIM
    seg = jnp.asarray((lane_seg[:, None] == lane_seg[None, :]).astype(np.float32)).astype(BF16)
    qn = jnp.tile(a_q_norm.reshape(1, HEAD_DIM), (1, 2))
    kn = jnp.tile(a_k_norm.reshape(1, HEAD_DIM), (1, 2))
    qln, kvln = q_lora_norm.reshape(1, B_Q_LORA), kv_lora_norm.reshape(1, B_KV_LORA)
    nw0, nw1 = norm_w[0:1], norm_w[1:2]
    gate0, gate1 = mod3[0, 2:3], mod3[1, 2:3]
    a_tq, a_tk, b_tq, b_tk, bb_tk, c_t = min(A_TQ, S), min(A_TK, S), min(B_TQ, S), min(B_TK, S), min(B_BWD_TK, S), min(C_T, S)
    a_sub, b_sub, bb_sub = min(A_SUB, a_tk), min(B_SUB, b_tk), min(B_BWD_SUB, bb_tk)

    h0, proj_e, qa, ka, va, qcat, kcat, ka_t, va_t, kcat_t = _even_pre_fwd(x0, mod3[0], nw0, wie, qn, kn, seg, ca, sa, ct, st,
                                                                           qln, kvln, wuq, wuk)
    oa, lse_a, woe_g, wio_g, woo_g = _pp_fwd(qa, ka, va_t, kdiv=KV_SHARE, tq=a_tq, tk=a_tk, sub=min(A_FWD_SUB, a_tk), name="attn_a_fwd",
                                             side=_gather_chip4_halves(late_shards))
    woe = woe_g.reshape(D_MODEL, D_MODEL)
    wio = wio_g.reshape(N_CHIPS, D_MODEL, ODD_IN // N_CHIPS)
    woo = woo_g.reshape(D_MODEL, D_MODEL)
    olat, lse_b = _mla_fwd(qcat, kcat, kcat_t, tq=b_tq, tk=b_tk, sub=b_sub)
    y0, x1 = _even_post_fwd(oa, olat, proj_e, x0, gate0, wuv, woe)
    h1, gc, qc, kc, vc, kc_t, vc_t = _odd_pre_fwd(x1, mod3[1], nw1, wio)
    slopes = 2.0 ** (-8.0 * jnp.arange(1, C_HEADS + 1, dtype=F32) / C_HEADS)
    slope_rows = jnp.repeat(slopes.reshape(C_HEADS // 2, 2), c_t, axis=1)[:, None, :]
    sink_rows = jnp.repeat(c_sink.reshape(C_HEADS // 2, 2), c_t, axis=1)[:, None, :]
    win_dist = _win_dist_table(S, c_t)
    oc, lse_c = _win_fwd(qc, kc, vc_t, win_dist, slope_rows, sink_rows, kdiv=KV_SHARE, tq=c_t, nbs=C_BLOCKS_PER_STEP,
                         name="attn_c_fwd")
    doc, dgc, dx2, dwoo, st_f = _odd_post(oc, gc, x1, gate1, woo, final_norm.reshape(1, D_MODEL), tgt)
    dqc, dkc, dvc, dsink_raw = _win_bwd(qc, kc, kc_t, vc, oc, doc, lse_c, win_dist, slope_rows, sink_rows, kdiv=KV_SHARE, tq=c_t,
                                        nbs=C_BLOCKS_PER_STEP, name="attn_c_bwd")
    dx1, dwio, st_1 = _odd_pre_bwd(dqc, dkc, dvc, dgc, h1, x1, dx2, mod3[1], nw1, wio)
    doa, dga, dgb, dolat, dwoe, dwuv, st_e = _even_post_bwd(dx1, y0, oa, olat, proj_e, gate0, wuv, woe)
    late_grads = _reduce_exchange([dwoe.reshape(N_CHIPS, D_MODEL // N_CHIPS, D_MODEL), dwio,
                                   dwoo.reshape(N_CHIPS, D_MODEL // N_CHIPS, D_MODEL)])
    dqa, dka, dva, p_woe, p_wio, p_woo = _pp_bwd(qa, ka, ka_t, va, oa, doa, lse_a, kdiv=KV_SHARE, tq=a_tq, tk=a_tk, sub=a_sub,
                                                 name="attn_a_bwd", side=late_grads)
    dqcat, dkcat = _mla_bwd(qcat, kcat, kcat_t, olat, dolat, lse_b, tq=b_tq, tk=bb_tk, sub=bb_sub)
    dx0, dwie, dwuq, dwuk, st_0, nst = _even_pre_bwd(x0, h0, proj_e, dqa, dka, dva, dga, dgb, dqcat, dkcat, dx1, mod3[0], nw0,
                                                     wie, qn, kn, seg, ca, sa, ct, st, qln, kvln, wuq, wuk)
    dsink_pairs = jnp.stack([dsink_raw[:, 0, 0], dsink_raw[:, 1, 0]], axis=1).reshape(C_HEADS)
    return dict(
        loss_row=st_f[2:3], dx=dx0,
        dmod=jnp.stack([jnp.concatenate([st_0[0], st_0[1], st_e[0]]), jnp.concatenate([st_1[0], st_1[1], st_f[1]])]),
        norm_w=jnp.stack([st_0[2], st_1[2]]), final_norm=st_f[0],
        a_q_norm=nst[0:1, 0:HEAD_DIM], a_k_norm=nst[1:2, 0:HEAD_DIM], b_q_lora_norm=nst[2:3, :], b_kv_lora_norm=nst[3:4, 0:LANES],
        c_sink=dsink_pairs.reshape(1, C_HEADS),
        even_w_in=dwie, b_w_uq=dwuq, b_w_uk=dwuk, b_w_uv=dwuv, even_w_out=p_woe, odd_w_in=p_wio, odd_w_out=p_woo)


WEIGHT_NAMES = ("norm_w", "ada_w", "ada_b", "even_w_in", "a_q_norm", "a_k_norm", "b_q_lora_norm", "b_kv_lora_norm", "b_w_uq",
                "b_w_uk", "b_w_uv", "even_w_out", "odd_w_in", "c_sink", "odd_w_out", "final_norm")


def kernel(x, c, norm_w, ada_w, ada_b, even_w_in, a_q_norm, a_k_norm, b_q_lora_norm, b_kv_lora_norm, b_w_uq, b_w_uk, b_w_uv, even_w_out, odd_w_in, c_sink, odd_w_out, final_norm, loss_target, m_norm_w, m_ada_w, m_ada_b, m_even_w_in, m_a_q_norm, m_a_k_norm, m_b_q_lora_norm, m_b_kv_lora_norm, m_b_w_uq, m_b_w_uk, m_b_w_uv, m_even_w_out, m_odd_w_in, m_c_sink, m_odd_w_out, m_final_norm, v_norm_w, v_ada_w, v_ada_b, v_even_w_in, v_a_q_norm, v_a_k_norm, v_b_q_lora_norm, v_b_kv_lora_norm, v_b_w_uq, v_b_w_uk, v_b_w_uv, v_even_w_out, v_odd_w_in, v_c_sink, v_odd_w_out, v_final_norm):
    given = dict(locals())
    xi, yi, ci = _my_place()
    chip = 2 * xi + yi
    dev = 2 * chip + ci
    n_ada = ada_w.shape[2]

    (c_all,) = _gather_dev8([c], "gather_c")
    c_all = c_all.reshape(N_DEV, D_MODEL)
    bias = lax.dynamic_slice_in_dim(ada_b, chip * n_ada, n_ada, axis=1).reshape(2, 1, n_ada)
    mod_cols = _ada_fwd(c_all, ada_w, bias)
    def halves(w):
        return w.astype(BF16).reshape((2, w.shape[0] // 2) + w.shape[1:])

    mod_all, wie_g, wuq_g = _gather_chip4_halves(
        [mod_cols, _shard_halves_t(even_w_in[0]), _shard_halves_t(b_w_uq[0])]).run("gather_weights")
    mod = jnp.transpose(lax.dynamic_index_in_dim(mod_all, dev, axis=2, keepdims=False), (1, 0, 2)).reshape(2, 3 * D_MODEL)

    res = _local_step(
        x[0], loss_target[0], mod, norm_w,
        _even_in_layout_t(wie_g), _uq_layout_t(wuq_g), _uk_layout(b_w_uk[0].astype(BF16)),
        _uv_layout(b_w_uv[0].astype(BF16)), [halves(even_w_out[0]), halves(odd_w_in[0]), halves(odd_w_out[0])],
        a_q_norm, a_k_norm, b_q_lora_norm, b_kv_lora_norm, c_sink, final_norm)

    p_wie, p_wuq, small_all, p_wuk, p_wuv = _reduce_exchange(
        [_even_in_unlayout_t(res["even_w_in"]).reshape(N_CHIPS, EVEN_IN // N_CHIPS, D_MODEL),
         _uq_unlayout_t(res["b_w_uq"]).astype(BF16).reshape(N_CHIPS, -1, B_Q_LORA)],
        whole=[_pack_small(res), _uk_unlayout(res["b_w_uk"]).astype(BF16), _uv_unlayout(res["b_w_uv"]).astype(BF16)],
    ).run("reduce_exchange")
    shard_parts = dict(even_w_in=p_wie, b_w_uq=p_wuq, **{k: res[k] for k in ("even_w_out", "odd_w_in", "odd_w_out")})
    dmod_all = small_all[:, 0:6, :].reshape(N_DEV, 2, 3 * D_MODEL)
    dmod_cols = jnp.transpose(lax.dynamic_slice_in_dim(dmod_all, chip * n_ada, n_ada, axis=2), (1, 0, 2))
    parts = dict(shard_parts)
    parts["ada_w"] = _ada_bwd(c_all.T, dmod_cols).reshape(1, 2 * D_MODEL, n_ada)
    parts["b_w_uk"], parts["b_w_uv"] = p_wuk, p_wuv

    def as2d(a):
        return a.reshape((-1, a.shape[-1]) if a.ndim > 1 else (1, a.shape[0]))

    results = {}
    small_outs = _adam_small(small_all, *[[as2d(given[pre + k]) for k in SMALL_WEIGHTS] for pre in ("", "m_", "v_")])
    for idx, k in enumerate(SMALL_WEIGHTS):
        results[k] = small_outs[4 * idx:4 * idx + 4]
    for k, p in parts.items():
        if k in ("even_w_in", "b_w_uq"):
            outs = _adam(p, given[k][0].T, given["m_" + k][0].T, given["v_" + k][0].T, "adam_" + k, by_columns=k == "even_w_in")
            results[k] = [o.T for o in outs]
            continue
        if k in ("b_w_uk", "b_w_uv"):
            outs = _adam(p, _latent_rows(given[k]), _latent_rows(given["m_" + k]), _latent_rows(given["v_" + k]), "adam_" + k)
            results[k] = [_latent_unrows(o, given[k].shape) for o in outs]
            continue
        shape2 = (p.shape[-2], p.shape[-1])
        results[k] = _adam(p, given[k].reshape(shape2), given["m_" + k].reshape(shape2), given["v_" + k].reshape(shape2),
                           "adam_" + k)
    by_kind = [[results[k][t].reshape(given[k].shape) for k in WEIGHT_NAMES] for t in range(4)]
    return (small_outs[-1][0, 0], res["dx"][None], *by_kind[0], *by_kind[1], *by_kind[2], *by_kind[3])
```

```python
import functools

import numpy as np
import jax
import jax.numpy as jnp
from jax import lax
from jax.experimental import pallas as pl
from jax.experimental.pallas import tpu as pltpu

F32 = jnp.float32
BF16 = jnp.bfloat16
HIGHEST = lax.Precision.HIGHEST
MESH_ID = pl.DeviceIdType.MESH

D_MODEL = 1024
HEAD_DIM = 64
GRID_W = 64
EPS = 1e-6
ROPE_THETA = 10000.0
B_HEADS, B_NOPE, B_ROPE, B_V = 8, 64, 32, 64
B_Q_LORA, B_KV_LORA = 256, 128
C_HEADS = 16
WINDOW = 128
EVEN_IN, ODD_IN = 2208, 2560
EVEN_P = 2304
EVEN_GAP = 1696
N_CHIPS, N_DEV = 4, 8
LANES = 128
NEG = -1e30
VMEM_LIMIT = 60 * 1024 * 1024

ADAM_LR, ADAM_B1, ADAM_B2, ADAM_EPS, ADAM_WD, ADAM_STEP = 0.001, 0.9, 0.999, 1e-08, 0.01, 10

ROW_TILE = 512
IN_PROJ_ROW_TILE = 256


def _dot(a, b):
    return lax.dot_general(a, b, (((1,), (0,)), ((), ())), preferred_element_type=F32)


def _dot_nt(a, b):
    return lax.dot_general(a, b, (((1,), (1,)), ((), ())), preferred_element_type=F32)


def _dot_tn(a, b):
    return lax.dot_general(a, b, (((0,), (0,)), ((), ())), preferred_element_type=F32)


def _dot_f32(a, b):
    return lax.dot_general(a, b, (((1,), (0,)), ((), ())), precision=HIGHEST, preferred_element_type=F32)


def _sigmoid(x):
    return 1.0 / (1.0 + jnp.exp(-x))


def _silu_and_grad(g):
    s = _sigmoid(g)
    return g * s, s * (1.0 + g * (1.0 - s))


def _lane_iota():
    return lax.broadcasted_iota(jnp.int32, (1, LANES), 1)


def _partner(x, lane):
    return jnp.where((lane % 32) < 16, pltpu.roll(x, LANES - 16, 1), pltpu.roll(x, 16, 1))


def _rot(x, cos, sin_signed, lane):
    return x * cos + _partner(x, lane) * sin_signed


def _rot_bwd(dy, cos, sin_signed, lane):
    return dy * cos + _partner(dy * sin_signed, lane)


def _rms(x):
    return lax.rsqrt(jnp.mean(x * x, axis=-1, keepdims=True) + EPS)


def _rms_bwd(x, r, g):
    return r * g - x * (r * r * r) * jnp.mean(x * g, axis=-1, keepdims=True)


def _seg_mean(v, seg_ones):
    hi = v.astype(BF16)
    lo = (v - hi.astype(F32)).astype(BF16)
    return (_dot(hi, seg_ones) + _dot(lo, seg_ones)) * (1.0 / HEAD_DIM)


def _dup_heads(x, lane):
    swapped = pltpu.roll(x, HEAD_DIM, 1)
    lo = lane < HEAD_DIM
    return jnp.concatenate([jnp.where(lo, x, swapped), jnp.where(lo, swapped, x)], axis=1)


def _fold_heads(x2, lane):
    a, b = x2[:, 0:LANES], x2[:, LANES:2 * LANES]
    return jnp.where(lane < HEAD_DIM, a + pltpu.roll(a, HEAD_DIM, 1), b + pltpu.roll(b, HEAD_DIM, 1))


def _row_spec(ts, cols):
    return pl.BlockSpec((ts, cols), lambda i: (i, 0))


def _full_spec(shape, single=True):
    nd = len(shape)
    if single:
        return pl.BlockSpec(shape, lambda i: (0,) * nd, pipeline_mode=pl.Buffered(1))
    return pl.BlockSpec(shape, lambda i: (0,) * nd)


def _sds(shape, dtype):
    return jax.ShapeDtypeStruct(shape, dtype)


def _params(sem):
    return pltpu.CompilerParams(dimension_semantics=sem, vmem_limit_bytes=VMEM_LIMIT)


def _even_pre_fwd(x, mod, nw, wie, qn, kn, seg, ca, sa, ct, st, qln, kvln, wuq, wuk):
    S = x.shape[0]
    ts = min(IN_PROJ_ROW_TILE, S)

    def body(x_ref, mod_ref, nw_ref, wie_ref, qn_ref, kn_ref, seg_ref, ca_ref, sa_ref, ct_ref, st_ref, qln_ref,
             kvln_ref, wuq_ref, wuk_ref, h_ref, proj_ref, qa_ref, ka_ref, va_ref, qcat_ref, kcat_ref, kat_ref, vat_ref, kcatt_ref):
        xv = x_ref[...]
        h = (xv * _rms(xv) * nw_ref[...]) * (1.0 + mod_ref[1:2, :]) + mod_ref[0:1, :]
        hb = h.astype(BF16)
        h_ref[...] = hb
        proj = _dot_nt(hb, wie_ref[...])
        proj_ref[...] = proj
        lane = _lane_iota()
        ca_v, sa_v, ct_v, st_v = ca_ref[...], sa_ref[...], ct_ref[...], st_ref[...]
        seg_v = seg_ref[...]
        for cb in range(4):
            xc = proj[:, LANES * cb:LANES * (cb + 1)]
            r = lax.rsqrt(_seg_mean(xc * xc, seg_v) + EPS)
            y = _rot(xc * r * qn_ref[...], ca_v, sa_v, lane)
            qa_ref[:, LANES * cb:LANES * (cb + 1)] = (y * 0.125).astype(BF16)
        kc = proj[:, 512:640]
        r = lax.rsqrt(_seg_mean(kc * kc, seg_v) + EPS)
        ka_v = _dup_heads(_rot(kc * r * kn_ref[...], ca_v, sa_v, lane), lane)
        ka_ref[...] = ka_v.astype(BF16)
        kat_ref[...] = ka_v.T.astype(BF16)
        va_v = _dup_heads(proj[:, 640:768], lane)
        va_ref[...] = va_v.astype(BF16)
        vat_ref[...] = va_v.T.astype(BF16)
        cq = proj[:, 1280:1536]
        cqn = (cq * _rms(cq) * qln_ref[...]).astype(BF16)
        ckv = proj[:, 1536:1664]
        ckvn = ckv * _rms(ckv) * kvln_ref[...]
        qb = _dot_nt(cqn, wuq_ref[...])
        qlat = _dot(qb[:, 0:512].astype(BF16), wuk_ref[...])
        for hh in range(B_HEADS):
            qcat_ref[hh, :, 0:LANES] = qlat[:, LANES * hh:LANES * (hh + 1)].astype(BF16)
            qr = _rot(qb[:, 512 + LANES * hh:512 + LANES * (hh + 1)], ct_v, st_v, lane)
            qcat_ref[hh, :, LANES:2 * LANES] = qr.astype(BF16)
        kr = _rot(proj[:, 1664:1792], ct_v, st_v, lane)
        kcat_ref[:, 0:LANES] = ckvn.astype(BF16)
        kcat_ref[:, LANES:2 * LANES] = kr.astype(BF16)
        kcatt_ref[0:LANES, :] = ckvn.T.astype(BF16)
        kcatt_ref[LANES:2 * LANES, :] = kr.T.astype(BF16)

    col_spec = lambda rows: pl.BlockSpec((rows, ts), lambda i: (0, i))
    return pl.pallas_call(
        body, name="even_pre_fwd", grid=(S // ts,),
        in_specs=[_row_spec(ts, D_MODEL), _full_spec((3, D_MODEL)), _full_spec((1, D_MODEL)), _full_spec((EVEN_P, D_MODEL)),
                  _full_spec((1, LANES)), _full_spec((1, LANES)), _full_spec((LANES, LANES)),
                  _row_spec(ts, LANES), _row_spec(ts, LANES), _row_spec(ts, LANES), _row_spec(ts, LANES),
                  _full_spec((1, B_Q_LORA)), _full_spec((1, B_KV_LORA)), _full_spec((1536, B_Q_LORA)), _full_spec((512, 1024))],
        out_specs=[_row_spec(ts, D_MODEL), _row_spec(ts, EVEN_P), _row_spec(ts, 512), _row_spec(ts, 2 * LANES), _row_spec(ts, 2 * LANES),
                   pl.BlockSpec((B_HEADS, ts, 2 * LANES), lambda i: (0, i, 0)), _row_spec(ts, 2 * LANES),
                   col_spec(2 * LANES), col_spec(2 * LANES), col_spec(2 * LANES)],
        out_shape=[_sds((S, D_MODEL), BF16), _sds((S, EVEN_P), F32), _sds((S, 512), BF16), _sds((S, 2 * LANES), BF16),
                   _sds((S, 2 * LANES), BF16), _sds((B_HEADS, S, 2 * LANES), BF16), _sds((S, 2 * LANES), BF16),
                   _sds((2 * LANES, S), BF16), _sds((2 * LANES, S), BF16), _sds((2 * LANES, S), BF16)],
        compiler_params=_params(("arbitrary",)),
    )(x, mod, nw, wie, qn, kn, seg, ca, sa, ct, st, qln, kvln, wuq, wuk)


MLA_SCALE = (B_NOPE + B_ROPE) ** -0.5
LOG2E = 1.4426950408889634

def _row_lo():
    return lax.broadcasted_iota(jnp.int32, (LANES, 1), 0) < HEAD_DIM


def _stack_cols(vT, rlo):
    zero = jnp.zeros_like(vT)
    return jnp.concatenate([jnp.where(rlo, vT, zero), jnp.where(rlo, zero, vT)], axis=1)


def _stack_rows(v, lo):
    zero = jnp.zeros_like(v)
    return jnp.concatenate([jnp.where(lo, v, zero), jnp.where(lo, zero, v)], axis=0)


def _pick_halves_T(xT, rlo, t):
    return jnp.where(rlo, xT[:, 0:t], xT[:, t:2 * t]).T


def _side_split(refs, n_in, n_out, n_scratch, side):
    ns = side.n if side is not None else 0
    cuts = np.cumsum([0, n_in, ns, n_out, ns, n_scratch])
    return [refs[a:b] for a, b in zip(cuts[:-1], cuts[1:])] + [refs[cuts[-1]:]]


def _side_hooks(side, side_ins, side_outs, side_sems, step, total):
    if side is None:
        return lambda: None
    start, mid, end = side.phases(side_ins, side_outs, side_sems)
    pl.when(step == 0)(start)
    pl.when(step == total // 2)(mid)
    return lambda: pl.when(step == total - 1)(end)


def _side_specs(side):
    if side is None:
        return [], [], [], [], []
    return list(side.arrs), [_ANY] * side.n, [_ANY] * side.n, list(side.out_shapes), side.sem_shapes()


def _pp_fwd(q, k, vT, *, kdiv, tq, tk, sub, name, side=None):
    S = k.shape[0]; nb = q.shape[1] // LANES; nq = S // tq; nkv = S // tk; nsub = tk // sub

    def body(*refs):
        (q_ref, k_ref, vT_ref), side_ins, (o_ref, lse_ref), side_outs, (qs, m_s, l_s, acc), side_sems = _side_split(refs, 3, 2, 4, side)
        j = pl.program_id(2)
        rlo = _row_lo()
        step = (pl.program_id(0) * nq + pl.program_id(1)) * nkv + j
        side_end = _side_hooks(side, side_ins, side_outs, side_sems, step, nb * nq * nkv)

        @pl.when(j == 0)
        def _():
            qs[...] = _stack_cols(q_ref[...].astype(F32).T, rlo).astype(BF16)
            m_s[...] = jnp.full((1, 2 * tq), NEG, F32)
            l_s[...] = jnp.zeros((1, 2 * tq), F32)
            acc[...] = jnp.zeros((LANES, 2 * tq), F32)

        qsv = qs[...]
        m, l, a = m_s[...], l_s[...], acc[...]
        s_cur = _dot(k_ref[0:sub, :], qsv)
        for t in range(nsub):
            if t + 1 < nsub:
                s_next = _dot(k_ref[sub * (t + 1):sub * (t + 2), :], qsv)
            m_new = jnp.maximum(m, jnp.max(s_cur, axis=0, keepdims=True))
            alpha = jnp.exp(m - m_new)
            p = jnp.exp(s_cur - m_new)
            l = alpha * l + jnp.sum(p, axis=0, keepdims=True)
            a = alpha * a + _dot(vT_ref[:, sub * t:sub * (t + 1)], p.astype(BF16))
            m = m_new
            if t + 1 < nsub:
                s_cur = s_next
        m_s[...], l_s[...], acc[...] = m, l, a

        @pl.when(j == nkv - 1)
        def _():
            l_f = l_s[...]
            o_ref[...] = _pick_halves_T(acc[...] / l_f, rlo, tq).astype(BF16)
            lse_ref[0, 0] = m_s[...] + jnp.log(l_f)

        side_end()

    s_args, s_in, s_out, s_shapes, s_sems = _side_specs(side)
    return pl.pallas_call(
        body, name=name, grid=(nb, nq, nkv),
        in_specs=[pl.BlockSpec((tq, LANES), lambda b, i, j: (i, b)), pl.BlockSpec((tk, LANES), lambda b, i, j: (j, b // kdiv)),
                  pl.BlockSpec((LANES, tk), lambda b, i, j: (b // kdiv, j))] + s_in,
        out_specs=[pl.BlockSpec((tq, LANES), lambda b, i, j: (i, b)),
                   pl.BlockSpec((1, 1, 1, 2 * tq), lambda b, i, j: (b, i, 0, 0))] + s_out,
        out_shape=[_sds((S, nb * LANES), BF16), _sds((nb, nq, 1, 2 * tq), F32)] + s_shapes,
        scratch_shapes=[pltpu.VMEM((LANES, 2 * tq), BF16), pltpu.VMEM((1, 2 * tq), F32), pltpu.VMEM((1, 2 * tq), F32),
                        pltpu.VMEM((LANES, 2 * tq), F32)] + s_sems,
        compiler_params=_params(("arbitrary",) * 3))(q, k, vT, *s_args)


def _pp_bwd(q, k, kT, v, o, do, lse, *, kdiv, tq, tk, sub, name, side=None):
    S = k.shape[0]; nb = q.shape[1] // LANES; nkb = k.shape[1] // LANES; nq = S // tq; nkv = S // tk; nsub = tk // sub

    def body(*refs):
        ((q_ref, k_ref, kT_ref, v_ref, o_ref, do_ref, lse_ref), side_ins, (dq_ref, dk_ref, dv_ref), side_outs,
         (qsT, qs, dosT, dos, delta_s, dq_acc), side_sems) = _side_split(refs, 7, 3, 6, side)
        b, i, j = pl.program_id(0), pl.program_id(1), pl.program_id(2)
        rlo = _row_lo()
        lo = lax.broadcasted_iota(jnp.int32, (1, LANES), 1) < HEAD_DIM
        side_end = _side_hooks(side, side_ins, side_outs, side_sems, (b * nq + i) * nkv + j, nb * nq * nkv)

        @pl.when((b % kdiv == 0) & (i == 0) & (j == 0))
        def _():
            dk_ref[...] = jnp.zeros((S, LANES), F32)
            dv_ref[...] = jnp.zeros((S, LANES), F32)

        @pl.when(j == 0)
        def _():
            qv = q_ref[...]
            qs[...] = _stack_rows(qv, lo)
            qsT[...] = _stack_cols(qv.astype(F32).T, rlo).astype(BF16)
            dov = do_ref[...].astype(F32)
            dos[...] = _stack_rows(dov.astype(BF16), lo)
            dosT[...] = _stack_cols(dov.T, rlo).astype(BF16)
            prodT = (dov * o_ref[...].astype(F32)).T
            delta_s[...] = jnp.concatenate([jnp.sum(jnp.where(rlo, prodT, 0.0), axis=0, keepdims=True),
                                            jnp.sum(jnp.where(rlo, 0.0, prodT), axis=0, keepdims=True)], axis=1)
            dq_acc[...] = jnp.zeros((LANES, 2 * tq), F32)

        qsTv, dosTv, qsv, dosv = qsT[...], dosT[...], qs[...], dos[...]
        lse_v, delta_v = lse_ref[0, 0], delta_s[...]
        dqa = dq_acc[...]
        s_cur = _dot(k_ref[0:sub, :], qsTv)
        dp_cur = _dot(v_ref[0:sub, :], dosTv)
        for t in range(nsub):
            if t + 1 < nsub:
                s_next = _dot(k_ref[sub * (t + 1):sub * (t + 2), :], qsTv)
                dp_next = _dot(v_ref[sub * (t + 1):sub * (t + 2), :], dosTv)
            p = jnp.exp(s_cur - lse_v)
            ds = (p * (dp_cur - delta_v)).astype(BF16)
            rows = pl.ds(pl.multiple_of(j * tk + sub * t, sub), sub)
            dv_ref[rows, :] += _dot(p.astype(BF16), dosv)
            dk_ref[rows, :] += _dot(ds, qsv)
            dqa = dqa + _dot(kT_ref[:, sub * t:sub * (t + 1)], ds)
            if t + 1 < nsub:
                s_cur, dp_cur = s_next, dp_next
        dq_acc[...] = dqa

        @pl.when(j == nkv - 1)
        def _():
            dq_ref[...] = _pick_halves_T(dq_acc[...], rlo, tq)

        side_end()

    qmap = lambda b, i, j: (i, b)
    kmap = lambda b, i, j: (j, b // kdiv)
    res = lambda b, i, j: (0, b // kdiv)
    s_args, s_in, s_out, s_shapes, s_sems = _side_specs(side)
    return pl.pallas_call(
        body, name=name, grid=(nb, nq, nkv),
        in_specs=[pl.BlockSpec((tq, LANES), qmap), pl.BlockSpec((tk, LANES), kmap), pl.BlockSpec((LANES, tk), lambda b, i, j: (b // kdiv, j)),
                  pl.BlockSpec((tk, LANES), kmap), pl.BlockSpec((tq, LANES), qmap), pl.BlockSpec((tq, LANES), qmap),
                  pl.BlockSpec((1, 1, 1, 2 * tq), lambda b, i, j: (b, i, 0, 0))] + s_in,
        out_specs=[pl.BlockSpec((tq, LANES), qmap), pl.BlockSpec((S, LANES), res), pl.BlockSpec((S, LANES), res)] + s_out,
        out_shape=[_sds((S, nb * LANES), F32), _sds((S, nkb * LANES), F32), _sds((S, nkb * LANES), F32)] + s_shapes,
        scratch_shapes=[pltpu.VMEM((LANES, 2 * tq), BF16), pltpu.VMEM((2 * tq, LANES), BF16), pltpu.VMEM((LANES, 2 * tq), BF16),
                        pltpu.VMEM((2 * tq, LANES), BF16), pltpu.VMEM((1, 2 * tq), F32), pltpu.VMEM((LANES, 2 * tq), F32)] + s_sems,
        compiler_params=_params(("arbitrary",) * 3))(q, k, kT, v, o, do, lse, *s_args)


MLA_C = MLA_SCALE * LOG2E


def _mla_fwd(q, kcat, kcatT, *, tq, tk, sub):
    S = kcat.shape[0]; nq, nkv = S // tq, S // tk; R = B_HEADS * tq; nsub = tk // sub

    def body(q_ref, k_ref, vT_ref, o_ref, lse_ref, qT, m_s, l_s, acc):
        j = pl.program_id(1)

        @pl.when(j == 0)
        def _():
            qT[...] = q_ref[...].reshape(R, 2 * LANES).astype(F32).T.astype(BF16)
            m_s[...] = jnp.full((1, R), NEG, F32)
            l_s[...] = jnp.zeros((1, R), F32)
            acc[...] = jnp.zeros((LANES, R), F32)

        qTv = qT[...]
        m, l, a = m_s[...], l_s[...], acc[...]
        s_cur = _dot(k_ref[0:sub, :], qTv)
        for t in range(nsub):
            if t + 1 < nsub:
                s_next = _dot(k_ref[sub * (t + 1):sub * (t + 2), :], qTv)
            m_new = jnp.maximum(m, jnp.max(s_cur, axis=0, keepdims=True))
            alpha = jnp.exp2((m - m_new) * MLA_C)
            p = jnp.exp2((s_cur - m_new) * MLA_C)
            l = alpha * l + jnp.sum(p, axis=0, keepdims=True)
            a = alpha * a + _dot(vT_ref[:, sub * t:sub * (t + 1)], p.astype(BF16))
            m = m_new
            if t + 1 < nsub:
                s_cur = s_next
        m_s[...], l_s[...], acc[...] = m, l, a

        @pl.when(j == nkv - 1)
        def _():
            l_f = l_s[...]
            o_ref[...] = (acc[...] / l_f).T.reshape(B_HEADS, tq, LANES).astype(BF16)
            lse_ref[0] = m_s[...] * MLA_SCALE + jnp.log(l_f)

    return pl.pallas_call(
        body, name="mla_fwd", grid=(nq, nkv),
        in_specs=[pl.BlockSpec((B_HEADS, tq, 2 * LANES), lambda i, j: (0, i, 0)), pl.BlockSpec((tk, 2 * LANES), lambda i, j: (j, 0)),
                  pl.BlockSpec((LANES, tk), lambda i, j: (0, j))],
        out_specs=[pl.BlockSpec((B_HEADS, tq, LANES), lambda i, j: (0, i, 0)), pl.BlockSpec((1, 1, R), lambda i, j: (i, 0, 0))],
        out_shape=[_sds((B_HEADS, S, LANES), BF16), _sds((nq, 1, R), F32)],
        scratch_shapes=[pltpu.VMEM((2 * LANES, R), BF16), pltpu.VMEM((1, R), F32), pltpu.VMEM((1, R), F32), pltpu.VMEM((LANES, R), F32)],
        compiler_params=_params(("arbitrary", "arbitrary")))(q, kcat, kcatT)


def _mla_bwd(q, kcat, kcatT, o, do, lse, *, tq, tk, sub):
    S = kcat.shape[0]; nq, nkv = S // tq, S // tk; R = B_HEADS * tq; nsub = tk // sub

    def body(q_ref, k_ref, kT_ref, o_ref, do_ref, lse_ref, dq_ref, dk_ref, qT, dosT, dos, delta_s, dq_acc):
        i, j = pl.program_id(0), pl.program_id(1)

        @pl.when((i == 0) & (j == 0))
        def _():
            dk_ref[...] = jnp.zeros((S, 2 * LANES), F32)

        @pl.when(j == 0)
        def _():
            qT[...] = q_ref[...].reshape(R, 2 * LANES).astype(F32).T.astype(BF16)
            dov = do_ref[...].reshape(R, LANES).astype(F32)
            dos[...] = dov.astype(BF16)
            dosT[...] = dov.T.astype(BF16)
            delta_s[...] = jnp.sum((dov * o_ref[...].reshape(R, LANES).astype(F32)).T, axis=0, keepdims=True)
            dq_acc[...] = jnp.zeros((2 * LANES, R), F32)

        qTv, dosTv, dosv = qT[...], dosT[...], dos[...]
        qv = q_ref[...].reshape(R, 2 * LANES)
        lse_v, delta_v = lse_ref[0] * LOG2E, delta_s[...]
        dqa = dq_acc[...]
        s_cur = _dot(k_ref[0:sub, :], qTv)
        dp_cur = _dot(k_ref[0:sub, 0:LANES], dosTv)
        for t in range(nsub):
            if t + 1 < nsub:
                s_next = _dot(k_ref[sub * (t + 1):sub * (t + 2), :], qTv)
                dp_next = _dot(k_ref[sub * (t + 1):sub * (t + 2), 0:LANES], dosTv)
            p = jnp.exp2(s_cur * MLA_C - lse_v)
            ds = (p * (dp_cur - delta_v) * MLA_SCALE).astype(BF16)
            rows = pl.ds(pl.multiple_of(j * tk + sub * t, sub), sub)
            dk_ref[rows, :] += _dot(ds, qv)
            dk_ref[rows, 0:LANES] += _dot(p.astype(BF16), dosv)
            dqa = dqa + _dot(kT_ref[:, sub * t:sub * (t + 1)], ds)
            if t + 1 < nsub:
                s_cur, dp_cur = s_next, dp_next
        dq_acc[...] = dqa

        @pl.when(j == nkv - 1)
        def _():
            dq_ref[...] = dq_acc[...].T.reshape(B_HEADS, tq, 2 * LANES)

    hspec = lambda w: pl.BlockSpec((B_HEADS, tq, w), lambda i, j: (0, i, 0))
    return pl.pallas_call(
        body, name="mla_bwd", grid=(nq, nkv),
        in_specs=[hspec(2 * LANES), pl.BlockSpec((tk, 2 * LANES), lambda i, j: (j, 0)), pl.BlockSpec((2 * LANES, tk), lambda i, j: (0, j)),
                  hspec(LANES), hspec(LANES), pl.BlockSpec((1, 1, R), lambda i, j: (i, 0, 0))],
        out_specs=[hspec(2 * LANES), pl.BlockSpec((S, 2 * LANES), lambda i, j: (0, 0))],
        out_shape=[_sds((B_HEADS, S, 2 * LANES), F32), _sds((S, 2 * LANES), F32)],
        scratch_shapes=[pltpu.VMEM((2 * LANES, R), BF16), pltpu.VMEM((LANES, R), BF16), pltpu.VMEM((R, LANES), BF16),
                        pltpu.VMEM((1, R), F32), pltpu.VMEM((2 * LANES, R), F32)],
        compiler_params=_params(("arbitrary", "arbitrary")))(q, kcat, kcatT, o, do, lse)


def _win_start(i, tq, nk, S):
    return pl.multiple_of(jnp.clip(i * tq - WINDOW, 0, S - nk), LANES)


def _win_dist_table(S, tq):
    nk = min(tq + 2 * WINDOW, S)
    nq = S // tq
    r = np.arange(nk)[:, None]
    c = (np.arange(2 * tq) % tq)[None, :]
    tabs = []
    for rel in (0, WINDOW, (nq - 1) * tq - (S - nk)):
        dist = np.abs(rel + c - r).astype(np.float32)
        tabs.append(np.where(dist <= WINDOW, dist, np.float32(1e32)))
    return jnp.asarray(np.stack(tabs))


def _win_dist_spec(nk, tq, nq):
    return pl.BlockSpec((1, nk, 2 * tq), lambda b, i: (jnp.where(i == 0, 0, jnp.where(i == nq - 1, 2, 1)), 0, 0))


def _win_fwd(q, k, vT, dist, slope, sink, *, kdiv, tq, nbs, name):
    S = k.shape[0]; nb = q.shape[1] // LANES; nq = S // tq; nk = min(tq + 2 * WINDOW, S)
    assert nb % nbs == 0 and nbs % kdiv == 0
    kvw = (nbs // kdiv) * LANES

    def body(q_ref, k_ref, vT_ref, dist_ref, slope_ref, sink_ref, o_ref, lse_ref):
        i = pl.program_id(1)
        rlo = _row_lo()
        k0 = _win_start(i, tq, nk, S)
        kk, vv, dd = k_ref[pl.ds(k0, nk), :], vT_ref[:, pl.ds(k0, nk)], dist_ref[0]
        for u in range(nbs):
            kv = slice(LANES * (u // kdiv), LANES * (u // kdiv + 1))
            qsT = _stack_cols(q_ref[:, LANES * u:LANES * (u + 1)].astype(F32).T, rlo).astype(BF16)
            s = _dot(kk[:, kv], qsT) - slope_ref[u] * dd
            sk = sink_ref[u]
            m = jnp.maximum(jnp.max(s, axis=0, keepdims=True), sk)
            p = jnp.exp(s - m)
            l = jnp.sum(p, axis=0, keepdims=True) + jnp.exp(sk - m)
            o_ref[:, LANES * u:LANES * (u + 1)] = _pick_halves_T(_dot(vv[kv, :], p.astype(BF16)) / l, rlo, tq).astype(BF16)
            lse_ref[u, 0] = m + jnp.log(l)

    row_spec = pl.BlockSpec((nbs, 1, 2 * tq), lambda b, i: (b, 0, 0))
    return pl.pallas_call(
        body, name=name, grid=(nb // nbs, nq),
        in_specs=[pl.BlockSpec((tq, nbs * LANES), lambda b, i: (i, b)), pl.BlockSpec((S, kvw), lambda b, i: (0, b)),
                  pl.BlockSpec((kvw, S), lambda b, i: (b, 0)), _win_dist_spec(nk, tq, nq), row_spec, row_spec],
        out_specs=[pl.BlockSpec((tq, nbs * LANES), lambda b, i: (i, b)), pl.BlockSpec((nbs, 1, 1, 2 * tq), lambda b, i: (b, i, 0, 0))],
        out_shape=[_sds((S, nb * LANES), BF16), _sds((nb, nq, 1, 2 * tq), F32)],
        compiler_params=_params(("arbitrary", "arbitrary")))(q, k, vT, dist, slope, sink)


def _win_bwd(q, k, kT, v, o, do, lse, dist, slope, sink, *, kdiv, tq, nbs, name):
    S = k.shape[0]; nb = q.shape[1] // LANES; nkb = k.shape[1] // LANES; nq = S // tq; nk = min(tq + 2 * WINDOW, S)
    assert nb % nbs == 0 and nbs % kdiv == 0
    nkv = nbs // kdiv
    kvw = nkv * LANES

    def body(q_ref, k_ref, kT_ref, v_ref, o_ref, do_ref, lse_ref, dist_ref, slope_ref, sink_ref, dq_ref, dk_ref, dv_ref, dsink_ref, ds_acc):
        i = pl.program_id(1)
        rlo = _row_lo()
        lo = lax.broadcasted_iota(jnp.int32, (1, LANES), 1) < HEAD_DIM

        @pl.when(i == 0)
        def _():
            dk_ref[...] = jnp.zeros((S, kvw), F32)
            dv_ref[...] = jnp.zeros((S, kvw), F32)
            ds_acc[...] = jnp.zeros((nbs, 2 * tq), F32)

        k0 = _win_start(i, tq, nk, S)
        rows = pl.ds(k0, nk)
        kk_all, vv_all, kkT_all, dd = k_ref[rows, :], v_ref[rows, :], kT_ref[:, rows], dist_ref[0]
        dv_sum, dk_sum = [None] * nkv, [None] * nkv
        for u in range(nbs):
            g = u // kdiv
            kv = slice(LANES * g, LANES * (g + 1))
            kk, vv, kkT = kk_all[:, kv], vv_all[:, kv], kkT_all[kv, :]
            cols = slice(LANES * u, LANES * (u + 1))
            qv = q_ref[:, cols]
            qs = _stack_rows(qv, lo)
            qsT = _stack_cols(qv.astype(F32).T, rlo).astype(BF16)
            dov = do_ref[:, cols].astype(F32)
            dos = _stack_rows(dov.astype(BF16), lo)
            dosT = _stack_cols(dov.T, rlo).astype(BF16)
            prodT = (dov * o_ref[:, cols].astype(F32)).T
            delta = jnp.concatenate([jnp.sum(jnp.where(rlo, prodT, 0.0), axis=0, keepdims=True),
                                     jnp.sum(jnp.where(rlo, 0.0, prodT), axis=0, keepdims=True)], axis=1)
            lse_v = lse_ref[u, 0]
            ds_acc[u:u + 1, :] += -jnp.exp(sink_ref[u] - lse_v) * delta
            p = jnp.exp(_dot(kk, qsT) - slope_ref[u] * dd - lse_v)
            ds = (p * (_dot(vv, dosT) - delta)).astype(BF16)
            dv_u, dk_u = _dot(p.astype(BF16), dos), _dot(ds, qs)
            dv_sum[g] = dv_u if dv_sum[g] is None else dv_sum[g] + dv_u
            dk_sum[g] = dk_u if dk_sum[g] is None else dk_sum[g] + dk_u
            dq_ref[:, cols] = (_pick_halves_T(_dot(kkT, ds), rlo, tq) * 0.125).astype(BF16)
        dv_ref[rows, :] += jnp.concatenate(dv_sum, axis=1)
        dk_ref[rows, :] += jnp.concatenate(dk_sum, axis=1)

        @pl.when(i == nq - 1)
        def _():
            acc = ds_acc[...]
            for u in range(nbs):
                dsink_ref[u] = jnp.concatenate(
                    [jnp.broadcast_to(jnp.sum(acc[u:u + 1, 0:tq], axis=1, keepdims=True), (1, LANES)),
                     jnp.broadcast_to(jnp.sum(acc[u:u + 1, tq:2 * tq], axis=1, keepdims=True), (1, LANES)),
                     jnp.zeros((6, LANES), F32)], axis=0)

    qmap = lambda b, i: (i, b)
    kv_spec = pl.BlockSpec((S, kvw), lambda b, i: (0, b))
    row_spec = pl.BlockSpec((nbs, 1, 2 * tq), lambda b, i: (b, 0, 0))
    wide = pl.BlockSpec((tq, nbs * LANES), qmap)
    return pl.pallas_call(
        body, name=name, grid=(nb // nbs, nq),
        in_specs=[wide, kv_spec, pl.BlockSpec((kvw, S), lambda b, i: (b, 0)), kv_spec, wide, wide,
                  pl.BlockSpec((nbs, 1, 1, 2 * tq), lambda b, i: (b, i, 0, 0)), _win_dist_spec(nk, tq, nq), row_spec, row_spec],
        out_specs=[wide, kv_spec, kv_spec, pl.BlockSpec((nbs, 8, LANES), lambda b, i: (b, 0, 0))],
        out_shape=[_sds((S, nb * LANES), BF16), _sds((S, nkb * LANES), F32), _sds((S, nkb * LANES), F32), _sds((nb, 8, LANES), F32)],
        scratch_shapes=[pltpu.VMEM((nbs, 2 * tq), F32)],
        compiler_params=_params(("arbitrary", "arbitrary")))(q, k, kT, v, o, do, lse, dist, slope, sink)


def _sum_rows(v):
    return jnp.sum(v, axis=0, keepdims=True)


def _norm_mod_bwd(dh, xv, mod_ref, nw_ref, stats_ref):
    r = _rms(xv)
    xn = xv * r
    nw = nw_ref[...]
    stats_ref[0:1, :] += _sum_rows(dh)
    stats_ref[1:2, :] += _sum_rows(dh * (xn * nw))
    dn = dh * (1.0 + mod_ref[1:2, :])
    stats_ref[2:3, :] += _sum_rows(dn * xn)
    return _rms_bwd(xv, r, dn * nw)


def _even_gate_specs(ts):
    return [pl.BlockSpec((ts, 256), lambda i, c=c: (i, c)) for c in (3, 4, 7, 8)]


def _even_post_fwd(oa, olat, proj, x, gate, wuv, woe):
    S = x.shape[0]
    ts = min(ROW_TILE, S)

    def body(oa_ref, ol_ref, ga0_ref, ga1_ref, gb0_ref, gb1_ref, x_ref, gate_ref, wuv_ref, woe_ref, y_ref, x1_ref):
        sa, _ = _silu_and_grad(jnp.concatenate([ga0_ref[...], ga1_ref[...]], axis=1))
        sb, _ = _silu_and_grad(jnp.concatenate([gb0_ref[...], gb1_ref[...]], axis=1))
        olc = jnp.concatenate([ol_ref[hh] for hh in range(B_HEADS)], axis=1).astype(BF16)
        ob = _dot(olc, wuv_ref[...])
        mix = jnp.concatenate([oa_ref[...] * sa, ob * sb], axis=1).astype(BF16)
        y = _dot(mix, woe_ref[...])
        y_ref[...] = y.astype(BF16)
        x1_ref[...] = x_ref[...] + gate_ref[...] * y

    return pl.pallas_call(
        body, name="even_post_fwd", grid=(S // ts,),
        in_specs=[_row_spec(ts, 512), pl.BlockSpec((B_HEADS, ts, LANES), lambda i: (0, i, 0))] + _even_gate_specs(ts) +
                 [_row_spec(ts, D_MODEL), _full_spec((1, D_MODEL)), _full_spec((1024, 512)), _full_spec((1024, D_MODEL))],
        out_specs=[_row_spec(ts, D_MODEL), _row_spec(ts, D_MODEL)],
        out_shape=[_sds((S, D_MODEL), BF16), _sds((S, D_MODEL), F32)],
        compiler_params=_params(("arbitrary",)),
    )(oa, olat, proj, proj, proj, proj, x, gate, wuv, woe)


def _odd_pre_fwd(x, mod, nw, wio):
    S = x.shape[0]
    ts = min(ROW_TILE, S)

    def body(x_ref, mod_ref, nw_ref, wio_ref, h_ref, g_ref, q_ref, k_ref, v_ref, kt_ref, vt_ref):
        xv = x_ref[...]
        h = (xv * _rms(xv) * nw_ref[...]) * (1.0 + mod_ref[1:2, :]) + mod_ref[0:1, :]
        hb = h.astype(BF16)
        h_ref[...] = hb
        proj = jnp.concatenate([_dot(hb, wio_ref[p]) for p in range(N_CHIPS)], axis=1)
        g_ref[...] = proj[:, 1536:2560]
        q_ref[...] = (proj[:, 0:1024] * 0.125).astype(BF16)
        lane = _lane_iota()
        k_v = jnp.concatenate([_dup_heads(proj[:, 1024 + LANES * j:1024 + LANES * (j + 1)], lane) for j in range(2)], axis=1)
        v_v = jnp.concatenate([_dup_heads(proj[:, 1280 + LANES * j:1280 + LANES * (j + 1)], lane) for j in range(2)], axis=1)
        k_ref[...] = k_v.astype(BF16)
        v_ref[...] = v_v.astype(BF16)
        kt_ref[...] = k_v.T.astype(BF16)
        vt_ref[...] = v_v.T.astype(BF16)

    col_spec = pl.BlockSpec((512, ts), lambda i: (0, i))
    return pl.pallas_call(
        body, name="odd_pre_fwd", grid=(S // ts,),
        in_specs=[_row_spec(ts, D_MODEL), _full_spec((3, D_MODEL)), _full_spec((1, D_MODEL)),
                  _full_spec((N_CHIPS, D_MODEL, ODD_IN // N_CHIPS))],
        out_specs=[_row_spec(ts, D_MODEL), _row_spec(ts, 1024), _row_spec(ts, 1024), _row_spec(ts, 512), _row_spec(ts, 512),
                   col_spec, col_spec],
        out_shape=[_sds((S, D_MODEL), BF16), _sds((S, 1024), F32), _sds((S, 1024), BF16), _sds((S, 512), BF16),
                   _sds((S, 512), BF16), _sds((512, S), BF16), _sds((512, S), BF16)],
        compiler_params=_params(("arbitrary",)),
    )(x, mod, nw, wio)


def _odd_post(oc, g, x1, gate, woo, fw, tgt):
    S = x1.shape[0]
    ts = min(ROW_TILE, S)
    nsteps = S // ts

    def body(oc_ref, g_ref, x_ref, gate_ref, woo_ref, fw_ref, tgt_ref, doc_ref, dgc_ref, dx2_ref, dwoo_out, stats_ref, dwoo_ref):
        @pl.when(pl.program_id(0) == 0)
        def _():
            dwoo_ref[...] = jnp.zeros((D_MODEL, D_MODEL), F32)
            stats_ref[...] = jnp.zeros((8, D_MODEL), F32)

        ocv = oc_ref[...]
        sg, dsg = _silu_and_grad(g_ref[...])
        mix = (ocv * sg).astype(BF16)
        woo_v = woo_ref[...]
        y = _dot(mix, woo_v)
        gate_v = gate_ref[...]
        x2 = x_ref[...] + gate_v * y
        r = _rms(x2)
        fw_v = fw_ref[...]
        xn = x2 * r
        err = xn * fw_v - tgt_ref[...]
        dout = err * (1.0 / D_MODEL)
        dx2 = _rms_bwd(x2, r, dout * fw_v)
        dx2_ref[...] = dx2
        stats_ref[0:1, :] += _sum_rows(dout * xn)
        stats_ref[1:2, :] += _sum_rows(dx2 * y)
        loss_t = 0.5 * jnp.sum(_sum_rows(err * dout), axis=-1, keepdims=True)
        stats_ref[2:3, :] += jnp.broadcast_to(loss_t, (1, D_MODEL))
        dy = (gate_v * dx2).astype(BF16)
        dmix = _dot_nt(dy, woo_v)
        dwoo_ref[...] += _dot_tn(mix, dy)
        doc_ref[...] = (dmix * sg).astype(BF16)
        dgc_ref[...] = (dmix * ocv * dsg).astype(BF16)

        @pl.when(pl.program_id(0) == nsteps - 1)
        def _():
            dwoo_out[...] = dwoo_ref[...].astype(BF16)

    return pl.pallas_call(
        body, name="odd_post", grid=(nsteps,),
        in_specs=[_row_spec(ts, D_MODEL), _row_spec(ts, D_MODEL), _row_spec(ts, D_MODEL), _full_spec((1, D_MODEL)),
                  _full_spec((D_MODEL, D_MODEL)), _full_spec((1, D_MODEL)), _row_spec(ts, D_MODEL)],
        out_specs=[_row_spec(ts, D_MODEL), _row_spec(ts, D_MODEL), _row_spec(ts, D_MODEL),
                   _full_spec((D_MODEL, D_MODEL), single=False), _full_spec((8, D_MODEL), single=False)],
        out_shape=[_sds((S, D_MODEL), BF16), _sds((S, D_MODEL), BF16), _sds((S, D_MODEL), F32), _sds((D_MODEL, D_MODEL), BF16),
                   _sds((8, D_MODEL), F32)],
        scratch_shapes=[pltpu.VMEM((D_MODEL, D_MODEL), F32)],
        compiler_params=_params(("arbitrary",)),
    )(oc, g, x1, gate, woo, fw, tgt)


def _odd_pre_bwd(dq, dk, dv, dgc, h, x, dx_res, mod, nw, wio):
    S = x.shape[0]
    ts = min(IN_PROJ_ROW_TILE, S)
    nsteps = S // ts
    wsh = ODD_IN // N_CHIPS

    def body(dq_ref, dk_ref, dv_ref, dgc_ref, h_ref, x_ref, dxr_ref, mod_ref, nw_ref, wio_ref, dx_ref, dw_ref, stats_ref, dw_acc):
        @pl.when(pl.program_id(0) == 0)
        def _():
            dw_acc[...] = jnp.zeros((N_CHIPS, D_MODEL, wsh), F32)
            stats_ref[...] = jnp.zeros((8, D_MODEL), F32)

        lane = _lane_iota()
        dkv = [_fold_heads(r[:, 2 * LANES * j:2 * LANES * (j + 1)], lane).astype(BF16) for r in (dk_ref, dv_ref) for j in range(2)]
        dproj = jnp.concatenate([dq_ref[...]] + dkv + [dgc_ref[...]], axis=1)
        hv = h_ref[...]
        dh = None
        for p in range(N_CHIPS):
            dp_cols = dproj[:, wsh * p:wsh * (p + 1)]
            part = _dot_nt(dp_cols, wio_ref[p])
            dh = part if dh is None else dh + part
            dw_acc[p] += _dot_tn(hv, dp_cols)
        dx_ref[...] = dxr_ref[...] + _norm_mod_bwd(dh, x_ref[...], mod_ref, nw_ref, stats_ref)

        @pl.when(pl.program_id(0) == nsteps - 1)
        def _():
            dw_ref[...] = dw_acc[...].astype(BF16)

    return pl.pallas_call(
        body, name="odd_pre_bwd", grid=(nsteps,),
        in_specs=[_row_spec(ts, 1024), _row_spec(ts, 512), _row_spec(ts, 512), _row_spec(ts, 1024), _row_spec(ts, D_MODEL),
                  _row_spec(ts, D_MODEL), _row_spec(ts, D_MODEL), _full_spec((3, D_MODEL)), _full_spec((1, D_MODEL)),
                  _full_spec((N_CHIPS, D_MODEL, wsh))],
        out_specs=[_row_spec(ts, D_MODEL), _full_spec((N_CHIPS, D_MODEL, wsh), single=False), _full_spec((8, D_MODEL), single=False)],
        out_shape=[_sds((S, D_MODEL), F32), _sds((N_CHIPS, D_MODEL, wsh), BF16), _sds((8, D_MODEL), F32)],
        scratch_shapes=[pltpu.VMEM((N_CHIPS, D_MODEL, wsh), F32)],
        compiler_params=_params(("arbitrary",)),
    )(dq, dk, dv, dgc, h, x, dx_res, mod, nw, wio)


def _even_post_bwd(dx1, y, oa, olat, proj, gate, wuv, woe):
    S = dx1.shape[0]
    ts = min(ROW_TILE, S)
    nsteps = S // ts

    def body(dx_ref, y_ref, oa_ref, ol_ref, ga0_ref, ga1_ref, gb0_ref, gb1_ref, gate_ref, wuv_ref, woe_ref,
             doa_ref, dga_ref, dgb_ref, dol_ref, dwoe_out, dwuv_ref, stats_ref, dwoe_ref):
        @pl.when(pl.program_id(0) == 0)
        def _():
            dwoe_ref[...] = jnp.zeros((D_MODEL, D_MODEL), F32)
            dwuv_ref[...] = jnp.zeros((1024, 512), F32)
            stats_ref[...] = jnp.zeros((8, D_MODEL), F32)

        dxv = dx_ref[...]
        stats_ref[0:1, :] += _sum_rows(dxv * y_ref[...])
        dy = (gate_ref[...] * dxv).astype(BF16)
        sa, dsa = _silu_and_grad(jnp.concatenate([ga0_ref[...], ga1_ref[...]], axis=1))
        sb, dsb = _silu_and_grad(jnp.concatenate([gb0_ref[...], gb1_ref[...]], axis=1))
        olc = jnp.concatenate([ol_ref[hh] for hh in range(B_HEADS)], axis=1).astype(BF16)
        wuv_v = wuv_ref[...]
        ob = _dot(olc, wuv_v)
        oav = oa_ref[...]
        mix = jnp.concatenate([oav * sa, ob * sb], axis=1).astype(BF16)
        dmix = _dot_nt(dy, woe_ref[...])
        dwoe_ref[...] += _dot_tn(mix, dy)
        dma, dmb = dmix[:, 0:512], dmix[:, 512:1024]
        doa_ref[...] = (dma * sa).astype(BF16)
        dga_ref[...] = (dma * oav * dsa).astype(BF16)
        dgb_ref[...] = (dmb * ob * dsb).astype(BF16)
        dob = (dmb * sb).astype(BF16)
        dol = _dot_nt(dob, wuv_v)
        dwuv_ref[...] += _dot_tn(olc, dob)
        for hh in range(B_HEADS):
            dol_ref[hh] = dol[:, LANES * hh:LANES * (hh + 1)].astype(BF16)

        @pl.when(pl.program_id(0) == nsteps - 1)
        def _():
            dwoe_out[...] = dwoe_ref[...].astype(BF16)

    head_spec = pl.BlockSpec((B_HEADS, ts, LANES), lambda i: (0, i, 0))
    return pl.pallas_call(
        body, name="even_post_bwd", grid=(nsteps,),
        in_specs=[_row_spec(ts, D_MODEL), _row_spec(ts, D_MODEL), _row_spec(ts, 512), head_spec] + _even_gate_specs(ts) +
                 [_full_spec((1, D_MODEL)), _full_spec((1024, 512)), _full_spec((1024, D_MODEL))],
        out_specs=[_row_spec(ts, 512), _row_spec(ts, 512), _row_spec(ts, 512), head_spec,
                   _full_spec((D_MODEL, D_MODEL), single=False), _full_spec((1024, 512), single=False),
                   _full_spec((8, D_MODEL), single=False)],
        out_shape=[_sds((S, 512), BF16), _sds((S, 512), BF16), _sds((S, 512), BF16), _sds((B_HEADS, S, LANES), BF16),
                   _sds((D_MODEL, D_MODEL), BF16), _sds((1024, 512), F32), _sds((8, D_MODEL), F32)],
        scratch_shapes=[pltpu.VMEM((D_MODEL, D_MODEL), F32)],
        compiler_params=_params(("arbitrary",)),
    )(dx1, y, oa, olat, proj, proj, proj, proj, gate, wuv, woe)


def _even_pre_bwd(x, h, proj, dqa, dka, dva, dga, dgb, dqcat, dkcat, dx_res, mod, nw, wie, qn, kn, seg, ca, sa, ct, st,
                  qln, kvln, wuq, wuk):
    S = x.shape[0]
    ts = min(IN_PROJ_ROW_TILE, S)
    nsteps = S // ts

    def body(x_ref, h_ref, proj_ref, dqa_ref, dka_ref, dva_ref, dga_ref, dgb_ref, dqc_ref, dkc_ref, dxr_ref, mod_ref, nw_ref,
             wie_ref, qn_ref, kn_ref, seg_ref, ca_ref, sa_ref, ct_ref, st_ref, qln_ref, kvln_ref, wuq_ref, wuk_ref,
             dx_ref, dwie_out, dwuq_out, dwuk_out, stats_ref, nstats_ref, dwie_ref, dwuq_ref, dwuk_ref, stage):
        @pl.when(pl.program_id(0) == 0)
        def _():
            dwie_ref[...] = jnp.zeros((EVEN_P, D_MODEL), F32)
            dwuq_ref[...] = jnp.zeros((1536, B_Q_LORA), F32)
            dwuk_ref[...] = jnp.zeros((512, 1024), F32)
            stats_ref[...] = jnp.zeros((8, D_MODEL), F32)
            nstats_ref[...] = jnp.zeros((8, 256), F32)

        lane = _lane_iota()
        ca_v, sa_v, ct_v, st_v = ca_ref[...], sa_ref[...], ct_ref[...], st_ref[...]
        seg_v = seg_ref[...]

        def head_norm_bwd(xc, dy, w):
            r = lax.rsqrt(_seg_mean(xc * xc, seg_v) + EPS)
            g = dy * w
            dxc = r * g - xc * (r * r * r) * _seg_mean(xc * g, seg_v)
            return dxc, _sum_rows(dy * (xc * r))

        pieces = []
        dqn = jnp.zeros((1, LANES), F32)
        for cb in range(4):
            sl = slice(LANES * cb, LANES * (cb + 1))
            dy = _rot_bwd(dqa_ref[:, sl] * 0.125, ca_v, sa_v, lane)
            dxc, dw = head_norm_bwd(proj_ref[:, sl], dy, qn_ref[...])
            pieces.append(dxc)
            dqn = dqn + dw
        dxc, dkn = head_norm_bwd(proj_ref[:, 512:640], _rot_bwd(_fold_heads(dka_ref[...], lane), ca_v, sa_v, lane), kn_ref[...])
        pieces += [dxc, _fold_heads(dva_ref[...], lane), dga_ref[...]]
        nstats_ref[0:1, 0:LANES] += dqn + pltpu.roll(dqn, HEAD_DIM, 1)
        nstats_ref[1:2, 0:LANES] += dkn + pltpu.roll(dkn, HEAD_DIM, 1)

        cq = proj_ref[:, 1280:1536]
        rq = _rms(cq)
        cqn_f = cq * rq
        qln_v = qln_ref[...]
        cqn = (cqn_f * qln_v).astype(BF16)
        wuq_v, wuk_v = wuq_ref[...], wuk_ref[...]
        qnope = _dot_nt(cqn, wuq_v[0:512, :]).astype(BF16)
        dqlat = jnp.concatenate([dqc_ref[hh, :, 0:LANES] for hh in range(B_HEADS)], axis=1).astype(BF16)
        dqnope = _dot_nt(dqlat, wuk_v)
        dwuk_ref[...] += _dot_tn(qnope, dqlat)
        dqr = [_rot_bwd(dqc_ref[hh, :, LANES:2 * LANES], ct_v, st_v, lane) for hh in range(B_HEADS)]
        dqb = jnp.concatenate([dqnope] + dqr, axis=1).astype(BF16)
        dcqn = _dot(dqb, wuq_v)
        dwuq_ref[...] += _dot_tn(dqb, cqn)
        nstats_ref[2:3, :] += _sum_rows(dcqn * cqn_f)
        dcq = _rms_bwd(cq, rq, dcqn * qln_v)
        ckv = proj_ref[:, 1536:1664]
        rk = _rms(ckv)
        dckvn = dkc_ref[:, 0:LANES]
        nstats_ref[3:4, 0:LANES] += _sum_rows(dckvn * (ckv * rk))
        dckv = _rms_bwd(ckv, rk, dckvn * kvln_ref[...])
        dkr = _rot_bwd(dkc_ref[:, LANES:2 * LANES], ct_v, st_v, lane)
        pieces += [dcq, dckv, dkr, dgb_ref[...]]
        dproj = jnp.concatenate([piece.astype(BF16) for piece in pieces], axis=1)
        dh = _dot(dproj, wie_ref[...])
        dwie_ref[...] += _dot_tn(dproj, h_ref[...])
        dx_ref[...] = dxr_ref[...] + _norm_mod_bwd(dh, x_ref[...], mod_ref, nw_ref, stats_ref)

        @pl.when(pl.program_id(0) == nsteps - 1)
        def _():
            r_out = 0
            for lo, hi in ((0, EVEN_GAP), (EVEN_GAP + EVEN_P - EVEN_IN, EVEN_P)):
                for r0 in range(lo, hi, stage.shape[0]):
                    n = min(stage.shape[0], hi - r0)
                    stage[0:n, :] = dwie_ref[r0:r0 + n, :].astype(BF16)
                    pltpu.sync_copy(stage.at[0:n], dwie_out.at[pl.ds(r_out, n), :])
                    r_out += n
            pltpu.sync_copy(dwuq_ref, dwuq_out)
            pltpu.sync_copy(dwuk_ref, dwuk_out)

    return pl.pallas_call(
        body, name="even_pre_bwd", grid=(nsteps,),
        in_specs=[_row_spec(ts, D_MODEL), _row_spec(ts, D_MODEL), _row_spec(ts, EVEN_P), _row_spec(ts, 512), _row_spec(ts, 2 * LANES),
                  _row_spec(ts, 2 * LANES), _row_spec(ts, 512), _row_spec(ts, 512),
                  pl.BlockSpec((B_HEADS, ts, 2 * LANES), lambda i: (0, i, 0)), _row_spec(ts, 2 * LANES), _row_spec(ts, D_MODEL),
                  _full_spec((3, D_MODEL)), _full_spec((1, D_MODEL)), _full_spec((EVEN_P, D_MODEL)),
                  _full_spec((1, LANES)), _full_spec((1, LANES)), _full_spec((LANES, LANES)),
                  _row_spec(ts, LANES), _row_spec(ts, LANES), _row_spec(ts, LANES), _row_spec(ts, LANES),
                  _full_spec((1, B_Q_LORA)), _full_spec((1, B_KV_LORA)), _full_spec((1536, B_Q_LORA)), _full_spec((512, 1024))],
        out_specs=[_row_spec(ts, D_MODEL), _ANY, _ANY, _ANY, _full_spec((8, D_MODEL), single=False), _full_spec((8, 256), single=False)],
        out_shape=[_sds((S, D_MODEL), F32), _sds((EVEN_IN, D_MODEL), BF16), _sds((1536, B_Q_LORA), F32), _sds((512, 1024), F32),
                   _sds((8, D_MODEL), F32), _sds((8, 256), F32)],
        scratch_shapes=[pltpu.VMEM((EVEN_P, D_MODEL), F32), pltpu.VMEM((1536, B_Q_LORA), F32), pltpu.VMEM((512, 1024), F32),
                        pltpu.VMEM((256, D_MODEL), BF16)],
        compiler_params=_params(("arbitrary",)),
    )(x, h, proj, dqa, dka, dva, dga, dgb, dqcat, dkcat, dx_res, mod, nw, wie, qn, kn, seg, ca, sa, ct, st, qln, kvln, wuq, wuk)


def _ada_fwd(c_all, w, b):
    n = w.shape[2]

    def body(c_ref, w_ref, b_ref, o_ref):
        cv = c_ref[...]
        o_ref[0] = _dot_f32(cv * _sigmoid(cv), w_ref[0]) + b_ref[0]

    return pl.pallas_call(
        body, name="ada_fwd", grid=(2,),
        in_specs=[pl.BlockSpec((N_DEV, D_MODEL), lambda l: (0, 0)), pl.BlockSpec((1, D_MODEL, n), lambda l: (l, 0, 0)),
                  pl.BlockSpec((1, 1, n), lambda l: (l, 0, 0))],
        out_specs=pl.BlockSpec((1, N_DEV, n), lambda l: (l, 0, 0)),
        out_shape=_sds((2, N_DEV, n), F32),
        compiler_params=_params(("arbitrary",)),
    )(c_all, w, b)


def _ada_bwd(c_all_t, dmod):
    n = dmod.shape[2]

    def body(c_ref, d_ref, o_ref):
        cv = c_ref[...]
        act = cv * _sigmoid(cv)
        dv = d_ref[0]
        acc = act[:, 0:1] * dv[0:1, :]
        for bb in range(1, N_DEV):
            acc = acc + act[:, bb:bb + 1] * dv[bb:bb + 1, :]
        o_ref[0] = acc

    return pl.pallas_call(
        body, name="ada_bwd", grid=(2,),
        in_specs=[pl.BlockSpec((D_MODEL, N_DEV), lambda l: (0, 0)), pl.BlockSpec((1, N_DEV, n), lambda l: (l, 0, 0))],
        out_specs=pl.BlockSpec((1, D_MODEL, n), lambda l: (l, 0, 0)),
        out_shape=_sds((2, D_MODEL, n), F32),
        compiler_params=_params(("arbitrary",)),
    )(c_all_t, dmod)


ADAM_ROW_TILE = 512


def _adam_update(g, w, m, v):
    m_new = ADAM_B1 * m + (1.0 - ADAM_B1) * g
    v_new = ADAM_B2 * v + (1.0 - ADAM_B2) * jnp.square(g)
    m_hat = m_new / (1.0 - ADAM_B1 ** ADAM_STEP)
    v_hat = v_new / (1.0 - ADAM_B2 ** ADAM_STEP)
    return -ADAM_LR * (m_hat / (jnp.sqrt(v_hat) + ADAM_EPS) + ADAM_WD * w), m_new, v_new


SMALL_ROWS = dict(dmod=(0, D_MODEL), norm_w=(6, D_MODEL), final_norm=(8, D_MODEL), a_q_norm=(9, HEAD_DIM), a_k_norm=(10, HEAD_DIM),
                  b_q_lora_norm=(11, B_Q_LORA), b_kv_lora_norm=(12, B_KV_LORA), c_sink=(13, C_HEADS))
SMALL_WEIGHTS = ("ada_b", "norm_w", "final_norm", "a_q_norm", "a_k_norm", "b_q_lora_norm", "b_kv_lora_norm", "c_sink")
LOSS_ROW = 14


def _pack_small(res):
    def padded(v):
        return jnp.concatenate([v, jnp.zeros((v.shape[0], D_MODEL - v.shape[1]), F32)], axis=1)

    rows = [res["dmod"].reshape(6, D_MODEL), res["norm_w"], res["final_norm"].reshape(1, D_MODEL)]
    rows += [padded(res[k]) for k in ("a_q_norm", "a_k_norm", "b_q_lora_norm", "b_kv_lora_norm", "c_sink")]
    return jnp.concatenate(rows + [res["loss_row"], jnp.zeros((1, D_MODEL), F32)], axis=0)


def _adam_small(parts, ws, ms, vs):
    nw = len(SMALL_WEIGHTS)

    def body(*refs):
        p_ref = refs[0]
        w_refs, m_refs, v_refs = refs[1:1 + nw], refs[1 + nw:1 + 2 * nw], refs[1 + 2 * nw:1 + 3 * nw]
        outs = refs[1 + 3 * nw:]
        g_all = p_ref[0]
        for k in range(1, N_DEV):
            g_all = g_all + p_ref[k]
        for idx, name in enumerate(SMALL_WEIGHTS):
            if name == "ada_b":
                g = jnp.concatenate([jnp.concatenate([g_all[3 * l + t:3 * l + t + 1] for t in range(3)], axis=1) for l in range(2)],
                                    axis=0)
            else:
                row, width = SMALL_ROWS[name]
                g = g_all[row:row + w_refs[idx].shape[0], 0:width]
            d, m_new, v_new = _adam_update(g, w_refs[idx][...], m_refs[idx][...], v_refs[idx][...])
            outs[4 * idx][...], outs[4 * idx + 1][...], outs[4 * idx + 2][...], outs[4 * idx + 3][...] = g, d, m_new, v_new
        outs[4 * nw][...] = g_all[LOSS_ROW:LOSS_ROW + 1, 0:LANES]

    out_shape = []
    for w in ws:
        out_shape += [_sds(w.shape, F32)] * 4
    out_shape.append(_sds((1, LANES), F32))
    return pl.pallas_call(body, name="adam_small", out_shape=out_shape,
                          compiler_params=pltpu.CompilerParams(vmem_limit_bytes=VMEM_LIMIT))(parts, *ws, *ms, *vs)


def _adam(parts, w, m, v, name, by_columns=False):
    P, R, C = parts.shape
    if by_columns:
        tr, tc = R, 256
    else:
        tr, tc = (R if R <= ADAM_ROW_TILE else ADAM_ROW_TILE), C
    assert R % tr == 0 and C % tc == 0

    def body(p_ref, w_ref, m_ref, v_ref, g_ref, d_ref, nm_ref, nv_ref):
        g = p_ref[0].astype(F32)
        for k in range(1, P):
            g = g + p_ref[k].astype(F32)
        g_ref[...] = g
        d_ref[...], nm_ref[...], nv_ref[...] = _adam_update(g, w_ref[...], m_ref[...], v_ref[...])

    tile = (lambda i: (0, i)) if by_columns else (lambda i: (i, 0))
    spec = pl.BlockSpec((tr, tc), tile)
    return pl.pallas_call(
        body, name=name, grid=(C // tc if by_columns else R // tr,),
        in_specs=[pl.BlockSpec((P, tr, tc), lambda i: (0,) + tile(i)), spec, spec, spec],
        out_specs=[spec, spec, spec, spec], out_shape=[_sds((R, C), F32)] * 4,
        compiler_params=_params(("arbitrary",)),
    )(parts, w, m, v)


_ANY = pl.BlockSpec(memory_space=pl.ANY)
CHIP_FLIPS = ((1, 0), (0, 1), (1, 1))
DEV_FLIPS = tuple((dx, dy, dc) for dx in (0, 1) for dy in (0, 1) for dc in (0, 1) if dx + dy + dc)


def _flip(a, d):
    return a if d == 0 else 1 - a


def _my_place():
    return lax.axis_index("x"), lax.axis_index("y"), lax.axis_index("c")


def _gather8_copies(ins, outs, send_sems, recv_sems, loc_sems):
    x, y, c = _my_place()
    me = 4 * x + 2 * y + c
    copies = []
    for a in range(len(ins)):
        copies.append(pltpu.make_async_copy(ins[a], outs[a].at[me], loc_sems.at[a]))
        for k, (dx, dy, dc) in enumerate(DEV_FLIPS):
            copies.append(pltpu.make_async_remote_copy(
                src_ref=ins[a], dst_ref=outs[a].at[me], send_sem=send_sems.at[a, k], recv_sem=recv_sems.at[a, k],
                device_id=(_flip(x, dx), _flip(y, dy), _flip(c, dc)), device_id_type=MESH_ID))
    return copies


def _gather8_sems(n):
    return [pltpu.SemaphoreType.DMA((n, 7)), pltpu.SemaphoreType.DMA((n, 7)), pltpu.SemaphoreType.DMA((n,))]


def _gather_dev8(arrs, name):
    n = len(arrs)

    def body(*refs):
        copies = _gather8_copies(refs[:n], refs[n:2 * n], *refs[2 * n:])
        for cp in copies:
            cp.start()
        for cp in copies:
            cp.wait()

    return pl.pallas_call(
        body, name=name, in_specs=[_ANY] * n, out_specs=[_ANY] * n,
        out_shape=[_sds((N_DEV,) + a.shape, a.dtype) for a in arrs], scratch_shapes=_gather8_sems(n),
    )(*arrs)


class _Exchange:
    def __init__(self, arrs, out_shapes, n_sems, phases):
        self.arrs, self.out_shapes, self.n_sems, self._phases = list(arrs), list(out_shapes), n_sems, phases

    @property
    def n(self):
        return len(self.arrs)

    def sem_shapes(self):
        return [pltpu.SemaphoreType.DMA((self.n, self.n_sems)), pltpu.SemaphoreType.DMA((self.n, self.n_sems)),
                pltpu.SemaphoreType.DMA((self.n,))]

    def phases(self, ins, outs, sems):
        return self._phases(ins, outs, *sems)

    def run(self, name):
        n = self.n

        def body(*refs):
            start, mid, end = self.phases(refs[:n], refs[n:2 * n], refs[2 * n:])
            start()
            mid()
            end()

        return pl.pallas_call(body, name=name, in_specs=[_ANY] * n, out_specs=[_ANY] * n, out_shape=self.out_shapes,
                              scratch_shapes=self.sem_shapes())(*self.arrs)

def _gather_halves_phases(ins, outs, send_sems, recv_sems, loc_sems):
    n = len(ins)
    x, y, c = _my_place()
    chip = 2 * x + y
    sibling = (x, y, 1 - c)
    peers = [(_flip(x, dx), _flip(y, dy)) for dx, dy in CHIP_FLIPS]

    def remote(src, p, half, a, k, to):
        return pltpu.make_async_remote_copy(src_ref=src, dst_ref=outs[a].at[p, half], send_sem=send_sems.at[a, k],
                                            recv_sem=recv_sems.at[a, k], device_id=to, device_id_type=MESH_ID)

    def local(a):
        return pltpu.make_async_copy(ins[a], outs[a].at[chip], loc_sems.at[a])

    def first(a, k):
        return remote(ins[a].at[c], chip, c, a, k, (*peers[k], c))

    def passed(a, k):
        p = 2 * peers[k][0] + peers[k][1]
        return remote(outs[a].at[p, c], p, c, a, 3 + k, sibling)

    def start():
        for a in range(n):
            local(a).start()
            for k in range(3):
                first(a, k).start()

    def mid():
        for a in range(n):
            for k in range(3):
                p = 2 * peers[k][0] + peers[k][1]
                remote(outs[a].at[p, c], p, c, a, k, sibling).wait_recv()
                passed(a, k).start()

    def end():
        for a in range(n):
            for k in range(3):
                p = 2 * peers[k][0] + peers[k][1]
                remote(outs[a].at[p, 1 - c], p, 1 - c, a, 3 + k, sibling).wait_recv()
        for a in range(n):
            for k in range(3):
                first(a, k).wait_send()
                passed(a, k).wait_send()
            local(a).wait()

    return start, mid, end


def _gather_chip4_halves(arrs):
    return _Exchange(arrs, [_sds((N_CHIPS,) + a.shape, a.dtype) for a in arrs], 6, _gather_halves_phases)


def _reduce_phases(n_whole, ins, outs, send_sems, recv_sems, loc_sems):
    n = len(ins)
    x, y, c = _my_place()
    chip = 2 * x + y
    sibling = (x, y, 1 - c)
    peers = [(_flip(x, dx), _flip(y, dy)) for dx, dy in CHIP_FLIPS]

    def remote(src, slot, a, k, to):
        return pltpu.make_async_remote_copy(src_ref=src, dst_ref=outs[a].at[slot], send_sem=send_sems.at[a, k],
                                            recv_sem=recv_sems.at[a, k], device_id=to, device_id_type=MESH_ID)

    def block(a, p):
        return ins[a] if a >= n - n_whole else ins[a].at[p]

    def local(a):
        return pltpu.make_async_copy(block(a, chip), outs[a].at[2 * chip + c], loc_sems.at[a])

    def own(a):
        return remote(block(a, chip), 2 * chip + c, a, 0, sibling)

    def first(a, k):
        return remote(block(a, 2 * peers[k][0] + peers[k][1]), 2 * chip + c, a, 1 + k, (*peers[k], c))

    def passed(a, k):
        slot = 2 * (2 * peers[k][0] + peers[k][1]) + c
        return remote(outs[a].at[slot], slot, a, 4 + k, sibling)

    def start():
        for a in range(n):
            local(a).start()
            own(a).start()
            for k in range(3):
                first(a, k).start()

    def mid():
        for a in range(n):
            for k in range(3):
                slot = 2 * (2 * peers[k][0] + peers[k][1]) + c
                remote(outs[a].at[slot], slot, a, 1 + k, sibling).wait_recv()
                passed(a, k).start()

    def end():
        for a in range(n):
            remote(outs[a].at[2 * chip + 1 - c], 2 * chip + 1 - c, a, 0, sibling).wait_recv()
            for k in range(3):
                slot = 2 * (2 * peers[k][0] + peers[k][1]) + 1 - c
                remote(outs[a].at[slot], slot, a, 4 + k, sibling).wait_recv()
        for a in range(n):
            own(a).wait_send()
            for k in range(3):
                first(a, k).wait_send()
                passed(a, k).wait_send()
            local(a).wait()

    return start, mid, end


def _reduce_exchange(arrs, whole=()):
    shapes = [_sds((N_DEV,) + a.shape[1:], a.dtype) for a in arrs] + [_sds((N_DEV,) + a.shape, a.dtype) for a in whole]
    return _Exchange(list(arrs) + list(whole), shapes, 7, functools.partial(_reduce_phases, len(whole)))


def _shard_halves_t(w):
    wt = w.T.astype(BF16)
    n2 = wt.shape[0] // 2
    pad = jnp.zeros((-n2 % 16, wt.shape[1]), BF16)
    return jnp.stack([jnp.concatenate([wt[0:n2], pad], axis=0), jnp.concatenate([wt[n2:], pad], axis=0)])


def _gathered_rows(g, n):
    return [g[p, half, 0:n // 2] for p in range(N_CHIPS) for half in range(2)]


def _even_in_layout_t(g):
    n2 = EVEN_IN // N_CHIPS // 2
    gap, gap_rows = EVEN_GAP, EVEN_P - EVEN_IN
    assert n2 % 2 == 0 and gap % 2 == 0 and gap_rows % 2 == 0
    spans = []
    for p in range(N_CHIPS):
        for half in range(2):
            lo = (2 * p + half) * n2
            if lo < gap < lo + n2:
                spans += [(p, half, 0, (gap - lo) // 2, lo // 2), (p, half, (gap - lo) // 2, (lo + n2 - gap) // 2, (gap + gap_rows) // 2)]
            else:
                spans.append((p, half, 0, n2 // 2, (lo + (gap_rows if lo >= gap else 0)) // 2))

    def body(g_ref, o_ref, pairs):
        pairs[pl.ds(gap // 2, gap_rows // 2), :] = jnp.zeros((gap_rows // 2, pairs.shape[1]), jnp.uint32)
        for p, half, src, rows, dst in spans:
            pairs[pl.ds(dst, rows), :] = pltpu.bitcast(g_ref[p, half], jnp.uint32)[src:src + rows]
        step = 128
        for r0 in range(0, EVEN_P // 2, step):
            o_ref[pl.ds(2 * r0, 2 * step), :] = pltpu.bitcast(pairs[pl.ds(r0, step), :], o_ref.dtype)

    return pl.pallas_call(body, name="even_w_in_rows", out_shape=_sds((EVEN_P, g.shape[-1]), g.dtype),
                          scratch_shapes=[pltpu.VMEM((EVEN_P // 2, g.shape[-1]), jnp.uint32)],
                          compiler_params=pltpu.CompilerParams(vmem_limit_bytes=VMEM_LIMIT))(g)


def _uq_layout_t(g):
    wt = jnp.concatenate(_gathered_rows(g, B_HEADS * (B_NOPE + B_ROPE) // N_CHIPS), axis=0)
    per = B_NOPE + B_ROPE
    pad = jnp.zeros((LANES - B_ROPE, wt.shape[1]), wt.dtype)
    nope = [wt[per * h:per * h + B_NOPE] for h in range(B_HEADS)]
    rope = [jnp.concatenate([wt[per * h + B_NOPE:per * (h + 1)], pad], axis=0) for h in range(B_HEADS)]
    return jnp.concatenate(nope + rope, axis=0)


def _uq_unlayout_t(g):
    parts = []
    for h in range(B_HEADS):
        parts += [g[B_NOPE * h:B_NOPE * (h + 1)], g[512 + LANES * h:512 + LANES * h + B_ROPE]]
    return jnp.concatenate(parts, axis=0)


def _block_diag(blocks):
    rows = []
    for h, blk in enumerate(blocks):
        r, cdim = blk.shape
        n = len(blocks)
        rows.append(jnp.concatenate([jnp.zeros((r, cdim * h), blk.dtype), blk, jnp.zeros((r, cdim * (n - 1 - h)), blk.dtype)],
                                    axis=1))
    return jnp.concatenate(rows, axis=0)


def _uk_layout(w):
    return _block_diag([w[:, h, :].T for h in range(B_HEADS)])


def _latent_rows(w):
    return jnp.transpose(w[0], (1, 2, 0)).reshape(-1, w.shape[1])


def _latent_unrows(w2, shape):
    return jnp.transpose(w2.reshape(shape[2], shape[3], shape[1]), (2, 0, 1)).reshape(shape)


def _uk_unlayout(g):
    return jnp.concatenate([g[B_NOPE * h:B_NOPE * (h + 1), LANES * h:LANES * (h + 1)] for h in range(B_HEADS)], axis=0)


def _uv_layout(w):
    return _block_diag([w[:, h, :] for h in range(B_HEADS)])


def _uv_unlayout(g):
    return jnp.concatenate([g[LANES * h:LANES * (h + 1), B_V * h:B_V * (h + 1)].T for h in range(B_HEADS)], axis=0)


def _rope_tables(S):
    inv = ROPE_THETA ** (-jnp.arange(0, 32, 2, dtype=F32) / 32)
    tok = jnp.arange(S)

    def tab(pos):
        ang = pos.astype(F32)[:, None] * inv[None, :]
        cos, sin = jnp.cos(ang), jnp.sin(ang)
        return jnp.concatenate([cos, cos], axis=1), jnp.concatenate([-sin, sin], axis=1)

    cr, sr = tab(tok // GRID_W)
    cc, sc = tab(tok % GRID_W)
    ct, st = tab(tok)
    return (jnp.tile(jnp.concatenate([cr, cc], axis=1), (1, 2)), jnp.tile(jnp.concatenate([sr, sc], axis=1), (1, 2)),
            jnp.tile(ct, (1, 4)), jnp.tile(st, (1, 4)))


A_TQ, A_TK, A_SUB = 512, 4096, 512
A_FWD_SUB = 1024
B_TQ, B_TK, B_SUB = 128, 4096, 1024
B_BWD_TK, B_BWD_SUB = 4096, 512
C_T = 256
C_BLOCKS_PER_STEP = 8
KV_SHARE = 2


def _local_step(x0, tgt, mod, norm_w, wie, wuq, wuk, wuv, late_shards, a_q_norm, a_k_norm, q_lora_norm, kv_lora_norm,
                c_sink, final_norm):
    S = x0.shape[0]
    mod3 = mod.reshape(2, 3, D_MODEL)
    ca, sa, ct, st = _rope_tables(S)
    lane_seg = np.arange(LANES) // HEAD_DIM
    seg = jnp.asarray((lane_seg[:, None] == lane_seg[None, :]).astype(np.float32)).astype(BF16)
    qn = jnp.tile(a_q_norm.reshape(1, HEAD_DIM), (1, 2))
    kn = jnp.tile(a_k_norm.reshape(1, HEAD_DIM), (1, 2))
    qln, kvln = q_lora_norm.reshape(1, B_Q_LORA), kv_lora_norm.reshape(1, B_KV_LORA)
    nw0, nw1 = norm_w[0:1], norm_w[1:2]
    gate0, gate1 = mod3[0, 2:3], mod3[1, 2:3]
    a_tq, a_tk, b_tq, b_tk, bb_tk, c_t = min(A_TQ, S), min(A_TK, S), min(B_TQ, S), min(B_TK, S), min(B_BWD_TK, S), min(C_T, S)
    a_sub, b_sub, bb_sub = min(A_SUB, a_tk), min(B_SUB, b_tk), min(B_BWD_SUB, bb_tk)

    h0, proj_e, qa, ka, va, qcat, kcat, ka_t, va_t, kcat_t = _even_pre_fwd(x0, mod3[0], nw0, wie, qn, kn, seg, ca, sa, ct, st,
                                                                           qln, kvln, wuq, wuk)
    oa, lse_a, woe_g, wio_g, woo_g = _pp_fwd(qa, ka, va_t, kdiv=KV_SHARE, tq=a_tq, tk=a_tk, sub=min(A_FWD_SUB, a_tk), name="attn_a_fwd",
                                             side=_gather_chip4_halves(late_shards))
    woe = woe_g.reshape(D_MODEL, D_MODEL)
    wio = wio_g.reshape(N_CHIPS, D_MODEL, ODD_IN // N_CHIPS)
    woo = woo_g.reshape(D_MODEL, D_MODEL)
    olat, lse_b = _mla_fwd(qcat, kcat, kcat_t, tq=b_tq, tk=b_tk, sub=b_sub)
    y0, x1 = _even_post_fwd(oa, olat, proj_e, x0, gate0, wuv, woe)
    h1, gc, qc, kc, vc, kc_t, vc_t = _odd_pre_fwd(x1, mod3[1], nw1, wio)
    slopes = 2.0 ** (-8.0 * jnp.arange(1, C_HEADS + 1, dtype=F32) / C_HEADS)
    slope_rows = jnp.repeat(slopes.reshape(C_HEADS // 2, 2), c_t, axis=1)[:, None, :]
    sink_rows = jnp.repeat(c_sink.reshape(C_HEADS // 2, 2), c_t, axis=1)[:, None, :]
    win_dist = _win_dist_table(S, c_t)
    oc, lse_c = _win_fwd(qc, kc, vc_t, win_dist, slope_rows, sink_rows, kdiv=KV_SHARE, tq=c_t, nbs=C_BLOCKS_PER_STEP,
                         name="attn_c_fwd")
    doc, dgc, dx2, dwoo, st_f = _odd_post(oc, gc, x1, gate1, woo, final_norm.reshape(1, D_MODEL), tgt)
    dqc, dkc, dvc, dsink_raw = _win_bwd(qc, kc, kc_t, vc, oc, doc, lse_c, win_dist, slope_rows, sink_rows, kdiv=KV_SHARE, tq=c_t,
                                        nbs=C_BLOCKS_PER_STEP, name="attn_c_bwd")
    dx1, dwio, st_1 = _odd_pre_bwd(dqc, dkc, dvc, dgc, h1, x1, dx2, mod3[1], nw1, wio)
    doa, dga, dgb, dolat, dwoe, dwuv, st_e = _even_post_bwd(dx1, y0, oa, olat, proj_e, gate0, wuv, woe)
    late_grads = _reduce_exchange([dwoe.reshape(N_CHIPS, D_MODEL // N_CHIPS, D_MODEL), dwio,
                                   dwoo.reshape(N_CHIPS, D_MODEL // N_CHIPS, D_MODEL)])
    dqa, dka, dva, p_woe, p_wio, p_woo = _pp_bwd(qa, ka, ka_t, va, oa, doa, lse_a, kdiv=KV_SHARE, tq=a_tq, tk=a_tk, sub=a_sub,
                                                 name="attn_a_bwd", side=late_grads)
    dqcat, dkcat = _mla_bwd(qcat, kcat, kcat_t, olat, dolat, lse_b, tq=b_tq, tk=bb_tk, sub=bb_sub)
    dx0, dwie, dwuq, dwuk, st_0, nst = _even_pre_bwd(x0, h0, proj_e, dqa, dka, dva, dga, dgb, dqcat, dkcat, dx1, mod3[0], nw0,
                                                     wie, qn, kn, seg, ca, sa, ct, st, qln, kvln, wuq, wuk)
    dsink_pairs = jnp.stack([dsink_raw[:, 0, 0], dsink_raw[:, 1, 0]], axis=1).reshape(C_HEADS)
    return dict(
        loss_row=st_f[2:3], dx=dx0,
        dmod=jnp.stack([jnp.concatenate([st_0[0], st_0[1], st_e[0]]), jnp.concatenate([st_1[0], st_1[1], st_f[1]])]),
        norm_w=jnp.stack([st_0[2], st_1[2]]), final_norm=st_f[0],
        a_q_norm=nst[0:1, 0:HEAD_DIM], a_k_norm=nst[1:2, 0:HEAD_DIM], b_q_lora_norm=nst[2:3, :], b_kv_lora_norm=nst[3:4, 0:LANES],
        c_sink=dsink_pairs.reshape(1, C_HEADS),
        even_w_in=dwie, b_w_uq=dwuq, b_w_uk=dwuk, b_w_uv=dwuv, even_w_out=p_woe, odd_w_in=p_wio, odd_w_out=p_woo)


WEIGHT_NAMES = ("norm_w", "ada_w", "ada_b", "even_w_in", "a_q_norm", "a_k_norm", "b_q_lora_norm", "b_kv_lora_norm", "b_w_uq",
                "b_w_uk", "b_w_uv", "even_w_out", "odd_w_in", "c_sink", "odd_w_out", "final_norm")


def kernel(x, c, norm_w, ada_w, ada_b, even_w_in, a_q_norm, a_k_norm, b_q_lora_norm, b_kv_lora_norm, b_w_uq, b_w_uk, b_w_uv, even_w_out, odd_w_in, c_sink, odd_w_out, final_norm, loss_target, m_norm_w, m_ada_w, m_ada_b, m_even_w_in, m_a_q_norm, m_a_k_norm, m_b_q_lora_norm, m_b_kv_lora_norm, m_b_w_uq, m_b_w_uk, m_b_w_uv, m_even_w_out, m_odd_w_in, m_c_sink, m_odd_w_out, m_final_norm, v_norm_w, v_ada_w, v_ada_b, v_even_w_in, v_a_q_norm, v_a_k_norm, v_b_q_lora_norm, v_b_kv_lora_norm, v_b_w_uq, v_b_w_uk, v_b_w_uv, v_even_w_out, v_odd_w_in, v_c_sink, v_odd_w_out, v_final_norm):
    given = dict(locals())
    xi, yi, ci = _my_place()
    chip = 2 * xi + yi
    dev = 2 * chip + ci
    n_ada = ada_w.shape[2]

    (c_all,) = _gather_dev8([c], "gather_c")
    c_all = c_all.reshape(N_DEV, D_MODEL)
    bias = lax.dynamic_slice_in_dim(ada_b, chip * n_ada, n_ada, axis=1).reshape(2, 1, n_ada)
    mod_cols = _ada_fwd(c_all, ada_w, bias)
    def halves(w):
        return w.astype(BF16).reshape((2, w.shape[0] // 2) + w.shape[1:])

    mod_all, wie_g, wuq_g = _gather_chip4_halves(
        [mod_cols, _shard_halves_t(even_w_in[0]), _shard_halves_t(b_w_uq[0])]).run("gather_weights")
    mod = jnp.transpose(lax.dynamic_index_in_dim(mod_all, dev, axis=2, keepdims=False), (1, 0, 2)).reshape(2, 3 * D_MODEL)

    res = _local_step(
        x[0], loss_target[0], mod, norm_w,
        _even_in_layout_t(wie_g), _uq_layout_t(wuq_g), _uk_layout(b_w_uk[0].astype(BF16)),
        _uv_layout(b_w_uv[0].astype(BF16)), [halves(even_w_out[0]), halves(odd_w_in[0]), halves(odd_w_out[0])],
        a_q_norm, a_k_norm, b_q_lora_norm, b_kv_lora_norm, c_sink, final_norm)

    p_wie, p_wuq, small_all, p_wuk, p_wuv = _reduce_exchange(
        [res["even_w_in"].reshape(N_CHIPS, EVEN_IN // N_CHIPS, D_MODEL),
         _uq_unlayout_t(res["b_w_uq"]).astype(BF16).reshape(N_CHIPS, -1, B_Q_LORA)],
        whole=[_pack_small(res), _uk_unlayout(res["b_w_uk"]).astype(BF16), _uv_unlayout(res["b_w_uv"]).astype(BF16)],
    ).run("reduce_exchange")
    shard_parts = dict(even_w_in=p_wie, b_w_uq=p_wuq, **{k: res[k] for k in ("even_w_out", "odd_w_in", "odd_w_out")})
    dmod_all = small_all[:, 0:6, :].reshape(N_DEV, 2, 3 * D_MODEL)
    dmod_cols = jnp.transpose(lax.dynamic_slice_in_dim(dmod_all, chip * n_ada, n_ada, axis=2), (1, 0, 2))
    parts = dict(shard_parts)
    parts["ada_w"] = _ada_bwd(c_all.T, dmod_cols).reshape(1, 2 * D_MODEL, n_ada)
    parts["b_w_uk"], parts["b_w_uv"] = p_wuk, p_wuv

    def as2d(a):
        return a.reshape((-1, a.shape[-1]) if a.ndim > 1 else (1, a.shape[0]))

    results = {}
    small_outs = _adam_small(small_all, *[[as2d(given[pre + k]) for k in SMALL_WEIGHTS] for pre in ("", "m_", "v_")])
    for idx, k in enumerate(SMALL_WEIGHTS):
        results[k] = small_outs[4 * idx:4 * idx + 4]
    for k, p in parts.items():
        if k in ("even_w_in", "b_w_uq"):
            outs = _adam(p, given[k][0].T, given["m_" + k][0].T, given["v_" + k][0].T, "adam_" + k, by_columns=k == "even_w_in")
            results[k] = [o.T for o in outs]
            continue
        if k in ("b_w_uk", "b_w_uv"):
            outs = _adam(p, _latent_rows(given[k]), _latent_rows(given["m_" + k]), _latent_rows(given["v_" + k]), "adam_" + k)
            results[k] = [_latent_unrows(o, given[k].shape) for o in outs]
            continue
        shape2 = (p.shape[-2], p.shape[-1])
        results[k] = _adam(p, given[k].reshape(shape2), given["m_" + k].reshape(shape2), given["v_" + k].reshape(shape2),
                           "adam_" + k)
    by_kind = [[results[k][t].reshape(given[k].shape) for k in WEIGHT_NAMES] for t in range(4)]
    return (small_outs[-1][0, 0], res["dx"][None], *by_kind[0], *by_kind[1], *by_kind[2], *by_kind[3])
```

```python
import functools

import numpy as np
import jax
import jax.numpy as jnp
from jax import lax
from jax.experimental import pallas as pl
from jax.experimental.pallas import tpu as pltpu

F32 = jnp.float32
BF16 = jnp.bfloat16
HIGHEST = lax.Precision.HIGHEST
MESH_ID = pl.DeviceIdType.MESH

D_MODEL = 1024
HEAD_DIM = 64
GRID_W = 64
EPS = 1e-6
ROPE_THETA = 10000.0
B_HEADS, B_NOPE, B_ROPE, B_V = 8, 64, 32, 64
B_Q_LORA, B_KV_LORA = 256, 128
C_HEADS = 16
WINDOW = 128
EVEN_IN, ODD_IN = 2208, 2560
EVEN_P = 2304
EVEN_GAP = 1696
N_CHIPS, N_DEV = 4, 8
LANES = 128
NEG = -1e30
VMEM_LIMIT = 60 * 1024 * 1024

ADAM_LR, ADAM_B1, ADAM_B2, ADAM_EPS, ADAM_WD, ADAM_STEP = 0.001, 0.9, 0.999, 1e-08, 0.01, 10

ROW_TILE = 512
IN_PROJ_ROW_TILE = 256


def _dot(a, b):
    return lax.dot_general(a, b, (((1,), (0,)), ((), ())), preferred_element_type=F32)


def _dot_nt(a, b):
    return lax.dot_general(a, b, (((1,), (1,)), ((), ())), preferred_element_type=F32)


def _dot_tn(a, b):
    return lax.dot_general(a, b, (((0,), (0,)), ((), ())), preferred_element_type=F32)


def _dot_f32(a, b):
    return lax.dot_general(a, b, (((1,), (0,)), ((), ())), precision=HIGHEST, preferred_element_type=F32)


def _sigmoid(x):
    return 1.0 / (1.0 + jnp.exp(-x))


def _silu_and_grad(g):
    s = _sigmoid(g)
    return g * s, s * (1.0 + g * (1.0 - s))


def _lane_iota():
    return lax.broadcasted_iota(jnp.int32, (1, LANES), 1)


def _partner(x, lane):
    return jnp.where((lane % 32) < 16, pltpu.roll(x, LANES - 16, 1), pltpu.roll(x, 16, 1))


def _rot(x, cos, sin_signed, lane):
    return x * cos + _partner(x, lane) * sin_signed


def _rot_bwd(dy, cos, sin_signed, lane):
    return dy * cos + _partner(dy * sin_signed, lane)


def _rms(x):
    return lax.rsqrt(jnp.mean(x * x, axis=-1, keepdims=True) + EPS)


def _rms_bwd(x, r, g):
    return r * g - x * (r * r * r) * jnp.mean(x * g, axis=-1, keepdims=True)


def _seg_mean(v, seg_ones):
    hi = v.astype(BF16)
    lo = (v - hi.astype(F32)).astype(BF16)
    return (_dot(hi, seg_ones) + _dot(lo, seg_ones)) * (1.0 / HEAD_DIM)


def _dup_heads(x, lane):
    swapped = pltpu.roll(x, HEAD_DIM, 1)
    lo = lane < HEAD_DIM
    return jnp.concatenate([jnp.where(lo, x, swapped), jnp.where(lo, swapped, x)], axis=1)


def _fold_heads(x2, lane):
    a, b = x2[:, 0:LANES], x2[:, LANES:2 * LANES]
    return jnp.where(lane < HEAD_DIM, a + pltpu.roll(a, HEAD_DIM, 1), b + pltpu.roll(b, HEAD_DIM, 1))


def _row_spec(ts, cols):
    return pl.BlockSpec((ts, cols), lambda i: (i, 0))


def _full_spec(shape, single=True):
    nd = len(shape)
    if single:
        return pl.BlockSpec(shape, lambda i: (0,) * nd, pipeline_mode=pl.Buffered(1))
    return pl.BlockSpec(shape, lambda i: (0,) * nd)


def _sds(shape, dtype):
    return jax.ShapeDtypeStruct(shape, dtype)


def _params(sem):
    return pltpu.CompilerParams(dimension_semantics=sem, vmem_limit_bytes=VMEM_LIMIT)


def _even_pre_fwd(x, mod, nw, wie, qn, kn, seg, ca, sa, ct, st, qln, kvln, wuq, wuk):
    S = x.shape[0]
    ts = min(IN_PROJ_ROW_TILE, S)

    def body(x_ref, mod_ref, nw_ref, wie_ref, qn_ref, kn_ref, seg_ref, ca_ref, sa_ref, ct_ref, st_ref, qln_ref,
             kvln_ref, wuq_ref, wuk_ref, h_ref, proj_ref, qa_ref, ka_ref, va_ref, qcat_ref, kcat_ref, kat_ref, vat_ref, kcatt_ref):
        xv = x_ref[...]
        h = (xv * _rms(xv) * nw_ref[...]) * (1.0 + mod_ref[1:2, :]) + mod_ref[0:1, :]
        hb = h.astype(BF16)
        h_ref[...] = hb
        proj = _dot_nt(hb, wie_ref[...])
        proj_ref[...] = proj
        lane = _lane_iota()
        ca_v, sa_v, ct_v, st_v = ca_ref[...], sa_ref[...], ct_ref[...], st_ref[...]
        seg_v = seg_ref[...]
        for cb in range(4):
            xc = proj[:, LANES * cb:LANES * (cb + 1)]
            r = lax.rsqrt(_seg_mean(xc * xc, seg_v) + EPS)
            y = _rot(xc * r * qn_ref[...], ca_v, sa_v, lane)
            qa_ref[:, LANES * cb:LANES * (cb + 1)] = (y * 0.125).astype(BF16)
        kc = proj[:, 512:640]
        r = lax.rsqrt(_seg_mean(kc * kc, seg_v) + EPS)
        ka_v = _dup_heads(_rot(kc * r * kn_ref[...], ca_v, sa_v, lane), lane)
        ka_ref[...] = ka_v.astype(BF16)
        kat_ref[...] = ka_v.T.astype(BF16)
        va_v = _dup_heads(proj[:, 640:768], lane)
        va_ref[...] = va_v.astype(BF16)
        vat_ref[...] = va_v.T.astype(BF16)
        cq = proj[:, 1280:1536]
        cqn = (cq * _rms(cq) * qln_ref[...]).astype(BF16)
        ckv = proj[:, 1536:1664]
        ckvn = ckv * _rms(ckv) * kvln_ref[...]
        qb = _dot_nt(cqn, wuq_ref[...])
        qlat = _dot(qb[:, 0:512].astype(BF16), wuk_ref[...])
        for hh in range(B_HEADS):
            qcat_ref[hh, :, 0:LANES] = qlat[:, LANES * hh:LANES * (hh + 1)].astype(BF16)
            qr = _rot(qb[:, 512 + LANES * hh:512 + LANES * (hh + 1)], ct_v, st_v, lane)
            qcat_ref[hh, :, LANES:2 * LANES] = qr.astype(BF16)
        kr = _rot(proj[:, 1664:1792], ct_v, st_v, lane)
        kcat_ref[:, 0:LANES] = ckvn.astype(BF16)
        kcat_ref[:, LANES:2 * LANES] = kr.astype(BF16)
        kcatt_ref[0:LANES, :] = ckvn.T.astype(BF16)
        kcatt_ref[LANES:2 * LANES, :] = kr.T.astype(BF16)

    col_spec = lambda rows: pl.BlockSpec((rows, ts), lambda i: (0, i))
    return pl.pallas_call(
        body, name="even_pre_fwd", grid=(S // ts,),
        in_specs=[_row_spec(ts, D_MODEL), _full_spec((3, D_MODEL)), _full_spec((1, D_MODEL)), _full_spec((EVEN_P, D_MODEL)),
                  _full_spec((1, LANES)), _full_spec((1, LANES)), _full_spec((LANES, LANES)),
                  _row_spec(ts, LANES), _row_spec(ts, LANES), _row_spec(ts, LANES), _row_spec(ts, LANES),
                  _full_spec((1, B_Q_LORA)), _full_spec((1, B_KV_LORA)), _full_spec((1536, B_Q_LORA)), _full_spec((512, 1024))],
        out_specs=[_row_spec(ts, D_MODEL), _row_spec(ts, EVEN_P), _row_spec(ts, 512), _row_spec(ts, 2 * LANES), _row_spec(ts, 2 * LANES),
                   pl.BlockSpec((B_HEADS, ts, 2 * LANES), lambda i: (0, i, 0)), _row_spec(ts, 2 * LANES),
                   col_spec(2 * LANES), col_spec(2 * LANES), col_spec(2 * LANES)],
        out_shape=[_sds((S, D_MODEL), BF16), _sds((S, EVEN_P), F32), _sds((S, 512), BF16), _sds((S, 2 * LANES), BF16),
                   _sds((S, 2 * LANES), BF16), _sds((B_HEADS, S, 2 * LANES), BF16), _sds((S, 2 * LANES), BF16),
                   _sds((2 * LANES, S), BF16), _sds((2 * LANES, S), BF16), _sds((2 * LANES, S), BF16)],
        compiler_params=_params(("arbitrary",)),
    )(x, mod, nw, wie, qn, kn, seg, ca, sa, ct, st, qln, kvln, wuq, wuk)


MLA_SCALE = (B_NOPE + B_ROPE) ** -0.5
LOG2E = 1.4426950408889634

def _row_lo():
    return lax.broadcasted_iota(jnp.int32, (LANES, 1), 0) < HEAD_DIM


def _stack_cols(vT, rlo):
    zero = jnp.zeros_like(vT)
    return jnp.concatenate([jnp.where(rlo, vT, zero), jnp.where(rlo, zero, vT)], axis=1)


def _stack_rows(v, lo):
    zero = jnp.zeros_like(v)
    return jnp.concatenate([jnp.where(lo, v, zero), jnp.where(lo, zero, v)], axis=0)


def _pick_halves_T(xT, rlo, t):
    return jnp.where(rlo, xT[:, 0:t], xT[:, t:2 * t]).T


def _side_split(refs, n_in, n_out, n_scratch, side):
    ns = side.n if side is not None else 0
    cuts = np.cumsum([0, n_in, ns, n_out, ns, n_scratch])
    return [refs[a:b] for a, b in zip(cuts[:-1], cuts[1:])] + [refs[cuts[-1]:]]


def _side_hooks(side, side_ins, side_outs, side_sems, step, total):
    if side is None:
        return lambda: None
    start, mid, end = side.phases(side_ins, side_outs, side_sems)
    pl.when(step == 0)(start)
    pl.when(step == total // 2)(mid)
    return lambda: pl.when(step == total - 1)(end)


def _side_specs(side):
    if side is None:
        return [], [], [], [], []
    return list(side.arrs), [_ANY] * side.n, [_ANY] * side.n, list(side.out_shapes), side.sem_shapes()


def _pp_fwd(q, k, vT, *, kdiv, tq, tk, sub, name, side=None):
    S = k.shape[0]; nb = q.shape[1] // LANES; nq = S // tq; nkv = S // tk; nsub = tk // sub

    def body(*refs):
        (q_ref, k_ref, vT_ref), side_ins, (o_ref, lse_ref), side_outs, (qs, m_s, l_s, acc), side_sems = _side_split(refs, 3, 2, 4, side)
        j = pl.program_id(2)
        rlo = _row_lo()
        step = (pl.program_id(0) * nq + pl.program_id(1)) * nkv + j
        side_end = _side_hooks(side, side_ins, side_outs, side_sems, step, nb * nq * nkv)

        @pl.when(j == 0)
        def _():
            qs[...] = _stack_cols(q_ref[...].astype(F32).T, rlo).astype(BF16)
            m_s[...] = jnp.full((1, 2 * tq), NEG, F32)
            l_s[...] = jnp.zeros((1, 2 * tq), F32)
            acc[...] = jnp.zeros((LANES, 2 * tq), F32)

        qsv = qs[...]
        m, l, a = m_s[...], l_s[...], acc[...]
        s_cur = _dot(k_ref[0:sub, :], qsv)
        for t in range(nsub):
            if t + 1 < nsub:
                s_next = _dot(k_ref[sub * (t + 1):sub * (t + 2), :], qsv)
            m_new = jnp.maximum(m, jnp.max(s_cur, axis=0, keepdims=True))
            alpha = jnp.exp(m - m_new)
            p = jnp.exp(s_cur - m_new)
            l = alpha * l + jnp.sum(p, axis=0, keepdims=True)
            a = alpha * a + _dot(vT_ref[:, sub * t:sub * (t + 1)], p.astype(BF16))
            m = m_new
            if t + 1 < nsub:
                s_cur = s_next
        m_s[...], l_s[...], acc[...] = m, l, a

        @pl.when(j == nkv - 1)
        def _():
            l_f = l_s[...]
            o_ref[...] = _pick_halves_T(acc[...] / l_f, rlo, tq).astype(BF16)
            lse_ref[0, 0] = m_s[...] + jnp.log(l_f)

        side_end()

    s_args, s_in, s_out, s_shapes, s_sems = _side_specs(side)
    return pl.pallas_call(
        body, name=name, grid=(nb, nq, nkv),
        in_specs=[pl.BlockSpec((tq, LANES), lambda b, i, j: (i, b)), pl.BlockSpec((tk, LANES), lambda b, i, j: (j, b // kdiv)),
                  pl.BlockSpec((LANES, tk), lambda b, i, j: (b // kdiv, j))] + s_in,
        out_specs=[pl.BlockSpec((tq, LANES), lambda b, i, j: (i, b)),
                   pl.BlockSpec((1, 1, 1, 2 * tq), lambda b, i, j: (b, i, 0, 0))] + s_out,
        out_shape=[_sds((S, nb * LANES), BF16), _sds((nb, nq, 1, 2 * tq), F32)] + s_shapes,
        scratch_shapes=[pltpu.VMEM((LANES, 2 * tq), BF16), pltpu.VMEM((1, 2 * tq), F32), pltpu.VMEM((1, 2 * tq), F32),
                        pltpu.VMEM((LANES, 2 * tq), F32)] + s_sems,
        compiler_params=_params(("arbitrary",) * 3))(q, k, vT, *s_args)


def _pp_bwd(q, k, kT, v, o, do, lse, *, kdiv, tq, tk, sub, name, side=None):
    S = k.shape[0]; nb = q.shape[1] // LANES; nkb = k.shape[1] // LANES; nq = S // tq; nkv = S // tk; nsub = tk // sub

    def body(*refs):
        ((q_ref, k_ref, kT_ref, v_ref, o_ref, do_ref, lse_ref), side_ins, (dq_ref, dk_ref, dv_ref), side_outs,
         (qsT, qs, dosT, dos, delta_s, dq_acc), side_sems) = _side_split(refs, 7, 3, 6, side)
        b, i, j = pl.program_id(0), pl.program_id(1), pl.program_id(2)
        rlo = _row_lo()
        lo = lax.broadcasted_iota(jnp.int32, (1, LANES), 1) < HEAD_DIM
        side_end = _side_hooks(side, side_ins, side_outs, side_sems, (b * nq + i) * nkv + j, nb * nq * nkv)

        @pl.when((b % kdiv == 0) & (i == 0) & (j == 0))
        def _():
            dk_ref[...] = jnp.zeros((S, LANES), F32)
            dv_ref[...] = jnp.zeros((S, LANES), F32)

        @pl.when(j == 0)
        def _():
            qv = q_ref[...]
            qs[...] = _stack_rows(qv, lo)
            qsT[...] = _stack_cols(qv.astype(F32).T, rlo).astype(BF16)
            dov = do_ref[...].astype(F32)
            dos[...] = _stack_rows(dov.astype(BF16), lo)
            dosT[...] = _stack_cols(dov.T, rlo).astype(BF16)
            prodT = (dov * o_ref[...].astype(F32)).T
            delta_s[...] = jnp.concatenate([jnp.sum(jnp.where(rlo, prodT, 0.0), axis=0, keepdims=True),
                                            jnp.sum(jnp.where(rlo, 0.0, prodT), axis=0, keepdims=True)], axis=1)
            dq_acc[...] = jnp.zeros((LANES, 2 * tq), F32)

        qsTv, dosTv, qsv, dosv = qsT[...], dosT[...], qs[...], dos[...]
        lse_v, delta_v = lse_ref[0, 0], delta_s[...]
        dqa = dq_acc[...]
        s_cur = _dot(k_ref[0:sub, :], qsTv)
        dp_cur = _dot(v_ref[0:sub, :], dosTv)
        for t in range(nsub):
            if t + 1 < nsub:
                s_next = _dot(k_ref[sub * (t + 1):sub * (t + 2), :], qsTv)
                dp_next = _dot(v_ref[sub * (t + 1):sub * (t + 2), :], dosTv)
            p = jnp.exp(s_cur - lse_v)
            ds = (p * (dp_cur - delta_v)).astype(BF16)
            rows = pl.ds(pl.multiple_of(j * tk + sub * t, sub), sub)
            dv_ref[rows, :] += _dot(p.astype(BF16), dosv)
            dk_ref[rows, :] += _dot(ds, qsv)
            dqa = dqa + _dot(kT_ref[:, sub * t:sub * (t + 1)], ds)
            if t + 1 < nsub:
                s_cur, dp_cur = s_next, dp_next
        dq_acc[...] = dqa

        @pl.when(j == nkv - 1)
        def _():
            dq_ref[...] = _pick_halves_T(dq_acc[...], rlo, tq)

        side_end()

    qmap = lambda b, i, j: (i, b)
    kmap = lambda b, i, j: (j, b // kdiv)
    res = lambda b, i, j: (0, b // kdiv)
    s_args, s_in, s_out, s_shapes, s_sems = _side_specs(side)
    return pl.pallas_call(
        body, name=name, grid=(nb, nq, nkv),
        in_specs=[pl.BlockSpec((tq, LANES), qmap), pl.BlockSpec((tk, LANES), kmap), pl.BlockSpec((LANES, tk), lambda b, i, j: (b // kdiv, j)),
                  pl.BlockSpec((tk, LANES), kmap), pl.BlockSpec((tq, LANES), qmap), pl.BlockSpec((tq, LANES), qmap),
                  pl.BlockSpec((1, 1, 1, 2 * tq), lambda b, i, j: (b, i, 0, 0))] + s_in,
        out_specs=[pl.BlockSpec((tq, LANES), qmap), pl.BlockSpec((S, LANES), res), pl.BlockSpec((S, LANES), res)] + s_out,
        out_shape=[_sds((S, nb * LANES), F32), _sds((S, nkb * LANES), F32), _sds((S, nkb * LANES), F32)] + s_shapes,
        scratch_shapes=[pltpu.VMEM((LANES, 2 * tq), BF16), pltpu.VMEM((2 * tq, LANES), BF16), pltpu.VMEM((LANES, 2 * tq), BF16),
                        pltpu.VMEM((2 * tq, LANES), BF16), pltpu.VMEM((1, 2 * tq), F32), pltpu.VMEM((LANES, 2 * tq), F32)] + s_sems,
        compiler_params=_params(("arbitrary",) * 3))(q, k, kT, v, o, do, lse, *s_args)


MLA_C = MLA_SCALE * LOG2E


def _mla_fwd(q, kcat, kcatT, *, tq, tk, sub):
    S = kcat.shape[0]; nq, nkv = S // tq, S // tk; R = B_HEADS * tq; nsub = tk // sub

    def body(q_ref, k_ref, vT_ref, o_ref, lse_ref, qT, m_s, l_s, acc):
        j = pl.program_id(1)

        @pl.when(j == 0)
        def _():
            qT[...] = q_ref[...].reshape(R, 2 * LANES).astype(F32).T.astype(BF16)
            m_s[...] = jnp.full((1, R), NEG, F32)
            l_s[...] = jnp.zeros((1, R), F32)
            acc[...] = jnp.zeros((LANES, R), F32)

        qTv = qT[...]
        m, l, a = m_s[...], l_s[...], acc[...]
        s_cur = _dot(k_ref[0:sub, :], qTv)
        for t in range(nsub):
            if t + 1 < nsub:
                s_next = _dot(k_ref[sub * (t + 1):sub * (t + 2), :], qTv)
            m_new = jnp.maximum(m, jnp.max(s_cur, axis=0, keepdims=True))
            alpha = jnp.exp2((m - m_new) * MLA_C)
            p = jnp.exp2((s_cur - m_new) * MLA_C)
            l = alpha * l + jnp.sum(p, axis=0, keepdims=True)
            a = alpha * a + _dot(vT_ref[:, sub * t:sub * (t + 1)], p.astype(BF16))
            m = m_new
            if t + 1 < nsub:
                s_cur = s_next
        m_s[...], l_s[...], acc[...] = m, l, a

        @pl.when(j == nkv - 1)
        def _():
            l_f = l_s[...]
            o_ref[...] = (acc[...] / l_f).T.reshape(B_HEADS, tq, LANES).astype(BF16)
            lse_ref[0] = m_s[...] * MLA_SCALE + jnp.log(l_f)

    return pl.pallas_call(
        body, name="mla_fwd", grid=(nq, nkv),
        in_specs=[pl.BlockSpec((B_HEADS, tq, 2 * LANES), lambda i, j: (0, i, 0)), pl.BlockSpec((tk, 2 * LANES), lambda i, j: (j, 0)),
                  pl.BlockSpec((LANES, tk), lambda i, j: (0, j))],
        out_specs=[pl.BlockSpec((B_HEADS, tq, LANES), lambda i, j: (0, i, 0)), pl.BlockSpec((1, 1, R), lambda i, j: (i, 0, 0))],
        out_shape=[_sds((B_HEADS, S, LANES), BF16), _sds((nq, 1, R), F32)],
        scratch_shapes=[pltpu.VMEM((2 * LANES, R), BF16), pltpu.VMEM((1, R), F32), pltpu.VMEM((1, R), F32), pltpu.VMEM((LANES, R), F32)],
        compiler_params=_params(("arbitrary", "arbitrary")))(q, kcat, kcatT)


def _mla_bwd(q, kcat, kcatT, o, do, lse, *, tq, tk, sub):
    S = kcat.shape[0]; nq, nkv = S // tq, S // tk; R = B_HEADS * tq; nsub = tk // sub

    def body(q_ref, k_ref, kT_ref, o_ref, do_ref, lse_ref, dq_ref, dk_ref, qT, dosT, dos, delta_s, dq_acc):
        i, j = pl.program_id(0), pl.program_id(1)

        @pl.when((i == 0) & (j == 0))
        def _():
            dk_ref[...] = jnp.zeros((S, 2 * LANES), F32)

        @pl.when(j == 0)
        def _():
            qT[...] = q_ref[...].reshape(R, 2 * LANES).astype(F32).T.astype(BF16)
            dov = do_ref[...].reshape(R, LANES).astype(F32)
            dos[...] = dov.astype(BF16)
            dosT[...] = dov.T.astype(BF16)
            delta_s[...] = jnp.sum((dov * o_ref[...].reshape(R, LANES).astype(F32)).T, axis=0, keepdims=True)
            dq_acc[...] = jnp.zeros((2 * LANES, R), F32)

        qTv, dosTv, dosv = qT[...], dosT[...], dos[...]
        qv = q_ref[...].reshape(R, 2 * LANES)
        lse_v, delta_v = lse_ref[0] * LOG2E, delta_s[...]
        dqa = dq_acc[...]
        s_cur = _dot(k_ref[0:sub, :], qTv)
        dp_cur = _dot(k_ref[0:sub, 0:LANES], dosTv)
        for t in range(nsub):
            if t + 1 < nsub:
                s_next = _dot(k_ref[sub * (t + 1):sub * (t + 2), :], qTv)
                dp_next = _dot(k_ref[sub * (t + 1):sub * (t + 2), 0:LANES], dosTv)
            p = jnp.exp2(s_cur * MLA_C - lse_v)
            ds = (p * (dp_cur - delta_v) * MLA_SCALE).astype(BF16)
            rows = pl.ds(pl.multiple_of(j * tk + sub * t, sub), sub)
            dk_ref[rows, :] += _dot(ds, qv)
            dk_ref[rows, 0:LANES] += _dot(p.astype(BF16), dosv)
            dqa = dqa + _dot(kT_ref[:, sub * t:sub * (t + 1)], ds)
            if t + 1 < nsub:
                s_cur, dp_cur = s_next, dp_next
        dq_acc[...] = dqa

        @pl.when(j == nkv - 1)
        def _():
            dq_ref[...] = dq_acc[...].T.reshape(B_HEADS, tq, 2 * LANES)

    hspec = lambda w: pl.BlockSpec((B_HEADS, tq, w), lambda i, j: (0, i, 0))
    return pl.pallas_call(
        body, name="mla_bwd", grid=(nq, nkv),
        in_specs=[hspec(2 * LANES), pl.BlockSpec((tk, 2 * LANES), lambda i, j: (j, 0)), pl.BlockSpec((2 * LANES, tk), lambda i, j: (0, j)),
                  hspec(LANES), hspec(LANES), pl.BlockSpec((1, 1, R), lambda i, j: (i, 0, 0))],
        out_specs=[hspec(2 * LANES), pl.BlockSpec((S, 2 * LANES), lambda i, j: (0, 0))],
        out_shape=[_sds((B_HEADS, S, 2 * LANES), F32), _sds((S, 2 * LANES), F32)],
        scratch_shapes=[pltpu.VMEM((2 * LANES, R), BF16), pltpu.VMEM((LANES, R), BF16), pltpu.VMEM((R, LANES), BF16),
                        pltpu.VMEM((1, R), F32), pltpu.VMEM((2 * LANES, R), F32)],
        compiler_params=_params(("arbitrary", "arbitrary")))(q, kcat, kcatT, o, do, lse)


def _win_start(i, tq, nk, S):
    return pl.multiple_of(jnp.clip(i * tq - WINDOW, 0, S - nk), LANES)


def _win_dist_table(S, tq):
    nk = min(tq + 2 * WINDOW, S)
    nq = S // tq
    r = np.arange(nk)[:, None]
    c = (np.arange(2 * tq) % tq)[None, :]
    tabs = []
    for rel in (0, WINDOW, (nq - 1) * tq - (S - nk)):
        dist = np.abs(rel + c - r).astype(np.float32)
        tabs.append(np.where(dist <= WINDOW, dist, np.float32(1e32)))
    return jnp.asarray(np.stack(tabs))


def _win_dist_spec(nk, tq, nq):
    return pl.BlockSpec((1, nk, 2 * tq), lambda b, i: (jnp.where(i == 0, 0, jnp.where(i == nq - 1, 2, 1)), 0, 0))


def _win_fwd(q, k, vT, dist, slope, sink, *, kdiv, tq, nbs, name):
    S = k.shape[0]; nb = q.shape[1] // LANES; nq = S // tq; nk = min(tq + 2 * WINDOW, S)
    assert nb % nbs == 0 and nbs % kdiv == 0
    kvw = (nbs // kdiv) * LANES

    def body(q_ref, k_ref, vT_ref, dist_ref, slope_ref, sink_ref, o_ref, lse_ref):
        i = pl.program_id(1)
        rlo = _row_lo()
        k0 = _win_start(i, tq, nk, S)
        kk, vv, dd = k_ref[pl.ds(k0, nk), :], vT_ref[:, pl.ds(k0, nk)], dist_ref[0]
        for u in range(nbs):
            kv = slice(LANES * (u // kdiv), LANES * (u // kdiv + 1))
            qsT = _stack_cols(q_ref[:, LANES * u:LANES * (u + 1)].astype(F32).T, rlo).astype(BF16)
            s = _dot(kk[:, kv], qsT) - slope_ref[u] * dd
            sk = sink_ref[u]
            m = jnp.maximum(jnp.max(s, axis=0, keepdims=True), sk)
            p = jnp.exp(s - m)
            l = jnp.sum(p, axis=0, keepdims=True) + jnp.exp(sk - m)
            o_ref[:, LANES * u:LANES * (u + 1)] = _pick_halves_T(_dot(vv[kv, :], p.astype(BF16)) / l, rlo, tq).astype(BF16)
            lse_ref[u, 0] = m + jnp.log(l)

    row_spec = pl.BlockSpec((nbs, 1, 2 * tq), lambda b, i: (b, 0, 0))
    return pl.pallas_call(
        body, name=name, grid=(nb // nbs, nq),
        in_specs=[pl.BlockSpec((tq, nbs * LANES), lambda b, i: (i, b)), pl.BlockSpec((S, kvw), lambda b, i: (0, b)),
                  pl.BlockSpec((kvw, S), lambda b, i: (b, 0)), _win_dist_spec(nk, tq, nq), row_spec, row_spec],
        out_specs=[pl.BlockSpec((tq, nbs * LANES), lambda b, i: (i, b)), pl.BlockSpec((nbs, 1, 1, 2 * tq), lambda b, i: (b, i, 0, 0))],
        out_shape=[_sds((S, nb * LANES), BF16), _sds((nb, nq, 1, 2 * tq), F32)],
        compiler_params=_params(("arbitrary", "arbitrary")))(q, k, vT, dist, slope, sink)


def _win_bwd(q, k, kT, v, o, do, lse, dist, slope, sink, *, kdiv, tq, nbs, name):
    S = k.shape[0]; nb = q.shape[1] // LANES; nkb = k.shape[1] // LANES; nq = S // tq; nk = min(tq + 2 * WINDOW, S)
    assert nb % nbs == 0 and nbs % kdiv == 0
    nkv = nbs // kdiv
    kvw = nkv * LANES

    def body(q_ref, k_ref, kT_ref, v_ref, o_ref, do_ref, lse_ref, dist_ref, slope_ref, sink_ref, dq_ref, dk_ref, dv_ref, dsink_ref, ds_acc):
        i = pl.program_id(1)
        rlo = _row_lo()
        lo = lax.broadcasted_iota(jnp.int32, (1, LANES), 1) < HEAD_DIM

        @pl.when(i == 0)
        def _():
            dk_ref[...] = jnp.zeros((S, kvw), F32)
            dv_ref[...] = jnp.zeros((S, kvw), F32)
            ds_acc[...] = jnp.zeros((nbs, 2 * tq), F32)

        k0 = _win_start(i, tq, nk, S)
        rows = pl.ds(k0, nk)
        kk_all, vv_all, kkT_all, dd = k_ref[rows, :], v_ref[rows, :], kT_ref[:, rows], dist_ref[0]
        dv_sum, dk_sum = [None] * nkv, [None] * nkv
        for u in range(nbs):
            g = u // kdiv
            kv = slice(LANES * g, LANES * (g + 1))
            kk, vv, kkT = kk_all[:, kv], vv_all[:, kv], kkT_all[kv, :]
            cols = slice(LANES * u, LANES * (u + 1))
            qv = q_ref[:, cols]
            qs = _stack_rows(qv, lo)
            qsT = _stack_cols(qv.astype(F32).T, rlo).astype(BF16)
            dov = do_ref[:, cols].astype(F32)
            dos = _stack_rows(dov.astype(BF16), lo)
            dosT = _stack_cols(dov.T, rlo).astype(BF16)
            prodT = (dov * o_ref[:, cols].astype(F32)).T
            delta = jnp.concatenate([jnp.sum(jnp.where(rlo, prodT, 0.0), axis=0, keepdims=True),
                                     jnp.sum(jnp.where(rlo, 0.0, prodT), axis=0, keepdims=True)], axis=1)
            lse_v = lse_ref[u, 0]
            ds_acc[u:u + 1, :] += -jnp.exp(sink_ref[u] - lse_v) * delta
            p = jnp.exp(_dot(kk, qsT) - slope_ref[u] * dd - lse_v)
            ds = (p * (_dot(vv, dosT) - delta)).astype(BF16)
            dv_u, dk_u = _dot(p.astype(BF16), dos), _dot(ds, qs)
            dv_sum[g] = dv_u if dv_sum[g] is None else dv_sum[g] + dv_u
            dk_sum[g] = dk_u if dk_sum[g] is None else dk_sum[g] + dk_u
            dq_ref[:, cols] = (_pick_halves_T(_dot(kkT, ds), rlo, tq) * 0.125).astype(BF16)
        dv_ref[rows, :] += jnp.concatenate(dv_sum, axis=1)
        dk_ref[rows, :] += jnp.concatenate(dk_sum, axis=1)

        @pl.when(i == nq - 1)
        def _():
            acc = ds_acc[...]
            for u in range(nbs):
                dsink_ref[u] = jnp.concatenate(
                    [jnp.broadcast_to(jnp.sum(acc[u:u + 1, 0:tq], axis=1, keepdims=True), (1, LANES)),
                     jnp.broadcast_to(jnp.sum(acc[u:u + 1, tq:2 * tq], axis=1, keepdims=True), (1, LANES)),
                     jnp.zeros((6, LANES), F32)], axis=0)

    qmap = lambda b, i: (i, b)
    kv_spec = pl.BlockSpec((S, kvw), lambda b, i: (0, b))
    row_spec = pl.BlockSpec((nbs, 1, 2 * tq), lambda b, i: (b, 0, 0))
    wide = pl.BlockSpec((tq, nbs * LANES), qmap)
    return pl.pallas_call(
        body, name=name, grid=(nb // nbs, nq),
        in_specs=[wide, kv_spec, pl.BlockSpec((kvw, S), lambda b, i: (b, 0)), kv_spec, wide, wide,
                  pl.BlockSpec((nbs, 1, 1, 2 * tq), lambda b, i: (b, i, 0, 0)), _win_dist_spec(nk, tq, nq), row_spec, row_spec],
        out_specs=[wide, kv_spec, kv_spec, pl.BlockSpec((nbs, 8, LANES), lambda b, i: (b, 0, 0))],
        out_shape=[_sds((S, nb * LANES), BF16), _sds((S, nkb * LANES), F32), _sds((S, nkb * LANES), F32), _sds((nb, 8, LANES), F32)],
        scratch_shapes=[pltpu.VMEM((nbs, 2 * tq), F32)],
        compiler_params=_params(("arbitrary", "arbitrary")))(q, k, kT, v, o, do, lse, dist, slope, sink)


def _sum_rows(v):
    return jnp.sum(v, axis=0, keepdims=True)


def _norm_mod_bwd(dh, xv, mod_ref, nw_ref, stats_ref):
    r = _rms(xv)
    xn = xv * r
    nw = nw_ref[...]
    stats_ref[0:1, :] += _sum_rows(dh)
    stats_ref[1:2, :] += _sum_rows(dh * (xn * nw))
    dn = dh * (1.0 + mod_ref[1:2, :])
    stats_ref[2:3, :] += _sum_rows(dn * xn)
    return _rms_bwd(xv, r, dn * nw)


def _even_gate_specs(ts):
    return [pl.BlockSpec((ts, 256), lambda i, c=c: (i, c)) for c in (3, 4, 7, 8)]


def _even_post_fwd(oa, olat, proj, x, gate, wuv, woe):
    S = x.shape[0]
    ts = min(ROW_TILE, S)

    def body(oa_ref, ol_ref, ga0_ref, ga1_ref, gb0_ref, gb1_ref, x_ref, gate_ref, wuv_ref, woe_ref, y_ref, x1_ref):
        sa, _ = _silu_and_grad(jnp.concatenate([ga0_ref[...], ga1_ref[...]], axis=1))
        sb, _ = _silu_and_grad(jnp.concatenate([gb0_ref[...], gb1_ref[...]], axis=1))
        olc = jnp.concatenate([ol_ref[hh] for hh in range(B_HEADS)], axis=1).astype(BF16)
        ob = _dot(olc, wuv_ref[...])
        mix = jnp.concatenate([oa_ref[...] * sa, ob * sb], axis=1).astype(BF16)
        y = _dot(mix, woe_ref[...])
        y_ref[...] = y.astype(BF16)
        x1_ref[...] = x_ref[...] + gate_ref[...] * y

    return pl.pallas_call(
        body, name="even_post_fwd", grid=(S // ts,),
        in_specs=[_row_spec(ts, 512), pl.BlockSpec((B_HEADS, ts, LANES), lambda i: (0, i, 0))] + _even_gate_specs(ts) +
                 [_row_spec(ts, D_MODEL), _full_spec((1, D_MODEL)), _full_spec((1024, 512)), _full_spec((1024, D_MODEL))],
        out_specs=[_row_spec(ts, D_MODEL), _row_spec(ts, D_MODEL)],
        out_shape=[_sds((S, D_MODEL), BF16), _sds((S, D_MODEL), F32)],
        compiler_params=_params(("arbitrary",)),
    )(oa, olat, proj, proj, proj, proj, x, gate, wuv, woe)


def _odd_pre_fwd(x, mod, nw, wio):
    S = x.shape[0]
    ts = min(ROW_TILE, S)

    def body(x_ref, mod_ref, nw_ref, wio_ref, h_ref, g_ref, q_ref, k_ref, v_ref, kt_ref, vt_ref):
        xv = x_ref[...]
        h = (xv * _rms(xv) * nw_ref[...]) * (1.0 + mod_ref[1:2, :]) + mod_ref[0:1, :]
        hb = h.astype(BF16)
        h_ref[...] = hb
        proj = jnp.concatenate([_dot(hb, wio_ref[p]) for p in range(N_CHIPS)], axis=1)
        g_ref[...] = proj[:, 1536:2560]
        q_ref[...] = (proj[:, 0:1024] * 0.125).astype(BF16)
        lane = _lane_iota()
        k_v = jnp.concatenate([_dup_heads(proj[:, 1024 + LANES * j:1024 + LANES * (j + 1)], lane) for j in range(2)], axis=1)
        v_v = jnp.concatenate([_dup_heads(proj[:, 1280 + LANES * j:1280 + LANES * (j + 1)], lane) for j in range(2)], axis=1)
        k_ref[...] = k_v.astype(BF16)
        v_ref[...] = v_v.astype(BF16)
        kt_ref[...] = k_v.T.astype(BF16)
        vt_ref[...] = v_v.T.astype(BF16)

    col_spec = pl.BlockSpec((512, ts), lambda i: (0, i))
    return pl.pallas_call(
        body, name="odd_pre_fwd", grid=(S // ts,),
        in_specs=[_row_spec(ts, D_MODEL), _full_spec((3, D_MODEL)), _full_spec((1, D_MODEL)),
                  _full_spec((N_CHIPS, D_MODEL, ODD_IN // N_CHIPS))],
        out_specs=[_row_spec(ts, D_MODEL), _row_spec(ts, 1024), _row_spec(ts, 1024), _row_spec(ts, 512), _row_spec(ts, 512),
                   col_spec, col_spec],
        out_shape=[_sds((S, D_MODEL), BF16), _sds((S, 1024), F32), _sds((S, 1024), BF16), _sds((S, 512), BF16),
                   _sds((S, 512), BF16), _sds((512, S), BF16), _sds((512, S), BF16)],
        compiler_params=_params(("arbitrary",)),
    )(x, mod, nw, wio)


def _odd_post(oc, g, x1, gate, woo, fw, tgt):
    S = x1.shape[0]
    ts = min(ROW_TILE, S)
    nsteps = S // ts

    def body(oc_ref, g_ref, x_ref, gate_ref, woo_ref, fw_ref, tgt_ref, doc_ref, dgc_ref, dx2_ref, dwoo_out, stats_ref, dwoo_ref):
        @pl.when(pl.program_id(0) == 0)
        def _():
            dwoo_ref[...] = jnp.zeros((D_MODEL, D_MODEL), F32)
            stats_ref[...] = jnp.zeros((8, D_MODEL), F32)

        ocv = oc_ref[...]
        sg, dsg = _silu_and_grad(g_ref[...])
        mix = (ocv * sg).astype(BF16)
        woo_v = woo_ref[...]
        y = _dot(mix, woo_v)
        gate_v = gate_ref[...]
        x2 = x_ref[...] + gate_v * y
        r = _rms(x2)
        fw_v = fw_ref[...]
        xn = x2 * r
        err = xn * fw_v - tgt_ref[...]
        dout = err * (1.0 / D_MODEL)
        dx2 = _rms_bwd(x2, r, dout * fw_v)
        dx2_ref[...] = dx2
        stats_ref[0:1, :] += _sum_rows(dout * xn)
        stats_ref[1:2, :] += _sum_rows(dx2 * y)
        loss_t = 0.5 * jnp.sum(_sum_rows(err * dout), axis=-1, keepdims=True)
        stats_ref[2:3, :] += jnp.broadcast_to(loss_t, (1, D_MODEL))
        dy = (gate_v * dx2).astype(BF16)
        dmix = _dot_nt(dy, woo_v)
        dwoo_ref[...] += _dot_tn(mix, dy)
        doc_ref[...] = (dmix * sg).astype(BF16)
        dgc_ref[...] = (dmix * ocv * dsg).astype(BF16)

        @pl.when(pl.program_id(0) == nsteps - 1)
        def _():
            dwoo_out[...] = dwoo_ref[...].astype(BF16)

    return pl.pallas_call(
        body, name="odd_post", grid=(nsteps,),
        in_specs=[_row_spec(ts, D_MODEL), _row_spec(ts, D_MODEL), _row_spec(ts, D_MODEL), _full_spec((1, D_MODEL)),
                  _full_spec((D_MODEL, D_MODEL)), _full_spec((1, D_MODEL)), _row_spec(ts, D_MODEL)],
        out_specs=[_row_spec(ts, D_MODEL), _row_spec(ts, D_MODEL), _row_spec(ts, D_MODEL),
                   _full_spec((D_MODEL, D_MODEL), single=False), _full_spec((8, D_MODEL), single=False)],
        out_shape=[_sds((S, D_MODEL), BF16), _sds((S, D_MODEL), BF16), _sds((S, D_MODEL), F32), _sds((D_MODEL, D_MODEL), BF16),
                   _sds((8, D_MODEL), F32)],
        scratch_shapes=[pltpu.VMEM((D_MODEL, D_MODEL), F32)],
        compiler_params=_params(("arbitrary",)),
    )(oc, g, x1, gate, woo, fw, tgt)


def _odd_pre_bwd(dq, dk, dv, dgc, h, x, dx_res, mod, nw, wio):
    S = x.shape[0]
    ts = min(IN_PROJ_ROW_TILE, S)
    nsteps = S // ts
    wsh = ODD_IN // N_CHIPS

    def body(dq_ref, dk_ref, dv_ref, dgc_ref, h_ref, x_ref, dxr_ref, mod_ref, nw_ref, wio_ref, dx_ref, dw_ref, stats_ref, dw_acc):
        @pl.when(pl.program_id(0) == 0)
        def _():
            dw_acc[...] = jnp.zeros((N_CHIPS, D_MODEL, wsh), F32)
            stats_ref[...] = jnp.zeros((8, D_MODEL), F32)

        lane = _lane_iota()
        dkv = [_fold_heads(r[:, 2 * LANES * j:2 * LANES * (j + 1)], lane).astype(BF16) for r in (dk_ref, dv_ref) for j in range(2)]
        dproj = jnp.concatenate([dq_ref[...]] + dkv + [dgc_ref[...]], axis=1)
        hv = h_ref[...]
        dh = None
        for p in range(N_CHIPS):
            dp_cols = dproj[:, wsh * p:wsh * (p + 1)]
            part = _dot_nt(dp_cols, wio_ref[p])
            dh = part if dh is None else dh + part
            dw_acc[p] += _dot_tn(hv, dp_cols)
        dx_ref[...] = dxr_ref[...] + _norm_mod_bwd(dh, x_ref[...], mod_ref, nw_ref, stats_ref)

        @pl.when(pl.program_id(0) == nsteps - 1)
        def _():
            dw_ref[...] = dw_acc[...].astype(BF16)

    return pl.pallas_call(
        body, name="odd_pre_bwd", grid=(nsteps,),
        in_specs=[_row_spec(ts, 1024), _row_spec(ts, 512), _row_spec(ts, 512), _row_spec(ts, 1024), _row_spec(ts, D_MODEL),
                  _row_spec(ts, D_MODEL), _row_spec(ts, D_MODEL), _full_spec((3, D_MODEL)), _full_spec((1, D_MODEL)),
                  _full_spec((N_CHIPS, D_MODEL, wsh))],
        out_specs=[_row_spec(ts, D_MODEL), _full_spec((N_CHIPS, D_MODEL, wsh), single=False), _full_spec((8, D_MODEL), single=False)],
        out_shape=[_sds((S, D_MODEL), F32), _sds((N_CHIPS, D_MODEL, wsh), BF16), _sds((8, D_MODEL), F32)],
        scratch_shapes=[pltpu.VMEM((N_CHIPS, D_MODEL, wsh), F32)],
        compiler_params=_params(("arbitrary",)),
    )(dq, dk, dv, dgc, h, x, dx_res, mod, nw, wio)


def _even_post_bwd(dx1, y, oa, olat, proj, gate, wuv, woe):
    S = dx1.shape[0]
    ts = min(ROW_TILE, S)
    nsteps = S // ts

    def body(dx_ref, y_ref, oa_ref, ol_ref, ga0_ref, ga1_ref, gb0_ref, gb1_ref, gate_ref, wuv_ref, woe_ref,
             doa_ref, dga_ref, dgb_ref, dol_ref, dwoe_out, dwuv_ref, stats_ref, dwoe_ref):
        @pl.when(pl.program_id(0) == 0)
        def _():
            dwoe_ref[...] = jnp.zeros((D_MODEL, D_MODEL), F32)
            dwuv_ref[...] = jnp.zeros((1024, 512), F32)
            stats_ref[...] = jnp.zeros((8, D_MODEL), F32)

        dxv = dx_ref[...]
        stats_ref[0:1, :] += _sum_rows(dxv * y_ref[...])
        dy = (gate_ref[...] * dxv).astype(BF16)
        sa, dsa = _silu_and_grad(jnp.concatenate([ga0_ref[...], ga1_ref[...]], axis=1))
        sb, dsb = _silu_and_grad(jnp.concatenate([gb0_ref[...], gb1_ref[...]], axis=1))
        olc = jnp.concatenate([ol_ref[hh] for hh in range(B_HEADS)], axis=1).astype(BF16)
        wuv_v = wuv_ref[...]
        ob = _dot(olc, wuv_v)
        oav = oa_ref[...]
        mix = jnp.concatenate([oav * sa, ob * sb], axis=1).astype(BF16)
        dmix = _dot_nt(dy, woe_ref[...])
        dwoe_ref[...] += _dot_tn(mix, dy)
        dma, dmb = dmix[:, 0:512], dmix[:, 512:1024]
        doa_ref[...] = (dma * sa).astype(BF16)
        dga_ref[...] = (dma * oav * dsa).astype(BF16)
        dgb_ref[...] = (dmb * ob * dsb).astype(BF16)
        dob = (dmb * sb).astype(BF16)
        dol = _dot_nt(dob, wuv_v)
        dwuv_ref[...] += _dot_tn(olc, dob)
        for hh in range(B_HEADS):
            dol_ref[hh] = dol[:, LANES * hh:LANES * (hh + 1)].astype(BF16)

        @pl.when(pl.program_id(0) == nsteps - 1)
        def _():
            dwoe_out[...] = dwoe_ref[...].astype(BF16)

    head_spec = pl.BlockSpec((B_HEADS, ts, LANES), lambda i: (0, i, 0))
    return pl.pallas_call(
        body, name="even_post_bwd", grid=(nsteps,),
        in_specs=[_row_spec(ts, D_MODEL), _row_spec(ts, D_MODEL), _row_spec(ts, 512), head_spec] + _even_gate_specs(ts) +
                 [_full_spec((1, D_MODEL)), _full_spec((1024, 512)), _full_spec((1024, D_MODEL))],
        out_specs=[_row_spec(ts, 512), _row_spec(ts, 512), _row_spec(ts, 512), head_spec,
                   _full_spec((D_MODEL, D_MODEL), single=False), _full_spec((1024, 512), single=False),
                   _full_spec((8, D_MODEL), single=False)],
        out_shape=[_sds((S, 512), BF16), _sds((S, 512), BF16), _sds((S, 512), BF16), _sds((B_HEADS, S, LANES), BF16),
                   _sds((D_MODEL, D_MODEL), BF16), _sds((1024, 512), F32), _sds((8, D_MODEL), F32)],
        scratch_shapes=[pltpu.VMEM((D_MODEL, D_MODEL), F32)],
        compiler_params=_params(("arbitrary",)),
    )(dx1, y, oa, olat, proj, proj, proj, proj, gate, wuv, woe)


def _even_pre_bwd(x, h, proj, dqa, dka, dva, dga, dgb, dqcat, dkcat, dx_res, mod, nw, wie, qn, kn, seg, ca, sa, ct, st,
                  qln, kvln, wuq, wuk):
    S = x.shape[0]
    ts = min(IN_PROJ_ROW_TILE, S)
    nsteps = S // ts

    def body(x_ref, h_ref, proj_ref, dqa_ref, dka_ref, dva_ref, dga_ref, dgb_ref, dqc_ref, dkc_ref, dxr_ref, mod_ref, nw_ref,
             wie_ref, qn_ref, kn_ref, seg_ref, ca_ref, sa_ref, ct_ref, st_ref, qln_ref, kvln_ref, wuq_ref, wuk_ref,
             dx_ref, dwie_out, dwuq_out, dwuk_out, stats_ref, nstats_ref, dwie_ref, dwuq_ref, dwuk_ref, stage, uq_stage, uk_stage):
        @pl.when(pl.program_id(0) == 0)
        def _():
            dwie_ref[...] = jnp.zeros((EVEN_P, D_MODEL), F32)
            dwuq_ref[...] = jnp.zeros((1536, B_Q_LORA), F32)
            dwuk_ref[...] = jnp.zeros((512, 1024), F32)
            stats_ref[...] = jnp.zeros((8, D_MODEL), F32)
            nstats_ref[...] = jnp.zeros((8, 256), F32)

        lane = _lane_iota()
        ca_v, sa_v, ct_v, st_v = ca_ref[...], sa_ref[...], ct_ref[...], st_ref[...]
        seg_v = seg_ref[...]

        def head_norm_bwd(xc, dy, w):
            r = lax.rsqrt(_seg_mean(xc * xc, seg_v) + EPS)
            g = dy * w
            dxc = r * g - xc * (r * r * r) * _seg_mean(xc * g, seg_v)
            return dxc, _sum_rows(dy * (xc * r))

        pieces = []
        dqn = jnp.zeros((1, LANES), F32)
        for cb in range(4):
            sl = slice(LANES * cb, LANES * (cb + 1))
            dy = _rot_bwd(dqa_ref[:, sl] * 0.125, ca_v, sa_v, lane)
            dxc, dw = head_norm_bwd(proj_ref[:, sl], dy, qn_ref[...])
            pieces.append(dxc)
            dqn = dqn + dw
        dxc, dkn = head_norm_bwd(proj_ref[:, 512:640], _rot_bwd(_fold_heads(dka_ref[...], lane), ca_v, sa_v, lane), kn_ref[...])
        pieces += [dxc, _fold_heads(dva_ref[...], lane), dga_ref[...]]
        nstats_ref[0:1, 0:LANES] += dqn + pltpu.roll(dqn, HEAD_DIM, 1)
        nstats_ref[1:2, 0:LANES] += dkn + pltpu.roll(dkn, HEAD_DIM, 1)

        cq = proj_ref[:, 1280:1536]
        rq = _rms(cq)
        cqn_f = cq * rq
        qln_v = qln_ref[...]
        cqn = (cqn_f * qln_v).astype(BF16)
        wuq_v, wuk_v = wuq_ref[...], wuk_ref[...]
        qnope = _dot_nt(cqn, wuq_v[0:512, :]).astype(BF16)
        dqlat = jnp.concatenate([dqc_ref[hh, :, 0:LANES] for hh in range(B_HEADS)], axis=1).astype(BF16)
        dqnope = _dot_nt(dqlat, wuk_v)
        dwuk_ref[...] += _dot_tn(qnope, dqlat)
        dqr = [_rot_bwd(dqc_ref[hh, :, LANES:2 * LANES], ct_v, st_v, lane) for hh in range(B_HEADS)]
        dqb = jnp.concatenate([dqnope] + dqr, axis=1).astype(BF16)
        dcqn = _dot(dqb, wuq_v)
        dwuq_ref[...] += _dot_tn(dqb, cqn)
        nstats_ref[2:3, :] += _sum_rows(dcqn * cqn_f)
        dcq = _rms_bwd(cq, rq, dcqn * qln_v)
        ckv = proj_ref[:, 1536:1664]
        rk = _rms(ckv)
        dckvn = dkc_ref[:, 0:LANES]
        nstats_ref[3:4, 0:LANES] += _sum_rows(dckvn * (ckv * rk))
        dckv = _rms_bwd(ckv, rk, dckvn * kvln_ref[...])
        dkr = _rot_bwd(dkc_ref[:, LANES:2 * LANES], ct_v, st_v, lane)
        pieces += [dcq, dckv, dkr, dgb_ref[...]]
        dproj = jnp.concatenate([piece.astype(BF16) for piece in pieces], axis=1)
        dh = _dot(dproj, wie_ref[...])
        dwie_ref[...] += _dot_tn(dproj, h_ref[...])
        dx_ref[...] = dxr_ref[...] + _norm_mod_bwd(dh, x_ref[...], mod_ref, nw_ref, stats_ref)

        @pl.when(pl.program_id(0) == nsteps - 1)
        def _():
            r_out = 0
            for lo, hi in ((0, EVEN_GAP), (EVEN_GAP + EVEN_P - EVEN_IN, EVEN_P)):
                for r0 in range(lo, hi, stage.shape[0]):
                    n = min(stage.shape[0], hi - r0)
                    stage[0:n, :] = dwie_ref[r0:r0 + n, :].astype(BF16)
                    pltpu.sync_copy(stage.at[0:n], dwie_out.at[pl.ds(r_out, n), :])
                    r_out += n
            per = B_NOPE + B_ROPE
            for hd in range(B_HEADS):
                uq_stage[per * hd:per * hd + B_NOPE, :] = dwuq_ref[B_NOPE * hd:B_NOPE * (hd + 1), :].astype(BF16)
                uq_stage[per * hd + B_NOPE:per * (hd + 1), :] = dwuq_ref[512 + LANES * hd:512 + LANES * hd + B_ROPE, :].astype(BF16)
                uk_stage[B_NOPE * hd:B_NOPE * (hd + 1), :] = dwuk_ref[B_NOPE * hd:B_NOPE * (hd + 1),
                                                                       LANES * hd:LANES * (hd + 1)].astype(BF16)
            pltpu.sync_copy(uq_stage, dwuq_out)
            pltpu.sync_copy(uk_stage, dwuk_out)

    return pl.pallas_call(
        body, name="even_pre_bwd", grid=(nsteps,),
        in_specs=[_row_spec(ts, D_MODEL), _row_spec(ts, D_MODEL), _row_spec(ts, EVEN_P), _row_spec(ts, 512), _row_spec(ts, 2 * LANES),
                  _row_spec(ts, 2 * LANES), _row_spec(ts, 512), _row_spec(ts, 512),
                  pl.BlockSpec((B_HEADS, ts, 2 * LANES), lambda i: (0, i, 0)), _row_spec(ts, 2 * LANES), _row_spec(ts, D_MODEL),
                  _full_spec((3, D_MODEL)), _full_spec((1, D_MODEL)), _full_spec((EVEN_P, D_MODEL)),
                  _full_spec((1, LANES)), _full_spec((1, LANES)), _full_spec((LANES, LANES)),
                  _row_spec(ts, LANES), _row_spec(ts, LANES), _row_spec(ts, LANES), _row_spec(ts, LANES),
                  _full_spec((1, B_Q_LORA)), _full_spec((1, B_KV_LORA)), _full_spec((1536, B_Q_LORA)), _full_spec((512, 1024))],
        out_specs=[_row_spec(ts, D_MODEL), _ANY, _ANY, _ANY, _full_spec((8, D_MODEL), single=False), _full_spec((8, 256), single=False)],
        out_shape=[_sds((S, D_MODEL), F32), _sds((EVEN_IN, D_MODEL), BF16), _sds((B_HEADS * (B_NOPE + B_ROPE), B_Q_LORA), BF16),
                   _sds((B_HEADS * B_NOPE, LANES), BF16), _sds((8, D_MODEL), F32), _sds((8, 256), F32)],
        scratch_shapes=[pltpu.VMEM((EVEN_P, D_MODEL), F32), pltpu.VMEM((1536, B_Q_LORA), F32), pltpu.VMEM((512, 1024), F32),
                        pltpu.VMEM((256, D_MODEL), BF16), pltpu.VMEM((B_HEADS * (B_NOPE + B_ROPE), B_Q_LORA), BF16),
                        pltpu.VMEM((B_HEADS * B_NOPE, LANES), BF16)],
        compiler_params=_params(("arbitrary",)),
    )(x, h, proj, dqa, dka, dva, dga, dgb, dqcat, dkcat, dx_res, mod, nw, wie, qn, kn, seg, ca, sa, ct, st, qln, kvln, wuq, wuk)


def _ada_fwd(c_all, w, b):
    n = w.shape[2]

    def body(c_ref, w_ref, b_ref, o_ref):
        cv = c_ref[...]
        o_ref[0] = _dot_f32(cv * _sigmoid(cv), w_ref[0]) + b_ref[0]

    return pl.pallas_call(
        body, name="ada_fwd", grid=(2,),
        in_specs=[pl.BlockSpec((N_DEV, D_MODEL), lambda l: (0, 0)), pl.BlockSpec((1, D_MODEL, n), lambda l: (l, 0, 0)),
                  pl.BlockSpec((1, 1, n), lambda l: (l, 0, 0))],
        out_specs=pl.BlockSpec((1, N_DEV, n), lambda l: (l, 0, 0)),
        out_shape=_sds((2, N_DEV, n), F32),
        compiler_params=_params(("arbitrary",)),
    )(c_all, w, b)


def _ada_bwd(c_all_t, dmod):
    n = dmod.shape[2]

    def body(c_ref, d_ref, o_ref):
        cv = c_ref[...]
        act = cv * _sigmoid(cv)
        dv = d_ref[0]
        acc = act[:, 0:1] * dv[0:1, :]
        for bb in range(1, N_DEV):
            acc = acc + act[:, bb:bb + 1] * dv[bb:bb + 1, :]
        o_ref[0] = acc

    return pl.pallas_call(
        body, name="ada_bwd", grid=(2,),
        in_specs=[pl.BlockSpec((D_MODEL, N_DEV), lambda l: (0, 0)), pl.BlockSpec((1, N_DEV, n), lambda l: (l, 0, 0))],
        out_specs=pl.BlockSpec((1, D_MODEL, n), lambda l: (l, 0, 0)),
        out_shape=_sds((2, D_MODEL, n), F32),
        compiler_params=_params(("arbitrary",)),
    )(c_all_t, dmod)


ADAM_ROW_TILE = 512


def _adam_update(g, w, m, v):
    m_new = ADAM_B1 * m + (1.0 - ADAM_B1) * g
    v_new = ADAM_B2 * v + (1.0 - ADAM_B2) * jnp.square(g)
    m_hat = m_new / (1.0 - ADAM_B1 ** ADAM_STEP)
    v_hat = v_new / (1.0 - ADAM_B2 ** ADAM_STEP)
    return -ADAM_LR * (m_hat / (jnp.sqrt(v_hat) + ADAM_EPS) + ADAM_WD * w), m_new, v_new


SMALL_ROWS = dict(dmod=(0, D_MODEL), norm_w=(6, D_MODEL), final_norm=(8, D_MODEL), a_q_norm=(9, HEAD_DIM), a_k_norm=(10, HEAD_DIM),
                  b_q_lora_norm=(11, B_Q_LORA), b_kv_lora_norm=(12, B_KV_LORA), c_sink=(13, C_HEADS))
SMALL_WEIGHTS = ("ada_b", "norm_w", "final_norm", "a_q_norm", "a_k_norm", "b_q_lora_norm", "b_kv_lora_norm", "c_sink")
LOSS_ROW = 14


def _pack_small(res):
    def padded(v):
        return jnp.concatenate([v, jnp.zeros((v.shape[0], D_MODEL - v.shape[1]), F32)], axis=1)

    rows = [res["dmod"].reshape(6, D_MODEL), res["norm_w"], res["final_norm"].reshape(1, D_MODEL)]
    rows += [padded(res[k]) for k in ("a_q_norm", "a_k_norm", "b_q_lora_norm", "b_kv_lora_norm", "c_sink")]
    return jnp.concatenate(rows + [res["loss_row"], jnp.zeros((1, D_MODEL), F32)], axis=0)


def _adam_small(parts, ws, ms, vs):
    nw = len(SMALL_WEIGHTS)

    def body(*refs):
        p_ref = refs[0]
        w_refs, m_refs, v_refs = refs[1:1 + nw], refs[1 + nw:1 + 2 * nw], refs[1 + 2 * nw:1 + 3 * nw]
        outs = refs[1 + 3 * nw:]
        g_all = p_ref[0]
        for k in range(1, N_DEV):
            g_all = g_all + p_ref[k]
        for idx, name in enumerate(SMALL_WEIGHTS):
            if name == "ada_b":
                g = jnp.concatenate([jnp.concatenate([g_all[3 * l + t:3 * l + t + 1] for t in range(3)], axis=1) for l in range(2)],
                                    axis=0)
            else:
                row, width = SMALL_ROWS[name]
                g = g_all[row:row + w_refs[idx].shape[0], 0:width]
            d, m_new, v_new = _adam_update(g, w_refs[idx][...], m_refs[idx][...], v_refs[idx][...])
            outs[4 * idx][...], outs[4 * idx + 1][...], outs[4 * idx + 2][...], outs[4 * idx + 3][...] = g, d, m_new, v_new
        outs[4 * nw][...] = g_all[LOSS_ROW:LOSS_ROW + 1, 0:LANES]

    out_shape = []
    for w in ws:
        out_shape += [_sds(w.shape, F32)] * 4
    out_shape.append(_sds((1, LANES), F32))
    return pl.pallas_call(body, name="adam_small", out_shape=out_shape,
                          compiler_params=pltpu.CompilerParams(vmem_limit_bytes=VMEM_LIMIT))(parts, *ws, *ms, *vs)


def _adam(parts, w, m, v, name, by_columns=False):
    P, R, C = parts.shape
    if by_columns:
        tr, tc = R, 256
    else:
        tr, tc = (R if R <= ADAM_ROW_TILE else ADAM_ROW_TILE), C
    assert R % tr == 0 and C % tc == 0

    def body(p_ref, w_ref, m_ref, v_ref, g_ref, d_ref, nm_ref, nv_ref):
        g = p_ref[0].astype(F32)
        for k in range(1, P):
            g = g + p_ref[k].astype(F32)
        g_ref[...] = g
        d_ref[...], nm_ref[...], nv_ref[...] = _adam_update(g, w_ref[...], m_ref[...], v_ref[...])

    tile = (lambda i: (0, i)) if by_columns else (lambda i: (i, 0))
    spec = pl.BlockSpec((tr, tc), tile)
    return pl.pallas_call(
        body, name=name, grid=(C // tc if by_columns else R // tr,),
        in_specs=[pl.BlockSpec((P, tr, tc), lambda i: (0,) + tile(i)), spec, spec, spec],
        out_specs=[spec, spec, spec, spec], out_shape=[_sds((R, C), F32)] * 4,
        compiler_params=_params(("arbitrary",)),
    )(parts, w, m, v)


_ANY = pl.BlockSpec(memory_space=pl.ANY)
CHIP_FLIPS = ((1, 0), (0, 1), (1, 1))
DEV_FLIPS = tuple((dx, dy, dc) for dx in (0, 1) for dy in (0, 1) for dc in (0, 1) if dx + dy + dc)


def _flip(a, d):
    return a if d == 0 else 1 - a


def _my_place():
    return lax.axis_index("x"), lax.axis_index("y"), lax.axis_index("c")


def _gather8_copies(ins, outs, send_sems, recv_sems, loc_sems):
    x, y, c = _my_place()
    me = 4 * x + 2 * y + c
    copies = []
    for a in range(len(ins)):
        copies.append(pltpu.make_async_copy(ins[a], outs[a].at[me], loc_sems.at[a]))
        for k, (dx, dy, dc) in enumerate(DEV_FLIPS):
            copies.append(pltpu.make_async_remote_copy(
                src_ref=ins[a], dst_ref=outs[a].at[me], send_sem=send_sems.at[a, k], recv_sem=recv_sems.at[a, k],
                device_id=(_flip(x, dx), _flip(y, dy), _flip(c, dc)), device_id_type=MESH_ID))
    return copies


def _gather8_sems(n):
    return [pltpu.SemaphoreType.DMA((n, 7)), pltpu.SemaphoreType.DMA((n, 7)), pltpu.SemaphoreType.DMA((n,))]


def _gather_dev8(arrs, name):
    n = len(arrs)

    def body(*refs):
        copies = _gather8_copies(refs[:n], refs[n:2 * n], *refs[2 * n:])
        for cp in copies:
            cp.start()
        for cp in copies:
            cp.wait()

    return pl.pallas_call(
        body, name=name, in_specs=[_ANY] * n, out_specs=[_ANY] * n,
        out_shape=[_sds((N_DEV,) + a.shape, a.dtype) for a in arrs], scratch_shapes=_gather8_sems(n),
    )(*arrs)


class _Exchange:
    def __init__(self, arrs, out_shapes, n_sems, phases):
        self.arrs, self.out_shapes, self.n_sems, self._phases = list(arrs), list(out_shapes), n_sems, phases

    @property
    def n(self):
        return len(self.arrs)

    def sem_shapes(self):
        return [pltpu.SemaphoreType.DMA((self.n, self.n_sems)), pltpu.SemaphoreType.DMA((self.n, self.n_sems)),
                pltpu.SemaphoreType.DMA((self.n,))]

    def phases(self, ins, outs, sems):
        return self._phases(ins, outs, *sems)

    def run(self, name):
        n = self.n

        def body(*refs):
            start, mid, end = self.phases(refs[:n], refs[n:2 * n], refs[2 * n:])
            start()
            mid()
            end()

        return pl.pallas_call(body, name=name, in_specs=[_ANY] * n, out_specs=[_ANY] * n, out_shape=self.out_shapes,
                              scratch_shapes=self.sem_shapes())(*self.arrs)

def _gather_halves_phases(ins, outs, send_sems, recv_sems, loc_sems):
    n = len(ins)
    x, y, c = _my_place()
    chip = 2 * x + y
    sibling = (x, y, 1 - c)
    peers = [(_flip(x, dx), _flip(y, dy)) for dx, dy in CHIP_FLIPS]

    def remote(src, p, half, a, k, to):
        return pltpu.make_async_remote_copy(src_ref=src, dst_ref=outs[a].at[p, half], send_sem=send_sems.at[a, k],
                                            recv_sem=recv_sems.at[a, k], device_id=to, device_id_type=MESH_ID)

    def local(a):
        return pltpu.make_async_copy(ins[a], outs[a].at[chip], loc_sems.at[a])

    def first(a, k):
        return remote(ins[a].at[c], chip, c, a, k, (*peers[k], c))

    def passed(a, k):
        p = 2 * peers[k][0] + peers[k][1]
        return remote(outs[a].at[p, c], p, c, a, 3 + k, sibling)

    def start():
        for a in range(n):
            local(a).start()
            for k in range(3):
                first(a, k).start()

    def mid():
        for a in range(n):
            for k in range(3):
                p = 2 * peers[k][0] + peers[k][1]
                remote(outs[a].at[p, c], p, c, a, k, sibling).wait_recv()
                passed(a, k).start()

    def end():
        for a in range(n):
            for k in range(3):
                p = 2 * peers[k][0] + peers[k][1]
                remote(outs[a].at[p, 1 - c], p, 1 - c, a, 3 + k, sibling).wait_recv()
        for a in range(n):
            for k in range(3):
                first(a, k).wait_send()
                passed(a, k).wait_send()
            local(a).wait()

    return start, mid, end


def _gather_chip4_halves(arrs):
    return _Exchange(arrs, [_sds((N_CHIPS,) + a.shape, a.dtype) for a in arrs], 6, _gather_halves_phases)


def _reduce_phases(n_whole, ins, outs, send_sems, recv_sems, loc_sems):
    n = len(ins)
    x, y, c = _my_place()
    chip = 2 * x + y
    sibling = (x, y, 1 - c)
    peers = [(_flip(x, dx), _flip(y, dy)) for dx, dy in CHIP_FLIPS]

    def remote(src, slot, a, k, to):
        return pltpu.make_async_remote_copy(src_ref=src, dst_ref=outs[a].at[slot], send_sem=send_sems.at[a, k],
                                            recv_sem=recv_sems.at[a, k], device_id=to, device_id_type=MESH_ID)

    def block(a, p):
        return ins[a] if a >= n - n_whole else ins[a].at[p]

    def local(a):
        return pltpu.make_async_copy(block(a, chip), outs[a].at[2 * chip + c], loc_sems.at[a])

    def own(a):
        return remote(block(a, chip), 2 * chip + c, a, 0, sibling)

    def first(a, k):
        return remote(block(a, 2 * peers[k][0] + peers[k][1]), 2 * chip + c, a, 1 + k, (*peers[k], c))

    def passed(a, k):
        slot = 2 * (2 * peers[k][0] + peers[k][1]) + c
        return remote(outs[a].at[slot], slot, a, 4 + k, sibling)

    def start():
        for a in range(n):
            local(a).start()
            own(a).start()
            for k in range(3):
                first(a, k).start()

    def mid():
        for a in range(n):
            for k in range(3):
                slot = 2 * (2 * peers[k][0] + peers[k][1]) + c
                remote(outs[a].at[slot], slot, a, 1 + k, sibling).wait_recv()
                passed(a, k).start()

    def end():
        for a in range(n):
            remote(outs[a].at[2 * chip + 1 - c], 2 * chip + 1 - c, a, 0, sibling).wait_recv()
            for k in range(3):
                slot = 2 * (2 * peers[k][0] + peers[k][1]) + 1 - c
                remote(outs[a].at[slot], slot, a, 4 + k, sibling).wait_recv()
        for a in range(n):
            own(a).wait_send()
            for k in range(3):
                first(a, k).wait_send()
                passed(a, k).wait_send()
            local(a).wait()

    return start, mid, end


def _reduce_exchange(arrs, whole=()):
    shapes = [_sds((N_DEV,) + a.shape[1:], a.dtype) for a in arrs] + [_sds((N_DEV,) + a.shape, a.dtype) for a in whole]
    return _Exchange(list(arrs) + list(whole), shapes, 7, functools.partial(_reduce_phases, len(whole)))


def _shard_halves_t(w):
    wt = w.T.astype(BF16)
    n2 = wt.shape[0] // 2
    pad = jnp.zeros((-n2 % 16, wt.shape[1]), BF16)
    return jnp.stack([jnp.concatenate([wt[0:n2], pad], axis=0), jnp.concatenate([wt[n2:], pad], axis=0)])


def _gathered_rows(g, n):
    return [g[p, half, 0:n // 2] for p in range(N_CHIPS) for half in range(2)]


def _even_in_layout_t(g):
    n2 = EVEN_IN // N_CHIPS // 2
    gap, gap_rows = EVEN_GAP, EVEN_P - EVEN_IN
    assert n2 % 2 == 0 and gap % 2 == 0 and gap_rows % 2 == 0
    spans = []
    for p in range(N_CHIPS):
        for half in range(2):
            lo = (2 * p + half) * n2
            if lo < gap < lo + n2:
                spans += [(p, half, 0, (gap - lo) // 2, lo // 2), (p, half, (gap - lo) // 2, (lo + n2 - gap) // 2, (gap + gap_rows) // 2)]
            else:
                spans.append((p, half, 0, n2 // 2, (lo + (gap_rows if lo >= gap else 0)) // 2))

    def body(g_ref, o_ref, pairs):
        pairs[pl.ds(gap // 2, gap_rows // 2), :] = jnp.zeros((gap_rows // 2, pairs.shape[1]), jnp.uint32)
        for p, half, src, rows, dst in spans:
            pairs[pl.ds(dst, rows), :] = pltpu.bitcast(g_ref[p, half], jnp.uint32)[src:src + rows]
        step = 128
        for r0 in range(0, EVEN_P // 2, step):
            o_ref[pl.ds(2 * r0, 2 * step), :] = pltpu.bitcast(pairs[pl.ds(r0, step), :], o_ref.dtype)

    return pl.pallas_call(body, name="even_w_in_rows", out_shape=_sds((EVEN_P, g.shape[-1]), g.dtype),
                          scratch_shapes=[pltpu.VMEM((EVEN_P // 2, g.shape[-1]), jnp.uint32)],
                          compiler_params=pltpu.CompilerParams(vmem_limit_bytes=VMEM_LIMIT))(g)


def _uq_layout_t(g):
    wt = jnp.concatenate(_gathered_rows(g, B_HEADS * (B_NOPE + B_ROPE) // N_CHIPS), axis=0)
    per = B_NOPE + B_ROPE
    pad = jnp.zeros((LANES - B_ROPE, wt.shape[1]), wt.dtype)
    nope = [wt[per * h:per * h + B_NOPE] for h in range(B_HEADS)]
    rope = [jnp.concatenate([wt[per * h + B_NOPE:per * (h + 1)], pad], axis=0) for h in range(B_HEADS)]
    return jnp.concatenate(nope + rope, axis=0)


def _block_diag(blocks):
    rows = []
    for h, blk in enumerate(blocks):
        r, cdim = blk.shape
        n = len(blocks)
        rows.append(jnp.concatenate([jnp.zeros((r, cdim * h), blk.dtype), blk, jnp.zeros((r, cdim * (n - 1 - h)), blk.dtype)],
                                    axis=1))
    return jnp.concatenate(rows, axis=0)


def _uk_layout(w):
    return _block_diag([w[:, h, :].T for h in range(B_HEADS)])


def _latent_rows(w):
    return jnp.transpose(w[0], (1, 2, 0)).reshape(-1, w.shape[1])


def _latent_unrows(w2, shape):
    return jnp.transpose(w2.reshape(shape[2], shape[3], shape[1]), (2, 0, 1)).reshape(shape)


def _uv_layout(w):
    return _block_diag([w[:, h, :] for h in range(B_HEADS)])


def _uv_unlayout(g):
    return jnp.concatenate([g[LANES * h:LANES * (h + 1), B_V * h:B_V * (h + 1)].T for h in range(B_HEADS)], axis=0)


def _rope_tables(S):
    inv = ROPE_THETA ** (-jnp.arange(0, 32, 2, dtype=F32) / 32)
    tok = jnp.arange(S)

    def tab(pos):
        ang = pos.astype(F32)[:, None] * inv[None, :]
        cos, sin = jnp.cos(ang), jnp.sin(ang)
        return jnp.concatenate([cos, cos], axis=1), jnp.concatenate([-sin, sin], axis=1)

    cr, sr = tab(tok // GRID_W)
    cc, sc = tab(tok % GRID_W)
    ct, st = tab(tok)
    return (jnp.tile(jnp.concatenate([cr, cc], axis=1), (1, 2)), jnp.tile(jnp.concatenate([sr, sc], axis=1), (1, 2)),
            jnp.tile(ct, (1, 4)), jnp.tile(st, (1, 4)))


A_TQ, A_TK, A_SUB = 512, 4096, 512
A_FWD_SUB = 1024
B_TQ, B_TK, B_SUB = 128, 4096, 1024
B_BWD_TK, B_BWD_SUB = 4096, 512
C_T = 256
C_BLOCKS_PER_STEP = 8
KV_SHARE = 2


def _local_step(x0, tgt, mod, norm_w, wie, wuq, wuk, wuv, late_shards, a_q_norm, a_k_norm, q_lora_norm, kv_lora_norm,
                c_sink, final_norm):
    S = x0.shape[0]
    mod3 = mod.reshape(2, 3, D_MODEL)
    ca, sa, ct, st = _rope_tables(S)
    lane_seg = np.arange(LANES) // HEAD_DIM
    seg = jnp.asarray((lane_seg[:, None] == lane_seg[None, :]).astype(np.float32)).astype(BF16)
    qn = jnp.tile(a_q_norm.reshape(1, HEAD_DIM), (1, 2))
    kn = jnp.tile(a_k_norm.reshape(1, HEAD_DIM), (1, 2))
    qln, kvln = q_lora_norm.reshape(1, B_Q_LORA), kv_lora_norm.reshape(1, B_KV_LORA)
    nw0, nw1 = norm_w[0:1], norm_w[1:2]
    gate0, gate1 = mod3[0, 2:3], mod3[1, 2:3]
    a_tq, a_tk, b_tq, b_tk, bb_tk, c_t = min(A_TQ, S), min(A_TK, S), min(B_TQ, S), min(B_TK, S), min(B_BWD_TK, S), min(C_T, S)
    a_sub, b_sub, bb_sub = min(A_SUB, a_tk), min(B_SUB, b_tk), min(B_BWD_SUB, bb_tk)

    h0, proj_e, qa, ka, va, qcat, kcat, ka_t, va_t, kcat_t = _even_pre_fwd(x0, mod3[0], nw0, wie, qn, kn, seg, ca, sa, ct, st,
                                                                           qln, kvln, wuq, wuk)
    oa, lse_a, woe_g, wio_g, woo_g = _pp_fwd(qa, ka, va_t, kdiv=KV_SHARE, tq=a_tq, tk=a_tk, sub=min(A_FWD_SUB, a_tk), name="attn_a_fwd",
                                             side=_gather_chip4_halves(late_shards))
    woe = woe_g.reshape(D_MODEL, D_MODEL)
    wio = wio_g.reshape(N_CHIPS, D_MODEL, ODD_IN // N_CHIPS)
    woo = woo_g.reshape(D_MODEL, D_MODEL)
    olat, lse_b = _mla_fwd(qcat, kcat, kcat_t, tq=b_tq, tk=b_tk, sub=b_sub)
    y0, x1 = _even_post_fwd(oa, olat, proj_e, x0, gate0, wuv, woe)
    h1, gc, qc, kc, vc, kc_t, vc_t = _odd_pre_fwd(x1, mod3[1], nw1, wio)
    slopes = 2.0 ** (-8.0 * jnp.arange(1, C_HEADS + 1, dtype=F32) / C_HEADS)
    slope_rows = jnp.repeat(slopes.reshape(C_HEADS // 2, 2), c_t, axis=1)[:, None, :]
    sink_rows = jnp.repeat(c_sink.reshape(C_HEADS // 2, 2), c_t, axis=1)[:, None, :]
    win_dist = _win_dist_table(S, c_t)
    oc, lse_c = _win_fwd(qc, kc, vc_t, win_dist, slope_rows, sink_rows, kdiv=KV_SHARE, tq=c_t, nbs=C_BLOCKS_PER_STEP,
                         name="attn_c_fwd")
    doc, dgc, dx2, dwoo, st_f = _odd_post(oc, gc, x1, gate1, woo, final_norm.reshape(1, D_MODEL), tgt)
    dqc, dkc, dvc, dsink_raw = _win_bwd(qc, kc, kc_t, vc, oc, doc, lse_c, win_dist, slope_rows, sink_rows, kdiv=KV_SHARE, tq=c_t,
                                        nbs=C_BLOCKS_PER_STEP, name="attn_c_bwd")
    dx1, dwio, st_1 = _odd_pre_bwd(dqc, dkc, dvc, dgc, h1, x1, dx2, mod3[1], nw1, wio)
    doa, dga, dgb, dolat, dwoe, dwuv, st_e = _even_post_bwd(dx1, y0, oa, olat, proj_e, gate0, wuv, woe)
    late_grads = _reduce_exchange([dwoe.reshape(N_CHIPS, D_MODEL // N_CHIPS, D_MODEL), dwio,
                                   dwoo.reshape(N_CHIPS, D_MODEL // N_CHIPS, D_MODEL)])
    dqa, dka, dva, p_woe, p_wio, p_woo = _pp_bwd(qa, ka, ka_t, va, oa, doa, lse_a, kdiv=KV_SHARE, tq=a_tq, tk=a_tk, sub=a_sub,
                                                 name="attn_a_bwd", side=late_grads)
    dqcat, dkcat = _mla_bwd(qcat, kcat, kcat_t, olat, dolat, lse_b, tq=b_tq, tk=bb_tk, sub=bb_sub)
    dx0, dwie, dwuq, dwuk, st_0, nst = _even_pre_bwd(x0, h0, proj_e, dqa, dka, dva, dga, dgb, dqcat, dkcat, dx1, mod3[0], nw0,
                                                     wie, qn, kn, seg, ca, sa, ct, st, qln, kvln, wuq, wuk)
    dsink_pairs = jnp.stack([dsink_raw[:, 0, 0], dsink_raw[:, 1, 0]], axis=1).reshape(C_HEADS)
    return dict(
        loss_row=st_f[2:3], dx=dx0,
        dmod=jnp.stack([jnp.concatenate([st_0[0], st_0[1], st_e[0]]), jnp.concatenate([st_1[0], st_1[1], st_f[1]])]),
        norm_w=jnp.stack([st_0[2], st_1[2]]), final_norm=st_f[0],
        a_q_norm=nst[0:1, 0:HEAD_DIM], a_k_norm=nst[1:2, 0:HEAD_DIM], b_q_lora_norm=nst[2:3, :], b_kv_lora_norm=nst[3:4, 0:LANES],
        c_sink=dsink_pairs.reshape(1, C_HEADS),
        even_w_in=dwie, b_w_uq=dwuq, b_w_uk=dwuk, b_w_uv=dwuv, even_w_out=p_woe, odd_w_in=p_wio, odd_w_out=p_woo)


WEIGHT_NAMES = ("norm_w", "ada_w", "ada_b", "even_w_in", "a_q_norm", "a_k_norm", "b_q_lora_norm", "b_kv_lora_norm", "b_w_uq",
                "b_w_uk", "b_w_uv", "even_w_out", "odd_w_in", "c_sink", "odd_w_out", "final_norm")


def kernel(x, c, norm_w, ada_w, ada_b, even_w_in, a_q_norm, a_k_norm, b_q_lora_norm, b_kv_lora_norm, b_w_uq, b_w_uk, b_w_uv, even_w_out, odd_w_in, c_sink, odd_w_out, final_norm, loss_target, m_norm_w, m_ada_w, m_ada_b, m_even_w_in, m_a_q_norm, m_a_k_norm, m_b_q_lora_norm, m_b_kv_lora_norm, m_b_w_uq, m_b_w_uk, m_b_w_uv, m_even_w_out, m_odd_w_in, m_c_sink, m_odd_w_out, m_final_norm, v_norm_w, v_ada_w, v_ada_b, v_even_w_in, v_a_q_norm, v_a_k_norm, v_b_q_lora_norm, v_b_kv_lora_norm, v_b_w_uq, v_b_w_uk, v_b_w_uv, v_even_w_out, v_odd_w_in, v_c_sink, v_odd_w_out, v_final_norm):
    given = dict(locals())
    xi, yi, ci = _my_place()
    chip = 2 * xi + yi
    dev = 2 * chip + ci
    n_ada = ada_w.shape[2]

    (c_all,) = _gather_dev8([c], "gather_c")
    c_all = c_all.reshape(N_DEV, D_MODEL)
    bias = lax.dynamic_slice_in_dim(ada_b, chip * n_ada, n_ada, axis=1).reshape(2, 1, n_ada)
    mod_cols = _ada_fwd(c_all, ada_w, bias)
    def halves(w):
        return w.astype(BF16).reshape((2, w.shape[0] // 2) + w.shape[1:])

    mod_all, wie_g, wuq_g = _gather_chip4_halves(
        [mod_cols, _shard_halves_t(even_w_in[0]), _shard_halves_t(b_w_uq[0])]).run("gather_weights")
    mod = jnp.transpose(lax.dynamic_index_in_dim(mod_all, dev, axis=2, keepdims=False), (1, 0, 2)).reshape(2, 3 * D_MODEL)

    res = _local_step(
        x[0], loss_target[0], mod, norm_w,
        _even_in_layout_t(wie_g), _uq_layout_t(wuq_g), _uk_layout(b_w_uk[0].astype(BF16)),
        _uv_layout(b_w_uv[0].astype(BF16)), [halves(even_w_out[0]), halves(odd_w_in[0]), halves(odd_w_out[0])],
        a_q_norm, a_k_norm, b_q_lora_norm, b_kv_lora_norm, c_sink, final_norm)

    p_wie, p_wuq, small_all, p_wuk, p_wuv = _reduce_exchange(
        [res["even_w_in"].reshape(N_CHIPS, EVEN_IN // N_CHIPS, D_MODEL),
         res["b_w_uq"].reshape(N_CHIPS, -1, B_Q_LORA)],
        whole=[_pack_small(res), res["b_w_uk"], _uv_unlayout(res["b_w_uv"]).astype(BF16)],
    ).run("reduce_exchange")
    shard_parts = dict(even_w_in=p_wie, b_w_uq=p_wuq, **{k: res[k] for k in ("even_w_out", "odd_w_in", "odd_w_out")})
    dmod_all = small_all[:, 0:6, :].reshape(N_DEV, 2, 3 * D_MODEL)
    dmod_cols = jnp.transpose(lax.dynamic_slice_in_dim(dmod_all, chip * n_ada, n_ada, axis=2), (1, 0, 2))
    parts = dict(shard_parts)
    parts["ada_w"] = _ada_bwd(c_all.T, dmod_cols).reshape(1, 2 * D_MODEL, n_ada)
    parts["b_w_uk"], parts["b_w_uv"] = p_wuk, p_wuv

    def as2d(a):
        return a.reshape((-1, a.shape[-1]) if a.ndim > 1 else (1, a.shape[0]))

    results = {}
    small_outs = _adam_small(small_all, *[[as2d(given[pre + k]) for k in SMALL_WEIGHTS] for pre in ("", "m_", "v_")])
    for idx, k in enumerate(SMALL_WEIGHTS):
        results[k] = small_outs[4 * idx:4 * idx + 4]
    for k, p in parts.items():
        if k in ("even_w_in", "b_w_uq"):
            outs = _adam(p, given[k][0].T, given["m_" + k][0].T, given["v_" + k][0].T, "adam_" + k, by_columns=k == "even_w_in")
            results[k] = [o.T for o in outs]
            continue
        if k in ("b_w_uk", "b_w_uv"):
            outs = _adam(p, _latent_rows(given[k]), _latent_rows(given["m_" + k]), _latent_rows(given["v_" + k]), "adam_" + k)
            results[k] = [_latent_unrows(o, given[k].shape) for o in outs]
            continue
        shape2 = (p.shape[-2], p.shape[-1])
        results[k] = _adam(p, given[k].reshape(shape2), given["m_" + k].reshape(shape2), given["v_" + k].reshape(shape2),
                           "adam_" + k)
    by_kind = [[results[k][t].reshape(given[k].shape) for k in WEIGHT_NAMES] for t in range(4)]
    return (small_outs[-1][0, 0], res["dx"][None], *by_kind[0], *by_kind[1], *by_kind[2], *by_kind[3])
```

```python
import functools

import numpy as np
import jax
import jax.numpy as jnp
from jax import lax
from jax.experimental import pallas as pl
from jax.experimental.pallas import tpu as pltpu

F32 = jnp.float32
BF16 = jnp.bfloat16
HIGHEST = lax.Precision.HIGHEST
MESH_ID = pl.DeviceIdType.MESH

D_MODEL = 1024
HEAD_DIM = 64
GRID_W = 64
EPS = 1e-6
ROPE_THETA = 10000.0
B_HEADS, B_NOPE, B_ROPE, B_V = 8, 64, 32, 64
B_Q_LORA, B_KV_LORA = 256, 128
C_HEADS = 16
WINDOW = 128
EVEN_IN, ODD_IN = 2208, 2560
EVEN_P = 2304
EVEN_GAP = 1696
N_CHIPS, N_DEV = 4, 8
LANES = 128
NEG = -1e30
VMEM_LIMIT = 60 * 1024 * 1024

ADAM_LR, ADAM_B1, ADAM_B2, ADAM_EPS, ADAM_WD, ADAM_STEP = 0.001, 0.9, 0.999, 1e-08, 0.01, 10

ROW_TILE = 512
IN_PROJ_ROW_TILE = 256


def _dot(a, b):
    return lax.dot_general(a, b, (((1,), (0,)), ((), ())), preferred_element_type=F32)


def _dot_nt(a, b):
    return lax.dot_general(a, b, (((1,), (1,)), ((), ())), preferred_element_type=F32)


def _dot_tn(a, b):
    return lax.dot_general(a, b, (((0,), (0,)), ((), ())), preferred_element_type=F32)


def _dot_f32(a, b):
    return lax.dot_general(a, b, (((1,), (0,)), ((), ())), precision=HIGHEST, preferred_element_type=F32)


def _sigmoid(x):
    return 1.0 / (1.0 + jnp.exp(-x))


def _silu_and_grad(g):
    s = _sigmoid(g)
    return g * s, s * (1.0 + g * (1.0 - s))


def _lane_iota():
    return lax.broadcasted_iota(jnp.int32, (1, LANES), 1)


def _partner(x, lane):
    return jnp.where((lane % 32) < 16, pltpu.roll(x, LANES - 16, 1), pltpu.roll(x, 16, 1))


def _rot(x, cos, sin_signed, lane):
    return x * cos + _partner(x, lane) * sin_signed


def _rot_bwd(dy, cos, sin_signed, lane):
    return dy * cos + _partner(dy * sin_signed, lane)


def _rms(x):
    return lax.rsqrt(jnp.mean(x * x, axis=-1, keepdims=True) + EPS)


def _rms_bwd(x, r, g):
    return r * g - x * (r * r * r) * jnp.mean(x * g, axis=-1, keepdims=True)


def _seg_mean(v, seg_ones):
    hi = v.astype(BF16)
    lo = (v - hi.astype(F32)).astype(BF16)
    return (_dot(hi, seg_ones) + _dot(lo, seg_ones)) * (1.0 / HEAD_DIM)


def _dup_heads(x, lane):
    swapped = pltpu.roll(x, HEAD_DIM, 1)
    lo = lane < HEAD_DIM
    return jnp.concatenate([jnp.where(lo, x, swapped), jnp.where(lo, swapped, x)], axis=1)


def _fold_heads(x2, lane):
    a, b = x2[:, 0:LANES], x2[:, LANES:2 * LANES]
    return jnp.where(lane < HEAD_DIM, a + pltpu.roll(a, HEAD_DIM, 1), b + pltpu.roll(b, HEAD_DIM, 1))


def _row_spec(ts, cols):
    return pl.BlockSpec((ts, cols), lambda i: (i, 0))


def _full_spec(shape, single=True):
    nd = len(shape)
    if single:
        return pl.BlockSpec(shape, lambda i: (0,) * nd, pipeline_mode=pl.Buffered(1))
    return pl.BlockSpec(shape, lambda i: (0,) * nd)


def _sds(shape, dtype):
    return jax.ShapeDtypeStruct(shape, dtype)


def _params(sem):
    return pltpu.CompilerParams(dimension_semantics=sem, vmem_limit_bytes=VMEM_LIMIT)


def _even_pre_fwd(x, mod, nw, wie, qn, kn, seg, ca, sa, ct, st, qln, kvln, wuq, wuk):
    S = x.shape[0]
    ts = min(IN_PROJ_ROW_TILE, S)

    def body(x_ref, mod_ref, nw_ref, wie_ref, qn_ref, kn_ref, seg_ref, ca_ref, sa_ref, ct_ref, st_ref, qln_ref,
             kvln_ref, wuq_ref, wuk_ref, h_ref, proj_ref, qa_ref, ka_ref, va_ref, qcat_ref, kcat_ref, kat_ref, vat_ref, kcatt_ref):
        xv = x_ref[...]
        h = (xv * _rms(xv) * nw_ref[...]) * (1.0 + mod_ref[1:2, :]) + mod_ref[0:1, :]
        hb = h.astype(BF16)
        h_ref[...] = hb
        proj = _dot_nt(hb, wie_ref[...])
        proj_ref[...] = proj
        lane = _lane_iota()
        ca_v, sa_v, ct_v, st_v = ca_ref[...], sa_ref[...], ct_ref[...], st_ref[...]
        seg_v = seg_ref[...]
        for cb in range(4):
            xc = proj[:, LANES * cb:LANES * (cb + 1)]
            r = lax.rsqrt(_seg_mean(xc * xc, seg_v) + EPS)
            y = _rot(xc * r * qn_ref[...], ca_v, sa_v, lane)
            qa_ref[:, LANES * cb:LANES * (cb + 1)] = (y * 0.125).astype(BF16)
        kc = proj[:, 512:640]
        r = lax.rsqrt(_seg_mean(kc * kc, seg_v) + EPS)
        ka_v = _dup_heads(_rot(kc * r * kn_ref[...], ca_v, sa_v, lane), lane)
        ka_ref[...] = ka_v.astype(BF16)
        kat_ref[...] = ka_v.T.astype(BF16)
        va_v = _dup_heads(proj[:, 640:768], lane)
        va_ref[...] = va_v.astype(BF16)
        vat_ref[...] = va_v.T.astype(BF16)
        cq = proj[:, 1280:1536]
        cqn = (cq * _rms(cq) * qln_ref[...]).astype(BF16)
        ckv = proj[:, 1536:1664]
        ckvn = ckv * _rms(ckv) * kvln_ref[...]
        qb = _dot_nt(cqn, wuq_ref[...])
        qlat = _dot(qb[:, 0:512].astype(BF16), wuk_ref[...])
        for hh in range(B_HEADS):
            qcat_ref[hh, :, 0:LANES] = qlat[:, LANES * hh:LANES * (hh + 1)].astype(BF16)
            qr = _rot(qb[:, 512 + LANES * hh:512 + LANES * (hh + 1)], ct_v, st_v, lane)
            qcat_ref[hh, :, LANES:2 * LANES] = qr.astype(BF16)
        kr = _rot(proj[:, 1664:1792], ct_v, st_v, lane)
        kcat_ref[:, 0:LANES] = ckvn.astype(BF16)
        kcat_ref[:, LANES:2 * LANES] = kr.astype(BF16)
        kcatt_ref[0:LANES, :] = ckvn.T.astype(BF16)
        kcatt_ref[LANES:2 * LANES, :] = kr.T.astype(BF16)

    col_spec = lambda rows: pl.BlockSpec((rows, ts), lambda i: (0, i))
    return pl.pallas_call(
        body, name="even_pre_fwd", grid=(S // ts,),
        in_specs=[_row_spec(ts, D_MODEL), _full_spec((3, D_MODEL)), _full_spec((1, D_MODEL)), _full_spec((EVEN_P, D_MODEL)),
                  _full_spec((1, LANES)), _full_spec((1, LANES)), _full_spec((LANES, LANES)),
                  _row_spec(ts, LANES), _row_spec(ts, LANES), _row_spec(ts, LANES), _row_spec(ts, LANES),
                  _full_spec((1, B_Q_LORA)), _full_spec((1, B_KV_LORA)), _full_spec((1536, B_Q_LORA)), _full_spec((512, 1024))],
        out_specs=[_row_spec(ts, D_MODEL), _row_spec(ts, EVEN_P), _row_spec(ts, 512), _row_spec(ts, 2 * LANES), _row_spec(ts, 2 * LANES),
                   pl.BlockSpec((B_HEADS, ts, 2 * LANES), lambda i: (0, i, 0)), _row_spec(ts, 2 * LANES),
                   col_spec(2 * LANES), col_spec(2 * LANES), col_spec(2 * LANES)],
        out_shape=[_sds((S, D_MODEL), BF16), _sds((S, EVEN_P), F32), _sds((S, 512), BF16), _sds((S, 2 * LANES), BF16),
                   _sds((S, 2 * LANES), BF16), _sds((B_HEADS, S, 2 * LANES), BF16), _sds((S, 2 * LANES), BF16),
                   _sds((2 * LANES, S), BF16), _sds((2 * LANES, S), BF16), _sds((2 * LANES, S), BF16)],
        compiler_params=_params(("arbitrary",)),
    )(x, mod, nw, wie, qn, kn, seg, ca, sa, ct, st, qln, kvln, wuq, wuk)


MLA_SCALE = (B_NOPE + B_ROPE) ** -0.5
LOG2E = 1.4426950408889634

def _row_lo():
    return lax.broadcasted_iota(jnp.int32, (LANES, 1), 0) < HEAD_DIM


def _stack_cols(vT, rlo):
    zero = jnp.zeros_like(vT)
    return jnp.concatenate([jnp.where(rlo, vT, zero), jnp.where(rlo, zero, vT)], axis=1)


def _stack_rows(v, lo):
    zero = jnp.zeros_like(v)
    return jnp.concatenate([jnp.where(lo, v, zero), jnp.where(lo, zero, v)], axis=0)


def _pick_halves_T(xT, rlo, t):
    return jnp.where(rlo, xT[:, 0:t], xT[:, t:2 * t]).T


def _side_split(refs, n_in, n_out, n_scratch, side):
    ns = side.n if side is not None else 0
    cuts = np.cumsum([0, n_in, ns, n_out, ns, n_scratch])
    return [refs[a:b] for a, b in zip(cuts[:-1], cuts[1:])] + [refs[cuts[-1]:]]


def _side_hooks(side, side_ins, side_outs, side_sems, step, total):
    if side is None:
        return lambda: None
    start, mid, end = side.phases(side_ins, side_outs, side_sems)
    pl.when(step == 0)(start)
    pl.when(step == total // 2)(mid)
    return lambda: pl.when(step == total - 1)(end)


def _side_specs(side):
    if side is None:
        return [], [], [], [], []
    return list(side.arrs), [_ANY] * side.n, [_ANY] * side.n, list(side.out_shapes), side.sem_shapes()


def _pp_fwd(q, k, vT, *, kdiv, tq, tk, sub, name, side=None):
    S = k.shape[0]; nb = q.shape[1] // LANES; nq = S // tq; nkv = S // tk; nsub = tk // sub

    def body(*refs):
        (q_ref, k_ref, vT_ref), side_ins, (o_ref, lse_ref), side_outs, (qs, m_s, l_s, acc), side_sems = _side_split(refs, 3, 2, 4, side)
        j = pl.program_id(2)
        rlo = _row_lo()
        step = (pl.program_id(0) * nq + pl.program_id(1)) * nkv + j
        side_end = _side_hooks(side, side_ins, side_outs, side_sems, step, nb * nq * nkv)

        @pl.when(j == 0)
        def _():
            qs[...] = _stack_cols(q_ref[...].astype(F32).T, rlo).astype(BF16)
            m_s[...] = jnp.full((1, 2 * tq), NEG, F32)
            l_s[...] = jnp.zeros((1, 2 * tq), F32)
            acc[...] = jnp.zeros((LANES, 2 * tq), F32)

        qsv = qs[...]
        m, l, a = m_s[...], l_s[...], acc[...]
        s_cur = _dot(k_ref[0:sub, :], qsv)
        for t in range(nsub):
            if t + 1 < nsub:
                s_next = _dot(k_ref[sub * (t + 1):sub * (t + 2), :], qsv)
            m_new = jnp.maximum(m, jnp.max(s_cur, axis=0, keepdims=True))
            alpha = jnp.exp(m - m_new)
            p = jnp.exp(s_cur - m_new)
            l = alpha * l + jnp.sum(p, axis=0, keepdims=True)
            a = alpha * a + _dot(vT_ref[:, sub * t:sub * (t + 1)], p.astype(BF16))
            m = m_new
            if t + 1 < nsub:
                s_cur = s_next
        m_s[...], l_s[...], acc[...] = m, l, a

        @pl.when(j == nkv - 1)
        def _():
            l_f = l_s[...]
            o_ref[...] = _pick_halves_T(acc[...] / l_f, rlo, tq).astype(BF16)
            lse_ref[0, 0] = m_s[...] + jnp.log(l_f)

        side_end()

    s_args, s_in, s_out, s_shapes, s_sems = _side_specs(side)
    return pl.pallas_call(
        body, name=name, grid=(nb, nq, nkv),
        in_specs=[pl.BlockSpec((tq, LANES), lambda b, i, j: (i, b)), pl.BlockSpec((tk, LANES), lambda b, i, j: (j, b // kdiv)),
                  pl.BlockSpec((LANES, tk), lambda b, i, j: (b // kdiv, j))] + s_in,
        out_specs=[pl.BlockSpec((tq, LANES), lambda b, i, j: (i, b)),
                   pl.BlockSpec((1, 1, 1, 2 * tq), lambda b, i, j: (b, i, 0, 0))] + s_out,
        out_shape=[_sds((S, nb * LANES), BF16), _sds((nb, nq, 1, 2 * tq), F32)] + s_shapes,
        scratch_shapes=[pltpu.VMEM((LANES, 2 * tq), BF16), pltpu.VMEM((1, 2 * tq), F32), pltpu.VMEM((1, 2 * tq), F32),
                        pltpu.VMEM((LANES, 2 * tq), F32)] + s_sems,
        compiler_params=_params(("arbitrary",) * 3))(q, k, vT, *s_args)


def _pp_bwd(q, k, kT, v, o, do, lse, *, kdiv, tq, tk, sub, name, side=None):
    S = k.shape[0]; nb = q.shape[1] // LANES; nkb = k.shape[1] // LANES; nq = S // tq; nkv = S // tk; nsub = tk // sub

    def body(*refs):
        ((q_ref, k_ref, kT_ref, v_ref, o_ref, do_ref, lse_ref), side_ins, (dq_ref, dk_ref, dv_ref), side_outs,
         (qsT, qs, dosT, dos, delta_s, dq_acc), side_sems) = _side_split(refs, 7, 3, 6, side)
        b, i, j = pl.program_id(0), pl.program_id(1), pl.program_id(2)
        rlo = _row_lo()
        lo = lax.broadcasted_iota(jnp.int32, (1, LANES), 1) < HEAD_DIM
        side_end = _side_hooks(side, side_ins, side_outs, side_sems, (b * nq + i) * nkv + j, nb * nq * nkv)

        @pl.when((b % kdiv == 0) & (i == 0) & (j == 0))
        def _():
            dk_ref[...] = jnp.zeros((S, LANES), F32)
            dv_ref[...] = jnp.zeros((S, LANES), F32)

        @pl.when(j == 0)
        def _():
            qv = q_ref[...]
            qs[...] = _stack_rows(qv, lo)
            qsT[...] = _stack_cols(qv.astype(F32).T, rlo).astype(BF16)
            dov = do_ref[...].astype(F32)
            dos[...] = _stack_rows(dov.astype(BF16), lo)
            dosT[...] = _stack_cols(dov.T, rlo).astype(BF16)
            prodT = (dov * o_ref[...].astype(F32)).T
            delta_s[...] = jnp.concatenate([jnp.sum(jnp.where(rlo, prodT, 0.0), axis=0, keepdims=True),
                                            jnp.sum(jnp.where(rlo, 0.0, prodT), axis=0, keepdims=True)], axis=1)
            dq_acc[...] = jnp.zeros((LANES, 2 * tq), F32)

        qsTv, dosTv, qsv, dosv = qsT[...], dosT[...], qs[...], dos[...]
        lse_v, delta_v = lse_ref[0, 0], delta_s[...]
        dqa = dq_acc[...]
        s_cur = _dot(k_ref[0:sub, :], qsTv)
        dp_cur = _dot(v_ref[0:sub, :], dosTv)
        for t in range(nsub):
            if t + 1 < nsub:
                s_next = _dot(k_ref[sub * (t + 1):sub * (t + 2), :], qsTv)
                dp_next = _dot(v_ref[sub * (t + 1):sub * (t + 2), :], dosTv)
            p = jnp.exp(s_cur - lse_v)
            ds = (p * (dp_cur - delta_v)).astype(BF16)
            rows = pl.ds(pl.multiple_of(j * tk + sub * t, sub), sub)
            dv_ref[rows, :] += _dot(p.astype(BF16), dosv)
            dk_ref[rows, :] += _dot(ds, qsv)
            dqa = dqa + _dot(kT_ref[:, sub * t:sub * (t + 1)], ds)
            if t + 1 < nsub:
                s_cur, dp_cur = s_next, dp_next
        dq_acc[...] = dqa

        @pl.when(j == nkv - 1)
        def _():
            dq_ref[...] = _pick_halves_T(dq_acc[...], rlo, tq)

        side_end()

    qmap = lambda b, i, j: (i, b)
    kmap = lambda b, i, j: (j, b // kdiv)
    res = lambda b, i, j: (0, b // kdiv)
    s_args, s_in, s_out, s_shapes, s_sems = _side_specs(side)
    return pl.pallas_call(
        body, name=name, grid=(nb, nq, nkv),
        in_specs=[pl.BlockSpec((tq, LANES), qmap), pl.BlockSpec((tk, LANES), kmap), pl.BlockSpec((LANES, tk), lambda b, i, j: (b // kdiv, j)),
                  pl.BlockSpec((tk, LANES), kmap), pl.BlockSpec((tq, LANES), qmap), pl.BlockSpec((tq, LANES), qmap),
                  pl.BlockSpec((1, 1, 1, 2 * tq), lambda b, i, j: (b, i, 0, 0))] + s_in,
        out_specs=[pl.BlockSpec((tq, LANES), qmap), pl.BlockSpec((S, LANES), res), pl.BlockSpec((S, LANES), res)] + s_out,
        out_shape=[_sds((S, nb * LANES), F32), _sds((S, nkb * LANES), F32), _sds((S, nkb * LANES), F32)] + s_shapes,
        scratch_shapes=[pltpu.VMEM((LANES, 2 * tq), BF16), pltpu.VMEM((2 * tq, LANES), BF16), pltpu.VMEM((LANES, 2 * tq), BF16),
                        pltpu.VMEM((2 * tq, LANES), BF16), pltpu.VMEM((1, 2 * tq), F32), pltpu.VMEM((LANES, 2 * tq), F32)] + s_sems,
        compiler_params=_params(("arbitrary",) * 3))(q, k, kT, v, o, do, lse, *s_args)


MLA_C = MLA_SCALE * LOG2E


def _mla_fwd(q, kcat, kcatT, *, tq, tk, sub):
    S = kcat.shape[0]; nq, nkv = S // tq, S // tk; R = B_HEADS * tq; nsub = tk // sub

    def body(q_ref, k_ref, vT_ref, o_ref, lse_ref, qT, m_s, l_s, acc):
        j = pl.program_id(1)

        @pl.when(j == 0)
        def _():
            qT[...] = q_ref[...].reshape(R, 2 * LANES).astype(F32).T.astype(BF16)
            m_s[...] = jnp.full((1, R), NEG, F32)
            l_s[...] = jnp.zeros((1, R), F32)
            acc[...] = jnp.zeros((LANES, R), F32)

        qTv = qT[...]
        m, l, a = m_s[...], l_s[...], acc[...]
        s_cur = _dot(k_ref[0:sub, :], qTv)
        for t in range(nsub):
            if t + 1 < nsub:
                s_next = _dot(k_ref[sub * (t + 1):sub * (t + 2), :], qTv)
            m_new = jnp.maximum(m, jnp.max(s_cur, axis=0, keepdims=True))
            alpha = jnp.exp2((m - m_new) * MLA_C)
            p = jnp.exp2((s_cur - m_new) * MLA_C)
            l = alpha * l + jnp.sum(p, axis=0, keepdims=True)
            a = alpha * a + _dot(vT_ref[:, sub * t:sub * (t + 1)], p.astype(BF16))
            m = m_new
            if t + 1 < nsub:
                s_cur = s_next
        m_s[...], l_s[...], acc[...] = m, l, a

        @pl.when(j == nkv - 1)
        def _():
            l_f = l_s[...]
            o_ref[...] = (acc[...] / l_f).T.reshape(B_HEADS, tq, LANES).astype(BF16)
            lse_ref[0] = m_s[...] * MLA_SCALE + jnp.log(l_f)

    return pl.pallas_call(
        body, name="mla_fwd", grid=(nq, nkv),
        in_specs=[pl.BlockSpec((B_HEADS, tq, 2 * LANES), lambda i, j: (0, i, 0)), pl.BlockSpec((tk, 2 * LANES), lambda i, j: (j, 0)),
                  pl.BlockSpec((LANES, tk), lambda i, j: (0, j))],
        out_specs=[pl.BlockSpec((B_HEADS, tq, LANES), lambda i, j: (0, i, 0)), pl.BlockSpec((1, 1, R), lambda i, j: (i, 0, 0))],
        out_shape=[_sds((B_HEADS, S, LANES), BF16), _sds((nq, 1, R), F32)],
        scratch_shapes=[pltpu.VMEM((2 * LANES, R), BF16), pltpu.VMEM((1, R), F32), pltpu.VMEM((1, R), F32), pltpu.VMEM((LANES, R), F32)],
        compiler_params=_params(("arbitrary", "arbitrary")))(q, kcat, kcatT)


def _mla_bwd(q, kcat, kcatT, o, do, lse, *, tq, tk, sub):
    S = kcat.shape[0]; nq, nkv = S // tq, S // tk; R = B_HEADS * tq; nsub = tk // sub

    def body(q_ref, k_ref, kT_ref, o_ref, do_ref, lse_ref, dq_ref, dk_ref, qT, dosT, dos, delta_s, dq_acc):
        i, j = pl.program_id(0), pl.program_id(1)

        @pl.when((i == 0) & (j == 0))
        def _():
            dk_ref[...] = jnp.zeros((S, 2 * LANES), F32)

        @pl.when(j == 0)
        def _():
            qT[...] = q_ref[...].reshape(R, 2 * LANES).astype(F32).T.astype(BF16)
            dov = do_ref[...].reshape(R, LANES).astype(F32)
            dos[...] = dov.astype(BF16)
            dosT[...] = dov.T.astype(BF16)
            delta_s[...] = jnp.sum((dov * o_ref[...].reshape(R, LANES).astype(F32)).T, axis=0, keepdims=True)
            dq_acc[...] = jnp.zeros((2 * LANES, R), F32)

        qTv, dosTv, dosv = qT[...], dosT[...], dos[...]
        qv = q_ref[...].reshape(R, 2 * LANES)
        lse_v, delta_v = lse_ref[0] * LOG2E, delta_s[...]
        dqa = dq_acc[...]
        s_cur = _dot(k_ref[0:sub, :], qTv)
        dp_cur = _dot(k_ref[0:sub, 0:LANES], dosTv)
        for t in range(nsub):
            if t + 1 < nsub:
                s_next = _dot(k_ref[sub * (t + 1):sub * (t + 2), :], qTv)
                dp_next = _dot(k_ref[sub * (t + 1):sub * (t + 2), 0:LANES], dosTv)
            p = jnp.exp2(s_cur * MLA_C - lse_v)
            ds = (p * (dp_cur - delta_v) * MLA_SCALE).astype(BF16)
            rows = pl.ds(pl.multiple_of(j * tk + sub * t, sub), sub)
            dk_ref[rows, :] += _dot(ds, qv)
            dk_ref[rows, 0:LANES] += _dot(p.astype(BF16), dosv)
            dqa = dqa + _dot(kT_ref[:, sub * t:sub * (t + 1)], ds)
            if t + 1 < nsub:
                s_cur, dp_cur = s_next, dp_next
        dq_acc[...] = dqa

        @pl.when(j == nkv - 1)
        def _():
            dq_ref[...] = dq_acc[...].T.reshape(B_HEADS, tq, 2 * LANES)

    hspec = lambda w: pl.BlockSpec((B_HEADS, tq, w), lambda i, j: (0, i, 0))
    return pl.pallas_call(
        body, name="mla_bwd", grid=(nq, nkv),
        in_specs=[hspec(2 * LANES), pl.BlockSpec((tk, 2 * LANES), lambda i, j: (j, 0)), pl.BlockSpec((2 * LANES, tk), lambda i, j: (0, j)),
                  hspec(LANES), hspec(LANES), pl.BlockSpec((1, 1, R), lambda i, j: (i, 0, 0))],
        out_specs=[hspec(2 * LANES), pl.BlockSpec((S, 2 * LANES), lambda i, j: (0, 0))],
        out_shape=[_sds((B_HEADS, S, 2 * LANES), F32), _sds((S, 2 * LANES), F32)],
        scratch_shapes=[pltpu.VMEM((2 * LANES, R), BF16), pltpu.VMEM((LANES, R), BF16), pltpu.VMEM((R, LANES), BF16),
                        pltpu.VMEM((1, R), F32), pltpu.VMEM((2 * LANES, R), F32)],
        compiler_params=_params(("arbitrary", "arbitrary")))(q, kcat, kcatT, o, do, lse)


def _win_start(i, tq, nk, S):
    return pl.multiple_of(jnp.clip(i * tq - WINDOW, 0, S - nk), LANES)


def _win_dist_table(S, tq):
    nk = min(tq + 2 * WINDOW, S)
    nq = S // tq
    r = np.arange(nk)[:, None]
    c = (np.arange(2 * tq) % tq)[None, :]
    tabs = []
    for rel in (0, WINDOW, (nq - 1) * tq - (S - nk)):
        dist = np.abs(rel + c - r).astype(np.float32)
        tabs.append(np.where(dist <= WINDOW, dist, np.float32(1e32)))
    return jnp.asarray(np.stack(tabs))


def _win_dist_spec(nk, tq, nq):
    return pl.BlockSpec((1, nk, 2 * tq), lambda b, i: (jnp.where(i == 0, 0, jnp.where(i == nq - 1, 2, 1)), 0, 0))


def _win_fwd(q, k, vT, dist, slope, sink, *, kdiv, tq, nbs, name):
    S = k.shape[0]; nb = q.shape[1] // LANES; nq = S // tq; nk = min(tq + 2 * WINDOW, S)
    assert nb % nbs == 0 and nbs % kdiv == 0
    kvw = (nbs // kdiv) * LANES

    def body(q_ref, k_ref, vT_ref, dist_ref, slope_ref, sink_ref, o_ref, lse_ref):
        i = pl.program_id(1)
        rlo = _row_lo()
        k0 = _win_start(i, tq, nk, S)
        kk, vv, dd = k_ref[pl.ds(k0, nk), :], vT_ref[:, pl.ds(k0, nk)], dist_ref[0]
        for u in range(nbs):
            kv = slice(LANES * (u // kdiv), LANES * (u // kdiv + 1))
            qsT = _stack_cols(q_ref[:, LANES * u:LANES * (u + 1)].astype(F32).T, rlo).astype(BF16)
            s = _dot(kk[:, kv], qsT) - slope_ref[u] * dd
            sk = sink_ref[u]
            m = jnp.maximum(jnp.max(s, axis=0, keepdims=True), sk)
            p = jnp.exp(s - m)
            l = jnp.sum(p, axis=0, keepdims=True) + jnp.exp(sk - m)
            o_ref[:, LANES * u:LANES * (u + 1)] = _pick_halves_T(_dot(vv[kv, :], p.astype(BF16)) / l, rlo, tq).astype(BF16)
            lse_ref[u, 0] = m + jnp.log(l)

    row_spec = pl.BlockSpec((nbs, 1, 2 * tq), lambda b, i: (b, 0, 0))
    return pl.pallas_call(
        body, name=name, grid=(nb // nbs, nq),
        in_specs=[pl.BlockSpec((tq, nbs * LANES), lambda b, i: (i, b)), pl.BlockSpec((S, kvw), lambda b, i: (0, b)),
                  pl.BlockSpec((kvw, S), lambda b, i: (b, 0)), _win_dist_spec(nk, tq, nq), row_spec, row_spec],
        out_specs=[pl.BlockSpec((tq, nbs * LANES), lambda b, i: (i, b)), pl.BlockSpec((nbs, 1, 1, 2 * tq), lambda b, i: (b, i, 0, 0))],
        out_shape=[_sds((S, nb * LANES), BF16), _sds((nb, nq, 1, 2 * tq), F32)],
        compiler_params=_params(("arbitrary", "arbitrary")))(q, k, vT, dist, slope, sink)


def _win_bwd(q, k, kT, v, o, do, lse, dist, slope, sink, *, kdiv, tq, nbs, name):
    S = k.shape[0]; nb = q.shape[1] // LANES; nkb = k.shape[1] // LANES; nq = S // tq; nk = min(tq + 2 * WINDOW, S)
    assert nb % nbs == 0 and nbs % kdiv == 0
    nkv = nbs // kdiv
    kvw = nkv * LANES

    def body(q_ref, k_ref, kT_ref, v_ref, o_ref, do_ref, lse_ref, dist_ref, slope_ref, sink_ref, dq_ref, dk_ref, dv_ref, dsink_ref, ds_acc):
        i = pl.program_id(1)
        rlo = _row_lo()
        lo = lax.broadcasted_iota(jnp.int32, (1, LANES), 1) < HEAD_DIM

        @pl.when(i == 0)
        def _():
            dk_ref[...] = jnp.zeros((S, kvw), F32)
            dv_ref[...] = jnp.zeros((S, kvw), F32)
            ds_acc[...] = jnp.zeros((nbs, 2 * tq), F32)

        k0 = _win_start(i, tq, nk, S)
        rows = pl.ds(k0, nk)
        kk_all, vv_all, kkT_all, dd = k_ref[rows, :], v_ref[rows, :], kT_ref[:, rows], dist_ref[0]
        dv_sum, dk_sum = [None] * nkv, [None] * nkv
        for u in range(nbs):
            g = u // kdiv
            kv = slice(LANES * g, LANES * (g + 1))
            kk, vv, kkT = kk_all[:, kv], vv_all[:, kv], kkT_all[kv, :]
            cols = slice(LANES * u, LANES * (u + 1))
            qv = q_ref[:, cols]
            qs = _stack_rows(qv, lo)
            qsT = _stack_cols(qv.astype(F32).T, rlo).astype(BF16)
            dov = do_ref[:, cols].astype(F32)
            dos = _stack_rows(dov.astype(BF16), lo)
            dosT = _stack_cols(dov.T, rlo).astype(BF16)
            prodT = (dov * o_ref[:, cols].astype(F32)).T
            delta = jnp.concatenate([jnp.sum(jnp.where(rlo, prodT, 0.0), axis=0, keepdims=True),
                                     jnp.sum(jnp.where(rlo, 0.0, prodT), axis=0, keepdims=True)], axis=1)
            lse_v = lse_ref[u, 0]
            ds_acc[u:u + 1, :] += -jnp.exp(sink_ref[u] - lse_v) * delta
            p = jnp.exp(_dot(kk, qsT) - slope_ref[u] * dd - lse_v)
            ds = (p * (_dot(vv, dosT) - delta)).astype(BF16)
            dv_u, dk_u = _dot(p.astype(BF16), dos), _dot(ds, qs)
            dv_sum[g] = dv_u if dv_sum[g] is None else dv_sum[g] + dv_u
            dk_sum[g] = dk_u if dk_sum[g] is None else dk_sum[g] + dk_u
            dq_ref[:, cols] = (_pick_halves_T(_dot(kkT, ds), rlo, tq) * 0.125).astype(BF16)
        dv_ref[rows, :] += jnp.concatenate(dv_sum, axis=1)
        dk_ref[rows, :] += jnp.concatenate(dk_sum, axis=1)

        @pl.when(i == nq - 1)
        def _():
            acc = ds_acc[...]
            for u in range(nbs):
                dsink_ref[u] = jnp.concatenate(
                    [jnp.broadcast_to(jnp.sum(acc[u:u + 1, 0:tq], axis=1, keepdims=True), (1, LANES)),
                     jnp.broadcast_to(jnp.sum(acc[u:u + 1, tq:2 * tq], axis=1, keepdims=True), (1, LANES)),
                     jnp.zeros((6, LANES), F32)], axis=0)

    qmap = lambda b, i: (i, b)
    kv_spec = pl.BlockSpec((S, kvw), lambda b, i: (0, b))
    row_spec = pl.BlockSpec((nbs, 1, 2 * tq), lambda b, i: (b, 0, 0))
    wide = pl.BlockSpec((tq, nbs * LANES), qmap)
    return pl.pallas_call(
        body, name=name, grid=(nb // nbs, nq),
        in_specs=[wide, kv_spec, pl.BlockSpec((kvw, S), lambda b, i: (b, 0)), kv_spec, wide, wide,
                  pl.BlockSpec((nbs, 1, 1, 2 * tq), lambda b, i: (b, i, 0, 0)), _win_dist_spec(nk, tq, nq), row_spec, row_spec],
        out_specs=[wide, kv_spec, kv_spec, pl.BlockSpec((nbs, 8, LANES), lambda b, i: (b, 0, 0))],
        out_shape=[_sds((S, nb * LANES), BF16), _sds((S, nkb * LANES), F32), _sds((S, nkb * LANES), F32), _sds((nb, 8, LANES), F32)],
        scratch_shapes=[pltpu.VMEM((nbs, 2 * tq), F32)],
        compiler_params=_params(("arbitrary", "arbitrary")))(q, k, kT, v, o, do, lse, dist, slope, sink)


def _sum_rows(v):
    return jnp.sum(v, axis=0, keepdims=True)


def _norm_mod_bwd(dh, xv, mod_ref, nw_ref, stats_ref):
    r = _rms(xv)
    xn = xv * r
    nw = nw_ref[...]
    stats_ref[0:1, :] += _sum_rows(dh)
    stats_ref[1:2, :] += _sum_rows(dh * (xn * nw))
    dn = dh * (1.0 + mod_ref[1:2, :])
    stats_ref[2:3, :] += _sum_rows(dn * xn)
    return _rms_bwd(xv, r, dn * nw)


def _even_gate_specs(ts):
    return [pl.BlockSpec((ts, 256), lambda i, c=c: (i, c)) for c in (3, 4, 7, 8)]


def _even_post_fwd(oa, olat, proj, x, gate, wuv, woe):
    S = x.shape[0]
    ts = min(ROW_TILE, S)

    def body(oa_ref, ol_ref, ga0_ref, ga1_ref, gb0_ref, gb1_ref, x_ref, gate_ref, wuv_ref, woe_ref, y_ref, x1_ref):
        sa, _ = _silu_and_grad(jnp.concatenate([ga0_ref[...], ga1_ref[...]], axis=1))
        sb, _ = _silu_and_grad(jnp.concatenate([gb0_ref[...], gb1_ref[...]], axis=1))
        olc = jnp.concatenate([ol_ref[hh] for hh in range(B_HEADS)], axis=1).astype(BF16)
        ob = _dot(olc, wuv_ref[...])
        mix = jnp.concatenate([oa_ref[...] * sa, ob * sb], axis=1).astype(BF16)
        y = _dot(mix, woe_ref[...])
        y_ref[...] = y.astype(BF16)
        x1_ref[...] = x_ref[...] + gate_ref[...] * y

    return pl.pallas_call(
        body, name="even_post_fwd", grid=(S // ts,),
        in_specs=[_row_spec(ts, 512), pl.BlockSpec((B_HEADS, ts, LANES), lambda i: (0, i, 0))] + _even_gate_specs(ts) +
                 [_row_spec(ts, D_MODEL), _full_spec((1, D_MODEL)), _full_spec((1024, 512)), _full_spec((1024, D_MODEL))],
        out_specs=[_row_spec(ts, D_MODEL), _row_spec(ts, D_MODEL)],
        out_shape=[_sds((S, D_MODEL), BF16), _sds((S, D_MODEL), F32)],
        compiler_params=_params(("arbitrary",)),
    )(oa, olat, proj, proj, proj, proj, x, gate, wuv, woe)


def _odd_pre_fwd(x, mod, nw, wio):
    S = x.shape[0]
    ts = min(ROW_TILE, S)

    def body(x_ref, mod_ref, nw_ref, wio_ref, h_ref, g_ref, q_ref, k_ref, v_ref, kt_ref, vt_ref):
        xv = x_ref[...]
        h = (xv * _rms(xv) * nw_ref[...]) * (1.0 + mod_ref[1:2, :]) + mod_ref[0:1, :]
        hb = h.astype(BF16)
        h_ref[...] = hb
        proj = jnp.concatenate([_dot(hb, wio_ref[p]) for p in range(N_CHIPS)], axis=1)
        g_ref[...] = proj[:, 1536:2560]
        q_ref[...] = (proj[:, 0:1024] * 0.125).astype(BF16)
        lane = _lane_iota()
        k_v = jnp.concatenate([_dup_heads(proj[:, 1024 + LANES * j:1024 + LANES * (j + 1)], lane) for j in range(2)], axis=1)
        v_v = jnp.concatenate([_dup_heads(proj[:, 1280 + LANES * j:1280 + LANES * (j + 1)], lane) for j in range(2)], axis=1)
        k_ref[...] = k_v.astype(BF16)
        v_ref[...] = v_v.astype(BF16)
        kt_ref[...] = k_v.T.astype(BF16)
        vt_ref[...] = v_v.T.astype(BF16)

    col_spec = pl.BlockSpec((512, ts), lambda i: (0, i))
    return pl.pallas_call(
        body, name="odd_pre_fwd", grid=(S // ts,),
        in_specs=[_row_spec(ts, D_MODEL), _full_spec((3, D_MODEL)), _full_spec((1, D_MODEL)),
                  _full_spec((N_CHIPS, D_MODEL, ODD_IN // N_CHIPS))],
        out_specs=[_row_spec(ts, D_MODEL), _row_spec(ts, 1024), _row_spec(ts, 1024), _row_spec(ts, 512), _row_spec(ts, 512),
                   col_spec, col_spec],
        out_shape=[_sds((S, D_MODEL), BF16), _sds((S, 1024), F32), _sds((S, 1024), BF16), _sds((S, 512), BF16),
                   _sds((S, 512), BF16), _sds((512, S), BF16), _sds((512, S), BF16)],
        compiler_params=_params(("arbitrary",)),
    )(x, mod, nw, wio)


def _odd_post(oc, g, x1, gate, woo, fw, tgt):
    S = x1.shape[0]
    ts = min(ROW_TILE, S)
    nsteps = S // ts

    def body(oc_ref, g_ref, x_ref, gate_ref, woo_ref, fw_ref, tgt_ref, doc_ref, dgc_ref, dx2_ref, dwoo_out, stats_ref, dwoo_ref):
        @pl.when(pl.program_id(0) == 0)
        def _():
            dwoo_ref[...] = jnp.zeros((D_MODEL, D_MODEL), F32)
            stats_ref[...] = jnp.zeros((8, D_MODEL), F32)

        ocv = oc_ref[...]
        sg, dsg = _silu_and_grad(g_ref[...])
        mix = (ocv * sg).astype(BF16)
        woo_v = woo_ref[...]
        y = _dot(mix, woo_v)
        gate_v = gate_ref[...]
        x2 = x_ref[...] + gate_v * y
        r = _rms(x2)
        fw_v = fw_ref[...]
        xn = x2 * r
        err = xn * fw_v - tgt_ref[...]
        dout = err * (1.0 / D_MODEL)
        dx2 = _rms_bwd(x2, r, dout * fw_v)
        dx2_ref[...] = dx2
        stats_ref[0:1, :] += _sum_rows(dout * xn)
        stats_ref[1:2, :] += _sum_rows(dx2 * y)
        loss_t = 0.5 * jnp.sum(_sum_rows(err * dout), axis=-1, keepdims=True)
        stats_ref[2:3, :] += jnp.broadcast_to(loss_t, (1, D_MODEL))
        dy = (gate_v * dx2).astype(BF16)
        dmix = _dot_nt(dy, woo_v)
        dwoo_ref[...] += _dot_tn(mix, dy)
        doc_ref[...] = (dmix * sg).astype(BF16)
        dgc_ref[...] = (dmix * ocv * dsg).astype(BF16)

        @pl.when(pl.program_id(0) == nsteps - 1)
        def _():
            dwoo_out[...] = dwoo_ref[...].astype(BF16)

    return pl.pallas_call(
        body, name="odd_post", grid=(nsteps,),
        in_specs=[_row_spec(ts, D_MODEL), _row_spec(ts, D_MODEL), _row_spec(ts, D_MODEL), _full_spec((1, D_MODEL)),
                  _full_spec((D_MODEL, D_MODEL)), _full_spec((1, D_MODEL)), _row_spec(ts, D_MODEL)],
        out_specs=[_row_spec(ts, D_MODEL), _row_spec(ts, D_MODEL), _row_spec(ts, D_MODEL),
                   _full_spec((D_MODEL, D_MODEL), single=False), _full_spec((8, D_MODEL), single=False)],
        out_shape=[_sds((S, D_MODEL), BF16), _sds((S, D_MODEL), BF16), _sds((S, D_MODEL), F32), _sds((D_MODEL, D_MODEL), BF16),
                   _sds((8, D_MODEL), F32)],
        scratch_shapes=[pltpu.VMEM((D_MODEL, D_MODEL), F32)],
        compiler_params=_params(("arbitrary",)),
    )(oc, g, x1, gate, woo, fw, tgt)


def _odd_pre_bwd(dq, dk, dv, dgc, h, x, dx_res, mod, nw, wio):
    S = x.shape[0]
    ts = min(IN_PROJ_ROW_TILE, S)
    nsteps = S // ts
    wsh = ODD_IN // N_CHIPS

    def body(dq_ref, dk_ref, dv_ref, dgc_ref, h_ref, x_ref, dxr_ref, mod_ref, nw_ref, wio_ref, dx_ref, dw_ref, stats_ref, dw_acc):
        @pl.when(pl.program_id(0) == 0)
        def _():
            dw_acc[...] = jnp.zeros((N_CHIPS, D_MODEL, wsh), F32)
            stats_ref[...] = jnp.zeros((8, D_MODEL), F32)

        lane = _lane_iota()
        dkv = [_fold_heads(r[:, 2 * LANES * j:2 * LANES * (j + 1)], lane).astype(BF16) for r in (dk_ref, dv_ref) for j in range(2)]
        dproj = jnp.concatenate([dq_ref[...]] + dkv + [dgc_ref[...]], axis=1)
        hv = h_ref[...]
        dh = None
        for p in range(N_CHIPS):
            dp_cols = dproj[:, wsh * p:wsh * (p + 1)]
            part = _dot_nt(dp_cols, wio_ref[p])
            dh = part if dh is None else dh + part
            dw_acc[p] += _dot_tn(hv, dp_cols)
        dx_ref[...] = dxr_ref[...] + _norm_mod_bwd(dh, x_ref[...], mod_ref, nw_ref, stats_ref)

        @pl.when(pl.program_id(0) == nsteps - 1)
        def _():
            dw_ref[...] = dw_acc[...].astype(BF16)

    return pl.pallas_call(
        body, name="odd_pre_bwd", grid=(nsteps,),
        in_specs=[_row_spec(ts, 1024), _row_spec(ts, 512), _row_spec(ts, 512), _row_spec(ts, 1024), _row_spec(ts, D_MODEL),
                  _row_spec(ts, D_MODEL), _row_spec(ts, D_MODEL), _full_spec((3, D_MODEL)), _full_spec((1, D_MODEL)),
                  _full_spec((N_CHIPS, D_MODEL, wsh))],
        out_specs=[_row_spec(ts, D_MODEL), _full_spec((N_CHIPS, D_MODEL, wsh), single=False), _full_spec((8, D_MODEL), single=False)],
        out_shape=[_sds((S, D_MODEL), F32), _sds((N_CHIPS, D_MODEL, wsh), BF16), _sds((8, D_MODEL), F32)],
        scratch_shapes=[pltpu.VMEM((N_CHIPS, D_MODEL, wsh), F32)],
        compiler_params=_params(("arbitrary",)),
    )(dq, dk, dv, dgc, h, x, dx_res, mod, nw, wio)


def _even_post_bwd(dx1, y, oa, olat, proj, gate, wuv, woe):
    S = dx1.shape[0]
    ts = min(ROW_TILE, S)
    nsteps = S // ts

    def body(dx_ref, y_ref, oa_ref, ol_ref, ga0_ref, ga1_ref, gb0_ref, gb1_ref, gate_ref, wuv_ref, woe_ref,
             doa_ref, dga_ref, dgb_ref, dol_ref, dwoe_out, dwuv_ref, stats_ref, dwoe_ref):
        @pl.when(pl.program_id(0) == 0)
        def _():
            dwoe_ref[...] = jnp.zeros((D_MODEL, D_MODEL), F32)
            dwuv_ref[...] = jnp.zeros((1024, 512), F32)
            stats_ref[...] = jnp.zeros((8, D_MODEL), F32)

        dxv = dx_ref[...]
        stats_ref[0:1, :] += _sum_rows(dxv * y_ref[...])
        dy = (gate_ref[...] * dxv).astype(BF16)
        sa, dsa = _silu_and_grad(jnp.concatenate([ga0_ref[...], ga1_ref[...]], axis=1))
        sb, dsb = _silu_and_grad(jnp.concatenate([gb0_ref[...], gb1_ref[...]], axis=1))
        olc = jnp.concatenate([ol_ref[hh] for hh in range(B_HEADS)], axis=1).astype(BF16)
        wuv_v = wuv_ref[...]
        ob = _dot(olc, wuv_v)
        oav = oa_ref[...]
        mix = jnp.concatenate([oav * sa, ob * sb], axis=1).astype(BF16)
        dmix = _dot_nt(dy, woe_ref[...])
        dwoe_ref[...] += _dot_tn(mix, dy)
        dma, dmb = dmix[:, 0:512], dmix[:, 512:1024]
        doa_ref[...] = (dma * sa).astype(BF16)
        dga_ref[...] = (dma * oav * dsa).astype(BF16)
        dgb_ref[...] = (dmb * ob * dsb).astype(BF16)
        dob = (dmb * sb).astype(BF16)
        dol = _dot_nt(dob, wuv_v)
        dwuv_ref[...] += _dot_tn(olc, dob)
        for hh in range(B_HEADS):
            dol_ref[hh] = dol[:, LANES * hh:LANES * (hh + 1)].astype(BF16)

        @pl.when(pl.program_id(0) == nsteps - 1)
        def _():
            dwoe_out[...] = dwoe_ref[...].astype(BF16)

    head_spec = pl.BlockSpec((B_HEADS, ts, LANES), lambda i: (0, i, 0))
    return pl.pallas_call(
        body, name="even_post_bwd", grid=(nsteps,),
        in_specs=[_row_spec(ts, D_MODEL), _row_spec(ts, D_MODEL), _row_spec(ts, 512), head_spec] + _even_gate_specs(ts) +
                 [_full_spec((1, D_MODEL)), _full_spec((1024, 512)), _full_spec((1024, D_MODEL))],
        out_specs=[_row_spec(ts, 512), _row_spec(ts, 512), _row_spec(ts, 512), head_spec,
                   _full_spec((D_MODEL, D_MODEL), single=False), _full_spec((1024, 512), single=False),
                   _full_spec((8, D_MODEL), single=False)],
        out_shape=[_sds((S, 512), BF16), _sds((S, 512), BF16), _sds((S, 512), BF16), _sds((B_HEADS, S, LANES), BF16),
                   _sds((D_MODEL, D_MODEL), BF16), _sds((1024, 512), F32), _sds((8, D_MODEL), F32)],
        scratch_shapes=[pltpu.VMEM((D_MODEL, D_MODEL), F32)],
        compiler_params=_params(("arbitrary",)),
    )(dx1, y, oa, olat, proj, proj, proj, proj, gate, wuv, woe)


def _even_pre_bwd(x, h, proj, dqa, dka, dva, dga, dgb, dqcat, dkcat, dx_res, mod, nw, wie, qn, kn, seg, ca, sa, ct, st,
                  qln, kvln, wuq, wuk):
    S = x.shape[0]
    ts = min(IN_PROJ_ROW_TILE, S)
    nsteps = S // ts

    def body(x_ref, h_ref, proj_ref, dqa_ref, dka_ref, dva_ref, dga_ref, dgb_ref, dqc_ref, dkc_ref, dxr_ref, mod_ref, nw_ref,
             wie_ref, qn_ref, kn_ref, seg_ref, ca_ref, sa_ref, ct_ref, st_ref, qln_ref, kvln_ref, wuq_ref, wuk_ref,
             dx_ref, dwie_out, dwuq_out, dwuk_out, stats_ref, nstats_ref, dwie_ref, dwuq_ref, dwuk_ref, stage, uq_stage, uk_stage):
        @pl.when(pl.program_id(0) == 0)
        def _():
            dwie_ref[...] = jnp.zeros((EVEN_P, D_MODEL), F32)
            dwuq_ref[...] = jnp.zeros((1536, B_Q_LORA), F32)
            dwuk_ref[...] = jnp.zeros((512, 1024), F32)
            stats_ref[...] = jnp.zeros((8, D_MODEL), F32)
            nstats_ref[...] = jnp.zeros((8, 256), F32)

        lane = _lane_iota()
        ca_v, sa_v, ct_v, st_v = ca_ref[...], sa_ref[...], ct_ref[...], st_ref[...]
        seg_v = seg_ref[...]

        def head_norm_bwd(xc, dy, w):
            r = lax.rsqrt(_seg_mean(xc * xc, seg_v) + EPS)
            g = dy * w
            dxc = r * g - xc * (r * r * r) * _seg_mean(xc * g, seg_v)
            return dxc, _sum_rows(dy * (xc * r))

        pieces = []
        dqn = jnp.zeros((1, LANES), F32)
        for cb in range(4):
            sl = slice(LANES * cb, LANES * (cb + 1))
            dy = _rot_bwd(dqa_ref[:, sl] * 0.125, ca_v, sa_v, lane)
            dxc, dw = head_norm_bwd(proj_ref[:, sl], dy, qn_ref[...])
            pieces.append(dxc)
            dqn = dqn + dw
        dxc, dkn = head_norm_bwd(proj_ref[:, 512:640], _rot_bwd(_fold_heads(dka_ref[...], lane), ca_v, sa_v, lane), kn_ref[...])
        pieces += [dxc, _fold_heads(dva_ref[...], lane), dga_ref[...]]
        nstats_ref[0:1, 0:LANES] += dqn + pltpu.roll(dqn, HEAD_DIM, 1)
        nstats_ref[1:2, 0:LANES] += dkn + pltpu.roll(dkn, HEAD_DIM, 1)

        cq = proj_ref[:, 1280:1536]
        rq = _rms(cq)
        cqn_f = cq * rq
        qln_v = qln_ref[...]
        cqn = (cqn_f * qln_v).astype(BF16)
        wuq_v, wuk_v = wuq_ref[...], wuk_ref[...]
        qnope = _dot_nt(cqn, wuq_v[0:512, :]).astype(BF16)
        dqlat = jnp.concatenate([dqc_ref[hh, :, 0:LANES] for hh in range(B_HEADS)], axis=1).astype(BF16)
        dqnope = _dot_nt(dqlat, wuk_v)
        dwuk_ref[...] += _dot_tn(qnope, dqlat)
        dqr = [_rot_bwd(dqc_ref[hh, :, LANES:2 * LANES], ct_v, st_v, lane) for hh in range(B_HEADS)]
        dqb = jnp.concatenate([dqnope] + dqr, axis=1).astype(BF16)
        dcqn = _dot(dqb, wuq_v)
        dwuq_ref[...] += _dot_tn(dqb, cqn)
        nstats_ref[2:3, :] += _sum_rows(dcqn * cqn_f)
        dcq = _rms_bwd(cq, rq, dcqn * qln_v)
        ckv = proj_ref[:, 1536:1664]
        rk = _rms(ckv)
        dckvn = dkc_ref[:, 0:LANES]
        nstats_ref[3:4, 0:LANES] += _sum_rows(dckvn * (ckv * rk))
        dckv = _rms_bwd(ckv, rk, dckvn * kvln_ref[...])
        dkr = _rot_bwd(dkc_ref[:, LANES:2 * LANES], ct_v, st_v, lane)
        pieces += [dcq, dckv, dkr, dgb_ref[...]]
        dproj = jnp.concatenate([piece.astype(BF16) for piece in pieces], axis=1)
        dh = _dot(dproj, wie_ref[...])
        dwie_ref[...] += _dot_tn(dproj, h_ref[...])
        dx_ref[...] = dxr_ref[...] + _norm_mod_bwd(dh, x_ref[...], mod_ref, nw_ref, stats_ref)

        @pl.when(pl.program_id(0) == nsteps - 1)
        def _():
            r_out = 0
            for lo, hi in ((0, EVEN_GAP), (EVEN_GAP + EVEN_P - EVEN_IN, EVEN_P)):
                for r0 in range(lo, hi, stage.shape[0]):
                    n = min(stage.shape[0], hi - r0)
                    stage[0:n, :] = dwie_ref[r0:r0 + n, :].astype(BF16)
                    pltpu.sync_copy(stage.at[0:n], dwie_out.at[pl.ds(r_out, n), :])
                    r_out += n
            per = B_NOPE + B_ROPE
            for hd in range(B_HEADS):
                uq_stage[per * hd:per * hd + B_NOPE, :] = dwuq_ref[B_NOPE * hd:B_NOPE * (hd + 1), :].astype(BF16)
                uq_stage[per * hd + B_NOPE:per * (hd + 1), :] = dwuq_ref[512 + LANES * hd:512 + LANES * hd + B_ROPE, :].astype(BF16)
                uk_stage[B_NOPE * hd:B_NOPE * (hd + 1), :] = dwuk_ref[B_NOPE * hd:B_NOPE * (hd + 1),
                                                                       LANES * hd:LANES * (hd + 1)].astype(BF16)
            pltpu.sync_copy(uq_stage, dwuq_out)
            pltpu.sync_copy(uk_stage, dwuk_out)

    return pl.pallas_call(
        body, name="even_pre_bwd", grid=(nsteps,),
        in_specs=[_row_spec(ts, D_MODEL), _row_spec(ts, D_MODEL), _row_spec(ts, EVEN_P), _row_spec(ts, 512), _row_spec(ts, 2 * LANES),
                  _row_spec(ts, 2 * LANES), _row_spec(ts, 512), _row_spec(ts, 512),
                  pl.BlockSpec((B_HEADS, ts, 2 * LANES), lambda i: (0, i, 0)), _row_spec(ts, 2 * LANES), _row_spec(ts, D_MODEL),
                  _full_spec((3, D_MODEL)), _full_spec((1, D_MODEL)), _full_spec((EVEN_P, D_MODEL)),
                  _full_spec((1, LANES)), _full_spec((1, LANES)), _full_spec((LANES, LANES)),
                  _row_spec(ts, LANES), _row_spec(ts, LANES), _row_spec(ts, LANES), _row_spec(ts, LANES),
                  _full_spec((1, B_Q_LORA)), _full_spec((1, B_KV_LORA)), _full_spec((1536, B_Q_LORA)), _full_spec((512, 1024))],
        out_specs=[_row_spec(ts, D_MODEL), _ANY, _ANY, _ANY, _full_spec((8, D_MODEL), single=False), _full_spec((8, 256), single=False)],
        out_shape=[_sds((S, D_MODEL), F32), _sds((EVEN_IN, D_MODEL), BF16), _sds((B_HEADS * (B_NOPE + B_ROPE), B_Q_LORA), BF16),
                   _sds((B_HEADS * B_NOPE, LANES), BF16), _sds((8, D_MODEL), F32), _sds((8, 256), F32)],
        scratch_shapes=[pltpu.VMEM((EVEN_P, D_MODEL), F32), pltpu.VMEM((1536, B_Q_LORA), F32), pltpu.VMEM((512, 1024), F32),
                        pltpu.VMEM((256, D_MODEL), BF16), pltpu.VMEM((B_HEADS * (B_NOPE + B_ROPE), B_Q_LORA), BF16),
                        pltpu.VMEM((B_HEADS * B_NOPE, LANES), BF16)],
        compiler_params=_params(("arbitrary",)),
    )(x, h, proj, dqa, dka, dva, dga, dgb, dqcat, dkcat, dx_res, mod, nw, wie, qn, kn, seg, ca, sa, ct, st, qln, kvln, wuq, wuk)


def _ada_fwd(c_all, w, b):
    n = w.shape[2]

    def body(c_ref, w_ref, b_ref, o_ref):
        cv = c_ref[...]
        o_ref[0] = _dot_f32(cv * _sigmoid(cv), w_ref[0]) + b_ref[0]

    return pl.pallas_call(
        body, name="ada_fwd", grid=(2,),
        in_specs=[pl.BlockSpec((N_DEV, D_MODEL), lambda l: (0, 0)), pl.BlockSpec((1, D_MODEL, n), lambda l: (l, 0, 0)),
                  pl.BlockSpec((1, 1, n), lambda l: (l, 0, 0))],
        out_specs=pl.BlockSpec((1, N_DEV, n), lambda l: (l, 0, 0)),
        out_shape=_sds((2, N_DEV, n), F32),
        compiler_params=_params(("arbitrary",)),
    )(c_all, w, b)


def _ada_bwd(c_all_t, dmod):
    n = dmod.shape[2]

    def body(c_ref, d_ref, o_ref):
        cv = c_ref[...]
        act = cv * _sigmoid(cv)
        dv = d_ref[0]
        acc = act[:, 0:1] * dv[0:1, :]
        for bb in range(1, N_DEV):
            acc = acc + act[:, bb:bb + 1] * dv[bb:bb + 1, :]
        o_ref[0] = acc

    return pl.pallas_call(
        body, name="ada_bwd", grid=(2,),
        in_specs=[pl.BlockSpec((D_MODEL, N_DEV), lambda l: (0, 0)), pl.BlockSpec((1, N_DEV, n), lambda l: (l, 0, 0))],
        out_specs=pl.BlockSpec((1, D_MODEL, n), lambda l: (l, 0, 0)),
        out_shape=_sds((2, D_MODEL, n), F32),
        compiler_params=_params(("arbitrary",)),
    )(c_all_t, dmod)


ADAM_ROW_TILE = 512


def _adam_update(g, w, m, v):
    m_new = ADAM_B1 * m + (1.0 - ADAM_B1) * g
    v_new = ADAM_B2 * v + (1.0 - ADAM_B2) * jnp.square(g)
    m_hat = m_new / (1.0 - ADAM_B1 ** ADAM_STEP)
    v_hat = v_new / (1.0 - ADAM_B2 ** ADAM_STEP)
    return -ADAM_LR * (m_hat / (jnp.sqrt(v_hat) + ADAM_EPS) + ADAM_WD * w), m_new, v_new


SMALL_ROWS = dict(dmod=(0, D_MODEL), norm_w=(6, D_MODEL), final_norm=(8, D_MODEL), a_q_norm=(9, HEAD_DIM), a_k_norm=(10, HEAD_DIM),
                  b_q_lora_norm=(11, B_Q_LORA), b_kv_lora_norm=(12, B_KV_LORA), c_sink=(13, C_HEADS))
SMALL_WEIGHTS = ("ada_b", "norm_w", "final_norm", "a_q_norm", "a_k_norm", "b_q_lora_norm", "b_kv_lora_norm", "c_sink")
LOSS_ROW = 14


def _pack_small(res):
    def padded(v):
        return jnp.concatenate([v, jnp.zeros((v.shape[0], D_MODEL - v.shape[1]), F32)], axis=1)

    rows = [res["dmod"].reshape(6, D_MODEL), res["norm_w"], res["final_norm"].reshape(1, D_MODEL)]
    rows += [padded(res[k]) for k in ("a_q_norm", "a_k_norm", "b_q_lora_norm", "b_kv_lora_norm", "c_sink")]
    return jnp.concatenate(rows + [res["loss_row"], jnp.zeros((1, D_MODEL), F32)], axis=0)


def _adam_small(parts, ws, ms, vs):
    nw = len(SMALL_WEIGHTS)

    def body(*refs):
        p_ref = refs[0]
        w_refs, m_refs, v_refs = refs[1:1 + nw], refs[1 + nw:1 + 2 * nw], refs[1 + 2 * nw:1 + 3 * nw]
        outs = refs[1 + 3 * nw:]
        g_all = p_ref[0]
        for k in range(1, N_DEV):
            g_all = g_all + p_ref[k]
        for idx, name in enumerate(SMALL_WEIGHTS):
            if name == "ada_b":
                g = jnp.concatenate([jnp.concatenate([g_all[3 * l + t:3 * l + t + 1] for t in range(3)], axis=1) for l in range(2)],
                                    axis=0)
            else:
                row, width = SMALL_ROWS[name]
                g = g_all[row:row + w_refs[idx].shape[0], 0:width]
            d, m_new, v_new = _adam_update(g, w_refs[idx][...], m_refs[idx][...], v_refs[idx][...])
            outs[4 * idx][...], outs[4 * idx + 1][...], outs[4 * idx + 2][...], outs[4 * idx + 3][...] = g, d, m_new, v_new
        outs[4 * nw][...] = g_all[LOSS_ROW:LOSS_ROW + 1, 0:LANES]

    out_shape = []
    for w in ws:
        out_shape += [_sds(w.shape, F32)] * 4
    out_shape.append(_sds((1, LANES), F32))
    return pl.pallas_call(body, name="adam_small", out_shape=out_shape,
                          compiler_params=pltpu.CompilerParams(vmem_limit_bytes=VMEM_LIMIT))(parts, *ws, *ms, *vs)


def _adam(parts, w, m, v, name, by_columns=False):
    P, R, C = parts.shape
    if by_columns:
        tr, tc = R, 256
    else:
        tr, tc = (R if R <= ADAM_ROW_TILE else ADAM_ROW_TILE), C
    assert R % tr == 0 and C % tc == 0

    def body(p_ref, w_ref, m_ref, v_ref, g_ref, d_ref, nm_ref, nv_ref):
        g = p_ref[0].astype(F32)
        for k in range(1, P):
            g = g + p_ref[k].astype(F32)
        g_ref[...] = g
        d_ref[...], nm_ref[...], nv_ref[...] = _adam_update(g, w_ref[...], m_ref[...], v_ref[...])

    tile = (lambda i: (0, i)) if by_columns else (lambda i: (i, 0))
    spec = pl.BlockSpec((tr, tc), tile)
    return pl.pallas_call(
        body, name=name, grid=(C // tc if by_columns else R // tr,),
        in_specs=[pl.BlockSpec((P, tr, tc), lambda i: (0,) + tile(i)), spec, spec, spec],
        out_specs=[spec, spec, spec, spec], out_shape=[_sds((R, C), F32)] * 4,
        compiler_params=_params(("arbitrary",)),
    )(parts, w, m, v)


_ANY = pl.BlockSpec(memory_space=pl.ANY)
CHIP_FLIPS = ((1, 0), (0, 1), (1, 1))
DEV_FLIPS = tuple((dx, dy, dc) for dx in (0, 1) for dy in (0, 1) for dc in (0, 1) if dx + dy + dc)


def _flip(a, d):
    return a if d == 0 else 1 - a


def _my_place():
    return lax.axis_index("x"), lax.axis_index("y"), lax.axis_index("c")


def _gather8_copies(ins, outs, send_sems, recv_sems, loc_sems):
    x, y, c = _my_place()
    me = 4 * x + 2 * y + c
    copies = []
    for a in range(len(ins)):
        copies.append(pltpu.make_async_copy(ins[a], outs[a].at[me], loc_sems.at[a]))
        for k, (dx, dy, dc) in enumerate(DEV_FLIPS):
            copies.append(pltpu.make_async_remote_copy(
                src_ref=ins[a], dst_ref=outs[a].at[me], send_sem=send_sems.at[a, k], recv_sem=recv_sems.at[a, k],
                device_id=(_flip(x, dx), _flip(y, dy), _flip(c, dc)), device_id_type=MESH_ID))
    return copies


def _gather8_sems(n):
    return [pltpu.SemaphoreType.DMA((n, 7)), pltpu.SemaphoreType.DMA((n, 7)), pltpu.SemaphoreType.DMA((n,))]


def _gather_dev8(arrs, name):
    n = len(arrs)

    def body(*refs):
        copies = _gather8_copies(refs[:n], refs[n:2 * n], *refs[2 * n:])
        for cp in copies:
            cp.start()
        for cp in copies:
            cp.wait()

    return pl.pallas_call(
        body, name=name, in_specs=[_ANY] * n, out_specs=[_ANY] * n,
        out_shape=[_sds((N_DEV,) + a.shape, a.dtype) for a in arrs], scratch_shapes=_gather8_sems(n),
    )(*arrs)


class _Exchange:
    def __init__(self, arrs, out_shapes, n_sems, phases):
        self.arrs, self.out_shapes, self.n_sems, self._phases = list(arrs), list(out_shapes), n_sems, phases

    @property
    def n(self):
        return len(self.arrs)

    def sem_shapes(self):
        return [pltpu.SemaphoreType.DMA((self.n, self.n_sems)), pltpu.SemaphoreType.DMA((self.n, self.n_sems)),
                pltpu.SemaphoreType.DMA((self.n,))]

    def phases(self, ins, outs, sems):
        return self._phases(ins, outs, *sems)

    def run(self, name):
        n = self.n

        def body(*refs):
            start, mid, end = self.phases(refs[:n], refs[n:2 * n], refs[2 * n:])
            start()
            mid()
            end()

        return pl.pallas_call(body, name=name, in_specs=[_ANY] * n, out_specs=[_ANY] * n, out_shape=self.out_shapes,
                              scratch_shapes=self.sem_shapes())(*self.arrs)

def _gather_halves_phases(ins, outs, send_sems, recv_sems, loc_sems):
    n = len(ins)
    x, y, c = _my_place()
    chip = 2 * x + y
    sibling = (x, y, 1 - c)
    peers = [(_flip(x, dx), _flip(y, dy)) for dx, dy in CHIP_FLIPS]

    def remote(src, p, half, a, k, to):
        return pltpu.make_async_remote_copy(src_ref=src, dst_ref=outs[a].at[p, half], send_sem=send_sems.at[a, k],
                                            recv_sem=recv_sems.at[a, k], device_id=to, device_id_type=MESH_ID)

    def local(a):
        return pltpu.make_async_copy(ins[a], outs[a].at[chip], loc_sems.at[a])

    def first(a, k):
        return remote(ins[a].at[c], chip, c, a, k, (*peers[k], c))

    def passed(a, k):
        p = 2 * peers[k][0] + peers[k][1]
        return remote(outs[a].at[p, c], p, c, a, 3 + k, sibling)

    def start():
        for a in range(n):
            local(a).start()
            for k in range(3):
                first(a, k).start()

    def mid():
        for a in range(n):
            for k in range(3):
                p = 2 * peers[k][0] + peers[k][1]
                remote(outs[a].at[p, c], p, c, a, k, sibling).wait_recv()
                passed(a, k).start()

    def end():
        for a in range(n):
            for k in range(3):
                p = 2 * peers[k][0] + peers[k][1]
                remote(outs[a].at[p, 1 - c], p, 1 - c, a, 3 + k, sibling).wait_recv()
        for a in range(n):
            for k in range(3):
                first(a, k).wait_send()
                passed(a, k).wait_send()
            local(a).wait()

    return start, mid, end


def _gather_chip4_halves(arrs):
    return _Exchange(arrs, [_sds((N_CHIPS,) + a.shape, a.dtype) for a in arrs], 6, _gather_halves_phases)


def _reduce_phases(n_whole, ins, outs, send_sems, recv_sems, loc_sems):
    n = len(ins)
    x, y, c = _my_place()
    chip = 2 * x + y
    sibling = (x, y, 1 - c)
    peers = [(_flip(x, dx), _flip(y, dy)) for dx, dy in CHIP_FLIPS]

    def remote(src, slot, a, k, to):
        return pltpu.make_async_remote_copy(src_ref=src, dst_ref=outs[a].at[slot], send_sem=send_sems.at[a, k],
                                            recv_sem=recv_sems.at[a, k], device_id=to, device_id_type=MESH_ID)

    def block(a, p):
        return ins[a] if a >= n - n_whole else ins[a].at[p]

    def local(a):
        return pltpu.make_async_copy(block(a, chip), outs[a].at[2 * chip + c], loc_sems.at[a])

    def own(a):
        return remote(block(a, chip), 2 * chip + c, a, 0, sibling)

    def first(a, k):
        return remote(block(a, 2 * peers[k][0] + peers[k][1]), 2 * chip + c, a, 1 + k, (*peers[k], c))

    def passed(a, k):
        slot = 2 * (2 * peers[k][0] + peers[k][1]) + c
        return remote(outs[a].at[slot], slot, a, 4 + k, sibling)

    def start():
        for a in range(n):
            local(a).start()
            own(a).start()
            for k in range(3):
                first(a, k).start()

    def mid():
        for a in range(n):
            for k in range(3):
                slot = 2 * (2 * peers[k][0] + peers[k][1]) + c
                remote(outs[a].at[slot], slot, a, 1 + k, sibling).wait_recv()
                passed(a, k).start()

    def end():
        for a in range(n):
            remote(outs[a].at[2 * chip + 1 - c], 2 * chip + 1 - c, a, 0, sibling).wait_recv()
            for k in range(3):
                slot = 2 * (2 * peers[k][0] + peers[k][1]) + 1 - c
                remote(outs[a].at[slot], slot, a, 4 + k, sibling).wait_recv()
        for a in range(n):
            own(a).wait_send()
            for k in range(3):
                first(a, k).wait_send()
                passed(a, k).wait_send()
            local(a).wait()

    return start, mid, end


def _reduce_exchange(arrs, whole=()):
    shapes = [_sds((N_DEV,) + a.shape[1:], a.dtype) for a in arrs] + [_sds((N_DEV,) + a.shape, a.dtype) for a in whole]
    return _Exchange(list(arrs) + list(whole), shapes, 7, functools.partial(_reduce_phases, len(whole)))


def _shard_halves_t(w):
    wt = w.T.astype(BF16)
    n2 = wt.shape[0] // 2
    pad = jnp.zeros((-n2 % 16, wt.shape[1]), BF16)
    return jnp.stack([jnp.concatenate([wt[0:n2], pad], axis=0), jnp.concatenate([wt[n2:], pad], axis=0)])


def _gathered_rows(g, n):
    return [g[p, half, 0:n // 2] for p in range(N_CHIPS) for half in range(2)]


def _even_in_layout_t(g):
    n2 = EVEN_IN // N_CHIPS // 2
    gap, gap_rows = EVEN_GAP, EVEN_P - EVEN_IN
    assert n2 % 2 == 0 and gap % 2 == 0 and gap_rows % 2 == 0
    spans = []
    for p in range(N_CHIPS):
        for half in range(2):
            lo = (2 * p + half) * n2
            if lo < gap < lo + n2:
                spans += [(p, half, 0, (gap - lo) // 2, lo // 2), (p, half, (gap - lo) // 2, (lo + n2 - gap) // 2, (gap + gap_rows) // 2)]
            else:
                spans.append((p, half, 0, n2 // 2, (lo + (gap_rows if lo >= gap else 0)) // 2))

    def body(g_ref, o_ref, pairs):
        pairs[pl.ds(gap // 2, gap_rows // 2), :] = jnp.zeros((gap_rows // 2, pairs.shape[1]), jnp.uint32)
        for p, half, src, rows, dst in spans:
            pairs[pl.ds(dst, rows), :] = pltpu.bitcast(g_ref[p, half], jnp.uint32)[src:src + rows]
        step = 128
        for r0 in range(0, EVEN_P // 2, step):
            o_ref[pl.ds(2 * r0, 2 * step), :] = pltpu.bitcast(pairs[pl.ds(r0, step), :], o_ref.dtype)

    return pl.pallas_call(body, name="even_w_in_rows", out_shape=_sds((EVEN_P, g.shape[-1]), g.dtype),
                          scratch_shapes=[pltpu.VMEM((EVEN_P // 2, g.shape[-1]), jnp.uint32)],
                          compiler_params=pltpu.CompilerParams(vmem_limit_bytes=VMEM_LIMIT))(g)


def _uq_layout_t(g):
    wt = jnp.concatenate(_gathered_rows(g, B_HEADS * (B_NOPE + B_ROPE) // N_CHIPS), axis=0)
    per = B_NOPE + B_ROPE
    pad = jnp.zeros((LANES - B_ROPE, wt.shape[1]), wt.dtype)
    nope = [wt[per * h:per * h + B_NOPE] for h in range(B_HEADS)]
    rope = [jnp.concatenate([wt[per * h + B_NOPE:per * (h + 1)], pad], axis=0) for h in range(B_HEADS)]
    return jnp.concatenate(nope + rope, axis=0)


def _block_diag(blocks):
    rows = []
    for h, blk in enumerate(blocks):
        r, cdim = blk.shape
        n = len(blocks)
        rows.append(jnp.concatenate([jnp.zeros((r, cdim * h), blk.dtype), blk, jnp.zeros((r, cdim * (n - 1 - h)), blk.dtype)],
                                    axis=1))
    return jnp.concatenate(rows, axis=0)


def _uk_layout(w):
    return _block_diag([w[:, h, :].T for h in range(B_HEADS)])


def _latent_rows(w):
    return jnp.transpose(w[0], (1, 2, 0)).reshape(-1, w.shape[1])


def _latent_unrows(w2, shape):
    return jnp.transpose(w2.reshape(shape[2], shape[3], shape[1]), (2, 0, 1)).reshape(shape)


def _uv_layout(w):
    return _block_diag([w[:, h, :] for h in range(B_HEADS)])


def _uv_unlayout(g):
    return jnp.concatenate([g[LANES * h:LANES * (h + 1), B_V * h:B_V * (h + 1)].T for h in range(B_HEADS)], axis=0)


def _rope_tables(S):
    inv = ROPE_THETA ** (-jnp.arange(0, 32, 2, dtype=F32) / 32)
    tok = jnp.arange(S)

    def tab(pos):
        ang = pos.astype(F32)[:, None] * inv[None, :]
        cos, sin = jnp.cos(ang), jnp.sin(ang)
        return jnp.concatenate([cos, cos], axis=1), jnp.concatenate([-sin, sin], axis=1)

    cr, sr = tab(tok // GRID_W)
    cc, sc = tab(tok % GRID_W)
    ct, st = tab(tok)
    return (jnp.tile(jnp.concatenate([cr, cc], axis=1), (1, 2)), jnp.tile(jnp.concatenate([sr, sc], axis=1), (1, 2)),
            jnp.tile(ct, (1, 4)), jnp.tile(st, (1, 4)))


A_TQ, A_TK, A_SUB = 512, 4096, 512
A_FWD_SUB = 1024
B_TQ, B_TK, B_SUB = 128, 4096, 1024
B_BWD_TK, B_BWD_SUB = 4096, 512
C_T = 256
C_BLOCKS_PER_STEP = 8
KV_SHARE = 2


def _local_step(x0, tgt, mod, norm_w, wie, wuq, wuk, wuv, late_shards, a_q_norm, a_k_norm, q_lora_norm, kv_lora_norm,
                c_sink, final_norm):
    S = x0.shape[0]
    mod3 = mod.reshape(2, 3, D_MODEL)
    ca, sa, ct, st = _rope_tables(S)
    lane_seg = np.arange(LANES) // HEAD_DIM
    seg = jnp.asarray((lane_seg[:, None] == lane_seg[None, :]).astype(np.float32)).astype(BF16)
    qn = jnp.tile(a_q_norm.reshape(1, HEAD_DIM), (1, 2))
    kn = jnp.tile(a_k_norm.reshape(1, HEAD_DIM), (1, 2))
    qln, kvln = q_lora_norm.reshape(1, B_Q_LORA), kv_lora_norm.reshape(1, B_KV_LORA)
    nw0, nw1 = norm_w[0:1], norm_w[1:2]
    gate0, gate1 = mod3[0, 2:3], mod3[1, 2:3]
    a_tq, a_tk, b_tq, b_tk, bb_tk, c_t = min(A_TQ, S), min(A_TK, S), min(B_TQ, S), min(B_TK, S), min(B_BWD_TK, S), min(C_T, S)
    a_sub, b_sub, bb_sub = min(A_SUB, a_tk), min(B_SUB, b_tk), min(B_BWD_SUB, bb_tk)

    h0, proj_e, qa, ka, va, qcat, kcat, ka_t, va_t, kcat_t = _even_pre_fwd(x0, mod3[0], nw0, wie, qn, kn, seg, ca, sa, ct, st,
                                                                           qln, kvln, wuq, wuk)
    oa, lse_a, woe_g, wio_g, woo_g = _pp_fwd(qa, ka, va_t, kdiv=KV_SHARE, tq=a_tq, tk=a_tk, sub=min(A_FWD_SUB, a_tk), name="attn_a_fwd",
                                             side=_gather_chip4_halves(late_shards))
    woe = woe_g.reshape(D_MODEL, D_MODEL)
    wio = wio_g.reshape(N_CHIPS, D_MODEL, ODD_IN // N_CHIPS)
    woo = woo_g.reshape(D_MODEL, D_MODEL)
    olat, lse_b = _mla_fwd(qcat, kcat, kcat_t, tq=b_tq, tk=b_tk, sub=b_sub)
    y0, x1 = _even_post_fwd(oa, olat, proj_e, x0, gate0, wuv, woe)
    h1, gc, qc, kc, vc, kc_t, vc_t = _odd_pre_fwd(x1, mod3[1], nw1, wio)
    slopes = 2.0 ** (-8.0 * jnp.arange(1, C_HEADS + 1, dtype=F32) / C_HEADS)
    slope_rows = jnp.repeat(slopes.reshape(C_HEADS // 2, 2), c_t, axis=1)[:, None, :]
    sink_rows = jnp.repeat(c_sink.reshape(C_HEADS // 2, 2), c_t, axis=1)[:, None, :]
    win_dist = _win_dist_table(S, c_t)
    oc, lse_c = _win_fwd(qc, kc, vc_t, win_dist, slope_rows, sink_rows, kdiv=KV_SHARE, tq=c_t, nbs=C_BLOCKS_PER_STEP,
                         name="attn_c_fwd")
    doc, dgc, dx2, dwoo, st_f = _odd_post(oc, gc, x1, gate1, woo, final_norm.reshape(1, D_MODEL), tgt)
    dqc, dkc, dvc, dsink_raw = _win_bwd(qc, kc, kc_t, vc, oc, doc, lse_c, win_dist, slope_rows, sink_rows, kdiv=KV_SHARE, tq=c_t,
                                        nbs=C_BLOCKS_PER_STEP, name="attn_c_bwd")
    dx1, dwio, st_1 = _odd_pre_bwd(dqc, dkc, dvc, dgc, h1, x1, dx2, mod3[1], nw1, wio)
    doa, dga, dgb, dolat, dwoe, dwuv, st_e = _even_post_bwd(dx1, y0, oa, olat, proj_e, gate0, wuv, woe)
    late_grads = _reduce_exchange([dwoe.reshape(N_CHIPS, D_MODEL // N_CHIPS, D_MODEL), dwio,
                                   dwoo.reshape(N_CHIPS, D_MODEL // N_CHIPS, D_MODEL)], whole=[_uv_unlayout(dwuv).astype(BF16)])
    dqa, dka, dva, p_woe, p_wio, p_woo, p_wuv = _pp_bwd(qa, ka, ka_t, va, oa, doa, lse_a, kdiv=KV_SHARE, tq=a_tq, tk=a_tk,
                                                        sub=a_sub, name="attn_a_bwd", side=late_grads)
    dqcat, dkcat = _mla_bwd(qcat, kcat, kcat_t, olat, dolat, lse_b, tq=b_tq, tk=bb_tk, sub=bb_sub)
    dx0, dwie, dwuq, dwuk, st_0, nst = _even_pre_bwd(x0, h0, proj_e, dqa, dka, dva, dga, dgb, dqcat, dkcat, dx1, mod3[0], nw0,
                                                     wie, qn, kn, seg, ca, sa, ct, st, qln, kvln, wuq, wuk)
    dsink_pairs = jnp.stack([dsink_raw[:, 0, 0], dsink_raw[:, 1, 0]], axis=1).reshape(C_HEADS)
    return dict(
        loss_row=st_f[2:3], dx=dx0,
        dmod=jnp.stack([jnp.concatenate([st_0[0], st_0[1], st_e[0]]), jnp.concatenate([st_1[0], st_1[1], st_f[1]])]),
        norm_w=jnp.stack([st_0[2], st_1[2]]), final_norm=st_f[0],
        a_q_norm=nst[0:1, 0:HEAD_DIM], a_k_norm=nst[1:2, 0:HEAD_DIM], b_q_lora_norm=nst[2:3, :], b_kv_lora_norm=nst[3:4, 0:LANES],
        c_sink=dsink_pairs.reshape(1, C_HEADS),
        even_w_in=dwie, b_w_uq=dwuq, b_w_uk=dwuk, b_w_uv=p_wuv, even_w_out=p_woe, odd_w_in=p_wio, odd_w_out=p_woo)


WEIGHT_NAMES = ("norm_w", "ada_w", "ada_b", "even_w_in", "a_q_norm", "a_k_norm", "b_q_lora_norm", "b_kv_lora_norm", "b_w_uq",
                "b_w_uk", "b_w_uv", "even_w_out", "odd_w_in", "c_sink", "odd_w_out", "final_norm")


def kernel(x, c, norm_w, ada_w, ada_b, even_w_in, a_q_norm, a_k_norm, b_q_lora_norm, b_kv_lora_norm, b_w_uq, b_w_uk, b_w_uv, even_w_out, odd_w_in, c_sink, odd_w_out, final_norm, loss_target, m_norm_w, m_ada_w, m_ada_b, m_even_w_in, m_a_q_norm, m_a_k_norm, m_b_q_lora_norm, m_b_kv_lora_norm, m_b_w_uq, m_b_w_uk, m_b_w_uv, m_even_w_out, m_odd_w_in, m_c_sink, m_odd_w_out, m_final_norm, v_norm_w, v_ada_w, v_ada_b, v_even_w_in, v_a_q_norm, v_a_k_norm, v_b_q_lora_norm, v_b_kv_lora_norm, v_b_w_uq, v_b_w_uk, v_b_w_uv, v_even_w_out, v_odd_w_in, v_c_sink, v_odd_w_out, v_final_norm):
    given = dict(locals())
    xi, yi, ci = _my_place()
    chip = 2 * xi + yi
    dev = 2 * chip + ci
    n_ada = ada_w.shape[2]

    (c_all,) = _gather_dev8([c], "gather_c")
    c_all = c_all.reshape(N_DEV, D_MODEL)
    bias = lax.dynamic_slice_in_dim(ada_b, chip * n_ada, n_ada, axis=1).reshape(2, 1, n_ada)
    mod_cols = _ada_fwd(c_all, ada_w, bias)
    def halves(w):
        return w.astype(BF16).reshape((2, w.shape[0] // 2) + w.shape[1:])

    mod_all, wie_g, wuq_g = _gather_chip4_halves(
        [mod_cols, _shard_halves_t(even_w_in[0]), _shard_halves_t(b_w_uq[0])]).run("gather_weights")
    mod = jnp.transpose(lax.dynamic_index_in_dim(mod_all, dev, axis=2, keepdims=False), (1, 0, 2)).reshape(2, 3 * D_MODEL)

    res = _local_step(
        x[0], loss_target[0], mod, norm_w,
        _even_in_layout_t(wie_g), _uq_layout_t(wuq_g), _uk_layout(b_w_uk[0].astype(BF16)),
        _uv_layout(b_w_uv[0].astype(BF16)), [halves(even_w_out[0]), halves(odd_w_in[0]), halves(odd_w_out[0])],
        a_q_norm, a_k_norm, b_q_lora_norm, b_kv_lora_norm, c_sink, final_norm)

    p_wie, p_wuq, small_all, p_wuk = _reduce_exchange(
        [res["even_w_in"].reshape(N_CHIPS, EVEN_IN // N_CHIPS, D_MODEL),
         res["b_w_uq"].reshape(N_CHIPS, -1, B_Q_LORA)],
        whole=[_pack_small(res), res["b_w_uk"]],
    ).run("reduce_exchange")
    shard_parts = dict(even_w_in=p_wie, b_w_uq=p_wuq, **{k: res[k] for k in ("even_w_out", "odd_w_in", "odd_w_out")})
    dmod_all = small_all[:, 0:6, :].reshape(N_DEV, 2, 3 * D_MODEL)
    dmod_cols = jnp.transpose(lax.dynamic_slice_in_dim(dmod_all, chip * n_ada, n_ada, axis=2), (1, 0, 2))
    parts = dict(shard_parts)
    parts["ada_w"] = _ada_bwd(c_all.T, dmod_cols).reshape(1, 2 * D_MODEL, n_ada)
    parts["b_w_uk"], parts["b_w_uv"] = p_wuk, res["b_w_uv"]

    def as2d(a):
        return a.reshape((-1, a.shape[-1]) if a.ndim > 1 else (1, a.shape[0]))

    results = {}
    small_outs = _adam_small(small_all, *[[as2d(given[pre + k]) for k in SMALL_WEIGHTS] for pre in ("", "m_", "v_")])
    for idx, k in enumerate(SMALL_WEIGHTS):
        results[k] = small_outs[4 * idx:4 * idx + 4]
    for k, p in parts.items():
        if k in ("even_w_in", "b_w_uq"):
            outs = _adam(p, given[k][0].T, given["m_" + k][0].T, given["v_" + k][0].T, "adam_" + k, by_columns=k == "even_w_in")
            results[k] = [o.T for o in outs]
            continue
        if k in ("b_w_uk", "b_w_uv"):
            outs = _adam(p, _latent_rows(given[k]), _latent_rows(given["m_" + k]), _latent_rows(given["v_" + k]), "adam_" + k)
            results[k] = [_latent_unrows(o, given[k].shape) for o in outs]
            continue
        shape2 = (p.shape[-2], p.shape[-1])
        results[k] = _adam(p, given[k].reshape(shape2), given["m_" + k].reshape(shape2), given["v_" + k].reshape(shape2),
                           "adam_" + k)
    by_kind = [[results[k][t].reshape(given[k].shape) for k in WEIGHT_NAMES] for t in range(4)]
    return (small_outs[-1][0, 0], res["dx"][None], *by_kind[0], *by_kind[1], *by_kind[2], *by_kind[3])
```

```python
import functools

import numpy as np
import jax
import jax.numpy as jnp
from jax import lax
from jax.experimental import pallas as pl
from jax.experimental.pallas import tpu as pltpu

F32 = jnp.float32
BF16 = jnp.bfloat16
HIGHEST = lax.Precision.HIGHEST
MESH_ID = pl.DeviceIdType.MESH

D_MODEL = 1024
HEAD_DIM = 64
GRID_W = 64
EPS = 1e-6
ROPE_THETA = 10000.0
B_HEADS, B_NOPE, B_ROPE, B_V = 8, 64, 32, 64
B_Q_LORA, B_KV_LORA = 256, 128
C_HEADS = 16
WINDOW = 128
EVEN_IN, ODD_IN = 2208, 2560
EVEN_P = 2304
EVEN_GAP = 1696
N_CHIPS, N_DEV = 4, 8
LANES = 128
NEG = -1e30
VMEM_LIMIT = 60 * 1024 * 1024

ADAM_LR, ADAM_B1, ADAM_B2, ADAM_EPS, ADAM_WD, ADAM_STEP = 0.001, 0.9, 0.999, 1e-08, 0.01, 10

ROW_TILE = 512
IN_PROJ_ROW_TILE = 256


def _dot(a, b):
    return lax.dot_general(a, b, (((1,), (0,)), ((), ())), preferred_element_type=F32)


def _dot_nt(a, b):
    return lax.dot_general(a, b, (((1,), (1,)), ((), ())), preferred_element_type=F32)


def _dot_tn(a, b):
    return lax.dot_general(a, b, (((0,), (0,)), ((), ())), preferred_element_type=F32)


def _dot_f32(a, b):
    return lax.dot_general(a, b, (((1,), (0,)), ((), ())), precision=HIGHEST, preferred_element_type=F32)


def _sigmoid(x):
    return 1.0 / (1.0 + jnp.exp(-x))


def _silu_and_grad(g):
    s = _sigmoid(g)
    return g * s, s * (1.0 + g * (1.0 - s))


def _lane_iota():
    return lax.broadcasted_iota(jnp.int32, (1, LANES), 1)


def _partner(x, lane):
    return jnp.where((lane % 32) < 16, pltpu.roll(x, LANES - 16, 1), pltpu.roll(x, 16, 1))


def _rot(x, cos, sin_signed, lane):
    return x * cos + _partner(x, lane) * sin_signed


def _rot_bwd(dy, cos, sin_signed, lane):
    return dy * cos + _partner(dy * sin_signed, lane)


def _rms(x):
    return lax.rsqrt(jnp.mean(x * x, axis=-1, keepdims=True) + EPS)


def _rms_bwd(x, r, g):
    return r * g - x * (r * r * r) * jnp.mean(x * g, axis=-1, keepdims=True)


def _seg_mean(v, seg_ones):
    hi = v.astype(BF16)
    lo = (v - hi.astype(F32)).astype(BF16)
    return (_dot(hi, seg_ones) + _dot(lo, seg_ones)) * (1.0 / HEAD_DIM)


def _dup_heads(x, lane):
    swapped = pltpu.roll(x, HEAD_DIM, 1)
    lo = lane < HEAD_DIM
    return jnp.concatenate([jnp.where(lo, x, swapped), jnp.where(lo, swapped, x)], axis=1)


def _fold_heads(x2, lane):
    a, b = x2[:, 0:LANES], x2[:, LANES:2 * LANES]
    return jnp.where(lane < HEAD_DIM, a + pltpu.roll(a, HEAD_DIM, 1), b + pltpu.roll(b, HEAD_DIM, 1))


def _row_spec(ts, cols):
    return pl.BlockSpec((ts, cols), lambda i: (i, 0))


def _full_spec(shape, single=True):
    nd = len(shape)
    if single:
        return pl.BlockSpec(shape, lambda i: (0,) * nd, pipeline_mode=pl.Buffered(1))
    return pl.BlockSpec(shape, lambda i: (0,) * nd)


def _sds(shape, dtype):
    return jax.ShapeDtypeStruct(shape, dtype)


def _params(sem):
    return pltpu.CompilerParams(dimension_semantics=sem, vmem_limit_bytes=VMEM_LIMIT)


def _even_pre_fwd(x, mod, nw, wie, qn, kn, seg, ca, sa, ct, st, qln, kvln, wuq, wuk):
    S = x.shape[0]
    ts = min(IN_PROJ_ROW_TILE, S)

    def body(x_ref, mod_ref, nw_ref, wie_ref, qn_ref, kn_ref, seg_ref, ca_ref, sa_ref, ct_ref, st_ref, qln_ref,
             kvln_ref, wuq_ref, wuk_ref, h_ref, proj_ref, qa_ref, ka_ref, va_ref, qcat_ref, kcat_ref, kat_ref, vat_ref, kcatt_ref):
        xv = x_ref[...]
        h = (xv * _rms(xv) * nw_ref[...]) * (1.0 + mod_ref[1:2, :]) + mod_ref[0:1, :]
        hb = h.astype(BF16)
        h_ref[...] = hb
        proj = _dot_nt(hb, wie_ref[...])
        proj_ref[...] = proj
        lane = _lane_iota()
        ca_v, sa_v, ct_v, st_v = ca_ref[...], sa_ref[...], ct_ref[...], st_ref[...]
        seg_v = seg_ref[...]
        for cb in range(4):
            xc = proj[:, LANES * cb:LANES * (cb + 1)]
            r = lax.rsqrt(_seg_mean(xc * xc, seg_v) + EPS)
            y = _rot(xc * r * qn_ref[...], ca_v, sa_v, lane)
            qa_ref[:, LANES * cb:LANES * (cb + 1)] = (y * 0.125).astype(BF16)
        kc = proj[:, 512:640]
        r = lax.rsqrt(_seg_mean(kc * kc, seg_v) + EPS)
        ka_v = _dup_heads(_rot(kc * r * kn_ref[...], ca_v, sa_v, lane), lane)
        ka_ref[...] = ka_v.astype(BF16)
        kat_ref[...] = ka_v.T.astype(BF16)
        va_v = _dup_heads(proj[:, 640:768], lane)
        va_ref[...] = va_v.astype(BF16)
        vat_ref[...] = va_v.T.astype(BF16)
        cq = proj[:, 1280:1536]
        cqn = (cq * _rms(cq) * qln_ref[...]).astype(BF16)
        ckv = proj[:, 1536:1664]
        ckvn = ckv * _rms(ckv) * kvln_ref[...]
        qb = _dot_nt(cqn, wuq_ref[...])
        qlat = _dot(qb[:, 0:512].astype(BF16), wuk_ref[...])
        for hh in range(B_HEADS):
            qcat_ref[hh, :, 0:LANES] = qlat[:, LANES * hh:LANES * (hh + 1)].astype(BF16)
            qr = _rot(qb[:, 512 + LANES * hh:512 + LANES * (hh + 1)], ct_v, st_v, lane)
            qcat_ref[hh, :, LANES:2 * LANES] = qr.astype(BF16)
        kr = _rot(proj[:, 1664:1792], ct_v, st_v, lane)
        kcat_ref[:, 0:LANES] = ckvn.astype(BF16)
        kcat_ref[:, LANES:2 * LANES] = kr.astype(BF16)
        kcatt_ref[0:LANES, :] = ckvn.T.astype(BF16)
        kcatt_ref[LANES:2 * LANES, :] = kr.T.astype(BF16)

    col_spec = lambda rows: pl.BlockSpec((rows, ts), lambda i: (0, i))
    return pl.pallas_call(
        body, name="even_pre_fwd", grid=(S // ts,),
        in_specs=[_row_spec(ts, D_MODEL), _full_spec((3, D_MODEL)), _full_spec((1, D_MODEL)), _full_spec((EVEN_P, D_MODEL)),
                  _full_spec((1, LANES)), _full_spec((1, LANES)), _full_spec((LANES, LANES)),
                  _row_spec(ts, LANES), _row_spec(ts, LANES), _row_spec(ts, LANES), _row_spec(ts, LANES),
                  _full_spec((1, B_Q_LORA)), _full_spec((1, B_KV_LORA)), _full_spec((1536, B_Q_LORA)), _full_spec((512, 1024))],
        out_specs=[_row_spec(ts, D_MODEL), _row_spec(ts, EVEN_P), _row_spec(ts, 512), _row_spec(ts, 2 * LANES), _row_spec(ts, 2 * LANES),
                   pl.BlockSpec((B_HEADS, ts, 2 * LANES), lambda i: (0, i, 0)), _row_spec(ts, 2 * LANES),
                   col_spec(2 * LANES), col_spec(2 * LANES), col_spec(2 * LANES)],
        out_shape=[_sds((S, D_MODEL), BF16), _sds((S, EVEN_P), F32), _sds((S, 512), BF16), _sds((S, 2 * LANES), BF16),
                   _sds((S, 2 * LANES), BF16), _sds((B_HEADS, S, 2 * LANES), BF16), _sds((S, 2 * LANES), BF16),
                   _sds((2 * LANES, S), BF16), _sds((2 * LANES, S), BF16), _sds((2 * LANES, S), BF16)],
        compiler_params=_params(("arbitrary",)),
    )(x, mod, nw, wie, qn, kn, seg, ca, sa, ct, st, qln, kvln, wuq, wuk)


MLA_SCALE = (B_NOPE + B_ROPE) ** -0.5
LOG2E = 1.4426950408889634

def _row_lo():
    return lax.broadcasted_iota(jnp.int32, (LANES, 1), 0) < HEAD_DIM


def _stack_cols(vT, rlo):
    zero = jnp.zeros_like(vT)
    return jnp.concatenate([jnp.where(rlo, vT, zero), jnp.where(rlo, zero, vT)], axis=1)


def _stack_rows(v, lo):
    zero = jnp.zeros_like(v)
    return jnp.concatenate([jnp.where(lo, v, zero), jnp.where(lo, zero, v)], axis=0)


def _pick_halves_T(xT, rlo, t):
    return jnp.where(rlo, xT[:, 0:t], xT[:, t:2 * t]).T


def _side_split(refs, n_in, n_out, n_scratch, side):
    ns = side.n if side is not None else 0
    cuts = np.cumsum([0, n_in, ns, n_out, ns, n_scratch])
    return [refs[a:b] for a, b in zip(cuts[:-1], cuts[1:])] + [refs[cuts[-1]:]]


def _side_hooks(side, side_ins, side_outs, side_sems, step, total):
    if side is None:
        return lambda: None
    start, mid, end = side.phases(side_ins, side_outs, side_sems)
    pl.when(step == 0)(start)
    pl.when(step == total // 2)(mid)
    return lambda: pl.when(step == total - 1)(end)


def _side_specs(side):
    if side is None:
        return [], [], [], [], []
    return list(side.arrs), [_ANY] * side.n, [_ANY] * side.n, list(side.out_shapes), side.sem_shapes()


def _pp_fwd(q, k, vT, *, kdiv, tq, tk, sub, name, side=None):
    S = k.shape[0]; nb = q.shape[1] // LANES; nq = S // tq; nkv = S // tk; nsub = tk // sub

    def body(*refs):
        (q_ref, k_ref, vT_ref), side_ins, (o_ref, lse_ref), side_outs, (qs, m_s, l_s, acc), side_sems = _side_split(refs, 3, 2, 4, side)
        j = pl.program_id(2)
        rlo = _row_lo()
        step = (pl.program_id(0) * nq + pl.program_id(1)) * nkv + j
        side_end = _side_hooks(side, side_ins, side_outs, side_sems, step, nb * nq * nkv)

        @pl.when(j == 0)
        def _():
            qs[...] = _stack_cols(q_ref[...].astype(F32).T, rlo).astype(BF16)
            m_s[...] = jnp.full((1, 2 * tq), NEG, F32)
            l_s[...] = jnp.zeros((1, 2 * tq), F32)
            acc[...] = jnp.zeros((LANES, 2 * tq), F32)

        qsv = qs[...]
        m, l, a = m_s[...], l_s[...], acc[...]
        s_cur = _dot(k_ref[0:sub, :], qsv)
        for t in range(nsub):
            if t + 1 < nsub:
                s_next = _dot(k_ref[sub * (t + 1):sub * (t + 2), :], qsv)
            m_new = jnp.maximum(m, jnp.max(s_cur, axis=0, keepdims=True))
            alpha = jnp.exp(m - m_new)
            p = jnp.exp(s_cur - m_new)
            l = alpha * l + jnp.sum(p, axis=0, keepdims=True)
            a = alpha * a + _dot(vT_ref[:, sub * t:sub * (t + 1)], p.astype(BF16))
            m = m_new
            if t + 1 < nsub:
                s_cur = s_next
        m_s[...], l_s[...], acc[...] = m, l, a

        @pl.when(j == nkv - 1)
        def _():
            l_f = l_s[...]
            o_ref[...] = _pick_halves_T(acc[...] / l_f, rlo, tq).astype(BF16)
            lse_ref[0, 0] = m_s[...] + jnp.log(l_f)

        side_end()

    s_args, s_in, s_out, s_shapes, s_sems = _side_specs(side)
    return pl.pallas_call(
        body, name=name, grid=(nb, nq, nkv),
        in_specs=[pl.BlockSpec((tq, LANES), lambda b, i, j: (i, b)), pl.BlockSpec((tk, LANES), lambda b, i, j: (j, b // kdiv)),
                  pl.BlockSpec((LANES, tk), lambda b, i, j: (b // kdiv, j))] + s_in,
        out_specs=[pl.BlockSpec((tq, LANES), lambda b, i, j: (i, b)),
                   pl.BlockSpec((1, 1, 1, 2 * tq), lambda b, i, j: (b, i, 0, 0))] + s_out,
        out_shape=[_sds((S, nb * LANES), BF16), _sds((nb, nq, 1, 2 * tq), F32)] + s_shapes,
        scratch_shapes=[pltpu.VMEM((LANES, 2 * tq), BF16), pltpu.VMEM((1, 2 * tq), F32), pltpu.VMEM((1, 2 * tq), F32),
                        pltpu.VMEM((LANES, 2 * tq), F32)] + s_sems,
        compiler_params=_params(("arbitrary",) * 3))(q, k, vT, *s_args)


def _pp_bwd(q, k, kT, v, o, do, lse, *, kdiv, tq, tk, sub, name, side=None):
    S = k.shape[0]; nb = q.shape[1] // LANES; nkb = k.shape[1] // LANES; nq = S // tq; nkv = S // tk; nsub = tk // sub

    def body(*refs):
        ((q_ref, k_ref, kT_ref, v_ref, o_ref, do_ref, lse_ref), side_ins, (dq_ref, dk_ref, dv_ref), side_outs,
         (qsT, qs, dosT, dos, delta_s, dq_acc), side_sems) = _side_split(refs, 7, 3, 6, side)
        b, i, j = pl.program_id(0), pl.program_id(1), pl.program_id(2)
        rlo = _row_lo()
        lo = lax.broadcasted_iota(jnp.int32, (1, LANES), 1) < HEAD_DIM
        side_end = _side_hooks(side, side_ins, side_outs, side_sems, (b * nq + i) * nkv + j, nb * nq * nkv)

        @pl.when((b % kdiv == 0) & (i == 0) & (j == 0))
        def _():
            dk_ref[...] = jnp.zeros((S, LANES), F32)
            dv_ref[...] = jnp.zeros((S, LANES), F32)

        @pl.when(j == 0)
        def _():
            qv = q_ref[...]
            qs[...] = _stack_rows(qv, lo)
            qsT[...] = _stack_cols(qv.astype(F32).T, rlo).astype(BF16)
            dov = do_ref[...].astype(F32)
            dos[...] = _stack_rows(dov.astype(BF16), lo)
            dosT[...] = _stack_cols(dov.T, rlo).astype(BF16)
            prodT = (dov * o_ref[...].astype(F32)).T
            delta_s[...] = jnp.concatenate([jnp.sum(jnp.where(rlo, prodT, 0.0), axis=0, keepdims=True),
                                            jnp.sum(jnp.where(rlo, 0.0, prodT), axis=0, keepdims=True)], axis=1)
            dq_acc[...] = jnp.zeros((LANES, 2 * tq), F32)

        qsTv, dosTv, qsv, dosv = qsT[...], dosT[...], qs[...], dos[...]
        lse_v, delta_v = lse_ref[0, 0], delta_s[...]
        dqa = dq_acc[...]
        s_cur = _dot(k_ref[0:sub, :], qsTv)
        dp_cur = _dot(v_ref[0:sub, :], dosTv)
        for t in range(nsub):
            if t + 1 < nsub:
                s_next = _dot(k_ref[sub * (t + 1):sub * (t + 2), :], qsTv)
                dp_next = _dot(v_ref[sub * (t + 1):sub * (t + 2), :], dosTv)
            p = jnp.exp(s_cur - lse_v)
            ds = (p * (dp_cur - delta_v)).astype(BF16)
            rows = pl.ds(pl.multiple_of(j * tk + sub * t, sub), sub)
            dv_ref[rows, :] += _dot(p.astype(BF16), dosv)
            dk_ref[rows, :] += _dot(ds, qsv)
            dqa = dqa + _dot(kT_ref[:, sub * t:sub * (t + 1)], ds)
            if t + 1 < nsub:
                s_cur, dp_cur = s_next, dp_next
        dq_acc[...] = dqa

        @pl.when(j == nkv - 1)
        def _():
            dq_ref[...] = _pick_halves_T(dq_acc[...], rlo, tq)

        side_end()

    qmap = lambda b, i, j: (i, b)
    kmap = lambda b, i, j: (j, b // kdiv)
    res = lambda b, i, j: (0, b // kdiv)
    s_args, s_in, s_out, s_shapes, s_sems = _side_specs(side)
    return pl.pallas_call(
        body, name=name, grid=(nb, nq, nkv),
        in_specs=[pl.BlockSpec((tq, LANES), qmap), pl.BlockSpec((tk, LANES), kmap), pl.BlockSpec((LANES, tk), lambda b, i, j: (b // kdiv, j)),
                  pl.BlockSpec((tk, LANES), kmap), pl.BlockSpec((tq, LANES), qmap), pl.BlockSpec((tq, LANES), qmap),
                  pl.BlockSpec((1, 1, 1, 2 * tq), lambda b, i, j: (b, i, 0, 0))] + s_in,
        out_specs=[pl.BlockSpec((tq, LANES), qmap), pl.BlockSpec((S, LANES), res), pl.BlockSpec((S, LANES), res)] + s_out,
        out_shape=[_sds((S, nb * LANES), F32), _sds((S, nkb * LANES), F32), _sds((S, nkb * LANES), F32)] + s_shapes,
        scratch_shapes=[pltpu.VMEM((LANES, 2 * tq), BF16), pltpu.VMEM((2 * tq, LANES), BF16), pltpu.VMEM((LANES, 2 * tq), BF16),
                        pltpu.VMEM((2 * tq, LANES), BF16), pltpu.VMEM((1, 2 * tq), F32), pltpu.VMEM((LANES, 2 * tq), F32)] + s_sems,
        compiler_params=_params(("arbitrary",) * 3))(q, k, kT, v, o, do, lse, *s_args)


MLA_C = MLA_SCALE * LOG2E


def _mla_fwd(q, kcat, kcatT, *, tq, tk, sub):
    S = kcat.shape[0]; nq, nkv = S // tq, S // tk; R = B_HEADS * tq; nsub = tk // sub

    def body(q_ref, k_ref, vT_ref, o_ref, lse_ref, qT, m_s, l_s, acc):
        j = pl.program_id(1)

        @pl.when(j == 0)
        def _():
            qT[...] = q_ref[...].reshape(R, 2 * LANES).astype(F32).T.astype(BF16)
            m_s[...] = jnp.full((1, R), NEG, F32)
            l_s[...] = jnp.zeros((1, R), F32)
            acc[...] = jnp.zeros((LANES, R), F32)

        qTv = qT[...]
        m, l, a = m_s[...], l_s[...], acc[...]
        s_cur = _dot(k_ref[0:sub, :], qTv)
        for t in range(nsub):
            if t + 1 < nsub:
                s_next = _dot(k_ref[sub * (t + 1):sub * (t + 2), :], qTv)
            m_new = jnp.maximum(m, jnp.max(s_cur, axis=0, keepdims=True))
            alpha = jnp.exp2((m - m_new) * MLA_C)
            p = jnp.exp2((s_cur - m_new) * MLA_C)
            l = alpha * l + jnp.sum(p, axis=0, keepdims=True)
            a = alpha * a + _dot(vT_ref[:, sub * t:sub * (t + 1)], p.astype(BF16))
            m = m_new
            if t + 1 < nsub:
                s_cur = s_next
        m_s[...], l_s[...], acc[...] = m, l, a

        @pl.when(j == nkv - 1)
        def _():
            l_f = l_s[...]
            o_ref[...] = (acc[...] / l_f).T.reshape(B_HEADS, tq, LANES).astype(BF16)
            lse_ref[0] = m_s[...] * MLA_SCALE + jnp.log(l_f)

    return pl.pallas_call(
        body, name="mla_fwd", grid=(nq, nkv),
        in_specs=[pl.BlockSpec((B_HEADS, tq, 2 * LANES), lambda i, j: (0, i, 0)), pl.BlockSpec((tk, 2 * LANES), lambda i, j: (j, 0)),
                  pl.BlockSpec((LANES, tk), lambda i, j: (0, j))],
        out_specs=[pl.BlockSpec((B_HEADS, tq, LANES), lambda i, j: (0, i, 0)), pl.BlockSpec((1, 1, R), lambda i, j: (i, 0, 0))],
        out_shape=[_sds((B_HEADS, S, LANES), BF16), _sds((nq, 1, R), F32)],
        scratch_shapes=[pltpu.VMEM((2 * LANES, R), BF16), pltpu.VMEM((1, R), F32), pltpu.VMEM((1, R), F32), pltpu.VMEM((LANES, R), F32)],
        compiler_params=_params(("arbitrary", "arbitrary")))(q, kcat, kcatT)


def _mla_bwd(q, kcat, kcatT, o, do, lse, *, tq, tk, sub):
    S = kcat.shape[0]; nq, nkv = S // tq, S // tk; R = B_HEADS * tq; nsub = tk // sub

    def body(q_ref, k_ref, kT_ref, o_ref, do_ref, lse_ref, dq_ref, dk_ref, qT, dosT, dos, delta_s, dq_acc):
        i, j = pl.program_id(0), pl.program_id(1)

        @pl.when((i == 0) & (j == 0))
        def _():
            dk_ref[...] = jnp.zeros((S, 2 * LANES), F32)

        @pl.when(j == 0)
        def _():
            qT[...] = q_ref[...].reshape(R, 2 * LANES).astype(F32).T.astype(BF16)
            dov = do_ref[...].reshape(R, LANES).astype(F32)
            dos[...] = dov.astype(BF16)
            dosT[...] = dov.T.astype(BF16)
            delta_s[...] = jnp.sum((dov * o_ref[...].reshape(R, LANES).astype(F32)).T, axis=0, keepdims=True)
            dq_acc[...] = jnp.zeros((2 * LANES, R), F32)

        qTv, dosTv, dosv = qT[...], dosT[...], dos[...]
        qv = q_ref[...].reshape(R, 2 * LANES)
        lse_v, delta_v = lse_ref[0] * LOG2E, delta_s[...]
        dqa = dq_acc[...]
        s_cur = _dot(k_ref[0:sub, :], qTv)
        dp_cur = _dot(k_ref[0:sub, 0:LANES], dosTv)
        for t in range(nsub):
            if t + 1 < nsub:
                s_next = _dot(k_ref[sub * (t + 1):sub * (t + 2), :], qTv)
                dp_next = _dot(k_ref[sub * (t + 1):sub * (t + 2), 0:LANES], dosTv)
            p = jnp.exp2(s_cur * MLA_C - lse_v)
            ds = (p * (dp_cur - delta_v) * MLA_SCALE).astype(BF16)
            rows = pl.ds(pl.multiple_of(j * tk + sub * t, sub), sub)
            dk_ref[rows, :] += _dot(ds, qv)
            dk_ref[rows, 0:LANES] += _dot(p.astype(BF16), dosv)
            dqa = dqa + _dot(kT_ref[:, sub * t:sub * (t + 1)], ds)
            if t + 1 < nsub:
                s_cur, dp_cur = s_next, dp_next
        dq_acc[...] = dqa

        @pl.when(j == nkv - 1)
        def _():
            dq_ref[...] = dq_acc[...].T.reshape(B_HEADS, tq, 2 * LANES)

    hspec = lambda w: pl.BlockSpec((B_HEADS, tq, w), lambda i, j: (0, i, 0))
    return pl.pallas_call(
        body, name="mla_bwd", grid=(nq, nkv),
        in_specs=[hspec(2 * LANES), pl.BlockSpec((tk, 2 * LANES), lambda i, j: (j, 0)), pl.BlockSpec((2 * LANES, tk), lambda i, j: (0, j)),
                  hspec(LANES), hspec(LANES), pl.BlockSpec((1, 1, R), lambda i, j: (i, 0, 0))],
        out_specs=[hspec(2 * LANES), pl.BlockSpec((S, 2 * LANES), lambda i, j: (0, 0))],
        out_shape=[_sds((B_HEADS, S, 2 * LANES), F32), _sds((S, 2 * LANES), F32)],
        scratch_shapes=[pltpu.VMEM((2 * LANES, R), BF16), pltpu.VMEM((LANES, R), BF16), pltpu.VMEM((R, LANES), BF16),
                        pltpu.VMEM((1, R), F32), pltpu.VMEM((2 * LANES, R), F32)],
        compiler_params=_params(("arbitrary", "arbitrary")))(q, kcat, kcatT, o, do, lse)


def _win_start(i, tq, nk, S):
    return pl.multiple_of(jnp.clip(i * tq - WINDOW, 0, S - nk), LANES)


def _win_dist_table(S, tq):
    nk = min(tq + 2 * WINDOW, S)
    nq = S // tq
    r = np.arange(nk)[:, None]
    c = (np.arange(2 * tq) % tq)[None, :]
    tabs = []
    for rel in (0, WINDOW, (nq - 1) * tq - (S - nk)):
        dist = np.abs(rel + c - r).astype(np.float32)
        tabs.append(np.where(dist <= WINDOW, dist, np.float32(1e32)))
    return jnp.asarray(np.stack(tabs))


def _win_dist_spec(nk, tq, nq):
    return pl.BlockSpec((1, nk, 2 * tq), lambda b, i: (jnp.where(i == 0, 0, jnp.where(i == nq - 1, 2, 1)), 0, 0))


def _win_fwd(q, k, vT, dist, slope, sink, *, kdiv, tq, nbs, name):
    S = k.shape[0]; nb = q.shape[1] // LANES; nq = S // tq; nk = min(tq + 2 * WINDOW, S)
    assert nb % nbs == 0 and nbs % kdiv == 0
    kvw = (nbs // kdiv) * LANES

    def body(q_ref, k_ref, vT_ref, dist_ref, slope_ref, sink_ref, o_ref, lse_ref):
        i = pl.program_id(1)
        rlo = _row_lo()
        k0 = _win_start(i, tq, nk, S)
        kk, vv, dd = k_ref[pl.ds(k0, nk), :], vT_ref[:, pl.ds(k0, nk)], dist_ref[0]
        for u in range(nbs):
            kv = slice(LANES * (u // kdiv), LANES * (u // kdiv + 1))
            qsT = _stack_cols(q_ref[:, LANES * u:LANES * (u + 1)].astype(F32).T, rlo).astype(BF16)
            s = _dot(kk[:, kv], qsT) - slope_ref[u] * dd
            sk = sink_ref[u]
            m = jnp.maximum(jnp.max(s, axis=0, keepdims=True), sk)
            p = jnp.exp(s - m)
            l = jnp.sum(p, axis=0, keepdims=True) + jnp.exp(sk - m)
            o_ref[:, LANES * u:LANES * (u + 1)] = _pick_halves_T(_dot(vv[kv, :], p.astype(BF16)) / l, rlo, tq).astype(BF16)
            lse_ref[u, 0] = m + jnp.log(l)

    row_spec = pl.BlockSpec((nbs, 1, 2 * tq), lambda b, i: (b, 0, 0))
    return pl.pallas_call(
        body, name=name, grid=(nb // nbs, nq),
        in_specs=[pl.BlockSpec((tq, nbs * LANES), lambda b, i: (i, b)), pl.BlockSpec((S, kvw), lambda b, i: (0, b)),
                  pl.BlockSpec((kvw, S), lambda b, i: (b, 0)), _win_dist_spec(nk, tq, nq), row_spec, row_spec],
        out_specs=[pl.BlockSpec((tq, nbs * LANES), lambda b, i: (i, b)), pl.BlockSpec((nbs, 1, 1, 2 * tq), lambda b, i: (b, i, 0, 0))],
        out_shape=[_sds((S, nb * LANES), BF16), _sds((nb, nq, 1, 2 * tq), F32)],
        compiler_params=_params(("arbitrary", "arbitrary")))(q, k, vT, dist, slope, sink)


def _win_bwd(q, k, kT, v, o, do, lse, dist, slope, sink, *, kdiv, tq, nbs, name):
    S = k.shape[0]; nb = q.shape[1] // LANES; nkb = k.shape[1] // LANES; nq = S // tq; nk = min(tq + 2 * WINDOW, S)
    assert nb % nbs == 0 and nbs % kdiv == 0
    nkv = nbs // kdiv
    kvw = nkv * LANES

    def body(q_ref, k_ref, kT_ref, v_ref, o_ref, do_ref, lse_ref, dist_ref, slope_ref, sink_ref, dq_ref, dk_ref, dv_ref, dsink_ref, ds_acc):
        i = pl.program_id(1)
        rlo = _row_lo()
        lo = lax.broadcasted_iota(jnp.int32, (1, LANES), 1) < HEAD_DIM

        @pl.when(i == 0)
        def _():
            dk_ref[...] = jnp.zeros((S, kvw), F32)
            dv_ref[...] = jnp.zeros((S, kvw), F32)
            ds_acc[...] = jnp.zeros((nbs, 2 * tq), F32)

        k0 = _win_start(i, tq, nk, S)
        rows = pl.ds(k0, nk)
        kk_all, vv_all, kkT_all, dd = k_ref[rows, :], v_ref[rows, :], kT_ref[:, rows], dist_ref[0]
        dv_sum, dk_sum = [None] * nkv, [None] * nkv
        for u in range(nbs):
            g = u // kdiv
            kv = slice(LANES * g, LANES * (g + 1))
            kk, vv, kkT = kk_all[:, kv], vv_all[:, kv], kkT_all[kv, :]
            cols = slice(LANES * u, LANES * (u + 1))
            qv = q_ref[:, cols]
            qs = _stack_rows(qv, lo)
            qsT = _stack_cols(qv.astype(F32).T, rlo).astype(BF16)
            dov = do_ref[:, cols].astype(F32)
            dos = _stack_rows(dov.astype(BF16), lo)
            dosT = _stack_cols(dov.T, rlo).astype(BF16)
            prodT = (dov * o_ref[:, cols].astype(F32)).T
            delta = jnp.concatenate([jnp.sum(jnp.where(rlo, prodT, 0.0), axis=0, keepdims=True),
                                     jnp.sum(jnp.where(rlo, 0.0, prodT), axis=0, keepdims=True)], axis=1)
            lse_v = lse_ref[u, 0]
            ds_acc[u:u + 1, :] += -jnp.exp(sink_ref[u] - lse_v) * delta
            p = jnp.exp(_dot(kk, qsT) - slope_ref[u] * dd - lse_v)
            ds = (p * (_dot(vv, dosT) - delta)).astype(BF16)
            dv_u, dk_u = _dot(p.astype(BF16), dos), _dot(ds, qs)
            dv_sum[g] = dv_u if dv_sum[g] is None else dv_sum[g] + dv_u
            dk_sum[g] = dk_u if dk_sum[g] is None else dk_sum[g] + dk_u
            dq_ref[:, cols] = (_pick_halves_T(_dot(kkT, ds), rlo, tq) * 0.125).astype(BF16)
        dv_ref[rows, :] += jnp.concatenate(dv_sum, axis=1)
        dk_ref[rows, :] += jnp.concatenate(dk_sum, axis=1)

        @pl.when(i == nq - 1)
        def _():
            acc = ds_acc[...]
            for u in range(nbs):
                dsink_ref[u] = jnp.concatenate(
                    [jnp.broadcast_to(jnp.sum(acc[u:u + 1, 0:tq], axis=1, keepdims=True), (1, LANES)),
                     jnp.broadcast_to(jnp.sum(acc[u:u + 1, tq:2 * tq], axis=1, keepdims=True), (1, LANES)),
                     jnp.zeros((6, LANES), F32)], axis=0)

    qmap = lambda b, i: (i, b)
    kv_spec = pl.BlockSpec((S, kvw), lambda b, i: (0, b))
    row_spec = pl.BlockSpec((nbs, 1, 2 * tq), lambda b, i: (b, 0, 0))
    wide = pl.BlockSpec((tq, nbs * LANES), qmap)
    return pl.pallas_call(
        body, name=name, grid=(nb // nbs, nq),
        in_specs=[wide, kv_spec, pl.BlockSpec((kvw, S), lambda b, i: (b, 0)), kv_spec, wide, wide,
                  pl.BlockSpec((nbs, 1, 1, 2 * tq), lambda b, i: (b, i, 0, 0)), _win_dist_spec(nk, tq, nq), row_spec, row_spec],
        out_specs=[wide, kv_spec, kv_spec, pl.BlockSpec((nbs, 8, LANES), lambda b, i: (b, 0, 0))],
        out_shape=[_sds((S, nb * LANES), BF16), _sds((S, nkb * LANES), F32), _sds((S, nkb * LANES), F32), _sds((nb, 8, LANES), F32)],
        scratch_shapes=[pltpu.VMEM((nbs, 2 * tq), F32)],
        compiler_params=_params(("arbitrary", "arbitrary")))(q, k, kT, v, o, do, lse, dist, slope, sink)


def _sum_rows(v):
    return jnp.sum(v, axis=0, keepdims=True)


def _norm_mod_bwd(dh, xv, mod_ref, nw_ref, stats_ref):
    r = _rms(xv)
    xn = xv * r
    nw = nw_ref[...]
    stats_ref[0:1, :] += _sum_rows(dh)
    stats_ref[1:2, :] += _sum_rows(dh * (xn * nw))
    dn = dh * (1.0 + mod_ref[1:2, :])
    stats_ref[2:3, :] += _sum_rows(dn * xn)
    return _rms_bwd(xv, r, dn * nw)


def _even_gate_specs(ts):
    return [pl.BlockSpec((ts, 256), lambda i, c=c: (i, c)) for c in (3, 4, 7, 8)]


def _even_post_fwd(oa, olat, proj, x, gate, wuv, woe):
    S = x.shape[0]
    ts = min(ROW_TILE, S)

    def body(oa_ref, ol_ref, ga0_ref, ga1_ref, gb0_ref, gb1_ref, x_ref, gate_ref, wuv_ref, woe_ref, y_ref, x1_ref):
        sa, _ = _silu_and_grad(jnp.concatenate([ga0_ref[...], ga1_ref[...]], axis=1))
        sb, _ = _silu_and_grad(jnp.concatenate([gb0_ref[...], gb1_ref[...]], axis=1))
        olc = jnp.concatenate([ol_ref[hh] for hh in range(B_HEADS)], axis=1).astype(BF16)
        ob = _dot(olc, wuv_ref[...])
        mix = jnp.concatenate([oa_ref[...] * sa, ob * sb], axis=1).astype(BF16)
        y = _dot(mix, woe_ref[...])
        y_ref[...] = y.astype(BF16)
        x1_ref[...] = x_ref[...] + gate_ref[...] * y

    return pl.pallas_call(
        body, name="even_post_fwd", grid=(S // ts,),
        in_specs=[_row_spec(ts, 512), pl.BlockSpec((B_HEADS, ts, LANES), lambda i: (0, i, 0))] + _even_gate_specs(ts) +
                 [_row_spec(ts, D_MODEL), _full_spec((1, D_MODEL)), _full_spec((1024, 512)), _full_spec((1024, D_MODEL))],
        out_specs=[_row_spec(ts, D_MODEL), _row_spec(ts, D_MODEL)],
        out_shape=[_sds((S, D_MODEL), BF16), _sds((S, D_MODEL), F32)],
        compiler_params=_params(("arbitrary",)),
    )(oa, olat, proj, proj, proj, proj, x, gate, wuv, woe)


def _odd_pre_fwd(x, mod, nw, wio):
    S = x.shape[0]
    ts = min(ROW_TILE, S)

    def body(x_ref, mod_ref, nw_ref, wio_ref, h_ref, g_ref, q_ref, k_ref, v_ref, kt_ref, vt_ref):
        xv = x_ref[...]
        h = (xv * _rms(xv) * nw_ref[...]) * (1.0 + mod_ref[1:2, :]) + mod_ref[0:1, :]
        hb = h.astype(BF16)
        h_ref[...] = hb
        proj = jnp.concatenate([_dot(hb, wio_ref[p]) for p in range(N_CHIPS)], axis=1)
        g_ref[...] = proj[:, 1536:2560]
        q_ref[...] = (proj[:, 0:1024] * 0.125).astype(BF16)
        lane = _lane_iota()
        k_v = jnp.concatenate([_dup_heads(proj[:, 1024 + LANES * j:1024 + LANES * (j + 1)], lane) for j in range(2)], axis=1)
        v_v = jnp.concatenate([_dup_heads(proj[:, 1280 + LANES * j:1280 + LANES * (j + 1)], lane) for j in range(2)], axis=1)
        k_ref[...] = k_v.astype(BF16)
        v_ref[...] = v_v.astype(BF16)
        kt_ref[...] = k_v.T.astype(BF16)
        vt_ref[...] = v_v.T.astype(BF16)

    col_spec = pl.BlockSpec((512, ts), lambda i: (0, i))
    return pl.pallas_call(
        body, name="odd_pre_fwd", grid=(S // ts,),
        in_specs=[_row_spec(ts, D_MODEL), _full_spec((3, D_MODEL)), _full_spec((1, D_MODEL)),
                  _full_spec((N_CHIPS, D_MODEL, ODD_IN // N_CHIPS))],
        out_specs=[_row_spec(ts, D_MODEL), _row_spec(ts, 1024), _row_spec(ts, 1024), _row_spec(ts, 512), _row_spec(ts, 512),
                   col_spec, col_spec],
        out_shape=[_sds((S, D_MODEL), BF16), _sds((S, 1024), F32), _sds((S, 1024), BF16), _sds((S, 512), BF16),
                   _sds((S, 512), BF16), _sds((512, S), BF16), _sds((512, S), BF16)],
        compiler_params=_params(("arbitrary",)),
    )(x, mod, nw, wio)


def _odd_post(oc, g, x1, gate, woo, fw, tgt):
    S = x1.shape[0]
    ts = min(ROW_TILE, S)
    nsteps = S // ts

    def body(oc_ref, g_ref, x_ref, gate_ref, woo_ref, fw_ref, tgt_ref, doc_ref, dgc_ref, dx2_ref, dwoo_out, stats_ref, dwoo_ref):
        @pl.when(pl.program_id(0) == 0)
        def _():
            dwoo_ref[...] = jnp.zeros((D_MODEL, D_MODEL), F32)
            stats_ref[...] = jnp.zeros((8, D_MODEL), F32)

        ocv = oc_ref[...]
        sg, dsg = _silu_and_grad(g_ref[...])
        mix = (ocv * sg).astype(BF16)
        woo_v = woo_ref[...]
        y = _dot(mix, woo_v)
        gate_v = gate_ref[...]
        x2 = x_ref[...] + gate_v * y
        r = _rms(x2)
        fw_v = fw_ref[...]
        xn = x2 * r
        err = xn * fw_v - tgt_ref[...]
        dout = err * (1.0 / D_MODEL)
        dx2 = _rms_bwd(x2, r, dout * fw_v)
        dx2_ref[...] = dx2
        stats_ref[0:1, :] += _sum_rows(dout * xn)
        stats_ref[1:2, :] += _sum_rows(dx2 * y)
        loss_t = 0.5 * jnp.sum(_sum_rows(err * dout), axis=-1, keepdims=True)
        stats_ref[2:3, :] += jnp.broadcast_to(loss_t, (1, D_MODEL))
        dy = (gate_v * dx2).astype(BF16)
        dmix = _dot_nt(dy, woo_v)
        dwoo_ref[...] += _dot_tn(mix, dy)
        doc_ref[...] = (dmix * sg).astype(BF16)
        dgc_ref[...] = (dmix * ocv * dsg).astype(BF16)

        @pl.when(pl.program_id(0) == nsteps - 1)
        def _():
            dwoo_out[...] = dwoo_ref[...].astype(BF16)

    return pl.pallas_call(
        body, name="odd_post", grid=(nsteps,),
        in_specs=[_row_spec(ts, D_MODEL), _row_spec(ts, D_MODEL), _row_spec(ts, D_MODEL), _full_spec((1, D_MODEL)),
                  _full_spec((D_MODEL, D_MODEL)), _full_spec((1, D_MODEL)), _row_spec(ts, D_MODEL)],
        out_specs=[_row_spec(ts, D_MODEL), _row_spec(ts, D_MODEL), _row_spec(ts, D_MODEL),
                   _full_spec((D_MODEL, D_MODEL), single=False), _full_spec((8, D_MODEL), single=False)],
        out_shape=[_sds((S, D_MODEL), BF16), _sds((S, D_MODEL), BF16), _sds((S, D_MODEL), F32), _sds((D_MODEL, D_MODEL), BF16),
                   _sds((8, D_MODEL), F32)],
        scratch_shapes=[pltpu.VMEM((D_MODEL, D_MODEL), F32)],
        compiler_params=_params(("arbitrary",)),
    )(oc, g, x1, gate, woo, fw, tgt)


def _odd_pre_bwd(dq, dk, dv, dgc, h, x, dx_res, mod, nw, wio):
    S = x.shape[0]
    ts = min(IN_PROJ_ROW_TILE, S)
    nsteps = S // ts
    wsh = ODD_IN // N_CHIPS

    def body(dq_ref, dk_ref, dv_ref, dgc_ref, h_ref, x_ref, dxr_ref, mod_ref, nw_ref, wio_ref, dx_ref, dw_ref, stats_ref, dw_acc):
        @pl.when(pl.program_id(0) == 0)
        def _():
            dw_acc[...] = jnp.zeros((N_CHIPS, D_MODEL, wsh), F32)
            stats_ref[...] = jnp.zeros((8, D_MODEL), F32)

        lane = _lane_iota()
        dkv = [_fold_heads(r[:, 2 * LANES * j:2 * LANES * (j + 1)], lane).astype(BF16) for r in (dk_ref, dv_ref) for j in range(2)]
        dproj = jnp.concatenate([dq_ref[...]] + dkv + [dgc_ref[...]], axis=1)
        hv = h_ref[...]
        dh = None
        for p in range(N_CHIPS):
            dp_cols = dproj[:, wsh * p:wsh * (p + 1)]
            part = _dot_nt(dp_cols, wio_ref[p])
            dh = part if dh is None else dh + part
            dw_acc[p] += _dot_tn(hv, dp_cols)
        dx_ref[...] = dxr_ref[...] + _norm_mod_bwd(dh, x_ref[...], mod_ref, nw_ref, stats_ref)

        @pl.when(pl.program_id(0) == nsteps - 1)
        def _():
            dw_ref[...] = dw_acc[...].astype(BF16)

    return pl.pallas_call(
        body, name="odd_pre_bwd", grid=(nsteps,),
        in_specs=[_row_spec(ts, 1024), _row_spec(ts, 512), _row_spec(ts, 512), _row_spec(ts, 1024), _row_spec(ts, D_MODEL),
                  _row_spec(ts, D_MODEL), _row_spec(ts, D_MODEL), _full_spec((3, D_MODEL)), _full_spec((1, D_MODEL)),
                  _full_spec((N_CHIPS, D_MODEL, wsh))],
        out_specs=[_row_spec(ts, D_MODEL), _full_spec((N_CHIPS, D_MODEL, wsh), single=False), _full_spec((8, D_MODEL), single=False)],
        out_shape=[_sds((S, D_MODEL), F32), _sds((N_CHIPS, D_MODEL, wsh), BF16), _sds((8, D_MODEL), F32)],
        scratch_shapes=[pltpu.VMEM((N_CHIPS, D_MODEL, wsh), F32)],
        compiler_params=_params(("arbitrary",)),
    )(dq, dk, dv, dgc, h, x, dx_res, mod, nw, wio)


def _even_post_bwd(dx1, y, oa, olat, proj, gate, wuv, woe):
    S = dx1.shape[0]
    ts = min(ROW_TILE, S)
    nsteps = S // ts

    def body(dx_ref, y_ref, oa_ref, ol_ref, ga0_ref, ga1_ref, gb0_ref, gb1_ref, gate_ref, wuv_ref, woe_ref,
             doa_ref, dga_ref, dgb_ref, dol_ref, dwoe_out, dwuv_ref, stats_ref, dwoe_ref):
        @pl.when(pl.program_id(0) == 0)
        def _():
            dwoe_ref[...] = jnp.zeros((D_MODEL, D_MODEL), F32)
            dwuv_ref[...] = jnp.zeros((1024, 512), F32)
            stats_ref[...] = jnp.zeros((8, D_MODEL), F32)

        dxv = dx_ref[...]
        stats_ref[0:1, :] += _sum_rows(dxv * y_ref[...])
        dy = (gate_ref[...] * dxv).astype(BF16)
        sa, dsa = _silu_and_grad(jnp.concatenate([ga0_ref[...], ga1_ref[...]], axis=1))
        sb, dsb = _silu_and_grad(jnp.concatenate([gb0_ref[...], gb1_ref[...]], axis=1))
        olc = jnp.concatenate([ol_ref[hh] for hh in range(B_HEADS)], axis=1).astype(BF16)
        wuv_v = wuv_ref[...]
        ob = _dot(olc, wuv_v)
        oav = oa_ref[...]
        mix = jnp.concatenate([oav * sa, ob * sb], axis=1).astype(BF16)
        dmix = _dot_nt(dy, woe_ref[...])
        dwoe_ref[...] += _dot_tn(mix, dy)
        dma, dmb = dmix[:, 0:512], dmix[:, 512:1024]
        doa_ref[...] = (dma * sa).astype(BF16)
        dga_ref[...] = (dma * oav * dsa).astype(BF16)
        dgb_ref[...] = (dmb * ob * dsb).astype(BF16)
        dob = (dmb * sb).astype(BF16)
        dol = _dot_nt(dob, wuv_v)
        dwuv_ref[...] += _dot_tn(olc, dob)
        for hh in range(B_HEADS):
            dol_ref[hh] = dol[:, LANES * hh:LANES * (hh + 1)].astype(BF16)

        @pl.when(pl.program_id(0) == nsteps - 1)
        def _():
            dwoe_out[...] = dwoe_ref[...].astype(BF16)

    head_spec = pl.BlockSpec((B_HEADS, ts, LANES), lambda i: (0, i, 0))
    return pl.pallas_call(
        body, name="even_post_bwd", grid=(nsteps,),
        in_specs=[_row_spec(ts, D_MODEL), _row_spec(ts, D_MODEL), _row_spec(ts, 512), head_spec] + _even_gate_specs(ts) +
                 [_full_spec((1, D_MODEL)), _full_spec((1024, 512)), _full_spec((1024, D_MODEL))],
        out_specs=[_row_spec(ts, 512), _row_spec(ts, 512), _row_spec(ts, 512), head_spec,
                   _full_spec((D_MODEL, D_MODEL), single=False), _full_spec((1024, 512), single=False),
                   _full_spec((8, D_MODEL), single=False)],
        out_shape=[_sds((S, 512), BF16), _sds((S, 512), BF16), _sds((S, 512), BF16), _sds((B_HEADS, S, LANES), BF16),
                   _sds((D_MODEL, D_MODEL), BF16), _sds((1024, 512), F32), _sds((8, D_MODEL), F32)],
        scratch_shapes=[pltpu.VMEM((D_MODEL, D_MODEL), F32)],
        compiler_params=_params(("arbitrary",)),
    )(dx1, y, oa, olat, proj, proj, proj, proj, gate, wuv, woe)


def _even_pre_bwd(x, h, proj, dqa, dka, dva, dga, dgb, dqcat, dkcat, dx_res, mod, nw, wie, qn, kn, seg, ca, sa, ct, st,
                  qln, kvln, wuq, wuk):
    S = x.shape[0]
    ts = min(IN_PROJ_ROW_TILE, S)
    nsteps = S // ts

    def body(x_ref, h_ref, proj_ref, dqa_ref, dka_ref, dva_ref, dga_ref, dgb_ref, dqc_ref, dkc_ref, dxr_ref, mod_ref, nw_ref,
             wie_ref, qn_ref, kn_ref, seg_ref, ca_ref, sa_ref, ct_ref, st_ref, qln_ref, kvln_ref, wuq_ref, wuk_ref,
             dx_ref, dwie_out, dwuq_out, dwuk_out, stats_ref, nstats_ref, dwie_ref, dwuq_ref, dwuk_ref, stage, uq_stage, uk_stage):
        @pl.when(pl.program_id(0) == 0)
        def _():
            dwie_ref[...] = jnp.zeros((EVEN_P, D_MODEL), F32)
            dwuq_ref[...] = jnp.zeros((1536, B_Q_LORA), F32)
            dwuk_ref[...] = jnp.zeros((512, 1024), F32)
            stats_ref[...] = jnp.zeros((8, D_MODEL), F32)
            nstats_ref[...] = jnp.zeros((8, 256), F32)

        lane = _lane_iota()
        ca_v, sa_v, ct_v, st_v = ca_ref[...], sa_ref[...], ct_ref[...], st_ref[...]
        seg_v = seg_ref[...]

        def head_norm_bwd(xc, dy, w):
            r = lax.rsqrt(_seg_mean(xc * xc, seg_v) + EPS)
            g = dy * w
            dxc = r * g - xc * (r * r * r) * _seg_mean(xc * g, seg_v)
            return dxc, _sum_rows(dy * (xc * r))

        pieces = []
        dqn = jnp.zeros((1, LANES), F32)
        for cb in range(4):
            sl = slice(LANES * cb, LANES * (cb + 1))
            dy = _rot_bwd(dqa_ref[:, sl] * 0.125, ca_v, sa_v, lane)
            dxc, dw = head_norm_bwd(proj_ref[:, sl], dy, qn_ref[...])
            pieces.append(dxc)
            dqn = dqn + dw
        dxc, dkn = head_norm_bwd(proj_ref[:, 512:640], _rot_bwd(_fold_heads(dka_ref[...], lane), ca_v, sa_v, lane), kn_ref[...])
        pieces += [dxc, _fold_heads(dva_ref[...], lane), dga_ref[...]]
        nstats_ref[0:1, 0:LANES] += dqn + pltpu.roll(dqn, HEAD_DIM, 1)
        nstats_ref[1:2, 0:LANES] += dkn + pltpu.roll(dkn, HEAD_DIM, 1)

        cq = proj_ref[:, 1280:1536]
        rq = _rms(cq)
        cqn_f = cq * rq
        qln_v = qln_ref[...]
        cqn = (cqn_f * qln_v).astype(BF16)
        wuq_v, wuk_v = wuq_ref[...], wuk_ref[...]
        qnope = _dot_nt(cqn, wuq_v[0:512, :]).astype(BF16)
        dqlat = jnp.concatenate([dqc_ref[hh, :, 0:LANES] for hh in range(B_HEADS)], axis=1).astype(BF16)
        dqnope = _dot_nt(dqlat, wuk_v)
        dwuk_ref[...] += _dot_tn(qnope, dqlat)
        dqr = [_rot_bwd(dqc_ref[hh, :, LANES:2 * LANES], ct_v, st_v, lane) for hh in range(B_HEADS)]
        dqb = jnp.concatenate([dqnope] + dqr, axis=1).astype(BF16)
        dcqn = _dot(dqb, wuq_v)
        dwuq_ref[...] += _dot_tn(dqb, cqn)
        nstats_ref[2:3, :] += _sum_rows(dcqn * cqn_f)
        dcq = _rms_bwd(cq, rq, dcqn * qln_v)
        ckv = proj_ref[:, 1536:1664]
        rk = _rms(ckv)
        dckvn = dkc_ref[:, 0:LANES]
        nstats_ref[3:4, 0:LANES] += _sum_rows(dckvn * (ckv * rk))
        dckv = _rms_bwd(ckv, rk, dckvn * kvln_ref[...])
        dkr = _rot_bwd(dkc_ref[:, LANES:2 * LANES], ct_v, st_v, lane)
        pieces += [dcq, dckv, dkr, dgb_ref[...]]
        dproj = jnp.concatenate([piece.astype(BF16) for piece in pieces], axis=1)
        dh = _dot(dproj, wie_ref[...])
        dwie_ref[...] += _dot_tn(dproj, h_ref[...])
        dx_ref[...] = dxr_ref[...] + _norm_mod_bwd(dh, x_ref[...], mod_ref, nw_ref, stats_ref)

        @pl.when(pl.program_id(0) == nsteps - 1)
        def _():
            r_out = 0
            for lo, hi in ((0, EVEN_GAP), (EVEN_GAP + EVEN_P - EVEN_IN, EVEN_P)):
                for r0 in range(lo, hi, stage.shape[0]):
                    n = min(stage.shape[0], hi - r0)
                    stage[0:n, :] = dwie_ref[r0:r0 + n, :].astype(BF16)
                    pltpu.sync_copy(stage.at[0:n], dwie_out.at[pl.ds(r_out, n), :])
                    r_out += n
            per = B_NOPE + B_ROPE
            for hd in range(B_HEADS):
                uq_stage[per * hd:per * hd + B_NOPE, :] = dwuq_ref[B_NOPE * hd:B_NOPE * (hd + 1), :].astype(BF16)
                uq_stage[per * hd + B_NOPE:per * (hd + 1), :] = dwuq_ref[512 + LANES * hd:512 + LANES * hd + B_ROPE, :].astype(BF16)
                uk_stage[B_NOPE * hd:B_NOPE * (hd + 1), :] = dwuk_ref[B_NOPE * hd:B_NOPE * (hd + 1),
                                                                       LANES * hd:LANES * (hd + 1)].astype(BF16)
            pltpu.sync_copy(uq_stage, dwuq_out)
            pltpu.sync_copy(uk_stage, dwuk_out)

    return pl.pallas_call(
        body, name="even_pre_bwd", grid=(nsteps,),
        in_specs=[_row_spec(ts, D_MODEL), _row_spec(ts, D_MODEL), _row_spec(ts, EVEN_P), _row_spec(ts, 512), _row_spec(ts, 2 * LANES),
                  _row_spec(ts, 2 * LANES), _row_spec(ts, 512), _row_spec(ts, 512),
                  pl.BlockSpec((B_HEADS, ts, 2 * LANES), lambda i: (0, i, 0)), _row_spec(ts, 2 * LANES), _row_spec(ts, D_MODEL),
                  _full_spec((3, D_MODEL)), _full_spec((1, D_MODEL)), _full_spec((EVEN_P, D_MODEL)),
                  _full_spec((1, LANES)), _full_spec((1, LANES)), _full_spec((LANES, LANES)),
                  _row_spec(ts, LANES), _row_spec(ts, LANES), _row_spec(ts, LANES), _row_spec(ts, LANES),
                  _full_spec((1, B_Q_LORA)), _full_spec((1, B_KV_LORA)), _full_spec((1536, B_Q_LORA)), _full_spec((512, 1024))],
        out_specs=[_row_spec(ts, D_MODEL), _ANY, _ANY, _ANY, _full_spec((8, D_MODEL), single=False), _full_spec((8, 256), single=False)],
        out_shape=[_sds((S, D_MODEL), F32), _sds((EVEN_IN, D_MODEL), BF16), _sds((B_HEADS * (B_NOPE + B_ROPE), B_Q_LORA), BF16),
                   _sds((B_HEADS * B_NOPE, LANES), BF16), _sds((8, D_MODEL), F32), _sds((8, 256), F32)],
        scratch_shapes=[pltpu.VMEM((EVEN_P, D_MODEL), F32), pltpu.VMEM((1536, B_Q_LORA), F32), pltpu.VMEM((512, 1024), F32),
                        pltpu.VMEM((256, D_MODEL), BF16), pltpu.VMEM((B_HEADS * (B_NOPE + B_ROPE), B_Q_LORA), BF16),
                        pltpu.VMEM((B_HEADS * B_NOPE, LANES), BF16)],
        compiler_params=_params(("arbitrary",)),
    )(x, h, proj, dqa, dka, dva, dga, dgb, dqcat, dkcat, dx_res, mod, nw, wie, qn, kn, seg, ca, sa, ct, st, qln, kvln, wuq, wuk)


def _ada_fwd(c_all, w, b):
    n = w.shape[2]

    def body(c_ref, w_ref, b_ref, o_ref):
        cv = c_ref[...]
        o_ref[0] = _dot_f32(cv * _sigmoid(cv), w_ref[0]) + b_ref[0]

    return pl.pallas_call(
        body, name="ada_fwd", grid=(2,),
        in_specs=[pl.BlockSpec((N_DEV, D_MODEL), lambda l: (0, 0)), pl.BlockSpec((1, D_MODEL, n), lambda l: (l, 0, 0)),
                  pl.BlockSpec((1, 1, n), lambda l: (l, 0, 0))],
        out_specs=pl.BlockSpec((1, N_DEV, n), lambda l: (l, 0, 0)),
        out_shape=_sds((2, N_DEV, n), F32),
        compiler_params=_params(("arbitrary",)),
    )(c_all, w, b)


def _ada_bwd(c_all_t, dmod):
    n = dmod.shape[2]

    def body(c_ref, d_ref, o_ref):
        cv = c_ref[...]
        act = cv * _sigmoid(cv)
        dv = d_ref[0]
        acc = act[:, 0:1] * dv[0:1, :]
        for bb in range(1, N_DEV):
            acc = acc + act[:, bb:bb + 1] * dv[bb:bb + 1, :]
        o_ref[0] = acc

    return pl.pallas_call(
        body, name="ada_bwd", grid=(2,),
        in_specs=[pl.BlockSpec((D_MODEL, N_DEV), lambda l: (0, 0)), pl.BlockSpec((1, N_DEV, n), lambda l: (l, 0, 0))],
        out_specs=pl.BlockSpec((1, D_MODEL, n), lambda l: (l, 0, 0)),
        out_shape=_sds((2, D_MODEL, n), F32),
        compiler_params=_params(("arbitrary",)),
    )(c_all_t, dmod)


ADAM_ROW_TILE = 512


def _adam_update(g, w, m, v):
    m_new = ADAM_B1 * m + (1.0 - ADAM_B1) * g
    v_new = ADAM_B2 * v + (1.0 - ADAM_B2) * jnp.square(g)
    m_hat = m_new / (1.0 - ADAM_B1 ** ADAM_STEP)
    v_hat = v_new / (1.0 - ADAM_B2 ** ADAM_STEP)
    return -ADAM_LR * (m_hat / (jnp.sqrt(v_hat) + ADAM_EPS) + ADAM_WD * w), m_new, v_new


SMALL_ROWS = dict(dmod=(0, D_MODEL), norm_w=(6, D_MODEL), final_norm=(8, D_MODEL), a_q_norm=(9, HEAD_DIM), a_k_norm=(10, HEAD_DIM),
                  b_q_lora_norm=(11, B_Q_LORA), b_kv_lora_norm=(12, B_KV_LORA), c_sink=(13, C_HEADS))
SMALL_WEIGHTS = ("ada_b", "norm_w", "final_norm", "a_q_norm", "a_k_norm", "b_q_lora_norm", "b_kv_lora_norm", "c_sink")
LOSS_ROW = 14


def _pack_small(res):
    def padded(v):
        return jnp.concatenate([v, jnp.zeros((v.shape[0], D_MODEL - v.shape[1]), F32)], axis=1)

    rows = [res["dmod"].reshape(6, D_MODEL), res["norm_w"], res["final_norm"].reshape(1, D_MODEL)]
    rows += [padded(res[k]) for k in ("a_q_norm", "a_k_norm", "b_q_lora_norm", "b_kv_lora_norm", "c_sink")]
    return jnp.concatenate(rows + [res["loss_row"], jnp.zeros((1, D_MODEL), F32)], axis=0)


def _adam_small(parts, ws, ms, vs):
    nw = len(SMALL_WEIGHTS)

    def body(*refs):
        p_ref = refs[0]
        w_refs, m_refs, v_refs = refs[1:1 + nw], refs[1 + nw:1 + 2 * nw], refs[1 + 2 * nw:1 + 3 * nw]
        outs = refs[1 + 3 * nw:]
        g_all = p_ref[0]
        for k in range(1, N_DEV):
            g_all = g_all + p_ref[k]
        for idx, name in enumerate(SMALL_WEIGHTS):
            if name == "ada_b":
                g = jnp.concatenate([jnp.concatenate([g_all[3 * l + t:3 * l + t + 1] for t in range(3)], axis=1) for l in range(2)],
                                    axis=0)
            else:
                row, width = SMALL_ROWS[name]
                g = g_all[row:row + w_refs[idx].shape[0], 0:width]
            d, m_new, v_new = _adam_update(g, w_refs[idx][...], m_refs[idx][...], v_refs[idx][...])
            outs[4 * idx][...], outs[4 * idx + 1][...], outs[4 * idx + 2][...], outs[4 * idx + 3][...] = g, d, m_new, v_new
        outs[4 * nw][...] = g_all[LOSS_ROW:LOSS_ROW + 1, 0:LANES]

    out_shape = []
    for w in ws:
        out_shape += [_sds(w.shape, F32)] * 4
    out_shape.append(_sds((1, LANES), F32))
    return pl.pallas_call(body, name="adam_small", out_shape=out_shape,
                          compiler_params=pltpu.CompilerParams(vmem_limit_bytes=VMEM_LIMIT))(parts, *ws, *ms, *vs)


def _adam(parts, w, m, v, name, by_columns=False):
    P, R, C = parts.shape
    if by_columns:
        tr, tc = R, 256
    else:
        tr, tc = (R if R <= ADAM_ROW_TILE else ADAM_ROW_TILE), C
    assert R % tr == 0 and C % tc == 0

    def body(p_ref, w_ref, m_ref, v_ref, g_ref, d_ref, nm_ref, nv_ref):
        g = p_ref[0].astype(F32)
        for k in range(1, P):
            g = g + p_ref[k].astype(F32)
        g_ref[...] = g
        d_ref[...], nm_ref[...], nv_ref[...] = _adam_update(g, w_ref[...], m_ref[...], v_ref[...])

    tile = (lambda i: (0, i)) if by_columns else (lambda i: (i, 0))
    spec = pl.BlockSpec((tr, tc), tile)
    return pl.pallas_call(
        body, name=name, grid=(C // tc if by_columns else R // tr,),
        in_specs=[pl.BlockSpec((P, tr, tc), lambda i: (0,) + tile(i)), spec, spec, spec],
        out_specs=[spec, spec, spec, spec], out_shape=[_sds((R, C), F32)] * 4,
        compiler_params=_params(("arbitrary",)),
    )(parts, w, m, v)


_ANY = pl.BlockSpec(memory_space=pl.ANY)
CHIP_FLIPS = ((1, 0), (0, 1), (1, 1))
DEV_FLIPS = tuple((dx, dy, dc) for dx in (0, 1) for dy in (0, 1) for dc in (0, 1) if dx + dy + dc)


def _flip(a, d):
    return a if d == 0 else 1 - a


def _my_place():
    return lax.axis_index("x"), lax.axis_index("y"), lax.axis_index("c")


def _gather8_copies(ins, outs, send_sems, recv_sems, loc_sems):
    x, y, c = _my_place()
    me = 4 * x + 2 * y + c
    copies = []
    for a in range(len(ins)):
        copies.append(pltpu.make_async_copy(ins[a], outs[a].at[me], loc_sems.at[a]))
        for k, (dx, dy, dc) in enumerate(DEV_FLIPS):
            copies.append(pltpu.make_async_remote_copy(
                src_ref=ins[a], dst_ref=outs[a].at[me], send_sem=send_sems.at[a, k], recv_sem=recv_sems.at[a, k],
                device_id=(_flip(x, dx), _flip(y, dy), _flip(c, dc)), device_id_type=MESH_ID))
    return copies


def _gather8_sems(n):
    return [pltpu.SemaphoreType.DMA((n, 7)), pltpu.SemaphoreType.DMA((n, 7)), pltpu.SemaphoreType.DMA((n,))]


def _gather_dev8(arrs, name):
    n = len(arrs)

    def body(*refs):
        copies = _gather8_copies(refs[:n], refs[n:2 * n], *refs[2 * n:])
        for cp in copies:
            cp.start()
        for cp in copies:
            cp.wait()

    return pl.pallas_call(
        body, name=name, in_specs=[_ANY] * n, out_specs=[_ANY] * n,
        out_shape=[_sds((N_DEV,) + a.shape, a.dtype) for a in arrs], scratch_shapes=_gather8_sems(n),
    )(*arrs)


class _Exchange:
    def __init__(self, arrs, out_shapes, n_sems, phases):
        self.arrs, self.out_shapes, self.n_sems, self._phases = list(arrs), list(out_shapes), n_sems, phases

    @property
    def n(self):
        return len(self.arrs)

    def sem_shapes(self):
        return [pltpu.SemaphoreType.DMA((self.n, self.n_sems)), pltpu.SemaphoreType.DMA((self.n, self.n_sems)),
                pltpu.SemaphoreType.DMA((self.n,))]

    def phases(self, ins, outs, sems):
        return self._phases(ins, outs, *sems)

    def run(self, name):
        n = self.n

        def body(*refs):
            start, mid, end = self.phases(refs[:n], refs[n:2 * n], refs[2 * n:])
            start()
            mid()
            end()

        return pl.pallas_call(body, name=name, in_specs=[_ANY] * n, out_specs=[_ANY] * n, out_shape=self.out_shapes,
                              scratch_shapes=self.sem_shapes())(*self.arrs)

def _gather_halves_phases(ins, outs, send_sems, recv_sems, loc_sems):
    n = len(ins)
    x, y, c = _my_place()
    chip = 2 * x + y
    sibling = (x, y, 1 - c)
    peers = [(_flip(x, dx), _flip(y, dy)) for dx, dy in CHIP_FLIPS]

    def remote(src, p, half, a, k, to):
        return pltpu.make_async_remote_copy(src_ref=src, dst_ref=outs[a].at[p, half], send_sem=send_sems.at[a, k],
                                            recv_sem=recv_sems.at[a, k], device_id=to, device_id_type=MESH_ID)

    def local(a):
        return pltpu.make_async_copy(ins[a], outs[a].at[chip], loc_sems.at[a])

    def first(a, k):
        return remote(ins[a].at[c], chip, c, a, k, (*peers[k], c))

    def passed(a, k):
        p = 2 * peers[k][0] + peers[k][1]
        return remote(outs[a].at[p, c], p, c, a, 3 + k, sibling)

    def start():
        for a in range(n):
            local(a).start()
            for k in range(3):
                first(a, k).start()

    def mid():
        for a in range(n):
            for k in range(3):
                p = 2 * peers[k][0] + peers[k][1]
                remote(outs[a].at[p, c], p, c, a, k, sibling).wait_recv()
                passed(a, k).start()

    def end():
        for a in range(n):
            for k in range(3):
                p = 2 * peers[k][0] + peers[k][1]
                remote(outs[a].at[p, 1 - c], p, 1 - c, a, 3 + k, sibling).wait_recv()
        for a in range(n):
            for k in range(3):
                first(a, k).wait_send()
                passed(a, k).wait_send()
            local(a).wait()

    return start, mid, end


def _gather_chip4_halves(arrs):
    return _Exchange(arrs, [_sds((N_CHIPS,) + a.shape, a.dtype) for a in arrs], 6, _gather_halves_phases)


def _reduce_phases(n_whole, ins, outs, send_sems, recv_sems, loc_sems):
    n = len(ins)
    x, y, c = _my_place()
    chip = 2 * x + y
    sibling = (x, y, 1 - c)
    peers = [(_flip(x, dx), _flip(y, dy)) for dx, dy in CHIP_FLIPS]

    def remote(src, slot, a, k, to):
        return pltpu.make_async_remote_copy(src_ref=src, dst_ref=outs[a].at[slot], send_sem=send_sems.at[a, k],
                                            recv_sem=recv_sems.at[a, k], device_id=to, device_id_type=MESH_ID)

    def block(a, p):
        return ins[a] if a >= n - n_whole else ins[a].at[p]

    def local(a):
        return pltpu.make_async_copy(block(a, chip), outs[a].at[2 * chip + c], loc_sems.at[a])

    def own(a):
        return remote(block(a, chip), 2 * chip + c, a, 0, sibling)

    def first(a, k):
        return remote(block(a, 2 * peers[k][0] + peers[k][1]), 2 * chip + c, a, 1 + k, (*peers[k], c))

    def passed(a, k):
        slot = 2 * (2 * peers[k][0] + peers[k][1]) + c
        return remote(outs[a].at[slot], slot, a, 4 + k, sibling)

    def start():
        for a in range(n):
            local(a).start()
            own(a).start()
            for k in range(3):
                first(a, k).start()

    def mid():
        for a in range(n):
            for k in range(3):
                slot = 2 * (2 * peers[k][0] + peers[k][1]) + c
                remote(outs[a].at[slot], slot, a, 1 + k, sibling).wait_recv()
                passed(a, k).start()

    def end():
        for a in range(n):
            remote(outs[a].at[2 * chip + 1 - c], 2 * chip + 1 - c, a, 0, sibling).wait_recv()
            for k in range(3):
                slot = 2 * (2 * peers[k][0] + peers[k][1]) + 1 - c
                remote(outs[a].at[slot], slot, a, 4 + k, sibling).wait_recv()
        for a in range(n):
            own(a).wait_send()
            for k in range(3):
                first(a, k).wait_send()
                passed(a, k).wait_send()
            local(a).wait()

    return start, mid, end


def _reduce_exchange(arrs, whole=()):
    shapes = [_sds((N_DEV,) + a.shape[1:], a.dtype) for a in arrs] + [_sds((N_DEV,) + a.shape, a.dtype) for a in whole]
    return _Exchange(list(arrs) + list(whole), shapes, 7, functools.partial(_reduce_phases, len(whole)))


def _shard_halves_t(w):
    wt = w.T.astype(BF16)
    n2 = wt.shape[0] // 2
    pad = jnp.zeros((-n2 % 16, wt.shape[1]), BF16)
    return jnp.stack([jnp.concatenate([wt[0:n2], pad], axis=0), jnp.concatenate([wt[n2:], pad], axis=0)])


def _gathered_rows(g, n):
    return [g[p, half, 0:n // 2] for p in range(N_CHIPS) for half in range(2)]


def _even_in_layout_t(g):
    n2 = EVEN_IN // N_CHIPS // 2
    gap, gap_rows = EVEN_GAP, EVEN_P - EVEN_IN
    assert n2 % 2 == 0 and gap % 2 == 0 and gap_rows % 2 == 0
    spans = []
    for p in range(N_CHIPS):
        for half in range(2):
            lo = (2 * p + half) * n2
            if lo < gap < lo + n2:
                spans += [(p, half, 0, (gap - lo) // 2, lo // 2), (p, half, (gap - lo) // 2, (lo + n2 - gap) // 2, (gap + gap_rows) // 2)]
            else:
                spans.append((p, half, 0, n2 // 2, (lo + (gap_rows if lo >= gap else 0)) // 2))

    def body(g_ref, o_ref, pairs):
        pairs[pl.ds(gap // 2, gap_rows // 2), :] = jnp.zeros((gap_rows // 2, pairs.shape[1]), jnp.uint32)
        for p, half, src, rows, dst in spans:
            pairs[pl.ds(dst, rows), :] = pltpu.bitcast(g_ref[p, half], jnp.uint32)[src:src + rows]
        step = 128
        for r0 in range(0, EVEN_P // 2, step):
            o_ref[pl.ds(2 * r0, 2 * step), :] = pltpu.bitcast(pairs[pl.ds(r0, step), :], o_ref.dtype)

    return pl.pallas_call(body, name="even_w_in_rows", out_shape=_sds((EVEN_P, g.shape[-1]), g.dtype),
                          scratch_shapes=[pltpu.VMEM((EVEN_P // 2, g.shape[-1]), jnp.uint32)],
                          compiler_params=pltpu.CompilerParams(vmem_limit_bytes=VMEM_LIMIT))(g)


def _uq_layout_t(g):
    wt = jnp.concatenate(_gathered_rows(g, B_HEADS * (B_NOPE + B_ROPE) // N_CHIPS), axis=0)
    per = B_NOPE + B_ROPE
    pad = jnp.zeros((LANES - B_ROPE, wt.shape[1]), wt.dtype)
    nope = [wt[per * h:per * h + B_NOPE] for h in range(B_HEADS)]
    rope = [jnp.concatenate([wt[per * h + B_NOPE:per * (h + 1)], pad], axis=0) for h in range(B_HEADS)]
    return jnp.concatenate(nope + rope, axis=0)


def _block_diag(blocks):
    rows = []
    for h, blk in enumerate(blocks):
        r, cdim = blk.shape
        n = len(blocks)
        rows.append(jnp.concatenate([jnp.zeros((r, cdim * h), blk.dtype), blk, jnp.zeros((r, cdim * (n - 1 - h)), blk.dtype)],
                                    axis=1))
    return jnp.concatenate(rows, axis=0)


def _uk_layout(w):
    return _block_diag([w[:, h, :].T for h in range(B_HEADS)])


def _latent_rows(w):
    return jnp.transpose(w[0], (1, 2, 0)).reshape(-1, w.shape[1])


def _latent_unrows(w2, shape):
    return jnp.transpose(w2.reshape(shape[2], shape[3], shape[1]), (2, 0, 1)).reshape(shape)


def _uv_layout(w):
    return _block_diag([w[:, h, :] for h in range(B_HEADS)])


def _uv_unlayout(g):
    return jnp.concatenate([g[LANES * h:LANES * (h + 1), B_V * h:B_V * (h + 1)].T for h in range(B_HEADS)], axis=0)


def _rope_tables(S):
    inv = ROPE_THETA ** (-jnp.arange(0, 32, 2, dtype=F32) / 32)
    tok = jnp.arange(S)
    nf = inv.shape[0]
    angles = [pos.astype(F32)[:, None] * inv[None, :] for pos in (tok // GRID_W, tok % GRID_W, tok)]
    compact = jnp.concatenate([jnp.cos(a) for a in angles] + [jnp.sin(a) for a in angles], axis=1)

    lane = np.arange(LANES)
    axial = np.where(lane % 64 < 32, 0, nf) + lane % nf
    sign = np.where(lane % (2 * nf) < nf, -1.0, 1.0)
    select = np.zeros((4, 6 * nf, LANES), np.float32)
    select[0, axial, lane] = 1.0
    select[1, 3 * nf + axial, lane] = sign
    select[2, 2 * nf + lane % nf, lane] = 1.0
    select[3, 5 * nf + lane % nf, lane] = sign
    ts = min(ROW_TILE, S)

    def body(c_ref, sel_ref, *out_refs):
        for t, o_ref in enumerate(out_refs):
            o_ref[...] = _dot_f32(c_ref[...], sel_ref[t])

    return pl.pallas_call(body, name="rope_tables", grid=(S // ts,),
                          in_specs=[_row_spec(ts, 6 * nf), _full_spec((4, 6 * nf, LANES))],
                          out_specs=[_row_spec(ts, LANES)] * 4, out_shape=[_sds((S, LANES), F32)] * 4,
                          compiler_params=_params(("arbitrary",)))(compact, jnp.asarray(select))


A_TQ, A_TK, A_SUB = 512, 4096, 512
A_FWD_SUB = 1024
B_TQ, B_TK, B_SUB = 128, 4096, 1024
B_BWD_TK, B_BWD_SUB = 4096, 512
C_T = 256
C_BLOCKS_PER_STEP = 8
KV_SHARE = 2


def _local_step(x0, tgt, mod, norm_w, wie, wuq, wuk, wuv, late_shards, a_q_norm, a_k_norm, q_lora_norm, kv_lora_norm,
                c_sink, final_norm):
    S = x0.shape[0]
    mod3 = mod.reshape(2, 3, D_MODEL)
    ca, sa, ct, st = _rope_tables(S)
    lane_seg = np.arange(LANES) // HEAD_DIM
    seg = jnp.asarray((lane_seg[:, None] == lane_seg[None, :]).astype(np.float32)).astype(BF16)
    qn = jnp.tile(a_q_norm.reshape(1, HEAD_DIM), (1, 2))
    kn = jnp.tile(a_k_norm.reshape(1, HEAD_DIM), (1, 2))
    qln, kvln = q_lora_norm.reshape(1, B_Q_LORA), kv_lora_norm.reshape(1, B_KV_LORA)
    nw0, nw1 = norm_w[0:1], norm_w[1:2]
    gate0, gate1 = mod3[0, 2:3], mod3[1, 2:3]
    a_tq, a_tk, b_tq, b_tk, bb_tk, c_t = min(A_TQ, S), min(A_TK, S), min(B_TQ, S), min(B_TK, S), min(B_BWD_TK, S), min(C_T, S)
    a_sub, b_sub, bb_sub = min(A_SUB, a_tk), min(B_SUB, b_tk), min(B_BWD_SUB, bb_tk)

    h0, proj_e, qa, ka, va, qcat, kcat, ka_t, va_t, kcat_t = _even_pre_fwd(x0, mod3[0], nw0, wie, qn, kn, seg, ca, sa, ct, st,
                                                                           qln, kvln, wuq, wuk)
    oa, lse_a, woe_g, wio_g, woo_g = _pp_fwd(qa, ka, va_t, kdiv=KV_SHARE, tq=a_tq, tk=a_tk, sub=min(A_FWD_SUB, a_tk), name="attn_a_fwd",
                                             side=_gather_chip4_halves(late_shards))
    woe = woe_g.reshape(D_MODEL, D_MODEL)
    wio = wio_g.reshape(N_CHIPS, D_MODEL, ODD_IN // N_CHIPS)
    woo = woo_g.reshape(D_MODEL, D_MODEL)
    olat, lse_b = _mla_fwd(qcat, kcat, kcat_t, tq=b_tq, tk=b_tk, sub=b_sub)
    y0, x1 = _even_post_fwd(oa, olat, proj_e, x0, gate0, wuv, woe)
    h1, gc, qc, kc, vc, kc_t, vc_t = _odd_pre_fwd(x1, mod3[1], nw1, wio)
    slopes = 2.0 ** (-8.0 * jnp.arange(1, C_HEADS + 1, dtype=F32) / C_HEADS)
    slope_rows = jnp.repeat(slopes.reshape(C_HEADS // 2, 2), c_t, axis=1)[:, None, :]
    sink_rows = jnp.repeat(c_sink.reshape(C_HEADS // 2, 2), c_t, axis=1)[:, None, :]
    win_dist = _win_dist_table(S, c_t)
    oc, lse_c = _win_fwd(qc, kc, vc_t, win_dist, slope_rows, sink_rows, kdiv=KV_SHARE, tq=c_t, nbs=C_BLOCKS_PER_STEP,
                         name="attn_c_fwd")
    doc, dgc, dx2, dwoo, st_f = _odd_post(oc, gc, x1, gate1, woo, final_norm.reshape(1, D_MODEL), tgt)
    dqc, dkc, dvc, dsink_raw = _win_bwd(qc, kc, kc_t, vc, oc, doc, lse_c, win_dist, slope_rows, sink_rows, kdiv=KV_SHARE, tq=c_t,
                                        nbs=C_BLOCKS_PER_STEP, name="attn_c_bwd")
    dx1, dwio, st_1 = _odd_pre_bwd(dqc, dkc, dvc, dgc, h1, x1, dx2, mod3[1], nw1, wio)
    doa, dga, dgb, dolat, dwoe, dwuv, st_e = _even_post_bwd(dx1, y0, oa, olat, proj_e, gate0, wuv, woe)
    late_grads = _reduce_exchange([dwoe.reshape(N_CHIPS, D_MODEL // N_CHIPS, D_MODEL), dwio,
                                   dwoo.reshape(N_CHIPS, D_MODEL // N_CHIPS, D_MODEL)], whole=[_uv_unlayout(dwuv).astype(BF16)])
    dqa, dka, dva, p_woe, p_wio, p_woo, p_wuv = _pp_bwd(qa, ka, ka_t, va, oa, doa, lse_a, kdiv=KV_SHARE, tq=a_tq, tk=a_tk,
                                                        sub=a_sub, name="attn_a_bwd", side=late_grads)
    dqcat, dkcat = _mla_bwd(qcat, kcat, kcat_t, olat, dolat, lse_b, tq=b_tq, tk=bb_tk, sub=bb_sub)
    dx0, dwie, dwuq, dwuk, st_0, nst = _even_pre_bwd(x0, h0, proj_e, dqa, dka, dva, dga, dgb, dqcat, dkcat, dx1, mod3[0], nw0,
                                                     wie, qn, kn, seg, ca, sa, ct, st, qln, kvln, wuq, wuk)
    dsink_pairs = jnp.stack([dsink_raw[:, 0, 0], dsink_raw[:, 1, 0]], axis=1).reshape(C_HEADS)
    return dict(
        loss_row=st_f[2:3], dx=dx0,
        dmod=jnp.stack([jnp.concatenate([st_0[0], st_0[1], st_e[0]]), jnp.concatenate([st_1[0], st_1[1], st_f[1]])]),
        norm_w=jnp.stack([st_0[2], st_1[2]]), final_norm=st_f[0],
        a_q_norm=nst[0:1, 0:HEAD_DIM], a_k_norm=nst[1:2, 0:HEAD_DIM], b_q_lora_norm=nst[2:3, :], b_kv_lora_norm=nst[3:4, 0:LANES],
        c_sink=dsink_pairs.reshape(1, C_HEADS),
        even_w_in=dwie, b_w_uq=dwuq, b_w_uk=dwuk, b_w_uv=p_wuv, even_w_out=p_woe, odd_w_in=p_wio, odd_w_out=p_woo)


WEIGHT_NAMES = ("norm_w", "ada_w", "ada_b", "even_w_in", "a_q_norm", "a_k_norm", "b_q_lora_norm", "b_kv_lora_norm", "b_w_uq",
                "b_w_uk", "b_w_uv", "even_w_out", "odd_w_in", "c_sink", "odd_w_out", "final_norm")


def kernel(x, c, norm_w, ada_w, ada_b, even_w_in, a_q_norm, a_k_norm, b_q_lora_norm, b_kv_lora_norm, b_w_uq, b_w_uk, b_w_uv, even_w_out, odd_w_in, c_sink, odd_w_out, final_norm, loss_target, m_norm_w, m_ada_w, m_ada_b, m_even_w_in, m_a_q_norm, m_a_k_norm, m_b_q_lora_norm, m_b_kv_lora_norm, m_b_w_uq, m_b_w_uk, m_b_w_uv, m_even_w_out, m_odd_w_in, m_c_sink, m_odd_w_out, m_final_norm, v_norm_w, v_ada_w, v_ada_b, v_even_w_in, v_a_q_norm, v_a_k_norm, v_b_q_lora_norm, v_b_kv_lora_norm, v_b_w_uq, v_b_w_uk, v_b_w_uv, v_even_w_out, v_odd_w_in, v_c_sink, v_odd_w_out, v_final_norm):
    given = dict(locals())
    xi, yi, ci = _my_place()
    chip = 2 * xi + yi
    dev = 2 * chip + ci
    n_ada = ada_w.shape[2]

    (c_all,) = _gather_dev8([c], "gather_c")
    c_all = c_all.reshape(N_DEV, D_MODEL)
    bias = lax.dynamic_slice_in_dim(ada_b, chip * n_ada, n_ada, axis=1).reshape(2, 1, n_ada)
    mod_cols = _ada_fwd(c_all, ada_w, bias)
    def halves(w):
        return w.astype(BF16).reshape((2, w.shape[0] // 2) + w.shape[1:])

    mod_all, wie_g, wuq_g = _gather_chip4_halves(
        [mod_cols, _shard_halves_t(even_w_in[0]), _shard_halves_t(b_w_uq[0])]).run("gather_weights")
    mod = jnp.transpose(lax.dynamic_index_in_dim(mod_all, dev, axis=2, keepdims=False), (1, 0, 2)).reshape(2, 3 * D_MODEL)

    res = _local_step(
        x[0], loss_target[0], mod, norm_w,
        _even_in_layout_t(wie_g), _uq_layout_t(wuq_g), _uk_layout(b_w_uk[0].astype(BF16)),
        _uv_layout(b_w_uv[0].astype(BF16)), [halves(even_w_out[0]), halves(odd_w_in[0]), halves(odd_w_out[0])],
        a_q_norm, a_k_norm, b_q_lora_norm, b_kv_lora_norm, c_sink, final_norm)

    p_wie, p_wuq, small_all, p_wuk = _reduce_exchange(
        [res["even_w_in"].reshape(N_CHIPS, EVEN_IN // N_CHIPS, D_MODEL),
         res["b_w_uq"].reshape(N_CHIPS, -1, B_Q_LORA)],
        whole=[_pack_small(res), res["b_w_uk"]],
    ).run("reduce_exchange")
    shard_parts = dict(even_w_in=p_wie, b_w_uq=p_wuq, **{k: res[k] for k in ("even_w_out", "odd_w_in", "odd_w_out")})
    dmod_all = small_all[:, 0:6, :].reshape(N_DEV, 2, 3 * D_MODEL)
    dmod_cols = jnp.transpose(lax.dynamic_slice_in_dim(dmod_all, chip * n_ada, n_ada, axis=2), (1, 0, 2))
    parts = dict(shard_parts)
    parts["ada_w"] = _ada_bwd(c_all.T, dmod_cols).reshape(1, 2 * D_MODEL, n_ada)
    parts["b_w_uk"], parts["b_w_uv"] = p_wuk, res["b_w_uv"]

    def as2d(a):
        return a.reshape((-1, a.shape[-1]) if a.ndim > 1 else (1, a.shape[0]))

    results = {}
    small_outs = _adam_small(small_all, *[[as2d(given[pre + k]) for k in SMALL_WEIGHTS] for pre in ("", "m_", "v_")])
    for idx, k in enumerate(SMALL_WEIGHTS):
        results[k] = small_outs[4 * idx:4 * idx + 4]
    for k, p in parts.items():
        if k in ("even_w_in", "b_w_uq"):
            outs = _adam(p, given[k][0].T, given["m_" + k][0].T, given["v_" + k][0].T, "adam_" + k, by_columns=k == "even_w_in")
            results[k] = [o.T for o in outs]
            continue
        if k in ("b_w_uk", "b_w_uv"):
            outs = _adam(p, _latent_rows(given[k]), _latent_rows(given["m_" + k]), _latent_rows(given["v_" + k]), "adam_" + k)
            results[k] = [_latent_unrows(o, given[k].shape) for o in outs]
            continue
        shape2 = (p.shape[-2], p.shape[-1])
        results[k] = _adam(p, given[k].reshape(shape2), given["m_" + k].reshape(shape2), given["v_" + k].reshape(shape2),
                           "adam_" + k)
    by_kind = [[results[k][t].reshape(given[k].shape) for k in WEIGHT_NAMES] for t in range(4)]
    return (small_outs[-1][0, 0], res["dx"][None], *by_kind[0], *by_kind[1], *by_kind[2], *by_kind[3])
```

```python
import functools

import numpy as np
import jax
import jax.numpy as jnp
from jax import lax
from jax.experimental import pallas as pl
from jax.experimental.pallas import tpu as pltpu

F32 = jnp.float32
BF16 = jnp.bfloat16
HIGHEST = lax.Precision.HIGHEST
MESH_ID = pl.DeviceIdType.MESH

D_MODEL = 1024
HEAD_DIM = 64
GRID_W = 64
EPS = 1e-6
ROPE_THETA = 10000.0
B_HEADS, B_NOPE, B_ROPE, B_V = 8, 64, 32, 64
B_Q_LORA, B_KV_LORA = 256, 128
C_HEADS = 16
WINDOW = 128
EVEN_IN, ODD_IN = 2208, 2560
EVEN_P = 2304
EVEN_GAP = 1696
N_CHIPS, N_DEV = 4, 8
LANES = 128
NEG = -1e30
VMEM_LIMIT = 60 * 1024 * 1024

ADAM_LR, ADAM_B1, ADAM_B2, ADAM_EPS, ADAM_WD, ADAM_STEP = 0.001, 0.9, 0.999, 1e-08, 0.01, 10

ROW_TILE = 512
IN_PROJ_ROW_TILE = 256


def _dot(a, b):
    return lax.dot_general(a, b, (((1,), (0,)), ((), ())), preferred_element_type=F32)


def _dot_nt(a, b):
    return lax.dot_general(a, b, (((1,), (1,)), ((), ())), preferred_element_type=F32)


def _dot_tn(a, b):
    return lax.dot_general(a, b, (((0,), (0,)), ((), ())), preferred_element_type=F32)


def _dot_f32(a, b):
    return lax.dot_general(a, b, (((1,), (0,)), ((), ())), precision=HIGHEST, preferred_element_type=F32)


def _sigmoid(x):
    return 1.0 / (1.0 + jnp.exp(-x))


def _silu_and_grad(g):
    s = _sigmoid(g)
    return g * s, s * (1.0 + g * (1.0 - s))


def _lane_iota():
    return lax.broadcasted_iota(jnp.int32, (1, LANES), 1)


def _partner(x, lane):
    return jnp.where((lane % 32) < 16, pltpu.roll(x, LANES - 16, 1), pltpu.roll(x, 16, 1))


def _rot(x, cos, sin_signed, lane):
    return x * cos + _partner(x, lane) * sin_signed


def _rot_bwd(dy, cos, sin_signed, lane):
    return dy * cos + _partner(dy * sin_signed, lane)


def _rms(x):
    return lax.rsqrt(jnp.mean(x * x, axis=-1, keepdims=True) + EPS)


def _rms_bwd(x, r, g):
    return r * g - x * (r * r * r) * jnp.mean(x * g, axis=-1, keepdims=True)


def _seg_mean(v, seg_ones):
    hi = v.astype(BF16)
    lo = (v - hi.astype(F32)).astype(BF16)
    return (_dot(hi, seg_ones) + _dot(lo, seg_ones)) * (1.0 / HEAD_DIM)


def _dup_heads(x, lane):
    swapped = pltpu.roll(x, HEAD_DIM, 1)
    lo = lane < HEAD_DIM
    return jnp.concatenate([jnp.where(lo, x, swapped), jnp.where(lo, swapped, x)], axis=1)


def _fold_heads(x2, lane):
    a, b = x2[:, 0:LANES], x2[:, LANES:2 * LANES]
    return jnp.where(lane < HEAD_DIM, a + pltpu.roll(a, HEAD_DIM, 1), b + pltpu.roll(b, HEAD_DIM, 1))


def _row_spec(ts, cols):
    return pl.BlockSpec((ts, cols), lambda i: (i, 0))


def _full_spec(shape, single=True):
    nd = len(shape)
    if single:
        return pl.BlockSpec(shape, lambda i: (0,) * nd, pipeline_mode=pl.Buffered(1))
    return pl.BlockSpec(shape, lambda i: (0,) * nd)


def _sds(shape, dtype):
    return jax.ShapeDtypeStruct(shape, dtype)


def _params(sem):
    return pltpu.CompilerParams(dimension_semantics=sem, vmem_limit_bytes=VMEM_LIMIT)


def _even_pre_fwd(x, mod, nw, wie, qn, kn, seg, ca, sa, ct, st, qln, kvln, wuq, wuk):
    S = x.shape[0]
    ts = min(IN_PROJ_ROW_TILE, S)

    def body(x_ref, mod_ref, nw_ref, wie_ref, qn_ref, kn_ref, seg_ref, ca_ref, sa_ref, ct_ref, st_ref, qln_ref,
             kvln_ref, wuq_ref, wuk_ref, h_ref, proj_ref, qa_ref, ka_ref, va_ref, qcat_ref, kcat_ref, kat_ref, vat_ref, kcatt_ref):
        xv = x_ref[...]
        h = (xv * _rms(xv) * nw_ref[...]) * (1.0 + mod_ref[1:2, :]) + mod_ref[0:1, :]
        hb = h.astype(BF16)
        h_ref[...] = hb
        proj = _dot_nt(hb, wie_ref[...])
        proj_ref[...] = proj
        lane = _lane_iota()
        ca_v, sa_v, ct_v, st_v = ca_ref[...], sa_ref[...], ct_ref[...], st_ref[...]
        seg_v = seg_ref[...]
        for cb in range(4):
            xc = proj[:, LANES * cb:LANES * (cb + 1)]
            r = lax.rsqrt(_seg_mean(xc * xc, seg_v) + EPS)
            y = _rot(xc * r * qn_ref[...], ca_v, sa_v, lane)
            qa_ref[:, LANES * cb:LANES * (cb + 1)] = (y * 0.125).astype(BF16)
        kc = proj[:, 512:640]
        r = lax.rsqrt(_seg_mean(kc * kc, seg_v) + EPS)
        ka_v = _dup_heads(_rot(kc * r * kn_ref[...], ca_v, sa_v, lane), lane)
        ka_ref[...] = ka_v.astype(BF16)
        kat_ref[...] = ka_v.T.astype(BF16)
        va_v = _dup_heads(proj[:, 640:768], lane)
        va_ref[...] = va_v.astype(BF16)
        vat_ref[...] = va_v.T.astype(BF16)
        cq = proj[:, 1280:1536]
        cqn = (cq * _rms(cq) * qln_ref[...]).astype(BF16)
        ckv = proj[:, 1536:1664]
        ckvn = ckv * _rms(ckv) * kvln_ref[...]
        qb = _dot_nt(cqn, wuq_ref[...])
        qlat = _dot(qb[:, 0:512].astype(BF16), wuk_ref[...])
        for hh in range(B_HEADS):
            qcat_ref[hh, :, 0:LANES] = qlat[:, LANES * hh:LANES * (hh + 1)].astype(BF16)
            qr = _rot(qb[:, 512 + LANES * hh:512 + LANES * (hh + 1)], ct_v, st_v, lane)
            qcat_ref[hh, :, LANES:2 * LANES] = qr.astype(BF16)
        kr = _rot(proj[:, 1664:1792], ct_v, st_v, lane)
        kcat_ref[:, 0:LANES] = ckvn.astype(BF16)
        kcat_ref[:, LANES:2 * LANES] = kr.astype(BF16)
        kcatt_ref[0:LANES, :] = ckvn.T.astype(BF16)
        kcatt_ref[LANES:2 * LANES, :] = kr.T.astype(BF16)

    col_spec = lambda rows: pl.BlockSpec((rows, ts), lambda i: (0, i))
    return pl.pallas_call(
        body, name="even_pre_fwd", grid=(S // ts,),
        in_specs=[_row_spec(ts, D_MODEL), _full_spec((3, D_MODEL)), _full_spec((1, D_MODEL)), _full_spec((EVEN_P, D_MODEL)),
                  _full_spec((1, LANES)), _full_spec((1, LANES)), _full_spec((LANES, LANES)),
                  _row_spec(ts, LANES), _row_spec(ts, LANES), _row_spec(ts, LANES), _row_spec(ts, LANES),
                  _full_spec((1, B_Q_LORA)), _full_spec((1, B_KV_LORA)), _full_spec((1536, B_Q_LORA)), _full_spec((512, 1024))],
        out_specs=[_row_spec(ts, D_MODEL), _row_spec(ts, EVEN_P), _row_spec(ts, 512), _row_spec(ts, 2 * LANES), _row_spec(ts, 2 * LANES),
                   pl.BlockSpec((B_HEADS, ts, 2 * LANES), lambda i: (0, i, 0)), _row_spec(ts, 2 * LANES),
                   col_spec(2 * LANES), col_spec(2 * LANES), col_spec(2 * LANES)],
        out_shape=[_sds((S, D_MODEL), BF16), _sds((S, EVEN_P), F32), _sds((S, 512), BF16), _sds((S, 2 * LANES), BF16),
                   _sds((S, 2 * LANES), BF16), _sds((B_HEADS, S, 2 * LANES), BF16), _sds((S, 2 * LANES), BF16),
                   _sds((2 * LANES, S), BF16), _sds((2 * LANES, S), BF16), _sds((2 * LANES, S), BF16)],
        compiler_params=_params(("arbitrary",)),
    )(x, mod, nw, wie, qn, kn, seg, ca, sa, ct, st, qln, kvln, wuq, wuk)


MLA_SCALE = (B_NOPE + B_ROPE) ** -0.5
LOG2E = 1.4426950408889634

def _row_lo():
    return lax.broadcasted_iota(jnp.int32, (LANES, 1), 0) < HEAD_DIM


def _stack_cols(vT, rlo):
    zero = jnp.zeros_like(vT)
    return jnp.concatenate([jnp.where(rlo, vT, zero), jnp.where(rlo, zero, vT)], axis=1)


def _stack_rows(v, lo):
    zero = jnp.zeros_like(v)
    return jnp.concatenate([jnp.where(lo, v, zero), jnp.where(lo, zero, v)], axis=0)


def _pick_halves_T(xT, rlo, t):
    return jnp.where(rlo, xT[:, 0:t], xT[:, t:2 * t]).T


def _side_split(refs, n_in, n_out, n_scratch, side):
    ns = side.n if side is not None else 0
    cuts = np.cumsum([0, n_in, ns, n_out, ns, n_scratch])
    return [refs[a:b] for a, b in zip(cuts[:-1], cuts[1:])] + [refs[cuts[-1]:]]


def _side_hooks(side, side_ins, side_outs, side_sems, step, total):
    if side is None:
        return lambda: None
    start, mid, end = side.phases(side_ins, side_outs, side_sems)
    pl.when(step == 0)(start)
    pl.when(step == total // 2)(mid)
    return lambda: pl.when(step == total - 1)(end)


def _side_specs(side):
    if side is None:
        return [], [], [], [], []
    return list(side.arrs), [_ANY] * side.n, [_ANY] * side.n, list(side.out_shapes), side.sem_shapes()


def _pp_fwd(q, k, vT, *, kdiv, tq, tk, sub, name, side=None):
    S = k.shape[0]; nb = q.shape[1] // LANES; nq = S // tq; nkv = S // tk; nsub = tk // sub

    def body(*refs):
        (q_ref, k_ref, vT_ref), side_ins, (o_ref, lse_ref), side_outs, (qs, m_s, l_s, acc), side_sems = _side_split(refs, 3, 2, 4, side)
        j = pl.program_id(2)
        rlo = _row_lo()
        step = (pl.program_id(0) * nq + pl.program_id(1)) * nkv + j
        side_end = _side_hooks(side, side_ins, side_outs, side_sems, step, nb * nq * nkv)

        @pl.when(j == 0)
        def _():
            qs[...] = _stack_cols(q_ref[...].astype(F32).T, rlo).astype(BF16)
            m_s[...] = jnp.full((1, 2 * tq), NEG, F32)
            l_s[...] = jnp.zeros((1, 2 * tq), F32)
            acc[...] = jnp.zeros((LANES, 2 * tq), F32)

        qsv = qs[...]
        m, l, a = m_s[...], l_s[...], acc[...]
        s_cur = _dot(k_ref[0:sub, :], qsv)
        for t in range(nsub):
            if t + 1 < nsub:
                s_next = _dot(k_ref[sub * (t + 1):sub * (t + 2), :], qsv)
            m_new = jnp.maximum(m, jnp.max(s_cur, axis=0, keepdims=True))
            alpha = jnp.exp(m - m_new)
            p = jnp.exp(s_cur - m_new)
            l = alpha * l + jnp.sum(p, axis=0, keepdims=True)
            a = alpha * a + _dot(vT_ref[:, sub * t:sub * (t + 1)], p.astype(BF16))
            m = m_new
            if t + 1 < nsub:
                s_cur = s_next
        m_s[...], l_s[...], acc[...] = m, l, a

        @pl.when(j == nkv - 1)
        def _():
            l_f = l_s[...]
            o_ref[...] = _pick_halves_T(acc[...] / l_f, rlo, tq).astype(BF16)
            lse_ref[0, 0] = m_s[...] + jnp.log(l_f)

        side_end()

    s_args, s_in, s_out, s_shapes, s_sems = _side_specs(side)
    return pl.pallas_call(
        body, name=name, grid=(nb, nq, nkv),
        in_specs=[pl.BlockSpec((tq, LANES), lambda b, i, j: (i, b)), pl.BlockSpec((tk, LANES), lambda b, i, j: (j, b // kdiv)),
                  pl.BlockSpec((LANES, tk), lambda b, i, j: (b // kdiv, j))] + s_in,
        out_specs=[pl.BlockSpec((tq, LANES), lambda b, i, j: (i, b)),
                   pl.BlockSpec((1, 1, 1, 2 * tq), lambda b, i, j: (b, i, 0, 0))] + s_out,
        out_shape=[_sds((S, nb * LANES), BF16), _sds((nb, nq, 1, 2 * tq), F32)] + s_shapes,
        scratch_shapes=[pltpu.VMEM((LANES, 2 * tq), BF16), pltpu.VMEM((1, 2 * tq), F32), pltpu.VMEM((1, 2 * tq), F32),
                        pltpu.VMEM((LANES, 2 * tq), F32)] + s_sems,
        compiler_params=_params(("arbitrary",) * 3))(q, k, vT, *s_args)


def _pp_bwd(q, k, kT, v, o, do, lse, *, kdiv, tq, tk, sub, name, side=None):
    S = k.shape[0]; nb = q.shape[1] // LANES; nkb = k.shape[1] // LANES; nq = S // tq; nkv = S // tk; nsub = tk // sub

    def body(*refs):
        ((q_ref, k_ref, kT_ref, v_ref, o_ref, do_ref, lse_ref), side_ins, (dq_ref, dk_ref, dv_ref), side_outs,
         (qsT, qs, dosT, dos, delta_s, dq_acc), side_sems) = _side_split(refs, 7, 3, 6, side)
        b, i, j = pl.program_id(0), pl.program_id(1), pl.program_id(2)
        rlo = _row_lo()
        lo = lax.broadcasted_iota(jnp.int32, (1, LANES), 1) < HEAD_DIM
        side_end = _side_hooks(side, side_ins, side_outs, side_sems, (b * nq + i) * nkv + j, nb * nq * nkv)

        @pl.when((b % kdiv == 0) & (i == 0) & (j == 0))
        def _():
            dk_ref[...] = jnp.zeros((S, LANES), F32)
            dv_ref[...] = jnp.zeros((S, LANES), F32)

        @pl.when(j == 0)
        def _():
            qv = q_ref[...]
            qs[...] = _stack_rows(qv, lo)
            qsT[...] = _stack_cols(qv.astype(F32).T, rlo).astype(BF16)
            dov = do_ref[...].astype(F32)
            dos[...] = _stack_rows(dov.astype(BF16), lo)
            dosT[...] = _stack_cols(dov.T, rlo).astype(BF16)
            prodT = (dov * o_ref[...].astype(F32)).T
            delta_s[...] = jnp.concatenate([jnp.sum(jnp.where(rlo, prodT, 0.0), axis=0, keepdims=True),
                                            jnp.sum(jnp.where(rlo, 0.0, prodT), axis=0, keepdims=True)], axis=1)
            dq_acc[...] = jnp.zeros((LANES, 2 * tq), F32)

        qsTv, dosTv, qsv, dosv = qsT[...], dosT[...], qs[...], dos[...]
        lse_v, delta_v = lse_ref[0, 0], delta_s[...]
        dqa = dq_acc[...]
        s_cur = _dot(k_ref[0:sub, :], qsTv)
        dp_cur = _dot(v_ref[0:sub, :], dosTv)
        for t in range(nsub):
            if t + 1 < nsub:
                s_next = _dot(k_ref[sub * (t + 1):sub * (t + 2), :], qsTv)
                dp_next = _dot(v_ref[sub * (t + 1):sub * (t + 2), :], dosTv)
            p = jnp.exp(s_cur - lse_v)
            ds = (p * (dp_cur - delta_v)).astype(BF16)
            rows = pl.ds(pl.multiple_of(j * tk + sub * t, sub), sub)
            dv_ref[rows, :] += _dot(p.astype(BF16), dosv)
            dk_ref[rows, :] += _dot(ds, qsv)
            dqa = dqa + _dot(kT_ref[:, sub * t:sub * (t + 1)], ds)
            if t + 1 < nsub:
                s_cur, dp_cur = s_next, dp_next
        dq_acc[...] = dqa

        @pl.when(j == nkv - 1)
        def _():
            dq_ref[...] = _pick_halves_T(dq_acc[...], rlo, tq)

        side_end()

    qmap = lambda b, i, j: (i, b)
    kmap = lambda b, i, j: (j, b // kdiv)
    res = lambda b, i, j: (0, b // kdiv)
    s_args, s_in, s_out, s_shapes, s_sems = _side_specs(side)
    return pl.pallas_call(
        body, name=name, grid=(nb, nq, nkv),
        in_specs=[pl.BlockSpec((tq, LANES), qmap), pl.BlockSpec((tk, LANES), kmap), pl.BlockSpec((LANES, tk), lambda b, i, j: (b // kdiv, j)),
                  pl.BlockSpec((tk, LANES), kmap), pl.BlockSpec((tq, LANES), qmap), pl.BlockSpec((tq, LANES), qmap),
                  pl.BlockSpec((1, 1, 1, 2 * tq), lambda b, i, j: (b, i, 0, 0))] + s_in,
        out_specs=[pl.BlockSpec((tq, LANES), qmap), pl.BlockSpec((S, LANES), res), pl.BlockSpec((S, LANES), res)] + s_out,
        out_shape=[_sds((S, nb * LANES), F32), _sds((S, nkb * LANES), F32), _sds((S, nkb * LANES), F32)] + s_shapes,
        scratch_shapes=[pltpu.VMEM((LANES, 2 * tq), BF16), pltpu.VMEM((2 * tq, LANES), BF16), pltpu.VMEM((LANES, 2 * tq), BF16),
                        pltpu.VMEM((2 * tq, LANES), BF16), pltpu.VMEM((1, 2 * tq), F32), pltpu.VMEM((LANES, 2 * tq), F32)] + s_sems,
        compiler_params=_params(("arbitrary",) * 3))(q, k, kT, v, o, do, lse, *s_args)


MLA_C = MLA_SCALE * LOG2E


def _mla_fwd(q, kcat, kcatT, *, tq, tk, sub):
    S = kcat.shape[0]; nq, nkv = S // tq, S // tk; R = B_HEADS * tq; nsub = tk // sub

    def body(q_ref, k_ref, vT_ref, o_ref, lse_ref, qT, m_s, l_s, acc):
        j = pl.program_id(1)

        @pl.when(j == 0)
        def _():
            qT[...] = q_ref[...].reshape(R, 2 * LANES).astype(F32).T.astype(BF16)
            m_s[...] = jnp.full((1, R), NEG, F32)
            l_s[...] = jnp.zeros((1, R), F32)
            acc[...] = jnp.zeros((LANES, R), F32)

        qTv = qT[...]
        m, l, a = m_s[...], l_s[...], acc[...]
        s_cur = _dot(k_ref[0:sub, :], qTv)
        for t in range(nsub):
            if t + 1 < nsub:
                s_next = _dot(k_ref[sub * (t + 1):sub * (t + 2), :], qTv)
            m_new = jnp.maximum(m, jnp.max(s_cur, axis=0, keepdims=True))
            alpha = jnp.exp2((m - m_new) * MLA_C)
            p = jnp.exp2((s_cur - m_new) * MLA_C)
            l = alpha * l + jnp.sum(p, axis=0, keepdims=True)
            a = alpha * a + _dot(vT_ref[:, sub * t:sub * (t + 1)], p.astype(BF16))
            m = m_new
            if t + 1 < nsub:
                s_cur = s_next
        m_s[...], l_s[...], acc[...] = m, l, a

        @pl.when(j == nkv - 1)
        def _():
            l_f = l_s[...]
            o_ref[...] = (acc[...] / l_f).T.reshape(B_HEADS, tq, LANES).astype(BF16)
            lse_ref[0] = m_s[...] * MLA_SCALE + jnp.log(l_f)

    return pl.pallas_call(
        body, name="mla_fwd", grid=(nq, nkv),
        in_specs=[pl.BlockSpec((B_HEADS, tq, 2 * LANES), lambda i, j: (0, i, 0)), pl.BlockSpec((tk, 2 * LANES), lambda i, j: (j, 0)),
                  pl.BlockSpec((LANES, tk), lambda i, j: (0, j))],
        out_specs=[pl.BlockSpec((B_HEADS, tq, LANES), lambda i, j: (0, i, 0)), pl.BlockSpec((1, 1, R), lambda i, j: (i, 0, 0))],
        out_shape=[_sds((B_HEADS, S, LANES), BF16), _sds((nq, 1, R), F32)],
        scratch_shapes=[pltpu.VMEM((2 * LANES, R), BF16), pltpu.VMEM((1, R), F32), pltpu.VMEM((1, R), F32), pltpu.VMEM((LANES, R), F32)],
        compiler_params=_params(("arbitrary", "arbitrary")))(q, kcat, kcatT)


def _mla_bwd(q, kcat, kcatT, o, do, lse, *, tq, tk, sub):
    S = kcat.shape[0]; nq, nkv = S // tq, S // tk; R = B_HEADS * tq; nsub = tk // sub

    def body(q_ref, k_ref, kT_ref, o_ref, do_ref, lse_ref, dq_ref, dk_ref, qT, dosT, dos, delta_s, dq_acc):
        i, j = pl.program_id(0), pl.program_id(1)

        @pl.when((i == 0) & (j == 0))
        def _():
            dk_ref[...] = jnp.zeros((S, 2 * LANES), F32)

        @pl.when(j == 0)
        def _():
            qT[...] = q_ref[...].reshape(R, 2 * LANES).astype(F32).T.astype(BF16)
            dov = do_ref[...].reshape(R, LANES).astype(F32)
            dos[...] = dov.astype(BF16)
            dosT[...] = dov.T.astype(BF16)
            delta_s[...] = jnp.sum((dov * o_ref[...].reshape(R, LANES).astype(F32)).T, axis=0, keepdims=True)
            dq_acc[...] = jnp.zeros((2 * LANES, R), F32)

        qTv, dosTv, dosv = qT[...], dosT[...], dos[...]
        qv = q_ref[...].reshape(R, 2 * LANES)
        lse_v, delta_v = lse_ref[0] * LOG2E, delta_s[...]
        dqa = dq_acc[...]
        s_cur = _dot(k_ref[0:sub, :], qTv)
        dp_cur = _dot(k_ref[0:sub, 0:LANES], dosTv)
        for t in range(nsub):
            if t + 1 < nsub:
                s_next = _dot(k_ref[sub * (t + 1):sub * (t + 2), :], qTv)
                dp_next = _dot(k_ref[sub * (t + 1):sub * (t + 2), 0:LANES], dosTv)
            p = jnp.exp2(s_cur * MLA_C - lse_v)
            ds = (p * (dp_cur - delta_v) * MLA_SCALE).astype(BF16)
            rows = pl.ds(pl.multiple_of(j * tk + sub * t, sub), sub)
            dk_ref[rows, :] += _dot(ds, qv)
            dk_ref[rows, 0:LANES] += _dot(p.astype(BF16), dosv)
            dqa = dqa + _dot(kT_ref[:, sub * t:sub * (t + 1)], ds)
            if t + 1 < nsub:
                s_cur, dp_cur = s_next, dp_next
        dq_acc[...] = dqa

        @pl.when(j == nkv - 1)
        def _():
            dq_ref[...] = dq_acc[...].T.reshape(B_HEADS, tq, 2 * LANES)

    hspec = lambda w: pl.BlockSpec((B_HEADS, tq, w), lambda i, j: (0, i, 0))
    return pl.pallas_call(
        body, name="mla_bwd", grid=(nq, nkv),
        in_specs=[hspec(2 * LANES), pl.BlockSpec((tk, 2 * LANES), lambda i, j: (j, 0)), pl.BlockSpec((2 * LANES, tk), lambda i, j: (0, j)),
                  hspec(LANES), hspec(LANES), pl.BlockSpec((1, 1, R), lambda i, j: (i, 0, 0))],
        out_specs=[hspec(2 * LANES), pl.BlockSpec((S, 2 * LANES), lambda i, j: (0, 0))],
        out_shape=[_sds((B_HEADS, S, 2 * LANES), F32), _sds((S, 2 * LANES), F32)],
        scratch_shapes=[pltpu.VMEM((2 * LANES, R), BF16), pltpu.VMEM((LANES, R), BF16), pltpu.VMEM((R, LANES), BF16),
                        pltpu.VMEM((1, R), F32), pltpu.VMEM((2 * LANES, R), F32)],
        compiler_params=_params(("arbitrary", "arbitrary")))(q, kcat, kcatT, o, do, lse)


def _win_start(i, tq, nk, S):
    return pl.multiple_of(jnp.clip(i * tq - WINDOW, 0, S - nk), LANES)


def _win_dist_table(S, tq):
    nk = min(tq + 2 * WINDOW, S)
    nq = S // tq
    r = np.arange(nk)[:, None]
    c = (np.arange(2 * tq) % tq)[None, :]
    tabs = []
    for rel in (0, WINDOW, (nq - 1) * tq - (S - nk)):
        dist = np.abs(rel + c - r).astype(np.float32)
        tabs.append(np.where(dist <= WINDOW, dist, np.float32(1e32)))
    return jnp.asarray(np.stack(tabs))


def _win_dist_spec(nk, tq, nq):
    return pl.BlockSpec((1, nk, 2 * tq), lambda b, i: (jnp.where(i == 0, 0, jnp.where(i == nq - 1, 2, 1)), 0, 0))


def _win_fwd(q, k, vT, dist, slope, sink, *, kdiv, tq, nbs, name):
    S = k.shape[0]; nb = q.shape[1] // LANES; nq = S // tq; nk = min(tq + 2 * WINDOW, S)
    assert nb % nbs == 0 and nbs % kdiv == 0
    kvw = (nbs // kdiv) * LANES

    def body(q_ref, k_ref, vT_ref, dist_ref, slope_ref, sink_ref, o_ref, lse_ref):
        i = pl.program_id(1)
        rlo = _row_lo()
        k0 = _win_start(i, tq, nk, S)
        kk, vv, dd = k_ref[pl.ds(k0, nk), :], vT_ref[:, pl.ds(k0, nk)], dist_ref[0]
        for u in range(nbs):
            kv = slice(LANES * (u // kdiv), LANES * (u // kdiv + 1))
            qsT = _stack_cols(q_ref[:, LANES * u:LANES * (u + 1)].astype(F32).T, rlo).astype(BF16)
            s = _dot(kk[:, kv], qsT) - slope_ref[u] * dd
            sk = sink_ref[u]
            m = jnp.maximum(jnp.max(s, axis=0, keepdims=True), sk)
            p = jnp.exp(s - m)
            l = jnp.sum(p, axis=0, keepdims=True) + jnp.exp(sk - m)
            o_ref[:, LANES * u:LANES * (u + 1)] = _pick_halves_T(_dot(vv[kv, :], p.astype(BF16)) / l, rlo, tq).astype(BF16)
            lse_ref[u, 0] = m + jnp.log(l)

    row_spec = pl.BlockSpec((nbs, 1, 2 * tq), lambda b, i: (b, 0, 0))
    return pl.pallas_call(
        body, name=name, grid=(nb // nbs, nq),
        in_specs=[pl.BlockSpec((tq, nbs * LANES), lambda b, i: (i, b)), pl.BlockSpec((S, kvw), lambda b, i: (0, b)),
                  pl.BlockSpec((kvw, S), lambda b, i: (b, 0)), _win_dist_spec(nk, tq, nq), row_spec, row_spec],
        out_specs=[pl.BlockSpec((tq, nbs * LANES), lambda b, i: (i, b)), pl.BlockSpec((nbs, 1, 1, 2 * tq), lambda b, i: (b, i, 0, 0))],
        out_shape=[_sds((S, nb * LANES), BF16), _sds((nb, nq, 1, 2 * tq), F32)],
        compiler_params=_params(("arbitrary", "arbitrary")))(q, k, vT, dist, slope, sink)


def _win_bwd(q, k, kT, v, o, do, lse, dist, slope, sink, *, kdiv, tq, nbs, name):
    S = k.shape[0]; nb = q.shape[1] // LANES; nkb = k.shape[1] // LANES; nq = S // tq; nk = min(tq + 2 * WINDOW, S)
    assert nb % nbs == 0 and nbs % kdiv == 0
    nkv = nbs // kdiv
    kvw = nkv * LANES

    def body(q_ref, k_ref, kT_ref, v_ref, o_ref, do_ref, lse_ref, dist_ref, slope_ref, sink_ref, dq_ref, dk_ref, dv_ref, dsink_ref, ds_acc):
        i = pl.program_id(1)
        rlo = _row_lo()
        lo = lax.broadcasted_iota(jnp.int32, (1, LANES), 1) < HEAD_DIM

        @pl.when(i == 0)
        def _():
            dk_ref[...] = jnp.zeros((S, kvw), F32)
            dv_ref[...] = jnp.zeros((S, kvw), F32)
            ds_acc[...] = jnp.zeros((nbs, 2 * tq), F32)

        k0 = _win_start(i, tq, nk, S)
        rows = pl.ds(k0, nk)
        kk_all, vv_all, kkT_all, dd = k_ref[rows, :], v_ref[rows, :], kT_ref[:, rows], dist_ref[0]
        dv_sum, dk_sum = [None] * nkv, [None] * nkv
        for u in range(nbs):
            g = u // kdiv
            kv = slice(LANES * g, LANES * (g + 1))
            kk, vv, kkT = kk_all[:, kv], vv_all[:, kv], kkT_all[kv, :]
            cols = slice(LANES * u, LANES * (u + 1))
            qv = q_ref[:, cols]
            qs = _stack_rows(qv, lo)
            qsT = _stack_cols(qv.astype(F32).T, rlo).astype(BF16)
            dov = do_ref[:, cols].astype(F32)
            dos = _stack_rows(dov.astype(BF16), lo)
            dosT = _stack_cols(dov.T, rlo).astype(BF16)
            prodT = (dov * o_ref[:, cols].astype(F32)).T
            delta = jnp.concatenate([jnp.sum(jnp.where(rlo, prodT, 0.0), axis=0, keepdims=True),
                                     jnp.sum(jnp.where(rlo, 0.0, prodT), axis=0, keepdims=True)], axis=1)
            lse_v = lse_ref[u, 0]
            ds_acc[u:u + 1, :] += -jnp.exp(sink_ref[u] - lse_v) * delta
            p = jnp.exp(_dot(kk, qsT) - slope_ref[u] * dd - lse_v)
            ds = (p * (_dot(vv, dosT) - delta)).astype(BF16)
            dv_u, dk_u = _dot(p.astype(BF16), dos), _dot(ds, qs)
            dv_sum[g] = dv_u if dv_sum[g] is None else dv_sum[g] + dv_u
            dk_sum[g] = dk_u if dk_sum[g] is None else dk_sum[g] + dk_u
            dq_ref[:, cols] = (_pick_halves_T(_dot(kkT, ds), rlo, tq) * 0.125).astype(BF16)
        dv_ref[rows, :] += jnp.concatenate(dv_sum, axis=1)
        dk_ref[rows, :] += jnp.concatenate(dk_sum, axis=1)

        @pl.when(i == nq - 1)
        def _():
            acc = ds_acc[...]
            for u in range(nbs):
                dsink_ref[u] = jnp.concatenate(
                    [jnp.broadcast_to(jnp.sum(acc[u:u + 1, 0:tq], axis=1, keepdims=True), (1, LANES)),
                     jnp.broadcast_to(jnp.sum(acc[u:u + 1, tq:2 * tq], axis=1, keepdims=True), (1, LANES)),
                     jnp.zeros((6, LANES), F32)], axis=0)

    qmap = lambda b, i: (i, b)
    kv_spec = pl.BlockSpec((S, kvw), lambda b, i: (0, b))
    row_spec = pl.BlockSpec((nbs, 1, 2 * tq), lambda b, i: (b, 0, 0))
    wide = pl.BlockSpec((tq, nbs * LANES), qmap)
    return pl.pallas_call(
        body, name=name, grid=(nb // nbs, nq),
        in_specs=[wide, kv_spec, pl.BlockSpec((kvw, S), lambda b, i: (b, 0)), kv_spec, wide, wide,
                  pl.BlockSpec((nbs, 1, 1, 2 * tq), lambda b, i: (b, i, 0, 0)), _win_dist_spec(nk, tq, nq), row_spec, row_spec],
        out_specs=[wide, kv_spec, kv_spec, pl.BlockSpec((nbs, 8, LANES), lambda b, i: (b, 0, 0))],
        out_shape=[_sds((S, nb * LANES), BF16), _sds((S, nkb * LANES), F32), _sds((S, nkb * LANES), F32), _sds((nb, 8, LANES), F32)],
        scratch_shapes=[pltpu.VMEM((nbs, 2 * tq), F32)],
        compiler_params=_params(("arbitrary", "arbitrary")))(q, k, kT, v, o, do, lse, dist, slope, sink)


def _sum_rows(v):
    return jnp.sum(v, axis=0, keepdims=True)


def _norm_mod_bwd(dh, xv, mod_ref, nw_ref, stats_ref):
    r = _rms(xv)
    xn = xv * r
    nw = nw_ref[...]
    stats_ref[0:1, :] += _sum_rows(dh)
    stats_ref[1:2, :] += _sum_rows(dh * (xn * nw))
    dn = dh * (1.0 + mod_ref[1:2, :])
    stats_ref[2:3, :] += _sum_rows(dn * xn)
    return _rms_bwd(xv, r, dn * nw)


def _even_gate_specs(ts):
    return [pl.BlockSpec((ts, 256), lambda i, c=c: (i, c)) for c in (3, 4, 7, 8)]


def _even_post_fwd(oa, olat, proj, x, gate, wuv, woe):
    S = x.shape[0]
    ts = min(ROW_TILE, S)

    def body(oa_ref, ol_ref, ga0_ref, ga1_ref, gb0_ref, gb1_ref, x_ref, gate_ref, wuv_ref, woe_ref, y_ref, x1_ref):
        sa, _ = _silu_and_grad(jnp.concatenate([ga0_ref[...], ga1_ref[...]], axis=1))
        sb, _ = _silu_and_grad(jnp.concatenate([gb0_ref[...], gb1_ref[...]], axis=1))
        olc = jnp.concatenate([ol_ref[hh] for hh in range(B_HEADS)], axis=1).astype(BF16)
        ob = _dot(olc, wuv_ref[...])
        mix = jnp.concatenate([oa_ref[...] * sa, ob * sb], axis=1).astype(BF16)
        y = _dot(mix, woe_ref[...])
        y_ref[...] = y.astype(BF16)
        x1_ref[...] = x_ref[...] + gate_ref[...] * y

    return pl.pallas_call(
        body, name="even_post_fwd", grid=(S // ts,),
        in_specs=[_row_spec(ts, 512), pl.BlockSpec((B_HEADS, ts, LANES), lambda i: (0, i, 0))] + _even_gate_specs(ts) +
                 [_row_spec(ts, D_MODEL), _full_spec((1, D_MODEL)), _full_spec((1024, 512)), _full_spec((1024, D_MODEL))],
        out_specs=[_row_spec(ts, D_MODEL), _row_spec(ts, D_MODEL)],
        out_shape=[_sds((S, D_MODEL), BF16), _sds((S, D_MODEL), F32)],
        compiler_params=_params(("arbitrary",)),
    )(oa, olat, proj, proj, proj, proj, x, gate, wuv, woe)


def _odd_pre_fwd(x, mod, nw, wio):
    S = x.shape[0]
    ts = min(ROW_TILE, S)

    def body(x_ref, mod_ref, nw_ref, wio_ref, h_ref, g_ref, q_ref, k_ref, v_ref, kt_ref, vt_ref):
        xv = x_ref[...]
        h = (xv * _rms(xv) * nw_ref[...]) * (1.0 + mod_ref[1:2, :]) + mod_ref[0:1, :]
        hb = h.astype(BF16)
        h_ref[...] = hb
        proj = jnp.concatenate([_dot(hb, wio_ref[p]) for p in range(N_CHIPS)], axis=1)
        g_ref[...] = proj[:, 1536:2560]
        q_ref[...] = (proj[:, 0:1024] * 0.125).astype(BF16)
        lane = _lane_iota()
        k_v = jnp.concatenate([_dup_heads(proj[:, 1024 + LANES * j:1024 + LANES * (j + 1)], lane) for j in range(2)], axis=1)
        v_v = jnp.concatenate([_dup_heads(proj[:, 1280 + LANES * j:1280 + LANES * (j + 1)], lane) for j in range(2)], axis=1)
        k_ref[...] = k_v.astype(BF16)
        v_ref[...] = v_v.astype(BF16)
        kt_ref[...] = k_v.T.astype(BF16)
        vt_ref[...] = v_v.T.astype(BF16)

    col_spec = pl.BlockSpec((512, ts), lambda i: (0, i))
    return pl.pallas_call(
        body, name="odd_pre_fwd", grid=(S // ts,),
        in_specs=[_row_spec(ts, D_MODEL), _full_spec((3, D_MODEL)), _full_spec((1, D_MODEL)),
                  _full_spec((N_CHIPS, D_MODEL, ODD_IN // N_CHIPS))],
        out_specs=[_row_spec(ts, D_MODEL), _row_spec(ts, 1024), _row_spec(ts, 1024), _row_spec(ts, 512), _row_spec(ts, 512),
                   col_spec, col_spec],
        out_shape=[_sds((S, D_MODEL), BF16), _sds((S, 1024), F32), _sds((S, 1024), BF16), _sds((S, 512), BF16),
                   _sds((S, 512), BF16), _sds((512, S), BF16), _sds((512, S), BF16)],
        compiler_params=_params(("arbitrary",)),
    )(x, mod, nw, wio)


def _odd_post(oc, g, x1, gate, woo, fw, tgt):
    S = x1.shape[0]
    ts = min(ROW_TILE, S)
    nsteps = S // ts

    def body(oc_ref, g_ref, x_ref, gate_ref, woo_ref, fw_ref, tgt_ref, doc_ref, dgc_ref, dx2_ref, dwoo_out, stats_ref, dwoo_ref):
        @pl.when(pl.program_id(0) == 0)
        def _():
            dwoo_ref[...] = jnp.zeros((D_MODEL, D_MODEL), F32)
            stats_ref[...] = jnp.zeros((8, D_MODEL), F32)

        ocv = oc_ref[...]
        sg, dsg = _silu_and_grad(g_ref[...])
        mix = (ocv * sg).astype(BF16)
        woo_v = woo_ref[...]
        y = _dot(mix, woo_v)
        gate_v = gate_ref[...]
        x2 = x_ref[...] + gate_v * y
        r = _rms(x2)
        fw_v = fw_ref[...]
        xn = x2 * r
        err = xn * fw_v - tgt_ref[...]
        dout = err * (1.0 / D_MODEL)
        dx2 = _rms_bwd(x2, r, dout * fw_v)
        dx2_ref[...] = dx2
        stats_ref[0:1, :] += _sum_rows(dout * xn)
        stats_ref[1:2, :] += _sum_rows(dx2 * y)
        loss_t = 0.5 * jnp.sum(_sum_rows(err * dout), axis=-1, keepdims=True)
        stats_ref[2:3, :] += jnp.broadcast_to(loss_t, (1, D_MODEL))
        dy = (gate_v * dx2).astype(BF16)
        dmix = _dot_nt(dy, woo_v)
        dwoo_ref[...] += _dot_tn(mix, dy)
        doc_ref[...] = (dmix * sg).astype(BF16)
        dgc_ref[...] = (dmix * ocv * dsg).astype(BF16)

        @pl.when(pl.program_id(0) == nsteps - 1)
        def _():
            dwoo_out[...] = dwoo_ref[...].astype(BF16)

    return pl.pallas_call(
        body, name="odd_post", grid=(nsteps,),
        in_specs=[_row_spec(ts, D_MODEL), _row_spec(ts, D_MODEL), _row_spec(ts, D_MODEL), _full_spec((1, D_MODEL)),
                  _full_spec((D_MODEL, D_MODEL)), _full_spec((1, D_MODEL)), _row_spec(ts, D_MODEL)],
        out_specs=[_row_spec(ts, D_MODEL), _row_spec(ts, D_MODEL), _row_spec(ts, D_MODEL),
                   _full_spec((D_MODEL, D_MODEL), single=False), _full_spec((8, D_MODEL), single=False)],
        out_shape=[_sds((S, D_MODEL), BF16), _sds((S, D_MODEL), BF16), _sds((S, D_MODEL), F32), _sds((D_MODEL, D_MODEL), BF16),
                   _sds((8, D_MODEL), F32)],
        scratch_shapes=[pltpu.VMEM((D_MODEL, D_MODEL), F32)],
        compiler_params=_params(("arbitrary",)),
    )(oc, g, x1, gate, woo, fw, tgt)


def _odd_pre_bwd(dq, dk, dv, dgc, h, x, dx_res, mod, nw, wio):
    S = x.shape[0]
    ts = min(IN_PROJ_ROW_TILE, S)
    nsteps = S // ts
    wsh = ODD_IN // N_CHIPS

    def body(dq_ref, dk_ref, dv_ref, dgc_ref, h_ref, x_ref, dxr_ref, mod_ref, nw_ref, wio_ref, dx_ref, dw_ref, stats_ref, dw_acc):
        @pl.when(pl.program_id(0) == 0)
        def _():
            dw_acc[...] = jnp.zeros((N_CHIPS, D_MODEL, wsh), F32)
            stats_ref[...] = jnp.zeros((8, D_MODEL), F32)

        lane = _lane_iota()
        dkv = [_fold_heads(r[:, 2 * LANES * j:2 * LANES * (j + 1)], lane).astype(BF16) for r in (dk_ref, dv_ref) for j in range(2)]
        dproj = jnp.concatenate([dq_ref[...]] + dkv + [dgc_ref[...]], axis=1)
        hv = h_ref[...]
        dh = None
        for p in range(N_CHIPS):
            dp_cols = dproj[:, wsh * p:wsh * (p + 1)]
            part = _dot_nt(dp_cols, wio_ref[p])
            dh = part if dh is None else dh + part
            dw_acc[p] += _dot_tn(hv, dp_cols)
        dx_ref[...] = dxr_ref[...] + _norm_mod_bwd(dh, x_ref[...], mod_ref, nw_ref, stats_ref)

        @pl.when(pl.program_id(0) == nsteps - 1)
        def _():
            dw_ref[...] = dw_acc[...].astype(BF16)

    return pl.pallas_call(
        body, name="odd_pre_bwd", grid=(nsteps,),
        in_specs=[_row_spec(ts, 1024), _row_spec(ts, 512), _row_spec(ts, 512), _row_spec(ts, 1024), _row_spec(ts, D_MODEL),
                  _row_spec(ts, D_MODEL), _row_spec(ts, D_MODEL), _full_spec((3, D_MODEL)), _full_spec((1, D_MODEL)),
                  _full_spec((N_CHIPS, D_MODEL, wsh))],
        out_specs=[_row_spec(ts, D_MODEL), _full_spec((N_CHIPS, D_MODEL, wsh), single=False), _full_spec((8, D_MODEL), single=False)],
        out_shape=[_sds((S, D_MODEL), F32), _sds((N_CHIPS, D_MODEL, wsh), BF16), _sds((8, D_MODEL), F32)],
        scratch_shapes=[pltpu.VMEM((N_CHIPS, D_MODEL, wsh), F32)],
        compiler_params=_params(("arbitrary",)),
    )(dq, dk, dv, dgc, h, x, dx_res, mod, nw, wio)


def _even_post_bwd(dx1, y, oa, olat, proj, gate, wuv, woe):
    S = dx1.shape[0]
    ts = min(ROW_TILE, S)
    nsteps = S // ts

    def body(dx_ref, y_ref, oa_ref, ol_ref, ga0_ref, ga1_ref, gb0_ref, gb1_ref, gate_ref, wuv_ref, woe_ref,
             doa_ref, dga_ref, dgb_ref, dol_ref, dwoe_out, dwuv_ref, stats_ref, dwoe_ref):
        @pl.when(pl.program_id(0) == 0)
        def _():
            dwoe_ref[...] = jnp.zeros((D_MODEL, D_MODEL), F32)
            dwuv_ref[...] = jnp.zeros((1024, 512), F32)
            stats_ref[...] = jnp.zeros((8, D_MODEL), F32)

        dxv = dx_ref[...]
        stats_ref[0:1, :] += _sum_rows(dxv * y_ref[...])
        dy = (gate_ref[...] * dxv).astype(BF16)
        sa, dsa = _silu_and_grad(jnp.concatenate([ga0_ref[...], ga1_ref[...]], axis=1))
        sb, dsb = _silu_and_grad(jnp.concatenate([gb0_ref[...], gb1_ref[...]], axis=1))
        olc = jnp.concatenate([ol_ref[hh] for hh in range(B_HEADS)], axis=1).astype(BF16)
        wuv_v = wuv_ref[...]
        ob = _dot(olc, wuv_v)
        oav = oa_ref[...]
        mix = jnp.concatenate([oav * sa, ob * sb], axis=1).astype(BF16)
        dmix = _dot_nt(dy, woe_ref[...])
        dwoe_ref[...] += _dot_tn(mix, dy)
        dma, dmb = dmix[:, 0:512], dmix[:, 512:1024]
        doa_ref[...] = (dma * sa).astype(BF16)
        dga_ref[...] = (dma * oav * dsa).astype(BF16)
        dgb_ref[...] = (dmb * ob * dsb).astype(BF16)
        dob = (dmb * sb).astype(BF16)
        dol = _dot_nt(dob, wuv_v)
        dwuv_ref[...] += _dot_tn(olc, dob)
        for hh in range(B_HEADS):
            dol_ref[hh] = dol[:, LANES * hh:LANES * (hh + 1)].astype(BF16)

        @pl.when(pl.program_id(0) == nsteps - 1)
        def _():
            dwoe_out[...] = dwoe_ref[...].astype(BF16)

    head_spec = pl.BlockSpec((B_HEADS, ts, LANES), lambda i: (0, i, 0))
    return pl.pallas_call(
        body, name="even_post_bwd", grid=(nsteps,),
        in_specs=[_row_spec(ts, D_MODEL), _row_spec(ts, D_MODEL), _row_spec(ts, 512), head_spec] + _even_gate_specs(ts) +
                 [_full_spec((1, D_MODEL)), _full_spec((1024, 512)), _full_spec((1024, D_MODEL))],
        out_specs=[_row_spec(ts, 512), _row_spec(ts, 512), _row_spec(ts, 512), head_spec,
                   _full_spec((D_MODEL, D_MODEL), single=False), _full_spec((1024, 512), single=False),
                   _full_spec((8, D_MODEL), single=False)],
        out_shape=[_sds((S, 512), BF16), _sds((S, 512), BF16), _sds((S, 512), BF16), _sds((B_HEADS, S, LANES), BF16),
                   _sds((D_MODEL, D_MODEL), BF16), _sds((1024, 512), F32), _sds((8, D_MODEL), F32)],
        scratch_shapes=[pltpu.VMEM((D_MODEL, D_MODEL), F32)],
        compiler_params=_params(("arbitrary",)),
    )(dx1, y, oa, olat, proj, proj, proj, proj, gate, wuv, woe)


def _even_pre_bwd(x, h, proj, dqa, dka, dva, dga, dgb, dqcat, dkcat, dx_res, mod, nw, wie, qn, kn, seg, ca, sa, ct, st,
                  qln, kvln, wuq, wuk):
    S = x.shape[0]
    ts = min(IN_PROJ_ROW_TILE, S)
    nsteps = S // ts

    def body(x_ref, h_ref, proj_ref, dqa_ref, dka_ref, dva_ref, dga_ref, dgb_ref, dqc_ref, dkc_ref, dxr_ref, mod_ref, nw_ref,
             wie_ref, qn_ref, kn_ref, seg_ref, ca_ref, sa_ref, ct_ref, st_ref, qln_ref, kvln_ref, wuq_ref, wuk_ref,
             dx_ref, dwie_out, dwuq_out, dwuk_out, stats_ref, nstats_ref, dwie_ref, dwuq_ref, dwuk_ref, stage, uq_stage, uk_stage):
        @pl.when(pl.program_id(0) == 0)
        def _():
            dwie_ref[...] = jnp.zeros((EVEN_P, D_MODEL), F32)
            dwuq_ref[...] = jnp.zeros((1536, B_Q_LORA), F32)
            dwuk_ref[...] = jnp.zeros((512, 1024), F32)
            stats_ref[...] = jnp.zeros((8, D_MODEL), F32)
            nstats_ref[...] = jnp.zeros((8, 256), F32)

        lane = _lane_iota()
        ca_v, sa_v, ct_v, st_v = ca_ref[...], sa_ref[...], ct_ref[...], st_ref[...]
        seg_v = seg_ref[...]

        def head_norm_bwd(xc, dy, w):
            r = lax.rsqrt(_seg_mean(xc * xc, seg_v) + EPS)
            g = dy * w
            dxc = r * g - xc * (r * r * r) * _seg_mean(xc * g, seg_v)
            return dxc, _sum_rows(dy * (xc * r))

        pieces = []
        dqn = jnp.zeros((1, LANES), F32)
        for cb in range(4):
            sl = slice(LANES * cb, LANES * (cb + 1))
            dy = _rot_bwd(dqa_ref[:, sl] * 0.125, ca_v, sa_v, lane)
            dxc, dw = head_norm_bwd(proj_ref[:, sl], dy, qn_ref[...])
            pieces.append(dxc)
            dqn = dqn + dw
        dxc, dkn = head_norm_bwd(proj_ref[:, 512:640], _rot_bwd(_fold_heads(dka_ref[...], lane), ca_v, sa_v, lane), kn_ref[...])
        pieces += [dxc, _fold_heads(dva_ref[...], lane), dga_ref[...]]
        nstats_ref[0:1, 0:LANES] += dqn + pltpu.roll(dqn, HEAD_DIM, 1)
        nstats_ref[1:2, 0:LANES] += dkn + pltpu.roll(dkn, HEAD_DIM, 1)

        cq = proj_ref[:, 1280:1536]
        rq = _rms(cq)
        cqn_f = cq * rq
        qln_v = qln_ref[...]
        cqn = (cqn_f * qln_v).astype(BF16)
        wuq_v, wuk_v = wuq_ref[...], wuk_ref[...]
        qnope = _dot_nt(cqn, wuq_v[0:512, :]).astype(BF16)
        dqlat = jnp.concatenate([dqc_ref[hh, :, 0:LANES] for hh in range(B_HEADS)], axis=1).astype(BF16)
        dqnope = _dot_nt(dqlat, wuk_v)
        dwuk_ref[...] += _dot_tn(qnope, dqlat)
        dqr = [_rot_bwd(dqc_ref[hh, :, LANES:2 * LANES], ct_v, st_v, lane) for hh in range(B_HEADS)]
        dqb = jnp.concatenate([dqnope] + dqr, axis=1).astype(BF16)
        dcqn = _dot(dqb, wuq_v)
        dwuq_ref[...] += _dot_tn(dqb, cqn)
        nstats_ref[2:3, :] += _sum_rows(dcqn * cqn_f)
        dcq = _rms_bwd(cq, rq, dcqn * qln_v)
        ckv = proj_ref[:, 1536:1664]
        rk = _rms(ckv)
        dckvn = dkc_ref[:, 0:LANES]
        nstats_ref[3:4, 0:LANES] += _sum_rows(dckvn * (ckv * rk))
        dckv = _rms_bwd(ckv, rk, dckvn * kvln_ref[...])
        dkr = _rot_bwd(dkc_ref[:, LANES:2 * LANES], ct_v, st_v, lane)
        pieces += [dcq, dckv, dkr, dgb_ref[...]]
        dproj = jnp.concatenate([piece.astype(BF16) for piece in pieces], axis=1)
        dh = _dot(dproj, wie_ref[...])
        dwie_ref[...] += _dot_tn(dproj, h_ref[...])
        dx_ref[...] = dxr_ref[...] + _norm_mod_bwd(dh, x_ref[...], mod_ref, nw_ref, stats_ref)

        @pl.when(pl.program_id(0) == nsteps - 1)
        def _():
            r_out = 0
            for lo, hi in ((0, EVEN_GAP), (EVEN_GAP + EVEN_P - EVEN_IN, EVEN_P)):
                for r0 in range(lo, hi, stage.shape[0]):
                    n = min(stage.shape[0], hi - r0)
                    stage[0:n, :] = dwie_ref[r0:r0 + n, :].astype(BF16)
                    pltpu.sync_copy(stage.at[0:n], dwie_out.at[pl.ds(r_out, n), :])
                    r_out += n
            per = B_NOPE + B_ROPE
            for hd in range(B_HEADS):
                uq_stage[per * hd:per * hd + B_NOPE, :] = dwuq_ref[B_NOPE * hd:B_NOPE * (hd + 1), :].astype(BF16)
                uq_stage[per * hd + B_NOPE:per * (hd + 1), :] = dwuq_ref[512 + LANES * hd:512 + LANES * hd + B_ROPE, :].astype(BF16)
                uk_stage[B_NOPE * hd:B_NOPE * (hd + 1), :] = dwuk_ref[B_NOPE * hd:B_NOPE * (hd + 1),
                                                                       LANES * hd:LANES * (hd + 1)].astype(BF16)
            pltpu.sync_copy(uq_stage, dwuq_out)
            pltpu.sync_copy(uk_stage, dwuk_out)

    return pl.pallas_call(
        body, name="even_pre_bwd", grid=(nsteps,),
        in_specs=[_row_spec(ts, D_MODEL), _row_spec(ts, D_MODEL), _row_spec(ts, EVEN_P), _row_spec(ts, 512), _row_spec(ts, 2 * LANES),
                  _row_spec(ts, 2 * LANES), _row_spec(ts, 512), _row_spec(ts, 512),
                  pl.BlockSpec((B_HEADS, ts, 2 * LANES), lambda i: (0, i, 0)), _row_spec(ts, 2 * LANES), _row_spec(ts, D_MODEL),
                  _full_spec((3, D_MODEL)), _full_spec((1, D_MODEL)), _full_spec((EVEN_P, D_MODEL)),
                  _full_spec((1, LANES)), _full_spec((1, LANES)), _full_spec((LANES, LANES)),
                  _row_spec(ts, LANES), _row_spec(ts, LANES), _row_spec(ts, LANES), _row_spec(ts, LANES),
                  _full_spec((1, B_Q_LORA)), _full_spec((1, B_KV_LORA)), _full_spec((1536, B_Q_LORA)), _full_spec((512, 1024))],
        out_specs=[_row_spec(ts, D_MODEL), _ANY, _ANY, _ANY, _full_spec((8, D_MODEL), single=False), _full_spec((8, 256), single=False)],
        out_shape=[_sds((S, D_MODEL), F32), _sds((EVEN_IN, D_MODEL), BF16), _sds((B_HEADS * (B_NOPE + B_ROPE), B_Q_LORA), BF16),
                   _sds((B_HEADS * B_NOPE, LANES), BF16), _sds((8, D_MODEL), F32), _sds((8, 256), F32)],
        scratch_shapes=[pltpu.VMEM((EVEN_P, D_MODEL), F32), pltpu.VMEM((1536, B_Q_LORA), F32), pltpu.VMEM((512, 1024), F32),
                        pltpu.VMEM((256, D_MODEL), BF16), pltpu.VMEM((B_HEADS * (B_NOPE + B_ROPE), B_Q_LORA), BF16),
                        pltpu.VMEM((B_HEADS * B_NOPE, LANES), BF16)],
        compiler_params=_params(("arbitrary",)),
    )(x, h, proj, dqa, dka, dva, dga, dgb, dqcat, dkcat, dx_res, mod, nw, wie, qn, kn, seg, ca, sa, ct, st, qln, kvln, wuq, wuk)


def _ada_fwd(c_all, w, b):
    n = w.shape[2]

    def body(c_ref, w_ref, b_ref, o_ref):
        cv = c_ref[...]
        o_ref[0] = _dot_f32(cv * _sigmoid(cv), w_ref[0]) + b_ref[0]

    return pl.pallas_call(
        body, name="ada_fwd", grid=(2,),
        in_specs=[pl.BlockSpec((N_DEV, D_MODEL), lambda l: (0, 0)), pl.BlockSpec((1, D_MODEL, n), lambda l: (l, 0, 0)),
                  pl.BlockSpec((1, 1, n), lambda l: (l, 0, 0))],
        out_specs=pl.BlockSpec((1, N_DEV, n), lambda l: (l, 0, 0)),
        out_shape=_sds((2, N_DEV, n), F32),
        compiler_params=_params(("arbitrary",)),
    )(c_all, w, b)


def _ada_bwd(c_all_t, dmod):
    n = dmod.shape[2]

    def body(c_ref, d_ref, o_ref):
        cv = c_ref[...]
        act = cv * _sigmoid(cv)
        dv = d_ref[0]
        acc = act[:, 0:1] * dv[0:1, :]
        for bb in range(1, N_DEV):
            acc = acc + act[:, bb:bb + 1] * dv[bb:bb + 1, :]
        o_ref[0] = acc

    return pl.pallas_call(
        body, name="ada_bwd", grid=(2,),
        in_specs=[pl.BlockSpec((D_MODEL, N_DEV), lambda l: (0, 0)), pl.BlockSpec((1, N_DEV, n), lambda l: (l, 0, 0))],
        out_specs=pl.BlockSpec((1, D_MODEL, n), lambda l: (l, 0, 0)),
        out_shape=_sds((2, D_MODEL, n), F32),
        compiler_params=_params(("arbitrary",)),
    )(c_all_t, dmod)


ADAM_ROW_TILE = 512


def _adam_update(g, w, m, v):
    m_new = ADAM_B1 * m + (1.0 - ADAM_B1) * g
    v_new = ADAM_B2 * v + (1.0 - ADAM_B2) * jnp.square(g)
    m_hat = m_new / (1.0 - ADAM_B1 ** ADAM_STEP)
    v_hat = v_new / (1.0 - ADAM_B2 ** ADAM_STEP)
    return -ADAM_LR * (m_hat / (jnp.sqrt(v_hat) + ADAM_EPS) + ADAM_WD * w), m_new, v_new


SMALL_ROWS = dict(dmod=(0, D_MODEL), norm_w=(6, D_MODEL), final_norm=(8, D_MODEL), a_q_norm=(9, HEAD_DIM), a_k_norm=(10, HEAD_DIM),
                  b_q_lora_norm=(11, B_Q_LORA), b_kv_lora_norm=(12, B_KV_LORA), c_sink=(13, C_HEADS))
SMALL_WEIGHTS = ("ada_b", "norm_w", "final_norm", "a_q_norm", "a_k_norm", "b_q_lora_norm", "b_kv_lora_norm", "c_sink")
LOSS_ROW = 14


def _pack_small(res):
    def padded(v):
        return jnp.concatenate([v, jnp.zeros((v.shape[0], D_MODEL - v.shape[1]), F32)], axis=1)

    rows = [res["dmod"].reshape(6, D_MODEL), res["norm_w"], res["final_norm"].reshape(1, D_MODEL)]
    rows += [padded(res[k]) for k in ("a_q_norm", "a_k_norm", "b_q_lora_norm", "b_kv_lora_norm", "c_sink")]
    return jnp.concatenate(rows + [res["loss_row"], jnp.zeros((1, D_MODEL), F32)], axis=0)


def _adam_small(parts, ws, ms, vs):
    nw = len(SMALL_WEIGHTS)

    def body(*refs):
        p_ref = refs[0]
        w_refs, m_refs, v_refs = refs[1:1 + nw], refs[1 + nw:1 + 2 * nw], refs[1 + 2 * nw:1 + 3 * nw]
        outs = refs[1 + 3 * nw:]
        g_all = p_ref[0]
        for k in range(1, N_DEV):
            g_all = g_all + p_ref[k]
        for idx, name in enumerate(SMALL_WEIGHTS):
            if name == "ada_b":
                g = jnp.concatenate([jnp.concatenate([g_all[3 * l + t:3 * l + t + 1] for t in range(3)], axis=1) for l in range(2)],
                                    axis=0)
            else:
                row, width = SMALL_ROWS[name]
                g = g_all[row:row + w_refs[idx].shape[0], 0:width]
            d, m_new, v_new = _adam_update(g, w_refs[idx][...], m_refs[idx][...], v_refs[idx][...])
            outs[4 * idx][...], outs[4 * idx + 1][...], outs[4 * idx + 2][...], outs[4 * idx + 3][...] = g, d, m_new, v_new
        outs[4 * nw][...] = g_all[LOSS_ROW:LOSS_ROW + 1, 0:LANES]

    out_shape = []
    for w in ws:
        out_shape += [_sds(w.shape, F32)] * 4
    out_shape.append(_sds((1, LANES), F32))
    return pl.pallas_call(body, name="adam_small", out_shape=out_shape,
                          compiler_params=pltpu.CompilerParams(vmem_limit_bytes=VMEM_LIMIT))(parts, *ws, *ms, *vs)


def _adam(parts, w, m, v, name, by_columns=False):
    P, R, C = parts.shape
    if by_columns:
        tr, tc = R, 256
    else:
        tr, tc = (R if R <= ADAM_ROW_TILE else ADAM_ROW_TILE), C
    assert R % tr == 0 and C % tc == 0

    def body(p_ref, w_ref, m_ref, v_ref, g_ref, d_ref, nm_ref, nv_ref):
        g = p_ref[0].astype(F32)
        for k in range(1, P):
            g = g + p_ref[k].astype(F32)
        g_ref[...] = g
        d_ref[...], nm_ref[...], nv_ref[...] = _adam_update(g, w_ref[...], m_ref[...], v_ref[...])

    tile = (lambda i: (0, i)) if by_columns else (lambda i: (i, 0))
    spec = pl.BlockSpec((tr, tc), tile)
    return pl.pallas_call(
        body, name=name, grid=(C // tc if by_columns else R // tr,),
        in_specs=[pl.BlockSpec((P, tr, tc), lambda i: (0,) + tile(i)), spec, spec, spec],
        out_specs=[spec, spec, spec, spec], out_shape=[_sds((R, C), F32)] * 4,
        compiler_params=_params(("arbitrary",)),
    )(parts, w, m, v)


_ANY = pl.BlockSpec(memory_space=pl.ANY)
CHIP_FLIPS = ((1, 0), (0, 1), (1, 1))
DEV_FLIPS = tuple((dx, dy, dc) for dx in (0, 1) for dy in (0, 1) for dc in (0, 1) if dx + dy + dc)


def _flip(a, d):
    return a if d == 0 else 1 - a


def _my_place():
    return lax.axis_index("x"), lax.axis_index("y"), lax.axis_index("c")


def _gather8_copies(ins, outs, send_sems, recv_sems, loc_sems):
    x, y, c = _my_place()
    me = 4 * x + 2 * y + c
    copies = []
    for a in range(len(ins)):
        copies.append(pltpu.make_async_copy(ins[a], outs[a].at[me], loc_sems.at[a]))
        for k, (dx, dy, dc) in enumerate(DEV_FLIPS):
            copies.append(pltpu.make_async_remote_copy(
                src_ref=ins[a], dst_ref=outs[a].at[me], send_sem=send_sems.at[a, k], recv_sem=recv_sems.at[a, k],
                device_id=(_flip(x, dx), _flip(y, dy), _flip(c, dc)), device_id_type=MESH_ID))
    return copies


def _gather8_sems(n):
    return [pltpu.SemaphoreType.DMA((n, 7)), pltpu.SemaphoreType.DMA((n, 7)), pltpu.SemaphoreType.DMA((n,))]


def _gather_dev8(arrs, name):
    n = len(arrs)

    def body(*refs):
        copies = _gather8_copies(refs[:n], refs[n:2 * n], *refs[2 * n:])
        for cp in copies:
            cp.start()
        for cp in copies:
            cp.wait()

    return pl.pallas_call(
        body, name=name, in_specs=[_ANY] * n, out_specs=[_ANY] * n,
        out_shape=[_sds((N_DEV,) + a.shape, a.dtype) for a in arrs], scratch_shapes=_gather8_sems(n),
    )(*arrs)


class _Exchange:
    def __init__(self, arrs, out_shapes, n_sems, phases):
        self.arrs, self.out_shapes, self.n_sems, self._phases = list(arrs), list(out_shapes), n_sems, phases

    @property
    def n(self):
        return len(self.arrs)

    def sem_shapes(self):
        return [pltpu.SemaphoreType.DMA((self.n, self.n_sems)), pltpu.SemaphoreType.DMA((self.n, self.n_sems)),
                pltpu.SemaphoreType.DMA((self.n,))]

    def phases(self, ins, outs, sems):
        return self._phases(ins, outs, *sems)

    def run(self, name):
        n = self.n

        def body(*refs):
            start, mid, end = self.phases(refs[:n], refs[n:2 * n], refs[2 * n:])
            start()
            mid()
            end()

        return pl.pallas_call(body, name=name, in_specs=[_ANY] * n, out_specs=[_ANY] * n, out_shape=self.out_shapes,
                              scratch_shapes=self.sem_shapes())(*self.arrs)

def _gather_halves_phases(ins, outs, send_sems, recv_sems, loc_sems):
    n = len(ins)
    x, y, c = _my_place()
    chip = 2 * x + y
    sibling = (x, y, 1 - c)
    peers = [(_flip(x, dx), _flip(y, dy)) for dx, dy in CHIP_FLIPS]

    def remote(src, p, half, a, k, to):
        return pltpu.make_async_remote_copy(src_ref=src, dst_ref=outs[a].at[p, half], send_sem=send_sems.at[a, k],
                                            recv_sem=recv_sems.at[a, k], device_id=to, device_id_type=MESH_ID)

    def local(a):
        return pltpu.make_async_copy(ins[a], outs[a].at[chip], loc_sems.at[a])

    def first(a, k):
        return remote(ins[a].at[c], chip, c, a, k, (*peers[k], c))

    def passed(a, k):
        p = 2 * peers[k][0] + peers[k][1]
        return remote(outs[a].at[p, c], p, c, a, 3 + k, sibling)

    def start():
        for a in range(n):
            local(a).start()
            for k in range(3):
                first(a, k).start()

    def mid():
        for a in range(n):
            for k in range(3):
                p = 2 * peers[k][0] + peers[k][1]
                remote(outs[a].at[p, c], p, c, a, k, sibling).wait_recv()
                passed(a, k).start()

    def end():
        for a in range(n):
            for k in range(3):
                p = 2 * peers[k][0] + peers[k][1]
                remote(outs[a].at[p, 1 - c], p, 1 - c, a, 3 + k, sibling).wait_recv()
        for a in range(n):
            for k in range(3):
                first(a, k).wait_send()
                passed(a, k).wait_send()
            local(a).wait()

    return start, mid, end


def _gather_chip4_halves(arrs):
    return _Exchange(arrs, [_sds((N_CHIPS,) + a.shape, a.dtype) for a in arrs], 6, _gather_halves_phases)


def _reduce_phases(n_whole, ins, outs, send_sems, recv_sems, loc_sems):
    n = len(ins)
    x, y, c = _my_place()
    chip = 2 * x + y
    sibling = (x, y, 1 - c)
    peers = [(_flip(x, dx), _flip(y, dy)) for dx, dy in CHIP_FLIPS]

    def remote(src, slot, a, k, to):
        return pltpu.make_async_remote_copy(src_ref=src, dst_ref=outs[a].at[slot], send_sem=send_sems.at[a, k],
                                            recv_sem=recv_sems.at[a, k], device_id=to, device_id_type=MESH_ID)

    def block(a, p):
        return ins[a] if a >= n - n_whole else ins[a].at[p]

    def local(a):
        return pltpu.make_async_copy(block(a, chip), outs[a].at[2 * chip + c], loc_sems.at[a])

    def own(a):
        return remote(block(a, chip), 2 * chip + c, a, 0, sibling)

    def first(a, k):
        return remote(block(a, 2 * peers[k][0] + peers[k][1]), 2 * chip + c, a, 1 + k, (*peers[k], c))

    def passed(a, k):
        slot = 2 * (2 * peers[k][0] + peers[k][1]) + c
        return remote(outs[a].at[slot], slot, a, 4 + k, sibling)

    def start():
        for a in range(n):
            local(a).start()
            own(a).start()
            for k in range(3):
                first(a, k).start()

    def mid():
        for a in range(n):
            for k in range(3):
                slot = 2 * (2 * peers[k][0] + peers[k][1]) + c
                remote(outs[a].at[slot], slot, a, 1 + k, sibling).wait_recv()
                passed(a, k).start()

    def end():
        for a in range(n):
            remote(outs[a].at[2 * chip + 1 - c], 2 * chip + 1 - c, a, 0, sibling).wait_recv()
            for k in range(3):
                slot = 2 * (2 * peers[k][0] + peers[k][1]) + 1 - c
                remote(outs[a].at[slot], slot, a, 4 + k, sibling).wait_recv()
        for a in range(n):
            own(a).wait_send()
            for k in range(3):
                first(a, k).wait_send()
                passed(a, k).wait_send()
            local(a).wait()

    return start, mid, end


def _reduce_exchange(arrs, whole=()):
    shapes = [_sds((N_DEV,) + a.shape[1:], a.dtype) for a in arrs] + [_sds((N_DEV,) + a.shape, a.dtype) for a in whole]
    return _Exchange(list(arrs) + list(whole), shapes, 7, functools.partial(_reduce_phases, len(whole)))


def _shard_halves_t(w):
    wt = w.T.astype(BF16)
    n2 = wt.shape[0] // 2
    pad = jnp.zeros((-n2 % 16, wt.shape[1]), BF16)
    return jnp.stack([jnp.concatenate([wt[0:n2], pad], axis=0), jnp.concatenate([wt[n2:], pad], axis=0)])


def _gathered_rows(g, n):
    return [g[p, half, 0:n // 2] for p in range(N_CHIPS) for half in range(2)]


def _even_in_layout_t(g):
    n2 = EVEN_IN // N_CHIPS // 2
    gap, gap_rows = EVEN_GAP, EVEN_P - EVEN_IN
    assert n2 % 2 == 0 and gap % 2 == 0 and gap_rows % 2 == 0
    spans = []
    for p in range(N_CHIPS):
        for half in range(2):
            lo = (2 * p + half) * n2
            if lo < gap < lo + n2:
                spans += [(p, half, 0, (gap - lo) // 2, lo // 2), (p, half, (gap - lo) // 2, (lo + n2 - gap) // 2, (gap + gap_rows) // 2)]
            else:
                spans.append((p, half, 0, n2 // 2, (lo + (gap_rows if lo >= gap else 0)) // 2))

    def body(g_ref, o_ref, pairs):
        pairs[pl.ds(gap // 2, gap_rows // 2), :] = jnp.zeros((gap_rows // 2, pairs.shape[1]), jnp.uint32)
        for p, half, src, rows, dst in spans:
            pairs[pl.ds(dst, rows), :] = pltpu.bitcast(g_ref[p, half], jnp.uint32)[src:src + rows]
        step = 128
        for r0 in range(0, EVEN_P // 2, step):
            o_ref[pl.ds(2 * r0, 2 * step), :] = pltpu.bitcast(pairs[pl.ds(r0, step), :], o_ref.dtype)

    return pl.pallas_call(body, name="even_w_in_rows", out_shape=_sds((EVEN_P, g.shape[-1]), g.dtype),
                          scratch_shapes=[pltpu.VMEM((EVEN_P // 2, g.shape[-1]), jnp.uint32)],
                          compiler_params=pltpu.CompilerParams(vmem_limit_bytes=VMEM_LIMIT))(g)


def _uq_layout_t(g):
    wt = jnp.concatenate(_gathered_rows(g, B_HEADS * (B_NOPE + B_ROPE) // N_CHIPS), axis=0)
    per = B_NOPE + B_ROPE
    pad = jnp.zeros((LANES - B_ROPE, wt.shape[1]), wt.dtype)
    nope = [wt[per * h:per * h + B_NOPE] for h in range(B_HEADS)]
    rope = [jnp.concatenate([wt[per * h + B_NOPE:per * (h + 1)], pad], axis=0) for h in range(B_HEADS)]
    return jnp.concatenate(nope + rope, axis=0)


def _block_diag(blocks):
    rows = []
    for h, blk in enumerate(blocks):
        r, cdim = blk.shape
        n = len(blocks)
        rows.append(jnp.concatenate([jnp.zeros((r, cdim * h), blk.dtype), blk, jnp.zeros((r, cdim * (n - 1 - h)), blk.dtype)],
                                    axis=1))
    return jnp.concatenate(rows, axis=0)


def _uk_layout(w):
    return _block_diag([w[:, h, :].T for h in range(B_HEADS)])


def _latent_rows(w):
    return jnp.transpose(w[0], (1, 2, 0)).reshape(-1, w.shape[1])


def _latent_unrows(w2, shape):
    return jnp.transpose(w2.reshape(shape[2], shape[3], shape[1]), (2, 0, 1)).reshape(shape)


def _uv_layout(w):
    return _block_diag([w[:, h, :] for h in range(B_HEADS)])


def _uv_unlayout(g):
    return jnp.concatenate([g[LANES * h:LANES * (h + 1), B_V * h:B_V * (h + 1)].T for h in range(B_HEADS)], axis=0)


def _rope_tables(S):
    inv = ROPE_THETA ** (-jnp.arange(0, 32, 2, dtype=F32) / 32)
    tok = jnp.arange(S)
    nf = inv.shape[0]
    angles = [inv[:, None] * pos.astype(F32)[None, :] for pos in (tok // GRID_W, tok % GRID_W, tok)]
    compact = jnp.concatenate([jnp.cos(a) for a in angles] + [jnp.sin(a) for a in angles], axis=0)

    lane = np.arange(LANES)
    axial = np.where(lane % 64 < 32, 0, nf) + lane % nf
    sign = np.where(lane % (2 * nf) < nf, -1.0, 1.0)
    select = np.zeros((6 * nf, 4 * LANES), np.float32)
    select[axial, lane] = 1.0
    select[3 * nf + axial, LANES + lane] = sign
    select[2 * nf + lane % nf, 2 * LANES + lane] = 1.0
    select[5 * nf + lane % nf, 3 * LANES + lane] = sign
    ts = min(ROW_TILE, S)

    def body(c_ref, sel_ref, *out_refs):
        x = c_ref[...]
        hi = x.astype(BF16)
        rest = x - hi.astype(F32)
        mid = rest.astype(BF16)
        lo = (rest - mid.astype(F32)).astype(BF16)
        full = _dot_tn(hi, sel_ref[...]) + _dot_tn(mid, sel_ref[...]) + _dot_tn(lo, sel_ref[...])
        for t, o_ref in enumerate(out_refs):
            o_ref[...] = full[:, LANES * t:LANES * (t + 1)]

    return pl.pallas_call(body, name="rope_tables", grid=(S // ts,),
                          in_specs=[pl.BlockSpec((6 * nf, ts), lambda i: (0, i)), _full_spec((6 * nf, 4 * LANES))],
                          out_specs=[_row_spec(ts, LANES)] * 4, out_shape=[_sds((S, LANES), F32)] * 4,
                          compiler_params=_params(("arbitrary",)))(compact, jnp.asarray(select, BF16))


A_TQ, A_TK, A_SUB = 512, 4096, 512
A_FWD_SUB = 1024
B_TQ, B_TK, B_SUB = 128, 4096, 1024
B_BWD_TK, B_BWD_SUB = 4096, 512
C_T = 256
C_BLOCKS_PER_STEP = 8
KV_SHARE = 2


def _local_step(x0, tgt, mod, norm_w, wie, wuq, wuk, wuv, late_shards, a_q_norm, a_k_norm, q_lora_norm, kv_lora_norm,
                c_sink, final_norm):
    S = x0.shape[0]
    mod3 = mod.reshape(2, 3, D_MODEL)
    ca, sa, ct, st = _rope_tables(S)
    lane_seg = np.arange(LANES) // HEAD_DIM
    seg = jnp.asarray((lane_seg[:, None] == lane_seg[None, :]).astype(np.float32)).astype(BF16)
    qn = jnp.tile(a_q_norm.reshape(1, HEAD_DIM), (1, 2))
    kn = jnp.tile(a_k_norm.reshape(1, HEAD_DIM), (1, 2))
    qln, kvln = q_lora_norm.reshape(1, B_Q_LORA), kv_lora_norm.reshape(1, B_KV_LORA)
    nw0, nw1 = norm_w[0:1], norm_w[1:2]
    gate0, gate1 = mod3[0, 2:3], mod3[1, 2:3]
    a_tq, a_tk, b_tq, b_tk, bb_tk, c_t = min(A_TQ, S), min(A_TK, S), min(B_TQ, S), min(B_TK, S), min(B_BWD_TK, S), min(C_T, S)
    a_sub, b_sub, bb_sub = min(A_SUB, a_tk), min(B_SUB, b_tk), min(B_BWD_SUB, bb_tk)

    h0, proj_e, qa, ka, va, qcat, kcat, ka_t, va_t, kcat_t = _even_pre_fwd(x0, mod3[0], nw0, wie, qn, kn, seg, ca, sa, ct, st,
                                                                           qln, kvln, wuq, wuk)
    oa, lse_a, woe_g, wio_g, woo_g = _pp_fwd(qa, ka, va_t, kdiv=KV_SHARE, tq=a_tq, tk=a_tk, sub=min(A_FWD_SUB, a_tk), name="attn_a_fwd",
                                             side=_gather_chip4_halves(late_shards))
    woe = woe_g.reshape(D_MODEL, D_MODEL)
    wio = wio_g.reshape(N_CHIPS, D_MODEL, ODD_IN // N_CHIPS)
    woo = woo_g.reshape(D_MODEL, D_MODEL)
    olat, lse_b = _mla_fwd(qcat, kcat, kcat_t, tq=b_tq, tk=b_tk, sub=b_sub)
    y0, x1 = _even_post_fwd(oa, olat, proj_e, x0, gate0, wuv, woe)
    h1, gc, qc, kc, vc, kc_t, vc_t = _odd_pre_fwd(x1, mod3[1], nw1, wio)
    slopes = 2.0 ** (-8.0 * jnp.arange(1, C_HEADS + 1, dtype=F32) / C_HEADS)
    slope_rows = jnp.repeat(slopes.reshape(C_HEADS // 2, 2), c_t, axis=1)[:, None, :]
    sink_rows = jnp.repeat(c_sink.reshape(C_HEADS // 2, 2), c_t, axis=1)[:, None, :]
    win_dist = _win_dist_table(S, c_t)
    oc, lse_c = _win_fwd(qc, kc, vc_t, win_dist, slope_rows, sink_rows, kdiv=KV_SHARE, tq=c_t, nbs=C_BLOCKS_PER_STEP,
                         name="attn_c_fwd")
    doc, dgc, dx2, dwoo, st_f = _odd_post(oc, gc, x1, gate1, woo, final_norm.reshape(1, D_MODEL), tgt)
    dqc, dkc, dvc, dsink_raw = _win_bwd(qc, kc, kc_t, vc, oc, doc, lse_c, win_dist, slope_rows, sink_rows, kdiv=KV_SHARE, tq=c_t,
                                        nbs=C_BLOCKS_PER_STEP, name="attn_c_bwd")
    dx1, dwio, st_1 = _odd_pre_bwd(dqc, dkc, dvc, dgc, h1, x1, dx2, mod3[1], nw1, wio)
    doa, dga, dgb, dolat, dwoe, dwuv, st_e = _even_post_bwd(dx1, y0, oa, olat, proj_e, gate0, wuv, woe)
    late_grads = _reduce_exchange([dwoe.reshape(N_CHIPS, D_MODEL // N_CHIPS, D_MODEL), dwio,
                                   dwoo.reshape(N_CHIPS, D_MODEL // N_CHIPS, D_MODEL)], whole=[_uv_unlayout(dwuv).astype(BF16)])
    dqa, dka, dva, p_woe, p_wio, p_woo, p_wuv = _pp_bwd(qa, ka, ka_t, va, oa, doa, lse_a, kdiv=KV_SHARE, tq=a_tq, tk=a_tk,
                                                        sub=a_sub, name="attn_a_bwd", side=late_grads)
    dqcat, dkcat = _mla_bwd(qcat, kcat, kcat_t, olat, dolat, lse_b, tq=b_tq, tk=bb_tk, sub=bb_sub)
    dx0, dwie, dwuq, dwuk, st_0, nst = _even_pre_bwd(x0, h0, proj_e, dqa, dka, dva, dga, dgb, dqcat, dkcat, dx1, mod3[0], nw0,
                                                     wie, qn, kn, seg, ca, sa, ct, st, qln, kvln, wuq, wuk)
    dsink_pairs = jnp.stack([dsink_raw[:, 0, 0], dsink_raw[:, 1, 0]], axis=1).reshape(C_HEADS)
    return dict(
        loss_row=st_f[2:3], dx=dx0,
        dmod=jnp.stack([jnp.concatenate([st_0[0], st_0[1], st_e[0]]), jnp.concatenate([st_1[0], st_1[1], st_f[1]])]),
        norm_w=jnp.stack([st_0[2], st_1[2]]), final_norm=st_f[0],
        a_q_norm=nst[0:1, 0:HEAD_DIM], a_k_norm=nst[1:2, 0:HEAD_DIM], b_q_lora_norm=nst[2:3, :], b_kv_lora_norm=nst[3:4, 0:LANES],
        c_sink=dsink_pairs.reshape(1, C_HEADS),
        even_w_in=dwie, b_w_uq=dwuq, b_w_uk=dwuk, b_w_uv=p_wuv, even_w_out=p_woe, odd_w_in=p_wio, odd_w_out=p_woo)


WEIGHT_NAMES = ("norm_w", "ada_w", "ada_b", "even_w_in", "a_q_norm", "a_k_norm", "b_q_lora_norm", "b_kv_lora_norm", "b_w_uq",
                "b_w_uk", "b_w_uv", "even_w_out", "odd_w_in", "c_sink", "odd_w_out", "final_norm")


def kernel(x, c, norm_w, ada_w, ada_b, even_w_in, a_q_norm, a_k_norm, b_q_lora_norm, b_kv_lora_norm, b_w_uq, b_w_uk, b_w_uv, even_w_out, odd_w_in, c_sink, odd_w_out, final_norm, loss_target, m_norm_w, m_ada_w, m_ada_b, m_even_w_in, m_a_q_norm, m_a_k_norm, m_b_q_lora_norm, m_b_kv_lora_norm, m_b_w_uq, m_b_w_uk, m_b_w_uv, m_even_w_out, m_odd_w_in, m_c_sink, m_odd_w_out, m_final_norm, v_norm_w, v_ada_w, v_ada_b, v_even_w_in, v_a_q_norm, v_a_k_norm, v_b_q_lora_norm, v_b_kv_lora_norm, v_b_w_uq, v_b_w_uk, v_b_w_uv, v_even_w_out, v_odd_w_in, v_c_sink, v_odd_w_out, v_final_norm):
    given = dict(locals())
    xi, yi, ci = _my_place()
    chip = 2 * xi + yi
    dev = 2 * chip + ci
    n_ada = ada_w.shape[2]

    (c_all,) = _gather_dev8([c], "gather_c")
    c_all = c_all.reshape(N_DEV, D_MODEL)
    bias = lax.dynamic_slice_in_dim(ada_b, chip * n_ada, n_ada, axis=1).reshape(2, 1, n_ada)
    mod_cols = _ada_fwd(c_all, ada_w, bias)
    def halves(w):
        return w.astype(BF16).reshape((2, w.shape[0] // 2) + w.shape[1:])

    mod_all, wie_g, wuq_g = _gather_chip4_halves(
        [mod_cols, _shard_halves_t(even_w_in[0]), _shard_halves_t(b_w_uq[0])]).run("gather_weights")
    mod = jnp.transpose(lax.dynamic_index_in_dim(mod_all, dev, axis=2, keepdims=False), (1, 0, 2)).reshape(2, 3 * D_MODEL)

    res = _local_step(
        x[0], loss_target[0], mod, norm_w,
        _even_in_layout_t(wie_g), _uq_layout_t(wuq_g), _uk_layout(b_w_uk[0].astype(BF16)),
        _uv_layout(b_w_uv[0].astype(BF16)), [halves(even_w_out[0]), halves(odd_w_in[0]), halves(odd_w_out[0])],
        a_q_norm, a_k_norm, b_q_lora_norm, b_kv_lora_norm, c_sink, final_norm)

    p_wie, p_wuq, small_all, p_wuk = _reduce_exchange(
        [res["even_w_in"].reshape(N_CHIPS, EVEN_IN // N_CHIPS, D_MODEL),
         res["b_w_uq"].reshape(N_CHIPS, -1, B_Q_LORA)],
        whole=[_pack_small(res), res["b_w_uk"]],
    ).run("reduce_exchange")
    shard_parts = dict(even_w_in=p_wie, b_w_uq=p_wuq, **{k: res[k] for k in ("even_w_out", "odd_w_in", "odd_w_out")})
    dmod_all = small_all[:, 0:6, :].reshape(N_DEV, 2, 3 * D_MODEL)
    dmod_cols = jnp.transpose(lax.dynamic_slice_in_dim(dmod_all, chip * n_ada, n_ada, axis=2), (1, 0, 2))
    parts = dict(shard_parts)
    parts["ada_w"] = _ada_bwd(c_all.T, dmod_cols).reshape(1, 2 * D_MODEL, n_ada)
    parts["b_w_uk"], parts["b_w_uv"] = p_wuk, res["b_w_uv"]

    def as2d(a):
        return a.reshape((-1, a.shape[-1]) if a.ndim > 1 else (1, a.shape[0]))

    results = {}
    small_outs = _adam_small(small_all, *[[as2d(given[pre + k]) for k in SMALL_WEIGHTS] for pre in ("", "m_", "v_")])
    for idx, k in enumerate(SMALL_WEIGHTS):
        results[k] = small_outs[4 * idx:4 * idx + 4]
    for k, p in parts.items():
        if k in ("even_w_in", "b_w_uq"):
            outs = _adam(p, given[k][0].T, given["m_" + k][0].T, given["v_" + k][0].T, "adam_" + k, by_columns=k == "even_w_in")
            results[k] = [o.T for o in outs]
            continue
        if k in ("b_w_uk", "b_w_uv"):
            outs = _adam(p, _latent_rows(given[k]), _latent_rows(given["m_" + k]), _latent_rows(given["v_" + k]), "adam_" + k)
            results[k] = [_latent_unrows(o, given[k].shape) for o in outs]
            continue
        shape2 = (p.shape[-2], p.shape[-1])
        results[k] = _adam(p, given[k].reshape(shape2), given["m_" + k].reshape(shape2), given["v_" + k].reshape(shape2),
                           "adam_" + k)
    by_kind = [[results[k][t].reshape(given[k].shape) for k in WEIGHT_NAMES] for t in range(4)]
    return (small_outs[-1][0, 0], res["dx"][None], *by_kind[0], *by_kind[1], *by_kind[2], *by_kind[3])
```
